```python
import jax
import jax.numpy as jnp
from jax import lax
import numpy as np

D_MODEL = 1024
BATCH = 1
SEQ = 16384
DEPTH = 2

CHUNK = 64
Q_BLOCK = 128
N_BRANCH = 4
BRANCH_W = D_MODEL // N_BRANCH
HEAD_DIM = 64
N_HEADS = BRANCH_W // HEAD_DIM
POOL_WINDOWS = (2, 4, 8, 16)
POOL_GROUP = BRANCH_W // len(POOL_WINDOWS)
RET_DECAY_BASE = 5.0
ROPE_BASE = 10000.0
FOX_BIAS_INIT = 2.0
D_FF = 7 * D_MODEL // 2
N_EXPERTS = 8
TOP_K = 2
N_DENSE = (DEPTH + 1) // 2
N_MOE = DEPTH // 2
RMS_EPS = 1e-6
LN_EPS = 1e-5
IN_SPLITS = (BRANCH_W, BRANCH_W, BRANCH_W, BRANCH_W,
             BRANCH_W, BRANCH_W, BRANCH_W, N_HEADS,
             BRANCH_W,
             BRANCH_W, BRANCH_W, BRANCH_W, BRANCH_W,
             N_BRANCH * D_MODEL)
IN_COLS = 12 * BRANCH_W + N_HEADS + N_BRANCH * D_MODEL

kernel_name = "hybrid_gated_mixers_moe_encoder"


def rmsnorm(x, gain):
    xf = x.astype(jnp.float32)
    y = xf * lax.rsqrt(jnp.mean(xf * xf, axis=-1, keepdims=True) + RMS_EPS)
    return (y * gain.astype(jnp.float32)).astype(x.dtype)


def split_heads(t):
    b, s, _ = t.shape
    return t.reshape(b, s, -1, HEAD_DIM).transpose(0, 2, 1, 3)


def merge_heads(t):
    b, h, s, d = t.shape
    return t.transpose(0, 2, 1, 3).reshape(b, s, h * d)


def split_columns(proj):
    parts, start = [], 0
    for width in IN_SPLITS:
        parts.append(proj[..., start:start + width])
        start += width
    return parts


def hgrn_lower_bounds(lb_logits):
    p = jax.nn.softmax(lb_logits.astype(jnp.float32), axis=0)
    return jnp.cumsum(p, axis=0) - p[0]


def hgrn2_mixer(q, f_logit, i, g, lb, norm_g):
    b, s, _ = q.shape
    n = s // CHUNK
    lbf = lb.astype(jnp.float32)
    log_f = jnp.log(lbf + (1.0 - lbf) * jax.nn.sigmoid(f_logit.astype(jnp.float32)))
    k = -jnp.expm1(log_f)

    def chunked(t):
        return t.astype(jnp.float32).reshape(b, n, CHUNK, N_HEADS, HEAD_DIM).transpose(1, 0, 3, 2, 4)

    qc, kc, vc = chunked(q), chunked(k), chunked(i)
    bc = jnp.cumsum(chunked(log_f), axis=3)
    causal = jnp.tril(jnp.ones((CHUNK, CHUNK), dtype=bool))[:, :, None]

    def step(state, inp):
        qq, kk, vv, bb = inp
        rel = bb[:, :, :, None, :] - bb[:, :, None, :, :]
        decay = jnp.exp(jnp.where(causal, rel, -jnp.inf))
        scores = jnp.einsum("bhtk,bhsk,bhtsk->bhts", qq, kk, decay)
        out = (jnp.einsum("bhts,bhsv->bhtv", scores, vv)
               + jnp.einsum("bhtk,bhkv->bhtv", qq * jnp.exp(bb), state))
        b_last = bb[:, :, -1:, :]
        state = (jnp.exp(b_last[:, :, 0, :])[..., None] * state
                 + jnp.einsum("bhsk,bhsv->bhkv", kk * jnp.exp(b_last - bb), vv))
        return state, out

    s0 = jnp.zeros((b, N_HEADS, HEAD_DIM, HEAD_DIM), jnp.float32)
    _, o = lax.scan(step, s0, (qc, kc, vc, bc))
    o = o.transpose(1, 0, 3, 2, 4)
    o = o * lax.rsqrt(jnp.mean(o * o, axis=-1, keepdims=True) + RMS_EPS)
    o = o.reshape(b, s, BRANCH_W) * norm_g.astype(jnp.float32)
    return (o * jax.nn.silu(g.astype(jnp.float32))).astype(q.dtype)


def fox_mixer(q, k, v, f_logit, f_bias):
    b, s, _ = q.shape
    qh = split_heads(q) * HEAD_DIM ** -0.5
    kh = split_heads(k)
    vh = split_heads(v)
    log_f = jax.nn.log_sigmoid(f_logit.astype(jnp.float32) + f_bias.astype(jnp.float32))
    cum = jnp.cumsum(log_f, axis=1).transpose(0, 2, 1)
    key_pos = jnp.arange(s)

    def block(start):
        qb = lax.dynamic_slice_in_dim(qh, start, Q_BLOCK, axis=2)
        cb = lax.dynamic_slice_in_dim(cum, start, Q_BLOCK, axis=2)
        logits = (jnp.einsum("bhtd,bhsd->bhts", qb, kh).astype(jnp.float32)
                  + (cb[..., None] - cum[:, :, None, :]))
        q_pos = start + jnp.arange(Q_BLOCK)
        logits = jnp.where(key_pos[None, :] <= q_pos[:, None], logits, -jnp.inf)
        p = jax.nn.softmax(logits, axis=-1)
        return jnp.einsum("bhts,bhsd->bhtd", p.astype(vh.dtype), vh)

    o = lax.map(block, jnp.arange(0, s, Q_BLOCK))
    o = o.transpose(1, 2, 0, 3, 4).reshape(b, N_HEADS, s, HEAD_DIM)
    return merge_heads(o).astype(q.dtype)


def pool_mixer(u, w_pool, scale):
    b, s, _ = u.shape
    uf = u.astype(jnp.float32)
    cs = jnp.concatenate([jnp.zeros((b, 1, BRANCH_W), jnp.float32), jnp.cumsum(uf, axis=1)], axis=1)
    count_pos = jnp.arange(1, s + 1, dtype=jnp.float32)[None, :, None]
    diffs = []
    for gi, w in enumerate(POOL_WINDOWS):
        lo, hi = gi * POOL_GROUP, (gi + 1) * POOL_GROUP
        c = cs[..., lo:hi]
        lower = jnp.pad(c[:, : s + 1 - w], ((0, 0), (w - 1, 0), (0, 0)))
        mean = (c[:, 1:] - lower) / jnp.minimum(count_pos, float(w))
        diffs.append(mean - uf[..., lo:hi])
    d = jnp.stack(diffs, axis=2)
    y = jnp.einsum("bsgc,gcd->bsgd", d, w_pool.astype(jnp.float32)).reshape(b, s, BRANCH_W)
    return (y * scale.astype(jnp.float32)).astype(u.dtype)


def retention_mixer(q, k, v, g, gn_g, gn_b):
    b, s, _ = q.shape
    n = s // CHUNK
    half = HEAD_DIM // 2
    pos = jnp.arange(s, dtype=jnp.float32)
    inv_freq = ROPE_BASE ** (-jnp.arange(half, dtype=jnp.float32) / half)
    ang = pos[:, None] * inv_freq[None, :]
    cos, sin = jnp.cos(ang), jnp.sin(ang)

    def rope(t):
        t1, t2 = t[..., :half], t[..., half:]
        return jnp.concatenate([t1 * cos - t2 * sin, t1 * sin + t2 * cos], axis=-1)

    def chunked(t):
        return t.reshape(b, N_HEADS, n, CHUNK, HEAD_DIM)

    qc = chunked(rope(split_heads(q.astype(jnp.float32))))
    kc = chunked(rope(split_heads(k.astype(jnp.float32))) * HEAD_DIM ** -0.5)
    vc = chunked(split_heads(v.astype(jnp.float32)))
    log_gamma = jnp.log1p(-jnp.exp2(-RET_DECAY_BASE - jnp.arange(N_HEADS, dtype=jnp.float32)))
    ci = jnp.arange(CHUNK, dtype=jnp.float32)
    diff = ci[:, None] - ci[None, :]
    intra_decay = jnp.where(diff >= 0, jnp.exp(diff * log_gamma[:, None, None]), 0.0)
    scores = jnp.einsum("bhntd,bhnsd->bhnts", qc, kc) * intra_decay[None, :, None]
    intra = jnp.einsum("bhnts,bhnsv->bhntv", scores, vc)
    zeta = jnp.exp((CHUNK - 1 - ci) * log_gamma[:, None])
    xi = jnp.exp((ci + 1) * log_gamma[:, None])
    u_chunk = jnp.einsum("bhnsd,bhnsv->nbhdv", kc * zeta[None, :, None, :, None], vc)
    g_chunk = jnp.exp(CHUNK * log_gamma)[None, :, None, None]

    def step(state, upd):
        return g_chunk * state + upd, state

    s0 = jnp.zeros((b, N_HEADS, HEAD_DIM, HEAD_DIM), jnp.float32)
    _, r_prev = lax.scan(step, s0, u_chunk)
    inter = jnp.einsum("bhntd,nbhdv->bhntv", qc * xi[None, :, None, :, None], r_prev)
    o = (intra + inter).reshape(b, N_HEADS, s, HEAD_DIM)
    mu = jnp.mean(o, axis=-1, keepdims=True)
    var = jnp.mean((o - mu) ** 2, axis=-1, keepdims=True)
    o = (o - mu) * lax.rsqrt(var + LN_EPS)
    o = merge_heads(o) * gn_g.astype(jnp.float32) + gn_b.astype(jnp.float32)
    return (o * jax.nn.silu(g.astype(jnp.float32))).astype(q.dtype)


def swiglu(h, w_gate, w_up, w_down):
    return (jax.nn.silu(h @ w_gate) * (h @ w_up)) @ w_down


def moe_ffn(h, router, w_gate, w_up, w_down):
    logits = (h @ router).astype(jnp.float32)
    top_val, top_idx = lax.top_k(logits, TOP_K)
    top_w = jax.nn.softmax(top_val, axis=-1)
    gate = jnp.einsum("bske,bsk->bse", jax.nn.one_hot(top_idx, N_EXPERTS, dtype=jnp.float32), top_w)
    out = jnp.zeros(h.shape, jnp.float32)
    for e in range(N_EXPERTS):
        out = out + gate[..., e:e + 1] * swiglu(h, w_gate[e], w_up[e], w_down[e]).astype(jnp.float32)
    return out.astype(h.dtype)


def setup_inputs(seed: int = 0) -> dict:
    key = jax.random.key(seed)
    ks = jax.random.split(key, 21)

    def normal(k, shape, scale):
        return scale * jax.random.normal(k, shape, jnp.float32)

    return {
        "x": normal(ks[0], (BATCH, SEQ, D_MODEL), 1.0),
        "w_in": normal(ks[1], (DEPTH, D_MODEL, IN_COLS), D_MODEL ** -0.5),
        "w_branch": normal(ks[2], (DEPTH, N_BRANCH, BRANCH_W, D_MODEL), BRANCH_W ** -0.5),
        "w_out": normal(ks[3], (DEPTH, D_MODEL, D_MODEL), D_MODEL ** -0.5),
        "norm_mix_g": 1.0 + normal(ks[4], (DEPTH, D_MODEL), 0.02),
        "hgrn_lb_logits": normal(ks[5], (DEPTH, BRANCH_W), 0.5),
        "hgrn_norm_g": 1.0 + normal(ks[6], (DEPTH, BRANCH_W), 0.02),
        "fox_f_bias": FOX_BIAS_INIT + normal(ks[7], (DEPTH, N_HEADS), 0.5),
        "pool_w": normal(ks[8], (DEPTH, len(POOL_WINDOWS), POOL_GROUP, POOL_GROUP), POOL_GROUP ** -0.5),
        "pool_scale": 1.0 + normal(ks[9], (DEPTH, BRANCH_W), 0.02),
        "ret_gn_g": 1.0 + normal(ks[10], (DEPTH, BRANCH_W), 0.02),
        "ret_gn_b": normal(ks[11], (DEPTH, BRANCH_W), 0.02),
        "norm_ffn_g": 1.0 + normal(ks[12], (DEPTH, D_MODEL), 0.02),
        "ffn_w_gate": normal(ks[13], (N_DENSE, D_MODEL, D_FF), D_MODEL ** -0.5),
        "ffn_w_up": normal(ks[14], (N_DENSE, D_MODEL, D_FF), D_MODEL ** -0.5),
        "ffn_w_down": normal(ks[15], (N_DENSE, D_FF, D_MODEL), D_FF ** -0.5),
        "moe_router": normal(ks[16], (N_MOE, D_MODEL, N_EXPERTS), D_MODEL ** -0.5),
        "moe_w_gate": normal(ks[17], (N_MOE, N_EXPERTS, D_MODEL, D_FF), D_MODEL ** -0.5),
        "moe_w_up": normal(ks[18], (N_MOE, N_EXPERTS, D_MODEL, D_FF), D_MODEL ** -0.5),
        "moe_w_down": normal(ks[19], (N_MOE, N_EXPERTS, D_FF, D_MODEL), D_FF ** -0.5),
        "final_norm_g": 1.0 + normal(ks[20], (D_MODEL,), 0.02),
    }


def reference(x, w_in, w_branch, w_out, norm_mix_g, hgrn_lb_logits, hgrn_norm_g, fox_f_bias,
              pool_w, pool_scale, ret_gn_g, ret_gn_b, norm_ffn_g, ffn_w_gate, ffn_w_up, ffn_w_down,
              moe_router, moe_w_gate, moe_w_up, moe_w_down, final_norm_g):
    b, s, _ = x.shape
    lower_bounds = hgrn_lower_bounds(hgrn_lb_logits)
    for layer in range(DEPTH):
        h = rmsnorm(x, norm_mix_g[layer])
        (hg_q, hg_f, hg_i, hg_g, fx_q, fx_k, fx_v, fx_f, pl_u,
         rt_q, rt_k, rt_v, rt_g, gate_logits) = split_columns(h @ w_in[layer])
        branches = (
            hgrn2_mixer(hg_q, hg_f, hg_i, hg_g, lower_bounds[layer], hgrn_norm_g[layer]),
            fox_mixer(fx_q, fx_k, fx_v, fx_f, fox_f_bias[layer]),
            pool_mixer(pl_u, pool_w[layer], pool_scale[layer]),
            retention_mixer(rt_q, rt_k, rt_v, rt_g, ret_gn_g[layer], ret_gn_b[layer]),
        )
        merged = jnp.zeros((b, s, D_MODEL), x.dtype)
        for bi, o_b in enumerate(branches):
            gate_b = jax.nn.sigmoid(gate_logits[..., bi * D_MODEL:(bi + 1) * D_MODEL])
            merged = merged + gate_b * (o_b @ w_branch[layer, bi])
        x = x + merged @ w_out[layer]
        h = rmsnorm(x, norm_ffn_g[layer])
        if layer % 2 == 0:
            ffn = swiglu(h, ffn_w_gate[layer // 2], ffn_w_up[layer // 2], ffn_w_down[layer // 2])
        else:
            ffn = moe_ffn(h, moe_router[layer // 2], moe_w_gate[layer // 2],
                          moe_w_up[layer // 2], moe_w_down[layer // 2])
        x = x + ffn
    return rmsnorm(x, final_norm_g)
```

```python
import functools
import math

import jax
import jax.numpy as jnp
from jax import lax
from jax.experimental import pallas as pl
from jax.experimental.pallas import tpu as pltpu

D_MODEL = 1024
N_BRANCH = 4
BRANCH_W = D_MODEL // N_BRANCH
HEAD_DIM = 64
N_HEADS = BRANCH_W // HEAD_DIM
POOL_WINDOWS = (2, 4, 8, 16)
POOL_GROUP = BRANCH_W // len(POOL_WINDOWS)
POOL_HALO = 16
RET_DECAY_BASE = 5.0
ROPE_BASE = 10000.0
D_FF = 7 * D_MODEL // 2
N_EXPERTS = 8
RMS_EPS = 1e-6
LN_EPS = 1e-5
N_MIX_COLS = 12 * BRANCH_W

LANE = 128
SUBLANE = 8
VMEM_LIMIT = 56 * 1024 * 1024

HG_CHUNK = 64
HG_SUB = 16
RET_CHUNK = 64

F32 = jnp.float32
BF16 = jnp.bfloat16
NT_DIMS = (((1,), (1,)), ((), ()))


def _params(*sem):
    return pltpu.CompilerParams(dimension_semantics=sem, vmem_limit_bytes=VMEM_LIMIT)


def _const_spec(shape):
    nd = len(shape)
    return pl.BlockSpec(shape, lambda *_: (0,) * nd, pipeline_mode=pl.Buffered(1))


def _split3(x):
    hi = x.astype(BF16)
    r1 = x - hi.astype(F32)
    mid = r1.astype(BF16)
    lo = (r1 - mid.astype(F32)).astype(BF16)
    return hi, mid, lo


def _dot(a, b):
    return jnp.dot(a, b, preferred_element_type=F32)


def _dot_nt(a, b):
    return lax.dot_general(a, b, NT_DIMS, preferred_element_type=F32)


def _dot_exact_rhs(x, m_bf16):
    hi, mid, lo = _split3(x)
    return _dot(hi, m_bf16) + _dot(mid, m_bf16) + _dot(lo, m_bf16)


def _dot_exact_lhs(m_bf16, x):
    hi, mid, lo = _split3(x)
    return _dot(m_bf16, hi) + _dot(m_bf16, mid) + _dot(m_bf16, lo)


def _sigmoid(x):
    return 1.0 / (1.0 + jnp.exp(-x))


def _silu(x):
    return x * _sigmoid(x)


def _rms(x, gain):
    return x * lax.rsqrt(jnp.mean(x * x, axis=-1, keepdims=True) + RMS_EPS) * gain


def _head_of(n):
    return jnp.arange(n) // HEAD_DIM


def _head_ones():
    h = _head_of(BRANCH_W)
    return (h[:, None] == h[None, :]).astype(BF16)


def _head_masks():
    return (_head_of(BRANCH_W)[None, :] == jnp.arange(N_HEADS)[:, None]).astype(F32)


def _rmsnorm_kernel(x_ref, g_ref, o_ref):
    o_ref[...] = _rms(x_ref[...], g_ref[...]).astype(o_ref.dtype)


def rmsnorm_bf16(x, gain, tm=1024):
    s, d = x.shape
    tm = min(tm, s)
    return pl.pallas_call(
        _rmsnorm_kernel,
        grid=(s // tm,),
        in_specs=[pl.BlockSpec((tm, d), lambda i: (i, 0)), _const_spec((1, d))],
        out_specs=pl.BlockSpec((tm, d), lambda i: (i, 0)),
        out_shape=jax.ShapeDtypeStruct((s, d), BF16),
        compiler_params=_params("parallel"),
        name="rmsnorm",
    )(x, gain.reshape(1, d))


def _matmul_kernel(a_ref, b_ref, o_ref):
    o_ref[...] = _dot(a_ref[...], b_ref[...]).astype(o_ref.dtype)


def matmul(a, b, out_dtype=BF16, tm=1024, tn=512):
    m, k = a.shape
    _, n = b.shape
    tm = min(tm, m)
    return pl.pallas_call(
        _matmul_kernel,
        grid=(n // tn, m // tm),
        in_specs=[pl.BlockSpec((tm, k), lambda j, i: (i, 0)),
                  pl.BlockSpec((k, tn), lambda j, i: (0, j))],
        out_specs=pl.BlockSpec((tm, tn), lambda j, i: (i, j)),
        out_shape=jax.ShapeDtypeStruct((m, n), out_dtype),
        compiler_params=_params("parallel", "parallel"),
        name="in_proj",
    )(a, b)


def _hgrn_kernel(q_ref, f_ref, i_ref, g_ref, lbl_ref, ng_ref, ones_ref, hm_ref, tril_ref,
                 o_ref, st_ref, bpad, kpad, vpad, astack, *, layer, tile):
    c, sub = HG_CHUNK, HG_SUB
    nsub = c // sub

    @pl.when(pl.program_id(0) == 0)
    def _():
        st_ref[...] = jnp.zeros_like(st_ref)
        bpad[...] = jnp.zeros_like(bpad)
        kpad[...] = jnp.zeros_like(kpad)
        vpad[...] = jnp.zeros_like(vpad)

    lbl = lbl_ref[...]
    e = jnp.exp(lbl - jnp.max(lbl, axis=0, keepdims=True))
    p = e / jnp.sum(e, axis=0, keepdims=True)
    lb = jnp.zeros((1, BRANCH_W), F32)
    for l in range(1, layer + 1):
        lb = lb + p[l:l + 1, :]

    ones_bd = ones_ref[...]
    hm = hm_ref[...]
    tril = tril_ref[...]
    row = lax.broadcasted_iota(jnp.int32, (c, 1), 0)
    row_in_sub = row % sub
    bd_mask = ones_bd.astype(F32)

    def chunk(ci, carry):
        r0 = pl.multiple_of(ci * c, c)
        q = q_ref[pl.ds(r0, c), :].astype(F32)
        fl = f_ref[pl.ds(r0, c), :].astype(F32)
        v = i_ref[pl.ds(r0, c), :].astype(F32)
        g = g_ref[pl.ds(r0, c), :].astype(F32)

        sig = _sigmoid(fl)
        logf = jnp.log(lb + (1.0 - lb) * sig)
        kk = (1.0 - lb) * (1.0 - sig)
        b = _dot_exact_lhs(tril, logf)

        bpad[pl.ds(sub, c), :] = b
        kpad[pl.ds(sub, c), :] = kk
        vpad[pl.ds(sub, c), :] = v

        for d in range(sub):
            b_d = bpad[pl.ds(sub - d, c), :]
            k_d = kpad[pl.ds(sub - d, c), :]
            a = jnp.where(row_in_sub >= d, q * k_d * jnp.exp(b - b_d), 0.0)
            astack[pl.ds(d * c, c), :] = a.astype(BF16)
        pall = _dot(astack[...], ones_bd)
        intra = jnp.zeros((c, BRANCH_W), F32)
        for d in range(sub):
            intra = intra + pall[d * c:(d + 1) * c, :] * vpad[pl.ds(sub - d, c), :]

        pieces = [jnp.zeros((sub, BRANCH_W), F32)]
        for si in range(1, nsub):
            lo = si * sub
            m_i = b[lo - 1:lo, :]
            qs = q[lo:lo + sub, :] * jnp.exp(b[lo:lo + sub, :] - m_i)
            ks = (kk[:lo, :] * jnp.exp(m_i - b[:lo, :])).astype(BF16)
            qx = jnp.concatenate([qs * hm[h:h + 1, :] for h in range(N_HEADS)], axis=0).astype(BF16)
            sc = _dot_nt(qx, ks)
            r = _dot(sc.astype(BF16), v[:lo, :].astype(BF16))
            acc = jnp.zeros((sub, BRANCH_W), F32)
            for h in range(N_HEADS):
                acc = acc + r[h * sub:(h + 1) * sub, :] * hm[h:h + 1, :]
            pieces.append(acc)
        intra = intra + jnp.concatenate(pieces, axis=0)

        st = st_ref[...]
        inter = _dot_nt((q * jnp.exp(b)).astype(BF16), st.astype(BF16))
        b_last = b[c - 1:c, :]
        ks_end = (kk * jnp.exp(b_last - b)).astype(BF16)
        upd = _dot(v.T.astype(BF16), ks_end)
        st_ref[...] = st * jnp.exp(b_last) + upd * bd_mask

        o = intra + inter
        ms = _dot_exact_rhs(o * o, ones_bd) * (1.0 / HEAD_DIM)
        y = o * lax.rsqrt(ms + RMS_EPS) * ng_ref[...] * _silu(g)
        o_ref[pl.ds(r0, c), :] = y.astype(o_ref.dtype)
        return carry

    lax.fori_loop(0, tile // c, chunk, 0)


def hgrn2(proj, lb_logits, norm_g, layer, tile=512):
    s = proj.shape[0]
    depth = lb_logits.shape[0]
    c, sub = HG_CHUNK, HG_SUB
    tile = min(tile, s)
    col = lambda j: pl.BlockSpec((tile, BRANCH_W), lambda i, j=j: (i, j))
    tril = jnp.tril(jnp.ones((c, c), F32)).astype(BF16)
    return pl.pallas_call(
        functools.partial(_hgrn_kernel, layer=layer, tile=tile),
        grid=(s // tile,),
        in_specs=[col(0), col(1), col(2), col(3),
                  _const_spec((depth, BRANCH_W)), _const_spec((1, BRANCH_W)),
                  _const_spec((BRANCH_W, BRANCH_W)), _const_spec((N_HEADS, BRANCH_W)),
                  _const_spec((c, c))],
        out_specs=pl.BlockSpec((tile, BRANCH_W), lambda i: (i, 0)),
        out_shape=jax.ShapeDtypeStruct((s, BRANCH_W), BF16),
        scratch_shapes=[pltpu.VMEM((BRANCH_W, BRANCH_W), F32),
                        pltpu.VMEM((c + sub, BRANCH_W), F32),
                        pltpu.VMEM((c + sub, BRANCH_W), F32),
                        pltpu.VMEM((c + sub, BRANCH_W), F32),
                        pltpu.VMEM((sub * c, BRANCH_W), BF16)],
        compiler_params=_params("arbitrary"),
        name="hgrn2",
    )(proj, proj, proj, proj, lb_logits.astype(F32), norm_g.reshape(1, BRANCH_W).astype(F32),
      _head_ones(), _head_masks(), tril)


def _fox_gate_kernel(h_ref, wf_ref, bias_ref, triu_ref, o_ref, carry_ref):
    @pl.when(pl.program_id(0) == 0)
    def _():
        carry_ref[...] = jnp.zeros_like(carry_ref)

    logit = _dot_nt(wf_ref[...], h_ref[...]) + bias_ref[...]
    logf = jnp.minimum(logit, 0.0) - jnp.log(1.0 + jnp.exp(-jnp.abs(logit)))
    cum = _dot_exact_rhs(logf, triu_ref[...]) + carry_ref[...]
    o_ref[...] = cum
    carry_ref[...] = cum[:, -1:]


def fox_gate(h, w_f, f_bias, tile=512):
    s, d = h.shape
    tile = min(tile, s)
    wft = jnp.zeros((SUBLANE, d), F32).at[:N_HEADS].set(w_f.T).astype(BF16)
    bias = jnp.zeros((SUBLANE, 1), F32).at[:N_HEADS, 0].set(f_bias.astype(F32))
    triu = jnp.triu(jnp.ones((tile, tile), F32)).astype(BF16)
    return pl.pallas_call(
        _fox_gate_kernel,
        grid=(s // tile,),
        in_specs=[pl.BlockSpec((tile, d), lambda i: (i, 0)),
                  _const_spec((SUBLANE, d)), _const_spec((SUBLANE, 1)), _const_spec((tile, tile))],
        out_specs=pl.BlockSpec((SUBLANE, tile), lambda i: (0, i)),
        out_shape=jax.ShapeDtypeStruct((SUBLANE, s), F32),
        scratch_shapes=[pltpu.VMEM((SUBLANE, 1), F32)],
        compiler_params=_params("arbitrary"),
        name="fox_gate",
    )(h, wft, bias, triu)


def _fox_kernel(q_ref, k_ref, v_ref, ct_ref, hm_ref, o_ref, m_sc, l_sc, acc_sc, *, tq):
    i = pl.program_id(0)
    q0 = pl.multiple_of(i * tq, tq)
    hm = hm_ref[...]
    q = q_ref[...].astype(F32) * (HEAD_DIM ** -0.5)
    qh = [(q * hm[h:h + 1, :]).astype(BF16) for h in range(N_HEADS)]
    c_q0 = ct_ref[:, pl.ds(q0, tq)][:, 0:1]

    m_sc[...] = jnp.full_like(m_sc, -jnp.inf)
    l_sc[...] = jnp.zeros_like(l_sc)
    acc_sc[...] = jnp.zeros_like(acc_sc)

    def block(s0, diagonal):
        kb = k_ref[pl.ds(s0, tq), :]
        vb = v_ref[pl.ds(s0, tq), :]
        bias = c_q0 - ct_ref[:, pl.ds(s0, tq)]
        for h in range(N_HEADS):
            sc = _dot_nt(qh[h], kb) + bias[h:h + 1, :]
            if diagonal:
                r = lax.broadcasted_iota(jnp.int32, (tq, tq), 0)
                cidx = lax.broadcasted_iota(jnp.int32, (tq, tq), 1)
                sc = jnp.where(cidx <= r, sc, -jnp.inf)
            m_prev = m_sc[h]
            m_new = jnp.maximum(m_prev, jnp.max(sc, axis=1, keepdims=True))
            alpha = jnp.exp(m_prev - m_new)
            p = jnp.exp(sc - jnp.tile(m_new, (1, tq // LANE)))
            l_sc[h] = alpha * l_sc[h] + jnp.sum(p, axis=1, keepdims=True)
            acc_sc[h] = acc_sc[h] * jnp.tile(alpha, (1, BRANCH_W // LANE)) + _dot(p.astype(BF16), vb)
            m_sc[h] = m_new

    def body(j, carry):
        block(pl.multiple_of(j * tq, tq), False)
        return carry

    lax.fori_loop(0, i, body, 0)
    block(q0, True)

    out = jnp.zeros((tq, BRANCH_W), F32)
    for h in range(N_HEADS):
        out = out + acc_sc[h] * hm[h:h + 1, :] / jnp.tile(l_sc[h], (1, BRANCH_W // LANE))
    o_ref[...] = out.astype(o_ref.dtype)


def fox_attention(proj, ct, tq=256):
    s = proj.shape[0]
    tq = min(tq, s)
    full = lambda j: pl.BlockSpec((s, BRANCH_W), lambda i, j=j: (0, j), pipeline_mode=pl.Buffered(1))
    return pl.pallas_call(
        functools.partial(_fox_kernel, tq=tq),
        grid=(s // tq,),
        in_specs=[pl.BlockSpec((tq, BRANCH_W), lambda i: (i, 4)), full(5), full(6),
                  _const_spec((SUBLANE, s)), _const_spec((N_HEADS, BRANCH_W))],
        out_specs=pl.BlockSpec((tq, BRANCH_W), lambda i: (i, 0)),
        out_shape=jax.ShapeDtypeStruct((s, BRANCH_W), BF16),
        scratch_shapes=[pltpu.VMEM((N_HEADS, tq, LANE), F32),
                        pltpu.VMEM((N_HEADS, tq, LANE), F32),
                        pltpu.VMEM((N_HEADS, tq, BRANCH_W), F32)],
        compiler_params=_params("parallel"),
        name="fox_attention",
    )(proj, proj, proj, ct, _head_masks())


def _pool_kernel(u_ref, w_ref, scale_ref, o_ref, ext, *, tile):
    i = pl.program_id(0)

    @pl.when(i == 0)
    def _():
        ext[pl.ds(0, POOL_HALO), :] = jnp.zeros((POOL_HALO, BRANCH_W), F32)

    u = u_ref[...].astype(F32)
    ext[pl.ds(POOL_HALO, tile), :] = u
    pos = (i * tile + lax.broadcasted_iota(jnp.int32, (tile, 1), 0) + 1).astype(F32)
    halves = []
    for half in range(BRANCH_W // LANE):
        lanes = pl.ds(half * LANE, LANE)
        w_small, w_big = POOL_WINDOWS[2 * half], POOL_WINDOWS[2 * half + 1]
        run = u[:, half * LANE:(half + 1) * LANE]
        sums = {}
        for j in range(1, w_big):
            if j == w_small:
                sums[w_small] = run
            run = run + ext[pl.ds(POOL_HALO - j, tile), lanes]
        sums[w_big] = run
        lane = lax.broadcasted_iota(jnp.int32, (1, LANE), 1)
        small = lane < POOL_GROUP
        total = jnp.where(small, sums[w_small], sums[w_big])
        count = jnp.where(small, jnp.minimum(pos, float(w_small)), jnp.minimum(pos, float(w_big)))
        halves.append(total / count)
    mean = jnp.concatenate(halves, axis=1)
    d = (mean - u).astype(BF16)
    y = _dot(d, w_ref[...]) * scale_ref[...]
    o_ref[...] = y.astype(o_ref.dtype)
    ext[pl.ds(0, POOL_HALO), :] = u[tile - POOL_HALO:, :]


def pool_mixer(proj, w_pool, scale, tile=512):
    s = proj.shape[0]
    tile = min(tile, s)
    ng = len(POOL_WINDOWS)
    w_bd = jnp.zeros((BRANCH_W, BRANCH_W), F32)
    for gi in range(ng):
        lo = gi * POOL_GROUP
        w_bd = w_bd.at[lo:lo + POOL_GROUP, lo:lo + POOL_GROUP].set(w_pool[gi].astype(F32))
    return pl.pallas_call(
        functools.partial(_pool_kernel, tile=tile),
        grid=(s // tile,),
        in_specs=[pl.BlockSpec((tile, BRANCH_W), lambda i: (i, 7)),
                  _const_spec((BRANCH_W, BRANCH_W)), _const_spec((1, BRANCH_W))],
        out_specs=pl.BlockSpec((tile, BRANCH_W), lambda i: (i, 0)),
        out_shape=jax.ShapeDtypeStruct((s, BRANCH_W), BF16),
        scratch_shapes=[pltpu.VMEM((tile + POOL_HALO, BRANCH_W), F32)],
        compiler_params=_params("arbitrary"),
        name="pool_mixer",
    )(proj, w_bd.astype(BF16), scale.reshape(1, BRANCH_W).astype(F32))


def _ret_kernel(q_ref, k_ref, v_ref, g_ref, cos_ref, sin_ref, perm_ref, ones_ref, hm_ref,
                dstack_ref, xi_ref, zeta_ref, gc_ref, gng_ref, gnb_ref, o_ref, st_ref, *, tile):
    c = RET_CHUNK

    @pl.when(pl.program_id(0) == 0)
    def _():
        st_ref[...] = jnp.zeros_like(st_ref)

    perm = perm_ref[...]
    ones_bd = ones_ref[...]
    bd_mask = ones_bd.astype(F32)
    hm = hm_ref[...]

    def chunk(ci, carry):
        r0 = pl.multiple_of(ci * c, c)
        qb = q_ref[pl.ds(r0, c), :]
        kb = k_ref[pl.ds(r0, c), :]
        v = v_ref[pl.ds(r0, c), :]
        g = g_ref[pl.ds(r0, c), :].astype(F32)
        cos = cos_ref[pl.ds(r0, c), :]
        sin = sin_ref[pl.ds(r0, c), :]
        qr = qb.astype(F32) * cos + _dot(qb, perm) * sin
        kr = (kb.astype(F32) * cos + _dot(kb, perm) * sin) * (HEAD_DIM ** -0.5)

        qx = jnp.concatenate([qr * hm[h:h + 1, :] for h in range(N_HEADS)], axis=0).astype(BF16)
        sc = _dot_nt(qx, kr.astype(BF16)) * dstack_ref[...]
        r = _dot(sc.astype(BF16), v)
        intra = jnp.zeros((c, BRANCH_W), F32)
        for h in range(N_HEADS):
            intra = intra + r[h * c:(h + 1) * c, :] * hm[h:h + 1, :]

        st = st_ref[...]
        inter = _dot_nt((qr * xi_ref[...]).astype(BF16), st.astype(BF16))
        upd = _dot(v.astype(F32).T.astype(BF16), (kr * zeta_ref[...]).astype(BF16))
        st_ref[...] = st * gc_ref[...] + upd * bd_mask

        o = intra + inter
        mu = _dot_exact_rhs(o, ones_bd) * (1.0 / HEAD_DIM)
        cen = o - mu
        var = _dot_exact_rhs(cen * cen, ones_bd) * (1.0 / HEAD_DIM)
        y = cen * lax.rsqrt(var + LN_EPS) * gng_ref[...] + gnb_ref[...]
        o_ref[pl.ds(r0, c), :] = (y * _silu(g)).astype(o_ref.dtype)
        return carry

    lax.fori_loop(0, tile // c, chunk, 0)


def _rope_tables(s):
    half = HEAD_DIM // 2
    pos = jnp.arange(s, dtype=F32)
    inv_freq = ROPE_BASE ** (-jnp.arange(half, dtype=F32) / half)
    ang = pos[:, None] * inv_freq[None, :]
    cos, sin = jnp.cos(ang), jnp.sin(ang)
    cos_t = jnp.tile(jnp.concatenate([cos, cos], axis=1), (1, N_HEADS))
    sin_t = jnp.tile(jnp.concatenate([-sin, sin], axis=1), (1, N_HEADS))
    return cos_t, sin_t


def _ret_constants():
    c = RET_CHUNK
    half = HEAD_DIM // 2
    lane = jnp.arange(BRANCH_W)
    partner = jnp.where(lane % HEAD_DIM < half, lane + half, lane - half)
    perm = (lane[:, None] == partner[None, :]).astype(BF16)
    log_gamma = jnp.log1p(-jnp.exp2(-RET_DECAY_BASE - jnp.arange(N_HEADS, dtype=F32)))
    ci = jnp.arange(c, dtype=F32)
    diff = ci[:, None] - ci[None, :]
    intra = jnp.where(diff >= 0, jnp.exp(diff * log_gamma[:, None, None]), 0.0)
    dstack = intra.reshape(N_HEADS * c, c)
    lg_lane = jnp.repeat(log_gamma, HEAD_DIM)[None, :]
    xi = jnp.exp((ci[:, None] + 1.0) * lg_lane)
    zeta = jnp.exp((c - 1.0 - ci[:, None]) * lg_lane)
    gc = jnp.exp(c * lg_lane)
    return perm, dstack, xi, zeta, gc


def retention(proj, gn_g, gn_b, tile=512):
    s = proj.shape[0]
    c = RET_CHUNK
    tile = min(tile, s)
    cos_t, sin_t = _rope_tables(s)
    perm, dstack, xi, zeta, gc = _ret_constants()
    col = lambda j: pl.BlockSpec((tile, BRANCH_W), lambda i, j=j: (i, j))
    row = pl.BlockSpec((tile, BRANCH_W), lambda i: (i, 0))
    return pl.pallas_call(
        functools.partial(_ret_kernel, tile=tile),
        grid=(s // tile,),
        in_specs=[col(8), col(9), col(10), col(11), row, row,
                  _const_spec((BRANCH_W, BRANCH_W)), _const_spec((BRANCH_W, BRANCH_W)),
                  _const_spec((N_HEADS, BRANCH_W)), _const_spec((N_HEADS * c, c)),
                  _const_spec((c, BRANCH_W)), _const_spec((c, BRANCH_W)), _const_spec((1, BRANCH_W)),
                  _const_spec((1, BRANCH_W)), _const_spec((1, BRANCH_W))],
        out_specs=row,
        out_shape=jax.ShapeDtypeStruct((s, BRANCH_W), BF16),
        scratch_shapes=[pltpu.VMEM((BRANCH_W, BRANCH_W), F32)],
        compiler_params=_params("arbitrary"),
        name="retention",
    )(proj, proj, proj, proj, cos_t, sin_t, perm, _head_ones(), _head_masks(), dstack, xi, zeta, gc,
      gn_g.reshape(1, BRANCH_W).astype(F32), gn_b.reshape(1, BRANCH_W).astype(F32))


def _merge_kernel(h_ref, o0_ref, o1_ref, o2_ref, o3_ref, x_ref, wg_ref, wb_ref, wo_ref, g_ref,
                  xo_ref, ho_ref):
    h = h_ref[...]
    merged = jnp.zeros(x_ref.shape, F32)
    for bi, o_ref in enumerate((o0_ref, o1_ref, o2_ref, o3_ref)):
        gate = _sigmoid(_dot(h, wg_ref[:, bi * D_MODEL:(bi + 1) * D_MODEL]))
        merged = merged + gate * _dot(o_ref[...], wb_ref[bi])
    x_new = x_ref[...] + _dot(merged.astype(BF16), wo_ref[...])
    xo_ref[...] = x_new
    ho_ref[...] = _rms(x_new, g_ref[...]).astype(ho_ref.dtype)


def merge(h, branches, x, w_gate, w_branch, w_out, next_gain, tm=512):
    s, d = x.shape
    tm = min(tm, s)
    row = lambda w: pl.BlockSpec((tm, w), lambda i: (i, 0))
    return pl.pallas_call(
        _merge_kernel,
        grid=(s // tm,),
        in_specs=[row(d), row(BRANCH_W), row(BRANCH_W), row(BRANCH_W), row(BRANCH_W), row(d),
                  _const_spec((d, N_BRANCH * d)), _const_spec((N_BRANCH, BRANCH_W, d)),
                  _const_spec((d, d)), _const_spec((1, d))],
        out_specs=[row(d), row(d)],
        out_shape=[jax.ShapeDtypeStruct((s, d), F32), jax.ShapeDtypeStruct((s, d), BF16)],
        compiler_params=_params("parallel"),
        name="merge",
    )(h, *branches, x, w_gate, w_branch, w_out, next_gain.reshape(1, d).astype(F32))


def _ffn_kernel(h_ref, x_ref, wg_ref, wu_ref, wd_ref, g_ref, xo_ref, ho_ref, acc_ref):
    f = pl.program_id(1)

    @pl.when(f == 0)
    def _():
        acc_ref[...] = jnp.zeros_like(acc_ref)

    h = h_ref[...]
    a = _silu(_dot(h, wg_ref[...])) * _dot(h, wu_ref[...])
    acc_ref[...] += _dot(a.astype(BF16), wd_ref[...])

    @pl.when(f == pl.num_programs(1) - 1)
    def _():
        x_new = x_ref[...] + acc_ref[...]
        xo_ref[...] = x_new
        ho_ref[...] = _rms(x_new, g_ref[...]).astype(ho_ref.dtype)


def ffn_dense(h, x, w_gate, w_up, w_down, next_gain, tm=1024, tf=512):
    s, d = x.shape
    tm = min(tm, s)
    dff = w_gate.shape[1]
    row = lambda: pl.BlockSpec((tm, d), lambda i, f: (i, 0))
    return pl.pallas_call(
        _ffn_kernel,
        grid=(s // tm, dff // tf),
        in_specs=[row(), row(),
                  pl.BlockSpec((d, tf), lambda i, f: (0, f)),
                  pl.BlockSpec((d, tf), lambda i, f: (0, f)),
                  pl.BlockSpec((tf, d), lambda i, f: (f, 0)),
                  _const_spec((1, d))],
        out_specs=[row(), row()],
        out_shape=[jax.ShapeDtypeStruct((s, d), F32), jax.ShapeDtypeStruct((s, d), BF16)],
        scratch_shapes=[pltpu.VMEM((tm, d), F32)],
        compiler_params=_params("parallel", "arbitrary"),
        name="ffn_dense",
    )(h, x, w_gate, w_up, w_down, next_gain.reshape(1, d).astype(F32))


def _moe_kernel(h_ref, x_ref, ng_ref, router_ref, wg_ref, wu_ref, wd_ref, fg_ref, o_ref,
                acc_ref, gate_ref):
    e = pl.program_id(1)
    f = pl.program_id(2)
    tm = x_ref.shape[0]

    @pl.when((e == 0) & (f == 0))
    def _():
        acc_ref[...] = jnp.zeros_like(acc_ref)
        hn = _rms(x_ref[...], ng_ref[...])
        logits = _dot_exact_lhs_f32(hn, router_ref[...])
        lane = lax.broadcasted_iota(jnp.int32, (tm, LANE), 1)
        logits = jnp.where(lane < N_EXPERTS, logits, -jnp.inf)
        v1 = jnp.max(logits, axis=1, keepdims=True)
        i1 = jnp.min(jnp.where(logits == v1, lane, LANE), axis=1, keepdims=True)
        rest = jnp.where(lane == i1, -jnp.inf, logits)
        v2 = jnp.max(rest, axis=1, keepdims=True)
        i2 = jnp.min(jnp.where(rest == v2, lane, LANE), axis=1, keepdims=True)
        w1 = 1.0 / (1.0 + jnp.exp(v2 - v1))
        gate_ref[...] = jnp.where(lane == i1, w1, 0.0) + jnp.where(lane == i2, 1.0 - w1, 0.0)

    lane = lax.broadcasted_iota(jnp.int32, (tm, LANE), 1)
    ge = jnp.sum(jnp.where(lane == e, gate_ref[...], 0.0), axis=1, keepdims=True)
    h = h_ref[...]
    a = _silu(_dot(h, wg_ref[0])) * _dot(h, wu_ref[0]) * ge
    acc_ref[...] += _dot(a.astype(BF16), wd_ref[0])

    @pl.when((e == pl.num_programs(1) - 1) & (f == pl.num_programs(2) - 1))
    def _():
        o_ref[...] = _rms(x_ref[...] + acc_ref[...], fg_ref[...]).astype(o_ref.dtype)


def _dot_exact_lhs_f32(x, w):
    xh, xm, xl = _split3(x)
    wh, wm, wl = _split3(w)
    return (_dot(xh, wh) + (_dot(xh, wm) + _dot(xm, wh))
            + (_dot(xh, wl) + _dot(xm, wm) + _dot(xl, wh)))


def moe_dense(h, x, norm_gain, router, w_gate, w_up, w_down, final_gain, tm=1024, tf=512):
    s, d = x.shape
    tm = min(tm, s)
    ne, _, dff = w_gate.shape
    router_p = jnp.zeros((d, LANE), F32).at[:, :ne].set(router.astype(F32))
    row = lambda: pl.BlockSpec((tm, d), lambda i, e, f: (i, 0))
    return pl.pallas_call(
        _moe_kernel,
        grid=(s // tm, ne, dff // tf),
        in_specs=[row(), row(), _const_spec((1, d)), _const_spec((d, LANE)),
                  pl.BlockSpec((1, d, tf), lambda i, e, f: (e, 0, f)),
                  pl.BlockSpec((1, d, tf), lambda i, e, f: (e, 0, f)),
                  pl.BlockSpec((1, tf, d), lambda i, e, f: (e, f, 0)),
                  _const_spec((1, d))],
        out_specs=row(),
        out_shape=jax.ShapeDtypeStruct((s, d), F32),
        scratch_shapes=[pltpu.VMEM((tm, d), F32), pltpu.VMEM((tm, LANE), F32)],
        compiler_params=_params("parallel", "arbitrary", "arbitrary"),
        name="moe",
    )(h, x, norm_gain.reshape(1, d).astype(F32), router_p, w_gate, w_up, w_down,
      final_gain.reshape(1, d).astype(F32))


def _mixer_weights(w_in_l):
    fox_f0 = 7 * BRANCH_W
    w_mix = jnp.concatenate([w_in_l[:, :fox_f0], w_in_l[:, fox_f0 + N_HEADS:fox_f0 + N_HEADS + 5 * BRANCH_W]],
                            axis=1)
    w_f = w_in_l[:, fox_f0:fox_f0 + N_HEADS]
    w_gate = w_in_l[:, N_MIX_COLS + N_HEADS:]
    return w_mix.astype(BF16), w_f, w_gate.astype(BF16)


def kernel(x, w_in, w_branch, w_out, norm_mix_g, hgrn_lb_logits, hgrn_norm_g, fox_f_bias, pool_w, pool_scale,
           ret_gn_g, ret_gn_b, norm_ffn_g, ffn_w_gate, ffn_w_up, ffn_w_down, moe_router, moe_w_gate, moe_w_up,
           moe_w_down, final_norm_g):
    b, s, d = x.shape
    assert b == 1 and d == D_MODEL
    depth = w_in.shape[0]
    assert depth == 2, "layer 0 uses the dense FFN, layer 1 the experts and the final norm"
    xs = x.reshape(s, d)
    h = rmsnorm_bf16(xs, norm_mix_g[0])
    out = None
    for layer in range(depth):
        w_mix, w_f, w_gate = _mixer_weights(w_in[layer])
        proj = matmul(h, w_mix)
        ct = fox_gate(h, w_f, fox_f_bias[layer])
        branches = (
            hgrn2(proj, hgrn_lb_logits, hgrn_norm_g[layer], layer),
            fox_attention(proj, ct),
            pool_mixer(proj, pool_w[layer], pool_scale[layer]),
            retention(proj, ret_gn_g[layer], ret_gn_b[layer]),
        )
        xs, h2 = merge(h, branches, xs, w_gate, w_branch[layer].astype(BF16), w_out[layer].astype(BF16),
                       norm_ffn_g[layer])
        if layer % 2 == 0:
            li = layer // 2
            xs, h = ffn_dense(h2, xs, ffn_w_gate[li].astype(BF16), ffn_w_up[li].astype(BF16),
                              ffn_w_down[li].astype(BF16), norm_mix_g[layer + 1])
        else:
            li = layer // 2
            out = moe_dense(h2, xs, norm_ffn_g[layer], moe_router[li], moe_w_gate[li].astype(BF16),
                            moe_w_up[li].astype(BF16), moe_w_down[li].astype(BF16), final_norm_g)
    return out.reshape(b, s, d)
```

```python
import functools
import math

import jax
import jax.numpy as jnp
from jax import lax
from jax.experimental import pallas as pl
from jax.experimental.pallas import tpu as pltpu

D_MODEL = 1024
N_BRANCH = 4
BRANCH_W = D_MODEL // N_BRANCH
HEAD_DIM = 64
N_HEADS = BRANCH_W // HEAD_DIM
POOL_WINDOWS = (2, 4, 8, 16)
POOL_GROUP = BRANCH_W // len(POOL_WINDOWS)
POOL_HALO = 16
RET_DECAY_BASE = 5.0
ROPE_BASE = 10000.0
D_FF = 7 * D_MODEL // 2
N_EXPERTS = 8
RMS_EPS = 1e-6
LN_EPS = 1e-5
N_MIX_COLS = 12 * BRANCH_W

LANE = 128
SUBLANE = 8
VMEM_LIMIT = 56 * 1024 * 1024

HG_CHUNK = 64
HG_SUB = 16
RET_CHUNK = 64

F32 = jnp.float32
BF16 = jnp.bfloat16
NT_DIMS = (((1,), (1,)), ((), ()))


def _params(*sem):
    return pltpu.CompilerParams(dimension_semantics=sem, vmem_limit_bytes=VMEM_LIMIT)


def _const_spec(shape):
    nd = len(shape)
    return pl.BlockSpec(shape, lambda *_: (0,) * nd, pipeline_mode=pl.Buffered(1))


def _split3(x):
    hi = x.astype(BF16)
    r1 = x - hi.astype(F32)
    mid = r1.astype(BF16)
    lo = (r1 - mid.astype(F32)).astype(BF16)
    return hi, mid, lo


def _dot(a, b):
    return jnp.dot(a, b, preferred_element_type=F32)


def _dot_nt(a, b):
    return lax.dot_general(a, b, NT_DIMS, preferred_element_type=F32)


def _dot_exact_rhs(x, m_bf16):
    hi, mid, lo = _split3(x)
    return _dot(hi, m_bf16) + _dot(mid, m_bf16) + _dot(lo, m_bf16)


def _dot_exact_lhs(m_bf16, x):
    hi, mid, lo = _split3(x)
    return _dot(m_bf16, hi) + _dot(m_bf16, mid) + _dot(m_bf16, lo)


def _sigmoid(x):
    return 1.0 / (1.0 + jnp.exp(-x))


def _silu(x):
    return x * _sigmoid(x)


def _rms(x, gain):
    return x * lax.rsqrt(jnp.mean(x * x, axis=-1, keepdims=True) + RMS_EPS) * gain


def _head_of(n):
    return jnp.arange(n) // HEAD_DIM


def _head_ones():
    h = _head_of(BRANCH_W)
    return (h[:, None] == h[None, :]).astype(BF16)


def _head_masks():
    return (_head_of(BRANCH_W)[None, :] == jnp.arange(N_HEADS)[:, None]).astype(F32)


def _rmsnorm_kernel(x_ref, g_ref, o_ref):
    o_ref[...] = _rms(x_ref[...], g_ref[...]).astype(o_ref.dtype)


def rmsnorm_bf16(x, gain, tm=1024):
    s, d = x.shape
    tm = min(tm, s)
    return pl.pallas_call(
        _rmsnorm_kernel,
        grid=(s // tm,),
        in_specs=[pl.BlockSpec((tm, d), lambda i: (i, 0)), _const_spec((1, d))],
        out_specs=pl.BlockSpec((tm, d), lambda i: (i, 0)),
        out_shape=jax.ShapeDtypeStruct((s, d), BF16),
        compiler_params=_params("parallel"),
        name="rmsnorm",
    )(x, gain.reshape(1, d))


def _matmul_kernel(a_ref, b_ref, o_ref):
    o_ref[...] = _dot(a_ref[...], b_ref[...]).astype(o_ref.dtype)


def matmul(a, b, out_dtype=BF16, tm=1024, tn=512):
    m, k = a.shape
    _, n = b.shape
    tm = min(tm, m)
    return pl.pallas_call(
        _matmul_kernel,
        grid=(n // tn, m // tm),
        in_specs=[pl.BlockSpec((tm, k), lambda j, i: (i, 0)),
                  pl.BlockSpec((k, tn), lambda j, i: (0, j))],
        out_specs=pl.BlockSpec((tm, tn), lambda j, i: (i, j)),
        out_shape=jax.ShapeDtypeStruct((m, n), out_dtype),
        compiler_params=_params("parallel", "parallel"),
        name="in_proj",
    )(a, b)


def _hgrn_kernel(q_ref, f_ref, i_ref, g_ref, lbl_ref, ng_ref, ones_ref, hm_ref, tril_ref,
                 o_ref, st_ref, bpad, kpad, vpad, astack, *, layer, tile):
    c, sub = HG_CHUNK, HG_SUB
    nsub = c // sub

    @pl.when(pl.program_id(0) == 0)
    def _():
        st_ref[...] = jnp.zeros_like(st_ref)
        bpad[...] = jnp.zeros_like(bpad)
        kpad[...] = jnp.zeros_like(kpad)
        vpad[...] = jnp.zeros_like(vpad)

    lbl = lbl_ref[...]
    e = jnp.exp(lbl - jnp.max(lbl, axis=0, keepdims=True))
    p = e / jnp.sum(e, axis=0, keepdims=True)
    lb = jnp.zeros((1, BRANCH_W), F32)
    for l in range(1, layer + 1):
        lb = lb + p[l:l + 1, :]

    ones_bd = ones_ref[...]
    hm = hm_ref[...]
    tril = tril_ref[...]
    row = lax.broadcasted_iota(jnp.int32, (c, 1), 0)
    row_in_sub = row % sub
    bd_mask = ones_bd.astype(F32)

    def chunk(ci, carry):
        r0 = pl.multiple_of(ci * c, c)
        q = q_ref[pl.ds(r0, c), :].astype(F32)
        fl = f_ref[pl.ds(r0, c), :].astype(F32)
        v = i_ref[pl.ds(r0, c), :].astype(F32)
        g = g_ref[pl.ds(r0, c), :].astype(F32)

        sig = _sigmoid(fl)
        logf = jnp.log(lb + (1.0 - lb) * sig)
        kk = (1.0 - lb) * (1.0 - sig)
        b = _dot_exact_lhs(tril, logf)

        bpad[pl.ds(sub, c), :] = b
        kpad[pl.ds(sub, c), :] = kk
        vpad[pl.ds(sub, c), :] = v

        for d in range(sub):
            b_d = bpad[pl.ds(sub - d, c), :]
            k_d = kpad[pl.ds(sub - d, c), :]
            a = jnp.where(row_in_sub >= d, q * k_d * jnp.exp(b - b_d), 0.0)
            astack[pl.ds(d * c, c), :] = a.astype(BF16)
        pall = _dot(astack[...], ones_bd)
        intra = jnp.zeros((c, BRANCH_W), F32)
        for d in range(sub):
            intra = intra + pall[d * c:(d + 1) * c, :] * vpad[pl.ds(sub - d, c), :]

        pieces = [jnp.zeros((sub, BRANCH_W), F32)]
        for si in range(1, nsub):
            lo = si * sub
            m_i = b[lo - 1:lo, :]
            qs = q[lo:lo + sub, :] * jnp.exp(b[lo:lo + sub, :] - m_i)
            ks = (kk[:lo, :] * jnp.exp(m_i - b[:lo, :])).astype(BF16)
            qx = jnp.concatenate([qs * hm[h:h + 1, :] for h in range(N_HEADS)], axis=0).astype(BF16)
            sc = _dot_nt(qx, ks)
            r = _dot(sc.astype(BF16), v[:lo, :].astype(BF16))
            acc = jnp.zeros((sub, BRANCH_W), F32)
            for h in range(N_HEADS):
                acc = acc + r[h * sub:(h + 1) * sub, :] * hm[h:h + 1, :]
            pieces.append(acc)
        intra = intra + jnp.concatenate(pieces, axis=0)

        st = st_ref[...]
        inter = _dot_nt((q * jnp.exp(b)).astype(BF16), st.astype(BF16))
        b_last = b[c - 1:c, :]
        ks_end = (kk * jnp.exp(b_last - b)).astype(BF16)
        upd = _dot(v.T.astype(BF16), ks_end)
        st_ref[...] = st * jnp.exp(b_last) + upd * bd_mask

        o = intra + inter
        ms = _dot_exact_rhs(o * o, ones_bd) * (1.0 / HEAD_DIM)
        y = o * lax.rsqrt(ms + RMS_EPS) * ng_ref[...] * _silu(g)
        o_ref[pl.ds(r0, c), :] = y.astype(o_ref.dtype)
        return carry

    lax.fori_loop(0, tile // c, chunk, 0)


def hgrn2(proj, lb_logits, norm_g, layer, tile=512):
    s = proj.shape[0]
    depth = lb_logits.shape[0]
    c, sub = HG_CHUNK, HG_SUB
    tile = min(tile, s)
    col = lambda j: pl.BlockSpec((tile, BRANCH_W), lambda i, j=j: (i, j))
    tril = jnp.tril(jnp.ones((c, c), F32)).astype(BF16)
    return pl.pallas_call(
        functools.partial(_hgrn_kernel, layer=layer, tile=tile),
        grid=(s // tile,),
        in_specs=[col(0), col(1), col(2), col(3),
                  _const_spec((depth, BRANCH_W)), _const_spec((1, BRANCH_W)),
                  _const_spec((BRANCH_W, BRANCH_W)), _const_spec((N_HEADS, BRANCH_W)),
                  _const_spec((c, c))],
        out_specs=pl.BlockSpec((tile, BRANCH_W), lambda i: (i, 0)),
        out_shape=jax.ShapeDtypeStruct((s, BRANCH_W), BF16),
        scratch_shapes=[pltpu.VMEM((BRANCH_W, BRANCH_W), F32),
                        pltpu.VMEM((c + sub, BRANCH_W), F32),
                        pltpu.VMEM((c + sub, BRANCH_W), F32),
                        pltpu.VMEM((c + sub, BRANCH_W), F32),
                        pltpu.VMEM((sub * c, BRANCH_W), BF16)],
        compiler_params=_params("arbitrary"),
        name="hgrn2",
    )(proj, proj, proj, proj, lb_logits.astype(F32), norm_g.reshape(1, BRANCH_W).astype(F32),
      _head_ones(), _head_masks(), tril)


FOX_TILE = 256
FOX_NSTAT = 16
FOX_SKIP_LOG = 40.0


def _fox_gate_kernel(h_ref, q_ref, k_ref, wf_ref, bias_ref, tril_ref, ones_ref, expand_ref,
                     ct_ref, stat_ref, carry_ref):
    @pl.when(pl.program_id(0) == 0)
    def _():
        carry_ref[...] = jnp.zeros_like(carry_ref)

    logit = _dot(h_ref[...], wf_ref[...]) + bias_ref[...]
    logf = jnp.minimum(logit, 0.0) - jnp.log(1.0 + jnp.exp(-jnp.abs(logit)))
    cum = _dot_exact_lhs(tril_ref[...], logf) + carry_ref[...]
    carry_ref[...] = cum[-1:, :]
    ct_ref[...] = cum.T[:SUBLANE, :]

    ones_bd = ones_ref[...]
    q = q_ref[...].astype(F32)
    k = k_ref[...].astype(F32)
    scale = HEAD_DIM ** -0.5
    c_heads = _dot_exact_rhs(cum, expand_ref[...])
    diag = _dot_exact_rhs(q * k, ones_bd) * scale
    qn = jnp.sqrt(_dot_exact_rhs(q * q, ones_bd)) * scale
    kn = jnp.sqrt(_dot_exact_rhs(k * k, ones_bd))
    rows = [jnp.max(qn, axis=0, keepdims=True),
            jnp.max(c_heads - diag, axis=0, keepdims=True),
            jnp.max(kn, axis=0, keepdims=True),
            c_heads[-1:, :]]
    stat_ref[0] = jnp.concatenate(rows + [jnp.zeros((SUBLANE - len(rows), BRANCH_W), F32)], axis=0)


def fox_gate(h, proj, w_f, f_bias):
    s, d = h.shape
    tile = min(FOX_TILE, s)
    bias = jnp.zeros((1, LANE), F32).at[0, :N_HEADS].set(f_bias.astype(F32))
    tril = jnp.tril(jnp.ones((tile, tile), F32)).astype(BF16)
    expand = (jnp.arange(LANE)[:, None] == _head_of(BRANCH_W)[None, :]).astype(BF16)
    ct, stats = pl.pallas_call(
        _fox_gate_kernel,
        grid=(s // tile,),
        in_specs=[pl.BlockSpec((tile, d), lambda i: (i, 0)),
                  pl.BlockSpec((tile, BRANCH_W), lambda i: (i, 4)),
                  pl.BlockSpec((tile, BRANCH_W), lambda i: (i, 5)),
                  _const_spec((d, LANE)), _const_spec((1, LANE)), _const_spec((tile, tile)),
                  _const_spec((BRANCH_W, BRANCH_W)), _const_spec((LANE, BRANCH_W))],
        out_specs=[pl.BlockSpec((SUBLANE, tile), lambda i: (0, i)),
                   pl.BlockSpec((1, SUBLANE, BRANCH_W), lambda i: (i, 0, 0))],
        out_shape=[jax.ShapeDtypeStruct((SUBLANE, s), F32),
                   jax.ShapeDtypeStruct((s // tile, SUBLANE, BRANCH_W), F32)],
        scratch_shapes=[pltpu.VMEM((1, LANE), F32)],
        compiler_params=_params("arbitrary"),
        name="fox_gate",
    )(h, proj, proj, w_f, bias, tril, _head_ones(), expand)
    return ct, stats[:, :4, ::HEAD_DIM].reshape(-1)


def _fox_kernel(stat_ref, q_ref, k_ref, v_ref, ct_ref, hm_ref, o_ref, m_sc, l_sc, acc_sc, out_sc, *, tq):
    i = pl.program_id(0)
    q0 = pl.multiple_of(i * tq, tq)
    hm = hm_ref[...]
    q = q_ref[...].astype(F32) * (HEAD_DIM ** -0.5)
    c_q0 = ct_ref[:, pl.ds(q0, tq)][:, 0:1]

    for h in range(N_HEADS):
        qh = (q * hm[h:h + 1, :]).astype(BF16)

        qmax = stat_ref[i * FOX_NSTAT + h]
        emax = stat_ref[i * FOX_NSTAT + N_HEADS + h]

        def scan(j, carry):
            kmax, first = carry
            kmax = jnp.maximum(kmax, stat_ref[j * FOX_NSTAT + 2 * N_HEADS + h])
            bound = qmax * kmax + emax - stat_ref[j * FOX_NSTAT + 3 * N_HEADS + h]
            return kmax, jnp.where(bound < -FOX_SKIP_LOG, j + 1, first)

        _, first = lax.fori_loop(0, i, scan, (jnp.float32(0.0), jnp.int32(0)))

        m_sc[...] = jnp.full_like(m_sc, -jnp.inf)
        l_sc[...] = jnp.zeros_like(l_sc)
        acc_sc[...] = jnp.zeros_like(acc_sc)

        def block(s0, diagonal):
            kb = k_ref[pl.ds(s0, tq), :]
            vb = v_ref[pl.ds(s0, tq), :]
            bias = c_q0 - ct_ref[:, pl.ds(s0, tq)]
            sc = _dot_nt(qh, kb) + bias[h:h + 1, :]
            if diagonal:
                r = lax.broadcasted_iota(jnp.int32, (tq, tq), 0)
                cidx = lax.broadcasted_iota(jnp.int32, (tq, tq), 1)
                sc = jnp.where(cidx <= r, sc, -jnp.inf)
            m_prev = m_sc[...]
            m_new = jnp.maximum(m_prev, jnp.max(sc, axis=1, keepdims=True))
            alpha = jnp.exp(m_prev - m_new)
            p = jnp.exp(sc - jnp.tile(m_new, (1, tq // LANE)))
            l_sc[...] = alpha * l_sc[...] + jnp.sum(p, axis=1, keepdims=True)
            acc_sc[...] = acc_sc[...] * jnp.tile(alpha, (1, BRANCH_W // LANE)) + _dot(p.astype(BF16), vb)
            m_sc[...] = m_new

        def body(j, carry):
            block(pl.multiple_of(j * tq, tq), False)
            return carry

        lax.fori_loop(first, i, body, 0)
        block(q0, True)

        res = acc_sc[...] * hm[h:h + 1, :] / jnp.tile(l_sc[...], (1, BRANCH_W // LANE))
        if h == 0:
            out_sc[...] = res
        else:
            out_sc[...] += res
    o_ref[...] = out_sc[...].astype(o_ref.dtype)


def fox_attention(proj, ct, stats):
    s = proj.shape[0]
    tq = min(FOX_TILE, s)
    full = lambda j: pl.BlockSpec((s, BRANCH_W), lambda i, j=j: (0, j), pipeline_mode=pl.Buffered(1))
    return pl.pallas_call(
        functools.partial(_fox_kernel, tq=tq),
        grid=(s // tq,),
        in_specs=[pl.BlockSpec(memory_space=pltpu.SMEM),
                  pl.BlockSpec((tq, BRANCH_W), lambda i: (i, 4)), full(5), full(6),
                  _const_spec((SUBLANE, s)), _const_spec((N_HEADS, BRANCH_W))],
        out_specs=pl.BlockSpec((tq, BRANCH_W), lambda i: (i, 0)),
        out_shape=jax.ShapeDtypeStruct((s, BRANCH_W), BF16),
        scratch_shapes=[pltpu.VMEM((tq, LANE), F32),
                        pltpu.VMEM((tq, LANE), F32),
                        pltpu.VMEM((tq, BRANCH_W), F32),
                        pltpu.VMEM((tq, BRANCH_W), F32)],
        compiler_params=_params("parallel"),
        name="fox_attention",
    )(stats, proj, proj, proj, ct, _head_masks())


def _pool_kernel(u_ref, w_ref, scale_ref, o_ref, ext, *, tile):
    i = pl.program_id(0)

    @pl.when(i == 0)
    def _():
        ext[pl.ds(0, POOL_HALO), :] = jnp.zeros((POOL_HALO, BRANCH_W), F32)

    u = u_ref[...].astype(F32)
    ext[pl.ds(POOL_HALO, tile), :] = u
    pos = (i * tile + lax.broadcasted_iota(jnp.int32, (tile, 1), 0) + 1).astype(F32)
    halves = []
    for half in range(BRANCH_W // LANE):
        lanes = pl.ds(half * LANE, LANE)
        w_small, w_big = POOL_WINDOWS[2 * half], POOL_WINDOWS[2 * half + 1]
        run = u[:, half * LANE:(half + 1) * LANE]
        sums = {}
        for j in range(1, w_big):
            if j == w_small:
                sums[w_small] = run
            run = run + ext[pl.ds(POOL_HALO - j, tile), lanes]
        sums[w_big] = run
        lane = lax.broadcasted_iota(jnp.int32, (1, LANE), 1)
        small = lane < POOL_GROUP
        total = jnp.where(small, sums[w_small], sums[w_big])
        count = jnp.where(small, jnp.minimum(pos, float(w_small)), jnp.minimum(pos, float(w_big)))
        halves.append(total / count)
    mean = jnp.concatenate(halves, axis=1)
    d = (mean - u).astype(BF16)
    y = _dot(d, w_ref[...]) * scale_ref[...]
    o_ref[...] = y.astype(o_ref.dtype)
    ext[pl.ds(0, POOL_HALO), :] = u[tile - POOL_HALO:, :]


def pool_mixer(proj, w_pool, scale, tile=512):
    s = proj.shape[0]
    tile = min(tile, s)
    ng = len(POOL_WINDOWS)
    w_bd = jnp.zeros((BRANCH_W, BRANCH_W), F32)
    for gi in range(ng):
        lo = gi * POOL_GROUP
        w_bd = w_bd.at[lo:lo + POOL_GROUP, lo:lo + POOL_GROUP].set(w_pool[gi].astype(F32))
    return pl.pallas_call(
        functools.partial(_pool_kernel, tile=tile),
        grid=(s // tile,),
        in_specs=[pl.BlockSpec((tile, BRANCH_W), lambda i: (i, 7)),
                  _const_spec((BRANCH_W, BRANCH_W)), _const_spec((1, BRANCH_W))],
        out_specs=pl.BlockSpec((tile, BRANCH_W), lambda i: (i, 0)),
        out_shape=jax.ShapeDtypeStruct((s, BRANCH_W), BF16),
        scratch_shapes=[pltpu.VMEM((tile + POOL_HALO, BRANCH_W), F32)],
        compiler_params=_params("arbitrary"),
        name="pool_mixer",
    )(proj, w_bd.astype(BF16), scale.reshape(1, BRANCH_W).astype(F32))


def _ret_kernel(q_ref, k_ref, v_ref, g_ref, cos_ref, sin_ref, perm_ref, ones_ref, hm_ref,
                dstack_ref, xi_ref, zeta_ref, gc_ref, gng_ref, gnb_ref, o_ref, st_ref, *, tile):
    c = RET_CHUNK

    @pl.when(pl.program_id(0) == 0)
    def _():
        st_ref[...] = jnp.zeros_like(st_ref)

    perm = perm_ref[...]
    ones_bd = ones_ref[...]
    bd_mask = ones_bd.astype(F32)
    hm = hm_ref[...]

    def chunk(ci, carry):
        r0 = pl.multiple_of(ci * c, c)
        qb = q_ref[pl.ds(r0, c), :]
        kb = k_ref[pl.ds(r0, c), :]
        v = v_ref[pl.ds(r0, c), :]
        g = g_ref[pl.ds(r0, c), :].astype(F32)
        cos = cos_ref[pl.ds(r0, c), :]
        sin = sin_ref[pl.ds(r0, c), :]
        qr = qb.astype(F32) * cos + _dot(qb, perm) * sin
        kr = (kb.astype(F32) * cos + _dot(kb, perm) * sin) * (HEAD_DIM ** -0.5)

        qx = jnp.concatenate([qr * hm[h:h + 1, :] for h in range(N_HEADS)], axis=0).astype(BF16)
        sc = _dot_nt(qx, kr.astype(BF16)) * dstack_ref[...]
        r = _dot(sc.astype(BF16), v)
        intra = jnp.zeros((c, BRANCH_W), F32)
        for h in range(N_HEADS):
            intra = intra + r[h * c:(h + 1) * c, :] * hm[h:h + 1, :]

        st = st_ref[...]
        inter = _dot_nt((qr * xi_ref[...]).astype(BF16), st.astype(BF16))
        upd = _dot(v.astype(F32).T.astype(BF16), (kr * zeta_ref[...]).astype(BF16))
        st_ref[...] = st * gc_ref[...] + upd * bd_mask

        o = intra + inter
        mu = _dot_exact_rhs(o, ones_bd) * (1.0 / HEAD_DIM)
        cen = o - mu
        var = _dot_exact_rhs(cen * cen, ones_bd) * (1.0 / HEAD_DIM)
        y = cen * lax.rsqrt(var + LN_EPS) * gng_ref[...] + gnb_ref[...]
        o_ref[pl.ds(r0, c), :] = (y * _silu(g)).astype(o_ref.dtype)
        return carry

    lax.fori_loop(0, tile // c, chunk, 0)


def _rope_tables(s):
    half = HEAD_DIM // 2
    pos = jnp.arange(s, dtype=F32)
    inv_freq = ROPE_BASE ** (-jnp.arange(half, dtype=F32) / half)
    ang = pos[:, None] * inv_freq[None, :]
    cos, sin = jnp.cos(ang), jnp.sin(ang)
    cos_t = jnp.tile(jnp.concatenate([cos, cos], axis=1), (1, N_HEADS))
    sin_t = jnp.tile(jnp.concatenate([-sin, sin], axis=1), (1, N_HEADS))
    return cos_t, sin_t


def _ret_constants():
    c = RET_CHUNK
    half = HEAD_DIM // 2
    lane = jnp.arange(BRANCH_W)
    partner = jnp.where(lane % HEAD_DIM < half, lane + half, lane - half)
    perm = (lane[:, None] == partner[None, :]).astype(BF16)
    log_gamma = jnp.log1p(-jnp.exp2(-RET_DECAY_BASE - jnp.arange(N_HEADS, dtype=F32)))
    ci = jnp.arange(c, dtype=F32)
    diff = ci[:, None] - ci[None, :]
    intra = jnp.where(diff >= 0, jnp.exp(diff * log_gamma[:, None, None]), 0.0)
    dstack = intra.reshape(N_HEADS * c, c)
    lg_lane = jnp.repeat(log_gamma, HEAD_DIM)[None, :]
    xi = jnp.exp((ci[:, None] + 1.0) * lg_lane)
    zeta = jnp.exp((c - 1.0 - ci[:, None]) * lg_lane)
    gc = jnp.exp(c * lg_lane)
    return perm, dstack, xi, zeta, gc


def retention(proj, gn_g, gn_b, tile=512):
    s = proj.shape[0]
    c = RET_CHUNK
    tile = min(tile, s)
    cos_t, sin_t = _rope_tables(s)
    perm, dstack, xi, zeta, gc = _ret_constants()
    col = lambda j: pl.BlockSpec((tile, BRANCH_W), lambda i, j=j: (i, j))
    row = pl.BlockSpec((tile, BRANCH_W), lambda i: (i, 0))
    return pl.pallas_call(
        functools.partial(_ret_kernel, tile=tile),
        grid=(s // tile,),
        in_specs=[col(8), col(9), col(10), col(11), row, row,
                  _const_spec((BRANCH_W, BRANCH_W)), _const_spec((BRANCH_W, BRANCH_W)),
                  _const_spec((N_HEADS, BRANCH_W)), _const_spec((N_HEADS * c, c)),
                  _const_spec((c, BRANCH_W)), _const_spec((c, BRANCH_W)), _const_spec((1, BRANCH_W)),
                  _const_spec((1, BRANCH_W)), _const_spec((1, BRANCH_W))],
        out_specs=row,
        out_shape=jax.ShapeDtypeStruct((s, BRANCH_W), BF16),
        scratch_shapes=[pltpu.VMEM((BRANCH_W, BRANCH_W), F32)],
        compiler_params=_params("arbitrary"),
        name="retention",
    )(proj, proj, proj, proj, cos_t, sin_t, perm, _head_ones(), _head_masks(), dstack, xi, zeta, gc,
      gn_g.reshape(1, BRANCH_W).astype(F32), gn_b.reshape(1, BRANCH_W).astype(F32))


def _merge_kernel(h_ref, o0_ref, o1_ref, o2_ref, o3_ref, x_ref, wg_ref, wb_ref, wo_ref, g_ref,
                  xo_ref, ho_ref):
    h = h_ref[...]
    merged = jnp.zeros(x_ref.shape, F32)
    for bi, o_ref in enumerate((o0_ref, o1_ref, o2_ref, o3_ref)):
        gate = _sigmoid(_dot(h, wg_ref[:, bi * D_MODEL:(bi + 1) * D_MODEL]))
        merged = merged + gate * _dot(o_ref[...], wb_ref[bi])
    x_new = x_ref[...] + _dot(merged.astype(BF16), wo_ref[...])
    xo_ref[...] = x_new
    ho_ref[...] = _rms(x_new, g_ref[...]).astype(ho_ref.dtype)


def merge(h, branches, x, w_gate, w_branch, w_out, next_gain, tm=512):
    s, d = x.shape
    tm = min(tm, s)
    row = lambda w: pl.BlockSpec((tm, w), lambda i: (i, 0))
    return pl.pallas_call(
        _merge_kernel,
        grid=(s // tm,),
        in_specs=[row(d), row(BRANCH_W), row(BRANCH_W), row(BRANCH_W), row(BRANCH_W), row(d),
                  _const_spec((d, N_BRANCH * d)), _const_spec((N_BRANCH, BRANCH_W, d)),
                  _const_spec((d, d)), _const_spec((1, d))],
        out_specs=[row(d), row(d)],
        out_shape=[jax.ShapeDtypeStruct((s, d), F32), jax.ShapeDtypeStruct((s, d), BF16)],
        compiler_params=_params("parallel"),
        name="merge",
    )(h, *branches, x, w_gate, w_branch, w_out, next_gain.reshape(1, d).astype(F32))


def _ffn_kernel(h_ref, x_ref, wg_ref, wu_ref, wd_ref, g_ref, xo_ref, ho_ref, acc_ref):
    f = pl.program_id(1)

    @pl.when(f == 0)
    def _():
        acc_ref[...] = jnp.zeros_like(acc_ref)

    h = h_ref[...]
    a = _silu(_dot(h, wg_ref[...])) * _dot(h, wu_ref[...])
    acc_ref[...] += _dot(a.astype(BF16), wd_ref[...])

    @pl.when(f == pl.num_programs(1) - 1)
    def _():
        x_new = x_ref[...] + acc_ref[...]
        xo_ref[...] = x_new
        ho_ref[...] = _rms(x_new, g_ref[...]).astype(ho_ref.dtype)


def ffn_dense(h, x, w_gate, w_up, w_down, next_gain, tm=1024, tf=512):
    s, d = x.shape
    tm = min(tm, s)
    dff = w_gate.shape[1]
    row = lambda: pl.BlockSpec((tm, d), lambda i, f: (i, 0))
    return pl.pallas_call(
        _ffn_kernel,
        grid=(s // tm, dff // tf),
        in_specs=[row(), row(),
                  pl.BlockSpec((d, tf), lambda i, f: (0, f)),
                  pl.BlockSpec((d, tf), lambda i, f: (0, f)),
                  pl.BlockSpec((tf, d), lambda i, f: (f, 0)),
                  _const_spec((1, d))],
        out_specs=[row(), row()],
        out_shape=[jax.ShapeDtypeStruct((s, d), F32), jax.ShapeDtypeStruct((s, d), BF16)],
        scratch_shapes=[pltpu.VMEM((tm, d), F32)],
        compiler_params=_params("parallel", "arbitrary"),
        name="ffn_dense",
    )(h, x, w_gate, w_up, w_down, next_gain.reshape(1, d).astype(F32))


def _moe_kernel(h_ref, x_ref, ng_ref, router_ref, wg_ref, wu_ref, wd_ref, fg_ref, o_ref,
                acc_ref, gate_ref):
    e = pl.program_id(1)
    f = pl.program_id(2)
    tm = x_ref.shape[0]

    @pl.when((e == 0) & (f == 0))
    def _():
        acc_ref[...] = jnp.zeros_like(acc_ref)
        hn = _rms(x_ref[...], ng_ref[...])
        logits = _dot_exact_lhs_f32(hn, router_ref[...])
        lane = lax.broadcasted_iota(jnp.int32, (tm, LANE), 1)
        logits = jnp.where(lane < N_EXPERTS, logits, -jnp.inf)
        v1 = jnp.max(logits, axis=1, keepdims=True)
        i1 = jnp.min(jnp.where(logits == v1, lane, LANE), axis=1, keepdims=True)
        rest = jnp.where(lane == i1, -jnp.inf, logits)
        v2 = jnp.max(rest, axis=1, keepdims=True)
        i2 = jnp.min(jnp.where(rest == v2, lane, LANE), axis=1, keepdims=True)
        w1 = 1.0 / (1.0 + jnp.exp(v2 - v1))
        gate_ref[...] = jnp.where(lane == i1, w1, 0.0) + jnp.where(lane == i2, 1.0 - w1, 0.0)

    lane = lax.broadcasted_iota(jnp.int32, (tm, LANE), 1)
    ge = jnp.sum(jnp.where(lane == e, gate_ref[...], 0.0), axis=1, keepdims=True)
    h = h_ref[...]
    a = _silu(_dot(h, wg_ref[0])) * _dot(h, wu_ref[0]) * ge
    acc_ref[...] += _dot(a.astype(BF16), wd_ref[0])

    @pl.when((e == pl.num_programs(1) - 1) & (f == pl.num_programs(2) - 1))
    def _():
        o_ref[...] = _rms(x_ref[...] + acc_ref[...], fg_ref[...]).astype(o_ref.dtype)


def _dot_exact_lhs_f32(x, w):
    xh, xm, xl = _split3(x)
    wh, wm, wl = _split3(w)
    return (_dot(xh, wh) + (_dot(xh, wm) + _dot(xm, wh))
            + (_dot(xh, wl) + _dot(xm, wm) + _dot(xl, wh)))


def moe_dense(h, x, norm_gain, router, w_gate, w_up, w_down, final_gain, tm=1024, tf=512):
    s, d = x.shape
    tm = min(tm, s)
    ne, _, dff = w_gate.shape
    router_p = jnp.zeros((d, LANE), F32).at[:, :ne].set(router.astype(F32))
    row = lambda: pl.BlockSpec((tm, d), lambda i, e, f: (i, 0))
    return pl.pallas_call(
        _moe_kernel,
        grid=(s // tm, ne, dff // tf),
        in_specs=[row(), row(), _const_spec((1, d)), _const_spec((d, LANE)),
                  pl.BlockSpec((1, d, tf), lambda i, e, f: (e, 0, f)),
                  pl.BlockSpec((1, d, tf), lambda i, e, f: (e, 0, f)),
                  pl.BlockSpec((1, tf, d), lambda i, e, f: (e, f, 0)),
                  _const_spec((1, d))],
        out_specs=row(),
        out_shape=jax.ShapeDtypeStruct((s, d), F32),
        scratch_shapes=[pltpu.VMEM((tm, d), F32), pltpu.VMEM((tm, LANE), F32)],
        compiler_params=_params("parallel", "arbitrary", "arbitrary"),
        name="moe",
    )(h, x, norm_gain.reshape(1, d).astype(F32), router_p, w_gate, w_up, w_down,
      final_gain.reshape(1, d).astype(F32))


FOX_F0 = 7 * BRANCH_W


def _split_w_in_kernel(w_ref, mix_ref, f_ref, gate_ref):
    w = w_ref[...]
    rows = w.shape[0]
    mix_ref[:, :FOX_F0] = w[:, :FOX_F0].astype(BF16)
    mix_ref[:, FOX_F0:] = w[:, FOX_F0 + N_HEADS:N_MIX_COLS + N_HEADS].astype(BF16)
    f_ref[...] = jnp.concatenate([w[:, FOX_F0:FOX_F0 + N_HEADS], jnp.zeros((rows, LANE - N_HEADS), F32)],
                                 axis=1).astype(BF16)
    gate_ref[...] = w[:, N_MIX_COLS + N_HEADS:].astype(BF16)


def _mixer_weights(w_in, layer, tr=128):
    _, d, cols = w_in.shape
    row = lambda w: pl.BlockSpec((tr, w), lambda i: (i, 0))
    return pl.pallas_call(
        _split_w_in_kernel,
        grid=(d // tr,),
        in_specs=[pl.BlockSpec((None, tr, cols), lambda i: (layer, i, 0))],
        out_specs=[row(N_MIX_COLS), row(LANE), row(N_BRANCH * D_MODEL)],
        out_shape=[jax.ShapeDtypeStruct((d, N_MIX_COLS), BF16), jax.ShapeDtypeStruct((d, LANE), BF16),
                   jax.ShapeDtypeStruct((d, N_BRANCH * D_MODEL), BF16)],
        compiler_params=_params("parallel"),
        name="split_w_in",
    )(w_in)


def kernel(x, w_in, w_branch, w_out, norm_mix_g, hgrn_lb_logits, hgrn_norm_g, fox_f_bias, pool_w, pool_scale,
           ret_gn_g, ret_gn_b, norm_ffn_g, ffn_w_gate, ffn_w_up, ffn_w_down, moe_router, moe_w_gate, moe_w_up,
           moe_w_down, final_norm_g):
    b, s, d = x.shape
    assert b == 1 and d == D_MODEL
    depth = w_in.shape[0]
    assert depth == 2, "layer 0 uses the dense FFN, layer 1 the experts and the final norm"
    xs = x.reshape(s, d)
    h = rmsnorm_bf16(xs, norm_mix_g[0])
    out = None
    for layer in range(depth):
        w_mix, w_f, w_gate = _mixer_weights(w_in, layer)
        proj = matmul(h, w_mix)
        ct, fox_stats = fox_gate(h, proj, w_f, fox_f_bias[layer])
        branches = (
            hgrn2(proj, hgrn_lb_logits, hgrn_norm_g[layer], layer),
            fox_attention(proj, ct, fox_stats),
            pool_mixer(proj, pool_w[layer], pool_scale[layer]),
            retention(proj, ret_gn_g[layer], ret_gn_b[layer]),
        )
        xs, h2 = merge(h, branches, xs, w_gate, w_branch[layer].astype(BF16), w_out[layer].astype(BF16),
                       norm_ffn_g[layer])
        if layer % 2 == 0:
            li = layer // 2
            xs, h = ffn_dense(h2, xs, ffn_w_gate[li].astype(BF16), ffn_w_up[li].astype(BF16),
                              ffn_w_down[li].astype(BF16), norm_mix_g[layer + 1])
        else:
            li = layer // 2
            out = moe_dense(h2, xs, norm_ffn_g[layer], moe_router[li], moe_w_gate[li].astype(BF16),
                            moe_w_up[li].astype(BF16), moe_w_down[li].astype(BF16), final_norm_g)
    return out.reshape(b, s, d)
```

```python
import functools
import math

import jax
import jax.numpy as jnp
from jax import lax
from jax.experimental import pallas as pl
from jax.experimental.pallas import tpu as pltpu

D_MODEL = 1024
N_BRANCH = 4
BRANCH_W = D_MODEL // N_BRANCH
HEAD_DIM = 64
N_HEADS = BRANCH_W // HEAD_DIM
POOL_WINDOWS = (2, 4, 8, 16)
POOL_GROUP = BRANCH_W // len(POOL_WINDOWS)
POOL_HALO = 16
RET_DECAY_BASE = 5.0
ROPE_BASE = 10000.0
D_FF = 7 * D_MODEL // 2
N_EXPERTS = 8
RMS_EPS = 1e-6
LN_EPS = 1e-5
N_MIX_COLS = 12 * BRANCH_W

LANE = 128
SUBLANE = 8
VMEM_LIMIT = 56 * 1024 * 1024

HG_CHUNK = 64
HG_SUB = 16
RET_CHUNK = 64

F32 = jnp.float32
BF16 = jnp.bfloat16
NT_DIMS = (((1,), (1,)), ((), ()))


def _params(*sem):
    return pltpu.CompilerParams(dimension_semantics=sem, vmem_limit_bytes=VMEM_LIMIT)


def _const_spec(shape):
    nd = len(shape)
    return pl.BlockSpec(shape, lambda *_: (0,) * nd, pipeline_mode=pl.Buffered(1))


def _split3(x):
    hi = x.astype(BF16)
    r1 = x - hi.astype(F32)
    mid = r1.astype(BF16)
    lo = (r1 - mid.astype(F32)).astype(BF16)
    return hi, mid, lo


def _dot(a, b):
    return jnp.dot(a, b, preferred_element_type=F32)


def _dot_nt(a, b):
    return lax.dot_general(a, b, NT_DIMS, preferred_element_type=F32)


def _dot_exact_rhs(x, m_bf16):
    hi, mid, lo = _split3(x)
    return _dot(hi, m_bf16) + _dot(mid, m_bf16) + _dot(lo, m_bf16)


def _dot_exact_lhs(m_bf16, x):
    hi, mid, lo = _split3(x)
    return _dot(m_bf16, hi) + _dot(m_bf16, mid) + _dot(m_bf16, lo)


def _sigmoid(x):
    return 1.0 / (1.0 + jnp.exp(-x))


def _silu(x):
    return x * _sigmoid(x)


def _rms(x, gain):
    return x * lax.rsqrt(jnp.mean(x * x, axis=-1, keepdims=True) + RMS_EPS) * gain


def _head_of(n):
    return jnp.arange(n) // HEAD_DIM


def _head_ones():
    h = _head_of(BRANCH_W)
    return (h[:, None] == h[None, :]).astype(BF16)


def _head_masks():
    return (_head_of(BRANCH_W)[None, :] == jnp.arange(N_HEADS)[:, None]).astype(F32)


def _rmsnorm_kernel(x_ref, g_ref, o_ref):
    o_ref[...] = _rms(x_ref[...], g_ref[...]).astype(o_ref.dtype)


def rmsnorm_bf16(x, gain, tm=1024):
    s, d = x.shape
    tm = min(tm, s)
    return pl.pallas_call(
        _rmsnorm_kernel,
        grid=(s // tm,),
        in_specs=[pl.BlockSpec((tm, d), lambda i: (i, 0)), _const_spec((1, d))],
        out_specs=pl.BlockSpec((tm, d), lambda i: (i, 0)),
        out_shape=jax.ShapeDtypeStruct((s, d), BF16),
        compiler_params=_params("parallel"),
        name="rmsnorm",
    )(x, gain.reshape(1, d))


def _matmul_kernel(a_ref, b_ref, o_ref):
    o_ref[...] = _dot(a_ref[...], b_ref[...]).astype(o_ref.dtype)


def matmul(a, b, out_dtype=BF16, tm=1024, tn=512):
    m, k = a.shape
    _, n = b.shape
    tm = min(tm, m)
    return pl.pallas_call(
        _matmul_kernel,
        grid=(n // tn, m // tm),
        in_specs=[pl.BlockSpec((tm, k), lambda j, i: (i, 0)),
                  pl.BlockSpec((k, tn), lambda j, i: (0, j))],
        out_specs=pl.BlockSpec((tm, tn), lambda j, i: (i, j)),
        out_shape=jax.ShapeDtypeStruct((m, n), out_dtype),
        compiler_params=_params("parallel", "parallel"),
        name="in_proj",
    )(a, b)


def _hgrn_kernel(q_ref, f_ref, i_ref, g_ref, lbl_ref, ng_ref, ones_ref, hm_ref, tril_ref,
                 o_ref, st_ref, bpad, kpad, vpad, astack, *, layer, tile):
    c, sub = HG_CHUNK, HG_SUB
    nsub = c // sub

    @pl.when(pl.program_id(0) == 0)
    def _():
        st_ref[...] = jnp.zeros_like(st_ref)
        bpad[...] = jnp.zeros_like(bpad)
        kpad[...] = jnp.zeros_like(kpad)
        vpad[...] = jnp.zeros_like(vpad)

    lbl = lbl_ref[...]
    e = jnp.exp(lbl - jnp.max(lbl, axis=0, keepdims=True))
    p = e / jnp.sum(e, axis=0, keepdims=True)
    lb = jnp.zeros((1, BRANCH_W), F32)
    for l in range(1, layer + 1):
        lb = lb + p[l:l + 1, :]

    ones_bd = ones_ref[...]
    hm = hm_ref[...]
    tril = tril_ref[...]
    row = lax.broadcasted_iota(jnp.int32, (c, 1), 0)
    row_in_sub = row % sub
    bd_mask = ones_bd.astype(F32)

    def chunk(ci, carry):
        r0 = pl.multiple_of(ci * c, c)
        q = q_ref[pl.ds(r0, c), :].astype(F32)
        fl = f_ref[pl.ds(r0, c), :].astype(F32)
        v = i_ref[pl.ds(r0, c), :].astype(F32)
        g = g_ref[pl.ds(r0, c), :].astype(F32)

        sig = _sigmoid(fl)
        logf = jnp.log(lb + (1.0 - lb) * sig)
        kk = (1.0 - lb) * (1.0 - sig)
        b = _dot_exact_lhs(tril, logf)

        bpad[pl.ds(sub, c), :] = b
        kpad[pl.ds(sub, c), :] = kk
        vpad[pl.ds(sub, c), :] = v

        for d in range(sub):
            b_d = bpad[pl.ds(sub - d, c), :]
            k_d = kpad[pl.ds(sub - d, c), :]
            a = jnp.where(row_in_sub >= d, q * k_d * jnp.exp(b - b_d), 0.0)
            astack[pl.ds(d * c, c), :] = a.astype(BF16)
        pall = _dot(astack[...], ones_bd)
        intra = jnp.zeros((c, BRANCH_W), F32)
        for d in range(sub):
            intra = intra + pall[d * c:(d + 1) * c, :] * vpad[pl.ds(sub - d, c), :]

        pieces = [jnp.zeros((sub, BRANCH_W), F32)]
        for si in range(1, nsub):
            lo = si * sub
            m_i = b[lo - 1:lo, :]
            qs = q[lo:lo + sub, :] * jnp.exp(b[lo:lo + sub, :] - m_i)
            ks = (kk[:lo, :] * jnp.exp(m_i - b[:lo, :])).astype(BF16)
            qx = jnp.concatenate([qs * hm[h:h + 1, :] for h in range(N_HEADS)], axis=0).astype(BF16)
            sc = _dot_nt(qx, ks)
            r = _dot(sc.astype(BF16), v[:lo, :].astype(BF16))
            acc = jnp.zeros((sub, BRANCH_W), F32)
            for h in range(N_HEADS):
                acc = acc + r[h * sub:(h + 1) * sub, :] * hm[h:h + 1, :]
            pieces.append(acc)
        intra = intra + jnp.concatenate(pieces, axis=0)

        st = st_ref[...]
        inter = _dot_nt((q * jnp.exp(b)).astype(BF16), st.astype(BF16))
        b_last = b[c - 1:c, :]
        ks_end = (kk * jnp.exp(b_last - b)).astype(BF16)
        upd = _dot(v.T.astype(BF16), ks_end)
        st_ref[...] = st * jnp.exp(b_last) + upd * bd_mask

        o = intra + inter
        ms = _dot_exact_rhs(o * o, ones_bd) * (1.0 / HEAD_DIM)
        y = o * lax.rsqrt(ms + RMS_EPS) * ng_ref[...] * _silu(g)
        o_ref[pl.ds(r0, c), :] = y.astype(o_ref.dtype)
        return carry

    lax.fori_loop(0, tile // c, chunk, 0)


def hgrn2(proj, lb_logits, norm_g, layer, tile=512):
    s = proj.shape[0]
    depth = lb_logits.shape[0]
    c, sub = HG_CHUNK, HG_SUB
    tile = min(tile, s)
    col = lambda j: pl.BlockSpec((tile, BRANCH_W), lambda i, j=j: (i, j))
    tril = jnp.tril(jnp.ones((c, c), F32)).astype(BF16)
    return pl.pallas_call(
        functools.partial(_hgrn_kernel, layer=layer, tile=tile),
        grid=(s // tile,),
        in_specs=[col(0), col(1), col(2), col(3),
                  _const_spec((depth, BRANCH_W)), _const_spec((1, BRANCH_W)),
                  _const_spec((BRANCH_W, BRANCH_W)), _const_spec((N_HEADS, BRANCH_W)),
                  _const_spec((c, c))],
        out_specs=pl.BlockSpec((tile, BRANCH_W), lambda i: (i, 0)),
        out_shape=jax.ShapeDtypeStruct((s, BRANCH_W), BF16),
        scratch_shapes=[pltpu.VMEM((BRANCH_W, BRANCH_W), F32),
                        pltpu.VMEM((c + sub, BRANCH_W), F32),
                        pltpu.VMEM((c + sub, BRANCH_W), F32),
                        pltpu.VMEM((c + sub, BRANCH_W), F32),
                        pltpu.VMEM((sub * c, BRANCH_W), BF16)],
        compiler_params=_params("arbitrary"),
        name="hgrn2",
    )(proj, proj, proj, proj, lb_logits.astype(F32), norm_g.reshape(1, BRANCH_W).astype(F32),
      _head_ones(), _head_masks(), tril)


FOX_TILE = 256
FOX_NSTAT = 16
FOX_SKIP_LOG = 40.0


def _fox_gate_kernel(h_ref, q_ref, k_ref, wf_ref, bias_ref, tril_ref, ones_ref, expand_ref,
                     ct_ref, stat_ref, carry_ref):
    @pl.when(pl.program_id(0) == 0)
    def _():
        carry_ref[...] = jnp.zeros_like(carry_ref)

    logit = _dot(h_ref[...], wf_ref[...]) + bias_ref[...]
    logf = jnp.minimum(logit, 0.0) - jnp.log(1.0 + jnp.exp(-jnp.abs(logit)))
    cum = _dot_exact_lhs(tril_ref[...], logf) + carry_ref[...]
    carry_ref[...] = cum[-1:, :]
    ct_ref[...] = cum.T[:SUBLANE, :]

    ones_bd = ones_ref[...]
    q = q_ref[...].astype(F32)
    k = k_ref[...].astype(F32)
    scale = HEAD_DIM ** -0.5
    c_heads = _dot_exact_rhs(cum, expand_ref[...])
    diag = _dot_exact_rhs(q * k, ones_bd) * scale
    qn = jnp.sqrt(_dot_exact_rhs(q * q, ones_bd)) * scale
    kn = jnp.sqrt(_dot_exact_rhs(k * k, ones_bd))
    rows = [jnp.max(qn, axis=0, keepdims=True),
            jnp.max(c_heads - diag, axis=0, keepdims=True),
            jnp.max(kn, axis=0, keepdims=True),
            c_heads[-1:, :]]
    stat_ref[0] = jnp.concatenate(rows + [jnp.zeros((SUBLANE - len(rows), BRANCH_W), F32)], axis=0)


def fox_gate(h, proj, w_f, f_bias):
    s, d = h.shape
    tile = min(FOX_TILE, s)
    bias = jnp.zeros((1, LANE), F32).at[0, :N_HEADS].set(f_bias.astype(F32))
    tril = jnp.tril(jnp.ones((tile, tile), F32)).astype(BF16)
    expand = (jnp.arange(LANE)[:, None] == _head_of(BRANCH_W)[None, :]).astype(BF16)
    ct, stats = pl.pallas_call(
        _fox_gate_kernel,
        grid=(s // tile,),
        in_specs=[pl.BlockSpec((tile, d), lambda i: (i, 0)),
                  pl.BlockSpec((tile, BRANCH_W), lambda i: (i, 4)),
                  pl.BlockSpec((tile, BRANCH_W), lambda i: (i, 5)),
                  _const_spec((d, LANE)), _const_spec((1, LANE)), _const_spec((tile, tile)),
                  _const_spec((BRANCH_W, BRANCH_W)), _const_spec((LANE, BRANCH_W))],
        out_specs=[pl.BlockSpec((SUBLANE, tile), lambda i: (0, i)),
                   pl.BlockSpec((1, SUBLANE, BRANCH_W), lambda i: (i, 0, 0))],
        out_shape=[jax.ShapeDtypeStruct((SUBLANE, s), F32),
                   jax.ShapeDtypeStruct((s // tile, SUBLANE, BRANCH_W), F32)],
        scratch_shapes=[pltpu.VMEM((1, LANE), F32)],
        compiler_params=_params("arbitrary"),
        name="fox_gate",
    )(h, proj, proj, w_f, bias, tril, _head_ones(), expand)
    return ct, stats[:, :4, ::HEAD_DIM].reshape(-1)


def _fox_kernel(stat_ref, q_ref, k_ref, v_ref, ct_ref, hm_ref, o_ref, m_sc, l_sc, acc_sc, out_sc, *, tq):
    i = pl.program_id(0)
    q0 = pl.multiple_of(i * tq, tq)
    hm = hm_ref[...]
    q = q_ref[...].astype(F32) * (HEAD_DIM ** -0.5)
    c_q0 = ct_ref[:, pl.ds(q0, tq)][:, 0:1]

    for h in range(N_HEADS):
        qh = (q * hm[h:h + 1, :]).astype(BF16)

        qmax = stat_ref[i * FOX_NSTAT + h]
        emax = stat_ref[i * FOX_NSTAT + N_HEADS + h]

        def scan(j, carry):
            kmax, first = carry
            kmax = jnp.maximum(kmax, stat_ref[j * FOX_NSTAT + 2 * N_HEADS + h])
            bound = qmax * kmax + emax - stat_ref[j * FOX_NSTAT + 3 * N_HEADS + h]
            return kmax, jnp.where(bound < -FOX_SKIP_LOG, j + 1, first)

        _, first = lax.fori_loop(0, i, scan, (jnp.float32(0.0), jnp.int32(0)))

        m_sc[...] = jnp.full_like(m_sc, -jnp.inf)
        l_sc[...] = jnp.zeros_like(l_sc)
        acc_sc[...] = jnp.zeros_like(acc_sc)

        def block(s0, diagonal):
            kb = k_ref[pl.ds(s0, tq), :]
            vb = v_ref[pl.ds(s0, tq), :]
            bias = c_q0 - ct_ref[:, pl.ds(s0, tq)]
            sc = _dot_nt(qh, kb) + bias[h:h + 1, :]
            if diagonal:
                r = lax.broadcasted_iota(jnp.int32, (tq, tq), 0)
                cidx = lax.broadcasted_iota(jnp.int32, (tq, tq), 1)
                sc = jnp.where(cidx <= r, sc, -jnp.inf)
            m_prev = m_sc[...]
            m_new = jnp.maximum(m_prev, jnp.max(sc, axis=1, keepdims=True))
            alpha = jnp.exp(m_prev - m_new)
            p = jnp.exp(sc - jnp.tile(m_new, (1, tq // LANE)))
            l_sc[...] = alpha * l_sc[...] + jnp.sum(p, axis=1, keepdims=True)
            acc_sc[...] = acc_sc[...] * jnp.tile(alpha, (1, BRANCH_W // LANE)) + _dot(p.astype(BF16), vb)
            m_sc[...] = m_new

        def body(j, carry):
            block(pl.multiple_of(j * tq, tq), False)
            return carry

        lax.fori_loop(first, i, body, 0)
        block(q0, True)

        res = acc_sc[...] * hm[h:h + 1, :] / jnp.tile(l_sc[...], (1, BRANCH_W // LANE))
        if h == 0:
            out_sc[...] = res
        else:
            out_sc[...] += res
    o_ref[...] = out_sc[...].astype(o_ref.dtype)


def fox_attention(proj, ct, stats):
    s = proj.shape[0]
    tq = min(FOX_TILE, s)
    full = lambda j: pl.BlockSpec((s, BRANCH_W), lambda i, j=j: (0, j), pipeline_mode=pl.Buffered(1))
    return pl.pallas_call(
        functools.partial(_fox_kernel, tq=tq),
        grid=(s // tq,),
        in_specs=[pl.BlockSpec(memory_space=pltpu.SMEM),
                  pl.BlockSpec((tq, BRANCH_W), lambda i: (i, 4)), full(5), full(6),
                  _const_spec((SUBLANE, s)), _const_spec((N_HEADS, BRANCH_W))],
        out_specs=pl.BlockSpec((tq, BRANCH_W), lambda i: (i, 0)),
        out_shape=jax.ShapeDtypeStruct((s, BRANCH_W), BF16),
        scratch_shapes=[pltpu.VMEM((tq, LANE), F32),
                        pltpu.VMEM((tq, LANE), F32),
                        pltpu.VMEM((tq, BRANCH_W), F32),
                        pltpu.VMEM((tq, BRANCH_W), F32)],
        compiler_params=_params("parallel"),
        name="fox_attention",
    )(stats, proj, proj, proj, ct, _head_masks())


def _pool_kernel(u_ref, w_ref, scale_ref, o_ref, ext, *, tile):
    i = pl.program_id(0)

    @pl.when(i == 0)
    def _():
        ext[pl.ds(0, POOL_HALO), :] = jnp.zeros((POOL_HALO, BRANCH_W), F32)

    u = u_ref[...].astype(F32)
    ext[pl.ds(POOL_HALO, tile), :] = u
    pos = (i * tile + lax.broadcasted_iota(jnp.int32, (tile, 1), 0) + 1).astype(F32)
    halves = []
    for half in range(BRANCH_W // LANE):
        lanes = pl.ds(half * LANE, LANE)
        w_small, w_big = POOL_WINDOWS[2 * half], POOL_WINDOWS[2 * half + 1]
        run = u[:, half * LANE:(half + 1) * LANE]
        sums = {}
        for j in range(1, w_big):
            if j == w_small:
                sums[w_small] = run
            run = run + ext[pl.ds(POOL_HALO - j, tile), lanes]
        sums[w_big] = run
        lane = lax.broadcasted_iota(jnp.int32, (1, LANE), 1)
        small = lane < POOL_GROUP
        total = jnp.where(small, sums[w_small], sums[w_big])
        count = jnp.where(small, jnp.minimum(pos, float(w_small)), jnp.minimum(pos, float(w_big)))
        halves.append(total / count)
    mean = jnp.concatenate(halves, axis=1)
    d = (mean - u).astype(BF16)
    y = _dot(d, w_ref[...]) * scale_ref[...]
    o_ref[...] = y.astype(o_ref.dtype)
    ext[pl.ds(0, POOL_HALO), :] = u[tile - POOL_HALO:, :]


def pool_mixer(proj, w_pool, scale, tile=512):
    s = proj.shape[0]
    tile = min(tile, s)
    ng = len(POOL_WINDOWS)
    w_bd = jnp.zeros((BRANCH_W, BRANCH_W), F32)
    for gi in range(ng):
        lo = gi * POOL_GROUP
        w_bd = w_bd.at[lo:lo + POOL_GROUP, lo:lo + POOL_GROUP].set(w_pool[gi].astype(F32))
    return pl.pallas_call(
        functools.partial(_pool_kernel, tile=tile),
        grid=(s // tile,),
        in_specs=[pl.BlockSpec((tile, BRANCH_W), lambda i: (i, 7)),
                  _const_spec((BRANCH_W, BRANCH_W)), _const_spec((1, BRANCH_W))],
        out_specs=pl.BlockSpec((tile, BRANCH_W), lambda i: (i, 0)),
        out_shape=jax.ShapeDtypeStruct((s, BRANCH_W), BF16),
        scratch_shapes=[pltpu.VMEM((tile + POOL_HALO, BRANCH_W), F32)],
        compiler_params=_params("arbitrary"),
        name="pool_mixer",
    )(proj, w_bd.astype(BF16), scale.reshape(1, BRANCH_W).astype(F32))


def _ret_kernel(q_ref, k_ref, v_ref, g_ref, cos_ref, sin_ref, perm_ref, ones_ref, hm_ref,
                dstack_ref, xi_ref, zeta_ref, gc_ref, gng_ref, gnb_ref, o_ref, st_ref, *, tile):
    c = RET_CHUNK

    @pl.when(pl.program_id(0) == 0)
    def _():
        st_ref[...] = jnp.zeros_like(st_ref)

    perm = perm_ref[...]
    ones_bd = ones_ref[...]
    bd_mask = ones_bd.astype(F32)
    hm = hm_ref[...]

    def chunk(ci, carry):
        r0 = pl.multiple_of(ci * c, c)
        qb = q_ref[pl.ds(r0, c), :]
        kb = k_ref[pl.ds(r0, c), :]
        v = v_ref[pl.ds(r0, c), :]
        g = g_ref[pl.ds(r0, c), :].astype(F32)
        cos = cos_ref[pl.ds(r0, c), :]
        sin = sin_ref[pl.ds(r0, c), :]
        qr = qb.astype(F32) * cos + _dot(qb, perm) * sin
        kr = (kb.astype(F32) * cos + _dot(kb, perm) * sin) * (HEAD_DIM ** -0.5)

        qx = jnp.concatenate([qr * hm[h:h + 1, :] for h in range(N_HEADS)], axis=0).astype(BF16)
        sc = _dot_nt(qx, kr.astype(BF16)) * dstack_ref[...]
        r = _dot(sc.astype(BF16), v)
        intra = jnp.zeros((c, BRANCH_W), F32)
        for h in range(N_HEADS):
            intra = intra + r[h * c:(h + 1) * c, :] * hm[h:h + 1, :]

        st = st_ref[...]
        inter = _dot_nt((qr * xi_ref[...]).astype(BF16), st.astype(BF16))
        upd = _dot(v.astype(F32).T.astype(BF16), (kr * zeta_ref[...]).astype(BF16))
        st_ref[...] = st * gc_ref[...] + upd * bd_mask

        o = intra + inter
        mu = _dot_exact_rhs(o, ones_bd) * (1.0 / HEAD_DIM)
        cen = o - mu
        var = _dot_exact_rhs(cen * cen, ones_bd) * (1.0 / HEAD_DIM)
        y = cen * lax.rsqrt(var + LN_EPS) * gng_ref[...] + gnb_ref[...]
        o_ref[pl.ds(r0, c), :] = (y * _silu(g)).astype(o_ref.dtype)
        return carry

    lax.fori_loop(0, tile // c, chunk, 0)


def _rope_tables(s):
    half = HEAD_DIM // 2
    pos = jnp.arange(s, dtype=F32)
    inv_freq = ROPE_BASE ** (-jnp.arange(half, dtype=F32) / half)
    ang = pos[:, None] * inv_freq[None, :]
    cos, sin = jnp.cos(ang), jnp.sin(ang)
    cos_t = jnp.tile(jnp.concatenate([cos, cos], axis=1), (1, N_HEADS))
    sin_t = jnp.tile(jnp.concatenate([-sin, sin], axis=1), (1, N_HEADS))
    return cos_t, sin_t


def _ret_constants():
    c = RET_CHUNK
    half = HEAD_DIM // 2
    lane = jnp.arange(BRANCH_W)
    partner = jnp.where(lane % HEAD_DIM < half, lane + half, lane - half)
    perm = (lane[:, None] == partner[None, :]).astype(BF16)
    log_gamma = jnp.log1p(-jnp.exp2(-RET_DECAY_BASE - jnp.arange(N_HEADS, dtype=F32)))
    ci = jnp.arange(c, dtype=F32)
    diff = ci[:, None] - ci[None, :]
    intra = jnp.where(diff >= 0, jnp.exp(diff * log_gamma[:, None, None]), 0.0)
    dstack = intra.reshape(N_HEADS * c, c)
    lg_lane = jnp.repeat(log_gamma, HEAD_DIM)[None, :]
    xi = jnp.exp((ci[:, None] + 1.0) * lg_lane)
    zeta = jnp.exp((c - 1.0 - ci[:, None]) * lg_lane)
    gc = jnp.exp(c * lg_lane)
    return perm, dstack, xi, zeta, gc


def retention(proj, gn_g, gn_b, tile=512):
    s = proj.shape[0]
    c = RET_CHUNK
    tile = min(tile, s)
    cos_t, sin_t = _rope_tables(s)
    perm, dstack, xi, zeta, gc = _ret_constants()
    col = lambda j: pl.BlockSpec((tile, BRANCH_W), lambda i, j=j: (i, j))
    row = pl.BlockSpec((tile, BRANCH_W), lambda i: (i, 0))
    return pl.pallas_call(
        functools.partial(_ret_kernel, tile=tile),
        grid=(s // tile,),
        in_specs=[col(8), col(9), col(10), col(11), row, row,
                  _const_spec((BRANCH_W, BRANCH_W)), _const_spec((BRANCH_W, BRANCH_W)),
                  _const_spec((N_HEADS, BRANCH_W)), _const_spec((N_HEADS * c, c)),
                  _const_spec((c, BRANCH_W)), _const_spec((c, BRANCH_W)), _const_spec((1, BRANCH_W)),
                  _const_spec((1, BRANCH_W)), _const_spec((1, BRANCH_W))],
        out_specs=row,
        out_shape=jax.ShapeDtypeStruct((s, BRANCH_W), BF16),
        scratch_shapes=[pltpu.VMEM((BRANCH_W, BRANCH_W), F32)],
        compiler_params=_params("arbitrary"),
        name="retention",
    )(proj, proj, proj, proj, cos_t, sin_t, perm, _head_ones(), _head_masks(), dstack, xi, zeta, gc,
      gn_g.reshape(1, BRANCH_W).astype(F32), gn_b.reshape(1, BRANCH_W).astype(F32))


def _merge_kernel(h_ref, o0_ref, o1_ref, o2_ref, o3_ref, x_ref, wg_ref, wb_ref, wo_ref, g_ref,
                  xo_ref, ho_ref):
    h = h_ref[...]
    merged = jnp.zeros(x_ref.shape, F32)
    for bi, o_ref in enumerate((o0_ref, o1_ref, o2_ref, o3_ref)):
        gate = _sigmoid(_dot(h, wg_ref[:, bi * D_MODEL:(bi + 1) * D_MODEL]))
        merged = merged + gate * _dot(o_ref[...], wb_ref[bi])
    x_new = x_ref[...] + _dot(merged.astype(BF16), wo_ref[...])
    xo_ref[...] = x_new
    ho_ref[...] = _rms(x_new, g_ref[...]).astype(ho_ref.dtype)


def merge(h, branches, x, w_gate, w_branch, w_out, next_gain, tm=512):
    s, d = x.shape
    tm = min(tm, s)
    row = lambda w: pl.BlockSpec((tm, w), lambda i: (i, 0))
    return pl.pallas_call(
        _merge_kernel,
        grid=(s // tm,),
        in_specs=[row(d), row(BRANCH_W), row(BRANCH_W), row(BRANCH_W), row(BRANCH_W), row(d),
                  _const_spec((d, N_BRANCH * d)), _const_spec((N_BRANCH, BRANCH_W, d)),
                  _const_spec((d, d)), _const_spec((1, d))],
        out_specs=[row(d), row(d)],
        out_shape=[jax.ShapeDtypeStruct((s, d), F32), jax.ShapeDtypeStruct((s, d), BF16)],
        compiler_params=_params("parallel"),
        name="merge",
    )(h, *branches, x, w_gate, w_branch, w_out, next_gain.reshape(1, d).astype(F32))


def _ffn_kernel(h_ref, x_ref, wg_ref, wu_ref, wd_ref, g_ref, xo_ref, ho_ref, acc_ref):
    f = pl.program_id(1)

    @pl.when(f == 0)
    def _():
        acc_ref[...] = jnp.zeros_like(acc_ref)

    h = h_ref[...]
    a = _silu(_dot(h, wg_ref[...])) * _dot(h, wu_ref[...])
    acc_ref[...] += _dot(a.astype(BF16), wd_ref[...])

    @pl.when(f == pl.num_programs(1) - 1)
    def _():
        x_new = x_ref[...] + acc_ref[...]
        xo_ref[...] = x_new
        ho_ref[...] = _rms(x_new, g_ref[...]).astype(ho_ref.dtype)


def ffn_dense(h, x, w_gate, w_up, w_down, next_gain, tm=1024, tf=512):
    s, d = x.shape
    tm = min(tm, s)
    dff = w_gate.shape[1]
    row = lambda: pl.BlockSpec((tm, d), lambda i, f: (i, 0))
    return pl.pallas_call(
        _ffn_kernel,
        grid=(s // tm, dff // tf),
        in_specs=[row(), row(),
                  pl.BlockSpec((d, tf), lambda i, f: (0, f)),
                  pl.BlockSpec((d, tf), lambda i, f: (0, f)),
                  pl.BlockSpec((tf, d), lambda i, f: (f, 0)),
                  _const_spec((1, d))],
        out_specs=[row(), row()],
        out_shape=[jax.ShapeDtypeStruct((s, d), F32), jax.ShapeDtypeStruct((s, d), BF16)],
        scratch_shapes=[pltpu.VMEM((tm, d), F32)],
        compiler_params=_params("parallel", "arbitrary"),
        name="ffn_dense",
    )(h, x, w_gate, w_up, w_down, next_gain.reshape(1, d).astype(F32))


MOE_TOK_TILE = 256
MOE_ROW_BLOCK = 512
MOE_Y_BLOCK = 256


def _dot_f32(x, w):
    xh, xm, xl = _split3(x)
    wh, wm, wl = _split3(w)
    return (_dot(xh, wh) + (_dot(xh, wm) + _dot(xm, wh))
            + (_dot(xh, wl) + _dot(xm, wm) + _dot(xl, wh)))


def _route_kernel(x_ref, ng_ref, router_ref, ltri_ref, pos_ref, gate_ref, post_ref, before_ref, total_ref,
                  carry_ref):
    tm = x_ref.shape[0]

    @pl.when(pl.program_id(0) == 0)
    def _():
        carry_ref[...] = jnp.zeros_like(carry_ref)

    hn = _rms(x_ref[...], ng_ref[...])
    logits = _dot_f32(hn, router_ref[...])
    lane = lax.broadcasted_iota(jnp.int32, (tm, LANE), 1)
    logits = jnp.where(lane < N_EXPERTS, logits, -jnp.inf)
    v1 = jnp.max(logits, axis=1, keepdims=True)
    i1 = jnp.min(jnp.where(logits == v1, lane, LANE), axis=1, keepdims=True)
    rest = jnp.where(lane == i1, -jnp.inf, logits)
    v2 = jnp.max(rest, axis=1, keepdims=True)
    i2 = jnp.min(jnp.where(rest == v2, lane, LANE), axis=1, keepdims=True)
    w1 = 1.0 / (1.0 + jnp.exp(v2 - v1))
    gate_ref[...] = jnp.where(lane == i1, w1, 0.0) + jnp.where(lane == i2, 1.0 - w1, 0.0)

    member = jnp.where((lane == i1) | (lane == i2), 1.0, 0.0)
    carry = carry_ref[...]
    rank = _dot(ltri_ref[...], member.astype(BF16)) + carry
    pos = jnp.where(member > 0.0, rank, -1.0)
    pos_ref[...] = pos
    post_ref[...] = pos.T[:SUBLANE, :]
    before_ref[0] = carry
    carry = carry + jnp.sum(member, axis=0, keepdims=True)
    carry_ref[...] = carry
    total_ref[...] = carry


def moe_route(x, norm_gain, router):
    s, d = x.shape
    tm = min(MOE_TOK_TILE, s)
    nt = s // tm
    router_p = jnp.zeros((d, LANE), F32).at[:, :N_EXPERTS].set(router.astype(F32))
    ltri = jnp.tril(jnp.ones((tm, tm), F32), -1).astype(BF16)
    row = pl.BlockSpec((tm, LANE), lambda i: (i, 0))
    return pl.pallas_call(
        _route_kernel,
        grid=(nt,),
        in_specs=[pl.BlockSpec((tm, d), lambda i: (i, 0)), _const_spec((1, d)), _const_spec((d, LANE)),
                  _const_spec((tm, tm))],
        out_specs=[row, row, pl.BlockSpec((SUBLANE, tm), lambda i: (0, i)),
                   pl.BlockSpec((1, 1, LANE), lambda i: (i, 0, 0)), pl.BlockSpec((1, LANE), lambda i: (0, 0))],
        out_shape=[jax.ShapeDtypeStruct((s, LANE), F32), jax.ShapeDtypeStruct((s, LANE), F32),
                   jax.ShapeDtypeStruct((SUBLANE, s), F32), jax.ShapeDtypeStruct((nt, 1, LANE), F32),
                   jax.ShapeDtypeStruct((1, LANE), F32)],
        scratch_shapes=[pltpu.VMEM((1, LANE), F32)],
        compiler_params=_params("arbitrary"),
        name="moe_route",
    )(x, norm_gain.reshape(1, d).astype(F32), router_p, ltri)


def _moe_ffn_kernel(be_ref, r0_ref, tlo_ref, thi_ref, nv_ref, h_ref, post_ref, wg_ref, wu_ref, wd_ref,
                    y_ref, x_sc, acc_sc, *, tt):
    b = pl.program_id(0)
    f = pl.program_id(1)
    valid = b < nv_ref[0]
    tmb = x_sc.shape[0]

    @pl.when(valid & (f == 0))
    def _():
        e = be_ref[b]
        want = lax.broadcasted_iota(jnp.int32, (tmb, 1), 0).astype(F32) + r0_ref[b].astype(F32)
        acc_sc[...] = jnp.zeros_like(acc_sc)

        def gather(t, carry):
            t0 = pl.multiple_of(t * tt, tt)
            p = post_ref[pl.ds(e, 1), pl.ds(t0, tt)]
            sel = jnp.where(p == want, 1.0, 0.0).astype(BF16)
            acc_sc[...] += _dot(sel, h_ref[pl.ds(t0, tt), :])
            return carry

        lax.fori_loop(tlo_ref[b], thi_ref[b] + 1, gather, 0)
        x_sc[...] = acc_sc[...].astype(BF16)
        acc_sc[...] = jnp.zeros_like(acc_sc)

    @pl.when(valid)
    def _():
        xb = x_sc[...]
        a = _silu(_dot(xb, wg_ref[0])) * _dot(xb, wu_ref[0])
        acc_sc[...] += _dot(a.astype(BF16), wd_ref[0])

    @pl.when(f == pl.num_programs(1) - 1)
    def _():
        y_ref[...] = jnp.where(valid, acc_sc[...], 0.0).astype(y_ref.dtype)


def moe_ffn(h, post, sched, w_gate, w_up, w_down, tf=512):
    s, d = h.shape
    ne, _, dff = w_gate.shape
    tmb = MOE_ROW_BLOCK
    tt = min(MOE_TOK_TILE, s)
    nb = sched[0].shape[0]
    nf = dff // tf

    def wspec(shape, fn):
        return pl.BlockSpec(shape, fn)

    def fidx(b, f, nv):
        return jnp.where(b < nv[0], f, nf - 1)

    grid_spec = pltpu.PrefetchScalarGridSpec(
        num_scalar_prefetch=5,
        grid=(nb, nf),
        in_specs=[pl.BlockSpec((s, d), lambda b, f, *_: (0, 0), pipeline_mode=pl.Buffered(1)),
                  pl.BlockSpec((SUBLANE, s), lambda b, f, *_: (0, 0), pipeline_mode=pl.Buffered(1)),
                  wspec((1, d, tf), lambda b, f, be, r0, tlo, thi, nv: (be[b], 0, fidx(b, f, nv))),
                  wspec((1, d, tf), lambda b, f, be, r0, tlo, thi, nv: (be[b], 0, fidx(b, f, nv))),
                  wspec((1, tf, d), lambda b, f, be, r0, tlo, thi, nv: (be[b], fidx(b, f, nv), 0))],
        out_specs=pl.BlockSpec((tmb, d), lambda b, f, *_: (b, 0)),
        scratch_shapes=[pltpu.VMEM((tmb, d), BF16), pltpu.VMEM((tmb, d), F32)],
    )
    return pl.pallas_call(
        functools.partial(_moe_ffn_kernel, tt=tt),
        grid_spec=grid_spec,
        out_shape=jax.ShapeDtypeStruct((nb * tmb, d), BF16),
        compiler_params=_params("arbitrary", "arbitrary"),
        name="moe_ffn",
    )(*sched, h, post, w_gate, w_up, w_down)


def _moe_combine_kernel(kb_ref, off_ref, x_ref, pos_ref, gate_ref, fg_ref, *rest):
    y_refs, o_ref = rest[:-1], rest[-1]
    t = pl.program_id(0)
    tm = x_ref.shape[0]
    yb = y_refs[0].shape[0]
    pos = pos_ref[...]
    gate = gate_ref[...]
    lane = lax.broadcasted_iota(jnp.int32, (tm, LANE), 1)
    col = lax.broadcasted_iota(jnp.int32, (1, yb), 1).astype(F32)
    acc = x_ref[...]
    for e in range(N_EXPERTS):
        pe = jnp.sum(jnp.where(lane == e, pos, 0.0), axis=1, keepdims=True)
        ge = jnp.sum(jnp.where(lane == e, gate, 0.0), axis=1, keepdims=True)
        r = jnp.where(pe >= 0.0, pe + off_ref[t * N_EXPERTS + e].astype(F32), -1.0)
        z = (_dot(jnp.where(r == col, 1.0, 0.0).astype(BF16), y_refs[2 * e][...])
             + _dot(jnp.where(r == col + float(yb), 1.0, 0.0).astype(BF16), y_refs[2 * e + 1][...]))
        acc = acc + ge * z
    o_ref[...] = _rms(acc, fg_ref[...]).astype(o_ref.dtype)


def moe_combine(x, pos, gate, y, kb, off, final_gain):
    s, d = x.shape
    tm = min(MOE_TOK_TILE, s)
    yb = MOE_Y_BLOCK
    last = y.shape[0] // yb - 1

    def yspec(e, k):
        return pl.BlockSpec((yb, d), lambda t, kb_r, off_r: (jnp.minimum(kb_r[t * N_EXPERTS + e] + k, last), 0))

    grid_spec = pltpu.PrefetchScalarGridSpec(
        num_scalar_prefetch=2,
        grid=(s // tm,),
        in_specs=[pl.BlockSpec((tm, d), lambda t, *_: (t, 0)),
                  pl.BlockSpec((tm, LANE), lambda t, *_: (t, 0)),
                  pl.BlockSpec((tm, LANE), lambda t, *_: (t, 0)),
                  pl.BlockSpec((1, d), lambda t, *_: (0, 0))]
        + [yspec(e, k) for e in range(N_EXPERTS) for k in range(2)],
        out_specs=pl.BlockSpec((tm, d), lambda t, *_: (t, 0)),
    )
    return pl.pallas_call(
        _moe_combine_kernel,
        grid_spec=grid_spec,
        out_shape=jax.ShapeDtypeStruct((s, d), F32),
        compiler_params=_params("arbitrary"),
        name="moe_combine",
    )(kb, off, x, pos, gate, final_gain.reshape(1, d).astype(F32), *([y] * (2 * N_EXPERTS)))


def _moe_schedule(before, total, s):
    tmb, yb = MOE_ROW_BLOCK, MOE_Y_BLOCK
    nb = 2 * s // tmb + N_EXPERTS
    counts = total[0, :N_EXPERTS].astype(jnp.int32)
    nblk = (counts + tmb - 1) // tmb
    end = jnp.cumsum(nblk)
    first = end - nblk
    nvalid = end[-1]
    b = jnp.minimum(jnp.arange(nb, dtype=jnp.int32), nvalid - 1)
    blk_e = jnp.sum(b[:, None] >= end[None, :], axis=1).astype(jnp.int32)
    r0 = (b - first[blk_e]) * tmb
    cb = before[:, 0, :N_EXPERTS].astype(jnp.int32)
    cbe = cb[:, blk_e]
    tlo = (jnp.sum(cbe <= r0[None, :], axis=0) - 1).astype(jnp.int32)
    thi = (jnp.sum(cbe < (r0 + tmb)[None, :], axis=0) - 1).astype(jnp.int32)
    sched = (blk_e, r0.astype(jnp.int32), tlo, thi, nvalid.reshape(1).astype(jnp.int32))
    row_start = first[None, :] * tmb + cb
    kb = row_start // yb
    off = first[None, :] * tmb - kb * yb
    return sched, kb.reshape(-1).astype(jnp.int32), off.reshape(-1).astype(jnp.int32)


def moe_sparse(h, x, norm_gain, router, w_gate, w_up, w_down, final_gain):
    s, _ = x.shape
    pos, gate, post, before, total = moe_route(x, norm_gain, router)
    sched, kb, off = _moe_schedule(before, total, s)
    y = moe_ffn(h, post, sched, w_gate, w_up, w_down)
    return moe_combine(x, pos, gate, y, kb, off, final_gain)


FOX_F0 = 7 * BRANCH_W


def _split_w_in_kernel(w_ref, mix_ref, f_ref, gate_ref):
    w = w_ref[...]
    rows = w.shape[0]
    mix_ref[:, :FOX_F0] = w[:, :FOX_F0].astype(BF16)
    mix_ref[:, FOX_F0:] = w[:, FOX_F0 + N_HEADS:N_MIX_COLS + N_HEADS].astype(BF16)
    f_ref[...] = jnp.concatenate([w[:, FOX_F0:FOX_F0 + N_HEADS], jnp.zeros((rows, LANE - N_HEADS), F32)],
                                 axis=1).astype(BF16)
    gate_ref[...] = w[:, N_MIX_COLS + N_HEADS:].astype(BF16)


def _mixer_weights(w_in, layer, tr=128):
    _, d, cols = w_in.shape
    row = lambda w: pl.BlockSpec((tr, w), lambda i: (i, 0))
    return pl.pallas_call(
        _split_w_in_kernel,
        grid=(d // tr,),
        in_specs=[pl.BlockSpec((None, tr, cols), lambda i: (layer, i, 0))],
        out_specs=[row(N_MIX_COLS), row(LANE), row(N_BRANCH * D_MODEL)],
        out_shape=[jax.ShapeDtypeStruct((d, N_MIX_COLS), BF16), jax.ShapeDtypeStruct((d, LANE), BF16),
                   jax.ShapeDtypeStruct((d, N_BRANCH * D_MODEL), BF16)],
        compiler_params=_params("parallel"),
        name="split_w_in",
    )(w_in)


def kernel(x, w_in, w_branch, w_out, norm_mix_g, hgrn_lb_logits, hgrn_norm_g, fox_f_bias, pool_w, pool_scale,
           ret_gn_g, ret_gn_b, norm_ffn_g, ffn_w_gate, ffn_w_up, ffn_w_down, moe_router, moe_w_gate, moe_w_up,
           moe_w_down, final_norm_g):
    b, s, d = x.shape
    assert b == 1 and d == D_MODEL
    depth = w_in.shape[0]
    assert depth == 2, "layer 0 uses the dense FFN, layer 1 the experts and the final norm"
    xs = x.reshape(s, d)
    h = rmsnorm_bf16(xs, norm_mix_g[0])
    out = None
    for layer in range(depth):
        w_mix, w_f, w_gate = _mixer_weights(w_in, layer)
        proj = matmul(h, w_mix)
        ct, fox_stats = fox_gate(h, proj, w_f, fox_f_bias[layer])
        branches = (
            hgrn2(proj, hgrn_lb_logits, hgrn_norm_g[layer], layer),
            fox_attention(proj, ct, fox_stats),
            pool_mixer(proj, pool_w[layer], pool_scale[layer]),
            retention(proj, ret_gn_g[layer], ret_gn_b[layer]),
        )
        xs, h2 = merge(h, branches, xs, w_gate, w_branch[layer].astype(BF16), w_out[layer].astype(BF16),
                       norm_ffn_g[layer])
        if layer % 2 == 0:
            li = layer // 2
            xs, h = ffn_dense(h2, xs, ffn_w_gate[li].astype(BF16), ffn_w_up[li].astype(BF16),
                              ffn_w_down[li].astype(BF16), norm_mix_g[layer + 1])
        else:
            li = layer // 2
            out = moe_sparse(h2, xs, norm_ffn_g[layer], moe_router[li], moe_w_gate[li].astype(BF16),
                             moe_w_up[li].astype(BF16), moe_w_down[li].astype(BF16), final_norm_g)
    return out.reshape(b, s, d)
```

```python
import functools
import math

import jax
import jax.numpy as jnp
from jax import lax
from jax.experimental import pallas as pl
from jax.experimental.pallas import tpu as pltpu

D_MODEL = 1024
N_BRANCH = 4
BRANCH_W = D_MODEL // N_BRANCH
HEAD_DIM = 64
N_HEADS = BRANCH_W // HEAD_DIM
POOL_WINDOWS = (2, 4, 8, 16)
POOL_GROUP = BRANCH_W // len(POOL_WINDOWS)
POOL_HALO = 16
RET_DECAY_BASE = 5.0
ROPE_BASE = 10000.0
D_FF = 7 * D_MODEL // 2
N_EXPERTS = 8
RMS_EPS = 1e-6
LN_EPS = 1e-5
N_MIX_COLS = 12 * BRANCH_W

LANE = 128
SUBLANE = 8
VMEM_LIMIT = 56 * 1024 * 1024

HG_CHUNK = 64
HG_SUB = 16
RET_CHUNK = 128

F32 = jnp.float32
BF16 = jnp.bfloat16
NT_DIMS = (((1,), (1,)), ((), ()))


def _params(*sem):
    return pltpu.CompilerParams(dimension_semantics=sem, vmem_limit_bytes=VMEM_LIMIT)


def _const_spec(shape):
    nd = len(shape)
    return pl.BlockSpec(shape, lambda *_: (0,) * nd, pipeline_mode=pl.Buffered(1))


def _split3(x):
    hi = x.astype(BF16)
    r1 = x - hi.astype(F32)
    mid = r1.astype(BF16)
    lo = (r1 - mid.astype(F32)).astype(BF16)
    return hi, mid, lo


def _dot(a, b):
    return jnp.dot(a, b, preferred_element_type=F32)


def _dot_nt(a, b):
    return lax.dot_general(a, b, NT_DIMS, preferred_element_type=F32)


def _dot_exact_rhs(x, m_bf16):
    hi, mid, lo = _split3(x)
    return _dot(hi, m_bf16) + _dot(mid, m_bf16) + _dot(lo, m_bf16)


def _dot_exact_lhs(m_bf16, x):
    hi, mid, lo = _split3(x)
    return _dot(m_bf16, hi) + _dot(m_bf16, mid) + _dot(m_bf16, lo)


def _sigmoid(x):
    return 1.0 / (1.0 + jnp.exp(-x))


def _silu(x):
    return x * _sigmoid(x)


def _rms(x, gain):
    return x * lax.rsqrt(jnp.mean(x * x, axis=-1, keepdims=True) + RMS_EPS) * gain


def _head_of(n):
    return jnp.arange(n) // HEAD_DIM


def _head_ones():
    h = _head_of(BRANCH_W)
    return (h[:, None] == h[None, :]).astype(BF16)


def _head_masks():
    return (_head_of(BRANCH_W)[None, :] == jnp.arange(N_HEADS)[:, None]).astype(F32)


def _rmsnorm_kernel(x_ref, g_ref, o_ref):
    o_ref[...] = _rms(x_ref[...], g_ref[...]).astype(o_ref.dtype)


def rmsnorm_bf16(x, gain, tm=1024):
    s, d = x.shape
    tm = min(tm, s)
    return pl.pallas_call(
        _rmsnorm_kernel,
        grid=(s // tm,),
        in_specs=[pl.BlockSpec((tm, d), lambda i: (i, 0)), _const_spec((1, d))],
        out_specs=pl.BlockSpec((tm, d), lambda i: (i, 0)),
        out_shape=jax.ShapeDtypeStruct((s, d), BF16),
        compiler_params=_params("parallel"),
        name="rmsnorm",
    )(x, gain.reshape(1, d))


def _matmul_kernel(a_ref, b_ref, o_ref):
    o_ref[...] = _dot(a_ref[...], b_ref[...]).astype(o_ref.dtype)


def matmul(a, b, out_dtype=BF16, tm=2048, tn=1024):
    m, k = a.shape
    _, n = b.shape
    tm = min(tm, m)
    return pl.pallas_call(
        _matmul_kernel,
        grid=(n // tn, m // tm),
        in_specs=[pl.BlockSpec((tm, k), lambda j, i: (i, 0)),
                  pl.BlockSpec((k, tn), lambda j, i: (0, j))],
        out_specs=pl.BlockSpec((tm, tn), lambda j, i: (i, j)),
        out_shape=jax.ShapeDtypeStruct((m, n), out_dtype),
        compiler_params=_params("parallel", "parallel"),
        name="in_proj",
    )(a, b)


def _hgrn_kernel(q_ref, f_ref, i_ref, g_ref, lbl_ref, ng_ref, ones_ref, hm_ref, tril_ref,
                 o_ref, st_ref, bpad, kpad, vpad, astack, *, layer, tile):
    c, sub = HG_CHUNK, HG_SUB
    nsub = c // sub

    @pl.when(pl.program_id(0) == 0)
    def _():
        st_ref[...] = jnp.zeros_like(st_ref)
        bpad[...] = jnp.zeros_like(bpad)
        kpad[...] = jnp.zeros_like(kpad)
        vpad[...] = jnp.zeros_like(vpad)

    lbl = lbl_ref[...]
    e = jnp.exp(lbl - jnp.max(lbl, axis=0, keepdims=True))
    p = e / jnp.sum(e, axis=0, keepdims=True)
    lb = jnp.zeros((1, BRANCH_W), F32)
    for l in range(1, layer + 1):
        lb = lb + p[l:l + 1, :]

    ones_bd = ones_ref[...]
    hm = hm_ref[...]
    tril = tril_ref[...]
    row = lax.broadcasted_iota(jnp.int32, (c, 1), 0)
    row_in_sub = row % sub
    bd_mask = ones_bd.astype(F32)

    def chunk(ci, carry):
        r0 = pl.multiple_of(ci * c, c)
        q = q_ref[pl.ds(r0, c), :].astype(F32)
        fl = f_ref[pl.ds(r0, c), :].astype(F32)
        v = i_ref[pl.ds(r0, c), :].astype(F32)
        g = g_ref[pl.ds(r0, c), :].astype(F32)

        sig = _sigmoid(fl)
        logf = jnp.log(lb + (1.0 - lb) * sig)
        kk = (1.0 - lb) * (1.0 - sig)
        b = _dot_exact_lhs(tril, logf)

        bpad[pl.ds(sub, c), :] = b
        kpad[pl.ds(sub, c), :] = kk
        vpad[pl.ds(sub, c), :] = v

        for d in range(sub):
            b_d = bpad[pl.ds(sub - d, c), :]
            k_d = kpad[pl.ds(sub - d, c), :]
            a = jnp.where(row_in_sub >= d, q * k_d * jnp.exp(b - b_d), 0.0)
            astack[pl.ds(d * c, c), :] = a.astype(BF16)
        pall = _dot(astack[...], ones_bd)
        intra = jnp.zeros((c, BRANCH_W), F32)
        for d in range(sub):
            intra = intra + pall[d * c:(d + 1) * c, :] * vpad[pl.ds(sub - d, c), :]

        pieces = [jnp.zeros((sub, BRANCH_W), F32)]
        for si in range(1, nsub):
            lo = si * sub
            m_i = b[lo - 1:lo, :]
            qs = q[lo:lo + sub, :] * jnp.exp(b[lo:lo + sub, :] - m_i)
            ks = (kk[:lo, :] * jnp.exp(m_i - b[:lo, :])).astype(BF16)
            qx = jnp.concatenate([qs * hm[h:h + 1, :] for h in range(N_HEADS)], axis=0).astype(BF16)
            sc = _dot_nt(qx, ks)
            r = _dot(sc.astype(BF16), v[:lo, :].astype(BF16))
            acc = jnp.zeros((sub, BRANCH_W), F32)
            for h in range(N_HEADS):
                acc = acc + r[h * sub:(h + 1) * sub, :] * hm[h:h + 1, :]
            pieces.append(acc)
        intra = intra + jnp.concatenate(pieces, axis=0)

        st = st_ref[...]
        inter = _dot_nt((q * jnp.exp(b)).astype(BF16), st.astype(BF16))
        b_last = b[c - 1:c, :]
        ks_end = (kk * jnp.exp(b_last - b)).astype(BF16)
        upd = _dot(v.T.astype(BF16), ks_end)
        st_ref[...] = st * jnp.exp(b_last) + upd * bd_mask

        o = intra + inter
        ms = _dot_exact_rhs(o * o, ones_bd) * (1.0 / HEAD_DIM)
        y = o * lax.rsqrt(ms + RMS_EPS) * ng_ref[...] * _silu(g)
        o_ref[pl.ds(r0, c), :] = y.astype(o_ref.dtype)
        return carry

    lax.fori_loop(0, tile // c, chunk, 0)


def hgrn2(proj, lb_logits, norm_g, layer, tile=512):
    s = proj.shape[0]
    depth = lb_logits.shape[0]
    c, sub = HG_CHUNK, HG_SUB
    tile = min(tile, s)
    col = lambda j: pl.BlockSpec((tile, BRANCH_W), lambda i, j=j: (i, j))
    tril = jnp.tril(jnp.ones((c, c), F32)).astype(BF16)
    return pl.pallas_call(
        functools.partial(_hgrn_kernel, layer=layer, tile=tile),
        grid=(s // tile,),
        in_specs=[col(0), col(1), col(2), col(3),
                  _const_spec((depth, BRANCH_W)), _const_spec((1, BRANCH_W)),
                  _const_spec((BRANCH_W, BRANCH_W)), _const_spec((N_HEADS, BRANCH_W)),
                  _const_spec((c, c))],
        out_specs=pl.BlockSpec((tile, BRANCH_W), lambda i: (i, 0)),
        out_shape=jax.ShapeDtypeStruct((s, BRANCH_W), BF16),
        scratch_shapes=[pltpu.VMEM((BRANCH_W, BRANCH_W), F32),
                        pltpu.VMEM((c + sub, BRANCH_W), F32),
                        pltpu.VMEM((c + sub, BRANCH_W), F32),
                        pltpu.VMEM((c + sub, BRANCH_W), F32),
                        pltpu.VMEM((sub * c, BRANCH_W), BF16)],
        compiler_params=_params("arbitrary"),
        name="hgrn2",
    )(proj, proj, proj, proj, lb_logits.astype(F32), norm_g.reshape(1, BRANCH_W).astype(F32),
      _head_ones(), _head_masks(), tril)


FOX_TILE = 256
FOX_NSTAT = 16
FOX_SKIP_LOG = 40.0


def _fox_gate_kernel(h_ref, q_ref, k_ref, wf_ref, bias_ref, tril_ref, ones_ref, expand_ref,
                     ct_ref, stat_ref, carry_ref, kmax_ref):
    @pl.when(pl.program_id(0) == 0)
    def _():
        carry_ref[...] = jnp.zeros_like(carry_ref)
        kmax_ref[...] = jnp.zeros_like(kmax_ref)

    logit = _dot(h_ref[...], wf_ref[...]) + bias_ref[...]
    logf = jnp.minimum(logit, 0.0) - jnp.log(1.0 + jnp.exp(-jnp.abs(logit)))
    cum = _dot_exact_lhs(tril_ref[...], logf) + carry_ref[...]
    carry_ref[...] = cum[-1:, :]
    ct_ref[...] = cum.T[:SUBLANE, :]

    ones_bd = ones_ref[...]
    q = q_ref[...].astype(F32)
    k = k_ref[...].astype(F32)
    scale = HEAD_DIM ** -0.5
    c_heads = _dot_exact_rhs(cum, expand_ref[...])
    diag = _dot_exact_rhs(q * k, ones_bd) * scale
    qn = jnp.sqrt(_dot_exact_rhs(q * q, ones_bd)) * scale
    kn = jnp.sqrt(_dot_exact_rhs(k * k, ones_bd))
    kmax = jnp.maximum(kmax_ref[...], jnp.max(kn, axis=0, keepdims=True))
    kmax_ref[...] = kmax
    rows = [jnp.max(qn, axis=0, keepdims=True),
            jnp.max(c_heads - diag, axis=0, keepdims=True),
            kmax,
            c_heads[-1:, :]]
    stat_ref[0] = jnp.concatenate(rows + [jnp.zeros((SUBLANE - len(rows), BRANCH_W), F32)], axis=0)


def fox_gate(h, proj, w_f, f_bias):
    s, d = h.shape
    tile = min(FOX_TILE, s)
    bias = jnp.zeros((1, LANE), F32).at[0, :N_HEADS].set(f_bias.astype(F32))
    tril = jnp.tril(jnp.ones((tile, tile), F32)).astype(BF16)
    expand = (jnp.arange(LANE)[:, None] == _head_of(BRANCH_W)[None, :]).astype(BF16)
    ct, stats = pl.pallas_call(
        _fox_gate_kernel,
        grid=(s // tile,),
        in_specs=[pl.BlockSpec((tile, d), lambda i: (i, 0)),
                  pl.BlockSpec((tile, BRANCH_W), lambda i: (i, 4)),
                  pl.BlockSpec((tile, BRANCH_W), lambda i: (i, 5)),
                  _const_spec((d, LANE)), _const_spec((1, LANE)), _const_spec((tile, tile)),
                  _const_spec((BRANCH_W, BRANCH_W)), _const_spec((LANE, BRANCH_W))],
        out_specs=[pl.BlockSpec((SUBLANE, tile), lambda i: (0, i)),
                   pl.BlockSpec((1, SUBLANE, BRANCH_W), lambda i: (i, 0, 0))],
        out_shape=[jax.ShapeDtypeStruct((SUBLANE, s), F32),
                   jax.ShapeDtypeStruct((s // tile, SUBLANE, BRANCH_W), F32)],
        scratch_shapes=[pltpu.VMEM((1, LANE), F32), pltpu.VMEM((1, BRANCH_W), F32)],
        compiler_params=_params("arbitrary"),
        name="fox_gate",
    )(h, proj, proj, w_f, bias, tril, _head_ones(), expand)
    return ct, stats[:, :4, ::HEAD_DIM].reshape(-1)


def _fox_kernel(stat_ref, q_ref, k_ref, v_ref, ct_ref, hm_ref, o_ref, m_sc, l_sc, acc_sc, *, tq):
    i = pl.program_id(0)
    q0 = pl.multiple_of(i * tq, tq)
    hm = hm_ref[...]
    q = q_ref[...].astype(F32) * (HEAD_DIM ** -0.5)
    qh = [(q * hm[h:h + 1, :]).astype(BF16) for h in range(N_HEADS)]
    c_q0 = ct_ref[:, pl.ds(q0, tq)][:, 0:1]

    first = i
    for h in range(N_HEADS):
        qmax = stat_ref[i * FOX_NSTAT + h]
        emax = stat_ref[i * FOX_NSTAT + N_HEADS + h]

        def needed(j, h=h, qmax=qmax, emax=emax):
            jc = jnp.maximum(j, 0)
            bound = (qmax * stat_ref[jc * FOX_NSTAT + 2 * N_HEADS + h] + emax
                     - stat_ref[jc * FOX_NSTAT + 3 * N_HEADS + h])
            return (j >= 0) & (bound >= -FOX_SKIP_LOG)

        last_dropped = lax.while_loop(needed, lambda j: j - 1, i - 1)
        first = jnp.minimum(first, last_dropped + 1)

    m_sc[...] = jnp.full_like(m_sc, -jnp.inf)
    l_sc[...] = jnp.zeros_like(l_sc)
    acc_sc[...] = jnp.zeros_like(acc_sc)

    def block(s0, diagonal):
        kb = k_ref[pl.ds(s0, tq), :]
        vb = v_ref[pl.ds(s0, tq), :]
        bias = c_q0 - ct_ref[:, pl.ds(s0, tq)]
        for h in range(N_HEADS):
            sc = _dot_nt(qh[h], kb) + bias[h:h + 1, :]
            if diagonal:
                r = lax.broadcasted_iota(jnp.int32, (tq, tq), 0)
                cidx = lax.broadcasted_iota(jnp.int32, (tq, tq), 1)
                sc = jnp.where(cidx <= r, sc, -jnp.inf)
            m_prev = m_sc[h]
            m_new = jnp.maximum(m_prev, jnp.max(sc, axis=1, keepdims=True))
            alpha = jnp.exp(m_prev - m_new)
            p = jnp.exp(sc - jnp.tile(m_new, (1, tq // LANE)))
            l_sc[h] = alpha * l_sc[h] + jnp.sum(p, axis=1, keepdims=True)
            acc_sc[h] = acc_sc[h] * jnp.tile(alpha, (1, BRANCH_W // LANE)) + _dot(p.astype(BF16), vb)
            m_sc[h] = m_new

    def body(j, carry):
        block(pl.multiple_of(j * tq, tq), False)
        return carry

    lax.fori_loop(first, i, body, 0)
    block(q0, True)

    out = jnp.zeros((tq, BRANCH_W), F32)
    for h in range(N_HEADS):
        out = out + acc_sc[h] * hm[h:h + 1, :] / jnp.tile(l_sc[h], (1, BRANCH_W // LANE))
    o_ref[...] = out.astype(o_ref.dtype)


def fox_attention(proj, ct, stats):
    s = proj.shape[0]
    tq = min(FOX_TILE, s)
    full = lambda j: pl.BlockSpec((s, BRANCH_W), lambda i, j=j: (0, j), pipeline_mode=pl.Buffered(1))
    return pl.pallas_call(
        functools.partial(_fox_kernel, tq=tq),
        grid=(s // tq,),
        in_specs=[pl.BlockSpec(memory_space=pltpu.SMEM),
                  pl.BlockSpec((tq, BRANCH_W), lambda i: (i, 4)), full(5), full(6),
                  _const_spec((SUBLANE, s)), _const_spec((N_HEADS, BRANCH_W))],
        out_specs=pl.BlockSpec((tq, BRANCH_W), lambda i: (i, 0)),
        out_shape=jax.ShapeDtypeStruct((s, BRANCH_W), BF16),
        scratch_shapes=[pltpu.VMEM((N_HEADS, tq, LANE), F32),
                        pltpu.VMEM((N_HEADS, tq, LANE), F32),
                        pltpu.VMEM((N_HEADS, tq, BRANCH_W), F32)],
        compiler_params=_params("parallel"),
        name="fox_attention",
    )(stats, proj, proj, proj, ct, _head_masks())


def _pool_kernel(u_ref, w_ref, scale_ref, o_ref, ext, *, tile):
    i = pl.program_id(0)

    @pl.when(i == 0)
    def _():
        ext[pl.ds(0, POOL_HALO), :] = jnp.zeros((POOL_HALO, BRANCH_W), F32)

    u = u_ref[...].astype(F32)
    ext[pl.ds(POOL_HALO, tile), :] = u
    pos = (i * tile + lax.broadcasted_iota(jnp.int32, (tile, 1), 0) + 1).astype(F32)
    halves = []
    for half in range(BRANCH_W // LANE):
        lanes = pl.ds(half * LANE, LANE)
        w_small, w_big = POOL_WINDOWS[2 * half], POOL_WINDOWS[2 * half + 1]
        run = u[:, half * LANE:(half + 1) * LANE]
        sums = {}
        for j in range(1, w_big):
            if j == w_small:
                sums[w_small] = run
            run = run + ext[pl.ds(POOL_HALO - j, tile), lanes]
        sums[w_big] = run
        lane = lax.broadcasted_iota(jnp.int32, (1, LANE), 1)
        small = lane < POOL_GROUP
        total = jnp.where(small, sums[w_small], sums[w_big])
        count = jnp.where(small, jnp.minimum(pos, float(w_small)), jnp.minimum(pos, float(w_big)))
        halves.append(total / count)
    mean = jnp.concatenate(halves, axis=1)
    d = (mean - u).astype(BF16)
    y = _dot(d, w_ref[...]) * scale_ref[...]
    o_ref[...] = y.astype(o_ref.dtype)
    ext[pl.ds(0, POOL_HALO), :] = u[tile - POOL_HALO:, :]


def pool_mixer(proj, w_pool, scale, tile=512):
    s = proj.shape[0]
    tile = min(tile, s)
    ng = len(POOL_WINDOWS)
    w_bd = jnp.zeros((BRANCH_W, BRANCH_W), F32)
    for gi in range(ng):
        lo = gi * POOL_GROUP
        w_bd = w_bd.at[lo:lo + POOL_GROUP, lo:lo + POOL_GROUP].set(w_pool[gi].astype(F32))
    return pl.pallas_call(
        functools.partial(_pool_kernel, tile=tile),
        grid=(s // tile,),
        in_specs=[pl.BlockSpec((tile, BRANCH_W), lambda i: (i, 7)),
                  _const_spec((BRANCH_W, BRANCH_W)), _const_spec((1, BRANCH_W))],
        out_specs=pl.BlockSpec((tile, BRANCH_W), lambda i: (i, 0)),
        out_shape=jax.ShapeDtypeStruct((s, BRANCH_W), BF16),
        scratch_shapes=[pltpu.VMEM((tile + POOL_HALO, BRANCH_W), F32)],
        compiler_params=_params("arbitrary"),
        name="pool_mixer",
    )(proj, w_bd.astype(BF16), scale.reshape(1, BRANCH_W).astype(F32))


def _ret_kernel(q_ref, k_ref, v_ref, g_ref, cos_ref, sin_ref, perm_ref, ones_ref, hm_ref,
                dstack_ref, xi_ref, zeta_ref, gc_ref, gng_ref, gnb_ref, o_ref, st_ref, *, tile):
    c = RET_CHUNK

    @pl.when(pl.program_id(0) == 0)
    def _():
        st_ref[...] = jnp.zeros_like(st_ref)

    perm = perm_ref[...]
    ones_bd = ones_ref[...]
    bd_mask = ones_bd.astype(F32)
    hm = hm_ref[...]

    def chunk(ci, carry):
        r0 = pl.multiple_of(ci * c, c)
        qb = q_ref[pl.ds(r0, c), :]
        kb = k_ref[pl.ds(r0, c), :]
        v = v_ref[pl.ds(r0, c), :]
        g = g_ref[pl.ds(r0, c), :].astype(F32)
        cos = cos_ref[pl.ds(r0, c), :]
        sin = sin_ref[pl.ds(r0, c), :]
        qr = qb.astype(F32) * cos + _dot(qb, perm) * sin
        kr = (kb.astype(F32) * cos + _dot(kb, perm) * sin) * (HEAD_DIM ** -0.5)

        qx = jnp.concatenate([qr * hm[h:h + 1, :] for h in range(N_HEADS)], axis=0).astype(BF16)
        sc = _dot_nt(qx, kr.astype(BF16)) * dstack_ref[...]
        r = _dot(sc.astype(BF16), v)
        intra = jnp.zeros((c, BRANCH_W), F32)
        for h in range(N_HEADS):
            intra = intra + r[h * c:(h + 1) * c, :] * hm[h:h + 1, :]

        st = st_ref[...]
        inter = _dot_nt((qr * xi_ref[...]).astype(BF16), st.astype(BF16))
        upd = _dot(v.astype(F32).T.astype(BF16), (kr * zeta_ref[...]).astype(BF16))
        st_ref[...] = st * gc_ref[...] + upd * bd_mask

        o = intra + inter
        mu = _dot_exact_rhs(o, ones_bd) * (1.0 / HEAD_DIM)
        cen = o - mu
        var = _dot_exact_rhs(cen * cen, ones_bd) * (1.0 / HEAD_DIM)
        y = cen * lax.rsqrt(var + LN_EPS) * gng_ref[...] + gnb_ref[...]
        o_ref[pl.ds(r0, c), :] = (y * _silu(g)).astype(o_ref.dtype)
        return carry

    lax.fori_loop(0, tile // c, chunk, 0)


def _rope_tables(s):
    half = HEAD_DIM // 2
    pos = jnp.arange(s, dtype=F32)
    inv_freq = ROPE_BASE ** (-jnp.arange(half, dtype=F32) / half)
    ang = pos[:, None] * inv_freq[None, :]
    cos, sin = jnp.cos(ang), jnp.sin(ang)
    cos_t = jnp.tile(jnp.concatenate([cos, cos], axis=1), (1, N_HEADS))
    sin_t = jnp.tile(jnp.concatenate([-sin, sin], axis=1), (1, N_HEADS))
    return cos_t, sin_t


def _ret_constants():
    c = RET_CHUNK
    half = HEAD_DIM // 2
    lane = jnp.arange(BRANCH_W)
    partner = jnp.where(lane % HEAD_DIM < half, lane + half, lane - half)
    perm = (lane[:, None] == partner[None, :]).astype(BF16)
    log_gamma = jnp.log1p(-jnp.exp2(-RET_DECAY_BASE - jnp.arange(N_HEADS, dtype=F32)))
    ci = jnp.arange(c, dtype=F32)
    diff = ci[:, None] - ci[None, :]
    intra = jnp.where(diff >= 0, jnp.exp(diff * log_gamma[:, None, None]), 0.0)
    dstack = intra.reshape(N_HEADS * c, c)
    lg_lane = jnp.repeat(log_gamma, HEAD_DIM)[None, :]
    xi = jnp.exp((ci[:, None] + 1.0) * lg_lane)
    zeta = jnp.exp((c - 1.0 - ci[:, None]) * lg_lane)
    gc = jnp.exp(c * lg_lane)
    return perm, dstack, xi, zeta, gc


def retention(proj, gn_g, gn_b, tile=512):
    s = proj.shape[0]
    c = RET_CHUNK
    tile = min(tile, s)
    cos_t, sin_t = _rope_tables(s)
    perm, dstack, xi, zeta, gc = _ret_constants()
    col = lambda j: pl.BlockSpec((tile, BRANCH_W), lambda i, j=j: (i, j))
    row = pl.BlockSpec((tile, BRANCH_W), lambda i: (i, 0))
    return pl.pallas_call(
        functools.partial(_ret_kernel, tile=tile),
        grid=(s // tile,),
        in_specs=[col(8), col(9), col(10), col(11), row, row,
                  _const_spec((BRANCH_W, BRANCH_W)), _const_spec((BRANCH_W, BRANCH_W)),
                  _const_spec((N_HEADS, BRANCH_W)), _const_spec((N_HEADS * c, c)),
                  _const_spec((c, BRANCH_W)), _const_spec((c, BRANCH_W)), _const_spec((1, BRANCH_W)),
                  _const_spec((1, BRANCH_W)), _const_spec((1, BRANCH_W))],
        out_specs=row,
        out_shape=jax.ShapeDtypeStruct((s, BRANCH_W), BF16),
        scratch_shapes=[pltpu.VMEM((BRANCH_W, BRANCH_W), F32)],
        compiler_params=_params("arbitrary"),
        name="retention",
    )(proj, proj, proj, proj, cos_t, sin_t, perm, _head_ones(), _head_masks(), dstack, xi, zeta, gc,
      gn_g.reshape(1, BRANCH_W).astype(F32), gn_b.reshape(1, BRANCH_W).astype(F32))


def _merge_kernel(h_ref, o0_ref, o1_ref, o2_ref, o3_ref, x_ref, wg_ref, wb_ref, wo_ref, g_ref,
                  xo_ref, ho_ref):
    h = h_ref[...]
    merged = jnp.zeros(x_ref.shape, F32)
    for bi, o_ref in enumerate((o0_ref, o1_ref, o2_ref, o3_ref)):
        gate = _sigmoid(_dot(h, wg_ref[:, bi * D_MODEL:(bi + 1) * D_MODEL]))
        merged = merged + gate * _dot(o_ref[...], wb_ref[bi])
    x_new = x_ref[...] + _dot(merged.astype(BF16), wo_ref[...])
    xo_ref[...] = x_new
    ho_ref[...] = _rms(x_new, g_ref[...]).astype(ho_ref.dtype)


def merge(h, branches, x, w_gate, w_branch, w_out, next_gain, tm=512):
    s, d = x.shape
    tm = min(tm, s)
    row = lambda w: pl.BlockSpec((tm, w), lambda i: (i, 0))
    return pl.pallas_call(
        _merge_kernel,
        grid=(s // tm,),
        in_specs=[row(d), row(BRANCH_W), row(BRANCH_W), row(BRANCH_W), row(BRANCH_W), row(d),
                  _const_spec((d, N_BRANCH * d)), _const_spec((N_BRANCH, BRANCH_W, d)),
                  _const_spec((d, d)), _const_spec((1, d))],
        out_specs=[row(d), row(d)],
        out_shape=[jax.ShapeDtypeStruct((s, d), F32), jax.ShapeDtypeStruct((s, d), BF16)],
        compiler_params=_params("parallel"),
        name="merge",
    )(h, *branches, x, w_gate, w_branch, w_out, next_gain.reshape(1, d).astype(F32))


def _ffn_kernel(h_ref, x_ref, wg_ref, wu_ref, wd_ref, g_ref, xo_ref, ho_ref, acc_ref):
    f = pl.program_id(1)

    @pl.when(f == 0)
    def _():
        acc_ref[...] = jnp.zeros_like(acc_ref)

    h = h_ref[...]
    a = _silu(_dot(h, wg_ref[...])) * _dot(h, wu_ref[...])
    acc_ref[...] += _dot(a.astype(BF16), wd_ref[...])

    @pl.when(f == pl.num_programs(1) - 1)
    def _():
        x_new = x_ref[...] + acc_ref[...]
        xo_ref[...] = x_new
        ho_ref[...] = _rms(x_new, g_ref[...]).astype(ho_ref.dtype)


def ffn_dense(h, x, w_gate, w_up, w_down, next_gain, tm=1024, tf=512):
    s, d = x.shape
    tm = min(tm, s)
    dff = w_gate.shape[1]
    row = lambda: pl.BlockSpec((tm, d), lambda i, f: (i, 0))
    return pl.pallas_call(
        _ffn_kernel,
        grid=(s // tm, dff // tf),
        in_specs=[row(), row(),
                  pl.BlockSpec((d, tf), lambda i, f: (0, f)),
                  pl.BlockSpec((d, tf), lambda i, f: (0, f)),
                  pl.BlockSpec((tf, d), lambda i, f: (f, 0)),
                  _const_spec((1, d))],
        out_specs=[row(), row()],
        out_shape=[jax.ShapeDtypeStruct((s, d), F32), jax.ShapeDtypeStruct((s, d), BF16)],
        scratch_shapes=[pltpu.VMEM((tm, d), F32)],
        compiler_params=_params("parallel", "arbitrary"),
        name="ffn_dense",
    )(h, x, w_gate, w_up, w_down, next_gain.reshape(1, d).astype(F32))


MOE_TOK_TILE = 256
MOE_ROW_BLOCK = 512
MOE_Y_BLOCK = 256


def _dot_f32(x, w):
    xh, xm, _ = _split3(x)
    wh, wm, _ = _split3(w)
    return _dot(xh, wh) + (_dot(xh, wm) + _dot(xm, wh))


def _route_kernel(x_ref, ng_ref, router_ref, ltri_ref, pos_ref, gate_ref, post_ref, before_ref, total_ref,
                  carry_ref):
    tm = x_ref.shape[0]

    @pl.when(pl.program_id(0) == 0)
    def _():
        carry_ref[...] = jnp.zeros_like(carry_ref)

    hn = _rms(x_ref[...], ng_ref[...])
    logits = _dot_f32(hn, router_ref[...])
    lane = lax.broadcasted_iota(jnp.int32, (tm, LANE), 1)
    logits = jnp.where(lane < N_EXPERTS, logits, -jnp.inf)
    v1 = jnp.max(logits, axis=1, keepdims=True)
    i1 = jnp.min(jnp.where(logits == v1, lane, LANE), axis=1, keepdims=True)
    rest = jnp.where(lane == i1, -jnp.inf, logits)
    v2 = jnp.max(rest, axis=1, keepdims=True)
    i2 = jnp.min(jnp.where(rest == v2, lane, LANE), axis=1, keepdims=True)
    w1 = 1.0 / (1.0 + jnp.exp(v2 - v1))
    gate_ref[...] = jnp.where(lane == i1, w1, 0.0) + jnp.where(lane == i2, 1.0 - w1, 0.0)

    member = jnp.where((lane == i1) | (lane == i2), 1.0, 0.0)
    carry = carry_ref[...]
    rank = _dot(ltri_ref[...], member.astype(BF16)) + carry
    pos = jnp.where(member > 0.0, rank, -1.0)
    pos_ref[...] = pos
    post_ref[...] = pos.T[:SUBLANE, :]
    before_ref[0] = carry
    carry = carry + jnp.sum(member, axis=0, keepdims=True)
    carry_ref[...] = carry
    total_ref[...] = carry


def moe_route(x, norm_gain, router):
    s, d = x.shape
    tm = min(MOE_TOK_TILE, s)
    nt = s // tm
    router_p = jnp.zeros((d, LANE), F32).at[:, :N_EXPERTS].set(router.astype(F32))
    ltri = jnp.tril(jnp.ones((tm, tm), F32), -1).astype(BF16)
    row = pl.BlockSpec((tm, LANE), lambda i: (i, 0))
    return pl.pallas_call(
        _route_kernel,
        grid=(nt,),
        in_specs=[pl.BlockSpec((tm, d), lambda i: (i, 0)), _const_spec((1, d)), _const_spec((d, LANE)),
                  _const_spec((tm, tm))],
        out_specs=[row, row, pl.BlockSpec((SUBLANE, tm), lambda i: (0, i)),
                   pl.BlockSpec((1, 1, LANE), lambda i: (i, 0, 0)), pl.BlockSpec((1, LANE), lambda i: (0, 0))],
        out_shape=[jax.ShapeDtypeStruct((s, LANE), F32), jax.ShapeDtypeStruct((s, LANE), F32),
                   jax.ShapeDtypeStruct((SUBLANE, s), F32), jax.ShapeDtypeStruct((nt, 1, LANE), F32),
                   jax.ShapeDtypeStruct((1, LANE), F32)],
        scratch_shapes=[pltpu.VMEM((1, LANE), F32)],
        compiler_params=_params("arbitrary"),
        name="moe_route",
    )(x, norm_gain.reshape(1, d).astype(F32), router_p, ltri)


def _moe_ffn_kernel(be_ref, r0_ref, tlo_ref, thi_ref, nv_ref, h_ref, post_ref, wg_ref, wu_ref, wd_ref,
                    y_ref, x_sc, acc_sc, *, tt):
    b = pl.program_id(0)
    f = pl.program_id(1)
    valid = b < nv_ref[0]
    tmb = x_sc.shape[0]

    @pl.when(valid & (f == 0))
    def _():
        e = be_ref[b]
        want = lax.broadcasted_iota(jnp.int32, (tmb, 1), 0).astype(F32) + r0_ref[b].astype(F32)
        acc_sc[...] = jnp.zeros_like(acc_sc)

        def gather(t, carry):
            t0 = pl.multiple_of(t * tt, tt)
            p = post_ref[pl.ds(e, 1), pl.ds(t0, tt)]
            sel = jnp.where(p == want, 1.0, 0.0).astype(BF16)
            acc_sc[...] += _dot(sel, h_ref[pl.ds(t0, tt), :])
            return carry

        lax.fori_loop(tlo_ref[b], thi_ref[b] + 1, gather, 0)
        x_sc[...] = acc_sc[...].astype(BF16)
        acc_sc[...] = jnp.zeros_like(acc_sc)

    @pl.when(valid)
    def _():
        xb = x_sc[...]
        a = _silu(_dot(xb, wg_ref[0])) * _dot(xb, wu_ref[0])
        acc_sc[...] += _dot(a.astype(BF16), wd_ref[0])

    @pl.when(f == pl.num_programs(1) - 1)
    def _():
        y_ref[...] = jnp.where(valid, acc_sc[...], 0.0).astype(y_ref.dtype)


def moe_ffn(h, post, sched, w_gate, w_up, w_down, tf=512):
    s, d = h.shape
    ne, _, dff = w_gate.shape
    tmb = MOE_ROW_BLOCK
    tt = min(MOE_TOK_TILE, s)
    nb = sched[0].shape[0]
    nf = dff // tf

    def wspec(shape, fn):
        return pl.BlockSpec(shape, fn)

    def fidx(b, f, nv):
        return jnp.where(b < nv[0], f, nf - 1)

    grid_spec = pltpu.PrefetchScalarGridSpec(
        num_scalar_prefetch=5,
        grid=(nb, nf),
        in_specs=[pl.BlockSpec((s, d), lambda b, f, *_: (0, 0), pipeline_mode=pl.Buffered(1)),
                  pl.BlockSpec((SUBLANE, s), lambda b, f, *_: (0, 0), pipeline_mode=pl.Buffered(1)),
                  wspec((1, d, tf), lambda b, f, be, r0, tlo, thi, nv: (be[b], 0, fidx(b, f, nv))),
                  wspec((1, d, tf), lambda b, f, be, r0, tlo, thi, nv: (be[b], 0, fidx(b, f, nv))),
                  wspec((1, tf, d), lambda b, f, be, r0, tlo, thi, nv: (be[b], fidx(b, f, nv), 0))],
        out_specs=pl.BlockSpec((tmb, d), lambda b, f, *_: (b, 0)),
        scratch_shapes=[pltpu.VMEM((tmb, d), BF16), pltpu.VMEM((tmb, d), F32)],
    )
    return pl.pallas_call(
        functools.partial(_moe_ffn_kernel, tt=tt),
        grid_spec=grid_spec,
        out_shape=jax.ShapeDtypeStruct((nb * tmb, d), BF16),
        compiler_params=_params("arbitrary", "arbitrary"),
        name="moe_ffn",
    )(*sched, h, post, w_gate, w_up, w_down)


def _moe_combine_kernel(kb_ref, off_ref, x_ref, pos_ref, gate_ref, fg_ref, *rest):
    y_refs, o_ref = rest[:-1], rest[-1]
    t = pl.program_id(0)
    tm = x_ref.shape[0]
    yb = y_refs[0].shape[0]
    pos = pos_ref[...]
    gate = gate_ref[...]
    lane = lax.broadcasted_iota(jnp.int32, (tm, LANE), 1)
    col = lax.broadcasted_iota(jnp.int32, (1, yb), 1).astype(F32)
    acc = x_ref[...]
    for e in range(N_EXPERTS):
        pe = jnp.sum(jnp.where(lane == e, pos, 0.0), axis=1, keepdims=True)
        ge = jnp.sum(jnp.where(lane == e, gate, 0.0), axis=1, keepdims=True)
        r = jnp.where(pe >= 0.0, pe + off_ref[t * N_EXPERTS + e].astype(F32), -1.0)
        z = (_dot(jnp.where(r == col, 1.0, 0.0).astype(BF16), y_refs[2 * e][...])
             + _dot(jnp.where(r == col + float(yb), 1.0, 0.0).astype(BF16), y_refs[2 * e + 1][...]))
        acc = acc + ge * z
    o_ref[...] = _rms(acc, fg_ref[...]).astype(o_ref.dtype)


def moe_combine(x, pos, gate, y, kb, off, final_gain):
    s, d = x.shape
    tm = min(MOE_TOK_TILE, s)
    yb = MOE_Y_BLOCK
    last = y.shape[0] // yb - 1

    def yspec(e, k):
        return pl.BlockSpec((yb, d), lambda t, kb_r, off_r: (jnp.minimum(kb_r[t * N_EXPERTS + e] + k, last), 0))

    grid_spec = pltpu.PrefetchScalarGridSpec(
        num_scalar_prefetch=2,
        grid=(s // tm,),
        in_specs=[pl.BlockSpec((tm, d), lambda t, *_: (t, 0)),
                  pl.BlockSpec((tm, LANE), lambda t, *_: (t, 0)),
                  pl.BlockSpec((tm, LANE), lambda t, *_: (t, 0)),
                  pl.BlockSpec((1, d), lambda t, *_: (0, 0))]
        + [yspec(e, k) for e in range(N_EXPERTS) for k in range(2)],
        out_specs=pl.BlockSpec((tm, d), lambda t, *_: (t, 0)),
    )
    return pl.pallas_call(
        _moe_combine_kernel,
        grid_spec=grid_spec,
        out_shape=jax.ShapeDtypeStruct((s, d), F32),
        compiler_params=_params("arbitrary"),
        name="moe_combine",
    )(kb, off, x, pos, gate, final_gain.reshape(1, d).astype(F32), *([y] * (2 * N_EXPERTS)))


def _moe_schedule(before, total, s):
    tmb, yb = MOE_ROW_BLOCK, MOE_Y_BLOCK
    nb = 2 * s // tmb + N_EXPERTS
    counts = total[0, :N_EXPERTS].astype(jnp.int32)
    nblk = (counts + tmb - 1) // tmb
    end = jnp.cumsum(nblk)
    first = end - nblk
    nvalid = end[-1]
    b = jnp.minimum(jnp.arange(nb, dtype=jnp.int32), nvalid - 1)
    blk_e = jnp.sum(b[:, None] >= end[None, :], axis=1).astype(jnp.int32)
    r0 = (b - first[blk_e]) * tmb
    cb = before[:, 0, :N_EXPERTS].astype(jnp.int32)
    cbe = cb[:, blk_e]
    tlo = (jnp.sum(cbe <= r0[None, :], axis=0) - 1).astype(jnp.int32)
    thi = (jnp.sum(cbe < (r0 + tmb)[None, :], axis=0) - 1).astype(jnp.int32)
    sched = (blk_e, r0.astype(jnp.int32), tlo, thi, nvalid.reshape(1).astype(jnp.int32))
    row_start = first[None, :] * tmb + cb
    kb = row_start // yb
    off = first[None, :] * tmb - kb * yb
    return sched, kb.reshape(-1).astype(jnp.int32), off.reshape(-1).astype(jnp.int32)


def moe_sparse(h, x, norm_gain, router, w_gate, w_up, w_down, final_gain):
    s, _ = x.shape
    pos, gate, post, before, total = moe_route(x, norm_gain, router)
    sched, kb, off = _moe_schedule(before, total, s)
    y = moe_ffn(h, post, sched, w_gate, w_up, w_down)
    return moe_combine(x, pos, gate, y, kb, off, final_gain)


FOX_F0 = 7 * BRANCH_W


def _split_w_in_kernel(w_ref, mix_ref, f_ref, gate_ref):
    w = w_ref[...]
    rows = w.shape[0]
    mix_ref[:, :FOX_F0] = w[:, :FOX_F0].astype(BF16)
    mix_ref[:, FOX_F0:] = w[:, FOX_F0 + N_HEADS:N_MIX_COLS + N_HEADS].astype(BF16)
    f_ref[...] = jnp.concatenate([w[:, FOX_F0:FOX_F0 + N_HEADS], jnp.zeros((rows, LANE - N_HEADS), F32)],
                                 axis=1).astype(BF16)
    gate_ref[...] = w[:, N_MIX_COLS + N_HEADS:].astype(BF16)


def _mixer_weights(w_in, layer, tr=128):
    _, d, cols = w_in.shape
    row = lambda w: pl.BlockSpec((tr, w), lambda i: (i, 0))
    return pl.pallas_call(
        _split_w_in_kernel,
        grid=(d // tr,),
        in_specs=[pl.BlockSpec((None, tr, cols), lambda i: (layer, i, 0))],
        out_specs=[row(N_MIX_COLS), row(LANE), row(N_BRANCH * D_MODEL)],
        out_shape=[jax.ShapeDtypeStruct((d, N_MIX_COLS), BF16), jax.ShapeDtypeStruct((d, LANE), BF16),
                   jax.ShapeDtypeStruct((d, N_BRANCH * D_MODEL), BF16)],
        compiler_params=_params("parallel"),
        name="split_w_in",
    )(w_in)


def kernel(x, w_in, w_branch, w_out, norm_mix_g, hgrn_lb_logits, hgrn_norm_g, fox_f_bias, pool_w, pool_scale,
           ret_gn_g, ret_gn_b, norm_ffn_g, ffn_w_gate, ffn_w_up, ffn_w_down, moe_router, moe_w_gate, moe_w_up,
           moe_w_down, final_norm_g):
    b, s, d = x.shape
    assert b == 1 and d == D_MODEL
    depth = w_in.shape[0]
    assert depth == 2, "layer 0 uses the dense FFN, layer 1 the experts and the final norm"
    xs = x.reshape(s, d)
    h = rmsnorm_bf16(xs, norm_mix_g[0])
    out = None
    for layer in range(depth):
        w_mix, w_f, w_gate = _mixer_weights(w_in, layer)
        proj = matmul(h, w_mix)
        ct, fox_stats = fox_gate(h, proj, w_f, fox_f_bias[layer])
        branches = (
            hgrn2(proj, hgrn_lb_logits, hgrn_norm_g[layer], layer),
            fox_attention(proj, ct, fox_stats),
            pool_mixer(proj, pool_w[layer], pool_scale[layer]),
            retention(proj, ret_gn_g[layer], ret_gn_b[layer]),
        )
        xs, h2 = merge(h, branches, xs, w_gate, w_branch[layer].astype(BF16), w_out[layer].astype(BF16),
                       norm_ffn_g[layer])
        if layer % 2 == 0:
            li = layer // 2
            xs, h = ffn_dense(h2, xs, ffn_w_gate[li].astype(BF16), ffn_w_up[li].astype(BF16),
                              ffn_w_down[li].astype(BF16), norm_mix_g[layer + 1])
        else:
            li = layer // 2
            out = moe_sparse(h2, xs, norm_ffn_g[layer], moe_router[li], moe_w_gate[li].astype(BF16),
                             moe_w_up[li].astype(BF16), moe_w_down[li].astype(BF16), final_norm_g)
    return out.reshape(b, s, d)
```

```python
import functools
import math

import jax
import jax.numpy as jnp
from jax import lax
from jax.experimental import pallas as pl
from jax.experimental.pallas import tpu as pltpu

D_MODEL = 1024
N_BRANCH = 4
BRANCH_W = D_MODEL // N_BRANCH
HEAD_DIM = 64
N_HEADS = BRANCH_W // HEAD_DIM
POOL_WINDOWS = (2, 4, 8, 16)
POOL_GROUP = BRANCH_W // len(POOL_WINDOWS)
POOL_HALO = 16
RET_DECAY_BASE = 5.0
ROPE_BASE = 10000.0
D_FF = 7 * D_MODEL // 2
N_EXPERTS = 8
RMS_EPS = 1e-6
LN_EPS = 1e-5
N_MIX_COLS = 12 * BRANCH_W

LANE = 128
SUBLANE = 8
VMEM_LIMIT = 56 * 1024 * 1024

HG_CHUNK = 64
HG_SUB = 16
HG_FAST_MIN_LOGDECAY = -60.0
RET_CHUNK = 256

F32 = jnp.float32
BF16 = jnp.bfloat16
NT_DIMS = (((1,), (1,)), ((), ()))


def _params(*sem):
    return pltpu.CompilerParams(dimension_semantics=sem, vmem_limit_bytes=VMEM_LIMIT)


def _const_spec(shape):
    nd = len(shape)
    return pl.BlockSpec(shape, lambda *_: (0,) * nd, pipeline_mode=pl.Buffered(1))


def _split3(x):
    hi = x.astype(BF16)
    r1 = x - hi.astype(F32)
    mid = r1.astype(BF16)
    lo = (r1 - mid.astype(F32)).astype(BF16)
    return hi, mid, lo


def _dot(a, b):
    return jnp.dot(a, b, preferred_element_type=F32)


def _dot_nt(a, b):
    return lax.dot_general(a, b, NT_DIMS, preferred_element_type=F32)


def _dot_exact_rhs(x, m_bf16):
    hi, mid, lo = _split3(x)
    return _dot(hi, m_bf16) + _dot(mid, m_bf16) + _dot(lo, m_bf16)


def _dot_exact_lhs(m_bf16, x):
    hi, mid, lo = _split3(x)
    return _dot(m_bf16, hi) + _dot(m_bf16, mid) + _dot(m_bf16, lo)


def _sigmoid(x):
    return 1.0 / (1.0 + jnp.exp(-x))


def _silu(x):
    return x * _sigmoid(x)


def _rms(x, gain):
    return x * lax.rsqrt(jnp.mean(x * x, axis=-1, keepdims=True) + RMS_EPS) * gain


def _head_of(n):
    return jnp.arange(n) // HEAD_DIM


def _head_ones():
    h = _head_of(BRANCH_W)
    return (h[:, None] == h[None, :]).astype(BF16)


def _head_masks():
    return (_head_of(BRANCH_W)[None, :] == jnp.arange(N_HEADS)[:, None]).astype(F32)


def _rmsnorm_kernel(x_ref, g_ref, o_ref):
    o_ref[...] = _rms(x_ref[...], g_ref[...]).astype(o_ref.dtype)


def rmsnorm_bf16(x, gain, tm=1024):
    s, d = x.shape
    tm = min(tm, s)
    return pl.pallas_call(
        _rmsnorm_kernel,
        grid=(s // tm,),
        in_specs=[pl.BlockSpec((tm, d), lambda i: (i, 0)), _const_spec((1, d))],
        out_specs=pl.BlockSpec((tm, d), lambda i: (i, 0)),
        out_shape=jax.ShapeDtypeStruct((s, d), BF16),
        compiler_params=_params("parallel"),
        name="rmsnorm",
    )(x, gain.reshape(1, d))


def _matmul_kernel(a_ref, b_ref, o_ref):
    o_ref[...] = _dot(a_ref[...], b_ref[...]).astype(o_ref.dtype)


def matmul(a, b, out_dtype=BF16, tm=2048, tn=1024):
    m, k = a.shape
    _, n = b.shape
    tm = min(tm, m)
    return pl.pallas_call(
        _matmul_kernel,
        grid=(n // tn, m // tm),
        in_specs=[pl.BlockSpec((tm, k), lambda j, i: (i, 0)),
                  pl.BlockSpec((k, tn), lambda j, i: (0, j))],
        out_specs=pl.BlockSpec((tm, tn), lambda j, i: (i, j)),
        out_shape=jax.ShapeDtypeStruct((m, n), out_dtype),
        compiler_params=_params("parallel", "parallel"),
        name="in_proj",
    )(a, b)


def _hgrn_kernel(q_ref, f_ref, i_ref, g_ref, lbl_ref, ng_ref, ones_ref, hm_ref, tril_ref, halfsum_ref, fmask_ref,
                 o_ref, st_ref, bpad, kpad, vpad, astack, lf_sc, kk_sc, *, layer, tile):
    c, sub = HG_CHUNK, HG_SUB
    nsub = c // sub
    half = c // 2

    @pl.when(pl.program_id(0) == 0)
    def _():
        st_ref[...] = jnp.zeros_like(st_ref)
        bpad[...] = jnp.zeros_like(bpad)
        kpad[...] = jnp.zeros_like(kpad)
        vpad[...] = jnp.zeros_like(vpad)

    lbl = lbl_ref[...]
    e = jnp.exp(lbl - jnp.max(lbl, axis=0, keepdims=True))
    p = e / jnp.sum(e, axis=0, keepdims=True)
    lb = jnp.zeros((1, BRANCH_W), F32)
    for l in range(1, layer + 1):
        lb = lb + p[l:l + 1, :]

    ones_bd = ones_ref[...]
    hm = hm_ref[...]
    tril = tril_ref[...]
    row = lax.broadcasted_iota(jnp.int32, (c, 1), 0)
    row_in_sub = row % sub
    bd_mask = ones_bd.astype(F32)

    sig = _sigmoid(f_ref[...].astype(F32))
    logf_all = jnp.log(lb + (1.0 - lb) * sig)
    lf_sc[...] = logf_all
    kk_sc[...] = (1.0 - lb) * (1.0 - sig)
    min_decay = jnp.min(_dot(halfsum_ref[...], logf_all.astype(BF16)))

    def load(ci):
        r0 = pl.multiple_of(ci * c, c)
        q = q_ref[pl.ds(r0, c), :].astype(F32)
        v = i_ref[pl.ds(r0, c), :].astype(F32)
        kk = kk_sc[pl.ds(r0, c), :]
        b = _dot_exact_lhs(tril, lf_sc[pl.ds(r0, c), :])
        return r0, q, v, kk, b

    def finish(r0, q_decayed, v, kk, b, intra):
        st = st_ref[...]
        inter = _dot_nt(q_decayed.astype(BF16), st.astype(BF16))
        b_last = b[c - 1:c, :]
        ks_end = (kk * jnp.exp(b_last - b)).astype(BF16)
        upd = _dot(v.T.astype(BF16), ks_end)
        st_ref[...] = st * jnp.exp(b_last) + upd * bd_mask
        o = intra + inter
        ms = _dot_exact_rhs(o * o, ones_bd) * (1.0 / HEAD_DIM)
        g = g_ref[pl.ds(r0, c), :].astype(F32)
        y = o * lax.rsqrt(ms + RMS_EPS) * ng_ref[...] * _silu(g)
        o_ref[pl.ds(r0, c), :] = y.astype(o_ref.dtype)

    def fast_chunk(ci, carry):
        r0, q, v, kk, b = load(ci)
        second = row >= half
        m_row = b[half - 1:half, :]
        mref = jnp.where(second, m_row, 0.0)
        qp = q * jnp.exp(b - mref)
        kp = kk * jnp.exp(mref - b)
        e_m = jnp.exp(m_row)
        kaug = jnp.concatenate([kp, kp[:half, :] * e_m], axis=0).astype(BF16)
        vaug = jnp.concatenate([v, v[:half, :]], axis=0).astype(BF16)
        qx = jnp.concatenate([qp * hm[h:h + 1, :] for h in range(N_HEADS)], axis=0).astype(BF16)
        sc = jnp.where(fmask_ref[...] > 0.0, _dot_nt(qx, kaug), 0.0)
        r = _dot(sc.astype(BF16), vaug)
        intra = jnp.zeros((c, BRANCH_W), F32)
        for h in range(N_HEADS):
            intra = intra + r[h * c:(h + 1) * c, :] * hm[h:h + 1, :]
        finish(r0, jnp.where(second, qp * e_m, qp), v, kk, b, intra)
        return carry

    def exact_chunk(ci, carry):
        r0, q, v, kk, b = load(ci)

        bpad[pl.ds(sub, c), :] = b
        kpad[pl.ds(sub, c), :] = kk
        vpad[pl.ds(sub, c), :] = v

        for d in range(sub):
            b_d = bpad[pl.ds(sub - d, c), :]
            k_d = kpad[pl.ds(sub - d, c), :]
            a = jnp.where(row_in_sub >= d, q * k_d * jnp.exp(b - b_d), 0.0)
            astack[pl.ds(d * c, c), :] = a.astype(BF16)
        pall = _dot(astack[...], ones_bd)
        intra = jnp.zeros((c, BRANCH_W), F32)
        for d in range(sub):
            intra = intra + pall[d * c:(d + 1) * c, :] * vpad[pl.ds(sub - d, c), :]

        pieces = [jnp.zeros((sub, BRANCH_W), F32)]
        for si in range(1, nsub):
            lo = si * sub
            m_i = b[lo - 1:lo, :]
            qs = q[lo:lo + sub, :] * jnp.exp(b[lo:lo + sub, :] - m_i)
            ks = (kk[:lo, :] * jnp.exp(m_i - b[:lo, :])).astype(BF16)
            qx = jnp.concatenate([qs * hm[h:h + 1, :] for h in range(N_HEADS)], axis=0).astype(BF16)
            sc = _dot_nt(qx, ks)
            r = _dot(sc.astype(BF16), v[:lo, :].astype(BF16))
            acc = jnp.zeros((sub, BRANCH_W), F32)
            for h in range(N_HEADS):
                acc = acc + r[h * sub:(h + 1) * sub, :] * hm[h:h + 1, :]
            pieces.append(acc)
        intra = intra + jnp.concatenate(pieces, axis=0)
        finish(r0, q * jnp.exp(b), v, kk, b, intra)
        return carry

    lax.cond(min_decay >= HG_FAST_MIN_LOGDECAY,
             lambda: lax.fori_loop(0, tile // c, fast_chunk, 0),
             lambda: lax.fori_loop(0, tile // c, exact_chunk, 0))


def hgrn2(proj, lb_logits, norm_g, layer, tile=512):
    s = proj.shape[0]
    depth = lb_logits.shape[0]
    c, sub = HG_CHUNK, HG_SUB
    half = c // 2
    tile = min(tile, s)
    col = lambda j: pl.BlockSpec((tile, BRANCH_W), lambda i, j=j: (i, j))
    tril = jnp.tril(jnp.ones((c, c), F32)).astype(BF16)
    nhalf = tile // half
    halfsum = (jnp.arange(tile)[None, :] // half == jnp.arange(nhalf)[:, None]).astype(BF16)
    t = jnp.arange(c)[:, None]
    col_s = jnp.arange(c + half)[None, :]
    same_half = (col_s < c) & (col_s // half == t // half) & (col_s <= t)
    cross = (col_s >= c) & (t >= half)
    fmask = jnp.tile((same_half | cross).astype(F32), (N_HEADS, 1))
    return pl.pallas_call(
        functools.partial(_hgrn_kernel, layer=layer, tile=tile),
        grid=(s // tile,),
        in_specs=[col(0), col(1), col(2), col(3),
                  _const_spec((depth, BRANCH_W)), _const_spec((1, BRANCH_W)),
                  _const_spec((BRANCH_W, BRANCH_W)), _const_spec((N_HEADS, BRANCH_W)),
                  _const_spec((c, c)), _const_spec((nhalf, tile)), _const_spec((N_HEADS * c, c + half))],
        out_specs=pl.BlockSpec((tile, BRANCH_W), lambda i: (i, 0)),
        out_shape=jax.ShapeDtypeStruct((s, BRANCH_W), BF16),
        scratch_shapes=[pltpu.VMEM((BRANCH_W, BRANCH_W), F32),
                        pltpu.VMEM((c + sub, BRANCH_W), F32),
                        pltpu.VMEM((c + sub, BRANCH_W), F32),
                        pltpu.VMEM((c + sub, BRANCH_W), F32),
                        pltpu.VMEM((sub * c, BRANCH_W), BF16),
                        pltpu.VMEM((tile, BRANCH_W), F32),
                        pltpu.VMEM((tile, BRANCH_W), F32)],
        compiler_params=_params("arbitrary"),
        name="hgrn2",
    )(proj, proj, proj, proj, lb_logits.astype(F32), norm_g.reshape(1, BRANCH_W).astype(F32),
      _head_ones(), _head_masks(), tril, halfsum, fmask)


FOX_TILE = 256
FOX_NSTAT = 16
FOX_SKIP_LOG = 40.0


def _fox_gate_kernel(h_ref, q_ref, k_ref, wf_ref, bias_ref, tril_ref, ones_ref, expand_ref,
                     ct_ref, stat_ref, carry_ref, kmax_ref):
    @pl.when(pl.program_id(0) == 0)
    def _():
        carry_ref[...] = jnp.zeros_like(carry_ref)
        kmax_ref[...] = jnp.zeros_like(kmax_ref)

    logit = _dot(h_ref[...], wf_ref[...]) + bias_ref[...]
    logf = jnp.minimum(logit, 0.0) - jnp.log(1.0 + jnp.exp(-jnp.abs(logit)))
    cum = _dot_exact_lhs(tril_ref[...], logf) + carry_ref[...]
    carry_ref[...] = cum[-1:, :]
    ct_ref[...] = cum.T[:SUBLANE, :]

    ones_bd = ones_ref[...]
    q = q_ref[...].astype(F32)
    k = k_ref[...].astype(F32)
    scale = HEAD_DIM ** -0.5
    c_heads = _dot_exact_rhs(cum, expand_ref[...])
    diag = _dot_exact_rhs(q * k, ones_bd) * scale
    qn = jnp.sqrt(_dot_exact_rhs(q * q, ones_bd)) * scale
    kn = jnp.sqrt(_dot_exact_rhs(k * k, ones_bd))
    kmax = jnp.maximum(kmax_ref[...], jnp.max(kn, axis=0, keepdims=True))
    kmax_ref[...] = kmax
    rows = [jnp.max(qn, axis=0, keepdims=True),
            jnp.max(c_heads - diag, axis=0, keepdims=True),
            kmax,
            c_heads[-1:, :]]
    stat_ref[0] = jnp.concatenate(rows + [jnp.zeros((SUBLANE - len(rows), BRANCH_W), F32)], axis=0)


def fox_gate(h, proj, w_f, f_bias):
    s, d = h.shape
    tile = min(FOX_TILE, s)
    bias = jnp.zeros((1, LANE), F32).at[0, :N_HEADS].set(f_bias.astype(F32))
    tril = jnp.tril(jnp.ones((tile, tile), F32)).astype(BF16)
    expand = (jnp.arange(LANE)[:, None] == _head_of(BRANCH_W)[None, :]).astype(BF16)
    ct, stats = pl.pallas_call(
        _fox_gate_kernel,
        grid=(s // tile,),
        in_specs=[pl.BlockSpec((tile, d), lambda i: (i, 0)),
                  pl.BlockSpec((tile, BRANCH_W), lambda i: (i, 4)),
                  pl.BlockSpec((tile, BRANCH_W), lambda i: (i, 5)),
                  _const_spec((d, LANE)), _const_spec((1, LANE)), _const_spec((tile, tile)),
                  _const_spec((BRANCH_W, BRANCH_W)), _const_spec((LANE, BRANCH_W))],
        out_specs=[pl.BlockSpec((SUBLANE, tile), lambda i: (0, i)),
                   pl.BlockSpec((1, SUBLANE, BRANCH_W), lambda i: (i, 0, 0))],
        out_shape=[jax.ShapeDtypeStruct((SUBLANE, s), F32),
                   jax.ShapeDtypeStruct((s // tile, SUBLANE, BRANCH_W), F32)],
        scratch_shapes=[pltpu.VMEM((1, LANE), F32), pltpu.VMEM((1, BRANCH_W), F32)],
        compiler_params=_params("arbitrary"),
        name="fox_gate",
    )(h, proj, proj, w_f, bias, tril, _head_ones(), expand)
    return ct, stats[:, :4, ::HEAD_DIM].reshape(-1)


def _fox_kernel(stat_ref, q_ref, k_ref, v_ref, ct_ref, hm_ref, o_ref, m_sc, l_sc, acc_sc, *, tq):
    i = pl.program_id(0)
    q0 = pl.multiple_of(i * tq, tq)
    hm = hm_ref[...]
    q = q_ref[...].astype(F32) * (HEAD_DIM ** -0.5)
    qh = [(q * hm[h:h + 1, :]).astype(BF16) for h in range(N_HEADS)]
    c_q0 = ct_ref[:, pl.ds(q0, tq)][:, 0:1]

    first = i
    for h in range(N_HEADS):
        qmax = stat_ref[i * FOX_NSTAT + h]
        emax = stat_ref[i * FOX_NSTAT + N_HEADS + h]

        def needed(j, h=h, qmax=qmax, emax=emax):
            jc = jnp.maximum(j, 0)
            bound = (qmax * stat_ref[jc * FOX_NSTAT + 2 * N_HEADS + h] + emax
                     - stat_ref[jc * FOX_NSTAT + 3 * N_HEADS + h])
            return (j >= 0) & (bound >= -FOX_SKIP_LOG)

        last_dropped = lax.while_loop(needed, lambda j: j - 1, i - 1)
        first = jnp.minimum(first, last_dropped + 1)

    m_sc[...] = jnp.full_like(m_sc, -jnp.inf)
    l_sc[...] = jnp.zeros_like(l_sc)
    acc_sc[...] = jnp.zeros_like(acc_sc)

    def block(s0, diagonal):
        kb = k_ref[pl.ds(s0, tq), :]
        vb = v_ref[pl.ds(s0, tq), :]
        bias = c_q0 - ct_ref[:, pl.ds(s0, tq)]
        for h in range(N_HEADS):
            sc = _dot_nt(qh[h], kb) + bias[h:h + 1, :]
            if diagonal:
                r = lax.broadcasted_iota(jnp.int32, (tq, tq), 0)
                cidx = lax.broadcasted_iota(jnp.int32, (tq, tq), 1)
                sc = jnp.where(cidx <= r, sc, -jnp.inf)
            m_prev = m_sc[h]
            m_new = jnp.maximum(m_prev, jnp.max(sc, axis=1, keepdims=True))
            alpha = jnp.exp(m_prev - m_new)
            p = jnp.exp(sc - jnp.tile(m_new, (1, tq // LANE)))
            l_sc[h] = alpha * l_sc[h] + jnp.sum(p, axis=1, keepdims=True)
            acc_sc[h] = acc_sc[h] * jnp.tile(alpha, (1, BRANCH_W // LANE)) + _dot(p.astype(BF16), vb)
            m_sc[h] = m_new

    def body(j, carry):
        block(pl.multiple_of(j * tq, tq), False)
        return carry

    lax.fori_loop(first, i, body, 0)
    block(q0, True)

    out = jnp.zeros((tq, BRANCH_W), F32)
    for h in range(N_HEADS):
        out = out + acc_sc[h] * hm[h:h + 1, :] / jnp.tile(l_sc[h], (1, BRANCH_W // LANE))
    o_ref[...] = out.astype(o_ref.dtype)


def fox_attention(proj, ct, stats):
    s = proj.shape[0]
    tq = min(FOX_TILE, s)
    full = lambda j: pl.BlockSpec((s, BRANCH_W), lambda i, j=j: (0, j), pipeline_mode=pl.Buffered(1))
    return pl.pallas_call(
        functools.partial(_fox_kernel, tq=tq),
        grid=(s // tq,),
        in_specs=[pl.BlockSpec(memory_space=pltpu.SMEM),
                  pl.BlockSpec((tq, BRANCH_W), lambda i: (i, 4)), full(5), full(6),
                  _const_spec((SUBLANE, s)), _const_spec((N_HEADS, BRANCH_W))],
        out_specs=pl.BlockSpec((tq, BRANCH_W), lambda i: (i, 0)),
        out_shape=jax.ShapeDtypeStruct((s, BRANCH_W), BF16),
        scratch_shapes=[pltpu.VMEM((N_HEADS, tq, LANE), F32),
                        pltpu.VMEM((N_HEADS, tq, LANE), F32),
                        pltpu.VMEM((N_HEADS, tq, BRANCH_W), F32)],
        compiler_params=_params("parallel"),
        name="fox_attention",
    )(stats, proj, proj, proj, ct, _head_masks())


def _pool_kernel(u_ref, w_ref, scale_ref, o_ref, ext, *, tile):
    i = pl.program_id(0)

    @pl.when(i == 0)
    def _():
        ext[pl.ds(0, POOL_HALO), :] = jnp.zeros((POOL_HALO, BRANCH_W), F32)

    u = u_ref[...].astype(F32)
    ext[pl.ds(POOL_HALO, tile), :] = u
    pos = (i * tile + lax.broadcasted_iota(jnp.int32, (tile, 1), 0) + 1).astype(F32)
    halves = []
    for half in range(BRANCH_W // LANE):
        lanes = pl.ds(half * LANE, LANE)
        w_small, w_big = POOL_WINDOWS[2 * half], POOL_WINDOWS[2 * half + 1]
        run = u[:, half * LANE:(half + 1) * LANE]
        sums = {}
        for j in range(1, w_big):
            if j == w_small:
                sums[w_small] = run
            run = run + ext[pl.ds(POOL_HALO - j, tile), lanes]
        sums[w_big] = run
        lane = lax.broadcasted_iota(jnp.int32, (1, LANE), 1)
        small = lane < POOL_GROUP
        total = jnp.where(small, sums[w_small], sums[w_big])
        count = jnp.where(small, jnp.minimum(pos, float(w_small)), jnp.minimum(pos, float(w_big)))
        halves.append(total / count)
    mean = jnp.concatenate(halves, axis=1)
    d = (mean - u).astype(BF16)
    y = _dot(d, w_ref[...]) * scale_ref[...]
    o_ref[...] = y.astype(o_ref.dtype)
    ext[pl.ds(0, POOL_HALO), :] = u[tile - POOL_HALO:, :]


def pool_mixer(proj, w_pool, scale, tile=512):
    s = proj.shape[0]
    tile = min(tile, s)
    ng = len(POOL_WINDOWS)
    w_bd = jnp.zeros((BRANCH_W, BRANCH_W), F32)
    for gi in range(ng):
        lo = gi * POOL_GROUP
        w_bd = w_bd.at[lo:lo + POOL_GROUP, lo:lo + POOL_GROUP].set(w_pool[gi].astype(F32))
    return pl.pallas_call(
        functools.partial(_pool_kernel, tile=tile),
        grid=(s // tile,),
        in_specs=[pl.BlockSpec((tile, BRANCH_W), lambda i: (i, 7)),
                  _const_spec((BRANCH_W, BRANCH_W)), _const_spec((1, BRANCH_W))],
        out_specs=pl.BlockSpec((tile, BRANCH_W), lambda i: (i, 0)),
        out_shape=jax.ShapeDtypeStruct((s, BRANCH_W), BF16),
        scratch_shapes=[pltpu.VMEM((tile + POOL_HALO, BRANCH_W), F32)],
        compiler_params=_params("arbitrary"),
        name="pool_mixer",
    )(proj, w_bd.astype(BF16), scale.reshape(1, BRANCH_W).astype(F32))


def _ret_kernel(q_ref, k_ref, v_ref, g_ref, cos_ref, sin_ref, perm_ref, ones_ref, hm_ref,
                dstack_ref, xi_ref, zeta_ref, gc_ref, gng_ref, gnb_ref, o_ref, st_ref, *, tile):
    c = RET_CHUNK

    @pl.when(pl.program_id(0) == 0)
    def _():
        st_ref[...] = jnp.zeros_like(st_ref)

    perm = perm_ref[...]
    ones_bd = ones_ref[...]
    bd_mask = ones_bd.astype(F32)
    hm = hm_ref[...]

    def chunk(ci, carry):
        r0 = pl.multiple_of(ci * c, c)
        qb = q_ref[pl.ds(r0, c), :]
        kb = k_ref[pl.ds(r0, c), :]
        v = v_ref[pl.ds(r0, c), :]
        g = g_ref[pl.ds(r0, c), :].astype(F32)
        cos = cos_ref[pl.ds(r0, c), :]
        sin = sin_ref[pl.ds(r0, c), :]
        qr = qb.astype(F32) * cos + _dot(qb, perm) * sin
        kr = (kb.astype(F32) * cos + _dot(kb, perm) * sin) * (HEAD_DIM ** -0.5)

        qx = jnp.concatenate([qr * hm[h:h + 1, :] for h in range(N_HEADS)], axis=0).astype(BF16)
        sc = _dot_nt(qx, kr.astype(BF16)) * dstack_ref[...]
        r = _dot(sc.astype(BF16), v)
        intra = jnp.zeros((c, BRANCH_W), F32)
        for h in range(N_HEADS):
            intra = intra + r[h * c:(h + 1) * c, :] * hm[h:h + 1, :]

        st = st_ref[...]
        inter = _dot_nt((qr * xi_ref[...]).astype(BF16), st.astype(BF16))
        upd = _dot(v.astype(F32).T.astype(BF16), (kr * zeta_ref[...]).astype(BF16))
        st_ref[...] = st * gc_ref[...] + upd * bd_mask

        o = intra + inter
        mu = _dot_exact_rhs(o, ones_bd) * (1.0 / HEAD_DIM)
        cen = o - mu
        var = _dot_exact_rhs(cen * cen, ones_bd) * (1.0 / HEAD_DIM)
        y = cen * lax.rsqrt(var + LN_EPS) * gng_ref[...] + gnb_ref[...]
        o_ref[pl.ds(r0, c), :] = (y * _silu(g)).astype(o_ref.dtype)
        return carry

    lax.fori_loop(0, tile // c, chunk, 0)


def _rope_tables(s):
    half = HEAD_DIM // 2
    pos = jnp.arange(s, dtype=F32)
    inv_freq = ROPE_BASE ** (-jnp.arange(half, dtype=F32) / half)
    ang = pos[:, None] * inv_freq[None, :]
    cos, sin = jnp.cos(ang), jnp.sin(ang)
    cos_t = jnp.tile(jnp.concatenate([cos, cos], axis=1), (1, N_HEADS))
    sin_t = jnp.tile(jnp.concatenate([-sin, sin], axis=1), (1, N_HEADS))
    return cos_t, sin_t


def _ret_constants():
    c = RET_CHUNK
    half = HEAD_DIM // 2
    lane = jnp.arange(BRANCH_W)
    partner = jnp.where(lane % HEAD_DIM < half, lane + half, lane - half)
    perm = (lane[:, None] == partner[None, :]).astype(BF16)
    log_gamma = jnp.log1p(-jnp.exp2(-RET_DECAY_BASE - jnp.arange(N_HEADS, dtype=F32)))
    ci = jnp.arange(c, dtype=F32)
    diff = ci[:, None] - ci[None, :]
    intra = jnp.where(diff >= 0, jnp.exp(diff * log_gamma[:, None, None]), 0.0)
    dstack = intra.reshape(N_HEADS * c, c)
    lg_lane = jnp.repeat(log_gamma, HEAD_DIM)[None, :]
    xi = jnp.exp((ci[:, None] + 1.0) * lg_lane)
    zeta = jnp.exp((c - 1.0 - ci[:, None]) * lg_lane)
    gc = jnp.exp(c * lg_lane)
    return perm, dstack, xi, zeta, gc


def retention(proj, gn_g, gn_b, tile=512):
    s = proj.shape[0]
    c = RET_CHUNK
    tile = min(tile, s)
    cos_t, sin_t = _rope_tables(s)
    perm, dstack, xi, zeta, gc = _ret_constants()
    col = lambda j: pl.BlockSpec((tile, BRANCH_W), lambda i, j=j: (i, j))
    row = pl.BlockSpec((tile, BRANCH_W), lambda i: (i, 0))
    return pl.pallas_call(
        functools.partial(_ret_kernel, tile=tile),
        grid=(s // tile,),
        in_specs=[col(8), col(9), col(10), col(11), row, row,
                  _const_spec((BRANCH_W, BRANCH_W)), _const_spec((BRANCH_W, BRANCH_W)),
                  _const_spec((N_HEADS, BRANCH_W)), _const_spec((N_HEADS * c, c)),
                  _const_spec((c, BRANCH_W)), _const_spec((c, BRANCH_W)), _const_spec((1, BRANCH_W)),
                  _const_spec((1, BRANCH_W)), _const_spec((1, BRANCH_W))],
        out_specs=row,
        out_shape=jax.ShapeDtypeStruct((s, BRANCH_W), BF16),
        scratch_shapes=[pltpu.VMEM((BRANCH_W, BRANCH_W), F32)],
        compiler_params=_params("arbitrary"),
        name="retention",
    )(proj, proj, proj, proj, cos_t, sin_t, perm, _head_ones(), _head_masks(), dstack, xi, zeta, gc,
      gn_g.reshape(1, BRANCH_W).astype(F32), gn_b.reshape(1, BRANCH_W).astype(F32))


def _merge_kernel(h_ref, o0_ref, o1_ref, o2_ref, o3_ref, x_ref, wg_ref, wb_ref, wo_ref, g_ref,
                  xo_ref, ho_ref):
    h = h_ref[...]
    merged = jnp.zeros(x_ref.shape, F32)
    for bi, o_ref in enumerate((o0_ref, o1_ref, o2_ref, o3_ref)):
        gate = _sigmoid(_dot(h, wg_ref[:, bi * D_MODEL:(bi + 1) * D_MODEL]))
        merged = merged + gate * _dot(o_ref[...], wb_ref[bi])
    x_new = x_ref[...] + _dot(merged.astype(BF16), wo_ref[...])
    xo_ref[...] = x_new
    ho_ref[...] = _rms(x_new, g_ref[...]).astype(ho_ref.dtype)


def merge(h, branches, x, w_gate, w_branch, w_out, next_gain, tm=512):
    s, d = x.shape
    tm = min(tm, s)
    row = lambda w: pl.BlockSpec((tm, w), lambda i: (i, 0))
    return pl.pallas_call(
        _merge_kernel,
        grid=(s // tm,),
        in_specs=[row(d), row(BRANCH_W), row(BRANCH_W), row(BRANCH_W), row(BRANCH_W), row(d),
                  _const_spec((d, N_BRANCH * d)), _const_spec((N_BRANCH, BRANCH_W, d)),
                  _const_spec((d, d)), _const_spec((1, d))],
        out_specs=[row(d), row(d)],
        out_shape=[jax.ShapeDtypeStruct((s, d), F32), jax.ShapeDtypeStruct((s, d), BF16)],
        compiler_params=_params("parallel"),
        name="merge",
    )(h, *branches, x, w_gate, w_branch, w_out, next_gain.reshape(1, d).astype(F32))


def _ffn_kernel(h_ref, x_ref, wg_ref, wu_ref, wd_ref, g_ref, xo_ref, ho_ref, acc_ref):
    f = pl.program_id(1)

    @pl.when(f == 0)
    def _():
        acc_ref[...] = jnp.zeros_like(acc_ref)

    h = h_ref[...]
    a = _silu(_dot(h, wg_ref[...])) * _dot(h, wu_ref[...])
    acc_ref[...] += _dot(a.astype(BF16), wd_ref[...])

    @pl.when(f == pl.num_programs(1) - 1)
    def _():
        x_new = x_ref[...] + acc_ref[...]
        xo_ref[...] = x_new
        ho_ref[...] = _rms(x_new, g_ref[...]).astype(ho_ref.dtype)


def ffn_dense(h, x, w_gate, w_up, w_down, next_gain, tm=1024, tf=512):
    s, d = x.shape
    tm = min(tm, s)
    dff = w_gate.shape[1]
    row = lambda: pl.BlockSpec((tm, d), lambda i, f: (i, 0))
    return pl.pallas_call(
        _ffn_kernel,
        grid=(s // tm, dff // tf),
        in_specs=[row(), row(),
                  pl.BlockSpec((d, tf), lambda i, f: (0, f)),
                  pl.BlockSpec((d, tf), lambda i, f: (0, f)),
                  pl.BlockSpec((tf, d), lambda i, f: (f, 0)),
                  _const_spec((1, d))],
        out_specs=[row(), row()],
        out_shape=[jax.ShapeDtypeStruct((s, d), F32), jax.ShapeDtypeStruct((s, d), BF16)],
        scratch_shapes=[pltpu.VMEM((tm, d), F32)],
        compiler_params=_params("parallel", "arbitrary"),
        name="ffn_dense",
    )(h, x, w_gate, w_up, w_down, next_gain.reshape(1, d).astype(F32))


MOE_TOK_TILE = 256
MOE_ROW_BLOCK = 512
MOE_Y_BLOCK = 256


def _dot_f32(x, w):
    xh, xm, _ = _split3(x)
    wh, wm, _ = _split3(w)
    return _dot(xh, wh) + (_dot(xh, wm) + _dot(xm, wh))


def _route_kernel(x_ref, ng_ref, router_ref, ltri_ref, pos_ref, gate_ref, post_ref, before_ref, total_ref,
                  carry_ref):
    tm = x_ref.shape[0]

    @pl.when(pl.program_id(0) == 0)
    def _():
        carry_ref[...] = jnp.zeros_like(carry_ref)

    hn = _rms(x_ref[...], ng_ref[...])
    logits = _dot_f32(hn, router_ref[...])
    lane = lax.broadcasted_iota(jnp.int32, (tm, LANE), 1)
    logits = jnp.where(lane < N_EXPERTS, logits, -jnp.inf)
    v1 = jnp.max(logits, axis=1, keepdims=True)
    i1 = jnp.min(jnp.where(logits == v1, lane, LANE), axis=1, keepdims=True)
    rest = jnp.where(lane == i1, -jnp.inf, logits)
    v2 = jnp.max(rest, axis=1, keepdims=True)
    i2 = jnp.min(jnp.where(rest == v2, lane, LANE), axis=1, keepdims=True)
    w1 = 1.0 / (1.0 + jnp.exp(v2 - v1))
    gate_ref[...] = jnp.where(lane == i1, w1, 0.0) + jnp.where(lane == i2, 1.0 - w1, 0.0)

    member = jnp.where((lane == i1) | (lane == i2), 1.0, 0.0)
    carry = carry_ref[...]
    rank = _dot(ltri_ref[...], member.astype(BF16)) + carry
    pos = jnp.where(member > 0.0, rank, -1.0)
    pos_ref[...] = pos
    post_ref[...] = pos.T[:SUBLANE, :]
    before_ref[0] = carry
    carry = carry + jnp.sum(member, axis=0, keepdims=True)
    carry_ref[...] = carry
    total_ref[...] = carry


def moe_route(x, norm_gain, router):
    s, d = x.shape
    tm = min(MOE_TOK_TILE, s)
    nt = s // tm
    router_p = jnp.zeros((d, LANE), F32).at[:, :N_EXPERTS].set(router.astype(F32))
    ltri = jnp.tril(jnp.ones((tm, tm), F32), -1).astype(BF16)
    row = pl.BlockSpec((tm, LANE), lambda i: (i, 0))
    return pl.pallas_call(
        _route_kernel,
        grid=(nt,),
        in_specs=[pl.BlockSpec((tm, d), lambda i: (i, 0)), _const_spec((1, d)), _const_spec((d, LANE)),
                  _const_spec((tm, tm))],
        out_specs=[row, row, pl.BlockSpec((SUBLANE, tm), lambda i: (0, i)),
                   pl.BlockSpec((1, 1, LANE), lambda i: (i, 0, 0)), pl.BlockSpec((1, LANE), lambda i: (0, 0))],
        out_shape=[jax.ShapeDtypeStruct((s, LANE), F32), jax.ShapeDtypeStruct((s, LANE), F32),
                   jax.ShapeDtypeStruct((SUBLANE, s), F32), jax.ShapeDtypeStruct((nt, 1, LANE), F32),
                   jax.ShapeDtypeStruct((1, LANE), F32)],
        scratch_shapes=[pltpu.VMEM((1, LANE), F32)],
        compiler_params=_params("arbitrary"),
        name="moe_route",
    )(x, norm_gain.reshape(1, d).astype(F32), router_p, ltri)


def _moe_ffn_kernel(be_ref, r0_ref, tlo_ref, thi_ref, nv_ref, h_ref, post_ref, wg_ref, wu_ref, wd_ref,
                    y_ref, x_sc, acc_sc, *, tt):
    b = pl.program_id(0)
    f = pl.program_id(1)
    valid = b < nv_ref[0]
    tmb = x_sc.shape[0]

    @pl.when(valid & (f == 0))
    def _():
        e = be_ref[b]
        want = lax.broadcasted_iota(jnp.int32, (tmb, 1), 0).astype(F32) + r0_ref[b].astype(F32)
        acc_sc[...] = jnp.zeros_like(acc_sc)

        def gather(t, carry):
            t0 = pl.multiple_of(t * tt, tt)
            p = post_ref[pl.ds(e, 1), pl.ds(t0, tt)]
            sel = jnp.where(p == want, 1.0, 0.0).astype(BF16)
            acc_sc[...] += _dot(sel, h_ref[pl.ds(t0, tt), :])
            return carry

        lax.fori_loop(tlo_ref[b], thi_ref[b] + 1, gather, 0)
        x_sc[...] = acc_sc[...].astype(BF16)
        acc_sc[...] = jnp.zeros_like(acc_sc)

    @pl.when(valid)
    def _():
        xb = x_sc[...]
        a = _silu(_dot(xb, wg_ref[0])) * _dot(xb, wu_ref[0])
        acc_sc[...] += _dot(a.astype(BF16), wd_ref[0])

    @pl.when(f == pl.num_programs(1) - 1)
    def _():
        y_ref[...] = jnp.where(valid, acc_sc[...], 0.0).astype(y_ref.dtype)


def moe_ffn(h, post, sched, w_gate, w_up, w_down, tf=512):
    s, d = h.shape
    ne, _, dff = w_gate.shape
    tmb = MOE_ROW_BLOCK
    tt = min(MOE_TOK_TILE, s)
    nb = sched[0].shape[0]
    nf = dff // tf

    def wspec(shape, fn):
        return pl.BlockSpec(shape, fn)

    def fidx(b, f, nv):
        return jnp.where(b < nv[0], f, nf - 1)

    grid_spec = pltpu.PrefetchScalarGridSpec(
        num_scalar_prefetch=5,
        grid=(nb, nf),
        in_specs=[pl.BlockSpec((s, d), lambda b, f, *_: (0, 0), pipeline_mode=pl.Buffered(1)),
                  pl.BlockSpec((SUBLANE, s), lambda b, f, *_: (0, 0), pipeline_mode=pl.Buffered(1)),
                  wspec((1, d, tf), lambda b, f, be, r0, tlo, thi, nv: (be[b], 0, fidx(b, f, nv))),
                  wspec((1, d, tf), lambda b, f, be, r0, tlo, thi, nv: (be[b], 0, fidx(b, f, nv))),
                  wspec((1, tf, d), lambda b, f, be, r0, tlo, thi, nv: (be[b], fidx(b, f, nv), 0))],
        out_specs=pl.BlockSpec((tmb, d), lambda b, f, *_: (b, 0)),
        scratch_shapes=[pltpu.VMEM((tmb, d), BF16), pltpu.VMEM((tmb, d), F32)],
    )
    return pl.pallas_call(
        functools.partial(_moe_ffn_kernel, tt=tt),
        grid_spec=grid_spec,
        out_shape=jax.ShapeDtypeStruct((nb * tmb, d), BF16),
        compiler_params=_params("arbitrary", "arbitrary"),
        name="moe_ffn",
    )(*sched, h, post, w_gate, w_up, w_down)


def _moe_combine_kernel(kb_ref, off_ref, x_ref, pos_ref, gate_ref, fg_ref, *rest):
    y_refs, o_ref = rest[:-1], rest[-1]
    t = pl.program_id(0)
    tm = x_ref.shape[0]
    yb = y_refs[0].shape[0]
    pos = pos_ref[...]
    gate = gate_ref[...]
    lane = lax.broadcasted_iota(jnp.int32, (tm, LANE), 1)
    col = lax.broadcasted_iota(jnp.int32, (1, yb), 1).astype(F32)
    acc = x_ref[...]
    for e in range(N_EXPERTS):
        pe = jnp.sum(jnp.where(lane == e, pos, 0.0), axis=1, keepdims=True)
        ge = jnp.sum(jnp.where(lane == e, gate, 0.0), axis=1, keepdims=True)
        r = jnp.where(pe >= 0.0, pe + off_ref[t * N_EXPERTS + e].astype(F32), -1.0)
        z = (_dot(jnp.where(r == col, 1.0, 0.0).astype(BF16), y_refs[2 * e][...])
             + _dot(jnp.where(r == col + float(yb), 1.0, 0.0).astype(BF16), y_refs[2 * e + 1][...]))
        acc = acc + ge * z
    o_ref[...] = _rms(acc, fg_ref[...]).astype(o_ref.dtype)


def moe_combine(x, pos, gate, y, kb, off, final_gain):
    s, d = x.shape
    tm = min(MOE_TOK_TILE, s)
    yb = MOE_Y_BLOCK
    last = y.shape[0] // yb - 1

    def yspec(e, k):
        return pl.BlockSpec((yb, d), lambda t, kb_r, off_r: (jnp.minimum(kb_r[t * N_EXPERTS + e] + k, last), 0))

    grid_spec = pltpu.PrefetchScalarGridSpec(
        num_scalar_prefetch=2,
        grid=(s // tm,),
        in_specs=[pl.BlockSpec((tm, d), lambda t, *_: (t, 0)),
                  pl.BlockSpec((tm, LANE), lambda t, *_: (t, 0)),
                  pl.BlockSpec((tm, LANE), lambda t, *_: (t, 0)),
                  pl.BlockSpec((1, d), lambda t, *_: (0, 0))]
        + [yspec(e, k) for e in range(N_EXPERTS) for k in range(2)],
        out_specs=pl.BlockSpec((tm, d), lambda t, *_: (t, 0)),
    )
    return pl.pallas_call(
        _moe_combine_kernel,
        grid_spec=grid_spec,
        out_shape=jax.ShapeDtypeStruct((s, d), F32),
        compiler_params=_params("arbitrary"),
        name="moe_combine",
    )(kb, off, x, pos, gate, final_gain.reshape(1, d).astype(F32), *([y] * (2 * N_EXPERTS)))


def _moe_schedule(before, total, s):
    tmb, yb = MOE_ROW_BLOCK, MOE_Y_BLOCK
    nb = 2 * s // tmb + N_EXPERTS
    counts = total[0, :N_EXPERTS].astype(jnp.int32)
    nblk = (counts + tmb - 1) // tmb
    end = jnp.cumsum(nblk)
    first = end - nblk
    nvalid = end[-1]
    b = jnp.minimum(jnp.arange(nb, dtype=jnp.int32), nvalid - 1)
    blk_e = jnp.sum(b[:, None] >= end[None, :], axis=1).astype(jnp.int32)
    r0 = (b - first[blk_e]) * tmb
    cb = before[:, 0, :N_EXPERTS].astype(jnp.int32)
    cbe = cb[:, blk_e]
    tlo = (jnp.sum(cbe <= r0[None, :], axis=0) - 1).astype(jnp.int32)
    thi = (jnp.sum(cbe < (r0 + tmb)[None, :], axis=0) - 1).astype(jnp.int32)
    sched = (blk_e, r0.astype(jnp.int32), tlo, thi, nvalid.reshape(1).astype(jnp.int32))
    row_start = first[None, :] * tmb + cb
    kb = row_start // yb
    off = first[None, :] * tmb - kb * yb
    return sched, kb.reshape(-1).astype(jnp.int32), off.reshape(-1).astype(jnp.int32)


def moe_sparse(h, x, norm_gain, router, w_gate, w_up, w_down, final_gain):
    s, _ = x.shape
    pos, gate, post, before, total = moe_route(x, norm_gain, router)
    sched, kb, off = _moe_schedule(before, total, s)
    y = moe_ffn(h, post, sched, w_gate, w_up, w_down)
    return moe_combine(x, pos, gate, y, kb, off, final_gain)


FOX_F0 = 7 * BRANCH_W


def _split_w_in_kernel(w_ref, mix_ref, f_ref, gate_ref):
    w = w_ref[...]
    rows = w.shape[0]
    mix_ref[:, :FOX_F0] = w[:, :FOX_F0].astype(BF16)
    mix_ref[:, FOX_F0:] = w[:, FOX_F0 + N_HEADS:N_MIX_COLS + N_HEADS].astype(BF16)
    f_ref[...] = jnp.concatenate([w[:, FOX_F0:FOX_F0 + N_HEADS], jnp.zeros((rows, LANE - N_HEADS), F32)],
                                 axis=1).astype(BF16)
    gate_ref[...] = w[:, N_MIX_COLS + N_HEADS:].astype(BF16)


def _mixer_weights(w_in, layer, tr=128):
    _, d, cols = w_in.shape
    row = lambda w: pl.BlockSpec((tr, w), lambda i: (i, 0))
    return pl.pallas_call(
        _split_w_in_kernel,
        grid=(d // tr,),
        in_specs=[pl.BlockSpec((None, tr, cols), lambda i: (layer, i, 0))],
        out_specs=[row(N_MIX_COLS), row(LANE), row(N_BRANCH * D_MODEL)],
        out_shape=[jax.ShapeDtypeStruct((d, N_MIX_COLS), BF16), jax.ShapeDtypeStruct((d, LANE), BF16),
                   jax.ShapeDtypeStruct((d, N_BRANCH * D_MODEL), BF16)],
        compiler_params=_params("parallel"),
        name="split_w_in",
    )(w_in)


def kernel(x, w_in, w_branch, w_out, norm_mix_g, hgrn_lb_logits, hgrn_norm_g, fox_f_bias, pool_w, pool_scale,
           ret_gn_g, ret_gn_b, norm_ffn_g, ffn_w_gate, ffn_w_up, ffn_w_down, moe_router, moe_w_gate, moe_w_up,
           moe_w_down, final_norm_g):
    b, s, d = x.shape
    assert b == 1 and d == D_MODEL
    depth = w_in.shape[0]
    assert depth == 2, "layer 0 uses the dense FFN, layer 1 the experts and the final norm"
    xs = x.reshape(s, d)
    h = rmsnorm_bf16(xs, norm_mix_g[0])
    out = None
    for layer in range(depth):
        w_mix, w_f, w_gate = _mixer_weights(w_in, layer)
        proj = matmul(h, w_mix)
        ct, fox_stats = fox_gate(h, proj, w_f, fox_f_bias[layer])
        branches = (
            hgrn2(proj, hgrn_lb_logits, hgrn_norm_g[layer], layer),
            fox_attention(proj, ct, fox_stats),
            pool_mixer(proj, pool_w[layer], pool_scale[layer]),
            retention(proj, ret_gn_g[layer], ret_gn_b[layer]),
        )
        xs, h2 = merge(h, branches, xs, w_gate, w_branch[layer].astype(BF16), w_out[layer].astype(BF16),
                       norm_ffn_g[layer])
        if layer % 2 == 0:
            li = layer // 2
            xs, h = ffn_dense(h2, xs, ffn_w_gate[li].astype(BF16), ffn_w_up[li].astype(BF16),
                              ffn_w_down[li].astype(BF16), norm_mix_g[layer + 1])
        else:
            li = layer // 2
            out = moe_sparse(h2, xs, norm_ffn_g[layer], moe_router[li], moe_w_gate[li].astype(BF16),
                             moe_w_up[li].astype(BF16), moe_w_down[li].astype(BF16), final_norm_g)
    return out.reshape(b, s, d)
```

```python
import functools
import math

import jax
import jax.numpy as jnp
from jax import lax
from jax.experimental import pallas as pl
from jax.experimental.pallas import tpu as pltpu

D_MODEL = 1024
N_BRANCH = 4
BRANCH_W = D_MODEL // N_BRANCH
HEAD_DIM = 64
N_HEADS = BRANCH_W // HEAD_DIM
POOL_WINDOWS = (2, 4, 8, 16)
POOL_GROUP = BRANCH_W // len(POOL_WINDOWS)
POOL_HALO = 16
RET_DECAY_BASE = 5.0
ROPE_BASE = 10000.0
D_FF = 7 * D_MODEL // 2
N_EXPERTS = 8
RMS_EPS = 1e-6
LN_EPS = 1e-5
N_MIX_COLS = 12 * BRANCH_W

LANE = 128
SUBLANE = 8
VMEM_LIMIT = 56 * 1024 * 1024

HG_CHUNK = 64
HG_SUB = 16
HG_FAST_MIN_LOGDECAY = -60.0
RET_CHUNK = 256

F32 = jnp.float32
BF16 = jnp.bfloat16
NT_DIMS = (((1,), (1,)), ((), ()))


def _params(*sem):
    return pltpu.CompilerParams(dimension_semantics=sem, vmem_limit_bytes=VMEM_LIMIT)


def _const_spec(shape):
    nd = len(shape)
    return pl.BlockSpec(shape, lambda *_: (0,) * nd, pipeline_mode=pl.Buffered(1))


def _split3(x):
    hi = x.astype(BF16)
    r1 = x - hi.astype(F32)
    mid = r1.astype(BF16)
    lo = (r1 - mid.astype(F32)).astype(BF16)
    return hi, mid, lo


def _dot(a, b):
    return jnp.dot(a, b, preferred_element_type=F32)


def _dot_nt(a, b):
    return lax.dot_general(a, b, NT_DIMS, preferred_element_type=F32)


def _dot_exact_rhs(x, m_bf16):
    hi, mid, lo = _split3(x)
    return _dot(hi, m_bf16) + _dot(mid, m_bf16) + _dot(lo, m_bf16)


def _dot_exact_lhs(m_bf16, x):
    hi, mid, lo = _split3(x)
    return _dot(m_bf16, hi) + _dot(m_bf16, mid) + _dot(m_bf16, lo)


def _sigmoid(x):
    return 1.0 / (1.0 + jnp.exp(-x))


def _silu(x):
    return x * _sigmoid(x)


def _rms(x, gain):
    return x * lax.rsqrt(jnp.mean(x * x, axis=-1, keepdims=True) + RMS_EPS) * gain


def _head_of(n):
    return jnp.arange(n) // HEAD_DIM


def _head_ones():
    h = _head_of(BRANCH_W)
    return (h[:, None] == h[None, :]).astype(BF16)


def _head_masks():
    return (_head_of(BRANCH_W)[None, :] == jnp.arange(N_HEADS)[:, None]).astype(F32)


def _rmsnorm_kernel(x_ref, g_ref, o_ref):
    o_ref[...] = _rms(x_ref[...], g_ref[...]).astype(o_ref.dtype)


def rmsnorm_bf16(x, gain, tm=1024):
    s, d = x.shape
    tm = min(tm, s)
    return pl.pallas_call(
        _rmsnorm_kernel,
        grid=(s // tm,),
        in_specs=[pl.BlockSpec((tm, d), lambda i: (i, 0)), _const_spec((1, d))],
        out_specs=pl.BlockSpec((tm, d), lambda i: (i, 0)),
        out_shape=jax.ShapeDtypeStruct((s, d), BF16),
        compiler_params=_params("parallel"),
        name="rmsnorm",
    )(x, gain.reshape(1, d))


def _matmul_kernel(a_ref, b_ref, o_ref):
    o_ref[...] = _dot(a_ref[...], b_ref[...]).astype(o_ref.dtype)


def matmul(a, b, out_dtype=BF16, tm=2048, tn=1024):
    m, k = a.shape
    _, n = b.shape
    tm = min(tm, m)
    return pl.pallas_call(
        _matmul_kernel,
        grid=(n // tn, m // tm),
        in_specs=[pl.BlockSpec((tm, k), lambda j, i: (i, 0)),
                  pl.BlockSpec((k, tn), lambda j, i: (0, j))],
        out_specs=pl.BlockSpec((tm, tn), lambda j, i: (i, j)),
        out_shape=jax.ShapeDtypeStruct((m, n), out_dtype),
        compiler_params=_params("parallel", "parallel"),
        name="in_proj",
    )(a, b)


def _hgrn_kernel(q_ref, f_ref, i_ref, g_ref, lbl_ref, ng_ref, ones_ref, hm_ref, tril_ref, halfsum_ref, fmask_ref,
                 o_ref, st_ref, bpad, kpad, vpad, astack, lf_sc, kk_sc, *, layer, tile):
    c, sub = HG_CHUNK, HG_SUB
    nsub = c // sub
    half = c // 2

    @pl.when(pl.program_id(0) == 0)
    def _():
        st_ref[...] = jnp.zeros_like(st_ref)
        bpad[...] = jnp.zeros_like(bpad)
        kpad[...] = jnp.zeros_like(kpad)
        vpad[...] = jnp.zeros_like(vpad)

    lbl = lbl_ref[...]
    e = jnp.exp(lbl - jnp.max(lbl, axis=0, keepdims=True))
    p = e / jnp.sum(e, axis=0, keepdims=True)
    lb = jnp.zeros((1, BRANCH_W), F32)
    for l in range(1, layer + 1):
        lb = lb + p[l:l + 1, :]

    ones_bd = ones_ref[...]
    hm = hm_ref[...]
    tril = tril_ref[...]
    row = lax.broadcasted_iota(jnp.int32, (c, 1), 0)
    row_in_sub = row % sub
    bd_mask = ones_bd.astype(F32)

    sig = _sigmoid(f_ref[...].astype(F32))
    logf_all = jnp.log(lb + (1.0 - lb) * sig)
    lf_sc[...] = logf_all
    kk_sc[...] = (1.0 - lb) * (1.0 - sig)
    min_decay = jnp.min(_dot(halfsum_ref[...], logf_all.astype(BF16)))

    def load(ci):
        r0 = pl.multiple_of(ci * c, c)
        q = q_ref[pl.ds(r0, c), :].astype(F32)
        v = i_ref[pl.ds(r0, c), :].astype(F32)
        kk = kk_sc[pl.ds(r0, c), :]
        b = _dot_exact_lhs(tril, lf_sc[pl.ds(r0, c), :])
        return r0, q, v, kk, b

    def finish(r0, q_decayed, v, kk, b, intra):
        st = st_ref[...]
        inter = _dot_nt(q_decayed.astype(BF16), st.astype(BF16))
        b_last = b[c - 1:c, :]
        ks_end = (kk * jnp.exp(b_last - b)).astype(BF16)
        upd = _dot(v.T.astype(BF16), ks_end)
        st_ref[...] = st * jnp.exp(b_last) + upd * bd_mask
        o = intra + inter
        ms = _dot_exact_rhs(o * o, ones_bd) * (1.0 / HEAD_DIM)
        g = g_ref[pl.ds(r0, c), :].astype(F32)
        y = o * lax.rsqrt(ms + RMS_EPS) * ng_ref[...] * _silu(g)
        o_ref[pl.ds(r0, c), :] = y.astype(o_ref.dtype)

    def fast_chunk(ci, carry):
        r0, q, v, kk, b = load(ci)
        second = row >= half
        m_row = b[half - 1:half, :]
        mref = jnp.where(second, m_row, 0.0)
        qp = q * jnp.exp(b - mref)
        kp = kk * jnp.exp(mref - b)
        e_m = jnp.exp(m_row)
        kaug = jnp.concatenate([kp, kp[:half, :] * e_m], axis=0).astype(BF16)
        vaug = jnp.concatenate([v, v[:half, :]], axis=0).astype(BF16)
        qx = jnp.concatenate([qp * hm[h:h + 1, :] for h in range(N_HEADS)], axis=0).astype(BF16)
        sc = jnp.where(fmask_ref[...] > 0.0, _dot_nt(qx, kaug), 0.0)
        r = _dot(sc.astype(BF16), vaug)
        intra = jnp.zeros((c, BRANCH_W), F32)
        for h in range(N_HEADS):
            intra = intra + r[h * c:(h + 1) * c, :] * hm[h:h + 1, :]
        finish(r0, jnp.where(second, qp * e_m, qp), v, kk, b, intra)
        return carry

    def exact_chunk(ci, carry):
        r0, q, v, kk, b = load(ci)

        bpad[pl.ds(sub, c), :] = b
        kpad[pl.ds(sub, c), :] = kk
        vpad[pl.ds(sub, c), :] = v

        for d in range(sub):
            b_d = bpad[pl.ds(sub - d, c), :]
            k_d = kpad[pl.ds(sub - d, c), :]
            a = jnp.where(row_in_sub >= d, q * k_d * jnp.exp(b - b_d), 0.0)
            astack[pl.ds(d * c, c), :] = a.astype(BF16)
        pall = _dot(astack[...], ones_bd)
        intra = jnp.zeros((c, BRANCH_W), F32)
        for d in range(sub):
            intra = intra + pall[d * c:(d + 1) * c, :] * vpad[pl.ds(sub - d, c), :]

        pieces = [jnp.zeros((sub, BRANCH_W), F32)]
        for si in range(1, nsub):
            lo = si * sub
            m_i = b[lo - 1:lo, :]
            qs = q[lo:lo + sub, :] * jnp.exp(b[lo:lo + sub, :] - m_i)
            ks = (kk[:lo, :] * jnp.exp(m_i - b[:lo, :])).astype(BF16)
            qx = jnp.concatenate([qs * hm[h:h + 1, :] for h in range(N_HEADS)], axis=0).astype(BF16)
            sc = _dot_nt(qx, ks)
            r = _dot(sc.astype(BF16), v[:lo, :].astype(BF16))
            acc = jnp.zeros((sub, BRANCH_W), F32)
            for h in range(N_HEADS):
                acc = acc + r[h * sub:(h + 1) * sub, :] * hm[h:h + 1, :]
            pieces.append(acc)
        intra = intra + jnp.concatenate(pieces, axis=0)
        finish(r0, q * jnp.exp(b), v, kk, b, intra)
        return carry

    lax.cond(min_decay >= HG_FAST_MIN_LOGDECAY,
             lambda: lax.fori_loop(0, tile // c, fast_chunk, 0, unroll=2),
             lambda: lax.fori_loop(0, tile // c, exact_chunk, 0))


def hgrn2(proj, lb_logits, norm_g, layer, tile=512):
    s = proj.shape[0]
    depth = lb_logits.shape[0]
    c, sub = HG_CHUNK, HG_SUB
    half = c // 2
    tile = min(tile, s)
    col = lambda j: pl.BlockSpec((tile, BRANCH_W), lambda i, j=j: (i, j))
    tril = jnp.tril(jnp.ones((c, c), F32)).astype(BF16)
    nhalf = tile // half
    halfsum = (jnp.arange(tile)[None, :] // half == jnp.arange(nhalf)[:, None]).astype(BF16)
    t = jnp.arange(c)[:, None]
    col_s = jnp.arange(c + half)[None, :]
    same_half = (col_s < c) & (col_s // half == t // half) & (col_s <= t)
    cross = (col_s >= c) & (t >= half)
    fmask = jnp.tile((same_half | cross).astype(F32), (N_HEADS, 1))
    return pl.pallas_call(
        functools.partial(_hgrn_kernel, layer=layer, tile=tile),
        grid=(s // tile,),
        in_specs=[col(0), col(1), col(2), col(3),
                  _const_spec((depth, BRANCH_W)), _const_spec((1, BRANCH_W)),
                  _const_spec((BRANCH_W, BRANCH_W)), _const_spec((N_HEADS, BRANCH_W)),
                  _const_spec((c, c)), _const_spec((nhalf, tile)), _const_spec((N_HEADS * c, c + half))],
        out_specs=pl.BlockSpec((tile, BRANCH_W), lambda i: (i, 0)),
        out_shape=jax.ShapeDtypeStruct((s, BRANCH_W), BF16),
        scratch_shapes=[pltpu.VMEM((BRANCH_W, BRANCH_W), F32),
                        pltpu.VMEM((c + sub, BRANCH_W), F32),
                        pltpu.VMEM((c + sub, BRANCH_W), F32),
                        pltpu.VMEM((c + sub, BRANCH_W), F32),
                        pltpu.VMEM((sub * c, BRANCH_W), BF16),
                        pltpu.VMEM((tile, BRANCH_W), F32),
                        pltpu.VMEM((tile, BRANCH_W), F32)],
        compiler_params=_params("arbitrary"),
        name="hgrn2",
    )(proj, proj, proj, proj, lb_logits.astype(F32), norm_g.reshape(1, BRANCH_W).astype(F32),
      _head_ones(), _head_masks(), tril, halfsum, fmask)


FOX_TILE = 256
FOX_NSTAT = 16
FOX_SKIP_LOG = 40.0


def _fox_gate_kernel(h_ref, q_ref, k_ref, wf_ref, bias_ref, tril_ref, ones_ref, expand_ref,
                     ct_ref, stat_ref, carry_ref, kmax_ref):
    @pl.when(pl.program_id(0) == 0)
    def _():
        carry_ref[...] = jnp.zeros_like(carry_ref)
        kmax_ref[...] = jnp.zeros_like(kmax_ref)

    logit = _dot(h_ref[...], wf_ref[...]) + bias_ref[...]
    logf = jnp.minimum(logit, 0.0) - jnp.log(1.0 + jnp.exp(-jnp.abs(logit)))
    cum = _dot_exact_lhs(tril_ref[...], logf) + carry_ref[...]
    carry_ref[...] = cum[-1:, :]
    ct_ref[...] = cum.T[:SUBLANE, :]

    ones_bd = ones_ref[...]
    q = q_ref[...].astype(F32)
    k = k_ref[...].astype(F32)
    scale = HEAD_DIM ** -0.5
    c_heads = _dot_exact_rhs(cum, expand_ref[...])
    diag = _dot_exact_rhs(q * k, ones_bd) * scale
    qn = jnp.sqrt(_dot_exact_rhs(q * q, ones_bd)) * scale
    kn = jnp.sqrt(_dot_exact_rhs(k * k, ones_bd))
    kmax = jnp.maximum(kmax_ref[...], jnp.max(kn, axis=0, keepdims=True))
    kmax_ref[...] = kmax
    rows = [jnp.max(qn, axis=0, keepdims=True),
            jnp.max(c_heads - diag, axis=0, keepdims=True),
            kmax,
            c_heads[-1:, :]]
    stat_ref[0] = jnp.concatenate(rows + [jnp.zeros((SUBLANE - len(rows), BRANCH_W), F32)], axis=0)


def fox_gate(h, proj, w_f, f_bias):
    s, d = h.shape
    tile = min(FOX_TILE, s)
    bias = jnp.zeros((1, LANE), F32).at[0, :N_HEADS].set(f_bias.astype(F32))
    tril = jnp.tril(jnp.ones((tile, tile), F32)).astype(BF16)
    expand = (jnp.arange(LANE)[:, None] == _head_of(BRANCH_W)[None, :]).astype(BF16)
    ct, stats = pl.pallas_call(
        _fox_gate_kernel,
        grid=(s // tile,),
        in_specs=[pl.BlockSpec((tile, d), lambda i: (i, 0)),
                  pl.BlockSpec((tile, BRANCH_W), lambda i: (i, 4)),
                  pl.BlockSpec((tile, BRANCH_W), lambda i: (i, 5)),
                  _const_spec((d, LANE)), _const_spec((1, LANE)), _const_spec((tile, tile)),
                  _const_spec((BRANCH_W, BRANCH_W)), _const_spec((LANE, BRANCH_W))],
        out_specs=[pl.BlockSpec((SUBLANE, tile), lambda i: (0, i)),
                   pl.BlockSpec((1, SUBLANE, BRANCH_W), lambda i: (i, 0, 0))],
        out_shape=[jax.ShapeDtypeStruct((SUBLANE, s), F32),
                   jax.ShapeDtypeStruct((s // tile, SUBLANE, BRANCH_W), F32)],
        scratch_shapes=[pltpu.VMEM((1, LANE), F32), pltpu.VMEM((1, BRANCH_W), F32)],
        compiler_params=_params("arbitrary"),
        name="fox_gate",
    )(h, proj, proj, w_f, bias, tril, _head_ones(), expand)
    return ct, stats[:, :4, ::HEAD_DIM].reshape(-1)


def _fox_kernel(stat_ref, q_ref, k_ref, v_ref, ct_ref, hm_ref, o_ref, m_sc, l_sc, acc_sc, *, tq):
    i = pl.program_id(0)
    q0 = pl.multiple_of(i * tq, tq)
    hm = hm_ref[...]
    q = q_ref[...].astype(F32) * (HEAD_DIM ** -0.5)
    qh = [(q * hm[h:h + 1, :]).astype(BF16) for h in range(N_HEADS)]
    c_q0 = ct_ref[:, pl.ds(q0, tq)][:, 0:1]

    first = i
    for h in range(N_HEADS):
        qmax = stat_ref[i * FOX_NSTAT + h]
        emax = stat_ref[i * FOX_NSTAT + N_HEADS + h]

        def needed(j, h=h, qmax=qmax, emax=emax):
            jc = jnp.maximum(j, 0)
            bound = (qmax * stat_ref[jc * FOX_NSTAT + 2 * N_HEADS + h] + emax
                     - stat_ref[jc * FOX_NSTAT + 3 * N_HEADS + h])
            return (j >= 0) & (bound >= -FOX_SKIP_LOG)

        last_dropped = lax.while_loop(needed, lambda j: j - 1, i - 1)
        first = jnp.minimum(first, last_dropped + 1)

    m_sc[...] = jnp.full_like(m_sc, -jnp.inf)
    l_sc[...] = jnp.zeros_like(l_sc)
    acc_sc[...] = jnp.zeros_like(acc_sc)

    def block(s0, diagonal):
        kb = k_ref[pl.ds(s0, tq), :]
        vb = v_ref[pl.ds(s0, tq), :]
        bias = c_q0 - ct_ref[:, pl.ds(s0, tq)]
        for h in range(N_HEADS):
            sc = _dot_nt(qh[h], kb) + bias[h:h + 1, :]
            if diagonal:
                r = lax.broadcasted_iota(jnp.int32, (tq, tq), 0)
                cidx = lax.broadcasted_iota(jnp.int32, (tq, tq), 1)
                sc = jnp.where(cidx <= r, sc, -jnp.inf)
            m_prev = m_sc[h]
            m_new = jnp.maximum(m_prev, jnp.max(sc, axis=1, keepdims=True))
            alpha = jnp.exp(m_prev - m_new)
            p = jnp.exp(sc - jnp.tile(m_new, (1, tq // LANE)))
            l_sc[h] = alpha * l_sc[h] + jnp.sum(p, axis=1, keepdims=True)
            acc_sc[h] = acc_sc[h] * jnp.tile(alpha, (1, BRANCH_W // LANE)) + _dot(p.astype(BF16), vb)
            m_sc[h] = m_new

    def body(j, carry):
        block(pl.multiple_of(j * tq, tq), False)
        return carry

    lax.fori_loop(first, i, body, 0)
    block(q0, True)

    out = jnp.zeros((tq, BRANCH_W), F32)
    for h in range(N_HEADS):
        out = out + acc_sc[h] * hm[h:h + 1, :] / jnp.tile(l_sc[h], (1, BRANCH_W // LANE))
    o_ref[...] = out.astype(o_ref.dtype)


def fox_attention(proj, ct, stats):
    s = proj.shape[0]
    tq = min(FOX_TILE, s)
    full = lambda j: pl.BlockSpec((s, BRANCH_W), lambda i, j=j: (0, j), pipeline_mode=pl.Buffered(1))
    return pl.pallas_call(
        functools.partial(_fox_kernel, tq=tq),
        grid=(s // tq,),
        in_specs=[pl.BlockSpec(memory_space=pltpu.SMEM),
                  pl.BlockSpec((tq, BRANCH_W), lambda i: (i, 4)), full(5), full(6),
                  _const_spec((SUBLANE, s)), _const_spec((N_HEADS, BRANCH_W))],
        out_specs=pl.BlockSpec((tq, BRANCH_W), lambda i: (i, 0)),
        out_shape=jax.ShapeDtypeStruct((s, BRANCH_W), BF16),
        scratch_shapes=[pltpu.VMEM((N_HEADS, tq, LANE), F32),
                        pltpu.VMEM((N_HEADS, tq, LANE), F32),
                        pltpu.VMEM((N_HEADS, tq, BRANCH_W), F32)],
        compiler_params=_params("parallel"),
        name="fox_attention",
    )(stats, proj, proj, proj, ct, _head_masks())


def _pool_kernel(u_ref, w_ref, scale_ref, o_ref, ext, *, tile):
    i = pl.program_id(0)

    @pl.when(i == 0)
    def _():
        ext[pl.ds(0, POOL_HALO), :] = jnp.zeros((POOL_HALO, BRANCH_W), F32)

    u = u_ref[...].astype(F32)
    ext[pl.ds(POOL_HALO, tile), :] = u
    pos = (i * tile + lax.broadcasted_iota(jnp.int32, (tile, 1), 0) + 1).astype(F32)
    halves = []
    for half in range(BRANCH_W // LANE):
        lanes = pl.ds(half * LANE, LANE)
        w_small, w_big = POOL_WINDOWS[2 * half], POOL_WINDOWS[2 * half + 1]
        run = u[:, half * LANE:(half + 1) * LANE]
        sums = {}
        for j in range(1, w_big):
            if j == w_small:
                sums[w_small] = run
            run = run + ext[pl.ds(POOL_HALO - j, tile), lanes]
        sums[w_big] = run
        lane = lax.broadcasted_iota(jnp.int32, (1, LANE), 1)
        small = lane < POOL_GROUP
        total = jnp.where(small, sums[w_small], sums[w_big])
        count = jnp.where(small, jnp.minimum(pos, float(w_small)), jnp.minimum(pos, float(w_big)))
        halves.append(total / count)
    mean = jnp.concatenate(halves, axis=1)
    d = (mean - u).astype(BF16)
    y = _dot(d, w_ref[...]) * scale_ref[...]
    o_ref[...] = y.astype(o_ref.dtype)
    ext[pl.ds(0, POOL_HALO), :] = u[tile - POOL_HALO:, :]


def pool_mixer(proj, w_pool, scale, tile=512):
    s = proj.shape[0]
    tile = min(tile, s)
    ng = len(POOL_WINDOWS)
    w_bd = jnp.zeros((BRANCH_W, BRANCH_W), F32)
    for gi in range(ng):
        lo = gi * POOL_GROUP
        w_bd = w_bd.at[lo:lo + POOL_GROUP, lo:lo + POOL_GROUP].set(w_pool[gi].astype(F32))
    return pl.pallas_call(
        functools.partial(_pool_kernel, tile=tile),
        grid=(s // tile,),
        in_specs=[pl.BlockSpec((tile, BRANCH_W), lambda i: (i, 7)),
                  _const_spec((BRANCH_W, BRANCH_W)), _const_spec((1, BRANCH_W))],
        out_specs=pl.BlockSpec((tile, BRANCH_W), lambda i: (i, 0)),
        out_shape=jax.ShapeDtypeStruct((s, BRANCH_W), BF16),
        scratch_shapes=[pltpu.VMEM((tile + POOL_HALO, BRANCH_W), F32)],
        compiler_params=_params("arbitrary"),
        name="pool_mixer",
    )(proj, w_bd.astype(BF16), scale.reshape(1, BRANCH_W).astype(F32))


def _ret_kernel(q_ref, k_ref, v_ref, g_ref, cos_ref, sin_ref, perm_ref, ones_ref, hm_ref,
                dstack_ref, xi_ref, zeta_ref, gc_ref, gng_ref, gnb_ref, o_ref, st_ref, *, tile):
    c = RET_CHUNK

    @pl.when(pl.program_id(0) == 0)
    def _():
        st_ref[...] = jnp.zeros_like(st_ref)

    perm = perm_ref[...]
    ones_bd = ones_ref[...]
    bd_mask = ones_bd.astype(F32)
    hm = hm_ref[...]

    def chunk(ci, carry):
        r0 = pl.multiple_of(ci * c, c)
        qb = q_ref[pl.ds(r0, c), :]
        kb = k_ref[pl.ds(r0, c), :]
        v = v_ref[pl.ds(r0, c), :]
        g = g_ref[pl.ds(r0, c), :].astype(F32)
        cos = cos_ref[pl.ds(r0, c), :]
        sin = sin_ref[pl.ds(r0, c), :]
        qr = qb.astype(F32) * cos + _dot(qb, perm) * sin
        kr = (kb.astype(F32) * cos + _dot(kb, perm) * sin) * (HEAD_DIM ** -0.5)

        qx = jnp.concatenate([qr * hm[h:h + 1, :] for h in range(N_HEADS)], axis=0).astype(BF16)
        sc = _dot_nt(qx, kr.astype(BF16)) * dstack_ref[...]
        r = _dot(sc.astype(BF16), v)
        intra = jnp.zeros((c, BRANCH_W), F32)
        for h in range(N_HEADS):
            intra = intra + r[h * c:(h + 1) * c, :] * hm[h:h + 1, :]

        st = st_ref[...]
        inter = _dot_nt((qr * xi_ref[...]).astype(BF16), st.astype(BF16))
        upd = _dot(v.astype(F32).T.astype(BF16), (kr * zeta_ref[...]).astype(BF16))
        st_ref[...] = st * gc_ref[...] + upd * bd_mask

        o = intra + inter
        mu = _dot_exact_rhs(o, ones_bd) * (1.0 / HEAD_DIM)
        cen = o - mu
        var = _dot_exact_rhs(cen * cen, ones_bd) * (1.0 / HEAD_DIM)
        y = cen * lax.rsqrt(var + LN_EPS) * gng_ref[...] + gnb_ref[...]
        o_ref[pl.ds(r0, c), :] = (y * _silu(g)).astype(o_ref.dtype)
        return carry

    lax.fori_loop(0, tile // c, chunk, 0, unroll=True)


def _rope_tables(s):
    half = HEAD_DIM // 2
    pos = jnp.arange(s, dtype=F32)
    inv_freq = ROPE_BASE ** (-jnp.arange(half, dtype=F32) / half)
    ang = pos[:, None] * inv_freq[None, :]
    cos, sin = jnp.cos(ang), jnp.sin(ang)
    cos_t = jnp.tile(jnp.concatenate([cos, cos], axis=1), (1, N_HEADS))
    sin_t = jnp.tile(jnp.concatenate([-sin, sin], axis=1), (1, N_HEADS))
    return cos_t, sin_t


def _ret_constants():
    c = RET_CHUNK
    half = HEAD_DIM // 2
    lane = jnp.arange(BRANCH_W)
    partner = jnp.where(lane % HEAD_DIM < half, lane + half, lane - half)
    perm = (lane[:, None] == partner[None, :]).astype(BF16)
    log_gamma = jnp.log1p(-jnp.exp2(-RET_DECAY_BASE - jnp.arange(N_HEADS, dtype=F32)))
    ci = jnp.arange(c, dtype=F32)
    diff = ci[:, None] - ci[None, :]
    intra = jnp.where(diff >= 0, jnp.exp(diff * log_gamma[:, None, None]), 0.0)
    dstack = intra.reshape(N_HEADS * c, c)
    lg_lane = jnp.repeat(log_gamma, HEAD_DIM)[None, :]
    xi = jnp.exp((ci[:, None] + 1.0) * lg_lane)
    zeta = jnp.exp((c - 1.0 - ci[:, None]) * lg_lane)
    gc = jnp.exp(c * lg_lane)
    return perm, dstack, xi, zeta, gc


def retention(proj, gn_g, gn_b, tile=512):
    s = proj.shape[0]
    c = RET_CHUNK
    tile = min(tile, s)
    cos_t, sin_t = _rope_tables(s)
    perm, dstack, xi, zeta, gc = _ret_constants()
    col = lambda j: pl.BlockSpec((tile, BRANCH_W), lambda i, j=j: (i, j))
    row = pl.BlockSpec((tile, BRANCH_W), lambda i: (i, 0))
    return pl.pallas_call(
        functools.partial(_ret_kernel, tile=tile),
        grid=(s // tile,),
        in_specs=[col(8), col(9), col(10), col(11), row, row,
                  _const_spec((BRANCH_W, BRANCH_W)), _const_spec((BRANCH_W, BRANCH_W)),
                  _const_spec((N_HEADS, BRANCH_W)), _const_spec((N_HEADS * c, c)),
                  _const_spec((c, BRANCH_W)), _const_spec((c, BRANCH_W)), _const_spec((1, BRANCH_W)),
                  _const_spec((1, BRANCH_W)), _const_spec((1, BRANCH_W))],
        out_specs=row,
        out_shape=jax.ShapeDtypeStruct((s, BRANCH_W), BF16),
        scratch_shapes=[pltpu.VMEM((BRANCH_W, BRANCH_W), F32)],
        compiler_params=_params("arbitrary"),
        name="retention",
    )(proj, proj, proj, proj, cos_t, sin_t, perm, _head_ones(), _head_masks(), dstack, xi, zeta, gc,
      gn_g.reshape(1, BRANCH_W).astype(F32), gn_b.reshape(1, BRANCH_W).astype(F32))


def _merge_kernel(h_ref, o0_ref, o1_ref, o2_ref, o3_ref, x_ref, wg_ref, wb_ref, wo_ref, g_ref,
                  xo_ref, ho_ref):
    h = h_ref[...]
    merged = jnp.zeros(x_ref.shape, F32)
    for bi, o_ref in enumerate((o0_ref, o1_ref, o2_ref, o3_ref)):
        gate = _sigmoid(_dot(h, wg_ref[:, bi * D_MODEL:(bi + 1) * D_MODEL]))
        merged = merged + gate * _dot(o_ref[...], wb_ref[bi])
    x_new = x_ref[...] + _dot(merged.astype(BF16), wo_ref[...])
    xo_ref[...] = x_new
    ho_ref[...] = _rms(x_new, g_ref[...]).astype(ho_ref.dtype)


def merge(h, branches, x, w_gate, w_branch, w_out, next_gain, tm=512):
    s, d = x.shape
    tm = min(tm, s)
    row = lambda w: pl.BlockSpec((tm, w), lambda i: (i, 0))
    return pl.pallas_call(
        _merge_kernel,
        grid=(s // tm,),
        in_specs=[row(d), row(BRANCH_W), row(BRANCH_W), row(BRANCH_W), row(BRANCH_W), row(d),
                  _const_spec((d, N_BRANCH * d)), _const_spec((N_BRANCH, BRANCH_W, d)),
                  _const_spec((d, d)), _const_spec((1, d))],
        out_specs=[row(d), row(d)],
        out_shape=[jax.ShapeDtypeStruct((s, d), F32), jax.ShapeDtypeStruct((s, d), BF16)],
        compiler_params=_params("parallel"),
        name="merge",
    )(h, *branches, x, w_gate, w_branch, w_out, next_gain.reshape(1, d).astype(F32))


def _ffn_kernel(h_ref, x_ref, wg_ref, wu_ref, wd_ref, g_ref, xo_ref, ho_ref, acc_ref):
    f = pl.program_id(1)

    @pl.when(f == 0)
    def _():
        acc_ref[...] = jnp.zeros_like(acc_ref)

    h = h_ref[...]
    a = _silu(_dot(h, wg_ref[...])) * _dot(h, wu_ref[...])
    acc_ref[...] += _dot(a.astype(BF16), wd_ref[...])

    @pl.when(f == pl.num_programs(1) - 1)
    def _():
        x_new = x_ref[...] + acc_ref[...]
        xo_ref[...] = x_new
        ho_ref[...] = _rms(x_new, g_ref[...]).astype(ho_ref.dtype)


def ffn_dense(h, x, w_gate, w_up, w_down, next_gain, tm=1024, tf=512):
    s, d = x.shape
    tm = min(tm, s)
    dff = w_gate.shape[1]
    row = lambda: pl.BlockSpec((tm, d), lambda i, f: (i, 0))
    return pl.pallas_call(
        _ffn_kernel,
        grid=(s // tm, dff // tf),
        in_specs=[row(), row(),
                  pl.BlockSpec((d, tf), lambda i, f: (0, f)),
                  pl.BlockSpec((d, tf), lambda i, f: (0, f)),
                  pl.BlockSpec((tf, d), lambda i, f: (f, 0)),
                  _const_spec((1, d))],
        out_specs=[row(), row()],
        out_shape=[jax.ShapeDtypeStruct((s, d), F32), jax.ShapeDtypeStruct((s, d), BF16)],
        scratch_shapes=[pltpu.VMEM((tm, d), F32)],
        compiler_params=_params("parallel", "arbitrary"),
        name="ffn_dense",
    )(h, x, w_gate, w_up, w_down, next_gain.reshape(1, d).astype(F32))


MOE_TOK_TILE = 256
MOE_ROW_BLOCK = 512
MOE_GATHER_ROWS = 128
MOE_Y_BLOCK = 128
MOE_Y_FETCH = MOE_TOK_TILE // MOE_Y_BLOCK + 1


def _dot_f32(x, w):
    xh, xm, _ = _split3(x)
    wh, wm, _ = _split3(w)
    return _dot(xh, wh) + (_dot(xh, wm) + _dot(xm, wh))


def _route_kernel(x_ref, ng_ref, router_ref, ltri_ref, pos_ref, gate_ref, post_ref, before_ref, total_ref,
                  carry_ref):
    tm = x_ref.shape[0]

    @pl.when(pl.program_id(0) == 0)
    def _():
        carry_ref[...] = jnp.zeros_like(carry_ref)

    hn = _rms(x_ref[...], ng_ref[...])
    logits = _dot_f32(hn, router_ref[...])
    lane = lax.broadcasted_iota(jnp.int32, (tm, LANE), 1)
    logits = jnp.where(lane < N_EXPERTS, logits, -jnp.inf)
    v1 = jnp.max(logits, axis=1, keepdims=True)
    i1 = jnp.min(jnp.where(logits == v1, lane, LANE), axis=1, keepdims=True)
    rest = jnp.where(lane == i1, -jnp.inf, logits)
    v2 = jnp.max(rest, axis=1, keepdims=True)
    i2 = jnp.min(jnp.where(rest == v2, lane, LANE), axis=1, keepdims=True)
    w1 = 1.0 / (1.0 + jnp.exp(v2 - v1))
    gate_ref[...] = jnp.where(lane == i1, w1, 0.0) + jnp.where(lane == i2, 1.0 - w1, 0.0)

    member = jnp.where((lane == i1) | (lane == i2), 1.0, 0.0)
    carry = carry_ref[...]
    rank = _dot(ltri_ref[...], member.astype(BF16)) + carry
    pos = jnp.where(member > 0.0, rank, -1.0)
    pos_ref[...] = pos
    post_ref[...] = pos.T[:SUBLANE, :]
    before_ref[0] = carry
    carry = carry + jnp.sum(member, axis=0, keepdims=True)
    carry_ref[...] = carry
    total_ref[...] = carry


def moe_route(x, norm_gain, router):
    s, d = x.shape
    tm = min(MOE_TOK_TILE, s)
    nt = s // tm
    router_p = jnp.zeros((d, LANE), F32).at[:, :N_EXPERTS].set(router.astype(F32))
    ltri = jnp.tril(jnp.ones((tm, tm), F32), -1).astype(BF16)
    row = pl.BlockSpec((tm, LANE), lambda i: (i, 0))
    return pl.pallas_call(
        _route_kernel,
        grid=(nt,),
        in_specs=[pl.BlockSpec((tm, d), lambda i: (i, 0)), _const_spec((1, d)), _const_spec((d, LANE)),
                  _const_spec((tm, tm))],
        out_specs=[row, row, pl.BlockSpec((SUBLANE, tm), lambda i: (0, i)),
                   pl.BlockSpec((1, 1, LANE), lambda i: (i, 0, 0)), pl.BlockSpec((1, LANE), lambda i: (0, 0))],
        out_shape=[jax.ShapeDtypeStruct((s, LANE), F32), jax.ShapeDtypeStruct((s, LANE), F32),
                   jax.ShapeDtypeStruct((SUBLANE, s), F32), jax.ShapeDtypeStruct((nt, 1, LANE), F32),
                   jax.ShapeDtypeStruct((1, LANE), F32)],
        scratch_shapes=[pltpu.VMEM((1, LANE), F32)],
        compiler_params=_params("arbitrary"),
        name="moe_route",
    )(x, norm_gain.reshape(1, d).astype(F32), router_p, ltri)


def _moe_ffn_kernel(be_ref, r0_ref, tlo_ref, thi_ref, nv_ref, h_ref, post_ref, wg_ref, wu_ref, wd_ref,
                    y_ref, x_sc, acc_sc, *, tt):
    b = pl.program_id(0)
    f = pl.program_id(1)
    valid = b < nv_ref[0]
    tmb = x_sc.shape[0]

    @pl.when(valid & (f == 0))
    def _():
        e = be_ref[b]
        gr = MOE_GATHER_ROWS
        nsub = tmb // gr
        acc_sc[...] = jnp.zeros_like(acc_sc)
        for sb in range(nsub):
            want = (lax.broadcasted_iota(jnp.int32, (gr, 1), 0) + (r0_ref[b] + sb * gr)).astype(F32)

            def gather(t, carry, want=want, sb=sb):
                t0 = pl.multiple_of(t * tt, tt)
                p = post_ref[pl.ds(e, 1), pl.ds(t0, tt)]
                sel = jnp.where(p == want, 1.0, 0.0).astype(BF16)
                acc_sc[pl.ds(sb * gr, gr), :] += _dot(sel, h_ref[pl.ds(t0, tt), :])
                return carry

            lax.fori_loop(tlo_ref[b * nsub + sb], thi_ref[b * nsub + sb] + 1, gather, 0)
        x_sc[...] = acc_sc[...].astype(BF16)
        acc_sc[...] = jnp.zeros_like(acc_sc)

    @pl.when(valid)
    def _():
        xb = x_sc[...]
        a = _silu(_dot(xb, wg_ref[0])) * _dot(xb, wu_ref[0])
        acc_sc[...] += _dot(a.astype(BF16), wd_ref[0])

    @pl.when(f == pl.num_programs(1) - 1)
    def _():
        y_ref[...] = jnp.where(valid, acc_sc[...], 0.0).astype(y_ref.dtype)


def moe_ffn(h, post, sched, w_gate, w_up, w_down, tf=512):
    s, d = h.shape
    ne, _, dff = w_gate.shape
    tmb = MOE_ROW_BLOCK
    tt = min(MOE_TOK_TILE, s)
    nb = sched[0].shape[0]
    nf = dff // tf

    def wspec(shape, fn):
        return pl.BlockSpec(shape, fn)

    def fidx(b, f, nv):
        return jnp.where(b < nv[0], f, nf - 1)

    grid_spec = pltpu.PrefetchScalarGridSpec(
        num_scalar_prefetch=5,
        grid=(nb, nf),
        in_specs=[pl.BlockSpec((s, d), lambda b, f, *_: (0, 0), pipeline_mode=pl.Buffered(1)),
                  pl.BlockSpec((SUBLANE, s), lambda b, f, *_: (0, 0), pipeline_mode=pl.Buffered(1)),
                  wspec((1, d, tf), lambda b, f, be, r0, tlo, thi, nv: (be[b], 0, fidx(b, f, nv))),
                  wspec((1, d, tf), lambda b, f, be, r0, tlo, thi, nv: (be[b], 0, fidx(b, f, nv))),
                  wspec((1, tf, d), lambda b, f, be, r0, tlo, thi, nv: (be[b], fidx(b, f, nv), 0))],
        out_specs=pl.BlockSpec((tmb, d), lambda b, f, *_: (b, 0)),
        scratch_shapes=[pltpu.VMEM((tmb, d), BF16), pltpu.VMEM((tmb, d), F32)],
    )
    return pl.pallas_call(
        functools.partial(_moe_ffn_kernel, tt=tt),
        grid_spec=grid_spec,
        out_shape=jax.ShapeDtypeStruct((nb * tmb, d), BF16),
        compiler_params=_params("arbitrary", "arbitrary"),
        name="moe_ffn",
    )(*sched, h, post, w_gate, w_up, w_down)


def _moe_combine_kernel(kb_ref, off_ref, lim_ref, x_ref, pos_ref, gate_ref, fg_ref, *rest):
    y_refs, o_ref, acc_sc = rest[:-2], rest[-2], rest[-1]
    t = pl.program_id(0)
    tm = x_ref.shape[0]
    yb = y_refs[0].shape[0]
    pos = pos_ref[...]
    gate = gate_ref[...]
    lane = lax.broadcasted_iota(jnp.int32, (tm, LANE), 1)
    col = lax.broadcasted_iota(jnp.int32, (1, yb), 1).astype(F32)
    acc_sc[...] = x_ref[...]
    for e in range(N_EXPERTS):
        pe = jnp.sum(jnp.where(lane == e, pos, 0.0), axis=1, keepdims=True)
        ge = jnp.sum(jnp.where(lane == e, gate, 0.0), axis=1, keepdims=True)
        r = jnp.where(pe >= 0.0, pe + off_ref[t * N_EXPERTS + e].astype(F32), -1.0)
        for k in range(MOE_Y_FETCH):
            @pl.when(lim_ref[t * N_EXPERTS + e] > k * yb)
            def _(k=k, e=e, r=r, ge=ge):
                sel = jnp.where(r == col + float(k * yb), 1.0, 0.0).astype(BF16)
                acc_sc[...] += ge * _dot(sel, y_refs[MOE_Y_FETCH * e + k][...])
    o_ref[...] = _rms(acc_sc[...], fg_ref[...]).astype(o_ref.dtype)


def moe_combine(x, pos, gate, y, kb, off, lim, final_gain):
    s, d = x.shape
    tm = min(MOE_TOK_TILE, s)
    yb = MOE_Y_BLOCK
    last = y.shape[0] // yb - 1

    def yspec(e, k):
        return pl.BlockSpec((yb, d), lambda t, kb_r, *_: (jnp.minimum(kb_r[t * N_EXPERTS + e] + k, last), 0))

    grid_spec = pltpu.PrefetchScalarGridSpec(
        num_scalar_prefetch=3,
        grid=(s // tm,),
        in_specs=[pl.BlockSpec((tm, d), lambda t, *_: (t, 0)),
                  pl.BlockSpec((tm, LANE), lambda t, *_: (t, 0)),
                  pl.BlockSpec((tm, LANE), lambda t, *_: (t, 0)),
                  pl.BlockSpec((1, d), lambda t, *_: (0, 0))]
        + [yspec(e, k) for e in range(N_EXPERTS) for k in range(MOE_Y_FETCH)],
        out_specs=pl.BlockSpec((tm, d), lambda t, *_: (t, 0)),
        scratch_shapes=[pltpu.VMEM((tm, d), F32)],
    )
    return pl.pallas_call(
        _moe_combine_kernel,
        grid_spec=grid_spec,
        out_shape=jax.ShapeDtypeStruct((s, d), F32),
        compiler_params=_params("arbitrary"),
        name="moe_combine",
    )(kb, off, lim, x, pos, gate, final_gain.reshape(1, d).astype(F32), *([y] * (MOE_Y_FETCH * N_EXPERTS)))


def _moe_schedule(before, total, s):
    tmb, yb, gr = MOE_ROW_BLOCK, MOE_Y_BLOCK, MOE_GATHER_ROWS
    nb = 2 * s // tmb + N_EXPERTS
    counts = total[0, :N_EXPERTS].astype(jnp.int32)
    nblk = (counts + tmb - 1) // tmb
    end = jnp.cumsum(nblk)
    first = end - nblk
    nvalid = end[-1]
    b = jnp.minimum(jnp.arange(nb, dtype=jnp.int32), nvalid - 1)
    blk_e = jnp.sum(b[:, None] >= end[None, :], axis=1).astype(jnp.int32)
    r0 = (b - first[blk_e]) * tmb
    cb = before[:, 0, :N_EXPERTS].astype(jnp.int32)
    r0s = (r0[:, None] + gr * jnp.arange(tmb // gr, dtype=jnp.int32)[None, :]).reshape(-1)
    cbe = cb[:, jnp.repeat(blk_e, tmb // gr)]
    tlo = (jnp.sum(cbe <= r0s[None, :], axis=0) - 1).astype(jnp.int32)
    thi = (jnp.sum(cbe < (r0s + gr)[None, :], axis=0) - 1).astype(jnp.int32)
    sched = (blk_e, r0.astype(jnp.int32), tlo, thi, nvalid.reshape(1).astype(jnp.int32))
    row_start = first[None, :] * tmb + cb
    kb = row_start // yb
    off = first[None, :] * tmb - kb * yb
    n_te = jnp.concatenate([cb[1:], counts[None, :]], axis=0) - cb
    lim = row_start - kb * yb + n_te
    flat = lambda a: a.reshape(-1).astype(jnp.int32)
    return sched, flat(kb), flat(off), flat(lim)


def moe_sparse(h, x, norm_gain, router, w_gate, w_up, w_down, final_gain):
    s, _ = x.shape
    pos, gate, post, before, total = moe_route(x, norm_gain, router)
    sched, kb, off, lim = _moe_schedule(before, total, s)
    y = moe_ffn(h, post, sched, w_gate, w_up, w_down)
    return moe_combine(x, pos, gate, y, kb, off, lim, final_gain)


FOX_F0 = 7 * BRANCH_W


def _split_w_in_kernel(w_ref, mix_ref, f_ref, gate_ref):
    w = w_ref[...]
    rows = w.shape[0]
    mix_ref[:, :FOX_F0] = w[:, :FOX_F0].astype(BF16)
    mix_ref[:, FOX_F0:] = w[:, FOX_F0 + N_HEADS:N_MIX_COLS + N_HEADS].astype(BF16)
    f_ref[...] = jnp.concatenate([w[:, FOX_F0:FOX_F0 + N_HEADS], jnp.zeros((rows, LANE - N_HEADS), F32)],
                                 axis=1).astype(BF16)
    gate_ref[...] = w[:, N_MIX_COLS + N_HEADS:].astype(BF16)


def _mixer_weights(w_in, layer, tr=128):
    _, d, cols = w_in.shape
    row = lambda w: pl.BlockSpec((tr, w), lambda i: (i, 0))
    return pl.pallas_call(
        _split_w_in_kernel,
        grid=(d // tr,),
        in_specs=[pl.BlockSpec((None, tr, cols), lambda i: (layer, i, 0))],
        out_specs=[row(N_MIX_COLS), row(LANE), row(N_BRANCH * D_MODEL)],
        out_shape=[jax.ShapeDtypeStruct((d, N_MIX_COLS), BF16), jax.ShapeDtypeStruct((d, LANE), BF16),
                   jax.ShapeDtypeStruct((d, N_BRANCH * D_MODEL), BF16)],
        compiler_params=_params("parallel"),
        name="split_w_in",
    )(w_in)


def kernel(x, w_in, w_branch, w_out, norm_mix_g, hgrn_lb_logits, hgrn_norm_g, fox_f_bias, pool_w, pool_scale,
           ret_gn_g, ret_gn_b, norm_ffn_g, ffn_w_gate, ffn_w_up, ffn_w_down, moe_router, moe_w_gate, moe_w_up,
           moe_w_down, final_norm_g):
    b, s, d = x.shape
    assert b == 1 and d == D_MODEL
    depth = w_in.shape[0]
    assert depth == 2, "layer 0 uses the dense FFN, layer 1 the experts and the final norm"
    xs = x.reshape(s, d)
    h = rmsnorm_bf16(xs, norm_mix_g[0])
    out = None
    for layer in range(depth):
        w_mix, w_f, w_gate = _mixer_weights(w_in, layer)
        proj = matmul(h, w_mix)
        ct, fox_stats = fox_gate(h, proj, w_f, fox_f_bias[layer])
        branches = (
            hgrn2(proj, hgrn_lb_logits, hgrn_norm_g[layer], layer),
            fox_attention(proj, ct, fox_stats),
            pool_mixer(proj, pool_w[layer], pool_scale[layer]),
            retention(proj, ret_gn_g[layer], ret_gn_b[layer]),
        )
        xs, h2 = merge(h, branches, xs, w_gate, w_branch[layer].astype(BF16), w_out[layer].astype(BF16),
                       norm_ffn_g[layer])
        if layer % 2 == 0:
            li = layer // 2
            xs, h = ffn_dense(h2, xs, ffn_w_gate[li].astype(BF16), ffn_w_up[li].astype(BF16),
                              ffn_w_down[li].astype(BF16), norm_mix_g[layer + 1])
        else:
            li = layer // 2
            out = moe_sparse(h2, xs, norm_ffn_g[layer], moe_router[li], moe_w_gate[li].astype(BF16),
                             moe_w_up[li].astype(BF16), moe_w_down[li].astype(BF16), final_norm_g)
    return out.reshape(b, s, d)
```

```python
import functools
import math

import jax
import jax.numpy as jnp
from jax import lax
from jax.experimental import pallas as pl
from jax.experimental.pallas import tpu as pltpu

D_MODEL = 1024
N_BRANCH = 4
BRANCH_W = D_MODEL // N_BRANCH
HEAD_DIM = 64
N_HEADS = BRANCH_W // HEAD_DIM
POOL_WINDOWS = (2, 4, 8, 16)
POOL_GROUP = BRANCH_W // len(POOL_WINDOWS)
POOL_HALO = 16
RET_DECAY_BASE = 5.0
ROPE_BASE = 10000.0
D_FF = 7 * D_MODEL // 2
N_EXPERTS = 8
RMS_EPS = 1e-6
LN_EPS = 1e-5
N_MIX_COLS = 12 * BRANCH_W

LANE = 128
SUBLANE = 8
VMEM_LIMIT = 56 * 1024 * 1024

HG_CHUNK = 64
HG_SUB = 16
HG_FAST_MIN_LOGDECAY = -60.0
RET_CHUNK = 256

F32 = jnp.float32
BF16 = jnp.bfloat16
NT_DIMS = (((1,), (1,)), ((), ()))


def _params(*sem):
    return pltpu.CompilerParams(dimension_semantics=sem, vmem_limit_bytes=VMEM_LIMIT)


def _const_spec(shape):
    nd = len(shape)
    return pl.BlockSpec(shape, lambda *_: (0,) * nd, pipeline_mode=pl.Buffered(1))


def _split3(x):
    hi = x.astype(BF16)
    r1 = x - hi.astype(F32)
    mid = r1.astype(BF16)
    lo = (r1 - mid.astype(F32)).astype(BF16)
    return hi, mid, lo


def _dot(a, b):
    return jnp.dot(a, b, preferred_element_type=F32)


def _dot_nt(a, b):
    return lax.dot_general(a, b, NT_DIMS, preferred_element_type=F32)


def _dot_exact_rhs(x, m_bf16):
    hi, mid, lo = _split3(x)
    return _dot(hi, m_bf16) + _dot(mid, m_bf16) + _dot(lo, m_bf16)


def _dot_exact_lhs(m_bf16, x):
    hi, mid, lo = _split3(x)
    return _dot(m_bf16, hi) + _dot(m_bf16, mid) + _dot(m_bf16, lo)


def _sigmoid(x):
    return 1.0 / (1.0 + jnp.exp(-x))


def _silu(x):
    return x * _sigmoid(x)


def _rms(x, gain):
    return x * lax.rsqrt(jnp.mean(x * x, axis=-1, keepdims=True) + RMS_EPS) * gain


def _head_of(n):
    return jnp.arange(n) // HEAD_DIM


def _head_ones():
    h = _head_of(BRANCH_W)
    return (h[:, None] == h[None, :]).astype(BF16)


def _head_masks():
    return (_head_of(BRANCH_W)[None, :] == jnp.arange(N_HEADS)[:, None]).astype(F32)


def _rmsnorm_kernel(x_ref, g_ref, o_ref):
    o_ref[...] = _rms(x_ref[...], g_ref[...]).astype(o_ref.dtype)


def rmsnorm_bf16(x, gain, tm=1024):
    s, d = x.shape
    tm = min(tm, s)
    return pl.pallas_call(
        _rmsnorm_kernel,
        grid=(s // tm,),
        in_specs=[pl.BlockSpec((tm, d), lambda i: (i, 0)), _const_spec((1, d))],
        out_specs=pl.BlockSpec((tm, d), lambda i: (i, 0)),
        out_shape=jax.ShapeDtypeStruct((s, d), BF16),
        compiler_params=_params("parallel"),
        name="rmsnorm",
    )(x, gain.reshape(1, d))


def _matmul_kernel(a_ref, b_ref, o_ref):
    o_ref[...] = _dot(a_ref[...], b_ref[...]).astype(o_ref.dtype)


def matmul(a, b, out_dtype=BF16, tm=2048, tn=1024):
    m, k = a.shape
    _, n = b.shape
    tm = min(tm, m)
    return pl.pallas_call(
        _matmul_kernel,
        grid=(n // tn, m // tm),
        in_specs=[pl.BlockSpec((tm, k), lambda j, i: (i, 0)),
                  pl.BlockSpec((k, tn), lambda j, i: (0, j))],
        out_specs=pl.BlockSpec((tm, tn), lambda j, i: (i, j)),
        out_shape=jax.ShapeDtypeStruct((m, n), out_dtype),
        compiler_params=_params("parallel", "parallel"),
        name="in_proj",
    )(a, b)


def _hgrn_kernel(q_ref, f_ref, i_ref, g_ref, lbl_ref, ng_ref, ones_ref, hm_ref, tril_ref, halfsum_ref, fmask_ref,
                 o_ref, st_ref, bpad, kpad, vpad, astack, lf_sc, kk_sc, *, layer, tile):
    c, sub = HG_CHUNK, HG_SUB
    nsub = c // sub
    half = c // 2

    @pl.when(pl.program_id(0) == 0)
    def _():
        st_ref[...] = jnp.zeros_like(st_ref)
        bpad[...] = jnp.zeros_like(bpad)
        kpad[...] = jnp.zeros_like(kpad)
        vpad[...] = jnp.zeros_like(vpad)

    lbl = lbl_ref[...]
    e = jnp.exp(lbl - jnp.max(lbl, axis=0, keepdims=True))
    p = e / jnp.sum(e, axis=0, keepdims=True)
    lb = jnp.zeros((1, BRANCH_W), F32)
    for l in range(1, layer + 1):
        lb = lb + p[l:l + 1, :]

    ones_bd = ones_ref[...]
    hm = hm_ref[...]
    tril = tril_ref[...]
    row = lax.broadcasted_iota(jnp.int32, (c, 1), 0)
    row_in_sub = row % sub
    bd_mask = ones_bd.astype(F32)

    sig = _sigmoid(f_ref[...].astype(F32))
    logf_all = jnp.log(lb + (1.0 - lb) * sig)
    lf_sc[...] = logf_all
    kk_sc[...] = (1.0 - lb) * (1.0 - sig)
    min_decay = jnp.min(_dot(halfsum_ref[...], logf_all.astype(BF16)))

    def load(ci):
        r0 = pl.multiple_of(ci * c, c)
        q = q_ref[pl.ds(r0, c), :].astype(F32)
        v = i_ref[pl.ds(r0, c), :].astype(F32)
        kk = kk_sc[pl.ds(r0, c), :]
        b = _dot_exact_lhs(tril, lf_sc[pl.ds(r0, c), :])
        return r0, q, v, kk, b

    def finish(r0, q_decayed, v, kk, b, intra):
        st = st_ref[...]
        inter = _dot_nt(q_decayed.astype(BF16), st.astype(BF16))
        b_last = b[c - 1:c, :]
        ks_end = (kk * jnp.exp(b_last - b)).astype(BF16)
        upd = _dot(v.T.astype(BF16), ks_end)
        st_ref[...] = st * jnp.exp(b_last) + upd * bd_mask
        o = intra + inter
        ms = _dot_exact_rhs(o * o, ones_bd) * (1.0 / HEAD_DIM)
        g = g_ref[pl.ds(r0, c), :].astype(F32)
        y = o * lax.rsqrt(ms + RMS_EPS) * ng_ref[...] * _silu(g)
        o_ref[pl.ds(r0, c), :] = y.astype(o_ref.dtype)

    def fast_chunk(ci, carry):
        r0, q, v, kk, b = load(ci)
        second = row >= half
        m_row = b[half - 1:half, :]
        mref = jnp.where(second, m_row, 0.0)
        qp = q * jnp.exp(b - mref)
        kp = kk * jnp.exp(mref - b)
        e_m = jnp.exp(m_row)
        kaug = jnp.concatenate([kp, kp[:half, :] * e_m], axis=0).astype(BF16)
        vaug = jnp.concatenate([v, v[:half, :]], axis=0).astype(BF16)
        qx = jnp.concatenate([qp * hm[h:h + 1, :] for h in range(N_HEADS)], axis=0).astype(BF16)
        sc = jnp.where(fmask_ref[...] > 0.0, _dot_nt(qx, kaug), 0.0)
        r = _dot(sc.astype(BF16), vaug)
        intra = jnp.zeros((c, BRANCH_W), F32)
        for h in range(N_HEADS):
            intra = intra + r[h * c:(h + 1) * c, :] * hm[h:h + 1, :]
        finish(r0, jnp.where(second, qp * e_m, qp), v, kk, b, intra)
        return carry

    def exact_chunk(ci, carry):
        r0, q, v, kk, b = load(ci)

        bpad[pl.ds(sub, c), :] = b
        kpad[pl.ds(sub, c), :] = kk
        vpad[pl.ds(sub, c), :] = v

        for d in range(sub):
            b_d = bpad[pl.ds(sub - d, c), :]
            k_d = kpad[pl.ds(sub - d, c), :]
            a = jnp.where(row_in_sub >= d, q * k_d * jnp.exp(b - b_d), 0.0)
            astack[pl.ds(d * c, c), :] = a.astype(BF16)
        pall = _dot(astack[...], ones_bd)
        intra = jnp.zeros((c, BRANCH_W), F32)
        for d in range(sub):
            intra = intra + pall[d * c:(d + 1) * c, :] * vpad[pl.ds(sub - d, c), :]

        pieces = [jnp.zeros((sub, BRANCH_W), F32)]
        for si in range(1, nsub):
            lo = si * sub
            m_i = b[lo - 1:lo, :]
            qs = q[lo:lo + sub, :] * jnp.exp(b[lo:lo + sub, :] - m_i)
            ks = (kk[:lo, :] * jnp.exp(m_i - b[:lo, :])).astype(BF16)
            qx = jnp.concatenate([qs * hm[h:h + 1, :] for h in range(N_HEADS)], axis=0).astype(BF16)
            sc = _dot_nt(qx, ks)
            r = _dot(sc.astype(BF16), v[:lo, :].astype(BF16))
            acc = jnp.zeros((sub, BRANCH_W), F32)
            for h in range(N_HEADS):
                acc = acc + r[h * sub:(h + 1) * sub, :] * hm[h:h + 1, :]
            pieces.append(acc)
        intra = intra + jnp.concatenate(pieces, axis=0)
        finish(r0, q * jnp.exp(b), v, kk, b, intra)
        return carry

    lax.cond(min_decay >= HG_FAST_MIN_LOGDECAY,
             lambda: lax.fori_loop(0, tile // c, fast_chunk, 0, unroll=2),
             lambda: lax.fori_loop(0, tile // c, exact_chunk, 0))


def hgrn2(proj, lb_logits, norm_g, layer, tile=512):
    s = proj.shape[0]
    depth = lb_logits.shape[0]
    c, sub = HG_CHUNK, HG_SUB
    half = c // 2
    tile = min(tile, s)
    col = lambda j: pl.BlockSpec((tile, BRANCH_W), lambda i, j=j: (i, j))
    tril = jnp.tril(jnp.ones((c, c), F32)).astype(BF16)
    nhalf = tile // half
    halfsum = (jnp.arange(tile)[None, :] // half == jnp.arange(nhalf)[:, None]).astype(BF16)
    t = jnp.arange(c)[:, None]
    col_s = jnp.arange(c + half)[None, :]
    same_half = (col_s < c) & (col_s // half == t // half) & (col_s <= t)
    cross = (col_s >= c) & (t >= half)
    fmask = jnp.tile((same_half | cross).astype(F32), (N_HEADS, 1))
    return pl.pallas_call(
        functools.partial(_hgrn_kernel, layer=layer, tile=tile),
        grid=(s // tile,),
        in_specs=[col(0), col(1), col(2), col(3),
                  _const_spec((depth, BRANCH_W)), _const_spec((1, BRANCH_W)),
                  _const_spec((BRANCH_W, BRANCH_W)), _const_spec((N_HEADS, BRANCH_W)),
                  _const_spec((c, c)), _const_spec((nhalf, tile)), _const_spec((N_HEADS * c, c + half))],
        out_specs=pl.BlockSpec((tile, BRANCH_W), lambda i: (i, 0)),
        out_shape=jax.ShapeDtypeStruct((s, BRANCH_W), BF16),
        scratch_shapes=[pltpu.VMEM((BRANCH_W, BRANCH_W), F32),
                        pltpu.VMEM((c + sub, BRANCH_W), F32),
                        pltpu.VMEM((c + sub, BRANCH_W), F32),
                        pltpu.VMEM((c + sub, BRANCH_W), F32),
                        pltpu.VMEM((sub * c, BRANCH_W), BF16),
                        pltpu.VMEM((tile, BRANCH_W), F32),
                        pltpu.VMEM((tile, BRANCH_W), F32)],
        compiler_params=_params("arbitrary"),
        name="hgrn2",
    )(proj, proj, proj, proj, lb_logits.astype(F32), norm_g.reshape(1, BRANCH_W).astype(F32),
      _head_ones(), _head_masks(), tril, halfsum, fmask)


FOX_TILE = 256
FOX_NSTAT = 16
FOX_SKIP_LOG = 40.0


def _fox_gate_kernel(h_ref, q_ref, k_ref, wf_ref, bias_ref, tril_ref, ones_ref,
                     ct_ref, stat_ref, carry_ref, kmax_ref):
    @pl.when(pl.program_id(0) == 0)
    def _():
        carry_ref[...] = jnp.zeros_like(carry_ref)
        kmax_ref[...] = jnp.zeros_like(kmax_ref)

    logit = _dot(h_ref[...], wf_ref[...]) + bias_ref[...]
    logf = jnp.minimum(logit, 0.0) - jnp.log(1.0 + jnp.exp(-jnp.abs(logit)))
    cum = _dot_exact_lhs(tril_ref[...], logf) + carry_ref[...]
    carry_ref[...] = cum[-1:, :]
    ct_ref[...] = cum.T[:SUBLANE, :]

    ones_bd = ones_ref[...]
    q = q_ref[...].astype(F32)
    k = k_ref[...].astype(F32)
    scale = HEAD_DIM ** -0.5
    head_lane = lax.broadcasted_iota(jnp.int32, (1, BRANCH_W), 1) // HEAD_DIM
    c_heads = jnp.zeros(q.shape, F32)
    for h in range(N_HEADS):
        c_heads = jnp.where(head_lane == h, cum[:, h:h + 1], c_heads)
    slack = 1.0 + 2.0 ** -6
    qn = jnp.sqrt(_dot((q * q).astype(BF16), ones_bd)) * (scale * slack)
    kn = jnp.sqrt(_dot((k * k).astype(BF16), ones_bd)) * slack
    diag = _dot((q * k).astype(BF16), ones_bd) * scale - (2.0 ** -6) * qn * kn
    kmax = jnp.maximum(kmax_ref[...], jnp.max(kn, axis=0, keepdims=True))
    kmax_ref[...] = kmax
    rows = [jnp.max(qn, axis=0, keepdims=True),
            jnp.max(c_heads - diag, axis=0, keepdims=True),
            kmax,
            c_heads[-1:, :]]
    stat_ref[0] = jnp.concatenate(rows + [jnp.zeros((SUBLANE - len(rows), BRANCH_W), F32)], axis=0)


def fox_gate(h, proj, w_f, f_bias):
    s, d = h.shape
    tile = min(FOX_TILE, s)
    bias = jnp.zeros((1, LANE), F32).at[0, :N_HEADS].set(f_bias.astype(F32))
    tril = jnp.tril(jnp.ones((tile, tile), F32)).astype(BF16)
    ct, stats = pl.pallas_call(
        _fox_gate_kernel,
        grid=(s // tile,),
        in_specs=[pl.BlockSpec((tile, d), lambda i: (i, 0)),
                  pl.BlockSpec((tile, BRANCH_W), lambda i: (i, 4)),
                  pl.BlockSpec((tile, BRANCH_W), lambda i: (i, 5)),
                  _const_spec((d, LANE)), _const_spec((1, LANE)), _const_spec((tile, tile)),
                  _const_spec((BRANCH_W, BRANCH_W))],
        out_specs=[pl.BlockSpec((SUBLANE, tile), lambda i: (0, i)),
                   pl.BlockSpec((1, SUBLANE, BRANCH_W), lambda i: (i, 0, 0))],
        out_shape=[jax.ShapeDtypeStruct((SUBLANE, s), F32),
                   jax.ShapeDtypeStruct((s // tile, SUBLANE, BRANCH_W), F32)],
        scratch_shapes=[pltpu.VMEM((1, LANE), F32), pltpu.VMEM((1, BRANCH_W), F32)],
        compiler_params=_params("arbitrary"),
        name="fox_gate",
    )(h, proj, proj, w_f, bias, tril, _head_ones())
    return ct, stats[:, :4, ::HEAD_DIM].reshape(-1)


def _fox_kernel(stat_ref, q_ref, k_ref, v_ref, ct_ref, hm_ref, o_ref, m_sc, l_sc, acc_sc, *, tq):
    i = pl.program_id(0)
    q0 = pl.multiple_of(i * tq, tq)
    hm = hm_ref[...]
    q = q_ref[...].astype(F32) * (HEAD_DIM ** -0.5)
    qh = [(q * hm[h:h + 1, :]).astype(BF16) for h in range(N_HEADS)]
    c_q0 = ct_ref[:, pl.ds(q0, tq)][:, 0:1]

    first = i
    for h in range(N_HEADS):
        qmax = stat_ref[i * FOX_NSTAT + h]
        emax = stat_ref[i * FOX_NSTAT + N_HEADS + h]

        def needed(j, h=h, qmax=qmax, emax=emax):
            jc = jnp.maximum(j, 0)
            bound = (qmax * stat_ref[jc * FOX_NSTAT + 2 * N_HEADS + h] + emax
                     - stat_ref[jc * FOX_NSTAT + 3 * N_HEADS + h])
            return (j >= 0) & (bound >= -FOX_SKIP_LOG)

        last_dropped = lax.while_loop(needed, lambda j: j - 1, i - 1)
        first = jnp.minimum(first, last_dropped + 1)

    m_sc[...] = jnp.full_like(m_sc, -jnp.inf)
    l_sc[...] = jnp.zeros_like(l_sc)
    acc_sc[...] = jnp.zeros_like(acc_sc)

    def block(s0, diagonal):
        kb = k_ref[pl.ds(s0, tq), :]
        vb = v_ref[pl.ds(s0, tq), :]
        bias = c_q0 - ct_ref[:, pl.ds(s0, tq)]
        for h in range(N_HEADS):
            sc = _dot_nt(qh[h], kb) + bias[h:h + 1, :]
            if diagonal:
                r = lax.broadcasted_iota(jnp.int32, (tq, tq), 0)
                cidx = lax.broadcasted_iota(jnp.int32, (tq, tq), 1)
                sc = jnp.where(cidx <= r, sc, -jnp.inf)
            m_prev = m_sc[h]
            m_new = jnp.maximum(m_prev, jnp.max(sc, axis=1, keepdims=True))
            alpha = jnp.exp(m_prev - m_new)
            p = jnp.exp(sc - jnp.tile(m_new, (1, tq // LANE)))
            l_sc[h] = alpha * l_sc[h] + jnp.sum(p, axis=1, keepdims=True)
            acc_sc[h] = acc_sc[h] * jnp.tile(alpha, (1, BRANCH_W // LANE)) + _dot(p.astype(BF16), vb)
            m_sc[h] = m_new

    def body(j, carry):
        block(pl.multiple_of(j * tq, tq), False)
        return carry

    lax.fori_loop(first, i, body, 0)
    block(q0, True)

    out = jnp.zeros((tq, BRANCH_W), F32)
    for h in range(N_HEADS):
        out = out + acc_sc[h] * hm[h:h + 1, :] / jnp.tile(l_sc[h], (1, BRANCH_W // LANE))
    o_ref[...] = out.astype(o_ref.dtype)


def fox_attention(proj, ct, stats):
    s = proj.shape[0]
    tq = min(FOX_TILE, s)
    full = lambda j: pl.BlockSpec((s, BRANCH_W), lambda i, j=j: (0, j), pipeline_mode=pl.Buffered(1))
    return pl.pallas_call(
        functools.partial(_fox_kernel, tq=tq),
        grid=(s // tq,),
        in_specs=[pl.BlockSpec(memory_space=pltpu.SMEM),
                  pl.BlockSpec((tq, BRANCH_W), lambda i: (i, 4)), full(5), full(6),
                  _const_spec((SUBLANE, s)), _const_spec((N_HEADS, BRANCH_W))],
        out_specs=pl.BlockSpec((tq, BRANCH_W), lambda i: (i, 0)),
        out_shape=jax.ShapeDtypeStruct((s, BRANCH_W), BF16),
        scratch_shapes=[pltpu.VMEM((N_HEADS, tq, LANE), F32),
                        pltpu.VMEM((N_HEADS, tq, LANE), F32),
                        pltpu.VMEM((N_HEADS, tq, BRANCH_W), F32)],
        compiler_params=_params("parallel"),
        name="fox_attention",
    )(stats, proj, proj, proj, ct, _head_masks())


def _pool_kernel(u_ref, w_ref, scale_ref, o_ref, ext, *, tile):
    i = pl.program_id(0)

    @pl.when(i == 0)
    def _():
        ext[pl.ds(0, POOL_HALO), :] = jnp.zeros((POOL_HALO, BRANCH_W), F32)

    u = u_ref[...].astype(F32)
    ext[pl.ds(POOL_HALO, tile), :] = u
    pos = (i * tile + lax.broadcasted_iota(jnp.int32, (tile, 1), 0) + 1).astype(F32)
    halves = []
    for half in range(BRANCH_W // LANE):
        lanes = pl.ds(half * LANE, LANE)
        w_small, w_big = POOL_WINDOWS[2 * half], POOL_WINDOWS[2 * half + 1]
        run = u[:, half * LANE:(half + 1) * LANE]
        sums = {}
        for j in range(1, w_big):
            if j == w_small:
                sums[w_small] = run
            run = run + ext[pl.ds(POOL_HALO - j, tile), lanes]
        sums[w_big] = run
        lane = lax.broadcasted_iota(jnp.int32, (1, LANE), 1)
        small = lane < POOL_GROUP
        total = jnp.where(small, sums[w_small], sums[w_big])
        count = jnp.where(small, jnp.minimum(pos, float(w_small)), jnp.minimum(pos, float(w_big)))
        halves.append(total / count)
    mean = jnp.concatenate(halves, axis=1)
    d = (mean - u).astype(BF16)
    y = _dot(d, w_ref[...]) * scale_ref[...]
    o_ref[...] = y.astype(o_ref.dtype)
    ext[pl.ds(0, POOL_HALO), :] = u[tile - POOL_HALO:, :]


def pool_mixer(proj, w_pool, scale, tile=512):
    s = proj.shape[0]
    tile = min(tile, s)
    ng = len(POOL_WINDOWS)
    w_bd = jnp.zeros((BRANCH_W, BRANCH_W), F32)
    for gi in range(ng):
        lo = gi * POOL_GROUP
        w_bd = w_bd.at[lo:lo + POOL_GROUP, lo:lo + POOL_GROUP].set(w_pool[gi].astype(F32))
    return pl.pallas_call(
        functools.partial(_pool_kernel, tile=tile),
        grid=(s // tile,),
        in_specs=[pl.BlockSpec((tile, BRANCH_W), lambda i: (i, 7)),
                  _const_spec((BRANCH_W, BRANCH_W)), _const_spec((1, BRANCH_W))],
        out_specs=pl.BlockSpec((tile, BRANCH_W), lambda i: (i, 0)),
        out_shape=jax.ShapeDtypeStruct((s, BRANCH_W), BF16),
        scratch_shapes=[pltpu.VMEM((tile + POOL_HALO, BRANCH_W), F32)],
        compiler_params=_params("arbitrary"),
        name="pool_mixer",
    )(proj, w_bd.astype(BF16), scale.reshape(1, BRANCH_W).astype(F32))


def _ret_kernel(q_ref, k_ref, v_ref, g_ref, cos_ref, sin_ref, perm_ref, ones_ref, hm_ref,
                dstack_ref, xi_ref, zeta_ref, gc_ref, gng_ref, gnb_ref, o_ref, st_ref, *, tile):
    c = RET_CHUNK

    @pl.when(pl.program_id(0) == 0)
    def _():
        st_ref[...] = jnp.zeros_like(st_ref)

    perm = perm_ref[...]
    ones_bd = ones_ref[...]
    bd_mask = ones_bd.astype(F32)
    hm = hm_ref[...]

    def chunk(ci, carry):
        r0 = pl.multiple_of(ci * c, c)
        qb = q_ref[pl.ds(r0, c), :]
        kb = k_ref[pl.ds(r0, c), :]
        v = v_ref[pl.ds(r0, c), :]
        g = g_ref[pl.ds(r0, c), :].astype(F32)
        cos = cos_ref[pl.ds(r0, c), :]
        sin = sin_ref[pl.ds(r0, c), :]
        qr = qb.astype(F32) * cos + _dot(qb, perm) * sin
        kr = (kb.astype(F32) * cos + _dot(kb, perm) * sin) * (HEAD_DIM ** -0.5)

        qx = jnp.concatenate([qr * hm[h:h + 1, :] for h in range(N_HEADS)], axis=0).astype(BF16)
        sc = _dot_nt(qx, kr.astype(BF16)) * dstack_ref[...]
        r = _dot(sc.astype(BF16), v)
        intra = jnp.zeros((c, BRANCH_W), F32)
        for h in range(N_HEADS):
            intra = intra + r[h * c:(h + 1) * c, :] * hm[h:h + 1, :]

        st = st_ref[...]
        inter = _dot_nt((qr * xi_ref[...]).astype(BF16), st.astype(BF16))
        upd = _dot(v.astype(F32).T.astype(BF16), (kr * zeta_ref[...]).astype(BF16))
        st_ref[...] = st * gc_ref[...] + upd * bd_mask

        o = intra + inter
        mu = _dot_exact_rhs(o, ones_bd) * (1.0 / HEAD_DIM)
        cen = o - mu
        var = _dot_exact_rhs(cen * cen, ones_bd) * (1.0 / HEAD_DIM)
        y = cen * lax.rsqrt(var + LN_EPS) * gng_ref[...] + gnb_ref[...]
        o_ref[pl.ds(r0, c), :] = (y * _silu(g)).astype(o_ref.dtype)
        return carry

    lax.fori_loop(0, tile // c, chunk, 0, unroll=True)


def _rope_tables(s):
    half = HEAD_DIM // 2
    pos = jnp.arange(s, dtype=F32)
    inv_freq = ROPE_BASE ** (-jnp.arange(half, dtype=F32) / half)
    lane = jnp.arange(BRANCH_W)
    ang = pos[:, None] * inv_freq[lane % half][None, :]
    sign = jnp.where(lane % HEAD_DIM < half, -1.0, 1.0).astype(F32)
    return jnp.cos(ang), jnp.sin(ang) * sign[None, :]


def _ret_constants():
    c = RET_CHUNK
    half = HEAD_DIM // 2
    lane = jnp.arange(BRANCH_W)
    partner = jnp.where(lane % HEAD_DIM < half, lane + half, lane - half)
    perm = (lane[:, None] == partner[None, :]).astype(BF16)
    log_gamma = jnp.log1p(-jnp.exp2(-RET_DECAY_BASE - jnp.arange(N_HEADS, dtype=F32)))
    ci = jnp.arange(c, dtype=F32)
    diff = ci[:, None] - ci[None, :]
    intra = jnp.where(diff >= 0, jnp.exp(diff * log_gamma[:, None, None]), 0.0)
    dstack = intra.reshape(N_HEADS * c, c)
    lg_lane = jnp.repeat(log_gamma, HEAD_DIM)[None, :]
    xi = jnp.exp((ci[:, None] + 1.0) * lg_lane)
    zeta = jnp.exp((c - 1.0 - ci[:, None]) * lg_lane)
    gc = jnp.exp(c * lg_lane)
    return perm, dstack, xi, zeta, gc


def retention(proj, gn_g, gn_b, tile=512):
    s = proj.shape[0]
    c = RET_CHUNK
    tile = min(tile, s)
    cos_t, sin_t = _rope_tables(s)
    perm, dstack, xi, zeta, gc = _ret_constants()
    col = lambda j: pl.BlockSpec((tile, BRANCH_W), lambda i, j=j: (i, j))
    row = pl.BlockSpec((tile, BRANCH_W), lambda i: (i, 0))
    return pl.pallas_call(
        functools.partial(_ret_kernel, tile=tile),
        grid=(s // tile,),
        in_specs=[col(8), col(9), col(10), col(11), row, row,
                  _const_spec((BRANCH_W, BRANCH_W)), _const_spec((BRANCH_W, BRANCH_W)),
                  _const_spec((N_HEADS, BRANCH_W)), _const_spec((N_HEADS * c, c)),
                  _const_spec((c, BRANCH_W)), _const_spec((c, BRANCH_W)), _const_spec((1, BRANCH_W)),
                  _const_spec((1, BRANCH_W)), _const_spec((1, BRANCH_W))],
        out_specs=row,
        out_shape=jax.ShapeDtypeStruct((s, BRANCH_W), BF16),
        scratch_shapes=[pltpu.VMEM((BRANCH_W, BRANCH_W), F32)],
        compiler_params=_params("arbitrary"),
        name="retention",
    )(proj, proj, proj, proj, cos_t, sin_t, perm, _head_ones(), _head_masks(), dstack, xi, zeta, gc,
      gn_g.reshape(1, BRANCH_W).astype(F32), gn_b.reshape(1, BRANCH_W).astype(F32))


def _merge_kernel(h_ref, o0_ref, o1_ref, o2_ref, o3_ref, x_ref, wg_ref, wb_ref, wo_ref, g_ref,
                  xo_ref, ho_ref):
    h = h_ref[...]
    merged = jnp.zeros(x_ref.shape, F32)
    for bi, o_ref in enumerate((o0_ref, o1_ref, o2_ref, o3_ref)):
        gate = _sigmoid(_dot(h, wg_ref[:, bi * D_MODEL:(bi + 1) * D_MODEL]))
        merged = merged + gate * _dot(o_ref[...], wb_ref[bi])
    x_new = x_ref[...] + _dot(merged.astype(BF16), wo_ref[...])
    xo_ref[...] = x_new
    ho_ref[...] = _rms(x_new, g_ref[...]).astype(ho_ref.dtype)


def merge(h, branches, x, w_gate, w_branch, w_out, next_gain, tm=512):
    s, d = x.shape
    tm = min(tm, s)
    row = lambda w: pl.BlockSpec((tm, w), lambda i: (i, 0))
    return pl.pallas_call(
        _merge_kernel,
        grid=(s // tm,),
        in_specs=[row(d), row(BRANCH_W), row(BRANCH_W), row(BRANCH_W), row(BRANCH_W), row(d),
                  _const_spec((d, N_BRANCH * d)), _const_spec((N_BRANCH, BRANCH_W, d)),
                  _const_spec((d, d)), _const_spec((1, d))],
        out_specs=[row(d), row(d)],
        out_shape=[jax.ShapeDtypeStruct((s, d), F32), jax.ShapeDtypeStruct((s, d), BF16)],
        compiler_params=_params("parallel"),
        name="merge",
    )(h, *branches, x, w_gate, w_branch, w_out, next_gain.reshape(1, d).astype(F32))


def _ffn_kernel(h_ref, x_ref, wg_ref, wu_ref, wd_ref, g_ref, xo_ref, ho_ref, acc_ref):
    f = pl.program_id(1)

    @pl.when(f == 0)
    def _():
        acc_ref[...] = jnp.zeros_like(acc_ref)

    h = h_ref[...]
    a = _silu(_dot(h, wg_ref[...].astype(BF16))) * _dot(h, wu_ref[...].astype(BF16))
    acc_ref[...] += _dot(a.astype(BF16), wd_ref[...].astype(BF16))

    @pl.when(f == pl.num_programs(1) - 1)
    def _():
        x_new = x_ref[...] + acc_ref[...]
        xo_ref[...] = x_new
        ho_ref[...] = _rms(x_new, g_ref[...]).astype(ho_ref.dtype)


def ffn_dense(h, x, w_gate, w_up, w_down, next_gain, tm=1024, tf=512):
    s, d = x.shape
    tm = min(tm, s)
    dff = w_gate.shape[1]
    row = lambda: pl.BlockSpec((tm, d), lambda i, f: (i, 0))
    return pl.pallas_call(
        _ffn_kernel,
        grid=(s // tm, dff // tf),
        in_specs=[row(), row(),
                  pl.BlockSpec((d, tf), lambda i, f: (0, f)),
                  pl.BlockSpec((d, tf), lambda i, f: (0, f)),
                  pl.BlockSpec((tf, d), lambda i, f: (f, 0)),
                  _const_spec((1, d))],
        out_specs=[row(), row()],
        out_shape=[jax.ShapeDtypeStruct((s, d), F32), jax.ShapeDtypeStruct((s, d), BF16)],
        scratch_shapes=[pltpu.VMEM((tm, d), F32)],
        compiler_params=_params("parallel", "arbitrary"),
        name="ffn_dense",
    )(h, x, w_gate, w_up, w_down, next_gain.reshape(1, d).astype(F32))


MOE_TOK_TILE = 256
MOE_ROW_BLOCK = 512
MOE_GATHER_ROWS = 128
MOE_Y_BLOCK = 128
MOE_Y_FETCH = MOE_TOK_TILE // MOE_Y_BLOCK + 1


def _dot_f32(x, w):
    xh, xm, _ = _split3(x)
    wh, wm, _ = _split3(w)
    return _dot(xh, wh) + (_dot(xh, wm) + _dot(xm, wh))


def _route_kernel(x_ref, ng_ref, router_ref, ltri_ref, pos_ref, gate_ref, post_ref, before_ref, total_ref,
                  carry_ref):
    tm = x_ref.shape[0]

    @pl.when(pl.program_id(0) == 0)
    def _():
        carry_ref[...] = jnp.zeros_like(carry_ref)

    hn = _rms(x_ref[...], ng_ref[...])
    logits = _dot_f32(hn, router_ref[...])
    lane = lax.broadcasted_iota(jnp.int32, (tm, LANE), 1)
    logits = jnp.where(lane < N_EXPERTS, logits, -jnp.inf)
    v1 = jnp.max(logits, axis=1, keepdims=True)
    i1 = jnp.min(jnp.where(logits == v1, lane, LANE), axis=1, keepdims=True)
    rest = jnp.where(lane == i1, -jnp.inf, logits)
    v2 = jnp.max(rest, axis=1, keepdims=True)
    i2 = jnp.min(jnp.where(rest == v2, lane, LANE), axis=1, keepdims=True)
    w1 = 1.0 / (1.0 + jnp.exp(v2 - v1))
    gate_ref[...] = jnp.where(lane == i1, w1, 0.0) + jnp.where(lane == i2, 1.0 - w1, 0.0)

    member = jnp.where((lane == i1) | (lane == i2), 1.0, 0.0)
    carry = carry_ref[...]
    rank = _dot(ltri_ref[...], member.astype(BF16)) + carry
    pos = jnp.where(member > 0.0, rank, -1.0)
    pos_ref[...] = pos
    post_ref[...] = pos.T[:SUBLANE, :]
    before_ref[0] = carry
    carry = carry + jnp.sum(member, axis=0, keepdims=True)
    carry_ref[...] = carry
    total_ref[...] = carry


def moe_route(x, norm_gain, router):
    s, d = x.shape
    tm = min(MOE_TOK_TILE, s)
    nt = s // tm
    router_p = jnp.zeros((d, LANE), F32).at[:, :N_EXPERTS].set(router.astype(F32))
    ltri = jnp.tril(jnp.ones((tm, tm), F32), -1).astype(BF16)
    row = pl.BlockSpec((tm, LANE), lambda i: (i, 0))
    return pl.pallas_call(
        _route_kernel,
        grid=(nt,),
        in_specs=[pl.BlockSpec((tm, d), lambda i: (i, 0)), _const_spec((1, d)), _const_spec((d, LANE)),
                  _const_spec((tm, tm))],
        out_specs=[row, row, pl.BlockSpec((SUBLANE, tm), lambda i: (0, i)),
                   pl.BlockSpec((1, 1, LANE), lambda i: (i, 0, 0)), pl.BlockSpec((1, LANE), lambda i: (0, 0))],
        out_shape=[jax.ShapeDtypeStruct((s, LANE), F32), jax.ShapeDtypeStruct((s, LANE), F32),
                   jax.ShapeDtypeStruct((SUBLANE, s), F32), jax.ShapeDtypeStruct((nt, 1, LANE), F32),
                   jax.ShapeDtypeStruct((1, LANE), F32)],
        scratch_shapes=[pltpu.VMEM((1, LANE), F32)],
        compiler_params=_params("arbitrary"),
        name="moe_route",
    )(x, norm_gain.reshape(1, d).astype(F32), router_p, ltri)


def _moe_ffn_kernel(be_ref, r0_ref, tlo_ref, thi_ref, nv_ref, h_ref, post_ref, wg_ref, wu_ref, wd_ref,
                    y_ref, x_sc, acc_sc, *, tt):
    b = pl.program_id(0)
    f = pl.program_id(1)
    valid = b < nv_ref[0]
    tmb = x_sc.shape[0]

    @pl.when(valid & (f == 0))
    def _():
        e = be_ref[b]
        gr = MOE_GATHER_ROWS
        nsub = tmb // gr
        acc_sc[...] = jnp.zeros_like(acc_sc)
        for sb in range(nsub):
            want = (lax.broadcasted_iota(jnp.int32, (gr, 1), 0) + (r0_ref[b] + sb * gr)).astype(F32)

            def gather(t, carry, want=want, sb=sb):
                t0 = pl.multiple_of(t * tt, tt)
                p = post_ref[pl.ds(e, 1), pl.ds(t0, tt)]
                sel = jnp.where(p == want, 1.0, 0.0).astype(BF16)
                acc_sc[pl.ds(sb * gr, gr), :] += _dot(sel, h_ref[pl.ds(t0, tt), :])
                return carry

            lax.fori_loop(tlo_ref[b * nsub + sb], thi_ref[b * nsub + sb] + 1, gather, 0)
        x_sc[...] = acc_sc[...].astype(BF16)
        acc_sc[...] = jnp.zeros_like(acc_sc)

    @pl.when(valid)
    def _():
        xb = x_sc[...]
        a = _silu(_dot(xb, wg_ref[0].astype(BF16))) * _dot(xb, wu_ref[0].astype(BF16))
        acc_sc[...] += _dot(a.astype(BF16), wd_ref[0].astype(BF16))

    @pl.when(f == pl.num_programs(1) - 1)
    def _():
        y_ref[...] = jnp.where(valid, acc_sc[...], 0.0).astype(y_ref.dtype)


def moe_ffn(h, post, sched, w_gate, w_up, w_down, tf=512):
    s, d = h.shape
    ne, _, dff = w_gate.shape
    tmb = MOE_ROW_BLOCK
    tt = min(MOE_TOK_TILE, s)
    nb = sched[0].shape[0]
    nf = dff // tf

    def wspec(shape, fn):
        return pl.BlockSpec(shape, fn)

    def fidx(b, f, nv):
        return jnp.where(b < nv[0], f, nf - 1)

    grid_spec = pltpu.PrefetchScalarGridSpec(
        num_scalar_prefetch=5,
        grid=(nb, nf),
        in_specs=[pl.BlockSpec((s, d), lambda b, f, *_: (0, 0), pipeline_mode=pl.Buffered(1)),
                  pl.BlockSpec((SUBLANE, s), lambda b, f, *_: (0, 0), pipeline_mode=pl.Buffered(1)),
                  wspec((1, d, tf), lambda b, f, be, r0, tlo, thi, nv: (be[b], 0, fidx(b, f, nv))),
                  wspec((1, d, tf), lambda b, f, be, r0, tlo, thi, nv: (be[b], 0, fidx(b, f, nv))),
                  wspec((1, tf, d), lambda b, f, be, r0, tlo, thi, nv: (be[b], fidx(b, f, nv), 0))],
        out_specs=pl.BlockSpec((tmb, d), lambda b, f, *_: (b, 0)),
        scratch_shapes=[pltpu.VMEM((tmb, d), BF16), pltpu.VMEM((tmb, d), F32)],
    )
    return pl.pallas_call(
        functools.partial(_moe_ffn_kernel, tt=tt),
        grid_spec=grid_spec,
        out_shape=jax.ShapeDtypeStruct((nb * tmb, d), BF16),
        compiler_params=_params("arbitrary", "arbitrary"),
        name="moe_ffn",
    )(*sched, h, post, w_gate, w_up, w_down)


def _moe_combine_kernel(kb_ref, off_ref, lim_ref, x_ref, pos_ref, gate_ref, fg_ref, *rest):
    y_refs, o_ref, acc_sc = rest[:-2], rest[-2], rest[-1]
    t = pl.program_id(0)
    tm = x_ref.shape[0]
    yb = y_refs[0].shape[0]
    pos = pos_ref[...]
    gate = gate_ref[...]
    lane = lax.broadcasted_iota(jnp.int32, (tm, LANE), 1)
    col = lax.broadcasted_iota(jnp.int32, (1, yb), 1).astype(F32)
    def routed(e):
        pe = jnp.sum(jnp.where(lane == e, pos, 0.0), axis=1, keepdims=True)
        ge = jnp.sum(jnp.where(lane == e, gate, 0.0), axis=1, keepdims=True)
        r = jnp.where(pe >= 0.0, pe + off_ref[t * N_EXPERTS + e].astype(F32), -1.0)
        return r, ge

    def picked(e, k, r):
        sel = jnp.where(r == col + float(k * yb), 1.0, 0.0).astype(BF16)
        return _dot(sel, y_refs[MOE_Y_FETCH * e + k][...])

    acc = x_ref[...]
    for e in range(N_EXPERTS):
        r, ge = routed(e)
        acc = acc + ge * (picked(e, 0, r) + picked(e, 1, r))
    acc_sc[...] = acc
    for e in range(N_EXPERTS):
        for k in range(2, MOE_Y_FETCH):
            @pl.when(lim_ref[t * N_EXPERTS + e] > k * yb)
            def _(k=k, e=e):
                r, ge = routed(e)
                acc_sc[...] += ge * picked(e, k, r)
    o_ref[...] = _rms(acc_sc[...], fg_ref[...]).astype(o_ref.dtype)


def moe_combine(x, pos, gate, y, kb, off, lim, final_gain):
    s, d = x.shape
    tm = min(MOE_TOK_TILE, s)
    yb = MOE_Y_BLOCK
    last = y.shape[0] // yb - 1

    def yspec(e, k):
        return pl.BlockSpec((yb, d), lambda t, kb_r, *_: (jnp.minimum(kb_r[t * N_EXPERTS + e] + k, last), 0))

    grid_spec = pltpu.PrefetchScalarGridSpec(
        num_scalar_prefetch=3,
        grid=(s // tm,),
        in_specs=[pl.BlockSpec((tm, d), lambda t, *_: (t, 0)),
                  pl.BlockSpec((tm, LANE), lambda t, *_: (t, 0)),
                  pl.BlockSpec((tm, LANE), lambda t, *_: (t, 0)),
                  pl.BlockSpec((1, d), lambda t, *_: (0, 0))]
        + [yspec(e, k) for e in range(N_EXPERTS) for k in range(MOE_Y_FETCH)],
        out_specs=pl.BlockSpec((tm, d), lambda t, *_: (t, 0)),
        scratch_shapes=[pltpu.VMEM((tm, d), F32)],
    )
    return pl.pallas_call(
        _moe_combine_kernel,
        grid_spec=grid_spec,
        out_shape=jax.ShapeDtypeStruct((s, d), F32),
        compiler_params=_params("arbitrary"),
        name="moe_combine",
    )(kb, off, lim, x, pos, gate, final_gain.reshape(1, d).astype(F32), *([y] * (MOE_Y_FETCH * N_EXPERTS)))


def _moe_schedule(before, total, s):
    tmb, yb, gr = MOE_ROW_BLOCK, MOE_Y_BLOCK, MOE_GATHER_ROWS
    nb = 2 * s // tmb + N_EXPERTS
    counts = total[0, :N_EXPERTS].astype(jnp.int32)
    nblk = (counts + tmb - 1) // tmb
    end = jnp.cumsum(nblk)
    first = end - nblk
    nvalid = end[-1]
    b = jnp.minimum(jnp.arange(nb, dtype=jnp.int32), nvalid - 1)
    blk_e = jnp.sum(b[:, None] >= end[None, :], axis=1).astype(jnp.int32)
    r0 = (b - first[blk_e]) * tmb
    cb = before[:, 0, :N_EXPERTS].astype(jnp.int32)
    r0s = (r0[:, None] + gr * jnp.arange(tmb // gr, dtype=jnp.int32)[None, :]).reshape(-1)
    cbe = cb[:, jnp.repeat(blk_e, tmb // gr)]
    tlo = (jnp.sum(cbe <= r0s[None, :], axis=0) - 1).astype(jnp.int32)
    thi = (jnp.sum(cbe < (r0s + gr)[None, :], axis=0) - 1).astype(jnp.int32)
    sched = (blk_e, r0.astype(jnp.int32), tlo, thi, nvalid.reshape(1).astype(jnp.int32))
    row_start = first[None, :] * tmb + cb
    kb = row_start // yb
    off = first[None, :] * tmb - kb * yb
    n_te = jnp.concatenate([cb[1:], counts[None, :]], axis=0) - cb
    lim = row_start - kb * yb + n_te
    flat = lambda a: a.reshape(-1).astype(jnp.int32)
    return sched, flat(kb), flat(off), flat(lim)


def moe_sparse(h, x, norm_gain, router, w_gate, w_up, w_down, final_gain):
    s, _ = x.shape
    pos, gate, post, before, total = moe_route(x, norm_gain, router)
    sched, kb, off, lim = _moe_schedule(before, total, s)
    y = moe_ffn(h, post, sched, w_gate, w_up, w_down)
    return moe_combine(x, pos, gate, y, kb, off, lim, final_gain)


FOX_F0 = 7 * BRANCH_W


def _split_w_in_kernel(w_ref, mix_ref, f_ref, gate_ref):
    w = w_ref[...]
    rows = w.shape[0]
    mix_ref[:, :FOX_F0] = w[:, :FOX_F0].astype(BF16)
    mix_ref[:, FOX_F0:] = w[:, FOX_F0 + N_HEADS:N_MIX_COLS + N_HEADS].astype(BF16)
    f_ref[...] = jnp.concatenate([w[:, FOX_F0:FOX_F0 + N_HEADS], jnp.zeros((rows, LANE - N_HEADS), F32)],
                                 axis=1).astype(BF16)
    gate_ref[...] = w[:, N_MIX_COLS + N_HEADS:].astype(BF16)


def _mixer_weights(w_in, layer, tr=128):
    _, d, cols = w_in.shape
    row = lambda w: pl.BlockSpec((tr, w), lambda i: (i, 0))
    return pl.pallas_call(
        _split_w_in_kernel,
        grid=(d // tr,),
        in_specs=[pl.BlockSpec((None, tr, cols), lambda i: (layer, i, 0))],
        out_specs=[row(N_MIX_COLS), row(LANE), row(N_BRANCH * D_MODEL)],
        out_shape=[jax.ShapeDtypeStruct((d, N_MIX_COLS), BF16), jax.ShapeDtypeStruct((d, LANE), BF16),
                   jax.ShapeDtypeStruct((d, N_BRANCH * D_MODEL), BF16)],
        compiler_params=_params("parallel"),
        name="split_w_in",
    )(w_in)


def kernel(x, w_in, w_branch, w_out, norm_mix_g, hgrn_lb_logits, hgrn_norm_g, fox_f_bias, pool_w, pool_scale,
           ret_gn_g, ret_gn_b, norm_ffn_g, ffn_w_gate, ffn_w_up, ffn_w_down, moe_router, moe_w_gate, moe_w_up,
           moe_w_down, final_norm_g):
    b, s, d = x.shape
    assert b == 1 and d == D_MODEL
    depth = w_in.shape[0]
    assert depth == 2, "layer 0 uses the dense FFN, layer 1 the experts and the final norm"
    xs = x.reshape(s, d)
    h = rmsnorm_bf16(xs, norm_mix_g[0])
    out = None
    for layer in range(depth):
        w_mix, w_f, w_gate = _mixer_weights(w_in, layer)
        proj = matmul(h, w_mix)
        ct, fox_stats = fox_gate(h, proj, w_f, fox_f_bias[layer])
        branches = (
            hgrn2(proj, hgrn_lb_logits, hgrn_norm_g[layer], layer),
            fox_attention(proj, ct, fox_stats),
            pool_mixer(proj, pool_w[layer], pool_scale[layer]),
            retention(proj, ret_gn_g[layer], ret_gn_b[layer]),
        )
        xs, h2 = merge(h, branches, xs, w_gate, w_branch[layer].astype(BF16), w_out[layer].astype(BF16),
                       norm_ffn_g[layer])
        if layer % 2 == 0:
            li = layer // 2
            xs, h = ffn_dense(h2, xs, ffn_w_gate[li], ffn_w_up[li], ffn_w_down[li], norm_mix_g[layer + 1])
        else:
            li = layer // 2
            out = moe_sparse(h2, xs, norm_ffn_g[layer], moe_router[li], moe_w_gate[li], moe_w_up[li],
                             moe_w_down[li], final_norm_g)
    return out.reshape(b, s, d)
```

```python
import functools
import math

import jax
import jax.numpy as jnp
from jax import lax
from jax.experimental import pallas as pl
from jax.experimental.pallas import tpu as pltpu

D_MODEL = 1024
N_BRANCH = 4
BRANCH_W = D_MODEL // N_BRANCH
HEAD_DIM = 64
N_HEADS = BRANCH_W // HEAD_DIM
POOL_WINDOWS = (2, 4, 8, 16)
POOL_GROUP = BRANCH_W // len(POOL_WINDOWS)
POOL_HALO = 16
RET_DECAY_BASE = 5.0
ROPE_BASE = 10000.0
D_FF = 7 * D_MODEL // 2
N_EXPERTS = 8
RMS_EPS = 1e-6
LN_EPS = 1e-5
N_MIX_COLS = 12 * BRANCH_W

LANE = 128
SUBLANE = 8
VMEM_LIMIT = 56 * 1024 * 1024

HG_CHUNK = 64
HG_SUB = 16
HG_FAST_MIN_LOGDECAY = -60.0
RET_CHUNK = 256

F32 = jnp.float32
BF16 = jnp.bfloat16
NT_DIMS = (((1,), (1,)), ((), ()))


def _params(*sem):
    return pltpu.CompilerParams(dimension_semantics=sem, vmem_limit_bytes=VMEM_LIMIT)


def _const_spec(shape):
    nd = len(shape)
    return pl.BlockSpec(shape, lambda *_: (0,) * nd, pipeline_mode=pl.Buffered(1))


def _split3(x):
    hi = x.astype(BF16)
    r1 = x - hi.astype(F32)
    mid = r1.astype(BF16)
    lo = (r1 - mid.astype(F32)).astype(BF16)
    return hi, mid, lo


def _dot(a, b):
    return jnp.dot(a, b, preferred_element_type=F32)


def _dot_nt(a, b):
    return lax.dot_general(a, b, NT_DIMS, preferred_element_type=F32)


def _dot_exact_rhs(x, m_bf16):
    hi, mid, lo = _split3(x)
    return _dot(hi, m_bf16) + _dot(mid, m_bf16) + _dot(lo, m_bf16)


def _dot_exact_lhs(m_bf16, x):
    hi, mid, lo = _split3(x)
    return _dot(m_bf16, hi) + _dot(m_bf16, mid) + _dot(m_bf16, lo)


def _sigmoid(x):
    return 1.0 / (1.0 + jnp.exp(-x))


def _silu(x):
    return x * _sigmoid(x)


def _rms(x, gain):
    return x * lax.rsqrt(jnp.mean(x * x, axis=-1, keepdims=True) + RMS_EPS) * gain


def _head_of(n):
    return jnp.arange(n) // HEAD_DIM


def _head_ones():
    h = _head_of(BRANCH_W)
    return (h[:, None] == h[None, :]).astype(BF16)


def _head_masks():
    return (_head_of(BRANCH_W)[None, :] == jnp.arange(N_HEADS)[:, None]).astype(F32)


def _rmsnorm_kernel(x_ref, g_ref, o_ref):
    o_ref[...] = _rms(x_ref[...], g_ref[...]).astype(o_ref.dtype)


def rmsnorm_bf16(x, gain, tm=1024):
    s, d = x.shape
    tm = min(tm, s)
    return pl.pallas_call(
        _rmsnorm_kernel,
        grid=(s // tm,),
        in_specs=[pl.BlockSpec((tm, d), lambda i: (i, 0)), _const_spec((1, d))],
        out_specs=pl.BlockSpec((tm, d), lambda i: (i, 0)),
        out_shape=jax.ShapeDtypeStruct((s, d), BF16),
        compiler_params=_params("parallel"),
        name="rmsnorm",
    )(x, gain.reshape(1, d))


def _matmul_kernel(a_ref, b_ref, o_ref):
    o_ref[...] = _dot(a_ref[...], b_ref[...]).astype(o_ref.dtype)


def matmul(a, b, out_dtype=BF16, tm=2048, tn=1024):
    m, k = a.shape
    _, n = b.shape
    tm = min(tm, m)
    return pl.pallas_call(
        _matmul_kernel,
        grid=(n // tn, m // tm),
        in_specs=[pl.BlockSpec((tm, k), lambda j, i: (i, 0)),
                  pl.BlockSpec((k, tn), lambda j, i: (0, j))],
        out_specs=pl.BlockSpec((tm, tn), lambda j, i: (i, j)),
        out_shape=jax.ShapeDtypeStruct((m, n), out_dtype),
        compiler_params=_params("parallel", "parallel"),
        name="in_proj",
    )(a, b)


def _hgrn_kernel(q_ref, f_ref, i_ref, g_ref, lbl_ref, ng_ref, ones_ref, hm_ref, tril_ref, halfsum_ref, fmask_ref,
                 o_ref, st_ref, bpad, kpad, vpad, astack, lf_sc, kk_sc, *, layer, tile):
    c, sub = HG_CHUNK, HG_SUB
    nsub = c // sub
    half = c // 2

    @pl.when(pl.program_id(0) == 0)
    def _():
        st_ref[...] = jnp.zeros_like(st_ref)
        bpad[...] = jnp.zeros_like(bpad)
        kpad[...] = jnp.zeros_like(kpad)
        vpad[...] = jnp.zeros_like(vpad)

    lbl = lbl_ref[...]
    e = jnp.exp(lbl - jnp.max(lbl, axis=0, keepdims=True))
    p = e / jnp.sum(e, axis=0, keepdims=True)
    lb = jnp.zeros((1, BRANCH_W), F32)
    for l in range(1, layer + 1):
        lb = lb + p[l:l + 1, :]

    ones_bd = ones_ref[...]
    hm = hm_ref[...]
    tril = tril_ref[...]
    row = lax.broadcasted_iota(jnp.int32, (c, 1), 0)
    row_in_sub = row % sub
    bd_mask = ones_bd.astype(F32)

    sig = _sigmoid(f_ref[...].astype(F32))
    logf_all = jnp.log(lb + (1.0 - lb) * sig)
    lf_sc[...] = logf_all
    kk_sc[...] = (1.0 - lb) * (1.0 - sig)
    min_decay = jnp.min(_dot(halfsum_ref[...], logf_all.astype(BF16)))

    def load(ci):
        r0 = pl.multiple_of(ci * c, c)
        q = q_ref[pl.ds(r0, c), :].astype(F32)
        v = i_ref[pl.ds(r0, c), :].astype(F32)
        kk = kk_sc[pl.ds(r0, c), :]
        b = _dot_exact_lhs(tril, lf_sc[pl.ds(r0, c), :])
        return r0, q, v, kk, b

    def finish(r0, q_decayed, v, kk, b, intra):
        st = st_ref[...]
        inter = _dot_nt(q_decayed.astype(BF16), st.astype(BF16))
        b_last = b[c - 1:c, :]
        ks_end = (kk * jnp.exp(b_last - b)).astype(BF16)
        upd = _dot(v.T.astype(BF16), ks_end)
        st_ref[...] = st * jnp.exp(b_last) + upd * bd_mask
        o = intra + inter
        ms = _dot_exact_rhs(o * o, ones_bd) * (1.0 / HEAD_DIM)
        g = g_ref[pl.ds(r0, c), :].astype(F32)
        y = o * lax.rsqrt(ms + RMS_EPS) * ng_ref[...] * _silu(g)
        o_ref[pl.ds(r0, c), :] = y.astype(o_ref.dtype)

    def fast_chunk(ci, carry):
        r0, q, v, kk, b = load(ci)
        second = row >= half
        m_row = b[half - 1:half, :]
        mref = jnp.where(second, m_row, 0.0)
        qp = q * jnp.exp(b - mref)
        kp = kk * jnp.exp(mref - b)
        e_m = jnp.exp(m_row)
        kaug = jnp.concatenate([kp, kp[:half, :] * e_m], axis=0).astype(BF16)
        vaug = jnp.concatenate([v, v[:half, :]], axis=0).astype(BF16)
        qx = jnp.concatenate([qp * hm[h:h + 1, :] for h in range(N_HEADS)], axis=0).astype(BF16)
        sc = jnp.where(fmask_ref[...] > 0.0, _dot_nt(qx, kaug), 0.0)
        r = _dot(sc.astype(BF16), vaug)
        intra = jnp.zeros((c, BRANCH_W), F32)
        for h in range(N_HEADS):
            intra = intra + r[h * c:(h + 1) * c, :] * hm[h:h + 1, :]
        finish(r0, jnp.where(second, qp * e_m, qp), v, kk, b, intra)
        return carry

    def exact_chunk(ci, carry):
        r0, q, v, kk, b = load(ci)

        bpad[pl.ds(sub, c), :] = b
        kpad[pl.ds(sub, c), :] = kk
        vpad[pl.ds(sub, c), :] = v

        for d in range(sub):
            b_d = bpad[pl.ds(sub - d, c), :]
            k_d = kpad[pl.ds(sub - d, c), :]
            a = jnp.where(row_in_sub >= d, q * k_d * jnp.exp(b - b_d), 0.0)
            astack[pl.ds(d * c, c), :] = a.astype(BF16)
        pall = _dot(astack[...], ones_bd)
        intra = jnp.zeros((c, BRANCH_W), F32)
        for d in range(sub):
            intra = intra + pall[d * c:(d + 1) * c, :] * vpad[pl.ds(sub - d, c), :]

        pieces = [jnp.zeros((sub, BRANCH_W), F32)]
        for si in range(1, nsub):
            lo = si * sub
            m_i = b[lo - 1:lo, :]
            qs = q[lo:lo + sub, :] * jnp.exp(b[lo:lo + sub, :] - m_i)
            ks = (kk[:lo, :] * jnp.exp(m_i - b[:lo, :])).astype(BF16)
            qx = jnp.concatenate([qs * hm[h:h + 1, :] for h in range(N_HEADS)], axis=0).astype(BF16)
            sc = _dot_nt(qx, ks)
            r = _dot(sc.astype(BF16), v[:lo, :].astype(BF16))
            acc = jnp.zeros((sub, BRANCH_W), F32)
            for h in range(N_HEADS):
                acc = acc + r[h * sub:(h + 1) * sub, :] * hm[h:h + 1, :]
            pieces.append(acc)
        intra = intra + jnp.concatenate(pieces, axis=0)
        finish(r0, q * jnp.exp(b), v, kk, b, intra)
        return carry

    lax.cond(min_decay >= HG_FAST_MIN_LOGDECAY,
             lambda: lax.fori_loop(0, tile // c, fast_chunk, 0, unroll=2),
             lambda: lax.fori_loop(0, tile // c, exact_chunk, 0))


def hgrn2(proj, lb_logits, norm_g, layer, tile=512):
    s = proj.shape[0]
    depth = lb_logits.shape[0]
    c, sub = HG_CHUNK, HG_SUB
    half = c // 2
    tile = min(tile, s)
    col = lambda j: pl.BlockSpec((tile, BRANCH_W), lambda i, j=j: (i, j))
    tril = jnp.tril(jnp.ones((c, c), F32)).astype(BF16)
    nhalf = tile // half
    halfsum = (jnp.arange(tile)[None, :] // half == jnp.arange(nhalf)[:, None]).astype(BF16)
    t = jnp.arange(c)[:, None]
    col_s = jnp.arange(c + half)[None, :]
    same_half = (col_s < c) & (col_s // half == t // half) & (col_s <= t)
    cross = (col_s >= c) & (t >= half)
    fmask = jnp.tile((same_half | cross).astype(F32), (N_HEADS, 1))
    return pl.pallas_call(
        functools.partial(_hgrn_kernel, layer=layer, tile=tile),
        grid=(s // tile,),
        in_specs=[col(0), col(1), col(2), col(3),
                  _const_spec((depth, BRANCH_W)), _const_spec((1, BRANCH_W)),
                  _const_spec((BRANCH_W, BRANCH_W)), _const_spec((N_HEADS, BRANCH_W)),
                  _const_spec((c, c)), _const_spec((nhalf, tile)), _const_spec((N_HEADS * c, c + half))],
        out_specs=pl.BlockSpec((tile, BRANCH_W), lambda i: (i, 0)),
        out_shape=jax.ShapeDtypeStruct((s, BRANCH_W), BF16),
        scratch_shapes=[pltpu.VMEM((BRANCH_W, BRANCH_W), F32),
                        pltpu.VMEM((c + sub, BRANCH_W), F32),
                        pltpu.VMEM((c + sub, BRANCH_W), F32),
                        pltpu.VMEM((c + sub, BRANCH_W), F32),
                        pltpu.VMEM((sub * c, BRANCH_W), BF16),
                        pltpu.VMEM((tile, BRANCH_W), F32),
                        pltpu.VMEM((tile, BRANCH_W), F32)],
        compiler_params=_params("arbitrary"),
        name="hgrn2",
    )(proj, proj, proj, proj, lb_logits.astype(F32), norm_g.reshape(1, BRANCH_W).astype(F32),
      _head_ones(), _head_masks(), tril, halfsum, fmask)


FOX_TILE = 256
FOX_NSTAT = 16
FOX_SKIP_LOG = 40.0


def _fox_gate_kernel(h_ref, q_ref, k_ref, wf_ref, bias_ref, tril_ref, ones_ref,
                     ct_ref, stat_ref, carry_ref, kmax_ref):
    @pl.when(pl.program_id(0) == 0)
    def _():
        carry_ref[...] = jnp.zeros_like(carry_ref)
        kmax_ref[...] = jnp.zeros_like(kmax_ref)

    logit = _dot(h_ref[...], wf_ref[...]) + bias_ref[...]
    logf = jnp.minimum(logit, 0.0) - jnp.log(1.0 + jnp.exp(-jnp.abs(logit)))
    cum = _dot_exact_lhs(tril_ref[...], logf) + carry_ref[...]
    carry_ref[...] = cum[-1:, :]
    ct_ref[...] = cum.T[:SUBLANE, :]

    ones_bd = ones_ref[...]
    q = q_ref[...].astype(F32)
    k = k_ref[...].astype(F32)
    scale = HEAD_DIM ** -0.5
    head_lane = lax.broadcasted_iota(jnp.int32, (1, BRANCH_W), 1) // HEAD_DIM
    c_heads = jnp.zeros(q.shape, F32)
    for h in range(N_HEADS):
        c_heads = jnp.where(head_lane == h, cum[:, h:h + 1], c_heads)
    slack = 1.0 + 2.0 ** -6
    qn = jnp.sqrt(_dot((q * q).astype(BF16), ones_bd)) * (scale * slack)
    kn = jnp.sqrt(_dot((k * k).astype(BF16), ones_bd)) * slack
    diag = _dot((q * k).astype(BF16), ones_bd) * scale - (2.0 ** -6) * qn * kn
    kmax = jnp.maximum(kmax_ref[...], jnp.max(kn, axis=0, keepdims=True))
    kmax_ref[...] = kmax
    rows = [jnp.max(qn, axis=0, keepdims=True),
            jnp.max(c_heads - diag, axis=0, keepdims=True),
            kmax,
            c_heads[-1:, :]]
    stat_ref[0] = jnp.concatenate(rows + [jnp.zeros((SUBLANE - len(rows), BRANCH_W), F32)], axis=0)


def fox_gate(h, proj, w_f, f_bias):
    s, d = h.shape
    tile = min(FOX_TILE, s)
    bias = jnp.zeros((1, LANE), F32).at[0, :N_HEADS].set(f_bias.astype(F32))
    tril = jnp.tril(jnp.ones((tile, tile), F32)).astype(BF16)
    ct, stats = pl.pallas_call(
        _fox_gate_kernel,
        grid=(s // tile,),
        in_specs=[pl.BlockSpec((tile, d), lambda i: (i, 0)),
                  pl.BlockSpec((tile, BRANCH_W), lambda i: (i, 4)),
                  pl.BlockSpec((tile, BRANCH_W), lambda i: (i, 5)),
                  _const_spec((d, LANE)), _const_spec((1, LANE)), _const_spec((tile, tile)),
                  _const_spec((BRANCH_W, BRANCH_W))],
        out_specs=[pl.BlockSpec((SUBLANE, tile), lambda i: (0, i)),
                   pl.BlockSpec((1, SUBLANE, BRANCH_W), lambda i: (i, 0, 0))],
        out_shape=[jax.ShapeDtypeStruct((SUBLANE, s), F32),
                   jax.ShapeDtypeStruct((s // tile, SUBLANE, BRANCH_W), F32)],
        scratch_shapes=[pltpu.VMEM((1, LANE), F32), pltpu.VMEM((1, BRANCH_W), F32)],
        compiler_params=_params("arbitrary"),
        name="fox_gate",
    )(h, proj, proj, w_f, bias, tril, _head_ones())
    return ct, stats[:, :4, ::HEAD_DIM].reshape(-1)


def _fox_kernel(stat_ref, q_ref, k_ref, v_ref, ct_ref, hm_ref, o_ref, m_sc, l_sc, acc_sc, *, tq):
    i = pl.program_id(0)
    q0 = pl.multiple_of(i * tq, tq)
    hm = hm_ref[...]
    q = q_ref[...].astype(F32) * (HEAD_DIM ** -0.5)
    qh = [(q * hm[h:h + 1, :]).astype(BF16) for h in range(N_HEADS)]
    c_q0 = ct_ref[:, pl.ds(q0, tq)][:, 0:1]

    first = i
    for h in range(N_HEADS):
        qmax = stat_ref[i * FOX_NSTAT + h]
        emax = stat_ref[i * FOX_NSTAT + N_HEADS + h]

        def needed(j, h=h, qmax=qmax, emax=emax):
            jc = jnp.maximum(j, 0)
            bound = (qmax * stat_ref[jc * FOX_NSTAT + 2 * N_HEADS + h] + emax
                     - stat_ref[jc * FOX_NSTAT + 3 * N_HEADS + h])
            return (j >= 0) & (bound >= -FOX_SKIP_LOG)

        last_dropped = lax.while_loop(needed, lambda j: j - 1, i - 1)
        first = jnp.minimum(first, last_dropped + 1)

    m_sc[...] = jnp.full_like(m_sc, -jnp.inf)
    l_sc[...] = jnp.zeros_like(l_sc)
    acc_sc[...] = jnp.zeros_like(acc_sc)

    def block(s0, diagonal):
        kb = k_ref[pl.ds(s0, tq), :]
        vb = v_ref[pl.ds(s0, tq), :]
        bias = c_q0 - ct_ref[:, pl.ds(s0, tq)]
        for h in range(N_HEADS):
            sc = _dot_nt(qh[h], kb) + bias[h:h + 1, :]
            if diagonal:
                r = lax.broadcasted_iota(jnp.int32, (tq, tq), 0)
                cidx = lax.broadcasted_iota(jnp.int32, (tq, tq), 1)
                sc = jnp.where(cidx <= r, sc, -jnp.inf)
            m_prev = m_sc[h]
            m_new = jnp.maximum(m_prev, jnp.max(sc, axis=1, keepdims=True))
            alpha = jnp.exp(m_prev - m_new)
            p = jnp.exp(sc - jnp.tile(m_new, (1, tq // LANE)))
            l_sc[h] = alpha * l_sc[h] + jnp.sum(p, axis=1, keepdims=True)
            acc_sc[h] = acc_sc[h] * jnp.tile(alpha, (1, BRANCH_W // LANE)) + _dot(p.astype(BF16), vb)
            m_sc[h] = m_new

    def body(j, carry):
        block(pl.multiple_of(j * tq, tq), False)
        return carry

    lax.fori_loop(first, i, body, 0)
    block(q0, True)

    out = jnp.zeros((tq, BRANCH_W), F32)
    for h in range(N_HEADS):
        out = out + acc_sc[h] * hm[h:h + 1, :] / jnp.tile(l_sc[h], (1, BRANCH_W // LANE))
    o_ref[...] = out.astype(o_ref.dtype)


def fox_attention(proj, ct, stats):
    s = proj.shape[0]
    tq = min(FOX_TILE, s)
    full = lambda j: pl.BlockSpec((s, BRANCH_W), lambda i, j=j: (0, j), pipeline_mode=pl.Buffered(1))
    return pl.pallas_call(
        functools.partial(_fox_kernel, tq=tq),
        grid=(s // tq,),
        in_specs=[pl.BlockSpec(memory_space=pltpu.SMEM),
                  pl.BlockSpec((tq, BRANCH_W), lambda i: (i, 4)), full(5), full(6),
                  _const_spec((SUBLANE, s)), _const_spec((N_HEADS, BRANCH_W))],
        out_specs=pl.BlockSpec((tq, BRANCH_W), lambda i: (i, 0)),
        out_shape=jax.ShapeDtypeStruct((s, BRANCH_W), BF16),
        scratch_shapes=[pltpu.VMEM((N_HEADS, tq, LANE), F32),
                        pltpu.VMEM((N_HEADS, tq, LANE), F32),
                        pltpu.VMEM((N_HEADS, tq, BRANCH_W), F32)],
        compiler_params=_params("parallel"),
        name="fox_attention",
    )(stats, proj, proj, proj, ct, _head_masks())


def _pool_kernel(u_ref, w_ref, scale_ref, o_ref, ext, *, tile):
    i = pl.program_id(0)

    @pl.when(i == 0)
    def _():
        ext[pl.ds(0, POOL_HALO), :] = jnp.zeros((POOL_HALO, BRANCH_W), F32)

    u = u_ref[...].astype(F32)
    ext[pl.ds(POOL_HALO, tile), :] = u
    pos = (i * tile + lax.broadcasted_iota(jnp.int32, (tile, 1), 0) + 1).astype(F32)
    halves = []
    for half in range(BRANCH_W // LANE):
        lanes = pl.ds(half * LANE, LANE)
        w_small, w_big = POOL_WINDOWS[2 * half], POOL_WINDOWS[2 * half + 1]
        run = u[:, half * LANE:(half + 1) * LANE]
        sums = {}
        for j in range(1, w_big):
            if j == w_small:
                sums[w_small] = run
            run = run + ext[pl.ds(POOL_HALO - j, tile), lanes]
        sums[w_big] = run
        lane = lax.broadcasted_iota(jnp.int32, (1, LANE), 1)
        small = lane < POOL_GROUP
        total = jnp.where(small, sums[w_small], sums[w_big])
        count = jnp.where(small, jnp.minimum(pos, float(w_small)), jnp.minimum(pos, float(w_big)))
        halves.append(total / count)
    mean = jnp.concatenate(halves, axis=1)
    d = (mean - u).astype(BF16)
    y = _dot(d, w_ref[...]) * scale_ref[...]
    o_ref[...] = y.astype(o_ref.dtype)
    ext[pl.ds(0, POOL_HALO), :] = u[tile - POOL_HALO:, :]


def pool_mixer(proj, w_pool, scale, tile=512):
    s = proj.shape[0]
    tile = min(tile, s)
    ng = len(POOL_WINDOWS)
    w_bd = jnp.zeros((BRANCH_W, BRANCH_W), F32)
    for gi in range(ng):
        lo = gi * POOL_GROUP
        w_bd = w_bd.at[lo:lo + POOL_GROUP, lo:lo + POOL_GROUP].set(w_pool[gi].astype(F32))
    return pl.pallas_call(
        functools.partial(_pool_kernel, tile=tile),
        grid=(s // tile,),
        in_specs=[pl.BlockSpec((tile, BRANCH_W), lambda i: (i, 7)),
                  _const_spec((BRANCH_W, BRANCH_W)), _const_spec((1, BRANCH_W))],
        out_specs=pl.BlockSpec((tile, BRANCH_W), lambda i: (i, 0)),
        out_shape=jax.ShapeDtypeStruct((s, BRANCH_W), BF16),
        scratch_shapes=[pltpu.VMEM((tile + POOL_HALO, BRANCH_W), F32)],
        compiler_params=_params("arbitrary"),
        name="pool_mixer",
    )(proj, w_bd.astype(BF16), scale.reshape(1, BRANCH_W).astype(F32))


def _ret_kernel(q_ref, k_ref, v_ref, g_ref, rope_ref, ecos_ref, esin_ref, perm_ref, ones_ref, hm_ref,
                dstack_ref, xi_ref, zeta_ref, gc_ref, gng_ref, gnb_ref, o_ref, st_ref, *, tile):
    c = RET_CHUNK

    @pl.when(pl.program_id(0) == 0)
    def _():
        st_ref[...] = jnp.zeros_like(st_ref)

    perm = perm_ref[...]
    ones_bd = ones_ref[...]
    bd_mask = ones_bd.astype(F32)
    hm = hm_ref[...]

    def chunk(ci, carry):
        r0 = pl.multiple_of(ci * c, c)
        qb = q_ref[pl.ds(r0, c), :]
        kb = k_ref[pl.ds(r0, c), :]
        v = v_ref[pl.ds(r0, c), :]
        g = g_ref[pl.ds(r0, c), :].astype(F32)
        tab = rope_ref[pl.ds(r0, c), :]
        cos = _dot_exact_rhs(tab, ecos_ref[...])
        sin = _dot_exact_rhs(tab, esin_ref[...])
        qr = qb.astype(F32) * cos + _dot(qb, perm) * sin
        kr = (kb.astype(F32) * cos + _dot(kb, perm) * sin) * (HEAD_DIM ** -0.5)

        qx = jnp.concatenate([qr * hm[h:h + 1, :] for h in range(N_HEADS)], axis=0).astype(BF16)
        sc = _dot_nt(qx, kr.astype(BF16)) * dstack_ref[...]
        r = _dot(sc.astype(BF16), v)
        intra = jnp.zeros((c, BRANCH_W), F32)
        for h in range(N_HEADS):
            intra = intra + r[h * c:(h + 1) * c, :] * hm[h:h + 1, :]

        st = st_ref[...]
        inter = _dot_nt((qr * xi_ref[...]).astype(BF16), st.astype(BF16))
        upd = _dot(v.astype(F32).T.astype(BF16), (kr * zeta_ref[...]).astype(BF16))
        st_ref[...] = st * gc_ref[...] + upd * bd_mask

        o = intra + inter
        mu = _dot_exact_rhs(o, ones_bd) * (1.0 / HEAD_DIM)
        cen = o - mu
        var = _dot_exact_rhs(cen * cen, ones_bd) * (1.0 / HEAD_DIM)
        y = cen * lax.rsqrt(var + LN_EPS) * gng_ref[...] + gnb_ref[...]
        o_ref[pl.ds(r0, c), :] = (y * _silu(g)).astype(o_ref.dtype)
        return carry

    lax.fori_loop(0, tile // c, chunk, 0, unroll=True)


def _rope_tables(s):
    half = HEAD_DIM // 2
    pos = jnp.arange(s, dtype=F32)
    inv_freq = ROPE_BASE ** (-jnp.arange(half, dtype=F32) / half)
    ang = pos[:, None] * inv_freq[None, :]
    table = jnp.concatenate([jnp.cos(ang), jnp.sin(ang), jnp.zeros((s, LANE - 2 * half), F32)], axis=1)
    lane = jnp.arange(BRANCH_W)
    src = jnp.arange(LANE)[:, None]
    ecos = (src == (lane % half)[None, :]).astype(F32)
    sign = jnp.where(lane % HEAD_DIM < half, -1.0, 1.0)
    esin = (src == (half + lane % half)[None, :]).astype(F32) * sign[None, :]
    return table, ecos.astype(BF16), esin.astype(BF16)


def _ret_constants():
    c = RET_CHUNK
    half = HEAD_DIM // 2
    lane = jnp.arange(BRANCH_W)
    partner = jnp.where(lane % HEAD_DIM < half, lane + half, lane - half)
    perm = (lane[:, None] == partner[None, :]).astype(BF16)
    log_gamma = jnp.log1p(-jnp.exp2(-RET_DECAY_BASE - jnp.arange(N_HEADS, dtype=F32)))
    ci = jnp.arange(c, dtype=F32)
    diff = ci[:, None] - ci[None, :]
    intra = jnp.where(diff >= 0, jnp.exp(diff * log_gamma[:, None, None]), 0.0)
    dstack = intra.reshape(N_HEADS * c, c)
    lg_lane = jnp.repeat(log_gamma, HEAD_DIM)[None, :]
    xi = jnp.exp((ci[:, None] + 1.0) * lg_lane)
    zeta = jnp.exp((c - 1.0 - ci[:, None]) * lg_lane)
    gc = jnp.exp(c * lg_lane)
    return perm, dstack, xi, zeta, gc


def retention(proj, gn_g, gn_b, tile=512):
    s = proj.shape[0]
    c = RET_CHUNK
    tile = min(tile, s)
    rope, ecos, esin = _rope_tables(s)
    perm, dstack, xi, zeta, gc = _ret_constants()
    col = lambda j: pl.BlockSpec((tile, BRANCH_W), lambda i, j=j: (i, j))
    row = pl.BlockSpec((tile, BRANCH_W), lambda i: (i, 0))
    return pl.pallas_call(
        functools.partial(_ret_kernel, tile=tile),
        grid=(s // tile,),
        in_specs=[col(8), col(9), col(10), col(11), pl.BlockSpec((tile, LANE), lambda i: (i, 0)),
                  _const_spec((LANE, BRANCH_W)), _const_spec((LANE, BRANCH_W)),
                  _const_spec((BRANCH_W, BRANCH_W)), _const_spec((BRANCH_W, BRANCH_W)),
                  _const_spec((N_HEADS, BRANCH_W)), _const_spec((N_HEADS * c, c)),
                  _const_spec((c, BRANCH_W)), _const_spec((c, BRANCH_W)), _const_spec((1, BRANCH_W)),
                  _const_spec((1, BRANCH_W)), _const_spec((1, BRANCH_W))],
        out_specs=row,
        out_shape=jax.ShapeDtypeStruct((s, BRANCH_W), BF16),
        scratch_shapes=[pltpu.VMEM((BRANCH_W, BRANCH_W), F32)],
        compiler_params=_params("arbitrary"),
        name="retention",
    )(proj, proj, proj, proj, rope, ecos, esin, perm, _head_ones(), _head_masks(), dstack, xi, zeta, gc,
      gn_g.reshape(1, BRANCH_W).astype(F32), gn_b.reshape(1, BRANCH_W).astype(F32))


def _merge_kernel(h_ref, o0_ref, o1_ref, o2_ref, o3_ref, x_ref, wg_ref, wb_ref, wo_ref, g_ref,
                  xo_ref, ho_ref):
    h = h_ref[...]
    merged = jnp.zeros(x_ref.shape, F32)
    for bi, o_ref in enumerate((o0_ref, o1_ref, o2_ref, o3_ref)):
        gate = _sigmoid(_dot(h, wg_ref[:, bi * D_MODEL:(bi + 1) * D_MODEL]))
        merged = merged + gate * _dot(o_ref[...], wb_ref[bi])
    x_new = x_ref[...] + _dot(merged.astype(BF16), wo_ref[...])
    xo_ref[...] = x_new
    ho_ref[...] = _rms(x_new, g_ref[...]).astype(ho_ref.dtype)


def merge(h, branches, x, w_gate, w_branch, w_out, next_gain, tm=512):
    s, d = x.shape
    tm = min(tm, s)
    row = lambda w: pl.BlockSpec((tm, w), lambda i: (i, 0))
    return pl.pallas_call(
        _merge_kernel,
        grid=(s // tm,),
        in_specs=[row(d), row(BRANCH_W), row(BRANCH_W), row(BRANCH_W), row(BRANCH_W), row(d),
                  _const_spec((d, N_BRANCH * d)), _const_spec((N_BRANCH, BRANCH_W, d)),
                  _const_spec((d, d)), _const_spec((1, d))],
        out_specs=[row(d), row(d)],
        out_shape=[jax.ShapeDtypeStruct((s, d), F32), jax.ShapeDtypeStruct((s, d), BF16)],
        compiler_params=_params("parallel"),
        name="merge",
    )(h, *branches, x, w_gate, w_branch, w_out, next_gain.reshape(1, d).astype(F32))


def _ffn_kernel(h_ref, x_ref, wg_ref, wu_ref, wd_ref, g_ref, xo_ref, ho_ref, acc_ref):
    f = pl.program_id(1)

    @pl.when(f == 0)
    def _():
        acc_ref[...] = jnp.zeros_like(acc_ref)

    h = h_ref[...]
    a = _silu(_dot(h, wg_ref[...].astype(BF16))) * _dot(h, wu_ref[...].astype(BF16))
    acc_ref[...] += _dot(a.astype(BF16), wd_ref[...].astype(BF16))

    @pl.when(f == pl.num_programs(1) - 1)
    def _():
        x_new = x_ref[...] + acc_ref[...]
        xo_ref[...] = x_new
        ho_ref[...] = _rms(x_new, g_ref[...]).astype(ho_ref.dtype)


def ffn_dense(h, x, w_gate, w_up, w_down, next_gain, tm=1024, tf=512):
    s, d = x.shape
    tm = min(tm, s)
    dff = w_gate.shape[1]
    row = lambda: pl.BlockSpec((tm, d), lambda i, f: (i, 0))
    return pl.pallas_call(
        _ffn_kernel,
        grid=(s // tm, dff // tf),
        in_specs=[row(), row(),
                  pl.BlockSpec((d, tf), lambda i, f: (0, f)),
                  pl.BlockSpec((d, tf), lambda i, f: (0, f)),
                  pl.BlockSpec((tf, d), lambda i, f: (f, 0)),
                  _const_spec((1, d))],
        out_specs=[row(), row()],
        out_shape=[jax.ShapeDtypeStruct((s, d), F32), jax.ShapeDtypeStruct((s, d), BF16)],
        scratch_shapes=[pltpu.VMEM((tm, d), F32)],
        compiler_params=_params("parallel", "arbitrary"),
        name="ffn_dense",
    )(h, x, w_gate, w_up, w_down, next_gain.reshape(1, d).astype(F32))


MOE_TOK_TILE = 256
MOE_ROW_BLOCK = 512
MOE_GATHER_ROWS = 128
MOE_GATHER_TILES = 4
MOE_VMEM_LIMIT = 60 * 1024 * 1024
MOE_Y_BLOCK = 128
MOE_Y_FETCH = MOE_TOK_TILE // MOE_Y_BLOCK + 1


def _dot_f32(x, w):
    xh, xm, _ = _split3(x)
    wh, wm, _ = _split3(w)
    return _dot(xh, wh) + (_dot(xh, wm) + _dot(xm, wh))


def _route_kernel(x_ref, ng_ref, router_ref, ltri_ref, pos_ref, gate_ref, post_ref, before_ref, total_ref,
                  carry_ref):
    tm = x_ref.shape[0]

    @pl.when(pl.program_id(0) == 0)
    def _():
        carry_ref[...] = jnp.zeros_like(carry_ref)

    hn = _rms(x_ref[...], ng_ref[...])
    logits = _dot_f32(hn, router_ref[...])
    lane = lax.broadcasted_iota(jnp.int32, (tm, LANE), 1)
    logits = jnp.where(lane < N_EXPERTS, logits, -jnp.inf)
    v1 = jnp.max(logits, axis=1, keepdims=True)
    i1 = jnp.min(jnp.where(logits == v1, lane, LANE), axis=1, keepdims=True)
    rest = jnp.where(lane == i1, -jnp.inf, logits)
    v2 = jnp.max(rest, axis=1, keepdims=True)
    i2 = jnp.min(jnp.where(rest == v2, lane, LANE), axis=1, keepdims=True)
    w1 = 1.0 / (1.0 + jnp.exp(v2 - v1))
    gate_ref[...] = jnp.where(lane == i1, w1, 0.0) + jnp.where(lane == i2, 1.0 - w1, 0.0)

    member = jnp.where((lane == i1) | (lane == i2), 1.0, 0.0)
    carry = carry_ref[...]
    rank = _dot(ltri_ref[...], member.astype(BF16)) + carry
    pos = jnp.where(member > 0.0, rank, -1.0)
    pos_ref[...] = pos
    post_ref[...] = pos.T[:SUBLANE, :]
    before_ref[0] = carry
    carry = carry + jnp.sum(member, axis=0, keepdims=True)
    carry_ref[...] = carry
    total_ref[...] = carry


def moe_route(x, norm_gain, router):
    s, d = x.shape
    tm = min(MOE_TOK_TILE, s)
    nt = s // tm
    router_p = jnp.zeros((d, LANE), F32).at[:, :N_EXPERTS].set(router.astype(F32))
    ltri = jnp.tril(jnp.ones((tm, tm), F32), -1).astype(BF16)
    row = pl.BlockSpec((tm, LANE), lambda i: (i, 0))
    return pl.pallas_call(
        _route_kernel,
        grid=(nt,),
        in_specs=[pl.BlockSpec((tm, d), lambda i: (i, 0)), _const_spec((1, d)), _const_spec((d, LANE)),
                  _const_spec((tm, tm))],
        out_specs=[row, row, pl.BlockSpec((SUBLANE, tm), lambda i: (0, i)),
                   pl.BlockSpec((1, 1, LANE), lambda i: (i, 0, 0)), pl.BlockSpec((1, LANE), lambda i: (0, 0))],
        out_shape=[jax.ShapeDtypeStruct((s, LANE), F32), jax.ShapeDtypeStruct((s, LANE), F32),
                   jax.ShapeDtypeStruct((SUBLANE, s), F32), jax.ShapeDtypeStruct((nt, 1, LANE), F32),
                   jax.ShapeDtypeStruct((1, LANE), F32)],
        scratch_shapes=[pltpu.VMEM((1, LANE), F32)],
        compiler_params=_params("arbitrary"),
        name="moe_route",
    )(x, norm_gain.reshape(1, d).astype(F32), router_p, ltri)


def _moe_ffn_kernel(be_ref, r0_ref, tlo_ref, thi_ref, nv_ref, h_ref, post_ref, wg_ref, wu_ref, wd_ref,
                    y_ref, x_sc, acc_sc, *, tt, nsub):
    b = pl.program_id(0)
    f = pl.program_id(1)
    nb = pl.num_programs(0)
    valid = b < nv_ref[0]
    tmb, d = acc_sc.shape
    gr = tmb // nsub
    last_tile = h_ref.shape[0] // tt - 1

    def picked(e, want, t):
        t0 = pl.multiple_of(t * tt, tt)
        p = post_ref[pl.ds(e, 1), pl.ds(t0, tt)]
        sel = jnp.where(p == want, 1.0, 0.0).astype(BF16)
        return _dot(sel, h_ref[pl.ds(t0, tt), :])

    def gather_head(blk, sb):
        e = be_ref[blk]
        lo = tlo_ref[blk * nsub + sb]
        hi = thi_ref[blk * nsub + sb]
        want = (lax.broadcasted_iota(jnp.int32, (gr, 1), 0) + (r0_ref[blk] + sb * gr)).astype(F32)
        rows = picked(e, want, lo)
        for k in range(1, MOE_GATHER_TILES):
            rows = rows + picked(e, jnp.where(lo + k <= hi, want, -2.0), jnp.minimum(lo + k, last_tile))
        return rows.astype(BF16)

    def gather_tail(slot, blk, sb):
        e = be_ref[blk]
        want = (lax.broadcasted_iota(jnp.int32, (gr, 1), 0) + (r0_ref[blk] + sb * gr)).astype(F32)
        rows = pl.ds(pl.multiple_of(sb * gr, gr), gr)

        def more(t, carry):
            x_sc[slot, rows, :] = (x_sc[slot, rows, :].astype(F32) + picked(e, want, t)).astype(BF16)
            return carry

        lax.fori_loop(tlo_ref[blk * nsub + sb] + MOE_GATHER_TILES, thi_ref[blk * nsub + sb] + 1, more, 0)

    @pl.when((b == 0) & (f == 0))
    def _():
        for sb in range(nsub):
            x_sc[0, pl.ds(sb * gr, gr), :] = gather_head(0, sb)
            gather_tail(0, 0, sb)

    nxt = jnp.minimum(b + 1, nb - 1)
    nslot = (b + 1) % 2

    @pl.when(valid)
    def _():
        x_sc[nslot, pl.ds(pl.multiple_of(f * gr, gr), gr), :] = gather_head(nxt, f)
        xb = x_sc[b % 2]
        a = _silu(_dot(xb, wg_ref[0])) * _dot(xb, wu_ref[0])
        acc_sc[...] = jnp.where(f == 0, 0.0, acc_sc[...]) + _dot(a.astype(BF16), wd_ref[0])

    @pl.when(valid & (thi_ref[nxt * nsub + f] - tlo_ref[nxt * nsub + f] >= MOE_GATHER_TILES))
    def _():
        gather_tail(nslot, nxt, f)

    @pl.when(f == nsub - 1)
    def _():
        y_ref[...] = jnp.where(valid, acc_sc[...], 0.0).astype(y_ref.dtype)


def moe_ffn(h, post, sched, w_gate, w_up, w_down):
    s, d = h.shape
    ne, _, dff = w_gate.shape
    tmb = MOE_ROW_BLOCK
    tt = min(MOE_TOK_TILE, s)
    nb = sched[0].shape[0]
    nf = tmb // MOE_GATHER_ROWS
    tf = dff // nf

    def fidx(b, f, nv):
        return jnp.where(b < nv[0], f, nf - 1)

    grid_spec = pltpu.PrefetchScalarGridSpec(
        num_scalar_prefetch=5,
        grid=(nb, nf),
        in_specs=[pl.BlockSpec((s, d), lambda b, f, *_: (0, 0), pipeline_mode=pl.Buffered(1)),
                  pl.BlockSpec((SUBLANE, s), lambda b, f, *_: (0, 0), pipeline_mode=pl.Buffered(1)),
                  pl.BlockSpec((1, d, tf), lambda b, f, be, r0, tlo, thi, nv: (be[b], 0, fidx(b, f, nv))),
                  pl.BlockSpec((1, d, tf), lambda b, f, be, r0, tlo, thi, nv: (be[b], 0, fidx(b, f, nv))),
                  pl.BlockSpec((1, tf, d), lambda b, f, be, r0, tlo, thi, nv: (be[b], fidx(b, f, nv), 0))],
        out_specs=pl.BlockSpec((tmb, d), lambda b, f, *_: (b, 0)),
        scratch_shapes=[pltpu.VMEM((2, tmb, d), BF16), pltpu.VMEM((tmb, d), F32)],
    )
    return pl.pallas_call(
        functools.partial(_moe_ffn_kernel, tt=tt, nsub=nf),
        grid_spec=grid_spec,
        out_shape=jax.ShapeDtypeStruct((nb * tmb, d), BF16),
        compiler_params=pltpu.CompilerParams(dimension_semantics=("arbitrary", "arbitrary"),
                                             vmem_limit_bytes=MOE_VMEM_LIMIT),
        name="moe_ffn",
    )(*sched, h, post, w_gate, w_up, w_down)


def _moe_combine_kernel(kb_ref, off_ref, lim_ref, x_ref, pos_ref, gate_ref, fg_ref, *rest):
    y_refs, o_ref, acc_sc = rest[:-2], rest[-2], rest[-1]
    t = pl.program_id(0)
    tm = x_ref.shape[0]
    yb = y_refs[0].shape[0]
    pos = pos_ref[...]
    gate = gate_ref[...]
    lane = lax.broadcasted_iota(jnp.int32, (tm, LANE), 1)
    col = lax.broadcasted_iota(jnp.int32, (1, yb), 1).astype(F32)
    def routed(e):
        pe = jnp.sum(jnp.where(lane == e, pos, 0.0), axis=1, keepdims=True)
        ge = jnp.sum(jnp.where(lane == e, gate, 0.0), axis=1, keepdims=True)
        r = jnp.where(pe >= 0.0, pe + off_ref[t * N_EXPERTS + e].astype(F32), -1.0)
        return r, ge

    def picked(e, k, r):
        sel = jnp.where(r == col + float(k * yb), 1.0, 0.0).astype(BF16)
        return _dot(sel, y_refs[MOE_Y_FETCH * e + k][...])

    acc = x_ref[...]
    for e in range(N_EXPERTS):
        r, ge = routed(e)
        acc = acc + ge * (picked(e, 0, r) + picked(e, 1, r))
    acc_sc[...] = acc
    for e in range(N_EXPERTS):
        for k in range(2, MOE_Y_FETCH):
            @pl.when(lim_ref[t * N_EXPERTS + e] > k * yb)
            def _(k=k, e=e):
                r, ge = routed(e)
                acc_sc[...] += ge * picked(e, k, r)
    o_ref[...] = _rms(acc_sc[...], fg_ref[...]).astype(o_ref.dtype)


def moe_combine(x, pos, gate, y, kb, off, lim, final_gain):
    s, d = x.shape
    tm = min(MOE_TOK_TILE, s)
    yb = MOE_Y_BLOCK
    last = y.shape[0] // yb - 1

    def yspec(e, k):
        def index(t, kb_r, off_r, lim_r):
            blk = jnp.minimum(kb_r[t * N_EXPERTS + e] + k, last)
            return (blk if k < 2 else jnp.where(lim_r[t * N_EXPERTS + e] > k * yb, blk, 0), 0)
        return pl.BlockSpec((yb, d), index)

    grid_spec = pltpu.PrefetchScalarGridSpec(
        num_scalar_prefetch=3,
        grid=(s // tm,),
        in_specs=[pl.BlockSpec((tm, d), lambda t, *_: (t, 0)),
                  pl.BlockSpec((tm, LANE), lambda t, *_: (t, 0)),
                  pl.BlockSpec((tm, LANE), lambda t, *_: (t, 0)),
                  pl.BlockSpec((1, d), lambda t, *_: (0, 0))]
        + [yspec(e, k) for e in range(N_EXPERTS) for k in range(MOE_Y_FETCH)],
        out_specs=pl.BlockSpec((tm, d), lambda t, *_: (t, 0)),
        scratch_shapes=[pltpu.VMEM((tm, d), F32)],
    )
    return pl.pallas_call(
        _moe_combine_kernel,
        grid_spec=grid_spec,
        out_shape=jax.ShapeDtypeStruct((s, d), F32),
        compiler_params=_params("arbitrary"),
        name="moe_combine",
    )(kb, off, lim, x, pos, gate, final_gain.reshape(1, d).astype(F32), *([y] * (MOE_Y_FETCH * N_EXPERTS)))


def _moe_schedule(before, total, s):
    tmb, yb, gr = MOE_ROW_BLOCK, MOE_Y_BLOCK, MOE_GATHER_ROWS
    nb = 2 * s // tmb + N_EXPERTS
    counts = total[0, :N_EXPERTS].astype(jnp.int32)
    nblk = (counts + tmb - 1) // tmb
    end = jnp.cumsum(nblk)
    first = end - nblk
    nvalid = end[-1]
    b = jnp.minimum(jnp.arange(nb, dtype=jnp.int32), nvalid - 1)
    blk_e = jnp.sum(b[:, None] >= end[None, :], axis=1).astype(jnp.int32)
    r0 = (b - first[blk_e]) * tmb
    cb = before[:, 0, :N_EXPERTS].astype(jnp.int32)
    r0s = (r0[:, None] + gr * jnp.arange(tmb // gr, dtype=jnp.int32)[None, :]).reshape(-1)
    cbe = cb[:, jnp.repeat(blk_e, tmb // gr)]
    tlo = (jnp.sum(cbe <= r0s[None, :], axis=0) - 1).astype(jnp.int32)
    thi = (jnp.sum(cbe < (r0s + gr)[None, :], axis=0) - 1).astype(jnp.int32)
    sched = (blk_e, r0.astype(jnp.int32), tlo, thi, nvalid.reshape(1).astype(jnp.int32))
    row_start = first[None, :] * tmb + cb
    kb = row_start // yb
    off = first[None, :] * tmb - kb * yb
    n_te = jnp.concatenate([cb[1:], counts[None, :]], axis=0) - cb
    lim = row_start - kb * yb + n_te
    flat = lambda a: a.reshape(-1).astype(jnp.int32)
    return sched, flat(kb), flat(off), flat(lim)


def moe_sparse(h, x, norm_gain, router, w_gate, w_up, w_down, final_gain):
    s, _ = x.shape
    pos, gate, post, before, total = moe_route(x, norm_gain, router)
    sched, kb, off, lim = _moe_schedule(before, total, s)
    y = moe_ffn(h, post, sched, w_gate, w_up, w_down)
    return moe_combine(x, pos, gate, y, kb, off, lim, final_gain)


FOX_F0 = 7 * BRANCH_W


def _split_w_in_kernel(w_ref, mix_ref, f_ref, gate_ref):
    w = w_ref[...]
    rows = w.shape[0]
    mix_ref[:, :FOX_F0] = w[:, :FOX_F0].astype(BF16)
    mix_ref[:, FOX_F0:] = w[:, FOX_F0 + N_HEADS:N_MIX_COLS + N_HEADS].astype(BF16)
    f_ref[...] = jnp.concatenate([w[:, FOX_F0:FOX_F0 + N_HEADS], jnp.zeros((rows, LANE - N_HEADS), F32)],
                                 axis=1).astype(BF16)
    gate_ref[...] = w[:, N_MIX_COLS + N_HEADS:].astype(BF16)


def _mixer_weights(w_in, layer, tr=128):
    _, d, cols = w_in.shape
    row = lambda w: pl.BlockSpec((tr, w), lambda i: (i, 0))
    return pl.pallas_call(
        _split_w_in_kernel,
        grid=(d // tr,),
        in_specs=[pl.BlockSpec((None, tr, cols), lambda i: (layer, i, 0))],
        out_specs=[row(N_MIX_COLS), row(LANE), row(N_BRANCH * D_MODEL)],
        out_shape=[jax.ShapeDtypeStruct((d, N_MIX_COLS), BF16), jax.ShapeDtypeStruct((d, LANE), BF16),
                   jax.ShapeDtypeStruct((d, N_BRANCH * D_MODEL), BF16)],
        compiler_params=_params("parallel"),
        name="split_w_in",
    )(w_in)


def kernel(x, w_in, w_branch, w_out, norm_mix_g, hgrn_lb_logits, hgrn_norm_g, fox_f_bias, pool_w, pool_scale,
           ret_gn_g, ret_gn_b, norm_ffn_g, ffn_w_gate, ffn_w_up, ffn_w_down, moe_router, moe_w_gate, moe_w_up,
           moe_w_down, final_norm_g):
    b, s, d = x.shape
    assert b == 1 and d == D_MODEL
    depth = w_in.shape[0]
    assert depth == 2, "layer 0 uses the dense FFN, layer 1 the experts and the final norm"
    xs = x.reshape(s, d)
    h = rmsnorm_bf16(xs, norm_mix_g[0])
    out = None
    for layer in range(depth):
        w_mix, w_f, w_gate = _mixer_weights(w_in, layer)
        proj = matmul(h, w_mix)
        ct, fox_stats = fox_gate(h, proj, w_f, fox_f_bias[layer])
        branches = (
            hgrn2(proj, hgrn_lb_logits, hgrn_norm_g[layer], layer),
            fox_attention(proj, ct, fox_stats),
            pool_mixer(proj, pool_w[layer], pool_scale[layer]),
            retention(proj, ret_gn_g[layer], ret_gn_b[layer]),
        )
        xs, h2 = merge(h, branches, xs, w_gate, w_branch[layer].astype(BF16), w_out[layer].astype(BF16),
                       norm_ffn_g[layer])
        if layer % 2 == 0:
            li = layer // 2
            xs, h = ffn_dense(h2, xs, ffn_w_gate[li], ffn_w_up[li], ffn_w_down[li], norm_mix_g[layer + 1])
        else:
            li = layer // 2
            out = moe_sparse(h2, xs, norm_ffn_g[layer], moe_router[li], moe_w_gate[li].astype(BF16),
                             moe_w_up[li].astype(BF16), moe_w_down[li].astype(BF16), final_norm_g)
    return out.reshape(b, s, d)
```

```python
import functools
import math

import jax
import jax.numpy as jnp
from jax import lax
from jax.experimental import pallas as pl
from jax.experimental.pallas import tpu as pltpu

D_MODEL = 1024
N_BRANCH = 4
BRANCH_W = D_MODEL // N_BRANCH
HEAD_DIM = 64
N_HEADS = BRANCH_W // HEAD_DIM
POOL_WINDOWS = (2, 4, 8, 16)
POOL_GROUP = BRANCH_W // len(POOL_WINDOWS)
POOL_HALO = 16
RET_DECAY_BASE = 5.0
ROPE_BASE = 10000.0
D_FF = 7 * D_MODEL // 2
N_EXPERTS = 8
RMS_EPS = 1e-6
LN_EPS = 1e-5
N_MIX_COLS = 12 * BRANCH_W

LANE = 128
SUBLANE = 8
VMEM_LIMIT = 56 * 1024 * 1024

HG_CHUNK = 64
HG_SUB = 16
HG_FAST_MIN_LOGDECAY = -60.0
RET_CHUNK = 256

F32 = jnp.float32
BF16 = jnp.bfloat16
NT_DIMS = (((1,), (1,)), ((), ()))


def _params(*sem):
    return pltpu.CompilerParams(dimension_semantics=sem, vmem_limit_bytes=VMEM_LIMIT)


def _const_spec(shape):
    nd = len(shape)
    return pl.BlockSpec(shape, lambda *_: (0,) * nd, pipeline_mode=pl.Buffered(1))


def _split3(x):
    hi = x.astype(BF16)
    r1 = x - hi.astype(F32)
    mid = r1.astype(BF16)
    lo = (r1 - mid.astype(F32)).astype(BF16)
    return hi, mid, lo


def _dot(a, b):
    return jnp.dot(a, b, preferred_element_type=F32)


def _dot_nt(a, b):
    return lax.dot_general(a, b, NT_DIMS, preferred_element_type=F32)


def _dot_exact_rhs(x, m_bf16):
    hi, mid, lo = _split3(x)
    return _dot(hi, m_bf16) + _dot(mid, m_bf16) + _dot(lo, m_bf16)


def _dot_exact_lhs(m_bf16, x):
    hi, mid, lo = _split3(x)
    return _dot(m_bf16, hi) + _dot(m_bf16, mid) + _dot(m_bf16, lo)


def _sigmoid(x):
    return 1.0 / (1.0 + jnp.exp(-x))


def _silu(x):
    return x * _sigmoid(x)


def _rms(x, gain):
    return x * lax.rsqrt(jnp.mean(x * x, axis=-1, keepdims=True) + RMS_EPS) * gain


def _head_of(n):
    return jnp.arange(n) // HEAD_DIM


def _head_ones():
    h = _head_of(BRANCH_W)
    return (h[:, None] == h[None, :]).astype(BF16)


def _head_masks():
    return (_head_of(BRANCH_W)[None, :] == jnp.arange(N_HEADS)[:, None]).astype(F32)


def _rmsnorm_kernel(x_ref, g_ref, o_ref):
    o_ref[...] = _rms(x_ref[...], g_ref[...]).astype(o_ref.dtype)


def rmsnorm_bf16(x, gain, tm=1024):
    s, d = x.shape
    tm = min(tm, s)
    return pl.pallas_call(
        _rmsnorm_kernel,
        grid=(s // tm,),
        in_specs=[pl.BlockSpec((tm, d), lambda i: (i, 0)), _const_spec((1, d))],
        out_specs=pl.BlockSpec((tm, d), lambda i: (i, 0)),
        out_shape=jax.ShapeDtypeStruct((s, d), BF16),
        compiler_params=_params("parallel"),
        name="rmsnorm",
    )(x, gain.reshape(1, d))


def _matmul_kernel(a_ref, b_ref, o_ref):
    o_ref[...] = _dot(a_ref[...], b_ref[...]).astype(o_ref.dtype)


def matmul(a, b, out_dtype=BF16, tm=2048, tn=1024):
    m, k = a.shape
    _, n = b.shape
    tm = min(tm, m)
    return pl.pallas_call(
        _matmul_kernel,
        grid=(n // tn, m // tm),
        in_specs=[pl.BlockSpec((tm, k), lambda j, i: (i, 0)),
                  pl.BlockSpec((k, tn), lambda j, i: (0, j))],
        out_specs=pl.BlockSpec((tm, tn), lambda j, i: (i, j)),
        out_shape=jax.ShapeDtypeStruct((m, n), out_dtype),
        compiler_params=_params("parallel", "parallel"),
        name="in_proj",
    )(a, b)


def _hgrn_kernel(q_ref, f_ref, i_ref, g_ref, lbl_ref, ng_ref, ones_ref, hm_ref, tril_ref, halfsum_ref, fmask_ref,
                 o_ref, st_ref, bpad, kpad, vpad, astack, lf_sc, kk_sc, *, layer, tile):
    c, sub = HG_CHUNK, HG_SUB
    nsub = c // sub
    half = c // 2

    @pl.when(pl.program_id(0) == 0)
    def _():
        st_ref[...] = jnp.zeros_like(st_ref)
        bpad[...] = jnp.zeros_like(bpad)
        kpad[...] = jnp.zeros_like(kpad)
        vpad[...] = jnp.zeros_like(vpad)

    lbl = lbl_ref[...]
    e = jnp.exp(lbl - jnp.max(lbl, axis=0, keepdims=True))
    p = e / jnp.sum(e, axis=0, keepdims=True)
    lb = jnp.zeros((1, BRANCH_W), F32)
    for l in range(1, layer + 1):
        lb = lb + p[l:l + 1, :]

    ones_bd = ones_ref[...]
    hm = hm_ref[...]
    tril = tril_ref[...]
    row = lax.broadcasted_iota(jnp.int32, (c, 1), 0)
    row_in_sub = row % sub
    bd_mask = ones_bd.astype(F32)

    sig = _sigmoid(f_ref[...].astype(F32))
    logf_all = jnp.log(lb + (1.0 - lb) * sig)
    lf_sc[...] = logf_all
    kk_sc[...] = (1.0 - lb) * (1.0 - sig)
    min_decay = jnp.min(_dot(halfsum_ref[...], logf_all.astype(BF16)))

    def load(ci):
        r0 = pl.multiple_of(ci * c, c)
        q = q_ref[pl.ds(r0, c), :].astype(F32)
        v = i_ref[pl.ds(r0, c), :].astype(F32)
        kk = kk_sc[pl.ds(r0, c), :]
        b = _dot_exact_lhs(tril, lf_sc[pl.ds(r0, c), :])
        return r0, q, v, kk, b

    def finish(r0, q_decayed, v, kk, b, intra):
        st = st_ref[...]
        inter = _dot_nt(q_decayed.astype(BF16), st.astype(BF16))
        b_last = b[c - 1:c, :]
        ks_end = (kk * jnp.exp(b_last - b)).astype(BF16)
        upd = _dot(v.T.astype(BF16), ks_end)
        st_ref[...] = st * jnp.exp(b_last) + upd * bd_mask
        o = intra + inter
        ms = _dot_exact_rhs(o * o, ones_bd) * (1.0 / HEAD_DIM)
        g = g_ref[pl.ds(r0, c), :].astype(F32)
        y = o * lax.rsqrt(ms + RMS_EPS) * ng_ref[...] * _silu(g)
        o_ref[pl.ds(r0, c), :] = y.astype(o_ref.dtype)

    def fast_chunk(ci, carry):
        r0, q, v, kk, b = load(ci)
        second = row >= half
        m_row = b[half - 1:half, :]
        mref = jnp.where(second, m_row, 0.0)
        qp = q * jnp.exp(b - mref)
        kp = kk * jnp.exp(mref - b)
        e_m = jnp.exp(m_row)
        kaug = jnp.concatenate([kp, kp[:half, :] * e_m], axis=0).astype(BF16)
        vaug = jnp.concatenate([v, v[:half, :]], axis=0).astype(BF16)
        qx = jnp.concatenate([qp * hm[h:h + 1, :] for h in range(N_HEADS)], axis=0).astype(BF16)
        sc = jnp.where(fmask_ref[...] > 0.0, _dot_nt(qx, kaug), 0.0)
        r = _dot(sc.astype(BF16), vaug)
        intra = jnp.zeros((c, BRANCH_W), F32)
        for h in range(N_HEADS):
            intra = intra + r[h * c:(h + 1) * c, :] * hm[h:h + 1, :]
        finish(r0, jnp.where(second, qp * e_m, qp), v, kk, b, intra)
        return carry

    def exact_chunk(ci, carry):
        r0, q, v, kk, b = load(ci)

        bpad[pl.ds(sub, c), :] = b
        kpad[pl.ds(sub, c), :] = kk
        vpad[pl.ds(sub, c), :] = v

        for d in range(sub):
            b_d = bpad[pl.ds(sub - d, c), :]
            k_d = kpad[pl.ds(sub - d, c), :]
            a = jnp.where(row_in_sub >= d, q * k_d * jnp.exp(b - b_d), 0.0)
            astack[pl.ds(d * c, c), :] = a.astype(BF16)
        pall = _dot(astack[...], ones_bd)
        intra = jnp.zeros((c, BRANCH_W), F32)
        for d in range(sub):
            intra = intra + pall[d * c:(d + 1) * c, :] * vpad[pl.ds(sub - d, c), :]

        pieces = [jnp.zeros((sub, BRANCH_W), F32)]
        for si in range(1, nsub):
            lo = si * sub
            m_i = b[lo - 1:lo, :]
            qs = q[lo:lo + sub, :] * jnp.exp(b[lo:lo + sub, :] - m_i)
            ks = (kk[:lo, :] * jnp.exp(m_i - b[:lo, :])).astype(BF16)
            qx = jnp.concatenate([qs * hm[h:h + 1, :] for h in range(N_HEADS)], axis=0).astype(BF16)
            sc = _dot_nt(qx, ks)
            r = _dot(sc.astype(BF16), v[:lo, :].astype(BF16))
            acc = jnp.zeros((sub, BRANCH_W), F32)
            for h in range(N_HEADS):
                acc = acc + r[h * sub:(h + 1) * sub, :] * hm[h:h + 1, :]
            pieces.append(acc)
        intra = intra + jnp.concatenate(pieces, axis=0)
        finish(r0, q * jnp.exp(b), v, kk, b, intra)
        return carry

    lax.cond(min_decay >= HG_FAST_MIN_LOGDECAY,
             lambda: lax.fori_loop(0, tile // c, fast_chunk, 0, unroll=2),
             lambda: lax.fori_loop(0, tile // c, exact_chunk, 0))


def hgrn2(proj, lb_logits, norm_g, layer, tile=512):
    s = proj.shape[0]
    depth = lb_logits.shape[0]
    c, sub = HG_CHUNK, HG_SUB
    half = c // 2
    tile = min(tile, s)
    col = lambda j: pl.BlockSpec((tile, BRANCH_W), lambda i, j=j: (i, j))
    tril = jnp.tril(jnp.ones((c, c), F32)).astype(BF16)
    nhalf = tile // half
    halfsum = (jnp.arange(tile)[None, :] // half == jnp.arange(nhalf)[:, None]).astype(BF16)
    t = jnp.arange(c)[:, None]
    col_s = jnp.arange(c + half)[None, :]
    same_half = (col_s < c) & (col_s // half == t // half) & (col_s <= t)
    cross = (col_s >= c) & (t >= half)
    fmask = jnp.tile((same_half | cross).astype(F32), (N_HEADS, 1))
    return pl.pallas_call(
        functools.partial(_hgrn_kernel, layer=layer, tile=tile),
        grid=(s // tile,),
        in_specs=[col(0), col(1), col(2), col(3),
                  _const_spec((depth, BRANCH_W)), _const_spec((1, BRANCH_W)),
                  _const_spec((BRANCH_W, BRANCH_W)), _const_spec((N_HEADS, BRANCH_W)),
                  _const_spec((c, c)), _const_spec((nhalf, tile)), _const_spec((N_HEADS * c, c + half))],
        out_specs=pl.BlockSpec((tile, BRANCH_W), lambda i: (i, 0)),
        out_shape=jax.ShapeDtypeStruct((s, BRANCH_W), BF16),
        scratch_shapes=[pltpu.VMEM((BRANCH_W, BRANCH_W), F32),
                        pltpu.VMEM((c + sub, BRANCH_W), F32),
                        pltpu.VMEM((c + sub, BRANCH_W), F32),
                        pltpu.VMEM((c + sub, BRANCH_W), F32),
                        pltpu.VMEM((sub * c, BRANCH_W), BF16),
                        pltpu.VMEM((tile, BRANCH_W), F32),
                        pltpu.VMEM((tile, BRANCH_W), F32)],
        compiler_params=_params("arbitrary"),
        name="hgrn2",
    )(proj, proj, proj, proj, lb_logits.astype(F32), norm_g.reshape(1, BRANCH_W).astype(F32),
      _head_ones(), _head_masks(), tril, halfsum, fmask)


FOX_TILE = 256
FOX_NSTAT = 16
FOX_SKIP_LOG = 40.0


def _fox_gate_kernel(h_ref, q_ref, k_ref, v_ref, wf_ref, bias_ref, tril_ref, ones_ref,
                     c_ref, vt_ref, stat_ref, carry_ref, kmax_ref):
    @pl.when(pl.program_id(0) == 0)
    def _():
        carry_ref[...] = jnp.zeros_like(carry_ref)
        kmax_ref[...] = jnp.zeros_like(kmax_ref)

    logit = _dot(h_ref[...], wf_ref[...]) + bias_ref[...]
    logf = jnp.minimum(logit, 0.0) - jnp.log(1.0 + jnp.exp(-jnp.abs(logit)))
    cum = _dot_exact_lhs(tril_ref[...], logf) + carry_ref[...]
    carry_ref[...] = cum[-1:, :]
    c_ref[...] = cum
    vt_ref[...] = v_ref[...].astype(F32).T.astype(vt_ref.dtype)

    ones_bd = ones_ref[...]
    q = q_ref[...].astype(F32)
    k = k_ref[...].astype(F32)
    scale = HEAD_DIM ** -0.5
    head_lane = lax.broadcasted_iota(jnp.int32, (1, BRANCH_W), 1) // HEAD_DIM
    c_heads = jnp.zeros(q.shape, F32)
    for h in range(N_HEADS):
        c_heads = jnp.where(head_lane == h, cum[:, h:h + 1], c_heads)
    slack = 1.0 + 2.0 ** -6
    qn = jnp.sqrt(_dot((q * q).astype(BF16), ones_bd)) * (scale * slack)
    kn = jnp.sqrt(_dot((k * k).astype(BF16), ones_bd)) * slack
    diag = _dot((q * k).astype(BF16), ones_bd) * scale - (2.0 ** -6) * qn * kn
    kmax = jnp.maximum(kmax_ref[...], jnp.max(kn, axis=0, keepdims=True))
    kmax_ref[...] = kmax
    rows = [jnp.max(qn, axis=0, keepdims=True),
            jnp.max(c_heads - diag, axis=0, keepdims=True),
            kmax,
            c_heads[-1:, :]]
    stat_ref[0] = jnp.concatenate(rows + [jnp.zeros((SUBLANE - len(rows), BRANCH_W), F32)], axis=0)


def fox_gate(h, proj, w_f, f_bias):
    s, d = h.shape
    tile = min(FOX_TILE, s)
    bias = jnp.zeros((1, LANE), F32).at[0, :N_HEADS].set(f_bias.astype(F32))
    tril = jnp.tril(jnp.ones((tile, tile), F32)).astype(BF16)
    col = lambda j: pl.BlockSpec((tile, BRANCH_W), lambda i, j=j: (i, j))
    c_rows, v_t, stats = pl.pallas_call(
        _fox_gate_kernel,
        grid=(s // tile,),
        in_specs=[pl.BlockSpec((tile, d), lambda i: (i, 0)), col(4), col(5), col(6),
                  _const_spec((d, LANE)), _const_spec((1, LANE)), _const_spec((tile, tile)),
                  _const_spec((BRANCH_W, BRANCH_W))],
        out_specs=[pl.BlockSpec((tile, LANE), lambda i: (i, 0)),
                   pl.BlockSpec((BRANCH_W, tile), lambda i: (0, i)),
                   pl.BlockSpec((1, SUBLANE, BRANCH_W), lambda i: (i, 0, 0))],
        out_shape=[jax.ShapeDtypeStruct((s, LANE), F32),
                   jax.ShapeDtypeStruct((BRANCH_W, s), BF16),
                   jax.ShapeDtypeStruct((s // tile, SUBLANE, BRANCH_W), F32)],
        scratch_shapes=[pltpu.VMEM((1, LANE), F32), pltpu.VMEM((1, BRANCH_W), F32)],
        compiler_params=_params("arbitrary"),
        name="fox_gate",
    )(h, proj, proj, proj, w_f, bias, tril, _head_ones())
    return c_rows, v_t, stats[:, :4, ::HEAD_DIM].reshape(-1)


def _fox_kernel(stat_ref, q_ref, k_ref, vt_ref, c_ref, hm_ref, o_ref, *, tq):
    i = pl.program_id(0)
    q0 = pl.multiple_of(i * tq, tq)
    q_t = (q_ref[...].astype(F32) * (HEAD_DIM ** -0.5)).T
    chan_head = lax.broadcasted_iota(jnp.int32, (BRANCH_W, 1), 0) // HEAD_DIM
    qh = [jnp.where(chan_head == h, q_t, 0.0).astype(BF16) for h in range(N_HEADS)]
    c_q0 = c_ref[pl.ds(q0, SUBLANE), :][0:1, :]

    first = i
    for h in range(N_HEADS):
        qmax = stat_ref[i * FOX_NSTAT + h]
        emax = stat_ref[i * FOX_NSTAT + N_HEADS + h]

        def needed(j, h=h, qmax=qmax, emax=emax):
            jc = jnp.maximum(j, 0)
            bound = (qmax * stat_ref[jc * FOX_NSTAT + 2 * N_HEADS + h] + emax
                     - stat_ref[jc * FOX_NSTAT + 3 * N_HEADS + h])
            return (j >= 0) & (bound >= -FOX_SKIP_LOG)

        last_dropped = lax.while_loop(needed, lambda j: j - 1, i - 1)
        first = jnp.minimum(first, last_dropped + 1)

    def block(s0, diagonal, state):
        kb = k_ref[pl.ds(s0, tq), :]
        bias = c_q0 - c_ref[pl.ds(s0, tq), :]
        scores = [_dot(kb, qh[h]) for h in range(N_HEADS)]
        new_state = []
        for h, (m_prev, l_prev, acc) in enumerate(state):
            sc = scores[h] + bias[:, h:h + 1]
            if diagonal:
                key = lax.broadcasted_iota(jnp.int32, (tq, tq), 0)
                qry = lax.broadcasted_iota(jnp.int32, (tq, tq), 1)
                sc = jnp.where(key <= qry, sc, -jnp.inf)
            m_new = jnp.maximum(m_prev, jnp.max(sc, axis=0, keepdims=True))
            alpha = jnp.exp(m_prev - m_new)
            p = jnp.exp(sc - m_new)
            l_new = alpha * l_prev + jnp.sum(p, axis=0, keepdims=True)
            v_h = vt_ref[pl.ds(h * HEAD_DIM, HEAD_DIM), pl.ds(s0, tq)]
            new_state.append((m_new, l_new, acc * alpha + _dot(v_h, p.astype(BF16))))
        return tuple(new_state)

    init = tuple((jnp.full((1, tq), -jnp.inf, F32), jnp.zeros((1, tq), F32), jnp.zeros((HEAD_DIM, tq), F32))
                 for _ in range(N_HEADS))
    state = lax.fori_loop(first, i, lambda j, st: block(pl.multiple_of(j * tq, tq), False, st), init)
    state = block(q0, True, state)

    out_t = jnp.concatenate([acc / l for _, l, acc in state], axis=0)
    o_ref[...] = out_t.T.astype(o_ref.dtype)


def fox_attention(proj, c_rows, v_t, stats):
    s = proj.shape[0]
    tq = min(FOX_TILE, s)
    return pl.pallas_call(
        functools.partial(_fox_kernel, tq=tq),
        grid=(s // tq,),
        in_specs=[pl.BlockSpec(memory_space=pltpu.SMEM),
                  pl.BlockSpec((tq, BRANCH_W), lambda i: (i, 4)),
                  pl.BlockSpec((s, BRANCH_W), lambda i: (0, 5), pipeline_mode=pl.Buffered(1)),
                  _const_spec((BRANCH_W, s)), _const_spec((s, LANE)), _const_spec((N_HEADS, BRANCH_W))],
        out_specs=pl.BlockSpec((tq, BRANCH_W), lambda i: (i, 0)),
        out_shape=jax.ShapeDtypeStruct((s, BRANCH_W), BF16),
        compiler_params=_params("parallel"),
        name="fox_attention",
    )(stats, proj, proj, v_t, c_rows, _head_masks())


def _pool_kernel(u_ref, w_ref, scale_ref, o_ref, ext, *, tile):
    i = pl.program_id(0)

    @pl.when(i == 0)
    def _():
        ext[pl.ds(0, POOL_HALO), :] = jnp.zeros((POOL_HALO, BRANCH_W), F32)

    u = u_ref[...].astype(F32)
    ext[pl.ds(POOL_HALO, tile), :] = u
    pos = (i * tile + lax.broadcasted_iota(jnp.int32, (tile, 1), 0) + 1).astype(F32)
    halves = []
    for half in range(BRANCH_W // LANE):
        lanes = pl.ds(half * LANE, LANE)
        w_small, w_big = POOL_WINDOWS[2 * half], POOL_WINDOWS[2 * half + 1]
        run = u[:, half * LANE:(half + 1) * LANE]
        sums = {}
        for j in range(1, w_big):
            if j == w_small:
                sums[w_small] = run
            run = run + ext[pl.ds(POOL_HALO - j, tile), lanes]
        sums[w_big] = run
        lane = lax.broadcasted_iota(jnp.int32, (1, LANE), 1)
        small = lane < POOL_GROUP
        total = jnp.where(small, sums[w_small], sums[w_big])
        count = jnp.where(small, jnp.minimum(pos, float(w_small)), jnp.minimum(pos, float(w_big)))
        halves.append(total / count)
    mean = jnp.concatenate(halves, axis=1)
    d = (mean - u).astype(BF16)
    y = _dot(d, w_ref[...]) * scale_ref[...]
    o_ref[...] = y.astype(o_ref.dtype)
    ext[pl.ds(0, POOL_HALO), :] = u[tile - POOL_HALO:, :]


def pool_mixer(proj, w_pool, scale, tile=512):
    s = proj.shape[0]
    tile = min(tile, s)
    ng = len(POOL_WINDOWS)
    w_bd = jnp.zeros((BRANCH_W, BRANCH_W), F32)
    for gi in range(ng):
        lo = gi * POOL_GROUP
        w_bd = w_bd.at[lo:lo + POOL_GROUP, lo:lo + POOL_GROUP].set(w_pool[gi].astype(F32))
    return pl.pallas_call(
        functools.partial(_pool_kernel, tile=tile),
        grid=(s // tile,),
        in_specs=[pl.BlockSpec((tile, BRANCH_W), lambda i: (i, 7)),
                  _const_spec((BRANCH_W, BRANCH_W)), _const_spec((1, BRANCH_W))],
        out_specs=pl.BlockSpec((tile, BRANCH_W), lambda i: (i, 0)),
        out_shape=jax.ShapeDtypeStruct((s, BRANCH_W), BF16),
        scratch_shapes=[pltpu.VMEM((tile + POOL_HALO, BRANCH_W), F32)],
        compiler_params=_params("arbitrary"),
        name="pool_mixer",
    )(proj, w_bd.astype(BF16), scale.reshape(1, BRANCH_W).astype(F32))


def _ret_kernel(q_ref, k_ref, v_ref, g_ref, rope_ref, ecos_ref, esin_ref, perm_ref, ones_ref, hm_ref,
                dstack_ref, xi_ref, zeta_ref, gc_ref, gng_ref, gnb_ref, o_ref, st_ref, *, tile):
    c = RET_CHUNK

    @pl.when(pl.program_id(0) == 0)
    def _():
        st_ref[...] = jnp.zeros_like(st_ref)

    perm = perm_ref[...]
    ones_bd = ones_ref[...]
    bd_mask = ones_bd.astype(F32)
    hm = hm_ref[...]

    def chunk(ci, carry):
        r0 = pl.multiple_of(ci * c, c)
        qb = q_ref[pl.ds(r0, c), :]
        kb = k_ref[pl.ds(r0, c), :]
        v = v_ref[pl.ds(r0, c), :]
        g = g_ref[pl.ds(r0, c), :].astype(F32)
        tab = rope_ref[pl.ds(r0, c), :]
        cos = _dot_exact_rhs(tab, ecos_ref[...])
        sin = _dot_exact_rhs(tab, esin_ref[...])
        qr = qb.astype(F32) * cos + _dot(qb, perm) * sin
        kr = (kb.astype(F32) * cos + _dot(kb, perm) * sin) * (HEAD_DIM ** -0.5)

        qx = jnp.concatenate([qr * hm[h:h + 1, :] for h in range(N_HEADS)], axis=0).astype(BF16)
        sc = _dot_nt(qx, kr.astype(BF16)) * dstack_ref[...]
        r = _dot(sc.astype(BF16), v)
        intra = jnp.zeros((c, BRANCH_W), F32)
        for h in range(N_HEADS):
            intra = intra + r[h * c:(h + 1) * c, :] * hm[h:h + 1, :]

        st = st_ref[...]
        inter = _dot_nt((qr * xi_ref[...]).astype(BF16), st.astype(BF16))
        upd = _dot(v.astype(F32).T.astype(BF16), (kr * zeta_ref[...]).astype(BF16))
        st_ref[...] = st * gc_ref[...] + upd * bd_mask

        o = intra + inter
        mu = _dot_exact_rhs(o, ones_bd) * (1.0 / HEAD_DIM)
        cen = o - mu
        var = _dot_exact_rhs(cen * cen, ones_bd) * (1.0 / HEAD_DIM)
        y = cen * lax.rsqrt(var + LN_EPS) * gng_ref[...] + gnb_ref[...]
        o_ref[pl.ds(r0, c), :] = (y * _silu(g)).astype(o_ref.dtype)
        return carry

    lax.fori_loop(0, tile // c, chunk, 0, unroll=True)


def _rope_tables(s):
    half = HEAD_DIM // 2
    pos = jnp.arange(s, dtype=F32)
    inv_freq = ROPE_BASE ** (-jnp.arange(half, dtype=F32) / half)
    ang = pos[:, None] * inv_freq[None, :]
    table = jnp.concatenate([jnp.cos(ang), jnp.sin(ang), jnp.zeros((s, LANE - 2 * half), F32)], axis=1)
    lane = jnp.arange(BRANCH_W)
    src = jnp.arange(LANE)[:, None]
    ecos = (src == (lane % half)[None, :]).astype(F32)
    sign = jnp.where(lane % HEAD_DIM < half, -1.0, 1.0)
    esin = (src == (half + lane % half)[None, :]).astype(F32) * sign[None, :]
    return table, ecos.astype(BF16), esin.astype(BF16)


def _ret_constants():
    c = RET_CHUNK
    half = HEAD_DIM // 2
    lane = jnp.arange(BRANCH_W)
    partner = jnp.where(lane % HEAD_DIM < half, lane + half, lane - half)
    perm = (lane[:, None] == partner[None, :]).astype(BF16)
    log_gamma = jnp.log1p(-jnp.exp2(-RET_DECAY_BASE - jnp.arange(N_HEADS, dtype=F32)))
    ci = jnp.arange(c, dtype=F32)
    diff = ci[:, None] - ci[None, :]
    intra = jnp.where(diff >= 0, jnp.exp(diff * log_gamma[:, None, None]), 0.0)
    dstack = intra.reshape(N_HEADS * c, c)
    lg_lane = jnp.repeat(log_gamma, HEAD_DIM)[None, :]
    xi = jnp.exp((ci[:, None] + 1.0) * lg_lane)
    zeta = jnp.exp((c - 1.0 - ci[:, None]) * lg_lane)
    gc = jnp.exp(c * lg_lane)
    return perm, dstack, xi, zeta, gc


def retention(proj, gn_g, gn_b, tile=512):
    s = proj.shape[0]
    c = RET_CHUNK
    tile = min(tile, s)
    rope, ecos, esin = _rope_tables(s)
    perm, dstack, xi, zeta, gc = _ret_constants()
    col = lambda j: pl.BlockSpec((tile, BRANCH_W), lambda i, j=j: (i, j))
    row = pl.BlockSpec((tile, BRANCH_W), lambda i: (i, 0))
    return pl.pallas_call(
        functools.partial(_ret_kernel, tile=tile),
        grid=(s // tile,),
        in_specs=[col(8), col(9), col(10), col(11), pl.BlockSpec((tile, LANE), lambda i: (i, 0)),
                  _const_spec((LANE, BRANCH_W)), _const_spec((LANE, BRANCH_W)),
                  _const_spec((BRANCH_W, BRANCH_W)), _const_spec((BRANCH_W, BRANCH_W)),
                  _const_spec((N_HEADS, BRANCH_W)), _const_spec((N_HEADS * c, c)),
                  _const_spec((c, BRANCH_W)), _const_spec((c, BRANCH_W)), _const_spec((1, BRANCH_W)),
                  _const_spec((1, BRANCH_W)), _const_spec((1, BRANCH_W))],
        out_specs=row,
        out_shape=jax.ShapeDtypeStruct((s, BRANCH_W), BF16),
        scratch_shapes=[pltpu.VMEM((BRANCH_W, BRANCH_W), F32)],
        compiler_params=_params("arbitrary"),
        name="retention",
    )(proj, proj, proj, proj, rope, ecos, esin, perm, _head_ones(), _head_masks(), dstack, xi, zeta, gc,
      gn_g.reshape(1, BRANCH_W).astype(F32), gn_b.reshape(1, BRANCH_W).astype(F32))


def _merge_kernel(h_ref, o0_ref, o1_ref, o2_ref, o3_ref, x_ref, wg_ref, wb_ref, wo_ref, g_ref,
                  xo_ref, ho_ref):
    h = h_ref[...]
    merged = jnp.zeros(x_ref.shape, F32)
    for bi, o_ref in enumerate((o0_ref, o1_ref, o2_ref, o3_ref)):
        gate = _sigmoid(_dot(h, wg_ref[:, bi * D_MODEL:(bi + 1) * D_MODEL]))
        merged = merged + gate * _dot(o_ref[...], wb_ref[bi])
    x_new = x_ref[...] + _dot(merged.astype(BF16), wo_ref[...])
    xo_ref[...] = x_new
    ho_ref[...] = _rms(x_new, g_ref[...]).astype(ho_ref.dtype)


def merge(h, branches, x, w_gate, w_branch, w_out, next_gain, tm=512):
    s, d = x.shape
    tm = min(tm, s)
    row = lambda w: pl.BlockSpec((tm, w), lambda i: (i, 0))
    return pl.pallas_call(
        _merge_kernel,
        grid=(s // tm,),
        in_specs=[row(d), row(BRANCH_W), row(BRANCH_W), row(BRANCH_W), row(BRANCH_W), row(d),
                  _const_spec((d, N_BRANCH * d)), _const_spec((N_BRANCH, BRANCH_W, d)),
                  _const_spec((d, d)), _const_spec((1, d))],
        out_specs=[row(d), row(d)],
        out_shape=[jax.ShapeDtypeStruct((s, d), F32), jax.ShapeDtypeStruct((s, d), BF16)],
        compiler_params=_params("parallel"),
        name="merge",
    )(h, *branches, x, w_gate, w_branch, w_out, next_gain.reshape(1, d).astype(F32))


def _ffn_kernel(h_ref, x_ref, wg_ref, wu_ref, wd_ref, g_ref, xo_ref, ho_ref, acc_ref):
    f = pl.program_id(1)

    @pl.when(f == 0)
    def _():
        acc_ref[...] = jnp.zeros_like(acc_ref)

    h = h_ref[...]
    a = _silu(_dot(h, wg_ref[...].astype(BF16))) * _dot(h, wu_ref[...].astype(BF16))
    acc_ref[...] += _dot(a.astype(BF16), wd_ref[...].astype(BF16))

    @pl.when(f == pl.num_programs(1) - 1)
    def _():
        x_new = x_ref[...] + acc_ref[...]
        xo_ref[...] = x_new
        ho_ref[...] = _rms(x_new, g_ref[...]).astype(ho_ref.dtype)


def ffn_dense(h, x, w_gate, w_up, w_down, next_gain, tm=1024, tf=512):
    s, d = x.shape
    tm = min(tm, s)
    dff = w_gate.shape[1]
    row = lambda: pl.BlockSpec((tm, d), lambda i, f: (i, 0))
    return pl.pallas_call(
        _ffn_kernel,
        grid=(s // tm, dff // tf),
        in_specs=[row(), row(),
                  pl.BlockSpec((d, tf), lambda i, f: (0, f)),
                  pl.BlockSpec((d, tf), lambda i, f: (0, f)),
                  pl.BlockSpec((tf, d), lambda i, f: (f, 0)),
                  _const_spec((1, d))],
        out_specs=[row(), row()],
        out_shape=[jax.ShapeDtypeStruct((s, d), F32), jax.ShapeDtypeStruct((s, d), BF16)],
        scratch_shapes=[pltpu.VMEM((tm, d), F32)],
        compiler_params=_params("parallel", "arbitrary"),
        name="ffn_dense",
    )(h, x, w_gate, w_up, w_down, next_gain.reshape(1, d).astype(F32))


MOE_TOK_TILE = 256
MOE_ROW_BLOCK = 512
MOE_GATHER_ROWS = 128
MOE_GATHER_TILES = 4
MOE_VMEM_LIMIT = 60 * 1024 * 1024
MOE_Y_BLOCK = 128
MOE_Y_FETCH = MOE_TOK_TILE // MOE_Y_BLOCK + 1


def _dot_f32(x, w):
    xh, xm, _ = _split3(x)
    wh, wm, _ = _split3(w)
    return _dot(xh, wh) + (_dot(xh, wm) + _dot(xm, wh))


def _route_kernel(x_ref, ng_ref, router_ref, ltri_ref, pos_ref, gate_ref, post_ref, before_ref, total_ref,
                  carry_ref):
    tm = x_ref.shape[0]

    @pl.when(pl.program_id(0) == 0)
    def _():
        carry_ref[...] = jnp.zeros_like(carry_ref)

    hn = _rms(x_ref[...], ng_ref[...])
    logits = _dot_f32(hn, router_ref[...])
    lane = lax.broadcasted_iota(jnp.int32, (tm, LANE), 1)
    logits = jnp.where(lane < N_EXPERTS, logits, -jnp.inf)
    v1 = jnp.max(logits, axis=1, keepdims=True)
    i1 = jnp.min(jnp.where(logits == v1, lane, LANE), axis=1, keepdims=True)
    rest = jnp.where(lane == i1, -jnp.inf, logits)
    v2 = jnp.max(rest, axis=1, keepdims=True)
    i2 = jnp.min(jnp.where(rest == v2, lane, LANE), axis=1, keepdims=True)
    w1 = 1.0 / (1.0 + jnp.exp(v2 - v1))
    gate_ref[...] = jnp.where(lane == i1, w1, 0.0) + jnp.where(lane == i2, 1.0 - w1, 0.0)

    member = jnp.where((lane == i1) | (lane == i2), 1.0, 0.0)
    carry = carry_ref[...]
    rank = _dot(ltri_ref[...], member.astype(BF16)) + carry
    pos = jnp.where(member > 0.0, rank, -1.0)
    pos_ref[...] = pos
    post_ref[...] = pos.T[:SUBLANE, :]
    before_ref[0] = carry
    carry = carry + jnp.sum(member, axis=0, keepdims=True)
    carry_ref[...] = carry
    total_ref[...] = carry


def moe_route(x, norm_gain, router):
    s, d = x.shape
    tm = min(MOE_TOK_TILE, s)
    nt = s // tm
    router_p = jnp.zeros((d, LANE), F32).at[:, :N_EXPERTS].set(router.astype(F32))
    ltri = jnp.tril(jnp.ones((tm, tm), F32), -1).astype(BF16)
    row = pl.BlockSpec((tm, LANE), lambda i: (i, 0))
    return pl.pallas_call(
        _route_kernel,
        grid=(nt,),
        in_specs=[pl.BlockSpec((tm, d), lambda i: (i, 0)), _const_spec((1, d)), _const_spec((d, LANE)),
                  _const_spec((tm, tm))],
        out_specs=[row, row, pl.BlockSpec((SUBLANE, tm), lambda i: (0, i)),
                   pl.BlockSpec((1, 1, LANE), lambda i: (i, 0, 0)), pl.BlockSpec((1, LANE), lambda i: (0, 0))],
        out_shape=[jax.ShapeDtypeStruct((s, LANE), F32), jax.ShapeDtypeStruct((s, LANE), F32),
                   jax.ShapeDtypeStruct((SUBLANE, s), F32), jax.ShapeDtypeStruct((nt, 1, LANE), F32),
                   jax.ShapeDtypeStruct((1, LANE), F32)],
        scratch_shapes=[pltpu.VMEM((1, LANE), F32)],
        compiler_params=_params("arbitrary"),
        name="moe_route",
    )(x, norm_gain.reshape(1, d).astype(F32), router_p, ltri)


def _moe_ffn_kernel(be_ref, r0_ref, tlo_ref, thi_ref, nv_ref, h_ref, post_ref, wg_ref, wu_ref, wd_ref,
                    y_ref, x_sc, acc_sc, *, tt, nsub):
    b = pl.program_id(0)
    f = pl.program_id(1)
    nb = pl.num_programs(0)
    valid = b < nv_ref[0]
    tmb, d = acc_sc.shape
    gr = tmb // nsub
    last_tile = h_ref.shape[0] // tt - 1

    def picked(e, want, t):
        t0 = pl.multiple_of(t * tt, tt)
        p = post_ref[pl.ds(e, 1), pl.ds(t0, tt)]
        sel = jnp.where(p == want, 1.0, 0.0).astype(BF16)
        return _dot(sel, h_ref[pl.ds(t0, tt), :])

    def gather_head(blk, sb):
        e = be_ref[blk]
        lo = tlo_ref[blk * nsub + sb]
        hi = thi_ref[blk * nsub + sb]
        want = (lax.broadcasted_iota(jnp.int32, (gr, 1), 0) + (r0_ref[blk] + sb * gr)).astype(F32)
        rows = picked(e, want, lo)
        for k in range(1, MOE_GATHER_TILES):
            rows = rows + picked(e, jnp.where(lo + k <= hi, want, -2.0), jnp.minimum(lo + k, last_tile))
        return rows.astype(BF16)

    def gather_tail(slot, blk, sb):
        e = be_ref[blk]
        want = (lax.broadcasted_iota(jnp.int32, (gr, 1), 0) + (r0_ref[blk] + sb * gr)).astype(F32)
        rows = pl.ds(pl.multiple_of(sb * gr, gr), gr)

        def more(t, carry):
            x_sc[slot, rows, :] = (x_sc[slot, rows, :].astype(F32) + picked(e, want, t)).astype(BF16)
            return carry

        lax.fori_loop(tlo_ref[blk * nsub + sb] + MOE_GATHER_TILES, thi_ref[blk * nsub + sb] + 1, more, 0)

    @pl.when((b == 0) & (f == 0))
    def _():
        for sb in range(nsub):
            x_sc[0, pl.ds(sb * gr, gr), :] = gather_head(0, sb)
            gather_tail(0, 0, sb)

    nxt = jnp.minimum(b + 1, nb - 1)
    nslot = (b + 1) % 2

    @pl.when(valid)
    def _():
        x_sc[nslot, pl.ds(pl.multiple_of(f * gr, gr), gr), :] = gather_head(nxt, f)
        xb = x_sc[b % 2]
        a = _silu(_dot(xb, wg_ref[0])) * _dot(xb, wu_ref[0])
        acc_sc[...] = jnp.where(f == 0, 0.0, acc_sc[...]) + _dot(a.astype(BF16), wd_ref[0])

    @pl.when(valid & (thi_ref[nxt * nsub + f] - tlo_ref[nxt * nsub + f] >= MOE_GATHER_TILES))
    def _():
        gather_tail(nslot, nxt, f)

    @pl.when(f == nsub - 1)
    def _():
        y_ref[...] = jnp.where(valid, acc_sc[...], 0.0).astype(y_ref.dtype)


def moe_ffn(h, post, sched, w_gate, w_up, w_down):
    s, d = h.shape
    ne, _, dff = w_gate.shape
    tmb = MOE_ROW_BLOCK
    tt = min(MOE_TOK_TILE, s)
    nb = sched[0].shape[0]
    nf = tmb // MOE_GATHER_ROWS
    tf = dff // nf

    def fidx(b, f, nv):
        return jnp.where(b < nv[0], f, nf - 1)

    grid_spec = pltpu.PrefetchScalarGridSpec(
        num_scalar_prefetch=5,
        grid=(nb, nf),
        in_specs=[pl.BlockSpec((s, d), lambda b, f, *_: (0, 0), pipeline_mode=pl.Buffered(1)),
                  pl.BlockSpec((SUBLANE, s), lambda b, f, *_: (0, 0), pipeline_mode=pl.Buffered(1)),
                  pl.BlockSpec((1, d, tf), lambda b, f, be, r0, tlo, thi, nv: (be[b], 0, fidx(b, f, nv))),
                  pl.BlockSpec((1, d, tf), lambda b, f, be, r0, tlo, thi, nv: (be[b], 0, fidx(b, f, nv))),
                  pl.BlockSpec((1, tf, d), lambda b, f, be, r0, tlo, thi, nv: (be[b], fidx(b, f, nv), 0))],
        out_specs=pl.BlockSpec((tmb, d), lambda b, f, *_: (b, 0)),
        scratch_shapes=[pltpu.VMEM((2, tmb, d), BF16), pltpu.VMEM((tmb, d), F32)],
    )
    return pl.pallas_call(
        functools.partial(_moe_ffn_kernel, tt=tt, nsub=nf),
        grid_spec=grid_spec,
        out_shape=jax.ShapeDtypeStruct((nb * tmb, d), BF16),
        compiler_params=pltpu.CompilerParams(dimension_semantics=("arbitrary", "arbitrary"),
                                             vmem_limit_bytes=MOE_VMEM_LIMIT),
        name="moe_ffn",
    )(*sched, h, post, w_gate, w_up, w_down)


def _moe_combine_kernel(kb_ref, off_ref, lim_ref, x_ref, pos_ref, gate_ref, fg_ref, *rest):
    y_refs, o_ref, acc_sc = rest[:-2], rest[-2], rest[-1]
    t = pl.program_id(0)
    tm = x_ref.shape[0]
    yb = y_refs[0].shape[0]
    pos = pos_ref[...]
    gate = gate_ref[...]
    lane = lax.broadcasted_iota(jnp.int32, (tm, LANE), 1)
    col = lax.broadcasted_iota(jnp.int32, (1, yb), 1).astype(F32)
    def routed(e):
        pe = jnp.sum(jnp.where(lane == e, pos, 0.0), axis=1, keepdims=True)
        ge = jnp.sum(jnp.where(lane == e, gate, 0.0), axis=1, keepdims=True)
        r = jnp.where(pe >= 0.0, pe + off_ref[t * N_EXPERTS + e].astype(F32), -1.0)
        return r, ge

    def picked(e, k, r):
        sel = jnp.where(r == col + float(k * yb), 1.0, 0.0).astype(BF16)
        return _dot(sel, y_refs[MOE_Y_FETCH * e + k][...])

    acc = x_ref[...]
    for e in range(N_EXPERTS):
        r, ge = routed(e)
        acc = acc + ge * (picked(e, 0, r) + picked(e, 1, r))
    acc_sc[...] = acc
    for e in range(N_EXPERTS):
        for k in range(2, MOE_Y_FETCH):
            @pl.when(lim_ref[t * N_EXPERTS + e] > k * yb)
            def _(k=k, e=e):
                r, ge = routed(e)
                acc_sc[...] += ge * picked(e, k, r)
    o_ref[...] = _rms(acc_sc[...], fg_ref[...]).astype(o_ref.dtype)


def moe_combine(x, pos, gate, y, kb, off, lim, final_gain):
    s, d = x.shape
    tm = min(MOE_TOK_TILE, s)
    yb = MOE_Y_BLOCK
    last = y.shape[0] // yb - 1

    def yspec(e, k):
        def index(t, kb_r, off_r, lim_r):
            blk = jnp.minimum(kb_r[t * N_EXPERTS + e] + k, last)
            return (blk if k < 2 else jnp.where(lim_r[t * N_EXPERTS + e] > k * yb, blk, 0), 0)
        return pl.BlockSpec((yb, d), index)

    grid_spec = pltpu.PrefetchScalarGridSpec(
        num_scalar_prefetch=3,
        grid=(s // tm,),
        in_specs=[pl.BlockSpec((tm, d), lambda t, *_: (t, 0)),
                  pl.BlockSpec((tm, LANE), lambda t, *_: (t, 0)),
                  pl.BlockSpec((tm, LANE), lambda t, *_: (t, 0)),
                  pl.BlockSpec((1, d), lambda t, *_: (0, 0))]
        + [yspec(e, k) for e in range(N_EXPERTS) for k in range(MOE_Y_FETCH)],
        out_specs=pl.BlockSpec((tm, d), lambda t, *_: (t, 0)),
        scratch_shapes=[pltpu.VMEM((tm, d), F32)],
    )
    return pl.pallas_call(
        _moe_combine_kernel,
        grid_spec=grid_spec,
        out_shape=jax.ShapeDtypeStruct((s, d), F32),
        compiler_params=_params("arbitrary"),
        name="moe_combine",
    )(kb, off, lim, x, pos, gate, final_gain.reshape(1, d).astype(F32), *([y] * (MOE_Y_FETCH * N_EXPERTS)))


def _moe_schedule(before, total, s):
    tmb, yb, gr = MOE_ROW_BLOCK, MOE_Y_BLOCK, MOE_GATHER_ROWS
    nb = 2 * s // tmb + N_EXPERTS
    counts = total[0, :N_EXPERTS].astype(jnp.int32)
    nblk = (counts + tmb - 1) // tmb
    end = jnp.cumsum(nblk)
    first = end - nblk
    nvalid = end[-1]
    b = jnp.minimum(jnp.arange(nb, dtype=jnp.int32), nvalid - 1)
    blk_e = jnp.sum(b[:, None] >= end[None, :], axis=1).astype(jnp.int32)
    r0 = (b - first[blk_e]) * tmb
    cb = before[:, 0, :N_EXPERTS].astype(jnp.int32)
    r0s = (r0[:, None] + gr * jnp.arange(tmb // gr, dtype=jnp.int32)[None, :]).reshape(-1)
    cbe = cb[:, jnp.repeat(blk_e, tmb // gr)]
    tlo = (jnp.sum(cbe <= r0s[None, :], axis=0) - 1).astype(jnp.int32)
    thi = (jnp.sum(cbe < (r0s + gr)[None, :], axis=0) - 1).astype(jnp.int32)
    sched = (blk_e, r0.astype(jnp.int32), tlo, thi, nvalid.reshape(1).astype(jnp.int32))
    row_start = first[None, :] * tmb + cb
    kb = row_start // yb
    off = first[None, :] * tmb - kb * yb
    n_te = jnp.concatenate([cb[1:], counts[None, :]], axis=0) - cb
    lim = row_start - kb * yb + n_te
    flat = lambda a: a.reshape(-1).astype(jnp.int32)
    return sched, flat(kb), flat(off), flat(lim)


def moe_sparse(h, x, norm_gain, router, w_gate, w_up, w_down, final_gain):
    s, _ = x.shape
    pos, gate, post, before, total = moe_route(x, norm_gain, router)
    sched, kb, off, lim = _moe_schedule(before, total, s)
    y = moe_ffn(h, post, sched, w_gate, w_up, w_down)
    return moe_combine(x, pos, gate, y, kb, off, lim, final_gain)


FOX_F0 = 7 * BRANCH_W


def _split_w_in_kernel(w_ref, mix_ref, f_ref, gate_ref):
    w = w_ref[...]
    rows = w.shape[0]
    mix_ref[:, :FOX_F0] = w[:, :FOX_F0].astype(BF16)
    mix_ref[:, FOX_F0:] = w[:, FOX_F0 + N_HEADS:N_MIX_COLS + N_HEADS].astype(BF16)
    f_ref[...] = jnp.concatenate([w[:, FOX_F0:FOX_F0 + N_HEADS], jnp.zeros((rows, LANE - N_HEADS), F32)],
                                 axis=1).astype(BF16)
    gate_ref[...] = w[:, N_MIX_COLS + N_HEADS:].astype(BF16)


def _mixer_weights(w_in, layer, tr=128):
    _, d, cols = w_in.shape
    row = lambda w: pl.BlockSpec((tr, w), lambda i: (i, 0))
    return pl.pallas_call(
        _split_w_in_kernel,
        grid=(d // tr,),
        in_specs=[pl.BlockSpec((None, tr, cols), lambda i: (layer, i, 0))],
        out_specs=[row(N_MIX_COLS), row(LANE), row(N_BRANCH * D_MODEL)],
        out_shape=[jax.ShapeDtypeStruct((d, N_MIX_COLS), BF16), jax.ShapeDtypeStruct((d, LANE), BF16),
                   jax.ShapeDtypeStruct((d, N_BRANCH * D_MODEL), BF16)],
        compiler_params=_params("parallel"),
        name="split_w_in",
    )(w_in)


def kernel(x, w_in, w_branch, w_out, norm_mix_g, hgrn_lb_logits, hgrn_norm_g, fox_f_bias, pool_w, pool_scale,
           ret_gn_g, ret_gn_b, norm_ffn_g, ffn_w_gate, ffn_w_up, ffn_w_down, moe_router, moe_w_gate, moe_w_up,
           moe_w_down, final_norm_g):
    b, s, d = x.shape
    assert b == 1 and d == D_MODEL
    depth = w_in.shape[0]
    assert depth == 2, "layer 0 uses the dense FFN, layer 1 the experts and the final norm"
    xs = x.reshape(s, d)
    h = rmsnorm_bf16(xs, norm_mix_g[0])
    out = None
    for layer in range(depth):
        w_mix, w_f, w_gate = _mixer_weights(w_in, layer)
        proj = matmul(h, w_mix)
        fox_c, fox_vt, fox_stats = fox_gate(h, proj, w_f, fox_f_bias[layer])
        branches = (
            hgrn2(proj, hgrn_lb_logits, hgrn_norm_g[layer], layer),
            fox_attention(proj, fox_c, fox_vt, fox_stats),
            pool_mixer(proj, pool_w[layer], pool_scale[layer]),
            retention(proj, ret_gn_g[layer], ret_gn_b[layer]),
        )
        xs, h2 = merge(h, branches, xs, w_gate, w_branch[layer].astype(BF16), w_out[layer].astype(BF16),
                       norm_ffn_g[layer])
        if layer % 2 == 0:
            li = layer // 2
            xs, h = ffn_dense(h2, xs, ffn_w_gate[li], ffn_w_up[li], ffn_w_down[li], norm_mix_g[layer + 1])
        else:
            li = layer // 2
            out = moe_sparse(h2, xs, norm_ffn_g[layer], moe_router[li], moe_w_gate[li].astype(BF16),
                             moe_w_up[li].astype(BF16), moe_w_down[li].astype(BF16), final_norm_g)
    return out.reshape(b, s, d)
```

```python
import functools
import math

import jax
import jax.numpy as jnp
from jax import lax
from jax.experimental import pallas as pl
from jax.experimental.pallas import tpu as pltpu

D_MODEL = 1024
N_BRANCH = 4
BRANCH_W = D_MODEL // N_BRANCH
HEAD_DIM = 64
N_HEADS = BRANCH_W // HEAD_DIM
POOL_WINDOWS = (2, 4, 8, 16)
POOL_GROUP = BRANCH_W // len(POOL_WINDOWS)
POOL_HALO = 16
RET_DECAY_BASE = 5.0
ROPE_BASE = 10000.0
D_FF = 7 * D_MODEL // 2
N_EXPERTS = 8
RMS_EPS = 1e-6
LN_EPS = 1e-5
N_MIX_COLS = 12 * BRANCH_W

LANE = 128
SUBLANE = 8
VMEM_LIMIT = 56 * 1024 * 1024

HG_CHUNK = 64
HG_SUB = 16
HG_FAST_MIN_LOGDECAY = -60.0
RET_CHUNK = 256

F32 = jnp.float32
BF16 = jnp.bfloat16
NT_DIMS = (((1,), (1,)), ((), ()))


def _params(*sem):
    return pltpu.CompilerParams(dimension_semantics=sem, vmem_limit_bytes=VMEM_LIMIT)


def _const_spec(shape):
    nd = len(shape)
    return pl.BlockSpec(shape, lambda *_: (0,) * nd, pipeline_mode=pl.Buffered(1))


def _split3(x):
    hi = x.astype(BF16)
    r1 = x - hi.astype(F32)
    mid = r1.astype(BF16)
    lo = (r1 - mid.astype(F32)).astype(BF16)
    return hi, mid, lo


def _dot(a, b):
    return jnp.dot(a, b, preferred_element_type=F32)


def _dot_nt(a, b):
    return lax.dot_general(a, b, NT_DIMS, preferred_element_type=F32)


def _dot_exact_rhs(x, m_bf16):
    hi, mid, lo = _split3(x)
    return _dot(hi, m_bf16) + _dot(mid, m_bf16) + _dot(lo, m_bf16)


def _dot_exact_lhs(m_bf16, x):
    hi, mid, lo = _split3(x)
    return _dot(m_bf16, hi) + _dot(m_bf16, mid) + _dot(m_bf16, lo)


def _sigmoid(x):
    return 1.0 / (1.0 + jnp.exp(-x))


def _silu(x):
    return x * _sigmoid(x)


def _rms(x, gain):
    return x * lax.rsqrt(jnp.mean(x * x, axis=-1, keepdims=True) + RMS_EPS) * gain


def _head_of(n):
    return jnp.arange(n) // HEAD_DIM


def _head_ones():
    h = _head_of(BRANCH_W)
    return (h[:, None] == h[None, :]).astype(BF16)


def _head_masks():
    return (_head_of(BRANCH_W)[None, :] == jnp.arange(N_HEADS)[:, None]).astype(F32)


def _rmsnorm_kernel(x_ref, g_ref, o_ref):
    o_ref[...] = _rms(x_ref[...], g_ref[...]).astype(o_ref.dtype)


def rmsnorm_bf16(x, gain, tm=1024):
    s, d = x.shape
    tm = min(tm, s)
    return pl.pallas_call(
        _rmsnorm_kernel,
        grid=(s // tm,),
        in_specs=[pl.BlockSpec((tm, d), lambda i: (i, 0)), _const_spec((1, d))],
        out_specs=pl.BlockSpec((tm, d), lambda i: (i, 0)),
        out_shape=jax.ShapeDtypeStruct((s, d), BF16),
        compiler_params=_params("parallel"),
        name="rmsnorm",
    )(x, gain.reshape(1, d))


def _matmul_kernel(a_ref, b_ref, o_ref):
    o_ref[...] = _dot(a_ref[...], b_ref[...]).astype(o_ref.dtype)


def matmul(a, b, out_dtype=BF16, tm=2048, tn=1024):
    m, k = a.shape
    _, n = b.shape
    tm = min(tm, m)
    return pl.pallas_call(
        _matmul_kernel,
        grid=(n // tn, m // tm),
        in_specs=[pl.BlockSpec((tm, k), lambda j, i: (i, 0)),
                  pl.BlockSpec((k, tn), lambda j, i: (0, j))],
        out_specs=pl.BlockSpec((tm, tn), lambda j, i: (i, j)),
        out_shape=jax.ShapeDtypeStruct((m, n), out_dtype),
        compiler_params=_params("parallel", "parallel"),
        name="in_proj",
    )(a, b)


def _hgrn_kernel(q_ref, f_ref, i_ref, g_ref, lbl_ref, ng_ref, ones_ref, hm_ref, tril_ref, halfsum_ref, fmask_ref,
                 o_ref, st_ref, bpad, kpad, vpad, astack, lf_sc, kk_sc, *, layer, tile):
    c, sub = HG_CHUNK, HG_SUB
    nsub = c // sub
    half = c // 2

    @pl.when(pl.program_id(0) == 0)
    def _():
        st_ref[...] = jnp.zeros_like(st_ref)
        bpad[...] = jnp.zeros_like(bpad)
        kpad[...] = jnp.zeros_like(kpad)
        vpad[...] = jnp.zeros_like(vpad)

    lbl = lbl_ref[...]
    e = jnp.exp(lbl - jnp.max(lbl, axis=0, keepdims=True))
    p = e / jnp.sum(e, axis=0, keepdims=True)
    lb = jnp.zeros((1, BRANCH_W), F32)
    for l in range(1, layer + 1):
        lb = lb + p[l:l + 1, :]

    ones_bd = ones_ref[...]
    hm = hm_ref[...]
    tril = tril_ref[...]
    row = lax.broadcasted_iota(jnp.int32, (c, 1), 0)
    row_in_sub = row % sub
    bd_mask = ones_bd.astype(F32)

    sig = _sigmoid(f_ref[...].astype(F32))
    logf_all = jnp.log(lb + (1.0 - lb) * sig)
    lf_sc[...] = _dot_exact_lhs(tril, logf_all)
    kk_sc[...] = (1.0 - lb) * (1.0 - sig)
    min_decay = jnp.min(_dot(halfsum_ref[...], logf_all.astype(BF16)))

    def load(ci):
        r0 = pl.multiple_of(ci * c, c)
        q = q_ref[pl.ds(r0, c), :].astype(F32)
        v = i_ref[pl.ds(r0, c), :].astype(F32)
        kk = kk_sc[pl.ds(r0, c), :]
        b = lf_sc[pl.ds(r0, c), :]
        return r0, q, v, kk, b

    def finish(r0, q_decayed, v, kk, b, intra):
        st = st_ref[...]
        inter = _dot_nt(q_decayed.astype(BF16), st.astype(BF16))
        b_last = b[c - 1:c, :]
        ks_end = (kk * jnp.exp(b_last - b)).astype(BF16)
        upd = _dot(v.T.astype(BF16), ks_end)
        st_ref[...] = st * jnp.exp(b_last) + upd * bd_mask
        o = intra + inter
        ms = _dot_exact_rhs(o * o, ones_bd) * (1.0 / HEAD_DIM)
        g = g_ref[pl.ds(r0, c), :].astype(F32)
        y = o * lax.rsqrt(ms + RMS_EPS) * ng_ref[...] * _silu(g)
        o_ref[pl.ds(r0, c), :] = y.astype(o_ref.dtype)

    def fast_chunk(ci, carry):
        r0, q, v, kk, b = load(ci)
        second = row >= half
        m_row = b[half - 1:half, :]
        mref = jnp.where(second, m_row, 0.0)
        qp = q * jnp.exp(b - mref)
        kp = kk * jnp.exp(mref - b)
        e_m = jnp.exp(m_row)
        kaug = jnp.concatenate([kp, kp[:half, :] * e_m], axis=0).astype(BF16)
        vaug = jnp.concatenate([v, v[:half, :]], axis=0).astype(BF16)
        qx = jnp.concatenate([qp * hm[h:h + 1, :] for h in range(N_HEADS)], axis=0).astype(BF16)
        sc = jnp.where(fmask_ref[...] > 0.0, _dot_nt(qx, kaug), 0.0)
        r = _dot(sc.astype(BF16), vaug)
        intra = jnp.zeros((c, BRANCH_W), F32)
        for h in range(N_HEADS):
            intra = intra + r[h * c:(h + 1) * c, :] * hm[h:h + 1, :]
        finish(r0, jnp.where(second, qp * e_m, qp), v, kk, b, intra)
        return carry

    def exact_chunk(ci, carry):
        r0, q, v, kk, b = load(ci)

        bpad[pl.ds(sub, c), :] = b
        kpad[pl.ds(sub, c), :] = kk
        vpad[pl.ds(sub, c), :] = v

        for d in range(sub):
            b_d = bpad[pl.ds(sub - d, c), :]
            k_d = kpad[pl.ds(sub - d, c), :]
            a = jnp.where(row_in_sub >= d, q * k_d * jnp.exp(b - b_d), 0.0)
            astack[pl.ds(d * c, c), :] = a.astype(BF16)
        pall = _dot(astack[...], ones_bd)
        intra = jnp.zeros((c, BRANCH_W), F32)
        for d in range(sub):
            intra = intra + pall[d * c:(d + 1) * c, :] * vpad[pl.ds(sub - d, c), :]

        pieces = [jnp.zeros((sub, BRANCH_W), F32)]
        for si in range(1, nsub):
            lo = si * sub
            m_i = b[lo - 1:lo, :]
            qs = q[lo:lo + sub, :] * jnp.exp(b[lo:lo + sub, :] - m_i)
            ks = (kk[:lo, :] * jnp.exp(m_i - b[:lo, :])).astype(BF16)
            qx = jnp.concatenate([qs * hm[h:h + 1, :] for h in range(N_HEADS)], axis=0).astype(BF16)
            sc = _dot_nt(qx, ks)
            r = _dot(sc.astype(BF16), v[:lo, :].astype(BF16))
            acc = jnp.zeros((sub, BRANCH_W), F32)
            for h in range(N_HEADS):
                acc = acc + r[h * sub:(h + 1) * sub, :] * hm[h:h + 1, :]
            pieces.append(acc)
        intra = intra + jnp.concatenate(pieces, axis=0)
        finish(r0, q * jnp.exp(b), v, kk, b, intra)
        return carry

    lax.cond(min_decay >= HG_FAST_MIN_LOGDECAY,
             lambda: lax.fori_loop(0, tile // c, fast_chunk, 0, unroll=2),
             lambda: lax.fori_loop(0, tile // c, exact_chunk, 0))


def hgrn2(proj, lb_logits, norm_g, layer, tile=512):
    s = proj.shape[0]
    depth = lb_logits.shape[0]
    c, sub = HG_CHUNK, HG_SUB
    half = c // 2
    tile = min(tile, s)
    col = lambda j: pl.BlockSpec((tile, BRANCH_W), lambda i, j=j: (i, j))
    pos = jnp.arange(tile)
    tril = ((pos[:, None] // c == pos[None, :] // c) & (pos[None, :] <= pos[:, None])).astype(BF16)
    nhalf = tile // half
    halfsum = (jnp.arange(tile)[None, :] // half == jnp.arange(nhalf)[:, None]).astype(BF16)
    t = jnp.arange(c)[:, None]
    col_s = jnp.arange(c + half)[None, :]
    same_half = (col_s < c) & (col_s // half == t // half) & (col_s <= t)
    cross = (col_s >= c) & (t >= half)
    fmask = jnp.tile((same_half | cross).astype(F32), (N_HEADS, 1))
    return pl.pallas_call(
        functools.partial(_hgrn_kernel, layer=layer, tile=tile),
        grid=(s // tile,),
        in_specs=[col(0), col(1), col(2), col(3),
                  _const_spec((depth, BRANCH_W)), _const_spec((1, BRANCH_W)),
                  _const_spec((BRANCH_W, BRANCH_W)), _const_spec((N_HEADS, BRANCH_W)),
                  _const_spec((tile, tile)), _const_spec((nhalf, tile)), _const_spec((N_HEADS * c, c + half))],
        out_specs=pl.BlockSpec((tile, BRANCH_W), lambda i: (i, 0)),
        out_shape=jax.ShapeDtypeStruct((s, BRANCH_W), BF16),
        scratch_shapes=[pltpu.VMEM((BRANCH_W, BRANCH_W), F32),
                        pltpu.VMEM((c + sub, BRANCH_W), F32),
                        pltpu.VMEM((c + sub, BRANCH_W), F32),
                        pltpu.VMEM((c + sub, BRANCH_W), F32),
                        pltpu.VMEM((sub * c, BRANCH_W), BF16),
                        pltpu.VMEM((tile, BRANCH_W), F32),
                        pltpu.VMEM((tile, BRANCH_W), F32)],
        compiler_params=_params("arbitrary"),
        name="hgrn2",
    )(proj, proj, proj, proj, lb_logits.astype(F32), norm_g.reshape(1, BRANCH_W).astype(F32),
      _head_ones(), _head_masks(), tril, halfsum, fmask)


FOX_TILE = 256
FOX_NSTAT = 16
FOX_SKIP_LOG = 40.0


def _fox_gate_kernel(h_ref, q_ref, k_ref, wf_ref, bias_ref, tril_ref, ones_ref,
                     ct_ref, stat_ref, carry_ref, kmax_ref):
    @pl.when(pl.program_id(0) == 0)
    def _():
        carry_ref[...] = jnp.zeros_like(carry_ref)
        kmax_ref[...] = jnp.zeros_like(kmax_ref)

    logit = _dot(h_ref[...], wf_ref[...]) + bias_ref[...]
    logf = jnp.minimum(logit, 0.0) - jnp.log(1.0 + jnp.exp(-jnp.abs(logit)))
    cum = _dot_exact_lhs(tril_ref[...], logf) + carry_ref[...]
    carry_ref[...] = cum[-1:, :]
    ct_ref[...] = cum.T[:SUBLANE, :]

    ones_bd = ones_ref[...]
    q = q_ref[...].astype(F32)
    k = k_ref[...].astype(F32)
    scale = HEAD_DIM ** -0.5
    head_lane = lax.broadcasted_iota(jnp.int32, (1, BRANCH_W), 1) // HEAD_DIM
    c_heads = jnp.zeros(q.shape, F32)
    for h in range(N_HEADS):
        c_heads = jnp.where(head_lane == h, cum[:, h:h + 1], c_heads)
    slack = 1.0 + 2.0 ** -6
    qn = jnp.sqrt(_dot((q * q).astype(BF16), ones_bd)) * (scale * slack)
    kn = jnp.sqrt(_dot((k * k).astype(BF16), ones_bd)) * slack
    diag = _dot((q * k).astype(BF16), ones_bd) * scale - (2.0 ** -6) * qn * kn
    kmax = jnp.maximum(kmax_ref[...], jnp.max(kn, axis=0, keepdims=True))
    kmax_ref[...] = kmax
    rows = [jnp.max(qn, axis=0, keepdims=True),
            jnp.max(c_heads - diag, axis=0, keepdims=True),
            kmax,
            c_heads[-1:, :]]
    stat_ref[0] = jnp.concatenate(rows + [jnp.zeros((SUBLANE - len(rows), BRANCH_W), F32)], axis=0)


def fox_gate(h, proj, w_f, f_bias):
    s, d = h.shape
    tile = min(FOX_TILE, s)
    bias = jnp.zeros((1, LANE), F32).at[0, :N_HEADS].set(f_bias.astype(F32))
    tril = jnp.tril(jnp.ones((tile, tile), F32)).astype(BF16)
    ct, stats = pl.pallas_call(
        _fox_gate_kernel,
        grid=(s // tile,),
        in_specs=[pl.BlockSpec((tile, d), lambda i: (i, 0)),
                  pl.BlockSpec((tile, BRANCH_W), lambda i: (i, 4)),
                  pl.BlockSpec((tile, BRANCH_W), lambda i: (i, 5)),
                  _const_spec((d, LANE)), _const_spec((1, LANE)), _const_spec((tile, tile)),
                  _const_spec((BRANCH_W, BRANCH_W))],
        out_specs=[pl.BlockSpec((SUBLANE, tile), lambda i: (0, i)),
                   pl.BlockSpec((1, SUBLANE, BRANCH_W), lambda i: (i, 0, 0))],
        out_shape=[jax.ShapeDtypeStruct((SUBLANE, s), F32),
                   jax.ShapeDtypeStruct((s // tile, SUBLANE, BRANCH_W), F32)],
        scratch_shapes=[pltpu.VMEM((1, LANE), F32), pltpu.VMEM((1, BRANCH_W), F32)],
        compiler_params=_params("arbitrary"),
        name="fox_gate",
    )(h, proj, proj, w_f, bias, tril, _head_ones())
    return ct, stats[:, :4, ::HEAD_DIM].reshape(-1)


def _fox_kernel(stat_ref, q_ref, k_ref, v_ref, ct_ref, hm_ref, o_ref, m_sc, l_sc, acc_sc, *, tq):
    i = pl.program_id(0)
    q0 = pl.multiple_of(i * tq, tq)
    hm = hm_ref[...]
    q = q_ref[...].astype(F32) * (HEAD_DIM ** -0.5)
    qh = [(q * hm[h:h + 1, :]).astype(BF16) for h in range(N_HEADS)]
    c_q0 = ct_ref[:, pl.ds(q0, tq)][:, 0:1]

    first = i
    for h in range(N_HEADS):
        qmax = stat_ref[i * FOX_NSTAT + h]
        emax = stat_ref[i * FOX_NSTAT + N_HEADS + h]

        def needed(j, h=h, qmax=qmax, emax=emax):
            jc = jnp.maximum(j, 0)
            bound = (qmax * stat_ref[jc * FOX_NSTAT + 2 * N_HEADS + h] + emax
                     - stat_ref[jc * FOX_NSTAT + 3 * N_HEADS + h])
            return (j >= 0) & (bound >= -FOX_SKIP_LOG)

        last_dropped = lax.while_loop(needed, lambda j: j - 1, i - 1)
        first = jnp.minimum(first, last_dropped + 1)

    m_sc[...] = jnp.full_like(m_sc, -jnp.inf)
    l_sc[...] = jnp.zeros_like(l_sc)
    acc_sc[...] = jnp.zeros_like(acc_sc)

    def block(s0, diagonal):
        kb = k_ref[pl.ds(s0, tq), :]
        vb = v_ref[pl.ds(s0, tq), :]
        bias = c_q0 - ct_ref[:, pl.ds(s0, tq)]
        for h in range(N_HEADS):
            sc = _dot_nt(qh[h], kb) + bias[h:h + 1, :]
            if diagonal:
                r = lax.broadcasted_iota(jnp.int32, (tq, tq), 0)
                cidx = lax.broadcasted_iota(jnp.int32, (tq, tq), 1)
                sc = jnp.where(cidx <= r, sc, -jnp.inf)
            m_prev = m_sc[h]
            m_new = jnp.maximum(m_prev, jnp.max(sc, axis=1, keepdims=True))
            alpha = jnp.exp(m_prev - m_new)
            p = jnp.exp(sc - jnp.tile(m_new, (1, tq // LANE)))
            l_sc[h] = alpha * l_sc[h] + jnp.sum(p, axis=1, keepdims=True)
            acc_sc[h] = acc_sc[h] * jnp.tile(alpha, (1, BRANCH_W // LANE)) + _dot(p.astype(BF16), vb)
            m_sc[h] = m_new

    def body(j, carry):
        block(pl.multiple_of(j * tq, tq), False)
        return carry

    lax.fori_loop(first, i, body, 0)
    block(q0, True)

    out = jnp.zeros((tq, BRANCH_W), F32)
    for h in range(N_HEADS):
        out = out + acc_sc[h] * hm[h:h + 1, :] / jnp.tile(l_sc[h], (1, BRANCH_W // LANE))
    o_ref[...] = out.astype(o_ref.dtype)


def fox_attention(proj, ct, stats):
    s = proj.shape[0]
    tq = min(FOX_TILE, s)
    full = lambda j: pl.BlockSpec((s, BRANCH_W), lambda i, j=j: (0, j), pipeline_mode=pl.Buffered(1))
    return pl.pallas_call(
        functools.partial(_fox_kernel, tq=tq),
        grid=(s // tq,),
        in_specs=[pl.BlockSpec(memory_space=pltpu.SMEM),
                  pl.BlockSpec((tq, BRANCH_W), lambda i: (i, 4)), full(5), full(6),
                  _const_spec((SUBLANE, s)), _const_spec((N_HEADS, BRANCH_W))],
        out_specs=pl.BlockSpec((tq, BRANCH_W), lambda i: (i, 0)),
        out_shape=jax.ShapeDtypeStruct((s, BRANCH_W), BF16),
        scratch_shapes=[pltpu.VMEM((N_HEADS, tq, LANE), F32),
                        pltpu.VMEM((N_HEADS, tq, LANE), F32),
                        pltpu.VMEM((N_HEADS, tq, BRANCH_W), F32)],
        compiler_params=_params("parallel"),
        name="fox_attention",
    )(stats, proj, proj, proj, ct, _head_masks())


def _pool_kernel(u_ref, w_ref, scale_ref, o_ref, ext, *, tile):
    i = pl.program_id(0)

    @pl.when(i == 0)
    def _():
        ext[pl.ds(0, POOL_HALO), :] = jnp.zeros((POOL_HALO, BRANCH_W), F32)

    u = u_ref[...].astype(F32)
    ext[pl.ds(POOL_HALO, tile), :] = u
    pos = (i * tile + lax.broadcasted_iota(jnp.int32, (tile, 1), 0) + 1).astype(F32)
    halves = []
    for half in range(BRANCH_W // LANE):
        lanes = pl.ds(half * LANE, LANE)
        w_small, w_big = POOL_WINDOWS[2 * half], POOL_WINDOWS[2 * half + 1]
        run = u[:, half * LANE:(half + 1) * LANE]
        sums = {}
        for j in range(1, w_big):
            if j == w_small:
                sums[w_small] = run
            run = run + ext[pl.ds(POOL_HALO - j, tile), lanes]
        sums[w_big] = run
        lane = lax.broadcasted_iota(jnp.int32, (1, LANE), 1)
        small = lane < POOL_GROUP
        total = jnp.where(small, sums[w_small], sums[w_big])
        count = jnp.where(small, jnp.minimum(pos, float(w_small)), jnp.minimum(pos, float(w_big)))
        halves.append(total / count)
    mean = jnp.concatenate(halves, axis=1)
    d = (mean - u).astype(BF16)
    y = _dot(d, w_ref[...]) * scale_ref[...]
    o_ref[...] = y.astype(o_ref.dtype)
    ext[pl.ds(0, POOL_HALO), :] = u[tile - POOL_HALO:, :]


def pool_mixer(proj, w_pool, scale, tile=512):
    s = proj.shape[0]
    tile = min(tile, s)
    ng = len(POOL_WINDOWS)
    w_bd = jnp.zeros((BRANCH_W, BRANCH_W), F32)
    for gi in range(ng):
        lo = gi * POOL_GROUP
        w_bd = w_bd.at[lo:lo + POOL_GROUP, lo:lo + POOL_GROUP].set(w_pool[gi].astype(F32))
    return pl.pallas_call(
        functools.partial(_pool_kernel, tile=tile),
        grid=(s // tile,),
        in_specs=[pl.BlockSpec((tile, BRANCH_W), lambda i: (i, 7)),
                  _const_spec((BRANCH_W, BRANCH_W)), _const_spec((1, BRANCH_W))],
        out_specs=pl.BlockSpec((tile, BRANCH_W), lambda i: (i, 0)),
        out_shape=jax.ShapeDtypeStruct((s, BRANCH_W), BF16),
        scratch_shapes=[pltpu.VMEM((tile + POOL_HALO, BRANCH_W), F32)],
        compiler_params=_params("arbitrary"),
        name="pool_mixer",
    )(proj, w_bd.astype(BF16), scale.reshape(1, BRANCH_W).astype(F32))


def _ret_kernel(q_ref, k_ref, v_ref, g_ref, rope_ref, ecos_ref, esin_ref, perm_ref, ones_ref, hm_ref,
                dstack_ref, xi_ref, zeta_ref, gc_ref, gng_ref, gnb_ref, o_ref, st_ref, *, tile):
    c = RET_CHUNK

    @pl.when(pl.program_id(0) == 0)
    def _():
        st_ref[...] = jnp.zeros_like(st_ref)

    perm = perm_ref[...]
    ones_bd = ones_ref[...]
    bd_mask = ones_bd.astype(F32)
    hm = hm_ref[...]

    def chunk(ci, carry):
        r0 = pl.multiple_of(ci * c, c)
        qb = q_ref[pl.ds(r0, c), :]
        kb = k_ref[pl.ds(r0, c), :]
        v = v_ref[pl.ds(r0, c), :]
        g = g_ref[pl.ds(r0, c), :].astype(F32)
        tab = rope_ref[pl.ds(r0, c), :]
        cos = _dot_exact_rhs(tab, ecos_ref[...])
        sin = _dot_exact_rhs(tab, esin_ref[...])
        qr = qb.astype(F32) * cos + _dot(qb, perm) * sin
        kr = (kb.astype(F32) * cos + _dot(kb, perm) * sin) * (HEAD_DIM ** -0.5)

        qx = jnp.concatenate([qr * hm[h:h + 1, :] for h in range(N_HEADS)], axis=0).astype(BF16)
        sc = _dot_nt(qx, kr.astype(BF16)) * dstack_ref[...]
        r = _dot(sc.astype(BF16), v)
        intra = jnp.zeros((c, BRANCH_W), F32)
        for h in range(N_HEADS):
            intra = intra + r[h * c:(h + 1) * c, :] * hm[h:h + 1, :]

        st = st_ref[...]
        inter = _dot_nt((qr * xi_ref[...]).astype(BF16), st.astype(BF16))
        upd = _dot(v.astype(F32).T.astype(BF16), (kr * zeta_ref[...]).astype(BF16))
        st_ref[...] = st * gc_ref[...] + upd * bd_mask

        o = intra + inter
        mu = _dot_exact_rhs(o, ones_bd) * (1.0 / HEAD_DIM)
        cen = o - mu
        var = _dot_exact_rhs(cen * cen, ones_bd) * (1.0 / HEAD_DIM)
        y = cen * lax.rsqrt(var + LN_EPS) * gng_ref[...] + gnb_ref[...]
        o_ref[pl.ds(r0, c), :] = (y * _silu(g)).astype(o_ref.dtype)
        return carry

    lax.fori_loop(0, tile // c, chunk, 0, unroll=True)


def _rope_tables(s):
    half = HEAD_DIM // 2
    pos = jnp.arange(s, dtype=F32)
    inv_freq = ROPE_BASE ** (-jnp.arange(half, dtype=F32) / half)
    ang = pos[:, None] * inv_freq[None, :]
    table = jnp.concatenate([jnp.cos(ang), jnp.sin(ang), jnp.zeros((s, LANE - 2 * half), F32)], axis=1)
    lane = jnp.arange(BRANCH_W)
    src = jnp.arange(LANE)[:, None]
    ecos = (src == (lane % half)[None, :]).astype(F32)
    sign = jnp.where(lane % HEAD_DIM < half, -1.0, 1.0)
    esin = (src == (half + lane % half)[None, :]).astype(F32) * sign[None, :]
    return table, ecos.astype(BF16), esin.astype(BF16)


def _ret_constants():
    c = RET_CHUNK
    half = HEAD_DIM // 2
    lane = jnp.arange(BRANCH_W)
    partner = jnp.where(lane % HEAD_DIM < half, lane + half, lane - half)
    perm = (lane[:, None] == partner[None, :]).astype(BF16)
    log_gamma = jnp.log1p(-jnp.exp2(-RET_DECAY_BASE - jnp.arange(N_HEADS, dtype=F32)))
    ci = jnp.arange(c, dtype=F32)
    diff = ci[:, None] - ci[None, :]
    intra = jnp.where(diff >= 0, jnp.exp(diff * log_gamma[:, None, None]), 0.0)
    dstack = intra.reshape(N_HEADS * c, c)
    lg_lane = jnp.repeat(log_gamma, HEAD_DIM)[None, :]
    xi = jnp.exp((ci[:, None] + 1.0) * lg_lane)
    zeta = jnp.exp((c - 1.0 - ci[:, None]) * lg_lane)
    gc = jnp.exp(c * lg_lane)
    return perm, dstack, xi, zeta, gc


def retention(proj, gn_g, gn_b, tile=512):
    s = proj.shape[0]
    c = RET_CHUNK
    tile = min(tile, s)
    rope, ecos, esin = _rope_tables(s)
    perm, dstack, xi, zeta, gc = _ret_constants()
    col = lambda j: pl.BlockSpec((tile, BRANCH_W), lambda i, j=j: (i, j))
    row = pl.BlockSpec((tile, BRANCH_W), lambda i: (i, 0))
    return pl.pallas_call(
        functools.partial(_ret_kernel, tile=tile),
        grid=(s // tile,),
        in_specs=[col(8), col(9), col(10), col(11), pl.BlockSpec((tile, LANE), lambda i: (i, 0)),
                  _const_spec((LANE, BRANCH_W)), _const_spec((LANE, BRANCH_W)),
                  _const_spec((BRANCH_W, BRANCH_W)), _const_spec((BRANCH_W, BRANCH_W)),
                  _const_spec((N_HEADS, BRANCH_W)), _const_spec((N_HEADS * c, c)),
                  _const_spec((c, BRANCH_W)), _const_spec((c, BRANCH_W)), _const_spec((1, BRANCH_W)),
                  _const_spec((1, BRANCH_W)), _const_spec((1, BRANCH_W))],
        out_specs=row,
        out_shape=jax.ShapeDtypeStruct((s, BRANCH_W), BF16),
        scratch_shapes=[pltpu.VMEM((BRANCH_W, BRANCH_W), F32)],
        compiler_params=_params("arbitrary"),
        name="retention",
    )(proj, proj, proj, proj, rope, ecos, esin, perm, _head_ones(), _head_masks(), dstack, xi, zeta, gc,
      gn_g.reshape(1, BRANCH_W).astype(F32), gn_b.reshape(1, BRANCH_W).astype(F32))


def _merge_kernel(h_ref, o0_ref, o1_ref, o2_ref, o3_ref, x_ref, wg_ref, wb_ref, wo_ref, g_ref,
                  xo_ref, ho_ref):
    h = h_ref[...]
    merged = jnp.zeros(x_ref.shape, F32)
    for bi, o_ref in enumerate((o0_ref, o1_ref, o2_ref, o3_ref)):
        gate = _sigmoid(_dot(h, wg_ref[:, bi * D_MODEL:(bi + 1) * D_MODEL]))
        merged = merged + gate * _dot(o_ref[...], wb_ref[bi])
    x_new = x_ref[...] + _dot(merged.astype(BF16), wo_ref[...])
    xo_ref[...] = x_new
    ho_ref[...] = _rms(x_new, g_ref[...]).astype(ho_ref.dtype)


def merge(h, branches, x, w_gate, w_branch, w_out, next_gain, tm=512):
    s, d = x.shape
    tm = min(tm, s)
    row = lambda w: pl.BlockSpec((tm, w), lambda i: (i, 0))
    return pl.pallas_call(
        _merge_kernel,
        grid=(s // tm,),
        in_specs=[row(d), row(BRANCH_W), row(BRANCH_W), row(BRANCH_W), row(BRANCH_W), row(d),
                  _const_spec((d, N_BRANCH * d)), _const_spec((N_BRANCH, BRANCH_W, d)),
                  _const_spec((d, d)), _const_spec((1, d))],
        out_specs=[row(d), row(d)],
        out_shape=[jax.ShapeDtypeStruct((s, d), F32), jax.ShapeDtypeStruct((s, d), BF16)],
        compiler_params=_params("parallel"),
        name="merge",
    )(h, *branches, x, w_gate, w_branch, w_out, next_gain.reshape(1, d).astype(F32))


def _ffn_kernel(h_ref, x_ref, wg_ref, wu_ref, wd_ref, g_ref, xo_ref, ho_ref, acc_ref):
    f = pl.program_id(1)

    @pl.when(f == 0)
    def _():
        acc_ref[...] = jnp.zeros_like(acc_ref)

    h = h_ref[...]
    a = _silu(_dot(h, wg_ref[...].astype(BF16))) * _dot(h, wu_ref[...].astype(BF16))
    acc_ref[...] += _dot(a.astype(BF16), wd_ref[...].astype(BF16))

    @pl.when(f == pl.num_programs(1) - 1)
    def _():
        x_new = x_ref[...] + acc_ref[...]
        xo_ref[...] = x_new
        ho_ref[...] = _rms(x_new, g_ref[...]).astype(ho_ref.dtype)


def ffn_dense(h, x, w_gate, w_up, w_down, next_gain, tm=1024, tf=512):
    s, d = x.shape
    tm = min(tm, s)
    dff = w_gate.shape[1]
    row = lambda: pl.BlockSpec((tm, d), lambda i, f: (i, 0))
    return pl.pallas_call(
        _ffn_kernel,
        grid=(s // tm, dff // tf),
        in_specs=[row(), row(),
                  pl.BlockSpec((d, tf), lambda i, f: (0, f)),
                  pl.BlockSpec((d, tf), lambda i, f: (0, f)),
                  pl.BlockSpec((tf, d), lambda i, f: (f, 0)),
                  _const_spec((1, d))],
        out_specs=[row(), row()],
        out_shape=[jax.ShapeDtypeStruct((s, d), F32), jax.ShapeDtypeStruct((s, d), BF16)],
        scratch_shapes=[pltpu.VMEM((tm, d), F32)],
        compiler_params=_params("parallel", "arbitrary"),
        name="ffn_dense",
    )(h, x, w_gate, w_up, w_down, next_gain.reshape(1, d).astype(F32))


MOE_TOK_TILE = 256
MOE_ROW_BLOCK = 512
MOE_GATHER_ROWS = 128
MOE_GATHER_TILES = 4
MOE_VMEM_LIMIT = 60 * 1024 * 1024
MOE_Y_BLOCK = 128
MOE_Y_FETCH = MOE_TOK_TILE // MOE_Y_BLOCK + 1


def _dot_f32(x, w):
    xh, xm, _ = _split3(x)
    wh, wm, _ = _split3(w)
    return _dot(xh, wh) + (_dot(xh, wm) + _dot(xm, wh))


def _route_kernel(x_ref, ng_ref, router_ref, ltri_ref, pos_ref, gate_ref, post_ref, before_ref, total_ref,
                  carry_ref):
    tm = x_ref.shape[0]

    @pl.when(pl.program_id(0) == 0)
    def _():
        carry_ref[...] = jnp.zeros_like(carry_ref)

    hn = _rms(x_ref[...], ng_ref[...])
    logits = _dot_f32(hn, router_ref[...])
    lane = lax.broadcasted_iota(jnp.int32, (tm, LANE), 1)
    logits = jnp.where(lane < N_EXPERTS, logits, -jnp.inf)
    v1 = jnp.max(logits, axis=1, keepdims=True)
    i1 = jnp.min(jnp.where(logits == v1, lane, LANE), axis=1, keepdims=True)
    rest = jnp.where(lane == i1, -jnp.inf, logits)
    v2 = jnp.max(rest, axis=1, keepdims=True)
    i2 = jnp.min(jnp.where(rest == v2, lane, LANE), axis=1, keepdims=True)
    w1 = 1.0 / (1.0 + jnp.exp(v2 - v1))
    gate_ref[...] = jnp.where(lane == i1, w1, 0.0) + jnp.where(lane == i2, 1.0 - w1, 0.0)

    member = jnp.where((lane == i1) | (lane == i2), 1.0, 0.0)
    carry = carry_ref[...]
    rank = _dot(ltri_ref[...], member.astype(BF16)) + carry
    pos = jnp.where(member > 0.0, rank, -1.0)
    pos_ref[...] = pos
    post_ref[...] = pos.T[:SUBLANE, :]
    before_ref[0] = carry
    carry = carry + jnp.sum(member, axis=0, keepdims=True)
    carry_ref[...] = carry
    total_ref[...] = carry


def moe_route(x, norm_gain, router):
    s, d = x.shape
    tm = min(MOE_TOK_TILE, s)
    nt = s // tm
    router_p = jnp.zeros((d, LANE), F32).at[:, :N_EXPERTS].set(router.astype(F32))
    ltri = jnp.tril(jnp.ones((tm, tm), F32), -1).astype(BF16)
    row = pl.BlockSpec((tm, LANE), lambda i: (i, 0))
    return pl.pallas_call(
        _route_kernel,
        grid=(nt,),
        in_specs=[pl.BlockSpec((tm, d), lambda i: (i, 0)), _const_spec((1, d)), _const_spec((d, LANE)),
                  _const_spec((tm, tm))],
        out_specs=[row, row, pl.BlockSpec((SUBLANE, tm), lambda i: (0, i)),
                   pl.BlockSpec((1, 1, LANE), lambda i: (i, 0, 0)), pl.BlockSpec((1, LANE), lambda i: (0, 0))],
        out_shape=[jax.ShapeDtypeStruct((s, LANE), F32), jax.ShapeDtypeStruct((s, LANE), F32),
                   jax.ShapeDtypeStruct((SUBLANE, s), F32), jax.ShapeDtypeStruct((nt, 1, LANE), F32),
                   jax.ShapeDtypeStruct((1, LANE), F32)],
        scratch_shapes=[pltpu.VMEM((1, LANE), F32)],
        compiler_params=_params("arbitrary"),
        name="moe_route",
    )(x, norm_gain.reshape(1, d).astype(F32), router_p, ltri)


def _moe_ffn_kernel(be_ref, r0_ref, tlo_ref, thi_ref, nv_ref, h_ref, post_ref, wg_ref, wu_ref, wd_ref,
                    y_ref, x_sc, acc_sc, *, tt, nsub):
    b = pl.program_id(0)
    f = pl.program_id(1)
    nb = pl.num_programs(0)
    valid = b < nv_ref[0]
    tmb, d = acc_sc.shape
    gr = tmb // nsub
    last_tile = h_ref.shape[0] // tt - 1

    def picked(e, want, t):
        t0 = pl.multiple_of(t * tt, tt)
        p = post_ref[pl.ds(e, 1), pl.ds(t0, tt)]
        sel = jnp.where(p == want, 1.0, 0.0).astype(BF16)
        return _dot(sel, h_ref[pl.ds(t0, tt), :])

    def gather_head(blk, sb):
        e = be_ref[blk]
        lo = tlo_ref[blk * nsub + sb]
        hi = thi_ref[blk * nsub + sb]
        want = (lax.broadcasted_iota(jnp.int32, (gr, 1), 0) + (r0_ref[blk] + sb * gr)).astype(F32)
        rows = picked(e, want, lo)
        for k in range(1, MOE_GATHER_TILES):
            rows = rows + picked(e, jnp.where(lo + k <= hi, want, -2.0), jnp.minimum(lo + k, last_tile))
        return rows.astype(BF16)

    def gather_tail(slot, blk, sb):
        e = be_ref[blk]
        want = (lax.broadcasted_iota(jnp.int32, (gr, 1), 0) + (r0_ref[blk] + sb * gr)).astype(F32)
        rows = pl.ds(pl.multiple_of(sb * gr, gr), gr)

        def more(t, carry):
            x_sc[slot, rows, :] = (x_sc[slot, rows, :].astype(F32) + picked(e, want, t)).astype(BF16)
            return carry

        lax.fori_loop(tlo_ref[blk * nsub + sb] + MOE_GATHER_TILES, thi_ref[blk * nsub + sb] + 1, more, 0)

    @pl.when((b == 0) & (f == 0))
    def _():
        for sb in range(nsub):
            x_sc[0, pl.ds(sb * gr, gr), :] = gather_head(0, sb)
            gather_tail(0, 0, sb)

    nxt = jnp.minimum(b + 1, nb - 1)
    nslot = (b + 1) % 2

    @pl.when(valid)
    def _():
        x_sc[nslot, pl.ds(pl.multiple_of(f * gr, gr), gr), :] = gather_head(nxt, f)
        xb = x_sc[b % 2]
        a = _silu(_dot(xb, wg_ref[0])) * _dot(xb, wu_ref[0])
        acc_sc[...] = jnp.where(f == 0, 0.0, acc_sc[...]) + _dot(a.astype(BF16), wd_ref[0])

    @pl.when(valid & (thi_ref[nxt * nsub + f] - tlo_ref[nxt * nsub + f] >= MOE_GATHER_TILES))
    def _():
        gather_tail(nslot, nxt, f)

    @pl.when(f == nsub - 1)
    def _():
        y_ref[...] = jnp.where(valid, acc_sc[...], 0.0).astype(y_ref.dtype)


def moe_ffn(h, post, sched, w_gate, w_up, w_down):
    s, d = h.shape
    ne, _, dff = w_gate.shape
    tmb = MOE_ROW_BLOCK
    tt = min(MOE_TOK_TILE, s)
    nb = sched[0].shape[0]
    nf = tmb // MOE_GATHER_ROWS
    tf = dff // nf

    def fidx(b, f, nv):
        return jnp.where(b < nv[0], f, nf - 1)

    grid_spec = pltpu.PrefetchScalarGridSpec(
        num_scalar_prefetch=5,
        grid=(nb, nf),
        in_specs=[pl.BlockSpec((s, d), lambda b, f, *_: (0, 0), pipeline_mode=pl.Buffered(1)),
                  pl.BlockSpec((SUBLANE, s), lambda b, f, *_: (0, 0), pipeline_mode=pl.Buffered(1)),
                  pl.BlockSpec((1, d, tf), lambda b, f, be, r0, tlo, thi, nv: (be[b], 0, fidx(b, f, nv))),
                  pl.BlockSpec((1, d, tf), lambda b, f, be, r0, tlo, thi, nv: (be[b], 0, fidx(b, f, nv))),
                  pl.BlockSpec((1, tf, d), lambda b, f, be, r0, tlo, thi, nv: (be[b], fidx(b, f, nv), 0))],
        out_specs=pl.BlockSpec((tmb, d), lambda b, f, *_: (b, 0)),
        scratch_shapes=[pltpu.VMEM((2, tmb, d), BF16), pltpu.VMEM((tmb, d), F32)],
    )
    return pl.pallas_call(
        functools.partial(_moe_ffn_kernel, tt=tt, nsub=nf),
        grid_spec=grid_spec,
        out_shape=jax.ShapeDtypeStruct((nb * tmb, d), BF16),
        compiler_params=pltpu.CompilerParams(dimension_semantics=("arbitrary", "arbitrary"),
                                             vmem_limit_bytes=MOE_VMEM_LIMIT),
        name="moe_ffn",
    )(*sched, h, post, w_gate, w_up, w_down)


def _moe_combine_kernel(kb_ref, off_ref, lim_ref, x_ref, pos_ref, gate_ref, fg_ref, *rest):
    y_refs, o_ref, acc_sc = rest[:-2], rest[-2], rest[-1]
    t = pl.program_id(0)
    tm = x_ref.shape[0]
    yb = y_refs[0].shape[0]
    pos = pos_ref[...]
    gate = gate_ref[...]
    lane = lax.broadcasted_iota(jnp.int32, (tm, LANE), 1)
    col = lax.broadcasted_iota(jnp.int32, (1, yb), 1).astype(F32)
    def routed(e):
        pe = jnp.sum(jnp.where(lane == e, pos, 0.0), axis=1, keepdims=True)
        ge = jnp.sum(jnp.where(lane == e, gate, 0.0), axis=1, keepdims=True)
        r = jnp.where(pe >= 0.0, pe + off_ref[t * N_EXPERTS + e].astype(F32), -1.0)
        return r, ge

    def picked(e, k, r):
        sel = jnp.where(r == col + float(k * yb), 1.0, 0.0).astype(BF16)
        return _dot(sel, y_refs[MOE_Y_FETCH * e + k][...])

    acc = x_ref[...]
    for e in range(N_EXPERTS):
        r, ge = routed(e)
        acc = acc + ge * (picked(e, 0, r) + picked(e, 1, r))
    acc_sc[...] = acc
    for e in range(N_EXPERTS):
        for k in range(2, MOE_Y_FETCH):
            @pl.when(lim_ref[t * N_EXPERTS + e] > k * yb)
            def _(k=k, e=e):
                r, ge = routed(e)
                acc_sc[...] += ge * picked(e, k, r)
    o_ref[...] = _rms(acc_sc[...], fg_ref[...]).astype(o_ref.dtype)


def moe_combine(x, pos, gate, y, kb, off, lim, final_gain):
    s, d = x.shape
    tm = min(MOE_TOK_TILE, s)
    yb = MOE_Y_BLOCK
    last = y.shape[0] // yb - 1

    def yspec(e, k):
        def index(t, kb_r, off_r, lim_r):
            blk = jnp.minimum(kb_r[t * N_EXPERTS + e] + k, last)
            return (blk if k < 1 else jnp.where(lim_r[t * N_EXPERTS + e] > k * yb, blk, 0), 0)
        return pl.BlockSpec((yb, d), index)

    grid_spec = pltpu.PrefetchScalarGridSpec(
        num_scalar_prefetch=3,
        grid=(s // tm,),
        in_specs=[pl.BlockSpec((tm, d), lambda t, *_: (t, 0)),
                  pl.BlockSpec((tm, LANE), lambda t, *_: (t, 0)),
                  pl.BlockSpec((tm, LANE), lambda t, *_: (t, 0)),
                  pl.BlockSpec((1, d), lambda t, *_: (0, 0))]
        + [yspec(e, k) for e in range(N_EXPERTS) for k in range(MOE_Y_FETCH)],
        out_specs=pl.BlockSpec((tm, d), lambda t, *_: (t, 0)),
        scratch_shapes=[pltpu.VMEM((tm, d), F32)],
    )
    return pl.pallas_call(
        _moe_combine_kernel,
        grid_spec=grid_spec,
        out_shape=jax.ShapeDtypeStruct((s, d), F32),
        compiler_params=_params("arbitrary"),
        name="moe_combine",
    )(kb, off, lim, x, pos, gate, final_gain.reshape(1, d).astype(F32), *([y] * (MOE_Y_FETCH * N_EXPERTS)))


def _moe_schedule(before, total, s):
    tmb, yb, gr = MOE_ROW_BLOCK, MOE_Y_BLOCK, MOE_GATHER_ROWS
    nb = 2 * s // tmb + N_EXPERTS
    counts = total[0, :N_EXPERTS].astype(jnp.int32)
    nblk = (counts + tmb - 1) // tmb
    end = jnp.cumsum(nblk)
    first = end - nblk
    nvalid = end[-1]
    b = jnp.minimum(jnp.arange(nb, dtype=jnp.int32), nvalid - 1)
    blk_e = jnp.sum(b[:, None] >= end[None, :], axis=1).astype(jnp.int32)
    r0 = (b - first[blk_e]) * tmb
    cb = before[:, 0, :N_EXPERTS].astype(jnp.int32)
    r0s = (r0[:, None] + gr * jnp.arange(tmb // gr, dtype=jnp.int32)[None, :]).reshape(-1)
    cbe = cb[:, jnp.repeat(blk_e, tmb // gr)]
    tlo = (jnp.sum(cbe <= r0s[None, :], axis=0) - 1).astype(jnp.int32)
    thi = (jnp.sum(cbe < (r0s + gr)[None, :], axis=0) - 1).astype(jnp.int32)
    sched = (blk_e, r0.astype(jnp.int32), tlo, thi, nvalid.reshape(1).astype(jnp.int32))
    row_start = first[None, :] * tmb + cb
    kb = row_start // yb
    off = first[None, :] * tmb - kb * yb
    n_te = jnp.concatenate([cb[1:], counts[None, :]], axis=0) - cb
    lim = row_start - kb * yb + n_te
    flat = lambda a: a.reshape(-1).astype(jnp.int32)
    return sched, flat(kb), flat(off), flat(lim)


def moe_sparse(h, x, norm_gain, router, w_gate, w_up, w_down, final_gain):
    s, _ = x.shape
    pos, gate, post, before, total = moe_route(x, norm_gain, router)
    sched, kb, off, lim = _moe_schedule(before, total, s)
    y = moe_ffn(h, post, sched, w_gate, w_up, w_down)
    return moe_combine(x, pos, gate, y, kb, off, lim, final_gain)


FOX_F0 = 7 * BRANCH_W


def _split_w_in_kernel(w_ref, mix_ref, f_ref, gate_ref):
    w = w_ref[...]
    rows = w.shape[0]
    mix_ref[:, :FOX_F0] = w[:, :FOX_F0].astype(BF16)
    mix_ref[:, FOX_F0:] = w[:, FOX_F0 + N_HEADS:N_MIX_COLS + N_HEADS].astype(BF16)
    f_ref[...] = jnp.concatenate([w[:, FOX_F0:FOX_F0 + N_HEADS], jnp.zeros((rows, LANE - N_HEADS), F32)],
                                 axis=1).astype(BF16)
    gate_ref[...] = w[:, N_MIX_COLS + N_HEADS:].astype(BF16)


def _mixer_weights(w_in, layer, tr=128):
    _, d, cols = w_in.shape
    row = lambda w: pl.BlockSpec((tr, w), lambda i: (i, 0))
    return pl.pallas_call(
        _split_w_in_kernel,
        grid=(d // tr,),
        in_specs=[pl.BlockSpec((None, tr, cols), lambda i: (layer, i, 0))],
        out_specs=[row(N_MIX_COLS), row(LANE), row(N_BRANCH * D_MODEL)],
        out_shape=[jax.ShapeDtypeStruct((d, N_MIX_COLS), BF16), jax.ShapeDtypeStruct((d, LANE), BF16),
                   jax.ShapeDtypeStruct((d, N_BRANCH * D_MODEL), BF16)],
        compiler_params=_params("parallel"),
        name="split_w_in",
    )(w_in)


def kernel(x, w_in, w_branch, w_out, norm_mix_g, hgrn_lb_logits, hgrn_norm_g, fox_f_bias, pool_w, pool_scale,
           ret_gn_g, ret_gn_b, norm_ffn_g, ffn_w_gate, ffn_w_up, ffn_w_down, moe_router, moe_w_gate, moe_w_up,
           moe_w_down, final_norm_g):
    b, s, d = x.shape
    assert b == 1 and d == D_MODEL
    depth = w_in.shape[0]
    assert depth == 2, "layer 0 uses the dense FFN, layer 1 the experts and the final norm"
    xs = x.reshape(s, d)
    h = rmsnorm_bf16(xs, norm_mix_g[0])
    out = None
    for layer in range(depth):
        w_mix, w_f, w_gate = _mixer_weights(w_in, layer)
        proj = matmul(h, w_mix)
        ct, fox_stats = fox_gate(h, proj, w_f, fox_f_bias[layer])
        branches = (
            hgrn2(proj, hgrn_lb_logits, hgrn_norm_g[layer], layer),
            fox_attention(proj, ct, fox_stats),
            pool_mixer(proj, pool_w[layer], pool_scale[layer]),
            retention(proj, ret_gn_g[layer], ret_gn_b[layer]),
        )
        xs, h2 = merge(h, branches, xs, w_gate, w_branch[layer].astype(BF16), w_out[layer].astype(BF16),
                       norm_ffn_g[layer])
        if layer % 2 == 0:
            li = layer // 2
            xs, h = ffn_dense(h2, xs, ffn_w_gate[li], ffn_w_up[li], ffn_w_down[li], norm_mix_g[layer + 1])
        else:
            li = layer // 2
            out = moe_sparse(h2, xs, norm_ffn_g[layer], moe_router[li], moe_w_gate[li].astype(BF16),
                             moe_w_up[li].astype(BF16), moe_w_down[li].astype(BF16), final_norm_g)
    return out.reshape(b, s, d)
```

```python
import functools
import math

import jax
import jax.numpy as jnp
from jax import lax
from jax.experimental import pallas as pl
from jax.experimental.pallas import tpu as pltpu

D_MODEL = 1024
N_BRANCH = 4
BRANCH_W = D_MODEL // N_BRANCH
HEAD_DIM = 64
N_HEADS = BRANCH_W // HEAD_DIM
POOL_WINDOWS = (2, 4, 8, 16)
POOL_GROUP = BRANCH_W // len(POOL_WINDOWS)
POOL_HALO = 16
RET_DECAY_BASE = 5.0
ROPE_BASE = 10000.0
D_FF = 7 * D_MODEL // 2
N_EXPERTS = 8
RMS_EPS = 1e-6
LN_EPS = 1e-5
N_MIX_COLS = 12 * BRANCH_W

LANE = 128
SUBLANE = 8
VMEM_LIMIT = 56 * 1024 * 1024

HG_CHUNK = 64
HG_SUB = 16
HG_FAST_MIN_LOGDECAY = -60.0
RET_CHUNK = 256

F32 = jnp.float32
BF16 = jnp.bfloat16
NT_DIMS = (((1,), (1,)), ((), ()))


def _params(*sem):
    return pltpu.CompilerParams(dimension_semantics=sem, vmem_limit_bytes=VMEM_LIMIT)


def _const_spec(shape):
    nd = len(shape)
    return pl.BlockSpec(shape, lambda *_: (0,) * nd, pipeline_mode=pl.Buffered(1))


def _split3(x):
    hi = x.astype(BF16)
    r1 = x - hi.astype(F32)
    mid = r1.astype(BF16)
    lo = (r1 - mid.astype(F32)).astype(BF16)
    return hi, mid, lo


def _dot(a, b):
    return jnp.dot(a, b, preferred_element_type=F32)


def _dot_nt(a, b):
    return lax.dot_general(a, b, NT_DIMS, preferred_element_type=F32)


def _dot_exact_rhs(x, m_bf16):
    hi, mid, lo = _split3(x)
    return _dot(hi, m_bf16) + _dot(mid, m_bf16) + _dot(lo, m_bf16)


def _dot_exact_lhs(m_bf16, x):
    hi, mid, lo = _split3(x)
    return _dot(m_bf16, hi) + _dot(m_bf16, mid) + _dot(m_bf16, lo)


def _sigmoid(x):
    return 1.0 / (1.0 + jnp.exp(-x))


def _silu(x):
    return x * _sigmoid(x)


def _rms(x, gain):
    return x * lax.rsqrt(jnp.mean(x * x, axis=-1, keepdims=True) + RMS_EPS) * gain


def _head_of(n):
    return jnp.arange(n) // HEAD_DIM


def _head_ones():
    h = _head_of(BRANCH_W)
    return (h[:, None] == h[None, :]).astype(BF16)


def _head_masks():
    return (_head_of(BRANCH_W)[None, :] == jnp.arange(N_HEADS)[:, None]).astype(F32)


def _rmsnorm_kernel(x_ref, g_ref, o_ref):
    o_ref[...] = _rms(x_ref[...], g_ref[...]).astype(o_ref.dtype)


def rmsnorm_bf16(x, gain, tm=1024):
    s, d = x.shape
    tm = min(tm, s)
    return pl.pallas_call(
        _rmsnorm_kernel,
        grid=(s // tm,),
        in_specs=[pl.BlockSpec((tm, d), lambda i: (i, 0)), _const_spec((1, d))],
        out_specs=pl.BlockSpec((tm, d), lambda i: (i, 0)),
        out_shape=jax.ShapeDtypeStruct((s, d), BF16),
        compiler_params=_params("parallel"),
        name="rmsnorm",
    )(x, gain.reshape(1, d))


def _matmul_kernel(a_ref, b_ref, o_ref):
    o_ref[...] = _dot(a_ref[...], b_ref[...]).astype(o_ref.dtype)


def matmul(a, b, out_dtype=BF16, tm=2048, tn=1024):
    m, k = a.shape
    _, n = b.shape
    tm = min(tm, m)
    return pl.pallas_call(
        _matmul_kernel,
        grid=(n // tn, m // tm),
        in_specs=[pl.BlockSpec((tm, k), lambda j, i: (i, 0)),
                  pl.BlockSpec((k, tn), lambda j, i: (0, j))],
        out_specs=pl.BlockSpec((tm, tn), lambda j, i: (i, j)),
        out_shape=jax.ShapeDtypeStruct((m, n), out_dtype),
        compiler_params=_params("parallel", "parallel"),
        name="in_proj",
    )(a, b)


def _hgrn_kernel(q_ref, f_ref, i_ref, g_ref, lbl_ref, ng_ref, ones_ref, hm_ref, tril_ref, halfsum_ref, fmask_ref,
                 o_ref, st_ref, bpad, kpad, vpad, astack, lf_sc, kk_sc, o_sc, *, layer, tile):
    c, sub = HG_CHUNK, HG_SUB
    nsub = c // sub
    half = c // 2

    @pl.when(pl.program_id(0) == 0)
    def _():
        st_ref[...] = jnp.zeros_like(st_ref)
        bpad[...] = jnp.zeros_like(bpad)
        kpad[...] = jnp.zeros_like(kpad)
        vpad[...] = jnp.zeros_like(vpad)

    lbl = lbl_ref[...]
    e = jnp.exp(lbl - jnp.max(lbl, axis=0, keepdims=True))
    p = e / jnp.sum(e, axis=0, keepdims=True)
    lb = jnp.zeros((1, BRANCH_W), F32)
    for l in range(1, layer + 1):
        lb = lb + p[l:l + 1, :]

    ones_bd = ones_ref[...]
    hm = hm_ref[...]
    tril = tril_ref[...]
    row = lax.broadcasted_iota(jnp.int32, (c, 1), 0)
    row_in_sub = row % sub
    bd_mask = ones_bd.astype(F32)

    sig = _sigmoid(f_ref[...].astype(F32))
    logf_all = jnp.log(lb + (1.0 - lb) * sig)
    lf_sc[...] = _dot_exact_lhs(tril, logf_all)
    kk_sc[...] = (1.0 - lb) * (1.0 - sig)
    min_decay = jnp.min(_dot(halfsum_ref[...], logf_all.astype(BF16)))

    def load(ci):
        r0 = pl.multiple_of(ci * c, c)
        q = q_ref[pl.ds(r0, c), :].astype(F32)
        v = i_ref[pl.ds(r0, c), :].astype(F32)
        kk = kk_sc[pl.ds(r0, c), :]
        b = lf_sc[pl.ds(r0, c), :]
        return r0, q, v, kk, b

    def finish(r0, q_decayed, v, kk, b, intra):
        st = st_ref[...]
        inter = _dot_nt(q_decayed.astype(BF16), st.astype(BF16))
        b_last = b[c - 1:c, :]
        ks_end = (kk * jnp.exp(b_last - b)).astype(BF16)
        upd = _dot(v.T.astype(BF16), ks_end)
        st_ref[...] = st * jnp.exp(b_last) + upd * bd_mask
        o_sc[pl.ds(r0, c), :] = intra + inter

    def fast_chunk(ci, carry):
        r0, q, v, kk, b = load(ci)
        second = row >= half
        m_row = b[half - 1:half, :]
        mref = jnp.where(second, m_row, 0.0)
        qp = q * jnp.exp(b - mref)
        kp = kk * jnp.exp(mref - b)
        e_m = jnp.exp(m_row)
        kaug = jnp.concatenate([kp, kp[:half, :] * e_m], axis=0).astype(BF16)
        vaug = jnp.concatenate([v, v[:half, :]], axis=0).astype(BF16)
        qx = jnp.concatenate([qp * hm[h:h + 1, :] for h in range(N_HEADS)], axis=0).astype(BF16)
        sc = jnp.where(fmask_ref[...] > 0.0, _dot_nt(qx, kaug), 0.0)
        r = _dot(sc.astype(BF16), vaug)
        intra = jnp.zeros((c, BRANCH_W), F32)
        for h in range(N_HEADS):
            intra = intra + r[h * c:(h + 1) * c, :] * hm[h:h + 1, :]
        finish(r0, jnp.where(second, qp * e_m, qp), v, kk, b, intra)
        return carry

    def exact_chunk(ci, carry):
        r0, q, v, kk, b = load(ci)

        bpad[pl.ds(sub, c), :] = b
        kpad[pl.ds(sub, c), :] = kk
        vpad[pl.ds(sub, c), :] = v

        for d in range(sub):
            b_d = bpad[pl.ds(sub - d, c), :]
            k_d = kpad[pl.ds(sub - d, c), :]
            a = jnp.where(row_in_sub >= d, q * k_d * jnp.exp(b - b_d), 0.0)
            astack[pl.ds(d * c, c), :] = a.astype(BF16)
        pall = _dot(astack[...], ones_bd)
        intra = jnp.zeros((c, BRANCH_W), F32)
        for d in range(sub):
            intra = intra + pall[d * c:(d + 1) * c, :] * vpad[pl.ds(sub - d, c), :]

        pieces = [jnp.zeros((sub, BRANCH_W), F32)]
        for si in range(1, nsub):
            lo = si * sub
            m_i = b[lo - 1:lo, :]
            qs = q[lo:lo + sub, :] * jnp.exp(b[lo:lo + sub, :] - m_i)
            ks = (kk[:lo, :] * jnp.exp(m_i - b[:lo, :])).astype(BF16)
            qx = jnp.concatenate([qs * hm[h:h + 1, :] for h in range(N_HEADS)], axis=0).astype(BF16)
            sc = _dot_nt(qx, ks)
            r = _dot(sc.astype(BF16), v[:lo, :].astype(BF16))
            acc = jnp.zeros((sub, BRANCH_W), F32)
            for h in range(N_HEADS):
                acc = acc + r[h * sub:(h + 1) * sub, :] * hm[h:h + 1, :]
            pieces.append(acc)
        intra = intra + jnp.concatenate(pieces, axis=0)
        finish(r0, q * jnp.exp(b), v, kk, b, intra)
        return carry

    lax.cond(min_decay >= HG_FAST_MIN_LOGDECAY,
             lambda: lax.fori_loop(0, tile // c, fast_chunk, 0, unroll=2),
             lambda: lax.fori_loop(0, tile // c, exact_chunk, 0))

    o = o_sc[...]
    ms = _dot_exact_rhs(o * o, ones_bd) * (1.0 / HEAD_DIM)
    y = o * lax.rsqrt(ms + RMS_EPS) * ng_ref[...] * _silu(g_ref[...].astype(F32))
    o_ref[...] = y.astype(o_ref.dtype)


def hgrn2(proj, lb_logits, norm_g, layer, tile=512):
    s = proj.shape[0]
    depth = lb_logits.shape[0]
    c, sub = HG_CHUNK, HG_SUB
    half = c // 2
    tile = min(tile, s)
    col = lambda j: pl.BlockSpec((tile, BRANCH_W), lambda i, j=j: (i, j))
    pos = jnp.arange(tile)
    tril = ((pos[:, None] // c == pos[None, :] // c) & (pos[None, :] <= pos[:, None])).astype(BF16)
    nhalf = tile // half
    halfsum = (jnp.arange(tile)[None, :] // half == jnp.arange(nhalf)[:, None]).astype(BF16)
    t = jnp.arange(c)[:, None]
    col_s = jnp.arange(c + half)[None, :]
    same_half = (col_s < c) & (col_s // half == t // half) & (col_s <= t)
    cross = (col_s >= c) & (t >= half)
    fmask = jnp.tile((same_half | cross).astype(F32), (N_HEADS, 1))
    return pl.pallas_call(
        functools.partial(_hgrn_kernel, layer=layer, tile=tile),
        grid=(s // tile,),
        in_specs=[col(0), col(1), col(2), col(3),
                  _const_spec((depth, BRANCH_W)), _const_spec((1, BRANCH_W)),
                  _const_spec((BRANCH_W, BRANCH_W)), _const_spec((N_HEADS, BRANCH_W)),
                  _const_spec((tile, tile)), _const_spec((nhalf, tile)), _const_spec((N_HEADS * c, c + half))],
        out_specs=pl.BlockSpec((tile, BRANCH_W), lambda i: (i, 0)),
        out_shape=jax.ShapeDtypeStruct((s, BRANCH_W), BF16),
        scratch_shapes=[pltpu.VMEM((BRANCH_W, BRANCH_W), F32),
                        pltpu.VMEM((c + sub, BRANCH_W), F32),
                        pltpu.VMEM((c + sub, BRANCH_W), F32),
                        pltpu.VMEM((c + sub, BRANCH_W), F32),
                        pltpu.VMEM((sub * c, BRANCH_W), BF16),
                        pltpu.VMEM((tile, BRANCH_W), F32),
                        pltpu.VMEM((tile, BRANCH_W), F32),
                        pltpu.VMEM((tile, BRANCH_W), F32)],
        compiler_params=_params("arbitrary"),
        name="hgrn2",
    )(proj, proj, proj, proj, lb_logits.astype(F32), norm_g.reshape(1, BRANCH_W).astype(F32),
      _head_ones(), _head_masks(), tril, halfsum, fmask)


FOX_TILE = 256
FOX_GATE_BLOCKS = 2
FOX_NSTAT = 16
FOX_SKIP_LOG = 40.0


def _fox_gate_kernel(h_ref, q_ref, k_ref, wf_ref, bias_ref, tril_ref, ones_ref,
                     ct_ref, stat_ref, carry_ref, kmax_ref):
    @pl.when(pl.program_id(0) == 0)
    def _():
        carry_ref[...] = jnp.zeros_like(carry_ref)
        kmax_ref[...] = jnp.zeros_like(kmax_ref)

    logit = _dot(h_ref[...], wf_ref[...]) + bias_ref[...]
    logf = jnp.minimum(logit, 0.0) - jnp.log(1.0 + jnp.exp(-jnp.abs(logit)))
    cum = _dot_exact_lhs(tril_ref[...], logf) + carry_ref[...]
    carry_ref[...] = cum[-1:, :]
    ct_ref[...] = cum.T[:SUBLANE, :]

    ones_bd = ones_ref[...]
    q = q_ref[...].astype(F32)
    k = k_ref[...].astype(F32)
    scale = HEAD_DIM ** -0.5
    head_lane = lax.broadcasted_iota(jnp.int32, (1, BRANCH_W), 1) // HEAD_DIM
    c_heads = jnp.zeros(q.shape, F32)
    for h in range(N_HEADS):
        c_heads = jnp.where(head_lane == h, cum[:, h:h + 1], c_heads)
    slack = 1.0 + 2.0 ** -6
    qn = jnp.sqrt(_dot((q * q).astype(BF16), ones_bd)) * (scale * slack)
    kn = jnp.sqrt(_dot((k * k).astype(BF16), ones_bd)) * slack
    diag = _dot((q * k).astype(BF16), ones_bd) * scale - (2.0 ** -6) * qn * kn
    e_row = c_heads - diag
    kmax = kmax_ref[...]
    for blk in range(q.shape[0] // FOX_TILE):
        rs = slice(blk * FOX_TILE, (blk + 1) * FOX_TILE)
        kmax = jnp.maximum(kmax, jnp.max(kn[rs], axis=0, keepdims=True))
        rows = [jnp.max(qn[rs], axis=0, keepdims=True),
                jnp.max(e_row[rs], axis=0, keepdims=True),
                kmax,
                c_heads[(blk + 1) * FOX_TILE - 1:(blk + 1) * FOX_TILE, :]]
        stat_ref[blk] = jnp.concatenate(rows + [jnp.zeros((SUBLANE - len(rows), BRANCH_W), F32)], axis=0)
    kmax_ref[...] = kmax


def fox_gate(h, proj, w_f, f_bias):
    s, d = h.shape
    tile = min(FOX_GATE_BLOCKS * FOX_TILE, s)
    nblk = tile // FOX_TILE
    bias = jnp.zeros((1, LANE), F32).at[0, :N_HEADS].set(f_bias.astype(F32))
    tril = jnp.tril(jnp.ones((tile, tile), F32)).astype(BF16)
    ct, stats = pl.pallas_call(
        _fox_gate_kernel,
        grid=(s // tile,),
        in_specs=[pl.BlockSpec((tile, d), lambda i: (i, 0)),
                  pl.BlockSpec((tile, BRANCH_W), lambda i: (i, 4)),
                  pl.BlockSpec((tile, BRANCH_W), lambda i: (i, 5)),
                  _const_spec((d, LANE)), _const_spec((1, LANE)), _const_spec((tile, tile)),
                  _const_spec((BRANCH_W, BRANCH_W))],
        out_specs=[pl.BlockSpec((SUBLANE, tile), lambda i: (0, i)),
                   pl.BlockSpec((nblk, SUBLANE, BRANCH_W), lambda i: (i, 0, 0))],
        out_shape=[jax.ShapeDtypeStruct((SUBLANE, s), F32),
                   jax.ShapeDtypeStruct((s // FOX_TILE, SUBLANE, BRANCH_W), F32)],
        scratch_shapes=[pltpu.VMEM((1, LANE), F32), pltpu.VMEM((1, BRANCH_W), F32)],
        compiler_params=_params("arbitrary"),
        name="fox_gate",
    )(h, proj, proj, w_f, bias, tril, _head_ones())
    return ct, stats[:, :4, ::HEAD_DIM].reshape(-1)


def _fox_kernel(stat_ref, q_ref, k_ref, v_ref, ct_ref, hm_ref, o_ref, m_sc, l_sc, acc_sc, *, tq):
    i = pl.program_id(0)
    q0 = pl.multiple_of(i * tq, tq)
    hm = hm_ref[...]
    q = q_ref[...].astype(F32) * (HEAD_DIM ** -0.5)
    qh = [(q * hm[h:h + 1, :]).astype(BF16) for h in range(N_HEADS)]
    c_q0 = ct_ref[:, pl.ds(q0, tq)][:, 0:1]

    first = i
    for h in range(N_HEADS):
        qmax = stat_ref[i * FOX_NSTAT + h]
        emax = stat_ref[i * FOX_NSTAT + N_HEADS + h]

        def needed(j, h=h, qmax=qmax, emax=emax):
            jc = jnp.maximum(j, 0)
            bound = (qmax * stat_ref[jc * FOX_NSTAT + 2 * N_HEADS + h] + emax
                     - stat_ref[jc * FOX_NSTAT + 3 * N_HEADS + h])
            return (j >= 0) & (bound >= -FOX_SKIP_LOG)

        last_dropped = lax.while_loop(needed, lambda j: j - 1, i - 1)
        first = jnp.minimum(first, last_dropped + 1)

    m_sc[...] = jnp.full_like(m_sc, -jnp.inf)
    l_sc[...] = jnp.zeros_like(l_sc)
    acc_sc[...] = jnp.zeros_like(acc_sc)

    def block(s0, diagonal):
        kb = k_ref[pl.ds(s0, tq), :]
        vb = v_ref[pl.ds(s0, tq), :]
        bias = c_q0 - ct_ref[:, pl.ds(s0, tq)]
        for h in range(N_HEADS):
            sc = _dot_nt(qh[h], kb) + bias[h:h + 1, :]
            if diagonal:
                r = lax.broadcasted_iota(jnp.int32, (tq, tq), 0)
                cidx = lax.broadcasted_iota(jnp.int32, (tq, tq), 1)
                sc = jnp.where(cidx <= r, sc, -jnp.inf)
            m_prev = m_sc[h]
            m_new = jnp.maximum(m_prev, jnp.max(sc, axis=1, keepdims=True))
            alpha = jnp.exp(m_prev - m_new)
            p = jnp.exp(sc - jnp.tile(m_new, (1, tq // LANE)))
            l_sc[h] = alpha * l_sc[h] + jnp.sum(p, axis=1, keepdims=True)
            acc_sc[h] = acc_sc[h] * jnp.tile(alpha, (1, BRANCH_W // LANE)) + _dot(p.astype(BF16), vb)
            m_sc[h] = m_new

    def body(j, carry):
        block(pl.multiple_of(j * tq, tq), False)
        return carry

    lax.fori_loop(first, i, body, 0)
    block(q0, True)

    out = jnp.zeros((tq, BRANCH_W), F32)
    for h in range(N_HEADS):
        out = out + acc_sc[h] * hm[h:h + 1, :] / jnp.tile(l_sc[h], (1, BRANCH_W // LANE))
    o_ref[...] = out.astype(o_ref.dtype)


def fox_attention(proj, ct, stats):
    s = proj.shape[0]
    tq = min(FOX_TILE, s)
    full = lambda j: pl.BlockSpec((s, BRANCH_W), lambda i, j=j: (0, j), pipeline_mode=pl.Buffered(1))
    return pl.pallas_call(
        functools.partial(_fox_kernel, tq=tq),
        grid=(s // tq,),
        in_specs=[pl.BlockSpec(memory_space=pltpu.SMEM),
                  pl.BlockSpec((tq, BRANCH_W), lambda i: (i, 4)), full(5), full(6),
                  _const_spec((SUBLANE, s)), _const_spec((N_HEADS, BRANCH_W))],
        out_specs=pl.BlockSpec((tq, BRANCH_W), lambda i: (i, 0)),
        out_shape=jax.ShapeDtypeStruct((s, BRANCH_W), BF16),
        scratch_shapes=[pltpu.VMEM((N_HEADS, tq, LANE), F32),
                        pltpu.VMEM((N_HEADS, tq, LANE), F32),
                        pltpu.VMEM((N_HEADS, tq, BRANCH_W), F32)],
        compiler_params=_params("parallel"),
        name="fox_attention",
    )(stats, proj, proj, proj, ct, _head_masks())


def _pool_kernel(u_ref, w_ref, scale_ref, o_ref, ext, *, tile):
    i = pl.program_id(0)

    @pl.when(i == 0)
    def _():
        ext[pl.ds(0, POOL_HALO), :] = jnp.zeros((POOL_HALO, BRANCH_W), F32)

    u = u_ref[...].astype(F32)
    ext[pl.ds(POOL_HALO, tile), :] = u
    pos = (i * tile + lax.broadcasted_iota(jnp.int32, (tile, 1), 0) + 1).astype(F32)
    halves = []
    for half in range(BRANCH_W // LANE):
        lanes = pl.ds(half * LANE, LANE)
        w_small, w_big = POOL_WINDOWS[2 * half], POOL_WINDOWS[2 * half + 1]
        run = u[:, half * LANE:(half + 1) * LANE]
        sums = {}
        for j in range(1, w_big):
            if j == w_small:
                sums[w_small] = run
            run = run + ext[pl.ds(POOL_HALO - j, tile), lanes]
        sums[w_big] = run
        lane = lax.broadcasted_iota(jnp.int32, (1, LANE), 1)
        small = lane < POOL_GROUP
        total = jnp.where(small, sums[w_small], sums[w_big])
        count = jnp.where(small, jnp.minimum(pos, float(w_small)), jnp.minimum(pos, float(w_big)))
        halves.append(total / count)
    mean = jnp.concatenate(halves, axis=1)
    d = (mean - u).astype(BF16)
    y = _dot(d, w_ref[...]) * scale_ref[...]
    o_ref[...] = y.astype(o_ref.dtype)
    ext[pl.ds(0, POOL_HALO), :] = u[tile - POOL_HALO:, :]


def pool_mixer(proj, w_pool, scale, tile=512):
    s = proj.shape[0]
    tile = min(tile, s)
    ng = len(POOL_WINDOWS)
    w_bd = jnp.zeros((BRANCH_W, BRANCH_W), F32)
    for gi in range(ng):
        lo = gi * POOL_GROUP
        w_bd = w_bd.at[lo:lo + POOL_GROUP, lo:lo + POOL_GROUP].set(w_pool[gi].astype(F32))
    return pl.pallas_call(
        functools.partial(_pool_kernel, tile=tile),
        grid=(s // tile,),
        in_specs=[pl.BlockSpec((tile, BRANCH_W), lambda i: (i, 7)),
                  _const_spec((BRANCH_W, BRANCH_W)), _const_spec((1, BRANCH_W))],
        out_specs=pl.BlockSpec((tile, BRANCH_W), lambda i: (i, 0)),
        out_shape=jax.ShapeDtypeStruct((s, BRANCH_W), BF16),
        scratch_shapes=[pltpu.VMEM((tile + POOL_HALO, BRANCH_W), F32)],
        compiler_params=_params("arbitrary"),
        name="pool_mixer",
    )(proj, w_bd.astype(BF16), scale.reshape(1, BRANCH_W).astype(F32))


def _ret_kernel(q_ref, k_ref, v_ref, g_ref, rope_ref, ecos_ref, esin_ref, perm_ref, ones_ref, hm_ref,
                dstack_ref, xi_ref, zeta_ref, gc_ref, gng_ref, gnb_ref, o_ref, st_ref, *, tile):
    c = RET_CHUNK

    @pl.when(pl.program_id(0) == 0)
    def _():
        st_ref[...] = jnp.zeros_like(st_ref)

    perm = perm_ref[...]
    ones_bd = ones_ref[...]
    bd_mask = ones_bd.astype(F32)
    hm = hm_ref[...]

    tab = rope_ref[...]
    cos = _dot_exact_rhs(tab, ecos_ref[...])
    sin = _dot_exact_rhs(tab, esin_ref[...])
    q_all = q_ref[...]
    k_all = k_ref[...]
    qr_all = q_all.astype(F32) * cos + _dot(q_all, perm) * sin
    kr_all = (k_all.astype(F32) * cos + _dot(k_all, perm) * sin) * (HEAD_DIM ** -0.5)

    outs = []
    for ci in range(tile // c):
        r0 = ci * c
        qr = qr_all[r0:r0 + c, :]
        kr = kr_all[r0:r0 + c, :]
        v = v_ref[pl.ds(r0, c), :]

        qx = jnp.concatenate([qr * hm[h:h + 1, :] for h in range(N_HEADS)], axis=0).astype(BF16)
        sc = _dot_nt(qx, kr.astype(BF16)) * dstack_ref[...]
        r = _dot(sc.astype(BF16), v)
        intra = jnp.zeros((c, BRANCH_W), F32)
        for h in range(N_HEADS):
            intra = intra + r[h * c:(h + 1) * c, :] * hm[h:h + 1, :]

        st = st_ref[...]
        inter = _dot_nt((qr * xi_ref[...]).astype(BF16), st.astype(BF16))
        upd = _dot(v.astype(F32).T.astype(BF16), (kr * zeta_ref[...]).astype(BF16))
        st_ref[...] = st * gc_ref[...] + upd * bd_mask
        outs.append(intra + inter)

    o = jnp.concatenate(outs, axis=0)
    mu = _dot_exact_rhs(o, ones_bd) * (1.0 / HEAD_DIM)
    cen = o - mu
    var = _dot_exact_rhs(cen * cen, ones_bd) * (1.0 / HEAD_DIM)
    y = cen * lax.rsqrt(var + LN_EPS) * gng_ref[...] + gnb_ref[...]
    o_ref[...] = (y * _silu(g_ref[...].astype(F32))).astype(o_ref.dtype)


def _rope_tables(s):
    half = HEAD_DIM // 2
    pos = jnp.arange(s, dtype=F32)
    inv_freq = ROPE_BASE ** (-jnp.arange(half, dtype=F32) / half)
    ang = pos[:, None] * inv_freq[None, :]
    table = jnp.concatenate([jnp.cos(ang), jnp.sin(ang), jnp.zeros((s, LANE - 2 * half), F32)], axis=1)
    lane = jnp.arange(BRANCH_W)
    src = jnp.arange(LANE)[:, None]
    ecos = (src == (lane % half)[None, :]).astype(F32)
    sign = jnp.where(lane % HEAD_DIM < half, -1.0, 1.0)
    esin = (src == (half + lane % half)[None, :]).astype(F32) * sign[None, :]
    return table, ecos.astype(BF16), esin.astype(BF16)


def _ret_constants():
    c = RET_CHUNK
    half = HEAD_DIM // 2
    lane = jnp.arange(BRANCH_W)
    partner = jnp.where(lane % HEAD_DIM < half, lane + half, lane - half)
    perm = (lane[:, None] == partner[None, :]).astype(BF16)
    log_gamma = jnp.log1p(-jnp.exp2(-RET_DECAY_BASE - jnp.arange(N_HEADS, dtype=F32)))
    ci = jnp.arange(c, dtype=F32)
    diff = ci[:, None] - ci[None, :]
    intra = jnp.where(diff >= 0, jnp.exp(diff * log_gamma[:, None, None]), 0.0)
    dstack = intra.reshape(N_HEADS * c, c)
    lg_lane = jnp.repeat(log_gamma, HEAD_DIM)[None, :]
    xi = jnp.exp((ci[:, None] + 1.0) * lg_lane)
    zeta = jnp.exp((c - 1.0 - ci[:, None]) * lg_lane)
    gc = jnp.exp(c * lg_lane)
    return perm, dstack, xi, zeta, gc


def retention(proj, gn_g, gn_b, tile=512):
    s = proj.shape[0]
    c = RET_CHUNK
    tile = min(tile, s)
    rope, ecos, esin = _rope_tables(s)
    perm, dstack, xi, zeta, gc = _ret_constants()
    col = lambda j: pl.BlockSpec((tile, BRANCH_W), lambda i, j=j: (i, j))
    row = pl.BlockSpec((tile, BRANCH_W), lambda i: (i, 0))
    return pl.pallas_call(
        functools.partial(_ret_kernel, tile=tile),
        grid=(s // tile,),
        in_specs=[col(8), col(9), col(10), col(11), pl.BlockSpec((tile, LANE), lambda i: (i, 0)),
                  _const_spec((LANE, BRANCH_W)), _const_spec((LANE, BRANCH_W)),
                  _const_spec((BRANCH_W, BRANCH_W)), _const_spec((BRANCH_W, BRANCH_W)),
                  _const_spec((N_HEADS, BRANCH_W)), _const_spec((N_HEADS * c, c)),
                  _const_spec((c, BRANCH_W)), _const_spec((c, BRANCH_W)), _const_spec((1, BRANCH_W)),
                  _const_spec((1, BRANCH_W)), _const_spec((1, BRANCH_W))],
        out_specs=row,
        out_shape=jax.ShapeDtypeStruct((s, BRANCH_W), BF16),
        scratch_shapes=[pltpu.VMEM((BRANCH_W, BRANCH_W), F32)],
        compiler_params=_params("arbitrary"),
        name="retention",
    )(proj, proj, proj, proj, rope, ecos, esin, perm, _head_ones(), _head_masks(), dstack, xi, zeta, gc,
      gn_g.reshape(1, BRANCH_W).astype(F32), gn_b.reshape(1, BRANCH_W).astype(F32))


def _merge_kernel(h_ref, o0_ref, o1_ref, o2_ref, o3_ref, x_ref, wg_ref, wb_ref, wo_ref, g_ref,
                  xo_ref, ho_ref):
    h = h_ref[...]
    merged = jnp.zeros(x_ref.shape, F32)
    for bi, o_ref in enumerate((o0_ref, o1_ref, o2_ref, o3_ref)):
        gate = _sigmoid(_dot(h, wg_ref[:, bi * D_MODEL:(bi + 1) * D_MODEL]))
        merged = merged + gate * _dot(o_ref[...], wb_ref[bi])
    x_new = x_ref[...] + _dot(merged.astype(BF16), wo_ref[...])
    xo_ref[...] = x_new
    ho_ref[...] = _rms(x_new, g_ref[...]).astype(ho_ref.dtype)


def merge(h, branches, x, w_gate, w_branch, w_out, next_gain, tm=512):
    s, d = x.shape
    tm = min(tm, s)
    row = lambda w: pl.BlockSpec((tm, w), lambda i: (i, 0))
    return pl.pallas_call(
        _merge_kernel,
        grid=(s // tm,),
        in_specs=[row(d), row(BRANCH_W), row(BRANCH_W), row(BRANCH_W), row(BRANCH_W), row(d),
                  _const_spec((d, N_BRANCH * d)), _const_spec((N_BRANCH, BRANCH_W, d)),
                  _const_spec((d, d)), _const_spec((1, d))],
        out_specs=[row(d), row(d)],
        out_shape=[jax.ShapeDtypeStruct((s, d), F32), jax.ShapeDtypeStruct((s, d), BF16)],
        compiler_params=_params("parallel"),
        name="merge",
    )(h, *branches, x, w_gate, w_branch, w_out, next_gain.reshape(1, d).astype(F32))


def _ffn_kernel(h_ref, x_ref, wg_ref, wu_ref, wd_ref, g_ref, xo_ref, ho_ref, acc_ref):
    f = pl.program_id(1)

    @pl.when(f == 0)
    def _():
        acc_ref[...] = jnp.zeros_like(acc_ref)

    h = h_ref[...]
    a = _silu(_dot(h, wg_ref[...].astype(BF16))) * _dot(h, wu_ref[...].astype(BF16))
    acc_ref[...] += _dot(a.astype(BF16), wd_ref[...].astype(BF16))

    @pl.when(f == pl.num_programs(1) - 1)
    def _():
        x_new = x_ref[...] + acc_ref[...]
        xo_ref[...] = x_new
        ho_ref[...] = _rms(x_new, g_ref[...]).astype(ho_ref.dtype)


def ffn_dense(h, x, w_gate, w_up, w_down, next_gain, tm=1024, tf=512):
    s, d = x.shape
    tm = min(tm, s)
    dff = w_gate.shape[1]
    row = lambda: pl.BlockSpec((tm, d), lambda i, f: (i, 0))
    return pl.pallas_call(
        _ffn_kernel,
        grid=(s // tm, dff // tf),
        in_specs=[row(), row(),
                  pl.BlockSpec((d, tf), lambda i, f: (0, f)),
                  pl.BlockSpec((d, tf), lambda i, f: (0, f)),
                  pl.BlockSpec((tf, d), lambda i, f: (f, 0)),
                  _const_spec((1, d))],
        out_specs=[row(), row()],
        out_shape=[jax.ShapeDtypeStruct((s, d), F32), jax.ShapeDtypeStruct((s, d), BF16)],
        scratch_shapes=[pltpu.VMEM((tm, d), F32)],
        compiler_params=_params("parallel", "arbitrary"),
        name="ffn_dense",
    )(h, x, w_gate, w_up, w_down, next_gain.reshape(1, d).astype(F32))


MOE_TOK_TILE = 256
MOE_ROW_BLOCK = 512
MOE_GATHER_ROWS = 128
MOE_GATHER_TILES = 4
MOE_VMEM_LIMIT = 60 * 1024 * 1024
MOE_Y_BLOCK = 128
MOE_Y_FETCH = MOE_TOK_TILE // MOE_Y_BLOCK + 1


def _dot_f32(x, w):
    xh, xm, _ = _split3(x)
    wh, wm, _ = _split3(w)
    return _dot(xh, wh) + (_dot(xh, wm) + _dot(xm, wh))


def _route_kernel(x_ref, ng_ref, router_ref, ltri_ref, pos_ref, gate_ref, post_ref, before_ref, total_ref,
                  carry_ref):
    tm = x_ref.shape[0]

    @pl.when(pl.program_id(0) == 0)
    def _():
        carry_ref[...] = jnp.zeros_like(carry_ref)

    hn = _rms(x_ref[...], ng_ref[...])
    logits = _dot_f32(hn, router_ref[...])
    lane = lax.broadcasted_iota(jnp.int32, (tm, LANE), 1)
    logits = jnp.where(lane < N_EXPERTS, logits, -jnp.inf)
    v1 = jnp.max(logits, axis=1, keepdims=True)
    i1 = jnp.min(jnp.where(logits == v1, lane, LANE), axis=1, keepdims=True)
    rest = jnp.where(lane == i1, -jnp.inf, logits)
    v2 = jnp.max(rest, axis=1, keepdims=True)
    i2 = jnp.min(jnp.where(rest == v2, lane, LANE), axis=1, keepdims=True)
    w1 = 1.0 / (1.0 + jnp.exp(v2 - v1))
    gate_ref[...] = jnp.where(lane == i1, w1, 0.0) + jnp.where(lane == i2, 1.0 - w1, 0.0)

    member = jnp.where((lane == i1) | (lane == i2), 1.0, 0.0)
    carry = carry_ref[...]
    rank = _dot(ltri_ref[...], member.astype(BF16)) + carry
    pos = jnp.where(member > 0.0, rank, -1.0)
    pos_ref[...] = pos
    post_ref[...] = pos.T[:SUBLANE, :]
    before_ref[0] = carry
    carry = carry + jnp.sum(member, axis=0, keepdims=True)
    carry_ref[...] = carry
    total_ref[...] = carry


def moe_route(x, norm_gain, router):
    s, d = x.shape
    tm = min(MOE_TOK_TILE, s)
    nt = s // tm
    router_p = jnp.zeros((d, LANE), F32).at[:, :N_EXPERTS].set(router.astype(F32))
    ltri = jnp.tril(jnp.ones((tm, tm), F32), -1).astype(BF16)
    row = pl.BlockSpec((tm, LANE), lambda i: (i, 0))
    return pl.pallas_call(
        _route_kernel,
        grid=(nt,),
        in_specs=[pl.BlockSpec((tm, d), lambda i: (i, 0)), _const_spec((1, d)), _const_spec((d, LANE)),
                  _const_spec((tm, tm))],
        out_specs=[row, row, pl.BlockSpec((SUBLANE, tm), lambda i: (0, i)),
                   pl.BlockSpec((1, 1, LANE), lambda i: (i, 0, 0)), pl.BlockSpec((1, LANE), lambda i: (0, 0))],
        out_shape=[jax.ShapeDtypeStruct((s, LANE), F32), jax.ShapeDtypeStruct((s, LANE), F32),
                   jax.ShapeDtypeStruct((SUBLANE, s), F32), jax.ShapeDtypeStruct((nt, 1, LANE), F32),
                   jax.ShapeDtypeStruct((1, LANE), F32)],
        scratch_shapes=[pltpu.VMEM((1, LANE), F32)],
        compiler_params=_params("arbitrary"),
        name="moe_route",
    )(x, norm_gain.reshape(1, d).astype(F32), router_p, ltri)


def _moe_ffn_kernel(be_ref, r0_ref, tlo_ref, thi_ref, nv_ref, h_ref, post_ref, wg_ref, wu_ref, wd_ref,
                    y_ref, x_sc, acc_sc, *, tt, nsub):
    b = pl.program_id(0)
    f = pl.program_id(1)
    nb = pl.num_programs(0)
    valid = b < nv_ref[0]
    tmb, d = acc_sc.shape
    gr = tmb // nsub
    last_tile = h_ref.shape[0] // tt - 1

    def picked(e, want, t):
        t0 = pl.multiple_of(t * tt, tt)
        p = post_ref[pl.ds(e, 1), pl.ds(t0, tt)]
        sel = jnp.where(p == want, 1.0, 0.0).astype(BF16)
        return _dot(sel, h_ref[pl.ds(t0, tt), :])

    def gather_head(blk, sb):
        e = be_ref[blk]
        lo = tlo_ref[blk * nsub + sb]
        hi = thi_ref[blk * nsub + sb]
        want = (lax.broadcasted_iota(jnp.int32, (gr, 1), 0) + (r0_ref[blk] + sb * gr)).astype(F32)
        rows = picked(e, want, lo)
        for k in range(1, MOE_GATHER_TILES):
            rows = rows + picked(e, jnp.where(lo + k <= hi, want, -2.0), jnp.minimum(lo + k, last_tile))
        return rows.astype(BF16)

    def gather_tail(slot, blk, sb):
        e = be_ref[blk]
        want = (lax.broadcasted_iota(jnp.int32, (gr, 1), 0) + (r0_ref[blk] + sb * gr)).astype(F32)
        rows = pl.ds(pl.multiple_of(sb * gr, gr), gr)

        def more(t, carry):
            x_sc[slot, rows, :] = (x_sc[slot, rows, :].astype(F32) + picked(e, want, t)).astype(BF16)
            return carry

        lax.fori_loop(tlo_ref[blk * nsub + sb] + MOE_GATHER_TILES, thi_ref[blk * nsub + sb] + 1, more, 0)

    @pl.when((b == 0) & (f == 0))
    def _():
        for sb in range(nsub):
            x_sc[0, pl.ds(sb * gr, gr), :] = gather_head(0, sb)
            gather_tail(0, 0, sb)

    nxt = jnp.minimum(b + 1, nb - 1)
    nslot = (b + 1) % 2

    @pl.when(valid)
    def _():
        x_sc[nslot, pl.ds(pl.multiple_of(f * gr, gr), gr), :] = gather_head(nxt, f)
        xb = x_sc[b % 2]
        a = _silu(_dot(xb, wg_ref[0])) * _dot(xb, wu_ref[0])
        acc_sc[...] = jnp.where(f == 0, 0.0, acc_sc[...]) + _dot(a.astype(BF16), wd_ref[0])

    @pl.when(valid & (thi_ref[nxt * nsub + f] - tlo_ref[nxt * nsub + f] >= MOE_GATHER_TILES))
    def _():
        gather_tail(nslot, nxt, f)

    @pl.when(f == nsub - 1)
    def _():
        y_ref[...] = jnp.where(valid, acc_sc[...], 0.0).astype(y_ref.dtype)


def moe_ffn(h, post, sched, w_gate, w_up, w_down):
    s, d = h.shape
    ne, _, dff = w_gate.shape
    tmb = MOE_ROW_BLOCK
    tt = min(MOE_TOK_TILE, s)
    nb = sched[0].shape[0]
    nf = tmb // MOE_GATHER_ROWS
    tf = dff // nf

    def fidx(b, f, nv):
        return jnp.where(b < nv[0], f, nf - 1)

    grid_spec = pltpu.PrefetchScalarGridSpec(
        num_scalar_prefetch=5,
        grid=(nb, nf),
        in_specs=[pl.BlockSpec((s, d), lambda b, f, *_: (0, 0), pipeline_mode=pl.Buffered(1)),
                  pl.BlockSpec((SUBLANE, s), lambda b, f, *_: (0, 0), pipeline_mode=pl.Buffered(1)),
                  pl.BlockSpec((1, d, tf), lambda b, f, be, r0, tlo, thi, nv: (be[b], 0, fidx(b, f, nv))),
                  pl.BlockSpec((1, d, tf), lambda b, f, be, r0, tlo, thi, nv: (be[b], 0, fidx(b, f, nv))),
                  pl.BlockSpec((1, tf, d), lambda b, f, be, r0, tlo, thi, nv: (be[b], fidx(b, f, nv), 0))],
        out_specs=pl.BlockSpec((tmb, d), lambda b, f, *_: (b, 0)),
        scratch_shapes=[pltpu.VMEM((2, tmb, d), BF16), pltpu.VMEM((tmb, d), F32)],
    )
    return pl.pallas_call(
        functools.partial(_moe_ffn_kernel, tt=tt, nsub=nf),
        grid_spec=grid_spec,
        out_shape=jax.ShapeDtypeStruct((nb * tmb, d), BF16),
        compiler_params=pltpu.CompilerParams(dimension_semantics=("arbitrary", "arbitrary"),
                                             vmem_limit_bytes=MOE_VMEM_LIMIT),
        name="moe_ffn",
    )(*sched, h, post, w_gate, w_up, w_down)


def _moe_combine_kernel(kb_ref, off_ref, lim_ref, x_ref, pos_ref, gate_ref, fg_ref, *rest):
    y_refs, o_ref, acc_sc = rest[:-2], rest[-2], rest[-1]
    t = pl.program_id(0)
    tm = x_ref.shape[0]
    yb = y_refs[0].shape[0]
    pos = pos_ref[...]
    gate = gate_ref[...]
    lane = lax.broadcasted_iota(jnp.int32, (tm, LANE), 1)
    col = lax.broadcasted_iota(jnp.int32, (1, yb), 1).astype(F32)
    def routed(e):
        pe = jnp.sum(jnp.where(lane == e, pos, 0.0), axis=1, keepdims=True)
        ge = jnp.sum(jnp.where(lane == e, gate, 0.0), axis=1, keepdims=True)
        r = jnp.where(pe >= 0.0, pe + off_ref[t * N_EXPERTS + e].astype(F32), -1.0)
        return r, ge

    def picked(e, k, r):
        sel = jnp.where(r == col + float(k * yb), 1.0, 0.0).astype(BF16)
        return _dot(sel, y_refs[MOE_Y_FETCH * e + k][...])

    col2 = lax.broadcasted_iota(jnp.int32, (1, 2 * yb), 1).astype(F32)
    acc = x_ref[...]
    for e in range(N_EXPERTS):
        r, ge = routed(e)
        pair = jnp.concatenate([y_refs[MOE_Y_FETCH * e][...], y_refs[MOE_Y_FETCH * e + 1][...]], axis=0)
        acc = acc + ge * _dot(jnp.where(r == col2, 1.0, 0.0).astype(BF16), pair)
    acc_sc[...] = acc
    for e in range(N_EXPERTS):
        for k in range(2, MOE_Y_FETCH):
            @pl.when(lim_ref[t * N_EXPERTS + e] > k * yb)
            def _(k=k, e=e):
                r, ge = routed(e)
                acc_sc[...] += ge * picked(e, k, r)
    o_ref[...] = _rms(acc_sc[...], fg_ref[...]).astype(o_ref.dtype)


def moe_combine(x, pos, gate, y, kb, off, lim, final_gain):
    s, d = x.shape
    tm = min(MOE_TOK_TILE, s)
    yb = MOE_Y_BLOCK
    last = y.shape[0] // yb - 1

    def yspec(e, k):
        def index(t, kb_r, off_r, lim_r):
            blk = jnp.minimum(kb_r[t * N_EXPERTS + e] + k, last)
            return (blk if k < 1 else jnp.where(lim_r[t * N_EXPERTS + e] > k * yb, blk, 0), 0)
        return pl.BlockSpec((yb, d), index)

    grid_spec = pltpu.PrefetchScalarGridSpec(
        num_scalar_prefetch=3,
        grid=(s // tm,),
        in_specs=[pl.BlockSpec((tm, d), lambda t, *_: (t, 0)),
                  pl.BlockSpec((tm, LANE), lambda t, *_: (t, 0)),
                  pl.BlockSpec((tm, LANE), lambda t, *_: (t, 0)),
                  pl.BlockSpec((1, d), lambda t, *_: (0, 0))]
        + [yspec(e, k) for e in range(N_EXPERTS) for k in range(MOE_Y_FETCH)],
        out_specs=pl.BlockSpec((tm, d), lambda t, *_: (t, 0)),
        scratch_shapes=[pltpu.VMEM((tm, d), F32)],
    )
    return pl.pallas_call(
        _moe_combine_kernel,
        grid_spec=grid_spec,
        out_shape=jax.ShapeDtypeStruct((s, d), F32),
        compiler_params=_params("arbitrary"),
        name="moe_combine",
    )(kb, off, lim, x, pos, gate, final_gain.reshape(1, d).astype(F32), *([y] * (MOE_Y_FETCH * N_EXPERTS)))


def _moe_schedule(before, total, s):
    tmb, yb, gr = MOE_ROW_BLOCK, MOE_Y_BLOCK, MOE_GATHER_ROWS
    nb = 2 * s // tmb + N_EXPERTS
    counts = total[0, :N_EXPERTS].astype(jnp.int32)
    nblk = (counts + tmb - 1) // tmb
    end = jnp.cumsum(nblk)
    first = end - nblk
    nvalid = end[-1]
    b = jnp.minimum(jnp.arange(nb, dtype=jnp.int32), nvalid - 1)
    blk_e = jnp.sum(b[:, None] >= end[None, :], axis=1).astype(jnp.int32)
    r0 = (b - first[blk_e]) * tmb
    cb = before[:, 0, :N_EXPERTS].astype(jnp.int32)
    r0s = (r0[:, None] + gr * jnp.arange(tmb // gr, dtype=jnp.int32)[None, :]).reshape(-1)
    cbe = cb[:, jnp.repeat(blk_e, tmb // gr)]
    tlo = (jnp.sum(cbe <= r0s[None, :], axis=0) - 1).astype(jnp.int32)
    thi = (jnp.sum(cbe < (r0s + gr)[None, :], axis=0) - 1).astype(jnp.int32)
    sched = (blk_e, r0.astype(jnp.int32), tlo, thi, nvalid.reshape(1).astype(jnp.int32))
    row_start = first[None, :] * tmb + cb
    kb = row_start // yb
    off = first[None, :] * tmb - kb * yb
    n_te = jnp.concatenate([cb[1:], counts[None, :]], axis=0) - cb
    lim = row_start - kb * yb + n_te
    flat = lambda a: a.reshape(-1).astype(jnp.int32)
    return sched, flat(kb), flat(off), flat(lim)


def moe_sparse(h, x, norm_gain, router, w_gate, w_up, w_down, final_gain):
    s, _ = x.shape
    pos, gate, post, before, total = moe_route(x, norm_gain, router)
    sched, kb, off, lim = _moe_schedule(before, total, s)
    y = moe_ffn(h, post, sched, w_gate, w_up, w_down)
    return moe_combine(x, pos, gate, y, kb, off, lim, final_gain)


FOX_F0 = 7 * BRANCH_W


def _split_w_in_kernel(w_ref, mix_ref, f_ref, gate_ref):
    w = w_ref[...]
    rows = w.shape[0]
    mix_ref[:, :FOX_F0] = w[:, :FOX_F0].astype(BF16)
    mix_ref[:, FOX_F0:] = w[:, FOX_F0 + N_HEADS:N_MIX_COLS + N_HEADS].astype(BF16)
    f_ref[...] = jnp.concatenate([w[:, FOX_F0:FOX_F0 + N_HEADS], jnp.zeros((rows, LANE - N_HEADS), F32)],
                                 axis=1).astype(BF16)
    gate_ref[...] = w[:, N_MIX_COLS + N_HEADS:].astype(BF16)


def _mixer_weights(w_in, layer, tr=128):
    _, d, cols = w_in.shape
    row = lambda w: pl.BlockSpec((tr, w), lambda i: (i, 0))
    return pl.pallas_call(
        _split_w_in_kernel,
        grid=(d // tr,),
        in_specs=[pl.BlockSpec((None, tr, cols), lambda i: (layer, i, 0))],
        out_specs=[row(N_MIX_COLS), row(LANE), row(N_BRANCH * D_MODEL)],
        out_shape=[jax.ShapeDtypeStruct((d, N_MIX_COLS), BF16), jax.ShapeDtypeStruct((d, LANE), BF16),
                   jax.ShapeDtypeStruct((d, N_BRANCH * D_MODEL), BF16)],
        compiler_params=_params("parallel"),
        name="split_w_in",
    )(w_in)


def kernel(x, w_in, w_branch, w_out, norm_mix_g, hgrn_lb_logits, hgrn_norm_g, fox_f_bias, pool_w, pool_scale,
           ret_gn_g, ret_gn_b, norm_ffn_g, ffn_w_gate, ffn_w_up, ffn_w_down, moe_router, moe_w_gate, moe_w_up,
           moe_w_down, final_norm_g):
    b, s, d = x.shape
    assert b == 1 and d == D_MODEL
    depth = w_in.shape[0]
    assert depth == 2, "layer 0 uses the dense FFN, layer 1 the experts and the final norm"
    xs = x.reshape(s, d)
    h = rmsnorm_bf16(xs, norm_mix_g[0])
    out = None
    for layer in range(depth):
        w_mix, w_f, w_gate = _mixer_weights(w_in, layer)
        proj = matmul(h, w_mix)
        ct, fox_stats = fox_gate(h, proj, w_f, fox_f_bias[layer])
        branches = (
            hgrn2(proj, hgrn_lb_logits, hgrn_norm_g[layer], layer),
            fox_attention(proj, ct, fox_stats),
            pool_mixer(proj, pool_w[layer], pool_scale[layer]),
            retention(proj, ret_gn_g[layer], ret_gn_b[layer]),
        )
        xs, h2 = merge(h, branches, xs, w_gate, w_branch[layer].astype(BF16), w_out[layer].astype(BF16),
                       norm_ffn_g[layer])
        if layer % 2 == 0:
            li = layer // 2
            xs, h = ffn_dense(h2, xs, ffn_w_gate[li], ffn_w_up[li], ffn_w_down[li], norm_mix_g[layer + 1])
        else:
            li = layer // 2
            out = moe_sparse(h2, xs, norm_ffn_g[layer], moe_router[li], moe_w_gate[li].astype(BF16),
                             moe_w_up[li].astype(BF16), moe_w_down[li].astype(BF16), final_norm_g)
    return out.reshape(b, s, d)
```

```python
import functools
import math

import jax
import jax.numpy as jnp
from jax import lax
from jax.experimental import pallas as pl
from jax.experimental.pallas import tpu as pltpu

D_MODEL = 1024
N_BRANCH = 4
BRANCH_W = D_MODEL // N_BRANCH
HEAD_DIM = 64
N_HEADS = BRANCH_W // HEAD_DIM
POOL_WINDOWS = (2, 4, 8, 16)
POOL_GROUP = BRANCH_W // len(POOL_WINDOWS)
POOL_HALO = 16
RET_DECAY_BASE = 5.0
ROPE_BASE = 10000.0
D_FF = 7 * D_MODEL // 2
N_EXPERTS = 8
RMS_EPS = 1e-6
LN_EPS = 1e-5
N_MIX_COLS = 12 * BRANCH_W

LANE = 128
SUBLANE = 8
VMEM_LIMIT = 56 * 1024 * 1024

HG_CHUNK = 64
HG_SUB = 16
HG_FAST_MIN_LOGDECAY = -60.0
RET_CHUNK = 256

F32 = jnp.float32
BF16 = jnp.bfloat16
NT_DIMS = (((1,), (1,)), ((), ()))


def _params(*sem):
    return pltpu.CompilerParams(dimension_semantics=sem, vmem_limit_bytes=VMEM_LIMIT)


def _const_spec(shape):
    nd = len(shape)
    return pl.BlockSpec(shape, lambda *_: (0,) * nd, pipeline_mode=pl.Buffered(1))


def _split3(x):
    hi = x.astype(BF16)
    r1 = x - hi.astype(F32)
    mid = r1.astype(BF16)
    lo = (r1 - mid.astype(F32)).astype(BF16)
    return hi, mid, lo


def _dot(a, b):
    return jnp.dot(a, b, preferred_element_type=F32)


def _dot_nt(a, b):
    return lax.dot_general(a, b, NT_DIMS, preferred_element_type=F32)


def _dot_exact_rhs(x, m_bf16):
    hi, mid, lo = _split3(x)
    return _dot(hi, m_bf16) + _dot(mid, m_bf16) + _dot(lo, m_bf16)


def _dot_exact_lhs(m_bf16, x):
    hi, mid, lo = _split3(x)
    return _dot(m_bf16, hi) + _dot(m_bf16, mid) + _dot(m_bf16, lo)


def _sigmoid(x):
    return 1.0 / (1.0 + jnp.exp(-x))


def _silu(x):
    return x * _sigmoid(x)


def _rms(x, gain):
    return x * lax.rsqrt(jnp.mean(x * x, axis=-1, keepdims=True) + RMS_EPS) * gain


def _head_of(n):
    return jnp.arange(n) // HEAD_DIM


def _head_ones():
    h = _head_of(BRANCH_W)
    return (h[:, None] == h[None, :]).astype(BF16)


def _head_masks():
    return (_head_of(BRANCH_W)[None, :] == jnp.arange(N_HEADS)[:, None]).astype(F32)


def _rmsnorm_kernel(x_ref, g_ref, o_ref):
    o_ref[...] = _rms(x_ref[...], g_ref[...]).astype(o_ref.dtype)


def rmsnorm_bf16(x, gain, tm=1024):
    s, d = x.shape
    tm = min(tm, s)
    return pl.pallas_call(
        _rmsnorm_kernel,
        grid=(s // tm,),
        in_specs=[pl.BlockSpec((tm, d), lambda i: (i, 0)), _const_spec((1, d))],
        out_specs=pl.BlockSpec((tm, d), lambda i: (i, 0)),
        out_shape=jax.ShapeDtypeStruct((s, d), BF16),
        compiler_params=_params("parallel"),
        name="rmsnorm",
    )(x, gain.reshape(1, d))


def _matmul_kernel(a_ref, b_ref, o_ref):
    o_ref[...] = _dot(a_ref[...], b_ref[...]).astype(o_ref.dtype)


def matmul(a, b, out_dtype=BF16, tm=2048, tn=1024):
    m, k = a.shape
    _, n = b.shape
    tm = min(tm, m)
    return pl.pallas_call(
        _matmul_kernel,
        grid=(n // tn, m // tm),
        in_specs=[pl.BlockSpec((tm, k), lambda j, i: (i, 0)),
                  pl.BlockSpec((k, tn), lambda j, i: (0, j))],
        out_specs=pl.BlockSpec((tm, tn), lambda j, i: (i, j)),
        out_shape=jax.ShapeDtypeStruct((m, n), out_dtype),
        compiler_params=_params("parallel", "parallel"),
        name="in_proj",
    )(a, b)


def _hgrn_kernel(q_ref, f_ref, i_ref, g_ref, lbl_ref, ng_ref, ones_ref, hm_ref, tril_ref, halfsum_ref, fmask_ref,
                 o_ref, st_ref, bpad, kpad, vpad, astack, lf_sc, kk_sc, o_sc, *, layer, tile):
    c, sub = HG_CHUNK, HG_SUB
    nsub = c // sub
    half = c // 2

    @pl.when(pl.program_id(0) == 0)
    def _():
        st_ref[...] = jnp.zeros_like(st_ref)
        bpad[...] = jnp.zeros_like(bpad)
        kpad[...] = jnp.zeros_like(kpad)
        vpad[...] = jnp.zeros_like(vpad)

    lbl = lbl_ref[...]
    e = jnp.exp(lbl - jnp.max(lbl, axis=0, keepdims=True))
    p = e / jnp.sum(e, axis=0, keepdims=True)
    lb = jnp.zeros((1, BRANCH_W), F32)
    for l in range(1, layer + 1):
        lb = lb + p[l:l + 1, :]

    ones_bd = ones_ref[...]
    hm = hm_ref[...]
    tril = tril_ref[...]
    row = lax.broadcasted_iota(jnp.int32, (c, 1), 0)
    row_in_sub = row % sub
    bd_mask = ones_bd.astype(F32)

    sig = _sigmoid(f_ref[...].astype(F32))
    logf_all = jnp.log(lb + (1.0 - lb) * sig)
    lf_sc[...] = _dot_exact_lhs(tril, logf_all)
    kk_sc[...] = (1.0 - lb) * (1.0 - sig)
    min_decay = jnp.min(_dot(halfsum_ref[...], logf_all.astype(BF16)))

    def load(ci):
        r0 = pl.multiple_of(ci * c, c)
        q = q_ref[pl.ds(r0, c), :].astype(F32)
        v = i_ref[pl.ds(r0, c), :].astype(F32)
        kk = kk_sc[pl.ds(r0, c), :]
        b = lf_sc[pl.ds(r0, c), :]
        return r0, q, v, kk, b

    def finish(r0, q_decayed, v, kk, b, intra):
        st = st_ref[...]
        inter = _dot_nt(q_decayed.astype(BF16), st.astype(BF16))
        b_last = b[c - 1:c, :]
        ks_end = (kk * jnp.exp(b_last - b)).astype(BF16)
        upd = _dot(v.T.astype(BF16), ks_end)
        st_ref[...] = st * jnp.exp(b_last) + upd * bd_mask
        o_sc[pl.ds(r0, c), :] = intra + inter

    def fast_chunk(ci, carry):
        r0, q, v, kk, b = load(ci)
        second = row >= half
        m_row = b[half - 1:half, :]
        mref = jnp.where(second, m_row, 0.0)
        qp = q * jnp.exp(b - mref)
        kp = kk * jnp.exp(mref - b)
        e_m = jnp.exp(m_row)
        kaug = jnp.concatenate([kp, kp[:half, :] * e_m], axis=0).astype(BF16)
        vaug = jnp.concatenate([v, v[:half, :]], axis=0).astype(BF16)
        qx = jnp.concatenate([qp * hm[h:h + 1, :] for h in range(N_HEADS)], axis=0).astype(BF16)
        sc = jnp.where(fmask_ref[...] > 0.0, _dot_nt(qx, kaug), 0.0)
        r = _dot(sc.astype(BF16), vaug)
        intra = jnp.zeros((c, BRANCH_W), F32)
        for h in range(N_HEADS):
            intra = intra + r[h * c:(h + 1) * c, :] * hm[h:h + 1, :]
        finish(r0, jnp.where(second, qp * e_m, qp), v, kk, b, intra)
        return carry

    def exact_chunk(ci, carry):
        r0, q, v, kk, b = load(ci)

        bpad[pl.ds(sub, c), :] = b
        kpad[pl.ds(sub, c), :] = kk
        vpad[pl.ds(sub, c), :] = v

        for d in range(sub):
            b_d = bpad[pl.ds(sub - d, c), :]
            k_d = kpad[pl.ds(sub - d, c), :]
            a = jnp.where(row_in_sub >= d, q * k_d * jnp.exp(b - b_d), 0.0)
            astack[pl.ds(d * c, c), :] = a.astype(BF16)
        pall = _dot(astack[...], ones_bd)
        intra = jnp.zeros((c, BRANCH_W), F32)
        for d in range(sub):
            intra = intra + pall[d * c:(d + 1) * c, :] * vpad[pl.ds(sub - d, c), :]

        pieces = [jnp.zeros((sub, BRANCH_W), F32)]
        for si in range(1, nsub):
            lo = si * sub
            m_i = b[lo - 1:lo, :]
            qs = q[lo:lo + sub, :] * jnp.exp(b[lo:lo + sub, :] - m_i)
            ks = (kk[:lo, :] * jnp.exp(m_i - b[:lo, :])).astype(BF16)
            qx = jnp.concatenate([qs * hm[h:h + 1, :] for h in range(N_HEADS)], axis=0).astype(BF16)
            sc = _dot_nt(qx, ks)
            r = _dot(sc.astype(BF16), v[:lo, :].astype(BF16))
            acc = jnp.zeros((sub, BRANCH_W), F32)
            for h in range(N_HEADS):
                acc = acc + r[h * sub:(h + 1) * sub, :] * hm[h:h + 1, :]
            pieces.append(acc)
        intra = intra + jnp.concatenate(pieces, axis=0)
        finish(r0, q * jnp.exp(b), v, kk, b, intra)
        return carry

    lax.cond(min_decay >= HG_FAST_MIN_LOGDECAY,
             lambda: lax.fori_loop(0, tile // c, fast_chunk, 0, unroll=2),
             lambda: lax.fori_loop(0, tile // c, exact_chunk, 0))

    o = o_sc[...]
    ms = _dot_exact_rhs(o * o, ones_bd) * (1.0 / HEAD_DIM)
    y = o * lax.rsqrt(ms + RMS_EPS) * ng_ref[...] * _silu(g_ref[...].astype(F32))
    o_ref[...] = y.astype(o_ref.dtype)


def hgrn2(proj, lb_logits, norm_g, layer, tile=512):
    s = proj.shape[0]
    depth = lb_logits.shape[0]
    c, sub = HG_CHUNK, HG_SUB
    half = c // 2
    tile = min(tile, s)
    col = lambda j: pl.BlockSpec((tile, BRANCH_W), lambda i, j=j: (i, j))
    pos = jnp.arange(tile)
    tril = ((pos[:, None] // c == pos[None, :] // c) & (pos[None, :] <= pos[:, None])).astype(BF16)
    nhalf = tile // half
    halfsum = (jnp.arange(tile)[None, :] // half == jnp.arange(nhalf)[:, None]).astype(BF16)
    t = jnp.arange(c)[:, None]
    col_s = jnp.arange(c + half)[None, :]
    same_half = (col_s < c) & (col_s // half == t // half) & (col_s <= t)
    cross = (col_s >= c) & (t >= half)
    fmask = jnp.tile((same_half | cross).astype(F32), (N_HEADS, 1))
    return pl.pallas_call(
        functools.partial(_hgrn_kernel, layer=layer, tile=tile),
        grid=(s // tile,),
        in_specs=[col(0), col(1), col(2), col(3),
                  _const_spec((depth, BRANCH_W)), _const_spec((1, BRANCH_W)),
                  _const_spec((BRANCH_W, BRANCH_W)), _const_spec((N_HEADS, BRANCH_W)),
                  _const_spec((tile, tile)), _const_spec((nhalf, tile)), _const_spec((N_HEADS * c, c + half))],
        out_specs=pl.BlockSpec((tile, BRANCH_W), lambda i: (i, 0)),
        out_shape=jax.ShapeDtypeStruct((s, BRANCH_W), BF16),
        scratch_shapes=[pltpu.VMEM((BRANCH_W, BRANCH_W), F32),
                        pltpu.VMEM((c + sub, BRANCH_W), F32),
                        pltpu.VMEM((c + sub, BRANCH_W), F32),
                        pltpu.VMEM((c + sub, BRANCH_W), F32),
                        pltpu.VMEM((sub * c, BRANCH_W), BF16),
                        pltpu.VMEM((tile, BRANCH_W), F32),
                        pltpu.VMEM((tile, BRANCH_W), F32),
                        pltpu.VMEM((tile, BRANCH_W), F32)],
        compiler_params=_params("arbitrary"),
        name="hgrn2",
    )(proj, proj, proj, proj, lb_logits.astype(F32), norm_g.reshape(1, BRANCH_W).astype(F32),
      _head_ones(), _head_masks(), tril, halfsum, fmask)


FOX_TILE = 256
FOX_GATE_BLOCKS = 2
FOX_NSTAT = 16
FOX_FIXED_MAX = 30.0
FOX_SKIP_LOG = 40.0


def _fox_gate_kernel(h_ref, q_ref, k_ref, wf_ref, bias_ref, tril_ref, ones_ref,
                     ct_ref, c_ref, stat_ref, carry_ref, kmax_ref):
    @pl.when(pl.program_id(0) == 0)
    def _():
        carry_ref[...] = jnp.zeros_like(carry_ref)
        kmax_ref[...] = jnp.zeros_like(kmax_ref)

    logit = _dot(h_ref[...], wf_ref[...]) + bias_ref[...]
    logf = jnp.minimum(logit, 0.0) - jnp.log(1.0 + jnp.exp(-jnp.abs(logit)))
    cum = _dot_exact_lhs(tril_ref[...], logf) + carry_ref[...]
    carry_ref[...] = cum[-1:, :]
    ct_ref[...] = cum.T[:SUBLANE, :]
    c_ref[...] = cum

    ones_bd = ones_ref[...]
    q = q_ref[...].astype(F32)
    k = k_ref[...].astype(F32)
    scale = HEAD_DIM ** -0.5
    head_lane = lax.broadcasted_iota(jnp.int32, (1, BRANCH_W), 1) // HEAD_DIM
    c_heads = jnp.zeros(q.shape, F32)
    for h in range(N_HEADS):
        c_heads = jnp.where(head_lane == h, cum[:, h:h + 1], c_heads)
    slack = 1.0 + 2.0 ** -6
    qn = jnp.sqrt(_dot((q * q).astype(BF16), ones_bd)) * (scale * slack)
    kn = jnp.sqrt(_dot((k * k).astype(BF16), ones_bd)) * slack
    diag = _dot((q * k).astype(BF16), ones_bd) * scale - (2.0 ** -6) * qn * kn
    e_row = c_heads - diag
    kmax = kmax_ref[...]
    for blk in range(q.shape[0] // FOX_TILE):
        rs = slice(blk * FOX_TILE, (blk + 1) * FOX_TILE)
        kmax = jnp.maximum(kmax, jnp.max(kn[rs], axis=0, keepdims=True))
        rows = [jnp.max(qn[rs], axis=0, keepdims=True),
                jnp.max(e_row[rs], axis=0, keepdims=True),
                kmax,
                c_heads[(blk + 1) * FOX_TILE - 1:(blk + 1) * FOX_TILE, :]]
        stat_ref[blk] = jnp.concatenate(rows + [jnp.zeros((SUBLANE - len(rows), BRANCH_W), F32)], axis=0)
    kmax_ref[...] = kmax


def fox_gate(h, proj, w_f, f_bias):
    s, d = h.shape
    tile = min(FOX_GATE_BLOCKS * FOX_TILE, s)
    nblk = tile // FOX_TILE
    bias = jnp.zeros((1, LANE), F32).at[0, :N_HEADS].set(f_bias.astype(F32))
    tril = jnp.tril(jnp.ones((tile, tile), F32)).astype(BF16)
    ct, c_rows, stats = pl.pallas_call(
        _fox_gate_kernel,
        grid=(s // tile,),
        in_specs=[pl.BlockSpec((tile, d), lambda i: (i, 0)),
                  pl.BlockSpec((tile, BRANCH_W), lambda i: (i, 4)),
                  pl.BlockSpec((tile, BRANCH_W), lambda i: (i, 5)),
                  _const_spec((d, LANE)), _const_spec((1, LANE)), _const_spec((tile, tile)),
                  _const_spec((BRANCH_W, BRANCH_W))],
        out_specs=[pl.BlockSpec((SUBLANE, tile), lambda i: (0, i)),
                   pl.BlockSpec((tile, LANE), lambda i: (i, 0)),
                   pl.BlockSpec((nblk, SUBLANE, BRANCH_W), lambda i: (i, 0, 0))],
        out_shape=[jax.ShapeDtypeStruct((SUBLANE, s), F32),
                   jax.ShapeDtypeStruct((s, LANE), F32),
                   jax.ShapeDtypeStruct((s // FOX_TILE, SUBLANE, BRANCH_W), F32)],
        scratch_shapes=[pltpu.VMEM((1, LANE), F32), pltpu.VMEM((1, BRANCH_W), F32)],
        compiler_params=_params("arbitrary"),
        name="fox_gate",
    )(h, proj, proj, w_f, bias, tril, _head_ones())
    return ct, c_rows, stats[:, :4, ::HEAD_DIM].reshape(-1)


def _fox_kernel(stat_ref, q_ref, k_ref, v_ref, ct_ref, c_ref, hm_ref, o_ref, m_sc, l_sc, acc_sc, *, tq):
    i = pl.program_id(0)
    q0 = pl.multiple_of(i * tq, tq)
    hm = hm_ref[...]
    q = q_ref[...].astype(F32) * (HEAD_DIM ** -0.5)
    qh = [(q * hm[h:h + 1, :]).astype(BF16) for h in range(N_HEADS)]
    c_q0 = ct_ref[:, pl.ds(q0, tq)][:, 0:1]

    first = i
    for h in range(N_HEADS):
        qmax = stat_ref[i * FOX_NSTAT + h]
        emax = stat_ref[i * FOX_NSTAT + N_HEADS + h]

        def needed(j, h=h, qmax=qmax, emax=emax):
            jc = jnp.maximum(j, 0)
            bound = (qmax * stat_ref[jc * FOX_NSTAT + 2 * N_HEADS + h] + emax
                     - stat_ref[jc * FOX_NSTAT + 3 * N_HEADS + h])
            return (j >= 0) & (bound >= -FOX_SKIP_LOG)

        last_dropped = lax.while_loop(needed, lambda j: j - 1, i - 1)
        first = jnp.minimum(first, last_dropped + 1)

    l_sc[...] = jnp.zeros_like(l_sc)
    acc_sc[...] = jnp.zeros_like(acc_sc)

    def causal(sc):
        r = lax.broadcasted_iota(jnp.int32, (tq, tq), 0)
        cidx = lax.broadcasted_iota(jnp.int32, (tq, tq), 1)
        return jnp.where(cidx <= r, sc, -jnp.inf)

    def online_block(s0, diagonal):
        kb = k_ref[pl.ds(s0, tq), :]
        vb = v_ref[pl.ds(s0, tq), :]
        bias = c_q0 - ct_ref[:, pl.ds(s0, tq)]
        for h in range(N_HEADS):
            sc = _dot_nt(qh[h], kb) + bias[h:h + 1, :]
            if diagonal:
                sc = causal(sc)
            m_prev = m_sc[h]
            m_new = jnp.maximum(m_prev, jnp.max(sc, axis=1, keepdims=True))
            alpha = jnp.exp(m_prev - m_new)
            p = jnp.exp(sc - jnp.tile(m_new, (1, tq // LANE)))
            l_sc[h] = alpha * l_sc[h] + jnp.sum(p, axis=1, keepdims=True)
            acc_sc[h] = acc_sc[h] * jnp.tile(alpha, (1, BRANCH_W // LANE)) + _dot(p.astype(BF16), vb)
            m_sc[h] = m_new

    tops = [stat_ref[i * FOX_NSTAT + h] * stat_ref[i * FOX_NSTAT + 2 * N_HEADS + h] for h in range(N_HEADS)]
    c_tile = c_ref[...]
    shift = [c_tile[:, h:h + 1] - c_q0[h:h + 1, :] - tops[h] for h in range(N_HEADS)]

    def fixed_block(s0, diagonal):
        kb = k_ref[pl.ds(s0, tq), :]
        vb = v_ref[pl.ds(s0, tq), :]
        bias = c_q0 - ct_ref[:, pl.ds(s0, tq)]
        for h in range(N_HEADS):
            sc = _dot_nt(qh[h], kb) + bias[h:h + 1, :] + shift[h]
            if diagonal:
                sc = causal(sc)
            p = jnp.exp(sc)
            l_sc[h] += p[:, :LANE] + p[:, LANE:]
            acc_sc[h] += _dot(p.astype(BF16), vb)

    def run(block, row_sum):
        lax.fori_loop(first, i, lambda j, carry: (block(pl.multiple_of(j * tq, tq), False), carry)[1], 0)
        block(q0, True)
        out = jnp.zeros((tq, BRANCH_W), F32)
        for h in range(N_HEADS):
            out = out + acc_sc[h] * hm[h:h + 1, :] / row_sum(l_sc[h])
        o_ref[...] = out.astype(o_ref.dtype)

    def run_online():
        m_sc[...] = jnp.full_like(m_sc, -jnp.inf)
        run(online_block, lambda l: jnp.tile(l, (1, BRANCH_W // LANE)))

    def run_fixed():
        run(fixed_block, lambda l: jnp.sum(l, axis=1, keepdims=True))

    lax.cond(functools.reduce(jnp.maximum, tops) <= FOX_FIXED_MAX, run_fixed, run_online)


def fox_attention(proj, ct, c_rows, stats):
    s = proj.shape[0]
    tq = min(FOX_TILE, s)
    full = lambda j: pl.BlockSpec((s, BRANCH_W), lambda i, j=j: (0, j), pipeline_mode=pl.Buffered(1))
    return pl.pallas_call(
        functools.partial(_fox_kernel, tq=tq),
        grid=(s // tq,),
        in_specs=[pl.BlockSpec(memory_space=pltpu.SMEM),
                  pl.BlockSpec((tq, BRANCH_W), lambda i: (i, 4)), full(5), full(6),
                  _const_spec((SUBLANE, s)), pl.BlockSpec((tq, LANE), lambda i: (i, 0)),
                  _const_spec((N_HEADS, BRANCH_W))],
        out_specs=pl.BlockSpec((tq, BRANCH_W), lambda i: (i, 0)),
        out_shape=jax.ShapeDtypeStruct((s, BRANCH_W), BF16),
        scratch_shapes=[pltpu.VMEM((N_HEADS, tq, LANE), F32),
                        pltpu.VMEM((N_HEADS, tq, LANE), F32),
                        pltpu.VMEM((N_HEADS, tq, BRANCH_W), F32)],
        compiler_params=_params("parallel"),
        name="fox_attention",
    )(stats, proj, proj, proj, ct, c_rows, _head_masks())


def _pool_kernel(u_ref, w_ref, scale_ref, o_ref, ext, *, tile):
    i = pl.program_id(0)

    @pl.when(i == 0)
    def _():
        ext[pl.ds(0, POOL_HALO), :] = jnp.zeros((POOL_HALO, BRANCH_W), F32)

    u = u_ref[...].astype(F32)
    ext[pl.ds(POOL_HALO, tile), :] = u
    pos = (i * tile + lax.broadcasted_iota(jnp.int32, (tile, 1), 0) + 1).astype(F32)
    halves = []
    for half in range(BRANCH_W // LANE):
        lanes = pl.ds(half * LANE, LANE)
        w_small, w_big = POOL_WINDOWS[2 * half], POOL_WINDOWS[2 * half + 1]
        run = u[:, half * LANE:(half + 1) * LANE]
        sums = {}
        for j in range(1, w_big):
            if j == w_small:
                sums[w_small] = run
            run = run + ext[pl.ds(POOL_HALO - j, tile), lanes]
        sums[w_big] = run
        lane = lax.broadcasted_iota(jnp.int32, (1, LANE), 1)
        small = lane < POOL_GROUP
        total = jnp.where(small, sums[w_small], sums[w_big])
        count = jnp.where(small, jnp.minimum(pos, float(w_small)), jnp.minimum(pos, float(w_big)))
        halves.append(total / count)
    mean = jnp.concatenate(halves, axis=1)
    d = (mean - u).astype(BF16)
    y = _dot(d, w_ref[...]) * scale_ref[...]
    o_ref[...] = y.astype(o_ref.dtype)
    ext[pl.ds(0, POOL_HALO), :] = u[tile - POOL_HALO:, :]


def pool_mixer(proj, w_pool, scale, tile=512):
    s = proj.shape[0]
    tile = min(tile, s)
    ng = len(POOL_WINDOWS)
    w_bd = jnp.zeros((BRANCH_W, BRANCH_W), F32)
    for gi in range(ng):
        lo = gi * POOL_GROUP
        w_bd = w_bd.at[lo:lo + POOL_GROUP, lo:lo + POOL_GROUP].set(w_pool[gi].astype(F32))
    return pl.pallas_call(
        functools.partial(_pool_kernel, tile=tile),
        grid=(s // tile,),
        in_specs=[pl.BlockSpec((tile, BRANCH_W), lambda i: (i, 7)),
                  _const_spec((BRANCH_W, BRANCH_W)), _const_spec((1, BRANCH_W))],
        out_specs=pl.BlockSpec((tile, BRANCH_W), lambda i: (i, 0)),
        out_shape=jax.ShapeDtypeStruct((s, BRANCH_W), BF16),
        scratch_shapes=[pltpu.VMEM((tile + POOL_HALO, BRANCH_W), F32)],
        compiler_params=_params("arbitrary"),
        name="pool_mixer",
    )(proj, w_bd.astype(BF16), scale.reshape(1, BRANCH_W).astype(F32))


def _ret_kernel(q_ref, k_ref, v_ref, g_ref, rope_ref, ecos_ref, esin_ref, perm_ref, ones_ref, hm_ref,
                dstack_ref, xi_ref, zeta_ref, gc_ref, gng_ref, gnb_ref, o_ref, st_ref, *, tile):
    c = RET_CHUNK

    @pl.when(pl.program_id(0) == 0)
    def _():
        st_ref[...] = jnp.zeros_like(st_ref)

    perm = perm_ref[...]
    ones_bd = ones_ref[...]
    bd_mask = ones_bd.astype(F32)
    hm = hm_ref[...]

    tab = rope_ref[...]
    cos = _dot_exact_rhs(tab, ecos_ref[...])
    sin = _dot_exact_rhs(tab, esin_ref[...])
    q_all = q_ref[...]
    k_all = k_ref[...]
    qr_all = q_all.astype(F32) * cos + _dot(q_all, perm) * sin
    kr_all = (k_all.astype(F32) * cos + _dot(k_all, perm) * sin) * (HEAD_DIM ** -0.5)

    outs = []
    for ci in range(tile // c):
        r0 = ci * c
        qr = qr_all[r0:r0 + c, :]
        kr = kr_all[r0:r0 + c, :]
        v = v_ref[pl.ds(r0, c), :]

        qx = jnp.concatenate([qr * hm[h:h + 1, :] for h in range(N_HEADS)], axis=0).astype(BF16)
        sc = _dot_nt(qx, kr.astype(BF16)) * dstack_ref[...]
        r = _dot(sc.astype(BF16), v)
        intra = jnp.zeros((c, BRANCH_W), F32)
        for h in range(N_HEADS):
            intra = intra + r[h * c:(h + 1) * c, :] * hm[h:h + 1, :]

        st = st_ref[...]
        inter = _dot_nt((qr * xi_ref[...]).astype(BF16), st.astype(BF16))
        upd = _dot(v.astype(F32).T.astype(BF16), (kr * zeta_ref[...]).astype(BF16))
        st_ref[...] = st * gc_ref[...] + upd * bd_mask
        outs.append(intra + inter)

    o = jnp.concatenate(outs, axis=0)
    mu = _dot_exact_rhs(o, ones_bd) * (1.0 / HEAD_DIM)
    cen = o - mu
    var = _dot_exact_rhs(cen * cen, ones_bd) * (1.0 / HEAD_DIM)
    y = cen * lax.rsqrt(var + LN_EPS) * gng_ref[...] + gnb_ref[...]
    o_ref[...] = (y * _silu(g_ref[...].astype(F32))).astype(o_ref.dtype)


def _rope_tables(s):
    half = HEAD_DIM // 2
    pos = jnp.arange(s, dtype=F32)
    inv_freq = ROPE_BASE ** (-jnp.arange(half, dtype=F32) / half)
    ang = pos[:, None] * inv_freq[None, :]
    table = jnp.concatenate([jnp.cos(ang), jnp.sin(ang), jnp.zeros((s, LANE - 2 * half), F32)], axis=1)
    lane = jnp.arange(BRANCH_W)
    src = jnp.arange(LANE)[:, None]
    ecos = (src == (lane % half)[None, :]).astype(F32)
    sign = jnp.where(lane % HEAD_DIM < half, -1.0, 1.0)
    esin = (src == (half + lane % half)[None, :]).astype(F32) * sign[None, :]
    return table, ecos.astype(BF16), esin.astype(BF16)


def _ret_constants():
    c = RET_CHUNK
    half = HEAD_DIM // 2
    lane = jnp.arange(BRANCH_W)
    partner = jnp.where(lane % HEAD_DIM < half, lane + half, lane - half)
    perm = (lane[:, None] == partner[None, :]).astype(BF16)
    log_gamma = jnp.log1p(-jnp.exp2(-RET_DECAY_BASE - jnp.arange(N_HEADS, dtype=F32)))
    ci = jnp.arange(c, dtype=F32)
    diff = ci[:, None] - ci[None, :]
    intra = jnp.where(diff >= 0, jnp.exp(diff * log_gamma[:, None, None]), 0.0)
    dstack = intra.reshape(N_HEADS * c, c)
    lg_lane = jnp.repeat(log_gamma, HEAD_DIM)[None, :]
    xi = jnp.exp((ci[:, None] + 1.0) * lg_lane)
    zeta = jnp.exp((c - 1.0 - ci[:, None]) * lg_lane)
    gc = jnp.exp(c * lg_lane)
    return perm, dstack, xi, zeta, gc


def retention(proj, gn_g, gn_b, tile=512):
    s = proj.shape[0]
    c = RET_CHUNK
    tile = min(tile, s)
    rope, ecos, esin = _rope_tables(s)
    perm, dstack, xi, zeta, gc = _ret_constants()
    col = lambda j: pl.BlockSpec((tile, BRANCH_W), lambda i, j=j: (i, j))
    row = pl.BlockSpec((tile, BRANCH_W), lambda i: (i, 0))
    return pl.pallas_call(
        functools.partial(_ret_kernel, tile=tile),
        grid=(s // tile,),
        in_specs=[col(8), col(9), col(10), col(11), pl.BlockSpec((tile, LANE), lambda i: (i, 0)),
                  _const_spec((LANE, BRANCH_W)), _const_spec((LANE, BRANCH_W)),
                  _const_spec((BRANCH_W, BRANCH_W)), _const_spec((BRANCH_W, BRANCH_W)),
                  _const_spec((N_HEADS, BRANCH_W)), _const_spec((N_HEADS * c, c)),
                  _const_spec((c, BRANCH_W)), _const_spec((c, BRANCH_W)), _const_spec((1, BRANCH_W)),
                  _const_spec((1, BRANCH_W)), _const_spec((1, BRANCH_W))],
        out_specs=row,
        out_shape=jax.ShapeDtypeStruct((s, BRANCH_W), BF16),
        scratch_shapes=[pltpu.VMEM((BRANCH_W, BRANCH_W), F32)],
        compiler_params=_params("arbitrary"),
        name="retention",
    )(proj, proj, proj, proj, rope, ecos, esin, perm, _head_ones(), _head_masks(), dstack, xi, zeta, gc,
      gn_g.reshape(1, BRANCH_W).astype(F32), gn_b.reshape(1, BRANCH_W).astype(F32))


def _merge_kernel(h_ref, o0_ref, o1_ref, o2_ref, o3_ref, x_ref, wg_ref, wb_ref, wo_ref, g_ref,
                  xo_ref, ho_ref):
    h = h_ref[...]
    merged = jnp.zeros(x_ref.shape, F32)
    for bi, o_ref in enumerate((o0_ref, o1_ref, o2_ref, o3_ref)):
        gate = _sigmoid(_dot(h, wg_ref[:, bi * D_MODEL:(bi + 1) * D_MODEL]))
        merged = merged + gate * _dot(o_ref[...], wb_ref[bi])
    x_new = x_ref[...] + _dot(merged.astype(BF16), wo_ref[...])
    xo_ref[...] = x_new
    ho_ref[...] = _rms(x_new, g_ref[...]).astype(ho_ref.dtype)


def merge(h, branches, x, w_gate, w_branch, w_out, next_gain, tm=512):
    s, d = x.shape
    tm = min(tm, s)
    row = lambda w: pl.BlockSpec((tm, w), lambda i: (i, 0))
    return pl.pallas_call(
        _merge_kernel,
        grid=(s // tm,),
        in_specs=[row(d), row(BRANCH_W), row(BRANCH_W), row(BRANCH_W), row(BRANCH_W), row(d),
                  _const_spec((d, N_BRANCH * d)), _const_spec((N_BRANCH, BRANCH_W, d)),
                  _const_spec((d, d)), _const_spec((1, d))],
        out_specs=[row(d), row(d)],
        out_shape=[jax.ShapeDtypeStruct((s, d), F32), jax.ShapeDtypeStruct((s, d), BF16)],
        compiler_params=_params("parallel"),
        name="merge",
    )(h, *branches, x, w_gate, w_branch, w_out, next_gain.reshape(1, d).astype(F32))


def _ffn_kernel(h_ref, x_ref, wg_ref, wu_ref, wd_ref, g_ref, xo_ref, ho_ref, acc_ref):
    f = pl.program_id(1)

    @pl.when(f == 0)
    def _():
        acc_ref[...] = jnp.zeros_like(acc_ref)

    h = h_ref[...]
    a = _silu(_dot(h, wg_ref[...].astype(BF16))) * _dot(h, wu_ref[...].astype(BF16))
    acc_ref[...] += _dot(a.astype(BF16), wd_ref[...].astype(BF16))

    @pl.when(f == pl.num_programs(1) - 1)
    def _():
        x_new = x_ref[...] + acc_ref[...]
        xo_ref[...] = x_new
        ho_ref[...] = _rms(x_new, g_ref[...]).astype(ho_ref.dtype)


def ffn_dense(h, x, w_gate, w_up, w_down, next_gain, tm=1024, tf=512):
    s, d = x.shape
    tm = min(tm, s)
    dff = w_gate.shape[1]
    row = lambda: pl.BlockSpec((tm, d), lambda i, f: (i, 0))
    return pl.pallas_call(
        _ffn_kernel,
        grid=(s // tm, dff // tf),
        in_specs=[row(), row(),
                  pl.BlockSpec((d, tf), lambda i, f: (0, f)),
                  pl.BlockSpec((d, tf), lambda i, f: (0, f)),
                  pl.BlockSpec((tf, d), lambda i, f: (f, 0)),
                  _const_spec((1, d))],
        out_specs=[row(), row()],
        out_shape=[jax.ShapeDtypeStruct((s, d), F32), jax.ShapeDtypeStruct((s, d), BF16)],
        scratch_shapes=[pltpu.VMEM((tm, d), F32)],
        compiler_params=_params("parallel", "arbitrary"),
        name="ffn_dense",
    )(h, x, w_gate, w_up, w_down, next_gain.reshape(1, d).astype(F32))


MOE_TOK_TILE = 256
MOE_ROW_BLOCK = 512
MOE_GATHER_ROWS = 128
MOE_GATHER_TILES = 4
MOE_VMEM_LIMIT = 60 * 1024 * 1024
MOE_Y_BLOCK = 128
MOE_Y_FETCH = MOE_TOK_TILE // MOE_Y_BLOCK + 1


def _dot_f32(x, w):
    xh, xm, _ = _split3(x)
    wh, wm, _ = _split3(w)
    return _dot(xh, wh) + (_dot(xh, wm) + _dot(xm, wh))


def _route_kernel(x_ref, ng_ref, router_ref, ltri_ref, pos_ref, gate_ref, post_ref, before_ref, total_ref,
                  carry_ref):
    tm = x_ref.shape[0]

    @pl.when(pl.program_id(0) == 0)
    def _():
        carry_ref[...] = jnp.zeros_like(carry_ref)

    hn = _rms(x_ref[...], ng_ref[...])
    logits = _dot_f32(hn, router_ref[...])
    lane = lax.broadcasted_iota(jnp.int32, (tm, LANE), 1)
    logits = jnp.where(lane < N_EXPERTS, logits, -jnp.inf)
    v1 = jnp.max(logits, axis=1, keepdims=True)
    i1 = jnp.min(jnp.where(logits == v1, lane, LANE), axis=1, keepdims=True)
    rest = jnp.where(lane == i1, -jnp.inf, logits)
    v2 = jnp.max(rest, axis=1, keepdims=True)
    i2 = jnp.min(jnp.where(rest == v2, lane, LANE), axis=1, keepdims=True)
    w1 = 1.0 / (1.0 + jnp.exp(v2 - v1))
    gate_ref[...] = jnp.where(lane == i1, w1, 0.0) + jnp.where(lane == i2, 1.0 - w1, 0.0)

    member = jnp.where((lane == i1) | (lane == i2), 1.0, 0.0)
    carry = carry_ref[...]
    rank = _dot(ltri_ref[...], member.astype(BF16)) + carry
    pos = jnp.where(member > 0.0, rank, -1.0)
    pos_ref[...] = pos
    post_ref[...] = pos.T[:SUBLANE, :]
    before_ref[0] = carry
    carry = carry + jnp.sum(member, axis=0, keepdims=True)
    carry_ref[...] = carry
    total_ref[...] = carry


def moe_route(x, norm_gain, router):
    s, d = x.shape
    tm = min(MOE_TOK_TILE, s)
    nt = s // tm
    router_p = jnp.zeros((d, LANE), F32).at[:, :N_EXPERTS].set(router.astype(F32))
    ltri = jnp.tril(jnp.ones((tm, tm), F32), -1).astype(BF16)
    row = pl.BlockSpec((tm, LANE), lambda i: (i, 0))
    return pl.pallas_call(
        _route_kernel,
        grid=(nt,),
        in_specs=[pl.BlockSpec((tm, d), lambda i: (i, 0)), _const_spec((1, d)), _const_spec((d, LANE)),
                  _const_spec((tm, tm))],
        out_specs=[row, row, pl.BlockSpec((SUBLANE, tm), lambda i: (0, i)),
                   pl.BlockSpec((1, 1, LANE), lambda i: (i, 0, 0)), pl.BlockSpec((1, LANE), lambda i: (0, 0))],
        out_shape=[jax.ShapeDtypeStruct((s, LANE), F32), jax.ShapeDtypeStruct((s, LANE), F32),
                   jax.ShapeDtypeStruct((SUBLANE, s), F32), jax.ShapeDtypeStruct((nt, 1, LANE), F32),
                   jax.ShapeDtypeStruct((1, LANE), F32)],
        scratch_shapes=[pltpu.VMEM((1, LANE), F32)],
        compiler_params=_params("arbitrary"),
        name="moe_route",
    )(x, norm_gain.reshape(1, d).astype(F32), router_p, ltri)


def _moe_ffn_kernel(be_ref, r0_ref, tlo_ref, thi_ref, nv_ref, h_ref, post_ref, wg_ref, wu_ref, wd_ref,
                    y_ref, x_sc, acc_sc, *, tt, nsub):
    b = pl.program_id(0)
    f = pl.program_id(1)
    nb = pl.num_programs(0)
    valid = b < nv_ref[0]
    tmb, d = acc_sc.shape
    gr = tmb // nsub
    last_tile = h_ref.shape[0] // tt - 1

    def picked(e, want, t):
        t0 = pl.multiple_of(t * tt, tt)
        p = post_ref[pl.ds(e, 1), pl.ds(t0, tt)]
        sel = jnp.where(p == want, 1.0, 0.0).astype(BF16)
        return _dot(sel, h_ref[pl.ds(t0, tt), :])

    def gather_head(blk, sb):
        e = be_ref[blk]
        lo = tlo_ref[blk * nsub + sb]
        hi = thi_ref[blk * nsub + sb]
        want = (lax.broadcasted_iota(jnp.int32, (gr, 1), 0) + (r0_ref[blk] + sb * gr)).astype(F32)
        rows = picked(e, want, lo)
        for k in range(1, MOE_GATHER_TILES):
            rows = rows + picked(e, jnp.where(lo + k <= hi, want, -2.0), jnp.minimum(lo + k, last_tile))
        return rows.astype(BF16)

    def gather_tail(slot, blk, sb):
        e = be_ref[blk]
        want = (lax.broadcasted_iota(jnp.int32, (gr, 1), 0) + (r0_ref[blk] + sb * gr)).astype(F32)
        rows = pl.ds(pl.multiple_of(sb * gr, gr), gr)

        def more(t, carry):
            x_sc[slot, rows, :] = (x_sc[slot, rows, :].astype(F32) + picked(e, want, t)).astype(BF16)
            return carry

        lax.fori_loop(tlo_ref[blk * nsub + sb] + MOE_GATHER_TILES, thi_ref[blk * nsub + sb] + 1, more, 0)

    @pl.when((b == 0) & (f == 0))
    def _():
        for sb in range(nsub):
            x_sc[0, pl.ds(sb * gr, gr), :] = gather_head(0, sb)
            gather_tail(0, 0, sb)

    nxt = jnp.minimum(b + 1, nb - 1)
    nslot = (b + 1) % 2

    @pl.when(valid)
    def _():
        x_sc[nslot, pl.ds(pl.multiple_of(f * gr, gr), gr), :] = gather_head(nxt, f)
        xb = x_sc[b % 2]
        a = _silu(_dot(xb, wg_ref[0])) * _dot(xb, wu_ref[0])
        acc_sc[...] = jnp.where(f == 0, 0.0, acc_sc[...]) + _dot(a.astype(BF16), wd_ref[0])

    @pl.when(valid & (thi_ref[nxt * nsub + f] - tlo_ref[nxt * nsub + f] >= MOE_GATHER_TILES))
    def _():
        gather_tail(nslot, nxt, f)

    @pl.when(f == nsub - 1)
    def _():
        y_ref[...] = jnp.where(valid, acc_sc[...], 0.0).astype(y_ref.dtype)


def moe_ffn(h, post, sched, w_gate, w_up, w_down):
    s, d = h.shape
    ne, _, dff = w_gate.shape
    tmb = MOE_ROW_BLOCK
    tt = min(MOE_TOK_TILE, s)
    nb = sched[0].shape[0]
    nf = tmb // MOE_GATHER_ROWS
    tf = dff // nf

    def fidx(b, f, nv):
        return jnp.where(b < nv[0], f, nf - 1)

    grid_spec = pltpu.PrefetchScalarGridSpec(
        num_scalar_prefetch=5,
        grid=(nb, nf),
        in_specs=[pl.BlockSpec((s, d), lambda b, f, *_: (0, 0), pipeline_mode=pl.Buffered(1)),
                  pl.BlockSpec((SUBLANE, s), lambda b, f, *_: (0, 0), pipeline_mode=pl.Buffered(1)),
                  pl.BlockSpec((1, d, tf), lambda b, f, be, r0, tlo, thi, nv: (be[b], 0, fidx(b, f, nv))),
                  pl.BlockSpec((1, d, tf), lambda b, f, be, r0, tlo, thi, nv: (be[b], 0, fidx(b, f, nv))),
                  pl.BlockSpec((1, tf, d), lambda b, f, be, r0, tlo, thi, nv: (be[b], fidx(b, f, nv), 0))],
        out_specs=pl.BlockSpec((tmb, d), lambda b, f, *_: (b, 0)),
        scratch_shapes=[pltpu.VMEM((2, tmb, d), BF16), pltpu.VMEM((tmb, d), F32)],
    )
    return pl.pallas_call(
        functools.partial(_moe_ffn_kernel, tt=tt, nsub=nf),
        grid_spec=grid_spec,
        out_shape=jax.ShapeDtypeStruct((nb * tmb, d), BF16),
        compiler_params=pltpu.CompilerParams(dimension_semantics=("arbitrary", "arbitrary"),
                                             vmem_limit_bytes=MOE_VMEM_LIMIT),
        name="moe_ffn",
    )(*sched, h, post, w_gate, w_up, w_down)


def _moe_combine_kernel(kb_ref, off_ref, lim_ref, x_ref, pos_ref, gate_ref, fg_ref, *rest):
    y_refs, o_ref, acc_sc = rest[:-2], rest[-2], rest[-1]
    t = pl.program_id(0)
    tm = x_ref.shape[0]
    yb = y_refs[0].shape[0]
    pos = pos_ref[...]
    gate = gate_ref[...]
    lane = lax.broadcasted_iota(jnp.int32, (tm, LANE), 1)
    col = lax.broadcasted_iota(jnp.int32, (1, yb), 1).astype(F32)
    def routed(e):
        pe = jnp.sum(jnp.where(lane == e, pos, 0.0), axis=1, keepdims=True)
        ge = jnp.sum(jnp.where(lane == e, gate, 0.0), axis=1, keepdims=True)
        r = jnp.where(pe >= 0.0, pe + off_ref[t * N_EXPERTS + e].astype(F32), -1.0)
        return r, ge

    def picked(e, k, r):
        sel = jnp.where(r == col + float(k * yb), 1.0, 0.0).astype(BF16)
        return _dot(sel, y_refs[MOE_Y_FETCH * e + k][...])

    col2 = lax.broadcasted_iota(jnp.int32, (1, 2 * yb), 1).astype(F32)
    acc = x_ref[...]
    for e in range(N_EXPERTS):
        r, ge = routed(e)
        pair = jnp.concatenate([y_refs[MOE_Y_FETCH * e][...], y_refs[MOE_Y_FETCH * e + 1][...]], axis=0)
        acc = acc + ge * _dot(jnp.where(r == col2, 1.0, 0.0).astype(BF16), pair)
    acc_sc[...] = acc
    for e in range(N_EXPERTS):
        for k in range(2, MOE_Y_FETCH):
            @pl.when(lim_ref[t * N_EXPERTS + e] > k * yb)
            def _(k=k, e=e):
                r, ge = routed(e)
                acc_sc[...] += ge * picked(e, k, r)
    o_ref[...] = _rms(acc_sc[...], fg_ref[...]).astype(o_ref.dtype)


def moe_combine(x, pos, gate, y, kb, off, lim, final_gain):
    s, d = x.shape
    tm = min(MOE_TOK_TILE, s)
    yb = MOE_Y_BLOCK
    last = y.shape[0] // yb - 1

    def yspec(e, k):
        def index(t, kb_r, off_r, lim_r):
            blk = jnp.minimum(kb_r[t * N_EXPERTS + e] + k, last)
            return (blk if k < 1 else jnp.where(lim_r[t * N_EXPERTS + e] > k * yb, blk, 0), 0)
        return pl.BlockSpec((yb, d), index)

    grid_spec = pltpu.PrefetchScalarGridSpec(
        num_scalar_prefetch=3,
        grid=(s // tm,),
        in_specs=[pl.BlockSpec((tm, d), lambda t, *_: (t, 0)),
                  pl.BlockSpec((tm, LANE), lambda t, *_: (t, 0)),
                  pl.BlockSpec((tm, LANE), lambda t, *_: (t, 0)),
                  pl.BlockSpec((1, d), lambda t, *_: (0, 0))]
        + [yspec(e, k) for e in range(N_EXPERTS) for k in range(MOE_Y_FETCH)],
        out_specs=pl.BlockSpec((tm, d), lambda t, *_: (t, 0)),
        scratch_shapes=[pltpu.VMEM((tm, d), F32)],
    )
    return pl.pallas_call(
        _moe_combine_kernel,
        grid_spec=grid_spec,
        out_shape=jax.ShapeDtypeStruct((s, d), F32),
        compiler_params=_params("arbitrary"),
        name="moe_combine",
    )(kb, off, lim, x, pos, gate, final_gain.reshape(1, d).astype(F32), *([y] * (MOE_Y_FETCH * N_EXPERTS)))


def _moe_schedule(before, total, s):
    tmb, yb, gr = MOE_ROW_BLOCK, MOE_Y_BLOCK, MOE_GATHER_ROWS
    nb = 2 * s // tmb + N_EXPERTS
    counts = total[0, :N_EXPERTS].astype(jnp.int32)
    nblk = (counts + tmb - 1) // tmb
    end = jnp.cumsum(nblk)
    first = end - nblk
    nvalid = end[-1]
    b = jnp.minimum(jnp.arange(nb, dtype=jnp.int32), nvalid - 1)
    blk_e = jnp.sum(b[:, None] >= end[None, :], axis=1).astype(jnp.int32)
    r0 = (b - first[blk_e]) * tmb
    cb = before[:, 0, :N_EXPERTS].astype(jnp.int32)
    r0s = (r0[:, None] + gr * jnp.arange(tmb // gr, dtype=jnp.int32)[None, :]).reshape(-1)
    cbe = cb[:, jnp.repeat(blk_e, tmb // gr)]
    tlo = (jnp.sum(cbe <= r0s[None, :], axis=0) - 1).astype(jnp.int32)
    thi = (jnp.sum(cbe < (r0s + gr)[None, :], axis=0) - 1).astype(jnp.int32)
    sched = (blk_e, r0.astype(jnp.int32), tlo, thi, nvalid.reshape(1).astype(jnp.int32))
    row_start = first[None, :] * tmb + cb
    kb = row_start // yb
    off = first[None, :] * tmb - kb * yb
    n_te = jnp.concatenate([cb[1:], counts[None, :]], axis=0) - cb
    lim = row_start - kb * yb + n_te
    flat = lambda a: a.reshape(-1).astype(jnp.int32)
    return sched, flat(kb), flat(off), flat(lim)


def moe_sparse(h, x, norm_gain, router, w_gate, w_up, w_down, final_gain):
    s, _ = x.shape
    pos, gate, post, before, total = moe_route(x, norm_gain, router)
    sched, kb, off, lim = _moe_schedule(before, total, s)
    y = moe_ffn(h, post, sched, w_gate, w_up, w_down)
    return moe_combine(x, pos, gate, y, kb, off, lim, final_gain)


FOX_F0 = 7 * BRANCH_W


def _split_w_in_kernel(w_ref, mix_ref, f_ref, gate_ref):
    w = w_ref[...]
    rows = w.shape[0]
    mix_ref[:, :FOX_F0] = w[:, :FOX_F0].astype(BF16)
    mix_ref[:, FOX_F0:] = w[:, FOX_F0 + N_HEADS:N_MIX_COLS + N_HEADS].astype(BF16)
    f_ref[...] = jnp.concatenate([w[:, FOX_F0:FOX_F0 + N_HEADS], jnp.zeros((rows, LANE - N_HEADS), F32)],
                                 axis=1).astype(BF16)
    gate_ref[...] = w[:, N_MIX_COLS + N_HEADS:].astype(BF16)


def _mixer_weights(w_in, layer, tr=128):
    _, d, cols = w_in.shape
    row = lambda w: pl.BlockSpec((tr, w), lambda i: (i, 0))
    return pl.pallas_call(
        _split_w_in_kernel,
        grid=(d // tr,),
        in_specs=[pl.BlockSpec((None, tr, cols), lambda i: (layer, i, 0))],
        out_specs=[row(N_MIX_COLS), row(LANE), row(N_BRANCH * D_MODEL)],
        out_shape=[jax.ShapeDtypeStruct((d, N_MIX_COLS), BF16), jax.ShapeDtypeStruct((d, LANE), BF16),
                   jax.ShapeDtypeStruct((d, N_BRANCH * D_MODEL), BF16)],
        compiler_params=_params("parallel"),
        name="split_w_in",
    )(w_in)


def kernel(x, w_in, w_branch, w_out, norm_mix_g, hgrn_lb_logits, hgrn_norm_g, fox_f_bias, pool_w, pool_scale,
           ret_gn_g, ret_gn_b, norm_ffn_g, ffn_w_gate, ffn_w_up, ffn_w_down, moe_router, moe_w_gate, moe_w_up,
           moe_w_down, final_norm_g):
    b, s, d = x.shape
    assert b == 1 and d == D_MODEL
    depth = w_in.shape[0]
    assert depth == 2, "layer 0 uses the dense FFN, layer 1 the experts and the final norm"
    xs = x.reshape(s, d)
    h = rmsnorm_bf16(xs, norm_mix_g[0])
    out = None
    for layer in range(depth):
        w_mix, w_f, w_gate = _mixer_weights(w_in, layer)
        proj = matmul(h, w_mix)
        ct, c_rows, fox_stats = fox_gate(h, proj, w_f, fox_f_bias[layer])
        branches = (
            hgrn2(proj, hgrn_lb_logits, hgrn_norm_g[layer], layer),
            fox_attention(proj, ct, c_rows, fox_stats),
            pool_mixer(proj, pool_w[layer], pool_scale[layer]),
            retention(proj, ret_gn_g[layer], ret_gn_b[layer]),
        )
        xs, h2 = merge(h, branches, xs, w_gate, w_branch[layer].astype(BF16), w_out[layer].astype(BF16),
                       norm_ffn_g[layer])
        if layer % 2 == 0:
            li = layer // 2
            xs, h = ffn_dense(h2, xs, ffn_w_gate[li], ffn_w_up[li], ffn_w_down[li], norm_mix_g[layer + 1])
        else:
            li = layer // 2
            out = moe_sparse(h2, xs, norm_ffn_g[layer], moe_router[li], moe_w_gate[li].astype(BF16),
                             moe_w_up[li].astype(BF16), moe_w_down[li].astype(BF16), final_norm_g)
    return out.reshape(b, s, d)
```

```python
import functools
import math

import jax
import jax.numpy as jnp
from jax import lax
from jax.experimental import pallas as pl
from jax.experimental.pallas import tpu as pltpu

D_MODEL = 1024
N_BRANCH = 4
BRANCH_W = D_MODEL // N_BRANCH
HEAD_DIM = 64
N_HEADS = BRANCH_W // HEAD_DIM
POOL_WINDOWS = (2, 4, 8, 16)
POOL_GROUP = BRANCH_W // len(POOL_WINDOWS)
POOL_HALO = 16
RET_DECAY_BASE = 5.0
ROPE_BASE = 10000.0
D_FF = 7 * D_MODEL // 2
N_EXPERTS = 8
RMS_EPS = 1e-6
LN_EPS = 1e-5
N_MIX_COLS = 12 * BRANCH_W

LANE = 128
SUBLANE = 8
VMEM_LIMIT = 56 * 1024 * 1024

HG_CHUNK = 64
HG_SUB = 16
HG_GROUP = 2
HG_FAST_MIN_LOGDECAY = -60.0
RET_CHUNK = 512

F32 = jnp.float32
BF16 = jnp.bfloat16
NT_DIMS = (((1,), (1,)), ((), ()))


def _params(*sem):
    return pltpu.CompilerParams(dimension_semantics=sem, vmem_limit_bytes=VMEM_LIMIT)


def _const_spec(shape):
    nd = len(shape)
    return pl.BlockSpec(shape, lambda *_: (0,) * nd, pipeline_mode=pl.Buffered(1))


def _split3(x):
    hi = x.astype(BF16)
    r1 = x - hi.astype(F32)
    mid = r1.astype(BF16)
    lo = (r1 - mid.astype(F32)).astype(BF16)
    return hi, mid, lo


def _dot(a, b):
    return jnp.dot(a, b, preferred_element_type=F32)


def _dot_nt(a, b):
    return lax.dot_general(a, b, NT_DIMS, preferred_element_type=F32)


def _dot_exact_rhs(x, m_bf16):
    hi, mid, lo = _split3(x)
    return _dot(hi, m_bf16) + _dot(mid, m_bf16) + _dot(lo, m_bf16)


def _dot_exact_lhs(m_bf16, x):
    hi, mid, lo = _split3(x)
    return _dot(m_bf16, hi) + _dot(m_bf16, mid) + _dot(m_bf16, lo)


def _sigmoid(x):
    return 1.0 / (1.0 + jnp.exp(-x))


def _silu(x):
    return x * _sigmoid(x)


def _rms(x, gain):
    return x * lax.rsqrt(jnp.mean(x * x, axis=-1, keepdims=True) + RMS_EPS) * gain


def _head_of(n):
    return jnp.arange(n) // HEAD_DIM


def _head_ones():
    h = _head_of(BRANCH_W)
    return (h[:, None] == h[None, :]).astype(BF16)


def _head_masks():
    return (_head_of(BRANCH_W)[None, :] == jnp.arange(N_HEADS)[:, None]).astype(F32)


def _rmsnorm_kernel(x_ref, g_ref, o_ref):
    o_ref[...] = _rms(x_ref[...], g_ref[...]).astype(o_ref.dtype)


def rmsnorm_bf16(x, gain, tm=1024):
    s, d = x.shape
    tm = min(tm, s)
    return pl.pallas_call(
        _rmsnorm_kernel,
        grid=(s // tm,),
        in_specs=[pl.BlockSpec((tm, d), lambda i: (i, 0)), _const_spec((1, d))],
        out_specs=pl.BlockSpec((tm, d), lambda i: (i, 0)),
        out_shape=jax.ShapeDtypeStruct((s, d), BF16),
        compiler_params=_params("parallel"),
        name="rmsnorm",
    )(x, gain.reshape(1, d))


def _matmul_kernel(a_ref, b_ref, o_ref):
    o_ref[...] = _dot(a_ref[...], b_ref[...]).astype(o_ref.dtype)


def matmul(a, b, out_dtype=BF16, tm=2048, tn=1024):
    m, k = a.shape
    _, n = b.shape
    tm = min(tm, m)
    return pl.pallas_call(
        _matmul_kernel,
        grid=(n // tn, m // tm),
        in_specs=[pl.BlockSpec((tm, k), lambda j, i: (i, 0)),
                  pl.BlockSpec((k, tn), lambda j, i: (0, j))],
        out_specs=pl.BlockSpec((tm, tn), lambda j, i: (i, j)),
        out_shape=jax.ShapeDtypeStruct((m, n), out_dtype),
        compiler_params=_params("parallel", "parallel"),
        name="in_proj",
    )(a, b)


def _hgrn_kernel(q_ref, f_ref, i_ref, g_ref, lbl_ref, ng_ref, ones_ref, hm_ref, tril_ref, halfsum_ref, fmask_ref,
                 o_ref, st_ref, bpad, kpad, vpad, astack, lf_sc, kk_sc, o_sc, *, layer, tile):
    c, sub = HG_CHUNK, HG_SUB
    nsub = c // sub
    half = c // 2

    @pl.when(pl.program_id(0) == 0)
    def _():
        st_ref[...] = jnp.zeros_like(st_ref)
        bpad[...] = jnp.zeros_like(bpad)
        kpad[...] = jnp.zeros_like(kpad)
        vpad[...] = jnp.zeros_like(vpad)

    lbl = lbl_ref[...]
    e = jnp.exp(lbl - jnp.max(lbl, axis=0, keepdims=True))
    p = e / jnp.sum(e, axis=0, keepdims=True)
    lb = jnp.zeros((1, BRANCH_W), F32)
    for l in range(1, layer + 1):
        lb = lb + p[l:l + 1, :]

    ones_bd = ones_ref[...]
    hm = hm_ref[...]
    tril = tril_ref[...]
    row = lax.broadcasted_iota(jnp.int32, (c, 1), 0)
    row_in_sub = row % sub
    bd_mask = ones_bd.astype(F32)

    sig = _sigmoid(f_ref[...].astype(F32))
    logf_all = jnp.log(lb + (1.0 - lb) * sig)
    lf_sc[...] = _dot_exact_lhs(tril, logf_all)
    kk_sc[...] = (1.0 - lb) * (1.0 - sig)
    min_decay = jnp.min(_dot(halfsum_ref[...], logf_all.astype(BF16)))

    def load(ci):
        r0 = pl.multiple_of(ci * c, c)
        q = q_ref[pl.ds(r0, c), :].astype(F32)
        v = i_ref[pl.ds(r0, c), :].astype(F32)
        kk = kk_sc[pl.ds(r0, c), :]
        b = lf_sc[pl.ds(r0, c), :]
        return r0, q, v, kk, b

    def finish(r0, q_decayed, v, kk, b, intra):
        st = st_ref[...]
        inter = _dot_nt(q_decayed.astype(BF16), st.astype(BF16))
        b_last = b[c - 1:c, :]
        ks_end = (kk * jnp.exp(b_last - b)).astype(BF16)
        upd = _dot(v.T.astype(BF16), ks_end)
        st_ref[...] = st * jnp.exp(b_last) + upd * bd_mask
        o_sc[pl.ds(r0, c), :] = intra + inter

    def fast_group(gi, carry):
        second = row >= half
        chunks, qxs, kaugs, vaugs = [], [], [], []
        for j in range(HG_GROUP):
            r0, q, v, kk, b = load(gi * HG_GROUP + j)
            m_row = b[half - 1:half, :]
            mref = jnp.where(second, m_row, 0.0)
            qp = q * jnp.exp(b - mref)
            kp = kk * jnp.exp(mref - b)
            e_m = jnp.exp(m_row)
            kaugs.append(jnp.concatenate([kp, kp[:half, :] * e_m], axis=0))
            vaugs.append(jnp.concatenate([v, v[:half, :]], axis=0))
            qxs.extend(qp * hm[h:h + 1, :] for h in range(N_HEADS))
            chunks.append((r0, jnp.where(second, qp * e_m, qp), v, kk, b))
        qx = jnp.concatenate(qxs, axis=0).astype(BF16)
        kaug = jnp.concatenate(kaugs, axis=0).astype(BF16)
        vaug = jnp.concatenate(vaugs, axis=0).astype(BF16)
        sc = jnp.where(fmask_ref[...] > 0.0, _dot_nt(qx, kaug), 0.0)
        r = _dot(sc.astype(BF16), vaug)
        for j, (r0, q_decayed, v, kk, b) in enumerate(chunks):
            intra = jnp.zeros((c, BRANCH_W), F32)
            for h in range(N_HEADS):
                lo = (j * N_HEADS + h) * c
                intra = intra + r[lo:lo + c, :] * hm[h:h + 1, :]
            finish(r0, q_decayed, v, kk, b, intra)
        return carry

    def exact_chunk(ci, carry):
        r0, q, v, kk, b = load(ci)

        bpad[pl.ds(sub, c), :] = b
        kpad[pl.ds(sub, c), :] = kk
        vpad[pl.ds(sub, c), :] = v

        for d in range(sub):
            b_d = bpad[pl.ds(sub - d, c), :]
            k_d = kpad[pl.ds(sub - d, c), :]
            a = jnp.where(row_in_sub >= d, q * k_d * jnp.exp(b - b_d), 0.0)
            astack[pl.ds(d * c, c), :] = a.astype(BF16)
        pall = _dot(astack[...], ones_bd)
        intra = jnp.zeros((c, BRANCH_W), F32)
        for d in range(sub):
            intra = intra + pall[d * c:(d + 1) * c, :] * vpad[pl.ds(sub - d, c), :]

        pieces = [jnp.zeros((sub, BRANCH_W), F32)]
        for si in range(1, nsub):
            lo = si * sub
            m_i = b[lo - 1:lo, :]
            qs = q[lo:lo + sub, :] * jnp.exp(b[lo:lo + sub, :] - m_i)
            ks = (kk[:lo, :] * jnp.exp(m_i - b[:lo, :])).astype(BF16)
            qx = jnp.concatenate([qs * hm[h:h + 1, :] for h in range(N_HEADS)], axis=0).astype(BF16)
            sc = _dot_nt(qx, ks)
            r = _dot(sc.astype(BF16), v[:lo, :].astype(BF16))
            acc = jnp.zeros((sub, BRANCH_W), F32)
            for h in range(N_HEADS):
                acc = acc + r[h * sub:(h + 1) * sub, :] * hm[h:h + 1, :]
            pieces.append(acc)
        intra = intra + jnp.concatenate(pieces, axis=0)
        finish(r0, q * jnp.exp(b), v, kk, b, intra)
        return carry

    lax.cond(min_decay >= HG_FAST_MIN_LOGDECAY,
             lambda: lax.fori_loop(0, tile // (c * HG_GROUP), fast_group, 0),
             lambda: lax.fori_loop(0, tile // c, exact_chunk, 0))

    o = o_sc[...]
    ms = _dot_exact_rhs(o * o, ones_bd) * (1.0 / HEAD_DIM)
    y = o * lax.rsqrt(ms + RMS_EPS) * ng_ref[...] * _silu(g_ref[...].astype(F32))
    o_ref[...] = y.astype(o_ref.dtype)


def hgrn2(proj, lb_logits, norm_g, layer, tile=512):
    s = proj.shape[0]
    depth = lb_logits.shape[0]
    c, sub = HG_CHUNK, HG_SUB
    half = c // 2
    tile = min(tile, s)
    col = lambda j: pl.BlockSpec((tile, BRANCH_W), lambda i, j=j: (i, j))
    pos = jnp.arange(tile)
    tril = ((pos[:, None] // c == pos[None, :] // c) & (pos[None, :] <= pos[:, None])).astype(BF16)
    nhalf = tile // half
    halfsum = (jnp.arange(tile)[None, :] // half == jnp.arange(nhalf)[:, None]).astype(BF16)
    t = jnp.arange(c)[:, None]
    col_s = jnp.arange(c + half)[None, :]
    same_half = (col_s < c) & (col_s // half == t // half) & (col_s <= t)
    cross = (col_s >= c) & (t >= half)
    fmask1 = jnp.tile((same_half | cross).astype(F32), (N_HEADS, 1))
    fmask = jnp.kron(jnp.eye(HG_GROUP, dtype=F32), fmask1)
    return pl.pallas_call(
        functools.partial(_hgrn_kernel, layer=layer, tile=tile),
        grid=(s // tile,),
        in_specs=[col(0), col(1), col(2), col(3),
                  _const_spec((depth, BRANCH_W)), _const_spec((1, BRANCH_W)),
                  _const_spec((BRANCH_W, BRANCH_W)), _const_spec((N_HEADS, BRANCH_W)),
                  _const_spec((tile, tile)), _const_spec((nhalf, tile)),
                  _const_spec((HG_GROUP * N_HEADS * c, HG_GROUP * (c + half)))],
        out_specs=pl.BlockSpec((tile, BRANCH_W), lambda i: (i, 0)),
        out_shape=jax.ShapeDtypeStruct((s, BRANCH_W), BF16),
        scratch_shapes=[pltpu.VMEM((BRANCH_W, BRANCH_W), F32),
                        pltpu.VMEM((c + sub, BRANCH_W), F32),
                        pltpu.VMEM((c + sub, BRANCH_W), F32),
                        pltpu.VMEM((c + sub, BRANCH_W), F32),
                        pltpu.VMEM((sub * c, BRANCH_W), BF16),
                        pltpu.VMEM((tile, BRANCH_W), F32),
                        pltpu.VMEM((tile, BRANCH_W), F32),
                        pltpu.VMEM((tile, BRANCH_W), F32)],
        compiler_params=_params("arbitrary"),
        name="hgrn2",
    )(proj, proj, proj, proj, lb_logits.astype(F32), norm_g.reshape(1, BRANCH_W).astype(F32),
      _head_ones(), _head_masks(), tril, halfsum, fmask)


FOX_TILE = 256
FOX_GATE_BLOCKS = 2
FOX_NSTAT = 16
FOX_FIXED_MAX = 30.0
FOX_SKIP_LOG = 40.0


def _fox_gate_kernel(h_ref, q_ref, k_ref, wf_ref, bias_ref, tril_ref, ones_ref,
                     ct_ref, c_ref, stat_ref, carry_ref, kmax_ref):
    @pl.when(pl.program_id(0) == 0)
    def _():
        carry_ref[...] = jnp.zeros_like(carry_ref)
        kmax_ref[...] = jnp.zeros_like(kmax_ref)

    logit = _dot(h_ref[...], wf_ref[...]) + bias_ref[...]
    logf = jnp.minimum(logit, 0.0) - jnp.log(1.0 + jnp.exp(-jnp.abs(logit)))
    cum = _dot_exact_lhs(tril_ref[...], logf) + carry_ref[...]
    carry_ref[...] = cum[-1:, :]
    ct_ref[...] = cum.T[:SUBLANE, :]
    c_ref[...] = cum

    ones_bd = ones_ref[...]
    q = q_ref[...].astype(F32)
    k = k_ref[...].astype(F32)
    scale = HEAD_DIM ** -0.5
    head_lane = lax.broadcasted_iota(jnp.int32, (1, BRANCH_W), 1) // HEAD_DIM
    c_heads = jnp.zeros(q.shape, F32)
    for h in range(N_HEADS):
        c_heads = jnp.where(head_lane == h, cum[:, h:h + 1], c_heads)
    slack = 1.0 + 2.0 ** -6
    qn = jnp.sqrt(_dot((q * q).astype(BF16), ones_bd)) * (scale * slack)
    kn = jnp.sqrt(_dot((k * k).astype(BF16), ones_bd)) * slack
    diag = _dot((q * k).astype(BF16), ones_bd) * scale - (2.0 ** -6) * qn * kn
    e_row = c_heads - diag
    kmax = kmax_ref[...]
    for blk in range(q.shape[0] // FOX_TILE):
        rs = slice(blk * FOX_TILE, (blk + 1) * FOX_TILE)
        kmax = jnp.maximum(kmax, jnp.max(kn[rs], axis=0, keepdims=True))
        rows = [jnp.max(qn[rs], axis=0, keepdims=True),
                jnp.max(e_row[rs], axis=0, keepdims=True),
                kmax,
                c_heads[(blk + 1) * FOX_TILE - 1:(blk + 1) * FOX_TILE, :]]
        stat_ref[blk] = jnp.concatenate(rows + [jnp.zeros((SUBLANE - len(rows), BRANCH_W), F32)], axis=0)
    kmax_ref[...] = kmax


def fox_gate(h, proj, w_f, f_bias):
    s, d = h.shape
    tile = min(FOX_GATE_BLOCKS * FOX_TILE, s)
    nblk = tile // FOX_TILE
    bias = jnp.zeros((1, LANE), F32).at[0, :N_HEADS].set(f_bias.astype(F32))
    tril = jnp.tril(jnp.ones((tile, tile), F32)).astype(BF16)
    ct, c_rows, stats = pl.pallas_call(
        _fox_gate_kernel,
        grid=(s // tile,),
        in_specs=[pl.BlockSpec((tile, d), lambda i: (i, 0)),
                  pl.BlockSpec((tile, BRANCH_W), lambda i: (i, 4)),
                  pl.BlockSpec((tile, BRANCH_W), lambda i: (i, 5)),
                  _const_spec((d, LANE)), _const_spec((1, LANE)), _const_spec((tile, tile)),
                  _const_spec((BRANCH_W, BRANCH_W))],
        out_specs=[pl.BlockSpec((SUBLANE, tile), lambda i: (0, i)),
                   pl.BlockSpec((tile, LANE), lambda i: (i, 0)),
                   pl.BlockSpec((nblk, SUBLANE, BRANCH_W), lambda i: (i, 0, 0))],
        out_shape=[jax.ShapeDtypeStruct((SUBLANE, s), F32),
                   jax.ShapeDtypeStruct((s, LANE), F32),
                   jax.ShapeDtypeStruct((s // FOX_TILE, SUBLANE, BRANCH_W), F32)],
        scratch_shapes=[pltpu.VMEM((1, LANE), F32), pltpu.VMEM((1, BRANCH_W), F32)],
        compiler_params=_params("arbitrary"),
        name="fox_gate",
    )(h, proj, proj, w_f, bias, tril, _head_ones())
    return ct, c_rows, stats[:, :4, ::HEAD_DIM].reshape(-1)


def _fox_kernel(stat_ref, q_ref, k_ref, v_ref, ct_ref, c_ref, hm_ref, o_ref, m_sc, l_sc, acc_sc, *, tq):
    i = pl.program_id(0)
    q0 = pl.multiple_of(i * tq, tq)
    hm = hm_ref[...]
    q = q_ref[...].astype(F32) * (HEAD_DIM ** -0.5)
    qh = [(q * hm[h:h + 1, :]).astype(BF16) for h in range(N_HEADS)]
    c_q0 = ct_ref[:, pl.ds(q0, tq)][:, 0:1]

    first = i
    for h in range(N_HEADS):
        qmax = stat_ref[i * FOX_NSTAT + h]
        emax = stat_ref[i * FOX_NSTAT + N_HEADS + h]

        def needed(j, h=h, qmax=qmax, emax=emax):
            jc = jnp.maximum(j, 0)
            bound = (qmax * stat_ref[jc * FOX_NSTAT + 2 * N_HEADS + h] + emax
                     - stat_ref[jc * FOX_NSTAT + 3 * N_HEADS + h])
            return (j >= 0) & (bound >= -FOX_SKIP_LOG)

        last_dropped = lax.while_loop(needed, lambda j: j - 1, i - 1)
        first = jnp.minimum(first, last_dropped + 1)

    l_sc[...] = jnp.zeros_like(l_sc)
    acc_sc[...] = jnp.zeros_like(acc_sc)

    def causal(sc):
        r = lax.broadcasted_iota(jnp.int32, (tq, tq), 0)
        cidx = lax.broadcasted_iota(jnp.int32, (tq, tq), 1)
        return jnp.where(cidx <= r, sc, -jnp.inf)

    def online_block(s0, diagonal):
        kb = k_ref[pl.ds(s0, tq), :]
        vb = v_ref[pl.ds(s0, tq), :]
        bias = c_q0 - ct_ref[:, pl.ds(s0, tq)]
        for h in range(N_HEADS):
            sc = _dot_nt(qh[h], kb) + bias[h:h + 1, :]
            if diagonal:
                sc = causal(sc)
            m_prev = m_sc[h]
            m_new = jnp.maximum(m_prev, jnp.max(sc, axis=1, keepdims=True))
            alpha = jnp.exp(m_prev - m_new)
            p = jnp.exp(sc - jnp.tile(m_new, (1, tq // LANE)))
            l_sc[h] = alpha * l_sc[h] + jnp.sum(p, axis=1, keepdims=True)
            acc_sc[h] = acc_sc[h] * jnp.tile(alpha, (1, BRANCH_W // LANE)) + _dot(p.astype(BF16), vb)
            m_sc[h] = m_new

    tops = [stat_ref[i * FOX_NSTAT + h] * stat_ref[i * FOX_NSTAT + 2 * N_HEADS + h] for h in range(N_HEADS)]
    c_tile = c_ref[...]
    shift = [c_tile[:, h:h + 1] - c_q0[h:h + 1, :] - tops[h] for h in range(N_HEADS)]

    def fixed_block(s0, diagonal):
        kb = k_ref[pl.ds(s0, tq), :]
        vb = v_ref[pl.ds(s0, tq), :]
        bias = c_q0 - ct_ref[:, pl.ds(s0, tq)]
        for h in range(N_HEADS):
            sc = _dot_nt(qh[h], kb) + bias[h:h + 1, :] + shift[h]
            if diagonal:
                sc = causal(sc)
            p = jnp.exp(sc)
            l_sc[h] += p[:, :LANE] + p[:, LANE:]
            acc_sc[h] += _dot(p.astype(BF16), vb)

    def run(block, row_sum):
        lax.fori_loop(first, i, lambda j, carry: (block(pl.multiple_of(j * tq, tq), False), carry)[1], 0)
        block(q0, True)
        out = jnp.zeros((tq, BRANCH_W), F32)
        for h in range(N_HEADS):
            out = out + acc_sc[h] * hm[h:h + 1, :] / row_sum(l_sc[h])
        o_ref[...] = out.astype(o_ref.dtype)

    def run_online():
        m_sc[...] = jnp.full_like(m_sc, -jnp.inf)
        run(online_block, lambda l: jnp.tile(l, (1, BRANCH_W // LANE)))

    def run_fixed():
        run(fixed_block, lambda l: jnp.sum(l, axis=1, keepdims=True))

    lax.cond(functools.reduce(jnp.maximum, tops) <= FOX_FIXED_MAX, run_fixed, run_online)


def fox_attention(proj, ct, c_rows, stats):
    s = proj.shape[0]
    tq = min(FOX_TILE, s)
    full = lambda j: pl.BlockSpec((s, BRANCH_W), lambda i, j=j: (0, j), pipeline_mode=pl.Buffered(1))
    return pl.pallas_call(
        functools.partial(_fox_kernel, tq=tq),
        grid=(s // tq,),
        in_specs=[pl.BlockSpec(memory_space=pltpu.SMEM),
                  pl.BlockSpec((tq, BRANCH_W), lambda i: (i, 4)), full(5), full(6),
                  _const_spec((SUBLANE, s)), pl.BlockSpec((tq, LANE), lambda i: (i, 0)),
                  _const_spec((N_HEADS, BRANCH_W))],
        out_specs=pl.BlockSpec((tq, BRANCH_W), lambda i: (i, 0)),
        out_shape=jax.ShapeDtypeStruct((s, BRANCH_W), BF16),
        scratch_shapes=[pltpu.VMEM((N_HEADS, tq, LANE), F32),
                        pltpu.VMEM((N_HEADS, tq, LANE), F32),
                        pltpu.VMEM((N_HEADS, tq, BRANCH_W), F32)],
        compiler_params=_params("parallel"),
        name="fox_attention",
    )(stats, proj, proj, proj, ct, c_rows, _head_masks())


def _pool_kernel(u_ref, w_ref, scale_ref, o_ref, ext, *, tile):
    i = pl.program_id(0)

    @pl.when(i == 0)
    def _():
        ext[pl.ds(0, POOL_HALO), :] = jnp.zeros((POOL_HALO, BRANCH_W), F32)

    u = u_ref[...].astype(F32)
    ext[pl.ds(POOL_HALO, tile), :] = u
    pos = (i * tile + lax.broadcasted_iota(jnp.int32, (tile, 1), 0) + 1).astype(F32)
    halves = []
    for half in range(BRANCH_W // LANE):
        lanes = pl.ds(half * LANE, LANE)
        w_small, w_big = POOL_WINDOWS[2 * half], POOL_WINDOWS[2 * half + 1]
        run = u[:, half * LANE:(half + 1) * LANE]
        sums = {}
        for j in range(1, w_big):
            if j == w_small:
                sums[w_small] = run
            run = run + ext[pl.ds(POOL_HALO - j, tile), lanes]
        sums[w_big] = run
        lane = lax.broadcasted_iota(jnp.int32, (1, LANE), 1)
        small = lane < POOL_GROUP
        total = jnp.where(small, sums[w_small], sums[w_big])
        count = jnp.where(small, jnp.minimum(pos, float(w_small)), jnp.minimum(pos, float(w_big)))
        halves.append(total / count)
    mean = jnp.concatenate(halves, axis=1)
    d = (mean - u).astype(BF16)
    y = _dot(d, w_ref[...]) * scale_ref[...]
    o_ref[...] = y.astype(o_ref.dtype)
    ext[pl.ds(0, POOL_HALO), :] = u[tile - POOL_HALO:, :]


def pool_mixer(proj, w_pool, scale, tile=512):
    s = proj.shape[0]
    tile = min(tile, s)
    ng = len(POOL_WINDOWS)
    w_bd = jnp.zeros((BRANCH_W, BRANCH_W), F32)
    for gi in range(ng):
        lo = gi * POOL_GROUP
        w_bd = w_bd.at[lo:lo + POOL_GROUP, lo:lo + POOL_GROUP].set(w_pool[gi].astype(F32))
    return pl.pallas_call(
        functools.partial(_pool_kernel, tile=tile),
        grid=(s // tile,),
        in_specs=[pl.BlockSpec((tile, BRANCH_W), lambda i: (i, 7)),
                  _const_spec((BRANCH_W, BRANCH_W)), _const_spec((1, BRANCH_W))],
        out_specs=pl.BlockSpec((tile, BRANCH_W), lambda i: (i, 0)),
        out_shape=jax.ShapeDtypeStruct((s, BRANCH_W), BF16),
        scratch_shapes=[pltpu.VMEM((tile + POOL_HALO, BRANCH_W), F32)],
        compiler_params=_params("arbitrary"),
        name="pool_mixer",
    )(proj, w_bd.astype(BF16), scale.reshape(1, BRANCH_W).astype(F32))


def _ret_kernel(q_ref, k_ref, v_ref, g_ref, rope_ref, ecos_ref, esin_ref, perm_ref, ones_ref, hm_ref,
                dstack_ref, xi_ref, zeta_ref, gc_ref, gng_ref, gnb_ref, o_ref, st_ref, *, tile):
    c = RET_CHUNK

    @pl.when(pl.program_id(0) == 0)
    def _():
        st_ref[...] = jnp.zeros_like(st_ref)

    perm = perm_ref[...]
    ones_bd = ones_ref[...]
    bd_mask = ones_bd.astype(F32)
    hm = hm_ref[...]

    tab = rope_ref[...]
    cos = _dot_exact_rhs(tab, ecos_ref[...])
    sin = _dot_exact_rhs(tab, esin_ref[...])
    q_all = q_ref[...]
    k_all = k_ref[...]
    qr_all = q_all.astype(F32) * cos + _dot(q_all, perm) * sin
    kr_all = (k_all.astype(F32) * cos + _dot(k_all, perm) * sin) * (HEAD_DIM ** -0.5)

    outs = []
    for ci in range(tile // c):
        r0 = ci * c
        qr = qr_all[r0:r0 + c, :]
        kr = kr_all[r0:r0 + c, :]
        v = v_ref[pl.ds(r0, c), :]

        qx = jnp.concatenate([qr * hm[h:h + 1, :] for h in range(N_HEADS)], axis=0).astype(BF16)
        sc = _dot_nt(qx, kr.astype(BF16)) * dstack_ref[...]
        r = _dot(sc.astype(BF16), v)
        intra = jnp.zeros((c, BRANCH_W), F32)
        for h in range(N_HEADS):
            intra = intra + r[h * c:(h + 1) * c, :] * hm[h:h + 1, :]

        st = st_ref[...]
        inter = _dot_nt((qr * xi_ref[...]).astype(BF16), st.astype(BF16))
        upd = _dot(v.astype(F32).T.astype(BF16), (kr * zeta_ref[...]).astype(BF16))
        st_ref[...] = st * gc_ref[...] + upd * bd_mask
        outs.append(intra + inter)

    o = jnp.concatenate(outs, axis=0)
    mu = _dot_exact_rhs(o, ones_bd) * (1.0 / HEAD_DIM)
    cen = o - mu
    var = _dot_exact_rhs(cen * cen, ones_bd) * (1.0 / HEAD_DIM)
    y = cen * lax.rsqrt(var + LN_EPS) * gng_ref[...] + gnb_ref[...]
    o_ref[...] = (y * _silu(g_ref[...].astype(F32))).astype(o_ref.dtype)


def _rope_tables(s):
    half = HEAD_DIM // 2
    pos = jnp.arange(s, dtype=F32)
    inv_freq = ROPE_BASE ** (-jnp.arange(half, dtype=F32) / half)
    ang = pos[:, None] * inv_freq[None, :]
    table = jnp.concatenate([jnp.cos(ang), jnp.sin(ang), jnp.zeros((s, LANE - 2 * half), F32)], axis=1)
    lane = jnp.arange(BRANCH_W)
    src = jnp.arange(LANE)[:, None]
    ecos = (src == (lane % half)[None, :]).astype(F32)
    sign = jnp.where(lane % HEAD_DIM < half, -1.0, 1.0)
    esin = (src == (half + lane % half)[None, :]).astype(F32) * sign[None, :]
    return table, ecos.astype(BF16), esin.astype(BF16)


def _ret_constants():
    c = RET_CHUNK
    half = HEAD_DIM // 2
    lane = jnp.arange(BRANCH_W)
    partner = jnp.where(lane % HEAD_DIM < half, lane + half, lane - half)
    perm = (lane[:, None] == partner[None, :]).astype(BF16)
    log_gamma = jnp.log1p(-jnp.exp2(-RET_DECAY_BASE - jnp.arange(N_HEADS, dtype=F32)))
    ci = jnp.arange(c, dtype=F32)
    diff = ci[:, None] - ci[None, :]
    intra = jnp.where(diff >= 0, jnp.exp(diff * log_gamma[:, None, None]), 0.0)
    dstack = intra.reshape(N_HEADS * c, c)
    lg_lane = jnp.repeat(log_gamma, HEAD_DIM)[None, :]
    xi = jnp.exp((ci[:, None] + 1.0) * lg_lane)
    zeta = jnp.exp((c - 1.0 - ci[:, None]) * lg_lane)
    gc = jnp.exp(c * lg_lane)
    return perm, dstack, xi, zeta, gc


def retention(proj, gn_g, gn_b, tile=512):
    s = proj.shape[0]
    c = RET_CHUNK
    tile = min(tile, s)
    rope, ecos, esin = _rope_tables(s)
    perm, dstack, xi, zeta, gc = _ret_constants()
    col = lambda j: pl.BlockSpec((tile, BRANCH_W), lambda i, j=j: (i, j))
    row = pl.BlockSpec((tile, BRANCH_W), lambda i: (i, 0))
    return pl.pallas_call(
        functools.partial(_ret_kernel, tile=tile),
        grid=(s // tile,),
        in_specs=[col(8), col(9), col(10), col(11), pl.BlockSpec((tile, LANE), lambda i: (i, 0)),
                  _const_spec((LANE, BRANCH_W)), _const_spec((LANE, BRANCH_W)),
                  _const_spec((BRANCH_W, BRANCH_W)), _const_spec((BRANCH_W, BRANCH_W)),
                  _const_spec((N_HEADS, BRANCH_W)), _const_spec((N_HEADS * c, c)),
                  _const_spec((c, BRANCH_W)), _const_spec((c, BRANCH_W)), _const_spec((1, BRANCH_W)),
                  _const_spec((1, BRANCH_W)), _const_spec((1, BRANCH_W))],
        out_specs=row,
        out_shape=jax.ShapeDtypeStruct((s, BRANCH_W), BF16),
        scratch_shapes=[pltpu.VMEM((BRANCH_W, BRANCH_W), F32)],
        compiler_params=_params("arbitrary"),
        name="retention",
    )(proj, proj, proj, proj, rope, ecos, esin, perm, _head_ones(), _head_masks(), dstack, xi, zeta, gc,
      gn_g.reshape(1, BRANCH_W).astype(F32), gn_b.reshape(1, BRANCH_W).astype(F32))


def _merge_kernel(h_ref, o0_ref, o1_ref, o2_ref, o3_ref, x_ref, wg_ref, wb_ref, wo_ref, g_ref,
                  xo_ref, ho_ref):
    h = h_ref[...]
    merged = jnp.zeros(x_ref.shape, F32)
    for bi, o_ref in enumerate((o0_ref, o1_ref, o2_ref, o3_ref)):
        gate = _sigmoid(_dot(h, wg_ref[:, bi * D_MODEL:(bi + 1) * D_MODEL]))
        merged = merged + gate * _dot(o_ref[...], wb_ref[bi])
    x_new = x_ref[...] + _dot(merged.astype(BF16), wo_ref[...])
    xo_ref[...] = x_new
    ho_ref[...] = _rms(x_new, g_ref[...]).astype(ho_ref.dtype)


def merge(h, branches, x, w_gate, w_branch, w_out, next_gain, tm=512):
    s, d = x.shape
    tm = min(tm, s)
    row = lambda w: pl.BlockSpec((tm, w), lambda i: (i, 0))
    return pl.pallas_call(
        _merge_kernel,
        grid=(s // tm,),
        in_specs=[row(d), row(BRANCH_W), row(BRANCH_W), row(BRANCH_W), row(BRANCH_W), row(d),
                  _const_spec((d, N_BRANCH * d)), _const_spec((N_BRANCH, BRANCH_W, d)),
                  _const_spec((d, d)), _const_spec((1, d))],
        out_specs=[row(d), row(d)],
        out_shape=[jax.ShapeDtypeStruct((s, d), F32), jax.ShapeDtypeStruct((s, d), BF16)],
        compiler_params=_params("parallel"),
        name="merge",
    )(h, *branches, x, w_gate, w_branch, w_out, next_gain.reshape(1, d).astype(F32))


def _ffn_kernel(h_ref, x_ref, wg_ref, wu_ref, wd_ref, g_ref, xo_ref, ho_ref, acc_ref):
    f = pl.program_id(1)

    @pl.when(f == 0)
    def _():
        acc_ref[...] = jnp.zeros_like(acc_ref)

    h = h_ref[...]
    a = _silu(_dot(h, wg_ref[...].astype(BF16))) * _dot(h, wu_ref[...].astype(BF16))
    acc_ref[...] += _dot(a.astype(BF16), wd_ref[...].astype(BF16))

    @pl.when(f == pl.num_programs(1) - 1)
    def _():
        x_new = x_ref[...] + acc_ref[...]
        xo_ref[...] = x_new
        ho_ref[...] = _rms(x_new, g_ref[...]).astype(ho_ref.dtype)


def ffn_dense(h, x, w_gate, w_up, w_down, next_gain, tm=1024, tf=512):
    s, d = x.shape
    tm = min(tm, s)
    dff = w_gate.shape[1]
    row = lambda: pl.BlockSpec((tm, d), lambda i, f: (i, 0))
    return pl.pallas_call(
        _ffn_kernel,
        grid=(s // tm, dff // tf),
        in_specs=[row(), row(),
                  pl.BlockSpec((d, tf), lambda i, f: (0, f)),
                  pl.BlockSpec((d, tf), lambda i, f: (0, f)),
                  pl.BlockSpec((tf, d), lambda i, f: (f, 0)),
                  _const_spec((1, d))],
        out_specs=[row(), row()],
        out_shape=[jax.ShapeDtypeStruct((s, d), F32), jax.ShapeDtypeStruct((s, d), BF16)],
        scratch_shapes=[pltpu.VMEM((tm, d), F32)],
        compiler_params=_params("parallel", "arbitrary"),
        name="ffn_dense",
    )(h, x, w_gate, w_up, w_down, next_gain.reshape(1, d).astype(F32))


MOE_TOK_TILE = 256
MOE_ROUTE_TILES = 2
MOE_ROW_BLOCK = 512
MOE_GATHER_ROWS = 128
MOE_GATHER_TILES = 4
MOE_VMEM_LIMIT = 60 * 1024 * 1024
MOE_Y_BLOCK = 128
MOE_Y_FETCH = MOE_TOK_TILE // MOE_Y_BLOCK + 1


def _dot_f32(x, w):
    xh, xm, _ = _split3(x)
    wh, wm, _ = _split3(w)
    return _dot(xh, wh) + (_dot(xh, wm) + _dot(xm, wh))


def _route_kernel(x_ref, ng_ref, router_ref, ltri_ref, pos_ref, gate_ref, post_ref, before_ref, total_ref,
                  carry_ref):
    tm = x_ref.shape[0]

    @pl.when(pl.program_id(0) == 0)
    def _():
        carry_ref[...] = jnp.zeros_like(carry_ref)

    hn = _rms(x_ref[...], ng_ref[...])
    logits = _dot_f32(hn, router_ref[...])
    lane = lax.broadcasted_iota(jnp.int32, (tm, LANE), 1)
    logits = jnp.where(lane < N_EXPERTS, logits, -jnp.inf)
    v1 = jnp.max(logits, axis=1, keepdims=True)
    i1 = jnp.min(jnp.where(logits == v1, lane, LANE), axis=1, keepdims=True)
    rest = jnp.where(lane == i1, -jnp.inf, logits)
    v2 = jnp.max(rest, axis=1, keepdims=True)
    i2 = jnp.min(jnp.where(rest == v2, lane, LANE), axis=1, keepdims=True)
    w1 = 1.0 / (1.0 + jnp.exp(v2 - v1))
    gate_ref[...] = jnp.where(lane == i1, w1, 0.0) + jnp.where(lane == i2, 1.0 - w1, 0.0)

    member = jnp.where((lane == i1) | (lane == i2), 1.0, 0.0)
    carry = carry_ref[...]
    rank = _dot(ltri_ref[...], member.astype(BF16)) + carry
    pos = jnp.where(member > 0.0, rank, -1.0)
    pos_ref[...] = pos
    post_ref[...] = pos.T[:SUBLANE, :]
    for blk in range(tm // MOE_TOK_TILE):
        before_ref[blk] = carry
        carry = carry + jnp.sum(member[blk * MOE_TOK_TILE:(blk + 1) * MOE_TOK_TILE], axis=0, keepdims=True)
    carry_ref[...] = carry
    total_ref[...] = carry


def moe_route(x, norm_gain, router):
    s, d = x.shape
    tm = min(MOE_ROUTE_TILES * MOE_TOK_TILE, s)
    nsub = tm // MOE_TOK_TILE
    nt = s // tm
    router_p = jnp.zeros((d, LANE), F32).at[:, :N_EXPERTS].set(router.astype(F32))
    ltri = jnp.tril(jnp.ones((tm, tm), F32), -1).astype(BF16)
    row = pl.BlockSpec((tm, LANE), lambda i: (i, 0))
    return pl.pallas_call(
        _route_kernel,
        grid=(nt,),
        in_specs=[pl.BlockSpec((tm, d), lambda i: (i, 0)), _const_spec((1, d)), _const_spec((d, LANE)),
                  _const_spec((tm, tm))],
        out_specs=[row, row, pl.BlockSpec((SUBLANE, tm), lambda i: (0, i)),
                   pl.BlockSpec((nsub, 1, LANE), lambda i: (i, 0, 0)), pl.BlockSpec((1, LANE), lambda i: (0, 0))],
        out_shape=[jax.ShapeDtypeStruct((s, LANE), F32), jax.ShapeDtypeStruct((s, LANE), F32),
                   jax.ShapeDtypeStruct((SUBLANE, s), F32), jax.ShapeDtypeStruct((nt * nsub, 1, LANE), F32),
                   jax.ShapeDtypeStruct((1, LANE), F32)],
        scratch_shapes=[pltpu.VMEM((1, LANE), F32)],
        compiler_params=_params("arbitrary"),
        name="moe_route",
    )(x, norm_gain.reshape(1, d).astype(F32), router_p, ltri)


def _moe_ffn_kernel(be_ref, r0_ref, tlo_ref, thi_ref, nv_ref, h_ref, post_ref, wg_ref, wu_ref, wd_ref,
                    y_ref, x_sc, acc_sc, *, tt, nsub):
    b = pl.program_id(0)
    f = pl.program_id(1)
    nb = pl.num_programs(0)
    valid = b < nv_ref[0]
    tmb, d = acc_sc.shape
    gr = tmb // nsub
    last_tile = h_ref.shape[0] // tt - 1

    def picked(e, want, t):
        t0 = pl.multiple_of(t * tt, tt)
        p = post_ref[pl.ds(e, 1), pl.ds(t0, tt)]
        sel = jnp.where(p == want, 1.0, 0.0).astype(BF16)
        return _dot(sel, h_ref[pl.ds(t0, tt), :])

    def gather_head(blk, sb):
        e = be_ref[blk]
        lo = tlo_ref[blk * nsub + sb]
        hi = thi_ref[blk * nsub + sb]
        want = (lax.broadcasted_iota(jnp.int32, (gr, 1), 0) + (r0_ref[blk] + sb * gr)).astype(F32)
        rows = picked(e, want, lo)
        for k in range(1, MOE_GATHER_TILES):
            rows = rows + picked(e, jnp.where(lo + k <= hi, want, -2.0), jnp.minimum(lo + k, last_tile))
        return rows.astype(BF16)

    def gather_tail(slot, blk, sb):
        e = be_ref[blk]
        want = (lax.broadcasted_iota(jnp.int32, (gr, 1), 0) + (r0_ref[blk] + sb * gr)).astype(F32)
        rows = pl.ds(pl.multiple_of(sb * gr, gr), gr)

        def more(t, carry):
            x_sc[slot, rows, :] = (x_sc[slot, rows, :].astype(F32) + picked(e, want, t)).astype(BF16)
            return carry

        lax.fori_loop(tlo_ref[blk * nsub + sb] + MOE_GATHER_TILES, thi_ref[blk * nsub + sb] + 1, more, 0)

    @pl.when((b == 0) & (f == 0))
    def _():
        for sb in range(nsub):
            x_sc[0, pl.ds(sb * gr, gr), :] = gather_head(0, sb)
            gather_tail(0, 0, sb)

    nxt = jnp.minimum(b + 1, nb - 1)
    nslot = (b + 1) % 2

    @pl.when(valid)
    def _():
        x_sc[nslot, pl.ds(pl.multiple_of(f * gr, gr), gr), :] = gather_head(nxt, f)
        xb = x_sc[b % 2]
        a = _silu(_dot(xb, wg_ref[0])) * _dot(xb, wu_ref[0])
        acc_sc[...] = jnp.where(f == 0, 0.0, acc_sc[...]) + _dot(a.astype(BF16), wd_ref[0].astype(BF16))

    @pl.when(valid & (thi_ref[nxt * nsub + f] - tlo_ref[nxt * nsub + f] >= MOE_GATHER_TILES))
    def _():
        gather_tail(nslot, nxt, f)

    @pl.when(f == nsub - 1)
    def _():
        y_ref[...] = jnp.where(valid, acc_sc[...], 0.0).astype(y_ref.dtype)


def moe_ffn(h, post, sched, w_gate, w_up, w_down):
    s, d = h.shape
    ne, _, dff = w_gate.shape
    tmb = MOE_ROW_BLOCK
    tt = min(MOE_TOK_TILE, s)
    nb = sched[0].shape[0]
    nf = tmb // MOE_GATHER_ROWS
    tf = dff // nf

    def fidx(b, f, nv):
        return jnp.where(b < nv[0], f, nf - 1)

    grid_spec = pltpu.PrefetchScalarGridSpec(
        num_scalar_prefetch=5,
        grid=(nb, nf),
        in_specs=[pl.BlockSpec((s, d), lambda b, f, *_: (0, 0), pipeline_mode=pl.Buffered(1)),
                  pl.BlockSpec((SUBLANE, s), lambda b, f, *_: (0, 0), pipeline_mode=pl.Buffered(1)),
                  pl.BlockSpec((1, d, tf), lambda b, f, be, r0, tlo, thi, nv: (be[b], 0, fidx(b, f, nv))),
                  pl.BlockSpec((1, d, tf), lambda b, f, be, r0, tlo, thi, nv: (be[b], 0, fidx(b, f, nv))),
                  pl.BlockSpec((1, tf, d), lambda b, f, be, r0, tlo, thi, nv: (be[b], fidx(b, f, nv), 0))],
        out_specs=pl.BlockSpec((tmb, d), lambda b, f, *_: (b, 0)),
        scratch_shapes=[pltpu.VMEM((2, tmb, d), BF16), pltpu.VMEM((tmb, d), F32)],
    )
    return pl.pallas_call(
        functools.partial(_moe_ffn_kernel, tt=tt, nsub=nf),
        grid_spec=grid_spec,
        out_shape=jax.ShapeDtypeStruct((nb * tmb, d), BF16),
        compiler_params=pltpu.CompilerParams(dimension_semantics=("arbitrary", "arbitrary"),
                                             vmem_limit_bytes=MOE_VMEM_LIMIT),
        name="moe_ffn",
    )(*sched, h, post, w_gate, w_up, w_down)


def _moe_combine_kernel(kb_ref, off_ref, lim_ref, x_ref, pos_ref, gate_ref, fg_ref, *rest):
    y_refs, o_ref, acc_sc = rest[:-2], rest[-2], rest[-1]
    t = pl.program_id(0)
    tm = x_ref.shape[0]
    yb = y_refs[0].shape[0]
    pos = pos_ref[...]
    gate = gate_ref[...]
    lane = lax.broadcasted_iota(jnp.int32, (tm, LANE), 1)
    col = lax.broadcasted_iota(jnp.int32, (1, yb), 1).astype(F32)
    def routed(e):
        pe = jnp.sum(jnp.where(lane == e, pos, 0.0), axis=1, keepdims=True)
        ge = jnp.sum(jnp.where(lane == e, gate, 0.0), axis=1, keepdims=True)
        r = jnp.where(pe >= 0.0, pe + off_ref[t * N_EXPERTS + e].astype(F32), -1.0)
        return r, ge

    def picked(e, k, r):
        sel = jnp.where(r == col + float(k * yb), 1.0, 0.0).astype(BF16)
        return _dot(sel, y_refs[MOE_Y_FETCH * e + k][...])

    col2 = lax.broadcasted_iota(jnp.int32, (1, 2 * yb), 1).astype(F32)
    acc = x_ref[...]
    for e in range(N_EXPERTS):
        r, ge = routed(e)
        pair = jnp.concatenate([y_refs[MOE_Y_FETCH * e][...], y_refs[MOE_Y_FETCH * e + 1][...]], axis=0)
        acc = acc + ge * _dot(jnp.where(r == col2, 1.0, 0.0).astype(BF16), pair)
    acc_sc[...] = acc
    for e in range(N_EXPERTS):
        for k in range(2, MOE_Y_FETCH):
            @pl.when(lim_ref[t * N_EXPERTS + e] > k * yb)
            def _(k=k, e=e):
                r, ge = routed(e)
                acc_sc[...] += ge * picked(e, k, r)
    o_ref[...] = _rms(acc_sc[...], fg_ref[...]).astype(o_ref.dtype)


def moe_combine(x, pos, gate, y, kb, off, lim, final_gain):
    s, d = x.shape
    tm = min(MOE_TOK_TILE, s)
    yb = MOE_Y_BLOCK
    last = y.shape[0] // yb - 1

    def yspec(e, k):
        def index(t, kb_r, off_r, lim_r):
            blk = jnp.minimum(kb_r[t * N_EXPERTS + e] + k, last)
            return (blk if k < 1 else jnp.where(lim_r[t * N_EXPERTS + e] > k * yb, blk, 0), 0)
        return pl.BlockSpec((yb, d), index)

    grid_spec = pltpu.PrefetchScalarGridSpec(
        num_scalar_prefetch=3,
        grid=(s // tm,),
        in_specs=[pl.BlockSpec((tm, d), lambda t, *_: (t, 0)),
                  pl.BlockSpec((tm, LANE), lambda t, *_: (t, 0)),
                  pl.BlockSpec((tm, LANE), lambda t, *_: (t, 0)),
                  pl.BlockSpec((1, d), lambda t, *_: (0, 0))]
        + [yspec(e, k) for e in range(N_EXPERTS) for k in range(MOE_Y_FETCH)],
        out_specs=pl.BlockSpec((tm, d), lambda t, *_: (t, 0)),
        scratch_shapes=[pltpu.VMEM((tm, d), F32)],
    )
    return pl.pallas_call(
        _moe_combine_kernel,
        grid_spec=grid_spec,
        out_shape=jax.ShapeDtypeStruct((s, d), F32),
        compiler_params=_params("arbitrary"),
        name="moe_combine",
    )(kb, off, lim, x, pos, gate, final_gain.reshape(1, d).astype(F32), *([y] * (MOE_Y_FETCH * N_EXPERTS)))


def _moe_schedule(before, total, s):
    tmb, yb, gr = MOE_ROW_BLOCK, MOE_Y_BLOCK, MOE_GATHER_ROWS
    nb = 2 * s // tmb + N_EXPERTS
    counts = total[0, :N_EXPERTS].astype(jnp.int32)
    nblk = (counts + tmb - 1) // tmb
    end = jnp.cumsum(nblk)
    first = end - nblk
    nvalid = end[-1]
    b = jnp.minimum(jnp.arange(nb, dtype=jnp.int32), nvalid - 1)
    blk_e = jnp.sum(b[:, None] >= end[None, :], axis=1).astype(jnp.int32)
    r0 = (b - first[blk_e]) * tmb
    cb = before[:, 0, :N_EXPERTS].astype(jnp.int32)
    r0s = (r0[:, None] + gr * jnp.arange(tmb // gr, dtype=jnp.int32)[None, :]).reshape(-1)
    cbe = cb[:, jnp.repeat(blk_e, tmb // gr)]
    tlo = (jnp.sum(cbe <= r0s[None, :], axis=0) - 1).astype(jnp.int32)
    thi = (jnp.sum(cbe < (r0s + gr)[None, :], axis=0) - 1).astype(jnp.int32)
    sched = (blk_e, r0.astype(jnp.int32), tlo, thi, nvalid.reshape(1).astype(jnp.int32))
    row_start = first[None, :] * tmb + cb
    kb = row_start // yb
    off = first[None, :] * tmb - kb * yb
    n_te = jnp.concatenate([cb[1:], counts[None, :]], axis=0) - cb
    lim = row_start - kb * yb + n_te
    flat = lambda a: a.reshape(-1).astype(jnp.int32)
    return sched, flat(kb), flat(off), flat(lim)


def moe_sparse(h, x, norm_gain, router, w_gate, w_up, w_down, final_gain):
    s, _ = x.shape
    pos, gate, post, before, total = moe_route(x, norm_gain, router)
    sched, kb, off, lim = _moe_schedule(before, total, s)
    y = moe_ffn(h, post, sched, w_gate, w_up, w_down)
    return moe_combine(x, pos, gate, y, kb, off, lim, final_gain)


FOX_F0 = 7 * BRANCH_W


def _split_w_in_kernel(w_ref, mix_ref, f_ref, gate_ref):
    w = w_ref[...]
    rows = w.shape[0]
    mix_ref[:, :FOX_F0] = w[:, :FOX_F0].astype(BF16)
    mix_ref[:, FOX_F0:] = w[:, FOX_F0 + N_HEADS:N_MIX_COLS + N_HEADS].astype(BF16)
    f_ref[...] = jnp.concatenate([w[:, FOX_F0:FOX_F0 + N_HEADS], jnp.zeros((rows, LANE - N_HEADS), F32)],
                                 axis=1).astype(BF16)
    gate_ref[...] = w[:, N_MIX_COLS + N_HEADS:].astype(BF16)


def _mixer_weights(w_in, layer, tr=128):
    _, d, cols = w_in.shape
    row = lambda w: pl.BlockSpec((tr, w), lambda i: (i, 0))
    return pl.pallas_call(
        _split_w_in_kernel,
        grid=(d // tr,),
        in_specs=[pl.BlockSpec((None, tr, cols), lambda i: (layer, i, 0))],
        out_specs=[row(N_MIX_COLS), row(LANE), row(N_BRANCH * D_MODEL)],
        out_shape=[jax.ShapeDtypeStruct((d, N_MIX_COLS), BF16), jax.ShapeDtypeStruct((d, LANE), BF16),
                   jax.ShapeDtypeStruct((d, N_BRANCH * D_MODEL), BF16)],
        compiler_params=_params("parallel"),
        name="split_w_in",
    )(w_in)


def kernel(x, w_in, w_branch, w_out, norm_mix_g, hgrn_lb_logits, hgrn_norm_g, fox_f_bias, pool_w, pool_scale,
           ret_gn_g, ret_gn_b, norm_ffn_g, ffn_w_gate, ffn_w_up, ffn_w_down, moe_router, moe_w_gate, moe_w_up,
           moe_w_down, final_norm_g):
    b, s, d = x.shape
    assert b == 1 and d == D_MODEL
    depth = w_in.shape[0]
    assert depth == 2, "layer 0 uses the dense FFN, layer 1 the experts and the final norm"
    xs = x.reshape(s, d)
    h = rmsnorm_bf16(xs, norm_mix_g[0])
    out = None
    for layer in range(depth):
        w_mix, w_f, w_gate = _mixer_weights(w_in, layer)
        proj = matmul(h, w_mix)
        ct, c_rows, fox_stats = fox_gate(h, proj, w_f, fox_f_bias[layer])
        branches = (
            hgrn2(proj, hgrn_lb_logits, hgrn_norm_g[layer], layer),
            fox_attention(proj, ct, c_rows, fox_stats),
            pool_mixer(proj, pool_w[layer], pool_scale[layer]),
            retention(proj, ret_gn_g[layer], ret_gn_b[layer]),
        )
        xs, h2 = merge(h, branches, xs, w_gate, w_branch[layer].astype(BF16), w_out[layer].astype(BF16),
                       norm_ffn_g[layer])
        if layer % 2 == 0:
            li = layer // 2
            xs, h = ffn_dense(h2, xs, ffn_w_gate[li], ffn_w_up[li], ffn_w_down[li], norm_mix_g[layer + 1])
        else:
            li = layer // 2
            out = moe_sparse(h2, xs, norm_ffn_g[layer], moe_router[li], moe_w_gate[li].astype(BF16),
                             moe_w_up[li].astype(BF16), moe_w_down[li], final_norm_g)
    return out.reshape(b, s, d)
```

```python
import functools
import math

import jax
import jax.numpy as jnp
import numpy as np
from jax import lax
from jax.experimental import pallas as pl
from jax.experimental.pallas import tpu as pltpu

D_MODEL = 1024
N_BRANCH = 4
BRANCH_W = D_MODEL // N_BRANCH
HEAD_DIM = 64
N_HEADS = BRANCH_W // HEAD_DIM
POOL_WINDOWS = (2, 4, 8, 16)
POOL_GROUP = BRANCH_W // len(POOL_WINDOWS)
POOL_HALO = 16
RET_DECAY_BASE = 5.0
ROPE_BASE = 10000.0
D_FF = 7 * D_MODEL // 2
N_EXPERTS = 8
RMS_EPS = 1e-6
LN_EPS = 1e-5
N_MIX_COLS = 12 * BRANCH_W

LANE = 128
SUBLANE = 8
VMEM_LIMIT = 56 * 1024 * 1024

HG_CHUNK = 64
HG_SUB = 16
HG_GROUP = 1
HG_FAST_MIN_LOGDECAY = -60.0
RET_CHUNK = 256

F32 = jnp.float32
BF16 = jnp.bfloat16
NT_DIMS = (((1,), (1,)), ((), ()))


def _params(*sem):
    return pltpu.CompilerParams(dimension_semantics=sem, vmem_limit_bytes=VMEM_LIMIT)


def _const_spec(shape):
    nd = len(shape)
    return pl.BlockSpec(shape, lambda *_: (0,) * nd, pipeline_mode=pl.Buffered(1))


def _split3(x):
    hi = x.astype(BF16)
    r1 = x - hi.astype(F32)
    mid = r1.astype(BF16)
    lo = (r1 - mid.astype(F32)).astype(BF16)
    return hi, mid, lo


def _dot(a, b):
    return jnp.dot(a, b, preferred_element_type=F32)


def _dot_nt(a, b):
    return lax.dot_general(a, b, NT_DIMS, preferred_element_type=F32)


def _dot_exact_rhs(x, m_bf16):
    hi, mid, lo = _split3(x)
    return _dot(hi, m_bf16) + _dot(mid, m_bf16) + _dot(lo, m_bf16)


def _dot_exact_lhs(m_bf16, x):
    hi, mid, lo = _split3(x)
    return _dot(m_bf16, hi) + _dot(m_bf16, mid) + _dot(m_bf16, lo)


def _sigmoid(x):
    return 1.0 / (1.0 + jnp.exp(-x))


def _silu(x):
    return x * _sigmoid(x)


def _rms(x, gain):
    return x * lax.rsqrt(jnp.mean(x * x, axis=-1, keepdims=True) + RMS_EPS) * gain


def _const(a, dtype=F32):
    return jnp.asarray(np.asarray(a, np.float32), dtype)


def _head_of(n):
    return np.arange(n) // HEAD_DIM


def _head_ones():
    h = _head_of(BRANCH_W)
    return _const(h[:, None] == h[None, :], BF16)


def _head_masks():
    return _const(_head_of(BRANCH_W)[None, :] == np.arange(N_HEADS)[:, None])


def _rmsnorm_kernel(x_ref, g_ref, o_ref):
    o_ref[...] = _rms(x_ref[...], g_ref[...]).astype(o_ref.dtype)


def rmsnorm_bf16(x, gain, tm=1024):
    s, d = x.shape
    tm = min(tm, s)
    return pl.pallas_call(
        _rmsnorm_kernel,
        grid=(s // tm,),
        in_specs=[pl.BlockSpec((tm, d), lambda i: (i, 0)), _const_spec((1, d))],
        out_specs=pl.BlockSpec((tm, d), lambda i: (i, 0)),
        out_shape=jax.ShapeDtypeStruct((s, d), BF16),
        compiler_params=_params("parallel"),
        name="rmsnorm",
    )(x, gain.reshape(1, d))


def _matmul_kernel(a_ref, b_ref, o_ref):
    o_ref[...] = _dot(a_ref[...], b_ref[...]).astype(o_ref.dtype)


def matmul(a, b, out_dtype=BF16, tm=2048, tn=1024):
    m, k = a.shape
    _, n = b.shape
    tm = min(tm, m)
    return pl.pallas_call(
        _matmul_kernel,
        grid=(n // tn, m // tm),
        in_specs=[pl.BlockSpec((tm, k), lambda j, i: (i, 0)),
                  pl.BlockSpec((k, tn), lambda j, i: (0, j))],
        out_specs=pl.BlockSpec((tm, tn), lambda j, i: (i, j)),
        out_shape=jax.ShapeDtypeStruct((m, n), out_dtype),
        compiler_params=_params("parallel", "parallel"),
        name="in_proj",
    )(a, b)


def _hgrn_kernel(q_ref, f_ref, i_ref, g_ref, lbl_ref, ng_ref, ones_ref, hm_ref, tril_ref, halfsum_ref, fmask_ref,
                 o_ref, st_ref, bpad, kpad, vpad, astack, lf_sc, kk_sc, o_sc, *, layer, tile):
    c, sub = HG_CHUNK, HG_SUB
    nsub = c // sub
    half = c // 2

    @pl.when(pl.program_id(0) == 0)
    def _():
        st_ref[...] = jnp.zeros_like(st_ref)
        bpad[...] = jnp.zeros_like(bpad)
        kpad[...] = jnp.zeros_like(kpad)
        vpad[...] = jnp.zeros_like(vpad)

    lbl = lbl_ref[...]
    e = jnp.exp(lbl - jnp.max(lbl, axis=0, keepdims=True))
    p = e / jnp.sum(e, axis=0, keepdims=True)
    lb = jnp.zeros((1, BRANCH_W), F32)
    for l in range(1, layer + 1):
        lb = lb + p[l:l + 1, :]

    ones_bd = ones_ref[...]
    hm = hm_ref[...]
    tril = tril_ref[...]
    row = lax.broadcasted_iota(jnp.int32, (c, 1), 0)
    row_in_sub = row % sub
    bd_mask = ones_bd.astype(F32)

    sig = _sigmoid(f_ref[...].astype(F32))
    logf_all = jnp.log(lb + (1.0 - lb) * sig)
    lf_sc[...] = _dot_exact_lhs(tril, logf_all)
    kk_sc[...] = (1.0 - lb) * (1.0 - sig)
    min_decay = jnp.min(_dot(halfsum_ref[...], logf_all.astype(BF16)))

    def load(ci):
        r0 = pl.multiple_of(ci * c, c)
        q = q_ref[pl.ds(r0, c), :].astype(F32)
        v = i_ref[pl.ds(r0, c), :].astype(F32)
        kk = kk_sc[pl.ds(r0, c), :]
        b = lf_sc[pl.ds(r0, c), :]
        return r0, q, v, kk, b

    def finish(r0, q_decayed, v, kk, b, intra):
        st = st_ref[...]
        inter = _dot_nt(q_decayed.astype(BF16), st.astype(BF16))
        b_last = b[c - 1:c, :]
        ks_end = (kk * jnp.exp(b_last - b)).astype(BF16)
        upd = _dot(v.T.astype(BF16), ks_end)
        st_ref[...] = st * jnp.exp(b_last) + upd * bd_mask
        o_sc[pl.ds(r0, c), :] = intra + inter

    def fast_group(gi, carry):
        second = row >= half
        chunks, qxs, kaugs, vaugs = [], [], [], []
        for j in range(HG_GROUP):
            r0, q, v, kk, b = load(gi * HG_GROUP + j)
            m_row = b[half - 1:half, :]
            mref = jnp.where(second, m_row, 0.0)
            qp = q * jnp.exp(b - mref)
            kp = kk * jnp.exp(mref - b)
            e_m = jnp.exp(m_row)
            kaugs.append(jnp.concatenate([kp, kp[:half, :] * e_m], axis=0))
            vaugs.append(jnp.concatenate([v, v[:half, :]], axis=0))
            qxs.extend(qp * hm[h:h + 1, :] for h in range(N_HEADS))
            chunks.append((r0, jnp.where(second, qp * e_m, qp), v, kk, b))
        qx = jnp.concatenate(qxs, axis=0).astype(BF16)
        kaug = jnp.concatenate(kaugs, axis=0).astype(BF16)
        vaug = jnp.concatenate(vaugs, axis=0).astype(BF16)
        sc = jnp.where(fmask_ref[...] > 0.0, _dot_nt(qx, kaug), 0.0)
        r = _dot(sc.astype(BF16), vaug)
        for j, (r0, q_decayed, v, kk, b) in enumerate(chunks):
            intra = jnp.zeros((c, BRANCH_W), F32)
            for h in range(N_HEADS):
                lo = (j * N_HEADS + h) * c
                intra = intra + r[lo:lo + c, :] * hm[h:h + 1, :]
            finish(r0, q_decayed, v, kk, b, intra)
        return carry

    def exact_chunk(ci, carry):
        r0, q, v, kk, b = load(ci)

        bpad[pl.ds(sub, c), :] = b
        kpad[pl.ds(sub, c), :] = kk
        vpad[pl.ds(sub, c), :] = v

        for d in range(sub):
            b_d = bpad[pl.ds(sub - d, c), :]
            k_d = kpad[pl.ds(sub - d, c), :]
            a = jnp.where(row_in_sub >= d, q * k_d * jnp.exp(b - b_d), 0.0)
            astack[pl.ds(d * c, c), :] = a.astype(BF16)
        pall = _dot(astack[...], ones_bd)
        intra = jnp.zeros((c, BRANCH_W), F32)
        for d in range(sub):
            intra = intra + pall[d * c:(d + 1) * c, :] * vpad[pl.ds(sub - d, c), :]

        pieces = [jnp.zeros((sub, BRANCH_W), F32)]
        for si in range(1, nsub):
            lo = si * sub
            m_i = b[lo - 1:lo, :]
            qs = q[lo:lo + sub, :] * jnp.exp(b[lo:lo + sub, :] - m_i)
            ks = (kk[:lo, :] * jnp.exp(m_i - b[:lo, :])).astype(BF16)
            qx = jnp.concatenate([qs * hm[h:h + 1, :] for h in range(N_HEADS)], axis=0).astype(BF16)
            sc = _dot_nt(qx, ks)
            r = _dot(sc.astype(BF16), v[:lo, :].astype(BF16))
            acc = jnp.zeros((sub, BRANCH_W), F32)
            for h in range(N_HEADS):
                acc = acc + r[h * sub:(h + 1) * sub, :] * hm[h:h + 1, :]
            pieces.append(acc)
        intra = intra + jnp.concatenate(pieces, axis=0)
        finish(r0, q * jnp.exp(b), v, kk, b, intra)
        return carry

    lax.cond(min_decay >= HG_FAST_MIN_LOGDECAY,
             lambda: lax.fori_loop(0, tile // (c * HG_GROUP), fast_group, 0, unroll=2),
             lambda: lax.fori_loop(0, tile // c, exact_chunk, 0))

    o = o_sc[...]
    ms = _dot_exact_rhs(o * o, ones_bd) * (1.0 / HEAD_DIM)
    y = o * lax.rsqrt(ms + RMS_EPS) * ng_ref[...] * _silu(g_ref[...].astype(F32))
    o_ref[...] = y.astype(o_ref.dtype)


def hgrn2(proj, lb_logits, norm_g, layer, tile=512):
    s = proj.shape[0]
    depth = lb_logits.shape[0]
    c, sub = HG_CHUNK, HG_SUB
    half = c // 2
    tile = min(tile, s)
    col = lambda j: pl.BlockSpec((tile, BRANCH_W), lambda i, j=j: (i, j))
    pos = np.arange(tile)
    tril = _const((pos[:, None] // c == pos[None, :] // c) & (pos[None, :] <= pos[:, None]), BF16)
    nhalf = tile // half
    halfsum = _const(np.arange(tile)[None, :] // half == np.arange(nhalf)[:, None], BF16)
    t = np.arange(c)[:, None]
    col_s = np.arange(c + half)[None, :]
    same_half = (col_s < c) & (col_s // half == t // half) & (col_s <= t)
    cross = (col_s >= c) & (t >= half)
    fmask1 = np.tile((same_half | cross).astype(np.float32), (N_HEADS, 1))
    fmask = _const(np.kron(np.eye(HG_GROUP, dtype=np.float32), fmask1))
    return pl.pallas_call(
        functools.partial(_hgrn_kernel, layer=layer, tile=tile),
        grid=(s // tile,),
        in_specs=[col(0), col(1), col(2), col(3),
                  _const_spec((depth, BRANCH_W)), _const_spec((1, BRANCH_W)),
                  _const_spec((BRANCH_W, BRANCH_W)), _const_spec((N_HEADS, BRANCH_W)),
                  _const_spec((tile, tile)), _const_spec((nhalf, tile)),
                  _const_spec((HG_GROUP * N_HEADS * c, HG_GROUP * (c + half)))],
        out_specs=pl.BlockSpec((tile, BRANCH_W), lambda i: (i, 0)),
        out_shape=jax.ShapeDtypeStruct((s, BRANCH_W), BF16),
        scratch_shapes=[pltpu.VMEM((BRANCH_W, BRANCH_W), F32),
                        pltpu.VMEM((c + sub, BRANCH_W), F32),
                        pltpu.VMEM((c + sub, BRANCH_W), F32),
                        pltpu.VMEM((c + sub, BRANCH_W), F32),
                        pltpu.VMEM((sub * c, BRANCH_W), BF16),
                        pltpu.VMEM((tile, BRANCH_W), F32),
                        pltpu.VMEM((tile, BRANCH_W), F32),
                        pltpu.VMEM((tile, BRANCH_W), F32)],
        compiler_params=_params("arbitrary"),
        name="hgrn2",
    )(proj, proj, proj, proj, lb_logits.astype(F32), norm_g.reshape(1, BRANCH_W).astype(F32),
      _head_ones(), _head_masks(), tril, halfsum, fmask)


FOX_TILE = 256
FOX_GATE_BLOCKS = 2
FOX_NSTAT = 16
FOX_FIXED_MAX = 30.0
FOX_SKIP_LOG = 40.0


def _fox_gate_kernel(h_ref, q_ref, k_ref, wf_ref, bias_ref, tril_ref, ones_ref,
                     ct_ref, c_ref, stat_ref, carry_ref, kmax_ref):
    @pl.when(pl.program_id(0) == 0)
    def _():
        carry_ref[...] = jnp.zeros_like(carry_ref)
        kmax_ref[...] = jnp.zeros_like(kmax_ref)

    logit = _dot(h_ref[...], wf_ref[...]) + bias_ref[...]
    logf = jnp.minimum(logit, 0.0) - jnp.log(1.0 + jnp.exp(-jnp.abs(logit)))
    cum = _dot_exact_lhs(tril_ref[...], logf) + carry_ref[...]
    carry_ref[...] = cum[-1:, :]
    ct_ref[...] = cum.T[:SUBLANE, :]
    c_ref[...] = cum

    ones_bd = ones_ref[...]
    q = q_ref[...].astype(F32)
    k = k_ref[...].astype(F32)
    scale = HEAD_DIM ** -0.5
    head_lane = lax.broadcasted_iota(jnp.int32, (1, BRANCH_W), 1) // HEAD_DIM
    c_heads = jnp.zeros(q.shape, F32)
    for h in range(N_HEADS):
        c_heads = jnp.where(head_lane == h, cum[:, h:h + 1], c_heads)
    slack = 1.0 + 2.0 ** -6
    qn = jnp.sqrt(_dot((q * q).astype(BF16), ones_bd)) * (scale * slack)
    kn = jnp.sqrt(_dot((k * k).astype(BF16), ones_bd)) * slack
    diag = _dot((q * k).astype(BF16), ones_bd) * scale - (2.0 ** -6) * qn * kn
    e_row = c_heads - diag
    kmax = kmax_ref[...]
    for blk in range(q.shape[0] // FOX_TILE):
        rs = slice(blk * FOX_TILE, (blk + 1) * FOX_TILE)
        kmax = jnp.maximum(kmax, jnp.max(kn[rs], axis=0, keepdims=True))
        rows = [jnp.max(qn[rs], axis=0, keepdims=True),
                jnp.max(e_row[rs], axis=0, keepdims=True),
                kmax,
                c_heads[(blk + 1) * FOX_TILE - 1:(blk + 1) * FOX_TILE, :]]
        stat_ref[blk] = jnp.concatenate(rows + [jnp.zeros((SUBLANE - len(rows), BRANCH_W), F32)], axis=0)
    kmax_ref[...] = kmax


def fox_gate(h, proj, w_f, f_bias):
    s, d = h.shape
    tile = min(FOX_GATE_BLOCKS * FOX_TILE, s)
    nblk = tile // FOX_TILE
    bias = jnp.zeros((1, LANE), F32).at[0, :N_HEADS].set(f_bias.astype(F32))
    tril = _const(np.tril(np.ones((tile, tile))), BF16)
    ct, c_rows, stats = pl.pallas_call(
        _fox_gate_kernel,
        grid=(s // tile,),
        in_specs=[pl.BlockSpec((tile, d), lambda i: (i, 0)),
                  pl.BlockSpec((tile, BRANCH_W), lambda i: (i, 4)),
                  pl.BlockSpec((tile, BRANCH_W), lambda i: (i, 5)),
                  _const_spec((d, LANE)), _const_spec((1, LANE)), _const_spec((tile, tile)),
                  _const_spec((BRANCH_W, BRANCH_W))],
        out_specs=[pl.BlockSpec((SUBLANE, tile), lambda i: (0, i)),
                   pl.BlockSpec((tile, LANE), lambda i: (i, 0)),
                   pl.BlockSpec((nblk, SUBLANE, BRANCH_W), lambda i: (i, 0, 0))],
        out_shape=[jax.ShapeDtypeStruct((SUBLANE, s), F32),
                   jax.ShapeDtypeStruct((s, LANE), F32),
                   jax.ShapeDtypeStruct((s // FOX_TILE, SUBLANE, BRANCH_W), F32)],
        scratch_shapes=[pltpu.VMEM((1, LANE), F32), pltpu.VMEM((1, BRANCH_W), F32)],
        compiler_params=_params("arbitrary"),
        name="fox_gate",
    )(h, proj, proj, w_f, bias, tril, _head_ones())
    return ct, c_rows, stats[:, :4, ::HEAD_DIM].reshape(-1)


def _fox_kernel(stat_ref, q_ref, k_ref, v_ref, ct_ref, c_ref, hm_ref, o_ref, m_sc, l_sc, acc_sc, *, tq):
    i = pl.program_id(0)
    q0 = pl.multiple_of(i * tq, tq)
    hm = hm_ref[...]
    q = q_ref[...].astype(F32) * (HEAD_DIM ** -0.5)
    qh = [(q * hm[h:h + 1, :]).astype(BF16) for h in range(N_HEADS)]
    c_q0 = ct_ref[:, pl.ds(q0, tq)][:, 0:1]

    first = i
    for h in range(N_HEADS):
        qmax = stat_ref[i * FOX_NSTAT + h]
        emax = stat_ref[i * FOX_NSTAT + N_HEADS + h]

        def needed(j, h=h, qmax=qmax, emax=emax):
            jc = jnp.maximum(j, 0)
            bound = (qmax * stat_ref[jc * FOX_NSTAT + 2 * N_HEADS + h] + emax
                     - stat_ref[jc * FOX_NSTAT + 3 * N_HEADS + h])
            return (j >= 0) & (bound >= -FOX_SKIP_LOG)

        last_dropped = lax.while_loop(needed, lambda j: j - 1, i - 1)
        first = jnp.minimum(first, last_dropped + 1)

    l_sc[...] = jnp.zeros_like(l_sc)
    acc_sc[...] = jnp.zeros_like(acc_sc)

    def causal(sc):
        r = lax.broadcasted_iota(jnp.int32, (tq, tq), 0)
        cidx = lax.broadcasted_iota(jnp.int32, (tq, tq), 1)
        return jnp.where(cidx <= r, sc, -jnp.inf)

    def online_block(s0, diagonal):
        kb = k_ref[pl.ds(s0, tq), :]
        vb = v_ref[pl.ds(s0, tq), :]
        bias = c_q0 - ct_ref[:, pl.ds(s0, tq)]
        for h in range(N_HEADS):
            sc = _dot_nt(qh[h], kb) + bias[h:h + 1, :]
            if diagonal:
                sc = causal(sc)
            m_prev = m_sc[h]
            m_new = jnp.maximum(m_prev, jnp.max(sc, axis=1, keepdims=True))
            alpha = jnp.exp(m_prev - m_new)
            p = jnp.exp(sc - jnp.tile(m_new, (1, tq // LANE)))
            l_sc[h] = alpha * l_sc[h] + jnp.sum(p, axis=1, keepdims=True)
            acc_sc[h] = acc_sc[h] * jnp.tile(alpha, (1, BRANCH_W // LANE)) + _dot(p.astype(BF16), vb)
            m_sc[h] = m_new

    tops = [stat_ref[i * FOX_NSTAT + h] * stat_ref[i * FOX_NSTAT + 2 * N_HEADS + h] for h in range(N_HEADS)]
    c_tile = c_ref[...]
    shift = [c_tile[:, h:h + 1] - c_q0[h:h + 1, :] - tops[h] for h in range(N_HEADS)]

    def fixed_block(s0, diagonal):
        kb = k_ref[pl.ds(s0, tq), :]
        vb = v_ref[pl.ds(s0, tq), :]
        bias = c_q0 - ct_ref[:, pl.ds(s0, tq)]
        for h in range(N_HEADS):
            sc = _dot_nt(qh[h], kb) + bias[h:h + 1, :] + shift[h]
            if diagonal:
                sc = causal(sc)
            p = jnp.exp(sc)
            l_sc[h] += p[:, :LANE] + p[:, LANE:]
            acc_sc[h] += _dot(p.astype(BF16), vb)

    def run(block, row_sum):
        lax.fori_loop(first, i, lambda j, carry: (block(pl.multiple_of(j * tq, tq), False), carry)[1], 0)
        block(q0, True)
        out = jnp.zeros((tq, BRANCH_W), F32)
        for h in range(N_HEADS):
            out = out + acc_sc[h] * hm[h:h + 1, :] / row_sum(l_sc[h])
        o_ref[...] = out.astype(o_ref.dtype)

    def run_online():
        m_sc[...] = jnp.full_like(m_sc, -jnp.inf)
        run(online_block, lambda l: jnp.tile(l, (1, BRANCH_W // LANE)))

    def run_fixed():
        run(fixed_block, lambda l: jnp.sum(l, axis=1, keepdims=True))

    lax.cond(functools.reduce(jnp.maximum, tops) <= FOX_FIXED_MAX, run_fixed, run_online)


def fox_attention(proj, ct, c_rows, stats):
    s = proj.shape[0]
    tq = min(FOX_TILE, s)
    full = lambda j: pl.BlockSpec((s, BRANCH_W), lambda i, j=j: (0, j), pipeline_mode=pl.Buffered(1))
    return pl.pallas_call(
        functools.partial(_fox_kernel, tq=tq),
        grid=(s // tq,),
        in_specs=[pl.BlockSpec(memory_space=pltpu.SMEM),
                  pl.BlockSpec((tq, BRANCH_W), lambda i: (i, 4)), full(5), full(6),
                  _const_spec((SUBLANE, s)), pl.BlockSpec((tq, LANE), lambda i: (i, 0)),
                  _const_spec((N_HEADS, BRANCH_W))],
        out_specs=pl.BlockSpec((tq, BRANCH_W), lambda i: (i, 0)),
        out_shape=jax.ShapeDtypeStruct((s, BRANCH_W), BF16),
        scratch_shapes=[pltpu.VMEM((N_HEADS, tq, LANE), F32),
                        pltpu.VMEM((N_HEADS, tq, LANE), F32),
                        pltpu.VMEM((N_HEADS, tq, BRANCH_W), F32)],
        compiler_params=_params("parallel"),
        name="fox_attention",
    )(stats, proj, proj, proj, ct, c_rows, _head_masks())


def _pool_kernel(u_ref, w_ref, scale_ref, o_ref, ext, *, tile):
    i = pl.program_id(0)

    @pl.when(i == 0)
    def _():
        ext[pl.ds(0, POOL_HALO), :] = jnp.zeros((POOL_HALO, BRANCH_W), F32)

    u = u_ref[...].astype(F32)
    ext[pl.ds(POOL_HALO, tile), :] = u
    pos = (i * tile + lax.broadcasted_iota(jnp.int32, (tile, 1), 0) + 1).astype(F32)
    halves = []
    for half in range(BRANCH_W // LANE):
        lanes = pl.ds(half * LANE, LANE)
        w_small, w_big = POOL_WINDOWS[2 * half], POOL_WINDOWS[2 * half + 1]
        run = u[:, half * LANE:(half + 1) * LANE]
        sums = {}
        for j in range(1, w_big):
            if j == w_small:
                sums[w_small] = run
            run = run + ext[pl.ds(POOL_HALO - j, tile), lanes]
        sums[w_big] = run
        lane = lax.broadcasted_iota(jnp.int32, (1, LANE), 1)
        small = lane < POOL_GROUP
        total = jnp.where(small, sums[w_small], sums[w_big])
        count = jnp.where(small, jnp.minimum(pos, float(w_small)), jnp.minimum(pos, float(w_big)))
        halves.append(total / count)
    mean = jnp.concatenate(halves, axis=1)
    d = (mean - u).astype(BF16)
    y = _dot(d, w_ref[...]) * scale_ref[...]
    o_ref[...] = y.astype(o_ref.dtype)
    ext[pl.ds(0, POOL_HALO), :] = u[tile - POOL_HALO:, :]


def pool_mixer(proj, w_pool, scale, tile=512):
    s = proj.shape[0]
    tile = min(tile, s)
    ng = len(POOL_WINDOWS)
    w_bd = jnp.zeros((BRANCH_W, BRANCH_W), F32)
    for gi in range(ng):
        lo = gi * POOL_GROUP
        w_bd = w_bd.at[lo:lo + POOL_GROUP, lo:lo + POOL_GROUP].set(w_pool[gi].astype(F32))
    return pl.pallas_call(
        functools.partial(_pool_kernel, tile=tile),
        grid=(s // tile,),
        in_specs=[pl.BlockSpec((tile, BRANCH_W), lambda i: (i, 7)),
                  _const_spec((BRANCH_W, BRANCH_W)), _const_spec((1, BRANCH_W))],
        out_specs=pl.BlockSpec((tile, BRANCH_W), lambda i: (i, 0)),
        out_shape=jax.ShapeDtypeStruct((s, BRANCH_W), BF16),
        scratch_shapes=[pltpu.VMEM((tile + POOL_HALO, BRANCH_W), F32)],
        compiler_params=_params("arbitrary"),
        name="pool_mixer",
    )(proj, w_bd.astype(BF16), scale.reshape(1, BRANCH_W).astype(F32))


def _ret_kernel(q_ref, k_ref, v_ref, g_ref, rope_ref, ecos_ref, esin_ref, perm_ref, ones_ref, hm_ref,
                dstack_ref, xi_ref, zeta_ref, gc_ref, gng_ref, gnb_ref, o_ref, st_ref, *, tile):
    c = RET_CHUNK

    @pl.when(pl.program_id(0) == 0)
    def _():
        st_ref[...] = jnp.zeros_like(st_ref)

    perm = perm_ref[...]
    ones_bd = ones_ref[...]
    bd_mask = ones_bd.astype(F32)
    hm = hm_ref[...]

    tab = rope_ref[...]
    cos = _dot_exact_rhs(tab, ecos_ref[...])
    sin = _dot_exact_rhs(tab, esin_ref[...])
    q_all = q_ref[...]
    k_all = k_ref[...]
    qr_all = q_all.astype(F32) * cos + _dot(q_all, perm) * sin
    kr_all = (k_all.astype(F32) * cos + _dot(k_all, perm) * sin) * (HEAD_DIM ** -0.5)

    outs = []
    for ci in range(tile // c):
        r0 = ci * c
        qr = qr_all[r0:r0 + c, :]
        kr = kr_all[r0:r0 + c, :]
        v = v_ref[pl.ds(r0, c), :]

        qx = jnp.concatenate([qr * hm[h:h + 1, :] for h in range(N_HEADS)], axis=0).astype(BF16)
        sc = _dot_nt(qx, kr.astype(BF16)) * dstack_ref[...]
        r = _dot(sc.astype(BF16), v)
        intra = jnp.zeros((c, BRANCH_W), F32)
        for h in range(N_HEADS):
            intra = intra + r[h * c:(h + 1) * c, :] * hm[h:h + 1, :]

        st = st_ref[...]
        inter = _dot_nt((qr * xi_ref[...]).astype(BF16), st.astype(BF16))
        upd = _dot(v.astype(F32).T.astype(BF16), (kr * zeta_ref[...]).astype(BF16))
        st_ref[...] = st * gc_ref[...] + upd * bd_mask
        outs.append(intra + inter)

    o = jnp.concatenate(outs, axis=0)
    mu = _dot_exact_rhs(o, ones_bd) * (1.0 / HEAD_DIM)
    cen = o - mu
    var = _dot_exact_rhs(cen * cen, ones_bd) * (1.0 / HEAD_DIM)
    y = cen * lax.rsqrt(var + LN_EPS) * gng_ref[...] + gnb_ref[...]
    o_ref[...] = (y * _silu(g_ref[...].astype(F32))).astype(o_ref.dtype)


def _rope_tables(s):
    half = HEAD_DIM // 2
    pos = np.arange(s, dtype=np.float64)
    inv_freq = ROPE_BASE ** (-np.arange(half, dtype=np.float64) / half)
    ang = pos[:, None] * inv_freq[None, :]
    table = np.concatenate([np.cos(ang), np.sin(ang), np.zeros((s, LANE - 2 * half))], axis=1)
    lane = np.arange(BRANCH_W)
    src = np.arange(LANE)[:, None]
    ecos = src == (lane % half)[None, :]
    sign = np.where(lane % HEAD_DIM < half, -1.0, 1.0)
    esin = (src == (half + lane % half)[None, :]) * sign[None, :]
    return _const(table), _const(ecos, BF16), _const(esin, BF16)


def _ret_constants():
    c = RET_CHUNK
    half = HEAD_DIM // 2
    lane = np.arange(BRANCH_W)
    partner = np.where(lane % HEAD_DIM < half, lane + half, lane - half)
    perm = lane[:, None] == partner[None, :]
    log_gamma = np.log1p(-np.exp2(-RET_DECAY_BASE - np.arange(N_HEADS, dtype=np.float64)))
    ci = np.arange(c, dtype=np.float64)
    diff = ci[:, None] - ci[None, :]
    intra = np.where(diff >= 0, np.exp(diff * log_gamma[:, None, None]), 0.0)
    dstack = intra.reshape(N_HEADS * c, c)
    lg_lane = np.repeat(log_gamma, HEAD_DIM)[None, :]
    xi = np.exp((ci[:, None] + 1.0) * lg_lane)
    zeta = np.exp((c - 1.0 - ci[:, None]) * lg_lane)
    gc = np.exp(c * lg_lane)
    return _const(perm, BF16), _const(dstack), _const(xi), _const(zeta), _const(gc)


def retention(proj, gn_g, gn_b, tile=512):
    s = proj.shape[0]
    c = RET_CHUNK
    tile = min(tile, s)
    rope, ecos, esin = _rope_tables(s)
    perm, dstack, xi, zeta, gc = _ret_constants()
    col = lambda j: pl.BlockSpec((tile, BRANCH_W), lambda i, j=j: (i, j))
    row = pl.BlockSpec((tile, BRANCH_W), lambda i: (i, 0))
    return pl.pallas_call(
        functools.partial(_ret_kernel, tile=tile),
        grid=(s // tile,),
        in_specs=[col(8), col(9), col(10), col(11), pl.BlockSpec((tile, LANE), lambda i: (i, 0)),
                  _const_spec((LANE, BRANCH_W)), _const_spec((LANE, BRANCH_W)),
                  _const_spec((BRANCH_W, BRANCH_W)), _const_spec((BRANCH_W, BRANCH_W)),
                  _const_spec((N_HEADS, BRANCH_W)), _const_spec((N_HEADS * c, c)),
                  _const_spec((c, BRANCH_W)), _const_spec((c, BRANCH_W)), _const_spec((1, BRANCH_W)),
                  _const_spec((1, BRANCH_W)), _const_spec((1, BRANCH_W))],
        out_specs=row,
        out_shape=jax.ShapeDtypeStruct((s, BRANCH_W), BF16),
        scratch_shapes=[pltpu.VMEM((BRANCH_W, BRANCH_W), F32)],
        compiler_params=_params("arbitrary"),
        name="retention",
    )(proj, proj, proj, proj, rope, ecos, esin, perm, _head_ones(), _head_masks(), dstack, xi, zeta, gc,
      gn_g.reshape(1, BRANCH_W).astype(F32), gn_b.reshape(1, BRANCH_W).astype(F32))


def _merge_kernel(h_ref, o0_ref, o1_ref, o2_ref, o3_ref, x_ref, wg_ref, wb_ref, wo_ref, g_ref,
                  xo_ref, ho_ref):
    h = h_ref[...]
    merged = jnp.zeros(x_ref.shape, F32)
    for bi, o_ref in enumerate((o0_ref, o1_ref, o2_ref, o3_ref)):
        gate = _sigmoid(_dot(h, wg_ref[:, bi * D_MODEL:(bi + 1) * D_MODEL]))
        merged = merged + gate * _dot(o_ref[...], wb_ref[bi])
    x_new = x_ref[...] + _dot(merged.astype(BF16), wo_ref[...])
    xo_ref[...] = x_new
    ho_ref[...] = _rms(x_new, g_ref[...]).astype(ho_ref.dtype)


def merge(h, branches, x, w_gate, w_branch, w_out, next_gain, tm=512):
    s, d = x.shape
    tm = min(tm, s)
    row = lambda w: pl.BlockSpec((tm, w), lambda i: (i, 0))
    return pl.pallas_call(
        _merge_kernel,
        grid=(s // tm,),
        in_specs=[row(d), row(BRANCH_W), row(BRANCH_W), row(BRANCH_W), row(BRANCH_W), row(d),
                  _const_spec((d, N_BRANCH * d)), _const_spec((N_BRANCH, BRANCH_W, d)),
                  _const_spec((d, d)), _const_spec((1, d))],
        out_specs=[row(d), row(d)],
        out_shape=[jax.ShapeDtypeStruct((s, d), F32), jax.ShapeDtypeStruct((s, d), BF16)],
        compiler_params=_params("parallel"),
        name="merge",
    )(h, *branches, x, w_gate, w_branch, w_out, next_gain.reshape(1, d).astype(F32))


def _ffn_kernel(h_ref, x_ref, wg_ref, wu_ref, wd_ref, g_ref, xo_ref, ho_ref, acc_ref):
    f = pl.program_id(1)

    @pl.when(f == 0)
    def _():
        acc_ref[...] = jnp.zeros_like(acc_ref)

    h = h_ref[...]
    a = _silu(_dot(h, wg_ref[...].astype(BF16))) * _dot(h, wu_ref[...].astype(BF16))
    acc_ref[...] += _dot(a.astype(BF16), wd_ref[...].astype(BF16))

    @pl.when(f == pl.num_programs(1) - 1)
    def _():
        x_new = x_ref[...] + acc_ref[...]
        xo_ref[...] = x_new
        ho_ref[...] = _rms(x_new, g_ref[...]).astype(ho_ref.dtype)


def ffn_dense(h, x, w_gate, w_up, w_down, next_gain, tm=1024, tf=512):
    s, d = x.shape
    tm = min(tm, s)
    dff = w_gate.shape[1]
    row = lambda: pl.BlockSpec((tm, d), lambda i, f: (i, 0))
    return pl.pallas_call(
        _ffn_kernel,
        grid=(s // tm, dff // tf),
        in_specs=[row(), row(),
                  pl.BlockSpec((d, tf), lambda i, f: (0, f)),
                  pl.BlockSpec((d, tf), lambda i, f: (0, f)),
                  pl.BlockSpec((tf, d), lambda i, f: (f, 0)),
                  _const_spec((1, d))],
        out_specs=[row(), row()],
        out_shape=[jax.ShapeDtypeStruct((s, d), F32), jax.ShapeDtypeStruct((s, d), BF16)],
        scratch_shapes=[pltpu.VMEM((tm, d), F32)],
        compiler_params=_params("parallel", "arbitrary"),
        name="ffn_dense",
    )(h, x, w_gate, w_up, w_down, next_gain.reshape(1, d).astype(F32))


MOE_TOK_TILE = 256
MOE_ROUTE_TILES = 2
MOE_ROW_BLOCK = 512
MOE_GATHER_ROWS = 128
MOE_GATHER_TILES = 4
MOE_VMEM_LIMIT = 60 * 1024 * 1024
MOE_Y_BLOCK = 128
MOE_Y_FETCH = MOE_TOK_TILE // MOE_Y_BLOCK + 1


def _dot_f32(x, w):
    xh, xm, _ = _split3(x)
    wh, wm, _ = _split3(w)
    return _dot(xh, wh) + (_dot(xh, wm) + _dot(xm, wh))


def _route_kernel(x_ref, ng_ref, router_ref, ltri_ref, pos_ref, gate_ref, post_ref, before_ref, total_ref,
                  carry_ref):
    tm = x_ref.shape[0]

    @pl.when(pl.program_id(0) == 0)
    def _():
        carry_ref[...] = jnp.zeros_like(carry_ref)

    hn = _rms(x_ref[...], ng_ref[...])
    logits = _dot_f32(hn, router_ref[...])
    lane = lax.broadcasted_iota(jnp.int32, (tm, LANE), 1)
    logits = jnp.where(lane < N_EXPERTS, logits, -jnp.inf)
    v1 = jnp.max(logits, axis=1, keepdims=True)
    i1 = jnp.min(jnp.where(logits == v1, lane, LANE), axis=1, keepdims=True)
    rest = jnp.where(lane == i1, -jnp.inf, logits)
    v2 = jnp.max(rest, axis=1, keepdims=True)
    i2 = jnp.min(jnp.where(rest == v2, lane, LANE), axis=1, keepdims=True)
    w1 = 1.0 / (1.0 + jnp.exp(v2 - v1))
    gate_ref[...] = jnp.where(lane == i1, w1, 0.0) + jnp.where(lane == i2, 1.0 - w1, 0.0)

    member = jnp.where((lane == i1) | (lane == i2), 1.0, 0.0)
    carry = carry_ref[...]
    rank = _dot(ltri_ref[...], member.astype(BF16)) + carry
    pos = jnp.where(member > 0.0, rank, -1.0)
    pos_ref[...] = pos
    post_ref[...] = pos.T[:SUBLANE, :]
    for blk in range(tm // MOE_TOK_TILE):
        before_ref[blk] = carry
        carry = carry + jnp.sum(member[blk * MOE_TOK_TILE:(blk + 1) * MOE_TOK_TILE], axis=0, keepdims=True)
    carry_ref[...] = carry
    total_ref[...] = carry


def moe_route(x, norm_gain, router):
    s, d = x.shape
    tm = min(MOE_ROUTE_TILES * MOE_TOK_TILE, s)
    nsub = tm // MOE_TOK_TILE
    nt = s // tm
    router_p = jnp.zeros((d, LANE), F32).at[:, :N_EXPERTS].set(router.astype(F32))
    ltri = _const(np.tril(np.ones((tm, tm)), -1), BF16)
    row = pl.BlockSpec((tm, LANE), lambda i: (i, 0))
    return pl.pallas_call(
        _route_kernel,
        grid=(nt,),
        in_specs=[pl.BlockSpec((tm, d), lambda i: (i, 0)), _const_spec((1, d)), _const_spec((d, LANE)),
                  _const_spec((tm, tm))],
        out_specs=[row, row, pl.BlockSpec((SUBLANE, tm), lambda i: (0, i)),
                   pl.BlockSpec((nsub, 1, LANE), lambda i: (i, 0, 0)), pl.BlockSpec((1, LANE), lambda i: (0, 0))],
        out_shape=[jax.ShapeDtypeStruct((s, LANE), F32), jax.ShapeDtypeStruct((s, LANE), F32),
                   jax.ShapeDtypeStruct((SUBLANE, s), F32), jax.ShapeDtypeStruct((nt * nsub, 1, LANE), F32),
                   jax.ShapeDtypeStruct((1, LANE), F32)],
        scratch_shapes=[pltpu.VMEM((1, LANE), F32)],
        compiler_params=_params("arbitrary"),
        name="moe_route",
    )(x, norm_gain.reshape(1, d).astype(F32), router_p, ltri)


def _moe_ffn_kernel(be_ref, r0_ref, tlo_ref, thi_ref, nv_ref, h_ref, post_ref, wg_ref, wu_ref, wd_ref,
                    y_ref, x_sc, acc_sc, *, tt, nsub):
    b = pl.program_id(0)
    f = pl.program_id(1)
    nb = pl.num_programs(0)
    valid = b < nv_ref[0]
    tmb, d = acc_sc.shape
    gr = tmb // nsub
    last_tile = h_ref.shape[0] // tt - 1

    def picked(e, want, t):
        t0 = pl.multiple_of(t * tt, tt)
        p = post_ref[pl.ds(e, 1), pl.ds(t0, tt)]
        sel = jnp.where(p == want, 1.0, 0.0).astype(BF16)
        return _dot(sel, h_ref[pl.ds(t0, tt), :])

    def gather_head(blk, sb):
        e = be_ref[blk]
        lo = tlo_ref[blk * nsub + sb]
        hi = thi_ref[blk * nsub + sb]
        want = (lax.broadcasted_iota(jnp.int32, (gr, 1), 0) + (r0_ref[blk] + sb * gr)).astype(F32)
        rows = picked(e, want, lo)
        for k in range(1, MOE_GATHER_TILES):
            rows = rows + picked(e, jnp.where(lo + k <= hi, want, -2.0), jnp.minimum(lo + k, last_tile))
        return rows.astype(BF16)

    def gather_tail(slot, blk, sb):
        e = be_ref[blk]
        want = (lax.broadcasted_iota(jnp.int32, (gr, 1), 0) + (r0_ref[blk] + sb * gr)).astype(F32)
        rows = pl.ds(pl.multiple_of(sb * gr, gr), gr)

        def more(t, carry):
            x_sc[slot, rows, :] = (x_sc[slot, rows, :].astype(F32) + picked(e, want, t)).astype(BF16)
            return carry

        lax.fori_loop(tlo_ref[blk * nsub + sb] + MOE_GATHER_TILES, thi_ref[blk * nsub + sb] + 1, more, 0)

    @pl.when((b == 0) & (f == 0))
    def _():
        for sb in range(nsub):
            x_sc[0, pl.ds(sb * gr, gr), :] = gather_head(0, sb)
            gather_tail(0, 0, sb)

    nxt = jnp.minimum(b + 1, nb - 1)
    nslot = (b + 1) % 2

    @pl.when(valid)
    def _():
        x_sc[nslot, pl.ds(pl.multiple_of(f * gr, gr), gr), :] = gather_head(nxt, f)
        xb = x_sc[b % 2]
        a = _silu(_dot(xb, wg_ref[0])) * _dot(xb, wu_ref[0])
        acc_sc[...] = jnp.where(f == 0, 0.0, acc_sc[...]) + _dot(a.astype(BF16), wd_ref[0].astype(BF16))

    @pl.when(valid & (thi_ref[nxt * nsub + f] - tlo_ref[nxt * nsub + f] >= MOE_GATHER_TILES))
    def _():
        gather_tail(nslot, nxt, f)

    @pl.when(f == nsub - 1)
    def _():
        y_ref[...] = jnp.where(valid, acc_sc[...], 0.0).astype(y_ref.dtype)


def moe_ffn(h, post, sched, w_gate, w_up, w_down):
    s, d = h.shape
    ne, _, dff = w_gate.shape
    tmb = MOE_ROW_BLOCK
    tt = min(MOE_TOK_TILE, s)
    nb = sched[0].shape[0]
    nf = tmb // MOE_GATHER_ROWS
    tf = dff // nf

    def fidx(b, f, nv):
        return jnp.where(b < nv[0], f, nf - 1)

    grid_spec = pltpu.PrefetchScalarGridSpec(
        num_scalar_prefetch=5,
        grid=(nb, nf),
        in_specs=[pl.BlockSpec((s, d), lambda b, f, *_: (0, 0), pipeline_mode=pl.Buffered(1)),
                  pl.BlockSpec((SUBLANE, s), lambda b, f, *_: (0, 0), pipeline_mode=pl.Buffered(1)),
                  pl.BlockSpec((1, d, tf), lambda b, f, be, r0, tlo, thi, nv: (be[b], 0, fidx(b, f, nv))),
                  pl.BlockSpec((1, d, tf), lambda b, f, be, r0, tlo, thi, nv: (be[b], 0, fidx(b, f, nv))),
                  pl.BlockSpec((1, tf, d), lambda b, f, be, r0, tlo, thi, nv: (be[b], fidx(b, f, nv), 0))],
        out_specs=pl.BlockSpec((tmb, d), lambda b, f, *_: (b, 0)),
        scratch_shapes=[pltpu.VMEM((2, tmb, d), BF16), pltpu.VMEM((tmb, d), F32)],
    )
    return pl.pallas_call(
        functools.partial(_moe_ffn_kernel, tt=tt, nsub=nf),
        grid_spec=grid_spec,
        out_shape=jax.ShapeDtypeStruct((nb * tmb, d), BF16),
        compiler_params=pltpu.CompilerParams(dimension_semantics=("arbitrary", "arbitrary"),
                                             vmem_limit_bytes=MOE_VMEM_LIMIT),
        name="moe_ffn",
    )(*sched, h, post, w_gate, w_up, w_down)


def _moe_combine_kernel(kb_ref, off_ref, lim_ref, x_ref, pos_ref, gate_ref, fg_ref, *rest):
    y_refs, o_ref, acc_sc = rest[:-2], rest[-2], rest[-1]
    t = pl.program_id(0)
    tm = x_ref.shape[0]
    yb = y_refs[0].shape[0]
    pos = pos_ref[...]
    gate = gate_ref[...]
    lane = lax.broadcasted_iota(jnp.int32, (tm, LANE), 1)
    col = lax.broadcasted_iota(jnp.int32, (1, yb), 1).astype(F32)
    def routed(e):
        pe = jnp.sum(jnp.where(lane == e, pos, 0.0), axis=1, keepdims=True)
        ge = jnp.sum(jnp.where(lane == e, gate, 0.0), axis=1, keepdims=True)
        r = jnp.where(pe >= 0.0, pe + off_ref[t * N_EXPERTS + e].astype(F32), -1.0)
        return r, ge

    def picked(e, k, r):
        sel = jnp.where(r == col + float(k * yb), 1.0, 0.0).astype(BF16)
        return _dot(sel, y_refs[MOE_Y_FETCH * e + k][...])

    col2 = lax.broadcasted_iota(jnp.int32, (1, 2 * yb), 1).astype(F32)
    acc = x_ref[...]
    for e in range(N_EXPERTS):
        r, ge = routed(e)
        pair = jnp.concatenate([y_refs[MOE_Y_FETCH * e][...], y_refs[MOE_Y_FETCH * e + 1][...]], axis=0)
        acc = acc + ge * _dot(jnp.where(r == col2, 1.0, 0.0).astype(BF16), pair)
    acc_sc[...] = acc
    for e in range(N_EXPERTS):
        for k in range(2, MOE_Y_FETCH):
            @pl.when(lim_ref[t * N_EXPERTS + e] > k * yb)
            def _(k=k, e=e):
                r, ge = routed(e)
                acc_sc[...] += ge * picked(e, k, r)
    o_ref[...] = _rms(acc_sc[...], fg_ref[...]).astype(o_ref.dtype)


def moe_combine(x, pos, gate, y, kb, off, lim, final_gain):
    s, d = x.shape
    tm = min(MOE_TOK_TILE, s)
    yb = MOE_Y_BLOCK
    last = y.shape[0] // yb - 1

    def yspec(e, k):
        def index(t, kb_r, off_r, lim_r):
            blk = jnp.minimum(kb_r[t * N_EXPERTS + e] + k, last)
            return (blk if k < 1 else jnp.where(lim_r[t * N_EXPERTS + e] > k * yb, blk, 0), 0)
        return pl.BlockSpec((yb, d), index)

    grid_spec = pltpu.PrefetchScalarGridSpec(
        num_scalar_prefetch=3,
        grid=(s // tm,),
        in_specs=[pl.BlockSpec((tm, d), lambda t, *_: (t, 0)),
                  pl.BlockSpec((tm, LANE), lambda t, *_: (t, 0)),
                  pl.BlockSpec((tm, LANE), lambda t, *_: (t, 0)),
                  pl.BlockSpec((1, d), lambda t, *_: (0, 0))]
        + [yspec(e, k) for e in range(N_EXPERTS) for k in range(MOE_Y_FETCH)],
        out_specs=pl.BlockSpec((tm, d), lambda t, *_: (t, 0)),
        scratch_shapes=[pltpu.VMEM((tm, d), F32)],
    )
    return pl.pallas_call(
        _moe_combine_kernel,
        grid_spec=grid_spec,
        out_shape=jax.ShapeDtypeStruct((s, d), F32),
        compiler_params=_params("arbitrary"),
        name="moe_combine",
    )(kb, off, lim, x, pos, gate, final_gain.reshape(1, d).astype(F32), *([y] * (MOE_Y_FETCH * N_EXPERTS)))


def _moe_schedule(before, total, s):
    tmb, yb, gr = MOE_ROW_BLOCK, MOE_Y_BLOCK, MOE_GATHER_ROWS
    nb = 2 * s // tmb + N_EXPERTS
    counts = total[0, :N_EXPERTS].astype(jnp.int32)
    nblk = (counts + tmb - 1) // tmb
    end = jnp.cumsum(nblk)
    first = end - nblk
    nvalid = end[-1]
    b = jnp.minimum(jnp.arange(nb, dtype=jnp.int32), nvalid - 1)
    blk_e = jnp.sum(b[:, None] >= end[None, :], axis=1).astype(jnp.int32)
    r0 = (b - first[blk_e]) * tmb
    cb = before[:, 0, :N_EXPERTS].astype(jnp.int32)
    r0s = (r0[:, None] + gr * jnp.arange(tmb // gr, dtype=jnp.int32)[None, :]).reshape(-1)
    cbe = cb[:, jnp.repeat(blk_e, tmb // gr)]
    tlo = (jnp.sum(cbe <= r0s[None, :], axis=0) - 1).astype(jnp.int32)
    thi = (jnp.sum(cbe < (r0s + gr)[None, :], axis=0) - 1).astype(jnp.int32)
    sched = (blk_e, r0.astype(jnp.int32), tlo, thi, nvalid.reshape(1).astype(jnp.int32))
    row_start = first[None, :] * tmb + cb
    kb = row_start // yb
    off = first[None, :] * tmb - kb * yb
    n_te = jnp.concatenate([cb[1:], counts[None, :]], axis=0) - cb
    lim = row_start - kb * yb + n_te
    flat = lambda a: a.reshape(-1).astype(jnp.int32)
    return sched, flat(kb), flat(off), flat(lim)


def moe_sparse(h, x, norm_gain, router, w_gate, w_up, w_down, final_gain):
    s, _ = x.shape
    pos, gate, post, before, total = moe_route(x, norm_gain, router)
    sched, kb, off, lim = _moe_schedule(before, total, s)
    y = moe_ffn(h, post, sched, w_gate, w_up, w_down)
    return moe_combine(x, pos, gate, y, kb, off, lim, final_gain)


FOX_F0 = 7 * BRANCH_W


def _split_w_in_kernel(wt_ref, mix_ref, f_ref, gate_ref):
    rest0 = FOX_F0 + N_HEADS
    gate0 = N_MIX_COLS + N_HEADS
    mix_ref[:, :FOX_F0] = wt_ref[pl.ds(0, FOX_F0), :].T.astype(BF16)
    mix_ref[:, FOX_F0:] = wt_ref[pl.ds(rest0, N_MIX_COLS - FOX_F0), :].T.astype(BF16)
    gate_ref[...] = wt_ref[pl.ds(gate0, N_BRANCH * D_MODEL), :].T.astype(BF16)
    f_rows = jnp.concatenate([wt_ref[pl.ds(FOX_F0, N_HEADS), :], jnp.zeros((LANE - N_HEADS, LANE), F32)], axis=0)
    f_ref[...] = f_rows.T.astype(BF16)


def _mixer_weights(w_in, layer, tr=LANE):
    _, d, cols = w_in.shape
    w_t = jnp.swapaxes(w_in, 1, 2)
    row = lambda w: pl.BlockSpec((tr, w), lambda i: (i, 0))
    return pl.pallas_call(
        _split_w_in_kernel,
        grid=(d // tr,),
        in_specs=[pl.BlockSpec((None, cols, tr), lambda i: (layer, 0, i))],
        out_specs=[row(N_MIX_COLS), row(LANE), row(N_BRANCH * D_MODEL)],
        out_shape=[jax.ShapeDtypeStruct((d, N_MIX_COLS), BF16), jax.ShapeDtypeStruct((d, LANE), BF16),
                   jax.ShapeDtypeStruct((d, N_BRANCH * D_MODEL), BF16)],
        compiler_params=_params("parallel"),
        name="split_w_in",
    )(w_t)


def kernel(x, w_in, w_branch, w_out, norm_mix_g, hgrn_lb_logits, hgrn_norm_g, fox_f_bias, pool_w, pool_scale,
           ret_gn_g, ret_gn_b, norm_ffn_g, ffn_w_gate, ffn_w_up, ffn_w_down, moe_router, moe_w_gate, moe_w_up,
           moe_w_down, final_norm_g):
    b, s, d = x.shape
    assert b == 1 and d == D_MODEL
    depth = w_in.shape[0]
    assert depth == 2, "layer 0 uses the dense FFN, layer 1 the experts and the final norm"
    xs = x.reshape(s, d)
    h = rmsnorm_bf16(xs, norm_mix_g[0])
    out = None
    for layer in range(depth):
        w_mix, w_f, w_gate = _mixer_weights(w_in, layer)
        proj = matmul(h, w_mix)
        ct, c_rows, fox_stats = fox_gate(h, proj, w_f, fox_f_bias[layer])
        branches = (
            hgrn2(proj, hgrn_lb_logits, hgrn_norm_g[layer], layer),
            fox_attention(proj, ct, c_rows, fox_stats),
            pool_mixer(proj, pool_w[layer], pool_scale[layer]),
            retention(proj, ret_gn_g[layer], ret_gn_b[layer]),
        )
        xs, h2 = merge(h, branches, xs, w_gate, w_branch[layer].astype(BF16), w_out[layer].astype(BF16),
                       norm_ffn_g[layer])
        if layer % 2 == 0:
            li = layer // 2
            xs, h = ffn_dense(h2, xs, ffn_w_gate[li], ffn_w_up[li], ffn_w_down[li], norm_mix_g[layer + 1])
        else:
            li = layer // 2
            out = moe_sparse(h2, xs, norm_ffn_g[layer], moe_router[li], moe_w_gate[li].astype(BF16),
                             moe_w_up[li].astype(BF16), moe_w_down[li], final_norm_g)
    return out.reshape(b, s, d)
```

```python
import functools
import math

import jax
import jax.numpy as jnp
import numpy as np
from jax import lax
from jax.experimental import pallas as pl
from jax.experimental.pallas import tpu as pltpu

D_MODEL = 1024
N_BRANCH = 4
BRANCH_W = D_MODEL // N_BRANCH
HEAD_DIM = 64
N_HEADS = BRANCH_W // HEAD_DIM
POOL_WINDOWS = (2, 4, 8, 16)
POOL_GROUP = BRANCH_W // len(POOL_WINDOWS)
POOL_HALO = 16
RET_DECAY_BASE = 5.0
ROPE_BASE = 10000.0
D_FF = 7 * D_MODEL // 2
N_EXPERTS = 8
RMS_EPS = 1e-6
LN_EPS = 1e-5
N_MIX_COLS = 12 * BRANCH_W

LANE = 128
SUBLANE = 8
VMEM_LIMIT = 56 * 1024 * 1024

HG_CHUNK = 64
HG_SUB = 16
HG_GROUP = 1
HG_FAST_MIN_LOGDECAY = -60.0
RET_CHUNK = 256

F32 = jnp.float32
BF16 = jnp.bfloat16
NT_DIMS = (((1,), (1,)), ((), ()))


def _params(*sem):
    return pltpu.CompilerParams(dimension_semantics=sem, vmem_limit_bytes=VMEM_LIMIT)


def _const_spec(shape):
    nd = len(shape)
    return pl.BlockSpec(shape, lambda *_: (0,) * nd, pipeline_mode=pl.Buffered(1))


def _split3(x):
    hi = x.astype(BF16)
    r1 = x - hi.astype(F32)
    mid = r1.astype(BF16)
    lo = (r1 - mid.astype(F32)).astype(BF16)
    return hi, mid, lo


def _dot(a, b):
    return jnp.dot(a, b, preferred_element_type=F32)


def _dot_nt(a, b):
    return lax.dot_general(a, b, NT_DIMS, preferred_element_type=F32)


def _dot_exact_rhs(x, m_bf16):
    hi, mid, lo = _split3(x)
    return _dot(hi, m_bf16) + _dot(mid, m_bf16) + _dot(lo, m_bf16)


def _dot_exact_lhs(m_bf16, x):
    hi, mid, lo = _split3(x)
    return _dot(m_bf16, hi) + _dot(m_bf16, mid) + _dot(m_bf16, lo)


def _sigmoid(x):
    return 1.0 / (1.0 + jnp.exp(-x))


def _silu(x):
    return x * _sigmoid(x)


def _rms(x, gain):
    return x * lax.rsqrt(jnp.mean(x * x, axis=-1, keepdims=True) + RMS_EPS) * gain


def _const(a, dtype=F32):
    return jnp.asarray(np.asarray(a, np.float32), dtype)


def _head_of(n):
    return np.arange(n) // HEAD_DIM


def _head_ones():
    h = _head_of(BRANCH_W)
    return _const(h[:, None] == h[None, :], BF16)


def _head_masks():
    return _const(_head_of(BRANCH_W)[None, :] == np.arange(N_HEADS)[:, None])


def _norm_matmul_kernel(x_ref, g_ref, b_ref, o_ref, h_ref):
    h = _rms(x_ref[...], g_ref[...]).astype(h_ref.dtype)
    h_ref[...] = h
    o_ref[...] = _dot(h, b_ref[...]).astype(o_ref.dtype)


def norm_matmul(x, gain, b, tm=2048, tn=1024):
    m, k = x.shape
    _, n = b.shape
    tm = min(tm, m)
    return pl.pallas_call(
        _norm_matmul_kernel,
        grid=(n // tn, m // tm),
        in_specs=[pl.BlockSpec((tm, k), lambda j, i: (i, 0)), _const_spec((1, k)),
                  pl.BlockSpec((k, tn), lambda j, i: (0, j))],
        out_specs=[pl.BlockSpec((tm, tn), lambda j, i: (i, j)), pl.BlockSpec((tm, k), lambda j, i: (i, 0))],
        out_shape=[jax.ShapeDtypeStruct((m, n), BF16), jax.ShapeDtypeStruct((m, k), BF16)],
        compiler_params=_params("arbitrary", "arbitrary"),
        name="in_proj_norm",
    )(x, gain.reshape(1, k).astype(F32), b)


def _matmul_kernel(a_ref, b_ref, o_ref):
    o_ref[...] = _dot(a_ref[...], b_ref[...]).astype(o_ref.dtype)


def matmul(a, b, out_dtype=BF16, tm=2048, tn=1024):
    m, k = a.shape
    _, n = b.shape
    tm = min(tm, m)
    return pl.pallas_call(
        _matmul_kernel,
        grid=(n // tn, m // tm),
        in_specs=[pl.BlockSpec((tm, k), lambda j, i: (i, 0)),
                  pl.BlockSpec((k, tn), lambda j, i: (0, j))],
        out_specs=pl.BlockSpec((tm, tn), lambda j, i: (i, j)),
        out_shape=jax.ShapeDtypeStruct((m, n), out_dtype),
        compiler_params=_params("parallel", "parallel"),
        name="in_proj",
    )(a, b)


def _hgrn_kernel(q_ref, f_ref, i_ref, g_ref, lbl_ref, ng_ref, ones_ref, hm_ref, tril_ref, halfsum_ref, fmask_ref,
                 o_ref, st_ref, bpad, kpad, vpad, astack, lf_sc, kk_sc, o_sc, *, layer, tile):
    c, sub = HG_CHUNK, HG_SUB
    nsub = c // sub
    half = c // 2

    @pl.when(pl.program_id(0) == 0)
    def _():
        st_ref[...] = jnp.zeros_like(st_ref)
        bpad[...] = jnp.zeros_like(bpad)
        kpad[...] = jnp.zeros_like(kpad)
        vpad[...] = jnp.zeros_like(vpad)

    lbl = lbl_ref[...]
    e = jnp.exp(lbl - jnp.max(lbl, axis=0, keepdims=True))
    p = e / jnp.sum(e, axis=0, keepdims=True)
    lb = jnp.zeros((1, BRANCH_W), F32)
    for l in range(1, layer + 1):
        lb = lb + p[l:l + 1, :]

    ones_bd = ones_ref[...]
    hm = hm_ref[...]
    tril = tril_ref[...]
    row = lax.broadcasted_iota(jnp.int32, (c, 1), 0)
    row_in_sub = row % sub
    bd_mask = ones_bd.astype(F32)

    sig = _sigmoid(f_ref[...].astype(F32))
    logf_all = jnp.log(lb + (1.0 - lb) * sig)
    lf_sc[...] = _dot_exact_lhs(tril, logf_all)
    kk_sc[...] = (1.0 - lb) * (1.0 - sig)
    min_decay = jnp.min(_dot(halfsum_ref[...], logf_all.astype(BF16)))

    def load(ci):
        r0 = pl.multiple_of(ci * c, c)
        q = q_ref[pl.ds(r0, c), :].astype(F32)
        v = i_ref[pl.ds(r0, c), :].astype(F32)
        kk = kk_sc[pl.ds(r0, c), :]
        b = lf_sc[pl.ds(r0, c), :]
        return r0, q, v, kk, b

    def finish(r0, q_decayed, v, kk, b, intra):
        st = st_ref[...]
        inter = _dot_nt(q_decayed.astype(BF16), st.astype(BF16))
        b_last = b[c - 1:c, :]
        ks_end = (kk * jnp.exp(b_last - b)).astype(BF16)
        upd = _dot(v.T.astype(BF16), ks_end)
        st_ref[...] = st * jnp.exp(b_last) + upd * bd_mask
        o_sc[pl.ds(r0, c), :] = intra + inter

    def fast_group(gi, carry):
        second = row >= half
        chunks, qxs, kaugs, vaugs = [], [], [], []
        for j in range(HG_GROUP):
            r0, q, v, kk, b = load(gi * HG_GROUP + j)
            m_row = b[half - 1:half, :]
            mref = jnp.where(second, m_row, 0.0)
            qp = q * jnp.exp(b - mref)
            kp = kk * jnp.exp(mref - b)
            e_m = jnp.exp(m_row)
            kaugs.append(jnp.concatenate([kp, kp[:half, :] * e_m], axis=0))
            vaugs.append(jnp.concatenate([v, v[:half, :]], axis=0))
            qxs.extend(qp * hm[h:h + 1, :] for h in range(N_HEADS))
            chunks.append((r0, jnp.where(second, qp * e_m, qp), v, kk, b))
        qx = jnp.concatenate(qxs, axis=0).astype(BF16)
        kaug = jnp.concatenate(kaugs, axis=0).astype(BF16)
        vaug = jnp.concatenate(vaugs, axis=0).astype(BF16)
        sc = jnp.where(fmask_ref[...] > 0.0, _dot_nt(qx, kaug), 0.0)
        r = _dot(sc.astype(BF16), vaug)
        for j, (r0, q_decayed, v, kk, b) in enumerate(chunks):
            intra = jnp.zeros((c, BRANCH_W), F32)
            for h in range(N_HEADS):
                lo = (j * N_HEADS + h) * c
                intra = intra + r[lo:lo + c, :] * hm[h:h + 1, :]
            finish(r0, q_decayed, v, kk, b, intra)
        return carry

    def exact_chunk(ci, carry):
        r0, q, v, kk, b = load(ci)

        bpad[pl.ds(sub, c), :] = b
        kpad[pl.ds(sub, c), :] = kk
        vpad[pl.ds(sub, c), :] = v

        for d in range(sub):
            b_d = bpad[pl.ds(sub - d, c), :]
            k_d = kpad[pl.ds(sub - d, c), :]
            a = jnp.where(row_in_sub >= d, q * k_d * jnp.exp(b - b_d), 0.0)
            astack[pl.ds(d * c, c), :] = a.astype(BF16)
        pall = _dot(astack[...], ones_bd)
        intra = jnp.zeros((c, BRANCH_W), F32)
        for d in range(sub):
            intra = intra + pall[d * c:(d + 1) * c, :] * vpad[pl.ds(sub - d, c), :]

        pieces = [jnp.zeros((sub, BRANCH_W), F32)]
        for si in range(1, nsub):
            lo = si * sub
            m_i = b[lo - 1:lo, :]
            qs = q[lo:lo + sub, :] * jnp.exp(b[lo:lo + sub, :] - m_i)
            ks = (kk[:lo, :] * jnp.exp(m_i - b[:lo, :])).astype(BF16)
            qx = jnp.concatenate([qs * hm[h:h + 1, :] for h in range(N_HEADS)], axis=0).astype(BF16)
            sc = _dot_nt(qx, ks)
            r = _dot(sc.astype(BF16), v[:lo, :].astype(BF16))
            acc = jnp.zeros((sub, BRANCH_W), F32)
            for h in range(N_HEADS):
                acc = acc + r[h * sub:(h + 1) * sub, :] * hm[h:h + 1, :]
            pieces.append(acc)
        intra = intra + jnp.concatenate(pieces, axis=0)
        finish(r0, q * jnp.exp(b), v, kk, b, intra)
        return carry

    lax.cond(min_decay >= HG_FAST_MIN_LOGDECAY,
             lambda: lax.fori_loop(0, tile // (c * HG_GROUP), fast_group, 0, unroll=2),
             lambda: lax.fori_loop(0, tile // c, exact_chunk, 0))

    o = o_sc[...]
    ms = _dot_exact_rhs(o * o, ones_bd) * (1.0 / HEAD_DIM)
    y = o * lax.rsqrt(ms + RMS_EPS) * ng_ref[...] * _silu(g_ref[...].astype(F32))
    o_ref[...] = y.astype(o_ref.dtype)


def hgrn2(proj, lb_logits, norm_g, layer, tile=512):
    s = proj.shape[0]
    depth = lb_logits.shape[0]
    c, sub = HG_CHUNK, HG_SUB
    half = c // 2
    tile = min(tile, s)
    col = lambda j: pl.BlockSpec((tile, BRANCH_W), lambda i, j=j: (i, j))
    pos = np.arange(tile)
    tril = _const((pos[:, None] // c == pos[None, :] // c) & (pos[None, :] <= pos[:, None]), BF16)
    nhalf = tile // half
    halfsum = _const(np.arange(tile)[None, :] // half == np.arange(nhalf)[:, None], BF16)
    t = np.arange(c)[:, None]
    col_s = np.arange(c + half)[None, :]
    same_half = (col_s < c) & (col_s // half == t // half) & (col_s <= t)
    cross = (col_s >= c) & (t >= half)
    fmask1 = np.tile((same_half | cross).astype(np.float32), (N_HEADS, 1))
    fmask = _const(np.kron(np.eye(HG_GROUP, dtype=np.float32), fmask1))
    return pl.pallas_call(
        functools.partial(_hgrn_kernel, layer=layer, tile=tile),
        grid=(s // tile,),
        in_specs=[col(0), col(1), col(2), col(3),
                  _const_spec((depth, BRANCH_W)), _const_spec((1, BRANCH_W)),
                  _const_spec((BRANCH_W, BRANCH_W)), _const_spec((N_HEADS, BRANCH_W)),
                  _const_spec((tile, tile)), _const_spec((nhalf, tile)),
                  _const_spec((HG_GROUP * N_HEADS * c, HG_GROUP * (c + half)))],
        out_specs=pl.BlockSpec((tile, BRANCH_W), lambda i: (i, 0)),
        out_shape=jax.ShapeDtypeStruct((s, BRANCH_W), BF16),
        scratch_shapes=[pltpu.VMEM((BRANCH_W, BRANCH_W), F32),
                        pltpu.VMEM((c + sub, BRANCH_W), F32),
                        pltpu.VMEM((c + sub, BRANCH_W), F32),
                        pltpu.VMEM((c + sub, BRANCH_W), F32),
                        pltpu.VMEM((sub * c, BRANCH_W), BF16),
                        pltpu.VMEM((tile, BRANCH_W), F32),
                        pltpu.VMEM((tile, BRANCH_W), F32),
                        pltpu.VMEM((tile, BRANCH_W), F32)],
        compiler_params=_params("arbitrary"),
        name="hgrn2",
    )(proj, proj, proj, proj, lb_logits.astype(F32), norm_g.reshape(1, BRANCH_W).astype(F32),
      _head_ones(), _head_masks(), tril, halfsum, fmask)


FOX_TILE = 256
FOX_GATE_BLOCKS = 2
FOX_NSTAT = 16
FOX_FIXED_MAX = 30.0
FOX_SKIP_LOG = 40.0


def _fox_gate_kernel(h_ref, q_ref, k_ref, wf_ref, bias_ref, tril_ref, ones_ref,
                     ct_ref, c_ref, stat_ref, carry_ref, kmax_ref):
    @pl.when(pl.program_id(0) == 0)
    def _():
        carry_ref[...] = jnp.zeros_like(carry_ref)
        kmax_ref[...] = jnp.zeros_like(kmax_ref)

    logit = _dot(h_ref[...], wf_ref[...]) + bias_ref[...]
    logf = jnp.minimum(logit, 0.0) - jnp.log(1.0 + jnp.exp(-jnp.abs(logit)))
    cum = _dot_exact_lhs(tril_ref[...], logf) + carry_ref[...]
    carry_ref[...] = cum[-1:, :]
    ct_ref[...] = cum.T[:SUBLANE, :]
    c_ref[...] = cum

    ones_bd = ones_ref[...]
    q = q_ref[...].astype(F32)
    k = k_ref[...].astype(F32)
    scale = HEAD_DIM ** -0.5
    head_lane = lax.broadcasted_iota(jnp.int32, (1, BRANCH_W), 1) // HEAD_DIM
    c_heads = jnp.zeros(q.shape, F32)
    for h in range(N_HEADS):
        c_heads = jnp.where(head_lane == h, cum[:, h:h + 1], c_heads)
    slack = 1.0 + 2.0 ** -6
    qn = jnp.sqrt(_dot((q * q).astype(BF16), ones_bd)) * (scale * slack)
    kn = jnp.sqrt(_dot((k * k).astype(BF16), ones_bd)) * slack
    diag = _dot((q * k).astype(BF16), ones_bd) * scale - (2.0 ** -6) * qn * kn
    e_row = c_heads - diag
    kmax = kmax_ref[...]
    for blk in range(q.shape[0] // FOX_TILE):
        rs = slice(blk * FOX_TILE, (blk + 1) * FOX_TILE)
        kmax = jnp.maximum(kmax, jnp.max(kn[rs], axis=0, keepdims=True))
        rows = [jnp.max(qn[rs], axis=0, keepdims=True),
                jnp.max(e_row[rs], axis=0, keepdims=True),
                kmax,
                c_heads[(blk + 1) * FOX_TILE - 1:(blk + 1) * FOX_TILE, :]]
        stat_ref[blk] = jnp.concatenate(rows + [jnp.zeros((SUBLANE - len(rows), BRANCH_W), F32)], axis=0)
    kmax_ref[...] = kmax


def fox_gate(h, proj, w_f, f_bias):
    s, d = h.shape
    tile = min(FOX_GATE_BLOCKS * FOX_TILE, s)
    nblk = tile // FOX_TILE
    bias = jnp.zeros((1, LANE), F32).at[0, :N_HEADS].set(f_bias.astype(F32))
    tril = _const(np.tril(np.ones((tile, tile))), BF16)
    ct, c_rows, stats = pl.pallas_call(
        _fox_gate_kernel,
        grid=(s // tile,),
        in_specs=[pl.BlockSpec((tile, d), lambda i: (i, 0)),
                  pl.BlockSpec((tile, BRANCH_W), lambda i: (i, 4)),
                  pl.BlockSpec((tile, BRANCH_W), lambda i: (i, 5)),
                  _const_spec((d, LANE)), _const_spec((1, LANE)), _const_spec((tile, tile)),
                  _const_spec((BRANCH_W, BRANCH_W))],
        out_specs=[pl.BlockSpec((SUBLANE, tile), lambda i: (0, i)),
                   pl.BlockSpec((tile, LANE), lambda i: (i, 0)),
                   pl.BlockSpec((nblk, SUBLANE, BRANCH_W), lambda i: (i, 0, 0))],
        out_shape=[jax.ShapeDtypeStruct((SUBLANE, s), F32),
                   jax.ShapeDtypeStruct((s, LANE), F32),
                   jax.ShapeDtypeStruct((s // FOX_TILE, SUBLANE, BRANCH_W), F32)],
        scratch_shapes=[pltpu.VMEM((1, LANE), F32), pltpu.VMEM((1, BRANCH_W), F32)],
        compiler_params=_params("arbitrary"),
        name="fox_gate",
    )(h, proj, proj, w_f, bias, tril, _head_ones())
    return ct, c_rows, stats[:, :4, ::HEAD_DIM].reshape(-1)


def _fox_kernel(stat_ref, q_ref, k_ref, v_ref, ct_ref, c_ref, hm_ref, o_ref, m_sc, l_sc, acc_sc, *, tq):
    i = pl.program_id(0)
    q0 = pl.multiple_of(i * tq, tq)
    hm = hm_ref[...]
    q = q_ref[...].astype(F32) * (HEAD_DIM ** -0.5)
    qh = [(q * hm[h:h + 1, :]).astype(BF16) for h in range(N_HEADS)]
    c_q0 = ct_ref[:, pl.ds(q0, tq)][:, 0:1]

    first = i
    for h in range(N_HEADS):
        qmax = stat_ref[i * FOX_NSTAT + h]
        emax = stat_ref[i * FOX_NSTAT + N_HEADS + h]

        def needed(j, h=h, qmax=qmax, emax=emax):
            jc = jnp.maximum(j, 0)
            bound = (qmax * stat_ref[jc * FOX_NSTAT + 2 * N_HEADS + h] + emax
                     - stat_ref[jc * FOX_NSTAT + 3 * N_HEADS + h])
            return (j >= 0) & (bound >= -FOX_SKIP_LOG)

        last_dropped = lax.while_loop(needed, lambda j: j - 1, i - 1)
        first = jnp.minimum(first, last_dropped + 1)

    l_sc[...] = jnp.zeros_like(l_sc)
    acc_sc[...] = jnp.zeros_like(acc_sc)

    def causal(sc):
        r = lax.broadcasted_iota(jnp.int32, (tq, tq), 0)
        cidx = lax.broadcasted_iota(jnp.int32, (tq, tq), 1)
        return jnp.where(cidx <= r, sc, -jnp.inf)

    def online_block(s0, diagonal):
        kb = k_ref[pl.ds(s0, tq), :]
        vb = v_ref[pl.ds(s0, tq), :]
        bias = c_q0 - ct_ref[:, pl.ds(s0, tq)]
        for h in range(N_HEADS):
            sc = _dot_nt(qh[h], kb) + bias[h:h + 1, :]
            if diagonal:
                sc = causal(sc)
            m_prev = m_sc[h]
            m_new = jnp.maximum(m_prev, jnp.max(sc, axis=1, keepdims=True))
            alpha = jnp.exp(m_prev - m_new)
            p = jnp.exp(sc - jnp.tile(m_new, (1, tq // LANE)))
            l_sc[h] = alpha * l_sc[h] + jnp.sum(p, axis=1, keepdims=True)
            acc_sc[h] = acc_sc[h] * jnp.tile(alpha, (1, BRANCH_W // LANE)) + _dot(p.astype(BF16), vb)
            m_sc[h] = m_new

    tops = [stat_ref[i * FOX_NSTAT + h] * stat_ref[i * FOX_NSTAT + 2 * N_HEADS + h] for h in range(N_HEADS)]
    c_tile = c_ref[...]
    shift = [c_tile[:, h:h + 1] - c_q0[h:h + 1, :] - tops[h] for h in range(N_HEADS)]

    def fixed_block(s0, diagonal):
        kb = k_ref[pl.ds(s0, tq), :]
        vb = v_ref[pl.ds(s0, tq), :]
        bias = c_q0 - ct_ref[:, pl.ds(s0, tq)]
        for h in range(N_HEADS):
            sc = _dot_nt(qh[h], kb) + bias[h:h + 1, :] + shift[h]
            if diagonal:
                sc = causal(sc)
            p = jnp.exp(sc)
            l_sc[h] += p[:, :LANE] + p[:, LANE:]
            acc_sc[h] += _dot(p.astype(BF16), vb)

    def run(block, row_sum):
        lax.fori_loop(first, i, lambda j, carry: (block(pl.multiple_of(j * tq, tq), False), carry)[1], 0)
        block(q0, True)
        out = jnp.zeros((tq, BRANCH_W), F32)
        for h in range(N_HEADS):
            out = out + acc_sc[h] * hm[h:h + 1, :] / row_sum(l_sc[h])
        o_ref[...] = out.astype(o_ref.dtype)

    def run_online():
        m_sc[...] = jnp.full_like(m_sc, -jnp.inf)
        run(online_block, lambda l: jnp.tile(l, (1, BRANCH_W // LANE)))

    def run_fixed():
        run(fixed_block, lambda l: jnp.sum(l, axis=1, keepdims=True))

    lax.cond(functools.reduce(jnp.maximum, tops) <= FOX_FIXED_MAX, run_fixed, run_online)


def fox_attention(proj, ct, c_rows, stats):
    s = proj.shape[0]
    tq = min(FOX_TILE, s)
    full = lambda j: pl.BlockSpec((s, BRANCH_W), lambda i, j=j: (0, j), pipeline_mode=pl.Buffered(1))
    return pl.pallas_call(
        functools.partial(_fox_kernel, tq=tq),
        grid=(s // tq,),
        in_specs=[pl.BlockSpec(memory_space=pltpu.SMEM),
                  pl.BlockSpec((tq, BRANCH_W), lambda i: (i, 4)), full(5), full(6),
                  _const_spec((SUBLANE, s)), pl.BlockSpec((tq, LANE), lambda i: (i, 0)),
                  _const_spec((N_HEADS, BRANCH_W))],
        out_specs=pl.BlockSpec((tq, BRANCH_W), lambda i: (i, 0)),
        out_shape=jax.ShapeDtypeStruct((s, BRANCH_W), BF16),
        scratch_shapes=[pltpu.VMEM((N_HEADS, tq, LANE), F32),
                        pltpu.VMEM((N_HEADS, tq, LANE), F32),
                        pltpu.VMEM((N_HEADS, tq, BRANCH_W), F32)],
        compiler_params=_params("parallel"),
        name="fox_attention",
    )(stats, proj, proj, proj, ct, c_rows, _head_masks())


def _pool_kernel(u_ref, w_ref, scale_ref, o_ref, ext, *, tile):
    i = pl.program_id(0)

    @pl.when(i == 0)
    def _():
        ext[pl.ds(0, POOL_HALO), :] = jnp.zeros((POOL_HALO, BRANCH_W), F32)

    u = u_ref[...].astype(F32)
    ext[pl.ds(POOL_HALO, tile), :] = u
    pos = (i * tile + lax.broadcasted_iota(jnp.int32, (tile, 1), 0) + 1).astype(F32)
    halves = []
    for half in range(BRANCH_W // LANE):
        lanes = pl.ds(half * LANE, LANE)
        w_small, w_big = POOL_WINDOWS[2 * half], POOL_WINDOWS[2 * half + 1]
        run = u[:, half * LANE:(half + 1) * LANE]
        sums = {}
        for j in range(1, w_big):
            if j == w_small:
                sums[w_small] = run
            run = run + ext[pl.ds(POOL_HALO - j, tile), lanes]
        sums[w_big] = run
        lane = lax.broadcasted_iota(jnp.int32, (1, LANE), 1)
        small = lane < POOL_GROUP
        total = jnp.where(small, sums[w_small], sums[w_big])
        count = jnp.where(small, jnp.minimum(pos, float(w_small)), jnp.minimum(pos, float(w_big)))
        halves.append(total / count)
    mean = jnp.concatenate(halves, axis=1)
    d = (mean - u).astype(BF16)
    y = _dot(d, w_ref[...]) * scale_ref[...]
    o_ref[...] = y.astype(o_ref.dtype)
    ext[pl.ds(0, POOL_HALO), :] = u[tile - POOL_HALO:, :]


def pool_mixer(proj, w_pool, scale, tile=2048):
    s = proj.shape[0]
    tile = min(tile, s)
    ng = len(POOL_WINDOWS)
    w_bd = jnp.zeros((BRANCH_W, BRANCH_W), F32)
    for gi in range(ng):
        lo = gi * POOL_GROUP
        w_bd = w_bd.at[lo:lo + POOL_GROUP, lo:lo + POOL_GROUP].set(w_pool[gi].astype(F32))
    return pl.pallas_call(
        functools.partial(_pool_kernel, tile=tile),
        grid=(s // tile,),
        in_specs=[pl.BlockSpec((tile, BRANCH_W), lambda i: (i, 7)),
                  _const_spec((BRANCH_W, BRANCH_W)), _const_spec((1, BRANCH_W))],
        out_specs=pl.BlockSpec((tile, BRANCH_W), lambda i: (i, 0)),
        out_shape=jax.ShapeDtypeStruct((s, BRANCH_W), BF16),
        scratch_shapes=[pltpu.VMEM((tile + POOL_HALO, BRANCH_W), F32)],
        compiler_params=_params("arbitrary"),
        name="pool_mixer",
    )(proj, w_bd.astype(BF16), scale.reshape(1, BRANCH_W).astype(F32))


def _ret_kernel(q_ref, k_ref, v_ref, g_ref, rope_ref, ecos_ref, esin_ref, perm_ref, ones_ref, hm_ref,
                dstack_ref, xi_ref, zeta_ref, gc_ref, gng_ref, gnb_ref, o_ref, st_ref, *, tile):
    c = RET_CHUNK

    @pl.when(pl.program_id(0) == 0)
    def _():
        st_ref[...] = jnp.zeros_like(st_ref)

    perm = perm_ref[...]
    ones_bd = ones_ref[...]
    bd_mask = ones_bd.astype(F32)
    hm = hm_ref[...]

    tab = rope_ref[...]
    cos = _dot_exact_rhs(tab, ecos_ref[...])
    sin = _dot_exact_rhs(tab, esin_ref[...])
    q_all = q_ref[...]
    k_all = k_ref[...]
    qr_all = q_all.astype(F32) * cos + _dot(q_all, perm) * sin
    kr_all = (k_all.astype(F32) * cos + _dot(k_all, perm) * sin) * (HEAD_DIM ** -0.5)

    outs = []
    for ci in range(tile // c):
        r0 = ci * c
        qr = qr_all[r0:r0 + c, :]
        kr = kr_all[r0:r0 + c, :]
        v = v_ref[pl.ds(r0, c), :]

        qx = jnp.concatenate([qr * hm[h:h + 1, :] for h in range(N_HEADS)], axis=0).astype(BF16)
        sc = _dot_nt(qx, kr.astype(BF16)) * dstack_ref[...]
        r = _dot(sc.astype(BF16), v)
        intra = jnp.zeros((c, BRANCH_W), F32)
        for h in range(N_HEADS):
            intra = intra + r[h * c:(h + 1) * c, :] * hm[h:h + 1, :]

        st = st_ref[...]
        inter = _dot_nt((qr * xi_ref[...]).astype(BF16), st.astype(BF16))
        upd = _dot(v.astype(F32).T.astype(BF16), (kr * zeta_ref[...]).astype(BF16))
        st_ref[...] = st * gc_ref[...] + upd * bd_mask
        outs.append(intra + inter)

    o = jnp.concatenate(outs, axis=0)
    mu = _dot_exact_rhs(o, ones_bd) * (1.0 / HEAD_DIM)
    cen = o - mu
    var = _dot_exact_rhs(cen * cen, ones_bd) * (1.0 / HEAD_DIM)
    y = cen * lax.rsqrt(var + LN_EPS) * gng_ref[...] + gnb_ref[...]
    o_ref[...] = (y * _silu(g_ref[...].astype(F32))).astype(o_ref.dtype)


def _rope_tables(s):
    half = HEAD_DIM // 2
    pos = np.arange(s, dtype=np.float64)
    inv_freq = ROPE_BASE ** (-np.arange(half, dtype=np.float64) / half)
    ang = pos[:, None] * inv_freq[None, :]
    table = np.concatenate([np.cos(ang), np.sin(ang), np.zeros((s, LANE - 2 * half))], axis=1)
    lane = np.arange(BRANCH_W)
    src = np.arange(LANE)[:, None]
    ecos = src == (lane % half)[None, :]
    sign = np.where(lane % HEAD_DIM < half, -1.0, 1.0)
    esin = (src == (half + lane % half)[None, :]) * sign[None, :]
    return _const(table), _const(ecos, BF16), _const(esin, BF16)


def _ret_constants():
    c = RET_CHUNK
    half = HEAD_DIM // 2
    lane = np.arange(BRANCH_W)
    partner = np.where(lane % HEAD_DIM < half, lane + half, lane - half)
    perm = lane[:, None] == partner[None, :]
    log_gamma = np.log1p(-np.exp2(-RET_DECAY_BASE - np.arange(N_HEADS, dtype=np.float64)))
    ci = np.arange(c, dtype=np.float64)
    diff = ci[:, None] - ci[None, :]
    intra = np.where(diff >= 0, np.exp(diff * log_gamma[:, None, None]), 0.0)
    dstack = intra.reshape(N_HEADS * c, c)
    lg_lane = np.repeat(log_gamma, HEAD_DIM)[None, :]
    xi = np.exp((ci[:, None] + 1.0) * lg_lane)
    zeta = np.exp((c - 1.0 - ci[:, None]) * lg_lane)
    gc = np.exp(c * lg_lane)
    return _const(perm, BF16), _const(dstack), _const(xi), _const(zeta), _const(gc)


def retention(proj, gn_g, gn_b, tile=512):
    s = proj.shape[0]
    c = RET_CHUNK
    tile = min(tile, s)
    rope, ecos, esin = _rope_tables(s)
    perm, dstack, xi, zeta, gc = _ret_constants()
    col = lambda j: pl.BlockSpec((tile, BRANCH_W), lambda i, j=j: (i, j))
    row = pl.BlockSpec((tile, BRANCH_W), lambda i: (i, 0))
    return pl.pallas_call(
        functools.partial(_ret_kernel, tile=tile),
        grid=(s // tile,),
        in_specs=[col(8), col(9), col(10), col(11), pl.BlockSpec((tile, LANE), lambda i: (i, 0)),
                  _const_spec((LANE, BRANCH_W)), _const_spec((LANE, BRANCH_W)),
                  _const_spec((BRANCH_W, BRANCH_W)), _const_spec((BRANCH_W, BRANCH_W)),
                  _const_spec((N_HEADS, BRANCH_W)), _const_spec((N_HEADS * c, c)),
                  _const_spec((c, BRANCH_W)), _const_spec((c, BRANCH_W)), _const_spec((1, BRANCH_W)),
                  _const_spec((1, BRANCH_W)), _const_spec((1, BRANCH_W))],
        out_specs=row,
        out_shape=jax.ShapeDtypeStruct((s, BRANCH_W), BF16),
        scratch_shapes=[pltpu.VMEM((BRANCH_W, BRANCH_W), F32)],
        compiler_params=_params("arbitrary"),
        name="retention",
    )(proj, proj, proj, proj, rope, ecos, esin, perm, _head_ones(), _head_masks(), dstack, xi, zeta, gc,
      gn_g.reshape(1, BRANCH_W).astype(F32), gn_b.reshape(1, BRANCH_W).astype(F32))


def _merge_kernel(h_ref, o0_ref, o1_ref, o2_ref, o3_ref, x_ref, wg_ref, wb_ref, wo_ref, g_ref,
                  xo_ref, ho_ref):
    h = h_ref[...]
    merged = jnp.zeros(x_ref.shape, F32)
    for bi, o_ref in enumerate((o0_ref, o1_ref, o2_ref, o3_ref)):
        gate = _sigmoid(_dot(h, wg_ref[:, bi * D_MODEL:(bi + 1) * D_MODEL]))
        merged = merged + gate * _dot(o_ref[...], wb_ref[bi])
    x_new = x_ref[...] + _dot(merged.astype(BF16), wo_ref[...])
    xo_ref[...] = x_new
    ho_ref[...] = _rms(x_new, g_ref[...]).astype(ho_ref.dtype)


def merge(h, branches, x, w_gate, w_branch, w_out, next_gain, tm=512):
    s, d = x.shape
    tm = min(tm, s)
    row = lambda w: pl.BlockSpec((tm, w), lambda i: (i, 0))
    return pl.pallas_call(
        _merge_kernel,
        grid=(s // tm,),
        in_specs=[row(d), row(BRANCH_W), row(BRANCH_W), row(BRANCH_W), row(BRANCH_W), row(d),
                  _const_spec((d, N_BRANCH * d)), _const_spec((N_BRANCH, BRANCH_W, d)),
                  _const_spec((d, d)), _const_spec((1, d))],
        out_specs=[row(d), row(d)],
        out_shape=[jax.ShapeDtypeStruct((s, d), F32), jax.ShapeDtypeStruct((s, d), BF16)],
        compiler_params=_params("parallel"),
        name="merge",
    )(h, *branches, x, w_gate, w_branch, w_out, next_gain.reshape(1, d).astype(F32))


def _ffn_kernel(h_ref, x_ref, wg_ref, wu_ref, wd_ref, g_ref, xo_ref, ho_ref, acc_ref):
    f = pl.program_id(1)

    @pl.when(f == 0)
    def _():
        acc_ref[...] = jnp.zeros_like(acc_ref)

    h = h_ref[...]
    a = _silu(_dot(h, wg_ref[...].astype(BF16))) * _dot(h, wu_ref[...].astype(BF16))
    acc_ref[...] += _dot(a.astype(BF16), wd_ref[...].astype(BF16))

    @pl.when(f == pl.num_programs(1) - 1)
    def _():
        x_new = x_ref[...] + acc_ref[...]
        xo_ref[...] = x_new
        ho_ref[...] = _rms(x_new, g_ref[...]).astype(ho_ref.dtype)


def ffn_dense(h, x, w_gate, w_up, w_down, next_gain, tm=1024, tf=512):
    s, d = x.shape
    tm = min(tm, s)
    dff = w_gate.shape[1]
    row = lambda: pl.BlockSpec((tm, d), lambda i, f: (i, 0))
    return pl.pallas_call(
        _ffn_kernel,
        grid=(s // tm, dff // tf),
        in_specs=[row(), row(),
                  pl.BlockSpec((d, tf), lambda i, f: (0, f)),
                  pl.BlockSpec((d, tf), lambda i, f: (0, f)),
                  pl.BlockSpec((tf, d), lambda i, f: (f, 0)),
                  _const_spec((1, d))],
        out_specs=[row(), row()],
        out_shape=[jax.ShapeDtypeStruct((s, d), F32), jax.ShapeDtypeStruct((s, d), BF16)],
        scratch_shapes=[pltpu.VMEM((tm, d), F32)],
        compiler_params=_params("parallel", "arbitrary"),
        name="ffn_dense",
    )(h, x, w_gate, w_up, w_down, next_gain.reshape(1, d).astype(F32))


MOE_TOK_TILE = 256
MOE_ROUTE_TILES = 4
MOE_ROW_BLOCK = 512
MOE_GATHER_ROWS = 128
MOE_GATHER_TILES = 4
MOE_VMEM_LIMIT = 60 * 1024 * 1024
MOE_Y_BLOCK = 128
MOE_Y_FETCH = MOE_TOK_TILE // MOE_Y_BLOCK + 1


def _dot_f32(x, w):
    xh, xm, _ = _split3(x)
    wh, wm, _ = _split3(w)
    return _dot(xh, wh) + (_dot(xh, wm) + _dot(xm, wh))


def _route_kernel(x_ref, ng_ref, router_ref, ltri_ref, pos_ref, gate_ref, post_ref, before_ref, total_ref,
                  carry_ref):
    tm = x_ref.shape[0]

    @pl.when(pl.program_id(0) == 0)
    def _():
        carry_ref[...] = jnp.zeros_like(carry_ref)

    hn = _rms(x_ref[...], ng_ref[...])
    logits = _dot_f32(hn, router_ref[...])
    lane = lax.broadcasted_iota(jnp.int32, (tm, LANE), 1)
    logits = jnp.where(lane < N_EXPERTS, logits, -jnp.inf)
    v1 = jnp.max(logits, axis=1, keepdims=True)
    i1 = jnp.min(jnp.where(logits == v1, lane, LANE), axis=1, keepdims=True)
    rest = jnp.where(lane == i1, -jnp.inf, logits)
    v2 = jnp.max(rest, axis=1, keepdims=True)
    i2 = jnp.min(jnp.where(rest == v2, lane, LANE), axis=1, keepdims=True)
    w1 = 1.0 / (1.0 + jnp.exp(v2 - v1))
    gate_ref[...] = jnp.where(lane == i1, w1, 0.0) + jnp.where(lane == i2, 1.0 - w1, 0.0)

    member = jnp.where((lane == i1) | (lane == i2), 1.0, 0.0)
    carry = carry_ref[...]
    rank = _dot(ltri_ref[...], member.astype(BF16)) + carry
    pos = jnp.where(member > 0.0, rank, -1.0)
    pos_ref[...] = pos
    post_ref[...] = pos.T[:SUBLANE, :]
    for blk in range(tm // MOE_TOK_TILE):
        before_ref[blk] = carry
        carry = carry + jnp.sum(member[blk * MOE_TOK_TILE:(blk + 1) * MOE_TOK_TILE], axis=0, keepdims=True)
    carry_ref[...] = carry
    total_ref[...] = carry


def moe_route(x, norm_gain, router):
    s, d = x.shape
    tm = min(MOE_ROUTE_TILES * MOE_TOK_TILE, s)
    nsub = tm // MOE_TOK_TILE
    nt = s // tm
    router_p = jnp.zeros((d, LANE), F32).at[:, :N_EXPERTS].set(router.astype(F32))
    ltri = _const(np.tril(np.ones((tm, tm)), -1), BF16)
    row = pl.BlockSpec((tm, LANE), lambda i: (i, 0))
    return pl.pallas_call(
        _route_kernel,
        grid=(nt,),
        in_specs=[pl.BlockSpec((tm, d), lambda i: (i, 0)), _const_spec((1, d)), _const_spec((d, LANE)),
                  _const_spec((tm, tm))],
        out_specs=[row, row, pl.BlockSpec((SUBLANE, tm), lambda i: (0, i)),
                   pl.BlockSpec((nsub, 1, LANE), lambda i: (i, 0, 0)), pl.BlockSpec((1, LANE), lambda i: (0, 0))],
        out_shape=[jax.ShapeDtypeStruct((s, LANE), F32), jax.ShapeDtypeStruct((s, LANE), F32),
                   jax.ShapeDtypeStruct((SUBLANE, s), F32), jax.ShapeDtypeStruct((nt * nsub, 1, LANE), F32),
                   jax.ShapeDtypeStruct((1, LANE), F32)],
        scratch_shapes=[pltpu.VMEM((1, LANE), F32)],
        compiler_params=_params("arbitrary"),
        name="moe_route",
    )(x, norm_gain.reshape(1, d).astype(F32), router_p, ltri)


def _moe_ffn_kernel(be_ref, r0_ref, tlo_ref, thi_ref, nv_ref, h_ref, post_ref, wg_ref, wu_ref, wd_ref,
                    y_ref, x_sc, acc_sc, *, tt, nsub):
    b = pl.program_id(0)
    f = pl.program_id(1)
    nb = pl.num_programs(0)
    valid = b < nv_ref[0]
    tmb, d = acc_sc.shape
    gr = tmb // nsub
    last_tile = h_ref.shape[0] // tt - 1

    def picked(e, want, t):
        t0 = pl.multiple_of(t * tt, tt)
        p = post_ref[pl.ds(e, 1), pl.ds(t0, tt)]
        sel = jnp.where(p == want, 1.0, 0.0).astype(BF16)
        return _dot(sel, h_ref[pl.ds(t0, tt), :])

    def gather_head(blk, sb):
        e = be_ref[blk]
        lo = tlo_ref[blk * nsub + sb]
        hi = thi_ref[blk * nsub + sb]
        want = (lax.broadcasted_iota(jnp.int32, (gr, 1), 0) + (r0_ref[blk] + sb * gr)).astype(F32)
        rows = picked(e, want, lo)
        for k in range(1, MOE_GATHER_TILES):
            rows = rows + picked(e, jnp.where(lo + k <= hi, want, -2.0), jnp.minimum(lo + k, last_tile))
        return rows.astype(BF16)

    def gather_tail(slot, blk, sb):
        e = be_ref[blk]
        want = (lax.broadcasted_iota(jnp.int32, (gr, 1), 0) + (r0_ref[blk] + sb * gr)).astype(F32)
        rows = pl.ds(pl.multiple_of(sb * gr, gr), gr)

        def more(t, carry):
            x_sc[slot, rows, :] = (x_sc[slot, rows, :].astype(F32) + picked(e, want, t)).astype(BF16)
            return carry

        lax.fori_loop(tlo_ref[blk * nsub + sb] + MOE_GATHER_TILES, thi_ref[blk * nsub + sb] + 1, more, 0)

    @pl.when((b == 0) & (f == 0))
    def _():
        for sb in range(nsub):
            x_sc[0, pl.ds(sb * gr, gr), :] = gather_head(0, sb)
            gather_tail(0, 0, sb)

    nxt = jnp.minimum(b + 1, nb - 1)
    nslot = (b + 1) % 2

    @pl.when(valid)
    def _():
        x_sc[nslot, pl.ds(pl.multiple_of(f * gr, gr), gr), :] = gather_head(nxt, f)
        xb = x_sc[b % 2]
        a = _silu(_dot(xb, wg_ref[0])) * _dot(xb, wu_ref[0])
        acc_sc[...] = jnp.where(f == 0, 0.0, acc_sc[...]) + _dot(a.astype(BF16), wd_ref[0].astype(BF16))

    @pl.when(valid & (thi_ref[nxt * nsub + f] - tlo_ref[nxt * nsub + f] >= MOE_GATHER_TILES))
    def _():
        gather_tail(nslot, nxt, f)

    @pl.when(f == nsub - 1)
    def _():
        y_ref[...] = jnp.where(valid, acc_sc[...], 0.0).astype(y_ref.dtype)


def moe_ffn(h, post, sched, w_gate, w_up, w_down):
    s, d = h.shape
    ne, _, dff = w_gate.shape
    tmb = MOE_ROW_BLOCK
    tt = min(MOE_TOK_TILE, s)
    nb = sched[0].shape[0]
    nf = tmb // MOE_GATHER_ROWS
    tf = dff // nf

    def fidx(b, f, nv):
        return jnp.where(b < nv[0], f, nf - 1)

    grid_spec = pltpu.PrefetchScalarGridSpec(
        num_scalar_prefetch=5,
        grid=(nb, nf),
        in_specs=[pl.BlockSpec((s, d), lambda b, f, *_: (0, 0), pipeline_mode=pl.Buffered(1)),
                  pl.BlockSpec((SUBLANE, s), lambda b, f, *_: (0, 0), pipeline_mode=pl.Buffered(1)),
                  pl.BlockSpec((1, d, tf), lambda b, f, be, r0, tlo, thi, nv: (be[b], 0, fidx(b, f, nv))),
                  pl.BlockSpec((1, d, tf), lambda b, f, be, r0, tlo, thi, nv: (be[b], 0, fidx(b, f, nv))),
                  pl.BlockSpec((1, tf, d), lambda b, f, be, r0, tlo, thi, nv: (be[b], fidx(b, f, nv), 0))],
        out_specs=pl.BlockSpec((tmb, d), lambda b, f, *_: (b, 0)),
        scratch_shapes=[pltpu.VMEM((2, tmb, d), BF16), pltpu.VMEM((tmb, d), F32)],
    )
    return pl.pallas_call(
        functools.partial(_moe_ffn_kernel, tt=tt, nsub=nf),
        grid_spec=grid_spec,
        out_shape=jax.ShapeDtypeStruct((nb * tmb, d), BF16),
        compiler_params=pltpu.CompilerParams(dimension_semantics=("arbitrary", "arbitrary"),
                                             vmem_limit_bytes=MOE_VMEM_LIMIT),
        name="moe_ffn",
    )(*sched, h, post, w_gate, w_up, w_down)


def _moe_combine_kernel(kb_ref, off_ref, lim_ref, x_ref, pos_ref, gate_ref, fg_ref, *rest):
    y_refs, o_ref, acc_sc = rest[:-2], rest[-2], rest[-1]
    t = pl.program_id(0)
    yb = y_refs[0].shape[0]
    pos = pos_ref[...]
    gate = gate_ref[...]
    col = lax.broadcasted_iota(jnp.int32, (1, yb), 1).astype(F32)

    def routed(e):
        pe = pos[:, e:e + 1]
        ge = gate[:, e:e + 1]
        r = jnp.where(pe >= 0.0, pe + off_ref[t * N_EXPERTS + e].astype(F32), -1.0)
        return r, ge

    def picked(e, k, r):
        sel = jnp.where(r == col + float(k * yb), 1.0, 0.0).astype(BF16)
        return _dot(sel, y_refs[MOE_Y_FETCH * e + k][...])

    col2 = lax.broadcasted_iota(jnp.int32, (1, 2 * yb), 1).astype(F32)
    acc = x_ref[...]
    for e in range(N_EXPERTS):
        r, ge = routed(e)
        pair = jnp.concatenate([y_refs[MOE_Y_FETCH * e][...], y_refs[MOE_Y_FETCH * e + 1][...]], axis=0)
        acc = acc + ge * _dot(jnp.where(r == col2, 1.0, 0.0).astype(BF16), pair)
    acc_sc[...] = acc
    for e in range(N_EXPERTS):
        for k in range(2, MOE_Y_FETCH):
            @pl.when(lim_ref[t * N_EXPERTS + e] > k * yb)
            def _(k=k, e=e):
                r, ge = routed(e)
                acc_sc[...] += ge * picked(e, k, r)
    o_ref[...] = _rms(acc_sc[...], fg_ref[...]).astype(o_ref.dtype)


def moe_combine(x, pos, gate, y, kb, off, lim, final_gain):
    s, d = x.shape
    tm = min(MOE_TOK_TILE, s)
    yb = MOE_Y_BLOCK
    last = y.shape[0] // yb - 1

    def yspec(e, k):
        def index(t, kb_r, off_r, lim_r):
            blk = jnp.minimum(kb_r[t * N_EXPERTS + e] + k, last)
            return (blk if k < 1 else jnp.where(lim_r[t * N_EXPERTS + e] > k * yb, blk, 0), 0)
        return pl.BlockSpec((yb, d), index)

    grid_spec = pltpu.PrefetchScalarGridSpec(
        num_scalar_prefetch=3,
        grid=(s // tm,),
        in_specs=[pl.BlockSpec((tm, d), lambda t, *_: (t, 0)),
                  pl.BlockSpec((tm, LANE), lambda t, *_: (t, 0)),
                  pl.BlockSpec((tm, LANE), lambda t, *_: (t, 0)),
                  pl.BlockSpec((1, d), lambda t, *_: (0, 0))]
        + [yspec(e, k) for e in range(N_EXPERTS) for k in range(MOE_Y_FETCH)],
        out_specs=pl.BlockSpec((tm, d), lambda t, *_: (t, 0)),
        scratch_shapes=[pltpu.VMEM((tm, d), F32)],
    )
    return pl.pallas_call(
        _moe_combine_kernel,
        grid_spec=grid_spec,
        out_shape=jax.ShapeDtypeStruct((s, d), F32),
        compiler_params=_params("arbitrary"),
        name="moe_combine",
    )(kb, off, lim, x, pos, gate, final_gain.reshape(1, d).astype(F32), *([y] * (MOE_Y_FETCH * N_EXPERTS)))


def _moe_schedule(before, total, s):
    tmb, yb, gr = MOE_ROW_BLOCK, MOE_Y_BLOCK, MOE_GATHER_ROWS
    nb = 2 * s // tmb + N_EXPERTS
    counts = total[0, :N_EXPERTS].astype(jnp.int32)
    nblk = (counts + tmb - 1) // tmb
    end = jnp.cumsum(nblk)
    first = end - nblk
    nvalid = end[-1]
    b = jnp.minimum(jnp.arange(nb, dtype=jnp.int32), nvalid - 1)
    blk_e = jnp.sum(b[:, None] >= end[None, :], axis=1).astype(jnp.int32)
    r0 = (b - first[blk_e]) * tmb
    cb = before[:, 0, :N_EXPERTS].astype(jnp.int32)
    r0s = (r0[:, None] + gr * jnp.arange(tmb // gr, dtype=jnp.int32)[None, :]).reshape(-1)
    cbe = cb[:, jnp.repeat(blk_e, tmb // gr)]
    tlo = (jnp.sum(cbe <= r0s[None, :], axis=0) - 1).astype(jnp.int32)
    thi = (jnp.sum(cbe < (r0s + gr)[None, :], axis=0) - 1).astype(jnp.int32)
    sched = (blk_e, r0.astype(jnp.int32), tlo, thi, nvalid.reshape(1).astype(jnp.int32))
    row_start = first[None, :] * tmb + cb
    kb = row_start // yb
    off = first[None, :] * tmb - kb * yb
    n_te = jnp.concatenate([cb[1:], counts[None, :]], axis=0) - cb
    lim = row_start - kb * yb + n_te
    flat = lambda a: a.reshape(-1).astype(jnp.int32)
    return sched, flat(kb), flat(off), flat(lim)


def moe_sparse(h, x, norm_gain, router, w_gate, w_up, w_down, final_gain):
    s, _ = x.shape
    pos, gate, post, before, total = moe_route(x, norm_gain, router)
    sched, kb, off, lim = _moe_schedule(before, total, s)
    y = moe_ffn(h, post, sched, w_gate, w_up, w_down)
    return moe_combine(x, pos, gate, y, kb, off, lim, final_gain)


FOX_F0 = 7 * BRANCH_W


def _split_w_in_kernel(wt_ref, mix_ref, f_ref, gate_ref):
    rest0 = FOX_F0 + N_HEADS
    gate0 = N_MIX_COLS + N_HEADS
    mix_ref[:, :FOX_F0] = wt_ref[pl.ds(0, FOX_F0), :].T.astype(BF16)
    mix_ref[:, FOX_F0:] = wt_ref[pl.ds(rest0, N_MIX_COLS - FOX_F0), :].T.astype(BF16)
    gate_ref[...] = wt_ref[pl.ds(gate0, N_BRANCH * D_MODEL), :].T.astype(BF16)
    f_rows = jnp.concatenate([wt_ref[pl.ds(FOX_F0, N_HEADS), :], jnp.zeros((LANE - N_HEADS, LANE), F32)], axis=0)
    f_ref[...] = f_rows.T.astype(BF16)


def _mixer_weights(w_in, layer, tr=LANE):
    _, d, cols = w_in.shape
    w_t = jnp.swapaxes(w_in, 1, 2)
    row = lambda w: pl.BlockSpec((tr, w), lambda i: (i, 0))
    return pl.pallas_call(
        _split_w_in_kernel,
        grid=(d // tr,),
        in_specs=[pl.BlockSpec((None, cols, tr), lambda i: (layer, 0, i))],
        out_specs=[row(N_MIX_COLS), row(LANE), row(N_BRANCH * D_MODEL)],
        out_shape=[jax.ShapeDtypeStruct((d, N_MIX_COLS), BF16), jax.ShapeDtypeStruct((d, LANE), BF16),
                   jax.ShapeDtypeStruct((d, N_BRANCH * D_MODEL), BF16)],
        compiler_params=_params("parallel"),
        name="split_w_in",
    )(w_t)


def kernel(x, w_in, w_branch, w_out, norm_mix_g, hgrn_lb_logits, hgrn_norm_g, fox_f_bias, pool_w, pool_scale,
           ret_gn_g, ret_gn_b, norm_ffn_g, ffn_w_gate, ffn_w_up, ffn_w_down, moe_router, moe_w_gate, moe_w_up,
           moe_w_down, final_norm_g):
    b, s, d = x.shape
    assert b == 1 and d == D_MODEL
    depth = w_in.shape[0]
    assert depth == 2, "layer 0 uses the dense FFN, layer 1 the experts and the final norm"
    xs = x.reshape(s, d)
    h = None
    out = None
    for layer in range(depth):
        w_mix, w_f, w_gate = _mixer_weights(w_in, layer)
        if layer == 0:
            proj, h = norm_matmul(xs, norm_mix_g[0], w_mix)
        else:
            proj = matmul(h, w_mix)
        ct, c_rows, fox_stats = fox_gate(h, proj, w_f, fox_f_bias[layer])
        branches = (
            hgrn2(proj, hgrn_lb_logits, hgrn_norm_g[layer], layer),
            fox_attention(proj, ct, c_rows, fox_stats),
            pool_mixer(proj, pool_w[layer], pool_scale[layer]),
            retention(proj, ret_gn_g[layer], ret_gn_b[layer]),
        )
        xs, h2 = merge(h, branches, xs, w_gate, w_branch[layer].astype(BF16), w_out[layer].astype(BF16),
                       norm_ffn_g[layer])
        if layer % 2 == 0:
            li = layer // 2
            xs, h = ffn_dense(h2, xs, ffn_w_gate[li], ffn_w_up[li], ffn_w_down[li], norm_mix_g[layer + 1])
        else:
            li = layer // 2
            out = moe_sparse(h2, xs, norm_ffn_g[layer], moe_router[li], moe_w_gate[li].astype(BF16),
                             moe_w_up[li].astype(BF16), moe_w_down[li], final_norm_g)
    return out.reshape(b, s, d)
```

```python
import functools
import math

import jax
import jax.numpy as jnp
import numpy as np
from jax import lax
from jax.experimental import pallas as pl
from jax.experimental.pallas import tpu as pltpu

D_MODEL = 1024
N_BRANCH = 4
BRANCH_W = D_MODEL // N_BRANCH
HEAD_DIM = 64
N_HEADS = BRANCH_W // HEAD_DIM
POOL_WINDOWS = (2, 4, 8, 16)
POOL_GROUP = BRANCH_W // len(POOL_WINDOWS)
POOL_HALO = 16
RET_DECAY_BASE = 5.0
ROPE_BASE = 10000.0
D_FF = 7 * D_MODEL // 2
N_EXPERTS = 8
RMS_EPS = 1e-6
LN_EPS = 1e-5
N_MIX_COLS = 12 * BRANCH_W

LANE = 128
SUBLANE = 8
VMEM_LIMIT = 56 * 1024 * 1024

HG_CHUNK = 64
HG_SUB = 16
HG_GROUP = 1
HG_FAST_MIN_LOGDECAY = -60.0
RET_CHUNK = 256

F32 = jnp.float32
BF16 = jnp.bfloat16
NT_DIMS = (((1,), (1,)), ((), ()))


def _params(*sem):
    return pltpu.CompilerParams(dimension_semantics=sem, vmem_limit_bytes=VMEM_LIMIT)


def _const_spec(shape):
    nd = len(shape)
    return pl.BlockSpec(shape, lambda *_: (0,) * nd, pipeline_mode=pl.Buffered(1))


def _split3(x):
    hi = x.astype(BF16)
    r1 = x - hi.astype(F32)
    mid = r1.astype(BF16)
    lo = (r1 - mid.astype(F32)).astype(BF16)
    return hi, mid, lo


def _dot(a, b):
    return jnp.dot(a, b, preferred_element_type=F32)


def _dot_nt(a, b):
    return lax.dot_general(a, b, NT_DIMS, preferred_element_type=F32)


def _dot_exact_rhs(x, m_bf16):
    hi, mid, lo = _split3(x)
    return _dot(hi, m_bf16) + _dot(mid, m_bf16) + _dot(lo, m_bf16)


def _dot_exact_lhs(m_bf16, x):
    hi, mid, lo = _split3(x)
    return _dot(m_bf16, hi) + _dot(m_bf16, mid) + _dot(m_bf16, lo)


def _sigmoid(x):
    return 1.0 / (1.0 + jnp.exp(-x))


def _silu(x):
    return x * _sigmoid(x)


def _rms(x, gain):
    return x * lax.rsqrt(jnp.mean(x * x, axis=-1, keepdims=True) + RMS_EPS) * gain


def _const(a, dtype=F32):
    return jnp.asarray(np.asarray(a, np.float32), dtype)


def _head_of(n):
    return np.arange(n) // HEAD_DIM


def _head_ones():
    h = _head_of(BRANCH_W)
    return _const(h[:, None] == h[None, :], BF16)


def _head_masks():
    return _const(_head_of(BRANCH_W)[None, :] == np.arange(N_HEADS)[:, None])


def _norm_matmul_kernel(x_ref, g_ref, b_ref, o_ref, h_ref):
    h = _rms(x_ref[...], g_ref[...]).astype(h_ref.dtype)
    h_ref[...] = h
    o_ref[...] = _dot(h, b_ref[...]).astype(o_ref.dtype)


def norm_matmul(x, gain, b, tm=2048, tn=1024):
    m, k = x.shape
    _, n = b.shape
    tm = min(tm, m)
    return pl.pallas_call(
        _norm_matmul_kernel,
        grid=(n // tn, m // tm),
        in_specs=[pl.BlockSpec((tm, k), lambda j, i: (i, 0)), _const_spec((1, k)),
                  pl.BlockSpec((k, tn), lambda j, i: (0, j))],
        out_specs=[pl.BlockSpec((tm, tn), lambda j, i: (i, j)), pl.BlockSpec((tm, k), lambda j, i: (i, 0))],
        out_shape=[jax.ShapeDtypeStruct((m, n), BF16), jax.ShapeDtypeStruct((m, k), BF16)],
        compiler_params=_params("arbitrary", "arbitrary"),
        name="in_proj_norm",
    )(x, gain.reshape(1, k).astype(F32), b)


def _matmul_kernel(a_ref, b_ref, o_ref):
    o_ref[...] = _dot(a_ref[...], b_ref[...]).astype(o_ref.dtype)


def matmul(a, b, out_dtype=BF16, tm=2048, tn=1024):
    m, k = a.shape
    _, n = b.shape
    tm = min(tm, m)
    return pl.pallas_call(
        _matmul_kernel,
        grid=(n // tn, m // tm),
        in_specs=[pl.BlockSpec((tm, k), lambda j, i: (i, 0)),
                  pl.BlockSpec((k, tn), lambda j, i: (0, j))],
        out_specs=pl.BlockSpec((tm, tn), lambda j, i: (i, j)),
        out_shape=jax.ShapeDtypeStruct((m, n), out_dtype),
        compiler_params=_params("parallel", "parallel"),
        name="in_proj",
    )(a, b)


def _hgrn_kernel(q_ref, f_ref, i_ref, g_ref, lbl_ref, ng_ref, ones_ref, hm_ref, tril_ref, halfsum_ref, fmask_ref,
                 o_ref, st_ref, bpad, kpad, vpad, astack, lf_sc, kk_sc, o_sc, *, layer, tile):
    c, sub = HG_CHUNK, HG_SUB
    nsub = c // sub
    half = c // 2

    @pl.when(pl.program_id(0) == 0)
    def _():
        st_ref[...] = jnp.zeros_like(st_ref)
        bpad[...] = jnp.zeros_like(bpad)
        kpad[...] = jnp.zeros_like(kpad)
        vpad[...] = jnp.zeros_like(vpad)

    lbl = lbl_ref[...]
    e = jnp.exp(lbl - jnp.max(lbl, axis=0, keepdims=True))
    p = e / jnp.sum(e, axis=0, keepdims=True)
    lb = jnp.zeros((1, BRANCH_W), F32)
    for l in range(1, layer + 1):
        lb = lb + p[l:l + 1, :]

    ones_bd = ones_ref[...]
    hm = hm_ref[...]
    tril = tril_ref[...]
    row = lax.broadcasted_iota(jnp.int32, (c, 1), 0)
    row_in_sub = row % sub
    bd_mask = ones_bd.astype(F32)

    sig = _sigmoid(f_ref[...].astype(F32))
    logf_all = jnp.log(lb + (1.0 - lb) * sig)
    lf_sc[...] = _dot_exact_lhs(tril, logf_all)
    kk_sc[...] = (1.0 - lb) * (1.0 - sig)
    min_decay = jnp.min(_dot(halfsum_ref[...], logf_all.astype(BF16)))

    def load(ci):
        r0 = pl.multiple_of(ci * c, c)
        q = q_ref[pl.ds(r0, c), :].astype(F32)
        v = i_ref[pl.ds(r0, c), :].astype(F32)
        kk = kk_sc[pl.ds(r0, c), :]
        b = lf_sc[pl.ds(r0, c), :]
        return r0, q, v, kk, b

    def finish(r0, q_decayed, v, kk, b, intra):
        st = st_ref[...]
        inter = _dot_nt(q_decayed.astype(BF16), st.astype(BF16))
        b_last = b[c - 1:c, :]
        ks_end = (kk * jnp.exp(b_last - b)).astype(BF16)
        upd = _dot(v.T.astype(BF16), ks_end)
        st_ref[...] = st * jnp.exp(b_last) + upd * bd_mask
        o_sc[pl.ds(r0, c), :] = intra + inter

    def fast_group(gi, carry):
        second = row >= half
        chunks, qxs, kaugs, vaugs = [], [], [], []
        for j in range(HG_GROUP):
            r0, q, v, kk, b = load(gi * HG_GROUP + j)
            m_row = b[half - 1:half, :]
            mref = jnp.where(second, m_row, 0.0)
            qp = q * jnp.exp(b - mref)
            kp = kk * jnp.exp(mref - b)
            e_m = jnp.exp(m_row)
            kaugs.append(jnp.concatenate([kp, kp[:half, :] * e_m], axis=0))
            vaugs.append(jnp.concatenate([v, v[:half, :]], axis=0))
            qxs.extend(qp * hm[h:h + 1, :] for h in range(N_HEADS))
            chunks.append((r0, jnp.where(second, qp * e_m, qp), v, kk, b))
        qx = jnp.concatenate(qxs, axis=0).astype(BF16)
        kaug = jnp.concatenate(kaugs, axis=0).astype(BF16)
        vaug = jnp.concatenate(vaugs, axis=0).astype(BF16)
        sc = jnp.where(fmask_ref[...] > 0.0, _dot_nt(qx, kaug), 0.0)
        r = _dot(sc.astype(BF16), vaug)
        for j, (r0, q_decayed, v, kk, b) in enumerate(chunks):
            intra = jnp.zeros((c, BRANCH_W), F32)
            for h in range(N_HEADS):
                lo = (j * N_HEADS + h) * c
                intra = intra + r[lo:lo + c, :] * hm[h:h + 1, :]
            finish(r0, q_decayed, v, kk, b, intra)
        return carry

    def exact_chunk(ci, carry):
        r0, q, v, kk, b = load(ci)

        bpad[pl.ds(sub, c), :] = b
        kpad[pl.ds(sub, c), :] = kk
        vpad[pl.ds(sub, c), :] = v

        for d in range(sub):
            b_d = bpad[pl.ds(sub - d, c), :]
            k_d = kpad[pl.ds(sub - d, c), :]
            a = jnp.where(row_in_sub >= d, q * k_d * jnp.exp(b - b_d), 0.0)
            astack[pl.ds(d * c, c), :] = a.astype(BF16)
        pall = _dot(astack[...], ones_bd)
        intra = jnp.zeros((c, BRANCH_W), F32)
        for d in range(sub):
            intra = intra + pall[d * c:(d + 1) * c, :] * vpad[pl.ds(sub - d, c), :]

        pieces = [jnp.zeros((sub, BRANCH_W), F32)]
        for si in range(1, nsub):
            lo = si * sub
            m_i = b[lo - 1:lo, :]
            qs = q[lo:lo + sub, :] * jnp.exp(b[lo:lo + sub, :] - m_i)
            ks = (kk[:lo, :] * jnp.exp(m_i - b[:lo, :])).astype(BF16)
            qx = jnp.concatenate([qs * hm[h:h + 1, :] for h in range(N_HEADS)], axis=0).astype(BF16)
            sc = _dot_nt(qx, ks)
            r = _dot(sc.astype(BF16), v[:lo, :].astype(BF16))
            acc = jnp.zeros((sub, BRANCH_W), F32)
            for h in range(N_HEADS):
                acc = acc + r[h * sub:(h + 1) * sub, :] * hm[h:h + 1, :]
            pieces.append(acc)
        intra = intra + jnp.concatenate(pieces, axis=0)
        finish(r0, q * jnp.exp(b), v, kk, b, intra)
        return carry

    lax.cond(min_decay >= HG_FAST_MIN_LOGDECAY,
             lambda: lax.fori_loop(0, tile // (c * HG_GROUP), fast_group, 0, unroll=2),
             lambda: lax.fori_loop(0, tile // c, exact_chunk, 0))

    o = o_sc[...]
    ms = _dot_exact_rhs(o * o, ones_bd) * (1.0 / HEAD_DIM)
    y = o * lax.rsqrt(ms + RMS_EPS) * ng_ref[...] * _silu(g_ref[...].astype(F32))
    o_ref[...] = y.astype(o_ref.dtype)


def hgrn2(proj, lb_logits, norm_g, layer, tile=512):
    s = proj.shape[0]
    depth = lb_logits.shape[0]
    c, sub = HG_CHUNK, HG_SUB
    half = c // 2
    tile = min(tile, s)
    col = lambda j: pl.BlockSpec((tile, BRANCH_W), lambda i, j=j: (i, j))
    pos = np.arange(tile)
    tril = _const((pos[:, None] // c == pos[None, :] // c) & (pos[None, :] <= pos[:, None]), BF16)
    nhalf = tile // half
    halfsum = _const(np.arange(tile)[None, :] // half == np.arange(nhalf)[:, None], BF16)
    t = np.arange(c)[:, None]
    col_s = np.arange(c + half)[None, :]
    same_half = (col_s < c) & (col_s // half == t // half) & (col_s <= t)
    cross = (col_s >= c) & (t >= half)
    fmask1 = np.tile((same_half | cross).astype(np.float32), (N_HEADS, 1))
    fmask = _const(np.kron(np.eye(HG_GROUP, dtype=np.float32), fmask1))
    return pl.pallas_call(
        functools.partial(_hgrn_kernel, layer=layer, tile=tile),
        grid=(s // tile,),
        in_specs=[col(0), col(1), col(2), col(3),
                  _const_spec((depth, BRANCH_W)), _const_spec((1, BRANCH_W)),
                  _const_spec((BRANCH_W, BRANCH_W)), _const_spec((N_HEADS, BRANCH_W)),
                  _const_spec((tile, tile)), _const_spec((nhalf, tile)),
                  _const_spec((HG_GROUP * N_HEADS * c, HG_GROUP * (c + half)))],
        out_specs=pl.BlockSpec((tile, BRANCH_W), lambda i: (i, 0)),
        out_shape=jax.ShapeDtypeStruct((s, BRANCH_W), BF16),
        scratch_shapes=[pltpu.VMEM((BRANCH_W, BRANCH_W), F32),
                        pltpu.VMEM((c + sub, BRANCH_W), F32),
                        pltpu.VMEM((c + sub, BRANCH_W), F32),
                        pltpu.VMEM((c + sub, BRANCH_W), F32),
                        pltpu.VMEM((sub * c, BRANCH_W), BF16),
                        pltpu.VMEM((tile, BRANCH_W), F32),
                        pltpu.VMEM((tile, BRANCH_W), F32),
                        pltpu.VMEM((tile, BRANCH_W), F32)],
        compiler_params=_params("arbitrary"),
        name="hgrn2",
    )(proj, proj, proj, proj, lb_logits.astype(F32), norm_g.reshape(1, BRANCH_W).astype(F32),
      _head_ones(), _head_masks(), tril, halfsum, fmask)


FOX_TILE = 256
FOX_GATE_BLOCKS = 2
FOX_NSTAT = 16
FOX_FIXED_MAX = 30.0
FOX_SKIP_LOG = 40.0


def _fox_gate_kernel(h_ref, q_ref, k_ref, wf_ref, bias_ref, tril_ref, ones_ref,
                     ct_ref, c_ref, stat_ref, carry_ref, kmax_ref):
    @pl.when(pl.program_id(0) == 0)
    def _():
        carry_ref[...] = jnp.zeros_like(carry_ref)
        kmax_ref[...] = jnp.zeros_like(kmax_ref)

    logit = _dot(h_ref[...], wf_ref[...]) + bias_ref[...]
    logf = jnp.minimum(logit, 0.0) - jnp.log(1.0 + jnp.exp(-jnp.abs(logit)))
    cum = _dot_exact_lhs(tril_ref[...], logf) + carry_ref[...]
    carry_ref[...] = cum[-1:, :]
    ct_ref[...] = cum.T[:SUBLANE, :]
    c_ref[...] = cum

    ones_bd = ones_ref[...]
    q = q_ref[...].astype(F32)
    k = k_ref[...].astype(F32)
    scale = HEAD_DIM ** -0.5
    head_lane = lax.broadcasted_iota(jnp.int32, (1, BRANCH_W), 1) // HEAD_DIM
    c_heads = jnp.zeros(q.shape, F32)
    for h in range(N_HEADS):
        c_heads = jnp.where(head_lane == h, cum[:, h:h + 1], c_heads)
    slack = 1.0 + 2.0 ** -6
    qn = jnp.sqrt(_dot((q * q).astype(BF16), ones_bd)) * (scale * slack)
    kn = jnp.sqrt(_dot((k * k).astype(BF16), ones_bd)) * slack
    diag = _dot((q * k).astype(BF16), ones_bd) * scale - (2.0 ** -6) * qn * kn
    e_row = c_heads - diag
    kmax = kmax_ref[...]
    for blk in range(q.shape[0] // FOX_TILE):
        rs = slice(blk * FOX_TILE, (blk + 1) * FOX_TILE)
        kmax = jnp.maximum(kmax, jnp.max(kn[rs], axis=0, keepdims=True))
        rows = [jnp.max(qn[rs], axis=0, keepdims=True),
                jnp.max(e_row[rs], axis=0, keepdims=True),
                kmax,
                c_heads[(blk + 1) * FOX_TILE - 1:(blk + 1) * FOX_TILE, :]]
        stat_ref[blk] = jnp.concatenate(rows + [jnp.zeros((SUBLANE - len(rows), BRANCH_W), F32)], axis=0)
    kmax_ref[...] = kmax


def fox_gate(h, proj, w_f, f_bias):
    s, d = h.shape
    tile = min(FOX_GATE_BLOCKS * FOX_TILE, s)
    nblk = tile // FOX_TILE
    bias = jnp.zeros((1, LANE), F32).at[0, :N_HEADS].set(f_bias.astype(F32))
    tril = _const(np.tril(np.ones((tile, tile))), BF16)
    ct, c_rows, stats = pl.pallas_call(
        _fox_gate_kernel,
        grid=(s // tile,),
        in_specs=[pl.BlockSpec((tile, d), lambda i: (i, 0)),
                  pl.BlockSpec((tile, BRANCH_W), lambda i: (i, 4)),
                  pl.BlockSpec((tile, BRANCH_W), lambda i: (i, 5)),
                  _const_spec((d, LANE)), _const_spec((1, LANE)), _const_spec((tile, tile)),
                  _const_spec((BRANCH_W, BRANCH_W))],
        out_specs=[pl.BlockSpec((SUBLANE, tile), lambda i: (0, i)),
                   pl.BlockSpec((tile, LANE), lambda i: (i, 0)),
                   pl.BlockSpec((nblk, SUBLANE, BRANCH_W), lambda i: (i, 0, 0))],
        out_shape=[jax.ShapeDtypeStruct((SUBLANE, s), F32),
                   jax.ShapeDtypeStruct((s, LANE), F32),
                   jax.ShapeDtypeStruct((s // FOX_TILE, SUBLANE, BRANCH_W), F32)],
        scratch_shapes=[pltpu.VMEM((1, LANE), F32), pltpu.VMEM((1, BRANCH_W), F32)],
        compiler_params=_params("arbitrary"),
        name="fox_gate",
    )(h, proj, proj, w_f, bias, tril, _head_ones())
    return ct, c_rows, stats[:, :4, ::HEAD_DIM].reshape(-1)


def _fox_kernel(stat_ref, q_ref, k_ref, v_ref, ct_ref, c_ref, hm_ref, o_ref, m_sc, l_sc, acc_sc, *, tq):
    i = pl.program_id(0)
    q0 = pl.multiple_of(i * tq, tq)
    hm = hm_ref[...]
    q = q_ref[...].astype(F32) * (HEAD_DIM ** -0.5)
    qh = [(q * hm[h:h + 1, :]).astype(BF16) for h in range(N_HEADS)]
    c_q0 = ct_ref[:, pl.ds(q0, tq)][:, 0:1]

    first = i
    for h in range(N_HEADS):
        qmax = stat_ref[i * FOX_NSTAT + h]
        emax = stat_ref[i * FOX_NSTAT + N_HEADS + h]

        def needed(j, h=h, qmax=qmax, emax=emax):
            jc = jnp.maximum(j, 0)
            bound = (qmax * stat_ref[jc * FOX_NSTAT + 2 * N_HEADS + h] + emax
                     - stat_ref[jc * FOX_NSTAT + 3 * N_HEADS + h])
            return (j >= 0) & (bound >= -FOX_SKIP_LOG)

        last_dropped = lax.while_loop(needed, lambda j: j - 1, i - 1)
        first = jnp.minimum(first, last_dropped + 1)

    l_sc[...] = jnp.zeros_like(l_sc)
    acc_sc[...] = jnp.zeros_like(acc_sc)

    def causal(sc):
        r = lax.broadcasted_iota(jnp.int32, (tq, tq), 0)
        cidx = lax.broadcasted_iota(jnp.int32, (tq, tq), 1)
        return jnp.where(cidx <= r, sc, -jnp.inf)

    def online_block(s0, diagonal):
        kb = k_ref[pl.ds(s0, tq), :]
        vb = v_ref[pl.ds(s0, tq), :]
        bias = c_q0 - ct_ref[:, pl.ds(s0, tq)]
        for h in range(N_HEADS):
            sc = _dot_nt(qh[h], kb) + bias[h:h + 1, :]
            if diagonal:
                sc = causal(sc)
            m_prev = m_sc[h]
            m_new = jnp.maximum(m_prev, jnp.max(sc, axis=1, keepdims=True))
            alpha = jnp.exp(m_prev - m_new)
            p = jnp.exp(sc - jnp.tile(m_new, (1, tq // LANE)))
            l_sc[h] = alpha * l_sc[h] + jnp.sum(p, axis=1, keepdims=True)
            acc_sc[h] = acc_sc[h] * jnp.tile(alpha, (1, BRANCH_W // LANE)) + _dot(p.astype(BF16), vb)
            m_sc[h] = m_new

    tops = [stat_ref[i * FOX_NSTAT + h] * stat_ref[i * FOX_NSTAT + 2 * N_HEADS + h] for h in range(N_HEADS)]
    c_tile = c_ref[...]
    shift = [c_tile[:, h:h + 1] - c_q0[h:h + 1, :] - tops[h] for h in range(N_HEADS)]

    def fixed_block(s0, diagonal):
        kb = k_ref[pl.ds(s0, tq), :]
        vb = v_ref[pl.ds(s0, tq), :]
        bias = c_q0 - ct_ref[:, pl.ds(s0, tq)]
        for h in range(N_HEADS):
            sc = _dot_nt(qh[h], kb) + bias[h:h + 1, :] + shift[h]
            if diagonal:
                sc = causal(sc)
            p = jnp.exp(sc)
            l_sc[h] += p[:, :LANE] + p[:, LANE:]
            acc_sc[h] += _dot(p.astype(BF16), vb)

    def run(block, row_sum):
        lax.fori_loop(first, i, lambda j, carry: (block(pl.multiple_of(j * tq, tq), False), carry)[1], 0)
        block(q0, True)
        out = jnp.zeros((tq, BRANCH_W), F32)
        for h in range(N_HEADS):
            out = out + acc_sc[h] * hm[h:h + 1, :] / row_sum(l_sc[h])
        o_ref[...] = out.astype(o_ref.dtype)

    def run_online():
        m_sc[...] = jnp.full_like(m_sc, -jnp.inf)
        run(online_block, lambda l: jnp.tile(l, (1, BRANCH_W // LANE)))

    def run_fixed():
        run(fixed_block, lambda l: jnp.sum(l, axis=1, keepdims=True))

    lax.cond(functools.reduce(jnp.maximum, tops) <= FOX_FIXED_MAX, run_fixed, run_online)


def fox_attention(proj, ct, c_rows, stats):
    s = proj.shape[0]
    tq = min(FOX_TILE, s)
    full = lambda j: pl.BlockSpec((s, BRANCH_W), lambda i, j=j: (0, j), pipeline_mode=pl.Buffered(1))
    return pl.pallas_call(
        functools.partial(_fox_kernel, tq=tq),
        grid=(s // tq,),
        in_specs=[pl.BlockSpec(memory_space=pltpu.SMEM),
                  pl.BlockSpec((tq, BRANCH_W), lambda i: (i, 4)), full(5), full(6),
                  _const_spec((SUBLANE, s)), pl.BlockSpec((tq, LANE), lambda i: (i, 0)),
                  _const_spec((N_HEADS, BRANCH_W))],
        out_specs=pl.BlockSpec((tq, BRANCH_W), lambda i: (i, 0)),
        out_shape=jax.ShapeDtypeStruct((s, BRANCH_W), BF16),
        scratch_shapes=[pltpu.VMEM((N_HEADS, tq, LANE), F32),
                        pltpu.VMEM((N_HEADS, tq, LANE), F32),
                        pltpu.VMEM((N_HEADS, tq, BRANCH_W), F32)],
        compiler_params=_params("parallel"),
        name="fox_attention",
    )(stats, proj, proj, proj, ct, c_rows, _head_masks())


def _pool_kernel(u_ref, w_ref, scale_ref, o_ref, ext, *, tile):
    i = pl.program_id(0)

    @pl.when(i == 0)
    def _():
        ext[pl.ds(0, POOL_HALO), :] = jnp.zeros((POOL_HALO, BRANCH_W), F32)

    u = u_ref[...].astype(F32)
    ext[pl.ds(POOL_HALO, tile), :] = u
    pos = (i * tile + lax.broadcasted_iota(jnp.int32, (tile, 1), 0) + 1).astype(F32)
    halves = []
    for half in range(BRANCH_W // LANE):
        lanes = pl.ds(half * LANE, LANE)
        w_small, w_big = POOL_WINDOWS[2 * half], POOL_WINDOWS[2 * half + 1]
        run = u[:, half * LANE:(half + 1) * LANE]
        sums = {}
        for j in range(1, w_big):
            if j == w_small:
                sums[w_small] = run
            run = run + ext[pl.ds(POOL_HALO - j, tile), lanes]
        sums[w_big] = run
        lane = lax.broadcasted_iota(jnp.int32, (1, LANE), 1)
        small = lane < POOL_GROUP
        total = jnp.where(small, sums[w_small], sums[w_big])
        count = jnp.where(small, jnp.minimum(pos, float(w_small)), jnp.minimum(pos, float(w_big)))
        halves.append(total / count)
    mean = jnp.concatenate(halves, axis=1)
    d = (mean - u).astype(BF16)
    y = _dot(d, w_ref[...]) * scale_ref[...]
    o_ref[...] = y.astype(o_ref.dtype)
    ext[pl.ds(0, POOL_HALO), :] = u[tile - POOL_HALO:, :]


def pool_mixer(proj, w_pool, scale, tile=2048):
    s = proj.shape[0]
    tile = min(tile, s)
    ng = len(POOL_WINDOWS)
    w_bd = jnp.zeros((BRANCH_W, BRANCH_W), F32)
    for gi in range(ng):
        lo = gi * POOL_GROUP
        w_bd = w_bd.at[lo:lo + POOL_GROUP, lo:lo + POOL_GROUP].set(w_pool[gi].astype(F32))
    return pl.pallas_call(
        functools.partial(_pool_kernel, tile=tile),
        grid=(s // tile,),
        in_specs=[pl.BlockSpec((tile, BRANCH_W), lambda i: (i, 7)),
                  _const_spec((BRANCH_W, BRANCH_W)), _const_spec((1, BRANCH_W))],
        out_specs=pl.BlockSpec((tile, BRANCH_W), lambda i: (i, 0)),
        out_shape=jax.ShapeDtypeStruct((s, BRANCH_W), BF16),
        scratch_shapes=[pltpu.VMEM((tile + POOL_HALO, BRANCH_W), F32)],
        compiler_params=_params("arbitrary"),
        name="pool_mixer",
    )(proj, w_bd.astype(BF16), scale.reshape(1, BRANCH_W).astype(F32))


def _ret_kernel(q_ref, k_ref, v_ref, g_ref, rope_ref, ecos_ref, esin_ref, perm_ref, ones_ref, hm_ref,
                dstack_ref, xi_ref, zeta_ref, gc_ref, gng_ref, gnb_ref, o_ref, st_ref, *, tile):
    c = RET_CHUNK

    @pl.when(pl.program_id(0) == 0)
    def _():
        st_ref[...] = jnp.zeros_like(st_ref)

    perm = perm_ref[...]
    ones_bd = ones_ref[...]
    bd_mask = ones_bd.astype(F32)
    hm = hm_ref[...]

    tab = rope_ref[...]
    cos = _dot_exact_rhs(tab, ecos_ref[...])
    sin = _dot_exact_rhs(tab, esin_ref[...])
    q_all = q_ref[...]
    k_all = k_ref[...]
    qr_all = q_all.astype(F32) * cos + _dot(q_all, perm) * sin
    kr_all = (k_all.astype(F32) * cos + _dot(k_all, perm) * sin) * (HEAD_DIM ** -0.5)

    outs = []
    for ci in range(tile // c):
        r0 = ci * c
        qr = qr_all[r0:r0 + c, :]
        kr = kr_all[r0:r0 + c, :]
        v = v_ref[pl.ds(r0, c), :]

        qx = jnp.concatenate([qr * hm[h:h + 1, :] for h in range(N_HEADS)], axis=0).astype(BF16)
        sc = _dot_nt(qx, kr.astype(BF16)) * dstack_ref[...]
        r = _dot(sc.astype(BF16), v)
        intra = jnp.zeros((c, BRANCH_W), F32)
        for h in range(N_HEADS):
            intra = intra + r[h * c:(h + 1) * c, :] * hm[h:h + 1, :]

        st = st_ref[...]
        inter = _dot_nt((qr * xi_ref[...]).astype(BF16), st.astype(BF16))
        upd = _dot(v.astype(F32).T.astype(BF16), (kr * zeta_ref[...]).astype(BF16))
        st_ref[...] = st * gc_ref[...] + upd * bd_mask
        outs.append(intra + inter)

    o = jnp.concatenate(outs, axis=0)
    mu = _dot_exact_rhs(o, ones_bd) * (1.0 / HEAD_DIM)
    cen = o - mu
    var = _dot_exact_rhs(cen * cen, ones_bd) * (1.0 / HEAD_DIM)
    y = cen * lax.rsqrt(var + LN_EPS) * gng_ref[...] + gnb_ref[...]
    o_ref[...] = (y * _silu(g_ref[...].astype(F32))).astype(o_ref.dtype)


def _rope_tables(s):
    half = HEAD_DIM // 2
    pos = np.arange(s, dtype=np.float64)
    inv_freq = ROPE_BASE ** (-np.arange(half, dtype=np.float64) / half)
    ang = pos[:, None] * inv_freq[None, :]
    table = np.concatenate([np.cos(ang), np.sin(ang), np.zeros((s, LANE - 2 * half))], axis=1)
    lane = np.arange(BRANCH_W)
    src = np.arange(LANE)[:, None]
    ecos = src == (lane % half)[None, :]
    sign = np.where(lane % HEAD_DIM < half, -1.0, 1.0)
    esin = (src == (half + lane % half)[None, :]) * sign[None, :]
    return _const(table), _const(ecos, BF16), _const(esin, BF16)


def _ret_constants():
    c = RET_CHUNK
    half = HEAD_DIM // 2
    lane = np.arange(BRANCH_W)
    partner = np.where(lane % HEAD_DIM < half, lane + half, lane - half)
    perm = lane[:, None] == partner[None, :]
    log_gamma = np.log1p(-np.exp2(-RET_DECAY_BASE - np.arange(N_HEADS, dtype=np.float64)))
    ci = np.arange(c, dtype=np.float64)
    diff = ci[:, None] - ci[None, :]
    intra = np.where(diff >= 0, np.exp(diff * log_gamma[:, None, None]), 0.0)
    dstack = intra.reshape(N_HEADS * c, c)
    lg_lane = np.repeat(log_gamma, HEAD_DIM)[None, :]
    xi = np.exp((ci[:, None] + 1.0) * lg_lane)
    zeta = np.exp((c - 1.0 - ci[:, None]) * lg_lane)
    gc = np.exp(c * lg_lane)
    return _const(perm, BF16), _const(dstack), _const(xi), _const(zeta), _const(gc)


def retention(proj, gn_g, gn_b, tile=512):
    s = proj.shape[0]
    c = RET_CHUNK
    tile = min(tile, s)
    rope, ecos, esin = _rope_tables(s)
    perm, dstack, xi, zeta, gc = _ret_constants()
    col = lambda j: pl.BlockSpec((tile, BRANCH_W), lambda i, j=j: (i, j))
    row = pl.BlockSpec((tile, BRANCH_W), lambda i: (i, 0))
    return pl.pallas_call(
        functools.partial(_ret_kernel, tile=tile),
        grid=(s // tile,),
        in_specs=[col(8), col(9), col(10), col(11), pl.BlockSpec((tile, LANE), lambda i: (i, 0)),
                  _const_spec((LANE, BRANCH_W)), _const_spec((LANE, BRANCH_W)),
                  _const_spec((BRANCH_W, BRANCH_W)), _const_spec((BRANCH_W, BRANCH_W)),
                  _const_spec((N_HEADS, BRANCH_W)), _const_spec((N_HEADS * c, c)),
                  _const_spec((c, BRANCH_W)), _const_spec((c, BRANCH_W)), _const_spec((1, BRANCH_W)),
                  _const_spec((1, BRANCH_W)), _const_spec((1, BRANCH_W))],
        out_specs=row,
        out_shape=jax.ShapeDtypeStruct((s, BRANCH_W), BF16),
        scratch_shapes=[pltpu.VMEM((BRANCH_W, BRANCH_W), F32)],
        compiler_params=_params("arbitrary"),
        name="retention",
    )(proj, proj, proj, proj, rope, ecos, esin, perm, _head_ones(), _head_masks(), dstack, xi, zeta, gc,
      gn_g.reshape(1, BRANCH_W).astype(F32), gn_b.reshape(1, BRANCH_W).astype(F32))


def _merge_kernel(h_ref, o0_ref, o1_ref, o2_ref, o3_ref, x_ref, wg_ref, wb_ref, wo_ref, g_ref, *rest):
    n_cast = (len(rest) - 2) // 2
    cast_in, (xo_ref, ho_ref), cast_out = rest[:n_cast], rest[n_cast:n_cast + 2], rest[n_cast + 2:]
    for src, dst in zip(cast_in, cast_out):
        dst[...] = src[...].astype(dst.dtype)
    h = h_ref[...]
    merged = jnp.zeros(x_ref.shape, F32)
    for bi, o_ref in enumerate((o0_ref, o1_ref, o2_ref, o3_ref)):
        gate = _sigmoid(_dot(h, wg_ref[:, bi * D_MODEL:(bi + 1) * D_MODEL]))
        merged = merged + gate * _dot(o_ref[...], wb_ref[bi])
    x_new = x_ref[...] + _dot(merged.astype(BF16), wo_ref[...])
    xo_ref[...] = x_new
    ho_ref[...] = _rms(x_new, g_ref[...]).astype(ho_ref.dtype)


def merge(h, branches, x, w_gate, w_branch, w_out, next_gain, tm=512, cast=()):
    s, d = x.shape
    tm = min(tm, s)
    steps = s // tm
    row = lambda w: pl.BlockSpec((tm, w), lambda i: (i, 0))
    flat = [c.reshape(-1, c.shape[-1]) for c in cast]
    slab = lambda c: pl.BlockSpec((c.shape[0] // steps, c.shape[1]), lambda i: (i, 0))
    outs = pl.pallas_call(
        _merge_kernel,
        grid=(steps,),
        in_specs=[row(d), row(BRANCH_W), row(BRANCH_W), row(BRANCH_W), row(BRANCH_W), row(d),
                  _const_spec((d, N_BRANCH * d)), _const_spec((N_BRANCH, BRANCH_W, d)),
                  _const_spec((d, d)), _const_spec((1, d))] + [slab(c) for c in flat],
        out_specs=[row(d), row(d)] + [slab(c) for c in flat],
        out_shape=[jax.ShapeDtypeStruct((s, d), F32), jax.ShapeDtypeStruct((s, d), BF16)]
        + [jax.ShapeDtypeStruct(c.shape, BF16) for c in flat],
        compiler_params=_params("parallel"),
        name="merge",
    )(h, *branches, x, w_gate, w_branch, w_out, next_gain.reshape(1, d).astype(F32), *flat)
    return outs[0], outs[1], [o.reshape(c.shape) for o, c in zip(outs[2:], cast)]


def _ffn_kernel(h_ref, x_ref, wg_ref, wu_ref, wd_ref, g_ref, xo_ref, ho_ref, acc_ref):
    f = pl.program_id(1)

    @pl.when(f == 0)
    def _():
        acc_ref[...] = jnp.zeros_like(acc_ref)

    h = h_ref[...]
    a = _silu(_dot(h, wg_ref[...].astype(BF16))) * _dot(h, wu_ref[...].astype(BF16))
    acc_ref[...] += _dot(a.astype(BF16), wd_ref[...].astype(BF16))

    @pl.when(f == pl.num_programs(1) - 1)
    def _():
        x_new = x_ref[...] + acc_ref[...]
        xo_ref[...] = x_new
        ho_ref[...] = _rms(x_new, g_ref[...]).astype(ho_ref.dtype)


def ffn_dense(h, x, w_gate, w_up, w_down, next_gain, tm=1024, tf=512):
    s, d = x.shape
    tm = min(tm, s)
    dff = w_gate.shape[1]
    row = lambda: pl.BlockSpec((tm, d), lambda i, f: (i, 0))
    return pl.pallas_call(
        _ffn_kernel,
        grid=(s // tm, dff // tf),
        in_specs=[row(), row(),
                  pl.BlockSpec((d, tf), lambda i, f: (0, f)),
                  pl.BlockSpec((d, tf), lambda i, f: (0, f)),
                  pl.BlockSpec((tf, d), lambda i, f: (f, 0)),
                  _const_spec((1, d))],
        out_specs=[row(), row()],
        out_shape=[jax.ShapeDtypeStruct((s, d), F32), jax.ShapeDtypeStruct((s, d), BF16)],
        scratch_shapes=[pltpu.VMEM((tm, d), F32)],
        compiler_params=_params("parallel", "arbitrary"),
        name="ffn_dense",
    )(h, x, w_gate, w_up, w_down, next_gain.reshape(1, d).astype(F32))


MERGE_CAST_TILE = 256
MOE_TOK_TILE = 256
MOE_ROUTE_TILES = 4
MOE_ROW_BLOCK = 512
MOE_GATHER_ROWS = 128
MOE_GATHER_TILES = 4
MOE_VMEM_LIMIT = 60 * 1024 * 1024
MOE_Y_BLOCK = 128
MOE_Y_FETCH = MOE_TOK_TILE // MOE_Y_BLOCK + 1


def _dot_f32(x, w):
    xh, xm, _ = _split3(x)
    wh, wm, _ = _split3(w)
    return _dot(xh, wh) + (_dot(xh, wm) + _dot(xm, wh))


def _route_kernel(x_ref, ng_ref, router_ref, ltri_ref, pos_ref, gate_ref, post_ref, before_ref, total_ref,
                  carry_ref):
    tm = x_ref.shape[0]

    @pl.when(pl.program_id(0) == 0)
    def _():
        carry_ref[...] = jnp.zeros_like(carry_ref)

    hn = _rms(x_ref[...], ng_ref[...])
    logits = _dot_f32(hn, router_ref[...])
    lane = lax.broadcasted_iota(jnp.int32, (tm, LANE), 1)
    logits = jnp.where(lane < N_EXPERTS, logits, -jnp.inf)
    v1 = jnp.max(logits, axis=1, keepdims=True)
    i1 = jnp.min(jnp.where(logits == v1, lane, LANE), axis=1, keepdims=True)
    rest = jnp.where(lane == i1, -jnp.inf, logits)
    v2 = jnp.max(rest, axis=1, keepdims=True)
    i2 = jnp.min(jnp.where(rest == v2, lane, LANE), axis=1, keepdims=True)
    w1 = 1.0 / (1.0 + jnp.exp(v2 - v1))
    gate_ref[...] = jnp.where(lane == i1, w1, 0.0) + jnp.where(lane == i2, 1.0 - w1, 0.0)

    member = jnp.where((lane == i1) | (lane == i2), 1.0, 0.0)
    carry = carry_ref[...]
    rank = _dot(ltri_ref[...], member.astype(BF16)) + carry
    pos = jnp.where(member > 0.0, rank, -1.0)
    pos_ref[...] = pos
    post_ref[...] = pos.T[:SUBLANE, :]
    for blk in range(tm // MOE_TOK_TILE):
        before_ref[blk] = carry
        carry = carry + jnp.sum(member[blk * MOE_TOK_TILE:(blk + 1) * MOE_TOK_TILE], axis=0, keepdims=True)
    carry_ref[...] = carry
    total_ref[...] = carry


def moe_route(x, norm_gain, router):
    s, d = x.shape
    tm = min(MOE_ROUTE_TILES * MOE_TOK_TILE, s)
    nsub = tm // MOE_TOK_TILE
    nt = s // tm
    router_p = jnp.zeros((d, LANE), F32).at[:, :N_EXPERTS].set(router.astype(F32))
    ltri = _const(np.tril(np.ones((tm, tm)), -1), BF16)
    row = pl.BlockSpec((tm, LANE), lambda i: (i, 0))
    return pl.pallas_call(
        _route_kernel,
        grid=(nt,),
        in_specs=[pl.BlockSpec((tm, d), lambda i: (i, 0)), _const_spec((1, d)), _const_spec((d, LANE)),
                  _const_spec((tm, tm))],
        out_specs=[row, row, pl.BlockSpec((SUBLANE, tm), lambda i: (0, i)),
                   pl.BlockSpec((nsub, 1, LANE), lambda i: (i, 0, 0)), pl.BlockSpec((1, LANE), lambda i: (0, 0))],
        out_shape=[jax.ShapeDtypeStruct((s, LANE), F32), jax.ShapeDtypeStruct((s, LANE), F32),
                   jax.ShapeDtypeStruct((SUBLANE, s), F32), jax.ShapeDtypeStruct((nt * nsub, 1, LANE), F32),
                   jax.ShapeDtypeStruct((1, LANE), F32)],
        scratch_shapes=[pltpu.VMEM((1, LANE), F32)],
        compiler_params=_params("arbitrary"),
        name="moe_route",
    )(x, norm_gain.reshape(1, d).astype(F32), router_p, ltri)


def _moe_ffn_kernel(be_ref, r0_ref, tlo_ref, thi_ref, nv_ref, h_ref, post_ref, wg_ref, wu_ref, wd_ref,
                    y_ref, x_sc, acc_sc, *, tt, nsub):
    b = pl.program_id(0)
    f = pl.program_id(1)
    nb = pl.num_programs(0)
    valid = b < nv_ref[0]
    tmb, d = acc_sc.shape
    gr = tmb // nsub
    last_tile = h_ref.shape[0] // tt - 1

    def picked(e, want, t):
        t0 = pl.multiple_of(t * tt, tt)
        p = post_ref[pl.ds(e, 1), pl.ds(t0, tt)]
        sel = jnp.where(p == want, 1.0, 0.0).astype(BF16)
        return _dot(sel, h_ref[pl.ds(t0, tt), :])

    def gather_head(blk, sb):
        e = be_ref[blk]
        lo = tlo_ref[blk * nsub + sb]
        hi = thi_ref[blk * nsub + sb]
        want = (lax.broadcasted_iota(jnp.int32, (gr, 1), 0) + (r0_ref[blk] + sb * gr)).astype(F32)
        rows = picked(e, want, lo)
        for k in range(1, MOE_GATHER_TILES):
            rows = rows + picked(e, jnp.where(lo + k <= hi, want, -2.0), jnp.minimum(lo + k, last_tile))
        return rows.astype(BF16)

    def gather_tail(slot, blk, sb):
        e = be_ref[blk]
        want = (lax.broadcasted_iota(jnp.int32, (gr, 1), 0) + (r0_ref[blk] + sb * gr)).astype(F32)
        rows = pl.ds(pl.multiple_of(sb * gr, gr), gr)

        def more(t, carry):
            x_sc[slot, rows, :] = (x_sc[slot, rows, :].astype(F32) + picked(e, want, t)).astype(BF16)
            return carry

        lax.fori_loop(tlo_ref[blk * nsub + sb] + MOE_GATHER_TILES, thi_ref[blk * nsub + sb] + 1, more, 0)

    @pl.when((b == 0) & (f == 0))
    def _():
        for sb in range(nsub):
            x_sc[0, pl.ds(sb * gr, gr), :] = gather_head(0, sb)
            gather_tail(0, 0, sb)

    nxt = jnp.minimum(b + 1, nb - 1)
    nslot = (b + 1) % 2

    @pl.when(valid)
    def _():
        x_sc[nslot, pl.ds(pl.multiple_of(f * gr, gr), gr), :] = gather_head(nxt, f)
        xb = x_sc[b % 2]
        a = _silu(_dot(xb, wg_ref[0])) * _dot(xb, wu_ref[0])
        acc_sc[...] = jnp.where(f == 0, 0.0, acc_sc[...]) + _dot(a.astype(BF16), wd_ref[0].astype(BF16))

    @pl.when(valid & (thi_ref[nxt * nsub + f] - tlo_ref[nxt * nsub + f] >= MOE_GATHER_TILES))
    def _():
        gather_tail(nslot, nxt, f)

    @pl.when(f == nsub - 1)
    def _():
        y_ref[...] = jnp.where(valid, acc_sc[...], 0.0).astype(y_ref.dtype)


def moe_ffn(h, post, sched, w_gate, w_up, w_down):
    s, d = h.shape
    ne, _, dff = w_gate.shape
    tmb = MOE_ROW_BLOCK
    tt = min(MOE_TOK_TILE, s)
    nb = sched[0].shape[0]
    nf = tmb // MOE_GATHER_ROWS
    tf = dff // nf

    def fidx(b, f, nv):
        return jnp.where(b < nv[0], f, nf - 1)

    grid_spec = pltpu.PrefetchScalarGridSpec(
        num_scalar_prefetch=5,
        grid=(nb, nf),
        in_specs=[pl.BlockSpec((s, d), lambda b, f, *_: (0, 0), pipeline_mode=pl.Buffered(1)),
                  pl.BlockSpec((SUBLANE, s), lambda b, f, *_: (0, 0), pipeline_mode=pl.Buffered(1)),
                  pl.BlockSpec((1, d, tf), lambda b, f, be, r0, tlo, thi, nv: (be[b], 0, fidx(b, f, nv))),
                  pl.BlockSpec((1, d, tf), lambda b, f, be, r0, tlo, thi, nv: (be[b], 0, fidx(b, f, nv))),
                  pl.BlockSpec((1, tf, d), lambda b, f, be, r0, tlo, thi, nv: (be[b], fidx(b, f, nv), 0))],
        out_specs=pl.BlockSpec((tmb, d), lambda b, f, *_: (b, 0)),
        scratch_shapes=[pltpu.VMEM((2, tmb, d), BF16), pltpu.VMEM((tmb, d), F32)],
    )
    return pl.pallas_call(
        functools.partial(_moe_ffn_kernel, tt=tt, nsub=nf),
        grid_spec=grid_spec,
        out_shape=jax.ShapeDtypeStruct((nb * tmb, d), BF16),
        compiler_params=pltpu.CompilerParams(dimension_semantics=("arbitrary", "arbitrary"),
                                             vmem_limit_bytes=MOE_VMEM_LIMIT),
        name="moe_ffn",
    )(*sched, h, post, w_gate, w_up, w_down)


def _moe_combine_kernel(kb_ref, off_ref, lim_ref, x_ref, pos_ref, gate_ref, fg_ref, *rest):
    y_refs, o_ref, acc_sc = rest[:-2], rest[-2], rest[-1]
    t = pl.program_id(0)
    yb = y_refs[0].shape[0]
    pos = pos_ref[...]
    gate = gate_ref[...]
    col = lax.broadcasted_iota(jnp.int32, (1, yb), 1).astype(F32)

    def routed(e):
        pe = pos[:, e:e + 1]
        ge = gate[:, e:e + 1]
        r = jnp.where(pe >= 0.0, pe + off_ref[t * N_EXPERTS + e].astype(F32), -1.0)
        return r, ge

    def picked(e, k, r):
        sel = jnp.where(r == col + float(k * yb), 1.0, 0.0).astype(BF16)
        return _dot(sel, y_refs[MOE_Y_FETCH * e + k][...])

    col2 = lax.broadcasted_iota(jnp.int32, (1, 2 * yb), 1).astype(F32)
    acc = x_ref[...]
    for e in range(N_EXPERTS):
        r, ge = routed(e)
        pair = jnp.concatenate([y_refs[MOE_Y_FETCH * e][...], y_refs[MOE_Y_FETCH * e + 1][...]], axis=0)
        acc = acc + ge * _dot(jnp.where(r == col2, 1.0, 0.0).astype(BF16), pair)
    acc_sc[...] = acc
    for e in range(N_EXPERTS):
        for k in range(2, MOE_Y_FETCH):
            @pl.when(lim_ref[t * N_EXPERTS + e] > k * yb)
            def _(k=k, e=e):
                r, ge = routed(e)
                acc_sc[...] += ge * picked(e, k, r)
    o_ref[...] = _rms(acc_sc[...], fg_ref[...]).astype(o_ref.dtype)


def moe_combine(x, pos, gate, y, kb, off, lim, final_gain):
    s, d = x.shape
    tm = min(MOE_TOK_TILE, s)
    yb = MOE_Y_BLOCK
    last = y.shape[0] // yb - 1

    def yspec(e, k):
        def index(t, kb_r, off_r, lim_r):
            blk = jnp.minimum(kb_r[t * N_EXPERTS + e] + k, last)
            return (blk if k < 1 else jnp.where(lim_r[t * N_EXPERTS + e] > k * yb, blk, 0), 0)
        return pl.BlockSpec((yb, d), index)

    grid_spec = pltpu.PrefetchScalarGridSpec(
        num_scalar_prefetch=3,
        grid=(s // tm,),
        in_specs=[pl.BlockSpec((tm, d), lambda t, *_: (t, 0)),
                  pl.BlockSpec((tm, LANE), lambda t, *_: (t, 0)),
                  pl.BlockSpec((tm, LANE), lambda t, *_: (t, 0)),
                  pl.BlockSpec((1, d), lambda t, *_: (0, 0))]
        + [yspec(e, k) for e in range(N_EXPERTS) for k in range(MOE_Y_FETCH)],
        out_specs=pl.BlockSpec((tm, d), lambda t, *_: (t, 0)),
        scratch_shapes=[pltpu.VMEM((tm, d), F32)],
    )
    return pl.pallas_call(
        _moe_combine_kernel,
        grid_spec=grid_spec,
        out_shape=jax.ShapeDtypeStruct((s, d), F32),
        compiler_params=_params("arbitrary"),
        name="moe_combine",
    )(kb, off, lim, x, pos, gate, final_gain.reshape(1, d).astype(F32), *([y] * (MOE_Y_FETCH * N_EXPERTS)))


def _moe_schedule(before, total, s):
    tmb, yb, gr = MOE_ROW_BLOCK, MOE_Y_BLOCK, MOE_GATHER_ROWS
    nb = 2 * s // tmb + N_EXPERTS
    counts = total[0, :N_EXPERTS].astype(jnp.int32)
    nblk = (counts + tmb - 1) // tmb
    end = jnp.cumsum(nblk)
    first = end - nblk
    nvalid = end[-1]
    b = jnp.minimum(jnp.arange(nb, dtype=jnp.int32), nvalid - 1)
    blk_e = jnp.sum(b[:, None] >= end[None, :], axis=1).astype(jnp.int32)
    r0 = (b - first[blk_e]) * tmb
    cb = before[:, 0, :N_EXPERTS].astype(jnp.int32)
    r0s = (r0[:, None] + gr * jnp.arange(tmb // gr, dtype=jnp.int32)[None, :]).reshape(-1)
    cbe = cb[:, jnp.repeat(blk_e, tmb // gr)]
    tlo = (jnp.sum(cbe <= r0s[None, :], axis=0) - 1).astype(jnp.int32)
    thi = (jnp.sum(cbe < (r0s + gr)[None, :], axis=0) - 1).astype(jnp.int32)
    sched = (blk_e, r0.astype(jnp.int32), tlo, thi, nvalid.reshape(1).astype(jnp.int32))
    row_start = first[None, :] * tmb + cb
    kb = row_start // yb
    off = first[None, :] * tmb - kb * yb
    n_te = jnp.concatenate([cb[1:], counts[None, :]], axis=0) - cb
    lim = row_start - kb * yb + n_te
    flat = lambda a: a.reshape(-1).astype(jnp.int32)
    return sched, flat(kb), flat(off), flat(lim)


def moe_sparse(h, x, norm_gain, router, w_gate, w_up, w_down, final_gain):
    s, _ = x.shape
    pos, gate, post, before, total = moe_route(x, norm_gain, router)
    sched, kb, off, lim = _moe_schedule(before, total, s)
    y = moe_ffn(h, post, sched, w_gate, w_up, w_down)
    return moe_combine(x, pos, gate, y, kb, off, lim, final_gain)


FOX_F0 = 7 * BRANCH_W


def _split_w_in_kernel(wt_ref, mix_ref, f_ref, gate_ref):
    rest0 = FOX_F0 + N_HEADS
    gate0 = N_MIX_COLS + N_HEADS
    mix_ref[:, :FOX_F0] = wt_ref[pl.ds(0, FOX_F0), :].T.astype(BF16)
    mix_ref[:, FOX_F0:] = wt_ref[pl.ds(rest0, N_MIX_COLS - FOX_F0), :].T.astype(BF16)
    gate_ref[...] = wt_ref[pl.ds(gate0, N_BRANCH * D_MODEL), :].T.astype(BF16)
    f_rows = jnp.concatenate([wt_ref[pl.ds(FOX_F0, N_HEADS), :], jnp.zeros((LANE - N_HEADS, LANE), F32)], axis=0)
    f_ref[...] = f_rows.T.astype(BF16)


def _mixer_weights(w_in, layer, tr=LANE):
    _, d, cols = w_in.shape
    w_t = jnp.swapaxes(w_in, 1, 2)
    row = lambda w: pl.BlockSpec((tr, w), lambda i: (i, 0))
    return pl.pallas_call(
        _split_w_in_kernel,
        grid=(d // tr,),
        in_specs=[pl.BlockSpec((None, cols, tr), lambda i: (layer, 0, i))],
        out_specs=[row(N_MIX_COLS), row(LANE), row(N_BRANCH * D_MODEL)],
        out_shape=[jax.ShapeDtypeStruct((d, N_MIX_COLS), BF16), jax.ShapeDtypeStruct((d, LANE), BF16),
                   jax.ShapeDtypeStruct((d, N_BRANCH * D_MODEL), BF16)],
        compiler_params=_params("parallel"),
        name="split_w_in",
    )(w_t)


def kernel(x, w_in, w_branch, w_out, norm_mix_g, hgrn_lb_logits, hgrn_norm_g, fox_f_bias, pool_w, pool_scale,
           ret_gn_g, ret_gn_b, norm_ffn_g, ffn_w_gate, ffn_w_up, ffn_w_down, moe_router, moe_w_gate, moe_w_up,
           moe_w_down, final_norm_g):
    b, s, d = x.shape
    assert b == 1 and d == D_MODEL
    depth = w_in.shape[0]
    assert depth == 2, "layer 0 uses the dense FFN, layer 1 the experts and the final norm"
    xs = x.reshape(s, d)
    h = None
    out = None
    for layer in range(depth):
        w_mix, w_f, w_gate = _mixer_weights(w_in, layer)
        if layer == 0:
            proj, h = norm_matmul(xs, norm_mix_g[0], w_mix)
        else:
            proj = matmul(h, w_mix)
        ct, c_rows, fox_stats = fox_gate(h, proj, w_f, fox_f_bias[layer])
        branches = (
            hgrn2(proj, hgrn_lb_logits, hgrn_norm_g[layer], layer),
            fox_attention(proj, ct, c_rows, fox_stats),
            pool_mixer(proj, pool_w[layer], pool_scale[layer]),
            retention(proj, ret_gn_g[layer], ret_gn_b[layer]),
        )
        li = layer // 2
        if layer % 2 == 0:
            xs, h2, _ = merge(h, branches, xs, w_gate, w_branch[layer].astype(BF16), w_out[layer].astype(BF16),
                              norm_ffn_g[layer])
            xs, h = ffn_dense(h2, xs, ffn_w_gate[li], ffn_w_up[li], ffn_w_down[li], norm_mix_g[layer + 1])
        else:
            xs, h2, (moe_wg, moe_wu) = merge(h, branches, xs, w_gate, w_branch[layer].astype(BF16),
                                             w_out[layer].astype(BF16), norm_ffn_g[layer], tm=MERGE_CAST_TILE,
                                             cast=(moe_w_gate[li], moe_w_up[li]))
            out = moe_sparse(h2, xs, norm_ffn_g[layer], moe_router[li], moe_wg, moe_wu, moe_w_down[li],
                             final_norm_g)
    return out.reshape(b, s, d)
```

```python
import functools
import math

import jax
import jax.numpy as jnp
import numpy as np
from jax import lax
from jax.experimental import pallas as pl
from jax.experimental.pallas import tpu as pltpu

D_MODEL = 1024
N_BRANCH = 4
BRANCH_W = D_MODEL // N_BRANCH
HEAD_DIM = 64
N_HEADS = BRANCH_W // HEAD_DIM
POOL_WINDOWS = (2, 4, 8, 16)
POOL_GROUP = BRANCH_W // len(POOL_WINDOWS)
POOL_HALO = 16
RET_DECAY_BASE = 5.0
ROPE_BASE = 10000.0
D_FF = 7 * D_MODEL // 2
N_EXPERTS = 8
RMS_EPS = 1e-6
LN_EPS = 1e-5
N_MIX_COLS = 12 * BRANCH_W

LANE = 128
SUBLANE = 8
VMEM_LIMIT = 56 * 1024 * 1024

HG_CHUNK = 64
HG_SUB = 16
HG_GROUP = 1
HG_FAST_MIN_LOGDECAY = -60.0
RET_CHUNK = 256

F32 = jnp.float32
BF16 = jnp.bfloat16
NT_DIMS = (((1,), (1,)), ((), ()))


def _params(*sem):
    return pltpu.CompilerParams(dimension_semantics=sem, vmem_limit_bytes=VMEM_LIMIT)


def _const_spec(shape):
    nd = len(shape)
    return pl.BlockSpec(shape, lambda *_: (0,) * nd, pipeline_mode=pl.Buffered(1))


def _split3(x):
    hi = x.astype(BF16)
    r1 = x - hi.astype(F32)
    mid = r1.astype(BF16)
    lo = (r1 - mid.astype(F32)).astype(BF16)
    return hi, mid, lo


def _dot(a, b):
    return jnp.dot(a, b, preferred_element_type=F32)


def _dot_nt(a, b):
    return lax.dot_general(a, b, NT_DIMS, preferred_element_type=F32)


def _dot_exact_rhs(x, m_bf16):
    hi, mid, lo = _split3(x)
    return _dot(hi, m_bf16) + _dot(mid, m_bf16) + _dot(lo, m_bf16)


def _dot_exact_lhs(m_bf16, x):
    hi, mid, lo = _split3(x)
    return _dot(m_bf16, hi) + _dot(m_bf16, mid) + _dot(m_bf16, lo)


def _sigmoid(x):
    return 1.0 / (1.0 + jnp.exp(-x))


def _silu(x):
    return x * _sigmoid(x)


def _rms(x, gain):
    return x * lax.rsqrt(jnp.mean(x * x, axis=-1, keepdims=True) + RMS_EPS) * gain


def _const(a, dtype=F32):
    return jnp.asarray(np.asarray(a, np.float32), dtype)


def _head_of(n):
    return np.arange(n) // HEAD_DIM


def _head_ones():
    h = _head_of(BRANCH_W)
    return _const(h[:, None] == h[None, :], BF16)


def _head_masks():
    return _const(_head_of(BRANCH_W)[None, :] == np.arange(N_HEADS)[:, None])


def _norm_matmul_kernel(x_ref, g_ref, b_ref, o_ref, h_ref):
    @pl.when(pl.program_id(1) == 0)
    def _():
        h_ref[...] = _rms(x_ref[...], g_ref[...]).astype(h_ref.dtype)

    o_ref[...] = _dot(h_ref[...], b_ref[...]).astype(o_ref.dtype)


def norm_matmul(x, gain, b, tm=2048, tn=1024):
    m, k = x.shape
    _, n = b.shape
    tm = min(tm, m)
    return pl.pallas_call(
        _norm_matmul_kernel,
        grid=(m // tm, n // tn),
        in_specs=[pl.BlockSpec((tm, k), lambda i, j: (i, 0)), _const_spec((1, k)),
                  pl.BlockSpec((k, tn), lambda i, j: (0, j))],
        out_specs=[pl.BlockSpec((tm, tn), lambda i, j: (i, j)), pl.BlockSpec((tm, k), lambda i, j: (i, 0))],
        out_shape=[jax.ShapeDtypeStruct((m, n), BF16), jax.ShapeDtypeStruct((m, k), BF16)],
        compiler_params=_params("arbitrary", "arbitrary"),
        name="in_proj_norm",
    )(x, gain.reshape(1, k).astype(F32), b)


def _matmul_kernel(a_ref, b_ref, o_ref):
    o_ref[...] = _dot(a_ref[...], b_ref[...]).astype(o_ref.dtype)


def matmul(a, b, out_dtype=BF16, tm=2048, tn=1024):
    m, k = a.shape
    _, n = b.shape
    tm = min(tm, m)
    return pl.pallas_call(
        _matmul_kernel,
        grid=(n // tn, m // tm),
        in_specs=[pl.BlockSpec((tm, k), lambda j, i: (i, 0)),
                  pl.BlockSpec((k, tn), lambda j, i: (0, j))],
        out_specs=pl.BlockSpec((tm, tn), lambda j, i: (i, j)),
        out_shape=jax.ShapeDtypeStruct((m, n), out_dtype),
        compiler_params=_params("parallel", "parallel"),
        name="in_proj",
    )(a, b)


def _hgrn_kernel(q_ref, f_ref, i_ref, g_ref, lbl_ref, ng_ref, ones_ref, hm_ref, tril_ref, halfsum_ref, fmask_ref,
                 o_ref, st_ref, bpad, kpad, vpad, astack, lf_sc, kk_sc, o_sc, *, layer, tile):
    c, sub = HG_CHUNK, HG_SUB
    nsub = c // sub
    half = c // 2

    @pl.when(pl.program_id(0) == 0)
    def _():
        st_ref[...] = jnp.zeros_like(st_ref)
        bpad[...] = jnp.zeros_like(bpad)
        kpad[...] = jnp.zeros_like(kpad)
        vpad[...] = jnp.zeros_like(vpad)

    lbl = lbl_ref[...]
    e = jnp.exp(lbl - jnp.max(lbl, axis=0, keepdims=True))
    p = e / jnp.sum(e, axis=0, keepdims=True)
    lb = jnp.zeros((1, BRANCH_W), F32)
    for l in range(1, layer + 1):
        lb = lb + p[l:l + 1, :]

    ones_bd = ones_ref[...]
    hm = hm_ref[...]
    tril = tril_ref[...]
    row = lax.broadcasted_iota(jnp.int32, (c, 1), 0)
    row_in_sub = row % sub
    bd_mask = ones_bd.astype(F32)

    sig = _sigmoid(f_ref[...].astype(F32))
    logf_all = jnp.log(lb + (1.0 - lb) * sig)
    lf_sc[...] = _dot_exact_lhs(tril, logf_all)
    kk_sc[...] = (1.0 - lb) * (1.0 - sig)
    min_decay = jnp.min(_dot(halfsum_ref[...], logf_all.astype(BF16)))

    def load(ci):
        r0 = pl.multiple_of(ci * c, c)
        q = q_ref[pl.ds(r0, c), :].astype(F32)
        v = i_ref[pl.ds(r0, c), :].astype(F32)
        kk = kk_sc[pl.ds(r0, c), :]
        b = lf_sc[pl.ds(r0, c), :]
        return r0, q, v, kk, b

    def finish(r0, q_decayed, v, kk, b, intra):
        st = st_ref[...]
        inter = _dot_nt(q_decayed.astype(BF16), st.astype(BF16))
        b_last = b[c - 1:c, :]
        ks_end = (kk * jnp.exp(b_last - b)).astype(BF16)
        upd = _dot(v.T.astype(BF16), ks_end)
        st_ref[...] = st * jnp.exp(b_last) + upd * bd_mask
        o_sc[pl.ds(r0, c), :] = intra + inter

    def fast_group(gi, carry):
        second = row >= half
        chunks, qxs, kaugs, vaugs = [], [], [], []
        for j in range(HG_GROUP):
            r0, q, v, kk, b = load(gi * HG_GROUP + j)
            m_row = b[half - 1:half, :]
            mref = jnp.where(second, m_row, 0.0)
            qp = q * jnp.exp(b - mref)
            kp = kk * jnp.exp(mref - b)
            e_m = jnp.exp(m_row)
            kaugs.append(jnp.concatenate([kp, kp[:half, :] * e_m], axis=0))
            vaugs.append(jnp.concatenate([v, v[:half, :]], axis=0))
            qxs.extend(qp * hm[h:h + 1, :] for h in range(N_HEADS))
            chunks.append((r0, jnp.where(second, qp * e_m, qp), v, kk, b))
        qx = jnp.concatenate(qxs, axis=0).astype(BF16)
        kaug = jnp.concatenate(kaugs, axis=0).astype(BF16)
        vaug = jnp.concatenate(vaugs, axis=0).astype(BF16)
        sc = jnp.where(fmask_ref[...] > 0.0, _dot_nt(qx, kaug), 0.0)
        r = _dot(sc.astype(BF16), vaug)
        for j, (r0, q_decayed, v, kk, b) in enumerate(chunks):
            intra = jnp.zeros((c, BRANCH_W), F32)
            for h in range(N_HEADS):
                lo = (j * N_HEADS + h) * c
                intra = intra + r[lo:lo + c, :] * hm[h:h + 1, :]
            finish(r0, q_decayed, v, kk, b, intra)
        return carry

    def exact_chunk(ci, carry):
        r0, q, v, kk, b = load(ci)

        bpad[pl.ds(sub, c), :] = b
        kpad[pl.ds(sub, c), :] = kk
        vpad[pl.ds(sub, c), :] = v

        for d in range(sub):
            b_d = bpad[pl.ds(sub - d, c), :]
            k_d = kpad[pl.ds(sub - d, c), :]
            a = jnp.where(row_in_sub >= d, q * k_d * jnp.exp(b - b_d), 0.0)
            astack[pl.ds(d * c, c), :] = a.astype(BF16)
        pall = _dot(astack[...], ones_bd)
        intra = jnp.zeros((c, BRANCH_W), F32)
        for d in range(sub):
            intra = intra + pall[d * c:(d + 1) * c, :] * vpad[pl.ds(sub - d, c), :]

        pieces = [jnp.zeros((sub, BRANCH_W), F32)]
        for si in range(1, nsub):
            lo = si * sub
            m_i = b[lo - 1:lo, :]
            qs = q[lo:lo + sub, :] * jnp.exp(b[lo:lo + sub, :] - m_i)
            ks = (kk[:lo, :] * jnp.exp(m_i - b[:lo, :])).astype(BF16)
            qx = jnp.concatenate([qs * hm[h:h + 1, :] for h in range(N_HEADS)], axis=0).astype(BF16)
            sc = _dot_nt(qx, ks)
            r = _dot(sc.astype(BF16), v[:lo, :].astype(BF16))
            acc = jnp.zeros((sub, BRANCH_W), F32)
            for h in range(N_HEADS):
                acc = acc + r[h * sub:(h + 1) * sub, :] * hm[h:h + 1, :]
            pieces.append(acc)
        intra = intra + jnp.concatenate(pieces, axis=0)
        finish(r0, q * jnp.exp(b), v, kk, b, intra)
        return carry

    lax.cond(min_decay >= HG_FAST_MIN_LOGDECAY,
             lambda: lax.fori_loop(0, tile // (c * HG_GROUP), fast_group, 0, unroll=2),
             lambda: lax.fori_loop(0, tile // c, exact_chunk, 0))

    o = o_sc[...]
    ms = _dot_exact_rhs(o * o, ones_bd) * (1.0 / HEAD_DIM)
    y = o * lax.rsqrt(ms + RMS_EPS) * ng_ref[...] * _silu(g_ref[...].astype(F32))
    o_ref[...] = y.astype(o_ref.dtype)


def hgrn2(proj, lb_logits, norm_g, layer, tile=1024):
    s = proj.shape[0]
    depth = lb_logits.shape[0]
    c, sub = HG_CHUNK, HG_SUB
    half = c // 2
    tile = min(tile, s)
    col = lambda j: pl.BlockSpec((tile, BRANCH_W), lambda i, j=j: (i, j))
    pos = np.arange(tile)
    tril = _const((pos[:, None] // c == pos[None, :] // c) & (pos[None, :] <= pos[:, None]), BF16)
    nhalf = tile // half
    halfsum = _const(np.arange(tile)[None, :] // half == np.arange(nhalf)[:, None], BF16)
    t = np.arange(c)[:, None]
    col_s = np.arange(c + half)[None, :]
    same_half = (col_s < c) & (col_s // half == t // half) & (col_s <= t)
    cross = (col_s >= c) & (t >= half)
    fmask1 = np.tile((same_half | cross).astype(np.float32), (N_HEADS, 1))
    fmask = _const(np.kron(np.eye(HG_GROUP, dtype=np.float32), fmask1))
    return pl.pallas_call(
        functools.partial(_hgrn_kernel, layer=layer, tile=tile),
        grid=(s // tile,),
        in_specs=[col(0), col(1), col(2), col(3),
                  _const_spec((depth, BRANCH_W)), _const_spec((1, BRANCH_W)),
                  _const_spec((BRANCH_W, BRANCH_W)), _const_spec((N_HEADS, BRANCH_W)),
                  _const_spec((tile, tile)), _const_spec((nhalf, tile)),
                  _const_spec((HG_GROUP * N_HEADS * c, HG_GROUP * (c + half)))],
        out_specs=pl.BlockSpec((tile, BRANCH_W), lambda i: (i, 0)),
        out_shape=jax.ShapeDtypeStruct((s, BRANCH_W), BF16),
        scratch_shapes=[pltpu.VMEM((BRANCH_W, BRANCH_W), F32),
                        pltpu.VMEM((c + sub, BRANCH_W), F32),
                        pltpu.VMEM((c + sub, BRANCH_W), F32),
                        pltpu.VMEM((c + sub, BRANCH_W), F32),
                        pltpu.VMEM((sub * c, BRANCH_W), BF16),
                        pltpu.VMEM((tile, BRANCH_W), F32),
                        pltpu.VMEM((tile, BRANCH_W), F32),
                        pltpu.VMEM((tile, BRANCH_W), F32)],
        compiler_params=_params("arbitrary"),
        name="hgrn2",
    )(proj, proj, proj, proj, lb_logits.astype(F32), norm_g.reshape(1, BRANCH_W).astype(F32),
      _head_ones(), _head_masks(), tril, halfsum, fmask)


FOX_TILE = 256
FOX_GATE_BLOCKS = 4
FOX_NSTAT = 16
FOX_FIXED_MAX = 30.0
FOX_SKIP_LOG = 40.0


def _fox_gate_kernel(h_ref, q_ref, k_ref, wf_ref, bias_ref, tril_ref, ones_ref,
                     ct_ref, c_ref, stat_ref, carry_ref, kmax_ref):
    @pl.when(pl.program_id(0) == 0)
    def _():
        carry_ref[...] = jnp.zeros_like(carry_ref)
        kmax_ref[...] = jnp.zeros_like(kmax_ref)

    logit = _dot(h_ref[...], wf_ref[...]) + bias_ref[...]
    logf = jnp.minimum(logit, 0.0) - jnp.log(1.0 + jnp.exp(-jnp.abs(logit)))
    cum = _dot_exact_lhs(tril_ref[...], logf) + carry_ref[...]
    carry_ref[...] = cum[-1:, :]
    ct_ref[...] = cum.T[:SUBLANE, :]
    c_ref[...] = cum

    ones_bd = ones_ref[...]
    q = q_ref[...].astype(F32)
    k = k_ref[...].astype(F32)
    scale = HEAD_DIM ** -0.5
    head_lane = lax.broadcasted_iota(jnp.int32, (1, BRANCH_W), 1) // HEAD_DIM
    c_heads = jnp.zeros(q.shape, F32)
    for h in range(N_HEADS):
        c_heads = jnp.where(head_lane == h, cum[:, h:h + 1], c_heads)
    slack = 1.0 + 2.0 ** -6
    qn = jnp.sqrt(_dot((q * q).astype(BF16), ones_bd)) * (scale * slack)
    kn = jnp.sqrt(_dot((k * k).astype(BF16), ones_bd)) * slack
    diag = _dot((q * k).astype(BF16), ones_bd) * scale - (2.0 ** -6) * qn * kn
    e_row = c_heads - diag
    kmax = kmax_ref[...]
    for blk in range(q.shape[0] // FOX_TILE):
        rs = slice(blk * FOX_TILE, (blk + 1) * FOX_TILE)
        kmax = jnp.maximum(kmax, jnp.max(kn[rs], axis=0, keepdims=True))
        rows = [jnp.max(qn[rs], axis=0, keepdims=True),
                jnp.max(e_row[rs], axis=0, keepdims=True),
                kmax,
                c_heads[(blk + 1) * FOX_TILE - 1:(blk + 1) * FOX_TILE, :]]
        stat_ref[blk] = jnp.concatenate(rows + [jnp.zeros((SUBLANE - len(rows), BRANCH_W), F32)], axis=0)
    kmax_ref[...] = kmax


def fox_gate(h, proj, w_f, f_bias):
    s, d = h.shape
    tile = min(FOX_GATE_BLOCKS * FOX_TILE, s)
    nblk = tile // FOX_TILE
    bias = jnp.zeros((1, LANE), F32).at[0, :N_HEADS].set(f_bias.astype(F32))
    tril = _const(np.tril(np.ones((tile, tile))), BF16)
    ct, c_rows, stats = pl.pallas_call(
        _fox_gate_kernel,
        grid=(s // tile,),
        in_specs=[pl.BlockSpec((tile, d), lambda i: (i, 0)),
                  pl.BlockSpec((tile, BRANCH_W), lambda i: (i, 4)),
                  pl.BlockSpec((tile, BRANCH_W), lambda i: (i, 5)),
                  _const_spec((d, LANE)), _const_spec((1, LANE)), _const_spec((tile, tile)),
                  _const_spec((BRANCH_W, BRANCH_W))],
        out_specs=[pl.BlockSpec((SUBLANE, tile), lambda i: (0, i)),
                   pl.BlockSpec((tile, LANE), lambda i: (i, 0)),
                   pl.BlockSpec((nblk, SUBLANE, BRANCH_W), lambda i: (i, 0, 0))],
        out_shape=[jax.ShapeDtypeStruct((SUBLANE, s), F32),
                   jax.ShapeDtypeStruct((s, LANE), F32),
                   jax.ShapeDtypeStruct((s // FOX_TILE, SUBLANE, BRANCH_W), F32)],
        scratch_shapes=[pltpu.VMEM((1, LANE), F32), pltpu.VMEM((1, BRANCH_W), F32)],
        compiler_params=_params("arbitrary"),
        name="fox_gate",
    )(h, proj, proj, w_f, bias, tril, _head_ones())
    return ct, c_rows, stats[:, :4, ::HEAD_DIM].reshape(-1)


def _fox_kernel(stat_ref, q_ref, k_ref, v_ref, ct_ref, c_ref, hm_ref, o_ref, m_sc, l_sc, acc_sc, *, tq):
    i = pl.program_id(0)
    q0 = pl.multiple_of(i * tq, tq)
    hm = hm_ref[...]
    q = q_ref[...].astype(F32) * (HEAD_DIM ** -0.5)
    qh = [(q * hm[h:h + 1, :]).astype(BF16) for h in range(N_HEADS)]
    c_q0 = ct_ref[:, pl.ds(q0, tq)][:, 0:1]

    first = i
    for h in range(N_HEADS):
        qmax = stat_ref[i * FOX_NSTAT + h]
        emax = stat_ref[i * FOX_NSTAT + N_HEADS + h]

        def needed(j, h=h, qmax=qmax, emax=emax):
            jc = jnp.maximum(j, 0)
            bound = (qmax * stat_ref[jc * FOX_NSTAT + 2 * N_HEADS + h] + emax
                     - stat_ref[jc * FOX_NSTAT + 3 * N_HEADS + h])
            return (j >= 0) & (bound >= -FOX_SKIP_LOG)

        last_dropped = lax.while_loop(needed, lambda j: j - 1, i - 1)
        first = jnp.minimum(first, last_dropped + 1)

    l_sc[...] = jnp.zeros_like(l_sc)
    acc_sc[...] = jnp.zeros_like(acc_sc)

    def causal(sc):
        r = lax.broadcasted_iota(jnp.int32, (tq, tq), 0)
        cidx = lax.broadcasted_iota(jnp.int32, (tq, tq), 1)
        return jnp.where(cidx <= r, sc, -jnp.inf)

    def online_block(s0, diagonal):
        kb = k_ref[pl.ds(s0, tq), :]
        vb = v_ref[pl.ds(s0, tq), :]
        bias = c_q0 - ct_ref[:, pl.ds(s0, tq)]
        for h in range(N_HEADS):
            sc = _dot_nt(qh[h], kb) + bias[h:h + 1, :]
            if diagonal:
                sc = causal(sc)
            m_prev = m_sc[h]
            m_new = jnp.maximum(m_prev, jnp.max(sc, axis=1, keepdims=True))
            alpha = jnp.exp(m_prev - m_new)
            p = jnp.exp(sc - jnp.tile(m_new, (1, tq // LANE)))
            l_sc[h] = alpha * l_sc[h] + jnp.sum(p, axis=1, keepdims=True)
            acc_sc[h] = acc_sc[h] * jnp.tile(alpha, (1, BRANCH_W // LANE)) + _dot(p.astype(BF16), vb)
            m_sc[h] = m_new

    tops = [stat_ref[i * FOX_NSTAT + h] * stat_ref[i * FOX_NSTAT + 2 * N_HEADS + h] for h in range(N_HEADS)]
    c_tile = c_ref[...]
    shift = [c_tile[:, h:h + 1] - c_q0[h:h + 1, :] - tops[h] for h in range(N_HEADS)]

    def fixed_block(s0, diagonal):
        kb = k_ref[pl.ds(s0, tq), :]
        vb = v_ref[pl.ds(s0, tq), :]
        bias = c_q0 - ct_ref[:, pl.ds(s0, tq)]
        for h in range(N_HEADS):
            sc = _dot_nt(qh[h], kb) + bias[h:h + 1, :] + shift[h]
            if diagonal:
                sc = causal(sc)
            p = jnp.exp(sc)
            l_sc[h] += p[:, :LANE] + p[:, LANE:]
            acc_sc[h] += _dot(p.astype(BF16), vb)

    def run(block, row_sum):
        lax.fori_loop(first, i, lambda j, carry: (block(pl.multiple_of(j * tq, tq), False), carry)[1], 0)
        block(q0, True)
        out = jnp.zeros((tq, BRANCH_W), F32)
        for h in range(N_HEADS):
            out = out + acc_sc[h] * hm[h:h + 1, :] / row_sum(l_sc[h])
        o_ref[...] = out.astype(o_ref.dtype)

    def run_online():
        m_sc[...] = jnp.full_like(m_sc, -jnp.inf)
        run(online_block, lambda l: jnp.tile(l, (1, BRANCH_W // LANE)))

    def run_fixed():
        run(fixed_block, lambda l: jnp.sum(l, axis=1, keepdims=True))

    lax.cond(functools.reduce(jnp.maximum, tops) <= FOX_FIXED_MAX, run_fixed, run_online)


def fox_attention(proj, ct, c_rows, stats):
    s = proj.shape[0]
    tq = min(FOX_TILE, s)
    full = lambda j: pl.BlockSpec((s, BRANCH_W), lambda i, j=j: (0, j), pipeline_mode=pl.Buffered(1))
    return pl.pallas_call(
        functools.partial(_fox_kernel, tq=tq),
        grid=(s // tq,),
        in_specs=[pl.BlockSpec(memory_space=pltpu.SMEM),
                  pl.BlockSpec((tq, BRANCH_W), lambda i: (i, 4)), full(5), full(6),
                  _const_spec((SUBLANE, s)), pl.BlockSpec((tq, LANE), lambda i: (i, 0)),
                  _const_spec((N_HEADS, BRANCH_W))],
        out_specs=pl.BlockSpec((tq, BRANCH_W), lambda i: (i, 0)),
        out_shape=jax.ShapeDtypeStruct((s, BRANCH_W), BF16),
        scratch_shapes=[pltpu.VMEM((N_HEADS, tq, LANE), F32),
                        pltpu.VMEM((N_HEADS, tq, LANE), F32),
                        pltpu.VMEM((N_HEADS, tq, BRANCH_W), F32)],
        compiler_params=_params("parallel"),
        name="fox_attention",
    )(stats, proj, proj, proj, ct, c_rows, _head_masks())


def _pool_kernel(u_ref, w_ref, scale_ref, o_ref, ext, *, tile):
    i = pl.program_id(0)

    @pl.when(i == 0)
    def _():
        ext[pl.ds(0, POOL_HALO), :] = jnp.zeros((POOL_HALO, BRANCH_W), F32)

    u = u_ref[...].astype(F32)
    ext[pl.ds(POOL_HALO, tile), :] = u
    pos = (i * tile + lax.broadcasted_iota(jnp.int32, (tile, 1), 0) + 1).astype(F32)
    halves = []
    for half in range(BRANCH_W // LANE):
        lanes = pl.ds(half * LANE, LANE)
        w_small, w_big = POOL_WINDOWS[2 * half], POOL_WINDOWS[2 * half + 1]
        run = u[:, half * LANE:(half + 1) * LANE]
        sums = {}
        for j in range(1, w_big):
            if j == w_small:
                sums[w_small] = run
            run = run + ext[pl.ds(POOL_HALO - j, tile), lanes]
        sums[w_big] = run
        lane = lax.broadcasted_iota(jnp.int32, (1, LANE), 1)
        small = lane < POOL_GROUP
        total = jnp.where(small, sums[w_small], sums[w_big])
        count = jnp.where(small, jnp.minimum(pos, float(w_small)), jnp.minimum(pos, float(w_big)))
        halves.append(total / count)
    mean = jnp.concatenate(halves, axis=1)
    d = (mean - u).astype(BF16)
    y = _dot(d, w_ref[...]) * scale_ref[...]
    o_ref[...] = y.astype(o_ref.dtype)
    ext[pl.ds(0, POOL_HALO), :] = u[tile - POOL_HALO:, :]


def pool_mixer(proj, w_pool, scale, tile=2048):
    s = proj.shape[0]
    tile = min(tile, s)
    ng = len(POOL_WINDOWS)
    w_bd = jnp.zeros((BRANCH_W, BRANCH_W), F32)
    for gi in range(ng):
        lo = gi * POOL_GROUP
        w_bd = w_bd.at[lo:lo + POOL_GROUP, lo:lo + POOL_GROUP].set(w_pool[gi].astype(F32))
    return pl.pallas_call(
        functools.partial(_pool_kernel, tile=tile),
        grid=(s // tile,),
        in_specs=[pl.BlockSpec((tile, BRANCH_W), lambda i: (i, 7)),
                  _const_spec((BRANCH_W, BRANCH_W)), _const_spec((1, BRANCH_W))],
        out_specs=pl.BlockSpec((tile, BRANCH_W), lambda i: (i, 0)),
        out_shape=jax.ShapeDtypeStruct((s, BRANCH_W), BF16),
        scratch_shapes=[pltpu.VMEM((tile + POOL_HALO, BRANCH_W), F32)],
        compiler_params=_params("arbitrary"),
        name="pool_mixer",
    )(proj, w_bd.astype(BF16), scale.reshape(1, BRANCH_W).astype(F32))


def _ret_kernel(q_ref, k_ref, v_ref, g_ref, rope_ref, ecos_ref, esin_ref, perm_ref, ones_ref, hm_ref,
                dstack_ref, xi_ref, zeta_ref, gc_ref, gng_ref, gnb_ref, o_ref, st_ref, *, tile):
    c = RET_CHUNK

    @pl.when(pl.program_id(0) == 0)
    def _():
        st_ref[...] = jnp.zeros_like(st_ref)

    perm = perm_ref[...]
    ones_bd = ones_ref[...]
    bd_mask = ones_bd.astype(F32)
    hm = hm_ref[...]

    tab = rope_ref[...]
    cos = _dot_exact_rhs(tab, ecos_ref[...])
    sin = _dot_exact_rhs(tab, esin_ref[...])
    q_all = q_ref[...]
    k_all = k_ref[...]
    qr_all = q_all.astype(F32) * cos + _dot(q_all, perm) * sin
    kr_all = (k_all.astype(F32) * cos + _dot(k_all, perm) * sin) * (HEAD_DIM ** -0.5)

    outs = []
    for ci in range(tile // c):
        r0 = ci * c
        qr = qr_all[r0:r0 + c, :]
        kr = kr_all[r0:r0 + c, :]
        v = v_ref[pl.ds(r0, c), :]

        qx = jnp.concatenate([qr * hm[h:h + 1, :] for h in range(N_HEADS)], axis=0).astype(BF16)
        sc = _dot_nt(qx, kr.astype(BF16)) * dstack_ref[...]
        r = _dot(sc.astype(BF16), v)
        intra = jnp.zeros((c, BRANCH_W), F32)
        for h in range(N_HEADS):
            intra = intra + r[h * c:(h + 1) * c, :] * hm[h:h + 1, :]

        st = st_ref[...]
        inter = _dot_nt((qr * xi_ref[...]).astype(BF16), st.astype(BF16))
        upd = _dot(v.astype(F32).T.astype(BF16), (kr * zeta_ref[...]).astype(BF16))
        st_ref[...] = st * gc_ref[...] + upd * bd_mask
        outs.append(intra + inter)

    o = jnp.concatenate(outs, axis=0)
    mu = _dot_exact_rhs(o, ones_bd) * (1.0 / HEAD_DIM)
    cen = o - mu
    var = _dot_exact_rhs(cen * cen, ones_bd) * (1.0 / HEAD_DIM)
    y = cen * lax.rsqrt(var + LN_EPS) * gng_ref[...] + gnb_ref[...]
    o_ref[...] = (y * _silu(g_ref[...].astype(F32))).astype(o_ref.dtype)


def _rope_tables(s):
    half = HEAD_DIM // 2
    pos = np.arange(s, dtype=np.float64)
    inv_freq = ROPE_BASE ** (-np.arange(half, dtype=np.float64) / half)
    ang = pos[:, None] * inv_freq[None, :]
    table = np.concatenate([np.cos(ang), np.sin(ang), np.zeros((s, LANE - 2 * half))], axis=1)
    lane = np.arange(BRANCH_W)
    src = np.arange(LANE)[:, None]
    ecos = src == (lane % half)[None, :]
    sign = np.where(lane % HEAD_DIM < half, -1.0, 1.0)
    esin = (src == (half + lane % half)[None, :]) * sign[None, :]
    return _const(table), _const(ecos, BF16), _const(esin, BF16)


def _ret_constants():
    c = RET_CHUNK
    half = HEAD_DIM // 2
    lane = np.arange(BRANCH_W)
    partner = np.where(lane % HEAD_DIM < half, lane + half, lane - half)
    perm = lane[:, None] == partner[None, :]
    log_gamma = np.log1p(-np.exp2(-RET_DECAY_BASE - np.arange(N_HEADS, dtype=np.float64)))
    ci = np.arange(c, dtype=np.float64)
    diff = ci[:, None] - ci[None, :]
    intra = np.where(diff >= 0, np.exp(diff * log_gamma[:, None, None]), 0.0)
    dstack = intra.reshape(N_HEADS * c, c)
    lg_lane = np.repeat(log_gamma, HEAD_DIM)[None, :]
    xi = np.exp((ci[:, None] + 1.0) * lg_lane)
    zeta = np.exp((c - 1.0 - ci[:, None]) * lg_lane)
    gc = np.exp(c * lg_lane)
    return _const(perm, BF16), _const(dstack), _const(xi), _const(zeta), _const(gc)


def retention(proj, gn_g, gn_b, tile=1024):
    s = proj.shape[0]
    c = RET_CHUNK
    tile = min(tile, s)
    rope, ecos, esin = _rope_tables(s)
    perm, dstack, xi, zeta, gc = _ret_constants()
    col = lambda j: pl.BlockSpec((tile, BRANCH_W), lambda i, j=j: (i, j))
    row = pl.BlockSpec((tile, BRANCH_W), lambda i: (i, 0))
    return pl.pallas_call(
        functools.partial(_ret_kernel, tile=tile),
        grid=(s // tile,),
        in_specs=[col(8), col(9), col(10), col(11), pl.BlockSpec((tile, LANE), lambda i: (i, 0)),
                  _const_spec((LANE, BRANCH_W)), _const_spec((LANE, BRANCH_W)),
                  _const_spec((BRANCH_W, BRANCH_W)), _const_spec((BRANCH_W, BRANCH_W)),
                  _const_spec((N_HEADS, BRANCH_W)), _const_spec((N_HEADS * c, c)),
                  _const_spec((c, BRANCH_W)), _const_spec((c, BRANCH_W)), _const_spec((1, BRANCH_W)),
                  _const_spec((1, BRANCH_W)), _const_spec((1, BRANCH_W))],
        out_specs=row,
        out_shape=jax.ShapeDtypeStruct((s, BRANCH_W), BF16),
        scratch_shapes=[pltpu.VMEM((BRANCH_W, BRANCH_W), F32)],
        compiler_params=_params("arbitrary"),
        name="retention",
    )(proj, proj, proj, proj, rope, ecos, esin, perm, _head_ones(), _head_masks(), dstack, xi, zeta, gc,
      gn_g.reshape(1, BRANCH_W).astype(F32), gn_b.reshape(1, BRANCH_W).astype(F32))


def _merge_kernel(h_ref, o0_ref, o1_ref, o2_ref, o3_ref, x_ref, wg_ref, wb_ref, wo_ref, g_ref, *rest):
    n_cast = (len(rest) - 2) // 2
    cast_in, (xo_ref, ho_ref), cast_out = rest[:n_cast], rest[n_cast:n_cast + 2], rest[n_cast + 2:]
    for src, dst in zip(cast_in, cast_out):
        dst[...] = src[...].astype(dst.dtype)
    h = h_ref[...]
    merged = jnp.zeros(x_ref.shape, F32)
    for bi, o_ref in enumerate((o0_ref, o1_ref, o2_ref, o3_ref)):
        gate = _sigmoid(_dot(h, wg_ref[:, bi * D_MODEL:(bi + 1) * D_MODEL]))
        merged = merged + gate * _dot(o_ref[...], wb_ref[bi])
    x_new = x_ref[...] + _dot(merged.astype(BF16), wo_ref[...])
    xo_ref[...] = x_new
    ho_ref[...] = _rms(x_new, g_ref[...]).astype(ho_ref.dtype)


def merge(h, branches, x, w_gate, w_branch, w_out, next_gain, tm=512, cast=()):
    s, d = x.shape
    tm = min(tm, s)
    steps = s // tm
    row = lambda w: pl.BlockSpec((tm, w), lambda i: (i, 0))
    flat = [c.reshape(-1, c.shape[-1]) for c in cast]
    slab = lambda c: pl.BlockSpec((c.shape[0] // steps, c.shape[1]), lambda i: (i, 0))
    outs = pl.pallas_call(
        _merge_kernel,
        grid=(steps,),
        in_specs=[row(d), row(BRANCH_W), row(BRANCH_W), row(BRANCH_W), row(BRANCH_W), row(d),
                  _const_spec((d, N_BRANCH * d)), _const_spec((N_BRANCH, BRANCH_W, d)),
                  _const_spec((d, d)), _const_spec((1, d))] + [slab(c) for c in flat],
        out_specs=[row(d), row(d)] + [slab(c) for c in flat],
        out_shape=[jax.ShapeDtypeStruct((s, d), F32), jax.ShapeDtypeStruct((s, d), BF16)]
        + [jax.ShapeDtypeStruct(c.shape, BF16) for c in flat],
        compiler_params=_params("parallel"),
        name="merge",
    )(h, *branches, x, w_gate, w_branch, w_out, next_gain.reshape(1, d).astype(F32), *flat)
    return outs[0], outs[1], [o.reshape(c.shape) for o, c in zip(outs[2:], cast)]


def _ffn_kernel(h_ref, x_ref, wg_ref, wu_ref, wd_ref, g_ref, xo_ref, ho_ref, acc_ref):
    f = pl.program_id(1)

    @pl.when(f == 0)
    def _():
        acc_ref[...] = jnp.zeros_like(acc_ref)

    h = h_ref[...]
    a = _silu(_dot(h, wg_ref[...].astype(BF16))) * _dot(h, wu_ref[...].astype(BF16))
    acc_ref[...] += _dot(a.astype(BF16), wd_ref[...].astype(BF16))

    @pl.when(f == pl.num_programs(1) - 1)
    def _():
        x_new = x_ref[...] + acc_ref[...]
        xo_ref[...] = x_new
        ho_ref[...] = _rms(x_new, g_ref[...]).astype(ho_ref.dtype)


def ffn_dense(h, x, w_gate, w_up, w_down, next_gain, tm=1024, tf=512):
    s, d = x.shape
    tm = min(tm, s)
    dff = w_gate.shape[1]
    row = lambda: pl.BlockSpec((tm, d), lambda i, f: (i, 0))
    return pl.pallas_call(
        _ffn_kernel,
        grid=(s // tm, dff // tf),
        in_specs=[row(), row(),
                  pl.BlockSpec((d, tf), lambda i, f: (0, f)),
                  pl.BlockSpec((d, tf), lambda i, f: (0, f)),
                  pl.BlockSpec((tf, d), lambda i, f: (f, 0)),
                  _const_spec((1, d))],
        out_specs=[row(), row()],
        out_shape=[jax.ShapeDtypeStruct((s, d), F32), jax.ShapeDtypeStruct((s, d), BF16)],
        scratch_shapes=[pltpu.VMEM((tm, d), F32)],
        compiler_params=_params("parallel", "arbitrary"),
        name="ffn_dense",
    )(h, x, w_gate, w_up, w_down, next_gain.reshape(1, d).astype(F32))


MERGE_CAST_TILE = 256
MOE_TOK_TILE = 256
MOE_ROUTE_TILES = 4
MOE_ROW_BLOCK = 512
MOE_GATHER_ROWS = 128
MOE_GATHER_TILES = 4
MOE_VMEM_LIMIT = 60 * 1024 * 1024
MOE_Y_BLOCK = 128
MOE_Y_FETCH = MOE_TOK_TILE // MOE_Y_BLOCK + 1


def _dot_f32(x, w):
    xh, xm, _ = _split3(x)
    wh, wm, _ = _split3(w)
    return _dot(xh, wh) + (_dot(xh, wm) + _dot(xm, wh))


def _route_kernel(x_ref, ng_ref, router_ref, ltri_ref, pos_ref, gate_ref, post_ref, before_ref, total_ref,
                  carry_ref):
    tm = x_ref.shape[0]

    @pl.when(pl.program_id(0) == 0)
    def _():
        carry_ref[...] = jnp.zeros_like(carry_ref)

    hn = _rms(x_ref[...], ng_ref[...])
    logits = _dot_f32(hn, router_ref[...])
    lane = lax.broadcasted_iota(jnp.int32, (tm, LANE), 1)
    logits = jnp.where(lane < N_EXPERTS, logits, -jnp.inf)
    v1 = jnp.max(logits, axis=1, keepdims=True)
    i1 = jnp.min(jnp.where(logits == v1, lane, LANE), axis=1, keepdims=True)
    rest = jnp.where(lane == i1, -jnp.inf, logits)
    v2 = jnp.max(rest, axis=1, keepdims=True)
    i2 = jnp.min(jnp.where(rest == v2, lane, LANE), axis=1, keepdims=True)
    w1 = 1.0 / (1.0 + jnp.exp(v2 - v1))
    gate_ref[...] = jnp.where(lane == i1, w1, 0.0) + jnp.where(lane == i2, 1.0 - w1, 0.0)

    member = jnp.where((lane == i1) | (lane == i2), 1.0, 0.0)
    carry = carry_ref[...]
    rank = _dot(ltri_ref[...], member.astype(BF16)) + carry
    pos = jnp.where(member > 0.0, rank, -1.0)
    pos_ref[...] = pos
    post_ref[...] = pos.T[:SUBLANE, :]
    for blk in range(tm // MOE_TOK_TILE):
        before_ref[blk] = carry
        carry = carry + jnp.sum(member[blk * MOE_TOK_TILE:(blk + 1) * MOE_TOK_TILE], axis=0, keepdims=True)
    carry_ref[...] = carry
    total_ref[...] = carry


def moe_route(x, norm_gain, router):
    s, d = x.shape
    tm = min(MOE_ROUTE_TILES * MOE_TOK_TILE, s)
    nsub = tm // MOE_TOK_TILE
    nt = s // tm
    router_p = jnp.zeros((d, LANE), F32).at[:, :N_EXPERTS].set(router.astype(F32))
    ltri = _const(np.tril(np.ones((tm, tm)), -1), BF16)
    row = pl.BlockSpec((tm, LANE), lambda i: (i, 0))
    return pl.pallas_call(
        _route_kernel,
        grid=(nt,),
        in_specs=[pl.BlockSpec((tm, d), lambda i: (i, 0)), _const_spec((1, d)), _const_spec((d, LANE)),
                  _const_spec((tm, tm))],
        out_specs=[row, row, pl.BlockSpec((SUBLANE, tm), lambda i: (0, i)),
                   pl.BlockSpec((nsub, 1, LANE), lambda i: (i, 0, 0)), pl.BlockSpec((1, LANE), lambda i: (0, 0))],
        out_shape=[jax.ShapeDtypeStruct((s, LANE), F32), jax.ShapeDtypeStruct((s, LANE), F32),
                   jax.ShapeDtypeStruct((SUBLANE, s), F32), jax.ShapeDtypeStruct((nt * nsub, 1, LANE), F32),
                   jax.ShapeDtypeStruct((1, LANE), F32)],
        scratch_shapes=[pltpu.VMEM((1, LANE), F32)],
        compiler_params=_params("arbitrary"),
        name="moe_route",
    )(x, norm_gain.reshape(1, d).astype(F32), router_p, ltri)


def _moe_ffn_kernel(be_ref, r0_ref, tlo_ref, thi_ref, nv_ref, h_ref, post_ref, wg_ref, wu_ref, wd_ref,
                    y_ref, x_sc, acc_sc, *, tt, nsub):
    b = pl.program_id(0)
    f = pl.program_id(1)
    nb = pl.num_programs(0)
    valid = b < nv_ref[0]
    tmb, d = acc_sc.shape
    gr = tmb // nsub
    last_tile = h_ref.shape[0] // tt - 1

    def picked(e, want, t):
        t0 = pl.multiple_of(t * tt, tt)
        p = post_ref[pl.ds(e, 1), pl.ds(t0, tt)]
        sel = jnp.where(p == want, 1.0, 0.0).astype(BF16)
        return _dot(sel, h_ref[pl.ds(t0, tt), :])

    def gather_head(blk, sb):
        e = be_ref[blk]
        lo = tlo_ref[blk * nsub + sb]
        hi = thi_ref[blk * nsub + sb]
        want = (lax.broadcasted_iota(jnp.int32, (gr, 1), 0) + (r0_ref[blk] + sb * gr)).astype(F32)
        rows = picked(e, want, lo)
        for k in range(1, MOE_GATHER_TILES):
            rows = rows + picked(e, jnp.where(lo + k <= hi, want, -2.0), jnp.minimum(lo + k, last_tile))
        return rows.astype(BF16)

    def gather_tail(slot, blk, sb):
        e = be_ref[blk]
        want = (lax.broadcasted_iota(jnp.int32, (gr, 1), 0) + (r0_ref[blk] + sb * gr)).astype(F32)
        rows = pl.ds(pl.multiple_of(sb * gr, gr), gr)

        def more(t, carry):
            x_sc[slot, rows, :] = (x_sc[slot, rows, :].astype(F32) + picked(e, want, t)).astype(BF16)
            return carry

        lax.fori_loop(tlo_ref[blk * nsub + sb] + MOE_GATHER_TILES, thi_ref[blk * nsub + sb] + 1, more, 0)

    @pl.when((b == 0) & (f == 0))
    def _():
        for sb in range(nsub):
            x_sc[0, pl.ds(sb * gr, gr), :] = gather_head(0, sb)
            gather_tail(0, 0, sb)

    nxt = jnp.minimum(b + 1, nb - 1)
    nslot = (b + 1) % 2

    @pl.when(valid)
    def _():
        x_sc[nslot, pl.ds(pl.multiple_of(f * gr, gr), gr), :] = gather_head(nxt, f)
        xb = x_sc[b % 2]
        a = _silu(_dot(xb, wg_ref[0])) * _dot(xb, wu_ref[0])
        acc_sc[...] = jnp.where(f == 0, 0.0, acc_sc[...]) + _dot(a.astype(BF16), wd_ref[0].astype(BF16))

    @pl.when(valid & (thi_ref[nxt * nsub + f] - tlo_ref[nxt * nsub + f] >= MOE_GATHER_TILES))
    def _():
        gather_tail(nslot, nxt, f)

    @pl.when(f == nsub - 1)
    def _():
        y_ref[...] = jnp.where(valid, acc_sc[...], 0.0).astype(y_ref.dtype)


def moe_ffn(h, post, sched, w_gate, w_up, w_down):
    s, d = h.shape
    ne, _, dff = w_gate.shape
    tmb = MOE_ROW_BLOCK
    tt = min(MOE_TOK_TILE, s)
    nb = sched[0].shape[0]
    nf = tmb // MOE_GATHER_ROWS
    tf = dff // nf

    def fidx(b, f, nv):
        return jnp.where(b < nv[0], f, nf - 1)

    grid_spec = pltpu.PrefetchScalarGridSpec(
        num_scalar_prefetch=5,
        grid=(nb, nf),
        in_specs=[pl.BlockSpec((s, d), lambda b, f, *_: (0, 0), pipeline_mode=pl.Buffered(1)),
                  pl.BlockSpec((SUBLANE, s), lambda b, f, *_: (0, 0), pipeline_mode=pl.Buffered(1)),
                  pl.BlockSpec((1, d, tf), lambda b, f, be, r0, tlo, thi, nv: (be[b], 0, fidx(b, f, nv))),
                  pl.BlockSpec((1, d, tf), lambda b, f, be, r0, tlo, thi, nv: (be[b], 0, fidx(b, f, nv))),
                  pl.BlockSpec((1, tf, d), lambda b, f, be, r0, tlo, thi, nv: (be[b], fidx(b, f, nv), 0))],
        out_specs=pl.BlockSpec((tmb, d), lambda b, f, *_: (b, 0)),
        scratch_shapes=[pltpu.VMEM((2, tmb, d), BF16), pltpu.VMEM((tmb, d), F32)],
    )
    return pl.pallas_call(
        functools.partial(_moe_ffn_kernel, tt=tt, nsub=nf),
        grid_spec=grid_spec,
        out_shape=jax.ShapeDtypeStruct((nb * tmb, d), BF16),
        compiler_params=pltpu.CompilerParams(dimension_semantics=("arbitrary", "arbitrary"),
                                             vmem_limit_bytes=MOE_VMEM_LIMIT),
        name="moe_ffn",
    )(*sched, h, post, w_gate, w_up, w_down)


def _moe_combine_kernel(kb_ref, off_ref, lim_ref, x_ref, pos_ref, gate_ref, fg_ref, *rest):
    y_refs, o_ref, acc_sc = rest[:-2], rest[-2], rest[-1]
    t = pl.program_id(0)
    yb = y_refs[0].shape[0]
    pos = pos_ref[...]
    gate = gate_ref[...]
    col = lax.broadcasted_iota(jnp.int32, (1, yb), 1).astype(F32)

    def routed(e):
        pe = pos[:, e:e + 1]
        ge = gate[:, e:e + 1]
        r = jnp.where(pe >= 0.0, pe + off_ref[t * N_EXPERTS + e].astype(F32), -1.0)
        return r, ge

    def picked(e, k, r):
        sel = jnp.where(r == col + float(k * yb), 1.0, 0.0).astype(BF16)
        return _dot(sel, y_refs[MOE_Y_FETCH * e + k][...])

    col2 = lax.broadcasted_iota(jnp.int32, (1, 2 * yb), 1).astype(F32)
    acc = x_ref[...]
    for e in range(N_EXPERTS):
        r, ge = routed(e)
        pair = jnp.concatenate([y_refs[MOE_Y_FETCH * e][...], y_refs[MOE_Y_FETCH * e + 1][...]], axis=0)
        acc = acc + ge * _dot(jnp.where(r == col2, 1.0, 0.0).astype(BF16), pair)
    acc_sc[...] = acc
    for e in range(N_EXPERTS):
        for k in range(2, MOE_Y_FETCH):
            @pl.when(lim_ref[t * N_EXPERTS + e] > k * yb)
            def _(k=k, e=e):
                r, ge = routed(e)
                acc_sc[...] += ge * picked(e, k, r)
    o_ref[...] = _rms(acc_sc[...], fg_ref[...]).astype(o_ref.dtype)


def moe_combine(x, pos, gate, y, kb, off, lim, final_gain):
    s, d = x.shape
    tm = min(MOE_TOK_TILE, s)
    yb = MOE_Y_BLOCK
    last = y.shape[0] // yb - 1

    def yspec(e, k):
        def index(t, kb_r, off_r, lim_r):
            blk = jnp.minimum(kb_r[t * N_EXPERTS + e] + k, last)
            return (blk if k < 1 else jnp.where(lim_r[t * N_EXPERTS + e] > k * yb, blk, 0), 0)
        return pl.BlockSpec((yb, d), index)

    grid_spec = pltpu.PrefetchScalarGridSpec(
        num_scalar_prefetch=3,
        grid=(s // tm,),
        in_specs=[pl.BlockSpec((tm, d), lambda t, *_: (t, 0)),
                  pl.BlockSpec((tm, LANE), lambda t, *_: (t, 0)),
                  pl.BlockSpec((tm, LANE), lambda t, *_: (t, 0)),
                  pl.BlockSpec((1, d), lambda t, *_: (0, 0))]
        + [yspec(e, k) for e in range(N_EXPERTS) for k in range(MOE_Y_FETCH)],
        out_specs=pl.BlockSpec((tm, d), lambda t, *_: (t, 0)),
        scratch_shapes=[pltpu.VMEM((tm, d), F32)],
    )
    return pl.pallas_call(
        _moe_combine_kernel,
        grid_spec=grid_spec,
        out_shape=jax.ShapeDtypeStruct((s, d), F32),
        compiler_params=_params("arbitrary"),
        name="moe_combine",
    )(kb, off, lim, x, pos, gate, final_gain.reshape(1, d).astype(F32), *([y] * (MOE_Y_FETCH * N_EXPERTS)))


def _moe_schedule(before, total, s):
    tmb, yb, gr = MOE_ROW_BLOCK, MOE_Y_BLOCK, MOE_GATHER_ROWS
    nb = 2 * s // tmb + N_EXPERTS
    counts = total[0, :N_EXPERTS].astype(jnp.int32)
    nblk = (counts + tmb - 1) // tmb
    end = jnp.cumsum(nblk)
    first = end - nblk
    nvalid = end[-1]
    b = jnp.minimum(jnp.arange(nb, dtype=jnp.int32), nvalid - 1)
    blk_e = jnp.sum(b[:, None] >= end[None, :], axis=1).astype(jnp.int32)
    r0 = (b - first[blk_e]) * tmb
    cb = before[:, 0, :N_EXPERTS].astype(jnp.int32)
    r0s = (r0[:, None] + gr * jnp.arange(tmb // gr, dtype=jnp.int32)[None, :]).reshape(-1)
    cbe = cb[:, jnp.repeat(blk_e, tmb // gr)]
    tlo = (jnp.sum(cbe <= r0s[None, :], axis=0) - 1).astype(jnp.int32)
    thi = (jnp.sum(cbe < (r0s + gr)[None, :], axis=0) - 1).astype(jnp.int32)
    sched = (blk_e, r0.astype(jnp.int32), tlo, thi, nvalid.reshape(1).astype(jnp.int32))
    row_start = first[None, :] * tmb + cb
    kb = row_start // yb
    off = first[None, :] * tmb - kb * yb
    n_te = jnp.concatenate([cb[1:], counts[None, :]], axis=0) - cb
    lim = row_start - kb * yb + n_te
    flat = lambda a: a.reshape(-1).astype(jnp.int32)
    return sched, flat(kb), flat(off), flat(lim)


def moe_sparse(h, x, norm_gain, router, w_gate, w_up, w_down, final_gain):
    s, _ = x.shape
    pos, gate, post, before, total = moe_route(x, norm_gain, router)
    sched, kb, off, lim = _moe_schedule(before, total, s)
    y = moe_ffn(h, post, sched, w_gate, w_up, w_down)
    return moe_combine(x, pos, gate, y, kb, off, lim, final_gain)


FOX_F0 = 7 * BRANCH_W


def _split_w_in_kernel(wt_ref, mix_ref, f_ref, gate_ref):
    rest0 = FOX_F0 + N_HEADS
    gate0 = N_MIX_COLS + N_HEADS
    mix_ref[:, :FOX_F0] = wt_ref[pl.ds(0, FOX_F0), :].T.astype(BF16)
    mix_ref[:, FOX_F0:] = wt_ref[pl.ds(rest0, N_MIX_COLS - FOX_F0), :].T.astype(BF16)
    gate_ref[...] = wt_ref[pl.ds(gate0, N_BRANCH * D_MODEL), :].T.astype(BF16)
    f_rows = jnp.concatenate([wt_ref[pl.ds(FOX_F0, N_HEADS), :], jnp.zeros((LANE - N_HEADS, LANE), F32)], axis=0)
    f_ref[...] = f_rows.T.astype(BF16)


def _mixer_weights(w_in, layer, tr=LANE):
    _, d, cols = w_in.shape
    w_t = jnp.swapaxes(w_in, 1, 2)
    row = lambda w: pl.BlockSpec((tr, w), lambda i: (i, 0))
    return pl.pallas_call(
        _split_w_in_kernel,
        grid=(d // tr,),
        in_specs=[pl.BlockSpec((None, cols, tr), lambda i: (layer, 0, i))],
        out_specs=[row(N_MIX_COLS), row(LANE), row(N_BRANCH * D_MODEL)],
        out_shape=[jax.ShapeDtypeStruct((d, N_MIX_COLS), BF16), jax.ShapeDtypeStruct((d, LANE), BF16),
                   jax.ShapeDtypeStruct((d, N_BRANCH * D_MODEL), BF16)],
        compiler_params=_params("parallel"),
        name="split_w_in",
    )(w_t)


def kernel(x, w_in, w_branch, w_out, norm_mix_g, hgrn_lb_logits, hgrn_norm_g, fox_f_bias, pool_w, pool_scale,
           ret_gn_g, ret_gn_b, norm_ffn_g, ffn_w_gate, ffn_w_up, ffn_w_down, moe_router, moe_w_gate, moe_w_up,
           moe_w_down, final_norm_g):
    b, s, d = x.shape
    assert b == 1 and d == D_MODEL
    depth = w_in.shape[0]
    assert depth == 2, "layer 0 uses the dense FFN, layer 1 the experts and the final norm"
    xs = x.reshape(s, d)
    h = None
    out = None
    for layer in range(depth):
        w_mix, w_f, w_gate = _mixer_weights(w_in, layer)
        if layer == 0:
            proj, h = norm_matmul(xs, norm_mix_g[0], w_mix)
        else:
            proj = matmul(h, w_mix)
        ct, c_rows, fox_stats = fox_gate(h, proj, w_f, fox_f_bias[layer])
        branches = (
            hgrn2(proj, hgrn_lb_logits, hgrn_norm_g[layer], layer),
            fox_attention(proj, ct, c_rows, fox_stats),
            pool_mixer(proj, pool_w[layer], pool_scale[layer]),
            retention(proj, ret_gn_g[layer], ret_gn_b[layer]),
        )
        li = layer // 2
        if layer % 2 == 0:
            xs, h2, (ffn_wg, ffn_wu) = merge(h, branches, xs, w_gate, w_branch[layer].astype(BF16),
                                             w_out[layer].astype(BF16), norm_ffn_g[layer],
                                             cast=(ffn_w_gate[li], ffn_w_up[li]))
            xs, h = ffn_dense(h2, xs, ffn_wg, ffn_wu, ffn_w_down[li], norm_mix_g[layer + 1])
        else:
            xs, h2, (moe_wg, moe_wu) = merge(h, branches, xs, w_gate, w_branch[layer].astype(BF16),
                                             w_out[layer].astype(BF16), norm_ffn_g[layer], tm=MERGE_CAST_TILE,
                                             cast=(moe_w_gate[li], moe_w_up[li]))
            out = moe_sparse(h2, xs, norm_ffn_g[layer], moe_router[li], moe_wg, moe_wu, moe_w_down[li],
                             final_norm_g)
    return out.reshape(b, s, d)
```

```python
import functools
import math

import jax
import jax.numpy as jnp
import numpy as np
from jax import lax
from jax.experimental import pallas as pl
from jax.experimental.pallas import tpu as pltpu

D_MODEL = 1024
N_BRANCH = 4
BRANCH_W = D_MODEL // N_BRANCH
HEAD_DIM = 64
N_HEADS = BRANCH_W // HEAD_DIM
POOL_WINDOWS = (2, 4, 8, 16)
POOL_GROUP = BRANCH_W // len(POOL_WINDOWS)
POOL_HALO = 16
RET_DECAY_BASE = 5.0
ROPE_BASE = 10000.0
D_FF = 7 * D_MODEL // 2
N_EXPERTS = 8
RMS_EPS = 1e-6
LN_EPS = 1e-5
N_MIX_COLS = 12 * BRANCH_W

LANE = 128
SUBLANE = 8
VMEM_LIMIT = 56 * 1024 * 1024

HG_CHUNK = 64
HG_SUB = 16
HG_GROUP = 1
HG_FAST_MIN_LOGDECAY = -60.0
RET_CHUNK = 256

F32 = jnp.float32
BF16 = jnp.bfloat16
NT_DIMS = (((1,), (1,)), ((), ()))


def _params(*sem):
    return pltpu.CompilerParams(dimension_semantics=sem, vmem_limit_bytes=VMEM_LIMIT)


def _const_spec(shape):
    nd = len(shape)
    return pl.BlockSpec(shape, lambda *_: (0,) * nd, pipeline_mode=pl.Buffered(1))


def _split3(x):
    hi = x.astype(BF16)
    r1 = x - hi.astype(F32)
    mid = r1.astype(BF16)
    lo = (r1 - mid.astype(F32)).astype(BF16)
    return hi, mid, lo


def _dot(a, b):
    return jnp.dot(a, b, preferred_element_type=F32)


def _dot_nt(a, b):
    return lax.dot_general(a, b, NT_DIMS, preferred_element_type=F32)


def _dot_exact_rhs(x, m_bf16):
    hi, mid, lo = _split3(x)
    return _dot(hi, m_bf16) + _dot(mid, m_bf16) + _dot(lo, m_bf16)


def _dot_exact_lhs(m_bf16, x):
    hi, mid, lo = _split3(x)
    return _dot(m_bf16, hi) + _dot(m_bf16, mid) + _dot(m_bf16, lo)


def _sigmoid(x):
    return 1.0 / (1.0 + jnp.exp(-x))


def _silu(x):
    return x * _sigmoid(x)


def _rms(x, gain):
    return x * lax.rsqrt(jnp.mean(x * x, axis=-1, keepdims=True) + RMS_EPS) * gain


def _const(a, dtype=F32):
    return jnp.asarray(np.asarray(a, np.float32), dtype)


def _head_of(n):
    return np.arange(n) // HEAD_DIM


def _head_ones():
    h = _head_of(BRANCH_W)
    return _const(h[:, None] == h[None, :], BF16)


def _head_masks():
    return _const(_head_of(BRANCH_W)[None, :] == np.arange(N_HEADS)[:, None])


def _norm_matmul_kernel(x_ref, g_ref, b_ref, o_ref, h_ref):
    @pl.when(pl.program_id(1) == 0)
    def _():
        h_ref[...] = _rms(x_ref[...], g_ref[...]).astype(h_ref.dtype)

    o_ref[...] = _dot(h_ref[...], b_ref[...]).astype(o_ref.dtype)


def norm_matmul(x, gain, b, tm=2048, tn=1024):
    m, k = x.shape
    _, n = b.shape
    tm = min(tm, m)
    return pl.pallas_call(
        _norm_matmul_kernel,
        grid=(m // tm, n // tn),
        in_specs=[pl.BlockSpec((tm, k), lambda i, j: (i, 0)), _const_spec((1, k)),
                  pl.BlockSpec((k, tn), lambda i, j: (0, j))],
        out_specs=[pl.BlockSpec((tm, tn), lambda i, j: (i, j)), pl.BlockSpec((tm, k), lambda i, j: (i, 0))],
        out_shape=[jax.ShapeDtypeStruct((m, n), BF16), jax.ShapeDtypeStruct((m, k), BF16)],
        compiler_params=_params("arbitrary", "arbitrary"),
        name="in_proj_norm",
    )(x, gain.reshape(1, k).astype(F32), b)


def _matmul_kernel(a_ref, b_ref, o_ref):
    o_ref[...] = _dot(a_ref[...], b_ref[...]).astype(o_ref.dtype)


def matmul(a, b, out_dtype=BF16, tm=2048, tn=1024):
    m, k = a.shape
    _, n = b.shape
    tm = min(tm, m)
    return pl.pallas_call(
        _matmul_kernel,
        grid=(n // tn, m // tm),
        in_specs=[pl.BlockSpec((tm, k), lambda j, i: (i, 0)),
                  pl.BlockSpec((k, tn), lambda j, i: (0, j))],
        out_specs=pl.BlockSpec((tm, tn), lambda j, i: (i, j)),
        out_shape=jax.ShapeDtypeStruct((m, n), out_dtype),
        compiler_params=_params("parallel", "parallel"),
        name="in_proj",
    )(a, b)


def _hgrn_kernel(q_ref, f_ref, i_ref, g_ref, lbl_ref, ng_ref, ones_ref, hm_ref, tril_ref, halfsum_ref, fmask_ref,
                 o_ref, st_ref, bpad, kpad, vpad, astack, lf_sc, kk_sc, o_sc, *, layer, tile):
    c, sub = HG_CHUNK, HG_SUB
    nsub = c // sub
    half = c // 2

    @pl.when(pl.program_id(0) == 0)
    def _():
        st_ref[...] = jnp.zeros_like(st_ref)
        bpad[...] = jnp.zeros_like(bpad)
        kpad[...] = jnp.zeros_like(kpad)
        vpad[...] = jnp.zeros_like(vpad)

    lbl = lbl_ref[...]
    e = jnp.exp(lbl - jnp.max(lbl, axis=0, keepdims=True))
    p = e / jnp.sum(e, axis=0, keepdims=True)
    lb = jnp.zeros((1, BRANCH_W), F32)
    for l in range(1, layer + 1):
        lb = lb + p[l:l + 1, :]

    ones_bd = ones_ref[...]
    hm = hm_ref[...]
    tril = tril_ref[...]
    row = lax.broadcasted_iota(jnp.int32, (c, 1), 0)
    row_in_sub = row % sub
    bd_mask = ones_bd.astype(F32)

    sig = _sigmoid(f_ref[...].astype(F32))
    logf_all = jnp.log(lb + (1.0 - lb) * sig)
    lf_sc[...] = _dot_exact_lhs(tril, logf_all)
    kk_sc[...] = (1.0 - lb) * (1.0 - sig)
    min_decay = jnp.min(_dot(halfsum_ref[...], logf_all.astype(BF16)))

    def load(ci):
        r0 = pl.multiple_of(ci * c, c)
        q = q_ref[pl.ds(r0, c), :].astype(F32)
        v = i_ref[pl.ds(r0, c), :].astype(F32)
        kk = kk_sc[pl.ds(r0, c), :]
        b = lf_sc[pl.ds(r0, c), :]
        return r0, q, v, kk, b

    def finish(r0, q_decayed, v, kk, b, intra):
        st = st_ref[...]
        inter = _dot_nt(q_decayed.astype(BF16), st.astype(BF16))
        b_last = b[c - 1:c, :]
        ks_end = (kk * jnp.exp(b_last - b)).astype(BF16)
        upd = _dot(v.T.astype(BF16), ks_end)
        st_ref[...] = st * jnp.exp(b_last) + upd * bd_mask
        o_sc[pl.ds(r0, c), :] = intra + inter

    def fast_group(gi, carry):
        second = row >= half
        chunks, qxs, kaugs, vaugs = [], [], [], []
        for j in range(HG_GROUP):
            r0, q, v, kk, b = load(gi * HG_GROUP + j)
            m_row = b[half - 1:half, :]
            mref = jnp.where(second, m_row, 0.0)
            qp = q * jnp.exp(b - mref)
            kp = kk * jnp.exp(mref - b)
            e_m = jnp.exp(m_row)
            kaugs.append(jnp.concatenate([kp, kp[:half, :] * e_m], axis=0))
            vaugs.append(jnp.concatenate([v, v[:half, :]], axis=0))
            qxs.extend(qp * hm[h:h + 1, :] for h in range(N_HEADS))
            chunks.append((r0, jnp.where(second, qp * e_m, qp), v, kk, b))
        qx = jnp.concatenate(qxs, axis=0).astype(BF16)
        kaug = jnp.concatenate(kaugs, axis=0).astype(BF16)
        vaug = jnp.concatenate(vaugs, axis=0).astype(BF16)
        sc = jnp.where(fmask_ref[...] > 0.0, _dot_nt(qx, kaug), 0.0)
        r = _dot(sc.astype(BF16), vaug)
        for j, (r0, q_decayed, v, kk, b) in enumerate(chunks):
            intra = jnp.zeros((c, BRANCH_W), F32)
            for h in range(N_HEADS):
                lo = (j * N_HEADS + h) * c
                intra = intra + r[lo:lo + c, :] * hm[h:h + 1, :]
            finish(r0, q_decayed, v, kk, b, intra)
        return carry

    def exact_chunk(ci, carry):
        r0, q, v, kk, b = load(ci)

        bpad[pl.ds(sub, c), :] = b
        kpad[pl.ds(sub, c), :] = kk
        vpad[pl.ds(sub, c), :] = v

        for d in range(sub):
            b_d = bpad[pl.ds(sub - d, c), :]
            k_d = kpad[pl.ds(sub - d, c), :]
            a = jnp.where(row_in_sub >= d, q * k_d * jnp.exp(b - b_d), 0.0)
            astack[pl.ds(d * c, c), :] = a.astype(BF16)
        pall = _dot(astack[...], ones_bd)
        intra = jnp.zeros((c, BRANCH_W), F32)
        for d in range(sub):
            intra = intra + pall[d * c:(d + 1) * c, :] * vpad[pl.ds(sub - d, c), :]

        pieces = [jnp.zeros((sub, BRANCH_W), F32)]
        for si in range(1, nsub):
            lo = si * sub
            m_i = b[lo - 1:lo, :]
            qs = q[lo:lo + sub, :] * jnp.exp(b[lo:lo + sub, :] - m_i)
            ks = (kk[:lo, :] * jnp.exp(m_i - b[:lo, :])).astype(BF16)
            qx = jnp.concatenate([qs * hm[h:h + 1, :] for h in range(N_HEADS)], axis=0).astype(BF16)
            sc = _dot_nt(qx, ks)
            r = _dot(sc.astype(BF16), v[:lo, :].astype(BF16))
            acc = jnp.zeros((sub, BRANCH_W), F32)
            for h in range(N_HEADS):
                acc = acc + r[h * sub:(h + 1) * sub, :] * hm[h:h + 1, :]
            pieces.append(acc)
        intra = intra + jnp.concatenate(pieces, axis=0)
        finish(r0, q * jnp.exp(b), v, kk, b, intra)
        return carry

    lax.cond(min_decay >= HG_FAST_MIN_LOGDECAY,
             lambda: lax.fori_loop(0, tile // (c * HG_GROUP), fast_group, 0, unroll=2),
             lambda: lax.fori_loop(0, tile // c, exact_chunk, 0))

    o = o_sc[...]
    ms = _dot_exact_rhs(o * o, ones_bd) * (1.0 / HEAD_DIM)
    y = o * lax.rsqrt(ms + RMS_EPS) * ng_ref[...] * _silu(g_ref[...].astype(F32))
    o_ref[...] = y.astype(o_ref.dtype)


def hgrn2(proj, lb_logits, norm_g, layer, tile=512):
    s = proj.shape[0]
    depth = lb_logits.shape[0]
    c, sub = HG_CHUNK, HG_SUB
    half = c // 2
    tile = min(tile, s)
    col = lambda j: pl.BlockSpec((tile, BRANCH_W), lambda i, j=j: (i, j))
    pos = np.arange(tile)
    tril = _const((pos[:, None] // c == pos[None, :] // c) & (pos[None, :] <= pos[:, None]), BF16)
    nhalf = tile // half
    halfsum = _const(np.arange(tile)[None, :] // half == np.arange(nhalf)[:, None], BF16)
    t = np.arange(c)[:, None]
    col_s = np.arange(c + half)[None, :]
    same_half = (col_s < c) & (col_s // half == t // half) & (col_s <= t)
    cross = (col_s >= c) & (t >= half)
    fmask1 = np.tile((same_half | cross).astype(np.float32), (N_HEADS, 1))
    fmask = _const(np.kron(np.eye(HG_GROUP, dtype=np.float32), fmask1))
    return pl.pallas_call(
        functools.partial(_hgrn_kernel, layer=layer, tile=tile),
        grid=(s // tile,),
        in_specs=[col(0), col(1), col(2), col(3),
                  _const_spec((depth, BRANCH_W)), _const_spec((1, BRANCH_W)),
                  _const_spec((BRANCH_W, BRANCH_W)), _const_spec((N_HEADS, BRANCH_W)),
                  _const_spec((tile, tile)), _const_spec((nhalf, tile)),
                  _const_spec((HG_GROUP * N_HEADS * c, HG_GROUP * (c + half)))],
        out_specs=pl.BlockSpec((tile, BRANCH_W), lambda i: (i, 0)),
        out_shape=jax.ShapeDtypeStruct((s, BRANCH_W), BF16),
        scratch_shapes=[pltpu.VMEM((BRANCH_W, BRANCH_W), F32),
                        pltpu.VMEM((c + sub, BRANCH_W), F32),
                        pltpu.VMEM((c + sub, BRANCH_W), F32),
                        pltpu.VMEM((c + sub, BRANCH_W), F32),
                        pltpu.VMEM((sub * c, BRANCH_W), BF16),
                        pltpu.VMEM((tile, BRANCH_W), F32),
                        pltpu.VMEM((tile, BRANCH_W), F32),
                        pltpu.VMEM((tile, BRANCH_W), F32)],
        compiler_params=_params("arbitrary"),
        name="hgrn2",
    )(proj, proj, proj, proj, lb_logits.astype(F32), norm_g.reshape(1, BRANCH_W).astype(F32),
      _head_ones(), _head_masks(), tril, halfsum, fmask)


FOX_TILE = 256
FOX_GATE_BLOCKS = 2
FOX_NSTAT = 16
FOX_FIXED_MAX = 30.0
FOX_SKIP_LOG = 40.0


def _fox_gate_kernel(h_ref, q_ref, k_ref, wf_ref, bias_ref, tril_ref, ones_ref,
                     ct_ref, c_ref, stat_ref, carry_ref, kmax_ref):
    @pl.when(pl.program_id(0) == 0)
    def _():
        carry_ref[...] = jnp.zeros_like(carry_ref)
        kmax_ref[...] = jnp.zeros_like(kmax_ref)

    logit = _dot(h_ref[...], wf_ref[...]) + bias_ref[...]
    logf = jnp.minimum(logit, 0.0) - jnp.log(1.0 + jnp.exp(-jnp.abs(logit)))
    cum = _dot_exact_lhs(tril_ref[...], logf) + carry_ref[...]
    carry_ref[...] = cum[-1:, :]
    ct_ref[...] = cum.T[:SUBLANE, :]
    c_ref[...] = cum

    ones_bd = ones_ref[...]
    q = q_ref[...].astype(F32)
    k = k_ref[...].astype(F32)
    scale = HEAD_DIM ** -0.5
    head_lane = lax.broadcasted_iota(jnp.int32, (1, BRANCH_W), 1) // HEAD_DIM
    c_heads = jnp.zeros(q.shape, F32)
    for h in range(N_HEADS):
        c_heads = jnp.where(head_lane == h, cum[:, h:h + 1], c_heads)
    slack = 1.0 + 2.0 ** -6
    qn = jnp.sqrt(_dot((q * q).astype(BF16), ones_bd)) * (scale * slack)
    kn = jnp.sqrt(_dot((k * k).astype(BF16), ones_bd)) * slack
    diag = _dot((q * k).astype(BF16), ones_bd) * scale - (2.0 ** -6) * qn * kn
    e_row = c_heads - diag
    kmax = kmax_ref[...]
    for blk in range(q.shape[0] // FOX_TILE):
        rs = slice(blk * FOX_TILE, (blk + 1) * FOX_TILE)
        kmax = jnp.maximum(kmax, jnp.max(kn[rs], axis=0, keepdims=True))
        rows = [jnp.max(qn[rs], axis=0, keepdims=True),
                jnp.max(e_row[rs], axis=0, keepdims=True),
                kmax,
                c_heads[(blk + 1) * FOX_TILE - 1:(blk + 1) * FOX_TILE, :]]
        stat_ref[blk] = jnp.concatenate(rows + [jnp.zeros((SUBLANE - len(rows), BRANCH_W), F32)], axis=0)
    kmax_ref[...] = kmax


def fox_gate(h, proj, w_f, f_bias):
    s, d = h.shape
    tile = min(FOX_GATE_BLOCKS * FOX_TILE, s)
    nblk = tile // FOX_TILE
    bias = jnp.zeros((1, LANE), F32).at[0, :N_HEADS].set(f_bias.astype(F32))
    tril = _const(np.tril(np.ones((tile, tile))), BF16)
    ct, c_rows, stats = pl.pallas_call(
        _fox_gate_kernel,
        grid=(s // tile,),
        in_specs=[pl.BlockSpec((tile, d), lambda i: (i, 0)),
                  pl.BlockSpec((tile, BRANCH_W), lambda i: (i, 4)),
                  pl.BlockSpec((tile, BRANCH_W), lambda i: (i, 5)),
                  _const_spec((d, LANE)), _const_spec((1, LANE)), _const_spec((tile, tile)),
                  _const_spec((BRANCH_W, BRANCH_W))],
        out_specs=[pl.BlockSpec((SUBLANE, tile), lambda i: (0, i)),
                   pl.BlockSpec((tile, LANE), lambda i: (i, 0)),
                   pl.BlockSpec((nblk, SUBLANE, BRANCH_W), lambda i: (i, 0, 0))],
        out_shape=[jax.ShapeDtypeStruct((SUBLANE, s), F32),
                   jax.ShapeDtypeStruct((s, LANE), F32),
                   jax.ShapeDtypeStruct((s // FOX_TILE, SUBLANE, BRANCH_W), F32)],
        scratch_shapes=[pltpu.VMEM((1, LANE), F32), pltpu.VMEM((1, BRANCH_W), F32)],
        compiler_params=_params("arbitrary"),
        name="fox_gate",
    )(h, proj, proj, w_f, bias, tril, _head_ones())
    return ct, c_rows, stats[:, :4, ::HEAD_DIM].reshape(-1)


def _fox_kernel(stat_ref, q_ref, k_ref, v_ref, ct_ref, c_ref, hm_ref, o_ref, m_sc, l_sc, acc_sc, *, tq):
    i = pl.program_id(0)
    q0 = pl.multiple_of(i * tq, tq)
    hm = hm_ref[...]
    q = q_ref[...].astype(F32) * (HEAD_DIM ** -0.5)
    qh = [(q * hm[h:h + 1, :]).astype(BF16) for h in range(N_HEADS)]
    c_q0 = ct_ref[:, pl.ds(q0, tq)][:, 0:1]

    first = i
    for h in range(N_HEADS):
        qmax = stat_ref[i * FOX_NSTAT + h]
        emax = stat_ref[i * FOX_NSTAT + N_HEADS + h]

        def needed(j, h=h, qmax=qmax, emax=emax):
            jc = jnp.maximum(j, 0)
            bound = (qmax * stat_ref[jc * FOX_NSTAT + 2 * N_HEADS + h] + emax
                     - stat_ref[jc * FOX_NSTAT + 3 * N_HEADS + h])
            return (j >= 0) & (bound >= -FOX_SKIP_LOG)

        last_dropped = lax.while_loop(needed, lambda j: j - 1, i - 1)
        first = jnp.minimum(first, last_dropped + 1)

    l_sc[...] = jnp.zeros_like(l_sc)
    acc_sc[...] = jnp.zeros_like(acc_sc)

    def causal(sc):
        r = lax.broadcasted_iota(jnp.int32, (tq, tq), 0)
        cidx = lax.broadcasted_iota(jnp.int32, (tq, tq), 1)
        return jnp.where(cidx <= r, sc, -jnp.inf)

    def online_block(s0, diagonal):
        kb = k_ref[pl.ds(s0, tq), :]
        vb = v_ref[pl.ds(s0, tq), :]
        bias = c_q0 - ct_ref[:, pl.ds(s0, tq)]
        for h in range(N_HEADS):
            sc = _dot_nt(qh[h], kb) + bias[h:h + 1, :]
            if diagonal:
                sc = causal(sc)
            m_prev = m_sc[h]
            m_new = jnp.maximum(m_prev, jnp.max(sc, axis=1, keepdims=True))
            alpha = jnp.exp(m_prev - m_new)
            p = jnp.exp(sc - jnp.tile(m_new, (1, tq // LANE)))
            l_sc[h] = alpha * l_sc[h] + jnp.sum(p, axis=1, keepdims=True)
            acc_sc[h] = acc_sc[h] * jnp.tile(alpha, (1, BRANCH_W // LANE)) + _dot(p.astype(BF16), vb)
            m_sc[h] = m_new

    tops = [stat_ref[i * FOX_NSTAT + h] * stat_ref[i * FOX_NSTAT + 2 * N_HEADS + h] for h in range(N_HEADS)]
    c_tile = c_ref[...]
    shift = [c_tile[:, h:h + 1] - c_q0[h:h + 1, :] - tops[h] for h in range(N_HEADS)]

    def fixed_block(s0, diagonal):
        kb = k_ref[pl.ds(s0, tq), :]
        vb = v_ref[pl.ds(s0, tq), :]
        bias = c_q0 - ct_ref[:, pl.ds(s0, tq)]
        for h in range(N_HEADS):
            sc = _dot_nt(qh[h], kb) + bias[h:h + 1, :] + shift[h]
            if diagonal:
                sc = causal(sc)
            p = jnp.exp(sc)
            l_sc[h] += p[:, :LANE] + p[:, LANE:]
            acc_sc[h] += _dot(p.astype(BF16), vb)

    def run(block, row_sum):
        lax.fori_loop(first, i, lambda j, carry: (block(pl.multiple_of(j * tq, tq), False), carry)[1], 0)
        block(q0, True)
        out = jnp.zeros((tq, BRANCH_W), F32)
        for h in range(N_HEADS):
            out = out + acc_sc[h] * hm[h:h + 1, :] / row_sum(l_sc[h])
        o_ref[...] = out.astype(o_ref.dtype)

    def run_online():
        m_sc[...] = jnp.full_like(m_sc, -jnp.inf)
        run(online_block, lambda l: jnp.tile(l, (1, BRANCH_W // LANE)))

    def run_fixed():
        run(fixed_block, lambda l: jnp.sum(l, axis=1, keepdims=True))

    lax.cond(functools.reduce(jnp.maximum, tops) <= FOX_FIXED_MAX, run_fixed, run_online)


def fox_attention(proj, ct, c_rows, stats):
    s = proj.shape[0]
    tq = min(FOX_TILE, s)
    full = lambda j: pl.BlockSpec((s, BRANCH_W), lambda i, j=j: (0, j), pipeline_mode=pl.Buffered(1))
    return pl.pallas_call(
        functools.partial(_fox_kernel, tq=tq),
        grid=(s // tq,),
        in_specs=[pl.BlockSpec(memory_space=pltpu.SMEM),
                  pl.BlockSpec((tq, BRANCH_W), lambda i: (i, 4)), full(5), full(6),
                  _const_spec((SUBLANE, s)), pl.BlockSpec((tq, LANE), lambda i: (i, 0)),
                  _const_spec((N_HEADS, BRANCH_W))],
        out_specs=pl.BlockSpec((tq, BRANCH_W), lambda i: (i, 0)),
        out_shape=jax.ShapeDtypeStruct((s, BRANCH_W), BF16),
        scratch_shapes=[pltpu.VMEM((N_HEADS, tq, LANE), F32),
                        pltpu.VMEM((N_HEADS, tq, LANE), F32),
                        pltpu.VMEM((N_HEADS, tq, BRANCH_W), F32)],
        compiler_params=_params("parallel"),
        name="fox_attention",
    )(stats, proj, proj, proj, ct, c_rows, _head_masks())


def _pool_kernel(u_ref, w_ref, scale_ref, o_ref, ext, *, tile):
    i = pl.program_id(0)

    @pl.when(i == 0)
    def _():
        ext[pl.ds(0, POOL_HALO), :] = jnp.zeros((POOL_HALO, BRANCH_W), F32)

    u = u_ref[...].astype(F32)
    ext[pl.ds(POOL_HALO, tile), :] = u
    pos = (i * tile + lax.broadcasted_iota(jnp.int32, (tile, 1), 0) + 1).astype(F32)
    halves = []
    for half in range(BRANCH_W // LANE):
        lanes = pl.ds(half * LANE, LANE)
        w_small, w_big = POOL_WINDOWS[2 * half], POOL_WINDOWS[2 * half + 1]
        run = u[:, half * LANE:(half + 1) * LANE]
        sums = {}
        for j in range(1, w_big):
            if j == w_small:
                sums[w_small] = run
            run = run + ext[pl.ds(POOL_HALO - j, tile), lanes]
        sums[w_big] = run
        lane = lax.broadcasted_iota(jnp.int32, (1, LANE), 1)
        small = lane < POOL_GROUP
        total = jnp.where(small, sums[w_small], sums[w_big])
        count = jnp.where(small, jnp.minimum(pos, float(w_small)), jnp.minimum(pos, float(w_big)))
        halves.append(total / count)
    mean = jnp.concatenate(halves, axis=1)
    d = (mean - u).astype(BF16)
    y = _dot(d, w_ref[...]) * scale_ref[...]
    o_ref[...] = y.astype(o_ref.dtype)
    ext[pl.ds(0, POOL_HALO), :] = u[tile - POOL_HALO:, :]


def pool_mixer(proj, w_pool, scale, tile=2048):
    s = proj.shape[0]
    tile = min(tile, s)
    ng = len(POOL_WINDOWS)
    w_bd = jnp.zeros((BRANCH_W, BRANCH_W), F32)
    for gi in range(ng):
        lo = gi * POOL_GROUP
        w_bd = w_bd.at[lo:lo + POOL_GROUP, lo:lo + POOL_GROUP].set(w_pool[gi].astype(F32))
    return pl.pallas_call(
        functools.partial(_pool_kernel, tile=tile),
        grid=(s // tile,),
        in_specs=[pl.BlockSpec((tile, BRANCH_W), lambda i: (i, 7)),
                  _const_spec((BRANCH_W, BRANCH_W)), _const_spec((1, BRANCH_W))],
        out_specs=pl.BlockSpec((tile, BRANCH_W), lambda i: (i, 0)),
        out_shape=jax.ShapeDtypeStruct((s, BRANCH_W), BF16),
        scratch_shapes=[pltpu.VMEM((tile + POOL_HALO, BRANCH_W), F32)],
        compiler_params=_params("arbitrary"),
        name="pool_mixer",
    )(proj, w_bd.astype(BF16), scale.reshape(1, BRANCH_W).astype(F32))


def _ret_kernel(q_ref, k_ref, v_ref, g_ref, rope_ref, ecos_ref, esin_ref, perm_ref, ones_ref, hm_ref,
                dstack_ref, xi_ref, zeta_ref, gc_ref, gng_ref, gnb_ref, o_ref, st_ref, *, tile):
    c = RET_CHUNK

    @pl.when(pl.program_id(0) == 0)
    def _():
        st_ref[...] = jnp.zeros_like(st_ref)

    perm = perm_ref[...]
    ones_bd = ones_ref[...]
    bd_mask = ones_bd.astype(F32)
    hm = hm_ref[...]

    tab = rope_ref[...]
    cos = _dot_exact_rhs(tab, ecos_ref[...])
    sin = _dot_exact_rhs(tab, esin_ref[...])
    q_all = q_ref[...]
    k_all = k_ref[...]
    qr_all = q_all.astype(F32) * cos + _dot(q_all, perm) * sin
    kr_all = (k_all.astype(F32) * cos + _dot(k_all, perm) * sin) * (HEAD_DIM ** -0.5)

    outs = []
    for ci in range(tile // c):
        r0 = ci * c
        qr = qr_all[r0:r0 + c, :]
        kr = kr_all[r0:r0 + c, :]
        v = v_ref[pl.ds(r0, c), :]

        qx = jnp.concatenate([qr * hm[h:h + 1, :] for h in range(N_HEADS)], axis=0).astype(BF16)
        sc = _dot_nt(qx, kr.astype(BF16)) * dstack_ref[...]
        r = _dot(sc.astype(BF16), v)
        intra = jnp.zeros((c, BRANCH_W), F32)
        for h in range(N_HEADS):
            intra = intra + r[h * c:(h + 1) * c, :] * hm[h:h + 1, :]

        st = st_ref[...]
        inter = _dot_nt((qr * xi_ref[...]).astype(BF16), st.astype(BF16))
        upd = _dot(v.astype(F32).T.astype(BF16), (kr * zeta_ref[...]).astype(BF16))
        st_ref[...] = st * gc_ref[...] + upd * bd_mask
        outs.append(intra + inter)

    o = jnp.concatenate(outs, axis=0)
    mu = _dot_exact_rhs(o, ones_bd) * (1.0 / HEAD_DIM)
    cen = o - mu
    var = _dot_exact_rhs(cen * cen, ones_bd) * (1.0 / HEAD_DIM)
    y = cen * lax.rsqrt(var + LN_EPS) * gng_ref[...] + gnb_ref[...]
    o_ref[...] = (y * _silu(g_ref[...].astype(F32))).astype(o_ref.dtype)


def _rope_tables(s):
    half = HEAD_DIM // 2
    pos = np.arange(s, dtype=np.float64)
    inv_freq = ROPE_BASE ** (-np.arange(half, dtype=np.float64) / half)
    ang = pos[:, None] * inv_freq[None, :]
    table = np.concatenate([np.cos(ang), np.sin(ang), np.zeros((s, LANE - 2 * half))], axis=1)
    lane = np.arange(BRANCH_W)
    src = np.arange(LANE)[:, None]
    ecos = src == (lane % half)[None, :]
    sign = np.where(lane % HEAD_DIM < half, -1.0, 1.0)
    esin = (src == (half + lane % half)[None, :]) * sign[None, :]
    return _const(table), _const(ecos, BF16), _const(esin, BF16)


def _ret_constants():
    c = RET_CHUNK
    half = HEAD_DIM // 2
    lane = np.arange(BRANCH_W)
    partner = np.where(lane % HEAD_DIM < half, lane + half, lane - half)
    perm = lane[:, None] == partner[None, :]
    log_gamma = np.log1p(-np.exp2(-RET_DECAY_BASE - np.arange(N_HEADS, dtype=np.float64)))
    ci = np.arange(c, dtype=np.float64)
    diff = ci[:, None] - ci[None, :]
    intra = np.where(diff >= 0, np.exp(diff * log_gamma[:, None, None]), 0.0)
    dstack = intra.reshape(N_HEADS * c, c)
    lg_lane = np.repeat(log_gamma, HEAD_DIM)[None, :]
    xi = np.exp((ci[:, None] + 1.0) * lg_lane)
    zeta = np.exp((c - 1.0 - ci[:, None]) * lg_lane)
    gc = np.exp(c * lg_lane)
    return _const(perm, BF16), _const(dstack), _const(xi), _const(zeta), _const(gc)


def retention(proj, gn_g, gn_b, tile=1024):
    s = proj.shape[0]
    c = RET_CHUNK
    tile = min(tile, s)
    rope, ecos, esin = _rope_tables(s)
    perm, dstack, xi, zeta, gc = _ret_constants()
    col = lambda j: pl.BlockSpec((tile, BRANCH_W), lambda i, j=j: (i, j))
    row = pl.BlockSpec((tile, BRANCH_W), lambda i: (i, 0))
    return pl.pallas_call(
        functools.partial(_ret_kernel, tile=tile),
        grid=(s // tile,),
        in_specs=[col(8), col(9), col(10), col(11), pl.BlockSpec((tile, LANE), lambda i: (i, 0)),
                  _const_spec((LANE, BRANCH_W)), _const_spec((LANE, BRANCH_W)),
                  _const_spec((BRANCH_W, BRANCH_W)), _const_spec((BRANCH_W, BRANCH_W)),
                  _const_spec((N_HEADS, BRANCH_W)), _const_spec((N_HEADS * c, c)),
                  _const_spec((c, BRANCH_W)), _const_spec((c, BRANCH_W)), _const_spec((1, BRANCH_W)),
                  _const_spec((1, BRANCH_W)), _const_spec((1, BRANCH_W))],
        out_specs=row,
        out_shape=jax.ShapeDtypeStruct((s, BRANCH_W), BF16),
        scratch_shapes=[pltpu.VMEM((BRANCH_W, BRANCH_W), F32)],
        compiler_params=_params("arbitrary"),
        name="retention",
    )(proj, proj, proj, proj, rope, ecos, esin, perm, _head_ones(), _head_masks(), dstack, xi, zeta, gc,
      gn_g.reshape(1, BRANCH_W).astype(F32), gn_b.reshape(1, BRANCH_W).astype(F32))


def _merge_kernel(h_ref, o0_ref, o1_ref, o2_ref, o3_ref, x_ref, wg_ref, wb_ref, wo_ref, g_ref, *rest):
    n_cast = (len(rest) - 2) // 2
    cast_in, (xo_ref, ho_ref), cast_out = rest[:n_cast], rest[n_cast:n_cast + 2], rest[n_cast + 2:]
    for src, dst in zip(cast_in, cast_out):
        dst[...] = src[...].astype(dst.dtype)
    h = h_ref[...]
    merged = jnp.zeros(x_ref.shape, F32)
    for bi, o_ref in enumerate((o0_ref, o1_ref, o2_ref, o3_ref)):
        gate = _sigmoid(_dot(h, wg_ref[:, bi * D_MODEL:(bi + 1) * D_MODEL]))
        merged = merged + gate * _dot(o_ref[...], wb_ref[bi])
    x_new = x_ref[...] + _dot(merged.astype(BF16), wo_ref[...])
    xo_ref[...] = x_new
    ho_ref[...] = _rms(x_new, g_ref[...]).astype(ho_ref.dtype)


def merge(h, branches, x, w_gate, w_branch, w_out, next_gain, tm=512, cast=()):
    s, d = x.shape
    tm = min(tm, s)
    steps = s // tm
    row = lambda w: pl.BlockSpec((tm, w), lambda i: (i, 0))
    flat = [c.reshape(-1, c.shape[-1]) for c in cast]
    slab = lambda c: pl.BlockSpec((c.shape[0] // steps, c.shape[1]), lambda i: (i, 0))
    outs = pl.pallas_call(
        _merge_kernel,
        grid=(steps,),
        in_specs=[row(d), row(BRANCH_W), row(BRANCH_W), row(BRANCH_W), row(BRANCH_W), row(d),
                  _const_spec((d, N_BRANCH * d)), _const_spec((N_BRANCH, BRANCH_W, d)),
                  _const_spec((d, d)), _const_spec((1, d))] + [slab(c) for c in flat],
        out_specs=[row(d), row(d)] + [slab(c) for c in flat],
        out_shape=[jax.ShapeDtypeStruct((s, d), F32), jax.ShapeDtypeStruct((s, d), BF16)]
        + [jax.ShapeDtypeStruct(c.shape, BF16) for c in flat],
        compiler_params=_params("parallel"),
        name="merge",
    )(h, *branches, x, w_gate, w_branch, w_out, next_gain.reshape(1, d).astype(F32), *flat)
    return outs[0], outs[1], [o.reshape(c.shape) for o, c in zip(outs[2:], cast)]


def _ffn_kernel(h_ref, x_ref, wg_ref, wu_ref, wd_ref, g_ref, xo_ref, ho_ref, acc_ref):
    f = pl.program_id(1)

    @pl.when(f == 0)
    def _():
        acc_ref[...] = jnp.zeros_like(acc_ref)

    h = h_ref[...]
    a = _silu(_dot(h, wg_ref[...].astype(BF16))) * _dot(h, wu_ref[...].astype(BF16))
    acc_ref[...] += _dot(a.astype(BF16), wd_ref[...].astype(BF16))

    @pl.when(f == pl.num_programs(1) - 1)
    def _():
        x_new = x_ref[...] + acc_ref[...]
        xo_ref[...] = x_new
        ho_ref[...] = _rms(x_new, g_ref[...]).astype(ho_ref.dtype)


def ffn_dense(h, x, w_gate, w_up, w_down, next_gain, tm=1024, tf=512):
    s, d = x.shape
    tm = min(tm, s)
    dff = w_gate.shape[1]
    row = lambda: pl.BlockSpec((tm, d), lambda i, f: (i, 0))
    return pl.pallas_call(
        _ffn_kernel,
        grid=(s // tm, dff // tf),
        in_specs=[row(), row(),
                  pl.BlockSpec((d, tf), lambda i, f: (0, f)),
                  pl.BlockSpec((d, tf), lambda i, f: (0, f)),
                  pl.BlockSpec((tf, d), lambda i, f: (f, 0)),
                  _const_spec((1, d))],
        out_specs=[row(), row()],
        out_shape=[jax.ShapeDtypeStruct((s, d), F32), jax.ShapeDtypeStruct((s, d), BF16)],
        scratch_shapes=[pltpu.VMEM((tm, d), F32)],
        compiler_params=_params("parallel", "arbitrary"),
        name="ffn_dense",
    )(h, x, w_gate, w_up, w_down, next_gain.reshape(1, d).astype(F32))


MERGE_CAST_TILE = 256
MOE_TOK_TILE = 256
MOE_ROUTE_TILES = 4
MOE_ROW_BLOCK = 512
MOE_GATHER_ROWS = 128
MOE_GATHER_TILES = 4
MOE_VMEM_LIMIT = 60 * 1024 * 1024
MOE_Y_BLOCK = 128
MOE_Y_FETCH = MOE_TOK_TILE // MOE_Y_BLOCK + 1


def _dot_f32(x, w):
    xh, xm, _ = _split3(x)
    wh, wm, _ = _split3(w)
    return _dot(xh, wh) + (_dot(xh, wm) + _dot(xm, wh))


def _route_kernel(x_ref, ng_ref, router_ref, ltri_ref, pos_ref, gate_ref, post_ref, before_ref, total_ref,
                  carry_ref):
    tm = x_ref.shape[0]

    @pl.when(pl.program_id(0) == 0)
    def _():
        carry_ref[...] = jnp.zeros_like(carry_ref)

    hn = _rms(x_ref[...], ng_ref[...])
    logits = _dot_f32(hn, router_ref[...])
    lane = lax.broadcasted_iota(jnp.int32, (tm, LANE), 1)
    logits = jnp.where(lane < N_EXPERTS, logits, -jnp.inf)
    v1 = jnp.max(logits, axis=1, keepdims=True)
    i1 = jnp.min(jnp.where(logits == v1, lane, LANE), axis=1, keepdims=True)
    rest = jnp.where(lane == i1, -jnp.inf, logits)
    v2 = jnp.max(rest, axis=1, keepdims=True)
    i2 = jnp.min(jnp.where(rest == v2, lane, LANE), axis=1, keepdims=True)
    w1 = 1.0 / (1.0 + jnp.exp(v2 - v1))
    gate_ref[...] = jnp.where(lane == i1, w1, 0.0) + jnp.where(lane == i2, 1.0 - w1, 0.0)

    member = jnp.where((lane == i1) | (lane == i2), 1.0, 0.0)
    carry = carry_ref[...]
    rank = _dot(ltri_ref[...], member.astype(BF16)) + carry
    pos = jnp.where(member > 0.0, rank, -1.0)
    pos_ref[...] = pos
    post_ref[...] = pos.T[:SUBLANE, :]
    for blk in range(tm // MOE_TOK_TILE):
        before_ref[blk] = carry
        carry = carry + jnp.sum(member[blk * MOE_TOK_TILE:(blk + 1) * MOE_TOK_TILE], axis=0, keepdims=True)
    carry_ref[...] = carry
    total_ref[...] = carry


def moe_route(x, norm_gain, router):
    s, d = x.shape
    tm = min(MOE_ROUTE_TILES * MOE_TOK_TILE, s)
    nsub = tm // MOE_TOK_TILE
    nt = s // tm
    router_p = jnp.zeros((d, LANE), F32).at[:, :N_EXPERTS].set(router.astype(F32))
    ltri = _const(np.tril(np.ones((tm, tm)), -1), BF16)
    row = pl.BlockSpec((tm, LANE), lambda i: (i, 0))
    return pl.pallas_call(
        _route_kernel,
        grid=(nt,),
        in_specs=[pl.BlockSpec((tm, d), lambda i: (i, 0)), _const_spec((1, d)), _const_spec((d, LANE)),
                  _const_spec((tm, tm))],
        out_specs=[row, row, pl.BlockSpec((SUBLANE, tm), lambda i: (0, i)),
                   pl.BlockSpec((nsub, 1, LANE), lambda i: (i, 0, 0)), pl.BlockSpec((1, LANE), lambda i: (0, 0))],
        out_shape=[jax.ShapeDtypeStruct((s, LANE), F32), jax.ShapeDtypeStruct((s, LANE), F32),
                   jax.ShapeDtypeStruct((SUBLANE, s), F32), jax.ShapeDtypeStruct((nt * nsub, 1, LANE), F32),
                   jax.ShapeDtypeStruct((1, LANE), F32)],
        scratch_shapes=[pltpu.VMEM((1, LANE), F32)],
        compiler_params=_params("arbitrary"),
        name="moe_route",
    )(x, norm_gain.reshape(1, d).astype(F32), router_p, ltri)


def _moe_ffn_kernel(be_ref, r0_ref, tlo_ref, thi_ref, nv_ref, h_ref, post_ref, wg_ref, wu_ref, wd_ref,
                    y_ref, x_sc, acc_sc, *, tt, nsub):
    b = pl.program_id(0)
    f = pl.program_id(1)
    nb = pl.num_programs(0)
    valid = b < nv_ref[0]
    tmb, d = acc_sc.shape
    gr = tmb // nsub
    last_tile = h_ref.shape[0] // tt - 1

    def picked(e, want, t):
        t0 = pl.multiple_of(t * tt, tt)
        p = post_ref[pl.ds(e, 1), pl.ds(t0, tt)]
        sel = jnp.where(p == want, 1.0, 0.0).astype(BF16)
        return _dot(sel, h_ref[pl.ds(t0, tt), :])

    def gather_head(blk, sb):
        e = be_ref[blk]
        lo = tlo_ref[blk * nsub + sb]
        hi = thi_ref[blk * nsub + sb]
        want = (lax.broadcasted_iota(jnp.int32, (gr, 1), 0) + (r0_ref[blk] + sb * gr)).astype(F32)
        rows = picked(e, want, lo)
        for k in range(1, MOE_GATHER_TILES):
            rows = rows + picked(e, jnp.where(lo + k <= hi, want, -2.0), jnp.minimum(lo + k, last_tile))
        return rows.astype(BF16)

    def gather_tail(slot, blk, sb):
        e = be_ref[blk]
        want = (lax.broadcasted_iota(jnp.int32, (gr, 1), 0) + (r0_ref[blk] + sb * gr)).astype(F32)
        rows = pl.ds(pl.multiple_of(sb * gr, gr), gr)

        def more(t, carry):
            x_sc[slot, rows, :] = (x_sc[slot, rows, :].astype(F32) + picked(e, want, t)).astype(BF16)
            return carry

        lax.fori_loop(tlo_ref[blk * nsub + sb] + MOE_GATHER_TILES, thi_ref[blk * nsub + sb] + 1, more, 0)

    @pl.when((b == 0) & (f == 0))
    def _():
        for sb in range(nsub):
            x_sc[0, pl.ds(sb * gr, gr), :] = gather_head(0, sb)
            gather_tail(0, 0, sb)

    nxt = jnp.minimum(b + 1, nb - 1)
    nslot = (b + 1) % 2

    @pl.when(valid)
    def _():
        x_sc[nslot, pl.ds(pl.multiple_of(f * gr, gr), gr), :] = gather_head(nxt, f)
        xb = x_sc[b % 2]
        a = _silu(_dot(xb, wg_ref[0])) * _dot(xb, wu_ref[0])
        acc_sc[...] = jnp.where(f == 0, 0.0, acc_sc[...]) + _dot(a.astype(BF16), wd_ref[0].astype(BF16))

    @pl.when(valid & (thi_ref[nxt * nsub + f] - tlo_ref[nxt * nsub + f] >= MOE_GATHER_TILES))
    def _():
        gather_tail(nslot, nxt, f)

    @pl.when(f == nsub - 1)
    def _():
        y_ref[...] = jnp.where(valid, acc_sc[...], 0.0).astype(y_ref.dtype)


def moe_ffn(h, post, sched, w_gate, w_up, w_down):
    s, d = h.shape
    ne, _, dff = w_gate.shape
    tmb = MOE_ROW_BLOCK
    tt = min(MOE_TOK_TILE, s)
    nb = sched[0].shape[0]
    nf = tmb // MOE_GATHER_ROWS
    tf = dff // nf

    def fidx(b, f, nv):
        return jnp.where(b < nv[0], f, nf - 1)

    grid_spec = pltpu.PrefetchScalarGridSpec(
        num_scalar_prefetch=5,
        grid=(nb, nf),
        in_specs=[pl.BlockSpec((s, d), lambda b, f, *_: (0, 0), pipeline_mode=pl.Buffered(1)),
                  pl.BlockSpec((SUBLANE, s), lambda b, f, *_: (0, 0), pipeline_mode=pl.Buffered(1)),
                  pl.BlockSpec((1, d, tf), lambda b, f, be, r0, tlo, thi, nv: (be[b], 0, fidx(b, f, nv))),
                  pl.BlockSpec((1, d, tf), lambda b, f, be, r0, tlo, thi, nv: (be[b], 0, fidx(b, f, nv))),
                  pl.BlockSpec((1, tf, d), lambda b, f, be, r0, tlo, thi, nv: (be[b], fidx(b, f, nv), 0))],
        out_specs=pl.BlockSpec((tmb, d), lambda b, f, *_: (b, 0)),
        scratch_shapes=[pltpu.VMEM((2, tmb, d), BF16), pltpu.VMEM((tmb, d), F32)],
    )
    return pl.pallas_call(
        functools.partial(_moe_ffn_kernel, tt=tt, nsub=nf),
        grid_spec=grid_spec,
        out_shape=jax.ShapeDtypeStruct((nb * tmb, d), BF16),
        compiler_params=pltpu.CompilerParams(dimension_semantics=("arbitrary", "arbitrary"),
                                             vmem_limit_bytes=MOE_VMEM_LIMIT),
        name="moe_ffn",
    )(*sched, h, post, w_gate, w_up, w_down)


def _moe_combine_kernel(kb_ref, off_ref, lim_ref, x_ref, pos_ref, gate_ref, fg_ref, *rest):
    y_refs, o_ref, acc_sc = rest[:-2], rest[-2], rest[-1]
    t = pl.program_id(0)
    yb = y_refs[0].shape[0]
    pos = pos_ref[...]
    gate = gate_ref[...]
    col = lax.broadcasted_iota(jnp.int32, (1, yb), 1).astype(F32)

    def routed(e):
        pe = pos[:, e:e + 1]
        ge = gate[:, e:e + 1]
        r = jnp.where(pe >= 0.0, pe + off_ref[t * N_EXPERTS + e].astype(F32), -1.0)
        return r, ge

    def picked(e, k, r):
        sel = jnp.where(r == col + float(k * yb), 1.0, 0.0).astype(BF16)
        return _dot(sel, y_refs[MOE_Y_FETCH * e + k][...])

    col2 = lax.broadcasted_iota(jnp.int32, (1, 2 * yb), 1).astype(F32)
    acc = x_ref[...]
    for e in range(N_EXPERTS):
        r, ge = routed(e)
        pair = jnp.concatenate([y_refs[MOE_Y_FETCH * e][...], y_refs[MOE_Y_FETCH * e + 1][...]], axis=0)
        acc = acc + ge * _dot(jnp.where(r == col2, 1.0, 0.0).astype(BF16), pair)
    acc_sc[...] = acc
    for e in range(N_EXPERTS):
        for k in range(2, MOE_Y_FETCH):
            @pl.when(lim_ref[t * N_EXPERTS + e] > k * yb)
            def _(k=k, e=e):
                r, ge = routed(e)
                acc_sc[...] += ge * picked(e, k, r)
    o_ref[...] = _rms(acc_sc[...], fg_ref[...]).astype(o_ref.dtype)


def moe_combine(x, pos, gate, y, kb, off, lim, final_gain):
    s, d = x.shape
    tm = min(MOE_TOK_TILE, s)
    yb = MOE_Y_BLOCK
    last = y.shape[0] // yb - 1

    def yspec(e, k):
        def index(t, kb_r, off_r, lim_r):
            blk = jnp.minimum(kb_r[t * N_EXPERTS + e] + k, last)
            return (blk if k < 1 else jnp.where(lim_r[t * N_EXPERTS + e] > k * yb, blk, 0), 0)
        return pl.BlockSpec((yb, d), index)

    grid_spec = pltpu.PrefetchScalarGridSpec(
        num_scalar_prefetch=3,
        grid=(s // tm,),
        in_specs=[pl.BlockSpec((tm, d), lambda t, *_: (t, 0)),
                  pl.BlockSpec((tm, LANE), lambda t, *_: (t, 0)),
                  pl.BlockSpec((tm, LANE), lambda t, *_: (t, 0)),
                  pl.BlockSpec((1, d), lambda t, *_: (0, 0))]
        + [yspec(e, k) for e in range(N_EXPERTS) for k in range(MOE_Y_FETCH)],
        out_specs=pl.BlockSpec((tm, d), lambda t, *_: (t, 0)),
        scratch_shapes=[pltpu.VMEM((tm, d), F32)],
    )
    return pl.pallas_call(
        _moe_combine_kernel,
        grid_spec=grid_spec,
        out_shape=jax.ShapeDtypeStruct((s, d), F32),
        compiler_params=_params("arbitrary"),
        name="moe_combine",
    )(kb, off, lim, x, pos, gate, final_gain.reshape(1, d).astype(F32), *([y] * (MOE_Y_FETCH * N_EXPERTS)))


def _moe_schedule(before, total, s):
    tmb, yb, gr = MOE_ROW_BLOCK, MOE_Y_BLOCK, MOE_GATHER_ROWS
    nb = 2 * s // tmb + N_EXPERTS
    counts = total[0, :N_EXPERTS].astype(jnp.int32)
    nblk = (counts + tmb - 1) // tmb
    end = jnp.cumsum(nblk)
    first = end - nblk
    nvalid = end[-1]
    b = jnp.minimum(jnp.arange(nb, dtype=jnp.int32), nvalid - 1)
    blk_e = jnp.sum(b[:, None] >= end[None, :], axis=1).astype(jnp.int32)
    r0 = (b - first[blk_e]) * tmb
    cb = before[:, 0, :N_EXPERTS].astype(jnp.int32)
    r0s = (r0[:, None] + gr * jnp.arange(tmb // gr, dtype=jnp.int32)[None, :]).reshape(-1)
    cbe = cb[:, jnp.repeat(blk_e, tmb // gr)]
    tlo = (jnp.sum(cbe <= r0s[None, :], axis=0) - 1).astype(jnp.int32)
    thi = (jnp.sum(cbe < (r0s + gr)[None, :], axis=0) - 1).astype(jnp.int32)
    sched = (blk_e, r0.astype(jnp.int32), tlo, thi, nvalid.reshape(1).astype(jnp.int32))
    row_start = first[None, :] * tmb + cb
    kb = row_start // yb
    off = first[None, :] * tmb - kb * yb
    n_te = jnp.concatenate([cb[1:], counts[None, :]], axis=0) - cb
    lim = row_start - kb * yb + n_te
    flat = lambda a: a.reshape(-1).astype(jnp.int32)
    return sched, flat(kb), flat(off), flat(lim)


def moe_sparse(h, x, norm_gain, router, w_gate, w_up, w_down, final_gain):
    s, _ = x.shape
    pos, gate, post, before, total = moe_route(x, norm_gain, router)
    sched, kb, off, lim = _moe_schedule(before, total, s)
    y = moe_ffn(h, post, sched, w_gate, w_up, w_down)
    return moe_combine(x, pos, gate, y, kb, off, lim, final_gain)


FOX_F0 = 7 * BRANCH_W


def _split_w_in_kernel(wt_ref, mix_ref, f_ref, gate_ref):
    rest0 = FOX_F0 + N_HEADS
    gate0 = N_MIX_COLS + N_HEADS
    mix_ref[:, :FOX_F0] = wt_ref[pl.ds(0, FOX_F0), :].T.astype(BF16)
    mix_ref[:, FOX_F0:] = wt_ref[pl.ds(rest0, N_MIX_COLS - FOX_F0), :].T.astype(BF16)
    gate_ref[...] = wt_ref[pl.ds(gate0, N_BRANCH * D_MODEL), :].T.astype(BF16)
    f_rows = jnp.concatenate([wt_ref[pl.ds(FOX_F0, N_HEADS), :], jnp.zeros((LANE - N_HEADS, LANE), F32)], axis=0)
    f_ref[...] = f_rows.T.astype(BF16)


def _mixer_weights(w_in, layer, tr=LANE):
    _, d, cols = w_in.shape
    w_t = jnp.swapaxes(w_in, 1, 2)
    row = lambda w: pl.BlockSpec((tr, w), lambda i: (i, 0))
    return pl.pallas_call(
        _split_w_in_kernel,
        grid=(d // tr,),
        in_specs=[pl.BlockSpec((None, cols, tr), lambda i: (layer, 0, i))],
        out_specs=[row(N_MIX_COLS), row(LANE), row(N_BRANCH * D_MODEL)],
        out_shape=[jax.ShapeDtypeStruct((d, N_MIX_COLS), BF16), jax.ShapeDtypeStruct((d, LANE), BF16),
                   jax.ShapeDtypeStruct((d, N_BRANCH * D_MODEL), BF16)],
        compiler_params=_params("parallel"),
        name="split_w_in",
    )(w_t)


def kernel(x, w_in, w_branch, w_out, norm_mix_g, hgrn_lb_logits, hgrn_norm_g, fox_f_bias, pool_w, pool_scale,
           ret_gn_g, ret_gn_b, norm_ffn_g, ffn_w_gate, ffn_w_up, ffn_w_down, moe_router, moe_w_gate, moe_w_up,
           moe_w_down, final_norm_g):
    b, s, d = x.shape
    assert b == 1 and d == D_MODEL
    depth = w_in.shape[0]
    assert depth == 2, "layer 0 uses the dense FFN, layer 1 the experts and the final norm"
    xs = x.reshape(s, d)
    h = None
    out = None
    for layer in range(depth):
        w_mix, w_f, w_gate = _mixer_weights(w_in, layer)
        if layer == 0:
            proj, h = norm_matmul(xs, norm_mix_g[0], w_mix)
        else:
            proj = matmul(h, w_mix)
        ct, c_rows, fox_stats = fox_gate(h, proj, w_f, fox_f_bias[layer])
        branches = (
            hgrn2(proj, hgrn_lb_logits, hgrn_norm_g[layer], layer),
            fox_attention(proj, ct, c_rows, fox_stats),
            pool_mixer(proj, pool_w[layer], pool_scale[layer]),
            retention(proj, ret_gn_g[layer], ret_gn_b[layer]),
        )
        li = layer // 2
        if layer % 2 == 0:
            xs, h2, (ffn_wg, ffn_wu) = merge(h, branches, xs, w_gate, w_branch[layer].astype(BF16),
                                             w_out[layer].astype(BF16), norm_ffn_g[layer],
                                             cast=(ffn_w_gate[li], ffn_w_up[li]))
            xs, h = ffn_dense(h2, xs, ffn_wg, ffn_wu, ffn_w_down[li], norm_mix_g[layer + 1])
        else:
            xs, h2, (moe_wg, moe_wu) = merge(h, branches, xs, w_gate, w_branch[layer].astype(BF16),
                                             w_out[layer].astype(BF16), norm_ffn_g[layer], tm=MERGE_CAST_TILE,
                                             cast=(moe_w_gate[li], moe_w_up[li]))
            out = moe_sparse(h2, xs, norm_ffn_g[layer], moe_router[li], moe_wg, moe_wu, moe_w_down[li],
                             final_norm_g)
    return out.reshape(b, s, d)
```

```python
import functools
import math

import jax
import jax.numpy as jnp
import numpy as np
from jax import lax
from jax.experimental import pallas as pl
from jax.experimental.pallas import tpu as pltpu

D_MODEL = 1024
N_BRANCH = 4
BRANCH_W = D_MODEL // N_BRANCH
HEAD_DIM = 64
N_HEADS = BRANCH_W // HEAD_DIM
POOL_WINDOWS = (2, 4, 8, 16)
POOL_GROUP = BRANCH_W // len(POOL_WINDOWS)
POOL_HALO = 16
RET_DECAY_BASE = 5.0
ROPE_BASE = 10000.0
D_FF = 7 * D_MODEL // 2
N_EXPERTS = 8
RMS_EPS = 1e-6
LN_EPS = 1e-5
N_MIX_COLS = 12 * BRANCH_W

LANE = 128
SUBLANE = 8
VMEM_LIMIT = 56 * 1024 * 1024

HG_CHUNK = 64
HG_SUB = 16
HG_FAST_MIN_LOGDECAY = -60.0
RET_CHUNK = 256

F32 = jnp.float32
BF16 = jnp.bfloat16
NT_DIMS = (((1,), (1,)), ((), ()))


def _params(*sem):
    return pltpu.CompilerParams(dimension_semantics=sem, vmem_limit_bytes=VMEM_LIMIT)


def _const_spec(shape):
    nd = len(shape)
    return pl.BlockSpec(shape, lambda *_: (0,) * nd, pipeline_mode=pl.Buffered(1))


def _split3(x):
    hi = x.astype(BF16)
    r1 = x - hi.astype(F32)
    mid = r1.astype(BF16)
    lo = (r1 - mid.astype(F32)).astype(BF16)
    return hi, mid, lo


def _dot(a, b):
    return jnp.dot(a, b, preferred_element_type=F32)


def _dot_nt(a, b):
    return lax.dot_general(a, b, NT_DIMS, preferred_element_type=F32)


def _dot_exact_rhs(x, m_bf16, terms=3):
    return sum(_dot(part, m_bf16) for part in _split3(x)[:terms])


def _dot_exact_lhs(m_bf16, x, terms=3):
    return sum(_dot(m_bf16, part) for part in _split3(x)[:terms])


def _sigmoid(x):
    return 1.0 / (1.0 + jnp.exp(-x))


def _silu(x):
    return x * _sigmoid(x)


def _rms(x, gain):
    return x * lax.rsqrt(jnp.mean(x * x, axis=-1, keepdims=True) + RMS_EPS) * gain


def _const(a, dtype=F32):
    return jnp.asarray(np.asarray(a, np.float32), dtype)


def _head_of(n):
    return np.arange(n) // HEAD_DIM


def _head_ones():
    h = _head_of(BRANCH_W)
    return _const(h[:, None] == h[None, :], BF16)


def _head_masks():
    return _const(_head_of(BRANCH_W)[None, :] == np.arange(N_HEADS)[:, None])


def _norm_matmul_kernel(x_ref, g_ref, b_ref, o_ref, h_ref):
    @pl.when(pl.program_id(1) == 0)
    def _():
        h_ref[...] = _rms(x_ref[...], g_ref[...]).astype(h_ref.dtype)

    o_ref[...] = _dot(h_ref[...], b_ref[...]).astype(o_ref.dtype)


def norm_matmul(x, gain, b, tm=2048, tn=1024):
    m, k = x.shape
    _, n = b.shape
    tm = min(tm, m)
    return pl.pallas_call(
        _norm_matmul_kernel,
        grid=(m // tm, n // tn),
        in_specs=[pl.BlockSpec((tm, k), lambda i, j: (i, 0)), _const_spec((1, k)),
                  pl.BlockSpec((k, tn), lambda i, j: (0, j))],
        out_specs=[pl.BlockSpec((tm, tn), lambda i, j: (i, j)), pl.BlockSpec((tm, k), lambda i, j: (i, 0))],
        out_shape=[jax.ShapeDtypeStruct((m, n), BF16), jax.ShapeDtypeStruct((m, k), BF16)],
        compiler_params=_params("arbitrary", "arbitrary"),
        name="in_proj_norm",
    )(x, gain.reshape(1, k).astype(F32), b)


def _matmul_kernel(a_ref, b_ref, o_ref):
    o_ref[...] = _dot(a_ref[...], b_ref[...]).astype(o_ref.dtype)


def matmul(a, b, out_dtype=BF16, tm=2048, tn=1024):
    m, k = a.shape
    _, n = b.shape
    tm = min(tm, m)
    return pl.pallas_call(
        _matmul_kernel,
        grid=(n // tn, m // tm),
        in_specs=[pl.BlockSpec((tm, k), lambda j, i: (i, 0)),
                  pl.BlockSpec((k, tn), lambda j, i: (0, j))],
        out_specs=pl.BlockSpec((tm, tn), lambda j, i: (i, j)),
        out_shape=jax.ShapeDtypeStruct((m, n), out_dtype),
        compiler_params=_params("parallel", "parallel"),
        name="in_proj",
    )(a, b)


def _hgrn_kernel(q_ref, f_ref, i_ref, g_ref, lbl_ref, ng_ref, ones_ref, hm_ref, tril_ref, halfsum_ref, fmask_ref,
                 o_ref, st_ref, bpad, kpad, vpad, astack, lf_sc, kk_sc, o_sc, *, layer, tile):
    c, sub = HG_CHUNK, HG_SUB
    nsub = c // sub
    half = c // 2

    @pl.when(pl.program_id(0) == 0)
    def _():
        st_ref[...] = jnp.zeros_like(st_ref)
        bpad[...] = jnp.zeros_like(bpad)
        kpad[...] = jnp.zeros_like(kpad)
        vpad[...] = jnp.zeros_like(vpad)

    lbl = lbl_ref[...]
    e = jnp.exp(lbl - jnp.max(lbl, axis=0, keepdims=True))
    p = e / jnp.sum(e, axis=0, keepdims=True)
    lb = jnp.zeros((1, BRANCH_W), F32)
    for l in range(1, layer + 1):
        lb = lb + p[l:l + 1, :]

    ones_bd = ones_ref[...]
    hm = hm_ref[...]
    tril = tril_ref[...]
    row = lax.broadcasted_iota(jnp.int32, (c, 1), 0)
    row_in_sub = row % sub
    bd_mask = ones_bd.astype(F32)

    sig = _sigmoid(f_ref[...].astype(F32))
    logf_all = jnp.log(lb + (1.0 - lb) * sig)
    lf_sc[...] = _dot_exact_lhs(tril, logf_all, terms=2)
    kk_sc[...] = (1.0 - lb) * (1.0 - sig)
    min_decay = jnp.min(_dot(halfsum_ref[...], logf_all.astype(BF16)))

    def load(ci):
        r0 = pl.multiple_of(ci * c, c)
        q = q_ref[pl.ds(r0, c), :].astype(F32)
        v = i_ref[pl.ds(r0, c), :].astype(F32)
        kk = kk_sc[pl.ds(r0, c), :]
        b = lf_sc[pl.ds(r0, c), :]
        return r0, q, v, kk, b

    def finish(r0, q_decayed, v, kk, b, intra):
        st = st_ref[...]
        inter = _dot_nt(q_decayed.astype(BF16), st.astype(BF16))
        b_last = b[c - 1:c, :]
        ks_end = (kk * jnp.exp(b_last - b)).astype(BF16)
        upd = _dot(v.T.astype(BF16), ks_end)
        st_ref[...] = st * jnp.exp(b_last) + upd * bd_mask
        o_sc[pl.ds(r0, c), :] = intra + inter

    def fast_chunk(ci, carry):
        r0, q, v, kk, b = load(ci)
        second = row >= half
        m_row = b[half - 1:half, :]
        mref = jnp.where(second, m_row, 0.0)
        qp = q * jnp.exp(b - mref)
        kp = kk * jnp.exp(mref - b)
        e_m = jnp.exp(m_row)
        kaug = jnp.concatenate([kp, kp[:half, :] * e_m], axis=0)
        vaug = jnp.concatenate([v, v[:half, :]], axis=0)
        k_heads = jnp.concatenate([kaug * hm[h:h + 1, :] for h in range(N_HEADS)], axis=0).astype(BF16)
        v_heads = jnp.concatenate([vaug * hm[h:h + 1, :] for h in range(N_HEADS)], axis=0).astype(BF16)
        sc = jnp.where(fmask_ref[...] > 0.0, _dot_nt(qp.astype(BF16), k_heads), 0.0)
        intra = _dot(sc.astype(BF16), v_heads)
        finish(r0, jnp.where(second, qp * e_m, qp), v, kk, b, intra)
        return carry

    def exact_chunk(ci, carry):
        r0, q, v, kk, b = load(ci)

        bpad[pl.ds(sub, c), :] = b
        kpad[pl.ds(sub, c), :] = kk
        vpad[pl.ds(sub, c), :] = v

        for d in range(sub):
            b_d = bpad[pl.ds(sub - d, c), :]
            k_d = kpad[pl.ds(sub - d, c), :]
            a = jnp.where(row_in_sub >= d, q * k_d * jnp.exp(b - b_d), 0.0)
            astack[pl.ds(d * c, c), :] = a.astype(BF16)
        pall = _dot(astack[...], ones_bd)
        intra = jnp.zeros((c, BRANCH_W), F32)
        for d in range(sub):
            intra = intra + pall[d * c:(d + 1) * c, :] * vpad[pl.ds(sub - d, c), :]

        pieces = [jnp.zeros((sub, BRANCH_W), F32)]
        for si in range(1, nsub):
            lo = si * sub
            m_i = b[lo - 1:lo, :]
            qs = q[lo:lo + sub, :] * jnp.exp(b[lo:lo + sub, :] - m_i)
            ks = (kk[:lo, :] * jnp.exp(m_i - b[:lo, :])).astype(BF16)
            qx = jnp.concatenate([qs * hm[h:h + 1, :] for h in range(N_HEADS)], axis=0).astype(BF16)
            sc = _dot_nt(qx, ks)
            r = _dot(sc.astype(BF16), v[:lo, :].astype(BF16))
            acc = jnp.zeros((sub, BRANCH_W), F32)
            for h in range(N_HEADS):
                acc = acc + r[h * sub:(h + 1) * sub, :] * hm[h:h + 1, :]
            pieces.append(acc)
        intra = intra + jnp.concatenate(pieces, axis=0)
        finish(r0, q * jnp.exp(b), v, kk, b, intra)
        return carry

    lax.cond(min_decay >= HG_FAST_MIN_LOGDECAY,
             lambda: lax.fori_loop(0, tile // c, fast_chunk, 0, unroll=2),
             lambda: lax.fori_loop(0, tile // c, exact_chunk, 0))

    o = o_sc[...]
    ms = _dot_exact_rhs(o * o, ones_bd, terms=2) * (1.0 / HEAD_DIM)
    y = o * lax.rsqrt(ms + RMS_EPS) * ng_ref[...] * _silu(g_ref[...].astype(F32))
    o_ref[...] = y.astype(o_ref.dtype)


def hgrn2(proj, lb_logits, norm_g, layer, tile=512):
    s = proj.shape[0]
    depth = lb_logits.shape[0]
    c, sub = HG_CHUNK, HG_SUB
    half = c // 2
    tile = min(tile, s)
    col = lambda j: pl.BlockSpec((tile, BRANCH_W), lambda i, j=j: (i, j))
    pos = np.arange(tile)
    tril = _const((pos[:, None] // c == pos[None, :] // c) & (pos[None, :] <= pos[:, None]), BF16)
    nhalf = tile // half
    halfsum = _const(np.arange(tile)[None, :] // half == np.arange(nhalf)[:, None], BF16)
    t = np.arange(c)[:, None]
    col_s = np.arange(c + half)[None, :]
    same_half = (col_s < c) & (col_s // half == t // half) & (col_s <= t)
    cross = (col_s >= c) & (t >= half)
    fmask = _const(np.tile(same_half | cross, (1, N_HEADS)))
    return pl.pallas_call(
        functools.partial(_hgrn_kernel, layer=layer, tile=tile),
        grid=(s // tile,),
        in_specs=[col(0), col(1), col(2), col(3),
                  _const_spec((depth, BRANCH_W)), _const_spec((1, BRANCH_W)),
                  _const_spec((BRANCH_W, BRANCH_W)), _const_spec((N_HEADS, BRANCH_W)),
                  _const_spec((tile, tile)), _const_spec((nhalf, tile)),
                  _const_spec((c, N_HEADS * (c + half)))],
        out_specs=pl.BlockSpec((tile, BRANCH_W), lambda i: (i, 0)),
        out_shape=jax.ShapeDtypeStruct((s, BRANCH_W), BF16),
        scratch_shapes=[pltpu.VMEM((BRANCH_W, BRANCH_W), F32),
                        pltpu.VMEM((c + sub, BRANCH_W), F32),
                        pltpu.VMEM((c + sub, BRANCH_W), F32),
                        pltpu.VMEM((c + sub, BRANCH_W), F32),
                        pltpu.VMEM((sub * c, BRANCH_W), BF16),
                        pltpu.VMEM((tile, BRANCH_W), F32),
                        pltpu.VMEM((tile, BRANCH_W), F32),
                        pltpu.VMEM((tile, BRANCH_W), F32)],
        compiler_params=_params("arbitrary"),
        name="hgrn2",
    )(proj, proj, proj, proj, lb_logits.astype(F32), norm_g.reshape(1, BRANCH_W).astype(F32),
      _head_ones(), _head_masks(), tril, halfsum, fmask)


FOX_TILE = 256
FOX_GATE_BLOCKS = 2
FOX_NSTAT = 16
FOX_FIXED_MAX = 30.0
FOX_SKIP_LOG = 40.0


def _fox_gate_kernel(h_ref, q_ref, k_ref, wf_ref, bias_ref, tril_ref, ones_ref,
                     ct_ref, c_ref, stat_ref, carry_ref, kmax_ref):
    @pl.when(pl.program_id(0) == 0)
    def _():
        carry_ref[...] = jnp.zeros_like(carry_ref)
        kmax_ref[...] = jnp.zeros_like(kmax_ref)

    logit = _dot(h_ref[...], wf_ref[...]) + bias_ref[...]
    logf = jnp.minimum(logit, 0.0) - jnp.log(1.0 + jnp.exp(-jnp.abs(logit)))
    cum = _dot_exact_lhs(tril_ref[...], logf) + carry_ref[...]
    carry_ref[...] = cum[-1:, :]
    ct_ref[...] = cum.T[:SUBLANE, :]
    c_ref[...] = cum

    ones_bd = ones_ref[...]
    q = q_ref[...].astype(F32)
    k = k_ref[...].astype(F32)
    scale = HEAD_DIM ** -0.5
    head_lane = lax.broadcasted_iota(jnp.int32, (1, BRANCH_W), 1) // HEAD_DIM
    c_heads = jnp.zeros(q.shape, F32)
    for h in range(N_HEADS):
        c_heads = jnp.where(head_lane == h, cum[:, h:h + 1], c_heads)
    slack = 1.0 + 2.0 ** -6
    qn = jnp.sqrt(_dot((q * q).astype(BF16), ones_bd)) * (scale * slack)
    kn = jnp.sqrt(_dot((k * k).astype(BF16), ones_bd)) * slack
    diag = _dot((q * k).astype(BF16), ones_bd) * scale - (2.0 ** -6) * qn * kn
    e_row = c_heads - diag
    kmax = kmax_ref[...]
    for blk in range(q.shape[0] // FOX_TILE):
        rs = slice(blk * FOX_TILE, (blk + 1) * FOX_TILE)
        kmax = jnp.maximum(kmax, jnp.max(kn[rs], axis=0, keepdims=True))
        rows = [jnp.max(qn[rs], axis=0, keepdims=True),
                jnp.max(e_row[rs], axis=0, keepdims=True),
                kmax,
                c_heads[(blk + 1) * FOX_TILE - 1:(blk + 1) * FOX_TILE, :]]
        stat_ref[blk] = jnp.concatenate(rows + [jnp.zeros((SUBLANE - len(rows), BRANCH_W), F32)], axis=0)
    kmax_ref[...] = kmax


def fox_gate(h, proj, w_f, f_bias):
    s, d = h.shape
    tile = min(FOX_GATE_BLOCKS * FOX_TILE, s)
    nblk = tile // FOX_TILE
    bias = jnp.zeros((1, LANE), F32).at[0, :N_HEADS].set(f_bias.astype(F32))
    tril = _const(np.tril(np.ones((tile, tile))), BF16)
    ct, c_rows, stats = pl.pallas_call(
        _fox_gate_kernel,
        grid=(s // tile,),
        in_specs=[pl.BlockSpec((tile, d), lambda i: (i, 0)),
                  pl.BlockSpec((tile, BRANCH_W), lambda i: (i, 4)),
                  pl.BlockSpec((tile, BRANCH_W), lambda i: (i, 5)),
                  _const_spec((d, LANE)), _const_spec((1, LANE)), _const_spec((tile, tile)),
                  _const_spec((BRANCH_W, BRANCH_W))],
        out_specs=[pl.BlockSpec((SUBLANE, tile), lambda i: (0, i)),
                   pl.BlockSpec((tile, LANE), lambda i: (i, 0)),
                   pl.BlockSpec((nblk, SUBLANE, BRANCH_W), lambda i: (i, 0, 0))],
        out_shape=[jax.ShapeDtypeStruct((SUBLANE, s), F32),
                   jax.ShapeDtypeStruct((s, LANE), F32),
                   jax.ShapeDtypeStruct((s // FOX_TILE, SUBLANE, BRANCH_W), F32)],
        scratch_shapes=[pltpu.VMEM((1, LANE), F32), pltpu.VMEM((1, BRANCH_W), F32)],
        compiler_params=_params("arbitrary"),
        name="fox_gate",
    )(h, proj, proj, w_f, bias, tril, _head_ones())
    return ct, c_rows, stats[:, :4, ::HEAD_DIM].reshape(-1)


def _fox_kernel(stat_ref, q_ref, k_ref, v_ref, ct_ref, c_ref, hm_ref, o_ref, m_sc, l_sc, acc_sc, *, tq):
    i = pl.program_id(0)
    q0 = pl.multiple_of(i * tq, tq)
    hm = hm_ref[...]
    q = q_ref[...].astype(F32) * (HEAD_DIM ** -0.5)
    qh = [(q * hm[h:h + 1, :]).astype(BF16) for h in range(N_HEADS)]
    c_q0 = ct_ref[:, pl.ds(q0, tq)][:, 0:1]

    first = i
    for h in range(N_HEADS):
        qmax = stat_ref[i * FOX_NSTAT + h]
        emax = stat_ref[i * FOX_NSTAT + N_HEADS + h]

        def needed(j, h=h, qmax=qmax, emax=emax):
            jc = jnp.maximum(j, 0)
            bound = (qmax * stat_ref[jc * FOX_NSTAT + 2 * N_HEADS + h] + emax
                     - stat_ref[jc * FOX_NSTAT + 3 * N_HEADS + h])
            return (j >= 0) & (bound >= -FOX_SKIP_LOG)

        last_dropped = lax.while_loop(needed, lambda j: j - 1, i - 1)
        first = jnp.minimum(first, last_dropped + 1)

    l_sc[...] = jnp.zeros_like(l_sc)
    acc_sc[...] = jnp.zeros_like(acc_sc)

    def causal(sc):
        r = lax.broadcasted_iota(jnp.int32, (tq, tq), 0)
        cidx = lax.broadcasted_iota(jnp.int32, (tq, tq), 1)
        return jnp.where(cidx <= r, sc, -jnp.inf)

    def online_block(s0, diagonal):
        kb = k_ref[pl.ds(s0, tq), :]
        vb = v_ref[pl.ds(s0, tq), :]
        bias = c_q0 - ct_ref[:, pl.ds(s0, tq)]
        for h in range(N_HEADS):
            sc = _dot_nt(qh[h], kb) + bias[h:h + 1, :]
            if diagonal:
                sc = causal(sc)
            m_prev = m_sc[h]
            m_new = jnp.maximum(m_prev, jnp.max(sc, axis=1, keepdims=True))
            alpha = jnp.exp(m_prev - m_new)
            p = jnp.exp(sc - jnp.tile(m_new, (1, tq // LANE)))
            l_sc[h] = alpha * l_sc[h] + jnp.sum(p, axis=1, keepdims=True)
            acc_sc[h] = acc_sc[h] * jnp.tile(alpha, (1, BRANCH_W // LANE)) + _dot(p.astype(BF16), vb)
            m_sc[h] = m_new

    tops = [stat_ref[i * FOX_NSTAT + h] * stat_ref[i * FOX_NSTAT + 2 * N_HEADS + h] for h in range(N_HEADS)]
    c_tile = c_ref[...]
    shift = [c_tile[:, h:h + 1] - c_q0[h:h + 1, :] - tops[h] for h in range(N_HEADS)]

    def fixed_block(s0, diagonal):
        kb = k_ref[pl.ds(s0, tq), :]
        vb = v_ref[pl.ds(s0, tq), :]
        bias = c_q0 - ct_ref[:, pl.ds(s0, tq)]
        for h in range(N_HEADS):
            sc = _dot_nt(qh[h], kb) + bias[h:h + 1, :] + shift[h]
            if diagonal:
                sc = causal(sc)
            p = jnp.exp(sc)
            l_sc[h] += p[:, :LANE] + p[:, LANE:]
            acc_sc[h] += _dot(p.astype(BF16), vb)

    def run(block, row_sum):
        lax.fori_loop(first, i, lambda j, carry: (block(pl.multiple_of(j * tq, tq), False), carry)[1], 0)
        block(q0, True)
        out = jnp.zeros((tq, BRANCH_W), F32)
        for h in range(N_HEADS):
            out = out + acc_sc[h] * hm[h:h + 1, :] / row_sum(l_sc[h])
        o_ref[...] = out.astype(o_ref.dtype)

    def run_online():
        m_sc[...] = jnp.full_like(m_sc, -jnp.inf)
        run(online_block, lambda l: jnp.tile(l, (1, BRANCH_W // LANE)))

    def run_fixed():
        run(fixed_block, lambda l: jnp.sum(l, axis=1, keepdims=True))

    lax.cond(functools.reduce(jnp.maximum, tops) <= FOX_FIXED_MAX, run_fixed, run_online)


def fox_attention(proj, ct, c_rows, stats):
    s = proj.shape[0]
    tq = min(FOX_TILE, s)
    full = lambda j: pl.BlockSpec((s, BRANCH_W), lambda i, j=j: (0, j), pipeline_mode=pl.Buffered(1))
    return pl.pallas_call(
        functools.partial(_fox_kernel, tq=tq),
        grid=(s // tq,),
        in_specs=[pl.BlockSpec(memory_space=pltpu.SMEM),
                  pl.BlockSpec((tq, BRANCH_W), lambda i: (i, 4)), full(5), full(6),
                  _const_spec((SUBLANE, s)), pl.BlockSpec((tq, LANE), lambda i: (i, 0)),
                  _const_spec((N_HEADS, BRANCH_W))],
        out_specs=pl.BlockSpec((tq, BRANCH_W), lambda i: (i, 0)),
        out_shape=jax.ShapeDtypeStruct((s, BRANCH_W), BF16),
        scratch_shapes=[pltpu.VMEM((N_HEADS, tq, LANE), F32),
                        pltpu.VMEM((N_HEADS, tq, LANE), F32),
                        pltpu.VMEM((N_HEADS, tq, BRANCH_W), F32)],
        compiler_params=_params("parallel"),
        name="fox_attention",
    )(stats, proj, proj, proj, ct, c_rows, _head_masks())


def _pool_kernel(u_ref, w_ref, scale_ref, o_ref, ext, *, tile):
    i = pl.program_id(0)

    @pl.when(i == 0)
    def _():
        ext[pl.ds(0, POOL_HALO), :] = jnp.zeros((POOL_HALO, BRANCH_W), F32)

    u = u_ref[...].astype(F32)
    ext[pl.ds(POOL_HALO, tile), :] = u
    pos = (i * tile + lax.broadcasted_iota(jnp.int32, (tile, 1), 0) + 1).astype(F32)
    halves = []
    for half in range(BRANCH_W // LANE):
        lanes = pl.ds(half * LANE, LANE)
        w_small, w_big = POOL_WINDOWS[2 * half], POOL_WINDOWS[2 * half + 1]
        run = u[:, half * LANE:(half + 1) * LANE]
        sums = {}
        for j in range(1, w_big):
            if j == w_small:
                sums[w_small] = run
            run = run + ext[pl.ds(POOL_HALO - j, tile), lanes]
        sums[w_big] = run
        lane = lax.broadcasted_iota(jnp.int32, (1, LANE), 1)
        small = lane < POOL_GROUP
        total = jnp.where(small, sums[w_small], sums[w_big])
        count = jnp.where(small, jnp.minimum(pos, float(w_small)), jnp.minimum(pos, float(w_big)))
        halves.append(total / count)
    mean = jnp.concatenate(halves, axis=1)
    d = (mean - u).astype(BF16)
    y = _dot(d, w_ref[...]) * scale_ref[...]
    o_ref[...] = y.astype(o_ref.dtype)
    ext[pl.ds(0, POOL_HALO), :] = u[tile - POOL_HALO:, :]


def pool_mixer(proj, w_pool, scale, tile=2048):
    s = proj.shape[0]
    tile = min(tile, s)
    ng = len(POOL_WINDOWS)
    w_bd = jnp.zeros((BRANCH_W, BRANCH_W), F32)
    for gi in range(ng):
        lo = gi * POOL_GROUP
        w_bd = w_bd.at[lo:lo + POOL_GROUP, lo:lo + POOL_GROUP].set(w_pool[gi].astype(F32))
    return pl.pallas_call(
        functools.partial(_pool_kernel, tile=tile),
        grid=(s // tile,),
        in_specs=[pl.BlockSpec((tile, BRANCH_W), lambda i: (i, 7)),
                  _const_spec((BRANCH_W, BRANCH_W)), _const_spec((1, BRANCH_W))],
        out_specs=pl.BlockSpec((tile, BRANCH_W), lambda i: (i, 0)),
        out_shape=jax.ShapeDtypeStruct((s, BRANCH_W), BF16),
        scratch_shapes=[pltpu.VMEM((tile + POOL_HALO, BRANCH_W), F32)],
        compiler_params=_params("arbitrary"),
        name="pool_mixer",
    )(proj, w_bd.astype(BF16), scale.reshape(1, BRANCH_W).astype(F32))


def _ret_kernel(q_ref, k_ref, v_ref, g_ref, rope_ref, ecos_ref, esin_ref, perm_ref, ones_ref, hm_ref,
                dstack_ref, xi_ref, zeta_ref, gc_ref, gng_ref, gnb_ref, o_ref, st_ref, *, tile):
    c = RET_CHUNK

    @pl.when(pl.program_id(0) == 0)
    def _():
        st_ref[...] = jnp.zeros_like(st_ref)

    perm = perm_ref[...]
    ones_bd = ones_ref[...]
    bd_mask = ones_bd.astype(F32)
    hm = hm_ref[...]

    tab = rope_ref[...]
    cos = _dot_exact_rhs(tab, ecos_ref[...], terms=2)
    sin = _dot_exact_rhs(tab, esin_ref[...], terms=2)
    q_all = q_ref[...]
    k_all = k_ref[...]
    qr_all = q_all.astype(F32) * cos + _dot(q_all, perm) * sin
    kr_all = (k_all.astype(F32) * cos + _dot(k_all, perm) * sin) * (HEAD_DIM ** -0.5)

    outs = []
    for ci in range(tile // c):
        r0 = ci * c
        qr = qr_all[r0:r0 + c, :]
        kr = kr_all[r0:r0 + c, :]
        v = v_ref[pl.ds(r0, c), :]

        qx = jnp.concatenate([qr * hm[h:h + 1, :] for h in range(N_HEADS)], axis=0).astype(BF16)
        sc = _dot_nt(qx, kr.astype(BF16)) * dstack_ref[...]
        r = _dot(sc.astype(BF16), v)
        intra = jnp.zeros((c, BRANCH_W), F32)
        for h in range(N_HEADS):
            intra = intra + r[h * c:(h + 1) * c, :] * hm[h:h + 1, :]

        st = st_ref[...]
        inter = _dot_nt((qr * xi_ref[...]).astype(BF16), st.astype(BF16))
        upd = _dot(v.astype(F32).T.astype(BF16), (kr * zeta_ref[...]).astype(BF16))
        st_ref[...] = st * gc_ref[...] + upd * bd_mask
        outs.append(intra + inter)

    o = jnp.concatenate(outs, axis=0)
    mu = _dot_exact_rhs(o, ones_bd, terms=2) * (1.0 / HEAD_DIM)
    cen = o - mu
    var = _dot_exact_rhs(cen * cen, ones_bd, terms=2) * (1.0 / HEAD_DIM)
    y = cen * lax.rsqrt(var + LN_EPS) * gng_ref[...] + gnb_ref[...]
    o_ref[...] = (y * _silu(g_ref[...].astype(F32))).astype(o_ref.dtype)


def _rope_tables(s):
    half = HEAD_DIM // 2
    pos = np.arange(s, dtype=np.float64)
    inv_freq = ROPE_BASE ** (-np.arange(half, dtype=np.float64) / half)
    ang = pos[:, None] * inv_freq[None, :]
    table = np.concatenate([np.cos(ang), np.sin(ang), np.zeros((s, LANE - 2 * half))], axis=1)
    lane = np.arange(BRANCH_W)
    src = np.arange(LANE)[:, None]
    ecos = src == (lane % half)[None, :]
    sign = np.where(lane % HEAD_DIM < half, -1.0, 1.0)
    esin = (src == (half + lane % half)[None, :]) * sign[None, :]
    return _const(table), _const(ecos, BF16), _const(esin, BF16)


def _ret_constants():
    c = RET_CHUNK
    half = HEAD_DIM // 2
    lane = np.arange(BRANCH_W)
    partner = np.where(lane % HEAD_DIM < half, lane + half, lane - half)
    perm = lane[:, None] == partner[None, :]
    log_gamma = np.log1p(-np.exp2(-RET_DECAY_BASE - np.arange(N_HEADS, dtype=np.float64)))
    ci = np.arange(c, dtype=np.float64)
    diff = ci[:, None] - ci[None, :]
    intra = np.where(diff >= 0, np.exp(diff * log_gamma[:, None, None]), 0.0)
    dstack = intra.reshape(N_HEADS * c, c)
    lg_lane = np.repeat(log_gamma, HEAD_DIM)[None, :]
    xi = np.exp((ci[:, None] + 1.0) * lg_lane)
    zeta = np.exp((c - 1.0 - ci[:, None]) * lg_lane)
    gc = np.exp(c * lg_lane)
    return _const(perm, BF16), _const(dstack), _const(xi), _const(zeta), _const(gc)


def retention(proj, gn_g, gn_b, tile=1024):
    s = proj.shape[0]
    c = RET_CHUNK
    tile = min(tile, s)
    rope, ecos, esin = _rope_tables(s)
    perm, dstack, xi, zeta, gc = _ret_constants()
    col = lambda j: pl.BlockSpec((tile, BRANCH_W), lambda i, j=j: (i, j))
    row = pl.BlockSpec((tile, BRANCH_W), lambda i: (i, 0))
    return pl.pallas_call(
        functools.partial(_ret_kernel, tile=tile),
        grid=(s // tile,),
        in_specs=[col(8), col(9), col(10), col(11), pl.BlockSpec((tile, LANE), lambda i: (i, 0)),
                  _const_spec((LANE, BRANCH_W)), _const_spec((LANE, BRANCH_W)),
                  _const_spec((BRANCH_W, BRANCH_W)), _const_spec((BRANCH_W, BRANCH_W)),
                  _const_spec((N_HEADS, BRANCH_W)), _const_spec((N_HEADS * c, c)),
                  _const_spec((c, BRANCH_W)), _const_spec((c, BRANCH_W)), _const_spec((1, BRANCH_W)),
                  _const_spec((1, BRANCH_W)), _const_spec((1, BRANCH_W))],
        out_specs=row,
        out_shape=jax.ShapeDtypeStruct((s, BRANCH_W), BF16),
        scratch_shapes=[pltpu.VMEM((BRANCH_W, BRANCH_W), F32)],
        compiler_params=_params("arbitrary"),
        name="retention",
    )(proj, proj, proj, proj, rope, ecos, esin, perm, _head_ones(), _head_masks(), dstack, xi, zeta, gc,
      gn_g.reshape(1, BRANCH_W).astype(F32), gn_b.reshape(1, BRANCH_W).astype(F32))


def _merge_kernel(h_ref, o0_ref, o1_ref, o2_ref, o3_ref, x_ref, wg_ref, wb_ref, wo_ref, g_ref, *rest):
    n_cast = (len(rest) - 2) // 2
    cast_in, (xo_ref, ho_ref), cast_out = rest[:n_cast], rest[n_cast:n_cast + 2], rest[n_cast + 2:]
    for src, dst in zip(cast_in, cast_out):
        dst[...] = src[...].astype(dst.dtype)
    h = h_ref[...]
    merged = jnp.zeros(x_ref.shape, F32)
    for bi, o_ref in enumerate((o0_ref, o1_ref, o2_ref, o3_ref)):
        gate = _sigmoid(_dot(h, wg_ref[:, bi * D_MODEL:(bi + 1) * D_MODEL]))
        merged = merged + gate * _dot(o_ref[...], wb_ref[bi])
    x_new = x_ref[...] + _dot(merged.astype(BF16), wo_ref[...])
    xo_ref[...] = x_new
    ho_ref[...] = _rms(x_new, g_ref[...]).astype(ho_ref.dtype)


def merge(h, branches, x, w_gate, w_branch, w_out, next_gain, tm=512, cast=()):
    s, d = x.shape
    tm = min(tm, s)
    steps = s // tm
    row = lambda w: pl.BlockSpec((tm, w), lambda i: (i, 0))
    flat = [c.reshape(-1, c.shape[-1]) for c in cast]
    slab = lambda c: pl.BlockSpec((c.shape[0] // steps, c.shape[1]), lambda i: (i, 0))
    outs = pl.pallas_call(
        _merge_kernel,
        grid=(steps,),
        in_specs=[row(d), row(BRANCH_W), row(BRANCH_W), row(BRANCH_W), row(BRANCH_W), row(d),
                  _const_spec((d, N_BRANCH * d)), _const_spec((N_BRANCH, BRANCH_W, d)),
                  _const_spec((d, d)), _const_spec((1, d))] + [slab(c) for c in flat],
        out_specs=[row(d), row(d)] + [slab(c) for c in flat],
        out_shape=[jax.ShapeDtypeStruct((s, d), F32), jax.ShapeDtypeStruct((s, d), BF16)]
        + [jax.ShapeDtypeStruct(c.shape, BF16) for c in flat],
        compiler_params=_params("parallel"),
        name="merge",
    )(h, *branches, x, w_gate, w_branch, w_out, next_gain.reshape(1, d).astype(F32), *flat)
    return outs[0], outs[1], [o.reshape(c.shape) for o, c in zip(outs[2:], cast)]


def _ffn_kernel(h_ref, x_ref, wg_ref, wu_ref, wd_ref, g_ref, xo_ref, ho_ref, acc_ref):
    f = pl.program_id(1)

    @pl.when(f == 0)
    def _():
        acc_ref[...] = jnp.zeros_like(acc_ref)

    h = h_ref[...]
    a = _silu(_dot(h, wg_ref[...].astype(BF16))) * _dot(h, wu_ref[...].astype(BF16))
    acc_ref[...] += _dot(a.astype(BF16), wd_ref[...].astype(BF16))

    @pl.when(f == pl.num_programs(1) - 1)
    def _():
        x_new = x_ref[...] + acc_ref[...]
        xo_ref[...] = x_new
        ho_ref[...] = _rms(x_new, g_ref[...]).astype(ho_ref.dtype)


def ffn_dense(h, x, w_gate, w_up, w_down, next_gain, tm=1024, tf=512):
    s, d = x.shape
    tm = min(tm, s)
    dff = w_gate.shape[1]
    row = lambda: pl.BlockSpec((tm, d), lambda i, f: (i, 0))
    return pl.pallas_call(
        _ffn_kernel,
        grid=(s // tm, dff // tf),
        in_specs=[row(), row(),
                  pl.BlockSpec((d, tf), lambda i, f: (0, f)),
                  pl.BlockSpec((d, tf), lambda i, f: (0, f)),
                  pl.BlockSpec((tf, d), lambda i, f: (f, 0)),
                  _const_spec((1, d))],
        out_specs=[row(), row()],
        out_shape=[jax.ShapeDtypeStruct((s, d), F32), jax.ShapeDtypeStruct((s, d), BF16)],
        scratch_shapes=[pltpu.VMEM((tm, d), F32)],
        compiler_params=_params("parallel", "arbitrary"),
        name="ffn_dense",
    )(h, x, w_gate, w_up, w_down, next_gain.reshape(1, d).astype(F32))


MERGE_CAST_TILE = 256
MOE_TOK_TILE = 256
MOE_ROUTE_TILES = 4
MOE_ROW_BLOCK = 512
MOE_GATHER_ROWS = 128
MOE_GATHER_TILES = 4
MOE_VMEM_LIMIT = 60 * 1024 * 1024
MOE_Y_BLOCK = 128
MOE_Y_FETCH = MOE_TOK_TILE // MOE_Y_BLOCK + 1


def _dot_f32(x, w):
    xh, xm, _ = _split3(x)
    wh, wm, _ = _split3(w)
    return _dot(xh, wh) + (_dot(xh, wm) + _dot(xm, wh))


def _route_kernel(x_ref, ng_ref, router_ref, ltri_ref, pos_ref, gate_ref, post_ref, before_ref, total_ref,
                  carry_ref):
    tm = x_ref.shape[0]

    @pl.when(pl.program_id(0) == 0)
    def _():
        carry_ref[...] = jnp.zeros_like(carry_ref)

    hn = _rms(x_ref[...], ng_ref[...])
    logits = _dot_f32(hn, router_ref[...])
    lane = lax.broadcasted_iota(jnp.int32, (tm, LANE), 1)
    logits = jnp.where(lane < N_EXPERTS, logits, -jnp.inf)
    v1 = jnp.max(logits, axis=1, keepdims=True)
    i1 = jnp.min(jnp.where(logits == v1, lane, LANE), axis=1, keepdims=True)
    rest = jnp.where(lane == i1, -jnp.inf, logits)
    v2 = jnp.max(rest, axis=1, keepdims=True)
    i2 = jnp.min(jnp.where(rest == v2, lane, LANE), axis=1, keepdims=True)
    w1 = 1.0 / (1.0 + jnp.exp(v2 - v1))
    gate_ref[...] = jnp.where(lane == i1, w1, 0.0) + jnp.where(lane == i2, 1.0 - w1, 0.0)

    member = jnp.where((lane == i1) | (lane == i2), 1.0, 0.0)
    carry = carry_ref[...]
    rank = _dot(ltri_ref[...], member.astype(BF16)) + carry
    pos = jnp.where(member > 0.0, rank, -1.0)
    pos_ref[...] = pos
    post_ref[...] = pos.T[:SUBLANE, :]
    for blk in range(tm // MOE_TOK_TILE):
        before_ref[blk] = carry
        carry = carry + jnp.sum(member[blk * MOE_TOK_TILE:(blk + 1) * MOE_TOK_TILE], axis=0, keepdims=True)
    carry_ref[...] = carry
    total_ref[...] = carry


def moe_route(x, norm_gain, router):
    s, d = x.shape
    tm = min(MOE_ROUTE_TILES * MOE_TOK_TILE, s)
    nsub = tm // MOE_TOK_TILE
    nt = s // tm
    router_p = jnp.zeros((d, LANE), F32).at[:, :N_EXPERTS].set(router.astype(F32))
    ltri = _const(np.tril(np.ones((tm, tm)), -1), BF16)
    row = pl.BlockSpec((tm, LANE), lambda i: (i, 0))
    return pl.pallas_call(
        _route_kernel,
        grid=(nt,),
        in_specs=[pl.BlockSpec((tm, d), lambda i: (i, 0)), _const_spec((1, d)), _const_spec((d, LANE)),
                  _const_spec((tm, tm))],
        out_specs=[row, row, pl.BlockSpec((SUBLANE, tm), lambda i: (0, i)),
                   pl.BlockSpec((nsub, 1, LANE), lambda i: (i, 0, 0)), pl.BlockSpec((1, LANE), lambda i: (0, 0))],
        out_shape=[jax.ShapeDtypeStruct((s, LANE), F32), jax.ShapeDtypeStruct((s, LANE), F32),
                   jax.ShapeDtypeStruct((SUBLANE, s), F32), jax.ShapeDtypeStruct((nt * nsub, 1, LANE), F32),
                   jax.ShapeDtypeStruct((1, LANE), F32)],
        scratch_shapes=[pltpu.VMEM((1, LANE), F32)],
        compiler_params=_params("arbitrary"),
        name="moe_route",
    )(x, norm_gain.reshape(1, d).astype(F32), router_p, ltri)


def _moe_ffn_kernel(be_ref, r0_ref, tlo_ref, thi_ref, nv_ref, h_ref, post_ref, wg_ref, wu_ref, wd_ref,
                    y_ref, x_sc, acc_sc, *, tt, nsub):
    b = pl.program_id(0)
    f = pl.program_id(1)
    nb = pl.num_programs(0)
    valid = b < nv_ref[0]
    tmb, d = acc_sc.shape
    gr = tmb // nsub
    last_tile = h_ref.shape[0] // tt - 1

    def picked(e, want, t):
        t0 = pl.multiple_of(t * tt, tt)
        p = post_ref[pl.ds(e, 1), pl.ds(t0, tt)]
        sel = jnp.where(p == want, 1.0, 0.0).astype(BF16)
        return _dot(sel, h_ref[pl.ds(t0, tt), :])

    def gather_head(blk, sb):
        e = be_ref[blk]
        lo = tlo_ref[blk * nsub + sb]
        hi = thi_ref[blk * nsub + sb]
        want = (lax.broadcasted_iota(jnp.int32, (gr, 1), 0) + (r0_ref[blk] + sb * gr)).astype(F32)
        rows = picked(e, want, lo)
        for k in range(1, MOE_GATHER_TILES):
            rows = rows + picked(e, jnp.where(lo + k <= hi, want, -2.0), jnp.minimum(lo + k, last_tile))
        return rows.astype(BF16)

    def gather_tail(slot, blk, sb):
        e = be_ref[blk]
        want = (lax.broadcasted_iota(jnp.int32, (gr, 1), 0) + (r0_ref[blk] + sb * gr)).astype(F32)
        rows = pl.ds(pl.multiple_of(sb * gr, gr), gr)

        def more(t, carry):
            x_sc[slot, rows, :] = (x_sc[slot, rows, :].astype(F32) + picked(e, want, t)).astype(BF16)
            return carry

        lax.fori_loop(tlo_ref[blk * nsub + sb] + MOE_GATHER_TILES, thi_ref[blk * nsub + sb] + 1, more, 0)

    @pl.when((b == 0) & (f == 0))
    def _():
        for sb in range(nsub):
            x_sc[0, pl.ds(sb * gr, gr), :] = gather_head(0, sb)
            gather_tail(0, 0, sb)

    nxt = jnp.minimum(b + 1, nb - 1)
    nslot = (b + 1) % 2

    @pl.when(valid)
    def _():
        x_sc[nslot, pl.ds(pl.multiple_of(f * gr, gr), gr), :] = gather_head(nxt, f)
        xb = x_sc[b % 2]
        a = _silu(_dot(xb, wg_ref[0])) * _dot(xb, wu_ref[0])
        acc_sc[...] = jnp.where(f == 0, 0.0, acc_sc[...]) + _dot(a.astype(BF16), wd_ref[0].astype(BF16))

    @pl.when(valid & (thi_ref[nxt * nsub + f] - tlo_ref[nxt * nsub + f] >= MOE_GATHER_TILES))
    def _():
        gather_tail(nslot, nxt, f)

    @pl.when(f == nsub - 1)
    def _():
        y_ref[...] = jnp.where(valid, acc_sc[...], 0.0).astype(y_ref.dtype)


def moe_ffn(h, post, sched, w_gate, w_up, w_down):
    s, d = h.shape
    ne, _, dff = w_gate.shape
    tmb = MOE_ROW_BLOCK
    tt = min(MOE_TOK_TILE, s)
    nb = sched[0].shape[0]
    nf = tmb // MOE_GATHER_ROWS
    tf = dff // nf

    def fidx(b, f, nv):
        return jnp.where(b < nv[0], f, nf - 1)

    grid_spec = pltpu.PrefetchScalarGridSpec(
        num_scalar_prefetch=5,
        grid=(nb, nf),
        in_specs=[pl.BlockSpec((s, d), lambda b, f, *_: (0, 0), pipeline_mode=pl.Buffered(1)),
                  pl.BlockSpec((SUBLANE, s), lambda b, f, *_: (0, 0), pipeline_mode=pl.Buffered(1)),
                  pl.BlockSpec((1, d, tf), lambda b, f, be, r0, tlo, thi, nv: (be[b], 0, fidx(b, f, nv))),
                  pl.BlockSpec((1, d, tf), lambda b, f, be, r0, tlo, thi, nv: (be[b], 0, fidx(b, f, nv))),
                  pl.BlockSpec((1, tf, d), lambda b, f, be, r0, tlo, thi, nv: (be[b], fidx(b, f, nv), 0))],
        out_specs=pl.BlockSpec((tmb, d), lambda b, f, *_: (b, 0)),
        scratch_shapes=[pltpu.VMEM((2, tmb, d), BF16), pltpu.VMEM((tmb, d), F32)],
    )
    return pl.pallas_call(
        functools.partial(_moe_ffn_kernel, tt=tt, nsub=nf),
        grid_spec=grid_spec,
        out_shape=jax.ShapeDtypeStruct((nb * tmb, d), BF16),
        compiler_params=pltpu.CompilerParams(dimension_semantics=("arbitrary", "arbitrary"),
                                             vmem_limit_bytes=MOE_VMEM_LIMIT),
        name="moe_ffn",
    )(*sched, h, post, w_gate, w_up, w_down)


def _moe_combine_kernel(kb_ref, off_ref, lim_ref, x_ref, pos_ref, gate_ref, fg_ref, *rest):
    y_refs, o_ref, acc_sc = rest[:-2], rest[-2], rest[-1]
    t = pl.program_id(0)
    yb = y_refs[0].shape[0]
    pos = pos_ref[...]
    gate = gate_ref[...]
    col = lax.broadcasted_iota(jnp.int32, (1, yb), 1).astype(F32)

    def routed(e):
        pe = pos[:, e:e + 1]
        ge = gate[:, e:e + 1]
        r = jnp.where(pe >= 0.0, pe + off_ref[t * N_EXPERTS + e].astype(F32), -1.0)
        return r, ge

    def picked(e, k, r):
        sel = jnp.where(r == col + float(k * yb), 1.0, 0.0).astype(BF16)
        return _dot(sel, y_refs[MOE_Y_FETCH * e + k][...])

    col2 = lax.broadcasted_iota(jnp.int32, (1, 2 * yb), 1).astype(F32)
    acc = x_ref[...]
    for e in range(N_EXPERTS):
        r, ge = routed(e)
        pair = jnp.concatenate([y_refs[MOE_Y_FETCH * e][...], y_refs[MOE_Y_FETCH * e + 1][...]], axis=0)
        acc = acc + ge * _dot(jnp.where(r == col2, 1.0, 0.0).astype(BF16), pair)
    acc_sc[...] = acc
    for e in range(N_EXPERTS):
        for k in range(2, MOE_Y_FETCH):
            @pl.when(lim_ref[t * N_EXPERTS + e] > k * yb)
            def _(k=k, e=e):
                r, ge = routed(e)
                acc_sc[...] += ge * picked(e, k, r)
    o_ref[...] = _rms(acc_sc[...], fg_ref[...]).astype(o_ref.dtype)


def moe_combine(x, pos, gate, y, kb, off, lim, final_gain):
    s, d = x.shape
    tm = min(MOE_TOK_TILE, s)
    yb = MOE_Y_BLOCK
    last = y.shape[0] // yb - 1

    def yspec(e, k):
        def index(t, kb_r, off_r, lim_r):
            blk = jnp.minimum(kb_r[t * N_EXPERTS + e] + k, last)
            return (blk if k < 1 else jnp.where(lim_r[t * N_EXPERTS + e] > k * yb, blk, 0), 0)
        return pl.BlockSpec((yb, d), index)

    grid_spec = pltpu.PrefetchScalarGridSpec(
        num_scalar_prefetch=3,
        grid=(s // tm,),
        in_specs=[pl.BlockSpec((tm, d), lambda t, *_: (t, 0)),
                  pl.BlockSpec((tm, LANE), lambda t, *_: (t, 0)),
                  pl.BlockSpec((tm, LANE), lambda t, *_: (t, 0)),
                  pl.BlockSpec((1, d), lambda t, *_: (0, 0))]
        + [yspec(e, k) for e in range(N_EXPERTS) for k in range(MOE_Y_FETCH)],
        out_specs=pl.BlockSpec((tm, d), lambda t, *_: (t, 0)),
        scratch_shapes=[pltpu.VMEM((tm, d), F32)],
    )
    return pl.pallas_call(
        _moe_combine_kernel,
        grid_spec=grid_spec,
        out_shape=jax.ShapeDtypeStruct((s, d), F32),
        compiler_params=_params("arbitrary"),
        name="moe_combine",
    )(kb, off, lim, x, pos, gate, final_gain.reshape(1, d).astype(F32), *([y] * (MOE_Y_FETCH * N_EXPERTS)))


def _moe_schedule(before, total, s):
    tmb, yb, gr = MOE_ROW_BLOCK, MOE_Y_BLOCK, MOE_GATHER_ROWS
    nb = 2 * s // tmb + N_EXPERTS
    counts = total[0, :N_EXPERTS].astype(jnp.int32)
    nblk = (counts + tmb - 1) // tmb
    end = jnp.cumsum(nblk)
    first = end - nblk
    nvalid = end[-1]
    b = jnp.minimum(jnp.arange(nb, dtype=jnp.int32), nvalid - 1)
    blk_e = jnp.sum(b[:, None] >= end[None, :], axis=1).astype(jnp.int32)
    r0 = (b - first[blk_e]) * tmb
    cb = before[:, 0, :N_EXPERTS].astype(jnp.int32)
    r0s = (r0[:, None] + gr * jnp.arange(tmb // gr, dtype=jnp.int32)[None, :]).reshape(-1)
    cbe = cb[:, jnp.repeat(blk_e, tmb // gr)]
    tlo = (jnp.sum(cbe <= r0s[None, :], axis=0) - 1).astype(jnp.int32)
    thi = (jnp.sum(cbe < (r0s + gr)[None, :], axis=0) - 1).astype(jnp.int32)
    sched = (blk_e, r0.astype(jnp.int32), tlo, thi, nvalid.reshape(1).astype(jnp.int32))
    row_start = first[None, :] * tmb + cb
    kb = row_start // yb
    off = first[None, :] * tmb - kb * yb
    n_te = jnp.concatenate([cb[1:], counts[None, :]], axis=0) - cb
    lim = row_start - kb * yb + n_te
    flat = lambda a: a.reshape(-1).astype(jnp.int32)
    return sched, flat(kb), flat(off), flat(lim)


def moe_sparse(h, x, norm_gain, router, w_gate, w_up, w_down, final_gain):
    s, _ = x.shape
    pos, gate, post, before, total = moe_route(x, norm_gain, router)
    sched, kb, off, lim = _moe_schedule(before, total, s)
    y = moe_ffn(h, post, sched, w_gate, w_up, w_down)
    return moe_combine(x, pos, gate, y, kb, off, lim, final_gain)


FOX_F0 = 7 * BRANCH_W


def _split_w_in_kernel(wt_ref, mix_ref, f_ref, gate_ref):
    rest0 = FOX_F0 + N_HEADS
    gate0 = N_MIX_COLS + N_HEADS
    mix_ref[:, :FOX_F0] = wt_ref[pl.ds(0, FOX_F0), :].T.astype(BF16)
    mix_ref[:, FOX_F0:] = wt_ref[pl.ds(rest0, N_MIX_COLS - FOX_F0), :].T.astype(BF16)
    gate_ref[...] = wt_ref[pl.ds(gate0, N_BRANCH * D_MODEL), :].T.astype(BF16)
    f_rows = jnp.concatenate([wt_ref[pl.ds(FOX_F0, N_HEADS), :], jnp.zeros((LANE - N_HEADS, LANE), F32)], axis=0)
    f_ref[...] = f_rows.T.astype(BF16)


def _mixer_weights(w_in, layer, tr=LANE):
    _, d, cols = w_in.shape
    w_t = jnp.swapaxes(w_in, 1, 2)
    row = lambda w: pl.BlockSpec((tr, w), lambda i: (i, 0))
    return pl.pallas_call(
        _split_w_in_kernel,
        grid=(d // tr,),
        in_specs=[pl.BlockSpec((None, cols, tr), lambda i: (layer, 0, i))],
        out_specs=[row(N_MIX_COLS), row(LANE), row(N_BRANCH * D_MODEL)],
        out_shape=[jax.ShapeDtypeStruct((d, N_MIX_COLS), BF16), jax.ShapeDtypeStruct((d, LANE), BF16),
                   jax.ShapeDtypeStruct((d, N_BRANCH * D_MODEL), BF16)],
        compiler_params=_params("parallel"),
        name="split_w_in",
    )(w_t)


def kernel(x, w_in, w_branch, w_out, norm_mix_g, hgrn_lb_logits, hgrn_norm_g, fox_f_bias, pool_w, pool_scale,
           ret_gn_g, ret_gn_b, norm_ffn_g, ffn_w_gate, ffn_w_up, ffn_w_down, moe_router, moe_w_gate, moe_w_up,
           moe_w_down, final_norm_g):
    b, s, d = x.shape
    assert b == 1 and d == D_MODEL
    depth = w_in.shape[0]
    assert depth == 2, "layer 0 uses the dense FFN, layer 1 the experts and the final norm"
    xs = x.reshape(s, d)
    h = None
    out = None
    for layer in range(depth):
        w_mix, w_f, w_gate = _mixer_weights(w_in, layer)
        if layer == 0:
            proj, h = norm_matmul(xs, norm_mix_g[0], w_mix)
        else:
            proj = matmul(h, w_mix)
        ct, c_rows, fox_stats = fox_gate(h, proj, w_f, fox_f_bias[layer])
        branches = (
            hgrn2(proj, hgrn_lb_logits, hgrn_norm_g[layer], layer),
            fox_attention(proj, ct, c_rows, fox_stats),
            pool_mixer(proj, pool_w[layer], pool_scale[layer]),
            retention(proj, ret_gn_g[layer], ret_gn_b[layer]),
        )
        li = layer // 2
        if layer % 2 == 0:
            xs, h2, (ffn_wg, ffn_wu) = merge(h, branches, xs, w_gate, w_branch[layer].astype(BF16),
                                             w_out[layer].astype(BF16), norm_ffn_g[layer],
                                             cast=(ffn_w_gate[li], ffn_w_up[li]))
            xs, h = ffn_dense(h2, xs, ffn_wg, ffn_wu, ffn_w_down[li], norm_mix_g[layer + 1])
        else:
            xs, h2, (moe_wg, moe_wu) = merge(h, branches, xs, w_gate, w_branch[layer].astype(BF16),
                                             w_out[layer].astype(BF16), norm_ffn_g[layer], tm=MERGE_CAST_TILE,
                                             cast=(moe_w_gate[li], moe_w_up[li]))
            out = moe_sparse(h2, xs, norm_ffn_g[layer], moe_router[li], moe_wg, moe_wu, moe_w_down[li],
                             final_norm_g)
    return out.reshape(b, s, d)
```

```python
import functools
import math

import jax
import jax.numpy as jnp
import numpy as np
from jax import lax
from jax.experimental import pallas as pl
from jax.experimental.pallas import tpu as pltpu

D_MODEL = 1024
N_BRANCH = 4
BRANCH_W = D_MODEL // N_BRANCH
HEAD_DIM = 64
N_HEADS = BRANCH_W // HEAD_DIM
POOL_WINDOWS = (2, 4, 8, 16)
POOL_GROUP = BRANCH_W // len(POOL_WINDOWS)
POOL_HALO = 16
RET_DECAY_BASE = 5.0
ROPE_BASE = 10000.0
D_FF = 7 * D_MODEL // 2
N_EXPERTS = 8
RMS_EPS = 1e-6
LN_EPS = 1e-5
N_MIX_COLS = 12 * BRANCH_W

LANE = 128
SUBLANE = 8
VMEM_LIMIT = 56 * 1024 * 1024

HG_CHUNK = 64
HG_SUB = 16
HG_FAST_MIN_LOGDECAY = -60.0
RET_CHUNK = 256

F32 = jnp.float32
BF16 = jnp.bfloat16
NT_DIMS = (((1,), (1,)), ((), ()))


def _params(*sem):
    return pltpu.CompilerParams(dimension_semantics=sem, vmem_limit_bytes=VMEM_LIMIT)


def _const_spec(shape):
    nd = len(shape)
    return pl.BlockSpec(shape, lambda *_: (0,) * nd, pipeline_mode=pl.Buffered(1))


def _split3(x):
    hi = x.astype(BF16)
    r1 = x - hi.astype(F32)
    mid = r1.astype(BF16)
    lo = (r1 - mid.astype(F32)).astype(BF16)
    return hi, mid, lo


def _dot(a, b):
    return jnp.dot(a, b, preferred_element_type=F32)


def _dot_nt(a, b):
    return lax.dot_general(a, b, NT_DIMS, preferred_element_type=F32)


def _dot_exact_rhs(x, m_bf16, terms=3):
    return sum(_dot(part, m_bf16) for part in _split3(x)[:terms])


def _dot_exact_lhs(m_bf16, x, terms=3):
    return sum(_dot(m_bf16, part) for part in _split3(x)[:terms])


def _sigmoid(x):
    return 1.0 / (1.0 + jnp.exp(-x))


def _silu(x):
    return x * _sigmoid(x)


def _rms(x, gain):
    return x * lax.rsqrt(jnp.mean(x * x, axis=-1, keepdims=True) + RMS_EPS) * gain


def _const(a, dtype=F32):
    return jnp.asarray(np.asarray(a, np.float32), dtype)


def _head_of(n):
    return np.arange(n) // HEAD_DIM


def _head_ones():
    h = _head_of(BRANCH_W)
    return _const(h[:, None] == h[None, :], BF16)


def _head_masks():
    return _const(_head_of(BRANCH_W)[None, :] == np.arange(N_HEADS)[:, None])


def _norm_matmul_kernel(x_ref, g_ref, b_ref, o_ref, h_ref):
    @pl.when(pl.program_id(1) == 0)
    def _():
        h_ref[...] = _rms(x_ref[...], g_ref[...]).astype(h_ref.dtype)

    o_ref[...] = _dot(h_ref[...], b_ref[...]).astype(o_ref.dtype)


def norm_matmul(x, gain, b, tm=2048, tn=1024):
    m, k = x.shape
    _, n = b.shape
    tm = min(tm, m)
    return pl.pallas_call(
        _norm_matmul_kernel,
        grid=(m // tm, n // tn),
        in_specs=[pl.BlockSpec((tm, k), lambda i, j: (i, 0)), _const_spec((1, k)),
                  pl.BlockSpec((k, tn), lambda i, j: (0, j))],
        out_specs=[pl.BlockSpec((tm, tn), lambda i, j: (i, j)), pl.BlockSpec((tm, k), lambda i, j: (i, 0))],
        out_shape=[jax.ShapeDtypeStruct((m, n), BF16), jax.ShapeDtypeStruct((m, k), BF16)],
        compiler_params=_params("arbitrary", "arbitrary"),
        name="in_proj_norm",
    )(x, gain.reshape(1, k).astype(F32), b)


def _matmul_kernel(a_ref, b_ref, o_ref):
    o_ref[...] = _dot(a_ref[...], b_ref[...]).astype(o_ref.dtype)


def matmul(a, b, out_dtype=BF16, tm=2048, tn=1024):
    m, k = a.shape
    _, n = b.shape
    tm = min(tm, m)
    return pl.pallas_call(
        _matmul_kernel,
        grid=(n // tn, m // tm),
        in_specs=[pl.BlockSpec((tm, k), lambda j, i: (i, 0)),
                  pl.BlockSpec((k, tn), lambda j, i: (0, j))],
        out_specs=pl.BlockSpec((tm, tn), lambda j, i: (i, j)),
        out_shape=jax.ShapeDtypeStruct((m, n), out_dtype),
        compiler_params=_params("parallel", "parallel"),
        name="in_proj",
    )(a, b)


def _hgrn_kernel(q_ref, f_ref, i_ref, g_ref, lbl_ref, ng_ref, ones_ref, hm_ref, tril_ref, halfsum_ref, fmask_ref,
                 o_ref, st_ref, bpad, kpad, vpad, astack, lf_sc, kk_sc, o_sc, *, layer, tile):
    c, sub = HG_CHUNK, HG_SUB
    nsub = c // sub
    half = c // 2

    @pl.when(pl.program_id(0) == 0)
    def _():
        st_ref[...] = jnp.zeros_like(st_ref)
        bpad[...] = jnp.zeros_like(bpad)
        kpad[...] = jnp.zeros_like(kpad)
        vpad[...] = jnp.zeros_like(vpad)

    lbl = lbl_ref[...]
    e = jnp.exp(lbl - jnp.max(lbl, axis=0, keepdims=True))
    p = e / jnp.sum(e, axis=0, keepdims=True)
    lb = jnp.zeros((1, BRANCH_W), F32)
    for l in range(1, layer + 1):
        lb = lb + p[l:l + 1, :]

    ones_bd = ones_ref[...]
    hm = hm_ref[...]
    tril = tril_ref[...]
    row = lax.broadcasted_iota(jnp.int32, (c, 1), 0)
    row_in_sub = row % sub
    bd_mask = ones_bd.astype(F32)

    sig = _sigmoid(f_ref[...].astype(F32))
    logf_all = jnp.log(lb + (1.0 - lb) * sig)
    lf_sc[...] = _dot_exact_lhs(tril, logf_all, terms=2)
    kk_sc[...] = (1.0 - lb) * (1.0 - sig)
    min_decay = jnp.min(_dot(halfsum_ref[...], logf_all.astype(BF16)))

    def load(ci):
        r0 = pl.multiple_of(ci * c, c)
        q = q_ref[pl.ds(r0, c), :].astype(F32)
        v = i_ref[pl.ds(r0, c), :].astype(F32)
        kk = kk_sc[pl.ds(r0, c), :]
        b = lf_sc[pl.ds(r0, c), :]
        return r0, q, v, kk, b

    def finish(r0, q_decayed, v, kk, b, intra):
        st = st_ref[...]
        inter = _dot_nt(q_decayed.astype(BF16), st.astype(BF16))
        b_last = b[c - 1:c, :]
        ks_end = (kk * jnp.exp(b_last - b)).astype(BF16)
        upd = _dot(v.T.astype(BF16), ks_end)
        st_ref[...] = st * jnp.exp(b_last) + upd * bd_mask
        o_sc[pl.ds(r0, c), :] = intra + inter

    def fast_chunk(ci, carry):
        r0, q, v, kk, b = load(ci)
        second = row >= half
        m_row = b[half - 1:half, :]
        mref = jnp.where(second, m_row, 0.0)
        qp = q * jnp.exp(b - mref)
        kp = kk * jnp.exp(mref - b)
        e_m = jnp.exp(m_row)
        kaug = jnp.concatenate([kp, kp[:half, :] * e_m], axis=0)
        vaug = jnp.concatenate([v, v[:half, :]], axis=0)
        k_heads = jnp.concatenate([kaug * hm[h:h + 1, :] for h in range(N_HEADS)], axis=0).astype(BF16)
        v_heads = jnp.concatenate([vaug * hm[h:h + 1, :] for h in range(N_HEADS)], axis=0).astype(BF16)
        sc = jnp.where(fmask_ref[...] > 0.0, _dot_nt(qp.astype(BF16), k_heads), 0.0)
        intra = _dot(sc.astype(BF16), v_heads)
        finish(r0, jnp.where(second, qp * e_m, qp), v, kk, b, intra)
        return carry

    def exact_chunk(ci, carry):
        r0, q, v, kk, b = load(ci)

        bpad[pl.ds(sub, c), :] = b
        kpad[pl.ds(sub, c), :] = kk
        vpad[pl.ds(sub, c), :] = v

        for d in range(sub):
            b_d = bpad[pl.ds(sub - d, c), :]
            k_d = kpad[pl.ds(sub - d, c), :]
            a = jnp.where(row_in_sub >= d, q * k_d * jnp.exp(b - b_d), 0.0)
            astack[pl.ds(d * c, c), :] = a.astype(BF16)
        pall = _dot(astack[...], ones_bd)
        intra = jnp.zeros((c, BRANCH_W), F32)
        for d in range(sub):
            intra = intra + pall[d * c:(d + 1) * c, :] * vpad[pl.ds(sub - d, c), :]

        pieces = [jnp.zeros((sub, BRANCH_W), F32)]
        for si in range(1, nsub):
            lo = si * sub
            m_i = b[lo - 1:lo, :]
            qs = q[lo:lo + sub, :] * jnp.exp(b[lo:lo + sub, :] - m_i)
            ks = (kk[:lo, :] * jnp.exp(m_i - b[:lo, :])).astype(BF16)
            qx = jnp.concatenate([qs * hm[h:h + 1, :] for h in range(N_HEADS)], axis=0).astype(BF16)
            sc = _dot_nt(qx, ks)
            r = _dot(sc.astype(BF16), v[:lo, :].astype(BF16))
            acc = jnp.zeros((sub, BRANCH_W), F32)
            for h in range(N_HEADS):
                acc = acc + r[h * sub:(h + 1) * sub, :] * hm[h:h + 1, :]
            pieces.append(acc)
        intra = intra + jnp.concatenate(pieces, axis=0)
        finish(r0, q * jnp.exp(b), v, kk, b, intra)
        return carry

    lax.cond(min_decay >= HG_FAST_MIN_LOGDECAY,
             lambda: lax.fori_loop(0, tile // c, fast_chunk, 0, unroll=2),
             lambda: lax.fori_loop(0, tile // c, exact_chunk, 0))

    o = o_sc[...]
    ms = _dot_exact_rhs(o * o, ones_bd, terms=2) * (1.0 / HEAD_DIM)
    y = o * lax.rsqrt(ms + RMS_EPS) * ng_ref[...] * _silu(g_ref[...].astype(F32))
    o_ref[...] = y.astype(o_ref.dtype)


def hgrn2(proj, lb_logits, norm_g, layer, tile=512):
    s = proj.shape[0]
    depth = lb_logits.shape[0]
    c, sub = HG_CHUNK, HG_SUB
    half = c // 2
    tile = min(tile, s)
    col = lambda j: pl.BlockSpec((tile, BRANCH_W), lambda i, j=j: (i, j))
    pos = np.arange(tile)
    tril = _const((pos[:, None] // c == pos[None, :] // c) & (pos[None, :] <= pos[:, None]), BF16)
    nhalf = tile // half
    halfsum = _const(np.arange(tile)[None, :] // half == np.arange(nhalf)[:, None], BF16)
    t = np.arange(c)[:, None]
    col_s = np.arange(c + half)[None, :]
    same_half = (col_s < c) & (col_s // half == t // half) & (col_s <= t)
    cross = (col_s >= c) & (t >= half)
    fmask = _const(np.tile(same_half | cross, (1, N_HEADS)))
    return pl.pallas_call(
        functools.partial(_hgrn_kernel, layer=layer, tile=tile),
        grid=(s // tile,),
        in_specs=[col(0), col(1), col(2), col(3),
                  _const_spec((depth, BRANCH_W)), _const_spec((1, BRANCH_W)),
                  _const_spec((BRANCH_W, BRANCH_W)), _const_spec((N_HEADS, BRANCH_W)),
                  _const_spec((tile, tile)), _const_spec((nhalf, tile)),
                  _const_spec((c, N_HEADS * (c + half)))],
        out_specs=pl.BlockSpec((tile, BRANCH_W), lambda i: (i, 0)),
        out_shape=jax.ShapeDtypeStruct((s, BRANCH_W), BF16),
        scratch_shapes=[pltpu.VMEM((BRANCH_W, BRANCH_W), F32),
                        pltpu.VMEM((c + sub, BRANCH_W), F32),
                        pltpu.VMEM((c + sub, BRANCH_W), F32),
                        pltpu.VMEM((c + sub, BRANCH_W), F32),
                        pltpu.VMEM((sub * c, BRANCH_W), BF16),
                        pltpu.VMEM((tile, BRANCH_W), F32),
                        pltpu.VMEM((tile, BRANCH_W), F32),
                        pltpu.VMEM((tile, BRANCH_W), F32)],
        compiler_params=_params("arbitrary"),
        name="hgrn2",
    )(proj, proj, proj, proj, lb_logits.astype(F32), norm_g.reshape(1, BRANCH_W).astype(F32),
      _head_ones(), _head_masks(), tril, halfsum, fmask)


FOX_TILE = 256
FOX_GATE_BLOCKS = 2
FOX_NSTAT = 16
FOX_FIXED_MAX = 30.0
FOX_SKIP_LOG = 40.0


def _fox_gate_kernel(h_ref, q_ref, k_ref, wf_ref, bias_ref, tril_ref, ones_ref,
                     ct_ref, c_ref, stat_ref, carry_ref, kmax_ref):
    @pl.when(pl.program_id(0) == 0)
    def _():
        carry_ref[...] = jnp.zeros_like(carry_ref)
        kmax_ref[...] = jnp.zeros_like(kmax_ref)

    logit = _dot(h_ref[...], wf_ref[...]) + bias_ref[...]
    logf = jnp.minimum(logit, 0.0) - jnp.log(1.0 + jnp.exp(-jnp.abs(logit)))
    carry = carry_ref[...]
    parts = []
    for blk in range(logf.shape[0] // FOX_TILE):
        part = _dot_exact_lhs(tril_ref[...], logf[blk * FOX_TILE:(blk + 1) * FOX_TILE]) + carry
        carry = part[-1:, :]
        parts.append(part)
    cum = jnp.concatenate(parts, axis=0)
    carry_ref[...] = carry
    ct_ref[...] = cum.T[:SUBLANE, :]
    c_ref[...] = cum

    ones_bd = ones_ref[...]
    q = q_ref[...].astype(F32)
    k = k_ref[...].astype(F32)
    scale = HEAD_DIM ** -0.5
    head_lane = lax.broadcasted_iota(jnp.int32, (1, BRANCH_W), 1) // HEAD_DIM
    c_heads = jnp.zeros(q.shape, F32)
    for h in range(N_HEADS):
        c_heads = jnp.where(head_lane == h, cum[:, h:h + 1], c_heads)
    slack = 1.0 + 2.0 ** -6
    qn = jnp.sqrt(_dot((q * q).astype(BF16), ones_bd)) * (scale * slack)
    kn = jnp.sqrt(_dot((k * k).astype(BF16), ones_bd)) * slack
    diag = _dot((q * k).astype(BF16), ones_bd) * scale - (2.0 ** -6) * qn * kn
    e_row = c_heads - diag
    kmax = kmax_ref[...]
    for blk in range(q.shape[0] // FOX_TILE):
        rs = slice(blk * FOX_TILE, (blk + 1) * FOX_TILE)
        kmax = jnp.maximum(kmax, jnp.max(kn[rs], axis=0, keepdims=True))
        rows = [jnp.max(qn[rs], axis=0, keepdims=True),
                jnp.max(e_row[rs], axis=0, keepdims=True),
                kmax,
                c_heads[(blk + 1) * FOX_TILE - 1:(blk + 1) * FOX_TILE, :]]
        stat_ref[blk] = jnp.concatenate(rows + [jnp.zeros((SUBLANE - len(rows), BRANCH_W), F32)], axis=0)
    kmax_ref[...] = kmax


def fox_gate(h, proj, w_f, f_bias):
    s, d = h.shape
    tile = min(FOX_GATE_BLOCKS * FOX_TILE, s)
    nblk = tile // FOX_TILE
    bias = jnp.zeros((1, LANE), F32).at[0, :N_HEADS].set(f_bias.astype(F32))
    tril = _const(np.tril(np.ones((FOX_TILE, FOX_TILE))), BF16)
    ct, c_rows, stats = pl.pallas_call(
        _fox_gate_kernel,
        grid=(s // tile,),
        in_specs=[pl.BlockSpec((tile, d), lambda i: (i, 0)),
                  pl.BlockSpec((tile, BRANCH_W), lambda i: (i, 4)),
                  pl.BlockSpec((tile, BRANCH_W), lambda i: (i, 5)),
                  _const_spec((d, LANE)), _const_spec((1, LANE)), _const_spec((FOX_TILE, FOX_TILE)),
                  _const_spec((BRANCH_W, BRANCH_W))],
        out_specs=[pl.BlockSpec((SUBLANE, tile), lambda i: (0, i)),
                   pl.BlockSpec((tile, LANE), lambda i: (i, 0)),
                   pl.BlockSpec((nblk, SUBLANE, BRANCH_W), lambda i: (i, 0, 0))],
        out_shape=[jax.ShapeDtypeStruct((SUBLANE, s), F32),
                   jax.ShapeDtypeStruct((s, LANE), F32),
                   jax.ShapeDtypeStruct((s // FOX_TILE, SUBLANE, BRANCH_W), F32)],
        scratch_shapes=[pltpu.VMEM((1, LANE), F32), pltpu.VMEM((1, BRANCH_W), F32)],
        compiler_params=_params("arbitrary"),
        name="fox_gate",
    )(h, proj, proj, w_f, bias, tril, _head_ones())
    return ct, c_rows, stats[:, :4, ::HEAD_DIM].reshape(-1)


def _fox_kernel(stat_ref, q_ref, k_ref, v_ref, ct_ref, c_ref, hm_ref, o_ref, m_sc, l_sc, acc_sc, *, tq):
    i = pl.program_id(0)
    q0 = pl.multiple_of(i * tq, tq)
    hm = hm_ref[...]
    q = q_ref[...].astype(F32) * (HEAD_DIM ** -0.5)
    qh = [(q * hm[h:h + 1, :]).astype(BF16) for h in range(N_HEADS)]
    c_q0 = ct_ref[:, pl.ds(q0, tq)][:, 0:1]

    first = i
    for h in range(N_HEADS):
        qmax = stat_ref[i * FOX_NSTAT + h]
        emax = stat_ref[i * FOX_NSTAT + N_HEADS + h]

        def needed(j, h=h, qmax=qmax, emax=emax):
            jc = jnp.maximum(j, 0)
            bound = (qmax * stat_ref[jc * FOX_NSTAT + 2 * N_HEADS + h] + emax
                     - stat_ref[jc * FOX_NSTAT + 3 * N_HEADS + h])
            return (j >= 0) & (bound >= -FOX_SKIP_LOG)

        last_dropped = lax.while_loop(needed, lambda j: j - 1, i - 1)
        first = jnp.minimum(first, last_dropped + 1)

    l_sc[...] = jnp.zeros_like(l_sc)
    acc_sc[...] = jnp.zeros_like(acc_sc)

    def causal(sc):
        r = lax.broadcasted_iota(jnp.int32, (tq, tq), 0)
        cidx = lax.broadcasted_iota(jnp.int32, (tq, tq), 1)
        return jnp.where(cidx <= r, sc, -jnp.inf)

    def online_block(s0, diagonal):
        kb = k_ref[pl.ds(s0, tq), :]
        vb = v_ref[pl.ds(s0, tq), :]
        bias = c_q0 - ct_ref[:, pl.ds(s0, tq)]
        for h in range(N_HEADS):
            sc = _dot_nt(qh[h], kb) + bias[h:h + 1, :]
            if diagonal:
                sc = causal(sc)
            m_prev = m_sc[h]
            m_new = jnp.maximum(m_prev, jnp.max(sc, axis=1, keepdims=True))
            alpha = jnp.exp(m_prev - m_new)
            p = jnp.exp(sc - jnp.tile(m_new, (1, tq // LANE)))
            l_sc[h] = alpha * l_sc[h] + jnp.sum(p, axis=1, keepdims=True)
            acc_sc[h] = acc_sc[h] * jnp.tile(alpha, (1, BRANCH_W // LANE)) + _dot(p.astype(BF16), vb)
            m_sc[h] = m_new

    tops = [stat_ref[i * FOX_NSTAT + h] * stat_ref[i * FOX_NSTAT + 2 * N_HEADS + h] for h in range(N_HEADS)]
    c_tile = c_ref[...]
    shift = [c_tile[:, h:h + 1] - c_q0[h:h + 1, :] - tops[h] for h in range(N_HEADS)]

    def fixed_block(s0, diagonal):
        kb = k_ref[pl.ds(s0, tq), :]
        vb = v_ref[pl.ds(s0, tq), :]
        bias = c_q0 - ct_ref[:, pl.ds(s0, tq)]
        for h in range(N_HEADS):
            sc = _dot_nt(qh[h], kb) + bias[h:h + 1, :] + shift[h]
            if diagonal:
                sc = causal(sc)
            p = jnp.exp(sc)
            l_sc[h] += p[:, :LANE] + p[:, LANE:]
            acc_sc[h] += _dot(p.astype(BF16), vb)

    def run(block, row_sum):
        lax.fori_loop(first, i, lambda j, carry: (block(pl.multiple_of(j * tq, tq), False), carry)[1], 0)
        block(q0, True)
        out = jnp.zeros((tq, BRANCH_W), F32)
        for h in range(N_HEADS):
            out = out + acc_sc[h] * hm[h:h + 1, :] / row_sum(l_sc[h])
        o_ref[...] = out.astype(o_ref.dtype)

    def run_online():
        m_sc[...] = jnp.full_like(m_sc, -jnp.inf)
        run(online_block, lambda l: jnp.tile(l, (1, BRANCH_W // LANE)))

    def run_fixed():
        run(fixed_block, lambda l: jnp.sum(l, axis=1, keepdims=True))

    lax.cond(functools.reduce(jnp.maximum, tops) <= FOX_FIXED_MAX, run_fixed, run_online)


def fox_attention(proj, ct, c_rows, stats):
    s = proj.shape[0]
    tq = min(FOX_TILE, s)
    full = lambda j: pl.BlockSpec((s, BRANCH_W), lambda i, j=j: (0, j), pipeline_mode=pl.Buffered(1))
    return pl.pallas_call(
        functools.partial(_fox_kernel, tq=tq),
        grid=(s // tq,),
        in_specs=[pl.BlockSpec(memory_space=pltpu.SMEM),
                  pl.BlockSpec((tq, BRANCH_W), lambda i: (i, 4)), full(5), full(6),
                  _const_spec((SUBLANE, s)), pl.BlockSpec((tq, LANE), lambda i: (i, 0)),
                  _const_spec((N_HEADS, BRANCH_W))],
        out_specs=pl.BlockSpec((tq, BRANCH_W), lambda i: (i, 0)),
        out_shape=jax.ShapeDtypeStruct((s, BRANCH_W), BF16),
        scratch_shapes=[pltpu.VMEM((N_HEADS, tq, LANE), F32),
                        pltpu.VMEM((N_HEADS, tq, LANE), F32),
                        pltpu.VMEM((N_HEADS, tq, BRANCH_W), F32)],
        compiler_params=_params("parallel"),
        name="fox_attention",
    )(stats, proj, proj, proj, ct, c_rows, _head_masks())


def _pool_kernel(u_ref, w_ref, scale_ref, o_ref, ext, *, tile):
    i = pl.program_id(0)

    @pl.when(i == 0)
    def _():
        ext[pl.ds(0, POOL_HALO), :] = jnp.zeros((POOL_HALO, BRANCH_W), F32)

    u = u_ref[...].astype(F32)
    ext[pl.ds(POOL_HALO, tile), :] = u
    pos = (i * tile + lax.broadcasted_iota(jnp.int32, (tile, 1), 0) + 1).astype(F32)
    halves = []
    for half in range(BRANCH_W // LANE):
        lanes = pl.ds(half * LANE, LANE)
        w_small, w_big = POOL_WINDOWS[2 * half], POOL_WINDOWS[2 * half + 1]
        run = u[:, half * LANE:(half + 1) * LANE]
        sums = {}
        for j in range(1, w_big):
            if j == w_small:
                sums[w_small] = run
            run = run + ext[pl.ds(POOL_HALO - j, tile), lanes]
        sums[w_big] = run
        lane = lax.broadcasted_iota(jnp.int32, (1, LANE), 1)
        small = lane < POOL_GROUP
        total = jnp.where(small, sums[w_small], sums[w_big])
        count = jnp.where(small, jnp.minimum(pos, float(w_small)), jnp.minimum(pos, float(w_big)))
        halves.append(total / count)
    mean = jnp.concatenate(halves, axis=1)
    d = (mean - u).astype(BF16)
    y = _dot(d, w_ref[...]) * scale_ref[...]
    o_ref[...] = y.astype(o_ref.dtype)
    ext[pl.ds(0, POOL_HALO), :] = u[tile - POOL_HALO:, :]


def pool_mixer(proj, w_pool, scale, tile=2048):
    s = proj.shape[0]
    tile = min(tile, s)
    ng = len(POOL_WINDOWS)
    w_bd = jnp.zeros((BRANCH_W, BRANCH_W), F32)
    for gi in range(ng):
        lo = gi * POOL_GROUP
        w_bd = w_bd.at[lo:lo + POOL_GROUP, lo:lo + POOL_GROUP].set(w_pool[gi].astype(F32))
    return pl.pallas_call(
        functools.partial(_pool_kernel, tile=tile),
        grid=(s // tile,),
        in_specs=[pl.BlockSpec((tile, BRANCH_W), lambda i: (i, 7)),
                  _const_spec((BRANCH_W, BRANCH_W)), _const_spec((1, BRANCH_W))],
        out_specs=pl.BlockSpec((tile, BRANCH_W), lambda i: (i, 0)),
        out_shape=jax.ShapeDtypeStruct((s, BRANCH_W), BF16),
        scratch_shapes=[pltpu.VMEM((tile + POOL_HALO, BRANCH_W), F32)],
        compiler_params=_params("arbitrary"),
        name="pool_mixer",
    )(proj, w_bd.astype(BF16), scale.reshape(1, BRANCH_W).astype(F32))


def _ret_kernel(q_ref, k_ref, v_ref, g_ref, rope_ref, ecos_ref, esin_ref, perm_ref, ones_ref, hm_ref,
                dstack_ref, xi_ref, zeta_ref, gc_ref, gng_ref, gnb_ref, o_ref, st_ref, *, tile):
    c = RET_CHUNK

    @pl.when(pl.program_id(0) == 0)
    def _():
        st_ref[...] = jnp.zeros_like(st_ref)

    perm = perm_ref[...]
    ones_bd = ones_ref[...]
    bd_mask = ones_bd.astype(F32)
    hm = hm_ref[...]

    tab = rope_ref[...]
    cos = _dot_exact_rhs(tab, ecos_ref[...], terms=2)
    sin = _dot_exact_rhs(tab, esin_ref[...], terms=2)
    q_all = q_ref[...]
    k_all = k_ref[...]
    qr_all = q_all.astype(F32) * cos + _dot(q_all, perm) * sin
    kr_all = (k_all.astype(F32) * cos + _dot(k_all, perm) * sin) * (HEAD_DIM ** -0.5)

    outs = []
    for ci in range(tile // c):
        r0 = ci * c
        qr = qr_all[r0:r0 + c, :]
        kr = kr_all[r0:r0 + c, :]
        v = v_ref[pl.ds(r0, c), :]

        qx = jnp.concatenate([qr * hm[h:h + 1, :] for h in range(N_HEADS)], axis=0).astype(BF16)
        sc = _dot_nt(qx, kr.astype(BF16)) * dstack_ref[...]
        r = _dot(sc.astype(BF16), v)
        intra = jnp.zeros((c, BRANCH_W), F32)
        for h in range(N_HEADS):
            intra = intra + r[h * c:(h + 1) * c, :] * hm[h:h + 1, :]

        st = st_ref[...]
        inter = _dot_nt((qr * xi_ref[...]).astype(BF16), st.astype(BF16))
        upd = _dot(v.astype(F32).T.astype(BF16), (kr * zeta_ref[...]).astype(BF16))
        st_ref[...] = st * gc_ref[...] + upd * bd_mask
        outs.append(intra + inter)

    o = jnp.concatenate(outs, axis=0)
    mu = _dot_exact_rhs(o, ones_bd, terms=2) * (1.0 / HEAD_DIM)
    cen = o - mu
    var = _dot_exact_rhs(cen * cen, ones_bd, terms=2) * (1.0 / HEAD_DIM)
    y = cen * lax.rsqrt(var + LN_EPS) * gng_ref[...] + gnb_ref[...]
    o_ref[...] = (y * _silu(g_ref[...].astype(F32))).astype(o_ref.dtype)


def _rope_tables(s):
    half = HEAD_DIM // 2
    pos = np.arange(s, dtype=np.float64)
    inv_freq = ROPE_BASE ** (-np.arange(half, dtype=np.float64) / half)
    ang = pos[:, None] * inv_freq[None, :]
    table = np.concatenate([np.cos(ang), np.sin(ang), np.zeros((s, LANE - 2 * half))], axis=1)
    lane = np.arange(BRANCH_W)
    src = np.arange(LANE)[:, None]
    ecos = src == (lane % half)[None, :]
    sign = np.where(lane % HEAD_DIM < half, -1.0, 1.0)
    esin = (src == (half + lane % half)[None, :]) * sign[None, :]
    return _const(table), _const(ecos, BF16), _const(esin, BF16)


def _ret_constants():
    c = RET_CHUNK
    half = HEAD_DIM // 2
    lane = np.arange(BRANCH_W)
    partner = np.where(lane % HEAD_DIM < half, lane + half, lane - half)
    perm = lane[:, None] == partner[None, :]
    log_gamma = np.log1p(-np.exp2(-RET_DECAY_BASE - np.arange(N_HEADS, dtype=np.float64)))
    ci = np.arange(c, dtype=np.float64)
    diff = ci[:, None] - ci[None, :]
    intra = np.where(diff >= 0, np.exp(diff * log_gamma[:, None, None]), 0.0)
    dstack = intra.reshape(N_HEADS * c, c)
    lg_lane = np.repeat(log_gamma, HEAD_DIM)[None, :]
    xi = np.exp((ci[:, None] + 1.0) * lg_lane)
    zeta = np.exp((c - 1.0 - ci[:, None]) * lg_lane)
    gc = np.exp(c * lg_lane)
    return _const(perm, BF16), _const(dstack), _const(xi), _const(zeta), _const(gc)


def retention(proj, gn_g, gn_b, tile=1024):
    s = proj.shape[0]
    c = RET_CHUNK
    tile = min(tile, s)
    rope, ecos, esin = _rope_tables(s)
    perm, dstack, xi, zeta, gc = _ret_constants()
    col = lambda j: pl.BlockSpec((tile, BRANCH_W), lambda i, j=j: (i, j))
    row = pl.BlockSpec((tile, BRANCH_W), lambda i: (i, 0))
    return pl.pallas_call(
        functools.partial(_ret_kernel, tile=tile),
        grid=(s // tile,),
        in_specs=[col(8), col(9), col(10), col(11), pl.BlockSpec((tile, LANE), lambda i: (i, 0)),
                  _const_spec((LANE, BRANCH_W)), _const_spec((LANE, BRANCH_W)),
                  _const_spec((BRANCH_W, BRANCH_W)), _const_spec((BRANCH_W, BRANCH_W)),
                  _const_spec((N_HEADS, BRANCH_W)), _const_spec((N_HEADS * c, c)),
                  _const_spec((c, BRANCH_W)), _const_spec((c, BRANCH_W)), _const_spec((1, BRANCH_W)),
                  _const_spec((1, BRANCH_W)), _const_spec((1, BRANCH_W))],
        out_specs=row,
        out_shape=jax.ShapeDtypeStruct((s, BRANCH_W), BF16),
        scratch_shapes=[pltpu.VMEM((BRANCH_W, BRANCH_W), F32)],
        compiler_params=_params("arbitrary"),
        name="retention",
    )(proj, proj, proj, proj, rope, ecos, esin, perm, _head_ones(), _head_masks(), dstack, xi, zeta, gc,
      gn_g.reshape(1, BRANCH_W).astype(F32), gn_b.reshape(1, BRANCH_W).astype(F32))


def _merge_kernel(h_ref, o0_ref, o1_ref, o2_ref, o3_ref, x_ref, wg_ref, wb_ref, wo_ref, g_ref, *rest):
    n_cast = (len(rest) - 2) // 2
    cast_in, (xo_ref, ho_ref), cast_out = rest[:n_cast], rest[n_cast:n_cast + 2], rest[n_cast + 2:]
    for src, dst in zip(cast_in, cast_out):
        dst[...] = src[...].astype(dst.dtype)
    h = h_ref[...]
    merged = jnp.zeros(x_ref.shape, F32)
    for bi, o_ref in enumerate((o0_ref, o1_ref, o2_ref, o3_ref)):
        gate = _sigmoid(_dot(h, wg_ref[:, bi * D_MODEL:(bi + 1) * D_MODEL]))
        merged = merged + gate * _dot(o_ref[...], wb_ref[bi])
    x_new = x_ref[...] + _dot(merged.astype(BF16), wo_ref[...])
    xo_ref[...] = x_new
    ho_ref[...] = _rms(x_new, g_ref[...]).astype(ho_ref.dtype)


def merge(h, branches, x, w_gate, w_branch, w_out, next_gain, tm=512, cast=()):
    s, d = x.shape
    tm = min(tm, s)
    steps = s // tm
    row = lambda w: pl.BlockSpec((tm, w), lambda i: (i, 0))
    flat = [c.reshape(-1, c.shape[-1]) for c in cast]
    slab = lambda c: pl.BlockSpec((c.shape[0] // steps, c.shape[1]), lambda i: (i, 0))
    outs = pl.pallas_call(
        _merge_kernel,
        grid=(steps,),
        in_specs=[row(d), row(BRANCH_W), row(BRANCH_W), row(BRANCH_W), row(BRANCH_W), row(d),
                  _const_spec((d, N_BRANCH * d)), _const_spec((N_BRANCH, BRANCH_W, d)),
                  _const_spec((d, d)), _const_spec((1, d))] + [slab(c) for c in flat],
        out_specs=[row(d), row(d)] + [slab(c) for c in flat],
        out_shape=[jax.ShapeDtypeStruct((s, d), F32), jax.ShapeDtypeStruct((s, d), BF16)]
        + [jax.ShapeDtypeStruct(c.shape, BF16) for c in flat],
        compiler_params=_params("parallel"),
        name="merge",
    )(h, *branches, x, w_gate, w_branch, w_out, next_gain.reshape(1, d).astype(F32), *flat)
    return outs[0], outs[1], [o.reshape(c.shape) for o, c in zip(outs[2:], cast)]


def _ffn_kernel(h_ref, x_ref, wg_ref, wu_ref, wd_ref, g_ref, xo_ref, ho_ref, acc_ref):
    f = pl.program_id(1)

    @pl.when(f == 0)
    def _():
        acc_ref[...] = jnp.zeros_like(acc_ref)

    h = h_ref[...]
    a = _silu(_dot(h, wg_ref[...].astype(BF16))) * _dot(h, wu_ref[...].astype(BF16))
    acc_ref[...] += _dot(a.astype(BF16), wd_ref[...].astype(BF16))

    @pl.when(f == pl.num_programs(1) - 1)
    def _():
        x_new = x_ref[...] + acc_ref[...]
        xo_ref[...] = x_new
        ho_ref[...] = _rms(x_new, g_ref[...]).astype(ho_ref.dtype)


def ffn_dense(h, x, w_gate, w_up, w_down, next_gain, tm=1024, tf=512):
    s, d = x.shape
    tm = min(tm, s)
    dff = w_gate.shape[1]
    row = lambda: pl.BlockSpec((tm, d), lambda i, f: (i, 0))
    return pl.pallas_call(
        _ffn_kernel,
        grid=(s // tm, dff // tf),
        in_specs=[row(), row(),
                  pl.BlockSpec((d, tf), lambda i, f: (0, f)),
                  pl.BlockSpec((d, tf), lambda i, f: (0, f)),
                  pl.BlockSpec((tf, d), lambda i, f: (f, 0)),
                  _const_spec((1, d))],
        out_specs=[row(), row()],
        out_shape=[jax.ShapeDtypeStruct((s, d), F32), jax.ShapeDtypeStruct((s, d), BF16)],
        scratch_shapes=[pltpu.VMEM((tm, d), F32)],
        compiler_params=_params("parallel", "arbitrary"),
        name="ffn_dense",
    )(h, x, w_gate, w_up, w_down, next_gain.reshape(1, d).astype(F32))


MERGE_CAST_TILE = 256
MOE_TOK_TILE = 256
MOE_ROUTE_TILES = 4
MOE_ROW_BLOCK = 512
MOE_GATHER_ROWS = 128
MOE_GATHER_TILES = 4
MOE_VMEM_LIMIT = 60 * 1024 * 1024
MOE_Y_BLOCK = 128
MOE_Y_FETCH = MOE_TOK_TILE // MOE_Y_BLOCK + 1


def _router_logits(x, w):
    xh, xm, _ = _split3(x)
    wh, wm, _ = _split3(w)
    packed = (wh.astype(F32) + pltpu.roll(wm.astype(F32), N_EXPERTS, axis=1)).astype(BF16)
    s = _dot(xh, packed) + _dot(xm, packed)
    return s + pltpu.roll(s, LANE - N_EXPERTS, axis=1)


def _route_kernel(x_ref, ng_ref, router_ref, ltri_ref, pos_ref, gate_ref, post_ref, before_ref, total_ref,
                  carry_ref):
    tm = x_ref.shape[0]

    @pl.when(pl.program_id(0) == 0)
    def _():
        carry_ref[...] = jnp.zeros_like(carry_ref)

    hn = _rms(x_ref[...], ng_ref[...])
    logits = _router_logits(hn, router_ref[...])
    lane = lax.broadcasted_iota(jnp.int32, (tm, LANE), 1)
    logits = jnp.where(lane < N_EXPERTS, logits, -jnp.inf)
    v1 = jnp.max(logits, axis=1, keepdims=True)
    i1 = jnp.min(jnp.where(logits == v1, lane, LANE), axis=1, keepdims=True)
    rest = jnp.where(lane == i1, -jnp.inf, logits)
    v2 = jnp.max(rest, axis=1, keepdims=True)
    i2 = jnp.min(jnp.where(rest == v2, lane, LANE), axis=1, keepdims=True)
    w1 = 1.0 / (1.0 + jnp.exp(v2 - v1))
    gate_ref[...] = jnp.where(lane == i1, w1, 0.0) + jnp.where(lane == i2, 1.0 - w1, 0.0)

    member = jnp.where((lane == i1) | (lane == i2), 1.0, 0.0)
    carry = carry_ref[...]
    rank = _dot(ltri_ref[...], member.astype(BF16)) + carry
    pos = jnp.where(member > 0.0, rank, -1.0)
    pos_ref[...] = pos
    post_ref[...] = pos.T[:SUBLANE, :]
    for blk in range(tm // MOE_TOK_TILE):
        before_ref[blk] = carry
        carry = carry + jnp.sum(member[blk * MOE_TOK_TILE:(blk + 1) * MOE_TOK_TILE], axis=0, keepdims=True)
    carry_ref[...] = carry
    total_ref[...] = carry


def moe_route(x, norm_gain, router):
    s, d = x.shape
    tm = min(MOE_ROUTE_TILES * MOE_TOK_TILE, s)
    nsub = tm // MOE_TOK_TILE
    nt = s // tm
    router_p = jnp.zeros((d, LANE), F32).at[:, :N_EXPERTS].set(router.astype(F32))
    ltri = _const(np.tril(np.ones((tm, tm)), -1), BF16)
    row = pl.BlockSpec((tm, LANE), lambda i: (i, 0))
    return pl.pallas_call(
        _route_kernel,
        grid=(nt,),
        in_specs=[pl.BlockSpec((tm, d), lambda i: (i, 0)), _const_spec((1, d)), _const_spec((d, LANE)),
                  _const_spec((tm, tm))],
        out_specs=[row, row, pl.BlockSpec((SUBLANE, tm), lambda i: (0, i)),
                   pl.BlockSpec((nsub, 1, LANE), lambda i: (i, 0, 0)), pl.BlockSpec((1, LANE), lambda i: (0, 0))],
        out_shape=[jax.ShapeDtypeStruct((s, LANE), F32), jax.ShapeDtypeStruct((s, LANE), F32),
                   jax.ShapeDtypeStruct((SUBLANE, s), F32), jax.ShapeDtypeStruct((nt * nsub, 1, LANE), F32),
                   jax.ShapeDtypeStruct((1, LANE), F32)],
        scratch_shapes=[pltpu.VMEM((1, LANE), F32)],
        compiler_params=_params("arbitrary"),
        name="moe_route",
    )(x, norm_gain.reshape(1, d).astype(F32), router_p, ltri)


def _moe_ffn_kernel(be_ref, r0_ref, tlo_ref, thi_ref, nv_ref, h_ref, post_ref, wg_ref, wu_ref, wd_ref,
                    y_ref, x_sc, acc_sc, *, tt, nsub):
    b = pl.program_id(0)
    f = pl.program_id(1)
    nb = pl.num_programs(0)
    valid = b < nv_ref[0]
    tmb, d = acc_sc.shape
    gr = tmb // nsub
    last_tile = h_ref.shape[0] // tt - 1

    def picked(e, want, t):
        t0 = pl.multiple_of(t * tt, tt)
        p = post_ref[pl.ds(e, 1), pl.ds(t0, tt)]
        sel = jnp.where(p == want, 1.0, 0.0).astype(BF16)
        return _dot(sel, h_ref[pl.ds(t0, tt), :])

    def gather_head(blk, sb):
        e = be_ref[blk]
        lo = tlo_ref[blk * nsub + sb]
        hi = thi_ref[blk * nsub + sb]
        want = (lax.broadcasted_iota(jnp.int32, (gr, 1), 0) + (r0_ref[blk] + sb * gr)).astype(F32)
        rows = picked(e, want, lo)
        for k in range(1, MOE_GATHER_TILES):
            rows = rows + picked(e, jnp.where(lo + k <= hi, want, -2.0), jnp.minimum(lo + k, last_tile))
        return rows.astype(BF16)

    def gather_tail(slot, blk, sb):
        e = be_ref[blk]
        want = (lax.broadcasted_iota(jnp.int32, (gr, 1), 0) + (r0_ref[blk] + sb * gr)).astype(F32)
        rows = pl.ds(pl.multiple_of(sb * gr, gr), gr)

        def more(t, carry):
            x_sc[slot, rows, :] = (x_sc[slot, rows, :].astype(F32) + picked(e, want, t)).astype(BF16)
            return carry

        lax.fori_loop(tlo_ref[blk * nsub + sb] + MOE_GATHER_TILES, thi_ref[blk * nsub + sb] + 1, more, 0)

    @pl.when((b == 0) & (f == 0))
    def _():
        for sb in range(nsub):
            x_sc[0, pl.ds(sb * gr, gr), :] = gather_head(0, sb)
            gather_tail(0, 0, sb)

    nxt = jnp.minimum(b + 1, nb - 1)
    nslot = (b + 1) % 2

    @pl.when(valid)
    def _():
        x_sc[nslot, pl.ds(pl.multiple_of(f * gr, gr), gr), :] = gather_head(nxt, f)
        xb = x_sc[b % 2]
        a = _silu(_dot(xb, wg_ref[0])) * _dot(xb, wu_ref[0])
        acc_sc[...] = jnp.where(f == 0, 0.0, acc_sc[...]) + _dot(a.astype(BF16), wd_ref[0].astype(BF16))

    @pl.when(valid & (thi_ref[nxt * nsub + f] - tlo_ref[nxt * nsub + f] >= MOE_GATHER_TILES))
    def _():
        gather_tail(nslot, nxt, f)

    @pl.when(f == nsub - 1)
    def _():
        y_ref[...] = jnp.where(valid, acc_sc[...], 0.0).astype(y_ref.dtype)


def moe_ffn(h, post, sched, w_gate, w_up, w_down):
    s, d = h.shape
    ne, _, dff = w_gate.shape
    tmb = MOE_ROW_BLOCK
    tt = min(MOE_TOK_TILE, s)
    nb = sched[0].shape[0]
    nf = tmb // MOE_GATHER_ROWS
    tf = dff // nf

    def fidx(b, f, nv):
        return jnp.where(b < nv[0], f, nf - 1)

    grid_spec = pltpu.PrefetchScalarGridSpec(
        num_scalar_prefetch=5,
        grid=(nb, nf),
        in_specs=[pl.BlockSpec((s, d), lambda b, f, *_: (0, 0), pipeline_mode=pl.Buffered(1)),
                  pl.BlockSpec((SUBLANE, s), lambda b, f, *_: (0, 0), pipeline_mode=pl.Buffered(1)),
                  pl.BlockSpec((1, d, tf), lambda b, f, be, r0, tlo, thi, nv: (be[b], 0, fidx(b, f, nv))),
                  pl.BlockSpec((1, d, tf), lambda b, f, be, r0, tlo, thi, nv: (be[b], 0, fidx(b, f, nv))),
                  pl.BlockSpec((1, tf, d), lambda b, f, be, r0, tlo, thi, nv: (be[b], fidx(b, f, nv), 0))],
        out_specs=pl.BlockSpec((tmb, d), lambda b, f, *_: (b, 0)),
        scratch_shapes=[pltpu.VMEM((2, tmb, d), BF16), pltpu.VMEM((tmb, d), F32)],
    )
    return pl.pallas_call(
        functools.partial(_moe_ffn_kernel, tt=tt, nsub=nf),
        grid_spec=grid_spec,
        out_shape=jax.ShapeDtypeStruct((nb * tmb, d), BF16),
        compiler_params=pltpu.CompilerParams(dimension_semantics=("arbitrary", "arbitrary"),
                                             vmem_limit_bytes=MOE_VMEM_LIMIT),
        name="moe_ffn",
    )(*sched, h, post, w_gate, w_up, w_down)


def _moe_combine_kernel(kb_ref, off_ref, lim_ref, x_ref, pos_ref, gate_ref, fg_ref, *rest):
    y_refs, o_ref, acc_sc = rest[:-2], rest[-2], rest[-1]
    t = pl.program_id(0)
    yb = y_refs[0].shape[0]
    pos = pos_ref[...]
    gate = gate_ref[...]
    col = lax.broadcasted_iota(jnp.int32, (1, yb), 1).astype(F32)

    def routed(e):
        pe = pos[:, e:e + 1]
        ge = gate[:, e:e + 1]
        r = jnp.where(pe >= 0.0, pe + off_ref[t * N_EXPERTS + e].astype(F32), -1.0)
        return r, ge

    def picked(e, k, r):
        sel = jnp.where(r == col + float(k * yb), 1.0, 0.0).astype(BF16)
        return _dot(sel, y_refs[MOE_Y_FETCH * e + k][...])

    col2 = lax.broadcasted_iota(jnp.int32, (1, 2 * yb), 1).astype(F32)
    acc = x_ref[...]
    for e in range(N_EXPERTS):
        r, ge = routed(e)
        pair = jnp.concatenate([y_refs[MOE_Y_FETCH * e][...], y_refs[MOE_Y_FETCH * e + 1][...]], axis=0)
        acc = acc + ge * _dot(jnp.where(r == col2, 1.0, 0.0).astype(BF16), pair)
    acc_sc[...] = acc
    for e in range(N_EXPERTS):
        for k in range(2, MOE_Y_FETCH):
            @pl.when(lim_ref[t * N_EXPERTS + e] > k * yb)
            def _(k=k, e=e):
                r, ge = routed(e)
                acc_sc[...] += ge * picked(e, k, r)
    o_ref[...] = _rms(acc_sc[...], fg_ref[...]).astype(o_ref.dtype)


def moe_combine(x, pos, gate, y, kb, off, lim, final_gain):
    s, d = x.shape
    tm = min(MOE_TOK_TILE, s)
    yb = MOE_Y_BLOCK
    last = y.shape[0] // yb - 1

    def yspec(e, k):
        def index(t, kb_r, off_r, lim_r):
            blk = jnp.minimum(kb_r[t * N_EXPERTS + e] + k, last)
            return (blk if k < 1 else jnp.where(lim_r[t * N_EXPERTS + e] > k * yb, blk, 0), 0)
        return pl.BlockSpec((yb, d), index)

    grid_spec = pltpu.PrefetchScalarGridSpec(
        num_scalar_prefetch=3,
        grid=(s // tm,),
        in_specs=[pl.BlockSpec((tm, d), lambda t, *_: (t, 0)),
                  pl.BlockSpec((tm, LANE), lambda t, *_: (t, 0)),
                  pl.BlockSpec((tm, LANE), lambda t, *_: (t, 0)),
                  pl.BlockSpec((1, d), lambda t, *_: (0, 0))]
        + [yspec(e, k) for e in range(N_EXPERTS) for k in range(MOE_Y_FETCH)],
        out_specs=pl.BlockSpec((tm, d), lambda t, *_: (t, 0)),
        scratch_shapes=[pltpu.VMEM((tm, d), F32)],
    )
    return pl.pallas_call(
        _moe_combine_kernel,
        grid_spec=grid_spec,
        out_shape=jax.ShapeDtypeStruct((s, d), F32),
        compiler_params=_params("arbitrary"),
        name="moe_combine",
    )(kb, off, lim, x, pos, gate, final_gain.reshape(1, d).astype(F32), *([y] * (MOE_Y_FETCH * N_EXPERTS)))


def _moe_schedule(before, total, s):
    tmb, yb, gr = MOE_ROW_BLOCK, MOE_Y_BLOCK, MOE_GATHER_ROWS
    nb = 2 * s // tmb + N_EXPERTS
    counts = total[0, :N_EXPERTS].astype(jnp.int32)
    nblk = (counts + tmb - 1) // tmb
    end = jnp.cumsum(nblk)
    first = end - nblk
    nvalid = end[-1]
    b = jnp.minimum(jnp.arange(nb, dtype=jnp.int32), nvalid - 1)
    blk_e = jnp.sum(b[:, None] >= end[None, :], axis=1).astype(jnp.int32)
    r0 = (b - first[blk_e]) * tmb
    cb = before[:, 0, :N_EXPERTS].astype(jnp.int32)
    r0s = (r0[:, None] + gr * jnp.arange(tmb // gr, dtype=jnp.int32)[None, :]).reshape(-1)
    cbe = cb[:, jnp.repeat(blk_e, tmb // gr)]
    tlo = (jnp.sum(cbe <= r0s[None, :], axis=0) - 1).astype(jnp.int32)
    thi = (jnp.sum(cbe < (r0s + gr)[None, :], axis=0) - 1).astype(jnp.int32)
    sched = (blk_e, r0.astype(jnp.int32), tlo, thi, nvalid.reshape(1).astype(jnp.int32))
    row_start = first[None, :] * tmb + cb
    kb = row_start // yb
    off = first[None, :] * tmb - kb * yb
    n_te = jnp.concatenate([cb[1:], counts[None, :]], axis=0) - cb
    lim = row_start - kb * yb + n_te
    flat = lambda a: a.reshape(-1).astype(jnp.int32)
    return sched, flat(kb), flat(off), flat(lim)


def moe_sparse(h, x, norm_gain, router, w_gate, w_up, w_down, final_gain):
    s, _ = x.shape
    pos, gate, post, before, total = moe_route(x, norm_gain, router)
    sched, kb, off, lim = _moe_schedule(before, total, s)
    y = moe_ffn(h, post, sched, w_gate, w_up, w_down)
    return moe_combine(x, pos, gate, y, kb, off, lim, final_gain)


FOX_F0 = 7 * BRANCH_W


def _split_w_in_kernel(wt_ref, mix_ref, f_ref, gate_ref):
    rest0 = FOX_F0 + N_HEADS
    gate0 = N_MIX_COLS + N_HEADS
    mix_ref[:, :FOX_F0] = wt_ref[pl.ds(0, FOX_F0), :].T.astype(BF16)
    mix_ref[:, FOX_F0:] = wt_ref[pl.ds(rest0, N_MIX_COLS - FOX_F0), :].T.astype(BF16)
    gate_ref[...] = wt_ref[pl.ds(gate0, N_BRANCH * D_MODEL), :].T.astype(BF16)
    f_rows = jnp.concatenate([wt_ref[pl.ds(FOX_F0, N_HEADS), :], jnp.zeros((LANE - N_HEADS, LANE), F32)], axis=0)
    f_ref[...] = f_rows.T.astype(BF16)


def _mixer_weights(w_in, layer, tr=LANE):
    _, d, cols = w_in.shape
    w_t = jnp.swapaxes(w_in, 1, 2)
    row = lambda w: pl.BlockSpec((tr, w), lambda i: (i, 0))
    return pl.pallas_call(
        _split_w_in_kernel,
        grid=(d // tr,),
        in_specs=[pl.BlockSpec((None, cols, tr), lambda i: (layer, 0, i))],
        out_specs=[row(N_MIX_COLS), row(LANE), row(N_BRANCH * D_MODEL)],
        out_shape=[jax.ShapeDtypeStruct((d, N_MIX_COLS), BF16), jax.ShapeDtypeStruct((d, LANE), BF16),
                   jax.ShapeDtypeStruct((d, N_BRANCH * D_MODEL), BF16)],
        compiler_params=_params("parallel"),
        name="split_w_in",
    )(w_t)


def kernel(x, w_in, w_branch, w_out, norm_mix_g, hgrn_lb_logits, hgrn_norm_g, fox_f_bias, pool_w, pool_scale,
           ret_gn_g, ret_gn_b, norm_ffn_g, ffn_w_gate, ffn_w_up, ffn_w_down, moe_router, moe_w_gate, moe_w_up,
           moe_w_down, final_norm_g):
    b, s, d = x.shape
    assert b == 1 and d == D_MODEL
    depth = w_in.shape[0]
    assert depth == 2, "layer 0 uses the dense FFN, layer 1 the experts and the final norm"
    xs = x.reshape(s, d)
    h = None
    out = None
    for layer in range(depth):
        w_mix, w_f, w_gate = _mixer_weights(w_in, layer)
        if layer == 0:
            proj, h = norm_matmul(xs, norm_mix_g[0], w_mix)
        else:
            proj = matmul(h, w_mix)
        ct, c_rows, fox_stats = fox_gate(h, proj, w_f, fox_f_bias[layer])
        branches = (
            hgrn2(proj, hgrn_lb_logits, hgrn_norm_g[layer], layer),
            fox_attention(proj, ct, c_rows, fox_stats),
            pool_mixer(proj, pool_w[layer], pool_scale[layer]),
            retention(proj, ret_gn_g[layer], ret_gn_b[layer]),
        )
        li = layer // 2
        if layer % 2 == 0:
            xs, h2, (ffn_wg, ffn_wu) = merge(h, branches, xs, w_gate, w_branch[layer].astype(BF16),
                                             w_out[layer].astype(BF16), norm_ffn_g[layer],
                                             cast=(ffn_w_gate[li], ffn_w_up[li]))
            xs, h = ffn_dense(h2, xs, ffn_wg, ffn_wu, ffn_w_down[li], norm_mix_g[layer + 1])
        else:
            xs, h2, (moe_wg, moe_wu) = merge(h, branches, xs, w_gate, w_branch[layer].astype(BF16),
                                             w_out[layer].astype(BF16), norm_ffn_g[layer], tm=MERGE_CAST_TILE,
                                             cast=(moe_w_gate[li], moe_w_up[li]))
            out = moe_sparse(h2, xs, norm_ffn_g[layer], moe_router[li], moe_wg, moe_wu, moe_w_down[li],
                             final_norm_g)
    return out.reshape(b, s, d)
```

```python
import functools
import math

import jax
import jax.numpy as jnp
import numpy as np
from jax import lax
from jax.experimental import pallas as pl
from jax.experimental.pallas import tpu as pltpu

D_MODEL = 1024
N_BRANCH = 4
BRANCH_W = D_MODEL // N_BRANCH
HEAD_DIM = 64
N_HEADS = BRANCH_W // HEAD_DIM
POOL_WINDOWS = (2, 4, 8, 16)
POOL_GROUP = BRANCH_W // len(POOL_WINDOWS)
POOL_HALO = 16
RET_DECAY_BASE = 5.0
ROPE_BASE = 10000.0
D_FF = 7 * D_MODEL // 2
N_EXPERTS = 8
RMS_EPS = 1e-6
LN_EPS = 1e-5
N_MIX_COLS = 12 * BRANCH_W

LANE = 128
SUBLANE = 8
VMEM_LIMIT = 56 * 1024 * 1024

HG_CHUNK = 64
HG_SUB = 16
HG_FAST_MIN_LOGDECAY = -60.0
RET_CHUNK = 256

F32 = jnp.float32
BF16 = jnp.bfloat16
NT_DIMS = (((1,), (1,)), ((), ()))


def _params(*sem):
    return pltpu.CompilerParams(dimension_semantics=sem, vmem_limit_bytes=VMEM_LIMIT)


def _const_spec(shape):
    nd = len(shape)
    return pl.BlockSpec(shape, lambda *_: (0,) * nd, pipeline_mode=pl.Buffered(1))


def _split3(x):
    hi = x.astype(BF16)
    r1 = x - hi.astype(F32)
    mid = r1.astype(BF16)
    lo = (r1 - mid.astype(F32)).astype(BF16)
    return hi, mid, lo


def _dot(a, b):
    return jnp.dot(a, b, preferred_element_type=F32)


def _dot_nt(a, b):
    return lax.dot_general(a, b, NT_DIMS, preferred_element_type=F32)


def _dot_exact_rhs(x, m_bf16, terms=3):
    return sum(_dot(part, m_bf16) for part in _split3(x)[:terms])


def _dot_exact_lhs(m_bf16, x, terms=3):
    return sum(_dot(m_bf16, part) for part in _split3(x)[:terms])


def _sigmoid(x):
    return 1.0 / (1.0 + jnp.exp(-x))


def _silu(x):
    return x * _sigmoid(x)


def _rms(x, gain):
    return x * lax.rsqrt(jnp.mean(x * x, axis=-1, keepdims=True) + RMS_EPS) * gain


def _const(a, dtype=F32):
    return jnp.asarray(np.asarray(a, np.float32), dtype)


def _head_of(n):
    return np.arange(n) // HEAD_DIM


def _head_ones():
    h = _head_of(BRANCH_W)
    return _const(h[:, None] == h[None, :], BF16)


def _head_masks():
    return _const(_head_of(BRANCH_W)[None, :] == np.arange(N_HEADS)[:, None])


def _norm_matmul_kernel(x_ref, g_ref, b_ref, o_ref, h_ref):
    @pl.when(pl.program_id(1) == 0)
    def _():
        h_ref[...] = _rms(x_ref[...], g_ref[...]).astype(h_ref.dtype)

    o_ref[...] = _dot(h_ref[...], b_ref[...]).astype(o_ref.dtype)


def norm_matmul(x, gain, b, tm=2048, tn=1024):
    m, k = x.shape
    _, n = b.shape
    tm = min(tm, m)
    return pl.pallas_call(
        _norm_matmul_kernel,
        grid=(m // tm, n // tn),
        in_specs=[pl.BlockSpec((tm, k), lambda i, j: (i, 0)), _const_spec((1, k)),
                  pl.BlockSpec((k, tn), lambda i, j: (0, j))],
        out_specs=[pl.BlockSpec((tm, tn), lambda i, j: (i, j)), pl.BlockSpec((tm, k), lambda i, j: (i, 0))],
        out_shape=[jax.ShapeDtypeStruct((m, n), BF16), jax.ShapeDtypeStruct((m, k), BF16)],
        compiler_params=_params("arbitrary", "arbitrary"),
        name="in_proj_norm",
    )(x, gain.reshape(1, k).astype(F32), b)


def _matmul_kernel(a_ref, b_ref, o_ref):
    o_ref[...] = _dot(a_ref[...], b_ref[...]).astype(o_ref.dtype)


def matmul(a, b, out_dtype=BF16, tm=2048, tn=1024):
    m, k = a.shape
    _, n = b.shape
    tm = min(tm, m)
    return pl.pallas_call(
        _matmul_kernel,
        grid=(n // tn, m // tm),
        in_specs=[pl.BlockSpec((tm, k), lambda j, i: (i, 0)),
                  pl.BlockSpec((k, tn), lambda j, i: (0, j))],
        out_specs=pl.BlockSpec((tm, tn), lambda j, i: (i, j)),
        out_shape=jax.ShapeDtypeStruct((m, n), out_dtype),
        compiler_params=_params("parallel", "parallel"),
        name="in_proj",
    )(a, b)


def _hgrn_kernel(q_ref, f_ref, i_ref, g_ref, lbl_ref, ng_ref, ones_ref, hm_ref, tril_ref, halfsum_ref, fmask_ref,
                 o_ref, st_ref, bpad, kpad, vpad, astack, lf_sc, kk_sc, o_sc, *, layer, tile):
    c, sub = HG_CHUNK, HG_SUB
    nsub = c // sub
    half = c // 2

    @pl.when(pl.program_id(0) == 0)
    def _():
        st_ref[...] = jnp.zeros_like(st_ref)
        bpad[...] = jnp.zeros_like(bpad)
        kpad[...] = jnp.zeros_like(kpad)
        vpad[...] = jnp.zeros_like(vpad)

    lbl = lbl_ref[...]
    e = jnp.exp(lbl - jnp.max(lbl, axis=0, keepdims=True))
    p = e / jnp.sum(e, axis=0, keepdims=True)
    lb = jnp.zeros((1, BRANCH_W), F32)
    for l in range(1, layer + 1):
        lb = lb + p[l:l + 1, :]

    ones_bd = ones_ref[...]
    hm = hm_ref[...]
    tril = tril_ref[...]
    row = lax.broadcasted_iota(jnp.int32, (c, 1), 0)
    row_in_sub = row % sub
    bd_mask = ones_bd.astype(F32)

    sig = _sigmoid(f_ref[...].astype(F32))
    logf_all = jnp.log(lb + (1.0 - lb) * sig)
    lf_sc[...] = _dot_exact_lhs(tril, logf_all, terms=2)
    kk_sc[...] = (1.0 - lb) * (1.0 - sig)
    min_decay = jnp.min(_dot(halfsum_ref[...], logf_all.astype(BF16)))

    def load(ci):
        r0 = pl.multiple_of(ci * c, c)
        q = q_ref[pl.ds(r0, c), :].astype(F32)
        v = i_ref[pl.ds(r0, c), :].astype(F32)
        kk = kk_sc[pl.ds(r0, c), :]
        b = lf_sc[pl.ds(r0, c), :]
        return r0, q, v, kk, b

    def finish(r0, q_decayed, v, kk, b, intra):
        st = st_ref[...]
        inter = _dot_nt(q_decayed.astype(BF16), st.astype(BF16))
        b_last = b[c - 1:c, :]
        ks_end = (kk * jnp.exp(b_last - b)).astype(BF16)
        upd = _dot(v.T.astype(BF16), ks_end)
        st_ref[...] = st * jnp.exp(b_last) + upd * bd_mask
        o_sc[pl.ds(r0, c), :] = intra + inter

    def fast_chunk(ci, carry):
        r0, q, v, kk, b = load(ci)
        second = row >= half
        m_row = b[half - 1:half, :]
        mref = jnp.where(second, m_row, 0.0)
        qp = q * jnp.exp(b - mref)
        kp = kk * jnp.exp(mref - b)
        e_m = jnp.exp(m_row)
        kaug = jnp.concatenate([kp, kp[:half, :] * e_m], axis=0)
        vaug = jnp.concatenate([v, v[:half, :]], axis=0)
        k_heads = jnp.concatenate([kaug * hm[h:h + 1, :] for h in range(N_HEADS)], axis=0).astype(BF16)
        v_heads = jnp.concatenate([vaug * hm[h:h + 1, :] for h in range(N_HEADS)], axis=0).astype(BF16)
        sc = jnp.where(fmask_ref[...] > 0.0, _dot_nt(qp.astype(BF16), k_heads), 0.0)
        intra = _dot(sc.astype(BF16), v_heads)
        finish(r0, jnp.where(second, qp * e_m, qp), v, kk, b, intra)
        return carry

    def exact_chunk(ci, carry):
        r0, q, v, kk, b = load(ci)

        bpad[pl.ds(sub, c), :] = b
        kpad[pl.ds(sub, c), :] = kk
        vpad[pl.ds(sub, c), :] = v

        for d in range(sub):
            b_d = bpad[pl.ds(sub - d, c), :]
            k_d = kpad[pl.ds(sub - d, c), :]
            a = jnp.where(row_in_sub >= d, q * k_d * jnp.exp(b - b_d), 0.0)
            astack[pl.ds(d * c, c), :] = a.astype(BF16)
        pall = _dot(astack[...], ones_bd)
        intra = jnp.zeros((c, BRANCH_W), F32)
        for d in range(sub):
            intra = intra + pall[d * c:(d + 1) * c, :] * vpad[pl.ds(sub - d, c), :]

        pieces = [jnp.zeros((sub, BRANCH_W), F32)]
        for si in range(1, nsub):
            lo = si * sub
            m_i = b[lo - 1:lo, :]
            qs = q[lo:lo + sub, :] * jnp.exp(b[lo:lo + sub, :] - m_i)
            ks = (kk[:lo, :] * jnp.exp(m_i - b[:lo, :])).astype(BF16)
            qx = jnp.concatenate([qs * hm[h:h + 1, :] for h in range(N_HEADS)], axis=0).astype(BF16)
            sc = _dot_nt(qx, ks)
            r = _dot(sc.astype(BF16), v[:lo, :].astype(BF16))
            acc = jnp.zeros((sub, BRANCH_W), F32)
            for h in range(N_HEADS):
                acc = acc + r[h * sub:(h + 1) * sub, :] * hm[h:h + 1, :]
            pieces.append(acc)
        intra = intra + jnp.concatenate(pieces, axis=0)
        finish(r0, q * jnp.exp(b), v, kk, b, intra)
        return carry

    lax.cond(min_decay >= HG_FAST_MIN_LOGDECAY,
             lambda: lax.fori_loop(0, tile // c, fast_chunk, 0, unroll=2),
             lambda: lax.fori_loop(0, tile // c, exact_chunk, 0))

    o = o_sc[...]
    ms = _dot_exact_rhs(o * o, ones_bd, terms=2) * (1.0 / HEAD_DIM)
    y = o * lax.rsqrt(ms + RMS_EPS) * ng_ref[...] * _silu(g_ref[...].astype(F32))
    o_ref[...] = y.astype(o_ref.dtype)


def hgrn2(proj, lb_logits, norm_g, layer, tile=512):
    s = proj.shape[0]
    depth = lb_logits.shape[0]
    c, sub = HG_CHUNK, HG_SUB
    half = c // 2
    tile = min(tile, s)
    col = lambda j: pl.BlockSpec((tile, BRANCH_W), lambda i, j=j: (i, j))
    pos = np.arange(tile)
    tril = _const((pos[:, None] // c == pos[None, :] // c) & (pos[None, :] <= pos[:, None]), BF16)
    nhalf = tile // half
    halfsum = _const(np.arange(tile)[None, :] // half == np.arange(nhalf)[:, None], BF16)
    t = np.arange(c)[:, None]
    col_s = np.arange(c + half)[None, :]
    same_half = (col_s < c) & (col_s // half == t // half) & (col_s <= t)
    cross = (col_s >= c) & (t >= half)
    fmask = _const(np.tile(same_half | cross, (1, N_HEADS)))
    return pl.pallas_call(
        functools.partial(_hgrn_kernel, layer=layer, tile=tile),
        grid=(s // tile,),
        in_specs=[col(0), col(1), col(2), col(3),
                  _const_spec((depth, BRANCH_W)), _const_spec((1, BRANCH_W)),
                  _const_spec((BRANCH_W, BRANCH_W)), _const_spec((N_HEADS, BRANCH_W)),
                  _const_spec((tile, tile)), _const_spec((nhalf, tile)),
                  _const_spec((c, N_HEADS * (c + half)))],
        out_specs=pl.BlockSpec((tile, BRANCH_W), lambda i: (i, 0)),
        out_shape=jax.ShapeDtypeStruct((s, BRANCH_W), BF16),
        scratch_shapes=[pltpu.VMEM((BRANCH_W, BRANCH_W), F32),
                        pltpu.VMEM((c + sub, BRANCH_W), F32),
                        pltpu.VMEM((c + sub, BRANCH_W), F32),
                        pltpu.VMEM((c + sub, BRANCH_W), F32),
                        pltpu.VMEM((sub * c, BRANCH_W), BF16),
                        pltpu.VMEM((tile, BRANCH_W), F32),
                        pltpu.VMEM((tile, BRANCH_W), F32),
                        pltpu.VMEM((tile, BRANCH_W), F32)],
        compiler_params=_params("arbitrary"),
        name="hgrn2",
    )(proj, proj, proj, proj, lb_logits.astype(F32), norm_g.reshape(1, BRANCH_W).astype(F32),
      _head_ones(), _head_masks(), tril, halfsum, fmask)


FOX_TILE = 256
FOX_GATE_BLOCKS = 2
FOX_NSTAT = 16
FOX_FIXED_MAX = 30.0
FOX_SKIP_LOG = 40.0


def _fox_gate_kernel(h_ref, q_ref, k_ref, wf_ref, bias_ref, tril_ref, ones_ref,
                     ct_ref, c_ref, stat_ref, carry_ref, kmax_ref):
    @pl.when(pl.program_id(0) == 0)
    def _():
        carry_ref[...] = jnp.zeros_like(carry_ref)
        kmax_ref[...] = jnp.zeros_like(kmax_ref)

    logit = _dot(h_ref[...], wf_ref[...]) + bias_ref[...]
    logf = jnp.minimum(logit, 0.0) - jnp.log(1.0 + jnp.exp(-jnp.abs(logit)))
    carry = carry_ref[...]
    parts = []
    for blk in range(logf.shape[0] // FOX_TILE):
        part = _dot_exact_lhs(tril_ref[...], logf[blk * FOX_TILE:(blk + 1) * FOX_TILE]) + carry
        carry = part[-1:, :]
        parts.append(part)
    cum = jnp.concatenate(parts, axis=0)
    carry_ref[...] = carry
    ct_ref[...] = cum.T[:SUBLANE, :]
    c_ref[...] = cum

    ones_bd = ones_ref[...]
    q = q_ref[...].astype(F32)
    k = k_ref[...].astype(F32)
    scale = HEAD_DIM ** -0.5
    head_lane = lax.broadcasted_iota(jnp.int32, (1, BRANCH_W), 1) // HEAD_DIM
    c_heads = jnp.zeros(q.shape, F32)
    for h in range(N_HEADS):
        c_heads = jnp.where(head_lane == h, cum[:, h:h + 1], c_heads)
    slack = 1.0 + 2.0 ** -6
    qn = jnp.sqrt(_dot((q * q).astype(BF16), ones_bd)) * (scale * slack)
    kn = jnp.sqrt(_dot((k * k).astype(BF16), ones_bd)) * slack
    diag = _dot((q * k).astype(BF16), ones_bd) * scale - (2.0 ** -6) * qn * kn
    e_row = c_heads - diag
    kmax = kmax_ref[...]
    for blk in range(q.shape[0] // FOX_TILE):
        rs = slice(blk * FOX_TILE, (blk + 1) * FOX_TILE)
        kmax = jnp.maximum(kmax, jnp.max(kn[rs], axis=0, keepdims=True))
        rows = [jnp.max(qn[rs], axis=0, keepdims=True),
                jnp.max(e_row[rs], axis=0, keepdims=True),
                kmax,
                c_heads[(blk + 1) * FOX_TILE - 1:(blk + 1) * FOX_TILE, :]]
        stat_ref[blk] = jnp.concatenate(rows + [jnp.zeros((SUBLANE - len(rows), BRANCH_W), F32)], axis=0)
    kmax_ref[...] = kmax


def fox_gate(h, proj, w_f, f_bias):
    s, d = h.shape
    tile = min(FOX_GATE_BLOCKS * FOX_TILE, s)
    nblk = tile // FOX_TILE
    bias = jnp.zeros((1, LANE), F32).at[0, :N_HEADS].set(f_bias.astype(F32))
    tril = _const(np.tril(np.ones((FOX_TILE, FOX_TILE))), BF16)
    ct, c_rows, stats = pl.pallas_call(
        _fox_gate_kernel,
        grid=(s // tile,),
        in_specs=[pl.BlockSpec((tile, d), lambda i: (i, 0)),
                  pl.BlockSpec((tile, BRANCH_W), lambda i: (i, 4)),
                  pl.BlockSpec((tile, BRANCH_W), lambda i: (i, 5)),
                  _const_spec((d, LANE)), _const_spec((1, LANE)), _const_spec((FOX_TILE, FOX_TILE)),
                  _const_spec((BRANCH_W, BRANCH_W))],
        out_specs=[pl.BlockSpec((SUBLANE, tile), lambda i: (0, i)),
                   pl.BlockSpec((tile, LANE), lambda i: (i, 0)),
                   pl.BlockSpec((nblk, SUBLANE, BRANCH_W), lambda i: (i, 0, 0))],
        out_shape=[jax.ShapeDtypeStruct((SUBLANE, s), F32),
                   jax.ShapeDtypeStruct((s, LANE), F32),
                   jax.ShapeDtypeStruct((s // FOX_TILE, SUBLANE, BRANCH_W), F32)],
        scratch_shapes=[pltpu.VMEM((1, LANE), F32), pltpu.VMEM((1, BRANCH_W), F32)],
        compiler_params=_params("arbitrary"),
        name="fox_gate",
    )(h, proj, proj, w_f, bias, tril, _head_ones())
    return ct, c_rows, stats[:, :4, ::HEAD_DIM].reshape(-1)


def _fox_kernel(stat_ref, q_ref, k_ref, v_ref, ct_ref, c_ref, hm_ref, o_ref, m_sc, l_sc, acc_sc, *, tq):
    i = pl.program_id(0)
    q0 = pl.multiple_of(i * tq, tq)
    hm = hm_ref[...]
    q = q_ref[...].astype(F32) * (HEAD_DIM ** -0.5)
    qh = [(q * hm[h:h + 1, :]).astype(BF16) for h in range(N_HEADS)]
    c_q0 = ct_ref[:, pl.ds(q0, tq)][:, 0:1]

    first = i
    for h in range(N_HEADS):
        qmax = stat_ref[i * FOX_NSTAT + h]
        emax = stat_ref[i * FOX_NSTAT + N_HEADS + h]

        def needed(j, h=h, qmax=qmax, emax=emax):
            jc = jnp.maximum(j, 0)
            bound = (qmax * stat_ref[jc * FOX_NSTAT + 2 * N_HEADS + h] + emax
                     - stat_ref[jc * FOX_NSTAT + 3 * N_HEADS + h])
            return (j >= 0) & (bound >= -FOX_SKIP_LOG)

        last_dropped = lax.while_loop(needed, lambda j: j - 1, i - 1)
        first = jnp.minimum(first, last_dropped + 1)

    l_sc[...] = jnp.zeros_like(l_sc)
    acc_sc[...] = jnp.zeros_like(acc_sc)

    def causal(sc):
        r = lax.broadcasted_iota(jnp.int32, (tq, tq), 0)
        cidx = lax.broadcasted_iota(jnp.int32, (tq, tq), 1)
        return jnp.where(cidx <= r, sc, -jnp.inf)

    def online_block(s0, diagonal):
        kb = k_ref[pl.ds(s0, tq), :]
        vb = v_ref[pl.ds(s0, tq), :]
        bias = c_q0 - ct_ref[:, pl.ds(s0, tq)]
        for h in range(N_HEADS):
            sc = _dot_nt(qh[h], kb) + bias[h:h + 1, :]
            if diagonal:
                sc = causal(sc)
            m_prev = m_sc[h]
            m_new = jnp.maximum(m_prev, jnp.max(sc, axis=1, keepdims=True))
            alpha = jnp.exp(m_prev - m_new)
            p = jnp.exp(sc - jnp.tile(m_new, (1, tq // LANE)))
            l_sc[h] = alpha * l_sc[h] + jnp.sum(p, axis=1, keepdims=True)
            acc_sc[h] = acc_sc[h] * jnp.tile(alpha, (1, BRANCH_W // LANE)) + _dot(p.astype(BF16), vb)
            m_sc[h] = m_new

    tops = [stat_ref[i * FOX_NSTAT + h] * stat_ref[i * FOX_NSTAT + 2 * N_HEADS + h] for h in range(N_HEADS)]
    c_tile = c_ref[...]
    shift = [c_tile[:, h:h + 1] - c_q0[h:h + 1, :] - tops[h] for h in range(N_HEADS)]

    def fixed_block(s0, diagonal):
        kb = k_ref[pl.ds(s0, tq), :]
        vb = v_ref[pl.ds(s0, tq), :]
        bias = c_q0 - ct_ref[:, pl.ds(s0, tq)]
        for h in range(N_HEADS):
            sc = _dot_nt(qh[h], kb) + bias[h:h + 1, :] + shift[h]
            if diagonal:
                sc = causal(sc)
            p = jnp.exp(sc)
            l_sc[h] += p[:, :LANE] + p[:, LANE:]
            acc_sc[h] += _dot(p.astype(BF16), vb)

    def run(block, row_sum):
        lax.fori_loop(first, i, lambda j, carry: (block(pl.multiple_of(j * tq, tq), False), carry)[1], 0)
        block(q0, True)
        out = jnp.zeros((tq, BRANCH_W), F32)
        for h in range(N_HEADS):
            out = out + acc_sc[h] * hm[h:h + 1, :] / row_sum(l_sc[h])
        o_ref[...] = out.astype(o_ref.dtype)

    def run_online():
        m_sc[...] = jnp.full_like(m_sc, -jnp.inf)
        run(online_block, lambda l: jnp.tile(l, (1, BRANCH_W // LANE)))

    def run_fixed():
        run(fixed_block, lambda l: jnp.sum(l, axis=1, keepdims=True))

    lax.cond(functools.reduce(jnp.maximum, tops) <= FOX_FIXED_MAX, run_fixed, run_online)


def fox_attention(proj, ct, c_rows, stats):
    s = proj.shape[0]
    tq = min(FOX_TILE, s)
    full = lambda j: pl.BlockSpec((s, BRANCH_W), lambda i, j=j: (0, j), pipeline_mode=pl.Buffered(1))
    return pl.pallas_call(
        functools.partial(_fox_kernel, tq=tq),
        grid=(s // tq,),
        in_specs=[pl.BlockSpec(memory_space=pltpu.SMEM),
                  pl.BlockSpec((tq, BRANCH_W), lambda i: (i, 4)), full(5), full(6),
                  _const_spec((SUBLANE, s)), pl.BlockSpec((tq, LANE), lambda i: (i, 0)),
                  _const_spec((N_HEADS, BRANCH_W))],
        out_specs=pl.BlockSpec((tq, BRANCH_W), lambda i: (i, 0)),
        out_shape=jax.ShapeDtypeStruct((s, BRANCH_W), BF16),
        scratch_shapes=[pltpu.VMEM((N_HEADS, tq, LANE), F32),
                        pltpu.VMEM((N_HEADS, tq, LANE), F32),
                        pltpu.VMEM((N_HEADS, tq, BRANCH_W), F32)],
        compiler_params=_params("parallel"),
        name="fox_attention",
    )(stats, proj, proj, proj, ct, c_rows, _head_masks())


def _pool_kernel(u_ref, w_ref, scale_ref, o_ref, ext, *, tile):
    i = pl.program_id(0)

    @pl.when(i == 0)
    def _():
        ext[pl.ds(0, POOL_HALO), :] = jnp.zeros((POOL_HALO, BRANCH_W), F32)

    u = u_ref[...].astype(F32)
    ext[pl.ds(POOL_HALO, tile), :] = u
    pos = (i * tile + lax.broadcasted_iota(jnp.int32, (tile, 1), 0) + 1).astype(F32)
    halves = []
    for half in range(BRANCH_W // LANE):
        lanes = pl.ds(half * LANE, LANE)
        w_small, w_big = POOL_WINDOWS[2 * half], POOL_WINDOWS[2 * half + 1]
        run = u[:, half * LANE:(half + 1) * LANE]
        sums = {}
        for j in range(1, w_big):
            if j == w_small:
                sums[w_small] = run
            run = run + ext[pl.ds(POOL_HALO - j, tile), lanes]
        sums[w_big] = run
        lane = lax.broadcasted_iota(jnp.int32, (1, LANE), 1)
        small = lane < POOL_GROUP
        total = jnp.where(small, sums[w_small], sums[w_big])
        count = jnp.where(small, jnp.minimum(pos, float(w_small)), jnp.minimum(pos, float(w_big)))
        halves.append(total / count)
    mean = jnp.concatenate(halves, axis=1)
    d = (mean - u).astype(BF16)
    y = _dot(d, w_ref[...]) * scale_ref[...]
    o_ref[...] = y.astype(o_ref.dtype)
    ext[pl.ds(0, POOL_HALO), :] = u[tile - POOL_HALO:, :]


def pool_mixer(proj, w_pool, scale, tile=2048):
    s = proj.shape[0]
    tile = min(tile, s)
    ng = len(POOL_WINDOWS)
    w_bd = jnp.zeros((BRANCH_W, BRANCH_W), F32)
    for gi in range(ng):
        lo = gi * POOL_GROUP
        w_bd = w_bd.at[lo:lo + POOL_GROUP, lo:lo + POOL_GROUP].set(w_pool[gi].astype(F32))
    return pl.pallas_call(
        functools.partial(_pool_kernel, tile=tile),
        grid=(s // tile,),
        in_specs=[pl.BlockSpec((tile, BRANCH_W), lambda i: (i, 7)),
                  _const_spec((BRANCH_W, BRANCH_W)), _const_spec((1, BRANCH_W))],
        out_specs=pl.BlockSpec((tile, BRANCH_W), lambda i: (i, 0)),
        out_shape=jax.ShapeDtypeStruct((s, BRANCH_W), BF16),
        scratch_shapes=[pltpu.VMEM((tile + POOL_HALO, BRANCH_W), F32)],
        compiler_params=_params("arbitrary"),
        name="pool_mixer",
    )(proj, w_bd.astype(BF16), scale.reshape(1, BRANCH_W).astype(F32))


def _ret_kernel(q_ref, k_ref, v_ref, g_ref, cos_ref, sin_ref, perm_ref, ones_ref, hm_ref,
                dstack_ref, xi_ref, zeta_ref, gc_ref, gng_ref, gnb_ref, o_ref, st_ref, *, tile):
    c = RET_CHUNK

    @pl.when(pl.program_id(0) == 0)
    def _():
        st_ref[...] = jnp.zeros_like(st_ref)

    perm = perm_ref[...]
    ones_bd = ones_ref[...]
    bd_mask = ones_bd.astype(F32)
    hm = hm_ref[...]

    cos = cos_ref[...]
    sin = sin_ref[...]
    q_all = q_ref[...]
    k_all = k_ref[...]
    qr_all = q_all.astype(F32) * cos + _dot(q_all, perm) * sin
    kr_all = (k_all.astype(F32) * cos + _dot(k_all, perm) * sin) * (HEAD_DIM ** -0.5)

    outs = []
    for ci in range(tile // c):
        r0 = ci * c
        qr = qr_all[r0:r0 + c, :]
        kr = kr_all[r0:r0 + c, :]
        v = v_ref[pl.ds(r0, c), :]

        qx = jnp.concatenate([qr * hm[h:h + 1, :] for h in range(N_HEADS)], axis=0).astype(BF16)
        sc = _dot_nt(qx, kr.astype(BF16)) * dstack_ref[...]
        r = _dot(sc.astype(BF16), v)
        intra = jnp.zeros((c, BRANCH_W), F32)
        for h in range(N_HEADS):
            intra = intra + r[h * c:(h + 1) * c, :] * hm[h:h + 1, :]

        st = st_ref[...]
        inter = _dot_nt((qr * xi_ref[...]).astype(BF16), st.astype(BF16))
        upd = _dot(v.astype(F32).T.astype(BF16), (kr * zeta_ref[...]).astype(BF16))
        st_ref[...] = st * gc_ref[...] + upd * bd_mask
        outs.append(intra + inter)

    o = jnp.concatenate(outs, axis=0)
    mu = _dot_exact_rhs(o, ones_bd, terms=2) * (1.0 / HEAD_DIM)
    cen = o - mu
    var = _dot_exact_rhs(cen * cen, ones_bd, terms=2) * (1.0 / HEAD_DIM)
    y = cen * lax.rsqrt(var + LN_EPS) * gng_ref[...] + gnb_ref[...]
    o_ref[...] = (y * _silu(g_ref[...].astype(F32))).astype(o_ref.dtype)


def _rope_tables(s):
    half = HEAD_DIM // 2
    pos = np.arange(s, dtype=np.float64)
    inv_freq = ROPE_BASE ** (-np.arange(half, dtype=np.float64) / half)
    lane = np.arange(BRANCH_W)
    ang = pos[:, None] * inv_freq[lane % half][None, :]
    sign = np.where(lane % HEAD_DIM < half, -1.0, 1.0)
    return _const(np.cos(ang)), _const(np.sin(ang) * sign[None, :])


def _ret_constants():
    c = RET_CHUNK
    half = HEAD_DIM // 2
    lane = np.arange(BRANCH_W)
    partner = np.where(lane % HEAD_DIM < half, lane + half, lane - half)
    perm = lane[:, None] == partner[None, :]
    log_gamma = np.log1p(-np.exp2(-RET_DECAY_BASE - np.arange(N_HEADS, dtype=np.float64)))
    ci = np.arange(c, dtype=np.float64)
    diff = ci[:, None] - ci[None, :]
    intra = np.where(diff >= 0, np.exp(diff * log_gamma[:, None, None]), 0.0)
    dstack = intra.reshape(N_HEADS * c, c)
    lg_lane = np.repeat(log_gamma, HEAD_DIM)[None, :]
    xi = np.exp((ci[:, None] + 1.0) * lg_lane)
    zeta = np.exp((c - 1.0 - ci[:, None]) * lg_lane)
    gc = np.exp(c * lg_lane)
    return _const(perm, BF16), _const(dstack), _const(xi), _const(zeta), _const(gc)


def retention(proj, gn_g, gn_b, tile=1024):
    s = proj.shape[0]
    c = RET_CHUNK
    tile = min(tile, s)
    cos_t, sin_t = _rope_tables(s)
    perm, dstack, xi, zeta, gc = _ret_constants()
    col = lambda j: pl.BlockSpec((tile, BRANCH_W), lambda i, j=j: (i, j))
    row = pl.BlockSpec((tile, BRANCH_W), lambda i: (i, 0))
    return pl.pallas_call(
        functools.partial(_ret_kernel, tile=tile),
        grid=(s // tile,),
        in_specs=[col(8), col(9), col(10), col(11), row, row,
                  _const_spec((BRANCH_W, BRANCH_W)), _const_spec((BRANCH_W, BRANCH_W)),
                  _const_spec((N_HEADS, BRANCH_W)), _const_spec((N_HEADS * c, c)),
                  _const_spec((c, BRANCH_W)), _const_spec((c, BRANCH_W)), _const_spec((1, BRANCH_W)),
                  _const_spec((1, BRANCH_W)), _const_spec((1, BRANCH_W))],
        out_specs=row,
        out_shape=jax.ShapeDtypeStruct((s, BRANCH_W), BF16),
        scratch_shapes=[pltpu.VMEM((BRANCH_W, BRANCH_W), F32)],
        compiler_params=_params("arbitrary"),
        name="retention",
    )(proj, proj, proj, proj, cos_t, sin_t, perm, _head_ones(), _head_masks(), dstack, xi, zeta, gc,
      gn_g.reshape(1, BRANCH_W).astype(F32), gn_b.reshape(1, BRANCH_W).astype(F32))


def _merge_kernel(h_ref, o0_ref, o1_ref, o2_ref, o3_ref, x_ref, wg_ref, wb_ref, wo_ref, g_ref, *rest):
    n_cast = (len(rest) - 2) // 2
    cast_in, (xo_ref, ho_ref), cast_out = rest[:n_cast], rest[n_cast:n_cast + 2], rest[n_cast + 2:]
    for src, dst in zip(cast_in, cast_out):
        dst[...] = src[...].astype(dst.dtype)
    h = h_ref[...]
    merged = jnp.zeros(x_ref.shape, F32)
    for bi, o_ref in enumerate((o0_ref, o1_ref, o2_ref, o3_ref)):
        gate = _sigmoid(_dot(h, wg_ref[:, bi * D_MODEL:(bi + 1) * D_MODEL]))
        merged = merged + gate * _dot(o_ref[...], wb_ref[bi])
    x_new = x_ref[...] + _dot(merged.astype(BF16), wo_ref[...])
    xo_ref[...] = x_new
    ho_ref[...] = _rms(x_new, g_ref[...]).astype(ho_ref.dtype)


def merge(h, branches, x, w_gate, w_branch, w_out, next_gain, tm=512, cast=()):
    s, d = x.shape
    tm = min(tm, s)
    steps = s // tm
    row = lambda w: pl.BlockSpec((tm, w), lambda i: (i, 0))
    flat = [c.reshape(-1, c.shape[-1]) for c in cast]
    slab = lambda c: pl.BlockSpec((c.shape[0] // steps, c.shape[1]), lambda i: (i, 0))
    outs = pl.pallas_call(
        _merge_kernel,
        grid=(steps,),
        in_specs=[row(d), row(BRANCH_W), row(BRANCH_W), row(BRANCH_W), row(BRANCH_W), row(d),
                  _const_spec((d, N_BRANCH * d)), _const_spec((N_BRANCH, BRANCH_W, d)),
                  _const_spec((d, d)), _const_spec((1, d))] + [slab(c) for c in flat],
        out_specs=[row(d), row(d)] + [slab(c) for c in flat],
        out_shape=[jax.ShapeDtypeStruct((s, d), F32), jax.ShapeDtypeStruct((s, d), BF16)]
        + [jax.ShapeDtypeStruct(c.shape, BF16) for c in flat],
        compiler_params=_params("parallel"),
        name="merge",
    )(h, *branches, x, w_gate, w_branch, w_out, next_gain.reshape(1, d).astype(F32), *flat)
    return outs[0], outs[1], [o.reshape(c.shape) for o, c in zip(outs[2:], cast)]


def _ffn_kernel(h_ref, x_ref, wg_ref, wu_ref, wd_ref, g_ref, xo_ref, ho_ref, acc_ref):
    f = pl.program_id(1)

    @pl.when(f == 0)
    def _():
        acc_ref[...] = jnp.zeros_like(acc_ref)

    h = h_ref[...]
    a = _silu(_dot(h, wg_ref[...].astype(BF16))) * _dot(h, wu_ref[...].astype(BF16))
    acc_ref[...] += _dot(a.astype(BF16), wd_ref[...].astype(BF16))

    @pl.when(f == pl.num_programs(1) - 1)
    def _():
        x_new = x_ref[...] + acc_ref[...]
        xo_ref[...] = x_new
        ho_ref[...] = _rms(x_new, g_ref[...]).astype(ho_ref.dtype)


def ffn_dense(h, x, w_gate, w_up, w_down, next_gain, tm=1024, tf=512):
    s, d = x.shape
    tm = min(tm, s)
    dff = w_gate.shape[1]
    row = lambda: pl.BlockSpec((tm, d), lambda i, f: (i, 0))
    return pl.pallas_call(
        _ffn_kernel,
        grid=(s // tm, dff // tf),
        in_specs=[row(), row(),
                  pl.BlockSpec((d, tf), lambda i, f: (0, f)),
                  pl.BlockSpec((d, tf), lambda i, f: (0, f)),
                  pl.BlockSpec((tf, d), lambda i, f: (f, 0)),
                  _const_spec((1, d))],
        out_specs=[row(), row()],
        out_shape=[jax.ShapeDtypeStruct((s, d), F32), jax.ShapeDtypeStruct((s, d), BF16)],
        scratch_shapes=[pltpu.VMEM((tm, d), F32)],
        compiler_params=_params("parallel", "arbitrary"),
        name="ffn_dense",
    )(h, x, w_gate, w_up, w_down, next_gain.reshape(1, d).astype(F32))


MERGE_CAST_TILE = 256
MOE_TOK_TILE = 256
MOE_ROUTE_TILES = 4
MOE_ROW_BLOCK = 512
MOE_GATHER_ROWS = 128
MOE_GATHER_TILES = 4
MOE_VMEM_LIMIT = 60 * 1024 * 1024
MOE_Y_BLOCK = 128
MOE_Y_FETCH = MOE_TOK_TILE // MOE_Y_BLOCK + 1


def _router_logits(x, w):
    xh, xm, _ = _split3(x)
    wh, wm, _ = _split3(w)
    packed = (wh.astype(F32) + pltpu.roll(wm.astype(F32), N_EXPERTS, axis=1)).astype(BF16)
    s = _dot(xh, packed) + _dot(xm, packed)
    return s + pltpu.roll(s, LANE - N_EXPERTS, axis=1)


def _route_kernel(x_ref, ng_ref, router_ref, ltri_ref, pos_ref, gate_ref, post_ref, before_ref, total_ref,
                  carry_ref):
    tm = x_ref.shape[0]

    @pl.when(pl.program_id(0) == 0)
    def _():
        carry_ref[...] = jnp.zeros_like(carry_ref)

    hn = _rms(x_ref[...], ng_ref[...])
    logits = _router_logits(hn, router_ref[...])
    lane = lax.broadcasted_iota(jnp.int32, (tm, LANE), 1)
    logits = jnp.where(lane < N_EXPERTS, logits, -jnp.inf)
    v1 = jnp.max(logits, axis=1, keepdims=True)
    i1 = jnp.min(jnp.where(logits == v1, lane, LANE), axis=1, keepdims=True)
    rest = jnp.where(lane == i1, -jnp.inf, logits)
    v2 = jnp.max(rest, axis=1, keepdims=True)
    i2 = jnp.min(jnp.where(rest == v2, lane, LANE), axis=1, keepdims=True)
    w1 = 1.0 / (1.0 + jnp.exp(v2 - v1))
    gate_ref[...] = jnp.where(lane == i1, w1, 0.0) + jnp.where(lane == i2, 1.0 - w1, 0.0)

    member = jnp.where((lane == i1) | (lane == i2), 1.0, 0.0)
    carry = carry_ref[...]
    ranks = []
    for blk in range(tm // MOE_TOK_TILE):
        before_ref[blk] = carry
        mb = member[blk * MOE_TOK_TILE:(blk + 1) * MOE_TOK_TILE]
        ranks.append(_dot(ltri_ref[...], mb.astype(BF16)) + carry)
        carry = carry + jnp.sum(mb, axis=0, keepdims=True)
    pos = jnp.where(member > 0.0, jnp.concatenate(ranks, axis=0), -1.0)
    pos_ref[...] = pos
    post_ref[...] = pos.T[:SUBLANE, :]
    carry_ref[...] = carry
    total_ref[...] = carry


def moe_route(x, norm_gain, router):
    s, d = x.shape
    tm = min(MOE_ROUTE_TILES * MOE_TOK_TILE, s)
    nsub = tm // MOE_TOK_TILE
    nt = s // tm
    router_p = jnp.zeros((d, LANE), F32).at[:, :N_EXPERTS].set(router.astype(F32))
    tt = min(MOE_TOK_TILE, s)
    ltri = _const(np.tril(np.ones((tt, tt)), -1), BF16)
    row = pl.BlockSpec((tm, LANE), lambda i: (i, 0))
    return pl.pallas_call(
        _route_kernel,
        grid=(nt,),
        in_specs=[pl.BlockSpec((tm, d), lambda i: (i, 0)), _const_spec((1, d)), _const_spec((d, LANE)),
                  _const_spec((tt, tt))],
        out_specs=[row, row, pl.BlockSpec((SUBLANE, tm), lambda i: (0, i)),
                   pl.BlockSpec((nsub, 1, LANE), lambda i: (i, 0, 0)), pl.BlockSpec((1, LANE), lambda i: (0, 0))],
        out_shape=[jax.ShapeDtypeStruct((s, LANE), F32), jax.ShapeDtypeStruct((s, LANE), F32),
                   jax.ShapeDtypeStruct((SUBLANE, s), F32), jax.ShapeDtypeStruct((nt * nsub, 1, LANE), F32),
                   jax.ShapeDtypeStruct((1, LANE), F32)],
        scratch_shapes=[pltpu.VMEM((1, LANE), F32)],
        compiler_params=_params("arbitrary"),
        name="moe_route",
    )(x, norm_gain.reshape(1, d).astype(F32), router_p, ltri)


def _moe_ffn_kernel(be_ref, r0_ref, tlo_ref, thi_ref, nv_ref, h_ref, post_ref, wg_ref, wu_ref, wd_ref,
                    y_ref, x_sc, acc_sc, *, tt, nsub):
    b = pl.program_id(0)
    f = pl.program_id(1)
    nb = pl.num_programs(0)
    valid = b < nv_ref[0]
    tmb, d = acc_sc.shape
    gr = tmb // nsub
    last_tile = h_ref.shape[0] // tt - 1

    def picked(e, want, t):
        t0 = pl.multiple_of(t * tt, tt)
        p = post_ref[pl.ds(e, 1), pl.ds(t0, tt)]
        sel = jnp.where(p == want, 1.0, 0.0).astype(BF16)
        return _dot(sel, h_ref[pl.ds(t0, tt), :])

    def gather_head(blk, sb):
        e = be_ref[blk]
        lo = tlo_ref[blk * nsub + sb]
        hi = thi_ref[blk * nsub + sb]
        want = (lax.broadcasted_iota(jnp.int32, (gr, 1), 0) + (r0_ref[blk] + sb * gr)).astype(F32)
        rows = picked(e, want, lo)
        for k in range(1, MOE_GATHER_TILES):
            rows = rows + picked(e, jnp.where(lo + k <= hi, want, -2.0), jnp.minimum(lo + k, last_tile))
        return rows.astype(BF16)

    def gather_tail(slot, blk, sb):
        e = be_ref[blk]
        want = (lax.broadcasted_iota(jnp.int32, (gr, 1), 0) + (r0_ref[blk] + sb * gr)).astype(F32)
        rows = pl.ds(pl.multiple_of(sb * gr, gr), gr)

        def more(t, carry):
            x_sc[slot, rows, :] = (x_sc[slot, rows, :].astype(F32) + picked(e, want, t)).astype(BF16)
            return carry

        lax.fori_loop(tlo_ref[blk * nsub + sb] + MOE_GATHER_TILES, thi_ref[blk * nsub + sb] + 1, more, 0)

    @pl.when((b == 0) & (f == 0))
    def _():
        for sb in range(nsub):
            x_sc[0, pl.ds(sb * gr, gr), :] = gather_head(0, sb)
            gather_tail(0, 0, sb)

    nxt = jnp.minimum(b + 1, nb - 1)
    nslot = (b + 1) % 2

    @pl.when(valid)
    def _():
        x_sc[nslot, pl.ds(pl.multiple_of(f * gr, gr), gr), :] = gather_head(nxt, f)
        xb = x_sc[b % 2]
        a = _silu(_dot(xb, wg_ref[0])) * _dot(xb, wu_ref[0])
        acc_sc[...] = jnp.where(f == 0, 0.0, acc_sc[...]) + _dot(a.astype(BF16), wd_ref[0].astype(BF16))

    @pl.when(valid & (thi_ref[nxt * nsub + f] - tlo_ref[nxt * nsub + f] >= MOE_GATHER_TILES))
    def _():
        gather_tail(nslot, nxt, f)

    @pl.when(f == nsub - 1)
    def _():
        y_ref[...] = jnp.where(valid, acc_sc[...], 0.0).astype(y_ref.dtype)


def moe_ffn(h, post, sched, w_gate, w_up, w_down):
    s, d = h.shape
    ne, _, dff = w_gate.shape
    tmb = MOE_ROW_BLOCK
    tt = min(MOE_TOK_TILE, s)
    nb = sched[0].shape[0]
    nf = tmb // MOE_GATHER_ROWS
    tf = dff // nf

    def fidx(b, f, nv):
        return jnp.where(b < nv[0], f, nf - 1)

    grid_spec = pltpu.PrefetchScalarGridSpec(
        num_scalar_prefetch=5,
        grid=(nb, nf),
        in_specs=[pl.BlockSpec((s, d), lambda b, f, *_: (0, 0), pipeline_mode=pl.Buffered(1)),
                  pl.BlockSpec((SUBLANE, s), lambda b, f, *_: (0, 0), pipeline_mode=pl.Buffered(1)),
                  pl.BlockSpec((1, d, tf), lambda b, f, be, r0, tlo, thi, nv: (be[b], 0, fidx(b, f, nv))),
                  pl.BlockSpec((1, d, tf), lambda b, f, be, r0, tlo, thi, nv: (be[b], 0, fidx(b, f, nv))),
                  pl.BlockSpec((1, tf, d), lambda b, f, be, r0, tlo, thi, nv: (be[b], fidx(b, f, nv), 0))],
        out_specs=pl.BlockSpec((tmb, d), lambda b, f, *_: (b, 0)),
        scratch_shapes=[pltpu.VMEM((2, tmb, d), BF16), pltpu.VMEM((tmb, d), F32)],
    )
    return pl.pallas_call(
        functools.partial(_moe_ffn_kernel, tt=tt, nsub=nf),
        grid_spec=grid_spec,
        out_shape=jax.ShapeDtypeStruct((nb * tmb, d), BF16),
        compiler_params=pltpu.CompilerParams(dimension_semantics=("arbitrary", "arbitrary"),
                                             vmem_limit_bytes=MOE_VMEM_LIMIT),
        name="moe_ffn",
    )(*sched, h, post, w_gate, w_up, w_down)


def _moe_combine_kernel(kb_ref, off_ref, lim_ref, x_ref, pos_ref, gate_ref, fg_ref, *rest):
    y_refs, o_ref, acc_sc = rest[:-2], rest[-2], rest[-1]
    t = pl.program_id(0)
    yb = y_refs[0].shape[0]
    pos = pos_ref[...]
    gate = gate_ref[...]
    col = lax.broadcasted_iota(jnp.int32, (1, yb), 1).astype(F32)

    def routed(e):
        pe = pos[:, e:e + 1]
        ge = gate[:, e:e + 1]
        r = jnp.where(pe >= 0.0, pe + off_ref[t * N_EXPERTS + e].astype(F32), -1.0)
        return r, ge

    def picked(e, k, r):
        sel = jnp.where(r == col + float(k * yb), 1.0, 0.0).astype(BF16)
        return _dot(sel, y_refs[MOE_Y_FETCH * e + k][...])

    col2 = lax.broadcasted_iota(jnp.int32, (1, 2 * yb), 1).astype(F32)
    acc = x_ref[...]
    for e in range(N_EXPERTS):
        r, ge = routed(e)
        pair = jnp.concatenate([y_refs[MOE_Y_FETCH * e][...], y_refs[MOE_Y_FETCH * e + 1][...]], axis=0)
        acc = acc + ge * _dot(jnp.where(r == col2, 1.0, 0.0).astype(BF16), pair)
    acc_sc[...] = acc
    for e in range(N_EXPERTS):
        for k in range(2, MOE_Y_FETCH):
            @pl.when(lim_ref[t * N_EXPERTS + e] > k * yb)
            def _(k=k, e=e):
                r, ge = routed(e)
                acc_sc[...] += ge * picked(e, k, r)
    o_ref[...] = _rms(acc_sc[...], fg_ref[...]).astype(o_ref.dtype)


def moe_combine(x, pos, gate, y, kb, off, lim, final_gain):
    s, d = x.shape
    tm = min(MOE_TOK_TILE, s)
    yb = MOE_Y_BLOCK
    last = y.shape[0] // yb - 1

    def yspec(e, k):
        def index(t, kb_r, off_r, lim_r):
            blk = jnp.minimum(kb_r[t * N_EXPERTS + e] + k, last)
            return (blk if k < 1 else jnp.where(lim_r[t * N_EXPERTS + e] > k * yb, blk, 0), 0)
        return pl.BlockSpec((yb, d), index)

    grid_spec = pltpu.PrefetchScalarGridSpec(
        num_scalar_prefetch=3,
        grid=(s // tm,),
        in_specs=[pl.BlockSpec((tm, d), lambda t, *_: (t, 0)),
                  pl.BlockSpec((tm, LANE), lambda t, *_: (t, 0)),
                  pl.BlockSpec((tm, LANE), lambda t, *_: (t, 0)),
                  pl.BlockSpec((1, d), lambda t, *_: (0, 0))]
        + [yspec(e, k) for e in range(N_EXPERTS) for k in range(MOE_Y_FETCH)],
        out_specs=pl.BlockSpec((tm, d), lambda t, *_: (t, 0)),
        scratch_shapes=[pltpu.VMEM((tm, d), F32)],
    )
    return pl.pallas_call(
        _moe_combine_kernel,
        grid_spec=grid_spec,
        out_shape=jax.ShapeDtypeStruct((s, d), F32),
        compiler_params=_params("arbitrary"),
        name="moe_combine",
    )(kb, off, lim, x, pos, gate, final_gain.reshape(1, d).astype(F32), *([y] * (MOE_Y_FETCH * N_EXPERTS)))


def _moe_schedule(before, total, s):
    tmb, yb, gr = MOE_ROW_BLOCK, MOE_Y_BLOCK, MOE_GATHER_ROWS
    nb = 2 * s // tmb + N_EXPERTS
    counts = total[0, :N_EXPERTS].astype(jnp.int32)
    nblk = (counts + tmb - 1) // tmb
    end = jnp.cumsum(nblk)
    first = end - nblk
    nvalid = end[-1]
    b = jnp.minimum(jnp.arange(nb, dtype=jnp.int32), nvalid - 1)
    blk_e = jnp.sum(b[:, None] >= end[None, :], axis=1).astype(jnp.int32)
    r0 = (b - first[blk_e]) * tmb
    cb = before[:, 0, :N_EXPERTS].astype(jnp.int32)
    r0s = (r0[:, None] + gr * jnp.arange(tmb // gr, dtype=jnp.int32)[None, :]).reshape(-1)
    cbe = cb[:, jnp.repeat(blk_e, tmb // gr)]
    tlo = (jnp.sum(cbe <= r0s[None, :], axis=0) - 1).astype(jnp.int32)
    thi = (jnp.sum(cbe < (r0s + gr)[None, :], axis=0) - 1).astype(jnp.int32)
    sched = (blk_e, r0.astype(jnp.int32), tlo, thi, nvalid.reshape(1).astype(jnp.int32))
    row_start = first[None, :] * tmb + cb
    kb = row_start // yb
    off = first[None, :] * tmb - kb * yb
    n_te = jnp.concatenate([cb[1:], counts[None, :]], axis=0) - cb
    lim = row_start - kb * yb + n_te
    flat = lambda a: a.reshape(-1).astype(jnp.int32)
    return sched, flat(kb), flat(off), flat(lim)


def moe_sparse(h, x, norm_gain, router, w_gate, w_up, w_down, final_gain):
    s, _ = x.shape
    pos, gate, post, before, total = moe_route(x, norm_gain, router)
    sched, kb, off, lim = _moe_schedule(before, total, s)
    y = moe_ffn(h, post, sched, w_gate, w_up, w_down)
    return moe_combine(x, pos, gate, y, kb, off, lim, final_gain)


FOX_F0 = 7 * BRANCH_W


def _split_w_in_kernel(wt_ref, mix_ref, f_ref, gate_ref):
    rest0 = FOX_F0 + N_HEADS
    gate0 = N_MIX_COLS + N_HEADS
    mix_ref[:, :FOX_F0] = wt_ref[pl.ds(0, FOX_F0), :].T.astype(BF16)
    mix_ref[:, FOX_F0:] = wt_ref[pl.ds(rest0, N_MIX_COLS - FOX_F0), :].T.astype(BF16)
    gate_ref[...] = wt_ref[pl.ds(gate0, N_BRANCH * D_MODEL), :].T.astype(BF16)
    f_rows = jnp.concatenate([wt_ref[pl.ds(FOX_F0, N_HEADS), :], jnp.zeros((LANE - N_HEADS, LANE), F32)], axis=0)
    f_ref[...] = f_rows.T.astype(BF16)


def _mixer_weights(w_in, layer, tr=LANE):
    _, d, cols = w_in.shape
    w_t = jnp.swapaxes(w_in, 1, 2)
    row = lambda w: pl.BlockSpec((tr, w), lambda i: (i, 0))
    return pl.pallas_call(
        _split_w_in_kernel,
        grid=(d // tr,),
        in_specs=[pl.BlockSpec((None, cols, tr), lambda i: (layer, 0, i))],
        out_specs=[row(N_MIX_COLS), row(LANE), row(N_BRANCH * D_MODEL)],
        out_shape=[jax.ShapeDtypeStruct((d, N_MIX_COLS), BF16), jax.ShapeDtypeStruct((d, LANE), BF16),
                   jax.ShapeDtypeStruct((d, N_BRANCH * D_MODEL), BF16)],
        compiler_params=_params("parallel"),
        name="split_w_in",
    )(w_t)


def kernel(x, w_in, w_branch, w_out, norm_mix_g, hgrn_lb_logits, hgrn_norm_g, fox_f_bias, pool_w, pool_scale,
           ret_gn_g, ret_gn_b, norm_ffn_g, ffn_w_gate, ffn_w_up, ffn_w_down, moe_router, moe_w_gate, moe_w_up,
           moe_w_down, final_norm_g):
    b, s, d = x.shape
    assert b == 1 and d == D_MODEL
    depth = w_in.shape[0]
    assert depth == 2, "layer 0 uses the dense FFN, layer 1 the experts and the final norm"
    xs = x.reshape(s, d)
    h = None
    out = None
    for layer in range(depth):
        w_mix, w_f, w_gate = _mixer_weights(w_in, layer)
        if layer == 0:
            proj, h = norm_matmul(xs, norm_mix_g[0], w_mix)
        else:
            proj = matmul(h, w_mix)
        ct, c_rows, fox_stats = fox_gate(h, proj, w_f, fox_f_bias[layer])
        branches = (
            hgrn2(proj, hgrn_lb_logits, hgrn_norm_g[layer], layer),
            fox_attention(proj, ct, c_rows, fox_stats),
            pool_mixer(proj, pool_w[layer], pool_scale[layer]),
            retention(proj, ret_gn_g[layer], ret_gn_b[layer]),
        )
        li = layer // 2
        if layer % 2 == 0:
            xs, h2, (ffn_wg, ffn_wu) = merge(h, branches, xs, w_gate, w_branch[layer].astype(BF16),
                                             w_out[layer].astype(BF16), norm_ffn_g[layer],
                                             cast=(ffn_w_gate[li], ffn_w_up[li]))
            xs, h = ffn_dense(h2, xs, ffn_wg, ffn_wu, ffn_w_down[li], norm_mix_g[layer + 1])
        else:
            xs, h2, (moe_wg, moe_wu) = merge(h, branches, xs, w_gate, w_branch[layer].astype(BF16),
                                             w_out[layer].astype(BF16), norm_ffn_g[layer], tm=MERGE_CAST_TILE,
                                             cast=(moe_w_gate[li], moe_w_up[li]))
            out = moe_sparse(h2, xs, norm_ffn_g[layer], moe_router[li], moe_wg, moe_wu, moe_w_down[li],
                             final_norm_g)
    return out.reshape(b, s, d)
```

```python
import functools
import math

import jax
import jax.numpy as jnp
import numpy as np
from jax import lax
from jax.experimental import pallas as pl
from jax.experimental.pallas import tpu as pltpu

D_MODEL = 1024
N_BRANCH = 4
BRANCH_W = D_MODEL // N_BRANCH
HEAD_DIM = 64
N_HEADS = BRANCH_W // HEAD_DIM
POOL_WINDOWS = (2, 4, 8, 16)
POOL_GROUP = BRANCH_W // len(POOL_WINDOWS)
POOL_HALO = 16
RET_DECAY_BASE = 5.0
ROPE_BASE = 10000.0
D_FF = 7 * D_MODEL // 2
N_EXPERTS = 8
RMS_EPS = 1e-6
LN_EPS = 1e-5
N_MIX_COLS = 12 * BRANCH_W

LANE = 128
SUBLANE = 8
VMEM_LIMIT = 56 * 1024 * 1024

HG_CHUNK = 64
HG_SUB = 16
HG_FAST_MIN_LOGDECAY = -60.0
RET_CHUNK = 256

F32 = jnp.float32
BF16 = jnp.bfloat16
NT_DIMS = (((1,), (1,)), ((), ()))


def _params(*sem):
    return pltpu.CompilerParams(dimension_semantics=sem, vmem_limit_bytes=VMEM_LIMIT)


def _const_spec(shape):
    nd = len(shape)
    return pl.BlockSpec(shape, lambda *_: (0,) * nd, pipeline_mode=pl.Buffered(1))


def _split3(x):
    hi = x.astype(BF16)
    r1 = x - hi.astype(F32)
    mid = r1.astype(BF16)
    lo = (r1 - mid.astype(F32)).astype(BF16)
    return hi, mid, lo


def _dot(a, b):
    return jnp.dot(a, b, preferred_element_type=F32)


def _dot_nt(a, b):
    return lax.dot_general(a, b, NT_DIMS, preferred_element_type=F32)


def _dot_exact_rhs(x, m_bf16, terms=3):
    return sum(_dot(part, m_bf16) for part in _split3(x)[:terms])


def _dot_exact_lhs(m_bf16, x, terms=3):
    return sum(_dot(m_bf16, part) for part in _split3(x)[:terms])


def _sigmoid(x):
    return 1.0 / (1.0 + jnp.exp(-x))


def _silu(x):
    return x * _sigmoid(x)


def _rms(x, gain):
    return x * lax.rsqrt(jnp.mean(x * x, axis=-1, keepdims=True) + RMS_EPS) * gain


def _const(a, dtype=F32):
    return jnp.asarray(np.asarray(a, np.float32), dtype)


def _head_of(n):
    return np.arange(n) // HEAD_DIM


def _head_ones():
    h = _head_of(BRANCH_W)
    return _const(h[:, None] == h[None, :], BF16)


def _head_masks():
    return _const(_head_of(BRANCH_W)[None, :] == np.arange(N_HEADS)[:, None])


def _norm_matmul_kernel(x_ref, g_ref, b_ref, o_ref, h_ref):
    @pl.when(pl.program_id(1) == 0)
    def _():
        h_ref[...] = _rms(x_ref[...], g_ref[...]).astype(h_ref.dtype)

    o_ref[...] = _dot(h_ref[...], b_ref[...]).astype(o_ref.dtype)


def norm_matmul(x, gain, b, tm=2048, tn=1024):
    m, k = x.shape
    _, n = b.shape
    tm = min(tm, m)
    return pl.pallas_call(
        _norm_matmul_kernel,
        grid=(m // tm, n // tn),
        in_specs=[pl.BlockSpec((tm, k), lambda i, j: (i, 0)), _const_spec((1, k)),
                  pl.BlockSpec((k, tn), lambda i, j: (0, j))],
        out_specs=[pl.BlockSpec((tm, tn), lambda i, j: (i, j)), pl.BlockSpec((tm, k), lambda i, j: (i, 0))],
        out_shape=[jax.ShapeDtypeStruct((m, n), BF16), jax.ShapeDtypeStruct((m, k), BF16)],
        compiler_params=_params("arbitrary", "arbitrary"),
        name="in_proj_norm",
    )(x, gain.reshape(1, k).astype(F32), b)


def _matmul_kernel(a_ref, b_ref, o_ref):
    o_ref[...] = _dot(a_ref[...], b_ref[...]).astype(o_ref.dtype)


def matmul(a, b, out_dtype=BF16, tm=2048, tn=1024):
    m, k = a.shape
    _, n = b.shape
    tm = min(tm, m)
    return pl.pallas_call(
        _matmul_kernel,
        grid=(n // tn, m // tm),
        in_specs=[pl.BlockSpec((tm, k), lambda j, i: (i, 0)),
                  pl.BlockSpec((k, tn), lambda j, i: (0, j))],
        out_specs=pl.BlockSpec((tm, tn), lambda j, i: (i, j)),
        out_shape=jax.ShapeDtypeStruct((m, n), out_dtype),
        compiler_params=_params("parallel", "parallel"),
        name="in_proj",
    )(a, b)


def _hgrn_kernel(q_ref, f_ref, i_ref, g_ref, lbl_ref, ng_ref, ones_ref, hm_ref, tril_ref, halfsum_ref, fmask_ref,
                 o_ref, st_ref, bpad, kpad, vpad, astack, lf_sc, kk_sc, o_sc, *, layer, tile):
    c, sub = HG_CHUNK, HG_SUB
    nsub = c // sub
    half = c // 2

    @pl.when(pl.program_id(0) == 0)
    def _():
        st_ref[...] = jnp.zeros_like(st_ref)
        bpad[...] = jnp.zeros_like(bpad)
        kpad[...] = jnp.zeros_like(kpad)
        vpad[...] = jnp.zeros_like(vpad)

    lbl = lbl_ref[...]
    e = jnp.exp(lbl - jnp.max(lbl, axis=0, keepdims=True))
    p = e / jnp.sum(e, axis=0, keepdims=True)
    lb = jnp.zeros((1, BRANCH_W), F32)
    for l in range(1, layer + 1):
        lb = lb + p[l:l + 1, :]

    ones_bd = ones_ref[...]
    hm = hm_ref[...]
    tril = tril_ref[...]
    row = lax.broadcasted_iota(jnp.int32, (c, 1), 0)
    row_in_sub = row % sub
    bd_mask = ones_bd.astype(F32)

    sig = _sigmoid(f_ref[...].astype(F32))
    logf_all = jnp.log(lb + (1.0 - lb) * sig)
    lf_sc[...] = _dot_exact_lhs(tril, logf_all, terms=2)
    kk_sc[...] = (1.0 - lb) * (1.0 - sig)
    min_decay = jnp.min(_dot(halfsum_ref[...], logf_all.astype(BF16)))

    def load(ci):
        r0 = pl.multiple_of(ci * c, c)
        q = q_ref[pl.ds(r0, c), :].astype(F32)
        v = i_ref[pl.ds(r0, c), :].astype(F32)
        kk = kk_sc[pl.ds(r0, c), :]
        b = lf_sc[pl.ds(r0, c), :]
        return r0, q, v, kk, b

    def finish(r0, q_decayed, v, kk, b, intra):
        st = st_ref[...]
        inter = _dot_nt(q_decayed.astype(BF16), st.astype(BF16))
        b_last = b[c - 1:c, :]
        ks_end = (kk * jnp.exp(b_last - b)).astype(BF16)
        upd = _dot(v.T.astype(BF16), ks_end)
        st_ref[...] = st * jnp.exp(b_last) + upd * bd_mask
        o_sc[pl.ds(r0, c), :] = intra + inter

    def fast_chunk(ci, carry):
        r0, q, v, kk, b = load(ci)
        second = row >= half
        m_row = b[half - 1:half, :]
        mref = jnp.where(second, m_row, 0.0)
        qp = q * jnp.exp(b - mref)
        kp = kk * jnp.exp(mref - b)
        e_m = jnp.exp(m_row)
        kaug = jnp.concatenate([kp, kp[:half, :] * e_m], axis=0)
        vaug = jnp.concatenate([v, v[:half, :]], axis=0)
        k_heads = jnp.concatenate([kaug * hm[h:h + 1, :] for h in range(N_HEADS)], axis=0).astype(BF16)
        v_heads = jnp.concatenate([vaug * hm[h:h + 1, :] for h in range(N_HEADS)], axis=0).astype(BF16)
        sc = jnp.where(fmask_ref[...] > 0.0, _dot_nt(qp.astype(BF16), k_heads), 0.0)
        intra = _dot(sc.astype(BF16), v_heads)
        finish(r0, jnp.where(second, qp * e_m, qp), v, kk, b, intra)
        return carry

    def exact_chunk(ci, carry):
        r0, q, v, kk, b = load(ci)

        bpad[pl.ds(sub, c), :] = b
        kpad[pl.ds(sub, c), :] = kk
        vpad[pl.ds(sub, c), :] = v

        for d in range(sub):
            b_d = bpad[pl.ds(sub - d, c), :]
            k_d = kpad[pl.ds(sub - d, c), :]
            a = jnp.where(row_in_sub >= d, q * k_d * jnp.exp(b - b_d), 0.0)
            astack[pl.ds(d * c, c), :] = a.astype(BF16)
        pall = _dot(astack[...], ones_bd)
        intra = jnp.zeros((c, BRANCH_W), F32)
        for d in range(sub):
            intra = intra + pall[d * c:(d + 1) * c, :] * vpad[pl.ds(sub - d, c), :]

        pieces = [jnp.zeros((sub, BRANCH_W), F32)]
        for si in range(1, nsub):
            lo = si * sub
            m_i = b[lo - 1:lo, :]
            qs = q[lo:lo + sub, :] * jnp.exp(b[lo:lo + sub, :] - m_i)
            ks = (kk[:lo, :] * jnp.exp(m_i - b[:lo, :])).astype(BF16)
            qx = jnp.concatenate([qs * hm[h:h + 1, :] for h in range(N_HEADS)], axis=0).astype(BF16)
            sc = _dot_nt(qx, ks)
            r = _dot(sc.astype(BF16), v[:lo, :].astype(BF16))
            acc = jnp.zeros((sub, BRANCH_W), F32)
            for h in range(N_HEADS):
                acc = acc + r[h * sub:(h + 1) * sub, :] * hm[h:h + 1, :]
            pieces.append(acc)
        intra = intra + jnp.concatenate(pieces, axis=0)
        finish(r0, q * jnp.exp(b), v, kk, b, intra)
        return carry

    lax.cond(min_decay >= HG_FAST_MIN_LOGDECAY,
             lambda: lax.fori_loop(0, tile // c, fast_chunk, 0, unroll=2),
             lambda: lax.fori_loop(0, tile // c, exact_chunk, 0))

    o = o_sc[...]
    ms = _dot_exact_rhs(o * o, ones_bd, terms=2) * (1.0 / HEAD_DIM)
    y = o * lax.rsqrt(ms + RMS_EPS) * ng_ref[...] * _silu(g_ref[...].astype(F32))
    o_ref[...] = y.astype(o_ref.dtype)


def hgrn2(proj, lb_logits, norm_g, layer, tile=512):
    s = proj.shape[0]
    depth = lb_logits.shape[0]
    c, sub = HG_CHUNK, HG_SUB
    half = c // 2
    tile = min(tile, s)
    col = lambda j: pl.BlockSpec((tile, BRANCH_W), lambda i, j=j: (i, j))
    pos = np.arange(tile)
    tril = _const((pos[:, None] // c == pos[None, :] // c) & (pos[None, :] <= pos[:, None]), BF16)
    nhalf = tile // half
    halfsum = _const(np.arange(tile)[None, :] // half == np.arange(nhalf)[:, None], BF16)
    t = np.arange(c)[:, None]
    col_s = np.arange(c + half)[None, :]
    same_half = (col_s < c) & (col_s // half == t // half) & (col_s <= t)
    cross = (col_s >= c) & (t >= half)
    fmask = _const(np.tile(same_half | cross, (1, N_HEADS)))
    return pl.pallas_call(
        functools.partial(_hgrn_kernel, layer=layer, tile=tile),
        grid=(s // tile,),
        in_specs=[col(0), col(1), col(2), col(3),
                  _const_spec((depth, BRANCH_W)), _const_spec((1, BRANCH_W)),
                  _const_spec((BRANCH_W, BRANCH_W)), _const_spec((N_HEADS, BRANCH_W)),
                  _const_spec((tile, tile)), _const_spec((nhalf, tile)),
                  _const_spec((c, N_HEADS * (c + half)))],
        out_specs=pl.BlockSpec((tile, BRANCH_W), lambda i: (i, 0)),
        out_shape=jax.ShapeDtypeStruct((s, BRANCH_W), BF16),
        scratch_shapes=[pltpu.VMEM((BRANCH_W, BRANCH_W), F32),
                        pltpu.VMEM((c + sub, BRANCH_W), F32),
                        pltpu.VMEM((c + sub, BRANCH_W), F32),
                        pltpu.VMEM((c + sub, BRANCH_W), F32),
                        pltpu.VMEM((sub * c, BRANCH_W), BF16),
                        pltpu.VMEM((tile, BRANCH_W), F32),
                        pltpu.VMEM((tile, BRANCH_W), F32),
                        pltpu.VMEM((tile, BRANCH_W), F32)],
        compiler_params=_params("arbitrary"),
        name="hgrn2",
    )(proj, proj, proj, proj, lb_logits.astype(F32), norm_g.reshape(1, BRANCH_W).astype(F32),
      _head_ones(), _head_masks(), tril, halfsum, fmask)


FOX_TILE = 256
FOX_GATE_BLOCKS = 2
FOX_NSTAT = 16
FOX_FIXED_MAX = 30.0
FOX_SKIP_LOG = 40.0


def _fox_gate_kernel(h_ref, q_ref, k_ref, wf_ref, bias_ref, tril_ref, ones_ref,
                     ct_ref, c_ref, stat_ref, carry_ref, kmax_ref):
    @pl.when(pl.program_id(0) == 0)
    def _():
        carry_ref[...] = jnp.zeros_like(carry_ref)
        kmax_ref[...] = jnp.zeros_like(kmax_ref)

    logit = _dot(h_ref[...], wf_ref[...]) + bias_ref[...]
    logf = jnp.minimum(logit, 0.0) - jnp.log(1.0 + jnp.exp(-jnp.abs(logit)))
    carry = carry_ref[...]
    parts = []
    for blk in range(logf.shape[0] // FOX_TILE):
        part = _dot_exact_lhs(tril_ref[...], logf[blk * FOX_TILE:(blk + 1) * FOX_TILE]) + carry
        carry = part[-1:, :]
        parts.append(part)
    cum = jnp.concatenate(parts, axis=0)
    carry_ref[...] = carry
    ct_ref[...] = cum.T[:SUBLANE, :]
    c_ref[...] = cum

    ones_bd = ones_ref[...]
    q = q_ref[...].astype(F32)
    k = k_ref[...].astype(F32)
    scale = HEAD_DIM ** -0.5
    head_lane = lax.broadcasted_iota(jnp.int32, (1, BRANCH_W), 1) // HEAD_DIM
    c_heads = jnp.zeros(q.shape, F32)
    for h in range(N_HEADS):
        c_heads = jnp.where(head_lane == h, cum[:, h:h + 1], c_heads)
    slack = 1.0 + 2.0 ** -6
    qn = jnp.sqrt(_dot((q * q).astype(BF16), ones_bd)) * (scale * slack)
    kn = jnp.sqrt(_dot((k * k).astype(BF16), ones_bd)) * slack
    diag = _dot((q * k).astype(BF16), ones_bd) * scale - (2.0 ** -6) * qn * kn
    e_row = c_heads - diag
    kmax = kmax_ref[...]
    for blk in range(q.shape[0] // FOX_TILE):
        rs = slice(blk * FOX_TILE, (blk + 1) * FOX_TILE)
        kmax = jnp.maximum(kmax, jnp.max(kn[rs], axis=0, keepdims=True))
        rows = [jnp.max(qn[rs], axis=0, keepdims=True),
                jnp.max(e_row[rs], axis=0, keepdims=True),
                kmax,
                c_heads[(blk + 1) * FOX_TILE - 1:(blk + 1) * FOX_TILE, :]]
        stat_ref[blk] = jnp.concatenate(rows + [jnp.zeros((SUBLANE - len(rows), BRANCH_W), F32)], axis=0)
    kmax_ref[...] = kmax


def fox_gate(h, proj, w_f, f_bias):
    s, d = h.shape
    tile = min(FOX_GATE_BLOCKS * FOX_TILE, s)
    nblk = tile // FOX_TILE
    bias = jnp.zeros((1, LANE), F32).at[0, :N_HEADS].set(f_bias.astype(F32))
    tril = _const(np.tril(np.ones((FOX_TILE, FOX_TILE))), BF16)
    ct, c_rows, stats = pl.pallas_call(
        _fox_gate_kernel,
        grid=(s // tile,),
        in_specs=[pl.BlockSpec((tile, d), lambda i: (i, 0)),
                  pl.BlockSpec((tile, BRANCH_W), lambda i: (i, 4)),
                  pl.BlockSpec((tile, BRANCH_W), lambda i: (i, 5)),
                  _const_spec((d, LANE)), _const_spec((1, LANE)), _const_spec((FOX_TILE, FOX_TILE)),
                  _const_spec((BRANCH_W, BRANCH_W))],
        out_specs=[pl.BlockSpec((SUBLANE, tile), lambda i: (0, i)),
                   pl.BlockSpec((tile, LANE), lambda i: (i, 0)),
                   pl.BlockSpec((nblk, SUBLANE, BRANCH_W), lambda i: (i, 0, 0))],
        out_shape=[jax.ShapeDtypeStruct((SUBLANE, s), F32),
                   jax.ShapeDtypeStruct((s, LANE), F32),
                   jax.ShapeDtypeStruct((s // FOX_TILE, SUBLANE, BRANCH_W), F32)],
        scratch_shapes=[pltpu.VMEM((1, LANE), F32), pltpu.VMEM((1, BRANCH_W), F32)],
        compiler_params=_params("arbitrary"),
        name="fox_gate",
    )(h, proj, proj, w_f, bias, tril, _head_ones())
    return ct, c_rows, stats[:, :4, ::HEAD_DIM].reshape(-1)


def _fox_kernel(stat_ref, q_ref, k_ref, v_ref, ct_ref, c_ref, hm_ref, o_ref, m_sc, l_sc, acc_sc, *, tq):
    i = pl.program_id(0)
    q0 = pl.multiple_of(i * tq, tq)
    hm = hm_ref[...]
    q = q_ref[...].astype(F32) * (HEAD_DIM ** -0.5)
    qh = [(q * hm[h:h + 1, :]).astype(BF16) for h in range(N_HEADS)]
    c_q0 = ct_ref[:, pl.ds(q0, tq)][:, 0:1]

    first = i
    for h in range(N_HEADS):
        qmax = stat_ref[i * FOX_NSTAT + h]
        emax = stat_ref[i * FOX_NSTAT + N_HEADS + h]

        def needed(j, h=h, qmax=qmax, emax=emax):
            jc = jnp.maximum(j, 0)
            bound = (qmax * stat_ref[jc * FOX_NSTAT + 2 * N_HEADS + h] + emax
                     - stat_ref[jc * FOX_NSTAT + 3 * N_HEADS + h])
            return (j >= 0) & (bound >= -FOX_SKIP_LOG)

        last_dropped = lax.while_loop(needed, lambda j: j - 1, i - 1)
        first = jnp.minimum(first, last_dropped + 1)

    l_sc[...] = jnp.zeros_like(l_sc)
    acc_sc[...] = jnp.zeros_like(acc_sc)

    def causal(sc):
        r = lax.broadcasted_iota(jnp.int32, (tq, tq), 0)
        cidx = lax.broadcasted_iota(jnp.int32, (tq, tq), 1)
        return jnp.where(cidx <= r, sc, -jnp.inf)

    def online_block(s0, diagonal):
        kb = k_ref[pl.ds(s0, tq), :]
        vb = v_ref[pl.ds(s0, tq), :]
        bias = c_q0 - ct_ref[:, pl.ds(s0, tq)]
        for h in range(N_HEADS):
            sc = _dot_nt(qh[h], kb) + bias[h:h + 1, :]
            if diagonal:
                sc = causal(sc)
            m_prev = m_sc[h]
            m_new = jnp.maximum(m_prev, jnp.max(sc, axis=1, keepdims=True))
            alpha = jnp.exp(m_prev - m_new)
            p = jnp.exp(sc - jnp.tile(m_new, (1, tq // LANE)))
            l_sc[h] = alpha * l_sc[h] + jnp.sum(p, axis=1, keepdims=True)
            acc_sc[h] = acc_sc[h] * jnp.tile(alpha, (1, BRANCH_W // LANE)) + _dot(p.astype(BF16), vb)
            m_sc[h] = m_new

    tops = [stat_ref[i * FOX_NSTAT + h] * stat_ref[i * FOX_NSTAT + 2 * N_HEADS + h] for h in range(N_HEADS)]
    c_tile = c_ref[...]
    shift = [c_tile[:, h:h + 1] - c_q0[h:h + 1, :] - tops[h] for h in range(N_HEADS)]

    def fixed_block(s0, diagonal):
        kb = k_ref[pl.ds(s0, tq), :]
        vb = v_ref[pl.ds(s0, tq), :]
        bias = c_q0 - ct_ref[:, pl.ds(s0, tq)]
        for h in range(N_HEADS):
            sc = _dot_nt(qh[h], kb) + bias[h:h + 1, :] + shift[h]
            if diagonal:
                sc = causal(sc)
            p = jnp.exp(sc)
            l_sc[h] += p[:, :LANE] + p[:, LANE:]
            acc_sc[h] += _dot(p.astype(BF16), vb)

    def run(block, row_sum):
        lax.fori_loop(first, i, lambda j, carry: (block(pl.multiple_of(j * tq, tq), False), carry)[1], 0)
        block(q0, True)
        out = jnp.zeros((tq, BRANCH_W), F32)
        for h in range(N_HEADS):
            out = out + acc_sc[h] * hm[h:h + 1, :] / row_sum(l_sc[h])
        o_ref[...] = out.astype(o_ref.dtype)

    def run_online():
        m_sc[...] = jnp.full_like(m_sc, -jnp.inf)
        run(online_block, lambda l: jnp.tile(l, (1, BRANCH_W // LANE)))

    def run_fixed():
        run(fixed_block, lambda l: jnp.sum(l, axis=1, keepdims=True))

    lax.cond(functools.reduce(jnp.maximum, tops) <= FOX_FIXED_MAX, run_fixed, run_online)


def fox_attention(proj, ct, c_rows, stats):
    s = proj.shape[0]
    tq = min(FOX_TILE, s)
    full = lambda j: pl.BlockSpec((s, BRANCH_W), lambda i, j=j: (0, j), pipeline_mode=pl.Buffered(1))
    return pl.pallas_call(
        functools.partial(_fox_kernel, tq=tq),
        grid=(s // tq,),
        in_specs=[pl.BlockSpec(memory_space=pltpu.SMEM),
                  pl.BlockSpec((tq, BRANCH_W), lambda i: (i, 4)), full(5), full(6),
                  _const_spec((SUBLANE, s)), pl.BlockSpec((tq, LANE), lambda i: (i, 0)),
                  _const_spec((N_HEADS, BRANCH_W))],
        out_specs=pl.BlockSpec((tq, BRANCH_W), lambda i: (i, 0)),
        out_shape=jax.ShapeDtypeStruct((s, BRANCH_W), BF16),
        scratch_shapes=[pltpu.VMEM((N_HEADS, tq, LANE), F32),
                        pltpu.VMEM((N_HEADS, tq, LANE), F32),
                        pltpu.VMEM((N_HEADS, tq, BRANCH_W), F32)],
        compiler_params=_params("parallel"),
        name="fox_attention",
    )(stats, proj, proj, proj, ct, c_rows, _head_masks())


def _pool_kernel(u_ref, w_ref, scale_ref, o_ref, ext, *, tile):
    i = pl.program_id(0)

    @pl.when(i == 0)
    def _():
        ext[pl.ds(0, POOL_HALO), :] = jnp.zeros((POOL_HALO, BRANCH_W), F32)

    u = u_ref[...].astype(F32)
    ext[pl.ds(POOL_HALO, tile), :] = u
    pos = (i * tile + lax.broadcasted_iota(jnp.int32, (tile, 1), 0) + 1).astype(F32)
    halves = []
    for half in range(BRANCH_W // LANE):
        lanes = pl.ds(half * LANE, LANE)
        w_small, w_big = POOL_WINDOWS[2 * half], POOL_WINDOWS[2 * half + 1]
        run = u[:, half * LANE:(half + 1) * LANE]
        sums = {}
        for j in range(1, w_big):
            if j == w_small:
                sums[w_small] = run
            run = run + ext[pl.ds(POOL_HALO - j, tile), lanes]
        sums[w_big] = run
        lane = lax.broadcasted_iota(jnp.int32, (1, LANE), 1)
        small = lane < POOL_GROUP
        total = jnp.where(small, sums[w_small], sums[w_big])
        count = jnp.where(small, jnp.minimum(pos, float(w_small)), jnp.minimum(pos, float(w_big)))
        halves.append(total / count)
    mean = jnp.concatenate(halves, axis=1)
    d = (mean - u).astype(BF16)
    y = _dot(d, w_ref[...]) * scale_ref[...]
    o_ref[...] = y.astype(o_ref.dtype)
    ext[pl.ds(0, POOL_HALO), :] = u[tile - POOL_HALO:, :]


def pool_mixer(proj, w_pool, scale, tile=2048):
    s = proj.shape[0]
    tile = min(tile, s)
    ng = len(POOL_WINDOWS)
    w_bd = jnp.zeros((BRANCH_W, BRANCH_W), F32)
    for gi in range(ng):
        lo = gi * POOL_GROUP
        w_bd = w_bd.at[lo:lo + POOL_GROUP, lo:lo + POOL_GROUP].set(w_pool[gi].astype(F32))
    return pl.pallas_call(
        functools.partial(_pool_kernel, tile=tile),
        grid=(s // tile,),
        in_specs=[pl.BlockSpec((tile, BRANCH_W), lambda i: (i, 7)),
                  _const_spec((BRANCH_W, BRANCH_W)), _const_spec((1, BRANCH_W))],
        out_specs=pl.BlockSpec((tile, BRANCH_W), lambda i: (i, 0)),
        out_shape=jax.ShapeDtypeStruct((s, BRANCH_W), BF16),
        scratch_shapes=[pltpu.VMEM((tile + POOL_HALO, BRANCH_W), F32)],
        compiler_params=_params("arbitrary"),
        name="pool_mixer",
    )(proj, w_bd.astype(BF16), scale.reshape(1, BRANCH_W).astype(F32))


def _ret_kernel(q_ref, k_ref, v_ref, g_ref, cos_ref, sin_ref, perm_ref, ones_ref, hm_ref,
                dstack_ref, xi_ref, zeta_ref, gc_ref, gng_ref, gnb_ref, o_ref, st_ref, *, tile):
    c = RET_CHUNK

    @pl.when(pl.program_id(0) == 0)
    def _():
        st_ref[...] = jnp.zeros_like(st_ref)

    perm = perm_ref[...]
    ones_bd = ones_ref[...]
    bd_mask = ones_bd.astype(F32)
    hm = hm_ref[...]

    cos = cos_ref[...]
    sin = sin_ref[...]
    q_all = q_ref[...]
    k_all = k_ref[...]
    qr_all = q_all.astype(F32) * cos + _dot(q_all, perm) * sin
    kr_all = (k_all.astype(F32) * cos + _dot(k_all, perm) * sin) * (HEAD_DIM ** -0.5)

    outs = []
    for ci in range(tile // c):
        r0 = ci * c
        qr = qr_all[r0:r0 + c, :]
        kr = kr_all[r0:r0 + c, :]
        v = v_ref[pl.ds(r0, c), :]

        qx = jnp.concatenate([qr * hm[h:h + 1, :] for h in range(N_HEADS)], axis=0).astype(BF16)
        sc = _dot_nt(qx, kr.astype(BF16)) * dstack_ref[...]
        r = _dot(sc.astype(BF16), v)
        intra = jnp.zeros((c, BRANCH_W), F32)
        for h in range(N_HEADS):
            intra = intra + r[h * c:(h + 1) * c, :] * hm[h:h + 1, :]

        st = st_ref[...]
        inter = _dot_nt((qr * xi_ref[...]).astype(BF16), st.astype(BF16))
        upd = _dot(v.astype(F32).T.astype(BF16), (kr * zeta_ref[...]).astype(BF16))
        st_ref[...] = st * gc_ref[...] + upd * bd_mask
        outs.append(intra + inter)

    o = jnp.concatenate(outs, axis=0)
    mu = _dot_exact_rhs(o, ones_bd, terms=2) * (1.0 / HEAD_DIM)
    cen = o - mu
    var = _dot_exact_rhs(cen * cen, ones_bd, terms=2) * (1.0 / HEAD_DIM)
    y = cen * lax.rsqrt(var + LN_EPS) * gng_ref[...] + gnb_ref[...]
    o_ref[...] = (y * _silu(g_ref[...].astype(F32))).astype(o_ref.dtype)


def _rope_tables(s):
    half = HEAD_DIM // 2
    pos = np.arange(s, dtype=np.float64)
    inv_freq = ROPE_BASE ** (-np.arange(half, dtype=np.float64) / half)
    lane = np.arange(BRANCH_W)
    ang = pos[:, None] * inv_freq[lane % half][None, :]
    sign = np.where(lane % HEAD_DIM < half, -1.0, 1.0)
    return _const(np.cos(ang)), _const(np.sin(ang) * sign[None, :])


def _ret_constants():
    c = RET_CHUNK
    half = HEAD_DIM // 2
    lane = np.arange(BRANCH_W)
    partner = np.where(lane % HEAD_DIM < half, lane + half, lane - half)
    perm = lane[:, None] == partner[None, :]
    log_gamma = np.log1p(-np.exp2(-RET_DECAY_BASE - np.arange(N_HEADS, dtype=np.float64)))
    ci = np.arange(c, dtype=np.float64)
    diff = ci[:, None] - ci[None, :]
    intra = np.where(diff >= 0, np.exp(diff * log_gamma[:, None, None]), 0.0)
    dstack = intra.reshape(N_HEADS * c, c)
    lg_lane = np.repeat(log_gamma, HEAD_DIM)[None, :]
    xi = np.exp((ci[:, None] + 1.0) * lg_lane)
    zeta = np.exp((c - 1.0 - ci[:, None]) * lg_lane)
    gc = np.exp(c * lg_lane)
    return _const(perm, BF16), _const(dstack), _const(xi), _const(zeta), _const(gc)


def retention(proj, gn_g, gn_b, tile=1024):
    s = proj.shape[0]
    c = RET_CHUNK
    tile = min(tile, s)
    cos_t, sin_t = _rope_tables(s)
    perm, dstack, xi, zeta, gc = _ret_constants()
    col = lambda j: pl.BlockSpec((tile, BRANCH_W), lambda i, j=j: (i, j))
    row = pl.BlockSpec((tile, BRANCH_W), lambda i: (i, 0))
    return pl.pallas_call(
        functools.partial(_ret_kernel, tile=tile),
        grid=(s // tile,),
        in_specs=[col(8), col(9), col(10), col(11), row, row,
                  _const_spec((BRANCH_W, BRANCH_W)), _const_spec((BRANCH_W, BRANCH_W)),
                  _const_spec((N_HEADS, BRANCH_W)), _const_spec((N_HEADS * c, c)),
                  _const_spec((c, BRANCH_W)), _const_spec((c, BRANCH_W)), _const_spec((1, BRANCH_W)),
                  _const_spec((1, BRANCH_W)), _const_spec((1, BRANCH_W))],
        out_specs=row,
        out_shape=jax.ShapeDtypeStruct((s, BRANCH_W), BF16),
        scratch_shapes=[pltpu.VMEM((BRANCH_W, BRANCH_W), F32)],
        compiler_params=_params("arbitrary"),
        name="retention",
    )(proj, proj, proj, proj, cos_t, sin_t, perm, _head_ones(), _head_masks(), dstack, xi, zeta, gc,
      gn_g.reshape(1, BRANCH_W).astype(F32), gn_b.reshape(1, BRANCH_W).astype(F32))


def _merge_kernel(h_ref, o0_ref, o1_ref, o2_ref, o3_ref, x_ref, wg_ref, wb_ref, wo_ref, g_ref, *rest):
    n_cast = (len(rest) - 2) // 2
    cast_in, (xo_ref, ho_ref), cast_out = rest[:n_cast], rest[n_cast:n_cast + 2], rest[n_cast + 2:]
    for src, dst in zip(cast_in, cast_out):
        dst[...] = src[...].astype(dst.dtype)
    h = h_ref[...]
    merged = jnp.zeros(x_ref.shape, F32)
    for bi, o_ref in enumerate((o0_ref, o1_ref, o2_ref, o3_ref)):
        gate = _sigmoid(_dot(h, wg_ref[:, bi * D_MODEL:(bi + 1) * D_MODEL]))
        merged = merged + gate * _dot(o_ref[...], wb_ref[bi])
    x_new = x_ref[...] + _dot(merged.astype(BF16), wo_ref[...])
    xo_ref[...] = x_new
    ho_ref[...] = _rms(x_new, g_ref[...]).astype(ho_ref.dtype)


def merge(h, branches, x, w_gate, w_branch, w_out, next_gain, tm=512, cast=()):
    s, d = x.shape
    tm = min(tm, s)
    steps = s // tm
    row = lambda w: pl.BlockSpec((tm, w), lambda i: (i, 0))
    flat = [c.reshape(-1, c.shape[-1]) for c in cast]
    slab = lambda c: pl.BlockSpec((c.shape[0] // steps, c.shape[1]), lambda i: (i, 0))
    outs = pl.pallas_call(
        _merge_kernel,
        grid=(steps,),
        in_specs=[row(d), row(BRANCH_W), row(BRANCH_W), row(BRANCH_W), row(BRANCH_W), row(d),
                  _const_spec((d, N_BRANCH * d)), _const_spec((N_BRANCH, BRANCH_W, d)),
                  _const_spec((d, d)), _const_spec((1, d))] + [slab(c) for c in flat],
        out_specs=[row(d), row(d)] + [slab(c) for c in flat],
        out_shape=[jax.ShapeDtypeStruct((s, d), F32), jax.ShapeDtypeStruct((s, d), BF16)]
        + [jax.ShapeDtypeStruct(c.shape, BF16) for c in flat],
        compiler_params=_params("parallel"),
        name="merge",
    )(h, *branches, x, w_gate, w_branch, w_out, next_gain.reshape(1, d).astype(F32), *flat)
    return outs[0], outs[1], [o.reshape(c.shape) for o, c in zip(outs[2:], cast)]


def _ffn_kernel(h_ref, x_ref, wg_ref, wu_ref, wd_ref, g_ref, xo_ref, ho_ref, acc_ref):
    f = pl.program_id(1)

    @pl.when(f == 0)
    def _():
        acc_ref[...] = jnp.zeros_like(acc_ref)

    h = h_ref[...]
    a = _silu(_dot(h, wg_ref[...].astype(BF16))) * _dot(h, wu_ref[...].astype(BF16))
    acc_ref[...] += _dot(a.astype(BF16), wd_ref[...].astype(BF16))

    @pl.when(f == pl.num_programs(1) - 1)
    def _():
        x_new = x_ref[...] + acc_ref[...]
        xo_ref[...] = x_new
        ho_ref[...] = _rms(x_new, g_ref[...]).astype(ho_ref.dtype)


def ffn_dense(h, x, w_gate, w_up, w_down, next_gain, tm=1024, tf=512):
    s, d = x.shape
    tm = min(tm, s)
    dff = w_gate.shape[1]
    row = lambda: pl.BlockSpec((tm, d), lambda i, f: (i, 0))
    return pl.pallas_call(
        _ffn_kernel,
        grid=(s // tm, dff // tf),
        in_specs=[row(), row(),
                  pl.BlockSpec((d, tf), lambda i, f: (0, f)),
                  pl.BlockSpec((d, tf), lambda i, f: (0, f)),
                  pl.BlockSpec((tf, d), lambda i, f: (f, 0)),
                  _const_spec((1, d))],
        out_specs=[row(), row()],
        out_shape=[jax.ShapeDtypeStruct((s, d), F32), jax.ShapeDtypeStruct((s, d), BF16)],
        scratch_shapes=[pltpu.VMEM((tm, d), F32)],
        compiler_params=_params("parallel", "arbitrary"),
        name="ffn_dense",
    )(h, x, w_gate, w_up, w_down, next_gain.reshape(1, d).astype(F32))


MERGE_CAST_TILE = 256
MOE_TOK_TILE = 256
MOE_ROUTE_TILES = 4
MOE_ROW_BLOCK = 512
MOE_GATHER_ROWS = 128
MOE_GATHER_TILES = 4
MOE_VMEM_LIMIT = 60 * 1024 * 1024
MOE_Y_BLOCK = 128
MOE_Y_FETCH = MOE_TOK_TILE // MOE_Y_BLOCK + 1


def _router_logits(x, w):
    xh, xm, _ = _split3(x)
    wh, wm, _ = _split3(w)
    packed = (wh.astype(F32) + pltpu.roll(wm.astype(F32), N_EXPERTS, axis=1)).astype(BF16)
    s = _dot(xh, packed) + _dot(xm, packed)
    return s + pltpu.roll(s, LANE - N_EXPERTS, axis=1)


def _route_kernel(x_ref, ng_ref, router_ref, ltri_ref, pos_ref, gate_ref, post_ref, before_ref, total_ref,
                  carry_ref):
    tm = x_ref.shape[0]

    @pl.when(pl.program_id(0) == 0)
    def _():
        carry_ref[...] = jnp.zeros_like(carry_ref)

    hn = _rms(x_ref[...], ng_ref[...])
    logits = _router_logits(hn, router_ref[...])
    lane = lax.broadcasted_iota(jnp.int32, (tm, LANE), 1)
    logits = jnp.where(lane < N_EXPERTS, logits, -jnp.inf)
    v1 = jnp.max(logits, axis=1, keepdims=True)
    i1 = jnp.min(jnp.where(logits == v1, lane, LANE), axis=1, keepdims=True)
    rest = jnp.where(lane == i1, -jnp.inf, logits)
    v2 = jnp.max(rest, axis=1, keepdims=True)
    i2 = jnp.min(jnp.where(rest == v2, lane, LANE), axis=1, keepdims=True)
    w1 = 1.0 / (1.0 + jnp.exp(v2 - v1))
    gate_ref[...] = jnp.where(lane == i1, w1, 0.0) + jnp.where(lane == i2, 1.0 - w1, 0.0)

    member = jnp.where((lane == i1) | (lane == i2), 1.0, 0.0)
    carry = carry_ref[...]
    ranks = []
    for blk in range(tm // MOE_TOK_TILE):
        before_ref[blk] = carry
        mb = member[blk * MOE_TOK_TILE:(blk + 1) * MOE_TOK_TILE]
        ranks.append(_dot(ltri_ref[...], mb.astype(BF16)) + carry)
        carry = carry + jnp.sum(mb, axis=0, keepdims=True)
    pos = jnp.where(member > 0.0, jnp.concatenate(ranks, axis=0), -1.0)
    pos_ref[...] = pos
    post_ref[...] = pos.T[:SUBLANE, :]
    carry_ref[...] = carry
    total_ref[...] = carry


def moe_route(x, norm_gain, router):
    s, d = x.shape
    tm = min(MOE_ROUTE_TILES * MOE_TOK_TILE, s)
    nsub = tm // MOE_TOK_TILE
    nt = s // tm
    router_p = jnp.zeros((d, LANE), F32).at[:, :N_EXPERTS].set(router.astype(F32))
    tt = min(MOE_TOK_TILE, s)
    ltri = _const(np.tril(np.ones((tt, tt)), -1), BF16)
    row = pl.BlockSpec((tm, LANE), lambda i: (i, 0))
    return pl.pallas_call(
        _route_kernel,
        grid=(nt,),
        in_specs=[pl.BlockSpec((tm, d), lambda i: (i, 0)), _const_spec((1, d)), _const_spec((d, LANE)),
                  _const_spec((tt, tt))],
        out_specs=[row, row, pl.BlockSpec((SUBLANE, tm), lambda i: (0, i)),
                   pl.BlockSpec((nsub, 1, LANE), lambda i: (i, 0, 0)), pl.BlockSpec((1, LANE), lambda i: (0, 0))],
        out_shape=[jax.ShapeDtypeStruct((s, LANE), F32), jax.ShapeDtypeStruct((s, LANE), F32),
                   jax.ShapeDtypeStruct((SUBLANE, s), F32), jax.ShapeDtypeStruct((nt * nsub, 1, LANE), F32),
                   jax.ShapeDtypeStruct((1, LANE), F32)],
        scratch_shapes=[pltpu.VMEM((1, LANE), F32)],
        compiler_params=_params("arbitrary"),
        name="moe_route",
    )(x, norm_gain.reshape(1, d).astype(F32), router_p, ltri)


def _moe_ffn_kernel(be_ref, r0_ref, tlo_ref, thi_ref, nv_ref, h_ref, post_ref, wg_ref, wu_ref, wd_ref,
                    y_ref, x_sc, acc_sc, *, tt, nsub):
    b = pl.program_id(0)
    f = pl.program_id(1)
    nb = pl.num_programs(0)
    valid = b < nv_ref[0]
    tmb, d = acc_sc.shape
    gr = tmb // nsub
    last_tile = h_ref.shape[0] // tt - 1

    def picked(e, want, t):
        t0 = pl.multiple_of(t * tt, tt)
        p = post_ref[pl.ds(e, 1), pl.ds(t0, tt)]
        sel = jnp.where(p == want, 1.0, 0.0).astype(BF16)
        return _dot(sel, h_ref[pl.ds(t0, tt), :])

    def gather_head(blk, sb):
        e = be_ref[blk]
        lo = tlo_ref[blk * nsub + sb]
        hi = thi_ref[blk * nsub + sb]
        want = (lax.broadcasted_iota(jnp.int32, (gr, 1), 0) + (r0_ref[blk] + sb * gr)).astype(F32)
        rows = picked(e, want, lo)
        for k in range(1, MOE_GATHER_TILES):
            rows = rows + picked(e, jnp.where(lo + k <= hi, want, -2.0), jnp.minimum(lo + k, last_tile))
        return rows.astype(BF16)

    def gather_tail(slot, blk, sb):
        e = be_ref[blk]
        want = (lax.broadcasted_iota(jnp.int32, (gr, 1), 0) + (r0_ref[blk] + sb * gr)).astype(F32)
        rows = pl.ds(pl.multiple_of(sb * gr, gr), gr)

        def more(t, carry):
            x_sc[slot, rows, :] = (x_sc[slot, rows, :].astype(F32) + picked(e, want, t)).astype(BF16)
            return carry

        lax.fori_loop(tlo_ref[blk * nsub + sb] + MOE_GATHER_TILES, thi_ref[blk * nsub + sb] + 1, more, 0)

    @pl.when((b == 0) & (f == 0))
    def _():
        for sb in range(nsub):
            x_sc[0, pl.ds(sb * gr, gr), :] = gather_head(0, sb)
            gather_tail(0, 0, sb)

    nxt = jnp.minimum(b + 1, nb - 1)
    nslot = (b + 1) % 2

    @pl.when(valid)
    def _():
        x_sc[nslot, pl.ds(pl.multiple_of(f * gr, gr), gr), :] = gather_head(nxt, f)
        xb = x_sc[b % 2]
        a = _silu(_dot(xb, wg_ref[0])) * _dot(xb, wu_ref[0])
        acc_sc[...] = jnp.where(f == 0, 0.0, acc_sc[...]) + _dot(a.astype(BF16), wd_ref[0].astype(BF16))

    @pl.when(valid & (thi_ref[nxt * nsub + f] - tlo_ref[nxt * nsub + f] >= MOE_GATHER_TILES))
    def _():
        gather_tail(nslot, nxt, f)

    @pl.when(f == nsub - 1)
    def _():
        y_ref[...] = jnp.where(valid, acc_sc[...], 0.0).astype(y_ref.dtype)


def moe_ffn(h, post, sched, w_gate, w_up, w_down):
    s, d = h.shape
    ne, _, dff = w_gate.shape
    tmb = MOE_ROW_BLOCK
    tt = min(MOE_TOK_TILE, s)
    nb = sched[0].shape[0]
    nf = tmb // MOE_GATHER_ROWS
    tf = dff // nf

    def fidx(b, f, nv):
        return jnp.where(b < nv[0], f, nf - 1)

    grid_spec = pltpu.PrefetchScalarGridSpec(
        num_scalar_prefetch=5,
        grid=(nb, nf),
        in_specs=[pl.BlockSpec((s, d), lambda b, f, *_: (0, 0), pipeline_mode=pl.Buffered(1)),
                  pl.BlockSpec((SUBLANE, s), lambda b, f, *_: (0, 0), pipeline_mode=pl.Buffered(1)),
                  pl.BlockSpec((1, d, tf), lambda b, f, be, r0, tlo, thi, nv: (be[b], 0, fidx(b, f, nv))),
                  pl.BlockSpec((1, d, tf), lambda b, f, be, r0, tlo, thi, nv: (be[b], 0, fidx(b, f, nv))),
                  pl.BlockSpec((1, tf, d), lambda b, f, be, r0, tlo, thi, nv: (be[b], fidx(b, f, nv), 0))],
        out_specs=pl.BlockSpec((tmb, d), lambda b, f, *_: (b, 0)),
        scratch_shapes=[pltpu.VMEM((2, tmb, d), BF16), pltpu.VMEM((tmb, d), F32)],
    )
    return pl.pallas_call(
        functools.partial(_moe_ffn_kernel, tt=tt, nsub=nf),
        grid_spec=grid_spec,
        out_shape=jax.ShapeDtypeStruct((nb * tmb, d), BF16),
        compiler_params=pltpu.CompilerParams(dimension_semantics=("arbitrary", "arbitrary"),
                                             vmem_limit_bytes=MOE_VMEM_LIMIT),
        name="moe_ffn",
    )(*sched, h, post, w_gate, w_up, w_down)


def _moe_combine_kernel(kb_ref, off_ref, lim_ref, x_ref, pos_ref, gate_ref, fg_ref, *rest):
    y_refs, o_ref, acc_sc = rest[:-2], rest[-2], rest[-1]
    t = pl.program_id(0)
    yb = y_refs[0].shape[0]
    pos = pos_ref[...]
    gate = gate_ref[...]
    col = lax.broadcasted_iota(jnp.int32, (1, yb), 1).astype(F32)

    def routed(e):
        pe = pos[:, e:e + 1]
        ge = gate[:, e:e + 1]
        r = jnp.where(pe >= 0.0, pe + off_ref[t * N_EXPERTS + e].astype(F32), -1.0)
        return r, ge

    col2 = lax.broadcasted_iota(jnp.int32, (1, 2 * yb), 1).astype(F32)
    routed_sum = jnp.zeros(x_ref.shape, F32)
    for e in range(N_EXPERTS):
        r, ge = routed(e)
        pair = jnp.concatenate([y_refs[MOE_Y_FETCH * e][...], y_refs[MOE_Y_FETCH * e + 1][...]], axis=0)
        routed_sum = routed_sum + _dot(jnp.where(r == col2, ge, 0.0).astype(BF16), pair)
    acc_sc[...] = x_ref[...] + routed_sum
    for e in range(N_EXPERTS):
        for k in range(2, MOE_Y_FETCH):
            @pl.when(lim_ref[t * N_EXPERTS + e] > k * yb)
            def _(k=k, e=e):
                r, ge = routed(e)
                sel = jnp.where(r == col + float(k * yb), ge, 0.0).astype(BF16)
                acc_sc[...] += _dot(sel, y_refs[MOE_Y_FETCH * e + k][...])
    o_ref[...] = _rms(acc_sc[...], fg_ref[...]).astype(o_ref.dtype)


def moe_combine(x, pos, gate, y, kb, off, lim, final_gain):
    s, d = x.shape
    tm = min(MOE_TOK_TILE, s)
    yb = MOE_Y_BLOCK
    last = y.shape[0] // yb - 1

    def yspec(e, k):
        def index(t, kb_r, off_r, lim_r):
            blk = jnp.minimum(kb_r[t * N_EXPERTS + e] + k, last)
            return (blk if k < 1 else jnp.where(lim_r[t * N_EXPERTS + e] > k * yb, blk, 0), 0)
        return pl.BlockSpec((yb, d), index)

    grid_spec = pltpu.PrefetchScalarGridSpec(
        num_scalar_prefetch=3,
        grid=(s // tm,),
        in_specs=[pl.BlockSpec((tm, d), lambda t, *_: (t, 0)),
                  pl.BlockSpec((tm, LANE), lambda t, *_: (t, 0)),
                  pl.BlockSpec((tm, LANE), lambda t, *_: (t, 0)),
                  pl.BlockSpec((1, d), lambda t, *_: (0, 0))]
        + [yspec(e, k) for e in range(N_EXPERTS) for k in range(MOE_Y_FETCH)],
        out_specs=pl.BlockSpec((tm, d), lambda t, *_: (t, 0)),
        scratch_shapes=[pltpu.VMEM((tm, d), F32)],
    )
    return pl.pallas_call(
        _moe_combine_kernel,
        grid_spec=grid_spec,
        out_shape=jax.ShapeDtypeStruct((s, d), F32),
        compiler_params=_params("arbitrary"),
        name="moe_combine",
    )(kb, off, lim, x, pos, gate, final_gain.reshape(1, d).astype(F32), *([y] * (MOE_Y_FETCH * N_EXPERTS)))


def _moe_schedule(before, total, s):
    tmb, yb, gr = MOE_ROW_BLOCK, MOE_Y_BLOCK, MOE_GATHER_ROWS
    nb = 2 * s // tmb + N_EXPERTS
    counts = total[0, :N_EXPERTS].astype(jnp.int32)
    nblk = (counts + tmb - 1) // tmb
    end = jnp.cumsum(nblk)
    first = end - nblk
    nvalid = end[-1]
    b = jnp.minimum(jnp.arange(nb, dtype=jnp.int32), nvalid - 1)
    blk_e = jnp.sum(b[:, None] >= end[None, :], axis=1).astype(jnp.int32)
    r0 = (b - first[blk_e]) * tmb
    cb = before[:, 0, :N_EXPERTS].astype(jnp.int32)
    r0s = (r0[:, None] + gr * jnp.arange(tmb // gr, dtype=jnp.int32)[None, :]).reshape(-1)
    cbe = cb[:, jnp.repeat(blk_e, tmb // gr)]
    tlo = (jnp.sum(cbe <= r0s[None, :], axis=0) - 1).astype(jnp.int32)
    thi = (jnp.sum(cbe < (r0s + gr)[None, :], axis=0) - 1).astype(jnp.int32)
    sched = (blk_e, r0.astype(jnp.int32), tlo, thi, nvalid.reshape(1).astype(jnp.int32))
    row_start = first[None, :] * tmb + cb
    kb = row_start // yb
    off = first[None, :] * tmb - kb * yb
    n_te = jnp.concatenate([cb[1:], counts[None, :]], axis=0) - cb
    lim = row_start - kb * yb + n_te
    flat = lambda a: a.reshape(-1).astype(jnp.int32)
    return sched, flat(kb), flat(off), flat(lim)


def moe_sparse(h, x, norm_gain, router, w_gate, w_up, w_down, final_gain):
    s, _ = x.shape
    pos, gate, post, before, total = moe_route(x, norm_gain, router)
    sched, kb, off, lim = _moe_schedule(before, total, s)
    y = moe_ffn(h, post, sched, w_gate, w_up, w_down)
    return moe_combine(x, pos, gate, y, kb, off, lim, final_gain)


FOX_F0 = 7 * BRANCH_W


def _split_w_in_kernel(wt_ref, mix_ref, f_ref, gate_ref):
    rest0 = FOX_F0 + N_HEADS
    gate0 = N_MIX_COLS + N_HEADS
    mix_ref[:, :FOX_F0] = wt_ref[pl.ds(0, FOX_F0), :].T.astype(BF16)
    mix_ref[:, FOX_F0:] = wt_ref[pl.ds(rest0, N_MIX_COLS - FOX_F0), :].T.astype(BF16)
    gate_ref[...] = wt_ref[pl.ds(gate0, N_BRANCH * D_MODEL), :].T.astype(BF16)
    f_rows = jnp.concatenate([wt_ref[pl.ds(FOX_F0, N_HEADS), :], jnp.zeros((LANE - N_HEADS, LANE), F32)], axis=0)
    f_ref[...] = f_rows.T.astype(BF16)


def _mixer_weights(w_in, layer, tr=LANE):
    _, d, cols = w_in.shape
    w_t = jnp.swapaxes(w_in, 1, 2)
    row = lambda w: pl.BlockSpec((tr, w), lambda i: (i, 0))
    return pl.pallas_call(
        _split_w_in_kernel,
        grid=(d // tr,),
        in_specs=[pl.BlockSpec((None, cols, tr), lambda i: (layer, 0, i))],
        out_specs=[row(N_MIX_COLS), row(LANE), row(N_BRANCH * D_MODEL)],
        out_shape=[jax.ShapeDtypeStruct((d, N_MIX_COLS), BF16), jax.ShapeDtypeStruct((d, LANE), BF16),
                   jax.ShapeDtypeStruct((d, N_BRANCH * D_MODEL), BF16)],
        compiler_params=_params("parallel"),
        name="split_w_in",
    )(w_t)


def kernel(x, w_in, w_branch, w_out, norm_mix_g, hgrn_lb_logits, hgrn_norm_g, fox_f_bias, pool_w, pool_scale,
           ret_gn_g, ret_gn_b, norm_ffn_g, ffn_w_gate, ffn_w_up, ffn_w_down, moe_router, moe_w_gate, moe_w_up,
           moe_w_down, final_norm_g):
    b, s, d = x.shape
    assert b == 1 and d == D_MODEL
    depth = w_in.shape[0]
    assert depth == 2, "layer 0 uses the dense FFN, layer 1 the experts and the final norm"
    xs = x.reshape(s, d)
    h = None
    out = None
    for layer in range(depth):
        w_mix, w_f, w_gate = _mixer_weights(w_in, layer)
        if layer == 0:
            proj, h = norm_matmul(xs, norm_mix_g[0], w_mix)
        else:
            proj = matmul(h, w_mix)
        ct, c_rows, fox_stats = fox_gate(h, proj, w_f, fox_f_bias[layer])
        branches = (
            hgrn2(proj, hgrn_lb_logits, hgrn_norm_g[layer], layer),
            fox_attention(proj, ct, c_rows, fox_stats),
            pool_mixer(proj, pool_w[layer], pool_scale[layer]),
            retention(proj, ret_gn_g[layer], ret_gn_b[layer]),
        )
        li = layer // 2
        if layer % 2 == 0:
            xs, h2, (ffn_wg, ffn_wu) = merge(h, branches, xs, w_gate, w_branch[layer].astype(BF16),
                                             w_out[layer].astype(BF16), norm_ffn_g[layer],
                                             cast=(ffn_w_gate[li], ffn_w_up[li]))
            xs, h = ffn_dense(h2, xs, ffn_wg, ffn_wu, ffn_w_down[li], norm_mix_g[layer + 1])
        else:
            xs, h2, (moe_wg, moe_wu) = merge(h, branches, xs, w_gate, w_branch[layer].astype(BF16),
                                             w_out[layer].astype(BF16), norm_ffn_g[layer], tm=MERGE_CAST_TILE,
                                             cast=(moe_w_gate[li], moe_w_up[li]))
            out = moe_sparse(h2, xs, norm_ffn_g[layer], moe_router[li], moe_wg, moe_wu, moe_w_down[li],
                             final_norm_g)
    return out.reshape(b, s, d)
```

```python
import functools
import math

import jax
import jax.numpy as jnp
import numpy as np
from jax import lax
from jax.experimental import pallas as pl
from jax.experimental.pallas import tpu as pltpu

D_MODEL = 1024
N_BRANCH = 4
BRANCH_W = D_MODEL // N_BRANCH
HEAD_DIM = 64
N_HEADS = BRANCH_W // HEAD_DIM
POOL_WINDOWS = (2, 4, 8, 16)
POOL_GROUP = BRANCH_W // len(POOL_WINDOWS)
POOL_HALO = 16
RET_DECAY_BASE = 5.0
ROPE_BASE = 10000.0
D_FF = 7 * D_MODEL // 2
N_EXPERTS = 8
RMS_EPS = 1e-6
LN_EPS = 1e-5
N_MIX_COLS = 12 * BRANCH_W

LANE = 128
SUBLANE = 8
VMEM_LIMIT = 56 * 1024 * 1024

HG_CHUNK = 64
HG_SUB = 16
HG_FAST_MIN_LOGDECAY = -60.0
RET_CHUNK = 256

F32 = jnp.float32
BF16 = jnp.bfloat16
NT_DIMS = (((1,), (1,)), ((), ()))


def _params(*sem):
    return pltpu.CompilerParams(dimension_semantics=sem, vmem_limit_bytes=VMEM_LIMIT)


def _const_spec(shape):
    nd = len(shape)
    return pl.BlockSpec(shape, lambda *_: (0,) * nd, pipeline_mode=pl.Buffered(1))


def _split3(x):
    hi = x.astype(BF16)
    r1 = x - hi.astype(F32)
    mid = r1.astype(BF16)
    lo = (r1 - mid.astype(F32)).astype(BF16)
    return hi, mid, lo


def _dot(a, b):
    return jnp.dot(a, b, preferred_element_type=F32)


def _dot_nt(a, b):
    return lax.dot_general(a, b, NT_DIMS, preferred_element_type=F32)


def _dot_exact_rhs(x, m_bf16, terms=3):
    return sum(_dot(part, m_bf16) for part in _split3(x)[:terms])


def _dot_exact_lhs(m_bf16, x, terms=3):
    return sum(_dot(m_bf16, part) for part in _split3(x)[:terms])


def _sigmoid(x):
    return 1.0 / (1.0 + jnp.exp(-x))


def _silu(x):
    return x * _sigmoid(x)


def _rms(x, gain):
    return x * lax.rsqrt(jnp.mean(x * x, axis=-1, keepdims=True) + RMS_EPS) * gain


def _const(a, dtype=F32):
    return jnp.asarray(np.asarray(a, np.float32), dtype)


def _head_of(n):
    return np.arange(n) // HEAD_DIM


def _head_ones():
    h = _head_of(BRANCH_W)
    return _const(h[:, None] == h[None, :], BF16)


def _head_masks():
    return _const(_head_of(BRANCH_W)[None, :] == np.arange(N_HEADS)[:, None])


def _norm_matmul_kernel(x_ref, g_ref, b_ref, o_ref, h_ref):
    @pl.when(pl.program_id(1) == 0)
    def _():
        h_ref[...] = _rms(x_ref[...], g_ref[...]).astype(h_ref.dtype)

    o_ref[...] = _dot(h_ref[...], b_ref[...]).astype(o_ref.dtype)


def norm_matmul(x, gain, b, tm=2048, tn=1024):
    m, k = x.shape
    _, n = b.shape
    tm = min(tm, m)
    return pl.pallas_call(
        _norm_matmul_kernel,
        grid=(m // tm, n // tn),
        in_specs=[pl.BlockSpec((tm, k), lambda i, j: (i, 0)), _const_spec((1, k)),
                  pl.BlockSpec((k, tn), lambda i, j: (0, j))],
        out_specs=[pl.BlockSpec((tm, tn), lambda i, j: (i, j)), pl.BlockSpec((tm, k), lambda i, j: (i, 0))],
        out_shape=[jax.ShapeDtypeStruct((m, n), BF16), jax.ShapeDtypeStruct((m, k), BF16)],
        compiler_params=_params("arbitrary", "arbitrary"),
        name="in_proj_norm",
    )(x, gain.reshape(1, k).astype(F32), b)


def _matmul_kernel(a_ref, b_ref, o_ref):
    o_ref[...] = _dot(a_ref[...], b_ref[...]).astype(o_ref.dtype)


def matmul(a, b, out_dtype=BF16, tm=2048, tn=1024):
    m, k = a.shape
    _, n = b.shape
    tm = min(tm, m)
    return pl.pallas_call(
        _matmul_kernel,
        grid=(n // tn, m // tm),
        in_specs=[pl.BlockSpec((tm, k), lambda j, i: (i, 0)),
                  pl.BlockSpec((k, tn), lambda j, i: (0, j))],
        out_specs=pl.BlockSpec((tm, tn), lambda j, i: (i, j)),
        out_shape=jax.ShapeDtypeStruct((m, n), out_dtype),
        compiler_params=_params("parallel", "parallel"),
        name="in_proj",
    )(a, b)


def _hgrn_kernel(q_ref, f_ref, i_ref, g_ref, lbl_ref, ng_ref, ones_ref, hm_ref, tril_ref, halfsum_ref, fmask_ref,
                 o_ref, st_ref, bpad, kpad, vpad, astack, lf_sc, kk_sc, o_sc, *, layer, tile):
    c, sub = HG_CHUNK, HG_SUB
    nsub = c // sub
    half = c // 2

    @pl.when(pl.program_id(0) == 0)
    def _():
        st_ref[...] = jnp.zeros_like(st_ref)
        bpad[...] = jnp.zeros_like(bpad)
        kpad[...] = jnp.zeros_like(kpad)
        vpad[...] = jnp.zeros_like(vpad)

    lbl = lbl_ref[...]
    e = jnp.exp(lbl - jnp.max(lbl, axis=0, keepdims=True))
    p = e / jnp.sum(e, axis=0, keepdims=True)
    lb = jnp.zeros((1, BRANCH_W), F32)
    for l in range(1, layer + 1):
        lb = lb + p[l:l + 1, :]

    ones_bd = ones_ref[...]
    hm = hm_ref[...]
    tril = tril_ref[...]
    row = lax.broadcasted_iota(jnp.int32, (c, 1), 0)
    row_in_sub = row % sub
    bd_mask = ones_bd.astype(F32)

    sig = _sigmoid(f_ref[...].astype(F32))
    logf_all = jnp.log(lb + (1.0 - lb) * sig)
    lf_sc[...] = _dot_exact_lhs(tril, logf_all, terms=2)
    kk_sc[...] = (1.0 - lb) * (1.0 - sig)
    min_decay = jnp.min(_dot(halfsum_ref[...], logf_all.astype(BF16)))

    def load(ci):
        r0 = pl.multiple_of(ci * c, c)
        q = q_ref[pl.ds(r0, c), :].astype(F32)
        v = i_ref[pl.ds(r0, c), :].astype(F32)
        kk = kk_sc[pl.ds(r0, c), :]
        b = lf_sc[pl.ds(r0, c), :]
        return r0, q, v, kk, b

    def finish(r0, q_decayed, v, kk, b, intra):
        st = st_ref[...]
        inter = _dot_nt(q_decayed.astype(BF16), st.astype(BF16))
        b_last = b[c - 1:c, :]
        ks_end = (kk * jnp.exp(b_last - b)).astype(BF16)
        upd = _dot(v.T.astype(BF16), ks_end)
        st_ref[...] = st * jnp.exp(b_last) + upd * bd_mask
        o_sc[pl.ds(r0, c), :] = intra + inter

    def fast_chunk(ci, carry):
        r0, q, v, kk, b = load(ci)
        second = row >= half
        m_row = b[half - 1:half, :]
        mref = jnp.where(second, m_row, 0.0)
        qp = q * jnp.exp(b - mref)
        kp = kk * jnp.exp(mref - b)
        e_m = jnp.exp(m_row)
        kaug = jnp.concatenate([kp, kp[:half, :] * e_m], axis=0)
        vaug = jnp.concatenate([v, v[:half, :]], axis=0)
        k_heads = jnp.concatenate([kaug * hm[h:h + 1, :] for h in range(N_HEADS)], axis=0).astype(BF16)
        v_heads = jnp.concatenate([vaug * hm[h:h + 1, :] for h in range(N_HEADS)], axis=0).astype(BF16)
        sc = jnp.where(fmask_ref[...] > 0.0, _dot_nt(qp.astype(BF16), k_heads), 0.0)
        intra = _dot(sc.astype(BF16), v_heads)
        finish(r0, jnp.where(second, qp * e_m, qp), v, kk, b, intra)
        return carry

    def exact_chunk(ci, carry):
        r0, q, v, kk, b = load(ci)

        bpad[pl.ds(sub, c), :] = b
        kpad[pl.ds(sub, c), :] = kk
        vpad[pl.ds(sub, c), :] = v

        for d in range(sub):
            b_d = bpad[pl.ds(sub - d, c), :]
            k_d = kpad[pl.ds(sub - d, c), :]
            a = jnp.where(row_in_sub >= d, q * k_d * jnp.exp(b - b_d), 0.0)
            astack[pl.ds(d * c, c), :] = a.astype(BF16)
        pall = _dot(astack[...], ones_bd)
        intra = jnp.zeros((c, BRANCH_W), F32)
        for d in range(sub):
            intra = intra + pall[d * c:(d + 1) * c, :] * vpad[pl.ds(sub - d, c), :]

        pieces = [jnp.zeros((sub, BRANCH_W), F32)]
        for si in range(1, nsub):
            lo = si * sub
            m_i = b[lo - 1:lo, :]
            qs = q[lo:lo + sub, :] * jnp.exp(b[lo:lo + sub, :] - m_i)
            ks = (kk[:lo, :] * jnp.exp(m_i - b[:lo, :])).astype(BF16)
            qx = jnp.concatenate([qs * hm[h:h + 1, :] for h in range(N_HEADS)], axis=0).astype(BF16)
            sc = _dot_nt(qx, ks)
            r = _dot(sc.astype(BF16), v[:lo, :].astype(BF16))
            acc = jnp.zeros((sub, BRANCH_W), F32)
            for h in range(N_HEADS):
                acc = acc + r[h * sub:(h + 1) * sub, :] * hm[h:h + 1, :]
            pieces.append(acc)
        intra = intra + jnp.concatenate(pieces, axis=0)
        finish(r0, q * jnp.exp(b), v, kk, b, intra)
        return carry

    lax.cond(min_decay >= HG_FAST_MIN_LOGDECAY,
             lambda: lax.fori_loop(0, tile // c, fast_chunk, 0, unroll=2),
             lambda: lax.fori_loop(0, tile // c, exact_chunk, 0))

    o = o_sc[...]
    ms = _dot_exact_rhs(o * o, ones_bd, terms=2) * (1.0 / HEAD_DIM)
    y = o * lax.rsqrt(ms + RMS_EPS) * ng_ref[...] * _silu(g_ref[...].astype(F32))
    o_ref[...] = y.astype(o_ref.dtype)


def hgrn2(proj, lb_logits, norm_g, layer, tile=512):
    s = proj.shape[0]
    depth = lb_logits.shape[0]
    c, sub = HG_CHUNK, HG_SUB
    half = c // 2
    tile = min(tile, s)
    col = lambda j: pl.BlockSpec((tile, BRANCH_W), lambda i, j=j: (i, j))
    pos = np.arange(tile)
    tril = _const((pos[:, None] // c == pos[None, :] // c) & (pos[None, :] <= pos[:, None]), BF16)
    nhalf = tile // half
    halfsum = _const(np.arange(tile)[None, :] // half == np.arange(nhalf)[:, None], BF16)
    t = np.arange(c)[:, None]
    col_s = np.arange(c + half)[None, :]
    same_half = (col_s < c) & (col_s // half == t // half) & (col_s <= t)
    cross = (col_s >= c) & (t >= half)
    fmask = _const(np.tile(same_half | cross, (1, N_HEADS)))
    return pl.pallas_call(
        functools.partial(_hgrn_kernel, layer=layer, tile=tile),
        grid=(s // tile,),
        in_specs=[col(0), col(1), col(2), col(3),
                  _const_spec((depth, BRANCH_W)), _const_spec((1, BRANCH_W)),
                  _const_spec((BRANCH_W, BRANCH_W)), _const_spec((N_HEADS, BRANCH_W)),
                  _const_spec((tile, tile)), _const_spec((nhalf, tile)),
                  _const_spec((c, N_HEADS * (c + half)))],
        out_specs=pl.BlockSpec((tile, BRANCH_W), lambda i: (i, 0)),
        out_shape=jax.ShapeDtypeStruct((s, BRANCH_W), BF16),
        scratch_shapes=[pltpu.VMEM((BRANCH_W, BRANCH_W), F32),
                        pltpu.VMEM((c + sub, BRANCH_W), F32),
                        pltpu.VMEM((c + sub, BRANCH_W), F32),
                        pltpu.VMEM((c + sub, BRANCH_W), F32),
                        pltpu.VMEM((sub * c, BRANCH_W), BF16),
                        pltpu.VMEM((tile, BRANCH_W), F32),
                        pltpu.VMEM((tile, BRANCH_W), F32),
                        pltpu.VMEM((tile, BRANCH_W), F32)],
        compiler_params=_params("arbitrary"),
        name="hgrn2",
    )(proj, proj, proj, proj, lb_logits.astype(F32), norm_g.reshape(1, BRANCH_W).astype(F32),
      _head_ones(), _head_masks(), tril, halfsum, fmask)


FOX_TILE = 256
FOX_GATE_BLOCKS = 2
FOX_NSTAT = 16
FOX_FIXED_MAX = 30.0
FOX_SKIP_LOG = 40.0


def _fox_gate_kernel(h_ref, q_ref, k_ref, wf_ref, bias_ref, tril_ref, ones_ref,
                     ct_ref, c_ref, stat_ref, carry_ref, kmax_ref):
    @pl.when(pl.program_id(0) == 0)
    def _():
        carry_ref[...] = jnp.zeros_like(carry_ref)
        kmax_ref[...] = jnp.zeros_like(kmax_ref)

    logit = _dot(h_ref[...], wf_ref[...]) + bias_ref[...]
    logf = jnp.minimum(logit, 0.0) - jnp.log(1.0 + jnp.exp(-jnp.abs(logit)))
    carry = carry_ref[...]
    parts = []
    for blk in range(logf.shape[0] // FOX_TILE):
        part = _dot_exact_lhs(tril_ref[...], logf[blk * FOX_TILE:(blk + 1) * FOX_TILE]) + carry
        carry = part[-1:, :]
        parts.append(part)
    cum = jnp.concatenate(parts, axis=0)
    carry_ref[...] = carry
    ct_ref[...] = cum.T[:SUBLANE, :]
    c_ref[...] = cum

    ones_bd = ones_ref[...]
    q = q_ref[...].astype(F32)
    k = k_ref[...].astype(F32)
    scale = HEAD_DIM ** -0.5
    head_lane = lax.broadcasted_iota(jnp.int32, (1, BRANCH_W), 1) // HEAD_DIM
    c_heads = jnp.zeros(q.shape, F32)
    for h in range(N_HEADS):
        c_heads = jnp.where(head_lane == h, cum[:, h:h + 1], c_heads)
    slack = 1.0 + 2.0 ** -6
    qn = jnp.sqrt(_dot((q * q).astype(BF16), ones_bd)) * (scale * slack)
    kn = jnp.sqrt(_dot((k * k).astype(BF16), ones_bd)) * slack
    diag = _dot((q * k).astype(BF16), ones_bd) * scale - (2.0 ** -6) * qn * kn
    e_row = c_heads - diag
    kmax = kmax_ref[...]
    for blk in range(q.shape[0] // FOX_TILE):
        rs = slice(blk * FOX_TILE, (blk + 1) * FOX_TILE)
        kmax = jnp.maximum(kmax, jnp.max(kn[rs], axis=0, keepdims=True))
        rows = [jnp.max(qn[rs], axis=0, keepdims=True),
                jnp.max(e_row[rs], axis=0, keepdims=True),
                kmax,
                c_heads[(blk + 1) * FOX_TILE - 1:(blk + 1) * FOX_TILE, :]]
        stat_ref[blk] = jnp.concatenate(rows + [jnp.zeros((SUBLANE - len(rows), BRANCH_W), F32)], axis=0)
    kmax_ref[...] = kmax


def fox_gate(h, proj, w_f, f_bias):
    s, d = h.shape
    tile = min(FOX_GATE_BLOCKS * FOX_TILE, s)
    nblk = tile // FOX_TILE
    bias = jnp.zeros((1, LANE), F32).at[0, :N_HEADS].set(f_bias.astype(F32))
    tril = _const(np.tril(np.ones((FOX_TILE, FOX_TILE))), BF16)
    ct, c_rows, stats = pl.pallas_call(
        _fox_gate_kernel,
        grid=(s // tile,),
        in_specs=[pl.BlockSpec((tile, d), lambda i: (i, 0)),
                  pl.BlockSpec((tile, BRANCH_W), lambda i: (i, 4)),
                  pl.BlockSpec((tile, BRANCH_W), lambda i: (i, 5)),
                  _const_spec((d, LANE)), _const_spec((1, LANE)), _const_spec((FOX_TILE, FOX_TILE)),
                  _const_spec((BRANCH_W, BRANCH_W))],
        out_specs=[pl.BlockSpec((SUBLANE, tile), lambda i: (0, i)),
                   pl.BlockSpec((tile, LANE), lambda i: (i, 0)),
                   pl.BlockSpec((nblk, SUBLANE, BRANCH_W), lambda i: (i, 0, 0))],
        out_shape=[jax.ShapeDtypeStruct((SUBLANE, s), F32),
                   jax.ShapeDtypeStruct((s, LANE), F32),
                   jax.ShapeDtypeStruct((s // FOX_TILE, SUBLANE, BRANCH_W), F32)],
        scratch_shapes=[pltpu.VMEM((1, LANE), F32), pltpu.VMEM((1, BRANCH_W), F32)],
        compiler_params=_params("arbitrary"),
        name="fox_gate",
    )(h, proj, proj, w_f, bias, tril, _head_ones())
    return ct, c_rows, stats[:, :4, ::HEAD_DIM].reshape(-1)


def _fox_kernel(stat_ref, q_ref, k_ref, v_ref, ct_ref, c_ref, hm_ref, o_ref, m_sc, l_sc, acc_sc, *, tq):
    i = pl.program_id(0)
    q0 = pl.multiple_of(i * tq, tq)
    hm = hm_ref[...]
    q = q_ref[...].astype(F32) * (HEAD_DIM ** -0.5)
    qh = [(q * hm[h:h + 1, :]).astype(BF16) for h in range(N_HEADS)]
    c_q0 = ct_ref[:, pl.ds(q0, tq)][:, 0:1]

    first = i
    for h in range(N_HEADS):
        qmax = stat_ref[i * FOX_NSTAT + h]
        emax = stat_ref[i * FOX_NSTAT + N_HEADS + h]

        def needed(j, h=h, qmax=qmax, emax=emax):
            jc = jnp.maximum(j, 0)
            bound = (qmax * stat_ref[jc * FOX_NSTAT + 2 * N_HEADS + h] + emax
                     - stat_ref[jc * FOX_NSTAT + 3 * N_HEADS + h])
            return (j >= 0) & (bound >= -FOX_SKIP_LOG)

        last_dropped = lax.while_loop(needed, lambda j: j - 1, i - 1)
        first = jnp.minimum(first, last_dropped + 1)

    l_sc[...] = jnp.zeros_like(l_sc)
    acc_sc[...] = jnp.zeros_like(acc_sc)

    def causal(sc):
        r = lax.broadcasted_iota(jnp.int32, (tq, tq), 0)
        cidx = lax.broadcasted_iota(jnp.int32, (tq, tq), 1)
        return jnp.where(cidx <= r, sc, -jnp.inf)

    def online_block(s0, diagonal):
        kb = k_ref[pl.ds(s0, tq), :]
        vb = v_ref[pl.ds(s0, tq), :]
        bias = c_q0 - ct_ref[:, pl.ds(s0, tq)]
        for h in range(N_HEADS):
            sc = _dot_nt(qh[h], kb) + bias[h:h + 1, :]
            if diagonal:
                sc = causal(sc)
            m_prev = m_sc[h]
            m_new = jnp.maximum(m_prev, jnp.max(sc, axis=1, keepdims=True))
            alpha = jnp.exp(m_prev - m_new)
            p = jnp.exp(sc - jnp.tile(m_new, (1, tq // LANE)))
            l_sc[h] = alpha * l_sc[h] + jnp.sum(p, axis=1, keepdims=True)
            acc_sc[h] = acc_sc[h] * jnp.tile(alpha, (1, BRANCH_W // LANE)) + _dot(p.astype(BF16), vb)
            m_sc[h] = m_new

    tops = [stat_ref[i * FOX_NSTAT + h] * stat_ref[i * FOX_NSTAT + 2 * N_HEADS + h] for h in range(N_HEADS)]
    c_tile = c_ref[...]
    shift = [c_tile[:, h:h + 1] - c_q0[h:h + 1, :] - tops[h] for h in range(N_HEADS)]

    def fixed_block(s0, diagonal):
        kb = k_ref[pl.ds(s0, tq), :]
        vb = v_ref[pl.ds(s0, tq), :]
        bias = c_q0 - ct_ref[:, pl.ds(s0, tq)]
        for h in range(N_HEADS):
            sc = _dot_nt(qh[h], kb) + bias[h:h + 1, :] + shift[h]
            if diagonal:
                sc = causal(sc)
            p = jnp.exp(sc)
            l_sc[h] += p[:, :LANE] + p[:, LANE:]
            acc_sc[h] += _dot(p.astype(BF16), vb)

    def run(block, row_sum):
        lax.fori_loop(first, i, lambda j, carry: (block(pl.multiple_of(j * tq, tq), False), carry)[1], 0)
        block(q0, True)
        out = jnp.zeros((tq, BRANCH_W), F32)
        for h in range(N_HEADS):
            out = out + acc_sc[h] * hm[h:h + 1, :] / row_sum(l_sc[h])
        o_ref[...] = out.astype(o_ref.dtype)

    def run_online():
        m_sc[...] = jnp.full_like(m_sc, -jnp.inf)
        run(online_block, lambda l: jnp.tile(l, (1, BRANCH_W // LANE)))

    def run_fixed():
        run(fixed_block, lambda l: jnp.sum(l, axis=1, keepdims=True))

    lax.cond(functools.reduce(jnp.maximum, tops) <= FOX_FIXED_MAX, run_fixed, run_online)


def fox_attention(proj, ct, c_rows, stats):
    s = proj.shape[0]
    tq = min(FOX_TILE, s)
    full = lambda j: pl.BlockSpec((s, BRANCH_W), lambda i, j=j: (0, j), pipeline_mode=pl.Buffered(1))
    return pl.pallas_call(
        functools.partial(_fox_kernel, tq=tq),
        grid=(s // tq,),
        in_specs=[pl.BlockSpec(memory_space=pltpu.SMEM),
                  pl.BlockSpec((tq, BRANCH_W), lambda i: (i, 4)), full(5), full(6),
                  _const_spec((SUBLANE, s)), pl.BlockSpec((tq, LANE), lambda i: (i, 0)),
                  _const_spec((N_HEADS, BRANCH_W))],
        out_specs=pl.BlockSpec((tq, BRANCH_W), lambda i: (i, 0)),
        out_shape=jax.ShapeDtypeStruct((s, BRANCH_W), BF16),
        scratch_shapes=[pltpu.VMEM((N_HEADS, tq, LANE), F32),
                        pltpu.VMEM((N_HEADS, tq, LANE), F32),
                        pltpu.VMEM((N_HEADS, tq, BRANCH_W), F32)],
        compiler_params=_params("parallel"),
        name="fox_attention",
    )(stats, proj, proj, proj, ct, c_rows, _head_masks())


def _pool_kernel(u_ref, w_ref, scale_ref, o_ref, ext, *, tile):
    i = pl.program_id(0)

    @pl.when(i == 0)
    def _():
        ext[pl.ds(0, POOL_HALO), :] = jnp.zeros((POOL_HALO, BRANCH_W), F32)

    u = u_ref[...].astype(F32)
    ext[pl.ds(POOL_HALO, tile), :] = u
    pos = (i * tile + lax.broadcasted_iota(jnp.int32, (tile, 1), 0) + 1).astype(F32)
    halves = []
    for half in range(BRANCH_W // LANE):
        lanes = pl.ds(half * LANE, LANE)
        w_small, w_big = POOL_WINDOWS[2 * half], POOL_WINDOWS[2 * half + 1]
        run = u[:, half * LANE:(half + 1) * LANE]
        sums = {}
        for j in range(1, w_big):
            if j == w_small:
                sums[w_small] = run
            run = run + ext[pl.ds(POOL_HALO - j, tile), lanes]
        sums[w_big] = run
        lane = lax.broadcasted_iota(jnp.int32, (1, LANE), 1)
        small = lane < POOL_GROUP
        total = jnp.where(small, sums[w_small], sums[w_big])
        count = jnp.where(small, jnp.minimum(pos, float(w_small)), jnp.minimum(pos, float(w_big)))
        halves.append(total / count)
    mean = jnp.concatenate(halves, axis=1)
    d = (mean - u).astype(BF16)
    y = _dot(d, w_ref[...]) * scale_ref[...]
    o_ref[...] = y.astype(o_ref.dtype)
    ext[pl.ds(0, POOL_HALO), :] = u[tile - POOL_HALO:, :]


def pool_mixer(proj, w_pool, scale, tile=2048):
    s = proj.shape[0]
    tile = min(tile, s)
    ng = len(POOL_WINDOWS)
    w_bd = jnp.zeros((BRANCH_W, BRANCH_W), F32)
    for gi in range(ng):
        lo = gi * POOL_GROUP
        w_bd = w_bd.at[lo:lo + POOL_GROUP, lo:lo + POOL_GROUP].set(w_pool[gi].astype(F32))
    return pl.pallas_call(
        functools.partial(_pool_kernel, tile=tile),
        grid=(s // tile,),
        in_specs=[pl.BlockSpec((tile, BRANCH_W), lambda i: (i, 7)),
                  _const_spec((BRANCH_W, BRANCH_W)), _const_spec((1, BRANCH_W))],
        out_specs=pl.BlockSpec((tile, BRANCH_W), lambda i: (i, 0)),
        out_shape=jax.ShapeDtypeStruct((s, BRANCH_W), BF16),
        scratch_shapes=[pltpu.VMEM((tile + POOL_HALO, BRANCH_W), F32)],
        compiler_params=_params("arbitrary"),
        name="pool_mixer",
    )(proj, w_bd.astype(BF16), scale.reshape(1, BRANCH_W).astype(F32))


def _ret_kernel(q_ref, k_ref, v_ref, g_ref, cos_ref, sin_ref, perm_ref, ones_ref, hm_ref,
                dstack_ref, xi_ref, zeta_ref, gc_ref, gng_ref, gnb_ref, o_ref, st_ref, *, tile):
    c = RET_CHUNK

    @pl.when(pl.program_id(0) == 0)
    def _():
        st_ref[...] = jnp.zeros_like(st_ref)

    perm = perm_ref[...]
    ones_bd = ones_ref[...]
    bd_mask = ones_bd.astype(F32)
    hm = hm_ref[...]

    cos = cos_ref[...]
    sin = sin_ref[...]
    q_all = q_ref[...]
    k_all = k_ref[...]
    qr_all = q_all.astype(F32) * cos + _dot(q_all, perm) * sin
    kr_all = (k_all.astype(F32) * cos + _dot(k_all, perm) * sin) * (HEAD_DIM ** -0.5)

    outs = []
    for ci in range(tile // c):
        r0 = ci * c
        qr = qr_all[r0:r0 + c, :]
        kr = kr_all[r0:r0 + c, :]
        v = v_ref[pl.ds(r0, c), :]

        qx = jnp.concatenate([qr * hm[h:h + 1, :] for h in range(N_HEADS)], axis=0).astype(BF16)
        sc = _dot_nt(qx, kr.astype(BF16)) * dstack_ref[...]
        r = _dot(sc.astype(BF16), v)
        intra = jnp.zeros((c, BRANCH_W), F32)
        for h in range(N_HEADS):
            intra = intra + r[h * c:(h + 1) * c, :] * hm[h:h + 1, :]

        st = st_ref[...]
        inter = _dot_nt((qr * xi_ref[...]).astype(BF16), st.astype(BF16))
        upd = _dot(v.astype(F32).T.astype(BF16), (kr * zeta_ref[...]).astype(BF16))
        st_ref[...] = st * gc_ref[...] + upd * bd_mask
        outs.append(intra + inter)

    o = jnp.concatenate(outs, axis=0)
    mu = _dot_exact_rhs(o, ones_bd, terms=2) * (1.0 / HEAD_DIM)
    cen = o - mu
    var = _dot_exact_rhs(cen * cen, ones_bd, terms=2) * (1.0 / HEAD_DIM)
    y = cen * lax.rsqrt(var + LN_EPS) * gng_ref[...] + gnb_ref[...]
    o_ref[...] = (y * _silu(g_ref[...].astype(F32))).astype(o_ref.dtype)


def _rope_tables(s):
    half = HEAD_DIM // 2
    pos = np.arange(s, dtype=np.float64)
    inv_freq = ROPE_BASE ** (-np.arange(half, dtype=np.float64) / half)
    lane = np.arange(BRANCH_W)
    ang = pos[:, None] * inv_freq[lane % half][None, :]
    sign = np.where(lane % HEAD_DIM < half, -1.0, 1.0)
    return _const(np.cos(ang)), _const(np.sin(ang) * sign[None, :])


def _ret_constants():
    c = RET_CHUNK
    half = HEAD_DIM // 2
    lane = np.arange(BRANCH_W)
    partner = np.where(lane % HEAD_DIM < half, lane + half, lane - half)
    perm = lane[:, None] == partner[None, :]
    log_gamma = np.log1p(-np.exp2(-RET_DECAY_BASE - np.arange(N_HEADS, dtype=np.float64)))
    ci = np.arange(c, dtype=np.float64)
    diff = ci[:, None] - ci[None, :]
    intra = np.where(diff >= 0, np.exp(diff * log_gamma[:, None, None]), 0.0)
    dstack = intra.reshape(N_HEADS * c, c)
    lg_lane = np.repeat(log_gamma, HEAD_DIM)[None, :]
    xi = np.exp((ci[:, None] + 1.0) * lg_lane)
    zeta = np.exp((c - 1.0 - ci[:, None]) * lg_lane)
    gc = np.exp(c * lg_lane)
    return _const(perm, BF16), _const(dstack), _const(xi), _const(zeta), _const(gc)


def retention(proj, gn_g, gn_b, tile=1024):
    s = proj.shape[0]
    c = RET_CHUNK
    tile = min(tile, s)
    cos_t, sin_t = _rope_tables(s)
    perm, dstack, xi, zeta, gc = _ret_constants()
    col = lambda j: pl.BlockSpec((tile, BRANCH_W), lambda i, j=j: (i, j))
    row = pl.BlockSpec((tile, BRANCH_W), lambda i: (i, 0))
    return pl.pallas_call(
        functools.partial(_ret_kernel, tile=tile),
        grid=(s // tile,),
        in_specs=[col(8), col(9), col(10), col(11), row, row,
                  _const_spec((BRANCH_W, BRANCH_W)), _const_spec((BRANCH_W, BRANCH_W)),
                  _const_spec((N_HEADS, BRANCH_W)), _const_spec((N_HEADS * c, c)),
                  _const_spec((c, BRANCH_W)), _const_spec((c, BRANCH_W)), _const_spec((1, BRANCH_W)),
                  _const_spec((1, BRANCH_W)), _const_spec((1, BRANCH_W))],
        out_specs=row,
        out_shape=jax.ShapeDtypeStruct((s, BRANCH_W), BF16),
        scratch_shapes=[pltpu.VMEM((BRANCH_W, BRANCH_W), F32)],
        compiler_params=_params("arbitrary"),
        name="retention",
    )(proj, proj, proj, proj, cos_t, sin_t, perm, _head_ones(), _head_masks(), dstack, xi, zeta, gc,
      gn_g.reshape(1, BRANCH_W).astype(F32), gn_b.reshape(1, BRANCH_W).astype(F32))


def _merge_kernel(h_ref, o0_ref, o1_ref, o2_ref, o3_ref, x_ref, wg_ref, wb_ref, wo_ref, g_ref, *rest):
    n_cast = (len(rest) - 2) // 2
    cast_in, (xo_ref, ho_ref), cast_out = rest[:n_cast], rest[n_cast:n_cast + 2], rest[n_cast + 2:]
    for src, dst in zip(cast_in, cast_out):
        dst[...] = src[...].astype(dst.dtype)
    h = h_ref[...]
    merged = jnp.zeros(x_ref.shape, F32)
    for bi, o_ref in enumerate((o0_ref, o1_ref, o2_ref, o3_ref)):
        gate = _sigmoid(_dot(h, wg_ref[:, bi * D_MODEL:(bi + 1) * D_MODEL]))
        merged = merged + gate * _dot(o_ref[...], wb_ref[bi])
    x_new = x_ref[...] + _dot(merged.astype(BF16), wo_ref[...])
    xo_ref[...] = x_new
    ho_ref[...] = _rms(x_new, g_ref[...]).astype(ho_ref.dtype)


def merge(h, branches, x, w_gate, w_branch, w_out, next_gain, tm=512, cast=()):
    s, d = x.shape
    tm = min(tm, s)
    steps = s // tm
    row = lambda w: pl.BlockSpec((tm, w), lambda i: (i, 0))
    flat = [c.reshape(-1, c.shape[-1]) for c in cast]
    slab = lambda c: pl.BlockSpec((c.shape[0] // steps, c.shape[1]), lambda i: (i, 0))
    outs = pl.pallas_call(
        _merge_kernel,
        grid=(steps,),
        in_specs=[row(d), row(BRANCH_W), row(BRANCH_W), row(BRANCH_W), row(BRANCH_W), row(d),
                  _const_spec((d, N_BRANCH * d)), _const_spec((N_BRANCH, BRANCH_W, d)),
                  _const_spec((d, d)), _const_spec((1, d))] + [slab(c) for c in flat],
        out_specs=[row(d), row(d)] + [slab(c) for c in flat],
        out_shape=[jax.ShapeDtypeStruct((s, d), F32), jax.ShapeDtypeStruct((s, d), BF16)]
        + [jax.ShapeDtypeStruct(c.shape, BF16) for c in flat],
        compiler_params=_params("parallel"),
        name="merge",
    )(h, *branches, x, w_gate, w_branch, w_out, next_gain.reshape(1, d).astype(F32), *flat)
    return outs[0], outs[1], [o.reshape(c.shape) for o, c in zip(outs[2:], cast)]


def _ffn_kernel(h_ref, x_ref, wg_ref, wu_ref, wd_ref, g_ref, xo_ref, ho_ref, acc_ref):
    f = pl.program_id(1)

    @pl.when(f == 0)
    def _():
        acc_ref[...] = jnp.zeros_like(acc_ref)

    h = h_ref[...]
    a = _silu(_dot(h, wg_ref[...].astype(BF16))) * _dot(h, wu_ref[...].astype(BF16))
    acc_ref[...] += _dot(a.astype(BF16), wd_ref[...].astype(BF16))

    @pl.when(f == pl.num_programs(1) - 1)
    def _():
        x_new = x_ref[...] + acc_ref[...]
        xo_ref[...] = x_new
        ho_ref[...] = _rms(x_new, g_ref[...]).astype(ho_ref.dtype)


def ffn_dense(h, x, w_gate, w_up, w_down, next_gain, tm=1024, tf=512):
    s, d = x.shape
    tm = min(tm, s)
    dff = w_gate.shape[1]
    row = lambda: pl.BlockSpec((tm, d), lambda i, f: (i, 0))
    return pl.pallas_call(
        _ffn_kernel,
        grid=(s // tm, dff // tf),
        in_specs=[row(), row(),
                  pl.BlockSpec((d, tf), lambda i, f: (0, f)),
                  pl.BlockSpec((d, tf), lambda i, f: (0, f)),
                  pl.BlockSpec((tf, d), lambda i, f: (f, 0)),
                  _const_spec((1, d))],
        out_specs=[row(), row()],
        out_shape=[jax.ShapeDtypeStruct((s, d), F32), jax.ShapeDtypeStruct((s, d), BF16)],
        scratch_shapes=[pltpu.VMEM((tm, d), F32)],
        compiler_params=_params("parallel", "arbitrary"),
        name="ffn_dense",
    )(h, x, w_gate, w_up, w_down, next_gain.reshape(1, d).astype(F32))


MERGE_CAST_TILE = 256
MOE_TOK_TILE = 256
MOE_ROUTE_TILES = 4
MOE_ROW_BLOCK = 512
MOE_GATHER_ROWS = 128
MOE_GATHER_TILES = 4
MOE_VMEM_LIMIT = 60 * 1024 * 1024
MOE_Y_BLOCK = 64
MOE_Y_COMMON = 4
MOE_Y_FETCH = MOE_TOK_TILE // MOE_Y_BLOCK + 1


def _router_logits(x, w):
    xh, xm, _ = _split3(x)
    wh, wm, _ = _split3(w)
    packed = (wh.astype(F32) + pltpu.roll(wm.astype(F32), N_EXPERTS, axis=1)).astype(BF16)
    s = _dot(xh, packed) + _dot(xm, packed)
    return s + pltpu.roll(s, LANE - N_EXPERTS, axis=1)


def _route_kernel(x_ref, ng_ref, router_ref, ltri_ref, pos_ref, gate_ref, post_ref, before_ref, total_ref,
                  carry_ref):
    tm = x_ref.shape[0]

    @pl.when(pl.program_id(0) == 0)
    def _():
        carry_ref[...] = jnp.zeros_like(carry_ref)

    hn = _rms(x_ref[...], ng_ref[...])
    logits = _router_logits(hn, router_ref[...])
    lane = lax.broadcasted_iota(jnp.int32, (tm, LANE), 1)
    logits = jnp.where(lane < N_EXPERTS, logits, -jnp.inf)
    v1 = jnp.max(logits, axis=1, keepdims=True)
    i1 = jnp.min(jnp.where(logits == v1, lane, LANE), axis=1, keepdims=True)
    rest = jnp.where(lane == i1, -jnp.inf, logits)
    v2 = jnp.max(rest, axis=1, keepdims=True)
    i2 = jnp.min(jnp.where(rest == v2, lane, LANE), axis=1, keepdims=True)
    w1 = 1.0 / (1.0 + jnp.exp(v2 - v1))
    gate_ref[...] = jnp.where(lane == i1, w1, 0.0) + jnp.where(lane == i2, 1.0 - w1, 0.0)

    member = jnp.where((lane == i1) | (lane == i2), 1.0, 0.0)
    carry = carry_ref[...]
    ranks = []
    for blk in range(tm // MOE_TOK_TILE):
        before_ref[blk] = carry
        mb = member[blk * MOE_TOK_TILE:(blk + 1) * MOE_TOK_TILE]
        ranks.append(_dot(ltri_ref[...], mb.astype(BF16)) + carry)
        carry = carry + jnp.sum(mb, axis=0, keepdims=True)
    pos = jnp.where(member > 0.0, jnp.concatenate(ranks, axis=0), -1.0)
    pos_ref[...] = pos
    post_ref[...] = pos.T[:SUBLANE, :]
    carry_ref[...] = carry
    total_ref[...] = carry


def moe_route(x, norm_gain, router):
    s, d = x.shape
    tm = min(MOE_ROUTE_TILES * MOE_TOK_TILE, s)
    nsub = tm // MOE_TOK_TILE
    nt = s // tm
    router_p = jnp.zeros((d, LANE), F32).at[:, :N_EXPERTS].set(router.astype(F32))
    tt = min(MOE_TOK_TILE, s)
    ltri = _const(np.tril(np.ones((tt, tt)), -1), BF16)
    row = pl.BlockSpec((tm, LANE), lambda i: (i, 0))
    return pl.pallas_call(
        _route_kernel,
        grid=(nt,),
        in_specs=[pl.BlockSpec((tm, d), lambda i: (i, 0)), _const_spec((1, d)), _const_spec((d, LANE)),
                  _const_spec((tt, tt))],
        out_specs=[row, row, pl.BlockSpec((SUBLANE, tm), lambda i: (0, i)),
                   pl.BlockSpec((nsub, 1, LANE), lambda i: (i, 0, 0)), pl.BlockSpec((1, LANE), lambda i: (0, 0))],
        out_shape=[jax.ShapeDtypeStruct((s, LANE), F32), jax.ShapeDtypeStruct((s, LANE), F32),
                   jax.ShapeDtypeStruct((SUBLANE, s), F32), jax.ShapeDtypeStruct((nt * nsub, 1, LANE), F32),
                   jax.ShapeDtypeStruct((1, LANE), F32)],
        scratch_shapes=[pltpu.VMEM((1, LANE), F32)],
        compiler_params=_params("arbitrary"),
        name="moe_route",
    )(x, norm_gain.reshape(1, d).astype(F32), router_p, ltri)


def _moe_ffn_kernel(be_ref, r0_ref, tlo_ref, thi_ref, nv_ref, h_ref, post_ref, wg_ref, wu_ref, wd_ref,
                    y_ref, x_sc, acc_sc, *, tt, nsub):
    b = pl.program_id(0)
    f = pl.program_id(1)
    nb = pl.num_programs(0)
    valid = b < nv_ref[0]
    tmb, d = acc_sc.shape
    gr = tmb // nsub
    last_tile = h_ref.shape[0] // tt - 1

    def picked(e, want, t):
        t0 = pl.multiple_of(t * tt, tt)
        p = post_ref[pl.ds(e, 1), pl.ds(t0, tt)]
        sel = jnp.where(p == want, 1.0, 0.0).astype(BF16)
        return _dot(sel, h_ref[pl.ds(t0, tt), :])

    def gather_head(blk, sb):
        e = be_ref[blk]
        lo = tlo_ref[blk * nsub + sb]
        hi = thi_ref[blk * nsub + sb]
        want = (lax.broadcasted_iota(jnp.int32, (gr, 1), 0) + (r0_ref[blk] + sb * gr)).astype(F32)
        rows = picked(e, want, lo)
        for k in range(1, MOE_GATHER_TILES):
            rows = rows + picked(e, jnp.where(lo + k <= hi, want, -2.0), jnp.minimum(lo + k, last_tile))
        return rows.astype(BF16)

    def gather_tail(slot, blk, sb):
        e = be_ref[blk]
        want = (lax.broadcasted_iota(jnp.int32, (gr, 1), 0) + (r0_ref[blk] + sb * gr)).astype(F32)
        rows = pl.ds(pl.multiple_of(sb * gr, gr), gr)

        def more(t, carry):
            x_sc[slot, rows, :] = (x_sc[slot, rows, :].astype(F32) + picked(e, want, t)).astype(BF16)
            return carry

        lax.fori_loop(tlo_ref[blk * nsub + sb] + MOE_GATHER_TILES, thi_ref[blk * nsub + sb] + 1, more, 0)

    @pl.when((b == 0) & (f == 0))
    def _():
        for sb in range(nsub):
            x_sc[0, pl.ds(sb * gr, gr), :] = gather_head(0, sb)
            gather_tail(0, 0, sb)

    nxt = jnp.minimum(b + 1, nb - 1)
    nslot = (b + 1) % 2

    @pl.when(valid)
    def _():
        x_sc[nslot, pl.ds(pl.multiple_of(f * gr, gr), gr), :] = gather_head(nxt, f)
        xb = x_sc[b % 2]
        a = _silu(_dot(xb, wg_ref[0])) * _dot(xb, wu_ref[0])
        acc_sc[...] = jnp.where(f == 0, 0.0, acc_sc[...]) + _dot(a.astype(BF16), wd_ref[0].astype(BF16))

    @pl.when(valid & (thi_ref[nxt * nsub + f] - tlo_ref[nxt * nsub + f] >= MOE_GATHER_TILES))
    def _():
        gather_tail(nslot, nxt, f)

    @pl.when(f == nsub - 1)
    def _():
        y_ref[...] = jnp.where(valid, acc_sc[...], 0.0).astype(y_ref.dtype)


def moe_ffn(h, post, sched, w_gate, w_up, w_down):
    s, d = h.shape
    ne, _, dff = w_gate.shape
    tmb = MOE_ROW_BLOCK
    tt = min(MOE_TOK_TILE, s)
    nb = sched[0].shape[0]
    nf = tmb // MOE_GATHER_ROWS
    tf = dff // nf

    def fidx(b, f, nv):
        return jnp.where(b < nv[0], f, nf - 1)

    grid_spec = pltpu.PrefetchScalarGridSpec(
        num_scalar_prefetch=5,
        grid=(nb, nf),
        in_specs=[pl.BlockSpec((s, d), lambda b, f, *_: (0, 0), pipeline_mode=pl.Buffered(1)),
                  pl.BlockSpec((SUBLANE, s), lambda b, f, *_: (0, 0), pipeline_mode=pl.Buffered(1)),
                  pl.BlockSpec((1, d, tf), lambda b, f, be, r0, tlo, thi, nv: (be[b], 0, fidx(b, f, nv))),
                  pl.BlockSpec((1, d, tf), lambda b, f, be, r0, tlo, thi, nv: (be[b], 0, fidx(b, f, nv))),
                  pl.BlockSpec((1, tf, d), lambda b, f, be, r0, tlo, thi, nv: (be[b], fidx(b, f, nv), 0))],
        out_specs=pl.BlockSpec((tmb, d), lambda b, f, *_: (b, 0)),
        scratch_shapes=[pltpu.VMEM((2, tmb, d), BF16), pltpu.VMEM((tmb, d), F32)],
    )
    return pl.pallas_call(
        functools.partial(_moe_ffn_kernel, tt=tt, nsub=nf),
        grid_spec=grid_spec,
        out_shape=jax.ShapeDtypeStruct((nb * tmb, d), BF16),
        compiler_params=pltpu.CompilerParams(dimension_semantics=("arbitrary", "arbitrary"),
                                             vmem_limit_bytes=MOE_VMEM_LIMIT),
        name="moe_ffn",
    )(*sched, h, post, w_gate, w_up, w_down)


def _moe_combine_kernel(kb_ref, off_ref, lim_ref, x_ref, pos_ref, gate_ref, fg_ref, *rest):
    y_refs, o_ref, acc_sc = rest[:-2], rest[-2], rest[-1]
    t = pl.program_id(0)
    yb = y_refs[0].shape[0]
    pos = pos_ref[...]
    gate = gate_ref[...]
    col = lax.broadcasted_iota(jnp.int32, (1, yb), 1).astype(F32)

    def routed(e):
        pe = pos[:, e:e + 1]
        ge = gate[:, e:e + 1]
        r = jnp.where(pe >= 0.0, pe + off_ref[t * N_EXPERTS + e].astype(F32), -1.0)
        return r, ge

    def picked(e, k, r):
        sel = jnp.where(r == col + float(k * yb), 1.0, 0.0).astype(BF16)
        return _dot(sel, y_refs[MOE_Y_FETCH * e + k][...])

    col2 = lax.broadcasted_iota(jnp.int32, (1, MOE_Y_COMMON * yb), 1).astype(F32)
    acc = x_ref[...]
    for e in range(N_EXPERTS):
        r, ge = routed(e)
        run = jnp.concatenate([y_refs[MOE_Y_FETCH * e + k][...] for k in range(MOE_Y_COMMON)], axis=0)
        acc = acc + ge * _dot(jnp.where(r == col2, 1.0, 0.0).astype(BF16), run)
    acc_sc[...] = acc
    for e in range(N_EXPERTS):
        for k in range(MOE_Y_COMMON, MOE_Y_FETCH):
            @pl.when(lim_ref[t * N_EXPERTS + e] > k * yb)
            def _(k=k, e=e):
                r, ge = routed(e)
                acc_sc[...] += ge * picked(e, k, r)
    o_ref[...] = _rms(acc_sc[...], fg_ref[...]).astype(o_ref.dtype)


def moe_combine(x, pos, gate, y, kb, off, lim, final_gain):
    s, d = x.shape
    tm = min(MOE_TOK_TILE, s)
    yb = MOE_Y_BLOCK
    last = y.shape[0] // yb - 1

    def yspec(e, k):
        def index(t, kb_r, off_r, lim_r):
            blk = jnp.minimum(kb_r[t * N_EXPERTS + e] + k, last)
            return (blk if k < 1 else jnp.where(lim_r[t * N_EXPERTS + e] > k * yb, blk, 0), 0)
        return pl.BlockSpec((yb, d), index)

    grid_spec = pltpu.PrefetchScalarGridSpec(
        num_scalar_prefetch=3,
        grid=(s // tm,),
        in_specs=[pl.BlockSpec((tm, d), lambda t, *_: (t, 0)),
                  pl.BlockSpec((tm, LANE), lambda t, *_: (t, 0)),
                  pl.BlockSpec((tm, LANE), lambda t, *_: (t, 0)),
                  pl.BlockSpec((1, d), lambda t, *_: (0, 0))]
        + [yspec(e, k) for e in range(N_EXPERTS) for k in range(MOE_Y_FETCH)],
        out_specs=pl.BlockSpec((tm, d), lambda t, *_: (t, 0)),
        scratch_shapes=[pltpu.VMEM((tm, d), F32)],
    )
    return pl.pallas_call(
        _moe_combine_kernel,
        grid_spec=grid_spec,
        out_shape=jax.ShapeDtypeStruct((s, d), F32),
        compiler_params=_params("arbitrary"),
        name="moe_combine",
    )(kb, off, lim, x, pos, gate, final_gain.reshape(1, d).astype(F32), *([y] * (MOE_Y_FETCH * N_EXPERTS)))


def _moe_schedule(before, total, s):
    tmb, yb, gr = MOE_ROW_BLOCK, MOE_Y_BLOCK, MOE_GATHER_ROWS
    nb = 2 * s // tmb + N_EXPERTS
    counts = total[0, :N_EXPERTS].astype(jnp.int32)
    nblk = (counts + tmb - 1) // tmb
    end = jnp.cumsum(nblk)
    first = end - nblk
    nvalid = end[-1]
    b = jnp.minimum(jnp.arange(nb, dtype=jnp.int32), nvalid - 1)
    blk_e = jnp.sum(b[:, None] >= end[None, :], axis=1).astype(jnp.int32)
    r0 = (b - first[blk_e]) * tmb
    cb = before[:, 0, :N_EXPERTS].astype(jnp.int32)
    r0s = (r0[:, None] + gr * jnp.arange(tmb // gr, dtype=jnp.int32)[None, :]).reshape(-1)
    cbe = cb[:, jnp.repeat(blk_e, tmb // gr)]
    tlo = (jnp.sum(cbe <= r0s[None, :], axis=0) - 1).astype(jnp.int32)
    thi = (jnp.sum(cbe < (r0s + gr)[None, :], axis=0) - 1).astype(jnp.int32)
    sched = (blk_e, r0.astype(jnp.int32), tlo, thi, nvalid.reshape(1).astype(jnp.int32))
    row_start = first[None, :] * tmb + cb
    kb = row_start // yb
    off = first[None, :] * tmb - kb * yb
    n_te = jnp.concatenate([cb[1:], counts[None, :]], axis=0) - cb
    lim = row_start - kb * yb + n_te
    flat = lambda a: a.reshape(-1).astype(jnp.int32)
    return sched, flat(kb), flat(off), flat(lim)


def moe_sparse(h, x, norm_gain, router, w_gate, w_up, w_down, final_gain):
    s, _ = x.shape
    pos, gate, post, before, total = moe_route(x, norm_gain, router)
    sched, kb, off, lim = _moe_schedule(before, total, s)
    y = moe_ffn(h, post, sched, w_gate, w_up, w_down)
    return moe_combine(x, pos, gate, y, kb, off, lim, final_gain)


FOX_F0 = 7 * BRANCH_W


def _split_w_in_kernel(wt_ref, mix_ref, f_ref, gate_ref):
    rest0 = FOX_F0 + N_HEADS
    gate0 = N_MIX_COLS + N_HEADS
    mix_ref[:, :FOX_F0] = wt_ref[pl.ds(0, FOX_F0), :].T.astype(BF16)
    mix_ref[:, FOX_F0:] = wt_ref[pl.ds(rest0, N_MIX_COLS - FOX_F0), :].T.astype(BF16)
    gate_ref[...] = wt_ref[pl.ds(gate0, N_BRANCH * D_MODEL), :].T.astype(BF16)
    f_rows = jnp.concatenate([wt_ref[pl.ds(FOX_F0, N_HEADS), :], jnp.zeros((LANE - N_HEADS, LANE), F32)], axis=0)
    f_ref[...] = f_rows.T.astype(BF16)


def _mixer_weights(w_in, layer, tr=LANE):
    _, d, cols = w_in.shape
    w_t = jnp.swapaxes(w_in, 1, 2)
    row = lambda w: pl.BlockSpec((tr, w), lambda i: (i, 0))
    return pl.pallas_call(
        _split_w_in_kernel,
        grid=(d // tr,),
        in_specs=[pl.BlockSpec((None, cols, tr), lambda i: (layer, 0, i))],
        out_specs=[row(N_MIX_COLS), row(LANE), row(N_BRANCH * D_MODEL)],
        out_shape=[jax.ShapeDtypeStruct((d, N_MIX_COLS), BF16), jax.ShapeDtypeStruct((d, LANE), BF16),
                   jax.ShapeDtypeStruct((d, N_BRANCH * D_MODEL), BF16)],
        compiler_params=_params("parallel"),
        name="split_w_in",
    )(w_t)


def kernel(x, w_in, w_branch, w_out, norm_mix_g, hgrn_lb_logits, hgrn_norm_g, fox_f_bias, pool_w, pool_scale,
           ret_gn_g, ret_gn_b, norm_ffn_g, ffn_w_gate, ffn_w_up, ffn_w_down, moe_router, moe_w_gate, moe_w_up,
           moe_w_down, final_norm_g):
    b, s, d = x.shape
    assert b == 1 and d == D_MODEL
    depth = w_in.shape[0]
    assert depth == 2, "layer 0 uses the dense FFN, layer 1 the experts and the final norm"
    xs = x.reshape(s, d)
    h = None
    out = None
    for layer in range(depth):
        w_mix, w_f, w_gate = _mixer_weights(w_in, layer)
        if layer == 0:
            proj, h = norm_matmul(xs, norm_mix_g[0], w_mix)
        else:
            proj = matmul(h, w_mix)
        ct, c_rows, fox_stats = fox_gate(h, proj, w_f, fox_f_bias[layer])
        branches = (
            hgrn2(proj, hgrn_lb_logits, hgrn_norm_g[layer], layer),
            fox_attention(proj, ct, c_rows, fox_stats),
            pool_mixer(proj, pool_w[layer], pool_scale[layer]),
            retention(proj, ret_gn_g[layer], ret_gn_b[layer]),
        )
        li = layer // 2
        if layer % 2 == 0:
            xs, h2, (ffn_wg, ffn_wu) = merge(h, branches, xs, w_gate, w_branch[layer].astype(BF16),
                                             w_out[layer].astype(BF16), norm_ffn_g[layer],
                                             cast=(ffn_w_gate[li], ffn_w_up[li]))
            xs, h = ffn_dense(h2, xs, ffn_wg, ffn_wu, ffn_w_down[li], norm_mix_g[layer + 1])
        else:
            xs, h2, (moe_wg, moe_wu) = merge(h, branches, xs, w_gate, w_branch[layer].astype(BF16),
                                             w_out[layer].astype(BF16), norm_ffn_g[layer], tm=MERGE_CAST_TILE,
                                             cast=(moe_w_gate[li], moe_w_up[li]))
            out = moe_sparse(h2, xs, norm_ffn_g[layer], moe_router[li], moe_wg, moe_wu, moe_w_down[li],
                             final_norm_g)
    return out.reshape(b, s, d)
```

```python
import functools
import math

import jax
import jax.numpy as jnp
import numpy as np
from jax import lax
from jax.experimental import pallas as pl
from jax.experimental.pallas import tpu as pltpu

D_MODEL = 1024
N_BRANCH = 4
BRANCH_W = D_MODEL // N_BRANCH
HEAD_DIM = 64
N_HEADS = BRANCH_W // HEAD_DIM
POOL_WINDOWS = (2, 4, 8, 16)
POOL_GROUP = BRANCH_W // len(POOL_WINDOWS)
POOL_HALO = 16
RET_DECAY_BASE = 5.0
ROPE_BASE = 10000.0
D_FF = 7 * D_MODEL // 2
N_EXPERTS = 8
RMS_EPS = 1e-6
LN_EPS = 1e-5
N_MIX_COLS = 12 * BRANCH_W

LANE = 128
SUBLANE = 8
VMEM_LIMIT = 56 * 1024 * 1024

HG_CHUNK = 64
HG_SUB = 16
HG_FAST_MIN_LOGDECAY = -60.0
RET_CHUNK = 256

F32 = jnp.float32
BF16 = jnp.bfloat16
NT_DIMS = (((1,), (1,)), ((), ()))


def _params(*sem):
    return pltpu.CompilerParams(dimension_semantics=sem, vmem_limit_bytes=VMEM_LIMIT)


def _const_spec(shape):
    nd = len(shape)
    return pl.BlockSpec(shape, lambda *_: (0,) * nd, pipeline_mode=pl.Buffered(1))


def _split3(x):
    hi = x.astype(BF16)
    r1 = x - hi.astype(F32)
    mid = r1.astype(BF16)
    lo = (r1 - mid.astype(F32)).astype(BF16)
    return hi, mid, lo


def _dot(a, b):
    return jnp.dot(a, b, preferred_element_type=F32)


def _dot_nt(a, b):
    return lax.dot_general(a, b, NT_DIMS, preferred_element_type=F32)


def _dot_exact_rhs(x, m_bf16, terms=3):
    return sum(_dot(part, m_bf16) for part in _split3(x)[:terms])


def _dot_exact_lhs(m_bf16, x, terms=3):
    return sum(_dot(m_bf16, part) for part in _split3(x)[:terms])


def _sigmoid(x):
    return 1.0 / (1.0 + jnp.exp(-x))


def _silu(x):
    return x * _sigmoid(x)


def _rms(x, gain):
    return x * lax.rsqrt(jnp.mean(x * x, axis=-1, keepdims=True) + RMS_EPS) * gain


def _const(a, dtype=F32):
    return jnp.asarray(np.asarray(a, np.float32), dtype)


def _head_of(n):
    return np.arange(n) // HEAD_DIM


def _head_ones():
    h = _head_of(BRANCH_W)
    return _const(h[:, None] == h[None, :], BF16)


def _head_masks():
    return _const(_head_of(BRANCH_W)[None, :] == np.arange(N_HEADS)[:, None])


def _norm_matmul_kernel(x_ref, g_ref, b_ref, o_ref, h_ref):
    @pl.when(pl.program_id(1) == 0)
    def _():
        h_ref[...] = _rms(x_ref[...], g_ref[...]).astype(h_ref.dtype)

    o_ref[...] = _dot(h_ref[...], b_ref[...]).astype(o_ref.dtype)


def norm_matmul(x, gain, b, tm=2048, tn=1024):
    m, k = x.shape
    _, n = b.shape
    tm = min(tm, m)
    return pl.pallas_call(
        _norm_matmul_kernel,
        grid=(m // tm, n // tn),
        in_specs=[pl.BlockSpec((tm, k), lambda i, j: (i, 0)), _const_spec((1, k)),
                  pl.BlockSpec((k, tn), lambda i, j: (0, j))],
        out_specs=[pl.BlockSpec((tm, tn), lambda i, j: (i, j)), pl.BlockSpec((tm, k), lambda i, j: (i, 0))],
        out_shape=[jax.ShapeDtypeStruct((m, n), BF16), jax.ShapeDtypeStruct((m, k), BF16)],
        compiler_params=_params("arbitrary", "arbitrary"),
        name="in_proj_norm",
    )(x, gain.reshape(1, k).astype(F32), b)


def _matmul_kernel(a_ref, b_ref, o_ref):
    o_ref[...] = _dot(a_ref[...], b_ref[...]).astype(o_ref.dtype)


def matmul(a, b, out_dtype=BF16, tm=2048, tn=1024):
    m, k = a.shape
    _, n = b.shape
    tm = min(tm, m)
    return pl.pallas_call(
        _matmul_kernel,
        grid=(n // tn, m // tm),
        in_specs=[pl.BlockSpec((tm, k), lambda j, i: (i, 0)),
                  pl.BlockSpec((k, tn), lambda j, i: (0, j))],
        out_specs=pl.BlockSpec((tm, tn), lambda j, i: (i, j)),
        out_shape=jax.ShapeDtypeStruct((m, n), out_dtype),
        compiler_params=_params("parallel", "parallel"),
        name="in_proj",
    )(a, b)


def _hgrn_kernel(q_ref, f_ref, i_ref, g_ref, lbl_ref, ng_ref, ones_ref, hm_ref, tril_ref, halfsum_ref, fmask_ref,
                 o_ref, st_ref, bpad, kpad, vpad, astack, lf_sc, kk_sc, o_sc, *, layer, tile):
    c, sub = HG_CHUNK, HG_SUB
    nsub = c // sub
    half = c // 2

    @pl.when(pl.program_id(0) == 0)
    def _():
        st_ref[...] = jnp.zeros_like(st_ref)
        bpad[...] = jnp.zeros_like(bpad)
        kpad[...] = jnp.zeros_like(kpad)
        vpad[...] = jnp.zeros_like(vpad)

    lbl = lbl_ref[...]
    e = jnp.exp(lbl - jnp.max(lbl, axis=0, keepdims=True))
    p = e / jnp.sum(e, axis=0, keepdims=True)
    lb = jnp.zeros((1, BRANCH_W), F32)
    for l in range(1, layer + 1):
        lb = lb + p[l:l + 1, :]

    ones_bd = ones_ref[...]
    hm = hm_ref[...]
    tril = tril_ref[...]
    row = lax.broadcasted_iota(jnp.int32, (c, 1), 0)
    row_in_sub = row % sub
    bd_mask = ones_bd.astype(F32)

    sig = _sigmoid(f_ref[...].astype(F32))
    logf_all = jnp.log(lb + (1.0 - lb) * sig)
    lf_sc[...] = _dot_exact_lhs(tril, logf_all, terms=2)
    kk_sc[...] = (1.0 - lb) * (1.0 - sig)
    min_decay = jnp.min(_dot(halfsum_ref[...], logf_all.astype(BF16)))

    def load(ci):
        r0 = pl.multiple_of(ci * c, c)
        q = q_ref[pl.ds(r0, c), :].astype(F32)
        v = i_ref[pl.ds(r0, c), :].astype(F32)
        kk = kk_sc[pl.ds(r0, c), :]
        b = lf_sc[pl.ds(r0, c), :]
        return r0, q, v, kk, b

    def finish(r0, q_decayed, v, kk, b, intra):
        st = st_ref[...]
        inter = _dot_nt(q_decayed.astype(BF16), st.astype(BF16))
        b_last = b[c - 1:c, :]
        ks_end = (kk * jnp.exp(b_last - b)).astype(BF16)
        upd = _dot(v.T.astype(BF16), ks_end)
        st_ref[...] = st * jnp.exp(b_last) + upd * bd_mask
        o_sc[pl.ds(r0, c), :] = intra + inter

    def fast_chunk(ci, carry):
        r0, q, v, kk, b = load(ci)
        second = row >= half
        m_row = b[half - 1:half, :]
        mref = jnp.where(second, m_row, 0.0)
        qp = q * jnp.exp(b - mref)
        kp = kk * jnp.exp(mref - b)
        e_m = jnp.exp(m_row)
        kaug = jnp.concatenate([kp, kp[:half, :] * e_m], axis=0)
        vaug = jnp.concatenate([v, v[:half, :]], axis=0)
        k_heads = jnp.concatenate([kaug * hm[h:h + 1, :] for h in range(N_HEADS)], axis=0).astype(BF16)
        v_heads = jnp.concatenate([vaug * hm[h:h + 1, :] for h in range(N_HEADS)], axis=0).astype(BF16)
        sc = jnp.where(fmask_ref[...] > 0.0, _dot_nt(qp.astype(BF16), k_heads), 0.0)
        intra = _dot(sc.astype(BF16), v_heads)
        finish(r0, jnp.where(second, qp * e_m, qp), v, kk, b, intra)
        return carry

    def exact_chunk(ci, carry):
        r0, q, v, kk, b = load(ci)

        bpad[pl.ds(sub, c), :] = b
        kpad[pl.ds(sub, c), :] = kk
        vpad[pl.ds(sub, c), :] = v

        for d in range(sub):
            b_d = bpad[pl.ds(sub - d, c), :]
            k_d = kpad[pl.ds(sub - d, c), :]
            a = jnp.where(row_in_sub >= d, q * k_d * jnp.exp(b - b_d), 0.0)
            astack[pl.ds(d * c, c), :] = a.astype(BF16)
        pall = _dot(astack[...], ones_bd)
        intra = jnp.zeros((c, BRANCH_W), F32)
        for d in range(sub):
            intra = intra + pall[d * c:(d + 1) * c, :] * vpad[pl.ds(sub - d, c), :]

        pieces = [jnp.zeros((sub, BRANCH_W), F32)]
        for si in range(1, nsub):
            lo = si * sub
            m_i = b[lo - 1:lo, :]
            qs = q[lo:lo + sub, :] * jnp.exp(b[lo:lo + sub, :] - m_i)
            ks = (kk[:lo, :] * jnp.exp(m_i - b[:lo, :])).astype(BF16)
            qx = jnp.concatenate([qs * hm[h:h + 1, :] for h in range(N_HEADS)], axis=0).astype(BF16)
            sc = _dot_nt(qx, ks)
            r = _dot(sc.astype(BF16), v[:lo, :].astype(BF16))
            acc = jnp.zeros((sub, BRANCH_W), F32)
            for h in range(N_HEADS):
                acc = acc + r[h * sub:(h + 1) * sub, :] * hm[h:h + 1, :]
            pieces.append(acc)
        intra = intra + jnp.concatenate(pieces, axis=0)
        finish(r0, q * jnp.exp(b), v, kk, b, intra)
        return carry

    lax.cond(min_decay >= HG_FAST_MIN_LOGDECAY,
             lambda: lax.fori_loop(0, tile // c, fast_chunk, 0, unroll=2),
             lambda: lax.fori_loop(0, tile // c, exact_chunk, 0))

    o = o_sc[...]
    ms = _dot_exact_rhs(o * o, ones_bd, terms=2) * (1.0 / HEAD_DIM)
    y = o * lax.rsqrt(ms + RMS_EPS) * ng_ref[...] * _silu(g_ref[...].astype(F32))
    o_ref[...] = y.astype(o_ref.dtype)


def hgrn2(proj, lb_logits, norm_g, layer, tile=512):
    s = proj.shape[0]
    depth = lb_logits.shape[0]
    c, sub = HG_CHUNK, HG_SUB
    half = c // 2
    tile = min(tile, s)
    col = lambda j: pl.BlockSpec((tile, BRANCH_W), lambda i, j=j: (i, j))
    pos = np.arange(tile)
    tril = _const((pos[:, None] // c == pos[None, :] // c) & (pos[None, :] <= pos[:, None]), BF16)
    nhalf = tile // half
    halfsum = _const(np.arange(tile)[None, :] // half == np.arange(nhalf)[:, None], BF16)
    t = np.arange(c)[:, None]
    col_s = np.arange(c + half)[None, :]
    same_half = (col_s < c) & (col_s // half == t // half) & (col_s <= t)
    cross = (col_s >= c) & (t >= half)
    fmask = _const(np.tile(same_half | cross, (1, N_HEADS)))
    return pl.pallas_call(
        functools.partial(_hgrn_kernel, layer=layer, tile=tile),
        grid=(s // tile,),
        in_specs=[col(0), col(1), col(2), col(3),
                  _const_spec((depth, BRANCH_W)), _const_spec((1, BRANCH_W)),
                  _const_spec((BRANCH_W, BRANCH_W)), _const_spec((N_HEADS, BRANCH_W)),
                  _const_spec((tile, tile)), _const_spec((nhalf, tile)),
                  _const_spec((c, N_HEADS * (c + half)))],
        out_specs=pl.BlockSpec((tile, BRANCH_W), lambda i: (i, 0)),
        out_shape=jax.ShapeDtypeStruct((s, BRANCH_W), BF16),
        scratch_shapes=[pltpu.VMEM((BRANCH_W, BRANCH_W), F32),
                        pltpu.VMEM((c + sub, BRANCH_W), F32),
                        pltpu.VMEM((c + sub, BRANCH_W), F32),
                        pltpu.VMEM((c + sub, BRANCH_W), F32),
                        pltpu.VMEM((sub * c, BRANCH_W), BF16),
                        pltpu.VMEM((tile, BRANCH_W), F32),
                        pltpu.VMEM((tile, BRANCH_W), F32),
                        pltpu.VMEM((tile, BRANCH_W), F32)],
        compiler_params=_params("arbitrary"),
        name="hgrn2",
    )(proj, proj, proj, proj, lb_logits.astype(F32), norm_g.reshape(1, BRANCH_W).astype(F32),
      _head_ones(), _head_masks(), tril, halfsum, fmask)


FOX_TILE = 256
FOX_GATE_BLOCKS = 2
FOX_NSTAT = 16
FOX_FIXED_MAX = 30.0
FOX_SKIP_LOG = 30.0


def _fox_gate_kernel(h_ref, q_ref, k_ref, wf_ref, bias_ref, tril_ref, ones_ref,
                     ct_ref, c_ref, stat_ref, carry_ref, kmax_ref):
    @pl.when(pl.program_id(0) == 0)
    def _():
        carry_ref[...] = jnp.zeros_like(carry_ref)
        kmax_ref[...] = jnp.zeros_like(kmax_ref)

    logit = _dot(h_ref[...], wf_ref[...]) + bias_ref[...]
    logf = jnp.minimum(logit, 0.0) - jnp.log(1.0 + jnp.exp(-jnp.abs(logit)))
    carry = carry_ref[...]
    parts = []
    for blk in range(logf.shape[0] // FOX_TILE):
        part = _dot_exact_lhs(tril_ref[...], logf[blk * FOX_TILE:(blk + 1) * FOX_TILE]) + carry
        carry = part[-1:, :]
        parts.append(part)
    cum = jnp.concatenate(parts, axis=0)
    carry_ref[...] = carry
    ct_ref[...] = cum.T[:SUBLANE, :]
    c_ref[...] = cum

    ones_bd = ones_ref[...]
    q = q_ref[...].astype(F32)
    k = k_ref[...].astype(F32)
    scale = HEAD_DIM ** -0.5
    head_lane = lax.broadcasted_iota(jnp.int32, (1, BRANCH_W), 1) // HEAD_DIM
    c_heads = jnp.zeros(q.shape, F32)
    for h in range(N_HEADS):
        c_heads = jnp.where(head_lane == h, cum[:, h:h + 1], c_heads)
    slack = 1.0 + 2.0 ** -6
    qn = jnp.sqrt(_dot((q * q).astype(BF16), ones_bd)) * (scale * slack)
    kn = jnp.sqrt(_dot((k * k).astype(BF16), ones_bd)) * slack
    diag = _dot((q * k).astype(BF16), ones_bd) * scale - (2.0 ** -6) * qn * kn
    e_row = c_heads - diag
    kmax = kmax_ref[...]
    for blk in range(q.shape[0] // FOX_TILE):
        rs = slice(blk * FOX_TILE, (blk + 1) * FOX_TILE)
        kmax = jnp.maximum(kmax, jnp.max(kn[rs], axis=0, keepdims=True))
        rows = [jnp.max(qn[rs], axis=0, keepdims=True),
                jnp.max(e_row[rs], axis=0, keepdims=True),
                kmax,
                c_heads[(blk + 1) * FOX_TILE - 1:(blk + 1) * FOX_TILE, :]]
        stat_ref[blk] = jnp.concatenate(rows + [jnp.zeros((SUBLANE - len(rows), BRANCH_W), F32)], axis=0)
    kmax_ref[...] = kmax


def fox_gate(h, proj, w_f, f_bias):
    s, d = h.shape
    tile = min(FOX_GATE_BLOCKS * FOX_TILE, s)
    nblk = tile // FOX_TILE
    bias = jnp.zeros((1, LANE), F32).at[0, :N_HEADS].set(f_bias.astype(F32))
    tril = _const(np.tril(np.ones((FOX_TILE, FOX_TILE))), BF16)
    ct, c_rows, stats = pl.pallas_call(
        _fox_gate_kernel,
        grid=(s // tile,),
        in_specs=[pl.BlockSpec((tile, d), lambda i: (i, 0)),
                  pl.BlockSpec((tile, BRANCH_W), lambda i: (i, 4)),
                  pl.BlockSpec((tile, BRANCH_W), lambda i: (i, 5)),
                  _const_spec((d, LANE)), _const_spec((1, LANE)), _const_spec((FOX_TILE, FOX_TILE)),
                  _const_spec((BRANCH_W, BRANCH_W))],
        out_specs=[pl.BlockSpec((SUBLANE, tile), lambda i: (0, i)),
                   pl.BlockSpec((tile, LANE), lambda i: (i, 0)),
                   pl.BlockSpec((nblk, SUBLANE, BRANCH_W), lambda i: (i, 0, 0))],
        out_shape=[jax.ShapeDtypeStruct((SUBLANE, s), F32),
                   jax.ShapeDtypeStruct((s, LANE), F32),
                   jax.ShapeDtypeStruct((s // FOX_TILE, SUBLANE, BRANCH_W), F32)],
        scratch_shapes=[pltpu.VMEM((1, LANE), F32), pltpu.VMEM((1, BRANCH_W), F32)],
        compiler_params=_params("arbitrary"),
        name="fox_gate",
    )(h, proj, proj, w_f, bias, tril, _head_ones())
    return ct, c_rows, stats[:, :4, ::HEAD_DIM].reshape(-1)


def _fox_kernel(stat_ref, q_ref, k_ref, v_ref, ct_ref, c_ref, hm_ref, o_ref, m_sc, l_sc, acc_sc, *, tq):
    i = pl.program_id(0)
    q0 = pl.multiple_of(i * tq, tq)
    hm = hm_ref[...]
    q = q_ref[...].astype(F32) * (HEAD_DIM ** -0.5)
    qh = [(q * hm[h:h + 1, :]).astype(BF16) for h in range(N_HEADS)]
    c_q0 = ct_ref[:, pl.ds(q0, tq)][:, 0:1]

    first = i
    for h in range(N_HEADS):
        qmax = stat_ref[i * FOX_NSTAT + h]
        emax = stat_ref[i * FOX_NSTAT + N_HEADS + h]

        def needed(j, h=h, qmax=qmax, emax=emax):
            jc = jnp.maximum(j, 0)
            bound = (qmax * stat_ref[jc * FOX_NSTAT + 2 * N_HEADS + h] + emax
                     - stat_ref[jc * FOX_NSTAT + 3 * N_HEADS + h])
            return (j >= 0) & (bound >= -FOX_SKIP_LOG)

        last_dropped = lax.while_loop(needed, lambda j: j - 1, i - 1)
        first = jnp.minimum(first, last_dropped + 1)

    l_sc[...] = jnp.zeros_like(l_sc)
    acc_sc[...] = jnp.zeros_like(acc_sc)

    def causal(sc):
        r = lax.broadcasted_iota(jnp.int32, (tq, tq), 0)
        cidx = lax.broadcasted_iota(jnp.int32, (tq, tq), 1)
        return jnp.where(cidx <= r, sc, -jnp.inf)

    def online_block(s0, diagonal):
        kb = k_ref[pl.ds(s0, tq), :]
        vb = v_ref[pl.ds(s0, tq), :]
        bias = c_q0 - ct_ref[:, pl.ds(s0, tq)]
        for h in range(N_HEADS):
            sc = _dot_nt(qh[h], kb) + bias[h:h + 1, :]
            if diagonal:
                sc = causal(sc)
            m_prev = m_sc[h]
            m_new = jnp.maximum(m_prev, jnp.max(sc, axis=1, keepdims=True))
            alpha = jnp.exp(m_prev - m_new)
            p = jnp.exp(sc - jnp.tile(m_new, (1, tq // LANE)))
            l_sc[h] = alpha * l_sc[h] + jnp.sum(p, axis=1, keepdims=True)
            acc_sc[h] = acc_sc[h] * jnp.tile(alpha, (1, BRANCH_W // LANE)) + _dot(p.astype(BF16), vb)
            m_sc[h] = m_new

    tops = [stat_ref[i * FOX_NSTAT + h] * stat_ref[i * FOX_NSTAT + 2 * N_HEADS + h] for h in range(N_HEADS)]
    c_tile = c_ref[...]
    shift = [c_tile[:, h:h + 1] - c_q0[h:h + 1, :] - tops[h] for h in range(N_HEADS)]

    def fixed_block(s0, diagonal):
        kb = k_ref[pl.ds(s0, tq), :]
        vb = v_ref[pl.ds(s0, tq), :]
        bias = c_q0 - ct_ref[:, pl.ds(s0, tq)]
        for h in range(N_HEADS):
            sc = _dot_nt(qh[h], kb) + bias[h:h + 1, :] + shift[h]
            if diagonal:
                sc = causal(sc)
            p = jnp.exp(sc)
            l_sc[h] += p[:, :LANE] + p[:, LANE:]
            acc_sc[h] += _dot(p.astype(BF16), vb)

    def run(block, row_sum):
        lax.fori_loop(first, i, lambda j, carry: (block(pl.multiple_of(j * tq, tq), False), carry)[1], 0)
        block(q0, True)
        out = jnp.zeros((tq, BRANCH_W), F32)
        for h in range(N_HEADS):
            out = out + acc_sc[h] * hm[h:h + 1, :] / row_sum(l_sc[h])
        o_ref[...] = out.astype(o_ref.dtype)

    def run_online():
        m_sc[...] = jnp.full_like(m_sc, -jnp.inf)
        run(online_block, lambda l: jnp.tile(l, (1, BRANCH_W // LANE)))

    def run_fixed():
        run(fixed_block, lambda l: jnp.sum(l, axis=1, keepdims=True))

    lax.cond(functools.reduce(jnp.maximum, tops) <= FOX_FIXED_MAX, run_fixed, run_online)


def fox_attention(proj, ct, c_rows, stats):
    s = proj.shape[0]
    tq = min(FOX_TILE, s)
    full = lambda j: pl.BlockSpec((s, BRANCH_W), lambda i, j=j: (0, j), pipeline_mode=pl.Buffered(1))
    return pl.pallas_call(
        functools.partial(_fox_kernel, tq=tq),
        grid=(s // tq,),
        in_specs=[pl.BlockSpec(memory_space=pltpu.SMEM),
                  pl.BlockSpec((tq, BRANCH_W), lambda i: (i, 4)), full(5), full(6),
                  _const_spec((SUBLANE, s)), pl.BlockSpec((tq, LANE), lambda i: (i, 0)),
                  _const_spec((N_HEADS, BRANCH_W))],
        out_specs=pl.BlockSpec((tq, BRANCH_W), lambda i: (i, 0)),
        out_shape=jax.ShapeDtypeStruct((s, BRANCH_W), BF16),
        scratch_shapes=[pltpu.VMEM((N_HEADS, tq, LANE), F32),
                        pltpu.VMEM((N_HEADS, tq, LANE), F32),
                        pltpu.VMEM((N_HEADS, tq, BRANCH_W), F32)],
        compiler_params=_params("parallel"),
        name="fox_attention",
    )(stats, proj, proj, proj, ct, c_rows, _head_masks())


def _pool_kernel(u_ref, w_ref, scale_ref, o_ref, ext, *, tile):
    i = pl.program_id(0)

    @pl.when(i == 0)
    def _():
        ext[pl.ds(0, POOL_HALO), :] = jnp.zeros((POOL_HALO, BRANCH_W), F32)

    u = u_ref[...].astype(F32)
    ext[pl.ds(POOL_HALO, tile), :] = u
    pos = (i * tile + lax.broadcasted_iota(jnp.int32, (tile, 1), 0) + 1).astype(F32)
    halves = []
    for half in range(BRANCH_W // LANE):
        lanes = pl.ds(half * LANE, LANE)
        w_small, w_big = POOL_WINDOWS[2 * half], POOL_WINDOWS[2 * half + 1]
        run = u[:, half * LANE:(half + 1) * LANE]
        sums = {}
        for j in range(1, w_big):
            if j == w_small:
                sums[w_small] = run
            run = run + ext[pl.ds(POOL_HALO - j, tile), lanes]
        sums[w_big] = run
        lane = lax.broadcasted_iota(jnp.int32, (1, LANE), 1)
        small = lane < POOL_GROUP
        total = jnp.where(small, sums[w_small], sums[w_big])
        count = jnp.where(small, jnp.minimum(pos, float(w_small)), jnp.minimum(pos, float(w_big)))
        halves.append(total / count)
    mean = jnp.concatenate(halves, axis=1)
    d = (mean - u).astype(BF16)
    y = _dot(d, w_ref[...]) * scale_ref[...]
    o_ref[...] = y.astype(o_ref.dtype)
    ext[pl.ds(0, POOL_HALO), :] = u[tile - POOL_HALO:, :]


def pool_mixer(proj, w_pool, scale, tile=2048):
    s = proj.shape[0]
    tile = min(tile, s)
    ng = len(POOL_WINDOWS)
    w_bd = jnp.zeros((BRANCH_W, BRANCH_W), F32)
    for gi in range(ng):
        lo = gi * POOL_GROUP
        w_bd = w_bd.at[lo:lo + POOL_GROUP, lo:lo + POOL_GROUP].set(w_pool[gi].astype(F32))
    return pl.pallas_call(
        functools.partial(_pool_kernel, tile=tile),
        grid=(s // tile,),
        in_specs=[pl.BlockSpec((tile, BRANCH_W), lambda i: (i, 7)),
                  _const_spec((BRANCH_W, BRANCH_W)), _const_spec((1, BRANCH_W))],
        out_specs=pl.BlockSpec((tile, BRANCH_W), lambda i: (i, 0)),
        out_shape=jax.ShapeDtypeStruct((s, BRANCH_W), BF16),
        scratch_shapes=[pltpu.VMEM((tile + POOL_HALO, BRANCH_W), F32)],
        compiler_params=_params("arbitrary"),
        name="pool_mixer",
    )(proj, w_bd.astype(BF16), scale.reshape(1, BRANCH_W).astype(F32))


def _ret_kernel(q_ref, k_ref, v_ref, g_ref, cos_ref, sin_ref, perm_ref, ones_ref, hm_ref,
                dstack_ref, xi_ref, zeta_ref, gc_ref, gng_ref, gnb_ref, o_ref, st_ref, *, tile):
    c = RET_CHUNK

    @pl.when(pl.program_id(0) == 0)
    def _():
        st_ref[...] = jnp.zeros_like(st_ref)

    perm = perm_ref[...]
    ones_bd = ones_ref[...]
    bd_mask = ones_bd.astype(F32)
    hm = hm_ref[...]

    cos = cos_ref[...]
    sin = sin_ref[...]
    q_all = q_ref[...]
    k_all = k_ref[...]
    qr_all = q_all.astype(F32) * cos + _dot(q_all, perm) * sin
    kr_all = (k_all.astype(F32) * cos + _dot(k_all, perm) * sin) * (HEAD_DIM ** -0.5)

    outs = []
    for ci in range(tile // c):
        r0 = ci * c
        qr = qr_all[r0:r0 + c, :]
        kr = kr_all[r0:r0 + c, :]
        v = v_ref[pl.ds(r0, c), :]

        qx = jnp.concatenate([qr * hm[h:h + 1, :] for h in range(N_HEADS)], axis=0).astype(BF16)
        sc = _dot_nt(qx, kr.astype(BF16)) * dstack_ref[...]
        r = _dot(sc.astype(BF16), v)
        intra = jnp.zeros((c, BRANCH_W), F32)
        for h in range(N_HEADS):
            intra = intra + r[h * c:(h + 1) * c, :] * hm[h:h + 1, :]

        st = st_ref[...]
        inter = _dot_nt((qr * xi_ref[...]).astype(BF16), st.astype(BF16))
        upd = _dot(v.astype(F32).T.astype(BF16), (kr * zeta_ref[...]).astype(BF16))
        st_ref[...] = st * gc_ref[...] + upd * bd_mask
        outs.append(intra + inter)

    o = jnp.concatenate(outs, axis=0)
    mu = _dot_exact_rhs(o, ones_bd, terms=2) * (1.0 / HEAD_DIM)
    cen = o - mu
    var = _dot_exact_rhs(cen * cen, ones_bd, terms=2) * (1.0 / HEAD_DIM)
    y = cen * lax.rsqrt(var + LN_EPS) * gng_ref[...] + gnb_ref[...]
    o_ref[...] = (y * _silu(g_ref[...].astype(F32))).astype(o_ref.dtype)


def _rope_tables(s):
    half = HEAD_DIM // 2
    pos = np.arange(s, dtype=np.float64)
    inv_freq = ROPE_BASE ** (-np.arange(half, dtype=np.float64) / half)
    lane = np.arange(BRANCH_W)
    ang = pos[:, None] * inv_freq[lane % half][None, :]
    sign = np.where(lane % HEAD_DIM < half, -1.0, 1.0)
    return _const(np.cos(ang)), _const(np.sin(ang) * sign[None, :])


def _ret_constants():
    c = RET_CHUNK
    half = HEAD_DIM // 2
    lane = np.arange(BRANCH_W)
    partner = np.where(lane % HEAD_DIM < half, lane + half, lane - half)
    perm = lane[:, None] == partner[None, :]
    log_gamma = np.log1p(-np.exp2(-RET_DECAY_BASE - np.arange(N_HEADS, dtype=np.float64)))
    ci = np.arange(c, dtype=np.float64)
    diff = ci[:, None] - ci[None, :]
    intra = np.where(diff >= 0, np.exp(diff * log_gamma[:, None, None]), 0.0)
    dstack = intra.reshape(N_HEADS * c, c)
    lg_lane = np.repeat(log_gamma, HEAD_DIM)[None, :]
    xi = np.exp((ci[:, None] + 1.0) * lg_lane)
    zeta = np.exp((c - 1.0 - ci[:, None]) * lg_lane)
    gc = np.exp(c * lg_lane)
    return _const(perm, BF16), _const(dstack), _const(xi), _const(zeta), _const(gc)


def retention(proj, gn_g, gn_b, tile=1024):
    s = proj.shape[0]
    c = RET_CHUNK
    tile = min(tile, s)
    cos_t, sin_t = _rope_tables(s)
    perm, dstack, xi, zeta, gc = _ret_constants()
    col = lambda j: pl.BlockSpec((tile, BRANCH_W), lambda i, j=j: (i, j))
    row = pl.BlockSpec((tile, BRANCH_W), lambda i: (i, 0))
    return pl.pallas_call(
        functools.partial(_ret_kernel, tile=tile),
        grid=(s // tile,),
        in_specs=[col(8), col(9), col(10), col(11), row, row,
                  _const_spec((BRANCH_W, BRANCH_W)), _const_spec((BRANCH_W, BRANCH_W)),
                  _const_spec((N_HEADS, BRANCH_W)), _const_spec((N_HEADS * c, c)),
                  _const_spec((c, BRANCH_W)), _const_spec((c, BRANCH_W)), _const_spec((1, BRANCH_W)),
                  _const_spec((1, BRANCH_W)), _const_spec((1, BRANCH_W))],
        out_specs=row,
        out_shape=jax.ShapeDtypeStruct((s, BRANCH_W), BF16),
        scratch_shapes=[pltpu.VMEM((BRANCH_W, BRANCH_W), F32)],
        compiler_params=_params("arbitrary"),
        name="retention",
    )(proj, proj, proj, proj, cos_t, sin_t, perm, _head_ones(), _head_masks(), dstack, xi, zeta, gc,
      gn_g.reshape(1, BRANCH_W).astype(F32), gn_b.reshape(1, BRANCH_W).astype(F32))


def _merge_kernel(h_ref, o0_ref, o1_ref, o2_ref, o3_ref, x_ref, wg_ref, wb_ref, wo_ref, g_ref, *rest):
    n_cast = (len(rest) - 2) // 2
    cast_in, (xo_ref, ho_ref), cast_out = rest[:n_cast], rest[n_cast:n_cast + 2], rest[n_cast + 2:]
    for src, dst in zip(cast_in, cast_out):
        dst[...] = src[...].astype(dst.dtype)
    h = h_ref[...]
    merged = jnp.zeros(x_ref.shape, F32)
    for bi, o_ref in enumerate((o0_ref, o1_ref, o2_ref, o3_ref)):
        gate = _sigmoid(_dot(h, wg_ref[:, bi * D_MODEL:(bi + 1) * D_MODEL]))
        merged = merged + gate * _dot(o_ref[...], wb_ref[bi])
    x_new = x_ref[...] + _dot(merged.astype(BF16), wo_ref[...])
    xo_ref[...] = x_new
    ho_ref[...] = _rms(x_new, g_ref[...]).astype(ho_ref.dtype)


def merge(h, branches, x, w_gate, w_branch, w_out, next_gain, tm=512, cast=()):
    s, d = x.shape
    tm = min(tm, s)
    steps = s // tm
    row = lambda w: pl.BlockSpec((tm, w), lambda i: (i, 0))
    flat = [c.reshape(-1, c.shape[-1]) for c in cast]
    slab = lambda c: pl.BlockSpec((c.shape[0] // steps, c.shape[1]), lambda i: (i, 0))
    outs = pl.pallas_call(
        _merge_kernel,
        grid=(steps,),
        in_specs=[row(d), row(BRANCH_W), row(BRANCH_W), row(BRANCH_W), row(BRANCH_W), row(d),
                  _const_spec((d, N_BRANCH * d)), _const_spec((N_BRANCH, BRANCH_W, d)),
                  _const_spec((d, d)), _const_spec((1, d))] + [slab(c) for c in flat],
        out_specs=[row(d), row(d)] + [slab(c) for c in flat],
        out_shape=[jax.ShapeDtypeStruct((s, d), F32), jax.ShapeDtypeStruct((s, d), BF16)]
        + [jax.ShapeDtypeStruct(c.shape, BF16) for c in flat],
        compiler_params=_params("parallel"),
        name="merge",
    )(h, *branches, x, w_gate, w_branch, w_out, next_gain.reshape(1, d).astype(F32), *flat)
    return outs[0], outs[1], [o.reshape(c.shape) for o, c in zip(outs[2:], cast)]


def _ffn_kernel(h_ref, x_ref, wg_ref, wu_ref, wd_ref, g_ref, xo_ref, ho_ref, acc_ref):
    f = pl.program_id(1)

    @pl.when(f == 0)
    def _():
        acc_ref[...] = jnp.zeros_like(acc_ref)

    h = h_ref[...]
    a = _silu(_dot(h, wg_ref[...].astype(BF16))) * _dot(h, wu_ref[...].astype(BF16))
    acc_ref[...] += _dot(a.astype(BF16), wd_ref[...].astype(BF16))

    @pl.when(f == pl.num_programs(1) - 1)
    def _():
        x_new = x_ref[...] + acc_ref[...]
        xo_ref[...] = x_new
        ho_ref[...] = _rms(x_new, g_ref[...]).astype(ho_ref.dtype)


def ffn_dense(h, x, w_gate, w_up, w_down, next_gain, tm=1024, tf=512):
    s, d = x.shape
    tm = min(tm, s)
    dff = w_gate.shape[1]
    row = lambda: pl.BlockSpec((tm, d), lambda i, f: (i, 0))
    return pl.pallas_call(
        _ffn_kernel,
        grid=(s // tm, dff // tf),
        in_specs=[row(), row(),
                  pl.BlockSpec((d, tf), lambda i, f: (0, f)),
                  pl.BlockSpec((d, tf), lambda i, f: (0, f)),
                  pl.BlockSpec((tf, d), lambda i, f: (f, 0)),
                  _const_spec((1, d))],
        out_specs=[row(), row()],
        out_shape=[jax.ShapeDtypeStruct((s, d), F32), jax.ShapeDtypeStruct((s, d), BF16)],
        scratch_shapes=[pltpu.VMEM((tm, d), F32)],
        compiler_params=_params("parallel", "arbitrary"),
        name="ffn_dense",
    )(h, x, w_gate, w_up, w_down, next_gain.reshape(1, d).astype(F32))


MERGE_CAST_TILE = 256
MOE_TOK_TILE = 256
MOE_ROUTE_TILES = 4
MOE_ROW_BLOCK = 512
MOE_GATHER_ROWS = 128
MOE_GATHER_TILES = 4
MOE_VMEM_LIMIT = 60 * 1024 * 1024
MOE_Y_BLOCK = 128
MOE_Y_FETCH = MOE_TOK_TILE // MOE_Y_BLOCK + 1


def _router_logits(x, w):
    xh, xm, _ = _split3(x)
    wh, wm, _ = _split3(w)
    packed = (wh.astype(F32) + pltpu.roll(wm.astype(F32), N_EXPERTS, axis=1)).astype(BF16)
    s = _dot(xh, packed) + _dot(xm, packed)
    return s + pltpu.roll(s, LANE - N_EXPERTS, axis=1)


def _route_kernel(x_ref, ng_ref, router_ref, ltri_ref, pos_ref, gate_ref, post_ref, before_ref, total_ref,
                  carry_ref):
    tm = x_ref.shape[0]

    @pl.when(pl.program_id(0) == 0)
    def _():
        carry_ref[...] = jnp.zeros_like(carry_ref)

    hn = _rms(x_ref[...], ng_ref[...])
    logits = _router_logits(hn, router_ref[...])
    lane = lax.broadcasted_iota(jnp.int32, (tm, LANE), 1)
    logits = jnp.where(lane < N_EXPERTS, logits, -jnp.inf)
    v1 = jnp.max(logits, axis=1, keepdims=True)
    i1 = jnp.min(jnp.where(logits == v1, lane, LANE), axis=1, keepdims=True)
    rest = jnp.where(lane == i1, -jnp.inf, logits)
    v2 = jnp.max(rest, axis=1, keepdims=True)
    i2 = jnp.min(jnp.where(rest == v2, lane, LANE), axis=1, keepdims=True)
    w1 = 1.0 / (1.0 + jnp.exp(v2 - v1))
    gate_ref[...] = jnp.where(lane == i1, w1, 0.0) + jnp.where(lane == i2, 1.0 - w1, 0.0)

    member = jnp.where((lane == i1) | (lane == i2), 1.0, 0.0)
    carry = carry_ref[...]
    ranks = []
    for blk in range(tm // MOE_TOK_TILE):
        before_ref[blk] = carry
        mb = member[blk * MOE_TOK_TILE:(blk + 1) * MOE_TOK_TILE]
        ranks.append(_dot(ltri_ref[...], mb.astype(BF16)) + carry)
        carry = carry + jnp.sum(mb, axis=0, keepdims=True)
    pos = jnp.where(member > 0.0, jnp.concatenate(ranks, axis=0), -1.0)
    pos_ref[...] = pos
    post_ref[...] = pos.T[:SUBLANE, :]
    carry_ref[...] = carry
    total_ref[...] = carry


def moe_route(x, norm_gain, router):
    s, d = x.shape
    tm = min(MOE_ROUTE_TILES * MOE_TOK_TILE, s)
    nsub = tm // MOE_TOK_TILE
    nt = s // tm
    router_p = jnp.zeros((d, LANE), F32).at[:, :N_EXPERTS].set(router.astype(F32))
    tt = min(MOE_TOK_TILE, s)
    ltri = _const(np.tril(np.ones((tt, tt)), -1), BF16)
    row = pl.BlockSpec((tm, LANE), lambda i: (i, 0))
    return pl.pallas_call(
        _route_kernel,
        grid=(nt,),
        in_specs=[pl.BlockSpec((tm, d), lambda i: (i, 0)), _const_spec((1, d)), _const_spec((d, LANE)),
                  _const_spec((tt, tt))],
        out_specs=[row, row, pl.BlockSpec((SUBLANE, tm), lambda i: (0, i)),
                   pl.BlockSpec((nsub, 1, LANE), lambda i: (i, 0, 0)), pl.BlockSpec((1, LANE), lambda i: (0, 0))],
        out_shape=[jax.ShapeDtypeStruct((s, LANE), F32), jax.ShapeDtypeStruct((s, LANE), F32),
                   jax.ShapeDtypeStruct((SUBLANE, s), F32), jax.ShapeDtypeStruct((nt * nsub, 1, LANE), F32),
                   jax.ShapeDtypeStruct((1, LANE), F32)],
        scratch_shapes=[pltpu.VMEM((1, LANE), F32)],
        compiler_params=_params("arbitrary"),
        name="moe_route",
    )(x, norm_gain.reshape(1, d).astype(F32), router_p, ltri)


def _moe_ffn_kernel(be_ref, r0_ref, tlo_ref, thi_ref, nv_ref, h_ref, post_ref, wg_ref, wu_ref, wd_ref,
                    y_ref, x_sc, acc_sc, *, tt, nsub):
    b = pl.program_id(0)
    f = pl.program_id(1)
    nb = pl.num_programs(0)
    valid = b < nv_ref[0]
    tmb, d = acc_sc.shape
    gr = tmb // nsub
    last_tile = h_ref.shape[0] // tt - 1

    def picked(e, want, t):
        t0 = pl.multiple_of(t * tt, tt)
        p = post_ref[pl.ds(e, 1), pl.ds(t0, tt)]
        sel = jnp.where(p == want, 1.0, 0.0).astype(BF16)
        return _dot(sel, h_ref[pl.ds(t0, tt), :])

    def gather_head(blk, sb):
        e = be_ref[blk]
        lo = tlo_ref[blk * nsub + sb]
        hi = thi_ref[blk * nsub + sb]
        want = (lax.broadcasted_iota(jnp.int32, (gr, 1), 0) + (r0_ref[blk] + sb * gr)).astype(F32)
        rows = picked(e, want, lo)
        for k in range(1, MOE_GATHER_TILES):
            rows = rows + picked(e, jnp.where(lo + k <= hi, want, -2.0), jnp.minimum(lo + k, last_tile))
        return rows.astype(BF16)

    def gather_tail(slot, blk, sb):
        e = be_ref[blk]
        want = (lax.broadcasted_iota(jnp.int32, (gr, 1), 0) + (r0_ref[blk] + sb * gr)).astype(F32)
        rows = pl.ds(pl.multiple_of(sb * gr, gr), gr)

        def more(t, carry):
            x_sc[slot, rows, :] = (x_sc[slot, rows, :].astype(F32) + picked(e, want, t)).astype(BF16)
            return carry

        lax.fori_loop(tlo_ref[blk * nsub + sb] + MOE_GATHER_TILES, thi_ref[blk * nsub + sb] + 1, more, 0)

    @pl.when((b == 0) & (f == 0))
    def _():
        for sb in range(nsub):
            x_sc[0, pl.ds(sb * gr, gr), :] = gather_head(0, sb)
            gather_tail(0, 0, sb)

    nxt = jnp.minimum(b + 1, nb - 1)
    nslot = (b + 1) % 2

    @pl.when(valid)
    def _():
        x_sc[nslot, pl.ds(pl.multiple_of(f * gr, gr), gr), :] = gather_head(nxt, f)
        xb = x_sc[b % 2]
        a = _silu(_dot(xb, wg_ref[0])) * _dot(xb, wu_ref[0])
        acc_sc[...] = jnp.where(f == 0, 0.0, acc_sc[...]) + _dot(a.astype(BF16), wd_ref[0].astype(BF16))

    @pl.when(valid & (thi_ref[nxt * nsub + f] - tlo_ref[nxt * nsub + f] >= MOE_GATHER_TILES))
    def _():
        gather_tail(nslot, nxt, f)

    @pl.when(f == nsub - 1)
    def _():
        y_ref[...] = jnp.where(valid, acc_sc[...], 0.0).astype(y_ref.dtype)


def moe_ffn(h, post, sched, w_gate, w_up, w_down):
    s, d = h.shape
    ne, _, dff = w_gate.shape
    tmb = MOE_ROW_BLOCK
    tt = min(MOE_TOK_TILE, s)
    nb = sched[0].shape[0]
    nf = tmb // MOE_GATHER_ROWS
    tf = dff // nf

    def fidx(b, f, nv):
        return jnp.where(b < nv[0], f, nf - 1)

    grid_spec = pltpu.PrefetchScalarGridSpec(
        num_scalar_prefetch=5,
        grid=(nb, nf),
        in_specs=[pl.BlockSpec((s, d), lambda b, f, *_: (0, 0), pipeline_mode=pl.Buffered(1)),
                  pl.BlockSpec((SUBLANE, s), lambda b, f, *_: (0, 0), pipeline_mode=pl.Buffered(1)),
                  pl.BlockSpec((1, d, tf), lambda b, f, be, r0, tlo, thi, nv: (be[b], 0, fidx(b, f, nv))),
                  pl.BlockSpec((1, d, tf), lambda b, f, be, r0, tlo, thi, nv: (be[b], 0, fidx(b, f, nv))),
                  pl.BlockSpec((1, tf, d), lambda b, f, be, r0, tlo, thi, nv: (be[b], fidx(b, f, nv), 0))],
        out_specs=pl.BlockSpec((tmb, d), lambda b, f, *_: (b, 0)),
        scratch_shapes=[pltpu.VMEM((2, tmb, d), BF16), pltpu.VMEM((tmb, d), F32)],
    )
    return pl.pallas_call(
        functools.partial(_moe_ffn_kernel, tt=tt, nsub=nf),
        grid_spec=grid_spec,
        out_shape=jax.ShapeDtypeStruct((nb * tmb, d), BF16),
        compiler_params=pltpu.CompilerParams(dimension_semantics=("arbitrary", "arbitrary"),
                                             vmem_limit_bytes=MOE_VMEM_LIMIT),
        name="moe_ffn",
    )(*sched, h, post, w_gate, w_up, w_down)


def _moe_combine_kernel(kb_ref, off_ref, lim_ref, x_ref, pos_ref, gate_ref, fg_ref, *rest):
    y_refs, o_ref, acc_sc = rest[:-2], rest[-2], rest[-1]
    t = pl.program_id(0)
    yb = y_refs[0].shape[0]
    pos = pos_ref[...]
    gate = gate_ref[...]
    col = lax.broadcasted_iota(jnp.int32, (1, yb), 1).astype(F32)

    def routed(e):
        pe = pos[:, e:e + 1]
        ge = gate[:, e:e + 1]
        r = jnp.where(pe >= 0.0, pe + off_ref[t * N_EXPERTS + e].astype(F32), -1.0)
        return r, ge

    def picked(e, k, r):
        sel = jnp.where(r == col + float(k * yb), 1.0, 0.0).astype(BF16)
        return _dot(sel, y_refs[MOE_Y_FETCH * e + k][...])

    col2 = lax.broadcasted_iota(jnp.int32, (1, 2 * yb), 1).astype(F32)
    acc = x_ref[...]
    for e in range(N_EXPERTS):
        r, ge = routed(e)
        pair = jnp.concatenate([y_refs[MOE_Y_FETCH * e][...], y_refs[MOE_Y_FETCH * e + 1][...]], axis=0)
        acc = acc + ge * _dot(jnp.where(r == col2, 1.0, 0.0).astype(BF16), pair)
    acc_sc[...] = acc
    for e in range(N_EXPERTS):
        for k in range(2, MOE_Y_FETCH):
            @pl.when(lim_ref[t * N_EXPERTS + e] > k * yb)
            def _(k=k, e=e):
                r, ge = routed(e)
                acc_sc[...] += ge * picked(e, k, r)
    o_ref[...] = _rms(acc_sc[...], fg_ref[...]).astype(o_ref.dtype)


def moe_combine(x, pos, gate, y, kb, off, lim, final_gain):
    s, d = x.shape
    tm = min(MOE_TOK_TILE, s)
    yb = MOE_Y_BLOCK
    last = y.shape[0] // yb - 1

    def yspec(e, k):
        def index(t, kb_r, off_r, lim_r):
            blk = jnp.minimum(kb_r[t * N_EXPERTS + e] + k, last)
            return (blk if k < 1 else jnp.where(lim_r[t * N_EXPERTS + e] > k * yb, blk, 0), 0)
        return pl.BlockSpec((yb, d), index)

    grid_spec = pltpu.PrefetchScalarGridSpec(
        num_scalar_prefetch=3,
        grid=(s // tm,),
        in_specs=[pl.BlockSpec((tm, d), lambda t, *_: (t, 0)),
                  pl.BlockSpec((tm, LANE), lambda t, *_: (t, 0)),
                  pl.BlockSpec((tm, LANE), lambda t, *_: (t, 0)),
                  pl.BlockSpec((1, d), lambda t, *_: (0, 0))]
        + [yspec(e, k) for e in range(N_EXPERTS) for k in range(MOE_Y_FETCH)],
        out_specs=pl.BlockSpec((tm, d), lambda t, *_: (t, 0)),
        scratch_shapes=[pltpu.VMEM((tm, d), F32)],
    )
    return pl.pallas_call(
        _moe_combine_kernel,
        grid_spec=grid_spec,
        out_shape=jax.ShapeDtypeStruct((s, d), F32),
        compiler_params=_params("arbitrary"),
        name="moe_combine",
    )(kb, off, lim, x, pos, gate, final_gain.reshape(1, d).astype(F32), *([y] * (MOE_Y_FETCH * N_EXPERTS)))


def _moe_schedule(before, total, s):
    tmb, yb, gr = MOE_ROW_BLOCK, MOE_Y_BLOCK, MOE_GATHER_ROWS
    nb = 2 * s // tmb + N_EXPERTS
    counts = total[0, :N_EXPERTS].astype(jnp.int32)
    nblk = (counts + tmb - 1) // tmb
    end = jnp.cumsum(nblk)
    first = end - nblk
    nvalid = end[-1]
    b = jnp.minimum(jnp.arange(nb, dtype=jnp.int32), nvalid - 1)
    blk_e = jnp.sum(b[:, None] >= end[None, :], axis=1).astype(jnp.int32)
    r0 = (b - first[blk_e]) * tmb
    cb = before[:, 0, :N_EXPERTS].astype(jnp.int32)
    r0s = (r0[:, None] + gr * jnp.arange(tmb // gr, dtype=jnp.int32)[None, :]).reshape(-1)
    cbe = cb[:, jnp.repeat(blk_e, tmb // gr)]
    tlo = (jnp.sum(cbe <= r0s[None, :], axis=0) - 1).astype(jnp.int32)
    thi = (jnp.sum(cbe < (r0s + gr)[None, :], axis=0) - 1).astype(jnp.int32)
    sched = (blk_e, r0.astype(jnp.int32), tlo, thi, nvalid.reshape(1).astype(jnp.int32))
    row_start = first[None, :] * tmb + cb
    kb = row_start // yb
    off = first[None, :] * tmb - kb * yb
    n_te = jnp.concatenate([cb[1:], counts[None, :]], axis=0) - cb
    lim = row_start - kb * yb + n_te
    flat = lambda a: a.reshape(-1).astype(jnp.int32)
    return sched, flat(kb), flat(off), flat(lim)


def moe_sparse(h, x, norm_gain, router, w_gate, w_up, w_down, final_gain):
    s, _ = x.shape
    pos, gate, post, before, total = moe_route(x, norm_gain, router)
    sched, kb, off, lim = _moe_schedule(before, total, s)
    y = moe_ffn(h, post, sched, w_gate, w_up, w_down)
    return moe_combine(x, pos, gate, y, kb, off, lim, final_gain)


FOX_F0 = 7 * BRANCH_W


def _split_w_in_kernel(wt_ref, mix_ref, f_ref, gate_ref):
    rest0 = FOX_F0 + N_HEADS
    gate0 = N_MIX_COLS + N_HEADS
    mix_ref[:, :FOX_F0] = wt_ref[pl.ds(0, FOX_F0), :].T.astype(BF16)
    mix_ref[:, FOX_F0:] = wt_ref[pl.ds(rest0, N_MIX_COLS - FOX_F0), :].T.astype(BF16)
    gate_ref[...] = wt_ref[pl.ds(gate0, N_BRANCH * D_MODEL), :].T.astype(BF16)
    f_rows = jnp.concatenate([wt_ref[pl.ds(FOX_F0, N_HEADS), :], jnp.zeros((LANE - N_HEADS, LANE), F32)], axis=0)
    f_ref[...] = f_rows.T.astype(BF16)


def _mixer_weights(w_in, layer, tr=LANE):
    _, d, cols = w_in.shape
    w_t = jnp.swapaxes(w_in, 1, 2)
    row = lambda w: pl.BlockSpec((tr, w), lambda i: (i, 0))
    return pl.pallas_call(
        _split_w_in_kernel,
        grid=(d // tr,),
        in_specs=[pl.BlockSpec((None, cols, tr), lambda i: (layer, 0, i))],
        out_specs=[row(N_MIX_COLS), row(LANE), row(N_BRANCH * D_MODEL)],
        out_shape=[jax.ShapeDtypeStruct((d, N_MIX_COLS), BF16), jax.ShapeDtypeStruct((d, LANE), BF16),
                   jax.ShapeDtypeStruct((d, N_BRANCH * D_MODEL), BF16)],
        compiler_params=_params("parallel"),
        name="split_w_in",
    )(w_t)


def kernel(x, w_in, w_branch, w_out, norm_mix_g, hgrn_lb_logits, hgrn_norm_g, fox_f_bias, pool_w, pool_scale,
           ret_gn_g, ret_gn_b, norm_ffn_g, ffn_w_gate, ffn_w_up, ffn_w_down, moe_router, moe_w_gate, moe_w_up,
           moe_w_down, final_norm_g):
    b, s, d = x.shape
    assert b == 1 and d == D_MODEL
    depth = w_in.shape[0]
    assert depth == 2, "layer 0 uses the dense FFN, layer 1 the experts and the final norm"
    xs = x.reshape(s, d)
    h = None
    out = None
    for layer in range(depth):
        w_mix, w_f, w_gate = _mixer_weights(w_in, layer)
        if layer == 0:
            proj, h = norm_matmul(xs, norm_mix_g[0], w_mix)
        else:
            proj = matmul(h, w_mix)
        ct, c_rows, fox_stats = fox_gate(h, proj, w_f, fox_f_bias[layer])
        branches = (
            hgrn2(proj, hgrn_lb_logits, hgrn_norm_g[layer], layer),
            fox_attention(proj, ct, c_rows, fox_stats),
            pool_mixer(proj, pool_w[layer], pool_scale[layer]),
            retention(proj, ret_gn_g[layer], ret_gn_b[layer]),
        )
        li = layer // 2
        if layer % 2 == 0:
            xs, h2, (ffn_wg, ffn_wu) = merge(h, branches, xs, w_gate, w_branch[layer].astype(BF16),
                                             w_out[layer].astype(BF16), norm_ffn_g[layer],
                                             cast=(ffn_w_gate[li], ffn_w_up[li]))
            xs, h = ffn_dense(h2, xs, ffn_wg, ffn_wu, ffn_w_down[li], norm_mix_g[layer + 1])
        else:
            xs, h2, (moe_wg, moe_wu) = merge(h, branches, xs, w_gate, w_branch[layer].astype(BF16),
                                             w_out[layer].astype(BF16), norm_ffn_g[layer], tm=MERGE_CAST_TILE,
                                             cast=(moe_w_gate[li], moe_w_up[li]))
            out = moe_sparse(h2, xs, norm_ffn_g[layer], moe_router[li], moe_wg, moe_wu, moe_w_down[li],
                             final_norm_g)
    return out.reshape(b, s, d)
```

```python
import functools
import math

import jax
import jax.numpy as jnp
import numpy as np
from jax import lax
from jax.experimental import pallas as pl
from jax.experimental.pallas import tpu as pltpu

D_MODEL = 1024
N_BRANCH = 4
BRANCH_W = D_MODEL // N_BRANCH
HEAD_DIM = 64
N_HEADS = BRANCH_W // HEAD_DIM
POOL_WINDOWS = (2, 4, 8, 16)
POOL_GROUP = BRANCH_W // len(POOL_WINDOWS)
POOL_HALO = 16
RET_DECAY_BASE = 5.0
ROPE_BASE = 10000.0
N_EXPERTS = 8
RMS_EPS = 1e-6
LN_EPS = 1e-5
N_MIX_COLS = 12 * BRANCH_W

LANE = 128
SUBLANE = 8
VMEM_LIMIT = 56 * 1024 * 1024

HG_CHUNK = 64
HG_SUB = 16
HG_FAST_MIN_LOGDECAY = -60.0
RET_CHUNK = 256

F32 = jnp.float32
BF16 = jnp.bfloat16
NT_DIMS = (((1,), (1,)), ((), ()))


def _params(*sem):
    return pltpu.CompilerParams(dimension_semantics=sem, vmem_limit_bytes=VMEM_LIMIT)


def _const_spec(shape):
    nd = len(shape)
    return pl.BlockSpec(shape, lambda *_: (0,) * nd, pipeline_mode=pl.Buffered(1))


def _split3(x):
    hi = x.astype(BF16)
    r1 = x - hi.astype(F32)
    mid = r1.astype(BF16)
    lo = (r1 - mid.astype(F32)).astype(BF16)
    return hi, mid, lo


def _dot(a, b):
    return jnp.dot(a, b, preferred_element_type=F32)


def _dot_nt(a, b):
    return lax.dot_general(a, b, NT_DIMS, preferred_element_type=F32)


def _dot_exact_rhs(x, m_bf16, terms=3):
    return sum(_dot(part, m_bf16) for part in _split3(x)[:terms])


def _dot_exact_lhs(m_bf16, x, terms=3):
    return sum(_dot(m_bf16, part) for part in _split3(x)[:terms])


def _sigmoid(x):
    return 1.0 / (1.0 + jnp.exp(-x))


def _silu(x):
    return x * _sigmoid(x)


def _rms(x, gain):
    return x * lax.rsqrt(jnp.mean(x * x, axis=-1, keepdims=True) + RMS_EPS) * gain


def _const(a, dtype=F32):
    return jnp.asarray(np.asarray(a, np.float32), dtype)


def _head_of(n):
    return np.arange(n) // HEAD_DIM


def _head_ones():
    h = _head_of(BRANCH_W)
    return _const(h[:, None] == h[None, :], BF16)


def _head_masks():
    return _const(_head_of(BRANCH_W)[None, :] == np.arange(N_HEADS)[:, None])


def _norm_matmul_kernel(x_ref, g_ref, b_ref, o_ref, h_ref):
    @pl.when(pl.program_id(1) == 0)
    def _():
        h_ref[...] = _rms(x_ref[...], g_ref[...]).astype(h_ref.dtype)

    o_ref[...] = _dot(h_ref[...], b_ref[...]).astype(o_ref.dtype)


def norm_matmul(x, gain, b, tm=2048, tn=1024):
    m, k = x.shape
    _, n = b.shape
    tm = min(tm, m)
    return pl.pallas_call(
        _norm_matmul_kernel,
        grid=(m // tm, n // tn),
        in_specs=[pl.BlockSpec((tm, k), lambda i, j: (i, 0)), _const_spec((1, k)),
                  pl.BlockSpec((k, tn), lambda i, j: (0, j))],
        out_specs=[pl.BlockSpec((tm, tn), lambda i, j: (i, j)), pl.BlockSpec((tm, k), lambda i, j: (i, 0))],
        out_shape=[jax.ShapeDtypeStruct((m, n), BF16), jax.ShapeDtypeStruct((m, k), BF16)],
        compiler_params=_params("arbitrary", "arbitrary"),
        name="in_proj_norm",
    )(x, gain.reshape(1, k).astype(F32), b)


def _matmul_kernel(a_ref, b_ref, o_ref):
    o_ref[...] = _dot(a_ref[...], b_ref[...]).astype(o_ref.dtype)


def matmul(a, b, out_dtype=BF16, tm=2048, tn=1024):
    m, k = a.shape
    _, n = b.shape
    tm = min(tm, m)
    return pl.pallas_call(
        _matmul_kernel,
        grid=(n // tn, m // tm),
        in_specs=[pl.BlockSpec((tm, k), lambda j, i: (i, 0)),
                  pl.BlockSpec((k, tn), lambda j, i: (0, j))],
        out_specs=pl.BlockSpec((tm, tn), lambda j, i: (i, j)),
        out_shape=jax.ShapeDtypeStruct((m, n), out_dtype),
        compiler_params=_params("parallel", "parallel"),
        name="in_proj",
    )(a, b)


def _hgrn_kernel(q_ref, f_ref, i_ref, g_ref, lbl_ref, ng_ref, ones_ref, hm_ref, tril_ref, halfsum_ref, fmask_ref,
                 o_ref, st_ref, bpad, kpad, vpad, astack, lf_sc, kk_sc, o_sc, *, layer, tile):
    c, sub = HG_CHUNK, HG_SUB
    nsub = c // sub
    half = c // 2

    @pl.when(pl.program_id(0) == 0)
    def _():
        st_ref[...] = jnp.zeros_like(st_ref)
        bpad[...] = jnp.zeros_like(bpad)
        kpad[...] = jnp.zeros_like(kpad)
        vpad[...] = jnp.zeros_like(vpad)

    lbl = lbl_ref[...]
    e = jnp.exp(lbl - jnp.max(lbl, axis=0, keepdims=True))
    p = e / jnp.sum(e, axis=0, keepdims=True)
    lb = jnp.zeros((1, BRANCH_W), F32)
    for l in range(1, layer + 1):
        lb = lb + p[l:l + 1, :]

    ones_bd = ones_ref[...]
    hm = hm_ref[...]
    tril = tril_ref[...]
    row = lax.broadcasted_iota(jnp.int32, (c, 1), 0)
    row_in_sub = row % sub
    bd_mask = ones_bd.astype(F32)

    sig = _sigmoid(f_ref[...].astype(F32))
    logf_all = jnp.log(lb + (1.0 - lb) * sig)
    lf_sc[...] = _dot_exact_lhs(tril, logf_all, terms=2)
    kk_sc[...] = (1.0 - lb) * (1.0 - sig)
    min_decay = jnp.min(_dot(halfsum_ref[...], logf_all.astype(BF16)))

    def load(ci):
        r0 = pl.multiple_of(ci * c, c)
        q = q_ref[pl.ds(r0, c), :].astype(F32)
        v = i_ref[pl.ds(r0, c), :].astype(F32)
        kk = kk_sc[pl.ds(r0, c), :]
        b = lf_sc[pl.ds(r0, c), :]
        return r0, q, v, kk, b

    def finish(r0, q_decayed, v, kk, b, intra):
        st = st_ref[...]
        inter = _dot_nt(q_decayed.astype(BF16), st.astype(BF16))
        b_last = b[c - 1:c, :]
        ks_end = (kk * jnp.exp(b_last - b)).astype(BF16)
        upd = _dot(v.T.astype(BF16), ks_end)
        st_ref[...] = st * jnp.exp(b_last) + upd * bd_mask
        o_sc[pl.ds(r0, c), :] = intra + inter

    def fast_chunk(ci, carry):
        r0, q, v, kk, b = load(ci)
        second = row >= half
        m_row = b[half - 1:half, :]
        mref = jnp.where(second, m_row, 0.0)
        qp = q * jnp.exp(b - mref)
        kp = kk * jnp.exp(mref - b)
        e_m = jnp.exp(m_row)
        kaug = jnp.concatenate([kp, kp[:half, :] * e_m], axis=0)
        vaug = jnp.concatenate([v, v[:half, :]], axis=0)
        k_heads = jnp.concatenate([kaug * hm[h:h + 1, :] for h in range(N_HEADS)], axis=0).astype(BF16)
        v_heads = jnp.concatenate([vaug * hm[h:h + 1, :] for h in range(N_HEADS)], axis=0).astype(BF16)
        sc = jnp.where(fmask_ref[...] > 0.0, _dot_nt(qp.astype(BF16), k_heads), 0.0)
        intra = _dot(sc.astype(BF16), v_heads)
        finish(r0, jnp.where(second, qp * e_m, qp), v, kk, b, intra)
        return carry

    def exact_chunk(ci, carry):
        r0, q, v, kk, b = load(ci)

        bpad[pl.ds(sub, c), :] = b
        kpad[pl.ds(sub, c), :] = kk
        vpad[pl.ds(sub, c), :] = v

        for d in range(sub):
            b_d = bpad[pl.ds(sub - d, c), :]
            k_d = kpad[pl.ds(sub - d, c), :]
            a = jnp.where(row_in_sub >= d, q * k_d * jnp.exp(b - b_d), 0.0)
            astack[pl.ds(d * c, c), :] = a.astype(BF16)
        pall = _dot(astack[...], ones_bd)
        intra = jnp.zeros((c, BRANCH_W), F32)
        for d in range(sub):
            intra = intra + pall[d * c:(d + 1) * c, :] * vpad[pl.ds(sub - d, c), :]

        pieces = [jnp.zeros((sub, BRANCH_W), F32)]
        for si in range(1, nsub):
            lo = si * sub
            m_i = b[lo - 1:lo, :]
            qs = q[lo:lo + sub, :] * jnp.exp(b[lo:lo + sub, :] - m_i)
            ks = (kk[:lo, :] * jnp.exp(m_i - b[:lo, :])).astype(BF16)
            qx = jnp.concatenate([qs * hm[h:h + 1, :] for h in range(N_HEADS)], axis=0).astype(BF16)
            sc = _dot_nt(qx, ks)
            r = _dot(sc.astype(BF16), v[:lo, :].astype(BF16))
            acc = jnp.zeros((sub, BRANCH_W), F32)
            for h in range(N_HEADS):
                acc = acc + r[h * sub:(h + 1) * sub, :] * hm[h:h + 1, :]
            pieces.append(acc)
        intra = intra + jnp.concatenate(pieces, axis=0)
        finish(r0, q * jnp.exp(b), v, kk, b, intra)
        return carry

    lax.cond(min_decay >= HG_FAST_MIN_LOGDECAY,
             lambda: lax.fori_loop(0, tile // c, fast_chunk, 0, unroll=2),
             lambda: lax.fori_loop(0, tile // c, exact_chunk, 0))

    o = o_sc[...]
    ms = _dot_exact_rhs(o * o, ones_bd, terms=2) * (1.0 / HEAD_DIM)
    y = o * lax.rsqrt(ms + RMS_EPS) * ng_ref[...] * _silu(g_ref[...].astype(F32))
    o_ref[...] = y.astype(o_ref.dtype)


def hgrn2(proj, lb_logits, norm_g, layer, tile=512):
    s = proj.shape[0]
    depth = lb_logits.shape[0]
    c, sub = HG_CHUNK, HG_SUB
    half = c // 2
    tile = min(tile, s)
    col = lambda j: pl.BlockSpec((tile, BRANCH_W), lambda i, j=j: (i, j))
    pos = np.arange(tile)
    tril = _const((pos[:, None] // c == pos[None, :] // c) & (pos[None, :] <= pos[:, None]), BF16)
    nhalf = tile // half
    halfsum = _const(np.arange(tile)[None, :] // half == np.arange(nhalf)[:, None], BF16)
    t = np.arange(c)[:, None]
    col_s = np.arange(c + half)[None, :]
    same_half = (col_s < c) & (col_s // half == t // half) & (col_s <= t)
    cross = (col_s >= c) & (t >= half)
    fmask = _const(np.tile(same_half | cross, (1, N_HEADS)))
    return pl.pallas_call(
        functools.partial(_hgrn_kernel, layer=layer, tile=tile),
        grid=(s // tile,),
        in_specs=[col(0), col(1), col(2), col(3),
                  _const_spec((depth, BRANCH_W)), _const_spec((1, BRANCH_W)),
                  _const_spec((BRANCH_W, BRANCH_W)), _const_spec((N_HEADS, BRANCH_W)),
                  _const_spec((tile, tile)), _const_spec((nhalf, tile)),
                  _const_spec((c, N_HEADS * (c + half)))],
        out_specs=pl.BlockSpec((tile, BRANCH_W), lambda i: (i, 0)),
        out_shape=jax.ShapeDtypeStruct((s, BRANCH_W), BF16),
        scratch_shapes=[pltpu.VMEM((BRANCH_W, BRANCH_W), F32),
                        pltpu.VMEM((c + sub, BRANCH_W), F32),
                        pltpu.VMEM((c + sub, BRANCH_W), F32),
                        pltpu.VMEM((c + sub, BRANCH_W), F32),
                        pltpu.VMEM((sub * c, BRANCH_W), BF16),
                        pltpu.VMEM((tile, BRANCH_W), F32),
                        pltpu.VMEM((tile, BRANCH_W), F32),
                        pltpu.VMEM((tile, BRANCH_W), F32)],
        compiler_params=_params("arbitrary"),
        name="hgrn2",
    )(proj, proj, proj, proj, lb_logits.astype(F32), norm_g.reshape(1, BRANCH_W).astype(F32),
      _head_ones(), _head_masks(), tril, halfsum, fmask)


FOX_TILE = 256
FOX_GATE_BLOCKS = 2
FOX_NSTAT = 16
FOX_FIXED_MAX = 30.0
FOX_SKIP_LOG = 30.0


def _fox_gate_kernel(h_ref, q_ref, k_ref, wf_ref, bias_ref, tril_ref, ones_ref,
                     ct_ref, c_ref, stat_ref, carry_ref, kmax_ref):
    @pl.when(pl.program_id(0) == 0)
    def _():
        carry_ref[...] = jnp.zeros_like(carry_ref)
        kmax_ref[...] = jnp.zeros_like(kmax_ref)

    logit = _dot(h_ref[...], wf_ref[...]) + bias_ref[...]
    logf = jnp.minimum(logit, 0.0) - jnp.log(1.0 + jnp.exp(-jnp.abs(logit)))
    carry = carry_ref[...]
    parts = []
    for blk in range(logf.shape[0] // FOX_TILE):
        part = _dot_exact_lhs(tril_ref[...], logf[blk * FOX_TILE:(blk + 1) * FOX_TILE]) + carry
        carry = part[-1:, :]
        parts.append(part)
    cum = jnp.concatenate(parts, axis=0)
    carry_ref[...] = carry
    ct_ref[...] = cum.T[:SUBLANE, :]
    c_ref[...] = cum

    ones_bd = ones_ref[...]
    q = q_ref[...].astype(F32)
    k = k_ref[...].astype(F32)
    scale = HEAD_DIM ** -0.5
    head_lane = lax.broadcasted_iota(jnp.int32, (1, BRANCH_W), 1) // HEAD_DIM
    c_heads = jnp.zeros(q.shape, F32)
    for h in range(N_HEADS):
        c_heads = jnp.where(head_lane == h, cum[:, h:h + 1], c_heads)
    slack = 1.0 + 2.0 ** -6
    qn = jnp.sqrt(_dot((q * q).astype(BF16), ones_bd)) * (scale * slack)
    kn = jnp.sqrt(_dot((k * k).astype(BF16), ones_bd)) * slack
    diag = _dot((q * k).astype(BF16), ones_bd) * scale - (2.0 ** -6) * qn * kn
    e_row = c_heads - diag
    kmax = kmax_ref[...]
    for blk in range(q.shape[0] // FOX_TILE):
        rs = slice(blk * FOX_TILE, (blk + 1) * FOX_TILE)
        kmax = jnp.maximum(kmax, jnp.max(kn[rs], axis=0, keepdims=True))
        rows = [jnp.max(qn[rs], axis=0, keepdims=True),
                jnp.max(e_row[rs], axis=0, keepdims=True),
                kmax,
                c_heads[(blk + 1) * FOX_TILE - 1:(blk + 1) * FOX_TILE, :]]
        stat_ref[blk] = jnp.concatenate(rows + [jnp.zeros((SUBLANE - len(rows), BRANCH_W), F32)], axis=0)
    kmax_ref[...] = kmax


def fox_gate(h, proj, w_f, f_bias):
    s, d = h.shape
    tile = min(FOX_GATE_BLOCKS * FOX_TILE, s)
    nblk = tile // FOX_TILE
    bias = jnp.zeros((1, LANE), F32).at[0, :N_HEADS].set(f_bias.astype(F32))
    tril = _const(np.tril(np.ones((FOX_TILE, FOX_TILE))), BF16)
    ct, c_rows, stats = pl.pallas_call(
        _fox_gate_kernel,
        grid=(s // tile,),
        in_specs=[pl.BlockSpec((tile, d), lambda i: (i, 0)),
                  pl.BlockSpec((tile, BRANCH_W), lambda i: (i, 4)),
                  pl.BlockSpec((tile, BRANCH_W), lambda i: (i, 5)),
                  _const_spec((d, LANE)), _const_spec((1, LANE)), _const_spec((FOX_TILE, FOX_TILE)),
                  _const_spec((BRANCH_W, BRANCH_W))],
        out_specs=[pl.BlockSpec((SUBLANE, tile), lambda i: (0, i)),
                   pl.BlockSpec((tile, LANE), lambda i: (i, 0)),
                   pl.BlockSpec((nblk, SUBLANE, BRANCH_W), lambda i: (i, 0, 0))],
        out_shape=[jax.ShapeDtypeStruct((SUBLANE, s), F32),
                   jax.ShapeDtypeStruct((s, LANE), F32),
                   jax.ShapeDtypeStruct((s // FOX_TILE, SUBLANE, BRANCH_W), F32)],
        scratch_shapes=[pltpu.VMEM((1, LANE), F32), pltpu.VMEM((1, BRANCH_W), F32)],
        compiler_params=_params("arbitrary"),
        name="fox_gate",
    )(h, proj, proj, w_f, bias, tril, _head_ones())
    return ct, c_rows, stats[:, :4, ::HEAD_DIM].reshape(-1)


def _fox_kernel(stat_ref, q_ref, k_ref, v_ref, ct_ref, c_ref, hm_ref, o_ref, m_sc, l_sc, acc_sc, *, tq):
    i = pl.program_id(0)
    q0 = pl.multiple_of(i * tq, tq)
    hm = hm_ref[...]
    q = q_ref[...].astype(F32) * (HEAD_DIM ** -0.5)
    qh = [(q * hm[h:h + 1, :]).astype(BF16) for h in range(N_HEADS)]
    c_q0 = ct_ref[:, pl.ds(q0, tq)][:, 0:1]

    first = i
    for h in range(N_HEADS):
        qmax = stat_ref[i * FOX_NSTAT + h]
        emax = stat_ref[i * FOX_NSTAT + N_HEADS + h]

        def needed(j, h=h, qmax=qmax, emax=emax):
            jc = jnp.maximum(j, 0)
            bound = (qmax * stat_ref[jc * FOX_NSTAT + 2 * N_HEADS + h] + emax
                     - stat_ref[jc * FOX_NSTAT + 3 * N_HEADS + h])
            return (j >= 0) & (bound >= -FOX_SKIP_LOG)

        last_dropped = lax.while_loop(needed, lambda j: j - 1, i - 1)
        first = jnp.minimum(first, last_dropped + 1)

    l_sc[...] = jnp.zeros_like(l_sc)
    acc_sc[...] = jnp.zeros_like(acc_sc)

    def causal(sc):
        r = lax.broadcasted_iota(jnp.int32, (tq, tq), 0)
        cidx = lax.broadcasted_iota(jnp.int32, (tq, tq), 1)
        return jnp.where(cidx <= r, sc, -jnp.inf)

    def online_block(s0, diagonal):
        kb = k_ref[pl.ds(s0, tq), :]
        vb = v_ref[pl.ds(s0, tq), :]
        bias = c_q0 - ct_ref[:, pl.ds(s0, tq)]
        for h in range(N_HEADS):
            sc = _dot_nt(qh[h], kb) + bias[h:h + 1, :]
            if diagonal:
                sc = causal(sc)
            m_prev = m_sc[h]
            m_new = jnp.maximum(m_prev, jnp.max(sc, axis=1, keepdims=True))
            alpha = jnp.exp(m_prev - m_new)
            p = jnp.exp(sc - jnp.tile(m_new, (1, tq // LANE)))
            l_sc[h] = alpha * l_sc[h] + jnp.sum(p, axis=1, keepdims=True)
            acc_sc[h] = acc_sc[h] * jnp.tile(alpha, (1, BRANCH_W // LANE)) + _dot(p.astype(BF16), vb)
            m_sc[h] = m_new

    tops = [stat_ref[i * FOX_NSTAT + h] * stat_ref[i * FOX_NSTAT + 2 * N_HEADS + h] for h in range(N_HEADS)]
    c_tile = c_ref[...]
    shift = [c_tile[:, h:h + 1] - c_q0[h:h + 1, :] - tops[h] for h in range(N_HEADS)]

    def fixed_block(s0, diagonal):
        kb = k_ref[pl.ds(s0, tq), :]
        vb = v_ref[pl.ds(s0, tq), :]
        bias = c_q0 - ct_ref[:, pl.ds(s0, tq)]
        for h in range(N_HEADS):
            sc = _dot_nt(qh[h], kb) + bias[h:h + 1, :] + shift[h]
            if diagonal:
                sc = causal(sc)
            p = jnp.exp(sc)
            l_sc[h] += p[:, :LANE] + p[:, LANE:]
            acc_sc[h] += _dot(p.astype(BF16), vb)

    def run(block, row_sum):
        lax.fori_loop(first, i, lambda j, carry: (block(pl.multiple_of(j * tq, tq), False), carry)[1], 0)
        block(q0, True)
        out = jnp.zeros((tq, BRANCH_W), F32)
        for h in range(N_HEADS):
            out = out + acc_sc[h] * hm[h:h + 1, :] / row_sum(l_sc[h])
        o_ref[...] = out.astype(o_ref.dtype)

    def run_online():
        m_sc[...] = jnp.full_like(m_sc, -jnp.inf)
        run(online_block, lambda l: jnp.tile(l, (1, BRANCH_W // LANE)))

    def run_fixed():
        run(fixed_block, lambda l: jnp.sum(l, axis=1, keepdims=True))

    lax.cond(functools.reduce(jnp.maximum, tops) <= FOX_FIXED_MAX, run_fixed, run_online)


def fox_attention(proj, ct, c_rows, stats):
    s = proj.shape[0]
    tq = min(FOX_TILE, s)
    full = lambda j: pl.BlockSpec((s, BRANCH_W), lambda i, j=j: (0, j), pipeline_mode=pl.Buffered(1))
    return pl.pallas_call(
        functools.partial(_fox_kernel, tq=tq),
        grid=(s // tq,),
        in_specs=[pl.BlockSpec(memory_space=pltpu.SMEM),
                  pl.BlockSpec((tq, BRANCH_W), lambda i: (i, 4)), full(5), full(6),
                  _const_spec((SUBLANE, s)), pl.BlockSpec((tq, LANE), lambda i: (i, 0)),
                  _const_spec((N_HEADS, BRANCH_W))],
        out_specs=pl.BlockSpec((tq, BRANCH_W), lambda i: (i, 0)),
        out_shape=jax.ShapeDtypeStruct((s, BRANCH_W), BF16),
        scratch_shapes=[pltpu.VMEM((N_HEADS, tq, LANE), F32),
                        pltpu.VMEM((N_HEADS, tq, LANE), F32),
                        pltpu.VMEM((N_HEADS, tq, BRANCH_W), F32)],
        compiler_params=_params("parallel"),
        name="fox_attention",
    )(stats, proj, proj, proj, ct, c_rows, _head_masks())


def _pool_kernel(u_ref, w_ref, scale_ref, o_ref, ext, *, tile):
    i = pl.program_id(0)

    @pl.when(i == 0)
    def _():
        ext[pl.ds(0, POOL_HALO), :] = jnp.zeros((POOL_HALO, BRANCH_W), F32)

    u = u_ref[...].astype(F32)
    ext[pl.ds(POOL_HALO, tile), :] = u
    pos = (i * tile + lax.broadcasted_iota(jnp.int32, (tile, 1), 0) + 1).astype(F32)
    halves = []
    for half in range(BRANCH_W // LANE):
        lanes = pl.ds(half * LANE, LANE)
        w_small, w_big = POOL_WINDOWS[2 * half], POOL_WINDOWS[2 * half + 1]
        run = u[:, half * LANE:(half + 1) * LANE]
        sums = {}
        for j in range(1, w_big):
            if j == w_small:
                sums[w_small] = run
            run = run + ext[pl.ds(POOL_HALO - j, tile), lanes]
        sums[w_big] = run
        lane = lax.broadcasted_iota(jnp.int32, (1, LANE), 1)
        small = lane < POOL_GROUP
        total = jnp.where(small, sums[w_small], sums[w_big])
        count = jnp.where(small, jnp.minimum(pos, float(w_small)), jnp.minimum(pos, float(w_big)))
        halves.append(total / count)
    mean = jnp.concatenate(halves, axis=1)
    d = (mean - u).astype(BF16)
    y = _dot(d, w_ref[...]) * scale_ref[...]
    o_ref[...] = y.astype(o_ref.dtype)
    ext[pl.ds(0, POOL_HALO), :] = u[tile - POOL_HALO:, :]


def pool_mixer(proj, w_pool, scale, tile=2048):
    s = proj.shape[0]
    tile = min(tile, s)
    ng = len(POOL_WINDOWS)
    w_bd = jnp.zeros((BRANCH_W, BRANCH_W), F32)
    for gi in range(ng):
        lo = gi * POOL_GROUP
        w_bd = w_bd.at[lo:lo + POOL_GROUP, lo:lo + POOL_GROUP].set(w_pool[gi].astype(F32))
    return pl.pallas_call(
        functools.partial(_pool_kernel, tile=tile),
        grid=(s // tile,),
        in_specs=[pl.BlockSpec((tile, BRANCH_W), lambda i: (i, 7)),
                  _const_spec((BRANCH_W, BRANCH_W)), _const_spec((1, BRANCH_W))],
        out_specs=pl.BlockSpec((tile, BRANCH_W), lambda i: (i, 0)),
        out_shape=jax.ShapeDtypeStruct((s, BRANCH_W), BF16),
        scratch_shapes=[pltpu.VMEM((tile + POOL_HALO, BRANCH_W), F32)],
        compiler_params=_params("arbitrary"),
        name="pool_mixer",
    )(proj, w_bd.astype(BF16), scale.reshape(1, BRANCH_W).astype(F32))


def _ret_kernel(q_ref, k_ref, v_ref, g_ref, cos_ref, sin_ref, perm_ref, ones_ref, hm_ref,
                dstack_ref, xi_ref, zeta_ref, gc_ref, gng_ref, gnb_ref, o_ref, st_ref, *, tile):
    c = RET_CHUNK

    @pl.when(pl.program_id(0) == 0)
    def _():
        st_ref[...] = jnp.zeros_like(st_ref)

    perm = perm_ref[...]
    ones_bd = ones_ref[...]
    bd_mask = ones_bd.astype(F32)
    hm = hm_ref[...]

    cos = cos_ref[...]
    sin = sin_ref[...]
    q_all = q_ref[...]
    k_all = k_ref[...]
    qr_all = q_all.astype(F32) * cos + _dot(q_all, perm) * sin
    kr_all = (k_all.astype(F32) * cos + _dot(k_all, perm) * sin) * (HEAD_DIM ** -0.5)

    outs = []
    for ci in range(tile // c):
        r0 = ci * c
        qr = qr_all[r0:r0 + c, :]
        kr = kr_all[r0:r0 + c, :]
        v = v_ref[pl.ds(r0, c), :]

        qx = jnp.concatenate([qr * hm[h:h + 1, :] for h in range(N_HEADS)], axis=0).astype(BF16)
        sc = _dot_nt(qx, kr.astype(BF16)) * dstack_ref[...]
        r = _dot(sc.astype(BF16), v)
        intra = jnp.zeros((c, BRANCH_W), F32)
        for h in range(N_HEADS):
            intra = intra + r[h * c:(h + 1) * c, :] * hm[h:h + 1, :]

        st = st_ref[...]
        inter = _dot_nt((qr * xi_ref[...]).astype(BF16), st.astype(BF16))
        upd = _dot(v.astype(F32).T.astype(BF16), (kr * zeta_ref[...]).astype(BF16))
        st_ref[...] = st * gc_ref[...] + upd * bd_mask
        outs.append(intra + inter)

    o = jnp.concatenate(outs, axis=0)
    mu = _dot_exact_rhs(o, ones_bd, terms=2) * (1.0 / HEAD_DIM)
    cen = o - mu
    var = _dot_exact_rhs(cen * cen, ones_bd, terms=2) * (1.0 / HEAD_DIM)
    y = cen * lax.rsqrt(var + LN_EPS) * gng_ref[...] + gnb_ref[...]
    o_ref[...] = (y * _silu(g_ref[...].astype(F32))).astype(o_ref.dtype)


def _rope_tables(s):
    half = HEAD_DIM // 2
    pos = np.arange(s, dtype=np.float64)
    inv_freq = ROPE_BASE ** (-np.arange(half, dtype=np.float64) / half)
    lane = np.arange(BRANCH_W)
    ang = pos[:, None] * inv_freq[lane % half][None, :]
    sign = np.where(lane % HEAD_DIM < half, -1.0, 1.0)
    return _const(np.cos(ang)), _const(np.sin(ang) * sign[None, :])


def _ret_constants():
    c = RET_CHUNK
    half = HEAD_DIM // 2
    lane = np.arange(BRANCH_W)
    partner = np.where(lane % HEAD_DIM < half, lane + half, lane - half)
    perm = lane[:, None] == partner[None, :]
    log_gamma = np.log1p(-np.exp2(-RET_DECAY_BASE - np.arange(N_HEADS, dtype=np.float64)))
    ci = np.arange(c, dtype=np.float64)
    diff = ci[:, None] - ci[None, :]
    intra = np.where(diff >= 0, np.exp(diff * log_gamma[:, None, None]), 0.0)
    dstack = intra.reshape(N_HEADS * c, c)
    lg_lane = np.repeat(log_gamma, HEAD_DIM)[None, :]
    xi = np.exp((ci[:, None] + 1.0) * lg_lane)
    zeta = np.exp((c - 1.0 - ci[:, None]) * lg_lane)
    gc = np.exp(c * lg_lane)
    return _const(perm, BF16), _const(dstack), _const(xi), _const(zeta), _const(gc)


def retention(proj, gn_g, gn_b, tile=1024):
    s = proj.shape[0]
    c = RET_CHUNK
    tile = min(tile, s)
    cos_t, sin_t = _rope_tables(s)
    perm, dstack, xi, zeta, gc = _ret_constants()
    col = lambda j: pl.BlockSpec((tile, BRANCH_W), lambda i, j=j: (i, j))
    row = pl.BlockSpec((tile, BRANCH_W), lambda i: (i, 0))
    return pl.pallas_call(
        functools.partial(_ret_kernel, tile=tile),
        grid=(s // tile,),
        in_specs=[col(8), col(9), col(10), col(11), row, row,
                  _const_spec((BRANCH_W, BRANCH_W)), _const_spec((BRANCH_W, BRANCH_W)),
                  _const_spec((N_HEADS, BRANCH_W)), _const_spec((N_HEADS * c, c)),
                  _const_spec((c, BRANCH_W)), _const_spec((c, BRANCH_W)), _const_spec((1, BRANCH_W)),
                  _const_spec((1, BRANCH_W)), _const_spec((1, BRANCH_W))],
        out_specs=row,
        out_shape=jax.ShapeDtypeStruct((s, BRANCH_W), BF16),
        scratch_shapes=[pltpu.VMEM((BRANCH_W, BRANCH_W), F32)],
        compiler_params=_params("arbitrary"),
        name="retention",
    )(proj, proj, proj, proj, cos_t, sin_t, perm, _head_ones(), _head_masks(), dstack, xi, zeta, gc,
      gn_g.reshape(1, BRANCH_W).astype(F32), gn_b.reshape(1, BRANCH_W).astype(F32))


def _merge_kernel(h_ref, o0_ref, o1_ref, o2_ref, o3_ref, x_ref, wg_ref, wb_ref, wo_ref, g_ref, *rest):
    n_cast = (len(rest) - 2) // 2
    cast_in, (xo_ref, ho_ref), cast_out = rest[:n_cast], rest[n_cast:n_cast + 2], rest[n_cast + 2:]
    for src, dst in zip(cast_in, cast_out):
        dst[...] = src[...].astype(dst.dtype)
    h = h_ref[...]
    merged = jnp.zeros(x_ref.shape, F32)
    for bi, o_ref in enumerate((o0_ref, o1_ref, o2_ref, o3_ref)):
        gate = _sigmoid(_dot(h, wg_ref[:, bi * D_MODEL:(bi + 1) * D_MODEL]))
        merged = merged + gate * _dot(o_ref[...], wb_ref[bi])
    x_new = x_ref[...] + _dot(merged.astype(BF16), wo_ref[...])
    xo_ref[...] = x_new
    ho_ref[...] = _rms(x_new, g_ref[...]).astype(ho_ref.dtype)


def merge(h, branches, x, w_gate, w_branch, w_out, next_gain, tm=512, cast=()):
    s, d = x.shape
    tm = min(tm, s)
    steps = s // tm
    row = lambda w: pl.BlockSpec((tm, w), lambda i: (i, 0))
    flat = [c.reshape(-1, c.shape[-1]) for c in cast]
    slab = lambda c: pl.BlockSpec((c.shape[0] // steps, c.shape[1]), lambda i: (i, 0))
    outs = pl.pallas_call(
        _merge_kernel,
        grid=(steps,),
        in_specs=[row(d), row(BRANCH_W), row(BRANCH_W), row(BRANCH_W), row(BRANCH_W), row(d),
                  _const_spec((d, N_BRANCH * d)), _const_spec((N_BRANCH, BRANCH_W, d)),
                  _const_spec((d, d)), _const_spec((1, d))] + [slab(c) for c in flat],
        out_specs=[row(d), row(d)] + [slab(c) for c in flat],
        out_shape=[jax.ShapeDtypeStruct((s, d), F32), jax.ShapeDtypeStruct((s, d), BF16)]
        + [jax.ShapeDtypeStruct(c.shape, BF16) for c in flat],
        compiler_params=_params("parallel"),
        name="merge",
    )(h, *branches, x, w_gate, w_branch, w_out, next_gain.reshape(1, d).astype(F32), *flat)
    return outs[0], outs[1], [o.reshape(c.shape) for o, c in zip(outs[2:], cast)]


def _ffn_kernel(h_ref, x_ref, wg_ref, wu_ref, wd_ref, g_ref, xo_ref, ho_ref, acc_ref):
    f = pl.program_id(1)

    @pl.when(f == 0)
    def _():
        acc_ref[...] = jnp.zeros_like(acc_ref)

    h = h_ref[...]
    a = _silu(_dot(h, wg_ref[...].astype(BF16))) * _dot(h, wu_ref[...].astype(BF16))
    acc_ref[...] += _dot(a.astype(BF16), wd_ref[...].astype(BF16))

    @pl.when(f == pl.num_programs(1) - 1)
    def _():
        x_new = x_ref[...] + acc_ref[...]
        xo_ref[...] = x_new
        ho_ref[...] = _rms(x_new, g_ref[...]).astype(ho_ref.dtype)


def ffn_dense(h, x, w_gate, w_up, w_down, next_gain, tm=1024, tf=512):
    s, d = x.shape
    tm = min(tm, s)
    dff = w_gate.shape[1]
    row = lambda: pl.BlockSpec((tm, d), lambda i, f: (i, 0))
    return pl.pallas_call(
        _ffn_kernel,
        grid=(s // tm, dff // tf),
        in_specs=[row(), row(),
                  pl.BlockSpec((d, tf), lambda i, f: (0, f)),
                  pl.BlockSpec((d, tf), lambda i, f: (0, f)),
                  pl.BlockSpec((tf, d), lambda i, f: (f, 0)),
                  _const_spec((1, d))],
        out_specs=[row(), row()],
        out_shape=[jax.ShapeDtypeStruct((s, d), F32), jax.ShapeDtypeStruct((s, d), BF16)],
        scratch_shapes=[pltpu.VMEM((tm, d), F32)],
        compiler_params=_params("parallel", "arbitrary"),
        name="ffn_dense",
    )(h, x, w_gate, w_up, w_down, next_gain.reshape(1, d).astype(F32))


MERGE_CAST_TILE = 256
MOE_TOK_TILE = 256
MOE_ROUTE_TILES = 4
MOE_ROW_BLOCK = 512
MOE_GATHER_ROWS = 128
MOE_GATHER_TILES = 4
MOE_VMEM_LIMIT = 60 * 1024 * 1024
MOE_Y_BLOCK = 128
MOE_Y_FETCH = MOE_TOK_TILE // MOE_Y_BLOCK + 1


def _router_logits(x, w):
    xh, xm, _ = _split3(x)
    wh, wm, _ = _split3(w)
    packed = (wh.astype(F32) + pltpu.roll(wm.astype(F32), N_EXPERTS, axis=1)).astype(BF16)
    s = _dot(xh, packed) + _dot(xm, packed)
    return s + pltpu.roll(s, LANE - N_EXPERTS, axis=1)


def _route_kernel(x_ref, ng_ref, router_ref, ltri_ref, pos_ref, gate_ref, post_ref, before_ref, total_ref,
                  carry_ref):
    tm = x_ref.shape[0]

    @pl.when(pl.program_id(0) == 0)
    def _():
        carry_ref[...] = jnp.zeros_like(carry_ref)

    hn = _rms(x_ref[...], ng_ref[...])
    logits = _router_logits(hn, router_ref[...])
    lane = lax.broadcasted_iota(jnp.int32, (tm, LANE), 1)
    logits = jnp.where(lane < N_EXPERTS, logits, -jnp.inf)
    v1 = jnp.max(logits, axis=1, keepdims=True)
    i1 = jnp.min(jnp.where(logits == v1, lane, LANE), axis=1, keepdims=True)
    rest = jnp.where(lane == i1, -jnp.inf, logits)
    v2 = jnp.max(rest, axis=1, keepdims=True)
    i2 = jnp.min(jnp.where(rest == v2, lane, LANE), axis=1, keepdims=True)
    w1 = 1.0 / (1.0 + jnp.exp(v2 - v1))
    gate_ref[...] = jnp.where(lane == i1, w1, 0.0) + jnp.where(lane == i2, 1.0 - w1, 0.0)

    member = jnp.where((lane == i1) | (lane == i2), 1.0, 0.0)
    carry = carry_ref[...]
    ranks = []
    for blk in range(tm // MOE_TOK_TILE):
        before_ref[blk] = carry
        mb = member[blk * MOE_TOK_TILE:(blk + 1) * MOE_TOK_TILE]
        ranks.append(_dot(ltri_ref[...], mb.astype(BF16)) + carry)
        carry = carry + jnp.sum(mb, axis=0, keepdims=True)
    pos = jnp.where(member > 0.0, jnp.concatenate(ranks, axis=0), -1.0)
    pos_ref[...] = pos
    post_ref[...] = pos.T[:SUBLANE, :]
    carry_ref[...] = carry
    total_ref[...] = carry


def moe_route(x, norm_gain, router):
    s, d = x.shape
    tm = min(MOE_ROUTE_TILES * MOE_TOK_TILE, s)
    nsub = tm // MOE_TOK_TILE
    nt = s // tm
    router_p = jnp.zeros((d, LANE), F32).at[:, :N_EXPERTS].set(router.astype(F32))
    tt = min(MOE_TOK_TILE, s)
    ltri = _const(np.tril(np.ones((tt, tt)), -1), BF16)
    row = pl.BlockSpec((tm, LANE), lambda i: (i, 0))
    return pl.pallas_call(
        _route_kernel,
        grid=(nt,),
        in_specs=[pl.BlockSpec((tm, d), lambda i: (i, 0)), _const_spec((1, d)), _const_spec((d, LANE)),
                  _const_spec((tt, tt))],
        out_specs=[row, row, pl.BlockSpec((SUBLANE, tm), lambda i: (0, i)),
                   pl.BlockSpec((nsub, 1, LANE), lambda i: (i, 0, 0)), pl.BlockSpec((1, LANE), lambda i: (0, 0))],
        out_shape=[jax.ShapeDtypeStruct((s, LANE), F32), jax.ShapeDtypeStruct((s, LANE), F32),
                   jax.ShapeDtypeStruct((SUBLANE, s), F32), jax.ShapeDtypeStruct((nt * nsub, 1, LANE), F32),
                   jax.ShapeDtypeStruct((1, LANE), F32)],
        scratch_shapes=[pltpu.VMEM((1, LANE), F32)],
        compiler_params=_params("arbitrary"),
        name="moe_route",
    )(x, norm_gain.reshape(1, d).astype(F32), router_p, ltri)


def _moe_ffn_kernel(be_ref, r0_ref, tlo_ref, thi_ref, nv_ref, h_ref, post_ref, wg_ref, wu_ref, wd_ref,
                    y_ref, x_sc, acc_sc, *, tt, nsub):
    b = pl.program_id(0)
    f = pl.program_id(1)
    nb = pl.num_programs(0)
    valid = b < nv_ref[0]
    tmb, d = acc_sc.shape
    gr = tmb // nsub
    last_tile = h_ref.shape[0] // tt - 1

    def picked(e, want, t):
        t0 = pl.multiple_of(t * tt, tt)
        p = post_ref[pl.ds(e, 1), pl.ds(t0, tt)]
        sel = jnp.where(p == want, 1.0, 0.0).astype(BF16)
        return _dot(sel, h_ref[pl.ds(t0, tt), :])

    def gather_head(blk, sb):
        e = be_ref[blk]
        lo = tlo_ref[blk * nsub + sb]
        hi = thi_ref[blk * nsub + sb]
        want = (lax.broadcasted_iota(jnp.int32, (gr, 1), 0) + (r0_ref[blk] + sb * gr)).astype(F32)
        rows = picked(e, want, lo)
        for k in range(1, MOE_GATHER_TILES):
            rows = rows + picked(e, jnp.where(lo + k <= hi, want, -2.0), jnp.minimum(lo + k, last_tile))
        return rows.astype(BF16)

    def gather_tail(slot, blk, sb):
        e = be_ref[blk]
        want = (lax.broadcasted_iota(jnp.int32, (gr, 1), 0) + (r0_ref[blk] + sb * gr)).astype(F32)
        rows = pl.ds(pl.multiple_of(sb * gr, gr), gr)

        def more(t, carry):
            x_sc[slot, rows, :] = (x_sc[slot, rows, :].astype(F32) + picked(e, want, t)).astype(BF16)
            return carry

        lax.fori_loop(tlo_ref[blk * nsub + sb] + MOE_GATHER_TILES, thi_ref[blk * nsub + sb] + 1, more, 0)

    @pl.when((b == 0) & (f == 0))
    def _():
        acc_sc[...] = jnp.zeros_like(acc_sc)
        for sb in range(nsub):
            x_sc[0, pl.ds(sb * gr, gr), :] = gather_head(0, sb)
            gather_tail(0, 0, sb)

    nxt = jnp.minimum(b + 1, nb - 1)
    nslot = (b + 1) % 2

    @pl.when(valid)
    def _():
        x_sc[nslot, pl.ds(pl.multiple_of(f * gr, gr), gr), :] = gather_head(nxt, f)
        xb = x_sc[b % 2]
        a = _silu(_dot(xb, wg_ref[0])) * _dot(xb, wu_ref[0])
        acc_sc[...] = jnp.where(f == 0, 0.0, acc_sc[...]) + _dot(a.astype(BF16), wd_ref[0].astype(BF16))

    @pl.when(valid & (thi_ref[nxt * nsub + f] - tlo_ref[nxt * nsub + f] >= MOE_GATHER_TILES))
    def _():
        gather_tail(nslot, nxt, f)

    @pl.when(f == nsub - 1)
    def _():
        y_ref[...] = jnp.where(valid, acc_sc[...], 0.0).astype(y_ref.dtype)


def moe_ffn(h, post, sched, w_gate, w_up, w_down):
    s, d = h.shape
    ne, _, dff = w_gate.shape
    tmb = MOE_ROW_BLOCK
    tt = min(MOE_TOK_TILE, s)
    nb = sched[0].shape[0]
    nf = tmb // MOE_GATHER_ROWS
    tf = dff // nf

    def fidx(b, f, nv):
        return jnp.where(b < nv[0], f, nf - 1)

    grid_spec = pltpu.PrefetchScalarGridSpec(
        num_scalar_prefetch=5,
        grid=(nb, nf),
        in_specs=[pl.BlockSpec((s, d), lambda b, f, *_: (0, 0), pipeline_mode=pl.Buffered(1)),
                  pl.BlockSpec((SUBLANE, s), lambda b, f, *_: (0, 0), pipeline_mode=pl.Buffered(1)),
                  pl.BlockSpec((1, d, tf), lambda b, f, be, r0, tlo, thi, nv: (be[b], 0, fidx(b, f, nv))),
                  pl.BlockSpec((1, d, tf), lambda b, f, be, r0, tlo, thi, nv: (be[b], 0, fidx(b, f, nv))),
                  pl.BlockSpec((1, tf, d), lambda b, f, be, r0, tlo, thi, nv: (be[b], fidx(b, f, nv), 0))],
        out_specs=pl.BlockSpec((tmb, d), lambda b, f, *_: (b, 0)),
        scratch_shapes=[pltpu.VMEM((2, tmb, d), BF16), pltpu.VMEM((tmb, d), F32)],
    )
    return pl.pallas_call(
        functools.partial(_moe_ffn_kernel, tt=tt, nsub=nf),
        grid_spec=grid_spec,
        out_shape=jax.ShapeDtypeStruct((nb * tmb, d), BF16),
        compiler_params=pltpu.CompilerParams(dimension_semantics=("arbitrary", "arbitrary"),
                                             vmem_limit_bytes=MOE_VMEM_LIMIT),
        name="moe_ffn",
    )(*sched, h, post, w_gate, w_up, w_down)


def _moe_combine_kernel(kb_ref, off_ref, lim_ref, x_ref, pos_ref, gate_ref, fg_ref, *rest):
    y_refs, o_ref, acc_sc = rest[:-2], rest[-2], rest[-1]
    t = pl.program_id(0)
    yb = y_refs[0].shape[0]
    pos = pos_ref[...]
    gate = gate_ref[...]
    col = lax.broadcasted_iota(jnp.int32, (1, yb), 1).astype(F32)

    def routed(e):
        pe = pos[:, e:e + 1]
        ge = gate[:, e:e + 1]
        r = jnp.where(pe >= 0.0, pe + off_ref[t * N_EXPERTS + e].astype(F32), -1.0)
        return r, ge

    def picked(e, k, r):
        sel = jnp.where(r == col + float(k * yb), 1.0, 0.0).astype(BF16)
        return _dot(sel, y_refs[MOE_Y_FETCH * e + k][...])

    col2 = lax.broadcasted_iota(jnp.int32, (1, 2 * yb), 1).astype(F32)
    acc = x_ref[...]
    for e in range(N_EXPERTS):
        r, ge = routed(e)
        pair = jnp.concatenate([y_refs[MOE_Y_FETCH * e][...], y_refs[MOE_Y_FETCH * e + 1][...]], axis=0)
        acc = acc + ge * _dot(jnp.where(r == col2, 1.0, 0.0).astype(BF16), pair)
    acc_sc[...] = acc
    for e in range(N_EXPERTS):
        for k in range(2, MOE_Y_FETCH):
            @pl.when(lim_ref[t * N_EXPERTS + e] > k * yb)
            def _(k=k, e=e):
                r, ge = routed(e)
                acc_sc[...] += ge * picked(e, k, r)
    o_ref[...] = _rms(acc_sc[...], fg_ref[...]).astype(o_ref.dtype)


def moe_combine(x, pos, gate, y, kb, off, lim, final_gain):
    s, d = x.shape
    tm = min(MOE_TOK_TILE, s)
    yb = MOE_Y_BLOCK
    last = y.shape[0] // yb - 1

    def yspec(e, k):
        def index(t, kb_r, off_r, lim_r):
            blk = jnp.minimum(kb_r[t * N_EXPERTS + e] + k, last)
            return (blk if k < 1 else jnp.where(lim_r[t * N_EXPERTS + e] > k * yb, blk, 0), 0)
        return pl.BlockSpec((yb, d), index)

    grid_spec = pltpu.PrefetchScalarGridSpec(
        num_scalar_prefetch=3,
        grid=(s // tm,),
        in_specs=[pl.BlockSpec((tm, d), lambda t, *_: (t, 0)),
                  pl.BlockSpec((tm, LANE), lambda t, *_: (t, 0)),
                  pl.BlockSpec((tm, LANE), lambda t, *_: (t, 0)),
                  pl.BlockSpec((1, d), lambda t, *_: (0, 0))]
        + [yspec(e, k) for e in range(N_EXPERTS) for k in range(MOE_Y_FETCH)],
        out_specs=pl.BlockSpec((tm, d), lambda t, *_: (t, 0)),
        scratch_shapes=[pltpu.VMEM((tm, d), F32)],
    )
    return pl.pallas_call(
        _moe_combine_kernel,
        grid_spec=grid_spec,
        out_shape=jax.ShapeDtypeStruct((s, d), F32),
        compiler_params=_params("arbitrary"),
        name="moe_combine",
    )(kb, off, lim, x, pos, gate, final_gain.reshape(1, d).astype(F32), *([y] * (MOE_Y_FETCH * N_EXPERTS)))


def _moe_schedule(before, total, s):
    tmb, yb, gr = MOE_ROW_BLOCK, MOE_Y_BLOCK, MOE_GATHER_ROWS
    nb = 2 * s // tmb + N_EXPERTS
    counts = total[0, :N_EXPERTS].astype(jnp.int32)
    nblk = (counts + tmb - 1) // tmb
    end = jnp.cumsum(nblk)
    first = end - nblk
    nvalid = end[-1]
    b = jnp.minimum(jnp.arange(nb, dtype=jnp.int32), nvalid - 1)
    blk_e = jnp.sum(b[:, None] >= end[None, :], axis=1).astype(jnp.int32)
    r0 = (b - first[blk_e]) * tmb
    cb = before[:, 0, :N_EXPERTS].astype(jnp.int32)
    r0s = (r0[:, None] + gr * jnp.arange(tmb // gr, dtype=jnp.int32)[None, :]).reshape(-1)
    cbe = cb[:, jnp.repeat(blk_e, tmb // gr)]
    tlo = (jnp.sum(cbe <= r0s[None, :], axis=0) - 1).astype(jnp.int32)
    thi = (jnp.sum(cbe < (r0s + gr)[None, :], axis=0) - 1).astype(jnp.int32)
    sched = (blk_e, r0.astype(jnp.int32), tlo, thi, nvalid.reshape(1).astype(jnp.int32))
    row_start = first[None, :] * tmb + cb
    kb = row_start // yb
    off = first[None, :] * tmb - kb * yb
    n_te = jnp.concatenate([cb[1:], counts[None, :]], axis=0) - cb
    lim = row_start - kb * yb + n_te
    flat = lambda a: a.reshape(-1).astype(jnp.int32)
    return sched, flat(kb), flat(off), flat(lim)


def moe_sparse(h, x, norm_gain, router, w_gate, w_up, w_down, final_gain):
    s, _ = x.shape
    pos, gate, post, before, total = moe_route(x, norm_gain, router)
    sched, kb, off, lim = _moe_schedule(before, total, s)
    y = moe_ffn(h, post, sched, w_gate, w_up, w_down)
    return moe_combine(x, pos, gate, y, kb, off, lim, final_gain)


FOX_F0 = 7 * BRANCH_W


def _split_w_in_kernel(wt_ref, mix_ref, f_ref, gate_ref):
    rest0 = FOX_F0 + N_HEADS
    gate0 = N_MIX_COLS + N_HEADS
    mix_ref[:, :FOX_F0] = wt_ref[pl.ds(0, FOX_F0), :].T.astype(BF16)
    mix_ref[:, FOX_F0:] = wt_ref[pl.ds(rest0, N_MIX_COLS - FOX_F0), :].T.astype(BF16)
    gate_ref[...] = wt_ref[pl.ds(gate0, N_BRANCH * D_MODEL), :].T.astype(BF16)
    f_rows = jnp.concatenate([wt_ref[pl.ds(FOX_F0, N_HEADS), :], jnp.zeros((LANE - N_HEADS, LANE), F32)], axis=0)
    f_ref[...] = f_rows.T.astype(BF16)


def _mixer_weights(w_in, layer, tr=LANE):
    _, d, cols = w_in.shape
    w_t = jnp.swapaxes(w_in, 1, 2)
    row = lambda w: pl.BlockSpec((tr, w), lambda i: (i, 0))
    return pl.pallas_call(
        _split_w_in_kernel,
        grid=(d // tr,),
        in_specs=[pl.BlockSpec((None, cols, tr), lambda i: (layer, 0, i))],
        out_specs=[row(N_MIX_COLS), row(LANE), row(N_BRANCH * D_MODEL)],
        out_shape=[jax.ShapeDtypeStruct((d, N_MIX_COLS), BF16), jax.ShapeDtypeStruct((d, LANE), BF16),
                   jax.ShapeDtypeStruct((d, N_BRANCH * D_MODEL), BF16)],
        compiler_params=_params("parallel"),
        name="split_w_in",
    )(w_t)


def kernel(x, w_in, w_branch, w_out, norm_mix_g, hgrn_lb_logits, hgrn_norm_g, fox_f_bias, pool_w, pool_scale,
           ret_gn_g, ret_gn_b, norm_ffn_g, ffn_w_gate, ffn_w_up, ffn_w_down, moe_router, moe_w_gate, moe_w_up,
           moe_w_down, final_norm_g):
    b, s, d = x.shape
    assert b == 1 and d == D_MODEL
    depth = w_in.shape[0]
    assert depth == 2, "layer 0 uses the dense FFN, layer 1 the experts and the final norm"
    xs = x.reshape(s, d)
    h = None
    out = None
    for layer in range(depth):
        w_mix, w_f, w_gate = _mixer_weights(w_in, layer)
        if layer == 0:
            proj, h = norm_matmul(xs, norm_mix_g[0], w_mix)
        else:
            proj = matmul(h, w_mix)
        ct, c_rows, fox_stats = fox_gate(h, proj, w_f, fox_f_bias[layer])
        branches = (
            hgrn2(proj, hgrn_lb_logits, hgrn_norm_g[layer], layer),
            fox_attention(proj, ct, c_rows, fox_stats),
            pool_mixer(proj, pool_w[layer], pool_scale[layer]),
            retention(proj, ret_gn_g[layer], ret_gn_b[layer]),
        )
        li = layer // 2
        if layer % 2 == 0:
            xs, h2, (ffn_wg, ffn_wu) = merge(h, branches, xs, w_gate, w_branch[layer].astype(BF16),
                                             w_out[layer].astype(BF16), norm_ffn_g[layer],
                                             cast=(ffn_w_gate[li], ffn_w_up[li]))
            xs, h = ffn_dense(h2, xs, ffn_wg, ffn_wu, ffn_w_down[li], norm_mix_g[layer + 1])
        else:
            xs, h2, (moe_wg, moe_wu) = merge(h, branches, xs, w_gate, w_branch[layer].astype(BF16),
                                             w_out[layer].astype(BF16), norm_ffn_g[layer], tm=MERGE_CAST_TILE,
                                             cast=(moe_w_gate[li], moe_w_up[li]))
            out = moe_sparse(h2, xs, norm_ffn_g[layer], moe_router[li], moe_wg, moe_wu, moe_w_down[li],
                             final_norm_g)
    return out.reshape(b, s, d)
```

```python
import functools
import math

import jax
import jax.numpy as jnp
import numpy as np
from jax import lax
from jax.experimental import pallas as pl
from jax.experimental.pallas import tpu as pltpu

D_MODEL = 1024
N_BRANCH = 4
BRANCH_W = D_MODEL // N_BRANCH
HEAD_DIM = 64
N_HEADS = BRANCH_W // HEAD_DIM
POOL_WINDOWS = (2, 4, 8, 16)
POOL_GROUP = BRANCH_W // len(POOL_WINDOWS)
POOL_HALO = 16
RET_DECAY_BASE = 5.0
ROPE_BASE = 10000.0
N_EXPERTS = 8
RMS_EPS = 1e-6
LN_EPS = 1e-5
N_MIX_COLS = 12 * BRANCH_W

LANE = 128
SUBLANE = 8
VMEM_LIMIT = 56 * 1024 * 1024

HG_CHUNK = 64
HG_SUB = 16
HG_FAST_MIN_LOGDECAY = -60.0
RET_CHUNK = 256

F32 = jnp.float32
BF16 = jnp.bfloat16
NT_DIMS = (((1,), (1,)), ((), ()))


def _params(*sem):
    return pltpu.CompilerParams(dimension_semantics=sem, vmem_limit_bytes=VMEM_LIMIT)


def _const_spec(shape):
    nd = len(shape)
    return pl.BlockSpec(shape, lambda *_: (0,) * nd, pipeline_mode=pl.Buffered(1))


def _split3(x):
    hi = x.astype(BF16)
    r1 = x - hi.astype(F32)
    mid = r1.astype(BF16)
    lo = (r1 - mid.astype(F32)).astype(BF16)
    return hi, mid, lo


def _dot(a, b):
    return jnp.dot(a, b, preferred_element_type=F32)


def _dot_nt(a, b):
    return lax.dot_general(a, b, NT_DIMS, preferred_element_type=F32)


def _dot_exact_rhs(x, m_bf16, terms=3):
    return sum(_dot(part, m_bf16) for part in _split3(x)[:terms])


def _dot_exact_lhs(m_bf16, x, terms=3):
    return sum(_dot(m_bf16, part) for part in _split3(x)[:terms])


def _sigmoid(x):
    return 1.0 / (1.0 + jnp.exp(-x))


def _silu(x):
    return x * _sigmoid(x)


def _rms(x, gain):
    return x * lax.rsqrt(jnp.mean(x * x, axis=-1, keepdims=True) + RMS_EPS) * gain


def _const(a, dtype=F32):
    return jnp.asarray(np.asarray(a, np.float32), dtype)


def _head_of(n):
    return np.arange(n) // HEAD_DIM


def _head_ones():
    h = _head_of(BRANCH_W)
    return _const(h[:, None] == h[None, :], BF16)


def _head_masks():
    return _const(_head_of(BRANCH_W)[None, :] == np.arange(N_HEADS)[:, None])


def _norm_matmul_kernel(x_ref, g_ref, b_ref, o_ref, h_ref):
    @pl.when(pl.program_id(1) == 0)
    def _():
        h_ref[...] = _rms(x_ref[...], g_ref[...]).astype(h_ref.dtype)

    o_ref[...] = _dot(h_ref[...], b_ref[...]).astype(o_ref.dtype)


def norm_matmul(x, gain, b, tm=2048, tn=1024):
    m, k = x.shape
    _, n = b.shape
    tm = min(tm, m)
    return pl.pallas_call(
        _norm_matmul_kernel,
        grid=(m // tm, n // tn),
        in_specs=[pl.BlockSpec((tm, k), lambda i, j: (i, 0)), _const_spec((1, k)),
                  pl.BlockSpec((k, tn), lambda i, j: (0, j))],
        out_specs=[pl.BlockSpec((tm, tn), lambda i, j: (i, j)), pl.BlockSpec((tm, k), lambda i, j: (i, 0))],
        out_shape=[jax.ShapeDtypeStruct((m, n), BF16), jax.ShapeDtypeStruct((m, k), BF16)],
        compiler_params=_params("arbitrary", "arbitrary"),
        name="in_proj_norm",
    )(x, gain.reshape(1, k).astype(F32), b)


def _matmul_kernel(a_ref, b_ref, o_ref):
    o_ref[...] = _dot(a_ref[...], b_ref[...]).astype(o_ref.dtype)


def matmul(a, b, out_dtype=BF16, tm=2048, tn=1024):
    m, k = a.shape
    _, n = b.shape
    tm = min(tm, m)
    return pl.pallas_call(
        _matmul_kernel,
        grid=(n // tn, m // tm),
        in_specs=[pl.BlockSpec((tm, k), lambda j, i: (i, 0)),
                  pl.BlockSpec((k, tn), lambda j, i: (0, j))],
        out_specs=pl.BlockSpec((tm, tn), lambda j, i: (i, j)),
        out_shape=jax.ShapeDtypeStruct((m, n), out_dtype),
        compiler_params=_params("parallel", "parallel"),
        name="in_proj",
    )(a, b)


def _hgrn_kernel(q_ref, f_ref, i_ref, g_ref, lbl_ref, ng_ref, ones_ref, hm_ref, tril_ref, halfsum_ref, fmask_ref,
                 o_ref, st_ref, bpad, kpad, vpad, astack, lf_sc, kk_sc, o_sc, *, layer, tile):
    c, sub = HG_CHUNK, HG_SUB
    nsub = c // sub
    half = c // 2

    @pl.when(pl.program_id(0) == 0)
    def _():
        st_ref[...] = jnp.zeros_like(st_ref)
        bpad[...] = jnp.zeros_like(bpad)
        kpad[...] = jnp.zeros_like(kpad)
        vpad[...] = jnp.zeros_like(vpad)

    lbl = lbl_ref[...]
    e = jnp.exp(lbl - jnp.max(lbl, axis=0, keepdims=True))
    p = e / jnp.sum(e, axis=0, keepdims=True)
    lb = jnp.zeros((1, BRANCH_W), F32)
    for l in range(1, layer + 1):
        lb = lb + p[l:l + 1, :]

    ones_bd = ones_ref[...]
    hm = hm_ref[...]
    tril = tril_ref[...]
    row = lax.broadcasted_iota(jnp.int32, (c, 1), 0)
    row_in_sub = row % sub
    bd_mask = ones_bd.astype(F32)

    sig = _sigmoid(f_ref[...].astype(F32))
    logf_all = jnp.log(lb + (1.0 - lb) * sig)
    lf_sc[...] = _dot_exact_lhs(tril, logf_all, terms=2)
    kk_sc[...] = (1.0 - lb) * (1.0 - sig)
    min_decay = jnp.min(_dot(halfsum_ref[...], logf_all.astype(BF16)))

    def load(ci):
        r0 = pl.multiple_of(ci * c, c)
        q = q_ref[pl.ds(r0, c), :].astype(F32)
        v = i_ref[pl.ds(r0, c), :].astype(F32)
        kk = kk_sc[pl.ds(r0, c), :]
        b = lf_sc[pl.ds(r0, c), :]
        return r0, q, v, kk, b

    def finish(r0, q_decayed, v, kk, b, intra):
        st = st_ref[...]
        inter = _dot_nt(q_decayed.astype(BF16), st.astype(BF16))
        b_last = b[c - 1:c, :]
        ks_end = (kk * jnp.exp(b_last - b)).astype(BF16)
        upd = _dot(v.T.astype(BF16), ks_end)
        st_ref[...] = st * jnp.exp(b_last) + upd * bd_mask
        o_sc[pl.ds(r0, c), :] = intra + inter

    def fast_chunk(ci, carry):
        r0, q, v, kk, b = load(ci)
        second = row >= half
        m_row = b[half - 1:half, :]
        mref = jnp.where(second, m_row, 0.0)
        qp = q * jnp.exp(b - mref)
        kp = kk * jnp.exp(mref - b)
        e_m = jnp.exp(m_row)
        kaug = jnp.concatenate([kp, kp[:half, :] * e_m], axis=0)
        vaug = jnp.concatenate([v, v[:half, :]], axis=0)
        k_heads = jnp.concatenate([kaug * hm[h:h + 1, :] for h in range(N_HEADS)], axis=0).astype(BF16)
        v_heads = jnp.concatenate([vaug * hm[h:h + 1, :] for h in range(N_HEADS)], axis=0).astype(BF16)
        sc = jnp.where(fmask_ref[...] > 0.0, _dot_nt(qp.astype(BF16), k_heads), 0.0)
        intra = _dot(sc.astype(BF16), v_heads)
        finish(r0, jnp.where(second, qp * e_m, qp), v, kk, b, intra)
        return carry

    def exact_chunk(ci, carry):
        r0, q, v, kk, b = load(ci)

        bpad[pl.ds(sub, c), :] = b
        kpad[pl.ds(sub, c), :] = kk
        vpad[pl.ds(sub, c), :] = v

        for d in range(sub):
            b_d = bpad[pl.ds(sub - d, c), :]
            k_d = kpad[pl.ds(sub - d, c), :]
            a = jnp.where(row_in_sub >= d, q * k_d * jnp.exp(b - b_d), 0.0)
            astack[pl.ds(d * c, c), :] = a.astype(BF16)
        pall = _dot(astack[...], ones_bd)
        intra = jnp.zeros((c, BRANCH_W), F32)
        for d in range(sub):
            intra = intra + pall[d * c:(d + 1) * c, :] * vpad[pl.ds(sub - d, c), :]

        pieces = [jnp.zeros((sub, BRANCH_W), F32)]
        for si in range(1, nsub):
            lo = si * sub
            m_i = b[lo - 1:lo, :]
            qs = q[lo:lo + sub, :] * jnp.exp(b[lo:lo + sub, :] - m_i)
            ks = (kk[:lo, :] * jnp.exp(m_i - b[:lo, :])).astype(BF16)
            qx = jnp.concatenate([qs * hm[h:h + 1, :] for h in range(N_HEADS)], axis=0).astype(BF16)
            sc = _dot_nt(qx, ks)
            r = _dot(sc.astype(BF16), v[:lo, :].astype(BF16))
            acc = jnp.zeros((sub, BRANCH_W), F32)
            for h in range(N_HEADS):
                acc = acc + r[h * sub:(h + 1) * sub, :] * hm[h:h + 1, :]
            pieces.append(acc)
        intra = intra + jnp.concatenate(pieces, axis=0)
        finish(r0, q * jnp.exp(b), v, kk, b, intra)
        return carry

    lax.cond(min_decay >= HG_FAST_MIN_LOGDECAY,
             lambda: lax.fori_loop(0, tile // c, fast_chunk, 0, unroll=2),
             lambda: lax.fori_loop(0, tile // c, exact_chunk, 0))

    o = o_sc[...]
    ms = _dot_exact_rhs(o * o, ones_bd, terms=2) * (1.0 / HEAD_DIM)
    y = o * lax.rsqrt(ms + RMS_EPS) * ng_ref[...] * _silu(g_ref[...].astype(F32))
    o_ref[...] = y.astype(o_ref.dtype)


def hgrn2(proj, lb_logits, norm_g, layer, tile=512):
    s = proj.shape[0]
    depth = lb_logits.shape[0]
    c, sub = HG_CHUNK, HG_SUB
    half = c // 2
    tile = min(tile, s)
    col = lambda j: pl.BlockSpec((tile, BRANCH_W), lambda i, j=j: (i, j))
    pos = np.arange(tile)
    tril = _const((pos[:, None] // c == pos[None, :] // c) & (pos[None, :] <= pos[:, None]), BF16)
    nhalf = tile // half
    halfsum = _const(np.arange(tile)[None, :] // half == np.arange(nhalf)[:, None], BF16)
    t = np.arange(c)[:, None]
    col_s = np.arange(c + half)[None, :]
    same_half = (col_s < c) & (col_s // half == t // half) & (col_s <= t)
    cross = (col_s >= c) & (t >= half)
    fmask = _const(np.tile(same_half | cross, (1, N_HEADS)))
    return pl.pallas_call(
        functools.partial(_hgrn_kernel, layer=layer, tile=tile),
        grid=(s // tile,),
        in_specs=[col(0), col(1), col(2), col(3),
                  _const_spec((depth, BRANCH_W)), _const_spec((1, BRANCH_W)),
                  _const_spec((BRANCH_W, BRANCH_W)), _const_spec((N_HEADS, BRANCH_W)),
                  _const_spec((tile, tile)), _const_spec((nhalf, tile)),
                  _const_spec((c, N_HEADS * (c + half)))],
        out_specs=pl.BlockSpec((tile, BRANCH_W), lambda i: (i, 0)),
        out_shape=jax.ShapeDtypeStruct((s, BRANCH_W), BF16),
        scratch_shapes=[pltpu.VMEM((BRANCH_W, BRANCH_W), F32),
                        pltpu.VMEM((c + sub, BRANCH_W), F32),
                        pltpu.VMEM((c + sub, BRANCH_W), F32),
                        pltpu.VMEM((c + sub, BRANCH_W), F32),
                        pltpu.VMEM((sub * c, BRANCH_W), BF16),
                        pltpu.VMEM((tile, BRANCH_W), F32),
                        pltpu.VMEM((tile, BRANCH_W), F32),
                        pltpu.VMEM((tile, BRANCH_W), F32)],
        compiler_params=_params("arbitrary"),
        name="hgrn2",
    )(proj, proj, proj, proj, lb_logits.astype(F32), norm_g.reshape(1, BRANCH_W).astype(F32),
      _head_ones(), _head_masks(), tril, halfsum, fmask)


FOX_TILE = 256
FOX_GATE_BLOCKS = 2
FOX_NSTAT = 16
FOX_FIXED_MAX = 30.0
FOX_SKIP_LOG = 30.0


def _fox_gate_kernel(h_ref, q_ref, k_ref, wf_ref, bias_ref, tril_ref, ones_ref,
                     ct_ref, c_ref, stat_ref, carry_ref, kmax_ref):
    @pl.when(pl.program_id(0) == 0)
    def _():
        carry_ref[...] = jnp.zeros_like(carry_ref)
        kmax_ref[...] = jnp.zeros_like(kmax_ref)

    logit = _dot(h_ref[...], wf_ref[...]) + bias_ref[...]
    logf = jnp.minimum(logit, 0.0) - jnp.log(1.0 + jnp.exp(-jnp.abs(logit)))
    carry = carry_ref[...]
    parts = []
    for blk in range(logf.shape[0] // FOX_TILE):
        part = _dot_exact_lhs(tril_ref[...], logf[blk * FOX_TILE:(blk + 1) * FOX_TILE]) + carry
        carry = part[-1:, :]
        parts.append(part)
    cum = jnp.concatenate(parts, axis=0)
    carry_ref[...] = carry
    ct_ref[...] = cum.T[:SUBLANE, :]
    c_ref[...] = cum

    ones_bd = ones_ref[...]
    q = q_ref[...].astype(F32)
    k = k_ref[...].astype(F32)
    scale = HEAD_DIM ** -0.5
    head_lane = lax.broadcasted_iota(jnp.int32, (1, BRANCH_W), 1) // HEAD_DIM
    c_heads = jnp.zeros(q.shape, F32)
    for h in range(N_HEADS):
        c_heads = jnp.where(head_lane == h, cum[:, h:h + 1], c_heads)
    slack = 1.0 + 2.0 ** -6
    qn = jnp.sqrt(_dot((q * q).astype(BF16), ones_bd)) * (scale * slack)
    kn = jnp.sqrt(_dot((k * k).astype(BF16), ones_bd)) * slack
    diag = _dot((q * k).astype(BF16), ones_bd) * scale - (2.0 ** -6) * qn * kn
    e_row = c_heads - diag
    kmax = kmax_ref[...]
    for blk in range(q.shape[0] // FOX_TILE):
        rs = slice(blk * FOX_TILE, (blk + 1) * FOX_TILE)
        kmax = jnp.maximum(kmax, jnp.max(kn[rs], axis=0, keepdims=True))
        rows = [jnp.max(qn[rs], axis=0, keepdims=True),
                jnp.max(e_row[rs], axis=0, keepdims=True),
                kmax,
                c_heads[(blk + 1) * FOX_TILE - 1:(blk + 1) * FOX_TILE, :]]
        stat_ref[blk] = jnp.concatenate(rows + [jnp.zeros((SUBLANE - len(rows), BRANCH_W), F32)], axis=0)
    kmax_ref[...] = kmax


def fox_gate(h, proj, w_f, f_bias):
    s, d = h.shape
    tile = min(FOX_GATE_BLOCKS * FOX_TILE, s)
    nblk = tile // FOX_TILE
    bias = jnp.zeros((1, LANE), F32).at[0, :N_HEADS].set(f_bias.astype(F32))
    tril = _const(np.tril(np.ones((FOX_TILE, FOX_TILE))), BF16)
    ct, c_rows, stats = pl.pallas_call(
        _fox_gate_kernel,
        grid=(s // tile,),
        in_specs=[pl.BlockSpec((tile, d), lambda i: (i, 0)),
                  pl.BlockSpec((tile, BRANCH_W), lambda i: (i, 4)),
                  pl.BlockSpec((tile, BRANCH_W), lambda i: (i, 5)),
                  _const_spec((d, LANE)), _const_spec((1, LANE)), _const_spec((FOX_TILE, FOX_TILE)),
                  _const_spec((BRANCH_W, BRANCH_W))],
        out_specs=[pl.BlockSpec((SUBLANE, tile), lambda i: (0, i)),
                   pl.BlockSpec((tile, LANE), lambda i: (i, 0)),
                   pl.BlockSpec((nblk, SUBLANE, BRANCH_W), lambda i: (i, 0, 0))],
        out_shape=[jax.ShapeDtypeStruct((SUBLANE, s), F32),
                   jax.ShapeDtypeStruct((s, LANE), F32),
                   jax.ShapeDtypeStruct((s // FOX_TILE, SUBLANE, BRANCH_W), F32)],
        scratch_shapes=[pltpu.VMEM((1, LANE), F32), pltpu.VMEM((1, BRANCH_W), F32)],
        compiler_params=_params("arbitrary"),
        name="fox_gate",
    )(h, proj, proj, w_f, bias, tril, _head_ones())
    return ct, c_rows, stats[:, :4, ::HEAD_DIM].reshape(-1)


def _fox_kernel(stat_ref, q_ref, k_ref, v_ref, ct_ref, c_ref, hm_ref, o_ref, m_sc, l_sc, acc_sc, *, tq):
    i = pl.program_id(0)
    q0 = pl.multiple_of(i * tq, tq)
    hm = hm_ref[...]
    q = q_ref[...].astype(F32) * (HEAD_DIM ** -0.5)
    qh = [(q * hm[h:h + 1, :]).astype(BF16) for h in range(N_HEADS)]
    c_q0 = ct_ref[:, pl.ds(q0, tq)][:, 0:1]

    first = i
    for h in range(N_HEADS):
        qmax = stat_ref[i * FOX_NSTAT + h]
        emax = stat_ref[i * FOX_NSTAT + N_HEADS + h]

        def needed(j, h=h, qmax=qmax, emax=emax):
            jc = jnp.maximum(j, 0)
            bound = (qmax * stat_ref[jc * FOX_NSTAT + 2 * N_HEADS + h] + emax
                     - stat_ref[jc * FOX_NSTAT + 3 * N_HEADS + h])
            return (j >= 0) & (bound >= -FOX_SKIP_LOG)

        last_dropped = lax.while_loop(needed, lambda j: j - 1, i - 1)
        first = jnp.minimum(first, last_dropped + 1)

    l_sc[...] = jnp.zeros_like(l_sc)
    acc_sc[...] = jnp.zeros_like(acc_sc)

    def causal(sc):
        r = lax.broadcasted_iota(jnp.int32, (tq, tq), 0)
        cidx = lax.broadcasted_iota(jnp.int32, (tq, tq), 1)
        return jnp.where(cidx <= r, sc, -jnp.inf)

    def online_block(s0, diagonal):
        kb = k_ref[pl.ds(s0, tq), :]
        vb = v_ref[pl.ds(s0, tq), :]
        bias = c_q0 - ct_ref[:, pl.ds(s0, tq)]
        for h in range(N_HEADS):
            sc = _dot_nt(qh[h], kb) + bias[h:h + 1, :]
            if diagonal:
                sc = causal(sc)
            m_prev = m_sc[h]
            m_new = jnp.maximum(m_prev, jnp.max(sc, axis=1, keepdims=True))
            alpha = jnp.exp(m_prev - m_new)
            p = jnp.exp(sc - jnp.tile(m_new, (1, tq // LANE)))
            l_sc[h] = alpha * l_sc[h] + jnp.sum(p, axis=1, keepdims=True)
            acc_sc[h] = acc_sc[h] * jnp.tile(alpha, (1, BRANCH_W // LANE)) + _dot(p.astype(BF16), vb)
            m_sc[h] = m_new

    tops = [stat_ref[i * FOX_NSTAT + h] * stat_ref[i * FOX_NSTAT + 2 * N_HEADS + h] for h in range(N_HEADS)]
    c_tile = c_ref[...]
    shift = [c_tile[:, h:h + 1] - c_q0[h:h + 1, :] - tops[h] for h in range(N_HEADS)]

    def fixed_block(s0, diagonal):
        kb = k_ref[pl.ds(s0, tq), :]
        vb = v_ref[pl.ds(s0, tq), :]
        bias = c_q0 - ct_ref[:, pl.ds(s0, tq)]
        for h in range(N_HEADS):
            sc = _dot_nt(qh[h], kb) + bias[h:h + 1, :] + shift[h]
            if diagonal:
                sc = causal(sc)
            p = jnp.exp(sc)
            l_sc[h] += p[:, :LANE] + p[:, LANE:]
            acc_sc[h] += _dot(p.astype(BF16), vb)

    def run(block, row_sum):
        lax.fori_loop(first, i, lambda j, carry: (block(pl.multiple_of(j * tq, tq), False), carry)[1], 0)
        block(q0, True)
        out = jnp.zeros((tq, BRANCH_W), F32)
        for h in range(N_HEADS):
            out = out + acc_sc[h] * hm[h:h + 1, :] / row_sum(l_sc[h])
        o_ref[...] = out.astype(o_ref.dtype)

    def run_online():
        m_sc[...] = jnp.full_like(m_sc, -jnp.inf)
        run(online_block, lambda l: jnp.tile(l, (1, BRANCH_W // LANE)))

    def run_fixed():
        run(fixed_block, lambda l: jnp.sum(l, axis=1, keepdims=True))

    lax.cond(functools.reduce(jnp.maximum, tops) <= FOX_FIXED_MAX, run_fixed, run_online)


def fox_attention(proj, ct, c_rows, stats):
    s = proj.shape[0]
    tq = min(FOX_TILE, s)
    full = lambda j: pl.BlockSpec((s, BRANCH_W), lambda i, j=j: (0, j), pipeline_mode=pl.Buffered(1))
    return pl.pallas_call(
        functools.partial(_fox_kernel, tq=tq),
        grid=(s // tq,),
        in_specs=[pl.BlockSpec(memory_space=pltpu.SMEM),
                  pl.BlockSpec((tq, BRANCH_W), lambda i: (i, 4)), full(5), full(6),
                  _const_spec((SUBLANE, s)), pl.BlockSpec((tq, LANE), lambda i: (i, 0)),
                  _const_spec((N_HEADS, BRANCH_W))],
        out_specs=pl.BlockSpec((tq, BRANCH_W), lambda i: (i, 0)),
        out_shape=jax.ShapeDtypeStruct((s, BRANCH_W), BF16),
        scratch_shapes=[pltpu.VMEM((N_HEADS, tq, LANE), F32),
                        pltpu.VMEM((N_HEADS, tq, LANE), F32),
                        pltpu.VMEM((N_HEADS, tq, BRANCH_W), F32)],
        compiler_params=_params("parallel"),
        name="fox_attention",
    )(stats, proj, proj, proj, ct, c_rows, _head_masks())


def _pool_kernel(u_ref, w_ref, scale_ref, o_ref, ext, *, tile):
    i = pl.program_id(0)

    @pl.when(i == 0)
    def _():
        ext[pl.ds(0, POOL_HALO), :] = jnp.zeros((POOL_HALO, BRANCH_W), F32)

    u = u_ref[...].astype(F32)
    ext[pl.ds(POOL_HALO, tile), :] = u
    pos = (i * tile + lax.broadcasted_iota(jnp.int32, (tile, 1), 0) + 1).astype(F32)
    halves = []
    for half in range(BRANCH_W // LANE):
        lanes = pl.ds(half * LANE, LANE)
        w_small, w_big = POOL_WINDOWS[2 * half], POOL_WINDOWS[2 * half + 1]
        run = u[:, half * LANE:(half + 1) * LANE]
        sums = {}
        for j in range(1, w_big):
            if j == w_small:
                sums[w_small] = run
            run = run + ext[pl.ds(POOL_HALO - j, tile), lanes]
        sums[w_big] = run
        lane = lax.broadcasted_iota(jnp.int32, (1, LANE), 1)
        small = lane < POOL_GROUP
        total = jnp.where(small, sums[w_small], sums[w_big])
        count = jnp.where(small, jnp.minimum(pos, float(w_small)), jnp.minimum(pos, float(w_big)))
        halves.append(total / count)
    mean = jnp.concatenate(halves, axis=1)
    d = (mean - u).astype(BF16)
    y = _dot(d, w_ref[...]) * scale_ref[...]
    o_ref[...] = y.astype(o_ref.dtype)
    ext[pl.ds(0, POOL_HALO), :] = u[tile - POOL_HALO:, :]


def pool_mixer(proj, w_pool, scale, tile=2048):
    s = proj.shape[0]
    tile = min(tile, s)
    ng = len(POOL_WINDOWS)
    w_bd = jnp.zeros((BRANCH_W, BRANCH_W), F32)
    for gi in range(ng):
        lo = gi * POOL_GROUP
        w_bd = w_bd.at[lo:lo + POOL_GROUP, lo:lo + POOL_GROUP].set(w_pool[gi].astype(F32))
    return pl.pallas_call(
        functools.partial(_pool_kernel, tile=tile),
        grid=(s // tile,),
        in_specs=[pl.BlockSpec((tile, BRANCH_W), lambda i: (i, 7)),
                  _const_spec((BRANCH_W, BRANCH_W)), _const_spec((1, BRANCH_W))],
        out_specs=pl.BlockSpec((tile, BRANCH_W), lambda i: (i, 0)),
        out_shape=jax.ShapeDtypeStruct((s, BRANCH_W), BF16),
        scratch_shapes=[pltpu.VMEM((tile + POOL_HALO, BRANCH_W), F32)],
        compiler_params=_params("arbitrary"),
        name="pool_mixer",
    )(proj, w_bd.astype(BF16), scale.reshape(1, BRANCH_W).astype(F32))


def _ret_kernel(q_ref, k_ref, v_ref, g_ref, cos_ref, sin_ref, perm_ref, ones_ref, hm_ref,
                dstack_ref, xi_ref, zeta_ref, gc_ref, gng_ref, gnb_ref, o_ref, st_ref, *, tile):
    c = RET_CHUNK

    @pl.when(pl.program_id(0) == 0)
    def _():
        st_ref[...] = jnp.zeros_like(st_ref)

    perm = perm_ref[...]
    ones_bd = ones_ref[...]
    bd_mask = ones_bd.astype(F32)
    hm = hm_ref[...]

    cos = cos_ref[...]
    sin = sin_ref[...]
    q_all = q_ref[...]
    k_all = k_ref[...]
    qr_all = q_all.astype(F32) * cos + _dot(q_all, perm) * sin
    kr_all = (k_all.astype(F32) * cos + _dot(k_all, perm) * sin) * (HEAD_DIM ** -0.5)

    outs = []
    for ci in range(tile // c):
        r0 = ci * c
        qr = qr_all[r0:r0 + c, :]
        kr = kr_all[r0:r0 + c, :]
        v = v_ref[pl.ds(r0, c), :]

        qx = jnp.concatenate([qr * hm[h:h + 1, :] for h in range(N_HEADS)], axis=0).astype(BF16)
        sc = _dot_nt(qx, kr.astype(BF16)) * dstack_ref[...]
        r = _dot(sc.astype(BF16), v)
        intra = jnp.zeros((c, BRANCH_W), F32)
        for h in range(N_HEADS):
            intra = intra + r[h * c:(h + 1) * c, :] * hm[h:h + 1, :]

        st = st_ref[...]
        inter = _dot_nt((qr * xi_ref[...]).astype(BF16), st.astype(BF16))
        upd = _dot(v.astype(F32).T.astype(BF16), (kr * zeta_ref[...]).astype(BF16))
        st_ref[...] = st * gc_ref[...] + upd * bd_mask
        outs.append(intra + inter)

    o = jnp.concatenate(outs, axis=0)
    mu = _dot_exact_rhs(o, ones_bd, terms=2) * (1.0 / HEAD_DIM)
    cen = o - mu
    var = _dot_exact_rhs(cen * cen, ones_bd, terms=2) * (1.0 / HEAD_DIM)
    y = cen * lax.rsqrt(var + LN_EPS) * gng_ref[...] + gnb_ref[...]
    o_ref[...] = (y * _silu(g_ref[...].astype(F32))).astype(o_ref.dtype)


def _rope_tables(s):
    half = HEAD_DIM // 2
    pos = np.arange(s, dtype=np.float64)
    inv_freq = ROPE_BASE ** (-np.arange(half, dtype=np.float64) / half)
    lane = np.arange(BRANCH_W)
    ang = pos[:, None] * inv_freq[lane % half][None, :]
    sign = np.where(lane % HEAD_DIM < half, -1.0, 1.0)
    return _const(np.cos(ang)), _const(np.sin(ang) * sign[None, :])


def _ret_constants():
    c = RET_CHUNK
    half = HEAD_DIM // 2
    lane = np.arange(BRANCH_W)
    partner = np.where(lane % HEAD_DIM < half, lane + half, lane - half)
    perm = lane[:, None] == partner[None, :]
    log_gamma = np.log1p(-np.exp2(-RET_DECAY_BASE - np.arange(N_HEADS, dtype=np.float64)))
    ci = np.arange(c, dtype=np.float64)
    diff = ci[:, None] - ci[None, :]
    intra = np.where(diff >= 0, np.exp(diff * log_gamma[:, None, None]), 0.0)
    dstack = intra.reshape(N_HEADS * c, c)
    lg_lane = np.repeat(log_gamma, HEAD_DIM)[None, :]
    xi = np.exp((ci[:, None] + 1.0) * lg_lane)
    zeta = np.exp((c - 1.0 - ci[:, None]) * lg_lane)
    gc = np.exp(c * lg_lane)
    return _const(perm, BF16), _const(dstack), _const(xi), _const(zeta), _const(gc)


def retention(proj, gn_g, gn_b, tile=1024):
    s = proj.shape[0]
    c = RET_CHUNK
    tile = min(tile, s)
    cos_t, sin_t = _rope_tables(s)
    perm, dstack, xi, zeta, gc = _ret_constants()
    col = lambda j: pl.BlockSpec((tile, BRANCH_W), lambda i, j=j: (i, j))
    row = pl.BlockSpec((tile, BRANCH_W), lambda i: (i, 0))
    return pl.pallas_call(
        functools.partial(_ret_kernel, tile=tile),
        grid=(s // tile,),
        in_specs=[col(8), col(9), col(10), col(11), row, row,
                  _const_spec((BRANCH_W, BRANCH_W)), _const_spec((BRANCH_W, BRANCH_W)),
                  _const_spec((N_HEADS, BRANCH_W)), _const_spec((N_HEADS * c, c)),
                  _const_spec((c, BRANCH_W)), _const_spec((c, BRANCH_W)), _const_spec((1, BRANCH_W)),
                  _const_spec((1, BRANCH_W)), _const_spec((1, BRANCH_W))],
        out_specs=row,
        out_shape=jax.ShapeDtypeStruct((s, BRANCH_W), BF16),
        scratch_shapes=[pltpu.VMEM((BRANCH_W, BRANCH_W), F32)],
        compiler_params=_params("arbitrary"),
        name="retention",
    )(proj, proj, proj, proj, cos_t, sin_t, perm, _head_ones(), _head_masks(), dstack, xi, zeta, gc,
      gn_g.reshape(1, BRANCH_W).astype(F32), gn_b.reshape(1, BRANCH_W).astype(F32))


def _merge_kernel(h_ref, o0_ref, o1_ref, o2_ref, o3_ref, x_ref, wg_ref, wb_ref, wo_ref, g_ref, *rest):
    n_cast = (len(rest) - 2) // 2
    cast_in, (xo_ref, ho_ref), cast_out = rest[:n_cast], rest[n_cast:n_cast + 2], rest[n_cast + 2:]
    for src, dst in zip(cast_in, cast_out):
        dst[...] = src[...].astype(dst.dtype)
    h = h_ref[...]
    merged = jnp.zeros(x_ref.shape, F32)
    for bi, o_ref in enumerate((o0_ref, o1_ref, o2_ref, o3_ref)):
        gate = _sigmoid(_dot(h, wg_ref[:, bi * D_MODEL:(bi + 1) * D_MODEL]))
        merged = merged + gate * _dot(o_ref[...], wb_ref[bi])
    x_new = x_ref[...] + _dot(merged.astype(BF16), wo_ref[...])
    xo_ref[...] = x_new
    ho_ref[...] = _rms(x_new, g_ref[...]).astype(ho_ref.dtype)


def merge(h, branches, x, w_gate, w_branch, w_out, next_gain, tm=512, cast=()):
    s, d = x.shape
    tm = min(tm, s)
    steps = s // tm
    row = lambda w: pl.BlockSpec((tm, w), lambda i: (i, 0))
    flat = [c.reshape(-1, c.shape[-1]) for c in cast]
    slab = lambda c: pl.BlockSpec((c.shape[0] // steps, c.shape[1]), lambda i: (i, 0))
    outs = pl.pallas_call(
        _merge_kernel,
        grid=(steps,),
        in_specs=[row(d), row(BRANCH_W), row(BRANCH_W), row(BRANCH_W), row(BRANCH_W), row(d),
                  _const_spec((d, N_BRANCH * d)), _const_spec((N_BRANCH, BRANCH_W, d)),
                  _const_spec((d, d)), _const_spec((1, d))] + [slab(c) for c in flat],
        out_specs=[row(d), row(d)] + [slab(c) for c in flat],
        out_shape=[jax.ShapeDtypeStruct((s, d), F32), jax.ShapeDtypeStruct((s, d), BF16)]
        + [jax.ShapeDtypeStruct(c.shape, BF16) for c in flat],
        compiler_params=_params("parallel"),
        name="merge",
    )(h, *branches, x, w_gate, w_branch, w_out, next_gain.reshape(1, d).astype(F32), *flat)
    return outs[0], outs[1], [o.reshape(c.shape) for o, c in zip(outs[2:], cast)]


def _ffn_kernel(h_ref, x_ref, wg_ref, wu_ref, wd_ref, g_ref, xo_ref, ho_ref, acc_ref):
    f = pl.program_id(1)

    @pl.when(f == 0)
    def _():
        acc_ref[...] = jnp.zeros_like(acc_ref)

    h = h_ref[...]
    a = _silu(_dot(h, wg_ref[...].astype(BF16))) * _dot(h, wu_ref[...].astype(BF16))
    acc_ref[...] += _dot(a.astype(BF16), wd_ref[...].astype(BF16))

    @pl.when(f == pl.num_programs(1) - 1)
    def _():
        x_new = x_ref[...] + acc_ref[...]
        xo_ref[...] = x_new
        ho_ref[...] = _rms(x_new, g_ref[...]).astype(ho_ref.dtype)


def ffn_dense(h, x, w_gate, w_up, w_down, next_gain, tm=1024, tf=896):
    s, d = x.shape
    tm = min(tm, s)
    dff = w_gate.shape[1]
    row = lambda: pl.BlockSpec((tm, d), lambda i, f: (i, 0))
    return pl.pallas_call(
        _ffn_kernel,
        grid=(s // tm, dff // tf),
        in_specs=[row(), row(),
                  pl.BlockSpec((d, tf), lambda i, f: (0, f)),
                  pl.BlockSpec((d, tf), lambda i, f: (0, f)),
                  pl.BlockSpec((tf, d), lambda i, f: (f, 0)),
                  _const_spec((1, d))],
        out_specs=[row(), row()],
        out_shape=[jax.ShapeDtypeStruct((s, d), F32), jax.ShapeDtypeStruct((s, d), BF16)],
        scratch_shapes=[pltpu.VMEM((tm, d), F32)],
        compiler_params=_params("parallel", "arbitrary"),
        name="ffn_dense",
    )(h, x, w_gate, w_up, w_down, next_gain.reshape(1, d).astype(F32))


MERGE_CAST_TILE = 256
MOE_TOK_TILE = 256
MOE_ROUTE_TILES = 4
MOE_ROW_BLOCK = 512
MOE_GATHER_ROWS = 128
MOE_GATHER_TILES = 4
MOE_VMEM_LIMIT = 60 * 1024 * 1024
MOE_Y_BLOCK = 128
MOE_Y_FETCH = MOE_TOK_TILE // MOE_Y_BLOCK + 1


def _router_logits(x, w):
    xh, xm, _ = _split3(x)
    wh, wm, _ = _split3(w)
    packed = (wh.astype(F32) + pltpu.roll(wm.astype(F32), N_EXPERTS, axis=1)).astype(BF16)
    s = _dot(xh, packed) + _dot(xm, packed)
    return s + pltpu.roll(s, LANE - N_EXPERTS, axis=1)


def _route_kernel(x_ref, ng_ref, router_ref, ltri_ref, pos_ref, gate_ref, post_ref, before_ref, total_ref,
                  carry_ref):
    tm = x_ref.shape[0]

    @pl.when(pl.program_id(0) == 0)
    def _():
        carry_ref[...] = jnp.zeros_like(carry_ref)

    hn = _rms(x_ref[...], ng_ref[...])
    logits = _router_logits(hn, router_ref[...])
    lane = lax.broadcasted_iota(jnp.int32, (tm, LANE), 1)
    logits = jnp.where(lane < N_EXPERTS, logits, -jnp.inf)
    v1 = jnp.max(logits, axis=1, keepdims=True)
    i1 = jnp.min(jnp.where(logits == v1, lane, LANE), axis=1, keepdims=True)
    rest = jnp.where(lane == i1, -jnp.inf, logits)
    v2 = jnp.max(rest, axis=1, keepdims=True)
    i2 = jnp.min(jnp.where(rest == v2, lane, LANE), axis=1, keepdims=True)
    w1 = 1.0 / (1.0 + jnp.exp(v2 - v1))
    gate_ref[...] = jnp.where(lane == i1, w1, 0.0) + jnp.where(lane == i2, 1.0 - w1, 0.0)

    member = jnp.where((lane == i1) | (lane == i2), 1.0, 0.0)
    carry = carry_ref[...]
    ranks = []
    for blk in range(tm // MOE_TOK_TILE):
        before_ref[blk] = carry
        mb = member[blk * MOE_TOK_TILE:(blk + 1) * MOE_TOK_TILE]
        ranks.append(_dot(ltri_ref[...], mb.astype(BF16)) + carry)
        carry = carry + jnp.sum(mb, axis=0, keepdims=True)
    pos = jnp.where(member > 0.0, jnp.concatenate(ranks, axis=0), -1.0)
    pos_ref[...] = pos
    post_ref[...] = pos.T[:SUBLANE, :]
    carry_ref[...] = carry
    total_ref[...] = carry


def moe_route(x, norm_gain, router):
    s, d = x.shape
    tm = min(MOE_ROUTE_TILES * MOE_TOK_TILE, s)
    nsub = tm // MOE_TOK_TILE
    nt = s // tm
    router_p = jnp.zeros((d, LANE), F32).at[:, :N_EXPERTS].set(router.astype(F32))
    tt = min(MOE_TOK_TILE, s)
    ltri = _const(np.tril(np.ones((tt, tt)), -1), BF16)
    row = pl.BlockSpec((tm, LANE), lambda i: (i, 0))
    return pl.pallas_call(
        _route_kernel,
        grid=(nt,),
        in_specs=[pl.BlockSpec((tm, d), lambda i: (i, 0)), _const_spec((1, d)), _const_spec((d, LANE)),
                  _const_spec((tt, tt))],
        out_specs=[row, row, pl.BlockSpec((SUBLANE, tm), lambda i: (0, i)),
                   pl.BlockSpec((nsub, 1, LANE), lambda i: (i, 0, 0)), pl.BlockSpec((1, LANE), lambda i: (0, 0))],
        out_shape=[jax.ShapeDtypeStruct((s, LANE), F32), jax.ShapeDtypeStruct((s, LANE), F32),
                   jax.ShapeDtypeStruct((SUBLANE, s), F32), jax.ShapeDtypeStruct((nt * nsub, 1, LANE), F32),
                   jax.ShapeDtypeStruct((1, LANE), F32)],
        scratch_shapes=[pltpu.VMEM((1, LANE), F32)],
        compiler_params=_params("arbitrary"),
        name="moe_route",
    )(x, norm_gain.reshape(1, d).astype(F32), router_p, ltri)


def _moe_ffn_kernel(be_ref, r0_ref, tlo_ref, thi_ref, nv_ref, h_ref, post_ref, wg_ref, wu_ref, wd_ref,
                    y_ref, x_sc, acc_sc, *, tt, nsub):
    b = pl.program_id(0)
    f = pl.program_id(1)
    nb = pl.num_programs(0)
    valid = b < nv_ref[0]
    tmb, d = acc_sc.shape
    gr = tmb // nsub
    last_tile = h_ref.shape[0] // tt - 1

    def picked(e, want, t):
        t0 = pl.multiple_of(t * tt, tt)
        p = post_ref[pl.ds(e, 1), pl.ds(t0, tt)]
        sel = jnp.where(p == want, 1.0, 0.0).astype(BF16)
        return _dot(sel, h_ref[pl.ds(t0, tt), :])

    def gather_head(blk, sb):
        e = be_ref[blk]
        lo = tlo_ref[blk * nsub + sb]
        hi = thi_ref[blk * nsub + sb]
        want = (lax.broadcasted_iota(jnp.int32, (gr, 1), 0) + (r0_ref[blk] + sb * gr)).astype(F32)
        rows = picked(e, want, lo)
        for k in range(1, MOE_GATHER_TILES):
            rows = rows + picked(e, jnp.where(lo + k <= hi, want, -2.0), jnp.minimum(lo + k, last_tile))
        return rows.astype(BF16)

    def gather_tail(slot, blk, sb):
        e = be_ref[blk]
        want = (lax.broadcasted_iota(jnp.int32, (gr, 1), 0) + (r0_ref[blk] + sb * gr)).astype(F32)
        rows = pl.ds(pl.multiple_of(sb * gr, gr), gr)

        def more(t, carry):
            x_sc[slot, rows, :] = (x_sc[slot, rows, :].astype(F32) + picked(e, want, t)).astype(BF16)
            return carry

        lax.fori_loop(tlo_ref[blk * nsub + sb] + MOE_GATHER_TILES, thi_ref[blk * nsub + sb] + 1, more, 0)

    @pl.when((b == 0) & (f == 0))
    def _():
        acc_sc[...] = jnp.zeros_like(acc_sc)
        for sb in range(nsub):
            x_sc[0, pl.ds(sb * gr, gr), :] = gather_head(0, sb)
            gather_tail(0, 0, sb)

    nxt = jnp.minimum(b + 1, nb - 1)
    nslot = (b + 1) % 2

    @pl.when(valid)
    def _():
        x_sc[nslot, pl.ds(pl.multiple_of(f * gr, gr), gr), :] = gather_head(nxt, f)
        xb = x_sc[b % 2]
        a = _silu(_dot(xb, wg_ref[0])) * _dot(xb, wu_ref[0])
        acc_sc[...] = jnp.where(f == 0, 0.0, acc_sc[...]) + _dot(a.astype(BF16), wd_ref[0].astype(BF16))

    @pl.when(valid & (thi_ref[nxt * nsub + f] - tlo_ref[nxt * nsub + f] >= MOE_GATHER_TILES))
    def _():
        gather_tail(nslot, nxt, f)

    @pl.when(f == nsub - 1)
    def _():
        y_ref[...] = jnp.where(valid, acc_sc[...], 0.0).astype(y_ref.dtype)


def moe_ffn(h, post, sched, w_gate, w_up, w_down):
    s, d = h.shape
    ne, _, dff = w_gate.shape
    tmb = MOE_ROW_BLOCK
    tt = min(MOE_TOK_TILE, s)
    nb = sched[0].shape[0]
    nf = tmb // MOE_GATHER_ROWS
    tf = dff // nf

    def fidx(b, f, nv):
        return jnp.where(b < nv[0], f, nf - 1)

    grid_spec = pltpu.PrefetchScalarGridSpec(
        num_scalar_prefetch=5,
        grid=(nb, nf),
        in_specs=[pl.BlockSpec((s, d), lambda b, f, *_: (0, 0), pipeline_mode=pl.Buffered(1)),
                  pl.BlockSpec((SUBLANE, s), lambda b, f, *_: (0, 0), pipeline_mode=pl.Buffered(1)),
                  pl.BlockSpec((1, d, tf), lambda b, f, be, r0, tlo, thi, nv: (be[b], 0, fidx(b, f, nv))),
                  pl.BlockSpec((1, d, tf), lambda b, f, be, r0, tlo, thi, nv: (be[b], 0, fidx(b, f, nv))),
                  pl.BlockSpec((1, tf, d), lambda b, f, be, r0, tlo, thi, nv: (be[b], fidx(b, f, nv), 0))],
        out_specs=pl.BlockSpec((tmb, d), lambda b, f, *_: (b, 0)),
        scratch_shapes=[pltpu.VMEM((2, tmb, d), BF16), pltpu.VMEM((tmb, d), F32)],
    )
    return pl.pallas_call(
        functools.partial(_moe_ffn_kernel, tt=tt, nsub=nf),
        grid_spec=grid_spec,
        out_shape=jax.ShapeDtypeStruct((nb * tmb, d), BF16),
        compiler_params=pltpu.CompilerParams(dimension_semantics=("arbitrary", "arbitrary"),
                                             vmem_limit_bytes=MOE_VMEM_LIMIT),
        name="moe_ffn",
    )(*sched, h, post, w_gate, w_up, w_down)


def _moe_combine_kernel(kb_ref, off_ref, lim_ref, x_ref, pos_ref, gate_ref, fg_ref, *rest):
    y_refs, o_ref, acc_sc = rest[:-2], rest[-2], rest[-1]
    t = pl.program_id(0)
    yb = y_refs[0].shape[0]
    pos = pos_ref[...]
    gate = gate_ref[...]
    col = lax.broadcasted_iota(jnp.int32, (1, yb), 1).astype(F32)

    def routed(e):
        pe = pos[:, e:e + 1]
        ge = gate[:, e:e + 1]
        r = jnp.where(pe >= 0.0, pe + off_ref[t * N_EXPERTS + e].astype(F32), -1.0)
        return r, ge

    def picked(e, k, r):
        sel = jnp.where(r == col + float(k * yb), 1.0, 0.0).astype(BF16)
        return _dot(sel, y_refs[MOE_Y_FETCH * e + k][...])

    col2 = lax.broadcasted_iota(jnp.int32, (1, 2 * yb), 1).astype(F32)
    acc = x_ref[...]
    for e in range(N_EXPERTS):
        r, ge = routed(e)
        pair = jnp.concatenate([y_refs[MOE_Y_FETCH * e][...], y_refs[MOE_Y_FETCH * e + 1][...]], axis=0)
        acc = acc + ge * _dot(jnp.where(r == col2, 1.0, 0.0).astype(BF16), pair)
    acc_sc[...] = acc
    for e in range(N_EXPERTS):
        for k in range(2, MOE_Y_FETCH):
            @pl.when(lim_ref[t * N_EXPERTS + e] > k * yb)
            def _(k=k, e=e):
                r, ge = routed(e)
                acc_sc[...] += ge * picked(e, k, r)
    o_ref[...] = _rms(acc_sc[...], fg_ref[...]).astype(o_ref.dtype)


def moe_combine(x, pos, gate, y, kb, off, lim, final_gain):
    s, d = x.shape
    tm = min(MOE_TOK_TILE, s)
    yb = MOE_Y_BLOCK
    last = y.shape[0] // yb - 1

    def yspec(e, k):
        def index(t, kb_r, off_r, lim_r):
            blk = jnp.minimum(kb_r[t * N_EXPERTS + e] + k, last)
            return (blk if k < 1 else jnp.where(lim_r[t * N_EXPERTS + e] > k * yb, blk, 0), 0)
        return pl.BlockSpec((yb, d), index)

    grid_spec = pltpu.PrefetchScalarGridSpec(
        num_scalar_prefetch=3,
        grid=(s // tm,),
        in_specs=[pl.BlockSpec((tm, d), lambda t, *_: (t, 0)),
                  pl.BlockSpec((tm, LANE), lambda t, *_: (t, 0)),
                  pl.BlockSpec((tm, LANE), lambda t, *_: (t, 0)),
                  pl.BlockSpec((1, d), lambda t, *_: (0, 0))]
        + [yspec(e, k) for e in range(N_EXPERTS) for k in range(MOE_Y_FETCH)],
        out_specs=pl.BlockSpec((tm, d), lambda t, *_: (t, 0)),
        scratch_shapes=[pltpu.VMEM((tm, d), F32)],
    )
    return pl.pallas_call(
        _moe_combine_kernel,
        grid_spec=grid_spec,
        out_shape=jax.ShapeDtypeStruct((s, d), F32),
        compiler_params=_params("arbitrary"),
        name="moe_combine",
    )(kb, off, lim, x, pos, gate, final_gain.reshape(1, d).astype(F32), *([y] * (MOE_Y_FETCH * N_EXPERTS)))


def _moe_schedule(before, total, s):
    tmb, yb, gr = MOE_ROW_BLOCK, MOE_Y_BLOCK, MOE_GATHER_ROWS
    nb = 2 * s // tmb + N_EXPERTS
    counts = total[0, :N_EXPERTS].astype(jnp.int32)
    nblk = (counts + tmb - 1) // tmb
    end = jnp.cumsum(nblk)
    first = end - nblk
    nvalid = end[-1]
    b = jnp.minimum(jnp.arange(nb, dtype=jnp.int32), nvalid - 1)
    blk_e = jnp.sum(b[:, None] >= end[None, :], axis=1).astype(jnp.int32)
    r0 = (b - first[blk_e]) * tmb
    cb = before[:, 0, :N_EXPERTS].astype(jnp.int32)
    r0s = (r0[:, None] + gr * jnp.arange(tmb // gr, dtype=jnp.int32)[None, :]).reshape(-1)
    cbe = cb[:, jnp.repeat(blk_e, tmb // gr)]
    tlo = (jnp.sum(cbe <= r0s[None, :], axis=0) - 1).astype(jnp.int32)
    thi = (jnp.sum(cbe < (r0s + gr)[None, :], axis=0) - 1).astype(jnp.int32)
    sched = (blk_e, r0.astype(jnp.int32), tlo, thi, nvalid.reshape(1).astype(jnp.int32))
    row_start = first[None, :] * tmb + cb
    kb = row_start // yb
    off = first[None, :] * tmb - kb * yb
    n_te = jnp.concatenate([cb[1:], counts[None, :]], axis=0) - cb
    lim = row_start - kb * yb + n_te
    flat = lambda a: a.reshape(-1).astype(jnp.int32)
    return sched, flat(kb), flat(off), flat(lim)


def moe_sparse(h, x, norm_gain, router, w_gate, w_up, w_down, final_gain):
    s, _ = x.shape
    pos, gate, post, before, total = moe_route(x, norm_gain, router)
    sched, kb, off, lim = _moe_schedule(before, total, s)
    y = moe_ffn(h, post, sched, w_gate, w_up, w_down)
    return moe_combine(x, pos, gate, y, kb, off, lim, final_gain)


FOX_F0 = 7 * BRANCH_W


def _split_w_in_kernel(wt_ref, mix_ref, f_ref, gate_ref):
    rest0 = FOX_F0 + N_HEADS
    gate0 = N_MIX_COLS + N_HEADS
    mix_ref[:, :FOX_F0] = wt_ref[pl.ds(0, FOX_F0), :].T.astype(BF16)
    mix_ref[:, FOX_F0:] = wt_ref[pl.ds(rest0, N_MIX_COLS - FOX_F0), :].T.astype(BF16)
    gate_ref[...] = wt_ref[pl.ds(gate0, N_BRANCH * D_MODEL), :].T.astype(BF16)
    f_rows = jnp.concatenate([wt_ref[pl.ds(FOX_F0, N_HEADS), :], jnp.zeros((LANE - N_HEADS, LANE), F32)], axis=0)
    f_ref[...] = f_rows.T.astype(BF16)


def _mixer_weights(w_in, layer, tr=LANE):
    _, d, cols = w_in.shape
    w_t = jnp.swapaxes(w_in, 1, 2)
    row = lambda w: pl.BlockSpec((tr, w), lambda i: (i, 0))
    return pl.pallas_call(
        _split_w_in_kernel,
        grid=(d // tr,),
        in_specs=[pl.BlockSpec((None, cols, tr), lambda i: (layer, 0, i))],
        out_specs=[row(N_MIX_COLS), row(LANE), row(N_BRANCH * D_MODEL)],
        out_shape=[jax.ShapeDtypeStruct((d, N_MIX_COLS), BF16), jax.ShapeDtypeStruct((d, LANE), BF16),
                   jax.ShapeDtypeStruct((d, N_BRANCH * D_MODEL), BF16)],
        compiler_params=_params("parallel"),
        name="split_w_in",
    )(w_t)


def kernel(x, w_in, w_branch, w_out, norm_mix_g, hgrn_lb_logits, hgrn_norm_g, fox_f_bias, pool_w, pool_scale,
           ret_gn_g, ret_gn_b, norm_ffn_g, ffn_w_gate, ffn_w_up, ffn_w_down, moe_router, moe_w_gate, moe_w_up,
           moe_w_down, final_norm_g):
    b, s, d = x.shape
    assert b == 1 and d == D_MODEL
    depth = w_in.shape[0]
    assert depth == 2, "layer 0 uses the dense FFN, layer 1 the experts and the final norm"
    xs = x.reshape(s, d)
    h = None
    out = None
    for layer in range(depth):
        w_mix, w_f, w_gate = _mixer_weights(w_in, layer)
        if layer == 0:
            proj, h = norm_matmul(xs, norm_mix_g[0], w_mix)
        else:
            proj = matmul(h, w_mix)
        ct, c_rows, fox_stats = fox_gate(h, proj, w_f, fox_f_bias[layer])
        branches = (
            hgrn2(proj, hgrn_lb_logits, hgrn_norm_g[layer], layer),
            fox_attention(proj, ct, c_rows, fox_stats),
            pool_mixer(proj, pool_w[layer], pool_scale[layer]),
            retention(proj, ret_gn_g[layer], ret_gn_b[layer]),
        )
        li = layer // 2
        if layer % 2 == 0:
            xs, h2, (ffn_wg, ffn_wu) = merge(h, branches, xs, w_gate, w_branch[layer].astype(BF16),
                                             w_out[layer].astype(BF16), norm_ffn_g[layer],
                                             cast=(ffn_w_gate[li], ffn_w_up[li]))
            xs, h = ffn_dense(h2, xs, ffn_wg, ffn_wu, ffn_w_down[li], norm_mix_g[layer + 1])
        else:
            xs, h2, (moe_wg, moe_wu) = merge(h, branches, xs, w_gate, w_branch[layer].astype(BF16),
                                             w_out[layer].astype(BF16), norm_ffn_g[layer], tm=MERGE_CAST_TILE,
                                             cast=(moe_w_gate[li], moe_w_up[li]))
            out = moe_sparse(h2, xs, norm_ffn_g[layer], moe_router[li], moe_wg, moe_wu, moe_w_down[li],
                             final_norm_g)
    return out.reshape(b, s, d)
```

```python
import functools
import math

import jax
import jax.numpy as jnp
import numpy as np
from jax import lax
from jax.experimental import pallas as pl
from jax.experimental.pallas import tpu as pltpu

D_MODEL = 1024
N_BRANCH = 4
BRANCH_W = D_MODEL // N_BRANCH
HEAD_DIM = 64
N_HEADS = BRANCH_W // HEAD_DIM
POOL_WINDOWS = (2, 4, 8, 16)
POOL_GROUP = BRANCH_W // len(POOL_WINDOWS)
POOL_HALO = 16
RET_DECAY_BASE = 5.0
ROPE_BASE = 10000.0
N_EXPERTS = 8
RMS_EPS = 1e-6
LN_EPS = 1e-5
N_MIX_COLS = 12 * BRANCH_W

LANE = 128
SUBLANE = 8
VMEM_LIMIT = 56 * 1024 * 1024

HG_CHUNK = 64
HG_SUB = 16
HG_FAST_MIN_LOGDECAY = -60.0
RET_CHUNK = 256

F32 = jnp.float32
BF16 = jnp.bfloat16
NT_DIMS = (((1,), (1,)), ((), ()))


def _params(*sem):
    return pltpu.CompilerParams(dimension_semantics=sem, vmem_limit_bytes=VMEM_LIMIT)


def _const_spec(shape):
    nd = len(shape)
    return pl.BlockSpec(shape, lambda *_: (0,) * nd, pipeline_mode=pl.Buffered(1))


def _split3(x):
    hi = x.astype(BF16)
    r1 = x - hi.astype(F32)
    mid = r1.astype(BF16)
    lo = (r1 - mid.astype(F32)).astype(BF16)
    return hi, mid, lo


def _dot(a, b):
    return jnp.dot(a, b, preferred_element_type=F32)


def _dot_nt(a, b):
    return lax.dot_general(a, b, NT_DIMS, preferred_element_type=F32)


def _dot_exact_rhs(x, m_bf16, terms=3):
    return sum(_dot(part, m_bf16) for part in _split3(x)[:terms])


def _dot_exact_lhs(m_bf16, x, terms=3):
    return sum(_dot(m_bf16, part) for part in _split3(x)[:terms])


def _sigmoid(x):
    return 1.0 / (1.0 + jnp.exp(-x))


def _silu(x):
    return x * _sigmoid(x)


def _rms(x, gain):
    return x * lax.rsqrt(jnp.mean(x * x, axis=-1, keepdims=True) + RMS_EPS) * gain


def _const(a, dtype=F32):
    return jnp.asarray(np.asarray(a, np.float32), dtype)


def _head_of(n):
    return np.arange(n) // HEAD_DIM


def _head_ones():
    h = _head_of(BRANCH_W)
    return _const(h[:, None] == h[None, :], BF16)


def _head_masks():
    return _const(_head_of(BRANCH_W)[None, :] == np.arange(N_HEADS)[:, None])


def _norm_matmul_kernel(x_ref, g_ref, b_ref, o_ref, h_ref):
    @pl.when(pl.program_id(1) == 0)
    def _():
        h_ref[...] = _rms(x_ref[...], g_ref[...]).astype(h_ref.dtype)

    o_ref[...] = _dot(h_ref[...], b_ref[...]).astype(o_ref.dtype)


def norm_matmul(x, gain, b, tm=2048, tn=1024):
    m, k = x.shape
    _, n = b.shape
    tm = min(tm, m)
    return pl.pallas_call(
        _norm_matmul_kernel,
        grid=(m // tm, n // tn),
        in_specs=[pl.BlockSpec((tm, k), lambda i, j: (i, 0)), _const_spec((1, k)),
                  pl.BlockSpec((k, tn), lambda i, j: (0, j))],
        out_specs=[pl.BlockSpec((tm, tn), lambda i, j: (i, j)), pl.BlockSpec((tm, k), lambda i, j: (i, 0))],
        out_shape=[jax.ShapeDtypeStruct((m, n), BF16), jax.ShapeDtypeStruct((m, k), BF16)],
        compiler_params=_params("arbitrary", "arbitrary"),
        name="in_proj_norm",
    )(x, gain.reshape(1, k).astype(F32), b)


def _matmul_kernel(a_ref, b_ref, o_ref):
    o_ref[...] = _dot(a_ref[...], b_ref[...]).astype(o_ref.dtype)


def matmul(a, b, out_dtype=BF16, tm=2048, tn=1536):
    m, k = a.shape
    _, n = b.shape
    tm = min(tm, m)
    return pl.pallas_call(
        _matmul_kernel,
        grid=(n // tn, m // tm),
        in_specs=[pl.BlockSpec((tm, k), lambda j, i: (i, 0)),
                  pl.BlockSpec((k, tn), lambda j, i: (0, j))],
        out_specs=pl.BlockSpec((tm, tn), lambda j, i: (i, j)),
        out_shape=jax.ShapeDtypeStruct((m, n), out_dtype),
        compiler_params=_params("parallel", "parallel"),
        name="in_proj",
    )(a, b)


def _hgrn_kernel(q_ref, f_ref, i_ref, g_ref, lbl_ref, ng_ref, ones_ref, hm_ref, tril_ref, halfsum_ref, fmask_ref,
                 o_ref, st_ref, bpad, kpad, vpad, astack, lf_sc, kk_sc, o_sc, *, layer, tile):
    c, sub = HG_CHUNK, HG_SUB
    nsub = c // sub
    half = c // 2

    @pl.when(pl.program_id(0) == 0)
    def _():
        st_ref[...] = jnp.zeros_like(st_ref)
        bpad[...] = jnp.zeros_like(bpad)
        kpad[...] = jnp.zeros_like(kpad)
        vpad[...] = jnp.zeros_like(vpad)

    lbl = lbl_ref[...]
    e = jnp.exp(lbl - jnp.max(lbl, axis=0, keepdims=True))
    p = e / jnp.sum(e, axis=0, keepdims=True)
    lb = jnp.zeros((1, BRANCH_W), F32)
    for l in range(1, layer + 1):
        lb = lb + p[l:l + 1, :]

    ones_bd = ones_ref[...]
    hm = hm_ref[...]
    tril = tril_ref[...]
    row = lax.broadcasted_iota(jnp.int32, (c, 1), 0)
    row_in_sub = row % sub
    bd_mask = ones_bd.astype(F32)

    sig = _sigmoid(f_ref[...].astype(F32))
    logf_all = jnp.log(lb + (1.0 - lb) * sig)
    lf_sc[...] = _dot_exact_lhs(tril, logf_all, terms=2)
    kk_sc[...] = (1.0 - lb) * (1.0 - sig)
    min_decay = jnp.min(_dot(halfsum_ref[...], logf_all.astype(BF16)))

    def load(ci):
        r0 = pl.multiple_of(ci * c, c)
        q = q_ref[pl.ds(r0, c), :].astype(F32)
        v = i_ref[pl.ds(r0, c), :].astype(F32)
        kk = kk_sc[pl.ds(r0, c), :]
        b = lf_sc[pl.ds(r0, c), :]
        return r0, q, v, kk, b

    def finish(r0, q_decayed, v, kk, b, intra):
        st = st_ref[...]
        inter = _dot_nt(q_decayed.astype(BF16), st.astype(BF16))
        b_last = b[c - 1:c, :]
        ks_end = (kk * jnp.exp(b_last - b)).astype(BF16)
        upd = _dot(v.T.astype(BF16), ks_end)
        st_ref[...] = st * jnp.exp(b_last) + upd * bd_mask
        o_sc[pl.ds(r0, c), :] = intra + inter

    def fast_chunk(ci, carry):
        r0, q, v, kk, b = load(ci)
        second = row >= half
        m_row = b[half - 1:half, :]
        mref = jnp.where(second, m_row, 0.0)
        qp = q * jnp.exp(b - mref)
        kp = kk * jnp.exp(mref - b)
        e_m = jnp.exp(m_row)
        kaug = jnp.concatenate([kp, kp[:half, :] * e_m], axis=0)
        vaug = jnp.concatenate([v, v[:half, :]], axis=0)
        k_heads = jnp.concatenate([kaug * hm[h:h + 1, :] for h in range(N_HEADS)], axis=0).astype(BF16)
        v_heads = jnp.concatenate([vaug * hm[h:h + 1, :] for h in range(N_HEADS)], axis=0).astype(BF16)
        sc = jnp.where(fmask_ref[...] > 0.0, _dot_nt(qp.astype(BF16), k_heads), 0.0)
        intra = _dot(sc.astype(BF16), v_heads)
        finish(r0, jnp.where(second, qp * e_m, qp), v, kk, b, intra)
        return carry

    def exact_chunk(ci, carry):
        r0, q, v, kk, b = load(ci)

        bpad[pl.ds(sub, c), :] = b
        kpad[pl.ds(sub, c), :] = kk
        vpad[pl.ds(sub, c), :] = v

        for d in range(sub):
            b_d = bpad[pl.ds(sub - d, c), :]
            k_d = kpad[pl.ds(sub - d, c), :]
            a = jnp.where(row_in_sub >= d, q * k_d * jnp.exp(b - b_d), 0.0)
            astack[pl.ds(d * c, c), :] = a.astype(BF16)
        pall = _dot(astack[...], ones_bd)
        intra = jnp.zeros((c, BRANCH_W), F32)
        for d in range(sub):
            intra = intra + pall[d * c:(d + 1) * c, :] * vpad[pl.ds(sub - d, c), :]

        pieces = [jnp.zeros((sub, BRANCH_W), F32)]
        for si in range(1, nsub):
            lo = si * sub
            m_i = b[lo - 1:lo, :]
            qs = q[lo:lo + sub, :] * jnp.exp(b[lo:lo + sub, :] - m_i)
            ks = (kk[:lo, :] * jnp.exp(m_i - b[:lo, :])).astype(BF16)
            qx = jnp.concatenate([qs * hm[h:h + 1, :] for h in range(N_HEADS)], axis=0).astype(BF16)
            sc = _dot_nt(qx, ks)
            r = _dot(sc.astype(BF16), v[:lo, :].astype(BF16))
            acc = jnp.zeros((sub, BRANCH_W), F32)
            for h in range(N_HEADS):
                acc = acc + r[h * sub:(h + 1) * sub, :] * hm[h:h + 1, :]
            pieces.append(acc)
        intra = intra + jnp.concatenate(pieces, axis=0)
        finish(r0, q * jnp.exp(b), v, kk, b, intra)
        return carry

    lax.cond(min_decay >= HG_FAST_MIN_LOGDECAY,
             lambda: lax.fori_loop(0, tile // c, fast_chunk, 0, unroll=2),
             lambda: lax.fori_loop(0, tile // c, exact_chunk, 0))

    o = o_sc[...]
    ms = _dot_exact_rhs(o * o, ones_bd, terms=2) * (1.0 / HEAD_DIM)
    y = o * lax.rsqrt(ms + RMS_EPS) * ng_ref[...] * _silu(g_ref[...].astype(F32))
    o_ref[...] = y.astype(o_ref.dtype)


def hgrn2(proj, lb_logits, norm_g, layer, tile=512):
    s = proj.shape[0]
    depth = lb_logits.shape[0]
    c, sub = HG_CHUNK, HG_SUB
    half = c // 2
    tile = min(tile, s)
    col = lambda j: pl.BlockSpec((tile, BRANCH_W), lambda i, j=j: (i, j))
    pos = np.arange(tile)
    tril = _const((pos[:, None] // c == pos[None, :] // c) & (pos[None, :] <= pos[:, None]), BF16)
    nhalf = tile // half
    halfsum = _const(np.arange(tile)[None, :] // half == np.arange(nhalf)[:, None], BF16)
    t = np.arange(c)[:, None]
    col_s = np.arange(c + half)[None, :]
    same_half = (col_s < c) & (col_s // half == t // half) & (col_s <= t)
    cross = (col_s >= c) & (t >= half)
    fmask = _const(np.tile(same_half | cross, (1, N_HEADS)))
    return pl.pallas_call(
        functools.partial(_hgrn_kernel, layer=layer, tile=tile),
        grid=(s // tile,),
        in_specs=[col(0), col(1), col(2), col(3),
                  _const_spec((depth, BRANCH_W)), _const_spec((1, BRANCH_W)),
                  _const_spec((BRANCH_W, BRANCH_W)), _const_spec((N_HEADS, BRANCH_W)),
                  _const_spec((tile, tile)), _const_spec((nhalf, tile)),
                  _const_spec((c, N_HEADS * (c + half)))],
        out_specs=pl.BlockSpec((tile, BRANCH_W), lambda i: (i, 0)),
        out_shape=jax.ShapeDtypeStruct((s, BRANCH_W), BF16),
        scratch_shapes=[pltpu.VMEM((BRANCH_W, BRANCH_W), F32),
                        pltpu.VMEM((c + sub, BRANCH_W), F32),
                        pltpu.VMEM((c + sub, BRANCH_W), F32),
                        pltpu.VMEM((c + sub, BRANCH_W), F32),
                        pltpu.VMEM((sub * c, BRANCH_W), BF16),
                        pltpu.VMEM((tile, BRANCH_W), F32),
                        pltpu.VMEM((tile, BRANCH_W), F32),
                        pltpu.VMEM((tile, BRANCH_W), F32)],
        compiler_params=_params("arbitrary"),
        name="hgrn2",
    )(proj, proj, proj, proj, lb_logits.astype(F32), norm_g.reshape(1, BRANCH_W).astype(F32),
      _head_ones(), _head_masks(), tril, halfsum, fmask)


FOX_TILE = 256
FOX_GATE_BLOCKS = 2
FOX_NSTAT = 16
FOX_FIXED_MAX = 30.0
FOX_SKIP_LOG = 30.0


def _fox_gate_kernel(h_ref, q_ref, k_ref, wf_ref, bias_ref, tril_ref, ones_ref,
                     ct_ref, c_ref, stat_ref, carry_ref, kmax_ref):
    @pl.when(pl.program_id(0) == 0)
    def _():
        carry_ref[...] = jnp.zeros_like(carry_ref)
        kmax_ref[...] = jnp.zeros_like(kmax_ref)

    logit = _dot(h_ref[...], wf_ref[...]) + bias_ref[...]
    logf = jnp.minimum(logit, 0.0) - jnp.log(1.0 + jnp.exp(-jnp.abs(logit)))
    carry = carry_ref[...]
    parts = []
    for blk in range(logf.shape[0] // FOX_TILE):
        part = _dot_exact_lhs(tril_ref[...], logf[blk * FOX_TILE:(blk + 1) * FOX_TILE]) + carry
        carry = part[-1:, :]
        parts.append(part)
    cum = jnp.concatenate(parts, axis=0)
    carry_ref[...] = carry
    ct_ref[...] = cum.T[:SUBLANE, :]
    c_ref[...] = cum

    ones_bd = ones_ref[...]
    q = q_ref[...].astype(F32)
    k = k_ref[...].astype(F32)
    scale = HEAD_DIM ** -0.5
    head_lane = lax.broadcasted_iota(jnp.int32, (1, BRANCH_W), 1) // HEAD_DIM
    c_heads = jnp.zeros(q.shape, F32)
    for h in range(N_HEADS):
        c_heads = jnp.where(head_lane == h, cum[:, h:h + 1], c_heads)
    slack = 1.0 + 2.0 ** -6
    qn = jnp.sqrt(_dot((q * q).astype(BF16), ones_bd)) * (scale * slack)
    kn = jnp.sqrt(_dot((k * k).astype(BF16), ones_bd)) * slack
    diag = _dot((q * k).astype(BF16), ones_bd) * scale - (2.0 ** -6) * qn * kn
    e_row = c_heads - diag
    kmax = kmax_ref[...]
    for blk in range(q.shape[0] // FOX_TILE):
        rs = slice(blk * FOX_TILE, (blk + 1) * FOX_TILE)
        kmax = jnp.maximum(kmax, jnp.max(kn[rs], axis=0, keepdims=True))
        rows = [jnp.max(qn[rs], axis=0, keepdims=True),
                jnp.max(e_row[rs], axis=0, keepdims=True),
                kmax,
                c_heads[(blk + 1) * FOX_TILE - 1:(blk + 1) * FOX_TILE, :]]
        stat_ref[blk] = jnp.concatenate(rows + [jnp.zeros((SUBLANE - len(rows), BRANCH_W), F32)], axis=0)
    kmax_ref[...] = kmax


def fox_gate(h, proj, w_f, f_bias):
    s, d = h.shape
    tile = min(FOX_GATE_BLOCKS * FOX_TILE, s)
    nblk = tile // FOX_TILE
    bias = jnp.zeros((1, LANE), F32).at[0, :N_HEADS].set(f_bias.astype(F32))
    tril = _const(np.tril(np.ones((FOX_TILE, FOX_TILE))), BF16)
    ct, c_rows, stats = pl.pallas_call(
        _fox_gate_kernel,
        grid=(s // tile,),
        in_specs=[pl.BlockSpec((tile, d), lambda i: (i, 0)),
                  pl.BlockSpec((tile, BRANCH_W), lambda i: (i, 4)),
                  pl.BlockSpec((tile, BRANCH_W), lambda i: (i, 5)),
                  _const_spec((d, LANE)), _const_spec((1, LANE)), _const_spec((FOX_TILE, FOX_TILE)),
                  _const_spec((BRANCH_W, BRANCH_W))],
        out_specs=[pl.BlockSpec((SUBLANE, tile), lambda i: (0, i)),
                   pl.BlockSpec((tile, LANE), lambda i: (i, 0)),
                   pl.BlockSpec((nblk, SUBLANE, BRANCH_W), lambda i: (i, 0, 0))],
        out_shape=[jax.ShapeDtypeStruct((SUBLANE, s), F32),
                   jax.ShapeDtypeStruct((s, LANE), F32),
                   jax.ShapeDtypeStruct((s // FOX_TILE, SUBLANE, BRANCH_W), F32)],
        scratch_shapes=[pltpu.VMEM((1, LANE), F32), pltpu.VMEM((1, BRANCH_W), F32)],
        compiler_params=_params("arbitrary"),
        name="fox_gate",
    )(h, proj, proj, w_f, bias, tril, _head_ones())
    return ct, c_rows, stats[:, :4, ::HEAD_DIM].reshape(-1)


def _fox_kernel(stat_ref, q_ref, k_ref, v_ref, ct_ref, c_ref, hm_ref, o_ref, m_sc, l_sc, acc_sc, *, tq):
    i = pl.program_id(0)
    q0 = pl.multiple_of(i * tq, tq)
    hm = hm_ref[...]
    q = q_ref[...].astype(F32) * (HEAD_DIM ** -0.5)
    qh = [(q * hm[h:h + 1, :]).astype(BF16) for h in range(N_HEADS)]
    c_q0 = ct_ref[:, pl.ds(q0, tq)][:, 0:1]

    first = i
    for h in range(N_HEADS):
        qmax = stat_ref[i * FOX_NSTAT + h]
        emax = stat_ref[i * FOX_NSTAT + N_HEADS + h]

        def needed(j, h=h, qmax=qmax, emax=emax):
            jc = jnp.maximum(j, 0)
            bound = (qmax * stat_ref[jc * FOX_NSTAT + 2 * N_HEADS + h] + emax
                     - stat_ref[jc * FOX_NSTAT + 3 * N_HEADS + h])
            return (j >= 0) & (bound >= -FOX_SKIP_LOG)

        last_dropped = lax.while_loop(needed, lambda j: j - 1, i - 1)
        first = jnp.minimum(first, last_dropped + 1)

    l_sc[...] = jnp.zeros_like(l_sc)
    acc_sc[...] = jnp.zeros_like(acc_sc)

    def causal(sc):
        r = lax.broadcasted_iota(jnp.int32, (tq, tq), 0)
        cidx = lax.broadcasted_iota(jnp.int32, (tq, tq), 1)
        return jnp.where(cidx <= r, sc, -jnp.inf)

    def online_block(s0, diagonal):
        kb = k_ref[pl.ds(s0, tq), :]
        vb = v_ref[pl.ds(s0, tq), :]
        bias = c_q0 - ct_ref[:, pl.ds(s0, tq)]
        for h in range(N_HEADS):
            sc = _dot_nt(qh[h], kb) + bias[h:h + 1, :]
            if diagonal:
                sc = causal(sc)
            m_prev = m_sc[h]
            m_new = jnp.maximum(m_prev, jnp.max(sc, axis=1, keepdims=True))
            alpha = jnp.exp(m_prev - m_new)
            p = jnp.exp(sc - jnp.tile(m_new, (1, tq // LANE)))
            l_sc[h] = alpha * l_sc[h] + jnp.sum(p, axis=1, keepdims=True)
            acc_sc[h] = acc_sc[h] * jnp.tile(alpha, (1, BRANCH_W // LANE)) + _dot(p.astype(BF16), vb)
            m_sc[h] = m_new

    tops = [stat_ref[i * FOX_NSTAT + h] * stat_ref[i * FOX_NSTAT + 2 * N_HEADS + h] for h in range(N_HEADS)]
    c_tile = c_ref[...]
    shift = [c_tile[:, h:h + 1] - c_q0[h:h + 1, :] - tops[h] for h in range(N_HEADS)]

    def fixed_block(s0, diagonal):
        kb = k_ref[pl.ds(s0, tq), :]
        vb = v_ref[pl.ds(s0, tq), :]
        bias = c_q0 - ct_ref[:, pl.ds(s0, tq)]
        for h in range(N_HEADS):
            sc = _dot_nt(qh[h], kb) + bias[h:h + 1, :] + shift[h]
            if diagonal:
                sc = causal(sc)
            p = jnp.exp(sc)
            l_sc[h] += p[:, :LANE] + p[:, LANE:]
            acc_sc[h] += _dot(p.astype(BF16), vb)

    def run(block, row_sum):
        lax.fori_loop(first, i, lambda j, carry: (block(pl.multiple_of(j * tq, tq), False), carry)[1], 0)
        block(q0, True)
        out = jnp.zeros((tq, BRANCH_W), F32)
        for h in range(N_HEADS):
            out = out + acc_sc[h] * hm[h:h + 1, :] / row_sum(l_sc[h])
        o_ref[...] = out.astype(o_ref.dtype)

    def run_online():
        m_sc[...] = jnp.full_like(m_sc, -jnp.inf)
        run(online_block, lambda l: jnp.tile(l, (1, BRANCH_W // LANE)))

    def run_fixed():
        run(fixed_block, lambda l: jnp.sum(l, axis=1, keepdims=True))

    lax.cond(functools.reduce(jnp.maximum, tops) <= FOX_FIXED_MAX, run_fixed, run_online)


def fox_attention(proj, ct, c_rows, stats):
    s = proj.shape[0]
    tq = min(FOX_TILE, s)
    full = lambda j: pl.BlockSpec((s, BRANCH_W), lambda i, j=j: (0, j), pipeline_mode=pl.Buffered(1))
    return pl.pallas_call(
        functools.partial(_fox_kernel, tq=tq),
        grid=(s // tq,),
        in_specs=[pl.BlockSpec(memory_space=pltpu.SMEM),
                  pl.BlockSpec((tq, BRANCH_W), lambda i: (i, 4)), full(5), full(6),
                  _const_spec((SUBLANE, s)), pl.BlockSpec((tq, LANE), lambda i: (i, 0)),
                  _const_spec((N_HEADS, BRANCH_W))],
        out_specs=pl.BlockSpec((tq, BRANCH_W), lambda i: (i, 0)),
        out_shape=jax.ShapeDtypeStruct((s, BRANCH_W), BF16),
        scratch_shapes=[pltpu.VMEM((N_HEADS, tq, LANE), F32),
                        pltpu.VMEM((N_HEADS, tq, LANE), F32),
                        pltpu.VMEM((N_HEADS, tq, BRANCH_W), F32)],
        compiler_params=_params("parallel"),
        name="fox_attention",
    )(stats, proj, proj, proj, ct, c_rows, _head_masks())


def _pool_kernel(u_ref, w_ref, scale_ref, o_ref, ext, *, tile):
    i = pl.program_id(0)

    @pl.when(i == 0)
    def _():
        ext[pl.ds(0, POOL_HALO), :] = jnp.zeros((POOL_HALO, BRANCH_W), F32)

    u = u_ref[...].astype(F32)
    ext[pl.ds(POOL_HALO, tile), :] = u
    pos = (i * tile + lax.broadcasted_iota(jnp.int32, (tile, 1), 0) + 1).astype(F32)
    halves = []
    for half in range(BRANCH_W // LANE):
        lanes = pl.ds(half * LANE, LANE)
        w_small, w_big = POOL_WINDOWS[2 * half], POOL_WINDOWS[2 * half + 1]
        run = u[:, half * LANE:(half + 1) * LANE]
        sums = {}
        for j in range(1, w_big):
            if j == w_small:
                sums[w_small] = run
            run = run + ext[pl.ds(POOL_HALO - j, tile), lanes]
        sums[w_big] = run
        lane = lax.broadcasted_iota(jnp.int32, (1, LANE), 1)
        small = lane < POOL_GROUP
        total = jnp.where(small, sums[w_small], sums[w_big])
        count = jnp.where(small, jnp.minimum(pos, float(w_small)), jnp.minimum(pos, float(w_big)))
        halves.append(total / count)
    mean = jnp.concatenate(halves, axis=1)
    d = (mean - u).astype(BF16)
    y = _dot(d, w_ref[...]) * scale_ref[...]
    o_ref[...] = y.astype(o_ref.dtype)
    ext[pl.ds(0, POOL_HALO), :] = u[tile - POOL_HALO:, :]


def pool_mixer(proj, w_pool, scale, tile=2048):
    s = proj.shape[0]
    tile = min(tile, s)
    ng = len(POOL_WINDOWS)
    w_bd = jnp.zeros((BRANCH_W, BRANCH_W), F32)
    for gi in range(ng):
        lo = gi * POOL_GROUP
        w_bd = w_bd.at[lo:lo + POOL_GROUP, lo:lo + POOL_GROUP].set(w_pool[gi].astype(F32))
    return pl.pallas_call(
        functools.partial(_pool_kernel, tile=tile),
        grid=(s // tile,),
        in_specs=[pl.BlockSpec((tile, BRANCH_W), lambda i: (i, 7)),
                  _const_spec((BRANCH_W, BRANCH_W)), _const_spec((1, BRANCH_W))],
        out_specs=pl.BlockSpec((tile, BRANCH_W), lambda i: (i, 0)),
        out_shape=jax.ShapeDtypeStruct((s, BRANCH_W), BF16),
        scratch_shapes=[pltpu.VMEM((tile + POOL_HALO, BRANCH_W), F32)],
        compiler_params=_params("arbitrary"),
        name="pool_mixer",
    )(proj, w_bd.astype(BF16), scale.reshape(1, BRANCH_W).astype(F32))


def _ret_kernel(q_ref, k_ref, v_ref, g_ref, cos_ref, sin_ref, perm_ref, ones_ref, hm_ref,
                dstack_ref, xi_ref, zeta_ref, gc_ref, gng_ref, gnb_ref, o_ref, st_ref, *, tile):
    c = RET_CHUNK

    @pl.when(pl.program_id(0) == 0)
    def _():
        st_ref[...] = jnp.zeros_like(st_ref)

    perm = perm_ref[...]
    ones_bd = ones_ref[...]
    bd_mask = ones_bd.astype(F32)
    hm = hm_ref[...]

    cos = cos_ref[...]
    sin = sin_ref[...]
    q_all = q_ref[...]
    k_all = k_ref[...]
    qr_all = q_all.astype(F32) * cos + _dot(q_all, perm) * sin
    kr_all = (k_all.astype(F32) * cos + _dot(k_all, perm) * sin) * (HEAD_DIM ** -0.5)

    outs = []
    for ci in range(tile // c):
        r0 = ci * c
        qr = qr_all[r0:r0 + c, :]
        kr = kr_all[r0:r0 + c, :]
        v = v_ref[pl.ds(r0, c), :]

        qx = jnp.concatenate([qr * hm[h:h + 1, :] for h in range(N_HEADS)], axis=0).astype(BF16)
        sc = _dot_nt(qx, kr.astype(BF16)) * dstack_ref[...]
        r = _dot(sc.astype(BF16), v)
        intra = jnp.zeros((c, BRANCH_W), F32)
        for h in range(N_HEADS):
            intra = intra + r[h * c:(h + 1) * c, :] * hm[h:h + 1, :]

        st = st_ref[...]
        inter = _dot_nt((qr * xi_ref[...]).astype(BF16), st.astype(BF16))
        upd = _dot(v.astype(F32).T.astype(BF16), (kr * zeta_ref[...]).astype(BF16))
        st_ref[...] = st * gc_ref[...] + upd * bd_mask
        outs.append(intra + inter)

    o = jnp.concatenate(outs, axis=0)
    mu = _dot_exact_rhs(o, ones_bd, terms=2) * (1.0 / HEAD_DIM)
    cen = o - mu
    var = _dot_exact_rhs(cen * cen, ones_bd, terms=2) * (1.0 / HEAD_DIM)
    y = cen * lax.rsqrt(var + LN_EPS) * gng_ref[...] + gnb_ref[...]
    o_ref[...] = (y * _silu(g_ref[...].astype(F32))).astype(o_ref.dtype)


def _rope_tables(s):
    half = HEAD_DIM // 2
    pos = np.arange(s, dtype=np.float64)
    inv_freq = ROPE_BASE ** (-np.arange(half, dtype=np.float64) / half)
    lane = np.arange(BRANCH_W)
    ang = pos[:, None] * inv_freq[lane % half][None, :]
    sign = np.where(lane % HEAD_DIM < half, -1.0, 1.0)
    return _const(np.cos(ang)), _const(np.sin(ang) * sign[None, :])


def _ret_constants():
    c = RET_CHUNK
    half = HEAD_DIM // 2
    lane = np.arange(BRANCH_W)
    partner = np.where(lane % HEAD_DIM < half, lane + half, lane - half)
    perm = lane[:, None] == partner[None, :]
    log_gamma = np.log1p(-np.exp2(-RET_DECAY_BASE - np.arange(N_HEADS, dtype=np.float64)))
    ci = np.arange(c, dtype=np.float64)
    diff = ci[:, None] - ci[None, :]
    intra = np.where(diff >= 0, np.exp(diff * log_gamma[:, None, None]), 0.0)
    dstack = intra.reshape(N_HEADS * c, c)
    lg_lane = np.repeat(log_gamma, HEAD_DIM)[None, :]
    xi = np.exp((ci[:, None] + 1.0) * lg_lane)
    zeta = np.exp((c - 1.0 - ci[:, None]) * lg_lane)
    gc = np.exp(c * lg_lane)
    return _const(perm, BF16), _const(dstack), _const(xi), _const(zeta), _const(gc)


def retention(proj, gn_g, gn_b, tile=2048):
    s = proj.shape[0]
    c = RET_CHUNK
    tile = min(tile, s)
    cos_t, sin_t = _rope_tables(s)
    perm, dstack, xi, zeta, gc = _ret_constants()
    col = lambda j: pl.BlockSpec((tile, BRANCH_W), lambda i, j=j: (i, j))
    row = pl.BlockSpec((tile, BRANCH_W), lambda i: (i, 0))
    return pl.pallas_call(
        functools.partial(_ret_kernel, tile=tile),
        grid=(s // tile,),
        in_specs=[col(8), col(9), col(10), col(11), row, row,
                  _const_spec((BRANCH_W, BRANCH_W)), _const_spec((BRANCH_W, BRANCH_W)),
                  _const_spec((N_HEADS, BRANCH_W)), _const_spec((N_HEADS * c, c)),
                  _const_spec((c, BRANCH_W)), _const_spec((c, BRANCH_W)), _const_spec((1, BRANCH_W)),
                  _const_spec((1, BRANCH_W)), _const_spec((1, BRANCH_W))],
        out_specs=row,
        out_shape=jax.ShapeDtypeStruct((s, BRANCH_W), BF16),
        scratch_shapes=[pltpu.VMEM((BRANCH_W, BRANCH_W), F32)],
        compiler_params=_params("arbitrary"),
        name="retention",
    )(proj, proj, proj, proj, cos_t, sin_t, perm, _head_ones(), _head_masks(), dstack, xi, zeta, gc,
      gn_g.reshape(1, BRANCH_W).astype(F32), gn_b.reshape(1, BRANCH_W).astype(F32))


def _merge_kernel(h_ref, o0_ref, o1_ref, o2_ref, o3_ref, x_ref, wg_ref, wb_ref, wo_ref, g_ref, *rest):
    n_cast = (len(rest) - 2) // 2
    cast_in, (xo_ref, ho_ref), cast_out = rest[:n_cast], rest[n_cast:n_cast + 2], rest[n_cast + 2:]
    for src, dst in zip(cast_in, cast_out):
        dst[...] = src[...].astype(dst.dtype)
    h = h_ref[...]
    merged = jnp.zeros(x_ref.shape, F32)
    for bi, o_ref in enumerate((o0_ref, o1_ref, o2_ref, o3_ref)):
        gate = _sigmoid(_dot(h, wg_ref[:, bi * D_MODEL:(bi + 1) * D_MODEL]))
        merged = merged + gate * _dot(o_ref[...], wb_ref[bi])
    x_new = x_ref[...] + _dot(merged.astype(BF16), wo_ref[...])
    xo_ref[...] = x_new
    ho_ref[...] = _rms(x_new, g_ref[...]).astype(ho_ref.dtype)


def merge(h, branches, x, w_gate, w_branch, w_out, next_gain, tm=1024, cast=()):
    s, d = x.shape
    tm = min(tm, s)
    steps = s // tm
    row = lambda w: pl.BlockSpec((tm, w), lambda i: (i, 0))
    flat = [c.reshape(-1, c.shape[-1]) for c in cast]
    slab = lambda c: pl.BlockSpec((c.shape[0] // steps, c.shape[1]), lambda i: (i, 0))
    outs = pl.pallas_call(
        _merge_kernel,
        grid=(steps,),
        in_specs=[row(d), row(BRANCH_W), row(BRANCH_W), row(BRANCH_W), row(BRANCH_W), row(d),
                  _const_spec((d, N_BRANCH * d)), _const_spec((N_BRANCH, BRANCH_W, d)),
                  _const_spec((d, d)), _const_spec((1, d))] + [slab(c) for c in flat],
        out_specs=[row(d), row(d)] + [slab(c) for c in flat],
        out_shape=[jax.ShapeDtypeStruct((s, d), F32), jax.ShapeDtypeStruct((s, d), BF16)]
        + [jax.ShapeDtypeStruct(c.shape, BF16) for c in flat],
        compiler_params=_params("parallel"),
        name="merge",
    )(h, *branches, x, w_gate, w_branch, w_out, next_gain.reshape(1, d).astype(F32), *flat)
    return outs[0], outs[1], [o.reshape(c.shape) for o, c in zip(outs[2:], cast)]


def _ffn_kernel(h_ref, x_ref, wg_ref, wu_ref, wd_ref, g_ref, xo_ref, ho_ref, acc_ref):
    f = pl.program_id(1)

    @pl.when(f == 0)
    def _():
        acc_ref[...] = jnp.zeros_like(acc_ref)

    h = h_ref[...]
    a = _silu(_dot(h, wg_ref[...].astype(BF16))) * _dot(h, wu_ref[...].astype(BF16))
    acc_ref[...] += _dot(a.astype(BF16), wd_ref[...].astype(BF16))

    @pl.when(f == pl.num_programs(1) - 1)
    def _():
        x_new = x_ref[...] + acc_ref[...]
        xo_ref[...] = x_new
        ho_ref[...] = _rms(x_new, g_ref[...]).astype(ho_ref.dtype)


def ffn_dense(h, x, w_gate, w_up, w_down, next_gain, tm=1024, tf=512):
    s, d = x.shape
    tm = min(tm, s)
    dff = w_gate.shape[1]
    row = lambda: pl.BlockSpec((tm, d), lambda i, f: (i, 0))
    return pl.pallas_call(
        _ffn_kernel,
        grid=(s // tm, dff // tf),
        in_specs=[row(), row(),
                  pl.BlockSpec((d, tf), lambda i, f: (0, f)),
                  pl.BlockSpec((d, tf), lambda i, f: (0, f)),
                  pl.BlockSpec((tf, d), lambda i, f: (f, 0)),
                  _const_spec((1, d))],
        out_specs=[row(), row()],
        out_shape=[jax.ShapeDtypeStruct((s, d), F32), jax.ShapeDtypeStruct((s, d), BF16)],
        scratch_shapes=[pltpu.VMEM((tm, d), F32)],
        compiler_params=_params("parallel", "arbitrary"),
        name="ffn_dense",
    )(h, x, w_gate, w_up, w_down, next_gain.reshape(1, d).astype(F32))


MERGE_CAST_TILE = 256
MOE_TOK_TILE = 256
MOE_ROUTE_TILES = 4
MOE_ROW_BLOCK = 512
MOE_GATHER_ROWS = 128
MOE_GATHER_TILES = 4
MOE_VMEM_LIMIT = 60 * 1024 * 1024
MOE_Y_BLOCK = 128
MOE_Y_FETCH = MOE_TOK_TILE // MOE_Y_BLOCK + 1


def _router_logits(x, w):
    xh, xm, _ = _split3(x)
    wh, wm, _ = _split3(w)
    packed = (wh.astype(F32) + pltpu.roll(wm.astype(F32), N_EXPERTS, axis=1)).astype(BF16)
    s = _dot(xh, packed) + _dot(xm, packed)
    return s + pltpu.roll(s, LANE - N_EXPERTS, axis=1)


def _route_kernel(x_ref, ng_ref, router_ref, ltri_ref, pos_ref, gate_ref, post_ref, before_ref, total_ref,
                  carry_ref):
    tm = x_ref.shape[0]

    @pl.when(pl.program_id(0) == 0)
    def _():
        carry_ref[...] = jnp.zeros_like(carry_ref)

    hn = _rms(x_ref[...], ng_ref[...])
    logits = _router_logits(hn, router_ref[...])
    lane = lax.broadcasted_iota(jnp.int32, (tm, LANE), 1)
    logits = jnp.where(lane < N_EXPERTS, logits, -jnp.inf)
    v1 = jnp.max(logits, axis=1, keepdims=True)
    i1 = jnp.min(jnp.where(logits == v1, lane, LANE), axis=1, keepdims=True)
    rest = jnp.where(lane == i1, -jnp.inf, logits)
    v2 = jnp.max(rest, axis=1, keepdims=True)
    i2 = jnp.min(jnp.where(rest == v2, lane, LANE), axis=1, keepdims=True)
    w1 = 1.0 / (1.0 + jnp.exp(v2 - v1))
    gate_ref[...] = jnp.where(lane == i1, w1, 0.0) + jnp.where(lane == i2, 1.0 - w1, 0.0)

    member = jnp.where((lane == i1) | (lane == i2), 1.0, 0.0)
    carry = carry_ref[...]
    ranks = []
    for blk in range(tm // MOE_TOK_TILE):
        before_ref[blk] = carry
        mb = member[blk * MOE_TOK_TILE:(blk + 1) * MOE_TOK_TILE]
        ranks.append(_dot(ltri_ref[...], mb.astype(BF16)) + carry)
        carry = carry + jnp.sum(mb, axis=0, keepdims=True)
    pos = jnp.where(member > 0.0, jnp.concatenate(ranks, axis=0), -1.0)
    pos_ref[...] = pos
    post_ref[...] = pos.T[:SUBLANE, :]
    carry_ref[...] = carry
    total_ref[...] = carry


def moe_route(x, norm_gain, router):
    s, d = x.shape
    tm = min(MOE_ROUTE_TILES * MOE_TOK_TILE, s)
    nsub = tm // MOE_TOK_TILE
    nt = s // tm
    router_p = jnp.zeros((d, LANE), F32).at[:, :N_EXPERTS].set(router.astype(F32))
    tt = min(MOE_TOK_TILE, s)
    ltri = _const(np.tril(np.ones((tt, tt)), -1), BF16)
    row = pl.BlockSpec((tm, LANE), lambda i: (i, 0))
    return pl.pallas_call(
        _route_kernel,
        grid=(nt,),
        in_specs=[pl.BlockSpec((tm, d), lambda i: (i, 0)), _const_spec((1, d)), _const_spec((d, LANE)),
                  _const_spec((tt, tt))],
        out_specs=[row, row, pl.BlockSpec((SUBLANE, tm), lambda i: (0, i)),
                   pl.BlockSpec((nsub, 1, LANE), lambda i: (i, 0, 0)), pl.BlockSpec((1, LANE), lambda i: (0, 0))],
        out_shape=[jax.ShapeDtypeStruct((s, LANE), F32), jax.ShapeDtypeStruct((s, LANE), F32),
                   jax.ShapeDtypeStruct((SUBLANE, s), F32), jax.ShapeDtypeStruct((nt * nsub, 1, LANE), F32),
                   jax.ShapeDtypeStruct((1, LANE), F32)],
        scratch_shapes=[pltpu.VMEM((1, LANE), F32)],
        compiler_params=_params("arbitrary"),
        name="moe_route",
    )(x, norm_gain.reshape(1, d).astype(F32), router_p, ltri)


def _moe_ffn_kernel(be_ref, r0_ref, tlo_ref, thi_ref, nv_ref, h_ref, post_ref, wg_ref, wu_ref, wd_ref,
                    y_ref, x_sc, acc_sc, *, tt, nsub):
    b = pl.program_id(0)
    f = pl.program_id(1)
    nb = pl.num_programs(0)
    valid = b < nv_ref[0]
    tmb, d = acc_sc.shape
    gr = tmb // nsub
    last_tile = h_ref.shape[0] // tt - 1

    def picked(e, want, t):
        t0 = pl.multiple_of(t * tt, tt)
        p = post_ref[pl.ds(e, 1), pl.ds(t0, tt)]
        sel = jnp.where(p == want, 1.0, 0.0).astype(BF16)
        return _dot(sel, h_ref[pl.ds(t0, tt), :])

    def gather_head(blk, sb):
        e = be_ref[blk]
        lo = tlo_ref[blk * nsub + sb]
        hi = thi_ref[blk * nsub + sb]
        want = (lax.broadcasted_iota(jnp.int32, (gr, 1), 0) + (r0_ref[blk] + sb * gr)).astype(F32)
        rows = picked(e, want, lo)
        for k in range(1, MOE_GATHER_TILES):
            rows = rows + picked(e, jnp.where(lo + k <= hi, want, -2.0), jnp.minimum(lo + k, last_tile))
        return rows.astype(BF16)

    def gather_tail(slot, blk, sb):
        e = be_ref[blk]
        want = (lax.broadcasted_iota(jnp.int32, (gr, 1), 0) + (r0_ref[blk] + sb * gr)).astype(F32)
        rows = pl.ds(pl.multiple_of(sb * gr, gr), gr)

        def more(t, carry):
            x_sc[slot, rows, :] = (x_sc[slot, rows, :].astype(F32) + picked(e, want, t)).astype(BF16)
            return carry

        lax.fori_loop(tlo_ref[blk * nsub + sb] + MOE_GATHER_TILES, thi_ref[blk * nsub + sb] + 1, more, 0)

    @pl.when((b == 0) & (f == 0))
    def _():
        acc_sc[...] = jnp.zeros_like(acc_sc)
        for sb in range(nsub):
            x_sc[0, pl.ds(sb * gr, gr), :] = gather_head(0, sb)
            gather_tail(0, 0, sb)

    nxt = jnp.minimum(b + 1, nb - 1)
    nslot = (b + 1) % 2

    @pl.when(valid)
    def _():
        x_sc[nslot, pl.ds(pl.multiple_of(f * gr, gr), gr), :] = gather_head(nxt, f)
        xb = x_sc[b % 2]
        a = _silu(_dot(xb, wg_ref[0])) * _dot(xb, wu_ref[0])
        acc_sc[...] = jnp.where(f == 0, 0.0, acc_sc[...]) + _dot(a.astype(BF16), wd_ref[0].astype(BF16))

    @pl.when(valid & (thi_ref[nxt * nsub + f] - tlo_ref[nxt * nsub + f] >= MOE_GATHER_TILES))
    def _():
        gather_tail(nslot, nxt, f)

    @pl.when(f == nsub - 1)
    def _():
        y_ref[...] = jnp.where(valid, acc_sc[...], 0.0).astype(y_ref.dtype)


def moe_ffn(h, post, sched, w_gate, w_up, w_down):
    s, d = h.shape
    ne, _, dff = w_gate.shape
    tmb = MOE_ROW_BLOCK
    tt = min(MOE_TOK_TILE, s)
    nb = sched[0].shape[0]
    nf = tmb // MOE_GATHER_ROWS
    tf = dff // nf

    def fidx(b, f, nv):
        return jnp.where(b < nv[0], f, nf - 1)

    grid_spec = pltpu.PrefetchScalarGridSpec(
        num_scalar_prefetch=5,
        grid=(nb, nf),
        in_specs=[pl.BlockSpec((s, d), lambda b, f, *_: (0, 0), pipeline_mode=pl.Buffered(1)),
                  pl.BlockSpec((SUBLANE, s), lambda b, f, *_: (0, 0), pipeline_mode=pl.Buffered(1)),
                  pl.BlockSpec((1, d, tf), lambda b, f, be, r0, tlo, thi, nv: (be[b], 0, fidx(b, f, nv))),
                  pl.BlockSpec((1, d, tf), lambda b, f, be, r0, tlo, thi, nv: (be[b], 0, fidx(b, f, nv))),
                  pl.BlockSpec((1, tf, d), lambda b, f, be, r0, tlo, thi, nv: (be[b], fidx(b, f, nv), 0))],
        out_specs=pl.BlockSpec((tmb, d), lambda b, f, *_: (b, 0)),
        scratch_shapes=[pltpu.VMEM((2, tmb, d), BF16), pltpu.VMEM((tmb, d), F32)],
    )
    return pl.pallas_call(
        functools.partial(_moe_ffn_kernel, tt=tt, nsub=nf),
        grid_spec=grid_spec,
        out_shape=jax.ShapeDtypeStruct((nb * tmb, d), BF16),
        compiler_params=pltpu.CompilerParams(dimension_semantics=("arbitrary", "arbitrary"),
                                             vmem_limit_bytes=MOE_VMEM_LIMIT),
        name="moe_ffn",
    )(*sched, h, post, w_gate, w_up, w_down)


def _moe_combine_kernel(kb_ref, off_ref, lim_ref, x_ref, pos_ref, gate_ref, fg_ref, *rest):
    y_refs, o_ref, acc_sc = rest[:-2], rest[-2], rest[-1]
    t = pl.program_id(0)
    yb = y_refs[0].shape[0]
    pos = pos_ref[...]
    gate = gate_ref[...]
    col = lax.broadcasted_iota(jnp.int32, (1, yb), 1).astype(F32)

    def routed(e):
        pe = pos[:, e:e + 1]
        ge = gate[:, e:e + 1]
        r = jnp.where(pe >= 0.0, pe + off_ref[t * N_EXPERTS + e].astype(F32), -1.0)
        return r, ge

    def picked(e, k, r):
        sel = jnp.where(r == col + float(k * yb), 1.0, 0.0).astype(BF16)
        return _dot(sel, y_refs[MOE_Y_FETCH * e + k][...])

    col2 = lax.broadcasted_iota(jnp.int32, (1, 2 * yb), 1).astype(F32)
    acc = x_ref[...]
    for e in range(N_EXPERTS):
        r, ge = routed(e)
        pair = jnp.concatenate([y_refs[MOE_Y_FETCH * e][...], y_refs[MOE_Y_FETCH * e + 1][...]], axis=0)
        acc = acc + ge * _dot(jnp.where(r == col2, 1.0, 0.0).astype(BF16), pair)
    acc_sc[...] = acc
    for e in range(N_EXPERTS):
        for k in range(2, MOE_Y_FETCH):
            @pl.when(lim_ref[t * N_EXPERTS + e] > k * yb)
            def _(k=k, e=e):
                r, ge = routed(e)
                acc_sc[...] += ge * picked(e, k, r)
    o_ref[...] = _rms(acc_sc[...], fg_ref[...]).astype(o_ref.dtype)


def moe_combine(x, pos, gate, y, kb, off, lim, final_gain):
    s, d = x.shape
    tm = min(MOE_TOK_TILE, s)
    yb = MOE_Y_BLOCK
    last = y.shape[0] // yb - 1

    def yspec(e, k):
        def index(t, kb_r, off_r, lim_r):
            blk = jnp.minimum(kb_r[t * N_EXPERTS + e] + k, last)
            return (blk if k < 1 else jnp.where(lim_r[t * N_EXPERTS + e] > k * yb, blk, 0), 0)
        return pl.BlockSpec((yb, d), index)

    grid_spec = pltpu.PrefetchScalarGridSpec(
        num_scalar_prefetch=3,
        grid=(s // tm,),
        in_specs=[pl.BlockSpec((tm, d), lambda t, *_: (t, 0)),
                  pl.BlockSpec((tm, LANE), lambda t, *_: (t, 0)),
                  pl.BlockSpec((tm, LANE), lambda t, *_: (t, 0)),
                  pl.BlockSpec((1, d), lambda t, *_: (0, 0))]
        + [yspec(e, k) for e in range(N_EXPERTS) for k in range(MOE_Y_FETCH)],
        out_specs=pl.BlockSpec((tm, d), lambda t, *_: (t, 0)),
        scratch_shapes=[pltpu.VMEM((tm, d), F32)],
    )
    return pl.pallas_call(
        _moe_combine_kernel,
        grid_spec=grid_spec,
        out_shape=jax.ShapeDtypeStruct((s, d), F32),
        compiler_params=_params("arbitrary"),
        name="moe_combine",
    )(kb, off, lim, x, pos, gate, final_gain.reshape(1, d).astype(F32), *([y] * (MOE_Y_FETCH * N_EXPERTS)))


def _moe_schedule(before, total, s):
    tmb, yb, gr = MOE_ROW_BLOCK, MOE_Y_BLOCK, MOE_GATHER_ROWS
    nb = 2 * s // tmb + N_EXPERTS
    counts = total[0, :N_EXPERTS].astype(jnp.int32)
    nblk = (counts + tmb - 1) // tmb
    end = jnp.cumsum(nblk)
    first = end - nblk
    nvalid = end[-1]
    b = jnp.minimum(jnp.arange(nb, dtype=jnp.int32), nvalid - 1)
    blk_e = jnp.sum(b[:, None] >= end[None, :], axis=1).astype(jnp.int32)
    r0 = (b - first[blk_e]) * tmb
    cb = before[:, 0, :N_EXPERTS].astype(jnp.int32)
    r0s = (r0[:, None] + gr * jnp.arange(tmb // gr, dtype=jnp.int32)[None, :]).reshape(-1)
    cbe = cb[:, jnp.repeat(blk_e, tmb // gr)]
    tlo = (jnp.sum(cbe <= r0s[None, :], axis=0) - 1).astype(jnp.int32)
    thi = (jnp.sum(cbe < (r0s + gr)[None, :], axis=0) - 1).astype(jnp.int32)
    sched = (blk_e, r0.astype(jnp.int32), tlo, thi, nvalid.reshape(1).astype(jnp.int32))
    row_start = first[None, :] * tmb + cb
    kb = row_start // yb
    off = first[None, :] * tmb - kb * yb
    n_te = jnp.concatenate([cb[1:], counts[None, :]], axis=0) - cb
    lim = row_start - kb * yb + n_te
    flat = lambda a: a.reshape(-1).astype(jnp.int32)
    return sched, flat(kb), flat(off), flat(lim)


def moe_sparse(h, x, norm_gain, router, w_gate, w_up, w_down, final_gain):
    s, _ = x.shape
    pos, gate, post, before, total = moe_route(x, norm_gain, router)
    sched, kb, off, lim = _moe_schedule(before, total, s)
    y = moe_ffn(h, post, sched, w_gate, w_up, w_down)
    return moe_combine(x, pos, gate, y, kb, off, lim, final_gain)


FOX_F0 = 7 * BRANCH_W


def _split_w_in_kernel(wt_ref, mix_ref, f_ref, gate_ref):
    rest0 = FOX_F0 + N_HEADS
    gate0 = N_MIX_COLS + N_HEADS
    mix_ref[:, :FOX_F0] = wt_ref[pl.ds(0, FOX_F0), :].T.astype(BF16)
    mix_ref[:, FOX_F0:] = wt_ref[pl.ds(rest0, N_MIX_COLS - FOX_F0), :].T.astype(BF16)
    gate_ref[...] = wt_ref[pl.ds(gate0, N_BRANCH * D_MODEL), :].T.astype(BF16)
    f_rows = jnp.concatenate([wt_ref[pl.ds(FOX_F0, N_HEADS), :], jnp.zeros((LANE - N_HEADS, LANE), F32)], axis=0)
    f_ref[...] = f_rows.T.astype(BF16)


def _mixer_weights(w_in, layer, tr=LANE):
    _, d, cols = w_in.shape
    w_t = jnp.swapaxes(w_in, 1, 2)
    row = lambda w: pl.BlockSpec((tr, w), lambda i: (i, 0))
    return pl.pallas_call(
        _split_w_in_kernel,
        grid=(d // tr,),
        in_specs=[pl.BlockSpec((None, cols, tr), lambda i: (layer, 0, i))],
        out_specs=[row(N_MIX_COLS), row(LANE), row(N_BRANCH * D_MODEL)],
        out_shape=[jax.ShapeDtypeStruct((d, N_MIX_COLS), BF16), jax.ShapeDtypeStruct((d, LANE), BF16),
                   jax.ShapeDtypeStruct((d, N_BRANCH * D_MODEL), BF16)],
        compiler_params=_params("parallel"),
        name="split_w_in",
    )(w_t)


def kernel(x, w_in, w_branch, w_out, norm_mix_g, hgrn_lb_logits, hgrn_norm_g, fox_f_bias, pool_w, pool_scale,
           ret_gn_g, ret_gn_b, norm_ffn_g, ffn_w_gate, ffn_w_up, ffn_w_down, moe_router, moe_w_gate, moe_w_up,
           moe_w_down, final_norm_g):
    b, s, d = x.shape
    assert b == 1 and d == D_MODEL
    depth = w_in.shape[0]
    assert depth == 2, "layer 0 uses the dense FFN, layer 1 the experts and the final norm"
    xs = x.reshape(s, d)
    h = None
    out = None
    for layer in range(depth):
        w_mix, w_f, w_gate = _mixer_weights(w_in, layer)
        if layer == 0:
            proj, h = norm_matmul(xs, norm_mix_g[0], w_mix)
        else:
            proj = matmul(h, w_mix)
        ct, c_rows, fox_stats = fox_gate(h, proj, w_f, fox_f_bias[layer])
        branches = (
            hgrn2(proj, hgrn_lb_logits, hgrn_norm_g[layer], layer),
            fox_attention(proj, ct, c_rows, fox_stats),
            pool_mixer(proj, pool_w[layer], pool_scale[layer]),
            retention(proj, ret_gn_g[layer], ret_gn_b[layer]),
        )
        li = layer // 2
        if layer % 2 == 0:
            xs, h2, (ffn_wg, ffn_wu) = merge(h, branches, xs, w_gate, w_branch[layer].astype(BF16),
                                             w_out[layer].astype(BF16), norm_ffn_g[layer],
                                             cast=(ffn_w_gate[li], ffn_w_up[li]))
            xs, h = ffn_dense(h2, xs, ffn_wg, ffn_wu, ffn_w_down[li], norm_mix_g[layer + 1])
        else:
            xs, h2, (moe_wg, moe_wu) = merge(h, branches, xs, w_gate, w_branch[layer].astype(BF16),
                                             w_out[layer].astype(BF16), norm_ffn_g[layer], tm=MERGE_CAST_TILE,
                                             cast=(moe_w_gate[li], moe_w_up[li]))
            out = moe_sparse(h2, xs, norm_ffn_g[layer], moe_router[li], moe_wg, moe_wu, moe_w_down[li],
                             final_norm_g)
    return out.reshape(b, s, d)
```

```python
import functools
import math

import jax
import jax.numpy as jnp
import numpy as np
from jax import lax
from jax.experimental import pallas as pl
from jax.experimental.pallas import tpu as pltpu

D_MODEL = 1024
N_BRANCH = 4
BRANCH_W = D_MODEL // N_BRANCH
HEAD_DIM = 64
N_HEADS = BRANCH_W // HEAD_DIM
POOL_WINDOWS = (2, 4, 8, 16)
POOL_GROUP = BRANCH_W // len(POOL_WINDOWS)
POOL_HALO = 16
RET_DECAY_BASE = 5.0
ROPE_BASE = 10000.0
N_EXPERTS = 8
RMS_EPS = 1e-6
LN_EPS = 1e-5
N_MIX_COLS = 12 * BRANCH_W

LANE = 128
SUBLANE = 8
VMEM_LIMIT = 56 * 1024 * 1024

HG_CHUNK = 64
HG_SUB = 16
HG_FAST_MIN_LOGDECAY = -60.0
RET_CHUNK = 256

F32 = jnp.float32
BF16 = jnp.bfloat16
NT_DIMS = (((1,), (1,)), ((), ()))


def _params(*sem):
    return pltpu.CompilerParams(dimension_semantics=sem, vmem_limit_bytes=VMEM_LIMIT)


def _const_spec(shape):
    nd = len(shape)
    return pl.BlockSpec(shape, lambda *_: (0,) * nd, pipeline_mode=pl.Buffered(1))


def _split3(x):
    hi = x.astype(BF16)
    r1 = x - hi.astype(F32)
    mid = r1.astype(BF16)
    lo = (r1 - mid.astype(F32)).astype(BF16)
    return hi, mid, lo


def _dot(a, b):
    return jnp.dot(a, b, preferred_element_type=F32)


def _dot_nt(a, b):
    return lax.dot_general(a, b, NT_DIMS, preferred_element_type=F32)


def _dot_exact_rhs(x, m_bf16, terms=3):
    return sum(_dot(part, m_bf16) for part in _split3(x)[:terms])


def _dot_exact_lhs(m_bf16, x, terms=3):
    return sum(_dot(m_bf16, part) for part in _split3(x)[:terms])


def _sigmoid(x):
    return 1.0 / (1.0 + jnp.exp(-x))


def _silu(x):
    return x * _sigmoid(x)


def _rms(x, gain):
    return x * lax.rsqrt(jnp.mean(x * x, axis=-1, keepdims=True) + RMS_EPS) * gain


def _const(a, dtype=F32):
    return jnp.asarray(np.asarray(a, np.float32), dtype)


def _head_of(n):
    return np.arange(n) // HEAD_DIM


def _head_ones():
    h = _head_of(BRANCH_W)
    return _const(h[:, None] == h[None, :], BF16)


def _head_masks():
    return _const(_head_of(BRANCH_W)[None, :] == np.arange(N_HEADS)[:, None])


def _norm_matmul_kernel(x_ref, g_ref, b_ref, o_ref, h_ref):
    @pl.when(pl.program_id(1) == 0)
    def _():
        h_ref[...] = _rms(x_ref[...], g_ref[...]).astype(h_ref.dtype)

    o_ref[...] = _dot(h_ref[...], b_ref[...]).astype(o_ref.dtype)


def norm_matmul(x, gain, b, tm=2048, tn=1024):
    m, k = x.shape
    _, n = b.shape
    tm = min(tm, m)
    return pl.pallas_call(
        _norm_matmul_kernel,
        grid=(m // tm, n // tn),
        in_specs=[pl.BlockSpec((tm, k), lambda i, j: (i, 0)), _const_spec((1, k)),
                  pl.BlockSpec((k, tn), lambda i, j: (0, j))],
        out_specs=[pl.BlockSpec((tm, tn), lambda i, j: (i, j)), pl.BlockSpec((tm, k), lambda i, j: (i, 0))],
        out_shape=[jax.ShapeDtypeStruct((m, n), BF16), jax.ShapeDtypeStruct((m, k), BF16)],
        compiler_params=_params("arbitrary", "arbitrary"),
        name="in_proj_norm",
    )(x, gain.reshape(1, k).astype(F32), b)


def _matmul_kernel(a_ref, b_ref, o_ref):
    o_ref[...] = _dot(a_ref[...], b_ref[...]).astype(o_ref.dtype)


def matmul(a, b, out_dtype=BF16, tm=2048, tn=1024):
    m, k = a.shape
    _, n = b.shape
    tm = min(tm, m)
    return pl.pallas_call(
        _matmul_kernel,
        grid=(n // tn, m // tm),
        in_specs=[pl.BlockSpec((tm, k), lambda j, i: (i, 0)),
                  pl.BlockSpec((k, tn), lambda j, i: (0, j))],
        out_specs=pl.BlockSpec((tm, tn), lambda j, i: (i, j)),
        out_shape=jax.ShapeDtypeStruct((m, n), out_dtype),
        compiler_params=_params("parallel", "parallel"),
        name="in_proj",
    )(a, b)


def _hgrn_kernel(q_ref, f_ref, i_ref, g_ref, lbl_ref, ng_ref, ones_ref, hm_ref, tril_ref, halfsum_ref, fmask_ref,
                 o_ref, st_ref, bpad, kpad, vpad, astack, lf_sc, kk_sc, o_sc, *, layer, tile):
    c, sub = HG_CHUNK, HG_SUB
    nsub = c // sub
    half = c // 2

    @pl.when(pl.program_id(0) == 0)
    def _():
        st_ref[...] = jnp.zeros_like(st_ref)
        bpad[...] = jnp.zeros_like(bpad)
        kpad[...] = jnp.zeros_like(kpad)
        vpad[...] = jnp.zeros_like(vpad)

    lbl = lbl_ref[...]
    e = jnp.exp(lbl - jnp.max(lbl, axis=0, keepdims=True))
    p = e / jnp.sum(e, axis=0, keepdims=True)
    lb = jnp.zeros((1, BRANCH_W), F32)
    for l in range(1, layer + 1):
        lb = lb + p[l:l + 1, :]

    ones_bd = ones_ref[...]
    hm = hm_ref[...]
    tril = tril_ref[...]
    row = lax.broadcasted_iota(jnp.int32, (c, 1), 0)
    row_in_sub = row % sub
    bd_mask = ones_bd.astype(F32)

    sig = _sigmoid(f_ref[...].astype(F32))
    logf_all = jnp.log(lb + (1.0 - lb) * sig)
    lf_sc[...] = _dot_exact_lhs(tril, logf_all, terms=2)
    kk_sc[...] = (1.0 - lb) * (1.0 - sig)
    min_decay = jnp.min(_dot(halfsum_ref[...], logf_all.astype(BF16)))

    def load(ci):
        r0 = pl.multiple_of(ci * c, c)
        q = q_ref[pl.ds(r0, c), :].astype(F32)
        v = i_ref[pl.ds(r0, c), :].astype(F32)
        kk = kk_sc[pl.ds(r0, c), :]
        b = lf_sc[pl.ds(r0, c), :]
        return r0, q, v, kk, b

    def finish(r0, q_decayed, v, kk, b, intra):
        st = st_ref[...]
        inter = _dot_nt(q_decayed.astype(BF16), st.astype(BF16))
        b_last = b[c - 1:c, :]
        ks_end = (kk * jnp.exp(b_last - b)).astype(BF16)
        upd = _dot(v.T.astype(BF16), ks_end)
        st_ref[...] = st * jnp.exp(b_last) + upd * bd_mask
        o_sc[pl.ds(r0, c), :] = intra + inter

    def fast_chunk(ci, carry):
        r0, q, v, kk, b = load(ci)
        second = row >= half
        m_row = b[half - 1:half, :]
        mref = jnp.where(second, m_row, 0.0)
        qp = q * jnp.exp(b - mref)
        kp = kk * jnp.exp(mref - b)
        e_m = jnp.exp(m_row)
        kaug = jnp.concatenate([kp, kp[:half, :] * e_m], axis=0)
        vaug = jnp.concatenate([v, v[:half, :]], axis=0)
        k_heads = jnp.concatenate([kaug * hm[h:h + 1, :] for h in range(N_HEADS)], axis=0).astype(BF16)
        v_heads = jnp.concatenate([vaug * hm[h:h + 1, :] for h in range(N_HEADS)], axis=0).astype(BF16)
        sc = jnp.where(fmask_ref[...] > 0.0, _dot_nt(qp.astype(BF16), k_heads), 0.0)
        intra = _dot(sc.astype(BF16), v_heads)
        finish(r0, jnp.where(second, qp * e_m, qp), v, kk, b, intra)
        return carry

    def exact_chunk(ci, carry):
        r0, q, v, kk, b = load(ci)

        bpad[pl.ds(sub, c), :] = b
        kpad[pl.ds(sub, c), :] = kk
        vpad[pl.ds(sub, c), :] = v

        for d in range(sub):
            b_d = bpad[pl.ds(sub - d, c), :]
            k_d = kpad[pl.ds(sub - d, c), :]
            a = jnp.where(row_in_sub >= d, q * k_d * jnp.exp(b - b_d), 0.0)
            astack[pl.ds(d * c, c), :] = a.astype(BF16)
        pall = _dot(astack[...], ones_bd)
        intra = jnp.zeros((c, BRANCH_W), F32)
        for d in range(sub):
            intra = intra + pall[d * c:(d + 1) * c, :] * vpad[pl.ds(sub - d, c), :]

        pieces = [jnp.zeros((sub, BRANCH_W), F32)]
        for si in range(1, nsub):
            lo = si * sub
            m_i = b[lo - 1:lo, :]
            qs = q[lo:lo + sub, :] * jnp.exp(b[lo:lo + sub, :] - m_i)
            ks = (kk[:lo, :] * jnp.exp(m_i - b[:lo, :])).astype(BF16)
            qx = jnp.concatenate([qs * hm[h:h + 1, :] for h in range(N_HEADS)], axis=0).astype(BF16)
            sc = _dot_nt(qx, ks)
            r = _dot(sc.astype(BF16), v[:lo, :].astype(BF16))
            acc = jnp.zeros((sub, BRANCH_W), F32)
            for h in range(N_HEADS):
                acc = acc + r[h * sub:(h + 1) * sub, :] * hm[h:h + 1, :]
            pieces.append(acc)
        intra = intra + jnp.concatenate(pieces, axis=0)
        finish(r0, q * jnp.exp(b), v, kk, b, intra)
        return carry

    lax.cond(min_decay >= HG_FAST_MIN_LOGDECAY,
             lambda: lax.fori_loop(0, tile // c, fast_chunk, 0, unroll=2),
             lambda: lax.fori_loop(0, tile // c, exact_chunk, 0))

    o = o_sc[...]
    ms = _dot_exact_rhs(o * o, ones_bd, terms=2) * (1.0 / HEAD_DIM)
    y = o * lax.rsqrt(ms + RMS_EPS) * ng_ref[...] * _silu(g_ref[...].astype(F32))
    o_ref[...] = y.astype(o_ref.dtype)


def hgrn2(proj, lb_logits, norm_g, layer, tile=512):
    s = proj.shape[0]
    depth = lb_logits.shape[0]
    c, sub = HG_CHUNK, HG_SUB
    half = c // 2
    tile = min(tile, s)
    col = lambda j: pl.BlockSpec((tile, BRANCH_W), lambda i, j=j: (i, j))
    pos = np.arange(tile)
    tril = _const((pos[:, None] // c == pos[None, :] // c) & (pos[None, :] <= pos[:, None]), BF16)
    nhalf = tile // half
    halfsum = _const(np.arange(tile)[None, :] // half == np.arange(nhalf)[:, None], BF16)
    t = np.arange(c)[:, None]
    col_s = np.arange(c + half)[None, :]
    same_half = (col_s < c) & (col_s // half == t // half) & (col_s <= t)
    cross = (col_s >= c) & (t >= half)
    fmask = _const(np.tile(same_half | cross, (1, N_HEADS)))
    return pl.pallas_call(
        functools.partial(_hgrn_kernel, layer=layer, tile=tile),
        grid=(s // tile,),
        in_specs=[col(0), col(1), col(2), col(3),
                  _const_spec((depth, BRANCH_W)), _const_spec((1, BRANCH_W)),
                  _const_spec((BRANCH_W, BRANCH_W)), _const_spec((N_HEADS, BRANCH_W)),
                  _const_spec((tile, tile)), _const_spec((nhalf, tile)),
                  _const_spec((c, N_HEADS * (c + half)))],
        out_specs=pl.BlockSpec((tile, BRANCH_W), lambda i: (i, 0)),
        out_shape=jax.ShapeDtypeStruct((s, BRANCH_W), BF16),
        scratch_shapes=[pltpu.VMEM((BRANCH_W, BRANCH_W), F32),
                        pltpu.VMEM((c + sub, BRANCH_W), F32),
                        pltpu.VMEM((c + sub, BRANCH_W), F32),
                        pltpu.VMEM((c + sub, BRANCH_W), F32),
                        pltpu.VMEM((sub * c, BRANCH_W), BF16),
                        pltpu.VMEM((tile, BRANCH_W), F32),
                        pltpu.VMEM((tile, BRANCH_W), F32),
                        pltpu.VMEM((tile, BRANCH_W), F32)],
        compiler_params=_params("arbitrary"),
        name="hgrn2",
    )(proj, proj, proj, proj, lb_logits.astype(F32), norm_g.reshape(1, BRANCH_W).astype(F32),
      _head_ones(), _head_masks(), tril, halfsum, fmask)


FOX_TILE = 256
FOX_GATE_BLOCKS = 2
FOX_NSTAT = 16
FOX_FIXED_MAX = 30.0
FOX_SKIP_LOG = 30.0


def _fox_gate_kernel(h_ref, q_ref, k_ref, wf_ref, bias_ref, tril_ref, ones_ref,
                     ct_ref, c_ref, stat_ref, carry_ref, kmax_ref):
    @pl.when(pl.program_id(0) == 0)
    def _():
        carry_ref[...] = jnp.zeros_like(carry_ref)
        kmax_ref[...] = jnp.zeros_like(kmax_ref)

    logit = _dot(h_ref[...], wf_ref[...]) + bias_ref[...]
    logf = jnp.minimum(logit, 0.0) - jnp.log(1.0 + jnp.exp(-jnp.abs(logit)))
    carry = carry_ref[...]
    parts = []
    for blk in range(logf.shape[0] // FOX_TILE):
        part = _dot_exact_lhs(tril_ref[...], logf[blk * FOX_TILE:(blk + 1) * FOX_TILE]) + carry
        carry = part[-1:, :]
        parts.append(part)
    cum = jnp.concatenate(parts, axis=0)
    carry_ref[...] = carry
    ct_ref[...] = cum.T[:SUBLANE, :]
    c_ref[...] = cum

    ones_bd = ones_ref[...]
    q = q_ref[...].astype(F32)
    k = k_ref[...].astype(F32)
    scale = HEAD_DIM ** -0.5
    head_lane = lax.broadcasted_iota(jnp.int32, (1, BRANCH_W), 1) // HEAD_DIM
    c_heads = jnp.zeros(q.shape, F32)
    for h in range(N_HEADS):
        c_heads = jnp.where(head_lane == h, cum[:, h:h + 1], c_heads)
    slack = 1.0 + 2.0 ** -6
    qn = jnp.sqrt(_dot((q * q).astype(BF16), ones_bd)) * (scale * slack)
    kn = jnp.sqrt(_dot((k * k).astype(BF16), ones_bd)) * slack
    diag = _dot((q * k).astype(BF16), ones_bd) * scale - (2.0 ** -6) * qn * kn
    e_row = c_heads - diag
    kmax = kmax_ref[...]
    for blk in range(q.shape[0] // FOX_TILE):
        rs = slice(blk * FOX_TILE, (blk + 1) * FOX_TILE)
        kmax = jnp.maximum(kmax, jnp.max(kn[rs], axis=0, keepdims=True))
        rows = [jnp.max(qn[rs], axis=0, keepdims=True),
                jnp.max(e_row[rs], axis=0, keepdims=True),
                kmax,
                c_heads[(blk + 1) * FOX_TILE - 1:(blk + 1) * FOX_TILE, :]]
        stat_ref[blk] = jnp.concatenate(rows + [jnp.zeros((SUBLANE - len(rows), BRANCH_W), F32)], axis=0)
    kmax_ref[...] = kmax


def fox_gate(h, proj, w_f, f_bias):
    s, d = h.shape
    tile = min(FOX_GATE_BLOCKS * FOX_TILE, s)
    nblk = tile // FOX_TILE
    bias = jnp.zeros((1, LANE), F32).at[0, :N_HEADS].set(f_bias.astype(F32))
    tril = _const(np.tril(np.ones((FOX_TILE, FOX_TILE))), BF16)
    ct, c_rows, stats = pl.pallas_call(
        _fox_gate_kernel,
        grid=(s // tile,),
        in_specs=[pl.BlockSpec((tile, d), lambda i: (i, 0)),
                  pl.BlockSpec((tile, BRANCH_W), lambda i: (i, 4)),
                  pl.BlockSpec((tile, BRANCH_W), lambda i: (i, 5)),
                  _const_spec((d, LANE)), _const_spec((1, LANE)), _const_spec((FOX_TILE, FOX_TILE)),
                  _const_spec((BRANCH_W, BRANCH_W))],
        out_specs=[pl.BlockSpec((SUBLANE, tile), lambda i: (0, i)),
                   pl.BlockSpec((tile, LANE), lambda i: (i, 0)),
                   pl.BlockSpec((nblk, SUBLANE, BRANCH_W), lambda i: (i, 0, 0))],
        out_shape=[jax.ShapeDtypeStruct((SUBLANE, s), F32),
                   jax.ShapeDtypeStruct((s, LANE), F32),
                   jax.ShapeDtypeStruct((s // FOX_TILE, SUBLANE, BRANCH_W), F32)],
        scratch_shapes=[pltpu.VMEM((1, LANE), F32), pltpu.VMEM((1, BRANCH_W), F32)],
        compiler_params=_params("arbitrary"),
        name="fox_gate",
    )(h, proj, proj, w_f, bias, tril, _head_ones())
    return ct, c_rows, stats[:, :4, ::HEAD_DIM].reshape(-1)


def _fox_kernel(stat_ref, q_ref, k_ref, v_ref, ct_ref, c_ref, hm_ref, o_ref, m_sc, l_sc, acc_sc, *, tq):
    i = pl.program_id(0)
    q0 = pl.multiple_of(i * tq, tq)
    hm = hm_ref[...]
    q = q_ref[...].astype(F32) * (HEAD_DIM ** -0.5)
    qh = [(q * hm[h:h + 1, :]).astype(BF16) for h in range(N_HEADS)]
    c_q0 = ct_ref[:, pl.ds(q0, tq)][:, 0:1]

    first = i
    for h in range(N_HEADS):
        qmax = stat_ref[i * FOX_NSTAT + h]
        emax = stat_ref[i * FOX_NSTAT + N_HEADS + h]

        def needed(j, h=h, qmax=qmax, emax=emax):
            jc = jnp.maximum(j, 0)
            bound = (qmax * stat_ref[jc * FOX_NSTAT + 2 * N_HEADS + h] + emax
                     - stat_ref[jc * FOX_NSTAT + 3 * N_HEADS + h])
            return (j >= 0) & (bound >= -FOX_SKIP_LOG)

        last_dropped = lax.while_loop(needed, lambda j: j - 1, i - 1)
        first = jnp.minimum(first, last_dropped + 1)

    l_sc[...] = jnp.zeros_like(l_sc)
    acc_sc[...] = jnp.zeros_like(acc_sc)

    def causal(sc):
        r = lax.broadcasted_iota(jnp.int32, (tq, tq), 0)
        cidx = lax.broadcasted_iota(jnp.int32, (tq, tq), 1)
        return jnp.where(cidx <= r, sc, -jnp.inf)

    def online_block(s0, diagonal):
        kb = k_ref[pl.ds(s0, tq), :]
        vb = v_ref[pl.ds(s0, tq), :]
        bias = c_q0 - ct_ref[:, pl.ds(s0, tq)]
        for h in range(N_HEADS):
            sc = _dot_nt(qh[h], kb) + bias[h:h + 1, :]
            if diagonal:
                sc = causal(sc)
            m_prev = m_sc[h]
            m_new = jnp.maximum(m_prev, jnp.max(sc, axis=1, keepdims=True))
            alpha = jnp.exp(m_prev - m_new)
            p = jnp.exp(sc - jnp.tile(m_new, (1, tq // LANE)))
            l_sc[h] = alpha * l_sc[h] + jnp.sum(p, axis=1, keepdims=True)
            acc_sc[h] = acc_sc[h] * jnp.tile(alpha, (1, BRANCH_W // LANE)) + _dot(p.astype(BF16), vb)
            m_sc[h] = m_new

    tops = [stat_ref[i * FOX_NSTAT + h] * stat_ref[i * FOX_NSTAT + 2 * N_HEADS + h] for h in range(N_HEADS)]
    c_tile = c_ref[...]
    shift = [c_tile[:, h:h + 1] - c_q0[h:h + 1, :] - tops[h] for h in range(N_HEADS)]

    q_heads = jnp.concatenate(qh, axis=0)

    def fixed_block(s0, diagonal):
        kb = k_ref[pl.ds(s0, tq), :]
        vb = v_ref[pl.ds(s0, tq), :]
        bias = c_q0 - ct_ref[:, pl.ds(s0, tq)]
        sc_all = _dot_nt(q_heads, kb)
        probs = []
        for h in range(N_HEADS):
            sc = sc_all[h * tq:(h + 1) * tq, :] + bias[h:h + 1, :] + shift[h]
            if diagonal:
                sc = causal(sc)
            p = jnp.exp(sc)
            l_sc[h] += p[:, :LANE] + p[:, LANE:]
            probs.append(p.astype(BF16))
        pv = _dot(jnp.concatenate(probs, axis=0), vb)
        for h in range(N_HEADS):
            acc_sc[h] += pv[h * tq:(h + 1) * tq, :]

    def run(block, row_sum):
        lax.fori_loop(first, i, lambda j, carry: (block(pl.multiple_of(j * tq, tq), False), carry)[1], 0)
        block(q0, True)
        out = jnp.zeros((tq, BRANCH_W), F32)
        for h in range(N_HEADS):
            out = out + acc_sc[h] * hm[h:h + 1, :] / row_sum(l_sc[h])
        o_ref[...] = out.astype(o_ref.dtype)

    def run_online():
        m_sc[...] = jnp.full_like(m_sc, -jnp.inf)
        run(online_block, lambda l: jnp.tile(l, (1, BRANCH_W // LANE)))

    def run_fixed():
        run(fixed_block, lambda l: jnp.sum(l, axis=1, keepdims=True))

    lax.cond(functools.reduce(jnp.maximum, tops) <= FOX_FIXED_MAX, run_fixed, run_online)


def fox_attention(proj, ct, c_rows, stats):
    s = proj.shape[0]
    tq = min(FOX_TILE, s)
    full = lambda j: pl.BlockSpec((s, BRANCH_W), lambda i, j=j: (0, j), pipeline_mode=pl.Buffered(1))
    return pl.pallas_call(
        functools.partial(_fox_kernel, tq=tq),
        grid=(s // tq,),
        in_specs=[pl.BlockSpec(memory_space=pltpu.SMEM),
                  pl.BlockSpec((tq, BRANCH_W), lambda i: (i, 4)), full(5), full(6),
                  _const_spec((SUBLANE, s)), pl.BlockSpec((tq, LANE), lambda i: (i, 0)),
                  _const_spec((N_HEADS, BRANCH_W))],
        out_specs=pl.BlockSpec((tq, BRANCH_W), lambda i: (i, 0)),
        out_shape=jax.ShapeDtypeStruct((s, BRANCH_W), BF16),
        scratch_shapes=[pltpu.VMEM((N_HEADS, tq, LANE), F32),
                        pltpu.VMEM((N_HEADS, tq, LANE), F32),
                        pltpu.VMEM((N_HEADS, tq, BRANCH_W), F32)],
        compiler_params=_params("parallel"),
        name="fox_attention",
    )(stats, proj, proj, proj, ct, c_rows, _head_masks())


def _pool_kernel(u_ref, w_ref, scale_ref, o_ref, ext, *, tile):
    i = pl.program_id(0)

    @pl.when(i == 0)
    def _():
        ext[pl.ds(0, POOL_HALO), :] = jnp.zeros((POOL_HALO, BRANCH_W), F32)

    u = u_ref[...].astype(F32)
    ext[pl.ds(POOL_HALO, tile), :] = u
    pos = (i * tile + lax.broadcasted_iota(jnp.int32, (tile, 1), 0) + 1).astype(F32)
    halves = []
    for half in range(BRANCH_W // LANE):
        lanes = pl.ds(half * LANE, LANE)
        w_small, w_big = POOL_WINDOWS[2 * half], POOL_WINDOWS[2 * half + 1]
        run = u[:, half * LANE:(half + 1) * LANE]
        sums = {}
        for j in range(1, w_big):
            if j == w_small:
                sums[w_small] = run
            run = run + ext[pl.ds(POOL_HALO - j, tile), lanes]
        sums[w_big] = run
        lane = lax.broadcasted_iota(jnp.int32, (1, LANE), 1)
        small = lane < POOL_GROUP
        total = jnp.where(small, sums[w_small], sums[w_big])
        count = jnp.where(small, jnp.minimum(pos, float(w_small)), jnp.minimum(pos, float(w_big)))
        halves.append(total / count)
    mean = jnp.concatenate(halves, axis=1)
    d = (mean - u).astype(BF16)
    y = _dot(d, w_ref[...]) * scale_ref[...]
    o_ref[...] = y.astype(o_ref.dtype)
    ext[pl.ds(0, POOL_HALO), :] = u[tile - POOL_HALO:, :]


def pool_mixer(proj, w_pool, scale, tile=2048):
    s = proj.shape[0]
    tile = min(tile, s)
    ng = len(POOL_WINDOWS)
    w_bd = jnp.zeros((BRANCH_W, BRANCH_W), F32)
    for gi in range(ng):
        lo = gi * POOL_GROUP
        w_bd = w_bd.at[lo:lo + POOL_GROUP, lo:lo + POOL_GROUP].set(w_pool[gi].astype(F32))
    return pl.pallas_call(
        functools.partial(_pool_kernel, tile=tile),
        grid=(s // tile,),
        in_specs=[pl.BlockSpec((tile, BRANCH_W), lambda i: (i, 7)),
                  _const_spec((BRANCH_W, BRANCH_W)), _const_spec((1, BRANCH_W))],
        out_specs=pl.BlockSpec((tile, BRANCH_W), lambda i: (i, 0)),
        out_shape=jax.ShapeDtypeStruct((s, BRANCH_W), BF16),
        scratch_shapes=[pltpu.VMEM((tile + POOL_HALO, BRANCH_W), F32)],
        compiler_params=_params("arbitrary"),
        name="pool_mixer",
    )(proj, w_bd.astype(BF16), scale.reshape(1, BRANCH_W).astype(F32))


def _ret_kernel(q_ref, k_ref, v_ref, g_ref, cos_ref, sin_ref, perm_ref, ones_ref, hm_ref,
                dstack_ref, xi_ref, zeta_ref, gc_ref, gng_ref, gnb_ref, o_ref, st_ref, *, tile):
    c = RET_CHUNK

    @pl.when(pl.program_id(0) == 0)
    def _():
        st_ref[...] = jnp.zeros_like(st_ref)

    perm = perm_ref[...]
    ones_bd = ones_ref[...]
    bd_mask = ones_bd.astype(F32)
    hm = hm_ref[...]

    cos = cos_ref[...]
    sin = sin_ref[...]
    q_all = q_ref[...]
    k_all = k_ref[...]
    qr_all = q_all.astype(F32) * cos + _dot(q_all, perm) * sin
    kr_all = (k_all.astype(F32) * cos + _dot(k_all, perm) * sin) * (HEAD_DIM ** -0.5)

    outs = []
    for ci in range(tile // c):
        r0 = ci * c
        qr = qr_all[r0:r0 + c, :]
        kr = kr_all[r0:r0 + c, :]
        v = v_ref[pl.ds(r0, c), :]

        qx = jnp.concatenate([qr * hm[h:h + 1, :] for h in range(N_HEADS)], axis=0).astype(BF16)
        sc = _dot_nt(qx, kr.astype(BF16)) * dstack_ref[...]
        r = _dot(sc.astype(BF16), v)
        intra = jnp.zeros((c, BRANCH_W), F32)
        for h in range(N_HEADS):
            intra = intra + r[h * c:(h + 1) * c, :] * hm[h:h + 1, :]

        st = st_ref[...]
        inter = _dot_nt((qr * xi_ref[...]).astype(BF16), st.astype(BF16))
        upd = _dot(v.astype(F32).T.astype(BF16), (kr * zeta_ref[...]).astype(BF16))
        st_ref[...] = st * gc_ref[...] + upd * bd_mask
        outs.append(intra + inter)

    o = jnp.concatenate(outs, axis=0)
    mu = _dot_exact_rhs(o, ones_bd, terms=2) * (1.0 / HEAD_DIM)
    cen = o - mu
    var = _dot_exact_rhs(cen * cen, ones_bd, terms=2) * (1.0 / HEAD_DIM)
    y = cen * lax.rsqrt(var + LN_EPS) * gng_ref[...] + gnb_ref[...]
    o_ref[...] = (y * _silu(g_ref[...].astype(F32))).astype(o_ref.dtype)


def _rope_tables(s):
    half = HEAD_DIM // 2
    pos = np.arange(s, dtype=np.float64)
    inv_freq = ROPE_BASE ** (-np.arange(half, dtype=np.float64) / half)
    lane = np.arange(BRANCH_W)
    ang = pos[:, None] * inv_freq[lane % half][None, :]
    sign = np.where(lane % HEAD_DIM < half, -1.0, 1.0)
    return _const(np.cos(ang)), _const(np.sin(ang) * sign[None, :])


def _ret_constants():
    c = RET_CHUNK
    half = HEAD_DIM // 2
    lane = np.arange(BRANCH_W)
    partner = np.where(lane % HEAD_DIM < half, lane + half, lane - half)
    perm = lane[:, None] == partner[None, :]
    log_gamma = np.log1p(-np.exp2(-RET_DECAY_BASE - np.arange(N_HEADS, dtype=np.float64)))
    ci = np.arange(c, dtype=np.float64)
    diff = ci[:, None] - ci[None, :]
    intra = np.where(diff >= 0, np.exp(diff * log_gamma[:, None, None]), 0.0)
    dstack = intra.reshape(N_HEADS * c, c)
    lg_lane = np.repeat(log_gamma, HEAD_DIM)[None, :]
    xi = np.exp((ci[:, None] + 1.0) * lg_lane)
    zeta = np.exp((c - 1.0 - ci[:, None]) * lg_lane)
    gc = np.exp(c * lg_lane)
    return _const(perm, BF16), _const(dstack), _const(xi), _const(zeta), _const(gc)


def retention(proj, gn_g, gn_b, tile=1024):
    s = proj.shape[0]
    c = RET_CHUNK
    tile = min(tile, s)
    cos_t, sin_t = _rope_tables(s)
    perm, dstack, xi, zeta, gc = _ret_constants()
    col = lambda j: pl.BlockSpec((tile, BRANCH_W), lambda i, j=j: (i, j))
    row = pl.BlockSpec((tile, BRANCH_W), lambda i: (i, 0))
    return pl.pallas_call(
        functools.partial(_ret_kernel, tile=tile),
        grid=(s // tile,),
        in_specs=[col(8), col(9), col(10), col(11), row, row,
                  _const_spec((BRANCH_W, BRANCH_W)), _const_spec((BRANCH_W, BRANCH_W)),
                  _const_spec((N_HEADS, BRANCH_W)), _const_spec((N_HEADS * c, c)),
                  _const_spec((c, BRANCH_W)), _const_spec((c, BRANCH_W)), _const_spec((1, BRANCH_W)),
                  _const_spec((1, BRANCH_W)), _const_spec((1, BRANCH_W))],
        out_specs=row,
        out_shape=jax.ShapeDtypeStruct((s, BRANCH_W), BF16),
        scratch_shapes=[pltpu.VMEM((BRANCH_W, BRANCH_W), F32)],
        compiler_params=_params("arbitrary"),
        name="retention",
    )(proj, proj, proj, proj, cos_t, sin_t, perm, _head_ones(), _head_masks(), dstack, xi, zeta, gc,
      gn_g.reshape(1, BRANCH_W).astype(F32), gn_b.reshape(1, BRANCH_W).astype(F32))


def _merge_kernel(h_ref, o0_ref, o1_ref, o2_ref, o3_ref, x_ref, wg_ref, wb_ref, wo_ref, g_ref, *rest):
    n_cast = (len(rest) - 2) // 2
    cast_in, (xo_ref, ho_ref), cast_out = rest[:n_cast], rest[n_cast:n_cast + 2], rest[n_cast + 2:]
    for src, dst in zip(cast_in, cast_out):
        dst[...] = src[...].astype(dst.dtype)
    h = h_ref[...]
    merged = jnp.zeros(x_ref.shape, F32)
    for bi, o_ref in enumerate((o0_ref, o1_ref, o2_ref, o3_ref)):
        gate = _sigmoid(_dot(h, wg_ref[:, bi * D_MODEL:(bi + 1) * D_MODEL]))
        merged = merged + gate * _dot(o_ref[...], wb_ref[bi])
    x_new = x_ref[...] + _dot(merged.astype(BF16), wo_ref[...])
    xo_ref[...] = x_new
    ho_ref[...] = _rms(x_new, g_ref[...]).astype(ho_ref.dtype)


def merge(h, branches, x, w_gate, w_branch, w_out, next_gain, tm=512, cast=()):
    s, d = x.shape
    tm = min(tm, s)
    steps = s // tm
    row = lambda w: pl.BlockSpec((tm, w), lambda i: (i, 0))
    flat = [c.reshape(-1, c.shape[-1]) for c in cast]
    slab = lambda c: pl.BlockSpec((c.shape[0] // steps, c.shape[1]), lambda i: (i, 0))
    outs = pl.pallas_call(
        _merge_kernel,
        grid=(steps,),
        in_specs=[row(d), row(BRANCH_W), row(BRANCH_W), row(BRANCH_W), row(BRANCH_W), row(d),
                  _const_spec((d, N_BRANCH * d)), _const_spec((N_BRANCH, BRANCH_W, d)),
                  _const_spec((d, d)), _const_spec((1, d))] + [slab(c) for c in flat],
        out_specs=[row(d), row(d)] + [slab(c) for c in flat],
        out_shape=[jax.ShapeDtypeStruct((s, d), F32), jax.ShapeDtypeStruct((s, d), BF16)]
        + [jax.ShapeDtypeStruct(c.shape, BF16) for c in flat],
        compiler_params=_params("parallel"),
        name="merge",
    )(h, *branches, x, w_gate, w_branch, w_out, next_gain.reshape(1, d).astype(F32), *flat)
    return outs[0], outs[1], [o.reshape(c.shape) for o, c in zip(outs[2:], cast)]


def _ffn_kernel(h_ref, x_ref, wg_ref, wu_ref, wd_ref, g_ref, xo_ref, ho_ref, acc_ref):
    f = pl.program_id(1)

    @pl.when(f == 0)
    def _():
        acc_ref[...] = jnp.zeros_like(acc_ref)

    h = h_ref[...]
    a = _silu(_dot(h, wg_ref[...].astype(BF16))) * _dot(h, wu_ref[...].astype(BF16))
    acc_ref[...] += _dot(a.astype(BF16), wd_ref[...].astype(BF16))

    @pl.when(f == pl.num_programs(1) - 1)
    def _():
        x_new = x_ref[...] + acc_ref[...]
        xo_ref[...] = x_new
        ho_ref[...] = _rms(x_new, g_ref[...]).astype(ho_ref.dtype)


def ffn_dense(h, x, w_gate, w_up, w_down, next_gain, tm=1024, tf=512):
    s, d = x.shape
    tm = min(tm, s)
    dff = w_gate.shape[1]
    row = lambda: pl.BlockSpec((tm, d), lambda i, f: (i, 0))
    return pl.pallas_call(
        _ffn_kernel,
        grid=(s // tm, dff // tf),
        in_specs=[row(), row(),
                  pl.BlockSpec((d, tf), lambda i, f: (0, f)),
                  pl.BlockSpec((d, tf), lambda i, f: (0, f)),
                  pl.BlockSpec((tf, d), lambda i, f: (f, 0)),
                  _const_spec((1, d))],
        out_specs=[row(), row()],
        out_shape=[jax.ShapeDtypeStruct((s, d), F32), jax.ShapeDtypeStruct((s, d), BF16)],
        scratch_shapes=[pltpu.VMEM((tm, d), F32)],
        compiler_params=_params("parallel", "arbitrary"),
        name="ffn_dense",
    )(h, x, w_gate, w_up, w_down, next_gain.reshape(1, d).astype(F32))


MERGE_CAST_TILE = 256
MOE_TOK_TILE = 256
MOE_ROUTE_TILES = 4
MOE_ROW_BLOCK = 512
MOE_GATHER_ROWS = 128
MOE_GATHER_TILES = 4
MOE_VMEM_LIMIT = 60 * 1024 * 1024
MOE_Y_BLOCK = 128
MOE_Y_FETCH = MOE_TOK_TILE // MOE_Y_BLOCK + 1


def _router_logits(x, w):
    xh, xm, _ = _split3(x)
    wh, wm, _ = _split3(w)
    packed = (wh.astype(F32) + pltpu.roll(wm.astype(F32), N_EXPERTS, axis=1)).astype(BF16)
    s = _dot(xh, packed) + _dot(xm, packed)
    return s + pltpu.roll(s, LANE - N_EXPERTS, axis=1)


def _route_kernel(x_ref, ng_ref, router_ref, ltri_ref, pos_ref, gate_ref, post_ref, before_ref, total_ref,
                  carry_ref):
    tm = x_ref.shape[0]

    @pl.when(pl.program_id(0) == 0)
    def _():
        carry_ref[...] = jnp.zeros_like(carry_ref)

    hn = _rms(x_ref[...], ng_ref[...])
    logits = _router_logits(hn, router_ref[...])
    lane = lax.broadcasted_iota(jnp.int32, (tm, LANE), 1)
    logits = jnp.where(lane < N_EXPERTS, logits, -jnp.inf)
    v1 = jnp.max(logits, axis=1, keepdims=True)
    i1 = jnp.min(jnp.where(logits == v1, lane, LANE), axis=1, keepdims=True)
    rest = jnp.where(lane == i1, -jnp.inf, logits)
    v2 = jnp.max(rest, axis=1, keepdims=True)
    i2 = jnp.min(jnp.where(rest == v2, lane, LANE), axis=1, keepdims=True)
    w1 = 1.0 / (1.0 + jnp.exp(v2 - v1))
    gate_ref[...] = jnp.where(lane == i1, w1, 0.0) + jnp.where(lane == i2, 1.0 - w1, 0.0)

    member = jnp.where((lane == i1) | (lane == i2), 1.0, 0.0)
    carry = carry_ref[...]
    ranks = []
    for blk in range(tm // MOE_TOK_TILE):
        before_ref[blk] = carry
        mb = member[blk * MOE_TOK_TILE:(blk + 1) * MOE_TOK_TILE]
        ranks.append(_dot(ltri_ref[...], mb.astype(BF16)) + carry)
        carry = carry + jnp.sum(mb, axis=0, keepdims=True)
    pos = jnp.where(member > 0.0, jnp.concatenate(ranks, axis=0), -1.0)
    pos_ref[...] = pos
    post_ref[...] = pos.T[:SUBLANE, :]
    carry_ref[...] = carry
    total_ref[...] = carry


def moe_route(x, norm_gain, router):
    s, d = x.shape
    tm = min(MOE_ROUTE_TILES * MOE_TOK_TILE, s)
    nsub = tm // MOE_TOK_TILE
    nt = s // tm
    router_p = jnp.zeros((d, LANE), F32).at[:, :N_EXPERTS].set(router.astype(F32))
    tt = min(MOE_TOK_TILE, s)
    ltri = _const(np.tril(np.ones((tt, tt)), -1), BF16)
    row = pl.BlockSpec((tm, LANE), lambda i: (i, 0))
    return pl.pallas_call(
        _route_kernel,
        grid=(nt,),
        in_specs=[pl.BlockSpec((tm, d), lambda i: (i, 0)), _const_spec((1, d)), _const_spec((d, LANE)),
                  _const_spec((tt, tt))],
        out_specs=[row, row, pl.BlockSpec((SUBLANE, tm), lambda i: (0, i)),
                   pl.BlockSpec((nsub, 1, LANE), lambda i: (i, 0, 0)), pl.BlockSpec((1, LANE), lambda i: (0, 0))],
        out_shape=[jax.ShapeDtypeStruct((s, LANE), F32), jax.ShapeDtypeStruct((s, LANE), F32),
                   jax.ShapeDtypeStruct((SUBLANE, s), F32), jax.ShapeDtypeStruct((nt * nsub, 1, LANE), F32),
                   jax.ShapeDtypeStruct((1, LANE), F32)],
        scratch_shapes=[pltpu.VMEM((1, LANE), F32)],
        compiler_params=_params("arbitrary"),
        name="moe_route",
    )(x, norm_gain.reshape(1, d).astype(F32), router_p, ltri)


def _moe_ffn_kernel(be_ref, r0_ref, tlo_ref, thi_ref, nv_ref, h_ref, post_ref, wg_ref, wu_ref, wd_ref,
                    y_ref, x_sc, acc_sc, *, tt, nsub):
    b = pl.program_id(0)
    f = pl.program_id(1)
    nb = pl.num_programs(0)
    valid = b < nv_ref[0]
    tmb, d = acc_sc.shape
    gr = tmb // nsub
    last_tile = h_ref.shape[0] // tt - 1

    def picked(e, want, t):
        t0 = pl.multiple_of(t * tt, tt)
        p = post_ref[pl.ds(e, 1), pl.ds(t0, tt)]
        sel = jnp.where(p == want, 1.0, 0.0).astype(BF16)
        return _dot(sel, h_ref[pl.ds(t0, tt), :])

    def gather_head(blk, sb):
        e = be_ref[blk]
        lo = tlo_ref[blk * nsub + sb]
        hi = thi_ref[blk * nsub + sb]
        want = (lax.broadcasted_iota(jnp.int32, (gr, 1), 0) + (r0_ref[blk] + sb * gr)).astype(F32)
        rows = picked(e, want, lo)
        for k in range(1, MOE_GATHER_TILES):
            rows = rows + picked(e, jnp.where(lo + k <= hi, want, -2.0), jnp.minimum(lo + k, last_tile))
        return rows.astype(BF16)

    def gather_tail(slot, blk, sb):
        e = be_ref[blk]
        want = (lax.broadcasted_iota(jnp.int32, (gr, 1), 0) + (r0_ref[blk] + sb * gr)).astype(F32)
        rows = pl.ds(pl.multiple_of(sb * gr, gr), gr)

        def more(t, carry):
            x_sc[slot, rows, :] = (x_sc[slot, rows, :].astype(F32) + picked(e, want, t)).astype(BF16)
            return carry

        lax.fori_loop(tlo_ref[blk * nsub + sb] + MOE_GATHER_TILES, thi_ref[blk * nsub + sb] + 1, more, 0)

    @pl.when((b == 0) & (f == 0))
    def _():
        acc_sc[...] = jnp.zeros_like(acc_sc)
        for sb in range(nsub):
            x_sc[0, pl.ds(sb * gr, gr), :] = gather_head(0, sb)
            gather_tail(0, 0, sb)

    nxt = jnp.minimum(b + 1, nb - 1)
    nslot = (b + 1) % 2

    @pl.when(valid)
    def _():
        x_sc[nslot, pl.ds(pl.multiple_of(f * gr, gr), gr), :] = gather_head(nxt, f)
        xb = x_sc[b % 2]
        a = _silu(_dot(xb, wg_ref[0])) * _dot(xb, wu_ref[0])
        acc_sc[...] = jnp.where(f == 0, 0.0, acc_sc[...]) + _dot(a.astype(BF16), wd_ref[0].astype(BF16))

    @pl.when(valid & (thi_ref[nxt * nsub + f] - tlo_ref[nxt * nsub + f] >= MOE_GATHER_TILES))
    def _():
        gather_tail(nslot, nxt, f)

    @pl.when(f == nsub - 1)
    def _():
        y_ref[...] = jnp.where(valid, acc_sc[...], 0.0).astype(y_ref.dtype)


def moe_ffn(h, post, sched, w_gate, w_up, w_down):
    s, d = h.shape
    ne, _, dff = w_gate.shape
    tmb = MOE_ROW_BLOCK
    tt = min(MOE_TOK_TILE, s)
    nb = sched[0].shape[0]
    nf = tmb // MOE_GATHER_ROWS
    tf = dff // nf

    def fidx(b, f, nv):
        return jnp.where(b < nv[0], f, nf - 1)

    grid_spec = pltpu.PrefetchScalarGridSpec(
        num_scalar_prefetch=5,
        grid=(nb, nf),
        in_specs=[pl.BlockSpec((s, d), lambda b, f, *_: (0, 0), pipeline_mode=pl.Buffered(1)),
                  pl.BlockSpec((SUBLANE, s), lambda b, f, *_: (0, 0), pipeline_mode=pl.Buffered(1)),
                  pl.BlockSpec((1, d, tf), lambda b, f, be, r0, tlo, thi, nv: (be[b], 0, fidx(b, f, nv))),
                  pl.BlockSpec((1, d, tf), lambda b, f, be, r0, tlo, thi, nv: (be[b], 0, fidx(b, f, nv))),
                  pl.BlockSpec((1, tf, d), lambda b, f, be, r0, tlo, thi, nv: (be[b], fidx(b, f, nv), 0))],
        out_specs=pl.BlockSpec((tmb, d), lambda b, f, *_: (b, 0)),
        scratch_shapes=[pltpu.VMEM((2, tmb, d), BF16), pltpu.VMEM((tmb, d), F32)],
    )
    return pl.pallas_call(
        functools.partial(_moe_ffn_kernel, tt=tt, nsub=nf),
        grid_spec=grid_spec,
        out_shape=jax.ShapeDtypeStruct((nb * tmb, d), BF16),
        compiler_params=pltpu.CompilerParams(dimension_semantics=("arbitrary", "arbitrary"),
                                             vmem_limit_bytes=MOE_VMEM_LIMIT),
        name="moe_ffn",
    )(*sched, h, post, w_gate, w_up, w_down)


def _moe_combine_kernel(kb_ref, off_ref, lim_ref, x_ref, pos_ref, gate_ref, fg_ref, *rest):
    y_refs, o_ref, acc_sc = rest[:-2], rest[-2], rest[-1]
    t = pl.program_id(0)
    yb = y_refs[0].shape[0]
    pos = pos_ref[...]
    gate = gate_ref[...]
    col = lax.broadcasted_iota(jnp.int32, (1, yb), 1).astype(F32)

    def routed(e):
        pe = pos[:, e:e + 1]
        ge = gate[:, e:e + 1]
        r = jnp.where(pe >= 0.0, pe + off_ref[t * N_EXPERTS + e].astype(F32), -1.0)
        return r, ge

    def picked(e, k, r):
        sel = jnp.where(r == col + float(k * yb), 1.0, 0.0).astype(BF16)
        return _dot(sel, y_refs[MOE_Y_FETCH * e + k][...])

    col2 = lax.broadcasted_iota(jnp.int32, (1, 2 * yb), 1).astype(F32)
    acc = x_ref[...]
    for e in range(N_EXPERTS):
        r, ge = routed(e)
        pair = jnp.concatenate([y_refs[MOE_Y_FETCH * e][...], y_refs[MOE_Y_FETCH * e + 1][...]], axis=0)
        acc = acc + ge * _dot(jnp.where(r == col2, 1.0, 0.0).astype(BF16), pair)
    acc_sc[...] = acc
    for e in range(N_EXPERTS):
        for k in range(2, MOE_Y_FETCH):
            @pl.when(lim_ref[t * N_EXPERTS + e] > k * yb)
            def _(k=k, e=e):
                r, ge = routed(e)
                acc_sc[...] += ge * picked(e, k, r)
    o_ref[...] = _rms(acc_sc[...], fg_ref[...]).astype(o_ref.dtype)


def moe_combine(x, pos, gate, y, kb, off, lim, final_gain):
    s, d = x.shape
    tm = min(MOE_TOK_TILE, s)
    yb = MOE_Y_BLOCK
    last = y.shape[0] // yb - 1

    def yspec(e, k):
        def index(t, kb_r, off_r, lim_r):
            blk = jnp.minimum(kb_r[t * N_EXPERTS + e] + k, last)
            return (blk if k < 1 else jnp.where(lim_r[t * N_EXPERTS + e] > k * yb, blk, 0), 0)
        return pl.BlockSpec((yb, d), index)

    grid_spec = pltpu.PrefetchScalarGridSpec(
        num_scalar_prefetch=3,
        grid=(s // tm,),
        in_specs=[pl.BlockSpec((tm, d), lambda t, *_: (t, 0)),
                  pl.BlockSpec((tm, LANE), lambda t, *_: (t, 0)),
                  pl.BlockSpec((tm, LANE), lambda t, *_: (t, 0)),
                  pl.BlockSpec((1, d), lambda t, *_: (0, 0))]
        + [yspec(e, k) for e in range(N_EXPERTS) for k in range(MOE_Y_FETCH)],
        out_specs=pl.BlockSpec((tm, d), lambda t, *_: (t, 0)),
        scratch_shapes=[pltpu.VMEM((tm, d), F32)],
    )
    return pl.pallas_call(
        _moe_combine_kernel,
        grid_spec=grid_spec,
        out_shape=jax.ShapeDtypeStruct((s, d), F32),
        compiler_params=_params("arbitrary"),
        name="moe_combine",
    )(kb, off, lim, x, pos, gate, final_gain.reshape(1, d).astype(F32), *([y] * (MOE_Y_FETCH * N_EXPERTS)))


def _moe_schedule(before, total, s):
    tmb, yb, gr = MOE_ROW_BLOCK, MOE_Y_BLOCK, MOE_GATHER_ROWS
    nb = 2 * s // tmb + N_EXPERTS
    counts = total[0, :N_EXPERTS].astype(jnp.int32)
    nblk = (counts + tmb - 1) // tmb
    end = jnp.cumsum(nblk)
    first = end - nblk
    nvalid = end[-1]
    b = jnp.minimum(jnp.arange(nb, dtype=jnp.int32), nvalid - 1)
    blk_e = jnp.sum(b[:, None] >= end[None, :], axis=1).astype(jnp.int32)
    r0 = (b - first[blk_e]) * tmb
    cb = before[:, 0, :N_EXPERTS].astype(jnp.int32)
    r0s = (r0[:, None] + gr * jnp.arange(tmb // gr, dtype=jnp.int32)[None, :]).reshape(-1)
    cbe = cb[:, jnp.repeat(blk_e, tmb // gr)]
    tlo = (jnp.sum(cbe <= r0s[None, :], axis=0) - 1).astype(jnp.int32)
    thi = (jnp.sum(cbe < (r0s + gr)[None, :], axis=0) - 1).astype(jnp.int32)
    sched = (blk_e, r0.astype(jnp.int32), tlo, thi, nvalid.reshape(1).astype(jnp.int32))
    row_start = first[None, :] * tmb + cb
    kb = row_start // yb
    off = first[None, :] * tmb - kb * yb
    n_te = jnp.concatenate([cb[1:], counts[None, :]], axis=0) - cb
    lim = row_start - kb * yb + n_te
    flat = lambda a: a.reshape(-1).astype(jnp.int32)
    return sched, flat(kb), flat(off), flat(lim)


def moe_sparse(h, x, norm_gain, router, w_gate, w_up, w_down, final_gain):
    s, _ = x.shape
    pos, gate, post, before, total = moe_route(x, norm_gain, router)
    sched, kb, off, lim = _moe_schedule(before, total, s)
    y = moe_ffn(h, post, sched, w_gate, w_up, w_down)
    return moe_combine(x, pos, gate, y, kb, off, lim, final_gain)


FOX_F0 = 7 * BRANCH_W


def _split_w_in_kernel(wt_ref, mix_ref, f_ref, gate_ref):
    rest0 = FOX_F0 + N_HEADS
    gate0 = N_MIX_COLS + N_HEADS
    mix_ref[:, :FOX_F0] = wt_ref[pl.ds(0, FOX_F0), :].T.astype(BF16)
    mix_ref[:, FOX_F0:] = wt_ref[pl.ds(rest0, N_MIX_COLS - FOX_F0), :].T.astype(BF16)
    gate_ref[...] = wt_ref[pl.ds(gate0, N_BRANCH * D_MODEL), :].T.astype(BF16)
    f_rows = jnp.concatenate([wt_ref[pl.ds(FOX_F0, N_HEADS), :], jnp.zeros((LANE - N_HEADS, LANE), F32)], axis=0)
    f_ref[...] = f_rows.T.astype(BF16)


def _mixer_weights(w_in, layer, tr=LANE):
    _, d, cols = w_in.shape
    w_t = jnp.swapaxes(w_in, 1, 2)
    row = lambda w: pl.BlockSpec((tr, w), lambda i: (i, 0))
    return pl.pallas_call(
        _split_w_in_kernel,
        grid=(d // tr,),
        in_specs=[pl.BlockSpec((None, cols, tr), lambda i: (layer, 0, i))],
        out_specs=[row(N_MIX_COLS), row(LANE), row(N_BRANCH * D_MODEL)],
        out_shape=[jax.ShapeDtypeStruct((d, N_MIX_COLS), BF16), jax.ShapeDtypeStruct((d, LANE), BF16),
                   jax.ShapeDtypeStruct((d, N_BRANCH * D_MODEL), BF16)],
        compiler_params=_params("parallel"),
        name="split_w_in",
    )(w_t)


def kernel(x, w_in, w_branch, w_out, norm_mix_g, hgrn_lb_logits, hgrn_norm_g, fox_f_bias, pool_w, pool_scale,
           ret_gn_g, ret_gn_b, norm_ffn_g, ffn_w_gate, ffn_w_up, ffn_w_down, moe_router, moe_w_gate, moe_w_up,
           moe_w_down, final_norm_g):
    b, s, d = x.shape
    assert b == 1 and d == D_MODEL
    depth = w_in.shape[0]
    assert depth == 2, "layer 0 uses the dense FFN, layer 1 the experts and the final norm"
    xs = x.reshape(s, d)
    h = None
    out = None
    for layer in range(depth):
        w_mix, w_f, w_gate = _mixer_weights(w_in, layer)
        if layer == 0:
            proj, h = norm_matmul(xs, norm_mix_g[0], w_mix)
        else:
            proj = matmul(h, w_mix)
        ct, c_rows, fox_stats = fox_gate(h, proj, w_f, fox_f_bias[layer])
        branches = (
            hgrn2(proj, hgrn_lb_logits, hgrn_norm_g[layer], layer),
            fox_attention(proj, ct, c_rows, fox_stats),
            pool_mixer(proj, pool_w[layer], pool_scale[layer]),
            retention(proj, ret_gn_g[layer], ret_gn_b[layer]),
        )
        li = layer // 2
        if layer % 2 == 0:
            xs, h2, (ffn_wg, ffn_wu) = merge(h, branches, xs, w_gate, w_branch[layer].astype(BF16),
                                             w_out[layer].astype(BF16), norm_ffn_g[layer],
                                             cast=(ffn_w_gate[li], ffn_w_up[li]))
            xs, h = ffn_dense(h2, xs, ffn_wg, ffn_wu, ffn_w_down[li], norm_mix_g[layer + 1])
        else:
            xs, h2, (moe_wg, moe_wu) = merge(h, branches, xs, w_gate, w_branch[layer].astype(BF16),
                                             w_out[layer].astype(BF16), norm_ffn_g[layer], tm=MERGE_CAST_TILE,
                                             cast=(moe_w_gate[li], moe_w_up[li]))
            out = moe_sparse(h2, xs, norm_ffn_g[layer], moe_router[li], moe_wg, moe_wu, moe_w_down[li],
                             final_norm_g)
    return out.reshape(b, s, d)
```

```python
import functools
import math

import jax
import jax.numpy as jnp
import numpy as np
from jax import lax
from jax.experimental import pallas as pl
from jax.experimental.pallas import tpu as pltpu

D_MODEL = 1024
N_BRANCH = 4
BRANCH_W = D_MODEL // N_BRANCH
HEAD_DIM = 64
N_HEADS = BRANCH_W // HEAD_DIM
POOL_WINDOWS = (2, 4, 8, 16)
POOL_GROUP = BRANCH_W // len(POOL_WINDOWS)
POOL_HALO = 16
RET_DECAY_BASE = 5.0
ROPE_BASE = 10000.0
N_EXPERTS = 8
RMS_EPS = 1e-6
LN_EPS = 1e-5
N_MIX_COLS = 12 * BRANCH_W

LANE = 128
SUBLANE = 8
VMEM_LIMIT = 56 * 1024 * 1024

HG_CHUNK = 64
HG_SUB = 16
HG_FAST_MIN_LOGDECAY = -60.0
RET_CHUNK = 256

F32 = jnp.float32
BF16 = jnp.bfloat16
NT_DIMS = (((1,), (1,)), ((), ()))


def _params(*sem):
    return pltpu.CompilerParams(dimension_semantics=sem, vmem_limit_bytes=VMEM_LIMIT)


def _const_spec(shape):
    nd = len(shape)
    return pl.BlockSpec(shape, lambda *_: (0,) * nd, pipeline_mode=pl.Buffered(1))


def _split3(x):
    hi = x.astype(BF16)
    r1 = x - hi.astype(F32)
    mid = r1.astype(BF16)
    lo = (r1 - mid.astype(F32)).astype(BF16)
    return hi, mid, lo


def _dot(a, b):
    return jnp.dot(a, b, preferred_element_type=F32)


def _dot_nt(a, b):
    return lax.dot_general(a, b, NT_DIMS, preferred_element_type=F32)


def _dot_exact_rhs(x, m_bf16, terms=3):
    return sum(_dot(part, m_bf16) for part in _split3(x)[:terms])


def _dot_exact_lhs(m_bf16, x, terms=3):
    return sum(_dot(m_bf16, part) for part in _split3(x)[:terms])


def _sigmoid(x):
    return 1.0 / (1.0 + jnp.exp(-x))


def _silu(x):
    return x * _sigmoid(x)


def _rms(x, gain):
    return x * lax.rsqrt(jnp.mean(x * x, axis=-1, keepdims=True) + RMS_EPS) * gain


def _const(a, dtype=F32):
    return jnp.asarray(np.asarray(a, np.float32), dtype)


def _head_of(n):
    return np.arange(n) // HEAD_DIM


def _head_ones():
    h = _head_of(BRANCH_W)
    return _const(h[:, None] == h[None, :], BF16)


def _head_masks():
    return _const(_head_of(BRANCH_W)[None, :] == np.arange(N_HEADS)[:, None])


def _norm_matmul_kernel(x_ref, g_ref, b_ref, o_ref, h_ref):
    @pl.when(pl.program_id(1) == 0)
    def _():
        h_ref[...] = _rms(x_ref[...], g_ref[...]).astype(h_ref.dtype)

    o_ref[...] = _dot(h_ref[...], b_ref[...]).astype(o_ref.dtype)


def norm_matmul(x, gain, b, tm=2048, tn=1024):
    m, k = x.shape
    _, n = b.shape
    tm = min(tm, m)
    return pl.pallas_call(
        _norm_matmul_kernel,
        grid=(m // tm, n // tn),
        in_specs=[pl.BlockSpec((tm, k), lambda i, j: (i, 0)), _const_spec((1, k)),
                  pl.BlockSpec((k, tn), lambda i, j: (0, j))],
        out_specs=[pl.BlockSpec((tm, tn), lambda i, j: (i, j)), pl.BlockSpec((tm, k), lambda i, j: (i, 0))],
        out_shape=[jax.ShapeDtypeStruct((m, n), BF16), jax.ShapeDtypeStruct((m, k), BF16)],
        compiler_params=_params("arbitrary", "arbitrary"),
        name="in_proj_norm",
    )(x, gain.reshape(1, k).astype(F32), b)


def _matmul_kernel(a_ref, b_ref, o_ref):
    o_ref[...] = _dot(a_ref[...], b_ref[...]).astype(o_ref.dtype)


def matmul(a, b, out_dtype=BF16, tm=2048, tn=1024):
    m, k = a.shape
    _, n = b.shape
    tm = min(tm, m)
    return pl.pallas_call(
        _matmul_kernel,
        grid=(n // tn, m // tm),
        in_specs=[pl.BlockSpec((tm, k), lambda j, i: (i, 0)),
                  pl.BlockSpec((k, tn), lambda j, i: (0, j))],
        out_specs=pl.BlockSpec((tm, tn), lambda j, i: (i, j)),
        out_shape=jax.ShapeDtypeStruct((m, n), out_dtype),
        compiler_params=_params("parallel", "parallel"),
        name="in_proj",
    )(a, b)


def _hgrn_kernel(q_ref, f_ref, i_ref, g_ref, lbl_ref, ng_ref, ones_ref, hm_ref, tril_ref, halfsum_ref, fmask_ref,
                 o_ref, st_ref, bpad, kpad, vpad, astack, lf_sc, kk_sc, o_sc, *, layer, tile):
    c, sub = HG_CHUNK, HG_SUB
    nsub = c // sub
    half = c // 2

    @pl.when(pl.program_id(0) == 0)
    def _():
        st_ref[...] = jnp.zeros_like(st_ref)
        bpad[...] = jnp.zeros_like(bpad)
        kpad[...] = jnp.zeros_like(kpad)
        vpad[...] = jnp.zeros_like(vpad)

    lbl = lbl_ref[...]
    e = jnp.exp(lbl - jnp.max(lbl, axis=0, keepdims=True))
    p = e / jnp.sum(e, axis=0, keepdims=True)
    lb = jnp.zeros((1, BRANCH_W), F32)
    for l in range(1, layer + 1):
        lb = lb + p[l:l + 1, :]

    ones_bd = ones_ref[...]
    hm = hm_ref[...]
    tril = tril_ref[...]
    row = lax.broadcasted_iota(jnp.int32, (c, 1), 0)
    row_in_sub = row % sub
    bd_mask = ones_bd.astype(F32)

    sig = _sigmoid(f_ref[...].astype(F32))
    logf_all = jnp.log(lb + (1.0 - lb) * sig)
    lf_sc[...] = _dot_exact_lhs(tril, logf_all, terms=2)
    kk_sc[...] = (1.0 - lb) * (1.0 - sig)
    min_decay = jnp.min(_dot(halfsum_ref[...], logf_all.astype(BF16)))

    def load(ci):
        r0 = pl.multiple_of(ci * c, c)
        q = q_ref[pl.ds(r0, c), :].astype(F32)
        v = i_ref[pl.ds(r0, c), :].astype(F32)
        kk = kk_sc[pl.ds(r0, c), :]
        b = lf_sc[pl.ds(r0, c), :]
        return r0, q, v, kk, b

    def finish(r0, q_decayed, v, kk, b, intra):
        st = st_ref[...]
        inter = _dot_nt(q_decayed.astype(BF16), st.astype(BF16))
        b_last = b[c - 1:c, :]
        ks_end = (kk * jnp.exp(b_last - b)).astype(BF16)
        upd = _dot(v.T.astype(BF16), ks_end)
        st_ref[...] = st * jnp.exp(b_last) + upd * bd_mask
        o_sc[pl.ds(r0, c), :] = intra + inter

    def fast_chunk(ci, carry):
        r0, q, v, kk, b = load(ci)
        second = row >= half
        m_row = b[half - 1:half, :]
        mref = jnp.where(second, m_row, 0.0)
        qp = q * jnp.exp(b - mref)
        kp = kk * jnp.exp(mref - b)
        e_m = jnp.exp(m_row)
        kaug = jnp.concatenate([kp, kp[:half, :] * e_m], axis=0)
        vaug = jnp.concatenate([v, v[:half, :]], axis=0)
        k_heads = jnp.concatenate([kaug * hm[h:h + 1, :] for h in range(N_HEADS)], axis=0).astype(BF16)
        v_heads = jnp.concatenate([vaug * hm[h:h + 1, :] for h in range(N_HEADS)], axis=0).astype(BF16)
        sc = jnp.where(fmask_ref[...] > 0.0, _dot_nt(qp.astype(BF16), k_heads), 0.0)
        intra = _dot(sc.astype(BF16), v_heads)
        finish(r0, jnp.where(second, qp * e_m, qp), v, kk, b, intra)
        return carry

    def exact_chunk(ci, carry):
        r0, q, v, kk, b = load(ci)

        bpad[pl.ds(sub, c), :] = b
        kpad[pl.ds(sub, c), :] = kk
        vpad[pl.ds(sub, c), :] = v

        for d in range(sub):
            b_d = bpad[pl.ds(sub - d, c), :]
            k_d = kpad[pl.ds(sub - d, c), :]
            a = jnp.where(row_in_sub >= d, q * k_d * jnp.exp(b - b_d), 0.0)
            astack[pl.ds(d * c, c), :] = a.astype(BF16)
        pall = _dot(astack[...], ones_bd)
        intra = jnp.zeros((c, BRANCH_W), F32)
        for d in range(sub):
            intra = intra + pall[d * c:(d + 1) * c, :] * vpad[pl.ds(sub - d, c), :]

        pieces = [jnp.zeros((sub, BRANCH_W), F32)]
        for si in range(1, nsub):
            lo = si * sub
            m_i = b[lo - 1:lo, :]
            qs = q[lo:lo + sub, :] * jnp.exp(b[lo:lo + sub, :] - m_i)
            ks = (kk[:lo, :] * jnp.exp(m_i - b[:lo, :])).astype(BF16)
            qx = jnp.concatenate([qs * hm[h:h + 1, :] for h in range(N_HEADS)], axis=0).astype(BF16)
            sc = _dot_nt(qx, ks)
            r = _dot(sc.astype(BF16), v[:lo, :].astype(BF16))
            acc = jnp.zeros((sub, BRANCH_W), F32)
            for h in range(N_HEADS):
                acc = acc + r[h * sub:(h + 1) * sub, :] * hm[h:h + 1, :]
            pieces.append(acc)
        intra = intra + jnp.concatenate(pieces, axis=0)
        finish(r0, q * jnp.exp(b), v, kk, b, intra)
        return carry

    lax.cond(min_decay >= HG_FAST_MIN_LOGDECAY,
             lambda: lax.fori_loop(0, tile // c, fast_chunk, 0, unroll=2),
             lambda: lax.fori_loop(0, tile // c, exact_chunk, 0))

    o = o_sc[...]
    ms = _dot_exact_rhs(o * o, ones_bd, terms=2) * (1.0 / HEAD_DIM)
    y = o * lax.rsqrt(ms + RMS_EPS) * ng_ref[...] * _silu(g_ref[...].astype(F32))
    o_ref[...] = y.astype(o_ref.dtype)


def hgrn2(proj, lb_logits, norm_g, layer, tile=512):
    s = proj.shape[0]
    depth = lb_logits.shape[0]
    c, sub = HG_CHUNK, HG_SUB
    half = c // 2
    tile = min(tile, s)
    col = lambda j: pl.BlockSpec((tile, BRANCH_W), lambda i, j=j: (i, j))
    pos = np.arange(tile)
    tril = _const((pos[:, None] // c == pos[None, :] // c) & (pos[None, :] <= pos[:, None]), BF16)
    nhalf = tile // half
    halfsum = _const(np.arange(tile)[None, :] // half == np.arange(nhalf)[:, None], BF16)
    t = np.arange(c)[:, None]
    col_s = np.arange(c + half)[None, :]
    same_half = (col_s < c) & (col_s // half == t // half) & (col_s <= t)
    cross = (col_s >= c) & (t >= half)
    fmask = _const(np.tile(same_half | cross, (1, N_HEADS)))
    return pl.pallas_call(
        functools.partial(_hgrn_kernel, layer=layer, tile=tile),
        grid=(s // tile,),
        in_specs=[col(0), col(1), col(2), col(3),
                  _const_spec((depth, BRANCH_W)), _const_spec((1, BRANCH_W)),
                  _const_spec((BRANCH_W, BRANCH_W)), _const_spec((N_HEADS, BRANCH_W)),
                  _const_spec((tile, tile)), _const_spec((nhalf, tile)),
                  _const_spec((c, N_HEADS * (c + half)))],
        out_specs=pl.BlockSpec((tile, BRANCH_W), lambda i: (i, 0)),
        out_shape=jax.ShapeDtypeStruct((s, BRANCH_W), BF16),
        scratch_shapes=[pltpu.VMEM((BRANCH_W, BRANCH_W), F32),
                        pltpu.VMEM((c + sub, BRANCH_W), F32),
                        pltpu.VMEM((c + sub, BRANCH_W), F32),
                        pltpu.VMEM((c + sub, BRANCH_W), F32),
                        pltpu.VMEM((sub * c, BRANCH_W), BF16),
                        pltpu.VMEM((tile, BRANCH_W), F32),
                        pltpu.VMEM((tile, BRANCH_W), F32),
                        pltpu.VMEM((tile, BRANCH_W), F32)],
        compiler_params=_params("arbitrary"),
        name="hgrn2",
    )(proj, proj, proj, proj, lb_logits.astype(F32), norm_g.reshape(1, BRANCH_W).astype(F32),
      _head_ones(), _head_masks(), tril, halfsum, fmask)


FOX_TILE = 256
FOX_GATE_BLOCKS = 2
FOX_NSTAT = 16
FOX_FIXED_MAX = 30.0
FOX_SKIP_LOG = 30.0


def _fox_gate_kernel(h_ref, q_ref, k_ref, wf_ref, bias_ref, tril_ref, ones_ref,
                     ct_ref, c_ref, stat_ref, carry_ref, kmax_ref):
    @pl.when(pl.program_id(0) == 0)
    def _():
        carry_ref[...] = jnp.zeros_like(carry_ref)
        kmax_ref[...] = jnp.zeros_like(kmax_ref)

    logit = _dot(h_ref[...], wf_ref[...]) + bias_ref[...]
    logf = jnp.minimum(logit, 0.0) - jnp.log(1.0 + jnp.exp(-jnp.abs(logit)))
    carry = carry_ref[...]
    parts = []
    for blk in range(logf.shape[0] // FOX_TILE):
        part = _dot_exact_lhs(tril_ref[...], logf[blk * FOX_TILE:(blk + 1) * FOX_TILE]) + carry
        carry = part[-1:, :]
        parts.append(part)
    cum = jnp.concatenate(parts, axis=0)
    carry_ref[...] = carry
    ct_ref[...] = cum.T[:SUBLANE, :]
    c_ref[...] = cum

    ones_bd = ones_ref[...]
    q = q_ref[...].astype(F32)
    k = k_ref[...].astype(F32)
    scale = HEAD_DIM ** -0.5
    head_lane = lax.broadcasted_iota(jnp.int32, (1, BRANCH_W), 1) // HEAD_DIM
    c_heads = jnp.zeros(q.shape, F32)
    for h in range(N_HEADS):
        c_heads = jnp.where(head_lane == h, cum[:, h:h + 1], c_heads)
    slack = 1.0 + 2.0 ** -6
    qn = jnp.sqrt(_dot((q * q).astype(BF16), ones_bd)) * (scale * slack)
    kn = jnp.sqrt(_dot((k * k).astype(BF16), ones_bd)) * slack
    diag = _dot((q * k).astype(BF16), ones_bd) * scale - (2.0 ** -6) * qn * kn
    e_row = c_heads - diag
    kmax = kmax_ref[...]
    for blk in range(q.shape[0] // FOX_TILE):
        rs = slice(blk * FOX_TILE, (blk + 1) * FOX_TILE)
        kmax = jnp.maximum(kmax, jnp.max(kn[rs], axis=0, keepdims=True))
        rows = [jnp.max(qn[rs], axis=0, keepdims=True),
                jnp.max(e_row[rs], axis=0, keepdims=True),
                kmax,
                c_heads[(blk + 1) * FOX_TILE - 1:(blk + 1) * FOX_TILE, :]]
        stat_ref[blk] = jnp.concatenate(rows + [jnp.zeros((SUBLANE - len(rows), BRANCH_W), F32)], axis=0)
    kmax_ref[...] = kmax


def fox_gate(h, proj, w_f, f_bias):
    s, d = h.shape
    tile = min(FOX_GATE_BLOCKS * FOX_TILE, s)
    nblk = tile // FOX_TILE
    bias = jnp.zeros((1, LANE), F32).at[0, :N_HEADS].set(f_bias.astype(F32))
    tril = _const(np.tril(np.ones((FOX_TILE, FOX_TILE))), BF16)
    ct, c_rows, stats = pl.pallas_call(
        _fox_gate_kernel,
        grid=(s // tile,),
        in_specs=[pl.BlockSpec((tile, d), lambda i: (i, 0)),
                  pl.BlockSpec((tile, BRANCH_W), lambda i: (i, 4)),
                  pl.BlockSpec((tile, BRANCH_W), lambda i: (i, 5)),
                  _const_spec((d, LANE)), _const_spec((1, LANE)), _const_spec((FOX_TILE, FOX_TILE)),
                  _const_spec((BRANCH_W, BRANCH_W))],
        out_specs=[pl.BlockSpec((SUBLANE, tile), lambda i: (0, i)),
                   pl.BlockSpec((tile, LANE), lambda i: (i, 0)),
                   pl.BlockSpec((nblk, SUBLANE, BRANCH_W), lambda i: (i, 0, 0))],
        out_shape=[jax.ShapeDtypeStruct((SUBLANE, s), F32),
                   jax.ShapeDtypeStruct((s, LANE), F32),
                   jax.ShapeDtypeStruct((s // FOX_TILE, SUBLANE, BRANCH_W), F32)],
        scratch_shapes=[pltpu.VMEM((1, LANE), F32), pltpu.VMEM((1, BRANCH_W), F32)],
        compiler_params=_params("arbitrary"),
        name="fox_gate",
    )(h, proj, proj, w_f, bias, tril, _head_ones())
    return ct, c_rows, stats[:, :4, ::HEAD_DIM].reshape(-1)


def _fox_kernel(stat_ref, q_ref, k_ref, v_ref, ct_ref, c_ref, hm_ref, o_ref, m_sc, l_sc, acc_sc, *, tq):
    i = pl.program_id(0)
    q0 = pl.multiple_of(i * tq, tq)
    hm = hm_ref[...]
    q = q_ref[...].astype(F32) * (HEAD_DIM ** -0.5)
    qh = [(q * hm[h:h + 1, :]).astype(BF16) for h in range(N_HEADS)]
    c_q0 = ct_ref[:, pl.ds(q0, tq)][:, 0:1]

    first = i
    for h in range(N_HEADS):
        qmax = stat_ref[i * FOX_NSTAT + h]
        emax = stat_ref[i * FOX_NSTAT + N_HEADS + h]

        def needed(j, h=h, qmax=qmax, emax=emax):
            jc = jnp.maximum(j, 0)
            bound = (qmax * stat_ref[jc * FOX_NSTAT + 2 * N_HEADS + h] + emax
                     - stat_ref[jc * FOX_NSTAT + 3 * N_HEADS + h])
            return (j >= 0) & (bound >= -FOX_SKIP_LOG)

        last_dropped = lax.while_loop(needed, lambda j: j - 1, i - 1)
        first = jnp.minimum(first, last_dropped + 1)

    l_sc[...] = jnp.zeros_like(l_sc)
    acc_sc[...] = jnp.zeros_like(acc_sc)

    def causal(sc):
        r = lax.broadcasted_iota(jnp.int32, (tq, tq), 0)
        cidx = lax.broadcasted_iota(jnp.int32, (tq, tq), 1)
        return jnp.where(cidx <= r, sc, -jnp.inf)

    def online_block(s0, diagonal):
        kb = k_ref[pl.ds(s0, tq), :]
        vb = v_ref[pl.ds(s0, tq), :]
        bias = c_q0 - ct_ref[:, pl.ds(s0, tq)]
        for h in range(N_HEADS):
            sc = _dot_nt(qh[h], kb) + bias[h:h + 1, :]
            if diagonal:
                sc = causal(sc)
            m_prev = m_sc[h]
            m_new = jnp.maximum(m_prev, jnp.max(sc, axis=1, keepdims=True))
            alpha = jnp.exp(m_prev - m_new)
            p = jnp.exp(sc - jnp.tile(m_new, (1, tq // LANE)))
            l_sc[h] = alpha * l_sc[h] + jnp.sum(p, axis=1, keepdims=True)
            acc_sc[h] = acc_sc[h] * jnp.tile(alpha, (1, BRANCH_W // LANE)) + _dot(p.astype(BF16), vb)
            m_sc[h] = m_new

    tops = [stat_ref[i * FOX_NSTAT + h] * stat_ref[i * FOX_NSTAT + 2 * N_HEADS + h] for h in range(N_HEADS)]
    c_tile = c_ref[...]
    shift = [c_tile[:, h:h + 1] - c_q0[h:h + 1, :] - tops[h] for h in range(N_HEADS)]

    q_heads = jnp.concatenate(qh, axis=0)

    def fixed_block(s0, diagonal):
        kb = k_ref[pl.ds(s0, tq), :]
        vb = v_ref[pl.ds(s0, tq), :]
        bias = c_q0 - ct_ref[:, pl.ds(s0, tq)]
        sc_all = _dot_nt(q_heads, kb)
        probs = []
        for h in range(N_HEADS):
            sc = sc_all[h * tq:(h + 1) * tq, :] + bias[h:h + 1, :] + shift[h]
            if diagonal:
                sc = causal(sc)
            p = jnp.exp(sc)
            l_sc[h] += p[:, :LANE] + p[:, LANE:]
            probs.append(p.astype(BF16))
        pv = _dot(jnp.concatenate(probs, axis=0), vb)
        for h in range(N_HEADS):
            acc_sc[h] += pv[h * tq:(h + 1) * tq, :]

    def run(block, row_sum):
        lax.fori_loop(first, i, lambda j, carry: (block(pl.multiple_of(j * tq, tq), False), carry)[1], 0)
        block(q0, True)
        out = jnp.zeros((tq, BRANCH_W), F32)
        for h in range(N_HEADS):
            out = out + acc_sc[h] * hm[h:h + 1, :] / row_sum(l_sc[h])
        o_ref[...] = out.astype(o_ref.dtype)

    def run_online():
        m_sc[...] = jnp.full_like(m_sc, -jnp.inf)
        run(online_block, lambda l: jnp.tile(l, (1, BRANCH_W // LANE)))

    def run_fixed():
        run(fixed_block, lambda l: jnp.sum(l, axis=1, keepdims=True))

    lax.cond(functools.reduce(jnp.maximum, tops) <= FOX_FIXED_MAX, run_fixed, run_online)


def fox_attention(proj, ct, c_rows, stats):
    s = proj.shape[0]
    tq = min(FOX_TILE, s)
    full = lambda j: pl.BlockSpec((s, BRANCH_W), lambda i, j=j: (0, j), pipeline_mode=pl.Buffered(1))
    return pl.pallas_call(
        functools.partial(_fox_kernel, tq=tq),
        grid=(s // tq,),
        in_specs=[pl.BlockSpec(memory_space=pltpu.SMEM),
                  pl.BlockSpec((tq, BRANCH_W), lambda i: (i, 4)), full(5), full(6),
                  _const_spec((SUBLANE, s)), pl.BlockSpec((tq, LANE), lambda i: (i, 0)),
                  _const_spec((N_HEADS, BRANCH_W))],
        out_specs=pl.BlockSpec((tq, BRANCH_W), lambda i: (i, 0)),
        out_shape=jax.ShapeDtypeStruct((s, BRANCH_W), BF16),
        scratch_shapes=[pltpu.VMEM((N_HEADS, tq, LANE), F32),
                        pltpu.VMEM((N_HEADS, tq, LANE), F32),
                        pltpu.VMEM((N_HEADS, tq, BRANCH_W), F32)],
        compiler_params=_params("parallel"),
        name="fox_attention",
    )(stats, proj, proj, proj, ct, c_rows, _head_masks())


def _pool_kernel(u_ref, w_ref, scale_ref, o_ref, ext, *, tile):
    i = pl.program_id(0)

    @pl.when(i == 0)
    def _():
        ext[pl.ds(0, POOL_HALO), :] = jnp.zeros((POOL_HALO, BRANCH_W), F32)

    u = u_ref[...].astype(F32)
    ext[pl.ds(POOL_HALO, tile), :] = u
    pos = (i * tile + lax.broadcasted_iota(jnp.int32, (tile, 1), 0) + 1).astype(F32)
    halves = []
    for half in range(BRANCH_W // LANE):
        lanes = pl.ds(half * LANE, LANE)
        w_small, w_big = POOL_WINDOWS[2 * half], POOL_WINDOWS[2 * half + 1]
        run = u[:, half * LANE:(half + 1) * LANE]
        sums = {}
        for j in range(1, w_big):
            if j == w_small:
                sums[w_small] = run
            run = run + ext[pl.ds(POOL_HALO - j, tile), lanes]
        sums[w_big] = run
        lane = lax.broadcasted_iota(jnp.int32, (1, LANE), 1)
        small = lane < POOL_GROUP
        total = jnp.where(small, sums[w_small], sums[w_big])
        count = jnp.where(small, jnp.minimum(pos, float(w_small)), jnp.minimum(pos, float(w_big)))
        halves.append(total / count)
    mean = jnp.concatenate(halves, axis=1)
    d = (mean - u).astype(BF16)
    y = _dot(d, w_ref[...]) * scale_ref[...]
    o_ref[...] = y.astype(o_ref.dtype)
    ext[pl.ds(0, POOL_HALO), :] = u[tile - POOL_HALO:, :]


def pool_mixer(proj, w_pool, scale, tile=2048):
    s = proj.shape[0]
    tile = min(tile, s)
    ng = len(POOL_WINDOWS)
    w_bd = jnp.zeros((BRANCH_W, BRANCH_W), F32)
    for gi in range(ng):
        lo = gi * POOL_GROUP
        w_bd = w_bd.at[lo:lo + POOL_GROUP, lo:lo + POOL_GROUP].set(w_pool[gi].astype(F32))
    return pl.pallas_call(
        functools.partial(_pool_kernel, tile=tile),
        grid=(s // tile,),
        in_specs=[pl.BlockSpec((tile, BRANCH_W), lambda i: (i, 7)),
                  _const_spec((BRANCH_W, BRANCH_W)), _const_spec((1, BRANCH_W))],
        out_specs=pl.BlockSpec((tile, BRANCH_W), lambda i: (i, 0)),
        out_shape=jax.ShapeDtypeStruct((s, BRANCH_W), BF16),
        scratch_shapes=[pltpu.VMEM((tile + POOL_HALO, BRANCH_W), F32)],
        compiler_params=_params("arbitrary"),
        name="pool_mixer",
    )(proj, w_bd.astype(BF16), scale.reshape(1, BRANCH_W).astype(F32))


def _ret_kernel(q_ref, k_ref, v_ref, g_ref, cos_ref, sin_ref, perm_ref, ones_ref, hm_ref,
                dstack_ref, xi_ref, zeta_ref, gc_ref, gng_ref, gnb_ref, o_ref, st_ref, *, tile):
    c = RET_CHUNK

    @pl.when(pl.program_id(0) == 0)
    def _():
        st_ref[...] = jnp.zeros_like(st_ref)

    perm = perm_ref[...]
    ones_bd = ones_ref[...]
    bd_mask = ones_bd.astype(F32)
    hm = hm_ref[...]

    cos = cos_ref[...]
    sin = sin_ref[...]
    q_all = q_ref[...]
    k_all = k_ref[...]
    qr_all = q_all.astype(F32) * cos + _dot(q_all, perm) * sin
    kr_all = (k_all.astype(F32) * cos + _dot(k_all, perm) * sin) * (HEAD_DIM ** -0.5)

    outs = []
    for ci in range(tile // c):
        r0 = ci * c
        qr = qr_all[r0:r0 + c, :]
        kr = kr_all[r0:r0 + c, :]
        v = v_ref[pl.ds(r0, c), :]

        qx = jnp.concatenate([qr * hm[h:h + 1, :] for h in range(N_HEADS)], axis=0).astype(BF16)
        sc = _dot_nt(qx, kr.astype(BF16)) * dstack_ref[...]
        r = _dot(sc.astype(BF16), v)
        intra = jnp.zeros((c, BRANCH_W), F32)
        for h in range(N_HEADS):
            intra = intra + r[h * c:(h + 1) * c, :] * hm[h:h + 1, :]

        st = st_ref[...]
        inter = _dot_nt((qr * xi_ref[...]).astype(BF16), st.astype(BF16))
        upd = _dot(v.astype(F32).T.astype(BF16), (kr * zeta_ref[...]).astype(BF16))
        st_ref[...] = st * gc_ref[...] + upd * bd_mask
        outs.append(intra + inter)

    o = jnp.concatenate(outs, axis=0)
    mu = _dot_exact_rhs(o, ones_bd, terms=2) * (1.0 / HEAD_DIM)
    cen = o - mu
    var = _dot_exact_rhs(cen * cen, ones_bd, terms=2) * (1.0 / HEAD_DIM)
    y = cen * lax.rsqrt(var + LN_EPS) * gng_ref[...] + gnb_ref[...]
    o_ref[...] = (y * _silu(g_ref[...].astype(F32))).astype(o_ref.dtype)


def _rope_tables(s):
    half = HEAD_DIM // 2
    pos = np.arange(s, dtype=np.float64)
    inv_freq = ROPE_BASE ** (-np.arange(half, dtype=np.float64) / half)
    lane = np.arange(BRANCH_W)
    ang = pos[:, None] * inv_freq[lane % half][None, :]
    sign = np.where(lane % HEAD_DIM < half, -1.0, 1.0)
    return _const(np.cos(ang)), _const(np.sin(ang) * sign[None, :])


def _ret_constants():
    c = RET_CHUNK
    half = HEAD_DIM // 2
    lane = np.arange(BRANCH_W)
    partner = np.where(lane % HEAD_DIM < half, lane + half, lane - half)
    perm = lane[:, None] == partner[None, :]
    log_gamma = np.log1p(-np.exp2(-RET_DECAY_BASE - np.arange(N_HEADS, dtype=np.float64)))
    ci = np.arange(c, dtype=np.float64)
    diff = ci[:, None] - ci[None, :]
    intra = np.where(diff >= 0, np.exp(diff * log_gamma[:, None, None]), 0.0)
    dstack = intra.reshape(N_HEADS * c, c)
    lg_lane = np.repeat(log_gamma, HEAD_DIM)[None, :]
    xi = np.exp((ci[:, None] + 1.0) * lg_lane)
    zeta = np.exp((c - 1.0 - ci[:, None]) * lg_lane)
    gc = np.exp(c * lg_lane)
    return _const(perm, BF16), _const(dstack), _const(xi), _const(zeta), _const(gc)


def retention(proj, gn_g, gn_b, tile=1024):
    s = proj.shape[0]
    c = RET_CHUNK
    tile = min(tile, s)
    cos_t, sin_t = _rope_tables(s)
    perm, dstack, xi, zeta, gc = _ret_constants()
    col = lambda j: pl.BlockSpec((tile, BRANCH_W), lambda i, j=j: (i, j))
    row = pl.BlockSpec((tile, BRANCH_W), lambda i: (i, 0))
    return pl.pallas_call(
        functools.partial(_ret_kernel, tile=tile),
        grid=(s // tile,),
        in_specs=[col(8), col(9), col(10), col(11), row, row,
                  _const_spec((BRANCH_W, BRANCH_W)), _const_spec((BRANCH_W, BRANCH_W)),
                  _const_spec((N_HEADS, BRANCH_W)), _const_spec((N_HEADS * c, c)),
                  _const_spec((c, BRANCH_W)), _const_spec((c, BRANCH_W)), _const_spec((1, BRANCH_W)),
                  _const_spec((1, BRANCH_W)), _const_spec((1, BRANCH_W))],
        out_specs=row,
        out_shape=jax.ShapeDtypeStruct((s, BRANCH_W), BF16),
        scratch_shapes=[pltpu.VMEM((BRANCH_W, BRANCH_W), F32)],
        compiler_params=_params("arbitrary"),
        name="retention",
    )(proj, proj, proj, proj, cos_t, sin_t, perm, _head_ones(), _head_masks(), dstack, xi, zeta, gc,
      gn_g.reshape(1, BRANCH_W).astype(F32), gn_b.reshape(1, BRANCH_W).astype(F32))


def _merge_kernel(h_ref, o0_ref, o1_ref, o2_ref, o3_ref, x_ref, wg_ref, wb_ref, wo_ref, g_ref, *rest):
    n_cast = (len(rest) - 2) // 2
    cast_in, (xo_ref, ho_ref), cast_out = rest[:n_cast], rest[n_cast:n_cast + 2], rest[n_cast + 2:]
    for src, dst in zip(cast_in, cast_out):
        dst[...] = src[...].astype(dst.dtype)
    h = h_ref[...]
    merged = jnp.zeros(x_ref.shape, F32)
    for bi, o_ref in enumerate((o0_ref, o1_ref, o2_ref, o3_ref)):
        gate = _sigmoid(_dot(h, wg_ref[:, bi * D_MODEL:(bi + 1) * D_MODEL]))
        merged = merged + gate * _dot(o_ref[...], wb_ref[bi])
    x_new = x_ref[...] + _dot(merged.astype(BF16), wo_ref[...])
    xo_ref[...] = x_new
    ho_ref[...] = _rms(x_new, g_ref[...]).astype(ho_ref.dtype)


def merge(h, branches, x, w_gate, w_branch, w_out, next_gain, tm=512, cast=()):
    s, d = x.shape
    tm = min(tm, s)
    steps = s // tm
    row = lambda w: pl.BlockSpec((tm, w), lambda i: (i, 0))
    flat = [c.reshape(-1, c.shape[-1]) for c in cast]
    slab = lambda c: pl.BlockSpec((c.shape[0] // steps, c.shape[1]), lambda i: (i, 0))
    outs = pl.pallas_call(
        _merge_kernel,
        grid=(steps,),
        in_specs=[row(d), row(BRANCH_W), row(BRANCH_W), row(BRANCH_W), row(BRANCH_W), row(d),
                  _const_spec((d, N_BRANCH * d)), _const_spec((N_BRANCH, BRANCH_W, d)),
                  _const_spec((d, d)), _const_spec((1, d))] + [slab(c) for c in flat],
        out_specs=[row(d), row(d)] + [slab(c) for c in flat],
        out_shape=[jax.ShapeDtypeStruct((s, d), F32), jax.ShapeDtypeStruct((s, d), BF16)]
        + [jax.ShapeDtypeStruct(c.shape, BF16) for c in flat],
        compiler_params=_params("parallel"),
        name="merge",
    )(h, *branches, x, w_gate, w_branch, w_out, next_gain.reshape(1, d).astype(F32), *flat)
    return outs[0], outs[1], [o.reshape(c.shape) for o, c in zip(outs[2:], cast)]


def _ffn_kernel(h_ref, x_ref, wg_ref, wu_ref, wd_ref, g_ref, xo_ref, ho_ref, acc_ref):
    f = pl.program_id(1)

    @pl.when(f == 0)
    def _():
        acc_ref[...] = jnp.zeros_like(acc_ref)

    h = h_ref[...]
    a = _silu(_dot(h, wg_ref[...].astype(BF16))) * _dot(h, wu_ref[...].astype(BF16))
    acc_ref[...] += _dot(a.astype(BF16), wd_ref[...].astype(BF16))

    @pl.when(f == pl.num_programs(1) - 1)
    def _():
        x_new = x_ref[...] + acc_ref[...]
        xo_ref[...] = x_new
        ho_ref[...] = _rms(x_new, g_ref[...]).astype(ho_ref.dtype)


def ffn_dense(h, x, w_gate, w_up, w_down, next_gain, tm=1024, tf=512):
    s, d = x.shape
    tm = min(tm, s)
    dff = w_gate.shape[1]
    row = lambda: pl.BlockSpec((tm, d), lambda i, f: (i, 0))
    return pl.pallas_call(
        _ffn_kernel,
        grid=(s // tm, dff // tf),
        in_specs=[row(), row(),
                  pl.BlockSpec((d, tf), lambda i, f: (0, f)),
                  pl.BlockSpec((d, tf), lambda i, f: (0, f)),
                  pl.BlockSpec((tf, d), lambda i, f: (f, 0)),
                  _const_spec((1, d))],
        out_specs=[row(), row()],
        out_shape=[jax.ShapeDtypeStruct((s, d), F32), jax.ShapeDtypeStruct((s, d), BF16)],
        scratch_shapes=[pltpu.VMEM((tm, d), F32)],
        compiler_params=_params("parallel", "arbitrary"),
        name="ffn_dense",
    )(h, x, w_gate, w_up, w_down, next_gain.reshape(1, d).astype(F32))


MERGE_CAST_TILE = 256
MOE_TOK_TILE = 256
MOE_ROUTE_TILES = 4
MOE_ROW_BLOCK = 512
MOE_GATHER_ROWS = 128
MOE_GATHER_TILES = 3
MOE_VMEM_LIMIT = 60 * 1024 * 1024
MOE_Y_BLOCK = 128
MOE_Y_FETCH = MOE_TOK_TILE // MOE_Y_BLOCK + 1


def _router_logits(x, w):
    xh, xm, _ = _split3(x)
    wh, wm, _ = _split3(w)
    packed = (wh.astype(F32) + pltpu.roll(wm.astype(F32), N_EXPERTS, axis=1)).astype(BF16)
    s = _dot(xh, packed) + _dot(xm, packed)
    return s + pltpu.roll(s, LANE - N_EXPERTS, axis=1)


def _route_kernel(x_ref, ng_ref, router_ref, ltri_ref, pos_ref, gate_ref, post_ref, before_ref, total_ref,
                  carry_ref):
    tm = x_ref.shape[0]

    @pl.when(pl.program_id(0) == 0)
    def _():
        carry_ref[...] = jnp.zeros_like(carry_ref)

    hn = _rms(x_ref[...], ng_ref[...])
    logits = _router_logits(hn, router_ref[...])
    lane = lax.broadcasted_iota(jnp.int32, (tm, LANE), 1)
    logits = jnp.where(lane < N_EXPERTS, logits, -jnp.inf)
    v1 = jnp.max(logits, axis=1, keepdims=True)
    i1 = jnp.min(jnp.where(logits == v1, lane, LANE), axis=1, keepdims=True)
    rest = jnp.where(lane == i1, -jnp.inf, logits)
    v2 = jnp.max(rest, axis=1, keepdims=True)
    i2 = jnp.min(jnp.where(rest == v2, lane, LANE), axis=1, keepdims=True)
    w1 = 1.0 / (1.0 + jnp.exp(v2 - v1))
    gate_ref[...] = jnp.where(lane == i1, w1, 0.0) + jnp.where(lane == i2, 1.0 - w1, 0.0)

    member = jnp.where((lane == i1) | (lane == i2), 1.0, 0.0)
    carry = carry_ref[...]
    ranks = []
    for blk in range(tm // MOE_TOK_TILE):
        before_ref[blk] = carry
        mb = member[blk * MOE_TOK_TILE:(blk + 1) * MOE_TOK_TILE]
        ranks.append(_dot(ltri_ref[...], mb.astype(BF16)) + carry)
        carry = carry + jnp.sum(mb, axis=0, keepdims=True)
    pos = jnp.where(member > 0.0, jnp.concatenate(ranks, axis=0), -1.0)
    pos_ref[...] = pos
    post_ref[...] = pos.T[:SUBLANE, :]
    carry_ref[...] = carry
    total_ref[...] = carry


def moe_route(x, norm_gain, router):
    s, d = x.shape
    tm = min(MOE_ROUTE_TILES * MOE_TOK_TILE, s)
    nsub = tm // MOE_TOK_TILE
    nt = s // tm
    router_p = jnp.zeros((d, LANE), F32).at[:, :N_EXPERTS].set(router.astype(F32))
    tt = min(MOE_TOK_TILE, s)
    ltri = _const(np.tril(np.ones((tt, tt)), -1), BF16)
    row = pl.BlockSpec((tm, LANE), lambda i: (i, 0))
    return pl.pallas_call(
        _route_kernel,
        grid=(nt,),
        in_specs=[pl.BlockSpec((tm, d), lambda i: (i, 0)), _const_spec((1, d)), _const_spec((d, LANE)),
                  _const_spec((tt, tt))],
        out_specs=[row, row, pl.BlockSpec((SUBLANE, tm), lambda i: (0, i)),
                   pl.BlockSpec((nsub, 1, LANE), lambda i: (i, 0, 0)), pl.BlockSpec((1, LANE), lambda i: (0, 0))],
        out_shape=[jax.ShapeDtypeStruct((s, LANE), F32), jax.ShapeDtypeStruct((s, LANE), F32),
                   jax.ShapeDtypeStruct((SUBLANE, s), F32), jax.ShapeDtypeStruct((nt * nsub, 1, LANE), F32),
                   jax.ShapeDtypeStruct((1, LANE), F32)],
        scratch_shapes=[pltpu.VMEM((1, LANE), F32)],
        compiler_params=_params("arbitrary"),
        name="moe_route",
    )(x, norm_gain.reshape(1, d).astype(F32), router_p, ltri)


def _moe_ffn_kernel(be_ref, r0_ref, tlo_ref, thi_ref, nv_ref, h_ref, post_ref, wg_ref, wu_ref, wd_ref,
                    y_ref, x_sc, acc_sc, *, tt, nsub):
    b = pl.program_id(0)
    f = pl.program_id(1)
    nb = pl.num_programs(0)
    valid = b < nv_ref[0]
    tmb, d = acc_sc.shape
    gr = tmb // nsub
    last_tile = h_ref.shape[0] // tt - 1

    def picked(e, want, t):
        t0 = pl.multiple_of(t * tt, tt)
        p = post_ref[pl.ds(e, 1), pl.ds(t0, tt)]
        sel = jnp.where(p == want, 1.0, 0.0).astype(BF16)
        return _dot(sel, h_ref[pl.ds(t0, tt), :])

    def gather_head(blk, sb):
        e = be_ref[blk]
        lo = tlo_ref[blk * nsub + sb]
        hi = thi_ref[blk * nsub + sb]
        want = (lax.broadcasted_iota(jnp.int32, (gr, 1), 0) + (r0_ref[blk] + sb * gr)).astype(F32)
        rows = picked(e, want, lo)
        for k in range(1, MOE_GATHER_TILES):
            rows = rows + picked(e, jnp.where(lo + k <= hi, want, -2.0), jnp.minimum(lo + k, last_tile))
        return rows.astype(BF16)

    def gather_tail(slot, blk, sb):
        e = be_ref[blk]
        want = (lax.broadcasted_iota(jnp.int32, (gr, 1), 0) + (r0_ref[blk] + sb * gr)).astype(F32)
        rows = pl.ds(pl.multiple_of(sb * gr, gr), gr)

        def more(t, carry):
            x_sc[slot, rows, :] = (x_sc[slot, rows, :].astype(F32) + picked(e, want, t)).astype(BF16)
            return carry

        lax.fori_loop(tlo_ref[blk * nsub + sb] + MOE_GATHER_TILES, thi_ref[blk * nsub + sb] + 1, more, 0)

    @pl.when((b == 0) & (f == 0))
    def _():
        acc_sc[...] = jnp.zeros_like(acc_sc)
        for sb in range(nsub):
            x_sc[0, pl.ds(sb * gr, gr), :] = gather_head(0, sb)
            gather_tail(0, 0, sb)

    nxt = jnp.minimum(b + 1, nb - 1)
    nslot = (b + 1) % 2

    @pl.when(valid)
    def _():
        x_sc[nslot, pl.ds(pl.multiple_of(f * gr, gr), gr), :] = gather_head(nxt, f)
        xb = x_sc[b % 2]
        a = _silu(_dot(xb, wg_ref[0])) * _dot(xb, wu_ref[0])
        acc_sc[...] = jnp.where(f == 0, 0.0, acc_sc[...]) + _dot(a.astype(BF16), wd_ref[0].astype(BF16))

    @pl.when(valid & (thi_ref[nxt * nsub + f] - tlo_ref[nxt * nsub + f] >= MOE_GATHER_TILES))
    def _():
        gather_tail(nslot, nxt, f)

    @pl.when(f == nsub - 1)
    def _():
        y_ref[...] = jnp.where(valid, acc_sc[...], 0.0).astype(y_ref.dtype)


def moe_ffn(h, post, sched, w_gate, w_up, w_down):
    s, d = h.shape
    ne, _, dff = w_gate.shape
    tmb = MOE_ROW_BLOCK
    tt = min(MOE_TOK_TILE, s)
    nb = sched[0].shape[0]
    nf = tmb // MOE_GATHER_ROWS
    tf = dff // nf

    def fidx(b, f, nv):
        return jnp.where(b < nv[0], f, nf - 1)

    grid_spec = pltpu.PrefetchScalarGridSpec(
        num_scalar_prefetch=5,
        grid=(nb, nf),
        in_specs=[pl.BlockSpec((s, d), lambda b, f, *_: (0, 0), pipeline_mode=pl.Buffered(1)),
                  pl.BlockSpec((SUBLANE, s), lambda b, f, *_: (0, 0), pipeline_mode=pl.Buffered(1)),
                  pl.BlockSpec((1, d, tf), lambda b, f, be, r0, tlo, thi, nv: (be[b], 0, fidx(b, f, nv))),
                  pl.BlockSpec((1, d, tf), lambda b, f, be, r0, tlo, thi, nv: (be[b], 0, fidx(b, f, nv))),
                  pl.BlockSpec((1, tf, d), lambda b, f, be, r0, tlo, thi, nv: (be[b], fidx(b, f, nv), 0))],
        out_specs=pl.BlockSpec((tmb, d), lambda b, f, *_: (b, 0)),
        scratch_shapes=[pltpu.VMEM((2, tmb, d), BF16), pltpu.VMEM((tmb, d), F32)],
    )
    return pl.pallas_call(
        functools.partial(_moe_ffn_kernel, tt=tt, nsub=nf),
        grid_spec=grid_spec,
        out_shape=jax.ShapeDtypeStruct((nb * tmb, d), BF16),
        compiler_params=pltpu.CompilerParams(dimension_semantics=("arbitrary", "arbitrary"),
                                             vmem_limit_bytes=MOE_VMEM_LIMIT),
        name="moe_ffn",
    )(*sched, h, post, w_gate, w_up, w_down)


def _moe_combine_kernel(kb_ref, off_ref, lim_ref, x_ref, pos_ref, gate_ref, fg_ref, *rest):
    y_refs, o_ref, acc_sc = rest[:-2], rest[-2], rest[-1]
    t = pl.program_id(0)
    yb = y_refs[0].shape[0]
    pos = pos_ref[...]
    gate = gate_ref[...]
    col = lax.broadcasted_iota(jnp.int32, (1, yb), 1).astype(F32)

    def routed(e):
        pe = pos[:, e:e + 1]
        ge = gate[:, e:e + 1]
        r = jnp.where(pe >= 0.0, pe + off_ref[t * N_EXPERTS + e].astype(F32), -1.0)
        return r, ge

    def picked(e, k, r):
        sel = jnp.where(r == col + float(k * yb), 1.0, 0.0).astype(BF16)
        return _dot(sel, y_refs[MOE_Y_FETCH * e + k][...])

    col2 = lax.broadcasted_iota(jnp.int32, (1, 2 * yb), 1).astype(F32)
    acc = x_ref[...]
    for e in range(N_EXPERTS):
        r, ge = routed(e)
        pair = jnp.concatenate([y_refs[MOE_Y_FETCH * e][...], y_refs[MOE_Y_FETCH * e + 1][...]], axis=0)
        acc = acc + ge * _dot(jnp.where(r == col2, 1.0, 0.0).astype(BF16), pair)
    acc_sc[...] = acc
    for e in range(N_EXPERTS):
        for k in range(2, MOE_Y_FETCH):
            @pl.when(lim_ref[t * N_EXPERTS + e] > k * yb)
            def _(k=k, e=e):
                r, ge = routed(e)
                acc_sc[...] += ge * picked(e, k, r)
    o_ref[...] = _rms(acc_sc[...], fg_ref[...]).astype(o_ref.dtype)


def moe_combine(x, pos, gate, y, kb, off, lim, final_gain):
    s, d = x.shape
    tm = min(MOE_TOK_TILE, s)
    yb = MOE_Y_BLOCK
    last = y.shape[0] // yb - 1

    def yspec(e, k):
        def index(t, kb_r, off_r, lim_r):
            blk = jnp.minimum(kb_r[t * N_EXPERTS + e] + k, last)
            return (blk if k < 1 else jnp.where(lim_r[t * N_EXPERTS + e] > k * yb, blk, 0), 0)
        return pl.BlockSpec((yb, d), index)

    grid_spec = pltpu.PrefetchScalarGridSpec(
        num_scalar_prefetch=3,
        grid=(s // tm,),
        in_specs=[pl.BlockSpec((tm, d), lambda t, *_: (t, 0)),
                  pl.BlockSpec((tm, LANE), lambda t, *_: (t, 0)),
                  pl.BlockSpec((tm, LANE), lambda t, *_: (t, 0)),
                  pl.BlockSpec((1, d), lambda t, *_: (0, 0))]
        + [yspec(e, k) for e in range(N_EXPERTS) for k in range(MOE_Y_FETCH)],
        out_specs=pl.BlockSpec((tm, d), lambda t, *_: (t, 0)),
        scratch_shapes=[pltpu.VMEM((tm, d), F32)],
    )
    return pl.pallas_call(
        _moe_combine_kernel,
        grid_spec=grid_spec,
        out_shape=jax.ShapeDtypeStruct((s, d), F32),
        compiler_params=_params("arbitrary"),
        name="moe_combine",
    )(kb, off, lim, x, pos, gate, final_gain.reshape(1, d).astype(F32), *([y] * (MOE_Y_FETCH * N_EXPERTS)))


def _moe_schedule(before, total, s):
    tmb, yb, gr = MOE_ROW_BLOCK, MOE_Y_BLOCK, MOE_GATHER_ROWS
    nb = 2 * s // tmb + N_EXPERTS
    counts = total[0, :N_EXPERTS].astype(jnp.int32)
    nblk = (counts + tmb - 1) // tmb
    end = jnp.cumsum(nblk)
    first = end - nblk
    nvalid = end[-1]
    b = jnp.minimum(jnp.arange(nb, dtype=jnp.int32), nvalid - 1)
    blk_e = jnp.sum(b[:, None] >= end[None, :], axis=1).astype(jnp.int32)
    r0 = (b - first[blk_e]) * tmb
    cb = before[:, 0, :N_EXPERTS].astype(jnp.int32)
    r0s = (r0[:, None] + gr * jnp.arange(tmb // gr, dtype=jnp.int32)[None, :]).reshape(-1)
    cbe = cb[:, jnp.repeat(blk_e, tmb // gr)]
    tlo = (jnp.sum(cbe <= r0s[None, :], axis=0) - 1).astype(jnp.int32)
    thi = (jnp.sum(cbe < (r0s + gr)[None, :], axis=0) - 1).astype(jnp.int32)
    sched = (blk_e, r0.astype(jnp.int32), tlo, thi, nvalid.reshape(1).astype(jnp.int32))
    row_start = first[None, :] * tmb + cb
    kb = row_start // yb
    off = first[None, :] * tmb - kb * yb
    n_te = jnp.concatenate([cb[1:], counts[None, :]], axis=0) - cb
    lim = row_start - kb * yb + n_te
    flat = lambda a: a.reshape(-1).astype(jnp.int32)
    return sched, flat(kb), flat(off), flat(lim)


def moe_sparse(h, x, norm_gain, router, w_gate, w_up, w_down, final_gain):
    s, _ = x.shape
    pos, gate, post, before, total = moe_route(x, norm_gain, router)
    sched, kb, off, lim = _moe_schedule(before, total, s)
    y = moe_ffn(h, post, sched, w_gate, w_up, w_down)
    return moe_combine(x, pos, gate, y, kb, off, lim, final_gain)


FOX_F0 = 7 * BRANCH_W


def _split_w_in_kernel(wt_ref, mix_ref, f_ref, gate_ref):
    rest0 = FOX_F0 + N_HEADS
    gate0 = N_MIX_COLS + N_HEADS
    mix_ref[:, :FOX_F0] = wt_ref[pl.ds(0, FOX_F0), :].T.astype(BF16)
    mix_ref[:, FOX_F0:] = wt_ref[pl.ds(rest0, N_MIX_COLS - FOX_F0), :].T.astype(BF16)
    gate_ref[...] = wt_ref[pl.ds(gate0, N_BRANCH * D_MODEL), :].T.astype(BF16)
    f_rows = jnp.concatenate([wt_ref[pl.ds(FOX_F0, N_HEADS), :], jnp.zeros((LANE - N_HEADS, LANE), F32)], axis=0)
    f_ref[...] = f_rows.T.astype(BF16)


def _mixer_weights(w_in, layer, tr=LANE):
    _, d, cols = w_in.shape
    w_t = jnp.swapaxes(w_in, 1, 2)
    row = lambda w: pl.BlockSpec((tr, w), lambda i: (i, 0))
    return pl.pallas_call(
        _split_w_in_kernel,
        grid=(d // tr,),
        in_specs=[pl.BlockSpec((None, cols, tr), lambda i: (layer, 0, i))],
        out_specs=[row(N_MIX_COLS), row(LANE), row(N_BRANCH * D_MODEL)],
        out_shape=[jax.ShapeDtypeStruct((d, N_MIX_COLS), BF16), jax.ShapeDtypeStruct((d, LANE), BF16),
                   jax.ShapeDtypeStruct((d, N_BRANCH * D_MODEL), BF16)],
        compiler_params=_params("parallel"),
        name="split_w_in",
    )(w_t)


def kernel(x, w_in, w_branch, w_out, norm_mix_g, hgrn_lb_logits, hgrn_norm_g, fox_f_bias, pool_w, pool_scale,
           ret_gn_g, ret_gn_b, norm_ffn_g, ffn_w_gate, ffn_w_up, ffn_w_down, moe_router, moe_w_gate, moe_w_up,
           moe_w_down, final_norm_g):
    b, s, d = x.shape
    assert b == 1 and d == D_MODEL
    depth = w_in.shape[0]
    assert depth == 2, "layer 0 uses the dense FFN, layer 1 the experts and the final norm"
    xs = x.reshape(s, d)
    h = None
    out = None
    for layer in range(depth):
        w_mix, w_f, w_gate = _mixer_weights(w_in, layer)
        if layer == 0:
            proj, h = norm_matmul(xs, norm_mix_g[0], w_mix)
        else:
            proj = matmul(h, w_mix)
        ct, c_rows, fox_stats = fox_gate(h, proj, w_f, fox_f_bias[layer])
        branches = (
            hgrn2(proj, hgrn_lb_logits, hgrn_norm_g[layer], layer),
            fox_attention(proj, ct, c_rows, fox_stats),
            pool_mixer(proj, pool_w[layer], pool_scale[layer]),
            retention(proj, ret_gn_g[layer], ret_gn_b[layer]),
        )
        li = layer // 2
        if layer % 2 == 0:
            xs, h2, (ffn_wg, ffn_wu) = merge(h, branches, xs, w_gate, w_branch[layer].astype(BF16),
                                             w_out[layer].astype(BF16), norm_ffn_g[layer],
                                             cast=(ffn_w_gate[li], ffn_w_up[li]))
            xs, h = ffn_dense(h2, xs, ffn_wg, ffn_wu, ffn_w_down[li], norm_mix_g[layer + 1])
        else:
            xs, h2, (moe_wg, moe_wu) = merge(h, branches, xs, w_gate, w_branch[layer].astype(BF16),
                                             w_out[layer].astype(BF16), norm_ffn_g[layer], tm=MERGE_CAST_TILE,
                                             cast=(moe_w_gate[li], moe_w_up[li]))
            out = moe_sparse(h2, xs, norm_ffn_g[layer], moe_router[li], moe_wg, moe_wu, moe_w_down[li],
                             final_norm_g)
    return out.reshape(b, s, d)
```

```python
import functools
import math

import jax
import jax.numpy as jnp
import numpy as np
from jax import lax
from jax.experimental import pallas as pl
from jax.experimental.pallas import tpu as pltpu

D_MODEL = 1024
N_BRANCH = 4
BRANCH_W = D_MODEL // N_BRANCH
HEAD_DIM = 64
N_HEADS = BRANCH_W // HEAD_DIM
POOL_WINDOWS = (2, 4, 8, 16)
POOL_GROUP = BRANCH_W // len(POOL_WINDOWS)
POOL_HALO = 16
RET_DECAY_BASE = 5.0
ROPE_BASE = 10000.0
N_EXPERTS = 8
RMS_EPS = 1e-6
LN_EPS = 1e-5
N_MIX_COLS = 12 * BRANCH_W

LANE = 128
SUBLANE = 8
VMEM_LIMIT = 56 * 1024 * 1024

HG_CHUNK = 64
HG_SUB = 16
HG_FAST_MIN_LOGDECAY = -60.0
RET_CHUNK = 256

F32 = jnp.float32
BF16 = jnp.bfloat16
NT_DIMS = (((1,), (1,)), ((), ()))


def _params(*sem):
    return pltpu.CompilerParams(dimension_semantics=sem, vmem_limit_bytes=VMEM_LIMIT)


def _const_spec(shape):
    nd = len(shape)
    return pl.BlockSpec(shape, lambda *_: (0,) * nd, pipeline_mode=pl.Buffered(1))


def _split3(x):
    hi = x.astype(BF16)
    r1 = x - hi.astype(F32)
    mid = r1.astype(BF16)
    lo = (r1 - mid.astype(F32)).astype(BF16)
    return hi, mid, lo


def _dot(a, b):
    return jnp.dot(a, b, preferred_element_type=F32)


def _dot_nt(a, b):
    return lax.dot_general(a, b, NT_DIMS, preferred_element_type=F32)


def _dot_exact_rhs(x, m_bf16, terms=3):
    return sum(_dot(part, m_bf16) for part in _split3(x)[:terms])


def _dot_exact_lhs(m_bf16, x, terms=3):
    return sum(_dot(m_bf16, part) for part in _split3(x)[:terms])


def _sigmoid(x):
    return 1.0 / (1.0 + jnp.exp(-x))


def _silu(x):
    return x * _sigmoid(x)


def _rms(x, gain):
    return x * lax.rsqrt(jnp.mean(x * x, axis=-1, keepdims=True) + RMS_EPS) * gain


def _const(a, dtype=F32):
    return jnp.asarray(np.asarray(a, np.float32), dtype)


def _head_of(n):
    return np.arange(n) // HEAD_DIM


def _head_ones():
    h = _head_of(BRANCH_W)
    return _const(h[:, None] == h[None, :], BF16)


def _head_masks():
    return _const(_head_of(BRANCH_W)[None, :] == np.arange(N_HEADS)[:, None])


def _norm_matmul_kernel(x_ref, g_ref, b_ref, o_ref, h_ref):
    @pl.when(pl.program_id(1) == 0)
    def _():
        h_ref[...] = _rms(x_ref[...], g_ref[...]).astype(h_ref.dtype)

    o_ref[...] = _dot(h_ref[...], b_ref[...]).astype(o_ref.dtype)


def norm_matmul(x, gain, b, tm=2048, tn=1024):
    m, k = x.shape
    _, n = b.shape
    tm = min(tm, m)
    return pl.pallas_call(
        _norm_matmul_kernel,
        grid=(m // tm, n // tn),
        in_specs=[pl.BlockSpec((tm, k), lambda i, j: (i, 0)), _const_spec((1, k)),
                  pl.BlockSpec((k, tn), lambda i, j: (0, j))],
        out_specs=[pl.BlockSpec((tm, tn), lambda i, j: (i, j)), pl.BlockSpec((tm, k), lambda i, j: (i, 0))],
        out_shape=[jax.ShapeDtypeStruct((m, n), BF16), jax.ShapeDtypeStruct((m, k), BF16)],
        compiler_params=_params("arbitrary", "arbitrary"),
        name="in_proj_norm",
    )(x, gain.reshape(1, k).astype(F32), b)


def _matmul_kernel(a_ref, b_ref, o_ref):
    o_ref[...] = _dot(a_ref[...], b_ref[...]).astype(o_ref.dtype)


def matmul(a, b, out_dtype=BF16, tm=2048, tn=1024):
    m, k = a.shape
    _, n = b.shape
    tm = min(tm, m)
    return pl.pallas_call(
        _matmul_kernel,
        grid=(n // tn, m // tm),
        in_specs=[pl.BlockSpec((tm, k), lambda j, i: (i, 0)),
                  pl.BlockSpec((k, tn), lambda j, i: (0, j))],
        out_specs=pl.BlockSpec((tm, tn), lambda j, i: (i, j)),
        out_shape=jax.ShapeDtypeStruct((m, n), out_dtype),
        compiler_params=_params("parallel", "parallel"),
        name="in_proj",
    )(a, b)


def _hgrn_kernel(q_ref, f_ref, i_ref, g_ref, lbl_ref, ng_ref, ones_ref, hm_ref, tril_ref, halfsum_ref, fmask_ref,
                 o_ref, st_ref, bpad, kpad, vpad, astack, lf_sc, kk_sc, o_sc, *, layer, tile):
    c, sub = HG_CHUNK, HG_SUB
    nsub = c // sub
    half = c // 2

    @pl.when(pl.program_id(0) == 0)
    def _():
        st_ref[...] = jnp.zeros_like(st_ref)
        bpad[...] = jnp.zeros_like(bpad)
        kpad[...] = jnp.zeros_like(kpad)
        vpad[...] = jnp.zeros_like(vpad)

    lbl = lbl_ref[...]
    e = jnp.exp(lbl - jnp.max(lbl, axis=0, keepdims=True))
    p = e / jnp.sum(e, axis=0, keepdims=True)
    lb = jnp.zeros((1, BRANCH_W), F32)
    for l in range(1, layer + 1):
        lb = lb + p[l:l + 1, :]

    ones_bd = ones_ref[...]
    hm = hm_ref[...]
    tril = tril_ref[...]
    row = lax.broadcasted_iota(jnp.int32, (c, 1), 0)
    row_in_sub = row % sub
    bd_mask = ones_bd.astype(F32)

    sig = _sigmoid(f_ref[...].astype(F32))
    logf_all = jnp.log(lb + (1.0 - lb) * sig)
    lf_sc[...] = _dot_exact_lhs(tril, logf_all, terms=2)
    kk_sc[...] = (1.0 - lb) * (1.0 - sig)
    min_decay = jnp.min(_dot(halfsum_ref[...], logf_all.astype(BF16)))

    def load(ci):
        r0 = pl.multiple_of(ci * c, c)
        q = q_ref[pl.ds(r0, c), :].astype(F32)
        v = i_ref[pl.ds(r0, c), :].astype(F32)
        kk = kk_sc[pl.ds(r0, c), :]
        b = lf_sc[pl.ds(r0, c), :]
        return r0, q, v, kk, b

    def finish(r0, q_decayed, v, kk, b, intra):
        st = st_ref[...]
        inter = _dot_nt(q_decayed.astype(BF16), st.astype(BF16))
        b_last = b[c - 1:c, :]
        ks_end = (kk * jnp.exp(b_last - b)).astype(BF16)
        upd = _dot(v.T.astype(BF16), ks_end)
        st_ref[...] = st * jnp.exp(b_last) + upd * bd_mask
        o_sc[pl.ds(r0, c), :] = intra + inter

    def fast_chunk(ci, carry):
        r0, q, v, kk, b = load(ci)
        second = row >= half
        m_row = b[half - 1:half, :]
        mref = jnp.where(second, m_row, 0.0)
        qp = q * jnp.exp(b - mref)
        kp = kk * jnp.exp(mref - b)
        e_m = jnp.exp(m_row)
        kaug = jnp.concatenate([kp, kp[:half, :] * e_m], axis=0)
        vaug = jnp.concatenate([v, v[:half, :]], axis=0)
        k_heads = jnp.concatenate([kaug * hm[h:h + 1, :] for h in range(N_HEADS)], axis=0).astype(BF16)
        v_heads = jnp.concatenate([vaug * hm[h:h + 1, :] for h in range(N_HEADS)], axis=0).astype(BF16)
        sc = jnp.where(fmask_ref[...] > 0.0, _dot_nt(qp.astype(BF16), k_heads), 0.0)
        intra = _dot(sc.astype(BF16), v_heads)
        finish(r0, jnp.where(second, qp * e_m, qp), v, kk, b, intra)
        return carry

    def exact_chunk(ci, carry):
        r0, q, v, kk, b = load(ci)

        bpad[pl.ds(sub, c), :] = b
        kpad[pl.ds(sub, c), :] = kk
        vpad[pl.ds(sub, c), :] = v

        for d in range(sub):
            b_d = bpad[pl.ds(sub - d, c), :]
            k_d = kpad[pl.ds(sub - d, c), :]
            a = jnp.where(row_in_sub >= d, q * k_d * jnp.exp(b - b_d), 0.0)
            astack[pl.ds(d * c, c), :] = a.astype(BF16)
        pall = _dot(astack[...], ones_bd)
        intra = jnp.zeros((c, BRANCH_W), F32)
        for d in range(sub):
            intra = intra + pall[d * c:(d + 1) * c, :] * vpad[pl.ds(sub - d, c), :]

        pieces = [jnp.zeros((sub, BRANCH_W), F32)]
        for si in range(1, nsub):
            lo = si * sub
            m_i = b[lo - 1:lo, :]
            qs = q[lo:lo + sub, :] * jnp.exp(b[lo:lo + sub, :] - m_i)
            ks = (kk[:lo, :] * jnp.exp(m_i - b[:lo, :])).astype(BF16)
            qx = jnp.concatenate([qs * hm[h:h + 1, :] for h in range(N_HEADS)], axis=0).astype(BF16)
            sc = _dot_nt(qx, ks)
            r = _dot(sc.astype(BF16), v[:lo, :].astype(BF16))
            acc = jnp.zeros((sub, BRANCH_W), F32)
            for h in range(N_HEADS):
                acc = acc + r[h * sub:(h + 1) * sub, :] * hm[h:h + 1, :]
            pieces.append(acc)
        intra = intra + jnp.concatenate(pieces, axis=0)
        finish(r0, q * jnp.exp(b), v, kk, b, intra)
        return carry

    lax.cond(min_decay >= HG_FAST_MIN_LOGDECAY,
             lambda: lax.fori_loop(0, tile // c, fast_chunk, 0, unroll=2),
             lambda: lax.fori_loop(0, tile // c, exact_chunk, 0))

    o = o_sc[...]
    ms = _dot_exact_rhs(o * o, ones_bd, terms=2) * (1.0 / HEAD_DIM)
    y = o * lax.rsqrt(ms + RMS_EPS) * ng_ref[...] * _silu(g_ref[...].astype(F32))
    o_ref[...] = y.astype(o_ref.dtype)


def hgrn2(proj, lb_logits, norm_g, layer, tile=512):
    s = proj.shape[0]
    depth = lb_logits.shape[0]
    c, sub = HG_CHUNK, HG_SUB
    half = c // 2
    tile = min(tile, s)
    col = lambda j: pl.BlockSpec((tile, BRANCH_W), lambda i, j=j: (i, j))
    pos = np.arange(tile)
    tril = _const((pos[:, None] // c == pos[None, :] // c) & (pos[None, :] <= pos[:, None]), BF16)
    nhalf = tile // half
    halfsum = _const(np.arange(tile)[None, :] // half == np.arange(nhalf)[:, None], BF16)
    t = np.arange(c)[:, None]
    col_s = np.arange(c + half)[None, :]
    same_half = (col_s < c) & (col_s // half == t // half) & (col_s <= t)
    cross = (col_s >= c) & (t >= half)
    fmask = _const(np.tile(same_half | cross, (1, N_HEADS)))
    return pl.pallas_call(
        functools.partial(_hgrn_kernel, layer=layer, tile=tile),
        grid=(s // tile,),
        in_specs=[col(0), col(1), col(2), col(3),
                  _const_spec((depth, BRANCH_W)), _const_spec((1, BRANCH_W)),
                  _const_spec((BRANCH_W, BRANCH_W)), _const_spec((N_HEADS, BRANCH_W)),
                  _const_spec((tile, tile)), _const_spec((nhalf, tile)),
                  _const_spec((c, N_HEADS * (c + half)))],
        out_specs=pl.BlockSpec((tile, BRANCH_W), lambda i: (i, 0)),
        out_shape=jax.ShapeDtypeStruct((s, BRANCH_W), BF16),
        scratch_shapes=[pltpu.VMEM((BRANCH_W, BRANCH_W), F32),
                        pltpu.VMEM((c + sub, BRANCH_W), F32),
                        pltpu.VMEM((c + sub, BRANCH_W), F32),
                        pltpu.VMEM((c + sub, BRANCH_W), F32),
                        pltpu.VMEM((sub * c, BRANCH_W), BF16),
                        pltpu.VMEM((tile, BRANCH_W), F32),
                        pltpu.VMEM((tile, BRANCH_W), F32),
                        pltpu.VMEM((tile, BRANCH_W), F32)],
        compiler_params=_params("arbitrary"),
        name="hgrn2",
    )(proj, proj, proj, proj, lb_logits.astype(F32), norm_g.reshape(1, BRANCH_W).astype(F32),
      _head_ones(), _head_masks(), tril, halfsum, fmask)


FOX_TILE = 256
FOX_GATE_BLOCKS = 2
FOX_NSTAT = 16
FOX_FIXED_MAX = 30.0
FOX_SKIP_LOG = 30.0


def _fox_gate_kernel(h_ref, q_ref, k_ref, wf_ref, bias_ref, tril_ref, ones_ref,
                     ct_ref, c_ref, stat_ref, carry_ref, kmax_ref):
    @pl.when(pl.program_id(0) == 0)
    def _():
        carry_ref[...] = jnp.zeros_like(carry_ref)
        kmax_ref[...] = jnp.zeros_like(kmax_ref)

    logit = _dot(h_ref[...], wf_ref[...]) + bias_ref[...]
    logf = jnp.minimum(logit, 0.0) - jnp.log(1.0 + jnp.exp(-jnp.abs(logit)))
    carry = carry_ref[...]
    parts = []
    for blk in range(logf.shape[0] // FOX_TILE):
        part = _dot_exact_lhs(tril_ref[...], logf[blk * FOX_TILE:(blk + 1) * FOX_TILE]) + carry
        carry = part[-1:, :]
        parts.append(part)
    cum = jnp.concatenate(parts, axis=0)
    carry_ref[...] = carry
    ct_ref[...] = cum.T[:SUBLANE, :]
    c_ref[...] = cum

    ones_bd = ones_ref[...]
    q = q_ref[...].astype(F32)
    k = k_ref[...].astype(F32)
    scale = HEAD_DIM ** -0.5
    head_lane = lax.broadcasted_iota(jnp.int32, (1, BRANCH_W), 1) // HEAD_DIM
    c_heads = jnp.zeros(q.shape, F32)
    for h in range(N_HEADS):
        c_heads = jnp.where(head_lane == h, cum[:, h:h + 1], c_heads)
    slack = 1.0 + 2.0 ** -6
    qn = jnp.sqrt(_dot((q * q).astype(BF16), ones_bd)) * (scale * slack)
    kn = jnp.sqrt(_dot((k * k).astype(BF16), ones_bd)) * slack
    diag = _dot((q * k).astype(BF16), ones_bd) * scale - (2.0 ** -6) * qn * kn
    e_row = c_heads - diag
    kmax = kmax_ref[...]
    for blk in range(q.shape[0] // FOX_TILE):
        rs = slice(blk * FOX_TILE, (blk + 1) * FOX_TILE)
        kmax = jnp.maximum(kmax, jnp.max(kn[rs], axis=0, keepdims=True))
        rows = [jnp.max(qn[rs], axis=0, keepdims=True),
                jnp.max(e_row[rs], axis=0, keepdims=True),
                kmax,
                c_heads[(blk + 1) * FOX_TILE - 1:(blk + 1) * FOX_TILE, :]]
        stat_ref[blk] = jnp.concatenate(rows + [jnp.zeros((SUBLANE - len(rows), BRANCH_W), F32)], axis=0)
    kmax_ref[...] = kmax


def fox_gate(h, proj, w_f, f_bias):
    s, d = h.shape
    tile = min(FOX_GATE_BLOCKS * FOX_TILE, s)
    nblk = tile // FOX_TILE
    bias = jnp.zeros((1, LANE), F32).at[0, :N_HEADS].set(f_bias.astype(F32))
    tril = _const(np.tril(np.ones((FOX_TILE, FOX_TILE))), BF16)
    ct, c_rows, stats = pl.pallas_call(
        _fox_gate_kernel,
        grid=(s // tile,),
        in_specs=[pl.BlockSpec((tile, d), lambda i: (i, 0)),
                  pl.BlockSpec((tile, BRANCH_W), lambda i: (i, 4)),
                  pl.BlockSpec((tile, BRANCH_W), lambda i: (i, 5)),
                  _const_spec((d, LANE)), _const_spec((1, LANE)), _const_spec((FOX_TILE, FOX_TILE)),
                  _const_spec((BRANCH_W, BRANCH_W))],
        out_specs=[pl.BlockSpec((SUBLANE, tile), lambda i: (0, i)),
                   pl.BlockSpec((tile, LANE), lambda i: (i, 0)),
                   pl.BlockSpec((nblk, SUBLANE, BRANCH_W), lambda i: (i, 0, 0))],
        out_shape=[jax.ShapeDtypeStruct((SUBLANE, s), F32),
                   jax.ShapeDtypeStruct((s, LANE), F32),
                   jax.ShapeDtypeStruct((s // FOX_TILE, SUBLANE, BRANCH_W), F32)],
        scratch_shapes=[pltpu.VMEM((1, LANE), F32), pltpu.VMEM((1, BRANCH_W), F32)],
        compiler_params=_params("arbitrary"),
        name="fox_gate",
    )(h, proj, proj, w_f, bias, tril, _head_ones())
    return ct, c_rows, stats[:, :4, ::HEAD_DIM].reshape(-1)


def _fox_kernel(stat_ref, q_ref, k_ref, v_ref, ct_ref, c_ref, hm_ref, o_ref, m_sc, l_sc, acc_sc, *, tq):
    i = pl.program_id(0)
    q0 = pl.multiple_of(i * tq, tq)
    hm = hm_ref[...]
    q = q_ref[...].astype(F32) * (HEAD_DIM ** -0.5)
    qh = [(q * hm[h:h + 1, :]).astype(BF16) for h in range(N_HEADS)]
    c_q0 = ct_ref[:, pl.ds(q0, tq)][:, 0:1]

    first = i
    for h in range(N_HEADS):
        qmax = stat_ref[i * FOX_NSTAT + h]
        emax = stat_ref[i * FOX_NSTAT + N_HEADS + h]

        def needed(j, h=h, qmax=qmax, emax=emax):
            jc = jnp.maximum(j, 0)
            bound = (qmax * stat_ref[jc * FOX_NSTAT + 2 * N_HEADS + h] + emax
                     - stat_ref[jc * FOX_NSTAT + 3 * N_HEADS + h])
            return (j >= 0) & (bound >= -FOX_SKIP_LOG)

        last_dropped = lax.while_loop(needed, lambda j: j - 1, i - 1)
        first = jnp.minimum(first, last_dropped + 1)

    l_sc[...] = jnp.zeros_like(l_sc)
    acc_sc[...] = jnp.zeros_like(acc_sc)

    def causal(sc):
        r = lax.broadcasted_iota(jnp.int32, (tq, tq), 0)
        cidx = lax.broadcasted_iota(jnp.int32, (tq, tq), 1)
        return jnp.where(cidx <= r, sc, -jnp.inf)

    def online_block(s0, diagonal):
        kb = k_ref[pl.ds(s0, tq), :]
        vb = v_ref[pl.ds(s0, tq), :]
        bias = c_q0 - ct_ref[:, pl.ds(s0, tq)]
        for h in range(N_HEADS):
            sc = _dot_nt(qh[h], kb) + bias[h:h + 1, :]
            if diagonal:
                sc = causal(sc)
            m_prev = m_sc[h]
            m_new = jnp.maximum(m_prev, jnp.max(sc, axis=1, keepdims=True))
            alpha = jnp.exp(m_prev - m_new)
            p = jnp.exp(sc - jnp.tile(m_new, (1, tq // LANE)))
            l_sc[h] = alpha * l_sc[h] + jnp.sum(p, axis=1, keepdims=True)
            acc_sc[h] = acc_sc[h] * jnp.tile(alpha, (1, BRANCH_W // LANE)) + _dot(p.astype(BF16), vb)
            m_sc[h] = m_new

    tops = [stat_ref[i * FOX_NSTAT + h] * stat_ref[i * FOX_NSTAT + 2 * N_HEADS + h] for h in range(N_HEADS)]
    c_tile = c_ref[...]
    shift = [c_tile[:, h:h + 1] - c_q0[h:h + 1, :] - tops[h] for h in range(N_HEADS)]

    q_heads = jnp.concatenate(qh, axis=0)

    def fixed_block(s0, diagonal):
        kb = k_ref[pl.ds(s0, tq), :]
        vb = v_ref[pl.ds(s0, tq), :]
        bias = c_q0 - ct_ref[:, pl.ds(s0, tq)]
        sc_all = _dot_nt(q_heads, kb)
        probs = []
        for h in range(N_HEADS):
            sc = sc_all[h * tq:(h + 1) * tq, :] + bias[h:h + 1, :] + shift[h]
            if diagonal:
                sc = causal(sc)
            p = jnp.exp(sc)
            l_sc[h] += p[:, :LANE] + p[:, LANE:]
            probs.append(p.astype(BF16))
        pv = _dot(jnp.concatenate(probs, axis=0), vb)
        for h in range(N_HEADS):
            acc_sc[h] += pv[h * tq:(h + 1) * tq, :]

    def run(block, row_sum):
        lax.fori_loop(first, i, lambda j, carry: (block(pl.multiple_of(j * tq, tq), False), carry)[1], 0)
        block(q0, True)
        out = jnp.zeros((tq, BRANCH_W), F32)
        for h in range(N_HEADS):
            out = out + acc_sc[h] * hm[h:h + 1, :] / row_sum(l_sc[h])
        o_ref[...] = out.astype(o_ref.dtype)

    def run_online():
        m_sc[...] = jnp.full_like(m_sc, -jnp.inf)
        run(online_block, lambda l: jnp.tile(l, (1, BRANCH_W // LANE)))

    def run_fixed():
        run(fixed_block, lambda l: jnp.sum(l, axis=1, keepdims=True))

    lax.cond(functools.reduce(jnp.maximum, tops) <= FOX_FIXED_MAX, run_fixed, run_online)


def fox_attention(proj, ct, c_rows, stats):
    s = proj.shape[0]
    tq = min(FOX_TILE, s)
    full = lambda j: pl.BlockSpec((s, BRANCH_W), lambda i, j=j: (0, j), pipeline_mode=pl.Buffered(1))
    return pl.pallas_call(
        functools.partial(_fox_kernel, tq=tq),
        grid=(s // tq,),
        in_specs=[pl.BlockSpec(memory_space=pltpu.SMEM),
                  pl.BlockSpec((tq, BRANCH_W), lambda i: (i, 4)), full(5), full(6),
                  _const_spec((SUBLANE, s)), pl.BlockSpec((tq, LANE), lambda i: (i, 0)),
                  _const_spec((N_HEADS, BRANCH_W))],
        out_specs=pl.BlockSpec((tq, BRANCH_W), lambda i: (i, 0)),
        out_shape=jax.ShapeDtypeStruct((s, BRANCH_W), BF16),
        scratch_shapes=[pltpu.VMEM((N_HEADS, tq, LANE), F32),
                        pltpu.VMEM((N_HEADS, tq, LANE), F32),
                        pltpu.VMEM((N_HEADS, tq, BRANCH_W), F32)],
        compiler_params=_params("parallel"),
        name="fox_attention",
    )(stats, proj, proj, proj, ct, c_rows, _head_masks())


def _pool_kernel(u_ref, w_ref, scale_ref, o_ref, ext, *, tile):
    i = pl.program_id(0)

    @pl.when(i == 0)
    def _():
        ext[pl.ds(0, POOL_HALO), :] = jnp.zeros((POOL_HALO, BRANCH_W), F32)

    u = u_ref[...].astype(F32)
    ext[pl.ds(POOL_HALO, tile), :] = u
    pos = (i * tile + lax.broadcasted_iota(jnp.int32, (tile, 1), 0) + 1).astype(F32)
    halves = []
    for half in range(BRANCH_W // LANE):
        lanes = pl.ds(half * LANE, LANE)
        w_small, w_big = POOL_WINDOWS[2 * half], POOL_WINDOWS[2 * half + 1]
        run = u[:, half * LANE:(half + 1) * LANE]
        sums = {}
        for j in range(1, w_big):
            if j == w_small:
                sums[w_small] = run
            run = run + ext[pl.ds(POOL_HALO - j, tile), lanes]
        sums[w_big] = run
        lane = lax.broadcasted_iota(jnp.int32, (1, LANE), 1)
        small = lane < POOL_GROUP
        total = jnp.where(small, sums[w_small], sums[w_big])
        count = jnp.where(small, jnp.minimum(pos, float(w_small)), jnp.minimum(pos, float(w_big)))
        halves.append(total / count)
    mean = jnp.concatenate(halves, axis=1)
    d = (mean - u).astype(BF16)
    y = _dot(d, w_ref[...]) * scale_ref[...]
    o_ref[...] = y.astype(o_ref.dtype)
    ext[pl.ds(0, POOL_HALO), :] = u[tile - POOL_HALO:, :]


def pool_mixer(proj, w_pool, scale, tile=2048):
    s = proj.shape[0]
    tile = min(tile, s)
    ng = len(POOL_WINDOWS)
    w_bd = jnp.zeros((BRANCH_W, BRANCH_W), F32)
    for gi in range(ng):
        lo = gi * POOL_GROUP
        w_bd = w_bd.at[lo:lo + POOL_GROUP, lo:lo + POOL_GROUP].set(w_pool[gi].astype(F32))
    return pl.pallas_call(
        functools.partial(_pool_kernel, tile=tile),
        grid=(s // tile,),
        in_specs=[pl.BlockSpec((tile, BRANCH_W), lambda i: (i, 7)),
                  _const_spec((BRANCH_W, BRANCH_W)), _const_spec((1, BRANCH_W))],
        out_specs=pl.BlockSpec((tile, BRANCH_W), lambda i: (i, 0)),
        out_shape=jax.ShapeDtypeStruct((s, BRANCH_W), BF16),
        scratch_shapes=[pltpu.VMEM((tile + POOL_HALO, BRANCH_W), F32)],
        compiler_params=_params("arbitrary"),
        name="pool_mixer",
    )(proj, w_bd.astype(BF16), scale.reshape(1, BRANCH_W).astype(F32))


def _ret_kernel(q_ref, k_ref, v_ref, g_ref, cos_ref, sin_ref, perm_ref, ones_ref, hm_ref,
                dstack_ref, xi_ref, zeta_ref, gc_ref, gng_ref, gnb_ref, o_ref, st_ref, *, tile):
    c = RET_CHUNK

    @pl.when(pl.program_id(0) == 0)
    def _():
        st_ref[...] = jnp.zeros_like(st_ref)

    perm = perm_ref[...]
    ones_bd = ones_ref[...]
    bd_mask = ones_bd.astype(F32)
    hm = hm_ref[...]

    cos = cos_ref[...]
    sin = sin_ref[...]
    q_all = q_ref[...]
    k_all = k_ref[...]
    qr_all = q_all.astype(F32) * cos + _dot(q_all, perm) * sin
    kr_all = (k_all.astype(F32) * cos + _dot(k_all, perm) * sin) * (HEAD_DIM ** -0.5)

    outs = []
    for ci in range(tile // c):
        r0 = ci * c
        qr = qr_all[r0:r0 + c, :]
        kr = kr_all[r0:r0 + c, :]
        v = v_ref[pl.ds(r0, c), :]

        qx = jnp.concatenate([qr * hm[h:h + 1, :] for h in range(N_HEADS)], axis=0).astype(BF16)
        sc = _dot_nt(qx, kr.astype(BF16)) * dstack_ref[...]
        r = _dot(sc.astype(BF16), v)
        intra = jnp.zeros((c, BRANCH_W), F32)
        for h in range(N_HEADS):
            intra = intra + r[h * c:(h + 1) * c, :] * hm[h:h + 1, :]

        st = st_ref[...]
        inter = _dot_nt((qr * xi_ref[...]).astype(BF16), st.astype(BF16))
        upd = _dot(v.astype(F32).T.astype(BF16), (kr * zeta_ref[...]).astype(BF16))
        st_ref[...] = st * gc_ref[...] + upd * bd_mask
        outs.append(intra + inter)

    o = jnp.concatenate(outs, axis=0)
    mu = _dot_exact_rhs(o, ones_bd, terms=2) * (1.0 / HEAD_DIM)
    cen = o - mu
    var = _dot_exact_rhs(cen * cen, ones_bd, terms=2) * (1.0 / HEAD_DIM)
    y = cen * lax.rsqrt(var + LN_EPS) * gng_ref[...] + gnb_ref[...]
    o_ref[...] = (y * _silu(g_ref[...].astype(F32))).astype(o_ref.dtype)


def _rope_tables(s):
    half = HEAD_DIM // 2
    pos = np.arange(s, dtype=np.float64)
    inv_freq = ROPE_BASE ** (-np.arange(half, dtype=np.float64) / half)
    lane = np.arange(BRANCH_W)
    ang = pos[:, None] * inv_freq[lane % half][None, :]
    sign = np.where(lane % HEAD_DIM < half, -1.0, 1.0)
    return _const(np.cos(ang)), _const(np.sin(ang) * sign[None, :])


def _ret_constants():
    c = RET_CHUNK
    half = HEAD_DIM // 2
    lane = np.arange(BRANCH_W)
    partner = np.where(lane % HEAD_DIM < half, lane + half, lane - half)
    perm = lane[:, None] == partner[None, :]
    log_gamma = np.log1p(-np.exp2(-RET_DECAY_BASE - np.arange(N_HEADS, dtype=np.float64)))
    ci = np.arange(c, dtype=np.float64)
    diff = ci[:, None] - ci[None, :]
    intra = np.where(diff >= 0, np.exp(diff * log_gamma[:, None, None]), 0.0)
    dstack = intra.reshape(N_HEADS * c, c)
    lg_lane = np.repeat(log_gamma, HEAD_DIM)[None, :]
    xi = np.exp((ci[:, None] + 1.0) * lg_lane)
    zeta = np.exp((c - 1.0 - ci[:, None]) * lg_lane)
    gc = np.exp(c * lg_lane)
    return _const(perm, BF16), _const(dstack), _const(xi), _const(zeta), _const(gc)


def retention(proj, gn_g, gn_b, tile=1024):
    s = proj.shape[0]
    c = RET_CHUNK
    tile = min(tile, s)
    cos_t, sin_t = _rope_tables(s)
    perm, dstack, xi, zeta, gc = _ret_constants()
    col = lambda j: pl.BlockSpec((tile, BRANCH_W), lambda i, j=j: (i, j))
    row = pl.BlockSpec((tile, BRANCH_W), lambda i: (i, 0))
    return pl.pallas_call(
        functools.partial(_ret_kernel, tile=tile),
        grid=(s // tile,),
        in_specs=[col(8), col(9), col(10), col(11), row, row,
                  _const_spec((BRANCH_W, BRANCH_W)), _const_spec((BRANCH_W, BRANCH_W)),
                  _const_spec((N_HEADS, BRANCH_W)), _const_spec((N_HEADS * c, c)),
                  _const_spec((c, BRANCH_W)), _const_spec((c, BRANCH_W)), _const_spec((1, BRANCH_W)),
                  _const_spec((1, BRANCH_W)), _const_spec((1, BRANCH_W))],
        out_specs=row,
        out_shape=jax.ShapeDtypeStruct((s, BRANCH_W), BF16),
        scratch_shapes=[pltpu.VMEM((BRANCH_W, BRANCH_W), F32)],
        compiler_params=_params("arbitrary"),
        name="retention",
    )(proj, proj, proj, proj, cos_t, sin_t, perm, _head_ones(), _head_masks(), dstack, xi, zeta, gc,
      gn_g.reshape(1, BRANCH_W).astype(F32), gn_b.reshape(1, BRANCH_W).astype(F32))


def _merge_kernel(h_ref, o0_ref, o1_ref, o2_ref, o3_ref, x_ref, wg_ref, wb_ref, wo_ref, g_ref, *rest):
    n_cast = (len(rest) - 2) // 2
    cast_in, (xo_ref, ho_ref), cast_out = rest[:n_cast], rest[n_cast:n_cast + 2], rest[n_cast + 2:]
    for src, dst in zip(cast_in, cast_out):
        dst[...] = src[...].astype(dst.dtype)
    h = h_ref[...]
    merged = jnp.zeros(x_ref.shape, F32)
    for bi, o_ref in enumerate((o0_ref, o1_ref, o2_ref, o3_ref)):
        gate = _sigmoid(_dot(h, wg_ref[:, bi * D_MODEL:(bi + 1) * D_MODEL]))
        merged = merged + gate * _dot(o_ref[...], wb_ref[bi])
    x_new = x_ref[...] + _dot(merged.astype(BF16), wo_ref[...])
    xo_ref[...] = x_new
    ho_ref[...] = _rms(x_new, g_ref[...]).astype(ho_ref.dtype)


def merge(h, branches, x, w_gate, w_branch, w_out, next_gain, tm=512, cast=()):
    s, d = x.shape
    tm = min(tm, s)
    steps = s // tm
    row = lambda w: pl.BlockSpec((tm, w), lambda i: (i, 0))
    flat = [c.reshape(-1, c.shape[-1]) for c in cast]
    slab = lambda c: pl.BlockSpec((c.shape[0] // steps, c.shape[1]), lambda i: (i, 0))
    outs = pl.pallas_call(
        _merge_kernel,
        grid=(steps,),
        in_specs=[row(d), row(BRANCH_W), row(BRANCH_W), row(BRANCH_W), row(BRANCH_W), row(d),
                  _const_spec((d, N_BRANCH * d)), _const_spec((N_BRANCH, BRANCH_W, d)),
                  _const_spec((d, d)), _const_spec((1, d))] + [slab(c) for c in flat],
        out_specs=[row(d), row(d)] + [slab(c) for c in flat],
        out_shape=[jax.ShapeDtypeStruct((s, d), F32), jax.ShapeDtypeStruct((s, d), BF16)]
        + [jax.ShapeDtypeStruct(c.shape, BF16) for c in flat],
        compiler_params=_params("parallel"),
        name="merge",
    )(h, *branches, x, w_gate, w_branch, w_out, next_gain.reshape(1, d).astype(F32), *flat)
    return outs[0], outs[1], [o.reshape(c.shape) for o, c in zip(outs[2:], cast)]


def _ffn_kernel(h_ref, x_ref, wg_ref, wu_ref, wd_ref, g_ref, xo_ref, ho_ref, acc_ref):
    f = pl.program_id(1)

    @pl.when(f == 0)
    def _():
        acc_ref[...] = jnp.zeros_like(acc_ref)

    h = h_ref[...]
    a = _silu(_dot(h, wg_ref[...].astype(BF16))) * _dot(h, wu_ref[...].astype(BF16))
    acc_ref[...] += _dot(a.astype(BF16), wd_ref[...].astype(BF16))

    @pl.when(f == pl.num_programs(1) - 1)
    def _():
        x_new = x_ref[...] + acc_ref[...]
        xo_ref[...] = x_new
        ho_ref[...] = _rms(x_new, g_ref[...]).astype(ho_ref.dtype)


def ffn_dense(h, x, w_gate, w_up, w_down, next_gain, tm=1024, tf=512):
    s, d = x.shape
    tm = min(tm, s)
    dff = w_gate.shape[1]
    row = lambda: pl.BlockSpec((tm, d), lambda i, f: (i, 0))
    return pl.pallas_call(
        _ffn_kernel,
        grid=(s // tm, dff // tf),
        in_specs=[row(), row(),
                  pl.BlockSpec((d, tf), lambda i, f: (0, f)),
                  pl.BlockSpec((d, tf), lambda i, f: (0, f)),
                  pl.BlockSpec((tf, d), lambda i, f: (f, 0)),
                  _const_spec((1, d))],
        out_specs=[row(), row()],
        out_shape=[jax.ShapeDtypeStruct((s, d), F32), jax.ShapeDtypeStruct((s, d), BF16)],
        scratch_shapes=[pltpu.VMEM((tm, d), F32)],
        compiler_params=_params("parallel", "arbitrary"),
        name="ffn_dense",
    )(h, x, w_gate, w_up, w_down, next_gain.reshape(1, d).astype(F32))


MERGE_CAST_TILE = 256
MOE_TOK_TILE = 256
MOE_ROUTE_TILES = 4
MOE_ROW_BLOCK = 512
MOE_GATHER_ROWS = 128
MOE_GATHER_TILES = 2
MOE_VMEM_LIMIT = 60 * 1024 * 1024
MOE_Y_BLOCK = 128
MOE_Y_FETCH = MOE_TOK_TILE // MOE_Y_BLOCK + 1


def _router_logits(x, w):
    xh, xm, _ = _split3(x)
    wh, wm, _ = _split3(w)
    packed = (wh.astype(F32) + pltpu.roll(wm.astype(F32), N_EXPERTS, axis=1)).astype(BF16)
    s = _dot(xh, packed) + _dot(xm, packed)
    return s + pltpu.roll(s, LANE - N_EXPERTS, axis=1)


def _route_kernel(x_ref, ng_ref, router_ref, ltri_ref, pos_ref, gate_ref, post_ref, before_ref, total_ref,
                  carry_ref):
    tm = x_ref.shape[0]

    @pl.when(pl.program_id(0) == 0)
    def _():
        carry_ref[...] = jnp.zeros_like(carry_ref)

    hn = _rms(x_ref[...], ng_ref[...])
    logits = _router_logits(hn, router_ref[...])
    lane = lax.broadcasted_iota(jnp.int32, (tm, LANE), 1)
    logits = jnp.where(lane < N_EXPERTS, logits, -jnp.inf)
    v1 = jnp.max(logits, axis=1, keepdims=True)
    i1 = jnp.min(jnp.where(logits == v1, lane, LANE), axis=1, keepdims=True)
    rest = jnp.where(lane == i1, -jnp.inf, logits)
    v2 = jnp.max(rest, axis=1, keepdims=True)
    i2 = jnp.min(jnp.where(rest == v2, lane, LANE), axis=1, keepdims=True)
    w1 = 1.0 / (1.0 + jnp.exp(v2 - v1))
    gate_ref[...] = jnp.where(lane == i1, w1, 0.0) + jnp.where(lane == i2, 1.0 - w1, 0.0)

    member = jnp.where((lane == i1) | (lane == i2), 1.0, 0.0)
    carry = carry_ref[...]
    ranks = []
    for blk in range(tm // MOE_TOK_TILE):
        before_ref[blk] = carry
        mb = member[blk * MOE_TOK_TILE:(blk + 1) * MOE_TOK_TILE]
        ranks.append(_dot(ltri_ref[...], mb.astype(BF16)) + carry)
        carry = carry + jnp.sum(mb, axis=0, keepdims=True)
    pos = jnp.where(member > 0.0, jnp.concatenate(ranks, axis=0), -1.0)
    pos_ref[...] = pos
    post_ref[...] = pos.T[:SUBLANE, :]
    carry_ref[...] = carry
    total_ref[...] = carry


def moe_route(x, norm_gain, router):
    s, d = x.shape
    tm = min(MOE_ROUTE_TILES * MOE_TOK_TILE, s)
    nsub = tm // MOE_TOK_TILE
    nt = s // tm
    router_p = jnp.zeros((d, LANE), F32).at[:, :N_EXPERTS].set(router.astype(F32))
    tt = min(MOE_TOK_TILE, s)
    ltri = _const(np.tril(np.ones((tt, tt)), -1), BF16)
    row = pl.BlockSpec((tm, LANE), lambda i: (i, 0))
    return pl.pallas_call(
        _route_kernel,
        grid=(nt,),
        in_specs=[pl.BlockSpec((tm, d), lambda i: (i, 0)), _const_spec((1, d)), _const_spec((d, LANE)),
                  _const_spec((tt, tt))],
        out_specs=[row, row, pl.BlockSpec((SUBLANE, tm), lambda i: (0, i)),
                   pl.BlockSpec((nsub, 1, LANE), lambda i: (i, 0, 0)), pl.BlockSpec((1, LANE), lambda i: (0, 0))],
        out_shape=[jax.ShapeDtypeStruct((s, LANE), F32), jax.ShapeDtypeStruct((s, LANE), F32),
                   jax.ShapeDtypeStruct((SUBLANE, s), F32), jax.ShapeDtypeStruct((nt * nsub, 1, LANE), F32),
                   jax.ShapeDtypeStruct((1, LANE), F32)],
        scratch_shapes=[pltpu.VMEM((1, LANE), F32)],
        compiler_params=_params("arbitrary"),
        name="moe_route",
    )(x, norm_gain.reshape(1, d).astype(F32), router_p, ltri)


def _moe_ffn_kernel(be_ref, r0_ref, tlo_ref, thi_ref, nv_ref, h_ref, post_ref, wg_ref, wu_ref, wd_ref,
                    y_ref, x_sc, acc_sc, *, tt, nsub):
    b = pl.program_id(0)
    f = pl.program_id(1)
    nb = pl.num_programs(0)
    valid = b < nv_ref[0]
    tmb, d = acc_sc.shape
    gr = tmb // nsub
    last_tile = h_ref.shape[0] // tt - 1

    def picked(e, want, t):
        t0 = pl.multiple_of(t * tt, tt)
        p = post_ref[pl.ds(e, 1), pl.ds(t0, tt)]
        sel = jnp.where(p == want, 1.0, 0.0).astype(BF16)
        return _dot(sel, h_ref[pl.ds(t0, tt), :])

    def gather_head(blk, sb):
        e = be_ref[blk]
        lo = tlo_ref[blk * nsub + sb]
        hi = thi_ref[blk * nsub + sb]
        want = (lax.broadcasted_iota(jnp.int32, (gr, 1), 0) + (r0_ref[blk] + sb * gr)).astype(F32)
        rows = picked(e, want, lo)
        for k in range(1, MOE_GATHER_TILES):
            rows = rows + picked(e, jnp.where(lo + k <= hi, want, -2.0), jnp.minimum(lo + k, last_tile))
        return rows.astype(BF16)

    def gather_tail(slot, blk, sb):
        e = be_ref[blk]
        want = (lax.broadcasted_iota(jnp.int32, (gr, 1), 0) + (r0_ref[blk] + sb * gr)).astype(F32)
        rows = pl.ds(pl.multiple_of(sb * gr, gr), gr)

        def more(t, carry):
            x_sc[slot, rows, :] = (x_sc[slot, rows, :].astype(F32) + picked(e, want, t)).astype(BF16)
            return carry

        lax.fori_loop(tlo_ref[blk * nsub + sb] + MOE_GATHER_TILES, thi_ref[blk * nsub + sb] + 1, more, 0)

    @pl.when((b == 0) & (f == 0))
    def _():
        acc_sc[...] = jnp.zeros_like(acc_sc)
        for sb in range(nsub):
            x_sc[0, pl.ds(sb * gr, gr), :] = gather_head(0, sb)
            gather_tail(0, 0, sb)

    nxt = jnp.minimum(b + 1, nb - 1)
    nslot = (b + 1) % 2

    @pl.when(valid)
    def _():
        x_sc[nslot, pl.ds(pl.multiple_of(f * gr, gr), gr), :] = gather_head(nxt, f)
        xb = x_sc[b % 2]
        a = _silu(_dot(xb, wg_ref[0])) * _dot(xb, wu_ref[0])
        acc_sc[...] = jnp.where(f == 0, 0.0, acc_sc[...]) + _dot(a.astype(BF16), wd_ref[0].astype(BF16))

    @pl.when(valid & (thi_ref[nxt * nsub + f] - tlo_ref[nxt * nsub + f] >= MOE_GATHER_TILES))
    def _():
        gather_tail(nslot, nxt, f)

    @pl.when(f == nsub - 1)
    def _():
        y_ref[...] = jnp.where(valid, acc_sc[...], 0.0).astype(y_ref.dtype)


def moe_ffn(h, post, sched, w_gate, w_up, w_down):
    s, d = h.shape
    ne, _, dff = w_gate.shape
    tmb = MOE_ROW_BLOCK
    tt = min(MOE_TOK_TILE, s)
    nb = sched[0].shape[0]
    nf = tmb // MOE_GATHER_ROWS
    tf = dff // nf

    def fidx(b, f, nv):
        return jnp.where(b < nv[0], f, nf - 1)

    grid_spec = pltpu.PrefetchScalarGridSpec(
        num_scalar_prefetch=5,
        grid=(nb, nf),
        in_specs=[pl.BlockSpec((s, d), lambda b, f, *_: (0, 0), pipeline_mode=pl.Buffered(1)),
                  pl.BlockSpec((SUBLANE, s), lambda b, f, *_: (0, 0), pipeline_mode=pl.Buffered(1)),
                  pl.BlockSpec((1, d, tf), lambda b, f, be, r0, tlo, thi, nv: (be[b], 0, fidx(b, f, nv))),
                  pl.BlockSpec((1, d, tf), lambda b, f, be, r0, tlo, thi, nv: (be[b], 0, fidx(b, f, nv))),
                  pl.BlockSpec((1, tf, d), lambda b, f, be, r0, tlo, thi, nv: (be[b], fidx(b, f, nv), 0))],
        out_specs=pl.BlockSpec((tmb, d), lambda b, f, *_: (b, 0)),
        scratch_shapes=[pltpu.VMEM((2, tmb, d), BF16), pltpu.VMEM((tmb, d), F32)],
    )
    return pl.pallas_call(
        functools.partial(_moe_ffn_kernel, tt=tt, nsub=nf),
        grid_spec=grid_spec,
        out_shape=jax.ShapeDtypeStruct((nb * tmb, d), BF16),
        compiler_params=pltpu.CompilerParams(dimension_semantics=("arbitrary", "arbitrary"),
                                             vmem_limit_bytes=MOE_VMEM_LIMIT),
        name="moe_ffn",
    )(*sched, h, post, w_gate, w_up, w_down)


def _moe_combine_kernel(kb_ref, off_ref, lim_ref, x_ref, pos_ref, gate_ref, fg_ref, *rest):
    y_refs, o_ref, acc_sc = rest[:-2], rest[-2], rest[-1]
    t = pl.program_id(0)
    yb = y_refs[0].shape[0]
    pos = pos_ref[...]
    gate = gate_ref[...]
    col = lax.broadcasted_iota(jnp.int32, (1, yb), 1).astype(F32)

    def routed(e):
        pe = pos[:, e:e + 1]
        ge = gate[:, e:e + 1]
        r = jnp.where(pe >= 0.0, pe + off_ref[t * N_EXPERTS + e].astype(F32), -1.0)
        return r, ge

    def picked(e, k, r):
        sel = jnp.where(r == col + float(k * yb), 1.0, 0.0).astype(BF16)
        return _dot(sel, y_refs[MOE_Y_FETCH * e + k][...])

    col2 = lax.broadcasted_iota(jnp.int32, (1, 2 * yb), 1).astype(F32)
    acc = x_ref[...]
    for e in range(N_EXPERTS):
        r, ge = routed(e)
        pair = jnp.concatenate([y_refs[MOE_Y_FETCH * e][...], y_refs[MOE_Y_FETCH * e + 1][...]], axis=0)
        acc = acc + ge * _dot(jnp.where(r == col2, 1.0, 0.0).astype(BF16), pair)
    acc_sc[...] = acc
    for e in range(N_EXPERTS):
        for k in range(2, MOE_Y_FETCH):
            @pl.when(lim_ref[t * N_EXPERTS + e] > k * yb)
            def _(k=k, e=e):
                r, ge = routed(e)
                acc_sc[...] += ge * picked(e, k, r)
    o_ref[...] = _rms(acc_sc[...], fg_ref[...]).astype(o_ref.dtype)


def moe_combine(x, pos, gate, y, kb, off, lim, final_gain):
    s, d = x.shape
    tm = min(MOE_TOK_TILE, s)
    yb = MOE_Y_BLOCK
    last = y.shape[0] // yb - 1

    def yspec(e, k):
        def index(t, kb_r, off_r, lim_r):
            blk = jnp.minimum(kb_r[t * N_EXPERTS + e] + k, last)
            return (blk if k < 1 else jnp.where(lim_r[t * N_EXPERTS + e] > k * yb, blk, 0), 0)
        return pl.BlockSpec((yb, d), index)

    grid_spec = pltpu.PrefetchScalarGridSpec(
        num_scalar_prefetch=3,
        grid=(s // tm,),
        in_specs=[pl.BlockSpec((tm, d), lambda t, *_: (t, 0)),
                  pl.BlockSpec((tm, LANE), lambda t, *_: (t, 0)),
                  pl.BlockSpec((tm, LANE), lambda t, *_: (t, 0)),
                  pl.BlockSpec((1, d), lambda t, *_: (0, 0))]
        + [yspec(e, k) for e in range(N_EXPERTS) for k in range(MOE_Y_FETCH)],
        out_specs=pl.BlockSpec((tm, d), lambda t, *_: (t, 0)),
        scratch_shapes=[pltpu.VMEM((tm, d), F32)],
    )
    return pl.pallas_call(
        _moe_combine_kernel,
        grid_spec=grid_spec,
        out_shape=jax.ShapeDtypeStruct((s, d), F32),
        compiler_params=_params("arbitrary"),
        name="moe_combine",
    )(kb, off, lim, x, pos, gate, final_gain.reshape(1, d).astype(F32), *([y] * (MOE_Y_FETCH * N_EXPERTS)))


def _moe_schedule(before, total, s):
    tmb, yb, gr = MOE_ROW_BLOCK, MOE_Y_BLOCK, MOE_GATHER_ROWS
    nb = 2 * s // tmb + N_EXPERTS
    counts = total[0, :N_EXPERTS].astype(jnp.int32)
    nblk = (counts + tmb - 1) // tmb
    end = jnp.cumsum(nblk)
    first = end - nblk
    nvalid = end[-1]
    b = jnp.minimum(jnp.arange(nb, dtype=jnp.int32), nvalid - 1)
    blk_e = jnp.sum(b[:, None] >= end[None, :], axis=1).astype(jnp.int32)
    r0 = (b - first[blk_e]) * tmb
    cb = before[:, 0, :N_EXPERTS].astype(jnp.int32)
    r0s = (r0[:, None] + gr * jnp.arange(tmb // gr, dtype=jnp.int32)[None, :]).reshape(-1)
    cbe = cb[:, jnp.repeat(blk_e, tmb // gr)]
    tlo = (jnp.sum(cbe <= r0s[None, :], axis=0) - 1).astype(jnp.int32)
    thi = (jnp.sum(cbe < (r0s + gr)[None, :], axis=0) - 1).astype(jnp.int32)
    sched = (blk_e, r0.astype(jnp.int32), tlo, thi, nvalid.reshape(1).astype(jnp.int32))
    row_start = first[None, :] * tmb + cb
    kb = row_start // yb
    off = first[None, :] * tmb - kb * yb
    n_te = jnp.concatenate([cb[1:], counts[None, :]], axis=0) - cb
    lim = row_start - kb * yb + n_te
    flat = lambda a: a.reshape(-1).astype(jnp.int32)
    return sched, flat(kb), flat(off), flat(lim)


def moe_sparse(h, x, norm_gain, router, w_gate, w_up, w_down, final_gain):
    s, _ = x.shape
    pos, gate, post, before, total = moe_route(x, norm_gain, router)
    sched, kb, off, lim = _moe_schedule(before, total, s)
    y = moe_ffn(h, post, sched, w_gate, w_up, w_down)
    return moe_combine(x, pos, gate, y, kb, off, lim, final_gain)


FOX_F0 = 7 * BRANCH_W


def _split_w_in_kernel(wt_ref, mix_ref, f_ref, gate_ref):
    rest0 = FOX_F0 + N_HEADS
    gate0 = N_MIX_COLS + N_HEADS
    mix_ref[:, :FOX_F0] = wt_ref[pl.ds(0, FOX_F0), :].T.astype(BF16)
    mix_ref[:, FOX_F0:] = wt_ref[pl.ds(rest0, N_MIX_COLS - FOX_F0), :].T.astype(BF16)
    gate_ref[...] = wt_ref[pl.ds(gate0, N_BRANCH * D_MODEL), :].T.astype(BF16)
    f_rows = jnp.concatenate([wt_ref[pl.ds(FOX_F0, N_HEADS), :], jnp.zeros((LANE - N_HEADS, LANE), F32)], axis=0)
    f_ref[...] = f_rows.T.astype(BF16)


def _mixer_weights(w_in, layer, tr=LANE):
    _, d, cols = w_in.shape
    w_t = jnp.swapaxes(w_in, 1, 2)
    row = lambda w: pl.BlockSpec((tr, w), lambda i: (i, 0))
    return pl.pallas_call(
        _split_w_in_kernel,
        grid=(d // tr,),
        in_specs=[pl.BlockSpec((None, cols, tr), lambda i: (layer, 0, i))],
        out_specs=[row(N_MIX_COLS), row(LANE), row(N_BRANCH * D_MODEL)],
        out_shape=[jax.ShapeDtypeStruct((d, N_MIX_COLS), BF16), jax.ShapeDtypeStruct((d, LANE), BF16),
                   jax.ShapeDtypeStruct((d, N_BRANCH * D_MODEL), BF16)],
        compiler_params=_params("parallel"),
        name="split_w_in",
    )(w_t)


def kernel(x, w_in, w_branch, w_out, norm_mix_g, hgrn_lb_logits, hgrn_norm_g, fox_f_bias, pool_w, pool_scale,
           ret_gn_g, ret_gn_b, norm_ffn_g, ffn_w_gate, ffn_w_up, ffn_w_down, moe_router, moe_w_gate, moe_w_up,
           moe_w_down, final_norm_g):
    b, s, d = x.shape
    assert b == 1 and d == D_MODEL
    depth = w_in.shape[0]
    assert depth == 2, "layer 0 uses the dense FFN, layer 1 the experts and the final norm"
    xs = x.reshape(s, d)
    h = None
    out = None
    for layer in range(depth):
        w_mix, w_f, w_gate = _mixer_weights(w_in, layer)
        if layer == 0:
            proj, h = norm_matmul(xs, norm_mix_g[0], w_mix)
        else:
            proj = matmul(h, w_mix)
        ct, c_rows, fox_stats = fox_gate(h, proj, w_f, fox_f_bias[layer])
        branches = (
            hgrn2(proj, hgrn_lb_logits, hgrn_norm_g[layer], layer),
            fox_attention(proj, ct, c_rows, fox_stats),
            pool_mixer(proj, pool_w[layer], pool_scale[layer]),
            retention(proj, ret_gn_g[layer], ret_gn_b[layer]),
        )
        li = layer // 2
        if layer % 2 == 0:
            xs, h2, (ffn_wg, ffn_wu) = merge(h, branches, xs, w_gate, w_branch[layer].astype(BF16),
                                             w_out[layer].astype(BF16), norm_ffn_g[layer],
                                             cast=(ffn_w_gate[li], ffn_w_up[li]))
            xs, h = ffn_dense(h2, xs, ffn_wg, ffn_wu, ffn_w_down[li], norm_mix_g[layer + 1])
        else:
            xs, h2, (moe_wg, moe_wu) = merge(h, branches, xs, w_gate, w_branch[layer].astype(BF16),
                                             w_out[layer].astype(BF16), norm_ffn_g[layer], tm=MERGE_CAST_TILE,
                                             cast=(moe_w_gate[li], moe_w_up[li]))
            out = moe_sparse(h2, xs, norm_ffn_g[layer], moe_router[li], moe_wg, moe_wu, moe_w_down[li],
                             final_norm_g)
    return out.reshape(b, s, d)
```

```python
import functools
import math

import jax
import jax.numpy as jnp
import numpy as np
from jax import lax
from jax.experimental import pallas as pl
from jax.experimental.pallas import tpu as pltpu

D_MODEL = 1024
N_BRANCH = 4
BRANCH_W = D_MODEL // N_BRANCH
HEAD_DIM = 64
N_HEADS = BRANCH_W // HEAD_DIM
POOL_WINDOWS = (2, 4, 8, 16)
POOL_GROUP = BRANCH_W // len(POOL_WINDOWS)
POOL_HALO = 16
RET_DECAY_BASE = 5.0
ROPE_BASE = 10000.0
N_EXPERTS = 8
RMS_EPS = 1e-6
LN_EPS = 1e-5
N_MIX_COLS = 12 * BRANCH_W

LANE = 128
SUBLANE = 8
VMEM_LIMIT = 56 * 1024 * 1024

HG_CHUNK = 64
HG_SUB = 16
HG_FAST_MIN_LOGDECAY = -60.0
RET_CHUNK = 256

F32 = jnp.float32
BF16 = jnp.bfloat16
NT_DIMS = (((1,), (1,)), ((), ()))


def _params(*sem):
    return pltpu.CompilerParams(dimension_semantics=sem, vmem_limit_bytes=VMEM_LIMIT)


def _const_spec(shape):
    nd = len(shape)
    return pl.BlockSpec(shape, lambda *_: (0,) * nd, pipeline_mode=pl.Buffered(1))


def _split3(x):
    hi = x.astype(BF16)
    r1 = x - hi.astype(F32)
    mid = r1.astype(BF16)
    lo = (r1 - mid.astype(F32)).astype(BF16)
    return hi, mid, lo


def _dot(a, b):
    return jnp.dot(a, b, preferred_element_type=F32)


def _dot_nt(a, b):
    return lax.dot_general(a, b, NT_DIMS, preferred_element_type=F32)


def _dot_exact_rhs(x, m_bf16, terms=3):
    return sum(_dot(part, m_bf16) for part in _split3(x)[:terms])


def _dot_exact_lhs(m_bf16, x, terms=3):
    return sum(_dot(m_bf16, part) for part in _split3(x)[:terms])


def _sigmoid(x):
    return 1.0 / (1.0 + jnp.exp(-x))


def _silu(x):
    return x * _sigmoid(x)


def _rms(x, gain):
    return x * lax.rsqrt(jnp.mean(x * x, axis=-1, keepdims=True) + RMS_EPS) * gain


def _const(a, dtype=F32):
    return jnp.asarray(np.asarray(a, np.float32), dtype)


def _head_of(n):
    return np.arange(n) // HEAD_DIM


def _head_ones():
    h = _head_of(BRANCH_W)
    return _const(h[:, None] == h[None, :], BF16)


def _head_masks():
    return _const(_head_of(BRANCH_W)[None, :] == np.arange(N_HEADS)[:, None])


def _norm_matmul_kernel(x_ref, g_ref, b_ref, o_ref, h_ref):
    @pl.when(pl.program_id(1) == 0)
    def _():
        h_ref[...] = _rms(x_ref[...], g_ref[...]).astype(h_ref.dtype)

    o_ref[...] = _dot(h_ref[...], b_ref[...]).astype(o_ref.dtype)


def norm_matmul(x, gain, b, tm=2048, tn=1024):
    m, k = x.shape
    _, n = b.shape
    tm = min(tm, m)
    return pl.pallas_call(
        _norm_matmul_kernel,
        grid=(m // tm, n // tn),
        in_specs=[pl.BlockSpec((tm, k), lambda i, j: (i, 0)), _const_spec((1, k)),
                  pl.BlockSpec((k, tn), lambda i, j: (0, j))],
        out_specs=[pl.BlockSpec((tm, tn), lambda i, j: (i, j)), pl.BlockSpec((tm, k), lambda i, j: (i, 0))],
        out_shape=[jax.ShapeDtypeStruct((m, n), BF16), jax.ShapeDtypeStruct((m, k), BF16)],
        compiler_params=_params("arbitrary", "arbitrary"),
        name="in_proj_norm",
    )(x, gain.reshape(1, k).astype(F32), b)


def _matmul_kernel(a_ref, b_ref, o_ref):
    o_ref[...] = _dot(a_ref[...], b_ref[...]).astype(o_ref.dtype)


def matmul(a, b, out_dtype=BF16, tm=2048, tn=1024):
    m, k = a.shape
    _, n = b.shape
    tm = min(tm, m)
    return pl.pallas_call(
        _matmul_kernel,
        grid=(n // tn, m // tm),
        in_specs=[pl.BlockSpec((tm, k), lambda j, i: (i, 0)),
                  pl.BlockSpec((k, tn), lambda j, i: (0, j))],
        out_specs=pl.BlockSpec((tm, tn), lambda j, i: (i, j)),
        out_shape=jax.ShapeDtypeStruct((m, n), out_dtype),
        compiler_params=_params("parallel", "parallel"),
        name="in_proj",
    )(a, b)


def _hgrn_kernel(q_ref, f_ref, i_ref, g_ref, lbl_ref, ng_ref, ones_ref, hm_ref, tril_ref, halfsum_ref, fmask_ref,
                 o_ref, st_ref, bpad, kpad, vpad, astack, lf_sc, kk_sc, o_sc, *, layer, tile):
    c, sub = HG_CHUNK, HG_SUB
    nsub = c // sub
    half = c // 2

    @pl.when(pl.program_id(0) == 0)
    def _():
        st_ref[...] = jnp.zeros_like(st_ref)
        bpad[...] = jnp.zeros_like(bpad)
        kpad[...] = jnp.zeros_like(kpad)
        vpad[...] = jnp.zeros_like(vpad)

    lbl = lbl_ref[...]
    e = jnp.exp(lbl - jnp.max(lbl, axis=0, keepdims=True))
    p = e / jnp.sum(e, axis=0, keepdims=True)
    lb = jnp.zeros((1, BRANCH_W), F32)
    for l in range(1, layer + 1):
        lb = lb + p[l:l + 1, :]

    ones_bd = ones_ref[...]
    hm = hm_ref[...]
    tril = tril_ref[...]
    row = lax.broadcasted_iota(jnp.int32, (c, 1), 0)
    row_in_sub = row % sub
    bd_mask = ones_bd.astype(F32)

    sig = _sigmoid(f_ref[...].astype(F32))
    logf_all = jnp.log(lb + (1.0 - lb) * sig)
    lf_sc[...] = _dot_exact_lhs(tril, logf_all, terms=2)
    kk_sc[...] = (1.0 - lb) * (1.0 - sig)
    min_decay = jnp.min(_dot(halfsum_ref[...], logf_all.astype(BF16)))

    def load(ci):
        r0 = pl.multiple_of(ci * c, c)
        q = q_ref[pl.ds(r0, c), :].astype(F32)
        v = i_ref[pl.ds(r0, c), :].astype(F32)
        kk = kk_sc[pl.ds(r0, c), :]
        b = lf_sc[pl.ds(r0, c), :]
        return r0, q, v, kk, b

    def finish(r0, q_decayed, v, kk, b, intra):
        st = st_ref[...]
        inter = _dot_nt(q_decayed.astype(BF16), st.astype(BF16))
        b_last = b[c - 1:c, :]
        ks_end = (kk * jnp.exp(b_last - b)).astype(BF16)
        upd = _dot(v.T.astype(BF16), ks_end)
        st_ref[...] = st * jnp.exp(b_last) + upd * bd_mask
        o_sc[pl.ds(r0, c), :] = intra + inter

    def fast_chunk(ci, carry):
        r0, q, v, kk, b = load(ci)
        second = row >= half
        m_row = b[half - 1:half, :]
        mref = jnp.where(second, m_row, 0.0)
        qp = q * jnp.exp(b - mref)
        kp = kk * jnp.exp(mref - b)
        e_m = jnp.exp(m_row)
        kaug = jnp.concatenate([kp, kp[:half, :] * e_m], axis=0)
        vaug = jnp.concatenate([v, v[:half, :]], axis=0)
        k_heads = jnp.concatenate([kaug * hm[h:h + 1, :] for h in range(N_HEADS)], axis=0).astype(BF16)
        v_heads = jnp.concatenate([vaug * hm[h:h + 1, :] for h in range(N_HEADS)], axis=0).astype(BF16)
        sc = jnp.where(fmask_ref[...] > 0.0, _dot_nt(qp.astype(BF16), k_heads), 0.0)
        intra = _dot(sc.astype(BF16), v_heads)
        finish(r0, jnp.where(second, qp * e_m, qp), v, kk, b, intra)
        return carry

    def exact_chunk(ci, carry):
        r0, q, v, kk, b = load(ci)

        bpad[pl.ds(sub, c), :] = b
        kpad[pl.ds(sub, c), :] = kk
        vpad[pl.ds(sub, c), :] = v

        for d in range(sub):
            b_d = bpad[pl.ds(sub - d, c), :]
            k_d = kpad[pl.ds(sub - d, c), :]
            a = jnp.where(row_in_sub >= d, q * k_d * jnp.exp(b - b_d), 0.0)
            astack[pl.ds(d * c, c), :] = a.astype(BF16)
        pall = _dot(astack[...], ones_bd)
        intra = jnp.zeros((c, BRANCH_W), F32)
        for d in range(sub):
            intra = intra + pall[d * c:(d + 1) * c, :] * vpad[pl.ds(sub - d, c), :]

        pieces = [jnp.zeros((sub, BRANCH_W), F32)]
        for si in range(1, nsub):
            lo = si * sub
            m_i = b[lo - 1:lo, :]
            qs = q[lo:lo + sub, :] * jnp.exp(b[lo:lo + sub, :] - m_i)
            ks = (kk[:lo, :] * jnp.exp(m_i - b[:lo, :])).astype(BF16)
            qx = jnp.concatenate([qs * hm[h:h + 1, :] for h in range(N_HEADS)], axis=0).astype(BF16)
            sc = _dot_nt(qx, ks)
            r = _dot(sc.astype(BF16), v[:lo, :].astype(BF16))
            acc = jnp.zeros((sub, BRANCH_W), F32)
            for h in range(N_HEADS):
                acc = acc + r[h * sub:(h + 1) * sub, :] * hm[h:h + 1, :]
            pieces.append(acc)
        intra = intra + jnp.concatenate(pieces, axis=0)
        finish(r0, q * jnp.exp(b), v, kk, b, intra)
        return carry

    lax.cond(min_decay >= HG_FAST_MIN_LOGDECAY,
             lambda: lax.fori_loop(0, tile // c, fast_chunk, 0, unroll=2),
             lambda: lax.fori_loop(0, tile // c, exact_chunk, 0))

    o = o_sc[...]
    ms = _dot_exact_rhs(o * o, ones_bd, terms=2) * (1.0 / HEAD_DIM)
    y = o * lax.rsqrt(ms + RMS_EPS) * ng_ref[...] * _silu(g_ref[...].astype(F32))
    o_ref[...] = y.astype(o_ref.dtype)


def hgrn2(proj, lb_logits, norm_g, layer, tile=512):
    s = proj.shape[0]
    depth = lb_logits.shape[0]
    c, sub = HG_CHUNK, HG_SUB
    half = c // 2
    tile = min(tile, s)
    col = lambda j: pl.BlockSpec((tile, BRANCH_W), lambda i, j=j: (i, j))
    pos = np.arange(tile)
    tril = _const((pos[:, None] // c == pos[None, :] // c) & (pos[None, :] <= pos[:, None]), BF16)
    nhalf = tile // half
    halfsum = _const(np.arange(tile)[None, :] // half == np.arange(nhalf)[:, None], BF16)
    t = np.arange(c)[:, None]
    col_s = np.arange(c + half)[None, :]
    same_half = (col_s < c) & (col_s // half == t // half) & (col_s <= t)
    cross = (col_s >= c) & (t >= half)
    fmask = _const(np.tile(same_half | cross, (1, N_HEADS)))
    return pl.pallas_call(
        functools.partial(_hgrn_kernel, layer=layer, tile=tile),
        grid=(s // tile,),
        in_specs=[col(0), col(1), col(2), col(3),
                  _const_spec((depth, BRANCH_W)), _const_spec((1, BRANCH_W)),
                  _const_spec((BRANCH_W, BRANCH_W)), _const_spec((N_HEADS, BRANCH_W)),
                  _const_spec((tile, tile)), _const_spec((nhalf, tile)),
                  _const_spec((c, N_HEADS * (c + half)))],
        out_specs=pl.BlockSpec((tile, BRANCH_W), lambda i: (i, 0)),
        out_shape=jax.ShapeDtypeStruct((s, BRANCH_W), BF16),
        scratch_shapes=[pltpu.VMEM((BRANCH_W, BRANCH_W), F32),
                        pltpu.VMEM((c + sub, BRANCH_W), F32),
                        pltpu.VMEM((c + sub, BRANCH_W), F32),
                        pltpu.VMEM((c + sub, BRANCH_W), F32),
                        pltpu.VMEM((sub * c, BRANCH_W), BF16),
                        pltpu.VMEM((tile, BRANCH_W), F32),
                        pltpu.VMEM((tile, BRANCH_W), F32),
                        pltpu.VMEM((tile, BRANCH_W), F32)],
        compiler_params=_params("arbitrary"),
        name="hgrn2",
    )(proj, proj, proj, proj, lb_logits.astype(F32), norm_g.reshape(1, BRANCH_W).astype(F32),
      _head_ones(), _head_masks(), tril, halfsum, fmask)


FOX_TILE = 256
FOX_GATE_BLOCKS = 2
FOX_NSTAT = 16
FOX_FIXED_MAX = 30.0
FOX_SKIP_LOG = 30.0


def _fox_gate_kernel(h_ref, q_ref, k_ref, wf_ref, bias_ref, tril_ref, ones_ref,
                     ct_ref, c_ref, stat_ref, carry_ref, kmax_ref):
    @pl.when(pl.program_id(0) == 0)
    def _():
        carry_ref[...] = jnp.zeros_like(carry_ref)
        kmax_ref[...] = jnp.zeros_like(kmax_ref)

    logit = _dot(h_ref[...], wf_ref[...]) + bias_ref[...]
    logf = jnp.minimum(logit, 0.0) - jnp.log(1.0 + jnp.exp(-jnp.abs(logit)))
    carry = carry_ref[...]
    parts = []
    for blk in range(logf.shape[0] // FOX_TILE):
        part = _dot_exact_lhs(tril_ref[...], logf[blk * FOX_TILE:(blk + 1) * FOX_TILE]) + carry
        carry = part[-1:, :]
        parts.append(part)
    cum = jnp.concatenate(parts, axis=0)
    carry_ref[...] = carry
    ct_ref[...] = cum.T[:SUBLANE, :]
    c_ref[...] = cum

    ones_bd = ones_ref[...]
    q = q_ref[...].astype(F32)
    k = k_ref[...].astype(F32)
    scale = HEAD_DIM ** -0.5
    head_lane = lax.broadcasted_iota(jnp.int32, (1, BRANCH_W), 1) // HEAD_DIM
    c_heads = jnp.zeros(q.shape, F32)
    for h in range(N_HEADS):
        c_heads = jnp.where(head_lane == h, cum[:, h:h + 1], c_heads)
    slack = 1.0 + 2.0 ** -6
    qn = jnp.sqrt(_dot((q * q).astype(BF16), ones_bd)) * (scale * slack)
    kn = jnp.sqrt(_dot((k * k).astype(BF16), ones_bd)) * slack
    diag = _dot((q * k).astype(BF16), ones_bd) * scale - (2.0 ** -6) * qn * kn
    e_row = c_heads - diag
    kmax = kmax_ref[...]
    for blk in range(q.shape[0] // FOX_TILE):
        rs = slice(blk * FOX_TILE, (blk + 1) * FOX_TILE)
        kmax = jnp.maximum(kmax, jnp.max(kn[rs], axis=0, keepdims=True))
        rows = [jnp.max(qn[rs], axis=0, keepdims=True),
                jnp.max(e_row[rs], axis=0, keepdims=True),
                kmax,
                c_heads[(blk + 1) * FOX_TILE - 1:(blk + 1) * FOX_TILE, :]]
        stat_ref[blk] = jnp.concatenate(rows + [jnp.zeros((SUBLANE - len(rows), BRANCH_W), F32)], axis=0)
    kmax_ref[...] = kmax


def fox_gate(h, proj, w_f, f_bias):
    s, d = h.shape
    tile = min(FOX_GATE_BLOCKS * FOX_TILE, s)
    nblk = tile // FOX_TILE
    bias = jnp.zeros((1, LANE), F32).at[0, :N_HEADS].set(f_bias.astype(F32))
    tril = _const(np.tril(np.ones((FOX_TILE, FOX_TILE))), BF16)
    ct, c_rows, stats = pl.pallas_call(
        _fox_gate_kernel,
        grid=(s // tile,),
        in_specs=[pl.BlockSpec((tile, d), lambda i: (i, 0)),
                  pl.BlockSpec((tile, BRANCH_W), lambda i: (i, 4)),
                  pl.BlockSpec((tile, BRANCH_W), lambda i: (i, 5)),
                  _const_spec((d, LANE)), _const_spec((1, LANE)), _const_spec((FOX_TILE, FOX_TILE)),
                  _const_spec((BRANCH_W, BRANCH_W))],
        out_specs=[pl.BlockSpec((SUBLANE, tile), lambda i: (0, i)),
                   pl.BlockSpec((tile, LANE), lambda i: (i, 0)),
                   pl.BlockSpec((nblk, SUBLANE, BRANCH_W), lambda i: (i, 0, 0))],
        out_shape=[jax.ShapeDtypeStruct((SUBLANE, s), F32),
                   jax.ShapeDtypeStruct((s, LANE), F32),
                   jax.ShapeDtypeStruct((s // FOX_TILE, SUBLANE, BRANCH_W), F32)],
        scratch_shapes=[pltpu.VMEM((1, LANE), F32), pltpu.VMEM((1, BRANCH_W), F32)],
        compiler_params=_params("arbitrary"),
        name="fox_gate",
    )(h, proj, proj, w_f, bias, tril, _head_ones())
    return ct, c_rows, stats[:, :4, ::HEAD_DIM].reshape(-1)


def _fox_kernel(stat_ref, q_ref, k_ref, v_ref, ct_ref, c_ref, hm_ref, o_ref, m_sc, l_sc, acc_sc, *, tq):
    i = pl.program_id(0)
    q0 = pl.multiple_of(i * tq, tq)
    hm = hm_ref[...]
    q = q_ref[...].astype(F32) * (HEAD_DIM ** -0.5)
    qh = [(q * hm[h:h + 1, :]).astype(BF16) for h in range(N_HEADS)]
    c_q0 = ct_ref[:, pl.ds(q0, tq)][:, 0:1]

    first = i
    for h in range(N_HEADS):
        qmax = stat_ref[i * FOX_NSTAT + h]
        emax = stat_ref[i * FOX_NSTAT + N_HEADS + h]

        def needed(j, h=h, qmax=qmax, emax=emax):
            jc = jnp.maximum(j, 0)
            bound = (qmax * stat_ref[jc * FOX_NSTAT + 2 * N_HEADS + h] + emax
                     - stat_ref[jc * FOX_NSTAT + 3 * N_HEADS + h])
            return (j >= 0) & (bound >= -FOX_SKIP_LOG)

        last_dropped = lax.while_loop(needed, lambda j: j - 1, i - 1)
        first = jnp.minimum(first, last_dropped + 1)

    l_sc[...] = jnp.zeros_like(l_sc)
    acc_sc[...] = jnp.zeros_like(acc_sc)

    def causal(sc):
        r = lax.broadcasted_iota(jnp.int32, (tq, tq), 0)
        cidx = lax.broadcasted_iota(jnp.int32, (tq, tq), 1)
        return jnp.where(cidx <= r, sc, -jnp.inf)

    def online_block(s0, diagonal):
        kb = k_ref[pl.ds(s0, tq), :]
        vb = v_ref[pl.ds(s0, tq), :]
        bias = c_q0 - ct_ref[:, pl.ds(s0, tq)]
        for h in range(N_HEADS):
            sc = _dot_nt(qh[h], kb) + bias[h:h + 1, :]
            if diagonal:
                sc = causal(sc)
            m_prev = m_sc[h]
            m_new = jnp.maximum(m_prev, jnp.max(sc, axis=1, keepdims=True))
            alpha = jnp.exp(m_prev - m_new)
            p = jnp.exp(sc - jnp.tile(m_new, (1, tq // LANE)))
            l_sc[h] = alpha * l_sc[h] + jnp.sum(p, axis=1, keepdims=True)
            acc_sc[h] = acc_sc[h] * jnp.tile(alpha, (1, BRANCH_W // LANE)) + _dot(p.astype(BF16), vb)
            m_sc[h] = m_new

    tops = [stat_ref[i * FOX_NSTAT + h] * stat_ref[i * FOX_NSTAT + 2 * N_HEADS + h] for h in range(N_HEADS)]
    c_tile = c_ref[...]
    shift = [c_tile[:, h:h + 1] - c_q0[h:h + 1, :] - tops[h] for h in range(N_HEADS)]

    q_heads = jnp.concatenate(qh, axis=0)

    def fixed_block(s0, diagonal):
        kb = k_ref[pl.ds(s0, tq), :]
        vb = v_ref[pl.ds(s0, tq), :]
        bias = c_q0 - ct_ref[:, pl.ds(s0, tq)]
        sc_all = _dot_nt(q_heads, kb)
        probs = []
        for h in range(N_HEADS):
            sc = sc_all[h * tq:(h + 1) * tq, :] + bias[h:h + 1, :] + shift[h]
            if diagonal:
                sc = causal(sc)
            p = jnp.exp(sc)
            l_sc[h] += p[:, :LANE] + p[:, LANE:]
            probs.append(p.astype(BF16))
        pv = _dot(jnp.concatenate(probs, axis=0), vb)
        for h in range(N_HEADS):
            acc_sc[h] += pv[h * tq:(h + 1) * tq, :]

    def run(block, row_sum):
        lax.fori_loop(first, i, lambda j, carry: (block(pl.multiple_of(j * tq, tq), False), carry)[1], 0)
        block(q0, True)
        out = jnp.zeros((tq, BRANCH_W), F32)
        for h in range(N_HEADS):
            out = out + acc_sc[h] * hm[h:h + 1, :] / row_sum(l_sc[h])
        o_ref[...] = out.astype(o_ref.dtype)

    def run_online():
        m_sc[...] = jnp.full_like(m_sc, -jnp.inf)
        run(online_block, lambda l: jnp.tile(l, (1, BRANCH_W // LANE)))

    def run_fixed():
        run(fixed_block, lambda l: jnp.sum(l, axis=1, keepdims=True))

    lax.cond(functools.reduce(jnp.maximum, tops) <= FOX_FIXED_MAX, run_fixed, run_online)


def fox_attention(proj, ct, c_rows, stats):
    s = proj.shape[0]
    tq = min(FOX_TILE, s)
    full = lambda j: pl.BlockSpec((s, BRANCH_W), lambda i, j=j: (0, j), pipeline_mode=pl.Buffered(1))
    return pl.pallas_call(
        functools.partial(_fox_kernel, tq=tq),
        grid=(s // tq,),
        in_specs=[pl.BlockSpec(memory_space=pltpu.SMEM),
                  pl.BlockSpec((tq, BRANCH_W), lambda i: (i, 4)), full(5), full(6),
                  _const_spec((SUBLANE, s)), pl.BlockSpec((tq, LANE), lambda i: (i, 0)),
                  _const_spec((N_HEADS, BRANCH_W))],
        out_specs=pl.BlockSpec((tq, BRANCH_W), lambda i: (i, 0)),
        out_shape=jax.ShapeDtypeStruct((s, BRANCH_W), BF16),
        scratch_shapes=[pltpu.VMEM((N_HEADS, tq, LANE), F32),
                        pltpu.VMEM((N_HEADS, tq, LANE), F32),
                        pltpu.VMEM((N_HEADS, tq, BRANCH_W), F32)],
        compiler_params=_params("parallel"),
        name="fox_attention",
    )(stats, proj, proj, proj, ct, c_rows, _head_masks())


def _pool_kernel(u_ref, w_ref, scale_ref, o_ref, ext, *, tile):
    i = pl.program_id(0)

    @pl.when(i == 0)
    def _():
        ext[pl.ds(0, POOL_HALO), :] = jnp.zeros((POOL_HALO, BRANCH_W), F32)

    u = u_ref[...].astype(F32)
    ext[pl.ds(POOL_HALO, tile), :] = u
    pos = (i * tile + lax.broadcasted_iota(jnp.int32, (tile, 1), 0) + 1).astype(F32)
    halves = []
    for half in range(BRANCH_W // LANE):
        lanes = pl.ds(half * LANE, LANE)
        w_small, w_big = POOL_WINDOWS[2 * half], POOL_WINDOWS[2 * half + 1]
        run = u[:, half * LANE:(half + 1) * LANE]
        sums = {}
        for j in range(1, w_big):
            if j == w_small:
                sums[w_small] = run
            run = run + ext[pl.ds(POOL_HALO - j, tile), lanes]
        sums[w_big] = run
        lane = lax.broadcasted_iota(jnp.int32, (1, LANE), 1)
        small = lane < POOL_GROUP
        total = jnp.where(small, sums[w_small], sums[w_big])
        count = jnp.where(small, jnp.minimum(pos, float(w_small)), jnp.minimum(pos, float(w_big)))
        halves.append(total / count)
    mean = jnp.concatenate(halves, axis=1)
    d = (mean - u).astype(BF16)
    y = _dot(d, w_ref[...]) * scale_ref[...]
    o_ref[...] = y.astype(o_ref.dtype)
    ext[pl.ds(0, POOL_HALO), :] = u[tile - POOL_HALO:, :]


def _pool_weights(w_pool):
    w_bd = jnp.zeros((BRANCH_W, BRANCH_W), F32)
    for gi in range(len(POOL_WINDOWS)):
        lo = gi * POOL_GROUP
        w_bd = w_bd.at[lo:lo + POOL_GROUP, lo:lo + POOL_GROUP].set(w_pool[gi].astype(F32))
    return w_bd.astype(BF16)


def _ret_kernel(q_ref, k_ref, v_ref, g_ref, cos_ref, sin_ref, perm_ref, ones_ref, hm_ref,
                dstack_ref, xi_ref, zeta_ref, gc_ref, gng_ref, gnb_ref, pool_u_ref, pool_w_ref, pool_s_ref,
                o_ref, pool_o_ref, st_ref, pool_ext, *, tile):
    c = RET_CHUNK

    @pl.when(pl.program_id(0) == 0)
    def _():
        st_ref[...] = jnp.zeros_like(st_ref)

    _pool_kernel(pool_u_ref, pool_w_ref, pool_s_ref, pool_o_ref, pool_ext, tile=tile)

    perm = perm_ref[...]
    ones_bd = ones_ref[...]
    bd_mask = ones_bd.astype(F32)
    hm = hm_ref[...]

    cos = cos_ref[...]
    sin = sin_ref[...]
    q_all = q_ref[...]
    k_all = k_ref[...]
    qr_all = q_all.astype(F32) * cos + _dot(q_all, perm) * sin
    kr_all = (k_all.astype(F32) * cos + _dot(k_all, perm) * sin) * (HEAD_DIM ** -0.5)

    outs = []
    for ci in range(tile // c):
        r0 = ci * c
        qr = qr_all[r0:r0 + c, :]
        kr = kr_all[r0:r0 + c, :]
        v = v_ref[pl.ds(r0, c), :]

        qx = jnp.concatenate([qr * hm[h:h + 1, :] for h in range(N_HEADS)], axis=0).astype(BF16)
        sc = _dot_nt(qx, kr.astype(BF16)) * dstack_ref[...]
        r = _dot(sc.astype(BF16), v)
        intra = jnp.zeros((c, BRANCH_W), F32)
        for h in range(N_HEADS):
            intra = intra + r[h * c:(h + 1) * c, :] * hm[h:h + 1, :]

        st = st_ref[...]
        inter = _dot_nt((qr * xi_ref[...]).astype(BF16), st.astype(BF16))
        upd = _dot(v.astype(F32).T.astype(BF16), (kr * zeta_ref[...]).astype(BF16))
        st_ref[...] = st * gc_ref[...] + upd * bd_mask
        outs.append(intra + inter)

    o = jnp.concatenate(outs, axis=0)
    mu = _dot_exact_rhs(o, ones_bd, terms=2) * (1.0 / HEAD_DIM)
    cen = o - mu
    var = _dot_exact_rhs(cen * cen, ones_bd, terms=2) * (1.0 / HEAD_DIM)
    y = cen * lax.rsqrt(var + LN_EPS) * gng_ref[...] + gnb_ref[...]
    o_ref[...] = (y * _silu(g_ref[...].astype(F32))).astype(o_ref.dtype)


def _rope_tables(s):
    half = HEAD_DIM // 2
    pos = np.arange(s, dtype=np.float64)
    inv_freq = ROPE_BASE ** (-np.arange(half, dtype=np.float64) / half)
    lane = np.arange(BRANCH_W)
    ang = pos[:, None] * inv_freq[lane % half][None, :]
    sign = np.where(lane % HEAD_DIM < half, -1.0, 1.0)
    return _const(np.cos(ang)), _const(np.sin(ang) * sign[None, :])


def _ret_constants():
    c = RET_CHUNK
    half = HEAD_DIM // 2
    lane = np.arange(BRANCH_W)
    partner = np.where(lane % HEAD_DIM < half, lane + half, lane - half)
    perm = lane[:, None] == partner[None, :]
    log_gamma = np.log1p(-np.exp2(-RET_DECAY_BASE - np.arange(N_HEADS, dtype=np.float64)))
    ci = np.arange(c, dtype=np.float64)
    diff = ci[:, None] - ci[None, :]
    intra = np.where(diff >= 0, np.exp(diff * log_gamma[:, None, None]), 0.0)
    dstack = intra.reshape(N_HEADS * c, c)
    lg_lane = np.repeat(log_gamma, HEAD_DIM)[None, :]
    xi = np.exp((ci[:, None] + 1.0) * lg_lane)
    zeta = np.exp((c - 1.0 - ci[:, None]) * lg_lane)
    gc = np.exp(c * lg_lane)
    return _const(perm, BF16), _const(dstack), _const(xi), _const(zeta), _const(gc)


def retention_and_pool(proj, gn_g, gn_b, w_pool, pool_scale, tile=1024):
    s = proj.shape[0]
    c = RET_CHUNK
    tile = min(tile, s)
    cos_t, sin_t = _rope_tables(s)
    perm, dstack, xi, zeta, gc = _ret_constants()
    col = lambda j: pl.BlockSpec((tile, BRANCH_W), lambda i, j=j: (i, j))
    row = pl.BlockSpec((tile, BRANCH_W), lambda i: (i, 0))
    ret_o, pool_o = pl.pallas_call(
        functools.partial(_ret_kernel, tile=tile),
        grid=(s // tile,),
        in_specs=[col(8), col(9), col(10), col(11), row, row,
                  _const_spec((BRANCH_W, BRANCH_W)), _const_spec((BRANCH_W, BRANCH_W)),
                  _const_spec((N_HEADS, BRANCH_W)), _const_spec((N_HEADS * c, c)),
                  _const_spec((c, BRANCH_W)), _const_spec((c, BRANCH_W)), _const_spec((1, BRANCH_W)),
                  _const_spec((1, BRANCH_W)), _const_spec((1, BRANCH_W)),
                  col(7), _const_spec((BRANCH_W, BRANCH_W)), _const_spec((1, BRANCH_W))],
        out_specs=[row, row],
        out_shape=[jax.ShapeDtypeStruct((s, BRANCH_W), BF16), jax.ShapeDtypeStruct((s, BRANCH_W), BF16)],
        scratch_shapes=[pltpu.VMEM((BRANCH_W, BRANCH_W), F32), pltpu.VMEM((tile + POOL_HALO, BRANCH_W), F32)],
        compiler_params=_params("arbitrary"),
        name="retention_pool",
    )(proj, proj, proj, proj, cos_t, sin_t, perm, _head_ones(), _head_masks(), dstack, xi, zeta, gc,
      gn_g.reshape(1, BRANCH_W).astype(F32), gn_b.reshape(1, BRANCH_W).astype(F32),
      proj, _pool_weights(w_pool), pool_scale.reshape(1, BRANCH_W).astype(F32))
    return pool_o, ret_o


def _merge_kernel(h_ref, o0_ref, o1_ref, o2_ref, o3_ref, x_ref, wg_ref, wb_ref, wo_ref, g_ref, *rest):
    n_cast = (len(rest) - 2) // 2
    cast_in, (xo_ref, ho_ref), cast_out = rest[:n_cast], rest[n_cast:n_cast + 2], rest[n_cast + 2:]
    for src, dst in zip(cast_in, cast_out):
        dst[...] = src[...].astype(dst.dtype)
    h = h_ref[...]
    merged = jnp.zeros(x_ref.shape, F32)
    for bi, o_ref in enumerate((o0_ref, o1_ref, o2_ref, o3_ref)):
        gate = _sigmoid(_dot(h, wg_ref[:, bi * D_MODEL:(bi + 1) * D_MODEL]))
        merged = merged + gate * _dot(o_ref[...], wb_ref[bi])
    x_new = x_ref[...] + _dot(merged.astype(BF16), wo_ref[...])
    xo_ref[...] = x_new
    ho_ref[...] = _rms(x_new, g_ref[...]).astype(ho_ref.dtype)


def merge(h, branches, x, w_gate, w_branch, w_out, next_gain, tm=512, cast=()):
    s, d = x.shape
    tm = min(tm, s)
    steps = s // tm
    row = lambda w: pl.BlockSpec((tm, w), lambda i: (i, 0))
    flat = [c.reshape(-1, c.shape[-1]) for c in cast]
    slab = lambda c: pl.BlockSpec((c.shape[0] // steps, c.shape[1]), lambda i: (i, 0))
    outs = pl.pallas_call(
        _merge_kernel,
        grid=(steps,),
        in_specs=[row(d), row(BRANCH_W), row(BRANCH_W), row(BRANCH_W), row(BRANCH_W), row(d),
                  _const_spec((d, N_BRANCH * d)), _const_spec((N_BRANCH, BRANCH_W, d)),
                  _const_spec((d, d)), _const_spec((1, d))] + [slab(c) for c in flat],
        out_specs=[row(d), row(d)] + [slab(c) for c in flat],
        out_shape=[jax.ShapeDtypeStruct((s, d), F32), jax.ShapeDtypeStruct((s, d), BF16)]
        + [jax.ShapeDtypeStruct(c.shape, BF16) for c in flat],
        compiler_params=_params("parallel"),
        name="merge",
    )(h, *branches, x, w_gate, w_branch, w_out, next_gain.reshape(1, d).astype(F32), *flat)
    return outs[0], outs[1], [o.reshape(c.shape) for o, c in zip(outs[2:], cast)]


def _ffn_kernel(h_ref, x_ref, wg_ref, wu_ref, wd_ref, g_ref, xo_ref, ho_ref, acc_ref):
    f = pl.program_id(1)

    @pl.when(f == 0)
    def _():
        acc_ref[...] = jnp.zeros_like(acc_ref)

    h = h_ref[...]
    a = _silu(_dot(h, wg_ref[...].astype(BF16))) * _dot(h, wu_ref[...].astype(BF16))
    acc_ref[...] += _dot(a.astype(BF16), wd_ref[...].astype(BF16))

    @pl.when(f == pl.num_programs(1) - 1)
    def _():
        x_new = x_ref[...] + acc_ref[...]
        xo_ref[...] = x_new
        ho_ref[...] = _rms(x_new, g_ref[...]).astype(ho_ref.dtype)


def ffn_dense(h, x, w_gate, w_up, w_down, next_gain, tm=1024, tf=512):
    s, d = x.shape
    tm = min(tm, s)
    dff = w_gate.shape[1]
    row = lambda: pl.BlockSpec((tm, d), lambda i, f: (i, 0))
    return pl.pallas_call(
        _ffn_kernel,
        grid=(s // tm, dff // tf),
        in_specs=[row(), row(),
                  pl.BlockSpec((d, tf), lambda i, f: (0, f)),
                  pl.BlockSpec((d, tf), lambda i, f: (0, f)),
                  pl.BlockSpec((tf, d), lambda i, f: (f, 0)),
                  _const_spec((1, d))],
        out_specs=[row(), row()],
        out_shape=[jax.ShapeDtypeStruct((s, d), F32), jax.ShapeDtypeStruct((s, d), BF16)],
        scratch_shapes=[pltpu.VMEM((tm, d), F32)],
        compiler_params=_params("parallel", "arbitrary"),
        name="ffn_dense",
    )(h, x, w_gate, w_up, w_down, next_gain.reshape(1, d).astype(F32))


MERGE_CAST_TILE = 256
MOE_TOK_TILE = 256
MOE_ROUTE_TILES = 4
MOE_ROW_BLOCK = 512
MOE_GATHER_ROWS = 128
MOE_GATHER_TILES = 3
MOE_VMEM_LIMIT = 60 * 1024 * 1024
MOE_Y_BLOCK = 128
MOE_Y_FETCH = MOE_TOK_TILE // MOE_Y_BLOCK + 1


def _router_logits(x, w):
    xh, xm, _ = _split3(x)
    wh, wm, _ = _split3(w)
    packed = (wh.astype(F32) + pltpu.roll(wm.astype(F32), N_EXPERTS, axis=1)).astype(BF16)
    s = _dot(xh, packed) + _dot(xm, packed)
    return s + pltpu.roll(s, LANE - N_EXPERTS, axis=1)


def _route_kernel(x_ref, ng_ref, router_ref, ltri_ref, pos_ref, gate_ref, post_ref, before_ref, total_ref,
                  carry_ref):
    tm = x_ref.shape[0]

    @pl.when(pl.program_id(0) == 0)
    def _():
        carry_ref[...] = jnp.zeros_like(carry_ref)

    hn = _rms(x_ref[...], ng_ref[...])
    logits = _router_logits(hn, router_ref[...])
    lane = lax.broadcasted_iota(jnp.int32, (tm, LANE), 1)
    logits = jnp.where(lane < N_EXPERTS, logits, -jnp.inf)
    v1 = jnp.max(logits, axis=1, keepdims=True)
    i1 = jnp.min(jnp.where(logits == v1, lane, LANE), axis=1, keepdims=True)
    rest = jnp.where(lane == i1, -jnp.inf, logits)
    v2 = jnp.max(rest, axis=1, keepdims=True)
    i2 = jnp.min(jnp.where(rest == v2, lane, LANE), axis=1, keepdims=True)
    w1 = 1.0 / (1.0 + jnp.exp(v2 - v1))
    gate_ref[...] = jnp.where(lane == i1, w1, 0.0) + jnp.where(lane == i2, 1.0 - w1, 0.0)

    member = jnp.where((lane == i1) | (lane == i2), 1.0, 0.0)
    carry = carry_ref[...]
    ranks = []
    for blk in range(tm // MOE_TOK_TILE):
        before_ref[blk] = carry
        mb = member[blk * MOE_TOK_TILE:(blk + 1) * MOE_TOK_TILE]
        ranks.append(_dot(ltri_ref[...], mb.astype(BF16)) + carry)
        carry = carry + jnp.sum(mb, axis=0, keepdims=True)
    pos = jnp.where(member > 0.0, jnp.concatenate(ranks, axis=0), -1.0)
    pos_ref[...] = pos
    post_ref[...] = pos.T[:SUBLANE, :]
    carry_ref[...] = carry
    total_ref[...] = carry


def moe_route(x, norm_gain, router):
    s, d = x.shape
    tm = min(MOE_ROUTE_TILES * MOE_TOK_TILE, s)
    nsub = tm // MOE_TOK_TILE
    nt = s // tm
    router_p = jnp.zeros((d, LANE), F32).at[:, :N_EXPERTS].set(router.astype(F32))
    tt = min(MOE_TOK_TILE, s)
    ltri = _const(np.tril(np.ones((tt, tt)), -1), BF16)
    row = pl.BlockSpec((tm, LANE), lambda i: (i, 0))
    return pl.pallas_call(
        _route_kernel,
        grid=(nt,),
        in_specs=[pl.BlockSpec((tm, d), lambda i: (i, 0)), _const_spec((1, d)), _const_spec((d, LANE)),
                  _const_spec((tt, tt))],
        out_specs=[row, row, pl.BlockSpec((SUBLANE, tm), lambda i: (0, i)),
                   pl.BlockSpec((nsub, 1, LANE), lambda i: (i, 0, 0)), pl.BlockSpec((1, LANE), lambda i: (0, 0))],
        out_shape=[jax.ShapeDtypeStruct((s, LANE), F32), jax.ShapeDtypeStruct((s, LANE), F32),
                   jax.ShapeDtypeStruct((SUBLANE, s), F32), jax.ShapeDtypeStruct((nt * nsub, 1, LANE), F32),
                   jax.ShapeDtypeStruct((1, LANE), F32)],
        scratch_shapes=[pltpu.VMEM((1, LANE), F32)],
        compiler_params=_params("arbitrary"),
        name="moe_route",
    )(x, norm_gain.reshape(1, d).astype(F32), router_p, ltri)


def _moe_ffn_kernel(be_ref, r0_ref, tlo_ref, thi_ref, nv_ref, h_ref, post_ref, wg_ref, wu_ref, wd_ref,
                    y_ref, x_sc, acc_sc, *, tt, nsub):
    b = pl.program_id(0)
    f = pl.program_id(1)
    nb = pl.num_programs(0)
    valid = b < nv_ref[0]
    tmb, d = acc_sc.shape
    gr = tmb // nsub
    last_tile = h_ref.shape[0] // tt - 1

    def picked(e, want, t):
        t0 = pl.multiple_of(t * tt, tt)
        p = post_ref[pl.ds(e, 1), pl.ds(t0, tt)]
        sel = jnp.where(p == want, 1.0, 0.0).astype(BF16)
        return _dot(sel, h_ref[pl.ds(t0, tt), :])

    def gather_head(blk, sb):
        e = be_ref[blk]
        lo = tlo_ref[blk * nsub + sb]
        hi = thi_ref[blk * nsub + sb]
        want = (lax.broadcasted_iota(jnp.int32, (gr, 1), 0) + (r0_ref[blk] + sb * gr)).astype(F32)
        rows = picked(e, want, lo)
        for k in range(1, MOE_GATHER_TILES):
            rows = rows + picked(e, jnp.where(lo + k <= hi, want, -2.0), jnp.minimum(lo + k, last_tile))
        return rows.astype(BF16)

    def gather_tail(slot, blk, sb):
        e = be_ref[blk]
        want = (lax.broadcasted_iota(jnp.int32, (gr, 1), 0) + (r0_ref[blk] + sb * gr)).astype(F32)
        rows = pl.ds(pl.multiple_of(sb * gr, gr), gr)

        def more(t, carry):
            x_sc[slot, rows, :] = (x_sc[slot, rows, :].astype(F32) + picked(e, want, t)).astype(BF16)
            return carry

        lax.fori_loop(tlo_ref[blk * nsub + sb] + MOE_GATHER_TILES, thi_ref[blk * nsub + sb] + 1, more, 0)

    @pl.when((b == 0) & (f == 0))
    def _():
        acc_sc[...] = jnp.zeros_like(acc_sc)
        for sb in range(nsub):
            x_sc[0, pl.ds(sb * gr, gr), :] = gather_head(0, sb)
            gather_tail(0, 0, sb)

    nxt = jnp.minimum(b + 1, nb - 1)
    nslot = (b + 1) % 2

    @pl.when(valid)
    def _():
        x_sc[nslot, pl.ds(pl.multiple_of(f * gr, gr), gr), :] = gather_head(nxt, f)
        xb = x_sc[b % 2]
        a = _silu(_dot(xb, wg_ref[0])) * _dot(xb, wu_ref[0])
        acc_sc[...] = jnp.where(f == 0, 0.0, acc_sc[...]) + _dot(a.astype(BF16), wd_ref[0].astype(BF16))

    @pl.when(valid & (thi_ref[nxt * nsub + f] - tlo_ref[nxt * nsub + f] >= MOE_GATHER_TILES))
    def _():
        gather_tail(nslot, nxt, f)

    @pl.when(f == nsub - 1)
    def _():
        y_ref[...] = jnp.where(valid, acc_sc[...], 0.0).astype(y_ref.dtype)


def moe_ffn(h, post, sched, w_gate, w_up, w_down):
    s, d = h.shape
    ne, _, dff = w_gate.shape
    tmb = MOE_ROW_BLOCK
    tt = min(MOE_TOK_TILE, s)
    nb = sched[0].shape[0]
    nf = tmb // MOE_GATHER_ROWS
    tf = dff // nf

    def fidx(b, f, nv):
        return jnp.where(b < nv[0], f, nf - 1)

    grid_spec = pltpu.PrefetchScalarGridSpec(
        num_scalar_prefetch=5,
        grid=(nb, nf),
        in_specs=[pl.BlockSpec((s, d), lambda b, f, *_: (0, 0), pipeline_mode=pl.Buffered(1)),
                  pl.BlockSpec((SUBLANE, s), lambda b, f, *_: (0, 0), pipeline_mode=pl.Buffered(1)),
                  pl.BlockSpec((1, d, tf), lambda b, f, be, r0, tlo, thi, nv: (be[b], 0, fidx(b, f, nv))),
                  pl.BlockSpec((1, d, tf), lambda b, f, be, r0, tlo, thi, nv: (be[b], 0, fidx(b, f, nv))),
                  pl.BlockSpec((1, tf, d), lambda b, f, be, r0, tlo, thi, nv: (be[b], fidx(b, f, nv), 0))],
        out_specs=pl.BlockSpec((tmb, d), lambda b, f, *_: (b, 0)),
        scratch_shapes=[pltpu.VMEM((2, tmb, d), BF16), pltpu.VMEM((tmb, d), F32)],
    )
    return pl.pallas_call(
        functools.partial(_moe_ffn_kernel, tt=tt, nsub=nf),
        grid_spec=grid_spec,
        out_shape=jax.ShapeDtypeStruct((nb * tmb, d), BF16),
        compiler_params=pltpu.CompilerParams(dimension_semantics=("arbitrary", "arbitrary"),
                                             vmem_limit_bytes=MOE_VMEM_LIMIT),
        name="moe_ffn",
    )(*sched, h, post, w_gate, w_up, w_down)


def _moe_combine_kernel(kb_ref, off_ref, lim_ref, x_ref, pos_ref, gate_ref, fg_ref, *rest):
    y_refs, o_ref, acc_sc = rest[:-2], rest[-2], rest[-1]
    t = pl.program_id(0)
    yb = y_refs[0].shape[0]
    pos = pos_ref[...]
    gate = gate_ref[...]
    col = lax.broadcasted_iota(jnp.int32, (1, yb), 1).astype(F32)

    def routed(e):
        pe = pos[:, e:e + 1]
        ge = gate[:, e:e + 1]
        r = jnp.where(pe >= 0.0, pe + off_ref[t * N_EXPERTS + e].astype(F32), -1.0)
        return r, ge

    def picked(e, k, r):
        sel = jnp.where(r == col + float(k * yb), 1.0, 0.0).astype(BF16)
        return _dot(sel, y_refs[MOE_Y_FETCH * e + k][...])

    col2 = lax.broadcasted_iota(jnp.int32, (1, 2 * yb), 1).astype(F32)
    acc = x_ref[...]
    for e in range(N_EXPERTS):
        r, ge = routed(e)
        pair = jnp.concatenate([y_refs[MOE_Y_FETCH * e][...], y_refs[MOE_Y_FETCH * e + 1][...]], axis=0)
        acc = acc + ge * _dot(jnp.where(r == col2, 1.0, 0.0).astype(BF16), pair)
    acc_sc[...] = acc
    for e in range(N_EXPERTS):
        for k in range(2, MOE_Y_FETCH):
            @pl.when(lim_ref[t * N_EXPERTS + e] > k * yb)
            def _(k=k, e=e):
                r, ge = routed(e)
                acc_sc[...] += ge * picked(e, k, r)
    o_ref[...] = _rms(acc_sc[...], fg_ref[...]).astype(o_ref.dtype)


def moe_combine(x, pos, gate, y, kb, off, lim, final_gain):
    s, d = x.shape
    tm = min(MOE_TOK_TILE, s)
    yb = MOE_Y_BLOCK
    last = y.shape[0] // yb - 1

    def yspec(e, k):
        def index(t, kb_r, off_r, lim_r):
            blk = jnp.minimum(kb_r[t * N_EXPERTS + e] + k, last)
            return (blk if k < 1 else jnp.where(lim_r[t * N_EXPERTS + e] > k * yb, blk, 0), 0)
        return pl.BlockSpec((yb, d), index)

    grid_spec = pltpu.PrefetchScalarGridSpec(
        num_scalar_prefetch=3,
        grid=(s // tm,),
        in_specs=[pl.BlockSpec((tm, d), lambda t, *_: (t, 0)),
                  pl.BlockSpec((tm, LANE), lambda t, *_: (t, 0)),
                  pl.BlockSpec((tm, LANE), lambda t, *_: (t, 0)),
                  pl.BlockSpec((1, d), lambda t, *_: (0, 0))]
        + [yspec(e, k) for e in range(N_EXPERTS) for k in range(MOE_Y_FETCH)],
        out_specs=pl.BlockSpec((tm, d), lambda t, *_: (t, 0)),
        scratch_shapes=[pltpu.VMEM((tm, d), F32)],
    )
    return pl.pallas_call(
        _moe_combine_kernel,
        grid_spec=grid_spec,
        out_shape=jax.ShapeDtypeStruct((s, d), F32),
        compiler_params=_params("arbitrary"),
        name="moe_combine",
    )(kb, off, lim, x, pos, gate, final_gain.reshape(1, d).astype(F32), *([y] * (MOE_Y_FETCH * N_EXPERTS)))


def _moe_schedule(before, total, s):
    tmb, yb, gr = MOE_ROW_BLOCK, MOE_Y_BLOCK, MOE_GATHER_ROWS
    nb = 2 * s // tmb + N_EXPERTS
    counts = total[0, :N_EXPERTS].astype(jnp.int32)
    nblk = (counts + tmb - 1) // tmb
    end = jnp.cumsum(nblk)
    first = end - nblk
    nvalid = end[-1]
    b = jnp.minimum(jnp.arange(nb, dtype=jnp.int32), nvalid - 1)
    blk_e = jnp.sum(b[:, None] >= end[None, :], axis=1).astype(jnp.int32)
    r0 = (b - first[blk_e]) * tmb
    cb = before[:, 0, :N_EXPERTS].astype(jnp.int32)
    r0s = (r0[:, None] + gr * jnp.arange(tmb // gr, dtype=jnp.int32)[None, :]).reshape(-1)
    cbe = cb[:, jnp.repeat(blk_e, tmb // gr)]
    tlo = (jnp.sum(cbe <= r0s[None, :], axis=0) - 1).astype(jnp.int32)
    thi = (jnp.sum(cbe < (r0s + gr)[None, :], axis=0) - 1).astype(jnp.int32)
    sched = (blk_e, r0.astype(jnp.int32), tlo, thi, nvalid.reshape(1).astype(jnp.int32))
    row_start = first[None, :] * tmb + cb
    kb = row_start // yb
    off = first[None, :] * tmb - kb * yb
    n_te = jnp.concatenate([cb[1:], counts[None, :]], axis=0) - cb
    lim = row_start - kb * yb + n_te
    flat = lambda a: a.reshape(-1).astype(jnp.int32)
    return sched, flat(kb), flat(off), flat(lim)


def moe_sparse(h, x, norm_gain, router, w_gate, w_up, w_down, final_gain):
    s, _ = x.shape
    pos, gate, post, before, total = moe_route(x, norm_gain, router)
    sched, kb, off, lim = _moe_schedule(before, total, s)
    y = moe_ffn(h, post, sched, w_gate, w_up, w_down)
    return moe_combine(x, pos, gate, y, kb, off, lim, final_gain)


FOX_F0 = 7 * BRANCH_W


def _split_w_in_kernel(wt_ref, mix_ref, f_ref, gate_ref):
    rest0 = FOX_F0 + N_HEADS
    gate0 = N_MIX_COLS + N_HEADS
    mix_ref[:, :FOX_F0] = wt_ref[pl.ds(0, FOX_F0), :].T.astype(BF16)
    mix_ref[:, FOX_F0:] = wt_ref[pl.ds(rest0, N_MIX_COLS - FOX_F0), :].T.astype(BF16)
    gate_ref[...] = wt_ref[pl.ds(gate0, N_BRANCH * D_MODEL), :].T.astype(BF16)
    f_rows = jnp.concatenate([wt_ref[pl.ds(FOX_F0, N_HEADS), :], jnp.zeros((LANE - N_HEADS, LANE), F32)], axis=0)
    f_ref[...] = f_rows.T.astype(BF16)


def _mixer_weights(w_in, layer, tr=LANE):
    _, d, cols = w_in.shape
    w_t = jnp.swapaxes(w_in, 1, 2)
    row = lambda w: pl.BlockSpec((tr, w), lambda i: (i, 0))
    return pl.pallas_call(
        _split_w_in_kernel,
        grid=(d // tr,),
        in_specs=[pl.BlockSpec((None, cols, tr), lambda i: (layer, 0, i))],
        out_specs=[row(N_MIX_COLS), row(LANE), row(N_BRANCH * D_MODEL)],
        out_shape=[jax.ShapeDtypeStruct((d, N_MIX_COLS), BF16), jax.ShapeDtypeStruct((d, LANE), BF16),
                   jax.ShapeDtypeStruct((d, N_BRANCH * D_MODEL), BF16)],
        compiler_params=_params("parallel"),
        name="split_w_in",
    )(w_t)


def kernel(x, w_in, w_branch, w_out, norm_mix_g, hgrn_lb_logits, hgrn_norm_g, fox_f_bias, pool_w, pool_scale,
           ret_gn_g, ret_gn_b, norm_ffn_g, ffn_w_gate, ffn_w_up, ffn_w_down, moe_router, moe_w_gate, moe_w_up,
           moe_w_down, final_norm_g):
    b, s, d = x.shape
    assert b == 1 and d == D_MODEL
    depth = w_in.shape[0]
    assert depth == 2, "layer 0 uses the dense FFN, layer 1 the experts and the final norm"
    xs = x.reshape(s, d)
    h = None
    out = None
    for layer in range(depth):
        w_mix, w_f, w_gate = _mixer_weights(w_in, layer)
        if layer == 0:
            proj, h = norm_matmul(xs, norm_mix_g[0], w_mix)
        else:
            proj = matmul(h, w_mix)
        ct, c_rows, fox_stats = fox_gate(h, proj, w_f, fox_f_bias[layer])
        pool_o, ret_o = retention_and_pool(proj, ret_gn_g[layer], ret_gn_b[layer], pool_w[layer], pool_scale[layer])
        branches = (
            hgrn2(proj, hgrn_lb_logits, hgrn_norm_g[layer], layer),
            fox_attention(proj, ct, c_rows, fox_stats),
            pool_o,
            ret_o,
        )
        li = layer // 2
        if layer % 2 == 0:
            xs, h2, (ffn_wg, ffn_wu) = merge(h, branches, xs, w_gate, w_branch[layer].astype(BF16),
                                             w_out[layer].astype(BF16), norm_ffn_g[layer],
                                             cast=(ffn_w_gate[li], ffn_w_up[li]))
            xs, h = ffn_dense(h2, xs, ffn_wg, ffn_wu, ffn_w_down[li], norm_mix_g[layer + 1])
        else:
            xs, h2, (moe_wg, moe_wu) = merge(h, branches, xs, w_gate, w_branch[layer].astype(BF16),
                                             w_out[layer].astype(BF16), norm_ffn_g[layer], tm=MERGE_CAST_TILE,
                                             cast=(moe_w_gate[li], moe_w_up[li]))
            out = moe_sparse(h2, xs, norm_ffn_g[layer], moe_router[li], moe_wg, moe_wu, moe_w_down[li],
                             final_norm_g)
    return out.reshape(b, s, d)
```
